```python
import math
import jax, jax.numpy as jnp
from jax import lax
import numpy as np

D_MODEL = 1024
BATCH = 8
SEQ = 4096
DEPTH = 1

HEAD_DIM = 64
D_MIX = D_MODEL
N_GMLP_HEADS = 8
D_GMLP = N_GMLP_HEADS * HEAD_DIM
N_Q_HEADS = 8
N_KV_HEADS = 2
GQA_GROUP = N_Q_HEADS // N_KV_HEADS
D_ATTN = N_Q_HEADS * HEAD_DIM
D_KV = N_KV_HEADS * HEAD_DIM
D_IN = 2 * D_GMLP + D_ATTN + 2 * D_KV
CHUNK = 128
WINDOW = 128
ATTN_BLOCK = 128
ROPE_THETA = 10000.0
D_FF = 4 * D_MODEL
LN_EPS = 1e-5
DEEPNORM_ALPHA = (2.0 * DEPTH) ** 0.25
DEEPNORM_BETA = (8.0 * DEPTH) ** -0.25
NEG_INF = -1e30

kernel_name = "hymba_gmlp_swa_sink_deepnorm"


def layer_norm(x, g, b):
    xf = x.astype(jnp.float32)
    mu = jnp.mean(xf, axis=-1, keepdims=True)
    var = jnp.mean(jnp.square(xf - mu), axis=-1, keepdims=True)
    y = (xf - mu) * lax.rsqrt(var + LN_EPS)
    return (y * g.astype(jnp.float32) + b.astype(jnp.float32)).astype(x.dtype)


def rope(t, positions):
    half = HEAD_DIM // 2
    inv_freq = ROPE_THETA ** (-jnp.arange(0, HEAD_DIM, 2, dtype=jnp.float32) / HEAD_DIM)
    ang = positions.astype(jnp.float32)[..., None] * inv_freq
    cos = jnp.cos(ang)[:, :, None, :]
    sin = jnp.sin(ang)[:, :, None, :]
    tf = t.astype(jnp.float32)
    t1, t2 = tf[..., :half], tf[..., half:]
    out = jnp.concatenate([t1 * cos - t2 * sin, t2 * cos + t1 * sin], axis=-1)
    return out.astype(t.dtype)


def gmlp_mixer(u, v, v_ln_g, v_ln_b, w_spatial, b_spatial):
    B, S, _ = u.shape
    nc = S // CHUNK
    u = jax.nn.gelu(u)
    v = layer_norm(jax.nn.gelu(v), v_ln_g, v_ln_b)
    vc = v.reshape(B, nc, CHUNK, N_GMLP_HEADS, HEAD_DIM)
    causal = jnp.tril(jnp.ones((CHUNK, CHUNK), dtype=w_spatial.dtype))
    w = w_spatial * causal
    mixed = jnp.einsum('hts,bcshd->bcthd', w, vc) + b_spatial.T[None, None, :, :, None]
    out = u.reshape(B, nc, CHUNK, N_GMLP_HEADS, HEAD_DIM) * mixed
    return out.reshape(B, S, D_GMLP)


def swa_sink_attention(q, k, v, positions, sinks):
    B, S, _ = q.shape
    nb = S // ATTN_BLOCK
    q = rope(q.reshape(B, S, N_Q_HEADS, HEAD_DIM), positions)
    k = rope(k.reshape(B, S, N_KV_HEADS, HEAD_DIM), positions)
    v = v.reshape(B, S, N_KV_HEADS, HEAD_DIM)
    qb = q.reshape(B, nb, ATTN_BLOCK, N_KV_HEADS, GQA_GROUP, HEAD_DIM)

    def banded(t):
        tb = t.reshape(B, nb, ATTN_BLOCK, N_KV_HEADS, HEAD_DIM)
        prev = jnp.pad(tb[:, :-1], ((0, 0), (1, 0), (0, 0), (0, 0), (0, 0)))
        return jnp.concatenate([prev, tb], axis=2)

    kb, vb = banded(k), banded(v)
    scores = jnp.einsum('bnqkgd,bnskd->bnkgqs', qb, kb).astype(jnp.float32)
    scores = scores * (1.0 / math.sqrt(HEAD_DIM))

    qi = jnp.arange(ATTN_BLOCK)[:, None]
    si = jnp.arange(2 * ATTN_BLOCK)[None, :]
    dist = qi + ATTN_BLOCK - si
    band = (dist >= 0) & (dist < WINDOW)
    key_abs = jnp.arange(nb)[:, None, None] * ATTN_BLOCK + si[None] - ATTN_BLOCK
    mask = band[None] & (key_abs >= 0)
    scores = jnp.where(mask[None, :, None, None], scores, NEG_INF)

    sink = sinks.astype(jnp.float32).reshape(N_KV_HEADS, GQA_GROUP)
    sink_col = jnp.broadcast_to(sink[None, None, :, :, None, None], scores.shape[:-1] + (1,))
    probs = jax.nn.softmax(jnp.concatenate([scores, sink_col], axis=-1), axis=-1)[..., :-1]
    out = jnp.einsum('bnkgqs,bnskd->bnqkgd', probs.astype(vb.dtype), vb)
    return out.reshape(B, S, D_ATTN)


def _fwd_setup_inputs(seed: int = 0) -> dict:
    key = jax.random.key(seed)
    ks = jax.random.split(key, 16)
    f32 = jnp.float32
    x = jax.random.normal(ks[0], (BATCH, SEQ, D_MODEL), f32)
    offset = jax.random.randint(ks[1], (BATCH, 1), 0, 1024, dtype=jnp.int32)
    positions = (offset + jnp.arange(SEQ, dtype=jnp.int32)[None, :]).astype(jnp.int32)
    w_in = jax.random.normal(ks[2], (DEPTH, D_MODEL, D_IN), f32) * D_MODEL ** -0.5
    v_ln_g = 1.0 + 0.05 * jax.random.normal(ks[3], (DEPTH, D_GMLP), f32)
    v_ln_b = 0.02 * jax.random.normal(ks[4], (DEPTH, D_GMLP), f32)
    w_spatial = jax.random.normal(ks[5], (DEPTH, N_GMLP_HEADS, CHUNK, CHUNK), f32) * CHUNK ** -0.5
    b_spatial = 1.0 + 0.1 * jax.random.normal(ks[6], (DEPTH, N_GMLP_HEADS, CHUNK), f32)
    sinks = 0.5 * jax.random.normal(ks[7], (DEPTH, N_Q_HEADS), f32)
    w_out = jax.random.normal(ks[8], (DEPTH, D_MIX, D_MODEL), f32) * (D_MIX ** -0.5) * DEEPNORM_BETA
    ln1_g = 1.0 + 0.05 * jax.random.normal(ks[9], (DEPTH, D_MODEL), f32)
    ln1_b = 0.02 * jax.random.normal(ks[10], (DEPTH, D_MODEL), f32)
    w_ff1 = jax.random.normal(ks[11], (DEPTH, D_MODEL, D_FF), f32) * D_MODEL ** -0.5
    w_ff2 = jax.random.normal(ks[12], (DEPTH, D_FF, D_MODEL), f32) * (D_FF ** -0.5) * DEEPNORM_BETA
    ln2_g = 1.0 + 0.05 * jax.random.normal(ks[13], (DEPTH, D_MODEL), f32)
    ln2_b = 0.02 * jax.random.normal(ks[14], (DEPTH, D_MODEL), f32)
    return {"x": x, "positions": positions, "w_in": w_in, "v_ln_g": v_ln_g, "v_ln_b": v_ln_b,
            "w_spatial": w_spatial, "b_spatial": b_spatial, "sinks": sinks, "w_out": w_out,
            "ln1_g": ln1_g, "ln1_b": ln1_b, "w_ff1": w_ff1, "w_ff2": w_ff2,
            "ln2_g": ln2_g, "ln2_b": ln2_b}


def _fwd_reference(x, positions, w_in, v_ln_g, v_ln_b, w_spatial, b_spatial, sinks, w_out,
              ln1_g, ln1_b, w_ff1, w_ff2, ln2_g, ln2_b):
    split_at = [D_GMLP, 2 * D_GMLP, 2 * D_GMLP + D_ATTN, 2 * D_GMLP + D_ATTN + D_KV]
    for l in range(DEPTH):
        h = x @ w_in[l]
        u, v_g, q, k, v_a = jnp.split(h, split_at, axis=-1)
        a_out = gmlp_mixer(u, v_g, v_ln_g[l], v_ln_b[l], w_spatial[l], b_spatial[l])
        b_out = swa_sink_attention(q, k, v_a, positions, sinks[l])
        mix = jnp.concatenate([a_out, b_out], axis=-1) @ w_out[l]
        x = layer_norm(DEEPNORM_ALPHA * x + mix, ln1_g[l], ln1_b[l])
        ff = jnp.square(jax.nn.relu(x @ w_ff1[l])) @ w_ff2[l]
        x = layer_norm(DEEPNORM_ALPHA * x + ff, ln2_g[l], ln2_b[l])
    return x


import jax as _jax
import jax.numpy as _jnp

TWIN_FORMAT = 'train_step'
FWD_PARAMS = ['x', 'positions', 'w_in', 'v_ln_g', 'v_ln_b', 'w_spatial', 'b_spatial', 'sinks', 'w_out', 'ln1_g', 'ln1_b', 'w_ff1', 'w_ff2', 'ln2_g', 'ln2_b']
TWIN_WEIGHTS = ['w_in', 'v_ln_g', 'v_ln_b', 'w_spatial', 'b_spatial', 'sinks', 'w_out', 'ln1_g', 'ln1_b', 'w_ff1', 'w_ff2', 'ln2_g', 'ln2_b']
TWIN_DIFF_INPUT = 'x'
TWIN_INPUTS = ['x', 'positions', 'w_in', 'v_ln_g', 'v_ln_b', 'w_spatial', 'b_spatial', 'sinks', 'w_out', 'ln1_g', 'ln1_b', 'w_ff1', 'w_ff2', 'ln2_g', 'ln2_b', 'loss_target', 'm_w_in', 'm_v_ln_g', 'm_v_ln_b', 'm_w_spatial', 'm_b_spatial', 'm_sinks', 'm_w_out', 'm_ln1_g', 'm_ln1_b', 'm_w_ff1', 'm_w_ff2', 'm_ln2_g', 'm_ln2_b', 'v_w_in', 'v_v_ln_g', 'v_v_ln_b', 'v_w_spatial', 'v_b_spatial', 'v_sinks', 'v_w_out', 'v_ln1_g', 'v_ln1_b', 'v_w_ff1', 'v_w_ff2', 'v_ln2_g', 'v_ln2_b']
TWIN_OUTPUTS = ['loss', 'grad_x', 'grad_w_in', 'grad_v_ln_g', 'grad_v_ln_b', 'grad_w_spatial', 'grad_b_spatial', 'grad_sinks', 'grad_w_out', 'grad_ln1_g', 'grad_ln1_b', 'grad_w_ff1', 'grad_w_ff2', 'grad_ln2_g', 'grad_ln2_b', 'delta_w_in', 'delta_v_ln_g', 'delta_v_ln_b', 'delta_w_spatial', 'delta_b_spatial', 'delta_sinks', 'delta_w_out', 'delta_ln1_g', 'delta_ln1_b', 'delta_w_ff1', 'delta_w_ff2', 'delta_ln2_g', 'delta_ln2_b', 'new_m_w_in', 'new_m_v_ln_g', 'new_m_v_ln_b', 'new_m_w_spatial', 'new_m_b_spatial', 'new_m_sinks', 'new_m_w_out', 'new_m_ln1_g', 'new_m_ln1_b', 'new_m_w_ff1', 'new_m_w_ff2', 'new_m_ln2_g', 'new_m_ln2_b', 'new_v_w_in', 'new_v_v_ln_g', 'new_v_v_ln_b', 'new_v_w_spatial', 'new_v_b_spatial', 'new_v_sinks', 'new_v_w_out', 'new_v_ln1_g', 'new_v_ln1_b', 'new_v_w_ff1', 'new_v_w_ff2', 'new_v_ln2_g', 'new_v_ln2_b']
TWIN_LEAF_KINDS = {'loss': 'loss', 'grad_x': 'grad_x', 'grad_w_in': 'grad_w', 'grad_v_ln_g': 'grad_w', 'grad_v_ln_b': 'grad_w', 'grad_w_spatial': 'grad_w', 'grad_b_spatial': 'grad_w', 'grad_sinks': 'grad_w', 'grad_w_out': 'grad_w', 'grad_ln1_g': 'grad_w', 'grad_ln1_b': 'grad_w', 'grad_w_ff1': 'grad_w', 'grad_w_ff2': 'grad_w', 'grad_ln2_g': 'grad_w', 'grad_ln2_b': 'grad_w', 'delta_w_in': 'delta_w', 'delta_v_ln_g': 'delta_w', 'delta_v_ln_b': 'delta_w', 'delta_w_spatial': 'delta_w', 'delta_b_spatial': 'delta_w', 'delta_sinks': 'delta_w', 'delta_w_out': 'delta_w', 'delta_ln1_g': 'delta_w', 'delta_ln1_b': 'delta_w', 'delta_w_ff1': 'delta_w', 'delta_w_ff2': 'delta_w', 'delta_ln2_g': 'delta_w', 'delta_ln2_b': 'delta_w', 'new_m_w_in': 'new_m', 'new_m_v_ln_g': 'new_m', 'new_m_v_ln_b': 'new_m', 'new_m_w_spatial': 'new_m', 'new_m_b_spatial': 'new_m', 'new_m_sinks': 'new_m', 'new_m_w_out': 'new_m', 'new_m_ln1_g': 'new_m', 'new_m_ln1_b': 'new_m', 'new_m_w_ff1': 'new_m', 'new_m_w_ff2': 'new_m', 'new_m_ln2_g': 'new_m', 'new_m_ln2_b': 'new_m', 'new_v_w_in': 'new_v', 'new_v_v_ln_g': 'new_v', 'new_v_v_ln_b': 'new_v', 'new_v_w_spatial': 'new_v', 'new_v_b_spatial': 'new_v', 'new_v_sinks': 'new_v', 'new_v_w_out': 'new_v', 'new_v_ln1_g': 'new_v', 'new_v_ln1_b': 'new_v', 'new_v_w_ff1': 'new_v', 'new_v_w_ff2': 'new_v', 'new_v_ln2_g': 'new_v', 'new_v_ln2_b': 'new_v'}


def _forward(args):
    return _fwd_reference(*[args[k] for k in FWD_PARAMS])


def _output_shape():
    def fwd():
        inp = _fwd_setup_inputs(0)
        return _fwd_reference(*[inp[k] for k in FWD_PARAMS])
    out = _jax.eval_shape(fwd)
    return out.shape, out.dtype

N_MICROBATCH = 1
ADAM_LR = 0.001
ADAM_B1 = 0.9
ADAM_B2 = 0.999
ADAM_EPS = 1e-08
ADAM_WD = 0.01
ADAM_STEP = 10
PER_EXAMPLE_BATCH_AXIS = {'x': 0, 'positions': 0, 'loss_target': 0}
SHARED_INPUTS = []
_WEIGHT_DTYPES = {'w_in': _jnp.float32, 'v_ln_g': _jnp.float32, 'v_ln_b': _jnp.float32, 'w_spatial': _jnp.float32, 'b_spatial': _jnp.float32, 'sinks': _jnp.float32, 'w_out': _jnp.float32, 'ln1_g': _jnp.float32, 'ln1_b': _jnp.float32, 'w_ff1': _jnp.float32, 'w_ff2': _jnp.float32, 'ln2_g': _jnp.float32, 'ln2_b': _jnp.float32}
MOMENT_SCALE = {'w_in': 4.959542e-02, 'v_ln_g': 4.391341e-02, 'v_ln_b': 4.444058e-02, 'w_spatial': 2.920087e-02, 'b_spatial': 4.151230e-02, 'sinks': 9.611560e-03, 'w_out': 1.384784e-01, 'ln1_g': 2.304865e+00, 'ln1_b': 8.398047e-01, 'w_ff1': 5.615435e-02, 'w_ff2': 3.255135e-01, 'ln2_g': 3.220406e+01, 'ln2_b': 7.507431e+00}


def _to_microbatches(a, axis):
    t = _jnp.moveaxis(a, axis, 0)
    t = t.reshape((N_MICROBATCH, t.shape[0] // N_MICROBATCH) + t.shape[1:])
    return _jnp.moveaxis(t, 1, axis + 1)


def setup_inputs(seed: int = 0) -> dict:
    inp = _fwd_setup_inputs(seed)
    key = _jax.random.fold_in(_jax.random.key(seed), 7919)
    shape, _ = _output_shape()
    out = dict(inp)
    out["loss_target"] = _jax.random.normal(_jax.random.fold_in(key, 0), shape, _jnp.float32)
    for i, name in enumerate(TWIN_WEIGHTS):
        w = inp[name].astype(_jnp.float32)
        if MOMENT_SCALE is None:
            s = _jnp.sqrt(_jnp.mean(_jnp.square(w)) + 1e-30)
        else:
            s = MOMENT_SCALE[name]
        km, kv = _jax.random.split(_jax.random.fold_in(key, i + 1))
        out[name] = w
        out["m_" + name] = s * _jax.random.normal(km, w.shape, _jnp.float32)
        out["v_" + name] = (s * s) * _jax.random.uniform(kv, w.shape, _jnp.float32, 0.5, 1.5)
    if N_MICROBATCH > 1:
        for name, axis in PER_EXAMPLE_BATCH_AXIS.items():
            out[name] = _to_microbatches(out[name], axis)
    return {'x': out['x'], 'positions': out['positions'], 'w_in': out['w_in'], 'v_ln_g': out['v_ln_g'], 'v_ln_b': out['v_ln_b'], 'w_spatial': out['w_spatial'], 'b_spatial': out['b_spatial'], 'sinks': out['sinks'], 'w_out': out['w_out'], 'ln1_g': out['ln1_g'], 'ln1_b': out['ln1_b'], 'w_ff1': out['w_ff1'], 'w_ff2': out['w_ff2'], 'ln2_g': out['ln2_g'], 'ln2_b': out['ln2_b'], 'loss_target': out['loss_target'], 'm_w_in': out['m_w_in'], 'm_v_ln_g': out['m_v_ln_g'], 'm_v_ln_b': out['m_v_ln_b'], 'm_w_spatial': out['m_w_spatial'], 'm_b_spatial': out['m_b_spatial'], 'm_sinks': out['m_sinks'], 'm_w_out': out['m_w_out'], 'm_ln1_g': out['m_ln1_g'], 'm_ln1_b': out['m_ln1_b'], 'm_w_ff1': out['m_w_ff1'], 'm_w_ff2': out['m_w_ff2'], 'm_ln2_g': out['m_ln2_g'], 'm_ln2_b': out['m_ln2_b'], 'v_w_in': out['v_w_in'], 'v_v_ln_g': out['v_v_ln_g'], 'v_v_ln_b': out['v_v_ln_b'], 'v_w_spatial': out['v_w_spatial'], 'v_b_spatial': out['v_b_spatial'], 'v_sinks': out['v_sinks'], 'v_w_out': out['v_w_out'], 'v_ln1_g': out['v_ln1_g'], 'v_ln1_b': out['v_ln1_b'], 'v_w_ff1': out['v_w_ff1'], 'v_w_ff2': out['v_w_ff2'], 'v_ln2_g': out['v_ln2_g'], 'v_ln2_b': out['v_ln2_b']}


def _loss(weights, diff, rest, loss_target):
    with _jax.named_scope("forward"):
        args = {**rest, TWIN_DIFF_INPUT: diff, **{k: w.astype(_WEIGHT_DTYPES[k]) for k, w in weights.items()}}
        y = _forward(args)
    with _jax.named_scope("loss_head"):
        err = _jnp.square(y.astype(_jnp.float32) - loss_target)
        return 0.5 * _jnp.sum(_jnp.mean(err, axis=-1)) if err.ndim else 0.5 * err


def _adamw(w, g, m, v):
    m = ADAM_B1 * m + (1.0 - ADAM_B1) * g
    v = ADAM_B2 * v + (1.0 - ADAM_B2) * _jnp.square(g)
    m_hat = m / (1.0 - ADAM_B1 ** ADAM_STEP)
    v_hat = v / (1.0 - ADAM_B2 ** ADAM_STEP)
    delta = -ADAM_LR * (m_hat / (_jnp.sqrt(v_hat) + ADAM_EPS) + ADAM_WD * w)
    return delta, m, v


def reference(x, positions, w_in, v_ln_g, v_ln_b, w_spatial, b_spatial, sinks, w_out, ln1_g, ln1_b, w_ff1, w_ff2, ln2_g, ln2_b, loss_target, m_w_in, m_v_ln_g, m_v_ln_b, m_w_spatial, m_b_spatial, m_sinks, m_w_out, m_ln1_g, m_ln1_b, m_w_ff1, m_w_ff2, m_ln2_g, m_ln2_b, v_w_in, v_v_ln_g, v_v_ln_b, v_w_spatial, v_b_spatial, v_sinks, v_w_out, v_ln1_g, v_ln1_b, v_w_ff1, v_w_ff2, v_ln2_g, v_ln2_b):
    given = dict(x=x, positions=positions, w_in=w_in, v_ln_g=v_ln_g, v_ln_b=v_ln_b, w_spatial=w_spatial, b_spatial=b_spatial, sinks=sinks, w_out=w_out, ln1_g=ln1_g, ln1_b=ln1_b, w_ff1=w_ff1, w_ff2=w_ff2, ln2_g=ln2_g, ln2_b=ln2_b, loss_target=loss_target, m_w_in=m_w_in, m_v_ln_g=m_v_ln_g, m_v_ln_b=m_v_ln_b, m_w_spatial=m_w_spatial, m_b_spatial=m_b_spatial, m_sinks=m_sinks, m_w_out=m_w_out, m_ln1_g=m_ln1_g, m_ln1_b=m_ln1_b, m_w_ff1=m_w_ff1, m_w_ff2=m_w_ff2, m_ln2_g=m_ln2_g, m_ln2_b=m_ln2_b, v_w_in=v_w_in, v_v_ln_g=v_v_ln_g, v_v_ln_b=v_v_ln_b, v_w_spatial=v_w_spatial, v_b_spatial=v_b_spatial, v_sinks=v_sinks, v_w_out=v_w_out, v_ln1_g=v_ln1_g, v_ln1_b=v_ln1_b, v_w_ff1=v_w_ff1, v_w_ff2=v_w_ff2, v_ln2_g=v_ln2_g, v_ln2_b=v_ln2_b)
    weights = {n: given[n] for n in TWIN_WEIGHTS}
    shared = {n: given[n] for n in SHARED_INPUTS}
    per_example = {n: given[n] for n in ['x', 'positions']}
    grad_fn = _jax.value_and_grad(_loss, argnums=(0, 1))

    def one_microbatch(ex, loss_target):
        ex = dict(ex)
        diff = ex.pop(TWIN_DIFF_INPUT)
        return grad_fn(weights, diff, {**shared, **ex}, loss_target)

    if N_MICROBATCH == 1:
        loss, (grad_w, grad_x) = one_microbatch(per_example, given["loss_target"])
    else:
        def body(carry, xs):
            loss_sum, grad_sum = carry
            l_k, (gw_k, gx_k) = one_microbatch(xs[0], xs[1])
            with _jax.named_scope("update"):
                return (loss_sum + l_k, _jax.tree.map(_jnp.add, grad_sum, gw_k)), gx_k

        init = (_jnp.zeros((), _jnp.float32), _jax.tree.map(_jnp.zeros_like, weights))
        (loss, grad_w), grad_x = _jax.lax.scan(body, init, (per_example, given["loss_target"]))
    with _jax.named_scope("update"):
        delta_w, new_m, new_v = {}, {}, {}
        for n in TWIN_WEIGHTS:
            delta_w[n], new_m[n], new_v[n] = _adamw(weights[n], grad_w[n], given["m_" + n], given["v_" + n])
    return (loss, grad_x, *[grad_w[n] for n in TWIN_WEIGHTS], *[delta_w[n] for n in TWIN_WEIGHTS],
            *[new_m[n] for n in TWIN_WEIGHTS], *[new_v[n] for n in TWIN_WEIGHTS])
```

```python
import functools
import math

import jax
import jax.numpy as jnp
from jax import lax
from jax.experimental import pallas as pl
from jax.experimental.pallas import tpu as pltpu

F32 = jnp.float32
BF16 = jnp.bfloat16
MESH = pl.DeviceIdType.MESH

HEAD_DIM = 64
N_HEADS = 8
N_KV_HEADS = 2
BLK = 128
D_GMLP = N_HEADS * HEAD_DIM
D_ATTN = N_HEADS * HEAD_DIM
D_KV = N_KV_HEADS * HEAD_DIM
D_IN = 2 * D_GMLP + D_ATTN + 2 * D_KV
COL_U, COL_V, COL_Q, COL_K = 0, D_GMLP, 2 * D_GMLP, 2 * D_GMLP + D_ATTN
ROPE_THETA = 10000.0
LN_EPS = 1e-5
ALPHA = 2.0 ** 0.25
NEG_INF = -1e30
SCORE_SCALE = 1.0 / math.sqrt(HEAD_DIM)
ADAM_LR, ADAM_B1, ADAM_B2, ADAM_EPS, ADAM_WD, ADAM_STEP = 0.001, 0.9, 0.999, 1e-08, 0.01, 10
N_DEV = 8
LANES = 128
VMEM_LIMIT = 56 * 1024 * 1024

NT = (((1,), (1,)), ((), ()))
TN = (((0,), (0,)), ((), ()))


def _params(*sem):
    return pltpu.CompilerParams(dimension_semantics=sem, vmem_limit_bytes=VMEM_LIMIT)


def _dot(a, b, dims=None):
    if dims is None:
        return jnp.dot(a, b, preferred_element_type=F32)
    return lax.dot_general(a, b, dims, preferred_element_type=F32)


def _mean(a):
    return jnp.mean(a, axis=-1, keepdims=True)


def _ln_fwd(z, g, b):
    zc = z - _mean(z)
    rstd = lax.rsqrt(_mean(zc * zc) + LN_EPS)
    xhat = zc * rstd
    return xhat * g + b, xhat, rstd


def _ln_bwd(dy, xhat, rstd, g):
    dxhat = dy * g
    return rstd * (dxhat - _mean(dxhat) - xhat * _mean(dxhat * xhat))


_GELU_C = math.sqrt(2.0 / math.pi)


def _gelu(x):
    t = jnp.tanh(_GELU_C * (x + 0.044715 * (x * x * x)))
    return 0.5 * x * (1.0 + t)


def _gelu_grad(x):
    t = jnp.tanh(_GELU_C * (x + 0.044715 * (x * x * x)))
    return 0.5 * (1.0 + t) + 0.5 * x * (1.0 - t * t) * (_GELU_C * (1.0 + 3.0 * 0.044715 * (x * x)))


def _swap_halves(t):
    n = t.shape[1]
    lane = lax.broadcasted_iota(jnp.int32, t.shape, 1)
    return jnp.where((lane % HEAD_DIM) < HEAD_DIM // 2, pltpu.roll(t, n - HEAD_DIM // 2, 1), pltpu.roll(t, HEAD_DIM // 2, 1))


def _rope(t, cos, sin_signed):
    return t * cos + _swap_halves(t) * sin_signed


def _rope_bwd(d, cos, sin_signed):
    return d * cos - _swap_halves(d) * sin_signed


def _tile_lanes(t, k):
    return jnp.concatenate([t] * k, axis=1)


def _rope_tables(pos_col, inv_freq):
    t_tok = pos_col.shape[0]
    tm = min(512, t_tok)

    def body(pos_ref, invf_ref, cos_ref, sin_ref):
        ang = pos_ref[...].astype(F32) * invf_ref[...]
        lane = lax.broadcasted_iota(jnp.int32, ang.shape, 1)
        sign = jnp.where((lane % HEAD_DIM) < HEAD_DIM // 2, -1.0, 1.0)
        cos_ref[...] = jnp.cos(ang)
        sin_ref[...] = jnp.sin(ang) * sign

    return pl.pallas_call(
        body, name="rope_tables", grid=(t_tok // tm,),
        in_specs=[pl.BlockSpec((tm, 1), lambda i: (i, 0)), pl.BlockSpec((1, LANES), lambda i: (0, 0))],
        out_specs=[pl.BlockSpec((tm, LANES), lambda i: (i, 0))] * 2,
        out_shape=[jax.ShapeDtypeStruct((t_tok, LANES), F32)] * 2,
        compiler_params=_params("parallel"),
    )(pos_col, inv_freq)


def _proj_in(x2, w_in_b):
    t_tok, d = x2.shape
    d_in = w_in_b.shape[1]
    tm = min(512, t_tok)

    def body(x_ref, w_ref, h_ref, xb_ref):
        xb = x_ref[...].astype(BF16)
        xb_ref[...] = xb
        h_ref[...] = _dot(xb, w_ref[...])

    return pl.pallas_call(
        body, name="proj_in", grid=(t_tok // tm,),
        in_specs=[pl.BlockSpec((tm, d), lambda i: (i, 0)), pl.BlockSpec((d, d_in), lambda i: (0, 0))],
        out_specs=[pl.BlockSpec((tm, d_in), lambda i: (i, 0)), pl.BlockSpec((tm, d), lambda i: (i, 0))],
        out_shape=[jax.ShapeDtypeStruct((t_tok, d_in), F32), jax.ShapeDtypeStruct((t_tok, d), BF16)],
        compiler_params=_params("parallel"),
    )(x2, w_in_b)


def _h_specs(nb):
    kv_col = COL_K // (2 * D_KV)
    return [
        pl.BlockSpec((BLK, D_GMLP), lambda i: (i, 0)),
        pl.BlockSpec((BLK, D_GMLP), lambda i: (i, 1)),
        pl.BlockSpec((BLK, D_ATTN), lambda i: (i, 2)),
        pl.BlockSpec((BLK, 2 * D_KV), lambda i: (i, kv_col)),
        pl.BlockSpec((BLK, 2 * D_KV), lambda i: (jnp.maximum(i - 1, 0), kv_col)),
    ]


def _table_specs():
    return [
        pl.BlockSpec((BLK, LANES), lambda i: (i, 0)),
        pl.BlockSpec((BLK, LANES), lambda i: (i, 0)),
        pl.BlockSpec((BLK, LANES), lambda i: (jnp.maximum(i - 1, 0), 0)),
        pl.BlockSpec((BLK, LANES), lambda i: (jnp.maximum(i - 1, 0), 0)),
    ]


def _band_mask(i):
    qi = lax.broadcasted_iota(jnp.int32, (BLK, 2 * BLK), 0)
    si = lax.broadcasted_iota(jnp.int32, (BLK, 2 * BLK), 1)
    dist = qi + BLK - si
    return (dist >= 0) & (dist < BLK) & ((si >= BLK) | (i > 0))


def _roped_keys(kvc, kvp, cosc, sinc, cosp, sinp):
    k = jnp.concatenate([_rope(kvp[:, :D_KV], cosp, sinp), _rope(kvc[:, :D_KV], cosc, sinc)], axis=0)
    v = jnp.concatenate([kvp[:, D_KV:], kvc[:, D_KV:]], axis=0)
    return k, v


def _head_place(hh):
    chunk, half, kv = hh // 2, hh % 2, hh // (N_HEADS // N_KV_HEADS)
    return chunk, half, kv != half


def _softmax_sink(s, sink):
    m = jnp.maximum(jnp.max(s, axis=-1, keepdims=True), sink)
    e = jnp.exp(s - m)
    es = jnp.exp(sink - m)
    denom = jnp.sum(e, axis=-1, keepdims=True) + es
    return e / denom, es / denom


def _mixer_fwd(h, cos_t, sin_t, w_spatial, bias_full, vln_g, vln_b, sinks):
    t_tok = h.shape[0]
    nb = t_tok // BLK

    def body(sinks_ref, u_ref, vg_ref, q_ref, kvc_ref, kvp_ref, cosc_ref, sinc_ref, cosp_ref, sinp_ref,
             wsp_ref, bias_ref, g_ref, b_ref, cat_ref):
        i = pl.program_id(0)
        lo = lax.broadcasted_iota(jnp.int32, (BLK, LANES), 1) < HEAD_DIM
        row = lax.broadcasted_iota(jnp.int32, (BLK, BLK), 0)
        col = lax.broadcasted_iota(jnp.int32, (BLK, BLK), 1)
        causal = row >= col

        ua = _gelu(u_ref[...])
        vp, _, _ = _ln_fwd(_gelu(vg_ref[...]), g_ref[...], b_ref[...])
        vpb = vp.astype(BF16)
        for c in range(D_GMLP // LANES):
            sl = slice(c * LANES, (c + 1) * LANES)
            w0 = jnp.where(causal, wsp_ref[2 * c], 0.0).astype(BF16)
            w1 = jnp.where(causal, wsp_ref[2 * c + 1], 0.0).astype(BF16)
            mixed = jnp.where(lo, _dot(w0, vpb[:, sl]), _dot(w1, vpb[:, sl])) + bias_ref[:, sl]
            cat_ref[:, sl] = (ua[:, sl] * mixed).astype(BF16)

        cosc, sinc = cosc_ref[...], sinc_ref[...]
        qr = _rope(q_ref[...], _tile_lanes(cosc, D_ATTN // LANES), _tile_lanes(sinc, D_ATTN // LANES))
        k, v = _roped_keys(kvc_ref[...], kvp_ref[...], cosc, sinc, cosp_ref[...], sinp_ref[...])
        kb = (k.astype(BF16), pltpu.roll(k, HEAD_DIM, 1).astype(BF16))
        vb = (v.astype(BF16), pltpu.roll(v, HEAD_DIM, 1).astype(BF16))
        mask = _band_mask(i)
        outs = []
        for hh in range(N_HEADS):
            chunk, half, other = _head_place(hh)
            qc = qr[:, chunk * LANES:(chunk + 1) * LANES]
            qm = jnp.where(lo if half == 0 else ~lo, qc, 0.0).astype(BF16)
            s = jnp.where(mask, _dot(qm, kb[other], NT) * SCORE_SCALE, NEG_INF)
            p, _ = _softmax_sink(s, sinks_ref[hh])
            outs.append(_dot(p.astype(BF16), vb[other]))
        for c in range(D_ATTN // LANES):
            cat_ref[:, D_GMLP + c * LANES:D_GMLP + (c + 1) * LANES] = jnp.where(lo, outs[2 * c], outs[2 * c + 1]).astype(BF16)

    full = lambda shape: pl.BlockSpec(shape, lambda i: (0,) * len(shape))
    return pl.pallas_call(
        body, name="mixer_fwd", grid=(nb,),
        in_specs=[pl.BlockSpec(memory_space=pltpu.SMEM)] + _h_specs(nb) + _table_specs() + [
            full((N_HEADS, BLK, BLK)), full((BLK, D_GMLP)), full((1, D_GMLP)), full((1, D_GMLP))],
        out_specs=pl.BlockSpec((BLK, D_GMLP + D_ATTN), lambda i: (i, 0)),
        out_shape=jax.ShapeDtypeStruct((t_tok, D_GMLP + D_ATTN), BF16),
        compiler_params=_params("parallel"),
    )(sinks, h, h, h, h, h, cos_t, sin_t, cos_t, sin_t, w_spatial, bias_full, vln_g, vln_b)


def _proj_out(cat_b, x2, w_out_b):
    t_tok, d = x2.shape
    tm = min(512, t_tok)

    def body(cat_ref, x_ref, w_ref, z_ref):
        z_ref[...] = ALPHA * x_ref[...] + _dot(cat_ref[...], w_ref[...])

    return pl.pallas_call(
        body, name="proj_out", grid=(t_tok // tm,),
        in_specs=[pl.BlockSpec((tm, cat_b.shape[1]), lambda i: (i, 0)), pl.BlockSpec((tm, d), lambda i: (i, 0)),
                  pl.BlockSpec(w_out_b.shape, lambda i: (0, 0))],
        out_specs=pl.BlockSpec((tm, d), lambda i: (i, 0)),
        out_shape=jax.ShapeDtypeStruct((t_tok, d), F32),
        compiler_params=_params("parallel"),
    )(cat_b, x2, w_out_b)


def _ffn_fwd_bwd(z1, target, w1_b, w2_b, ln1_g, ln1_b, ln2_g, ln2_b):
    t_tok, d = z1.shape
    n_chunk, _, fc = w1_b.shape
    f = n_chunk * fc
    tm = min(256, t_tok)

    def body(z1_ref, tgt_ref, w1_ref, w2_ref, g1_ref, b1_ref, g2_ref, b2_ref,
             act_ref, dpre_ref, x1b_ref, dz2b_ref, dz1_ref, stats_ref, r_scr):
        @pl.when(pl.program_id(0) == 0)
        def _():
            stats_ref[...] = jnp.zeros_like(stats_ref)

        g1, g2 = g1_ref[...], g2_ref[...]
        x1, xhat1, rstd1 = _ln_fwd(z1_ref[...], g1, b1_ref[...])
        x1b = x1.astype(BF16)
        x1b_ref[...] = x1b
        ff = jnp.zeros((tm, d), F32)
        for j in range(n_chunk):
            r = jnp.maximum(_dot(x1b, w1_ref[j]), 0.0)
            r_scr[:, j * fc:(j + 1) * fc] = r
            act = (r * r).astype(BF16)
            act_ref[:, j * fc:(j + 1) * fc] = act
            ff = ff + _dot(act, w2_ref[j])
        y, xhat2, rstd2 = _ln_fwd(ALPHA * x1 + ff, g2, b2_ref[...])
        diff = y - tgt_ref[...]
        loss = 0.5 * jnp.sum(jnp.sum(diff * diff, axis=-1, keepdims=True) / d, axis=0, keepdims=True)
        dy = diff / d
        dz2 = _ln_bwd(dy, xhat2, rstd2, g2)
        dz2b = dz2.astype(BF16)
        dz2b_ref[...] = dz2b
        dx1 = ALPHA * dz2
        for j in range(n_chunk):
            dpre = (_dot(dz2b, w2_ref[j], NT) * (2.0 * r_scr[:, j * fc:(j + 1) * fc])).astype(BF16)
            dpre_ref[:, j * fc:(j + 1) * fc] = dpre
            dx1 = dx1 + _dot(dpre, w1_ref[j], NT)
        dz1_ref[...] = _ln_bwd(dx1, xhat1, rstd1, g1)
        stats_ref[0:1, :] += jnp.sum(dx1 * xhat1, axis=0, keepdims=True)
        stats_ref[1:2, :] += jnp.sum(dx1, axis=0, keepdims=True)
        stats_ref[2:3, :] += jnp.sum(dy * xhat2, axis=0, keepdims=True)
        stats_ref[3:4, :] += jnp.sum(dy, axis=0, keepdims=True)
        stats_ref[4:5, :] += jnp.broadcast_to(loss, (1, d))

    tok = lambda w: pl.BlockSpec((tm, w), lambda i: (i, 0))
    vec = pl.BlockSpec((1, d), lambda i: (0, 0))
    return pl.pallas_call(
        body, name="ffn_fwd_bwd", grid=(t_tok // tm,),
        in_specs=[tok(d), tok(d),
                  pl.BlockSpec(w1_b.shape, lambda i: (0, 0, 0), pipeline_mode=pl.Buffered(1)),
                  pl.BlockSpec(w2_b.shape, lambda i: (0, 0, 0), pipeline_mode=pl.Buffered(1)),
                  vec, vec, vec, vec],
        out_specs=[tok(f), tok(f), tok(d), tok(d), tok(d), pl.BlockSpec((8, d), lambda i: (0, 0))],
        out_shape=[jax.ShapeDtypeStruct((t_tok, f), BF16), jax.ShapeDtypeStruct((t_tok, f), BF16),
                   jax.ShapeDtypeStruct((t_tok, d), BF16), jax.ShapeDtypeStruct((t_tok, d), BF16),
                   jax.ShapeDtypeStruct((t_tok, d), F32), jax.ShapeDtypeStruct((8, d), F32)],
        scratch_shapes=[pltpu.VMEM((tm, f), F32)],
        compiler_params=_params("arbitrary"),
    )(z1, target, w1_b, w2_b, ln1_g, ln1_b, ln2_g, ln2_b)


def _ffn_wgrad(x1b, dpre_b, act_b, dz2b, n_chunk):
    t_tok, d = x1b.shape
    fc = dpre_b.shape[1] // n_chunk
    tk = min(2048, t_tok)

    def body(x1_ref, dpre_ref, act_ref, dz2_ref, g1_ref, g2_ref):
        @pl.when(pl.program_id(1) == 0)
        def _():
            g1_ref[...] = jnp.zeros_like(g1_ref)
            g2_ref[...] = jnp.zeros_like(g2_ref)

        g1_ref[...] += _dot(x1_ref[...], dpre_ref[...], TN)
        g2_ref[...] += _dot(act_ref[...], dz2_ref[...], TN)

    return pl.pallas_call(
        body, name="ffn_wgrad", grid=(n_chunk, t_tok // tk),
        in_specs=[pl.BlockSpec((tk, d), lambda j, t: (t, 0)), pl.BlockSpec((tk, fc), lambda j, t: (t, j)),
                  pl.BlockSpec((tk, fc), lambda j, t: (t, j)), pl.BlockSpec((tk, d), lambda j, t: (t, 0))],
        out_specs=[pl.BlockSpec((None, d, fc), lambda j, t: (j, 0, 0)), pl.BlockSpec((None, fc, d), lambda j, t: (j, 0, 0))],
        out_shape=[jax.ShapeDtypeStruct((n_chunk, d, fc), F32), jax.ShapeDtypeStruct((n_chunk, fc, d), F32)],
        compiler_params=_params("parallel", "arbitrary"),
    )(x1b, dpre_b, act_b, dz2b)


def _proj_out_bwd(dz1, cat_b, w_out_b):
    t_tok, d = dz1.shape
    d_mix = cat_b.shape[1]
    tm = min(512, t_tok)

    def body(dz1_ref, cat_ref, w_ref, dcat_ref, gw_ref):
        @pl.when(pl.program_id(0) == 0)
        def _():
            gw_ref[...] = jnp.zeros_like(gw_ref)

        dzb = dz1_ref[...].astype(BF16)
        dcat_ref[...] = _dot(dzb, w_ref[...], NT)
        gw_ref[...] += _dot(cat_ref[...], dzb, TN)

    return pl.pallas_call(
        body, name="proj_out_bwd", grid=(t_tok // tm,),
        in_specs=[pl.BlockSpec((tm, d), lambda i: (i, 0)), pl.BlockSpec((tm, d_mix), lambda i: (i, 0)),
                  pl.BlockSpec((d_mix, d), lambda i: (0, 0))],
        out_specs=[pl.BlockSpec((tm, d_mix), lambda i: (i, 0)), pl.BlockSpec((d_mix, d), lambda i: (0, 0))],
        out_shape=[jax.ShapeDtypeStruct((t_tok, d_mix), F32), jax.ShapeDtypeStruct((d_mix, d), F32)],
        compiler_params=_params("arbitrary"),
    )(dz1, cat_b, w_out_b)


def _mixer_bwd(dcat, h, cos_t, sin_t, w_spatial, bias_full, vln_g, vln_b, sinks):
    t_tok = h.shape[0]
    nb = t_tok // BLK

    def body(sinks_ref, dcat_ref, u_ref, vg_ref, q_ref, kvc_ref, kvp_ref, cosc_ref, sinc_ref, cosp_ref, sinp_ref,
             wsp_ref, bias_ref, g_ref, b_ref, dh_ref, dkvp_ref, gws_ref, gbias_ref, gvln_ref, gsink_ref):
        i = pl.program_id(0)

        @pl.when(i == 0)
        def _():
            gws_ref[...] = jnp.zeros_like(gws_ref)
            gbias_ref[...] = jnp.zeros_like(gbias_ref)
            gvln_ref[...] = jnp.zeros_like(gvln_ref)
            gsink_ref[...] = jnp.zeros_like(gsink_ref)

        lo = lax.broadcasted_iota(jnp.int32, (BLK, LANES), 1) < HEAD_DIM
        row = lax.broadcasted_iota(jnp.int32, (BLK, BLK), 0)
        col = lax.broadcasted_iota(jnp.int32, (BLK, BLK), 1)
        causal = row >= col

        u, vg = u_ref[...], vg_ref[...]
        ua = _gelu(u)
        g = g_ref[...]
        vp, vhat, rstd = _ln_fwd(_gelu(vg), g, b_ref[...])
        vpb = vp.astype(BF16)
        da = dcat_ref[:, :D_GMLP]
        dmixed = da * ua
        gbias_ref[...] += dmixed
        dvp_parts = []
        for c in range(D_GMLP // LANES):
            sl = slice(c * LANES, (c + 1) * LANES)
            w0 = jnp.where(causal, wsp_ref[2 * c], 0.0).astype(BF16)
            w1 = jnp.where(causal, wsp_ref[2 * c + 1], 0.0).astype(BF16)
            mixed = jnp.where(lo, _dot(w0, vpb[:, sl]), _dot(w1, vpb[:, sl])) + bias_ref[:, sl]
            dh_ref[:, COL_U + c * LANES:COL_U + (c + 1) * LANES] = da[:, sl] * mixed * _gelu_grad(u[:, sl])
            dm = dmixed[:, sl]
            dm0 = jnp.where(lo, dm, 0.0).astype(BF16)
            dm1 = jnp.where(lo, 0.0, dm).astype(BF16)
            gws_ref[2 * c] += jnp.where(causal, _dot(dm0, vpb[:, sl], NT), 0.0)
            gws_ref[2 * c + 1] += jnp.where(causal, _dot(dm1, vpb[:, sl], NT), 0.0)
            dvp_parts.append(_dot(w0, dm0, TN) + _dot(w1, dm1, TN))
        dvp = jnp.concatenate(dvp_parts, axis=1)
        gvln_ref[0:1, :] += jnp.sum(dvp * vhat, axis=0, keepdims=True)
        gvln_ref[1:2, :] += jnp.sum(dvp, axis=0, keepdims=True)
        dh_ref[:, COL_V:COL_V + D_GMLP] = _ln_bwd(dvp, vhat, rstd, g) * _gelu_grad(vg)

        cosc, sinc, cosp, sinp = cosc_ref[...], sinc_ref[...], cosp_ref[...], sinp_ref[...]
        cos4, sin4 = _tile_lanes(cosc, D_ATTN // LANES), _tile_lanes(sinc, D_ATTN // LANES)
        qr = _rope(q_ref[...], cos4, sin4)
        k, v = _roped_keys(kvc_ref[...], kvp_ref[...], cosc, sinc, cosp, sinp)
        kb = (k.astype(BF16), pltpu.roll(k, HEAD_DIM, 1).astype(BF16))
        vb = (v.astype(BF16), pltpu.roll(v, HEAD_DIM, 1).astype(BF16))
        mask = _band_mask(i)
        dk = jnp.zeros((2 * BLK, D_KV), F32)
        dv = jnp.zeros((2 * BLK, D_KV), F32)
        dq_parts = []
        for hh in range(N_HEADS):
            chunk, half, other = _head_place(hh)
            mine = lo if half == 0 else ~lo
            qm = jnp.where(mine, qr[:, chunk * LANES:(chunk + 1) * LANES], 0.0).astype(BF16)
            dom = jnp.where(mine, dcat_ref[:, D_GMLP + chunk * LANES:D_GMLP + (chunk + 1) * LANES], 0.0).astype(BF16)
            s = jnp.where(mask, _dot(qm, kb[other], NT) * SCORE_SCALE, NEG_INF)
            p, p_sink = _softmax_sink(s, sinks_ref[hh])
            dp = _dot(dom, vb[other], NT)
            delta = jnp.sum(p * dp, axis=-1, keepdims=True)
            ds = (p * (dp - delta) * SCORE_SCALE).astype(BF16)
            gsink_ref[hh:hh + 1, :] += jnp.broadcast_to(-jnp.sum(p_sink * delta, axis=0, keepdims=True), (1, LANES))
            dq_parts.append(_dot(ds, kb[other]))
            dk_h = _dot(ds, qm, TN)
            dv_h = _dot(p.astype(BF16), dom, TN)
            if other:
                dk_h, dv_h = pltpu.roll(dk_h, HEAD_DIM, 1), pltpu.roll(dv_h, HEAD_DIM, 1)
            dk, dv = dk + dk_h, dv + dv_h
        dq = jnp.concatenate([jnp.where(lo, dq_parts[2 * c], dq_parts[2 * c + 1]) for c in range(D_ATTN // LANES)], axis=1)
        dh_ref[:, COL_Q:COL_Q + D_ATTN] = _rope_bwd(dq, cos4, sin4)
        dh_ref[:, COL_K:COL_K + D_KV] = _rope_bwd(dk[BLK:], cosc, sinc)
        dh_ref[:, COL_K + D_KV:COL_K + 2 * D_KV] = dv[BLK:]
        dkvp_ref[:, :D_KV] = _rope_bwd(dk[:BLK], cosp, sinp)
        dkvp_ref[:, D_KV:] = dv[:BLK]

    full = lambda shape: pl.BlockSpec(shape, lambda i: (0,) * len(shape))
    return pl.pallas_call(
        body, name="mixer_bwd", grid=(nb,),
        in_specs=[pl.BlockSpec(memory_space=pltpu.SMEM), pl.BlockSpec((BLK, D_GMLP + D_ATTN), lambda i: (i, 0))]
        + _h_specs(nb) + _table_specs()
        + [full((N_HEADS, BLK, BLK)), full((BLK, D_GMLP)), full((1, D_GMLP)), full((1, D_GMLP))],
        out_specs=[pl.BlockSpec((BLK, D_IN), lambda i: (i, 0)),
                   pl.BlockSpec((BLK, 2 * D_KV), lambda i: ((i + nb - 1) % nb, 0)),
                   full((N_HEADS, BLK, BLK)), full((BLK, D_GMLP)), full((8, D_GMLP)), full((8, LANES))],
        out_shape=[jax.ShapeDtypeStruct((t_tok, D_IN), F32), jax.ShapeDtypeStruct((t_tok, 2 * D_KV), F32),
                   jax.ShapeDtypeStruct((N_HEADS, BLK, BLK), F32), jax.ShapeDtypeStruct((BLK, D_GMLP), F32),
                   jax.ShapeDtypeStruct((8, D_GMLP), F32), jax.ShapeDtypeStruct((8, LANES), F32)],
        compiler_params=_params("arbitrary"),
    )(sinks, dcat, h, h, h, h, h, cos_t, sin_t, cos_t, sin_t, w_spatial, bias_full, vln_g, vln_b)


def _proj_in_bwd(dh, dkvp, dz1, xb, w_in_b):
    t_tok, d = dz1.shape
    d_in = dh.shape[1]
    tm = min(512, t_tok)
    kv_col = COL_K // (2 * D_KV)

    def body(dh_ref, dkvp_ref, dz1_ref, xb_ref, w_ref, dx_ref, gw_ref):
        @pl.when(pl.program_id(0) == 0)
        def _():
            gw_ref[...] = jnp.zeros_like(gw_ref)

        dhb = jnp.concatenate([dh_ref[:, :COL_K], dh_ref[:, COL_K:] + dkvp_ref[...]], axis=1).astype(BF16)
        dx_ref[...] = ALPHA * dz1_ref[...] + _dot(dhb, w_ref[...], NT)
        gw_ref[...] += _dot(xb_ref[...], dhb, TN)

    return pl.pallas_call(
        body, name="proj_in_bwd", grid=(t_tok // tm,),
        in_specs=[pl.BlockSpec((tm, d_in), lambda i: (i, 0)), pl.BlockSpec((tm, 2 * D_KV), lambda i: (i, 0)),
                  pl.BlockSpec((tm, d), lambda i: (i, 0)), pl.BlockSpec((tm, d), lambda i: (i, 0)),
                  pl.BlockSpec((d, d_in), lambda i: (0, 0))],
        out_specs=[pl.BlockSpec((tm, d), lambda i: (i, 0)), pl.BlockSpec((d, d_in), lambda i: (0, 0))],
        out_shape=[jax.ShapeDtypeStruct((t_tok, d), F32), jax.ShapeDtypeStruct((d, d_in), F32)],
        compiler_params=_params("arbitrary"),
    )(dh, dkvp, dz1, xb, w_in_b)


def _adamw(w, g, m, v):
    m = ADAM_B1 * m + (1.0 - ADAM_B1) * g
    v = ADAM_B2 * v + (1.0 - ADAM_B2) * (g * g)
    m_hat = m / (1.0 - ADAM_B1 ** ADAM_STEP)
    v_hat = v / (1.0 - ADAM_B2 ** ADAM_STEP)
    delta = -ADAM_LR * (m_hat / (jnp.sqrt(v_hat) + ADAM_EPS) + ADAM_WD * w)
    return delta, m, v


def _adamw_shard(name, own4, recv3, w, m, v, chip):
    r, c = w.shape
    tr = r if r <= 512 else 512

    def body(chip_ref, own_ref, recv_ref, w_ref, m_ref, v_ref, g_out, d_out, m_out, v_out):
        g = ((own_ref[...] + recv_ref[0]) + recv_ref[1]) + recv_ref[2]
        delta, m_new, v_new = _adamw(w_ref[...], g, m_ref[...], v_ref[...])
        g_out[...] = g
        d_out[...] = delta
        m_out[...] = m_new
        v_out[...] = v_new

    blk = pl.BlockSpec((tr, c), lambda i, chip_ref: (i, 0))
    grid_spec = pltpu.PrefetchScalarGridSpec(
        num_scalar_prefetch=1, grid=(r // tr,),
        in_specs=[pl.BlockSpec((None, tr, c), lambda i, chip_ref: (chip_ref[0], i, 0)),
                  pl.BlockSpec((3, tr, c), lambda i, chip_ref: (0, i, 0)), blk, blk, blk],
        out_specs=[blk] * 4)
    return pl.pallas_call(
        body, name=name, grid_spec=grid_spec,
        out_shape=[jax.ShapeDtypeStruct((r, c), F32)] * 4,
        compiler_params=_params("parallel"),
    )(chip, own4, recv3, w, m, v)


def _adamw_small(parts, w, m, v):
    n, r, c = parts.shape

    def body(p_ref, w_ref, m_ref, v_ref, g_out, d_out, m_out, v_out):
        g = p_ref[0]
        for k in range(1, n):
            g = g + p_ref[k]
        delta, m_new, v_new = _adamw(w_ref[...], g, m_ref[...], v_ref[...])
        g_out[...] = g
        d_out[...] = delta
        m_out[...] = m_new
        v_out[...] = v_new

    return pl.pallas_call(
        body, name="adamw_small",
        out_shape=[jax.ShapeDtypeStruct((r, c), F32)] * 4,
        compiler_params=pltpu.CompilerParams(vmem_limit_bytes=VMEM_LIMIT),
    )(parts, w, m, v)


def _pair_sum(name, parts, recv, core):
    _, r, c = parts.shape
    tr = r if r <= 512 else 512

    def body(core_ref, a_ref, b_ref, o_ref):
        o_ref[...] = a_ref[...] + b_ref[...]

    grid_spec = pltpu.PrefetchScalarGridSpec(
        num_scalar_prefetch=1, grid=(4, r // tr),
        in_specs=[pl.BlockSpec((None, tr, c), lambda q, i, core_ref: (2 * q + core_ref[0], i, 0)),
                  pl.BlockSpec((None, tr, c), lambda q, i, core_ref: (q, i, 0))],
        out_specs=pl.BlockSpec((None, tr, c), lambda q, i, core_ref: (q, i, 0)))
    return pl.pallas_call(
        body, name=name, grid_spec=grid_spec,
        out_shape=jax.ShapeDtypeStruct((4, r, c), F32),
        compiler_params=_params("parallel", "parallel"),
    )(core, parts, recv)


ANY = pl.BlockSpec(memory_space=pl.ANY)


def _place():
    x, y, c = lax.axis_index("x"), lax.axis_index("y"), lax.axis_index("c")
    return x, y, c


def _all_gather(name, arrs):
    n = len(arrs)

    def body(*refs):
        ins, outs = refs[:n], refs[n:2 * n]
        send_sems, recv_sems, local_sems = refs[2 * n:]
        x, y, c = _place()
        me, sibling = (x, y, c), (x, y, 1 - c)
        chips = [(1 - x, y), (x, 1 - y), (1 - x, 1 - y)]

        def copy(a, k, block, to, src=None):
            px, py, pc = block
            dst = outs[a].at[4 * px + 2 * py + pc]
            return pltpu.make_async_remote_copy(
                src_ref=dst if src is None else src, dst_ref=dst,
                send_sem=send_sems.at[a, k], recv_sem=recv_sems.at[a, k], device_id=to, device_id_type=MESH)

        mine = [pltpu.make_async_copy(ins[a], outs[a].at[4 * x + 2 * y + c], local_sems.at[a]) for a in range(n)]
        for cp in mine:
            cp.start()
        first = []
        for a in range(n):
            first.append(copy(a, 0, me, sibling, src=ins[a]))
            first += [copy(a, 1 + j, me, (*chip, c), src=ins[a]) for j, chip in enumerate(chips)]
        for cp in first:
            cp.start()
        passed = []
        for j, chip in enumerate(chips):
            for a in range(n):
                copy(a, 1 + j, (*chip, c), me).wait_recv()
                fwd = copy(a, 4 + j, (*chip, c), sibling)
                fwd.start()
                passed.append(fwd)
        for a in range(n):
            copy(a, 0, sibling, me).wait_recv()
        for j, chip in enumerate(chips):
            for a in range(n):
                copy(a, 4 + j, (*chip, 1 - c), me).wait_recv()
        for cp in first + passed:
            cp.wait_send()
        for cp in mine:
            cp.wait()

    return pl.pallas_call(
        body, name=name,
        in_specs=[ANY] * n, out_specs=[ANY] * n,
        out_shape=[jax.ShapeDtypeStruct((N_DEV,) + a.shape, a.dtype) for a in arrs],
        scratch_shapes=[pltpu.SemaphoreType.DMA((n, 7)), pltpu.SemaphoreType.DMA((n, 7)), pltpu.SemaphoreType.DMA((n,))],
    )(*arrs)


def _exchange_sibling(name, parts):
    n = len(parts)

    def body(*refs):
        ins, outs = refs[:n], refs[n:2 * n]
        send_sems, recv_sems = refs[2 * n:]
        x, y, c = _place()
        copies = []
        for a in range(n):
            for q in range(4):
                copies.append(pltpu.make_async_remote_copy(
                    src_ref=ins[a].at[2 * q + (1 - c)], dst_ref=outs[a].at[q],
                    send_sem=send_sems.at[a, q], recv_sem=recv_sems.at[a, q],
                    device_id=(x, y, 1 - c), device_id_type=MESH))
        for cp in copies:
            cp.start()
        for cp in copies:
            cp.wait()

    return pl.pallas_call(
        body, name=name,
        in_specs=[ANY] * n, out_specs=[ANY] * n,
        out_shape=[jax.ShapeDtypeStruct((4,) + p.shape[1:], p.dtype) for p in parts],
        scratch_shapes=[pltpu.SemaphoreType.DMA((n, 4)), pltpu.SemaphoreType.DMA((n, 4))],
    )(*parts)


def _exchange_chips(name, chip_parts):
    n = len(chip_parts)

    def body(*refs):
        ins, outs = refs[:n], refs[n:2 * n]
        send_sems, recv_sems = refs[2 * n:]
        x, y, c = _place()
        chips = [(1 - x, y), (x, 1 - y), (1 - x, 1 - y)]
        copies = []
        for a in range(n):
            for k, (px, py) in enumerate(chips):
                copies.append(pltpu.make_async_remote_copy(
                    src_ref=ins[a].at[2 * px + py], dst_ref=outs[a].at[k],
                    send_sem=send_sems.at[a, k], recv_sem=recv_sems.at[a, k],
                    device_id=(px, py, c), device_id_type=MESH))
        for cp in copies:
            cp.start()
        for cp in copies:
            cp.wait()

    return pl.pallas_call(
        body, name=name,
        in_specs=[ANY] * n, out_specs=[ANY] * n,
        out_shape=[jax.ShapeDtypeStruct((3,) + p.shape[1:], p.dtype) for p in chip_parts],
        scratch_shapes=[pltpu.SemaphoreType.DMA((n, 3)), pltpu.SemaphoreType.DMA((n, 3))],
    )(*chip_parts)


def _pack_small(v_ln_g, v_ln_b, w_spatial, b_spatial, sinks, ln1_g, ln1_b, ln2_g, ln2_b):
    d = ln1_g.shape[-1]
    rows = [w_spatial.reshape(-1, d),
            jnp.concatenate([v_ln_g.reshape(1, -1), v_ln_b.reshape(1, -1)], axis=1),
            b_spatial.reshape(1, -1),
            ln1_g.reshape(1, d), ln1_b.reshape(1, d), ln2_g.reshape(1, d), ln2_b.reshape(1, d),
            jnp.pad(sinks.reshape(1, -1), ((0, 1), (0, d - N_HEADS)))]
    return jnp.concatenate(rows, axis=0)


def _unpack_small(p):
    n_ws = N_HEADS * BLK * BLK // p.shape[1]
    w_spatial = p[:n_ws].reshape(1, N_HEADS, BLK, BLK)
    v_ln_g, v_ln_b = p[n_ws:n_ws + 1, :D_GMLP], p[n_ws:n_ws + 1, D_GMLP:2 * D_GMLP]
    b_spatial = p[n_ws + 1, :N_HEADS * BLK].reshape(1, N_HEADS, BLK)
    ln1_g, ln1_b, ln2_g, ln2_b = (p[n_ws + 2 + k:n_ws + 3 + k] for k in range(4))
    sinks = p[n_ws + 6:n_ws + 7, :N_HEADS]
    return [v_ln_g, v_ln_b, w_spatial, b_spatial, sinks, None, ln1_g, ln1_b, None, None, ln2_g, ln2_b]


def kernel(x, positions, w_in, v_ln_g, v_ln_b, w_spatial, b_spatial, sinks, w_out, ln1_g, ln1_b, w_ff1, w_ff2, ln2_g, ln2_b, loss_target, m_w_in, m_v_ln_g, m_v_ln_b, m_w_spatial, m_b_spatial, m_sinks, m_w_out, m_ln1_g, m_ln1_b, m_w_ff1, m_w_ff2, m_ln2_g, m_ln2_b, v_w_in, v_v_ln_g, v_v_ln_b, v_w_spatial, v_b_spatial, v_sinks, v_w_out, v_ln1_g, v_ln1_b, v_w_ff1, v_w_ff2, v_ln2_g, v_ln2_b):
    _, t_tok, d = x.shape
    xi, yi, ci = _place()
    core = ci.astype(jnp.int32).reshape(1)
    chip = (2 * xi + yi).astype(jnp.int32).reshape(1)

    big = [w_in[0], w_out[0], w_ff1[0], w_ff2[0]]
    g_in, g_out, w1_b, w2_b = _all_gather("gather_weights", [w.astype(BF16) for w in big])
    w_in_b = g_in.transpose(1, 0, 2).reshape(d, D_IN)
    w_out_b = g_out.reshape(-1, d)

    x2 = x.reshape(t_tok, d)
    target = loss_target.reshape(t_tok, d)
    inv_freq = ROPE_THETA ** (-jnp.arange(0, HEAD_DIM, 2, dtype=F32) / HEAD_DIM)
    cos_t, sin_t = _rope_tables(positions.reshape(t_tok, 1), jnp.tile(inv_freq, LANES // (HEAD_DIM // 2)).reshape(1, LANES))
    bias_full = jnp.repeat(b_spatial[0].T, HEAD_DIM, axis=1)
    wsp, sink_vec = w_spatial[0], sinks[0]

    h, xb = _proj_in(x2, w_in_b)
    cat_b = _mixer_fwd(h, cos_t, sin_t, wsp, bias_full, v_ln_g, v_ln_b, sink_vec)
    z1 = _proj_out(cat_b, x2, w_out_b)
    act_b, dpre_b, x1b, dz2b, dz1, stats = _ffn_fwd_bwd(z1, target, w1_b, w2_b, ln1_g, ln1_b, ln2_g, ln2_b)
    gp_ff1, gp_ff2 = _ffn_wgrad(x1b, dpre_b, act_b, dz2b, N_DEV)
    dcat, gw_out = _proj_out_bwd(dz1, cat_b, w_out_b)
    dh, dkvp, g_wsp, g_bias, g_vln, g_sink = _mixer_bwd(dcat, h, cos_t, sin_t, wsp, bias_full, v_ln_g, v_ln_b, sink_vec)
    grad_x, gw_in = _proj_in_bwd(dh, dkvp, dz1, xb, w_in_b)

    loss = lax.psum(stats[4, 0], ("x", "y", "c"))

    parts = [gw_in.reshape(d, N_DEV, -1).transpose(1, 0, 2), gw_out.reshape(N_DEV, -1, d), gp_ff1, gp_ff2]
    from_sibling = _exchange_sibling("reduce_sibling", parts)
    chip_parts = [_pair_sum("pair_sum_%d" % a, p, r, core) for a, (p, r) in enumerate(zip(parts, from_sibling))]
    from_chips = _exchange_chips("reduce_chips", chip_parts)
    moments = [(m_w_in, v_w_in), (m_w_out, v_w_out), (m_w_ff1, v_w_ff1), (m_w_ff2, v_w_ff2)]
    big_out = [_adamw_shard("adamw_%d" % a, chip_parts[a], from_chips[a], big[a], moments[a][0][0], moments[a][1][0], chip)
               for a in range(4)]

    g_b_spatial = g_bias.reshape(BLK, N_HEADS, HEAD_DIM).sum(axis=-1).T
    small_g = _pack_small(g_vln[0], g_vln[1], g_wsp, g_b_spatial, g_sink[:, 0], stats[0], stats[1], stats[2], stats[3])
    (small_parts,) = _all_gather("gather_small_grads", [small_g])
    small_w = _pack_small(v_ln_g, v_ln_b, w_spatial, b_spatial, sinks, ln1_g, ln1_b, ln2_g, ln2_b)
    small_m = _pack_small(m_v_ln_g, m_v_ln_b, m_w_spatial, m_b_spatial, m_sinks, m_ln1_g, m_ln1_b, m_ln2_g, m_ln2_b)
    small_v = _pack_small(v_v_ln_g, v_v_ln_b, v_w_spatial, v_b_spatial, v_sinks, v_ln1_g, v_ln1_b, v_ln2_g, v_ln2_b)
    small_out = [_unpack_small(p) for p in _adamw_small(small_parts, small_w, small_m, small_v)]

    big_slot = {0: 0, 6: 1, 9: 2, 10: 3}
    outs = [loss, grad_x.reshape(x.shape)]
    for kind in range(4):
        for wi in range(13):
            if wi in big_slot:
                outs.append(big_out[big_slot[wi]][kind][None])
            else:
                outs.append(small_out[kind][wi - 1])
    return tuple(outs)
```

```python
import functools
import math

import jax
import jax.numpy as jnp
from jax import lax
from jax.experimental import pallas as pl
from jax.experimental.pallas import tpu as pltpu

F32 = jnp.float32
BF16 = jnp.bfloat16
MESH = pl.DeviceIdType.MESH

HEAD_DIM = 64
N_HEADS = 8
N_KV_HEADS = 2
BLK = 128
D_GMLP = N_HEADS * HEAD_DIM
D_ATTN = N_HEADS * HEAD_DIM
D_KV = N_KV_HEADS * HEAD_DIM
D_IN = 2 * D_GMLP + D_ATTN + 2 * D_KV
COL_U, COL_V, COL_Q, COL_K = 0, D_GMLP, 2 * D_GMLP, 2 * D_GMLP + D_ATTN
ROPE_THETA = 10000.0
LN_EPS = 1e-5
ALPHA = 2.0 ** 0.25
NEG_INF = -1e30
SCORE_SCALE = 1.0 / math.sqrt(HEAD_DIM)
ADAM_LR, ADAM_B1, ADAM_B2, ADAM_EPS, ADAM_WD, ADAM_STEP = 0.001, 0.9, 0.999, 1e-08, 0.01, 10
N_DEV = 8
LANES = 128
VMEM_LIMIT = 56 * 1024 * 1024

NT = (((1,), (1,)), ((), ()))
TN = (((0,), (0,)), ((), ()))


def _params(*sem):
    return pltpu.CompilerParams(dimension_semantics=sem, vmem_limit_bytes=VMEM_LIMIT)


def _dot(a, b, dims=None):
    if dims is None:
        return jnp.dot(a, b, preferred_element_type=F32)
    return lax.dot_general(a, b, dims, preferred_element_type=F32)


def _mean(a):
    return jnp.mean(a, axis=-1, keepdims=True)


def _ln_fwd(z, g, b):
    zc = z - _mean(z)
    rstd = lax.rsqrt(_mean(zc * zc) + LN_EPS)
    xhat = zc * rstd
    return xhat * g + b, xhat, rstd


def _ln_bwd(dy, xhat, rstd, g):
    dxhat = dy * g
    return rstd * (dxhat - _mean(dxhat) - xhat * _mean(dxhat * xhat))


_GELU_C = math.sqrt(2.0 / math.pi)


def _gelu(x):
    t = jnp.tanh(_GELU_C * (x + 0.044715 * (x * x * x)))
    return 0.5 * x * (1.0 + t)


def _gelu_grad(x):
    t = jnp.tanh(_GELU_C * (x + 0.044715 * (x * x * x)))
    return 0.5 * (1.0 + t) + 0.5 * x * (1.0 - t * t) * (_GELU_C * (1.0 + 3.0 * 0.044715 * (x * x)))


def _swap_halves(t):
    n = t.shape[1]
    lane = lax.broadcasted_iota(jnp.int32, t.shape, 1)
    return jnp.where((lane % HEAD_DIM) < HEAD_DIM // 2, pltpu.roll(t, n - HEAD_DIM // 2, 1), pltpu.roll(t, HEAD_DIM // 2, 1))


def _rope(t, cos, sin_signed):
    return t * cos + _swap_halves(t) * sin_signed


def _rope_bwd(d, cos, sin_signed):
    return d * cos - _swap_halves(d) * sin_signed


def _tile_lanes(t, k):
    return jnp.concatenate([t] * k, axis=1)


ANY = pl.BlockSpec(memory_space=pl.ANY)


def _place():
    return lax.axis_index("x"), lax.axis_index("y"), lax.axis_index("c")


class _Comm:
    def __init__(self, ins, outs, sems, start, finish):
        self.ins, self.outs, self.sems, self.start, self.finish = ins, outs, sems, start, finish


def _gather_comm(arrs):
    n = len(arrs)

    def parts(ins, outs, sems):
        send_sems, recv_sems, local_sems = sems
        x, y, c = _place()
        me, sibling = (x, y, c), (x, y, 1 - c)
        chips = [(1 - x, y), (x, 1 - y), (1 - x, 1 - y)]

        def copy(a, k, block, to, src=None):
            px, py, pc = block
            dst = outs[a].at[4 * px + 2 * py + pc]
            return pltpu.make_async_remote_copy(
                src_ref=dst if src is None else src, dst_ref=dst,
                send_sem=send_sems.at[a, k], recv_sem=recv_sems.at[a, k], device_id=to, device_id_type=MESH)

        mine = [pltpu.make_async_copy(ins[a], outs[a].at[4 * x + 2 * y + c], local_sems.at[a]) for a in range(n)]
        first = []
        for a in range(n):
            first.append(copy(a, 0, me, sibling, src=ins[a]))
            first += [copy(a, 1 + j, me, (*chip, c), src=ins[a]) for j, chip in enumerate(chips)]
        return copy, mine, first, me, sibling, chips, c

    def start(ins, outs, sems):
        _, mine, first, *_ = parts(ins, outs, sems)
        for cp in mine + first:
            cp.start()

    def finish(ins, outs, sems):
        copy, mine, first, me, sibling, chips, c = parts(ins, outs, sems)
        passed = []
        for j, chip in enumerate(chips):
            for a in range(n):
                copy(a, 1 + j, (*chip, c), me).wait_recv()
                fwd = copy(a, 4 + j, (*chip, c), sibling)
                fwd.start()
                passed.append(fwd)
        for a in range(n):
            copy(a, 0, sibling, me).wait_recv()
        for j, chip in enumerate(chips):
            for a in range(n):
                copy(a, 4 + j, (*chip, 1 - c), me).wait_recv()
        for cp in first + passed:
            cp.wait_send()
        for cp in mine:
            cp.wait()

    return _Comm(list(arrs), [jax.ShapeDtypeStruct((N_DEV,) + a.shape, a.dtype) for a in arrs],
                 [pltpu.SemaphoreType.DMA((n, 7)), pltpu.SemaphoreType.DMA((n, 7)), pltpu.SemaphoreType.DMA((n,))],
                 start, finish)


def _sibling_comm(parts):
    n = len(parts)

    def copies(ins, outs, sems):
        x, y, c = _place()
        return [pltpu.make_async_remote_copy(
            src_ref=ins[a].at[2 * q + (1 - c)], dst_ref=outs[a].at[q],
            send_sem=sems[0].at[a, q], recv_sem=sems[1].at[a, q],
            device_id=(x, y, 1 - c), device_id_type=MESH) for a in range(n) for q in range(4)]

    return _Comm(list(parts), [jax.ShapeDtypeStruct((4,) + p.shape[1:], p.dtype) for p in parts],
                 [pltpu.SemaphoreType.DMA((n, 4)), pltpu.SemaphoreType.DMA((n, 4))],
                 lambda *r: [cp.start() for cp in copies(*r)], lambda *r: [cp.wait() for cp in copies(*r)])


def _chips_comm(chip_parts):
    n = len(chip_parts)

    def copies(ins, outs, sems):
        x, y, c = _place()
        chips = [(1 - x, y), (x, 1 - y), (1 - x, 1 - y)]
        return [pltpu.make_async_remote_copy(
            src_ref=ins[a].at[2 * px + py], dst_ref=outs[a].at[k],
            send_sem=sems[0].at[a, k], recv_sem=sems[1].at[a, k],
            device_id=(px, py, c), device_id_type=MESH) for a in range(n) for k, (px, py) in enumerate(chips)]

    return _Comm(list(chip_parts), [jax.ShapeDtypeStruct((3,) + p.shape[1:], p.dtype) for p in chip_parts],
                 [pltpu.SemaphoreType.DMA((n, 3)), pltpu.SemaphoreType.DMA((n, 3))],
                 lambda *r: [cp.start() for cp in copies(*r)], lambda *r: [cp.wait() for cp in copies(*r)])


def _carry(body, *, name, grid, in_specs, out_specs, out_shape, args, comms=(), scratch_shapes=(), prefetch=()):
    n_pre, n_in, n_out, n_scr = len(prefetch), len(in_specs), len(out_specs), len(scratch_shapes)
    c_ins = [a for cm in comms for a in cm.ins]
    c_outs = [s for cm in comms for s in cm.outs]
    c_sems = [s for cm in comms for s in cm.sems]

    def wrapped(*refs):
        pre, refs = refs[:n_pre], refs[n_pre:]
        ins, refs = refs[:n_in], refs[n_in:]
        cins, refs = refs[:len(c_ins)], refs[len(c_ins):]
        outs, refs = refs[:n_out], refs[n_out:]
        couts, refs = refs[:len(c_outs)], refs[len(c_outs):]
        scr, sems = refs[:n_scr], refs[n_scr:]
        groups, i0, o0, s0 = [], 0, 0, 0
        for cm in comms:
            groups.append((cm, cins[i0:i0 + len(cm.ins)], couts[o0:o0 + len(cm.outs)], sems[s0:s0 + len(cm.sems)]))
            i0, o0, s0 = i0 + len(cm.ins), o0 + len(cm.outs), s0 + len(cm.sems)
        first = pl.program_id(0) == 0
        last = pl.program_id(0) == grid[0] - 1
        for ax in range(1, len(grid)):
            first = first & (pl.program_id(ax) == 0)
            last = last & (pl.program_id(ax) == grid[ax] - 1)
        if comms:
            @pl.when(first)
            def _():
                for cm, ci, co, cs in groups:
                    cm.start(ci, co, cs)
        body(*pre, *ins, *outs, *scr)
        if comms:
            @pl.when(last)
            def _():
                for cm, ci, co, cs in groups:
                    cm.finish(ci, co, cs)

    grid_spec = pltpu.PrefetchScalarGridSpec(
        num_scalar_prefetch=n_pre, grid=grid,
        in_specs=list(in_specs) + [ANY] * len(c_ins), out_specs=list(out_specs) + [ANY] * len(c_outs),
        scratch_shapes=list(scratch_shapes) + c_sems)
    res = pl.pallas_call(
        wrapped, name=name, grid_spec=grid_spec, out_shape=list(out_shape) + c_outs,
        compiler_params=_params(*(["arbitrary"] * len(grid))),
    )(*prefetch, *args, *c_ins)
    outs, rest, per_comm = res[:n_out], res[n_out:], []
    for cm in comms:
        per_comm.append(rest[:len(cm.outs)])
        rest = rest[len(cm.outs):]
    return outs, per_comm


def _rope_tables(pos_col, inv_freq, comms=()):
    t_tok = pos_col.shape[0]
    tm = min(512, t_tok)

    def body(pos_ref, invf_ref, cos_ref, sin_ref):
        ang = pos_ref[...].astype(F32) * invf_ref[...]
        lane = lax.broadcasted_iota(jnp.int32, ang.shape, 1)
        sign = jnp.where((lane % HEAD_DIM) < HEAD_DIM // 2, -1.0, 1.0)
        cos_ref[...] = jnp.cos(ang)
        sin_ref[...] = jnp.sin(ang) * sign

    return _carry(
        body, name="rope_tables", grid=(t_tok // tm,), comms=comms,
        in_specs=[pl.BlockSpec((tm, 1), lambda i: (i, 0)), pl.BlockSpec((1, LANES), lambda i: (0, 0))],
        out_specs=[pl.BlockSpec((tm, LANES), lambda i: (i, 0))] * 2,
        out_shape=[jax.ShapeDtypeStruct((t_tok, LANES), F32)] * 2,
        args=(pos_col, inv_freq))


def _proj_in(x2, w_in_b, comms=()):
    t_tok, d = x2.shape
    d_in = w_in_b.shape[1]
    tm = min(512, t_tok)

    def body(x_ref, w_ref, h_ref, xb_ref):
        xb = x_ref[...].astype(BF16)
        xb_ref[...] = xb
        h_ref[...] = _dot(xb, w_ref[...])

    return _carry(
        body, name="proj_in", grid=(t_tok // tm,), comms=comms,
        in_specs=[pl.BlockSpec((tm, d), lambda i: (i, 0)), pl.BlockSpec((d, d_in), lambda i: (0, 0))],
        out_specs=[pl.BlockSpec((tm, d_in), lambda i: (i, 0)), pl.BlockSpec((tm, d), lambda i: (i, 0))],
        out_shape=[jax.ShapeDtypeStruct((t_tok, d_in), F32), jax.ShapeDtypeStruct((t_tok, d), BF16)],
        args=(x2, w_in_b))


def _h_specs(nb):
    kv_col = COL_K // (2 * D_KV)
    return [
        pl.BlockSpec((BLK, D_GMLP), lambda i: (i, 0)),
        pl.BlockSpec((BLK, D_GMLP), lambda i: (i, 1)),
        pl.BlockSpec((BLK, D_ATTN), lambda i: (i, 2)),
        pl.BlockSpec((BLK, 2 * D_KV), lambda i: (i, kv_col)),
        pl.BlockSpec((BLK, 2 * D_KV), lambda i: (jnp.maximum(i - 1, 0), kv_col)),
    ]


def _table_specs():
    return [
        pl.BlockSpec((BLK, LANES), lambda i: (i, 0)),
        pl.BlockSpec((BLK, LANES), lambda i: (i, 0)),
        pl.BlockSpec((BLK, LANES), lambda i: (jnp.maximum(i - 1, 0), 0)),
        pl.BlockSpec((BLK, LANES), lambda i: (jnp.maximum(i - 1, 0), 0)),
    ]


def _band_mask(i):
    qi = lax.broadcasted_iota(jnp.int32, (BLK, 2 * BLK), 0)
    si = lax.broadcasted_iota(jnp.int32, (BLK, 2 * BLK), 1)
    dist = qi + BLK - si
    return (dist >= 0) & (dist < BLK) & ((si >= BLK) | (i > 0))


def _roped_keys(kvc, kvp, cosc, sinc, cosp, sinp):
    k = jnp.concatenate([_rope(kvp[:, :D_KV], cosp, sinp), _rope(kvc[:, :D_KV], cosc, sinc)], axis=0)
    v = jnp.concatenate([kvp[:, D_KV:], kvc[:, D_KV:]], axis=0)
    return k, v


def _head_place(hh):
    chunk, half, kv = hh // 2, hh % 2, hh // (N_HEADS // N_KV_HEADS)
    return chunk, half, kv != half


def _softmax_sink(s, sink):
    m = jnp.maximum(jnp.max(s, axis=-1, keepdims=True), sink)
    e = jnp.exp(s - m)
    es = jnp.exp(sink - m)
    denom = jnp.sum(e, axis=-1, keepdims=True) + es
    return e / denom, es / denom


def _mixer_fwd(h, cos_t, sin_t, w_spatial, bias_full, vln_g, vln_b, sinks, comms=()):
    t_tok = h.shape[0]
    nb = t_tok // BLK

    def body(sinks_ref, u_ref, vg_ref, q_ref, kvc_ref, kvp_ref, cosc_ref, sinc_ref, cosp_ref, sinp_ref,
             wsp_ref, bias_ref, g_ref, b_ref, cat_ref):
        i = pl.program_id(0)
        lo = lax.broadcasted_iota(jnp.int32, (BLK, LANES), 1) < HEAD_DIM
        row = lax.broadcasted_iota(jnp.int32, (BLK, BLK), 0)
        col = lax.broadcasted_iota(jnp.int32, (BLK, BLK), 1)
        causal = row >= col

        ua = _gelu(u_ref[...])
        vp, _, _ = _ln_fwd(_gelu(vg_ref[...]), g_ref[...], b_ref[...])
        vpb = vp.astype(BF16)
        for c in range(D_GMLP // LANES):
            sl = slice(c * LANES, (c + 1) * LANES)
            w0 = jnp.where(causal, wsp_ref[2 * c], 0.0).astype(BF16)
            w1 = jnp.where(causal, wsp_ref[2 * c + 1], 0.0).astype(BF16)
            mixed = jnp.where(lo, _dot(w0, vpb[:, sl]), _dot(w1, vpb[:, sl])) + bias_ref[:, sl]
            cat_ref[:, sl] = (ua[:, sl] * mixed).astype(BF16)

        cosc, sinc = cosc_ref[...], sinc_ref[...]
        qr = _rope(q_ref[...], _tile_lanes(cosc, D_ATTN // LANES), _tile_lanes(sinc, D_ATTN // LANES))
        k, v = _roped_keys(kvc_ref[...], kvp_ref[...], cosc, sinc, cosp_ref[...], sinp_ref[...])
        kb = (k.astype(BF16), pltpu.roll(k, HEAD_DIM, 1).astype(BF16))
        vb = (v.astype(BF16), pltpu.roll(v, HEAD_DIM, 1).astype(BF16))
        mask = _band_mask(i)
        outs = []
        for hh in range(N_HEADS):
            chunk, half, other = _head_place(hh)
            qc = qr[:, chunk * LANES:(chunk + 1) * LANES]
            qm = jnp.where(lo if half == 0 else ~lo, qc, 0.0).astype(BF16)
            s = jnp.where(mask, _dot(qm, kb[other], NT) * SCORE_SCALE, NEG_INF)
            p, _ = _softmax_sink(s, sinks_ref[hh])
            outs.append(_dot(p.astype(BF16), vb[other]))
        for c in range(D_ATTN // LANES):
            cat_ref[:, D_GMLP + c * LANES:D_GMLP + (c + 1) * LANES] = jnp.where(lo, outs[2 * c], outs[2 * c + 1]).astype(BF16)

    full = lambda shape: pl.BlockSpec(shape, lambda i: (0,) * len(shape))
    return _carry(
        body, name="mixer_fwd", grid=(nb,), comms=comms,
        in_specs=[pl.BlockSpec(memory_space=pltpu.SMEM)] + _h_specs(nb) + _table_specs() + [
            full((N_HEADS, BLK, BLK)), full((BLK, D_GMLP)), full((1, D_GMLP)), full((1, D_GMLP))],
        out_specs=[pl.BlockSpec((BLK, D_GMLP + D_ATTN), lambda i: (i, 0))],
        out_shape=[jax.ShapeDtypeStruct((t_tok, D_GMLP + D_ATTN), BF16)],
        args=(sinks, h, h, h, h, h, cos_t, sin_t, cos_t, sin_t, w_spatial, bias_full, vln_g, vln_b))


def _proj_out(cat_b, x2, w_out_b):
    t_tok, d = x2.shape
    tm = min(512, t_tok)

    def body(cat_ref, x_ref, w_ref, z_ref):
        z_ref[...] = ALPHA * x_ref[...] + _dot(cat_ref[...], w_ref[...])

    return _carry(
        body, name="proj_out", grid=(t_tok // tm,),
        in_specs=[pl.BlockSpec((tm, cat_b.shape[1]), lambda i: (i, 0)), pl.BlockSpec((tm, d), lambda i: (i, 0)),
                  pl.BlockSpec(w_out_b.shape, lambda i: (0, 0))],
        out_specs=[pl.BlockSpec((tm, d), lambda i: (i, 0))],
        out_shape=[jax.ShapeDtypeStruct((t_tok, d), F32)],
        args=(cat_b, x2, w_out_b))[0][0]


def _ffn_fwd_bwd(z1, target, w1_b, w2_b, ln1_g, ln1_b, ln2_g, ln2_b):
    t_tok, d = z1.shape
    n_chunk, _, fc = w1_b.shape
    f = n_chunk * fc
    tm = min(256, t_tok)

    def body(z1_ref, tgt_ref, w1_ref, w2_ref, g1_ref, b1_ref, g2_ref, b2_ref,
             act_ref, dpre_ref, x1b_ref, dz2b_ref, dz1_ref, stats_ref, r_scr):
        @pl.when(pl.program_id(0) == 0)
        def _():
            stats_ref[...] = jnp.zeros_like(stats_ref)

        g1, g2 = g1_ref[...], g2_ref[...]
        x1, xhat1, rstd1 = _ln_fwd(z1_ref[...], g1, b1_ref[...])
        x1b = x1.astype(BF16)
        x1b_ref[...] = x1b
        ff = jnp.zeros((tm, d), F32)
        for j in range(n_chunk):
            r = jnp.maximum(_dot(x1b, w1_ref[j]), 0.0)
            r_scr[:, j * fc:(j + 1) * fc] = r
            act = (r * r).astype(BF16)
            act_ref[:, j * fc:(j + 1) * fc] = act
            ff = ff + _dot(act, w2_ref[j])
        y, xhat2, rstd2 = _ln_fwd(ALPHA * x1 + ff, g2, b2_ref[...])
        diff = y - tgt_ref[...]
        loss = 0.5 * jnp.sum(jnp.sum(diff * diff, axis=-1, keepdims=True) / d, axis=0, keepdims=True)
        dy = diff / d
        dz2 = _ln_bwd(dy, xhat2, rstd2, g2)
        dz2b = dz2.astype(BF16)
        dz2b_ref[...] = dz2b
        dx1 = ALPHA * dz2
        for j in range(n_chunk):
            dpre = (_dot(dz2b, w2_ref[j], NT) * (2.0 * r_scr[:, j * fc:(j + 1) * fc])).astype(BF16)
            dpre_ref[:, j * fc:(j + 1) * fc] = dpre
            dx1 = dx1 + _dot(dpre, w1_ref[j], NT)
        dz1_ref[...] = _ln_bwd(dx1, xhat1, rstd1, g1)
        stats_ref[0:1, :] += jnp.sum(dx1 * xhat1, axis=0, keepdims=True)
        stats_ref[1:2, :] += jnp.sum(dx1, axis=0, keepdims=True)
        stats_ref[2:3, :] += jnp.sum(dy * xhat2, axis=0, keepdims=True)
        stats_ref[3:4, :] += jnp.sum(dy, axis=0, keepdims=True)
        stats_ref[4:5, :] += jnp.broadcast_to(loss, (1, d))

    tok = lambda w: pl.BlockSpec((tm, w), lambda i: (i, 0))
    vec = pl.BlockSpec((1, d), lambda i: (0, 0))
    return _carry(
        body, name="ffn_fwd_bwd", grid=(t_tok // tm,),
        in_specs=[tok(d), tok(d),
                  pl.BlockSpec(w1_b.shape, lambda i: (0, 0, 0), pipeline_mode=pl.Buffered(1)),
                  pl.BlockSpec(w2_b.shape, lambda i: (0, 0, 0), pipeline_mode=pl.Buffered(1)),
                  vec, vec, vec, vec],
        out_specs=[tok(f), tok(f), tok(d), tok(d), tok(d), pl.BlockSpec((8, d), lambda i: (0, 0))],
        out_shape=[jax.ShapeDtypeStruct((t_tok, f), BF16), jax.ShapeDtypeStruct((t_tok, f), BF16),
                   jax.ShapeDtypeStruct((t_tok, d), BF16), jax.ShapeDtypeStruct((t_tok, d), BF16),
                   jax.ShapeDtypeStruct((t_tok, d), F32), jax.ShapeDtypeStruct((8, d), F32)],
        scratch_shapes=[pltpu.VMEM((tm, f), F32)],
        args=(z1, target, w1_b, w2_b, ln1_g, ln1_b, ln2_g, ln2_b))[0]


def _ffn_wgrad1(x1b, dpre_b, n_chunk, comms=()):
    t_tok, d = x1b.shape
    fc = dpre_b.shape[1] // n_chunk
    tk = min(2048, t_tok)

    def body(x1_ref, dpre_ref, g_ref):
        @pl.when(pl.program_id(1) == 0)
        def _():
            g_ref[...] = jnp.zeros_like(g_ref)

        g_ref[...] += _dot(x1_ref[...], dpre_ref[...], TN)

    return _carry(
        body, name="ffn_wgrad1", grid=(n_chunk, t_tok // tk), comms=comms,
        in_specs=[pl.BlockSpec((tk, d), lambda j, t: (t, 0)), pl.BlockSpec((tk, fc), lambda j, t: (t, j))],
        out_specs=[pl.BlockSpec((None, d, fc), lambda j, t: (j, 0, 0))],
        out_shape=[jax.ShapeDtypeStruct((n_chunk, d, fc), F32)],
        args=(x1b, dpre_b))


def _ffn_wgrad2(act_b, dz2b, n_chunk, comms=()):
    t_tok, d = dz2b.shape
    fc = act_b.shape[1] // n_chunk
    tk = min(2048, t_tok)

    def body(act_ref, dz2_ref, g_ref):
        @pl.when(pl.program_id(1) == 0)
        def _():
            g_ref[...] = jnp.zeros_like(g_ref)

        g_ref[...] += _dot(act_ref[...], dz2_ref[...], TN)

    return _carry(
        body, name="ffn_wgrad2", grid=(n_chunk, t_tok // tk), comms=comms,
        in_specs=[pl.BlockSpec((tk, fc), lambda j, t: (t, j)), pl.BlockSpec((tk, d), lambda j, t: (t, 0))],
        out_specs=[pl.BlockSpec((None, fc, d), lambda j, t: (j, 0, 0))],
        out_shape=[jax.ShapeDtypeStruct((n_chunk, fc, d), F32)],
        args=(act_b, dz2b))


def _proj_out_bwd(dz1, cat_b, w_out_b, comms=()):
    t_tok, d = dz1.shape
    d_mix = cat_b.shape[1]
    tm = min(512, t_tok)

    def body(dz1_ref, cat_ref, w_ref, dcat_ref, gw_ref):
        @pl.when(pl.program_id(0) == 0)
        def _():
            gw_ref[...] = jnp.zeros_like(gw_ref)

        dzb = dz1_ref[...].astype(BF16)
        dcat_ref[...] = _dot(dzb, w_ref[...], NT)
        gw_ref[...] += _dot(cat_ref[...], dzb, TN)

    return _carry(
        body, name="proj_out_bwd", grid=(t_tok // tm,), comms=comms,
        in_specs=[pl.BlockSpec((tm, d), lambda i: (i, 0)), pl.BlockSpec((tm, d_mix), lambda i: (i, 0)),
                  pl.BlockSpec((d_mix, d), lambda i: (0, 0))],
        out_specs=[pl.BlockSpec((tm, d_mix), lambda i: (i, 0)), pl.BlockSpec((d_mix, d), lambda i: (0, 0))],
        out_shape=[jax.ShapeDtypeStruct((t_tok, d_mix), F32), jax.ShapeDtypeStruct((d_mix, d), F32)],
        args=(dz1, cat_b, w_out_b))


def _mixer_bwd(dcat, h, cos_t, sin_t, w_spatial, bias_full, vln_g, vln_b, sinks, comms=()):
    t_tok = h.shape[0]
    nb = t_tok // BLK

    def body(sinks_ref, dcat_ref, u_ref, vg_ref, q_ref, kvc_ref, kvp_ref, cosc_ref, sinc_ref, cosp_ref, sinp_ref,
             wsp_ref, bias_ref, g_ref, b_ref, dh_ref, dkvp_ref, gws_ref, gbias_ref, gvln_ref, gsink_ref):
        i = pl.program_id(0)

        @pl.when(i == 0)
        def _():
            gws_ref[...] = jnp.zeros_like(gws_ref)
            gbias_ref[...] = jnp.zeros_like(gbias_ref)
            gvln_ref[...] = jnp.zeros_like(gvln_ref)
            gsink_ref[...] = jnp.zeros_like(gsink_ref)

        lo = lax.broadcasted_iota(jnp.int32, (BLK, LANES), 1) < HEAD_DIM
        row = lax.broadcasted_iota(jnp.int32, (BLK, BLK), 0)
        col = lax.broadcasted_iota(jnp.int32, (BLK, BLK), 1)
        causal = row >= col

        u, vg = u_ref[...], vg_ref[...]
        ua = _gelu(u)
        g = g_ref[...]
        vp, vhat, rstd = _ln_fwd(_gelu(vg), g, b_ref[...])
        vpb = vp.astype(BF16)
        da = dcat_ref[:, :D_GMLP]
        dmixed = da * ua
        gbias_ref[...] += dmixed
        dvp_parts = []
        for c in range(D_GMLP // LANES):
            sl = slice(c * LANES, (c + 1) * LANES)
            w0 = jnp.where(causal, wsp_ref[2 * c], 0.0).astype(BF16)
            w1 = jnp.where(causal, wsp_ref[2 * c + 1], 0.0).astype(BF16)
            mixed = jnp.where(lo, _dot(w0, vpb[:, sl]), _dot(w1, vpb[:, sl])) + bias_ref[:, sl]
            dh_ref[:, COL_U + c * LANES:COL_U + (c + 1) * LANES] = da[:, sl] * mixed * _gelu_grad(u[:, sl])
            dm = dmixed[:, sl]
            dm0 = jnp.where(lo, dm, 0.0).astype(BF16)
            dm1 = jnp.where(lo, 0.0, dm).astype(BF16)
            gws_ref[2 * c] += jnp.where(causal, _dot(dm0, vpb[:, sl], NT), 0.0)
            gws_ref[2 * c + 1] += jnp.where(causal, _dot(dm1, vpb[:, sl], NT), 0.0)
            dvp_parts.append(_dot(w0, dm0, TN) + _dot(w1, dm1, TN))
        dvp = jnp.concatenate(dvp_parts, axis=1)
        gvln_ref[0:1, :] += jnp.sum(dvp * vhat, axis=0, keepdims=True)
        gvln_ref[1:2, :] += jnp.sum(dvp, axis=0, keepdims=True)
        dh_ref[:, COL_V:COL_V + D_GMLP] = _ln_bwd(dvp, vhat, rstd, g) * _gelu_grad(vg)

        cosc, sinc, cosp, sinp = cosc_ref[...], sinc_ref[...], cosp_ref[...], sinp_ref[...]
        cos4, sin4 = _tile_lanes(cosc, D_ATTN // LANES), _tile_lanes(sinc, D_ATTN // LANES)
        qr = _rope(q_ref[...], cos4, sin4)
        k, v = _roped_keys(kvc_ref[...], kvp_ref[...], cosc, sinc, cosp, sinp)
        kb = (k.astype(BF16), pltpu.roll(k, HEAD_DIM, 1).astype(BF16))
        vb = (v.astype(BF16), pltpu.roll(v, HEAD_DIM, 1).astype(BF16))
        mask = _band_mask(i)
        dk = jnp.zeros((2 * BLK, D_KV), F32)
        dv = jnp.zeros((2 * BLK, D_KV), F32)
        dq_parts = []
        for hh in range(N_HEADS):
            chunk, half, other = _head_place(hh)
            mine = lo if half == 0 else ~lo
            qm = jnp.where(mine, qr[:, chunk * LANES:(chunk + 1) * LANES], 0.0).astype(BF16)
            dom = jnp.where(mine, dcat_ref[:, D_GMLP + chunk * LANES:D_GMLP + (chunk + 1) * LANES], 0.0).astype(BF16)
            s = jnp.where(mask, _dot(qm, kb[other], NT) * SCORE_SCALE, NEG_INF)
            p, p_sink = _softmax_sink(s, sinks_ref[hh])
            dp = _dot(dom, vb[other], NT)
            delta = jnp.sum(p * dp, axis=-1, keepdims=True)
            ds = (p * (dp - delta) * SCORE_SCALE).astype(BF16)
            gsink_ref[hh:hh + 1, :] += jnp.broadcast_to(-jnp.sum(p_sink * delta, axis=0, keepdims=True), (1, LANES))
            dq_parts.append(_dot(ds, kb[other]))
            dk_h = _dot(ds, qm, TN)
            dv_h = _dot(p.astype(BF16), dom, TN)
            if other:
                dk_h, dv_h = pltpu.roll(dk_h, HEAD_DIM, 1), pltpu.roll(dv_h, HEAD_DIM, 1)
            dk, dv = dk + dk_h, dv + dv_h
        dq = jnp.concatenate([jnp.where(lo, dq_parts[2 * c], dq_parts[2 * c + 1]) for c in range(D_ATTN // LANES)], axis=1)
        dh_ref[:, COL_Q:COL_Q + D_ATTN] = _rope_bwd(dq, cos4, sin4)
        dh_ref[:, COL_K:COL_K + D_KV] = _rope_bwd(dk[BLK:], cosc, sinc)
        dh_ref[:, COL_K + D_KV:COL_K + 2 * D_KV] = dv[BLK:]
        dkvp_ref[:, :D_KV] = _rope_bwd(dk[:BLK], cosp, sinp)
        dkvp_ref[:, D_KV:] = dv[:BLK]

    full = lambda shape: pl.BlockSpec(shape, lambda i: (0,) * len(shape))
    return _carry(
        body, name="mixer_bwd", grid=(nb,), comms=comms,
        in_specs=[pl.BlockSpec(memory_space=pltpu.SMEM), pl.BlockSpec((BLK, D_GMLP + D_ATTN), lambda i: (i, 0))]
        + _h_specs(nb) + _table_specs()
        + [full((N_HEADS, BLK, BLK)), full((BLK, D_GMLP)), full((1, D_GMLP)), full((1, D_GMLP))],
        out_specs=[pl.BlockSpec((BLK, D_IN), lambda i: (i, 0)),
                   pl.BlockSpec((BLK, 2 * D_KV), lambda i: ((i + nb - 1) % nb, 0)),
                   full((N_HEADS, BLK, BLK)), full((BLK, D_GMLP)), full((8, D_GMLP)), full((8, LANES))],
        out_shape=[jax.ShapeDtypeStruct((t_tok, D_IN), F32), jax.ShapeDtypeStruct((t_tok, 2 * D_KV), F32),
                   jax.ShapeDtypeStruct((N_HEADS, BLK, BLK), F32), jax.ShapeDtypeStruct((BLK, D_GMLP), F32),
                   jax.ShapeDtypeStruct((8, D_GMLP), F32), jax.ShapeDtypeStruct((8, LANES), F32)],
        args=(sinks, dcat, h, h, h, h, h, cos_t, sin_t, cos_t, sin_t, w_spatial, bias_full, vln_g, vln_b))


def _proj_in_wgrad(dh, dkvp, xb, comms=()):
    t_tok, d = xb.shape
    d_in = dh.shape[1]
    tm = min(512, t_tok)

    def body(dh_ref, dkvp_ref, xb_ref, dhb_ref, gw_ref):
        @pl.when(pl.program_id(0) == 0)
        def _():
            gw_ref[...] = jnp.zeros_like(gw_ref)

        dhb = jnp.concatenate([dh_ref[:, :COL_K], dh_ref[:, COL_K:] + dkvp_ref[...]], axis=1).astype(BF16)
        dhb_ref[...] = dhb
        gw_ref[...] += _dot(xb_ref[...], dhb, TN)

    return _carry(
        body, name="proj_in_wgrad", grid=(t_tok // tm,), comms=comms,
        in_specs=[pl.BlockSpec((tm, d_in), lambda i: (i, 0)), pl.BlockSpec((tm, 2 * D_KV), lambda i: (i, 0)),
                  pl.BlockSpec((tm, d), lambda i: (i, 0))],
        out_specs=[pl.BlockSpec((tm, d_in), lambda i: (i, 0)), pl.BlockSpec((d, d_in), lambda i: (0, 0))],
        out_shape=[jax.ShapeDtypeStruct((t_tok, d_in), BF16), jax.ShapeDtypeStruct((d, d_in), F32)],
        args=(dh, dkvp, xb))


def _proj_in_dgrad(dhb, dz1, w_in_b, comms=()):
    t_tok, d = dz1.shape
    d_in = dhb.shape[1]
    tm = min(512, t_tok)

    def body(dhb_ref, dz1_ref, w_ref, dx_ref):
        dx_ref[...] = ALPHA * dz1_ref[...] + _dot(dhb_ref[...], w_ref[...], NT)

    return _carry(
        body, name="proj_in_dgrad", grid=(t_tok // tm,), comms=comms,
        in_specs=[pl.BlockSpec((tm, d_in), lambda i: (i, 0)), pl.BlockSpec((tm, d), lambda i: (i, 0)),
                  pl.BlockSpec((d, d_in), lambda i: (0, 0))],
        out_specs=[pl.BlockSpec((tm, d), lambda i: (i, 0))],
        out_shape=[jax.ShapeDtypeStruct((t_tok, d), F32)],
        args=(dhb, dz1, w_in_b))


def _adamw(w, g, m, v):
    m = ADAM_B1 * m + (1.0 - ADAM_B1) * g
    v = ADAM_B2 * v + (1.0 - ADAM_B2) * (g * g)
    m_hat = m / (1.0 - ADAM_B1 ** ADAM_STEP)
    v_hat = v / (1.0 - ADAM_B2 ** ADAM_STEP)
    delta = -ADAM_LR * (m_hat / (jnp.sqrt(v_hat) + ADAM_EPS) + ADAM_WD * w)
    return delta, m, v


def _adamw_shard(name, own, recv3, w, m, v, comms=()):
    r, c = w.shape
    tr = r if r <= 512 else 512

    def body(own_ref, recv_ref, w_ref, m_ref, v_ref, g_out, d_out, m_out, v_out):
        g = ((own_ref[...] + recv_ref[0].astype(F32)) + recv_ref[1].astype(F32)) + recv_ref[2].astype(F32)
        delta, m_new, v_new = _adamw(w_ref[...], g, m_ref[...], v_ref[...])
        g_out[...] = g
        d_out[...] = delta
        m_out[...] = m_new
        v_out[...] = v_new

    blk = pl.BlockSpec((tr, c), lambda i: (i, 0))
    return _carry(
        body, name=name, grid=(r // tr,), comms=comms,
        in_specs=[blk, pl.BlockSpec((3, tr, c), lambda i: (0, i, 0)), blk, blk, blk],
        out_specs=[blk] * 4, out_shape=[jax.ShapeDtypeStruct((r, c), F32)] * 4,
        args=(own, recv3, w, m, v))


def _adamw_small(parts, w, m, v):
    n, r, c = parts.shape

    def body(p_ref, w_ref, m_ref, v_ref, g_out, d_out, m_out, v_out):
        g = p_ref[0]
        for k in range(1, n):
            g = g + p_ref[k]
        delta, m_new, v_new = _adamw(w_ref[...], g, m_ref[...], v_ref[...])
        g_out[...] = g
        d_out[...] = delta
        m_out[...] = m_new
        v_out[...] = v_new

    full = pl.BlockSpec((r, c), lambda i: (0, 0))
    return _carry(
        body, name="adamw_small", grid=(1,),
        in_specs=[pl.BlockSpec((n, r, c), lambda i: (0, 0, 0)), full, full, full],
        out_specs=[full] * 4, out_shape=[jax.ShapeDtypeStruct((r, c), F32)] * 4,
        args=(parts, w, m, v))[0]


def _pair_sum(name, parts, recv, core_chip):
    _, r, c = parts.shape
    tr = r if r <= 512 else 512

    def body(cc_ref, a_ref, b_ref, wire_ref, own_ref):
        s = a_ref[...] + b_ref[...]
        wire_ref[...] = s.astype(BF16)

        @pl.when(pl.program_id(1) == cc_ref[1])
        def _():
            own_ref[...] = s

    return _carry(
        body, name=name, grid=(r // tr, 4), prefetch=(core_chip,),
        in_specs=[pl.BlockSpec((None, tr, c), lambda i, q, cc: (2 * q + cc[0], i, 0)),
                  pl.BlockSpec((None, tr, c), lambda i, q, cc: (q, i, 0))],
        out_specs=[pl.BlockSpec((None, tr, c), lambda i, q, cc: (q, i, 0)), pl.BlockSpec((tr, c), lambda i, q, cc: (i, 0))],
        out_shape=[jax.ShapeDtypeStruct((4, r, c), BF16), jax.ShapeDtypeStruct((r, c), F32)],
        args=(parts, recv))[0]


def _pack_small(v_ln_g, v_ln_b, w_spatial, b_spatial, sinks, ln1_g, ln1_b, ln2_g, ln2_b):
    d = ln1_g.shape[-1]
    rows = [w_spatial.reshape(-1, d),
            jnp.concatenate([v_ln_g.reshape(1, -1), v_ln_b.reshape(1, -1)], axis=1),
            b_spatial.reshape(1, -1),
            ln1_g.reshape(1, d), ln1_b.reshape(1, d), ln2_g.reshape(1, d), ln2_b.reshape(1, d),
            jnp.pad(sinks.reshape(1, -1), ((0, 1), (0, d - N_HEADS)))]
    return jnp.concatenate(rows, axis=0)


def _unpack_small(p):
    n_ws = N_HEADS * BLK * BLK // p.shape[1]
    w_spatial = p[:n_ws].reshape(1, N_HEADS, BLK, BLK)
    v_ln_g, v_ln_b = p[n_ws:n_ws + 1, :D_GMLP], p[n_ws:n_ws + 1, D_GMLP:2 * D_GMLP]
    b_spatial = p[n_ws + 1, :N_HEADS * BLK].reshape(1, N_HEADS, BLK)
    ln1_g, ln1_b, ln2_g, ln2_b = (p[n_ws + 2 + k:n_ws + 3 + k] for k in range(4))
    sinks = p[n_ws + 6:n_ws + 7, :N_HEADS]
    return [v_ln_g, v_ln_b, w_spatial, b_spatial, sinks, None, ln1_g, ln1_b, None, None, ln2_g, ln2_b]


def kernel(x, positions, w_in, v_ln_g, v_ln_b, w_spatial, b_spatial, sinks, w_out, ln1_g, ln1_b, w_ff1, w_ff2, ln2_g, ln2_b, loss_target, m_w_in, m_v_ln_g, m_v_ln_b, m_w_spatial, m_b_spatial, m_sinks, m_w_out, m_ln1_g, m_ln1_b, m_w_ff1, m_w_ff2, m_ln2_g, m_ln2_b, v_w_in, v_v_ln_g, v_v_ln_b, v_w_spatial, v_b_spatial, v_sinks, v_w_out, v_ln1_g, v_ln1_b, v_w_ff1, v_w_ff2, v_ln2_g, v_ln2_b):
    _, t_tok, d = x.shape
    xi, yi, ci = _place()
    core_chip = jnp.stack([ci, 2 * xi + yi]).astype(jnp.int32)
    x2 = x.reshape(t_tok, d)
    target = loss_target.reshape(t_tok, d)
    inv_freq = ROPE_THETA ** (-jnp.arange(0, HEAD_DIM, 2, dtype=F32) / HEAD_DIM)
    bias_full = jnp.repeat(b_spatial[0].T, HEAD_DIM, axis=1)
    wsp, sink_vec = w_spatial[0], sinks[0]
    big = {"in": w_in[0], "out": w_out[0], "ff1": w_ff1[0], "ff2": w_ff2[0]}
    big_b = {k: w.astype(BF16) for k, w in big.items()}

    (cos_t, sin_t), ((g_in,),) = _rope_tables(
        positions.reshape(t_tok, 1), jnp.tile(inv_freq, LANES // (HEAD_DIM // 2)).reshape(1, LANES),
        comms=[_gather_comm([big_b["in"]])])
    w_in_b = g_in.transpose(1, 0, 2).reshape(d, D_IN)
    (h, xb), ((g_out, w1_b),) = _proj_in(x2, w_in_b, comms=[_gather_comm([big_b["out"], big_b["ff1"]])])
    w_out_b = g_out.reshape(-1, d)
    (cat_b,), ((w2_b,),) = _mixer_fwd(h, cos_t, sin_t, wsp, bias_full, v_ln_g, v_ln_b, sink_vec,
                                      comms=[_gather_comm([big_b["ff2"]])])
    z1 = _proj_out(cat_b, x2, w_out_b)
    act_b, dpre_b, x1b, dz2b, dz1, stats = _ffn_fwd_bwd(z1, target, w1_b, w2_b, ln1_g, ln1_b, ln2_g, ln2_b)
    loss = lax.psum(stats[4, 0], ("x", "y", "c"))

    (p_ff1,), _ = _ffn_wgrad1(x1b, dpre_b, N_DEV)
    (p_ff2,), ((s_ff1,),) = _ffn_wgrad2(act_b, dz2b, N_DEV, comms=[_sibling_comm([p_ff1])])
    wire_ff1, own_ff1 = _pair_sum("pair_sum_ff1", p_ff1, s_ff1, core_chip)
    (dcat, gw_out), ((s_ff2,), (r_ff1,)) = _proj_out_bwd(
        dz1, cat_b, w_out_b, comms=[_sibling_comm([p_ff2]), _chips_comm([wire_ff1])])
    wire_ff2, own_ff2 = _pair_sum("pair_sum_ff2", p_ff2, s_ff2, core_chip)
    p_out = gw_out.reshape(N_DEV, -1, d)
    (dh, dkvp, g_wsp, g_bias, g_vln, g_sink), ((r_ff2,), (s_out,)) = _mixer_bwd(
        dcat, h, cos_t, sin_t, wsp, bias_full, v_ln_g, v_ln_b, sink_vec,
        comms=[_chips_comm([wire_ff2]), _sibling_comm([p_out])])
    wire_out, own_out = _pair_sum("pair_sum_out", p_out, s_out, core_chip)
    g_b_spatial = g_bias.reshape(BLK, N_HEADS, HEAD_DIM).sum(axis=-1).T
    small_g = _pack_small(g_vln[0], g_vln[1], g_wsp, g_b_spatial, g_sink[:, 0], stats[0], stats[1], stats[2], stats[3])
    (dhb, gw_in), ((small_parts,), (r_out,)) = _proj_in_wgrad(
        dh, dkvp, xb, comms=[_gather_comm([small_g]), _chips_comm([wire_out])])
    p_in = gw_in.reshape(d, N_DEV, -1).transpose(1, 0, 2)
    (grad_x,), ((s_in,),) = _proj_in_dgrad(dhb, dz1, w_in_b, comms=[_sibling_comm([p_in])])
    wire_in, own_in = _pair_sum("pair_sum_in", p_in, s_in, core_chip)

    ff1_out, ((r_in,),) = _adamw_shard("adamw_ff1", own_ff1, r_ff1, big["ff1"], m_w_ff1[0], v_w_ff1[0], comms=[_chips_comm([wire_in])])
    ff2_out, _ = _adamw_shard("adamw_ff2", own_ff2, r_ff2, big["ff2"], m_w_ff2[0], v_w_ff2[0])
    out_out, _ = _adamw_shard("adamw_out", own_out, r_out, big["out"], m_w_out[0], v_w_out[0])
    in_out, _ = _adamw_shard("adamw_in", own_in, r_in, big["in"], m_w_in[0], v_w_in[0])
    small_w = _pack_small(v_ln_g, v_ln_b, w_spatial, b_spatial, sinks, ln1_g, ln1_b, ln2_g, ln2_b)
    small_m = _pack_small(m_v_ln_g, m_v_ln_b, m_w_spatial, m_b_spatial, m_sinks, m_ln1_g, m_ln1_b, m_ln2_g, m_ln2_b)
    small_v = _pack_small(v_v_ln_g, v_v_ln_b, v_w_spatial, v_b_spatial, v_sinks, v_ln1_g, v_ln1_b, v_ln2_g, v_ln2_b)
    small_out = [_unpack_small(p) for p in _adamw_small(small_parts, small_w, small_m, small_v)]

    big_out = {0: in_out, 6: out_out, 9: ff1_out, 10: ff2_out}
    outs = [loss, grad_x.reshape(x.shape)]
    for kind in range(4):
        for wi in range(13):
            outs.append(big_out[wi][kind][None] if wi in big_out else small_out[kind][wi - 1])
    return tuple(outs)
```

```python
import functools
import math

import jax
import jax.numpy as jnp
from jax import lax
from jax.experimental import pallas as pl
from jax.experimental.pallas import tpu as pltpu

F32 = jnp.float32
BF16 = jnp.bfloat16
MESH = pl.DeviceIdType.MESH

HEAD_DIM = 64
N_HEADS = 8
N_KV_HEADS = 2
BLK = 128
D_GMLP = N_HEADS * HEAD_DIM
D_ATTN = N_HEADS * HEAD_DIM
D_KV = N_KV_HEADS * HEAD_DIM
D_IN = 2 * D_GMLP + D_ATTN + 2 * D_KV
COL_U, COL_V, COL_Q, COL_K = 0, D_GMLP, 2 * D_GMLP, 2 * D_GMLP + D_ATTN
ROPE_THETA = 10000.0
LN_EPS = 1e-5
ALPHA = 2.0 ** 0.25
NEG_INF = -1e30
SCORE_SCALE = 1.0 / math.sqrt(HEAD_DIM)
ADAM_LR, ADAM_B1, ADAM_B2, ADAM_EPS, ADAM_WD, ADAM_STEP = 0.001, 0.9, 0.999, 1e-08, 0.01, 10
N_DEV = 8
LANES = 128
VMEM_LIMIT = 56 * 1024 * 1024

NT = (((1,), (1,)), ((), ()))
TN = (((0,), (0,)), ((), ()))


def _params(*sem):
    return pltpu.CompilerParams(dimension_semantics=sem, vmem_limit_bytes=VMEM_LIMIT)


def _dot(a, b, dims=None):
    if dims is None:
        return jnp.dot(a, b, preferred_element_type=F32)
    return lax.dot_general(a, b, dims, preferred_element_type=F32)


def _mean(a):
    return jnp.mean(a, axis=-1, keepdims=True)


def _ln_fwd(z, g, b):
    zc = z - _mean(z)
    rstd = lax.rsqrt(_mean(zc * zc) + LN_EPS)
    xhat = zc * rstd
    return xhat * g + b, xhat, rstd


def _ln_bwd(dy, xhat, rstd, g):
    dxhat = dy * g
    return rstd * (dxhat - _mean(dxhat) - xhat * _mean(dxhat * xhat))


_GELU_C = math.sqrt(2.0 / math.pi)


def _gelu(x):
    t = jnp.tanh(_GELU_C * (x + 0.044715 * (x * x * x)))
    return 0.5 * x * (1.0 + t)


def _gelu_and_grad(x):
    x2 = x * x
    t = jnp.tanh(_GELU_C * (x + 0.044715 * (x2 * x)))
    hx, ht = 0.5 * x, 0.5 * (1.0 + t)
    return x * ht, ht + hx * (1.0 - t * t) * (_GELU_C * (1.0 + 3.0 * 0.044715 * x2))


def _mean0(a):
    return jnp.mean(a, axis=0, keepdims=True)


def _ln_fwd_t(z, g, b):
    zc = z - _mean0(z)
    rstd = lax.rsqrt(_mean0(zc * zc) + LN_EPS)
    xhat = zc * rstd
    return xhat * g + b, xhat, rstd


def _ln_bwd_t(dy, xhat, rstd, g):
    dxhat = dy * g
    return rstd * (dxhat - _mean0(dxhat) - xhat * _mean0(dxhat * xhat))


def _rope_t(t, cos, sin_signed, bwd=False):
    half = HEAD_DIM // 2
    outs = []
    for r in range(0, t.shape[0], HEAD_DIM):
        th = t[r:r + HEAD_DIM]
        sw = jnp.concatenate([th[half:], th[:half]], axis=0) * sin_signed
        outs.append(th * cos - sw if bwd else th * cos + sw)
    return jnp.concatenate(outs, axis=0)


ANY = pl.BlockSpec(memory_space=pl.ANY)


def _place():
    return lax.axis_index("x"), lax.axis_index("y"), lax.axis_index("c")


class _Comm:
    def __init__(self, ins, outs, sems, start, finish):
        self.ins, self.outs, self.sems, self.start, self.finish = ins, outs, sems, start, finish


def _gather_comm(arrs):
    n = len(arrs)

    def parts(ins, outs, sems):
        send_sems, recv_sems, local_sems = sems
        x, y, c = _place()
        me, sibling = (x, y, c), (x, y, 1 - c)
        chips = [(1 - x, y), (x, 1 - y), (1 - x, 1 - y)]

        def copy(a, k, block, to, src=None):
            px, py, pc = block
            dst = outs[a].at[4 * px + 2 * py + pc]
            return pltpu.make_async_remote_copy(
                src_ref=dst if src is None else src, dst_ref=dst,
                send_sem=send_sems.at[a, k], recv_sem=recv_sems.at[a, k], device_id=to, device_id_type=MESH)

        mine = [pltpu.make_async_copy(ins[a], outs[a].at[4 * x + 2 * y + c], local_sems.at[a]) for a in range(n)]
        first = []
        for a in range(n):
            first.append(copy(a, 0, me, sibling, src=ins[a]))
            first += [copy(a, 1 + j, me, (*chip, c), src=ins[a]) for j, chip in enumerate(chips)]
        return copy, mine, first, me, sibling, chips, c

    def start(ins, outs, sems):
        _, mine, first, *_ = parts(ins, outs, sems)
        for cp in mine + first:
            cp.start()

    def finish(ins, outs, sems):
        copy, mine, first, me, sibling, chips, c = parts(ins, outs, sems)
        passed = []
        for j, chip in enumerate(chips):
            for a in range(n):
                copy(a, 1 + j, (*chip, c), me).wait_recv()
                fwd = copy(a, 4 + j, (*chip, c), sibling)
                fwd.start()
                passed.append(fwd)
        for a in range(n):
            copy(a, 0, sibling, me).wait_recv()
        for j, chip in enumerate(chips):
            for a in range(n):
                copy(a, 4 + j, (*chip, 1 - c), me).wait_recv()
        for cp in first + passed:
            cp.wait_send()
        for cp in mine:
            cp.wait()

    return _Comm(list(arrs), [jax.ShapeDtypeStruct((N_DEV,) + a.shape, a.dtype) for a in arrs],
                 [pltpu.SemaphoreType.DMA((n, 7)), pltpu.SemaphoreType.DMA((n, 7)), pltpu.SemaphoreType.DMA((n,))],
                 start, finish)


def _sibling_comm(parts):
    n = len(parts)

    def copies(ins, outs, sems):
        x, y, c = _place()
        return [pltpu.make_async_remote_copy(
            src_ref=ins[a].at[2 * q + (1 - c)], dst_ref=outs[a].at[q],
            send_sem=sems[0].at[a, q], recv_sem=sems[1].at[a, q],
            device_id=(x, y, 1 - c), device_id_type=MESH) for a in range(n) for q in range(4)]

    return _Comm(list(parts), [jax.ShapeDtypeStruct((4,) + p.shape[1:], p.dtype) for p in parts],
                 [pltpu.SemaphoreType.DMA((n, 4)), pltpu.SemaphoreType.DMA((n, 4))],
                 lambda *r: [cp.start() for cp in copies(*r)], lambda *r: [cp.wait() for cp in copies(*r)])


def _chips_comm(chip_parts):
    n = len(chip_parts)

    def copies(ins, outs, sems):
        x, y, c = _place()
        chips = [(1 - x, y), (x, 1 - y), (1 - x, 1 - y)]
        return [pltpu.make_async_remote_copy(
            src_ref=ins[a].at[2 * px + py], dst_ref=outs[a].at[k],
            send_sem=sems[0].at[a, k], recv_sem=sems[1].at[a, k],
            device_id=(px, py, c), device_id_type=MESH) for a in range(n) for k, (px, py) in enumerate(chips)]

    return _Comm(list(chip_parts), [jax.ShapeDtypeStruct((3,) + p.shape[1:], p.dtype) for p in chip_parts],
                 [pltpu.SemaphoreType.DMA((n, 3)), pltpu.SemaphoreType.DMA((n, 3))],
                 lambda *r: [cp.start() for cp in copies(*r)], lambda *r: [cp.wait() for cp in copies(*r)])


def _carry(body, *, name, grid, in_specs, out_specs, out_shape, args, comms=(), scratch_shapes=(), prefetch=()):
    n_pre, n_in, n_out, n_scr = len(prefetch), len(in_specs), len(out_specs), len(scratch_shapes)
    c_ins = [a for cm in comms for a in cm.ins]
    c_outs = [s for cm in comms for s in cm.outs]
    c_sems = [s for cm in comms for s in cm.sems]

    def wrapped(*refs):
        pre, refs = refs[:n_pre], refs[n_pre:]
        ins, refs = refs[:n_in], refs[n_in:]
        cins, refs = refs[:len(c_ins)], refs[len(c_ins):]
        outs, refs = refs[:n_out], refs[n_out:]
        couts, refs = refs[:len(c_outs)], refs[len(c_outs):]
        scr, sems = refs[:n_scr], refs[n_scr:]
        groups, i0, o0, s0 = [], 0, 0, 0
        for cm in comms:
            groups.append((cm, cins[i0:i0 + len(cm.ins)], couts[o0:o0 + len(cm.outs)], sems[s0:s0 + len(cm.sems)]))
            i0, o0, s0 = i0 + len(cm.ins), o0 + len(cm.outs), s0 + len(cm.sems)
        first = pl.program_id(0) == 0
        last = pl.program_id(0) == grid[0] - 1
        for ax in range(1, len(grid)):
            first = first & (pl.program_id(ax) == 0)
            last = last & (pl.program_id(ax) == grid[ax] - 1)
        if comms:
            @pl.when(first)
            def _():
                for cm, ci, co, cs in groups:
                    cm.start(ci, co, cs)
        body(*pre, *ins, *outs, *scr)
        if comms:
            @pl.when(last)
            def _():
                for cm, ci, co, cs in groups:
                    cm.finish(ci, co, cs)

    grid_spec = pltpu.PrefetchScalarGridSpec(
        num_scalar_prefetch=n_pre, grid=grid,
        in_specs=list(in_specs) + [ANY] * len(c_ins), out_specs=list(out_specs) + [ANY] * len(c_outs),
        scratch_shapes=list(scratch_shapes) + c_sems)
    res = pl.pallas_call(
        wrapped, name=name, grid_spec=grid_spec, out_shape=list(out_shape) + c_outs,
        compiler_params=_params(*(["arbitrary"] * len(grid))),
    )(*prefetch, *args, *c_ins)
    outs, rest, per_comm = res[:n_out], res[n_out:], []
    for cm in comms:
        per_comm.append(rest[:len(cm.outs)])
        rest = rest[len(cm.outs):]
    return outs, per_comm


def _rope_tables(pos_row, inv_freq_col, comms=()):
    t_tok = pos_row.shape[1]
    tm = min(512, t_tok)

    def body(pos_ref, invf_ref, cos_ref, sin_ref):
        ang = pos_ref[...].astype(F32) * invf_ref[...]
        row = lax.broadcasted_iota(jnp.int32, ang.shape, 0)
        cos_ref[...] = jnp.cos(ang)
        sin_ref[...] = jnp.sin(ang) * jnp.where(row < HEAD_DIM // 2, -1.0, 1.0)

    return _carry(
        body, name="rope_tables", grid=(t_tok // tm,), comms=comms,
        in_specs=[pl.BlockSpec((1, tm), lambda i: (0, i)), pl.BlockSpec((HEAD_DIM, 1), lambda i: (0, 0))],
        out_specs=[pl.BlockSpec((HEAD_DIM, tm), lambda i: (0, i))] * 2,
        out_shape=[jax.ShapeDtypeStruct((HEAD_DIM, t_tok), F32)] * 2,
        args=(pos_row, inv_freq_col))


def _proj_in(x2, w_in_t, comms=()):
    t_tok, d = x2.shape
    d_in = w_in_t.shape[0]
    tm = min(512, t_tok)

    def body(x_ref, w_ref, h_ref, xb_ref):
        xb = x_ref[...].astype(BF16)
        xb_ref[...] = xb
        h_ref[...] = _dot(w_ref[...], xb, NT)

    return _carry(
        body, name="proj_in", grid=(t_tok // tm,), comms=comms,
        in_specs=[pl.BlockSpec((tm, d), lambda i: (i, 0)), pl.BlockSpec((d_in, d), lambda i: (0, 0))],
        out_specs=[pl.BlockSpec((d_in, tm), lambda i: (0, i)), pl.BlockSpec((tm, d), lambda i: (i, 0))],
        out_shape=[jax.ShapeDtypeStruct((d_in, t_tok), F32), jax.ShapeDtypeStruct((t_tok, d), BF16)],
        args=(x2, w_in_t))


def _h_specs():
    kv_row = COL_K // (2 * D_KV)
    return [
        pl.BlockSpec((D_GMLP, BLK), lambda i: (0, i)),
        pl.BlockSpec((D_GMLP, BLK), lambda i: (1, i)),
        pl.BlockSpec((D_ATTN, BLK), lambda i: (2, i)),
        pl.BlockSpec((2 * D_KV, BLK), lambda i: (kv_row, i)),
        pl.BlockSpec((2 * D_KV, BLK), lambda i: (kv_row, jnp.maximum(i - 1, 0))),
    ]


def _table_specs():
    return [
        pl.BlockSpec((HEAD_DIM, BLK), lambda i: (0, i)),
        pl.BlockSpec((HEAD_DIM, BLK), lambda i: (0, i)),
        pl.BlockSpec((HEAD_DIM, BLK), lambda i: (0, jnp.maximum(i - 1, 0))),
        pl.BlockSpec((HEAD_DIM, BLK), lambda i: (0, jnp.maximum(i - 1, 0))),
    ]


def _band_bias():
    ki = lax.broadcasted_iota(jnp.int32, (2, 2 * BLK, BLK), 1)
    qi = lax.broadcasted_iota(jnp.int32, (2, 2 * BLK, BLK), 2)
    later = lax.broadcasted_iota(jnp.int32, (2, 2 * BLK, BLK), 0) > 0
    dist = qi + BLK - ki
    return jnp.where((dist >= 0) & (dist < BLK) & ((ki >= BLK) | later), 0.0, NEG_INF).astype(F32)


BIAS_SPEC = pl.BlockSpec((None, 2 * BLK, BLK), lambda i: (jnp.minimum(i, 1), 0, 0))


def _keys_values(kvc, kvp, cosc, sinc, cosp, sinp):
    kp, kc = _rope_t(kvp[:D_KV], cosp, sinp), _rope_t(kvc[:D_KV], cosc, sinc)
    k_t = jnp.concatenate([kp, kc], axis=1).astype(BF16)
    k_n = jnp.concatenate([kp.T, kc.T], axis=0).astype(BF16)
    v_t = jnp.concatenate([kvp[D_KV:], kvc[D_KV:]], axis=1).astype(BF16)
    return k_t, k_n, v_t


def _pad_head(th, kv):
    z = jnp.zeros_like(th)
    return jnp.concatenate([th, z] if kv == 0 else [z, th], axis=0)


def _group_lanes(parts):
    return jnp.concatenate(parts, axis=1)


def _softmax_sink_t(s, sink):
    m = jnp.maximum(jnp.max(s, axis=0, keepdims=True), sink)
    e = jnp.exp(s - m)
    es = jnp.exp(sink - m)
    r = 1.0 / (jnp.sum(e, axis=0, keepdims=True) + es)
    return e * r, es * r


def _causal():
    row = lax.broadcasted_iota(jnp.int32, (BLK, BLK), 0)
    col = lax.broadcasted_iota(jnp.int32, (BLK, BLK), 1)
    return row >= col


def _mask_w_once(wsp_ref, wm_scr):
    @pl.when(pl.program_id(0) == 0)
    def _():
        causal = _causal()
        for hh in range(N_HEADS):
            wm_scr[hh] = jnp.where(causal, wsp_ref[hh], 0.0).astype(BF16)


def _mixer_fwd(h_t, cos_t, sin_t, w_spatial, b_spatial, vln_g, vln_b, sinks, band_bias, comms=()):
    t_tok = h_t.shape[1]
    nb = t_tok // BLK
    group = N_HEADS // N_KV_HEADS

    def body(sinks_ref, u_ref, vg_ref, q_ref, kvc_ref, kvp_ref, cosc_ref, sinc_ref, cosp_ref, sinp_ref,
             wsp_ref, bsp_ref, g_ref, b_ref, bias_ref, cat_ref, wm_scr):
        _mask_w_once(wsp_ref, wm_scr)
        ua = _gelu(u_ref[...])
        vp, _, _ = _ln_fwd_t(_gelu(vg_ref[...]), g_ref[...], b_ref[...])
        vpb = vp.astype(BF16)
        for hh in range(N_HEADS):
            rows = slice(hh * HEAD_DIM, (hh + 1) * HEAD_DIM)
            mixed = _dot(vpb[rows], wm_scr[hh], NT) + bsp_ref[hh:hh + 1, :]
            cat_ref[rows, :] = (ua[rows] * mixed).astype(BF16)

        cosc, sinc = cosc_ref[...], sinc_ref[...]
        qr = (_rope_t(q_ref[...], cosc, sinc) * SCORE_SCALE).astype(BF16)
        _, k_n, v_t = _keys_values(kvc_ref[...], kvp_ref[...], cosc, sinc, cosp_ref[...], sinp_ref[...])
        bias = _group_lanes([bias_ref[...]] * group)
        for kv in range(N_KV_HEADS):
            heads = range(kv * group, (kv + 1) * group)
            qs = _group_lanes([qr[hh * HEAD_DIM:(hh + 1) * HEAD_DIM] for hh in heads])
            sink = _group_lanes([jnp.full((1, BLK), sinks_ref[hh], F32) for hh in heads])
            p, _ = _softmax_sink_t(_dot(k_n, _pad_head(qs, kv)) + bias, sink)
            o = _dot(v_t[kv * HEAD_DIM:(kv + 1) * HEAD_DIM], p.astype(BF16)).astype(BF16)
            for j, hh in enumerate(heads):
                cat_ref[D_GMLP + hh * HEAD_DIM:D_GMLP + (hh + 1) * HEAD_DIM, :] = o[:, j * BLK:(j + 1) * BLK]

    full = lambda shape: pl.BlockSpec(shape, lambda i: (0,) * len(shape))
    return _carry(
        body, name="mixer_fwd", grid=(nb,), comms=comms,
        in_specs=[pl.BlockSpec(memory_space=pltpu.SMEM)] + _h_specs() + _table_specs() + [
            full((N_HEADS, BLK, BLK)), full((N_HEADS, BLK)), full((D_GMLP, 1)), full((D_GMLP, 1)), BIAS_SPEC],
        out_specs=[pl.BlockSpec((D_GMLP + D_ATTN, BLK), lambda i: (0, i))],
        out_shape=[jax.ShapeDtypeStruct((D_GMLP + D_ATTN, t_tok), BF16)],
        scratch_shapes=[pltpu.VMEM((N_HEADS, BLK, BLK), BF16)],
        args=(sinks, h_t, h_t, h_t, h_t, h_t, cos_t, sin_t, cos_t, sin_t, w_spatial, b_spatial, vln_g, vln_b, band_bias))


def _proj_out(cat_t, x2, w_out_b):
    t_tok, d = x2.shape
    tm = min(512, t_tok)

    def body(cat_ref, x_ref, w_ref, z_ref):
        z_ref[...] = ALPHA * x_ref[...] + _dot(cat_ref[...], w_ref[...], TN)

    return _carry(
        body, name="proj_out", grid=(t_tok // tm,),
        in_specs=[pl.BlockSpec((cat_t.shape[0], tm), lambda i: (0, i)), pl.BlockSpec((tm, d), lambda i: (i, 0)),
                  pl.BlockSpec(w_out_b.shape, lambda i: (0, 0))],
        out_specs=[pl.BlockSpec((tm, d), lambda i: (i, 0))],
        out_shape=[jax.ShapeDtypeStruct((t_tok, d), F32)],
        args=(cat_t, x2, w_out_b))[0][0]


def _ffn_fwd_bwd(z1, target, w1_b, w2_b, ln1_g, ln1_b, ln2_g, ln2_b):
    t_tok, d = z1.shape
    n_chunk, _, fc = w1_b.shape
    f = n_chunk * fc
    tm = min(256, t_tok)

    def body(z1_ref, tgt_ref, w1_ref, w2_ref, g1_ref, b1_ref, g2_ref, b2_ref,
             act_ref, dpre_ref, x1b_ref, dz2b_ref, dz1_ref, stats_ref, r_scr):
        @pl.when(pl.program_id(0) == 0)
        def _():
            stats_ref[...] = jnp.zeros_like(stats_ref)

        g1, g2 = g1_ref[...], g2_ref[...]
        x1, xhat1, rstd1 = _ln_fwd(z1_ref[...], g1, b1_ref[...])
        x1b = x1.astype(BF16)
        x1b_ref[...] = x1b
        ff = jnp.zeros((tm, d), F32)
        for j in range(n_chunk):
            r = jnp.maximum(_dot(x1b, w1_ref[j]), 0.0)
            r_scr[:, j * fc:(j + 1) * fc] = r
            act = (r * r).astype(BF16)
            act_ref[:, j * fc:(j + 1) * fc] = act
            ff = ff + _dot(act, w2_ref[j])
        y, xhat2, rstd2 = _ln_fwd(ALPHA * x1 + ff, g2, b2_ref[...])
        diff = y - tgt_ref[...]
        loss = 0.5 * jnp.sum(jnp.sum(diff * diff, axis=-1, keepdims=True) / d, axis=0, keepdims=True)
        dy = diff / d
        dz2 = _ln_bwd(dy, xhat2, rstd2, g2)
        dz2b = dz2.astype(BF16)
        dz2b_ref[...] = dz2b
        dx1 = ALPHA * dz2
        for j in range(n_chunk):
            dpre = (_dot(dz2b, w2_ref[j], NT) * (2.0 * r_scr[:, j * fc:(j + 1) * fc])).astype(BF16)
            dpre_ref[:, j * fc:(j + 1) * fc] = dpre
            dx1 = dx1 + _dot(dpre, w1_ref[j], NT)
        dz1_ref[...] = _ln_bwd(dx1, xhat1, rstd1, g1)
        stats_ref[0:1, :] += jnp.sum(dx1 * xhat1, axis=0, keepdims=True)
        stats_ref[1:2, :] += jnp.sum(dx1, axis=0, keepdims=True)
        stats_ref[2:3, :] += jnp.sum(dy * xhat2, axis=0, keepdims=True)
        stats_ref[3:4, :] += jnp.sum(dy, axis=0, keepdims=True)
        stats_ref[4:5, :] += jnp.broadcast_to(loss, (1, d))

    tok = lambda w: pl.BlockSpec((tm, w), lambda i: (i, 0))
    vec = pl.BlockSpec((1, d), lambda i: (0, 0))
    return _carry(
        body, name="ffn_fwd_bwd", grid=(t_tok // tm,),
        in_specs=[tok(d), tok(d),
                  pl.BlockSpec(w1_b.shape, lambda i: (0, 0, 0), pipeline_mode=pl.Buffered(1)),
                  pl.BlockSpec(w2_b.shape, lambda i: (0, 0, 0), pipeline_mode=pl.Buffered(1)),
                  vec, vec, vec, vec],
        out_specs=[tok(f), tok(f), tok(d), tok(d), tok(d), pl.BlockSpec((8, d), lambda i: (0, 0))],
        out_shape=[jax.ShapeDtypeStruct((t_tok, f), BF16), jax.ShapeDtypeStruct((t_tok, f), BF16),
                   jax.ShapeDtypeStruct((t_tok, d), BF16), jax.ShapeDtypeStruct((t_tok, d), BF16),
                   jax.ShapeDtypeStruct((t_tok, d), F32), jax.ShapeDtypeStruct((8, d), F32)],
        scratch_shapes=[pltpu.VMEM((tm, f), F32)],
        args=(z1, target, w1_b, w2_b, ln1_g, ln1_b, ln2_g, ln2_b))[0]


def _ffn_wgrad1(x1b, dpre_b, n_chunk, comms=()):
    t_tok, d = x1b.shape
    fc = dpre_b.shape[1] // n_chunk
    tk = min(2048, t_tok)

    def body(x1_ref, dpre_ref, g_ref):
        @pl.when(pl.program_id(1) == 0)
        def _():
            g_ref[...] = jnp.zeros_like(g_ref)

        g_ref[...] += _dot(x1_ref[...], dpre_ref[...], TN)

    return _carry(
        body, name="ffn_wgrad1", grid=(n_chunk, t_tok // tk), comms=comms,
        in_specs=[pl.BlockSpec((tk, d), lambda j, t: (t, 0)), pl.BlockSpec((tk, fc), lambda j, t: (t, j))],
        out_specs=[pl.BlockSpec((None, d, fc), lambda j, t: (j, 0, 0))],
        out_shape=[jax.ShapeDtypeStruct((n_chunk, d, fc), F32)],
        args=(x1b, dpre_b))


def _ffn_wgrad2(act_b, dz2b, n_chunk, comms=()):
    t_tok, d = dz2b.shape
    fc = act_b.shape[1] // n_chunk
    tk = min(2048, t_tok)

    def body(act_ref, dz2_ref, g_ref):
        @pl.when(pl.program_id(1) == 0)
        def _():
            g_ref[...] = jnp.zeros_like(g_ref)

        g_ref[...] += _dot(act_ref[...], dz2_ref[...], TN)

    return _carry(
        body, name="ffn_wgrad2", grid=(n_chunk, t_tok // tk), comms=comms,
        in_specs=[pl.BlockSpec((tk, fc), lambda j, t: (t, j)), pl.BlockSpec((tk, d), lambda j, t: (t, 0))],
        out_specs=[pl.BlockSpec((None, fc, d), lambda j, t: (j, 0, 0))],
        out_shape=[jax.ShapeDtypeStruct((n_chunk, fc, d), F32)],
        args=(act_b, dz2b))


def _proj_out_bwd(dz1, cat_t, w_out_b, comms=()):
    t_tok, d = dz1.shape
    d_mix = cat_t.shape[0]
    tm = min(512, t_tok)

    def body(dz1_ref, cat_ref, w_ref, dcat_ref, gw_ref):
        @pl.when(pl.program_id(0) == 0)
        def _():
            gw_ref[...] = jnp.zeros_like(gw_ref)

        dzb = dz1_ref[...].astype(BF16)
        dcat_ref[...] = _dot(w_ref[...], dzb, NT)
        gw_ref[...] += _dot(cat_ref[...], dzb)

    return _carry(
        body, name="proj_out_bwd", grid=(t_tok // tm,), comms=comms,
        in_specs=[pl.BlockSpec((tm, d), lambda i: (i, 0)), pl.BlockSpec((d_mix, tm), lambda i: (0, i)),
                  pl.BlockSpec((d_mix, d), lambda i: (0, 0))],
        out_specs=[pl.BlockSpec((d_mix, tm), lambda i: (0, i)), pl.BlockSpec((d_mix, d), lambda i: (0, 0))],
        out_shape=[jax.ShapeDtypeStruct((d_mix, t_tok), F32), jax.ShapeDtypeStruct((d_mix, d), F32)],
        args=(dz1, cat_t, w_out_b))


def _mixer_bwd(dcat_t, h_t, cos_t, sin_t, w_spatial, b_spatial, vln_g, vln_b, sinks, band_bias, comms=()):
    t_tok = h_t.shape[1]
    nb = t_tok // BLK
    group = N_HEADS // N_KV_HEADS

    def body(sinks_ref, dcat_ref, u_ref, vg_ref, q_ref, kvc_ref, kvp_ref, cosc_ref, sinc_ref, cosp_ref, sinp_ref,
             wsp_ref, bsp_ref, g_ref, b_ref, bias_ref, dh_ref, dkvp_ref, gws_ref, gbsp_ref, gvln_ref, gsink_ref,
             dg_acc, db_acc, wm_scr):
        i = pl.program_id(0)

        @pl.when(i == 0)
        def _():
            gws_ref[...] = jnp.zeros_like(gws_ref)
            gbsp_ref[...] = jnp.zeros_like(gbsp_ref)
            gsink_ref[...] = jnp.zeros_like(gsink_ref)
            dg_acc[...] = jnp.zeros_like(dg_acc)
            db_acc[...] = jnp.zeros_like(db_acc)

        _mask_w_once(wsp_ref, wm_scr)

        g = g_ref[...]
        ua, ua_grad = _gelu_and_grad(u_ref[...])
        vv, vv_grad = _gelu_and_grad(vg_ref[...])
        vp, vhat, rstd = _ln_fwd_t(vv, g, b_ref[...])
        vpb = vp.astype(BF16)
        da = dcat_ref[0:D_GMLP, :]
        dmixed = da * ua
        dvp_parts = []
        for hh in range(N_HEADS):
            rows = slice(hh * HEAD_DIM, (hh + 1) * HEAD_DIM)
            mixed = _dot(vpb[rows], wm_scr[hh], NT) + bsp_ref[hh:hh + 1, :]
            dh_ref[COL_U + hh * HEAD_DIM:COL_U + (hh + 1) * HEAD_DIM, :] = da[rows] * mixed * ua_grad[rows]
            dm = dmixed[rows]
            dmb = dm.astype(BF16)
            gbsp_ref[hh:hh + 1, :] += jnp.sum(dm, axis=0, keepdims=True)
            gws_ref[hh] += _dot(dmb, vpb[rows], TN)
            dvp_parts.append(_dot(dmb, wm_scr[hh]))
        dvp = jnp.concatenate(dvp_parts, axis=0)
        dg_acc[...] += dvp * vhat
        db_acc[...] += dvp
        dh_ref[COL_V:COL_V + D_GMLP, :] = _ln_bwd_t(dvp, vhat, rstd, g) * vv_grad

        cosc, sinc, cosp, sinp = cosc_ref[...], sinc_ref[...], cosp_ref[...], sinp_ref[...]
        qr = (_rope_t(q_ref[...], cosc, sinc) * SCORE_SCALE).astype(BF16)
        k_t, k_n, v_t = _keys_values(kvc_ref[...], kvp_ref[...], cosc, sinc, cosp, sinp)
        v_n = jnp.concatenate([kvp_ref[D_KV:, :].T, kvc_ref[D_KV:, :].T], axis=0).astype(BF16)
        bias = _group_lanes([bias_ref[...]] * group)
        dk, dv, dq_parts = [], [], []
        for kv in range(N_KV_HEADS):
            heads = range(kv * group, (kv + 1) * group)
            kv_rows = slice(kv * HEAD_DIM, (kv + 1) * HEAD_DIM)
            qs = _group_lanes([qr[hh * HEAD_DIM:(hh + 1) * HEAD_DIM] for hh in heads])
            dos = _group_lanes([dcat_ref[D_GMLP + hh * HEAD_DIM:D_GMLP + (hh + 1) * HEAD_DIM, :] for hh in heads]).astype(BF16)
            sink = _group_lanes([jnp.full((1, BLK), sinks_ref[hh], F32) for hh in heads])
            p, p_sink = _softmax_sink_t(_dot(k_n, _pad_head(qs, kv)) + bias, sink)
            dp = _dot(v_n, _pad_head(dos, kv))
            delta = jnp.sum(p * dp, axis=0, keepdims=True)
            ds = (p * (dp - delta)).astype(BF16)
            dsink = p_sink * delta
            dq = _dot(k_t[kv_rows], ds) * SCORE_SCALE
            for j, hh in enumerate(heads):
                gsink_ref[hh:hh + 1, :] -= dsink[:, j * BLK:(j + 1) * BLK]
                dq_parts.append(dq[:, j * BLK:(j + 1) * BLK])
            dk.append(_dot(qs, ds, NT))
            dv.append(_dot(dos, p.astype(BF16), NT))
        dh_ref[COL_Q:COL_Q + D_ATTN, :] = _rope_t(jnp.concatenate(dq_parts, axis=0), cosc, sinc, bwd=True)
        dk_all = jnp.concatenate(dk, axis=0)
        dv_all = jnp.concatenate(dv, axis=0)
        dh_ref[COL_K:COL_K + D_KV, :] = _rope_t(dk_all[:, BLK:], cosc, sinc, bwd=True)
        dh_ref[COL_K + D_KV:COL_K + 2 * D_KV, :] = dv_all[:, BLK:]
        dkvp_ref[0:D_KV, :] = _rope_t(dk_all[:, :BLK], cosp, sinp, bwd=True)
        dkvp_ref[D_KV:2 * D_KV, :] = dv_all[:, :BLK]

        @pl.when(i == nb - 1)
        def _():
            causal = _causal()
            for hh in range(N_HEADS):
                gws_ref[hh] = jnp.where(causal, gws_ref[hh], 0.0)
            gvln_ref[...] = jnp.zeros_like(gvln_ref)
            gvln_ref[0:1, :] = jnp.sum(dg_acc[...].T, axis=0, keepdims=True)
            gvln_ref[1:2, :] = jnp.sum(db_acc[...].T, axis=0, keepdims=True)

    full = lambda shape: pl.BlockSpec(shape, lambda i: (0,) * len(shape))
    return _carry(
        body, name="mixer_bwd", grid=(nb,), comms=comms,
        in_specs=[pl.BlockSpec(memory_space=pltpu.SMEM), pl.BlockSpec((D_GMLP + D_ATTN, BLK), lambda i: (0, i))]
        + _h_specs() + _table_specs()
        + [full((N_HEADS, BLK, BLK)), full((N_HEADS, BLK)), full((D_GMLP, 1)), full((D_GMLP, 1)), BIAS_SPEC],
        out_specs=[pl.BlockSpec((D_IN, BLK), lambda i: (0, i)),
                   pl.BlockSpec((2 * D_KV, BLK), lambda i: (0, (i + nb - 1) % nb)),
                   full((N_HEADS, BLK, BLK)), full((N_HEADS, BLK)), full((8, D_GMLP)), full((N_HEADS, LANES))],
        out_shape=[jax.ShapeDtypeStruct((D_IN, t_tok), F32), jax.ShapeDtypeStruct((2 * D_KV, t_tok), F32),
                   jax.ShapeDtypeStruct((N_HEADS, BLK, BLK), F32), jax.ShapeDtypeStruct((N_HEADS, BLK), F32),
                   jax.ShapeDtypeStruct((8, D_GMLP), F32), jax.ShapeDtypeStruct((N_HEADS, LANES), F32)],
        scratch_shapes=[pltpu.VMEM((D_GMLP, BLK), F32), pltpu.VMEM((D_GMLP, BLK), F32), pltpu.VMEM((N_HEADS, BLK, BLK), BF16)],
        args=(sinks, dcat_t, h_t, h_t, h_t, h_t, h_t, cos_t, sin_t, cos_t, sin_t, w_spatial, b_spatial, vln_g, vln_b, band_bias))


def _proj_in_wgrad(dh_t, dkvp_t, xb, comms=()):
    t_tok, d = xb.shape
    d_in = dh_t.shape[0]
    tm = min(512, t_tok)

    def body(dh_ref, dkvp_ref, xb_ref, dhb_ref, gw_ref):
        @pl.when(pl.program_id(0) == 0)
        def _():
            gw_ref[...] = jnp.zeros_like(gw_ref)

        dhb = jnp.concatenate([dh_ref[0:COL_K, :], dh_ref[COL_K:, :] + dkvp_ref[...]], axis=0).astype(BF16)
        dhb_ref[...] = dhb
        gw_ref[...] += _dot(dhb, xb_ref[...])

    return _carry(
        body, name="proj_in_wgrad", grid=(t_tok // tm,), comms=comms,
        in_specs=[pl.BlockSpec((d_in, tm), lambda i: (0, i)), pl.BlockSpec((2 * D_KV, tm), lambda i: (0, i)),
                  pl.BlockSpec((tm, d), lambda i: (i, 0))],
        out_specs=[pl.BlockSpec((d_in, tm), lambda i: (0, i)), pl.BlockSpec((d_in, d), lambda i: (0, 0))],
        out_shape=[jax.ShapeDtypeStruct((d_in, t_tok), BF16), jax.ShapeDtypeStruct((d_in, d), F32)],
        args=(dh_t, dkvp_t, xb))


def _proj_in_dgrad(dhb_t, dz1, w_in_t, comms=()):
    t_tok, d = dz1.shape
    d_in = dhb_t.shape[0]
    tm = min(512, t_tok)

    def body(dhb_ref, dz1_ref, w_ref, dx_ref):
        dx_ref[...] = ALPHA * dz1_ref[...] + _dot(dhb_ref[...], w_ref[...], TN)

    return _carry(
        body, name="proj_in_dgrad", grid=(t_tok // tm,), comms=comms,
        in_specs=[pl.BlockSpec((d_in, tm), lambda i: (0, i)), pl.BlockSpec((tm, d), lambda i: (i, 0)),
                  pl.BlockSpec((d_in, d), lambda i: (0, 0))],
        out_specs=[pl.BlockSpec((tm, d), lambda i: (i, 0))],
        out_shape=[jax.ShapeDtypeStruct((t_tok, d), F32)],
        args=(dhb_t, dz1, w_in_t))


def _adamw(w, g, m, v):
    m = ADAM_B1 * m + (1.0 - ADAM_B1) * g
    v = ADAM_B2 * v + (1.0 - ADAM_B2) * (g * g)
    m_hat = m / (1.0 - ADAM_B1 ** ADAM_STEP)
    v_hat = v / (1.0 - ADAM_B2 ** ADAM_STEP)
    delta = -ADAM_LR * (m_hat / (jnp.sqrt(v_hat) + ADAM_EPS) + ADAM_WD * w)
    return delta, m, v


def _sum4(own_ref, recv_ref):
    return ((own_ref[...] + recv_ref[0].astype(F32)) + recv_ref[1].astype(F32)) + recv_ref[2].astype(F32)


def _adamw_shard(name, own, recv3, w, m, v, comms=()):
    r, c = w.shape
    tr = r if r <= 512 else 512
    blk = pl.BlockSpec((tr, c), lambda i: (i, 0))
    recv = [] if recv3 is None else [recv3]

    def body(own_ref, *refs):
        w_ref, m_ref, v_ref, g_out, d_out, m_out, v_out = refs[len(recv):]
        g = _sum4(own_ref, refs[0]) if recv else own_ref[...]
        delta, m_new, v_new = _adamw(w_ref[...], g, m_ref[...], v_ref[...])
        g_out[...] = g
        d_out[...] = delta
        m_out[...] = m_new
        v_out[...] = v_new

    return _carry(
        body, name=name, grid=(r // tr,), comms=comms,
        in_specs=[blk] + [pl.BlockSpec((3, tr, c), lambda i: (0, i, 0))] * len(recv) + [blk, blk, blk],
        out_specs=[blk] * 4, out_shape=[jax.ShapeDtypeStruct((r, c), F32)] * 4,
        args=(own, *recv, w, m, v))


def _sum_partials(name, own, recv3):
    r, c = own.shape

    def body(own_ref, recv_ref, g_out):
        g_out[...] = _sum4(own_ref, recv_ref)

    return _carry(
        body, name=name, grid=(1,),
        in_specs=[pl.BlockSpec((r, c), lambda i: (0, 0)), pl.BlockSpec((3, r, c), lambda i: (0, 0, 0))],
        out_specs=[pl.BlockSpec((r, c), lambda i: (0, 0))], out_shape=[jax.ShapeDtypeStruct((r, c), F32)],
        args=(own, recv3))[0][0]


def _adamw_small(parts, w, m, v):
    n, r, c = parts.shape

    def body(p_ref, w_ref, m_ref, v_ref, g_out, d_out, m_out, v_out):
        g = p_ref[0]
        for k in range(1, n):
            g = g + p_ref[k]
        delta, m_new, v_new = _adamw(w_ref[...], g, m_ref[...], v_ref[...])
        g_out[...] = g
        d_out[...] = delta
        m_out[...] = m_new
        v_out[...] = v_new

    full = pl.BlockSpec((r, c), lambda i: (0, 0))
    return _carry(
        body, name="adamw_small", grid=(1,),
        in_specs=[pl.BlockSpec((n, r, c), lambda i: (0, 0, 0)), full, full, full],
        out_specs=[full] * 4, out_shape=[jax.ShapeDtypeStruct((r, c), F32)] * 4,
        args=(parts, w, m, v))[0]


def _pair_sum(name, parts, recv, core_chip):
    _, r, c = parts.shape
    tr = r if r <= 512 else 512

    def body(cc_ref, a_ref, b_ref, wire_ref, own_ref):
        s = a_ref[...] + b_ref[...]
        wire_ref[...] = s.astype(BF16)

        @pl.when(pl.program_id(1) == cc_ref[1])
        def _():
            own_ref[...] = s

    return _carry(
        body, name=name, grid=(r // tr, 4), prefetch=(core_chip,),
        in_specs=[pl.BlockSpec((None, tr, c), lambda i, q, cc: (2 * q + cc[0], i, 0)),
                  pl.BlockSpec((None, tr, c), lambda i, q, cc: (q, i, 0))],
        out_specs=[pl.BlockSpec((None, tr, c), lambda i, q, cc: (q, i, 0)), pl.BlockSpec((tr, c), lambda i, q, cc: (i, 0))],
        out_shape=[jax.ShapeDtypeStruct((4, r, c), BF16), jax.ShapeDtypeStruct((r, c), F32)],
        args=(parts, recv))[0]


def _pack_small(v_ln_g, v_ln_b, w_spatial, b_spatial, sinks, ln1_g, ln1_b, ln2_g, ln2_b, tail=None):
    d = ln1_g.shape[-1]
    rows = [w_spatial.reshape(-1, d),
            jnp.concatenate([v_ln_g.reshape(1, -1), v_ln_b.reshape(1, -1)], axis=1),
            b_spatial.reshape(1, -1),
            ln1_g.reshape(1, d), ln1_b.reshape(1, d), ln2_g.reshape(1, d), ln2_b.reshape(1, d),
            jnp.pad(sinks.reshape(1, -1), ((0, 0), (0, d - N_HEADS))),
            jnp.zeros((1, d), F32) if tail is None else tail]
    return jnp.concatenate(rows, axis=0)


def _unpack_small(p):
    n_ws = N_HEADS * BLK * BLK // p.shape[1]
    w_spatial = p[:n_ws].reshape(1, N_HEADS, BLK, BLK)
    v_ln_g, v_ln_b = p[n_ws:n_ws + 1, :D_GMLP], p[n_ws:n_ws + 1, D_GMLP:2 * D_GMLP]
    b_spatial = p[n_ws + 1, :N_HEADS * BLK].reshape(1, N_HEADS, BLK)
    ln1_g, ln1_b, ln2_g, ln2_b = (p[n_ws + 2 + k:n_ws + 3 + k] for k in range(4))
    sinks = p[n_ws + 6:n_ws + 7, :N_HEADS]
    return [v_ln_g, v_ln_b, w_spatial, b_spatial, sinks, None, ln1_g, ln1_b, None, None, ln2_g, ln2_b]


def kernel(x, positions, w_in, v_ln_g, v_ln_b, w_spatial, b_spatial, sinks, w_out, ln1_g, ln1_b, w_ff1, w_ff2, ln2_g, ln2_b, loss_target, m_w_in, m_v_ln_g, m_v_ln_b, m_w_spatial, m_b_spatial, m_sinks, m_w_out, m_ln1_g, m_ln1_b, m_w_ff1, m_w_ff2, m_ln2_g, m_ln2_b, v_w_in, v_v_ln_g, v_v_ln_b, v_w_spatial, v_b_spatial, v_sinks, v_w_out, v_ln1_g, v_ln1_b, v_w_ff1, v_w_ff2, v_ln2_g, v_ln2_b):
    _, t_tok, d = x.shape
    xi, yi, ci = _place()
    core_chip = jnp.stack([ci, 2 * xi + yi]).astype(jnp.int32)
    x2 = x.reshape(t_tok, d)
    target = loss_target.reshape(t_tok, d)
    inv_freq = ROPE_THETA ** (-jnp.arange(0, HEAD_DIM, 2, dtype=F32) / HEAD_DIM)
    wsp, bsp, sink_vec = w_spatial[0], b_spatial[0], sinks[0]
    vg_col, vb_col = v_ln_g.reshape(D_GMLP, 1), v_ln_b.reshape(D_GMLP, 1)
    big = {"in": w_in[0], "out": w_out[0], "ff1": w_ff1[0], "ff2": w_ff2[0]}
    big_b = {k: w.astype(BF16) for k, w in big.items()}

    (cos_t, sin_t), ((g_in,),) = _rope_tables(
        positions, jnp.tile(inv_freq, 2).reshape(HEAD_DIM, 1), comms=[_gather_comm([big["in"].T.astype(BF16)])])
    w_in_t = g_in.reshape(D_IN, d)
    (h_t, xb), ((g_out, w1_b),) = _proj_in(x2, w_in_t, comms=[_gather_comm([big_b["out"], big_b["ff1"]])])
    w_out_b = g_out.reshape(-1, d)
    band_bias = _band_bias()
    (cat_t,), ((w2_b,),) = _mixer_fwd(h_t, cos_t, sin_t, wsp, bsp, vg_col, vb_col, sink_vec, band_bias,
                                      comms=[_gather_comm([big_b["ff2"]])])
    z1 = _proj_out(cat_t, x2, w_out_b)
    act_b, dpre_b, x1b, dz2b, dz1, stats = _ffn_fwd_bwd(z1, target, w1_b, w2_b, ln1_g, ln1_b, ln2_g, ln2_b)

    (p_ff1,), _ = _ffn_wgrad1(x1b, dpre_b, N_DEV)
    (p_ff2,), ((s_ff1,),) = _ffn_wgrad2(act_b, dz2b, N_DEV, comms=[_sibling_comm([p_ff1])])
    wire_ff1, own_ff1 = _pair_sum("pair_sum_ff1", p_ff1, s_ff1, core_chip)
    (dcat_t, gw_out), ((s_ff2,), (r_ff1,)) = _proj_out_bwd(
        dz1, cat_t, w_out_b, comms=[_sibling_comm([p_ff2]), _chips_comm([wire_ff1])])
    wire_ff2, own_ff2 = _pair_sum("pair_sum_ff2", p_ff2, s_ff2, core_chip)
    p_out = gw_out.reshape(N_DEV, -1, d)
    (dh_t, dkvp_t, g_wsp, g_bsp, g_vln, g_sink), ((r_ff2,), (s_out,)) = _mixer_bwd(
        dcat_t, h_t, cos_t, sin_t, wsp, bsp, vg_col, vb_col, sink_vec, band_bias,
        comms=[_chips_comm([wire_ff2]), _sibling_comm([p_out])])
    wire_out, own_out = _pair_sum("pair_sum_out", p_out, s_out, core_chip)
    small_g = _pack_small(g_vln[0], g_vln[1], g_wsp, g_bsp, g_sink.sum(axis=1), stats[0], stats[1], stats[2], stats[3],
                          tail=stats[4:5])
    (dhb_t, gw_in_t), ((small_parts,), (r_out,)) = _proj_in_wgrad(
        dh_t, dkvp_t, xb, comms=[_gather_comm([small_g]), _chips_comm([wire_out])])
    p_in = gw_in_t.reshape(N_DEV, -1, d)
    (grad_x,), ((s_in,),) = _proj_in_dgrad(dhb_t, dz1, w_in_t, comms=[_sibling_comm([p_in])])
    wire_in, own_in = _pair_sum("pair_sum_in", p_in, s_in, core_chip)

    ff1_out, ((r_in,),) = _adamw_shard("adamw_ff1", own_ff1, r_ff1, big["ff1"], m_w_ff1[0], v_w_ff1[0], comms=[_chips_comm([wire_in])])
    ff2_out, _ = _adamw_shard("adamw_ff2", own_ff2, r_ff2, big["ff2"], m_w_ff2[0], v_w_ff2[0])
    out_out, _ = _adamw_shard("adamw_out", own_out, r_out, big["out"], m_w_out[0], v_w_out[0])
    g_in_t = _sum_partials("sum_in", own_in, r_in)
    in_out, _ = _adamw_shard("adamw_in", g_in_t.T, None, big["in"], m_w_in[0], v_w_in[0])
    small_w = _pack_small(v_ln_g, v_ln_b, w_spatial, b_spatial, sinks, ln1_g, ln1_b, ln2_g, ln2_b)
    small_m = _pack_small(m_v_ln_g, m_v_ln_b, m_w_spatial, m_b_spatial, m_sinks, m_ln1_g, m_ln1_b, m_ln2_g, m_ln2_b)
    small_v = _pack_small(v_v_ln_g, v_v_ln_b, v_w_spatial, v_b_spatial, v_sinks, v_ln1_g, v_ln1_b, v_ln2_g, v_ln2_b)
    small_res = _adamw_small(small_parts, small_w, small_m, small_v)
    small_out = [_unpack_small(p) for p in small_res]
    loss = small_res[0][-1, 0]

    big_out = {0: in_out, 6: out_out, 9: ff1_out, 10: ff2_out}
    outs = [loss, grad_x.reshape(x.shape)]
    for kind in range(4):
        for wi in range(13):
            outs.append(big_out[wi][kind][None] if wi in big_out else small_out[kind][wi - 1])
    return tuple(outs)
```

```python
import functools
import math

import jax
import jax.numpy as jnp
from jax import lax
from jax.experimental import pallas as pl
from jax.experimental.pallas import tpu as pltpu

F32 = jnp.float32
BF16 = jnp.bfloat16
MESH = pl.DeviceIdType.MESH

HEAD_DIM = 64
N_HEADS = 8
N_KV_HEADS = 2
BLK = 128
D_GMLP = N_HEADS * HEAD_DIM
D_ATTN = N_HEADS * HEAD_DIM
D_KV = N_KV_HEADS * HEAD_DIM
D_IN = 2 * D_GMLP + D_ATTN + 2 * D_KV
COL_U, COL_V, COL_Q, COL_K = 0, D_GMLP, 2 * D_GMLP, 2 * D_GMLP + D_ATTN
ROPE_THETA = 10000.0
LN_EPS = 1e-5
ALPHA = 2.0 ** 0.25
NEG_INF = -1e30
SCORE_SCALE = 1.0 / math.sqrt(HEAD_DIM)
ADAM_LR, ADAM_B1, ADAM_B2, ADAM_EPS, ADAM_WD, ADAM_STEP = 0.001, 0.9, 0.999, 1e-08, 0.01, 10
N_DEV = 8
LANES = 128
VMEM_LIMIT = 56 * 1024 * 1024

NT = (((1,), (1,)), ((), ()))
TN = (((0,), (0,)), ((), ()))


def _params(*sem):
    return pltpu.CompilerParams(dimension_semantics=sem, vmem_limit_bytes=VMEM_LIMIT)


def _dot(a, b, dims=None):
    if dims is None:
        return jnp.dot(a, b, preferred_element_type=F32)
    return lax.dot_general(a, b, dims, preferred_element_type=F32)


def _mean(a):
    return jnp.mean(a, axis=-1, keepdims=True)


def _ln_fwd(z, g, b):
    zc = z - _mean(z)
    rstd = lax.rsqrt(_mean(zc * zc) + LN_EPS)
    xhat = zc * rstd
    return xhat * g + b, xhat, rstd


def _ln_bwd(dy, xhat, rstd, g):
    dxhat = dy * g
    return rstd * (dxhat - _mean(dxhat) - xhat * _mean(dxhat * xhat))


_GELU_C = math.sqrt(2.0 / math.pi)


def _gelu(x):
    t = jnp.tanh(_GELU_C * (x + 0.044715 * (x * x * x)))
    return 0.5 * x * (1.0 + t)


def _gelu_and_grad(x):
    x2 = x * x
    t = jnp.tanh(_GELU_C * (x + 0.044715 * (x2 * x)))
    hx, ht = 0.5 * x, 0.5 * (1.0 + t)
    return x * ht, ht + hx * (1.0 - t * t) * (_GELU_C * (1.0 + 3.0 * 0.044715 * x2))


def _mean0(a):
    return jnp.mean(a, axis=0, keepdims=True)


def _ln_fwd_t(z, g, b):
    zc = z - _mean0(z)
    rstd = lax.rsqrt(_mean0(zc * zc) + LN_EPS)
    xhat = zc * rstd
    return xhat * g + b, xhat, rstd


def _ln_bwd_t(dy, xhat, rstd, g):
    dxhat = dy * g
    return rstd * (dxhat - _mean0(dxhat) - xhat * _mean0(dxhat * xhat))


def _rope_t(t, cos, sin_signed, bwd=False):
    half = HEAD_DIM // 2
    outs = []
    for r in range(0, t.shape[0], HEAD_DIM):
        th = t[r:r + HEAD_DIM]
        sw = jnp.concatenate([th[half:], th[:half]], axis=0) * sin_signed
        outs.append(th * cos - sw if bwd else th * cos + sw)
    return jnp.concatenate(outs, axis=0)


ANY = pl.BlockSpec(memory_space=pl.ANY)


def _place():
    return lax.axis_index("x"), lax.axis_index("y"), lax.axis_index("c")


class _Comm:
    def __init__(self, ins, outs, sems, start, finish):
        self.ins, self.outs, self.sems, self.start, self.finish = ins, outs, sems, start, finish


def _gather_comm(arrs):
    n = len(arrs)

    def parts(ins, outs, sems):
        send_sems, recv_sems, local_sems = sems
        x, y, c = _place()
        me, sibling = (x, y, c), (x, y, 1 - c)
        chips = [(1 - x, y), (x, 1 - y), (1 - x, 1 - y)]

        def copy(a, k, block, to, src=None):
            px, py, pc = block
            dst = outs[a].at[4 * px + 2 * py + pc]
            return pltpu.make_async_remote_copy(
                src_ref=dst if src is None else src, dst_ref=dst,
                send_sem=send_sems.at[a, k], recv_sem=recv_sems.at[a, k], device_id=to, device_id_type=MESH)

        mine = [pltpu.make_async_copy(ins[a], outs[a].at[4 * x + 2 * y + c], local_sems.at[a]) for a in range(n)]
        first = []
        for a in range(n):
            first.append(copy(a, 0, me, sibling, src=ins[a]))
            first += [copy(a, 1 + j, me, (*chip, c), src=ins[a]) for j, chip in enumerate(chips)]
        return copy, mine, first, me, sibling, chips, c

    def start(ins, outs, sems):
        _, mine, first, *_ = parts(ins, outs, sems)
        for cp in mine + first:
            cp.start()

    def finish(ins, outs, sems):
        copy, mine, first, me, sibling, chips, c = parts(ins, outs, sems)
        passed = []
        for j, chip in enumerate(chips):
            for a in range(n):
                copy(a, 1 + j, (*chip, c), me).wait_recv()
                fwd = copy(a, 4 + j, (*chip, c), sibling)
                fwd.start()
                passed.append(fwd)
        for a in range(n):
            copy(a, 0, sibling, me).wait_recv()
        for j, chip in enumerate(chips):
            for a in range(n):
                copy(a, 4 + j, (*chip, 1 - c), me).wait_recv()
        for cp in first + passed:
            cp.wait_send()
        for cp in mine:
            cp.wait()

    return _Comm(list(arrs), [jax.ShapeDtypeStruct((N_DEV,) + a.shape, a.dtype) for a in arrs],
                 [pltpu.SemaphoreType.DMA((n, 7)), pltpu.SemaphoreType.DMA((n, 7)), pltpu.SemaphoreType.DMA((n,))],
                 start, finish)


def _sibling_comm(parts):
    n = len(parts)

    def copies(ins, outs, sems):
        x, y, c = _place()
        return [pltpu.make_async_remote_copy(
            src_ref=ins[a].at[2 * q + (1 - c)], dst_ref=outs[a].at[q],
            send_sem=sems[0].at[a, q], recv_sem=sems[1].at[a, q],
            device_id=(x, y, 1 - c), device_id_type=MESH) for a in range(n) for q in range(4)]

    return _Comm(list(parts), [jax.ShapeDtypeStruct((4,) + p.shape[1:], p.dtype) for p in parts],
                 [pltpu.SemaphoreType.DMA((n, 4)), pltpu.SemaphoreType.DMA((n, 4))],
                 lambda *r: [cp.start() for cp in copies(*r)], lambda *r: [cp.wait() for cp in copies(*r)])


def _chips_comm(chip_parts):
    n = len(chip_parts)

    def copies(ins, outs, sems):
        x, y, c = _place()
        chips = [(1 - x, y), (x, 1 - y), (1 - x, 1 - y)]
        return [pltpu.make_async_remote_copy(
            src_ref=ins[a].at[2 * px + py], dst_ref=outs[a].at[k],
            send_sem=sems[0].at[a, k], recv_sem=sems[1].at[a, k],
            device_id=(px, py, c), device_id_type=MESH) for a in range(n) for k, (px, py) in enumerate(chips)]

    return _Comm(list(chip_parts), [jax.ShapeDtypeStruct((3,) + p.shape[1:], p.dtype) for p in chip_parts],
                 [pltpu.SemaphoreType.DMA((n, 3)), pltpu.SemaphoreType.DMA((n, 3))],
                 lambda *r: [cp.start() for cp in copies(*r)], lambda *r: [cp.wait() for cp in copies(*r)])


def _carry(body, *, name, grid, in_specs, out_specs, out_shape, args, comms=(), scratch_shapes=(), prefetch=()):
    n_pre, n_in, n_out, n_scr = len(prefetch), len(in_specs), len(out_specs), len(scratch_shapes)
    c_ins = [a for cm in comms for a in cm.ins]
    c_outs = [s for cm in comms for s in cm.outs]
    c_sems = [s for cm in comms for s in cm.sems]

    def wrapped(*refs):
        pre, refs = refs[:n_pre], refs[n_pre:]
        ins, refs = refs[:n_in], refs[n_in:]
        cins, refs = refs[:len(c_ins)], refs[len(c_ins):]
        outs, refs = refs[:n_out], refs[n_out:]
        couts, refs = refs[:len(c_outs)], refs[len(c_outs):]
        scr, sems = refs[:n_scr], refs[n_scr:]
        groups, i0, o0, s0 = [], 0, 0, 0
        for cm in comms:
            groups.append((cm, cins[i0:i0 + len(cm.ins)], couts[o0:o0 + len(cm.outs)], sems[s0:s0 + len(cm.sems)]))
            i0, o0, s0 = i0 + len(cm.ins), o0 + len(cm.outs), s0 + len(cm.sems)
        first = pl.program_id(0) == 0
        last = pl.program_id(0) == grid[0] - 1
        for ax in range(1, len(grid)):
            first = first & (pl.program_id(ax) == 0)
            last = last & (pl.program_id(ax) == grid[ax] - 1)
        if comms:
            @pl.when(first)
            def _():
                for cm, ci, co, cs in groups:
                    cm.start(ci, co, cs)
        body(*pre, *ins, *outs, *scr)
        if comms:
            @pl.when(last)
            def _():
                for cm, ci, co, cs in groups:
                    cm.finish(ci, co, cs)

    grid_spec = pltpu.PrefetchScalarGridSpec(
        num_scalar_prefetch=n_pre, grid=grid,
        in_specs=list(in_specs) + [ANY] * len(c_ins), out_specs=list(out_specs) + [ANY] * len(c_outs),
        scratch_shapes=list(scratch_shapes) + c_sems)
    res = pl.pallas_call(
        wrapped, name=name, grid_spec=grid_spec, out_shape=list(out_shape) + c_outs,
        compiler_params=_params(*(["arbitrary"] * len(grid))),
    )(*prefetch, *args, *c_ins)
    outs, rest, per_comm = res[:n_out], res[n_out:], []
    for cm in comms:
        per_comm.append(rest[:len(cm.outs)])
        rest = rest[len(cm.outs):]
    return outs, per_comm


def _rope_tables(pos_row, inv_freq_col, comms=()):
    t_tok = pos_row.shape[1]
    tm = min(512, t_tok)

    def body(pos_ref, invf_ref, cos_ref, sin_ref):
        ang = pos_ref[...].astype(F32) * invf_ref[...]
        row = lax.broadcasted_iota(jnp.int32, ang.shape, 0)
        cos_ref[...] = jnp.cos(ang)
        sin_ref[...] = jnp.sin(ang) * jnp.where(row < HEAD_DIM // 2, -1.0, 1.0)

    return _carry(
        body, name="rope_tables", grid=(t_tok // tm,), comms=comms,
        in_specs=[pl.BlockSpec((1, tm), lambda i: (0, i)), pl.BlockSpec((HEAD_DIM, 1), lambda i: (0, 0))],
        out_specs=[pl.BlockSpec((HEAD_DIM, tm), lambda i: (0, i))] * 2,
        out_shape=[jax.ShapeDtypeStruct((HEAD_DIM, t_tok), F32)] * 2,
        args=(pos_row, inv_freq_col))


def _proj_in(x2, w_in_t, comms=()):
    t_tok, d = x2.shape
    d_in = w_in_t.shape[0]
    tm = min(512, t_tok)

    def body(x_ref, w_ref, h_ref, xb_ref):
        xb = x_ref[...].astype(BF16)
        xb_ref[...] = xb
        h_ref[...] = _dot(w_ref[...], xb, NT)

    return _carry(
        body, name="proj_in", grid=(t_tok // tm,), comms=comms,
        in_specs=[pl.BlockSpec((tm, d), lambda i: (i, 0)), pl.BlockSpec((d_in, d), lambda i: (0, 0))],
        out_specs=[pl.BlockSpec((d_in, tm), lambda i: (0, i)), pl.BlockSpec((tm, d), lambda i: (i, 0))],
        out_shape=[jax.ShapeDtypeStruct((d_in, t_tok), F32), jax.ShapeDtypeStruct((t_tok, d), BF16)],
        args=(x2, w_in_t))


def _h_specs():
    kv_row = COL_K // (2 * D_KV)
    return [
        pl.BlockSpec((D_GMLP, BLK), lambda i: (0, i)),
        pl.BlockSpec((D_GMLP, BLK), lambda i: (1, i)),
        pl.BlockSpec((D_ATTN, BLK), lambda i: (2, i)),
        pl.BlockSpec((2 * D_KV, BLK), lambda i: (kv_row, i)),
        pl.BlockSpec((2 * D_KV, BLK), lambda i: (kv_row, jnp.maximum(i - 1, 0))),
    ]


def _table_specs():
    return [
        pl.BlockSpec((HEAD_DIM, BLK), lambda i: (0, i)),
        pl.BlockSpec((HEAD_DIM, BLK), lambda i: (0, i)),
        pl.BlockSpec((HEAD_DIM, BLK), lambda i: (0, jnp.maximum(i - 1, 0))),
        pl.BlockSpec((HEAD_DIM, BLK), lambda i: (0, jnp.maximum(i - 1, 0))),
    ]


def _band_bias():
    ki = lax.broadcasted_iota(jnp.int32, (2, 2 * BLK, BLK), 1)
    qi = lax.broadcasted_iota(jnp.int32, (2, 2 * BLK, BLK), 2)
    later = lax.broadcasted_iota(jnp.int32, (2, 2 * BLK, BLK), 0) > 0
    dist = qi + BLK - ki
    return jnp.where((dist >= 0) & (dist < BLK) & ((ki >= BLK) | later), 0.0, NEG_INF).astype(F32)


BIAS_SPEC = pl.BlockSpec((None, 2 * BLK, BLK), lambda i: (jnp.minimum(i, 1), 0, 0))


def _keys_values(kvc, kvp, cosc, sinc, cosp, sinp):
    kp, kc = _rope_t(kvp[:D_KV], cosp, sinp), _rope_t(kvc[:D_KV], cosc, sinc)
    k_t = jnp.concatenate([kp, kc], axis=1).astype(BF16)
    k_n = jnp.concatenate([kp.T, kc.T], axis=0).astype(BF16)
    v_t = jnp.concatenate([kvp[D_KV:], kvc[D_KV:]], axis=1).astype(BF16)
    return k_t, k_n, v_t


def _pad_head(th, kv):
    z = jnp.zeros_like(th)
    return jnp.concatenate([th, z] if kv == 0 else [z, th], axis=0)


def _group_lanes(parts):
    return jnp.concatenate(parts, axis=1)


def _softmax_sink_t(s, sink):
    m = jnp.maximum(jnp.max(s, axis=0, keepdims=True), sink)
    e = jnp.exp(s - m)
    es = jnp.exp(sink - m)
    r = 1.0 / (jnp.sum(e, axis=0, keepdims=True) + es)
    return e * r, es * r


def _causal():
    row = lax.broadcasted_iota(jnp.int32, (BLK, BLK), 0)
    col = lax.broadcasted_iota(jnp.int32, (BLK, BLK), 1)
    return row >= col


def _mask_w_once(wsp_ref, wm_scr):
    @pl.when(pl.program_id(0) == 0)
    def _():
        causal = _causal()
        for hh in range(N_HEADS):
            wm_scr[hh] = jnp.where(causal, wsp_ref[hh], 0.0).astype(BF16)


def _mixer_fwd(h_t, cos_t, sin_t, w_spatial, b_spatial, vln_g, vln_b, sinks, band_bias, comms=()):
    t_tok = h_t.shape[1]
    nb = t_tok // BLK
    group = N_HEADS // N_KV_HEADS

    def body(sinks_ref, u_ref, vg_ref, q_ref, kvc_ref, kvp_ref, cosc_ref, sinc_ref, cosp_ref, sinp_ref,
             wsp_ref, bsp_ref, g_ref, b_ref, bias_ref, cat_ref, wm_scr):
        _mask_w_once(wsp_ref, wm_scr)
        ua = _gelu(u_ref[...])
        vp, _, _ = _ln_fwd_t(_gelu(vg_ref[...]), g_ref[...], b_ref[...])
        vpb = vp.astype(BF16)
        for hh in range(N_HEADS):
            rows = slice(hh * HEAD_DIM, (hh + 1) * HEAD_DIM)
            mixed = _dot(vpb[rows], wm_scr[hh], NT) + bsp_ref[hh:hh + 1, :]
            cat_ref[rows, :] = (ua[rows] * mixed).astype(BF16)

        cosc, sinc = cosc_ref[...], sinc_ref[...]
        qr = (_rope_t(q_ref[...], cosc, sinc) * SCORE_SCALE).astype(BF16)
        _, k_n, v_t = _keys_values(kvc_ref[...], kvp_ref[...], cosc, sinc, cosp_ref[...], sinp_ref[...])
        bias = _group_lanes([bias_ref[...]] * group)
        for kv in range(N_KV_HEADS):
            heads = range(kv * group, (kv + 1) * group)
            qs = _group_lanes([qr[hh * HEAD_DIM:(hh + 1) * HEAD_DIM] for hh in heads])
            sink = _group_lanes([jnp.full((1, BLK), sinks_ref[hh], F32) for hh in heads])
            p, _ = _softmax_sink_t(_dot(k_n, _pad_head(qs, kv)) + bias, sink)
            o = _dot(v_t[kv * HEAD_DIM:(kv + 1) * HEAD_DIM], p.astype(BF16)).astype(BF16)
            for j, hh in enumerate(heads):
                cat_ref[D_GMLP + hh * HEAD_DIM:D_GMLP + (hh + 1) * HEAD_DIM, :] = o[:, j * BLK:(j + 1) * BLK]

    full = lambda shape: pl.BlockSpec(shape, lambda i: (0,) * len(shape))
    return _carry(
        body, name="mixer_fwd", grid=(nb,), comms=comms,
        in_specs=[pl.BlockSpec(memory_space=pltpu.SMEM)] + _h_specs() + _table_specs() + [
            full((N_HEADS, BLK, BLK)), full((N_HEADS, BLK)), full((D_GMLP, 1)), full((D_GMLP, 1)), BIAS_SPEC],
        out_specs=[pl.BlockSpec((D_GMLP + D_ATTN, BLK), lambda i: (0, i))],
        out_shape=[jax.ShapeDtypeStruct((D_GMLP + D_ATTN, t_tok), BF16)],
        scratch_shapes=[pltpu.VMEM((N_HEADS, BLK, BLK), BF16)],
        args=(sinks, h_t, h_t, h_t, h_t, h_t, cos_t, sin_t, cos_t, sin_t, w_spatial, b_spatial, vln_g, vln_b, band_bias))


def _proj_out(cat_t, x2, w_out_b):
    t_tok, d = x2.shape
    tm = min(512, t_tok)

    def body(cat_ref, x_ref, w_ref, z_ref):
        z_ref[...] = ALPHA * x_ref[...] + _dot(cat_ref[...], w_ref[...], TN)

    return _carry(
        body, name="proj_out", grid=(t_tok // tm,),
        in_specs=[pl.BlockSpec((cat_t.shape[0], tm), lambda i: (0, i)), pl.BlockSpec((tm, d), lambda i: (i, 0)),
                  pl.BlockSpec(w_out_b.shape, lambda i: (0, 0))],
        out_specs=[pl.BlockSpec((tm, d), lambda i: (i, 0))],
        out_shape=[jax.ShapeDtypeStruct((t_tok, d), F32)],
        args=(cat_t, x2, w_out_b))[0][0]


def _ffn_fwd_bwd(z1, target, w1_b, w2_b, ln1_g, ln1_b, ln2_g, ln2_b):
    t_tok, d = z1.shape
    n_chunk, _, fc = w1_b.shape
    f = n_chunk * fc
    tm = min(256, t_tok)

    def body(z1_ref, tgt_ref, w1_ref, w2_ref, g1_ref, b1_ref, g2_ref, b2_ref,
             act_ref, dpre_ref, x1b_ref, dz2b_ref, dz1_ref, stats_ref, r_scr):
        @pl.when(pl.program_id(0) == 0)
        def _():
            stats_ref[...] = jnp.zeros_like(stats_ref)

        g1, g2 = g1_ref[...], g2_ref[...]
        x1, xhat1, rstd1 = _ln_fwd(z1_ref[...], g1, b1_ref[...])
        x1b = x1.astype(BF16)
        x1b_ref[...] = x1b
        ff = jnp.zeros((tm, d), F32)
        for j in range(n_chunk):
            r = jnp.maximum(_dot(x1b, w1_ref[j]), 0.0)
            r_scr[:, j * fc:(j + 1) * fc] = r
            act = (r * r).astype(BF16)
            act_ref[:, j * fc:(j + 1) * fc] = act
            ff = ff + _dot(act, w2_ref[j])
        y, xhat2, rstd2 = _ln_fwd(ALPHA * x1 + ff, g2, b2_ref[...])
        diff = y - tgt_ref[...]
        loss = 0.5 * jnp.sum(jnp.sum(diff * diff, axis=-1, keepdims=True) / d, axis=0, keepdims=True)
        dy = diff / d
        dz2 = _ln_bwd(dy, xhat2, rstd2, g2)
        dz2b = dz2.astype(BF16)
        dz2b_ref[...] = dz2b
        dx1 = ALPHA * dz2
        for j in range(n_chunk):
            dpre = (_dot(dz2b, w2_ref[j], NT) * (2.0 * r_scr[:, j * fc:(j + 1) * fc])).astype(BF16)
            dpre_ref[:, j * fc:(j + 1) * fc] = dpre
            dx1 = dx1 + _dot(dpre, w1_ref[j], NT)
        dz1_ref[...] = _ln_bwd(dx1, xhat1, rstd1, g1)
        stats_ref[0:1, :] += jnp.sum(dx1 * xhat1, axis=0, keepdims=True)
        stats_ref[1:2, :] += jnp.sum(dx1, axis=0, keepdims=True)
        stats_ref[2:3, :] += jnp.sum(dy * xhat2, axis=0, keepdims=True)
        stats_ref[3:4, :] += jnp.sum(dy, axis=0, keepdims=True)
        stats_ref[4:5, :] += jnp.broadcast_to(loss, (1, d))

    tok = lambda w: pl.BlockSpec((tm, w), lambda i: (i, 0))
    vec = pl.BlockSpec((1, d), lambda i: (0, 0))
    return _carry(
        body, name="ffn_fwd_bwd", grid=(t_tok // tm,),
        in_specs=[tok(d), tok(d),
                  pl.BlockSpec(w1_b.shape, lambda i: (0, 0, 0), pipeline_mode=pl.Buffered(1)),
                  pl.BlockSpec(w2_b.shape, lambda i: (0, 0, 0), pipeline_mode=pl.Buffered(1)),
                  vec, vec, vec, vec],
        out_specs=[tok(f), tok(f), tok(d), tok(d), tok(d), pl.BlockSpec((8, d), lambda i: (0, 0))],
        out_shape=[jax.ShapeDtypeStruct((t_tok, f), BF16), jax.ShapeDtypeStruct((t_tok, f), BF16),
                   jax.ShapeDtypeStruct((t_tok, d), BF16), jax.ShapeDtypeStruct((t_tok, d), BF16),
                   jax.ShapeDtypeStruct((t_tok, d), F32), jax.ShapeDtypeStruct((8, d), F32)],
        scratch_shapes=[pltpu.VMEM((tm, f), F32)],
        args=(z1, target, w1_b, w2_b, ln1_g, ln1_b, ln2_g, ln2_b))[0]


def _ffn_wgrad1(x1b, dpre_b, n_chunk, comms=()):
    t_tok, d = x1b.shape
    fc = dpre_b.shape[1] // n_chunk

    def body(x1_ref, dpre_ref, g_ref):
        g_ref[...] = _dot(x1_ref[...], dpre_ref[...], TN)

    return _carry(
        body, name="ffn_wgrad1", grid=(n_chunk,), comms=comms,
        in_specs=[pl.BlockSpec((t_tok, d), lambda j: (0, 0), pipeline_mode=pl.Buffered(1)),
                  pl.BlockSpec((t_tok, fc), lambda j: (0, j))],
        out_specs=[pl.BlockSpec((None, d, fc), lambda j: (j, 0, 0))],
        out_shape=[jax.ShapeDtypeStruct((n_chunk, d, fc), F32)],
        args=(x1b, dpre_b))


def _ffn_wgrad2(act_b, dz2b, n_chunk, comms=()):
    t_tok, d = dz2b.shape
    fc = act_b.shape[1] // n_chunk

    def body(act_ref, dz2_ref, g_ref):
        g_ref[...] = _dot(act_ref[...], dz2_ref[...], TN)

    return _carry(
        body, name="ffn_wgrad2", grid=(n_chunk,), comms=comms,
        in_specs=[pl.BlockSpec((t_tok, fc), lambda j: (0, j)),
                  pl.BlockSpec((t_tok, d), lambda j: (0, 0), pipeline_mode=pl.Buffered(1))],
        out_specs=[pl.BlockSpec((None, fc, d), lambda j: (j, 0, 0))],
        out_shape=[jax.ShapeDtypeStruct((n_chunk, fc, d), F32)],
        args=(act_b, dz2b))


def _proj_out_bwd(dz1, cat_t, w_out_b, comms=()):
    t_tok, d = dz1.shape
    d_mix = cat_t.shape[0]
    tm = min(512, t_tok)

    def body(dz1_ref, cat_ref, w_ref, dcat_ref, gw_ref):
        @pl.when(pl.program_id(0) == 0)
        def _():
            gw_ref[...] = jnp.zeros_like(gw_ref)

        dzb = dz1_ref[...].astype(BF16)
        dcat_ref[...] = _dot(w_ref[...], dzb, NT)
        gw_ref[...] += _dot(cat_ref[...], dzb)

    return _carry(
        body, name="proj_out_bwd", grid=(t_tok // tm,), comms=comms,
        in_specs=[pl.BlockSpec((tm, d), lambda i: (i, 0)), pl.BlockSpec((d_mix, tm), lambda i: (0, i)),
                  pl.BlockSpec((d_mix, d), lambda i: (0, 0))],
        out_specs=[pl.BlockSpec((d_mix, tm), lambda i: (0, i)), pl.BlockSpec((d_mix, d), lambda i: (0, 0))],
        out_shape=[jax.ShapeDtypeStruct((d_mix, t_tok), F32), jax.ShapeDtypeStruct((d_mix, d), F32)],
        args=(dz1, cat_t, w_out_b))


def _mixer_bwd(dcat_t, h_t, cos_t, sin_t, w_spatial, b_spatial, vln_g, vln_b, sinks, band_bias, comms=()):
    t_tok = h_t.shape[1]
    nb = t_tok // BLK
    group = N_HEADS // N_KV_HEADS

    def body(sinks_ref, dcat_ref, u_ref, vg_ref, q_ref, kvc_ref, kvp_ref, cosc_ref, sinc_ref, cosp_ref, sinp_ref,
             wsp_ref, bsp_ref, g_ref, b_ref, bias_ref, dh_ref, dkvp_ref, gws_ref, gbsp_ref, gvln_ref, gsink_ref,
             dg_acc, db_acc, wm_scr):
        i = pl.program_id(0)

        @pl.when(i == 0)
        def _():
            gws_ref[...] = jnp.zeros_like(gws_ref)
            gbsp_ref[...] = jnp.zeros_like(gbsp_ref)
            gsink_ref[...] = jnp.zeros_like(gsink_ref)
            dg_acc[...] = jnp.zeros_like(dg_acc)
            db_acc[...] = jnp.zeros_like(db_acc)

        _mask_w_once(wsp_ref, wm_scr)

        g = g_ref[...]
        ua, ua_grad = _gelu_and_grad(u_ref[...])
        vv, vv_grad = _gelu_and_grad(vg_ref[...])
        vp, vhat, rstd = _ln_fwd_t(vv, g, b_ref[...])
        vpb = vp.astype(BF16)
        da = dcat_ref[0:D_GMLP, :]
        dmixed = da * ua
        dvp_parts = []
        for hh in range(N_HEADS):
            rows = slice(hh * HEAD_DIM, (hh + 1) * HEAD_DIM)
            mixed = _dot(vpb[rows], wm_scr[hh], NT) + bsp_ref[hh:hh + 1, :]
            dh_ref[COL_U + hh * HEAD_DIM:COL_U + (hh + 1) * HEAD_DIM, :] = da[rows] * mixed * ua_grad[rows]
            dm = dmixed[rows]
            dmb = dm.astype(BF16)
            gbsp_ref[hh:hh + 1, :] += jnp.sum(dm, axis=0, keepdims=True)
            gws_ref[hh] += _dot(dmb, vpb[rows], TN)
            dvp_parts.append(_dot(dmb, wm_scr[hh]))
        dvp = jnp.concatenate(dvp_parts, axis=0)
        dg_acc[...] += dvp * vhat
        db_acc[...] += dvp
        dh_ref[COL_V:COL_V + D_GMLP, :] = _ln_bwd_t(dvp, vhat, rstd, g) * vv_grad

        cosc, sinc, cosp, sinp = cosc_ref[...], sinc_ref[...], cosp_ref[...], sinp_ref[...]
        qr = (_rope_t(q_ref[...], cosc, sinc) * SCORE_SCALE).astype(BF16)
        k_t, k_n, v_t = _keys_values(kvc_ref[...], kvp_ref[...], cosc, sinc, cosp, sinp)
        v_n = jnp.concatenate([kvp_ref[D_KV:, :].T, kvc_ref[D_KV:, :].T], axis=0).astype(BF16)
        bias = _group_lanes([bias_ref[...]] * group)
        dk, dv, dq_parts = [], [], []
        for kv in range(N_KV_HEADS):
            heads = range(kv * group, (kv + 1) * group)
            kv_rows = slice(kv * HEAD_DIM, (kv + 1) * HEAD_DIM)
            qs = _group_lanes([qr[hh * HEAD_DIM:(hh + 1) * HEAD_DIM] for hh in heads])
            dos = _group_lanes([dcat_ref[D_GMLP + hh * HEAD_DIM:D_GMLP + (hh + 1) * HEAD_DIM, :] for hh in heads]).astype(BF16)
            sink = _group_lanes([jnp.full((1, BLK), sinks_ref[hh], F32) for hh in heads])
            p, p_sink = _softmax_sink_t(_dot(k_n, _pad_head(qs, kv)) + bias, sink)
            dp = _dot(v_n, _pad_head(dos, kv))
            delta = jnp.sum(p * dp, axis=0, keepdims=True)
            ds = (p * (dp - delta)).astype(BF16)
            dsink = p_sink * delta
            dq = _dot(k_t[kv_rows], ds) * SCORE_SCALE
            for j, hh in enumerate(heads):
                gsink_ref[hh:hh + 1, :] -= dsink[:, j * BLK:(j + 1) * BLK]
                dq_parts.append(dq[:, j * BLK:(j + 1) * BLK])
            dk.append(_dot(qs, ds, NT))
            dv.append(_dot(dos, p.astype(BF16), NT))
        dh_ref[COL_Q:COL_Q + D_ATTN, :] = _rope_t(jnp.concatenate(dq_parts, axis=0), cosc, sinc, bwd=True)
        dk_all = jnp.concatenate(dk, axis=0)
        dv_all = jnp.concatenate(dv, axis=0)
        dh_ref[COL_K:COL_K + D_KV, :] = _rope_t(dk_all[:, BLK:], cosc, sinc, bwd=True)
        dh_ref[COL_K + D_KV:COL_K + 2 * D_KV, :] = dv_all[:, BLK:]
        dkvp_ref[0:D_KV, :] = _rope_t(dk_all[:, :BLK], cosp, sinp, bwd=True)
        dkvp_ref[D_KV:2 * D_KV, :] = dv_all[:, :BLK]

        @pl.when(i == nb - 1)
        def _():
            causal = _causal()
            for hh in range(N_HEADS):
                gws_ref[hh] = jnp.where(causal, gws_ref[hh], 0.0)
            gvln_ref[...] = jnp.zeros_like(gvln_ref)
            gvln_ref[0:1, :] = jnp.sum(dg_acc[...].T, axis=0, keepdims=True)
            gvln_ref[1:2, :] = jnp.sum(db_acc[...].T, axis=0, keepdims=True)

    full = lambda shape: pl.BlockSpec(shape, lambda i: (0,) * len(shape))
    return _carry(
        body, name="mixer_bwd", grid=(nb,), comms=comms,
        in_specs=[pl.BlockSpec(memory_space=pltpu.SMEM), pl.BlockSpec((D_GMLP + D_ATTN, BLK), lambda i: (0, i))]
        + _h_specs() + _table_specs()
        + [full((N_HEADS, BLK, BLK)), full((N_HEADS, BLK)), full((D_GMLP, 1)), full((D_GMLP, 1)), BIAS_SPEC],
        out_specs=[pl.BlockSpec((D_IN, BLK), lambda i: (0, i)),
                   pl.BlockSpec((2 * D_KV, BLK), lambda i: (0, (i + nb - 1) % nb)),
                   full((N_HEADS, BLK, BLK)), full((N_HEADS, BLK)), full((8, D_GMLP)), full((N_HEADS, LANES))],
        out_shape=[jax.ShapeDtypeStruct((D_IN, t_tok), F32), jax.ShapeDtypeStruct((2 * D_KV, t_tok), F32),
                   jax.ShapeDtypeStruct((N_HEADS, BLK, BLK), F32), jax.ShapeDtypeStruct((N_HEADS, BLK), F32),
                   jax.ShapeDtypeStruct((8, D_GMLP), F32), jax.ShapeDtypeStruct((N_HEADS, LANES), F32)],
        scratch_shapes=[pltpu.VMEM((D_GMLP, BLK), F32), pltpu.VMEM((D_GMLP, BLK), F32), pltpu.VMEM((N_HEADS, BLK, BLK), BF16)],
        args=(sinks, dcat_t, h_t, h_t, h_t, h_t, h_t, cos_t, sin_t, cos_t, sin_t, w_spatial, b_spatial, vln_g, vln_b, band_bias))


def _proj_in_dgrad(dh_t, dkvp_t, dz1, w_in_t):
    t_tok, d = dz1.shape
    d_in = dh_t.shape[0]
    tm = min(512, t_tok)

    def body(dh_ref, dkvp_ref, dz1_ref, w_ref, dx_ref, dhb_ref):
        dhb = jnp.concatenate([dh_ref[0:COL_K, :], dh_ref[COL_K:, :] + dkvp_ref[...]], axis=0).astype(BF16)
        dhb_ref[...] = dhb
        dx_ref[...] = ALPHA * dz1_ref[...] + _dot(dhb, w_ref[...], TN)

    return _carry(
        body, name="proj_in_dgrad", grid=(t_tok // tm,),
        in_specs=[pl.BlockSpec((d_in, tm), lambda i: (0, i)), pl.BlockSpec((2 * D_KV, tm), lambda i: (0, i)),
                  pl.BlockSpec((tm, d), lambda i: (i, 0)), pl.BlockSpec((d_in, d), lambda i: (0, 0))],
        out_specs=[pl.BlockSpec((tm, d), lambda i: (i, 0)), pl.BlockSpec((d_in, tm), lambda i: (0, i))],
        out_shape=[jax.ShapeDtypeStruct((t_tok, d), F32), jax.ShapeDtypeStruct((d_in, t_tok), BF16)],
        args=(dh_t, dkvp_t, dz1, w_in_t))[0]


def _proj_in_wgrad(dhb_t, xb, comms=()):
    t_tok, d = xb.shape
    d_in = dhb_t.shape[0]
    tm = min(1024, t_tok)

    def body(dhb_ref, xb_ref, gw_ref):
        @pl.when(pl.program_id(0) == 0)
        def _():
            gw_ref[...] = jnp.zeros_like(gw_ref)

        gw_ref[...] += _dot(dhb_ref[...], xb_ref[...])

    return _carry(
        body, name="proj_in_wgrad", grid=(t_tok // tm,), comms=comms,
        in_specs=[pl.BlockSpec((d_in, tm), lambda i: (0, i)), pl.BlockSpec((tm, d), lambda i: (i, 0))],
        out_specs=[pl.BlockSpec((d_in, d), lambda i: (0, 0))],
        out_shape=[jax.ShapeDtypeStruct((d_in, d), F32)],
        args=(dhb_t, xb))


def _adamw(w, g, m, v):
    m = ADAM_B1 * m + (1.0 - ADAM_B1) * g
    v = ADAM_B2 * v + (1.0 - ADAM_B2) * (g * g)
    m_hat = m / (1.0 - ADAM_B1 ** ADAM_STEP)
    v_hat = v / (1.0 - ADAM_B2 ** ADAM_STEP)
    delta = -ADAM_LR * (m_hat / (jnp.sqrt(v_hat) + ADAM_EPS) + ADAM_WD * w)
    return delta, m, v


def _sum4(own_ref, recv_ref):
    return ((own_ref[...] + recv_ref[0].astype(F32)) + recv_ref[1].astype(F32)) + recv_ref[2].astype(F32)


def _adamw_shard(name, own, recv3, w, m, v, comms=()):
    r, c = w.shape
    tr = r if r <= 512 else 512
    blk = pl.BlockSpec((tr, c), lambda i: (i, 0))
    recv = [] if recv3 is None else [recv3]

    def body(own_ref, *refs):
        w_ref, m_ref, v_ref, g_out, d_out, m_out, v_out = refs[len(recv):]
        g = _sum4(own_ref, refs[0]) if recv else own_ref[...]
        delta, m_new, v_new = _adamw(w_ref[...], g, m_ref[...], v_ref[...])
        g_out[...] = g
        d_out[...] = delta
        m_out[...] = m_new
        v_out[...] = v_new

    return _carry(
        body, name=name, grid=(r // tr,), comms=comms,
        in_specs=[blk] + [pl.BlockSpec((3, tr, c), lambda i: (0, i, 0))] * len(recv) + [blk, blk, blk],
        out_specs=[blk] * 4, out_shape=[jax.ShapeDtypeStruct((r, c), F32)] * 4,
        args=(own, *recv, w, m, v))


def _sum_partials(name, own, recv3):
    r, c = own.shape

    def body(own_ref, recv_ref, g_out):
        g_out[...] = _sum4(own_ref, recv_ref)

    return _carry(
        body, name=name, grid=(1,),
        in_specs=[pl.BlockSpec((r, c), lambda i: (0, 0)), pl.BlockSpec((3, r, c), lambda i: (0, 0, 0))],
        out_specs=[pl.BlockSpec((r, c), lambda i: (0, 0))], out_shape=[jax.ShapeDtypeStruct((r, c), F32)],
        args=(own, recv3))[0][0]


VEC_VLN, VEC_LN1G, VEC_LN1B, VEC_LN2G, VEC_LN2B, VEC_SINK, VEC_LOSS, VEC_BSP, VEC_ROWS = 0, 1, 2, 3, 4, 5, 6, 8, 16


def _adamw_small(parts_w, parts_vec, params):
    n = parts_w.shape[0]
    flat = [a for p in params for a in p]
    shapes = [p[0].shape for p in params]

    def grads(gw, gv):
        return [gw, gv[VEC_VLN:VEC_VLN + 1, 0:D_GMLP], gv[VEC_VLN:VEC_VLN + 1, D_GMLP:2 * D_GMLP],
                gv[VEC_BSP:VEC_BSP + N_HEADS, 0:BLK], gv[VEC_LN1G:VEC_LN1G + 1], gv[VEC_LN1B:VEC_LN1B + 1],
                gv[VEC_LN2G:VEC_LN2G + 1], gv[VEC_LN2B:VEC_LN2B + 1], gv[VEC_SINK:VEC_SINK + 1, 0:N_HEADS]]

    def body(pw_ref, pv_ref, *refs):
        ins, outs = refs[:len(flat)], refs[len(flat):]
        gw, gv = pw_ref[0], pv_ref[0]
        for k in range(1, n):
            gw, gv = gw + pw_ref[k], gv + pv_ref[k]
        for i, g in enumerate(grads(gw, gv)):
            w_ref, m_ref, v_ref = ins[3 * i:3 * i + 3]
            delta, m_new, v_new = _adamw(w_ref[...], g, m_ref[...], v_ref[...])
            for o_ref, val in zip(outs[4 * i:4 * i + 4], (g, delta, m_new, v_new)):
                o_ref[...] = val
        outs[-1][...] = gv[VEC_LOSS:VEC_LOSS + 1, 0:LANES]

    whole = lambda shape: pl.BlockSpec(shape, lambda i: (0,) * len(shape))
    res = _carry(
        body, name="adamw_small", grid=(1,),
        in_specs=[whole(parts_w.shape), whole(parts_vec.shape)] + [whole(a.shape) for a in flat],
        out_specs=[whole(s) for s in shapes for _ in range(4)] + [whole((1, LANES))],
        out_shape=[jax.ShapeDtypeStruct(s, F32) for s in shapes for _ in range(4)] + [jax.ShapeDtypeStruct((1, LANES), F32)],
        args=(parts_w, parts_vec, *flat))[0]
    return [res[4 * i:4 * i + 4] for i in range(len(params))], res[-1]


def _pair_sum(name, parts, recv, core_chip):
    _, r, c = parts.shape
    tr = r if r <= 512 else 512

    def body(cc_ref, a_ref, b_ref, wire_ref, own_ref):
        s = a_ref[...] + b_ref[...]
        wire_ref[...] = s.astype(BF16)

        @pl.when(pl.program_id(1) == cc_ref[1])
        def _():
            own_ref[...] = s

    return _carry(
        body, name=name, grid=(r // tr, 4), prefetch=(core_chip,),
        in_specs=[pl.BlockSpec((None, tr, c), lambda i, q, cc: (2 * q + cc[0], i, 0)),
                  pl.BlockSpec((None, tr, c), lambda i, q, cc: (q, i, 0))],
        out_specs=[pl.BlockSpec((None, tr, c), lambda i, q, cc: (q, i, 0)), pl.BlockSpec((tr, c), lambda i, q, cc: (i, 0))],
        out_shape=[jax.ShapeDtypeStruct((4, r, c), BF16), jax.ShapeDtypeStruct((r, c), F32)],
        args=(parts, recv))[0]


def kernel(x, positions, w_in, v_ln_g, v_ln_b, w_spatial, b_spatial, sinks, w_out, ln1_g, ln1_b, w_ff1, w_ff2, ln2_g, ln2_b, loss_target, m_w_in, m_v_ln_g, m_v_ln_b, m_w_spatial, m_b_spatial, m_sinks, m_w_out, m_ln1_g, m_ln1_b, m_w_ff1, m_w_ff2, m_ln2_g, m_ln2_b, v_w_in, v_v_ln_g, v_v_ln_b, v_w_spatial, v_b_spatial, v_sinks, v_w_out, v_ln1_g, v_ln1_b, v_w_ff1, v_w_ff2, v_ln2_g, v_ln2_b):
    _, t_tok, d = x.shape
    xi, yi, ci = _place()
    core_chip = jnp.stack([ci, 2 * xi + yi]).astype(jnp.int32)
    x2 = x.reshape(t_tok, d)
    target = loss_target.reshape(t_tok, d)
    inv_freq = ROPE_THETA ** (-jnp.arange(0, HEAD_DIM, 2, dtype=F32) / HEAD_DIM)
    wsp, bsp, sink_vec = w_spatial[0], b_spatial[0], sinks[0]
    vg_col, vb_col = v_ln_g.reshape(D_GMLP, 1), v_ln_b.reshape(D_GMLP, 1)
    big = {"in": w_in[0], "out": w_out[0], "ff1": w_ff1[0], "ff2": w_ff2[0]}
    big_b = {k: w.astype(BF16) for k, w in big.items()}

    (cos_t, sin_t), ((g_in,),) = _rope_tables(
        positions, jnp.tile(inv_freq, 2).reshape(HEAD_DIM, 1), comms=[_gather_comm([big["in"].T.astype(BF16)])])
    w_in_t = g_in.reshape(D_IN, d)
    (h_t, xb), ((g_out, w1_b),) = _proj_in(x2, w_in_t, comms=[_gather_comm([big_b["out"], big_b["ff1"]])])
    w_out_b = g_out.reshape(-1, d)
    band_bias = _band_bias()
    (cat_t,), ((w2_b,),) = _mixer_fwd(h_t, cos_t, sin_t, wsp, bsp, vg_col, vb_col, sink_vec, band_bias,
                                      comms=[_gather_comm([big_b["ff2"]])])
    z1 = _proj_out(cat_t, x2, w_out_b)
    act_b, dpre_b, x1b, dz2b, dz1, stats = _ffn_fwd_bwd(z1, target, w1_b, w2_b, ln1_g, ln1_b, ln2_g, ln2_b)

    (p_ff1,), _ = _ffn_wgrad1(x1b, dpre_b, N_DEV)
    (p_ff2,), ((s_ff1,),) = _ffn_wgrad2(act_b, dz2b, N_DEV, comms=[_sibling_comm([p_ff1])])
    wire_ff1, own_ff1 = _pair_sum("pair_sum_ff1", p_ff1, s_ff1, core_chip)
    (dcat_t, gw_out), ((s_ff2,),) = _proj_out_bwd(dz1, cat_t, w_out_b, comms=[_sibling_comm([p_ff2])])
    wire_ff2, own_ff2 = _pair_sum("pair_sum_ff2", p_ff2, s_ff2, core_chip)
    p_out = gw_out.reshape(N_DEV, -1, d)
    (dh_t, dkvp_t, g_wsp, g_bsp, g_vln, g_sink), ((r_ff1, r_ff2), (s_out,)) = _mixer_bwd(
        dcat_t, h_t, cos_t, sin_t, wsp, bsp, vg_col, vb_col, sink_vec, band_bias,
        comms=[_chips_comm([wire_ff1, wire_ff2]), _sibling_comm([p_out])])
    wire_out, own_out = _pair_sum("pair_sum_out", p_out, s_out, core_chip)
    grad_x, dhb_t = _proj_in_dgrad(dh_t, dkvp_t, dz1, w_in_t)
    sink_row = jnp.pad(g_sink.sum(axis=1).reshape(1, N_HEADS), ((0, 0), (0, d - N_HEADS)))
    small_vec = jnp.concatenate([g_vln[0:2].reshape(1, d), stats[0:4], sink_row, stats[4:5], jnp.zeros((1, d), F32),
                                 jnp.pad(g_bsp, ((0, 0), (0, d - BLK)))], axis=0)
    (gw_in_t,), ((parts_w, parts_vec), (r_out,)) = _proj_in_wgrad(
        dhb_t, xb, comms=[_gather_comm([g_wsp.reshape(-1, BLK), small_vec]), _chips_comm([wire_out])])
    p_in = gw_in_t.reshape(N_DEV, -1, d)

    out_out, ((s_in,),) = _adamw_shard("adamw_out", own_out, r_out, big["out"], m_w_out[0], v_w_out[0], comms=[_sibling_comm([p_in])])
    wire_in, own_in = _pair_sum("pair_sum_in", p_in, s_in, core_chip)
    ff1_out, ((r_in,),) = _adamw_shard("adamw_ff1", own_ff1, r_ff1, big["ff1"], m_w_ff1[0], v_w_ff1[0], comms=[_chips_comm([wire_in])])
    ff2_out, _ = _adamw_shard("adamw_ff2", own_ff2, r_ff2, big["ff2"], m_w_ff2[0], v_w_ff2[0])
    g_in_t = _sum_partials("sum_in", own_in, r_in)
    in_out, _ = _adamw_shard("adamw_in", g_in_t.T, None, big["in"], m_w_in[0], v_w_in[0])
    small = [(w_spatial, m_w_spatial, v_w_spatial), (v_ln_g, m_v_ln_g, v_v_ln_g), (v_ln_b, m_v_ln_b, v_v_ln_b),
             (b_spatial, m_b_spatial, v_b_spatial), (ln1_g, m_ln1_g, v_ln1_g), (ln1_b, m_ln1_b, v_ln1_b),
             (ln2_g, m_ln2_g, v_ln2_g), (ln2_b, m_ln2_b, v_ln2_b), (sinks, m_sinks, v_sinks)]
    views = [(-1, BLK), None, None, (N_HEADS, BLK)] + [None] * 5
    small_res, loss_row = _adamw_small(parts_w, parts_vec, [
        tuple(a if vw is None else a.reshape(vw) for a in p) for p, vw in zip(small, views)])
    small_out = [[o.reshape(p[0].shape) for o in res] for res, p in zip(small_res, small)]
    loss = loss_row[0, 0]

    big_out = {0: in_out, 6: out_out, 9: ff1_out, 10: ff2_out}
    small_slot = {3: 0, 1: 1, 2: 2, 4: 3, 7: 4, 8: 5, 11: 6, 12: 7, 5: 8}
    outs = [loss, grad_x.reshape(x.shape)]
    for kind in range(4):
        for wi in range(13):
            outs.append(big_out[wi][kind][None] if wi in big_out else small_out[small_slot[wi]][kind])
    return tuple(outs)
```

```python
import functools
import math

import jax
import jax.numpy as jnp
from jax import lax
from jax.experimental import pallas as pl
from jax.experimental.pallas import tpu as pltpu

F32 = jnp.float32
BF16 = jnp.bfloat16
MESH = pl.DeviceIdType.MESH

HEAD_DIM = 64
N_HEADS = 8
N_KV_HEADS = 2
BLK = 128
D_GMLP = N_HEADS * HEAD_DIM
D_ATTN = N_HEADS * HEAD_DIM
D_KV = N_KV_HEADS * HEAD_DIM
D_IN = 2 * D_GMLP + D_ATTN + 2 * D_KV
COL_U, COL_V, COL_Q, COL_K = 0, D_GMLP, 2 * D_GMLP, 2 * D_GMLP + D_ATTN
ROPE_THETA = 10000.0
LN_EPS = 1e-5
ALPHA = 2.0 ** 0.25
NEG_INF = -1e30
SCORE_SCALE = 1.0 / math.sqrt(HEAD_DIM)
ADAM_LR, ADAM_B1, ADAM_B2, ADAM_EPS, ADAM_WD, ADAM_STEP = 0.001, 0.9, 0.999, 1e-08, 0.01, 10
N_DEV = 8
LANES = 128
VMEM_LIMIT = 56 * 1024 * 1024

NT = (((1,), (1,)), ((), ()))
TN = (((0,), (0,)), ((), ()))


def _params(*sem):
    return pltpu.CompilerParams(dimension_semantics=sem, vmem_limit_bytes=VMEM_LIMIT)


def _dot(a, b, dims=None):
    if dims is None:
        return jnp.dot(a, b, preferred_element_type=F32)
    return lax.dot_general(a, b, dims, preferred_element_type=F32)


def _mean(a):
    return jnp.mean(a, axis=-1, keepdims=True)


def _ln_fwd(z, g, b):
    zc = z - _mean(z)
    rstd = lax.rsqrt(_mean(zc * zc) + LN_EPS)
    xhat = zc * rstd
    return xhat * g + b, xhat, rstd


def _ln_bwd(dy, xhat, rstd, g):
    dxhat = dy * g
    return rstd * (dxhat - _mean(dxhat) - xhat * _mean(dxhat * xhat))


_GELU_C = math.sqrt(2.0 / math.pi)


def _gelu(x):
    t = jnp.tanh(_GELU_C * (x + 0.044715 * (x * x * x)))
    return 0.5 * x * (1.0 + t)


def _gelu_and_grad(x):
    x2 = x * x
    t = jnp.tanh(_GELU_C * (x + 0.044715 * (x2 * x)))
    hx, ht = 0.5 * x, 0.5 * (1.0 + t)
    return x * ht, ht + hx * (1.0 - t * t) * (_GELU_C * (1.0 + 3.0 * 0.044715 * x2))


def _mean0(a):
    return jnp.mean(a, axis=0, keepdims=True)


def _ln_fwd_t(z, g, b):
    zc = z - _mean0(z)
    rstd = lax.rsqrt(_mean0(zc * zc) + LN_EPS)
    xhat = zc * rstd
    return xhat * g + b, xhat, rstd


def _ln_bwd_t(dy, xhat, rstd, g):
    dxhat = dy * g
    return rstd * (dxhat - _mean0(dxhat) - xhat * _mean0(dxhat * xhat))


def _rope_t(t, cos, sin_signed, bwd=False):
    half = HEAD_DIM // 2
    outs = []
    for r in range(0, t.shape[0], HEAD_DIM):
        th = t[r:r + HEAD_DIM]
        sw = jnp.concatenate([th[half:], th[:half]], axis=0) * sin_signed
        outs.append(th * cos - sw if bwd else th * cos + sw)
    return jnp.concatenate(outs, axis=0)


ANY = pl.BlockSpec(memory_space=pl.ANY)


def _place():
    return lax.axis_index("x"), lax.axis_index("y"), lax.axis_index("c")


class _Comm:
    def __init__(self, ins, outs, sems, start, finish):
        self.ins, self.outs, self.sems, self.start, self.finish = ins, outs, sems, start, finish


def _gather_comm(arrs):
    n = len(arrs)

    def parts(ins, outs, sems):
        send_sems, recv_sems, local_sems = sems
        x, y, c = _place()
        me, sibling = (x, y, c), (x, y, 1 - c)
        chips = [(1 - x, y), (x, 1 - y), (1 - x, 1 - y)]

        def copy(a, k, block, to, src=None):
            px, py, pc = block
            dst = outs[a].at[4 * px + 2 * py + pc]
            return pltpu.make_async_remote_copy(
                src_ref=dst if src is None else src, dst_ref=dst,
                send_sem=send_sems.at[a, k], recv_sem=recv_sems.at[a, k], device_id=to, device_id_type=MESH)

        mine = [pltpu.make_async_copy(ins[a], outs[a].at[4 * x + 2 * y + c], local_sems.at[a]) for a in range(n)]
        first = []
        for a in range(n):
            first.append(copy(a, 0, me, sibling, src=ins[a]))
            first += [copy(a, 1 + j, me, (*chip, c), src=ins[a]) for j, chip in enumerate(chips)]
        return copy, mine, first, me, sibling, chips, c

    def start(ins, outs, sems):
        _, mine, first, *_ = parts(ins, outs, sems)
        for cp in mine + first:
            cp.start()

    def finish(ins, outs, sems):
        copy, mine, first, me, sibling, chips, c = parts(ins, outs, sems)
        passed = []
        for j, chip in enumerate(chips):
            for a in range(n):
                copy(a, 1 + j, (*chip, c), me).wait_recv()
                fwd = copy(a, 4 + j, (*chip, c), sibling)
                fwd.start()
                passed.append(fwd)
        for a in range(n):
            copy(a, 0, sibling, me).wait_recv()
        for j, chip in enumerate(chips):
            for a in range(n):
                copy(a, 4 + j, (*chip, 1 - c), me).wait_recv()
        for cp in first + passed:
            cp.wait_send()
        for cp in mine:
            cp.wait()

    return _Comm(list(arrs), [jax.ShapeDtypeStruct((N_DEV,) + a.shape, a.dtype) for a in arrs],
                 [pltpu.SemaphoreType.DMA((n, 7)), pltpu.SemaphoreType.DMA((n, 7)), pltpu.SemaphoreType.DMA((n,))],
                 start, finish)


def _sibling_comm(parts):
    n = len(parts)

    def copies(ins, outs, sems):
        x, y, c = _place()
        return [pltpu.make_async_remote_copy(
            src_ref=ins[a].at[2 * q + (1 - c)], dst_ref=outs[a].at[q],
            send_sem=sems[0].at[a, q], recv_sem=sems[1].at[a, q],
            device_id=(x, y, 1 - c), device_id_type=MESH) for a in range(n) for q in range(4)]

    return _Comm(list(parts), [jax.ShapeDtypeStruct((4,) + p.shape[1:], p.dtype) for p in parts],
                 [pltpu.SemaphoreType.DMA((n, 4)), pltpu.SemaphoreType.DMA((n, 4))],
                 lambda *r: [cp.start() for cp in copies(*r)], lambda *r: [cp.wait() for cp in copies(*r)])


def _chips_comm(chip_parts, rows=None):
    n = len(chip_parts)
    r0, nr = (0, None) if rows is None else rows

    def copies(ins, outs, sems):
        x, y, c = _place()
        chips = [(1 - x, y), (x, 1 - y), (1 - x, 1 - y)]
        src = lambda a, q: ins[a].at[q] if rows is None else ins[a].at[q, pl.ds(r0, nr)]
        return [pltpu.make_async_remote_copy(
            src_ref=src(a, 2 * px + py), dst_ref=outs[a].at[k],
            send_sem=sems[0].at[a, k], recv_sem=sems[1].at[a, k],
            device_id=(px, py, c), device_id_type=MESH) for a in range(n) for k, (px, py) in enumerate(chips)]

    shape = lambda p: (3,) + p.shape[1:] if rows is None else (3, nr) + p.shape[2:]
    return _Comm(list(chip_parts), [jax.ShapeDtypeStruct(shape(p), p.dtype) for p in chip_parts],
                 [pltpu.SemaphoreType.DMA((n, 3)), pltpu.SemaphoreType.DMA((n, 3))],
                 lambda *r: [cp.start() for cp in copies(*r)], lambda *r: [cp.wait() for cp in copies(*r)])


def _carry(body, *, name, grid, in_specs, out_specs, out_shape, args, comms=(), scratch_shapes=(), prefetch=()):
    n_pre, n_in, n_out, n_scr = len(prefetch), len(in_specs), len(out_specs), len(scratch_shapes)
    c_ins = [a for cm in comms for a in cm.ins]
    c_outs = [s for cm in comms for s in cm.outs]
    c_sems = [s for cm in comms for s in cm.sems]

    def wrapped(*refs):
        pre, refs = refs[:n_pre], refs[n_pre:]
        ins, refs = refs[:n_in], refs[n_in:]
        cins, refs = refs[:len(c_ins)], refs[len(c_ins):]
        outs, refs = refs[:n_out], refs[n_out:]
        couts, refs = refs[:len(c_outs)], refs[len(c_outs):]
        scr, sems = refs[:n_scr], refs[n_scr:]
        groups, i0, o0, s0 = [], 0, 0, 0
        for cm in comms:
            groups.append((cm, cins[i0:i0 + len(cm.ins)], couts[o0:o0 + len(cm.outs)], sems[s0:s0 + len(cm.sems)]))
            i0, o0, s0 = i0 + len(cm.ins), o0 + len(cm.outs), s0 + len(cm.sems)
        first = pl.program_id(0) == 0
        last = pl.program_id(0) == grid[0] - 1
        for ax in range(1, len(grid)):
            first = first & (pl.program_id(ax) == 0)
            last = last & (pl.program_id(ax) == grid[ax] - 1)
        if comms:
            @pl.when(first)
            def _():
                for cm, ci, co, cs in groups:
                    cm.start(ci, co, cs)
        body(*pre, *ins, *outs, *scr)
        if comms:
            @pl.when(last)
            def _():
                for cm, ci, co, cs in groups:
                    cm.finish(ci, co, cs)

    grid_spec = pltpu.PrefetchScalarGridSpec(
        num_scalar_prefetch=n_pre, grid=grid,
        in_specs=list(in_specs) + [ANY] * len(c_ins), out_specs=list(out_specs) + [ANY] * len(c_outs),
        scratch_shapes=list(scratch_shapes) + c_sems)
    res = pl.pallas_call(
        wrapped, name=name, grid_spec=grid_spec, out_shape=list(out_shape) + c_outs,
        compiler_params=_params(*(["arbitrary"] * len(grid))),
    )(*prefetch, *args, *c_ins)
    outs, rest, per_comm = res[:n_out], res[n_out:], []
    for cm in comms:
        per_comm.append(rest[:len(cm.outs)])
        rest = rest[len(cm.outs):]
    return outs, per_comm


def _rope_tables(pos_row, inv_freq_col, comms=()):
    t_tok = pos_row.shape[1]
    tm = min(512, t_tok)

    def body(pos_ref, invf_ref, cos_ref, sin_ref):
        ang = pos_ref[...].astype(F32) * invf_ref[...]
        row = lax.broadcasted_iota(jnp.int32, ang.shape, 0)
        cos_ref[...] = jnp.cos(ang)
        sin_ref[...] = jnp.sin(ang) * jnp.where(row < HEAD_DIM // 2, -1.0, 1.0)

    return _carry(
        body, name="rope_tables", grid=(t_tok // tm,), comms=comms,
        in_specs=[pl.BlockSpec((1, tm), lambda i: (0, i)), pl.BlockSpec((HEAD_DIM, 1), lambda i: (0, 0))],
        out_specs=[pl.BlockSpec((HEAD_DIM, tm), lambda i: (0, i))] * 2,
        out_shape=[jax.ShapeDtypeStruct((HEAD_DIM, t_tok), F32)] * 2,
        args=(pos_row, inv_freq_col))


def _proj_in(x2, w_in_t, comms=()):
    t_tok, d = x2.shape
    d_in = w_in_t.shape[0]
    tm = min(512, t_tok)

    def body(x_ref, w_ref, h_ref, xb_ref):
        xb = x_ref[...].astype(BF16)
        xb_ref[...] = xb
        h_ref[...] = _dot(w_ref[...], xb, NT)

    return _carry(
        body, name="proj_in", grid=(t_tok // tm,), comms=comms,
        in_specs=[pl.BlockSpec((tm, d), lambda i: (i, 0)), pl.BlockSpec((d_in, d), lambda i: (0, 0))],
        out_specs=[pl.BlockSpec((d_in, tm), lambda i: (0, i)), pl.BlockSpec((tm, d), lambda i: (i, 0))],
        out_shape=[jax.ShapeDtypeStruct((d_in, t_tok), F32), jax.ShapeDtypeStruct((t_tok, d), BF16)],
        args=(x2, w_in_t))


def _h_specs():
    kv_row = COL_K // (2 * D_KV)
    return [
        pl.BlockSpec((D_GMLP, BLK), lambda i: (0, i)),
        pl.BlockSpec((D_GMLP, BLK), lambda i: (1, i)),
        pl.BlockSpec((D_ATTN, BLK), lambda i: (2, i)),
        pl.BlockSpec((2 * D_KV, BLK), lambda i: (kv_row, i)),
        pl.BlockSpec((2 * D_KV, BLK), lambda i: (kv_row, jnp.maximum(i - 1, 0))),
    ]


def _table_specs():
    return [
        pl.BlockSpec((HEAD_DIM, BLK), lambda i: (0, i)),
        pl.BlockSpec((HEAD_DIM, BLK), lambda i: (0, i)),
        pl.BlockSpec((HEAD_DIM, BLK), lambda i: (0, jnp.maximum(i - 1, 0))),
        pl.BlockSpec((HEAD_DIM, BLK), lambda i: (0, jnp.maximum(i - 1, 0))),
    ]


def _band_bias():
    ki = lax.broadcasted_iota(jnp.int32, (2, 2 * BLK, BLK), 1)
    qi = lax.broadcasted_iota(jnp.int32, (2, 2 * BLK, BLK), 2)
    later = lax.broadcasted_iota(jnp.int32, (2, 2 * BLK, BLK), 0) > 0
    dist = qi + BLK - ki
    return jnp.where((dist >= 0) & (dist < BLK) & ((ki >= BLK) | later), 0.0, NEG_INF).astype(F32)


BIAS_SPEC = pl.BlockSpec((None, 2 * BLK, BLK), lambda i: (jnp.minimum(i, 1), 0, 0))


def _keys_values(kvc, kvp, cosc, sinc, cosp, sinp):
    kp, kc = _rope_t(kvp[:D_KV], cosp, sinp), _rope_t(kvc[:D_KV], cosc, sinc)
    k_t = jnp.concatenate([kp, kc], axis=1).astype(BF16)
    k_n = jnp.concatenate([kp.T, kc.T], axis=0).astype(BF16)
    v_t = jnp.concatenate([kvp[D_KV:], kvc[D_KV:]], axis=1).astype(BF16)
    return k_t, k_n, v_t


def _pad_head(th, kv):
    z = jnp.zeros_like(th)
    return jnp.concatenate([th, z] if kv == 0 else [z, th], axis=0)


def _group_lanes(parts):
    return jnp.concatenate(parts, axis=1)


def _softmax_sink_t(s, sink):
    m = jnp.maximum(jnp.max(s, axis=0, keepdims=True), sink)
    e = jnp.exp(s - m)
    es = jnp.exp(sink - m)
    r = 1.0 / (jnp.sum(e, axis=0, keepdims=True) + es)
    return e * r, es * r


def _causal():
    row = lax.broadcasted_iota(jnp.int32, (BLK, BLK), 0)
    col = lax.broadcasted_iota(jnp.int32, (BLK, BLK), 1)
    return row >= col


def _mask_w_once(wsp_ref, wm_scr):
    @pl.when(pl.program_id(0) == 0)
    def _():
        causal = _causal()
        for hh in range(N_HEADS):
            wm_scr[hh] = jnp.where(causal, wsp_ref[hh], 0.0).astype(BF16)


def _mixer_fwd(h_t, cos_t, sin_t, w_spatial, b_spatial, vln_g, vln_b, sinks, band_bias, comms=()):
    t_tok = h_t.shape[1]
    nb = t_tok // BLK
    group = N_HEADS // N_KV_HEADS

    def body(sinks_ref, u_ref, vg_ref, q_ref, kvc_ref, kvp_ref, cosc_ref, sinc_ref, cosp_ref, sinp_ref,
             wsp_ref, bsp_ref, g_ref, b_ref, bias_ref, cat_ref, wm_scr):
        _mask_w_once(wsp_ref, wm_scr)
        ua = _gelu(u_ref[...])
        vp, _, _ = _ln_fwd_t(_gelu(vg_ref[...]), g_ref[...], b_ref[...])
        vpb = vp.astype(BF16)
        for hh in range(N_HEADS):
            rows = slice(hh * HEAD_DIM, (hh + 1) * HEAD_DIM)
            mixed = _dot(vpb[rows], wm_scr[hh], NT) + bsp_ref[hh:hh + 1, :]
            cat_ref[rows, :] = (ua[rows] * mixed).astype(BF16)

        cosc, sinc = cosc_ref[...], sinc_ref[...]
        qr = (_rope_t(q_ref[...], cosc, sinc) * SCORE_SCALE).astype(BF16)
        _, k_n, v_t = _keys_values(kvc_ref[...], kvp_ref[...], cosc, sinc, cosp_ref[...], sinp_ref[...])
        bias = _group_lanes([bias_ref[...]] * group)
        for kv in range(N_KV_HEADS):
            heads = range(kv * group, (kv + 1) * group)
            qs = _group_lanes([qr[hh * HEAD_DIM:(hh + 1) * HEAD_DIM] for hh in heads])
            sink = _group_lanes([jnp.full((1, BLK), sinks_ref[hh], F32) for hh in heads])
            p, _ = _softmax_sink_t(_dot(k_n, _pad_head(qs, kv)) + bias, sink)
            o = _dot(v_t[kv * HEAD_DIM:(kv + 1) * HEAD_DIM], p.astype(BF16)).astype(BF16)
            for j, hh in enumerate(heads):
                cat_ref[D_GMLP + hh * HEAD_DIM:D_GMLP + (hh + 1) * HEAD_DIM, :] = o[:, j * BLK:(j + 1) * BLK]

    full = lambda shape: pl.BlockSpec(shape, lambda i: (0,) * len(shape))
    return _carry(
        body, name="mixer_fwd", grid=(nb,), comms=comms,
        in_specs=[pl.BlockSpec(memory_space=pltpu.SMEM)] + _h_specs() + _table_specs() + [
            full((N_HEADS, BLK, BLK)), full((N_HEADS, BLK)), full((D_GMLP, 1)), full((D_GMLP, 1)), BIAS_SPEC],
        out_specs=[pl.BlockSpec((D_GMLP + D_ATTN, BLK), lambda i: (0, i))],
        out_shape=[jax.ShapeDtypeStruct((D_GMLP + D_ATTN, t_tok), BF16)],
        scratch_shapes=[pltpu.VMEM((N_HEADS, BLK, BLK), BF16)],
        args=(sinks, h_t, h_t, h_t, h_t, h_t, cos_t, sin_t, cos_t, sin_t, w_spatial, b_spatial, vln_g, vln_b, band_bias))


def _proj_out(cat_t, x2, w_out_b, comms=()):
    t_tok, d = x2.shape
    tm = min(512, t_tok)

    def body(cat_ref, x_ref, w_ref, z_ref):
        z_ref[...] = ALPHA * x_ref[...] + _dot(cat_ref[...], w_ref[...], TN)

    return _carry(
        body, name="proj_out", grid=(t_tok // tm,), comms=comms,
        in_specs=[pl.BlockSpec((cat_t.shape[0], tm), lambda i: (0, i)), pl.BlockSpec((tm, d), lambda i: (i, 0)),
                  pl.BlockSpec(w_out_b.shape, lambda i: (0, 0))],
        out_specs=[pl.BlockSpec((tm, d), lambda i: (i, 0))],
        out_shape=[jax.ShapeDtypeStruct((t_tok, d), F32)],
        args=(cat_t, x2, w_out_b))


def _ffn_fwd_bwd(z1, target, w1_parts, w2_parts, ln1_g, ln1_b, ln2_g, ln2_b):
    t_tok, d = z1.shape
    n_part = len(w1_parts)
    n_chunk, _, fc = w1_parts[0].shape
    f = n_part * n_chunk * fc
    tm = min(256, t_tok)
    pieces = [(j, p, (j * n_part + p) * fc) for j in range(n_chunk) for p in range(n_part)]

    def body(z1_ref, tgt_ref, *refs):
        w1_refs, w2_refs = refs[:n_part], refs[n_part:2 * n_part]
        (g1_ref, b1_ref, g2_ref, b2_ref, act_ref, dpre_ref, x1b_ref, dz2b_ref, dz1_ref, stats_ref, r_scr) = refs[2 * n_part:]

        @pl.when(pl.program_id(0) == 0)
        def _():
            stats_ref[...] = jnp.zeros_like(stats_ref)

        g1, g2 = g1_ref[...], g2_ref[...]
        x1, xhat1, rstd1 = _ln_fwd(z1_ref[...], g1, b1_ref[...])
        x1b = x1.astype(BF16)
        x1b_ref[...] = x1b
        ff = jnp.zeros((tm, d), F32)
        for j, p, c0 in pieces:
            r = jnp.maximum(_dot(x1b, w1_refs[p][j]), 0.0)
            r_scr[:, c0:c0 + fc] = r
            act = (r * r).astype(BF16)
            act_ref[:, c0:c0 + fc] = act
            ff = ff + _dot(act, w2_refs[p][j])
        y, xhat2, rstd2 = _ln_fwd(ALPHA * x1 + ff, g2, b2_ref[...])
        diff = y - tgt_ref[...]
        loss = 0.5 * jnp.sum(jnp.sum(diff * diff, axis=-1, keepdims=True) / d, axis=0, keepdims=True)
        dy = diff / d
        dz2 = _ln_bwd(dy, xhat2, rstd2, g2)
        dz2b = dz2.astype(BF16)
        dz2b_ref[...] = dz2b
        dx1 = ALPHA * dz2
        for j, p, c0 in pieces:
            dpre = (_dot(dz2b, w2_refs[p][j], NT) * (2.0 * r_scr[:, c0:c0 + fc])).astype(BF16)
            dpre_ref[:, c0:c0 + fc] = dpre
            dx1 = dx1 + _dot(dpre, w1_refs[p][j], NT)
        dz1_ref[...] = _ln_bwd(dx1, xhat1, rstd1, g1)
        stats_ref[0:1, :] += jnp.sum(dx1 * xhat1, axis=0, keepdims=True)
        stats_ref[1:2, :] += jnp.sum(dx1, axis=0, keepdims=True)
        stats_ref[2:3, :] += jnp.sum(dy * xhat2, axis=0, keepdims=True)
        stats_ref[3:4, :] += jnp.sum(dy, axis=0, keepdims=True)
        stats_ref[4:5, :] += jnp.broadcast_to(loss, (1, d))

    tok = lambda w: pl.BlockSpec((tm, w), lambda i: (i, 0))
    vec = pl.BlockSpec((1, d), lambda i: (0, 0))
    return _carry(
        body, name="ffn_fwd_bwd", grid=(t_tok // tm,),
        in_specs=[tok(d), tok(d)]
        + [pl.BlockSpec(w.shape, lambda i: (0, 0, 0), pipeline_mode=pl.Buffered(1)) for w in (*w1_parts, *w2_parts)]
        + [vec, vec, vec, vec],
        out_specs=[tok(f), tok(f), tok(d), tok(d), tok(d), pl.BlockSpec((8, d), lambda i: (0, 0))],
        out_shape=[jax.ShapeDtypeStruct((t_tok, f), BF16), jax.ShapeDtypeStruct((t_tok, f), BF16),
                   jax.ShapeDtypeStruct((t_tok, d), BF16), jax.ShapeDtypeStruct((t_tok, d), BF16),
                   jax.ShapeDtypeStruct((t_tok, d), F32), jax.ShapeDtypeStruct((8, d), F32)],
        scratch_shapes=[pltpu.VMEM((tm, f), F32)],
        args=(z1, target, *w1_parts, *w2_parts, ln1_g, ln1_b, ln2_g, ln2_b))[0]


def _ffn_wgrad1(x1b, dpre_b, n_chunk, comms=()):
    t_tok, d = x1b.shape
    fc = dpre_b.shape[1] // n_chunk

    def body(x1_ref, dpre_ref, g_ref):
        g_ref[...] = _dot(x1_ref[...], dpre_ref[...], TN)

    return _carry(
        body, name="ffn_wgrad1", grid=(n_chunk,), comms=comms,
        in_specs=[pl.BlockSpec((t_tok, d), lambda j: (0, 0), pipeline_mode=pl.Buffered(1)),
                  pl.BlockSpec((t_tok, fc), lambda j: (0, j))],
        out_specs=[pl.BlockSpec((None, d, fc), lambda j: (j, 0, 0))],
        out_shape=[jax.ShapeDtypeStruct((n_chunk, d, fc), F32)],
        args=(x1b, dpre_b))


def _ffn_wgrad2(act_b, dz2b, n_chunk, comms=()):
    t_tok, d = dz2b.shape
    fc = act_b.shape[1] // n_chunk

    def body(act_ref, dz2_ref, g_ref):
        g_ref[...] = _dot(act_ref[...], dz2_ref[...], TN)

    return _carry(
        body, name="ffn_wgrad2", grid=(n_chunk,), comms=comms,
        in_specs=[pl.BlockSpec((t_tok, fc), lambda j: (0, j)),
                  pl.BlockSpec((t_tok, d), lambda j: (0, 0), pipeline_mode=pl.Buffered(1))],
        out_specs=[pl.BlockSpec((None, fc, d), lambda j: (j, 0, 0))],
        out_shape=[jax.ShapeDtypeStruct((n_chunk, fc, d), F32)],
        args=(act_b, dz2b))


def _proj_out_bwd(dz1, cat_t, w_out_b, comms=()):
    t_tok, d = dz1.shape
    d_mix = cat_t.shape[0]
    tm = min(512, t_tok)

    def body(dz1_ref, cat_ref, w_ref, dcat_ref, gw_ref):
        @pl.when(pl.program_id(0) == 0)
        def _():
            gw_ref[...] = jnp.zeros_like(gw_ref)

        dzb = dz1_ref[...].astype(BF16)
        dcat_ref[...] = _dot(w_ref[...], dzb, NT)
        gw_ref[...] += _dot(cat_ref[...], dzb)

    return _carry(
        body, name="proj_out_bwd", grid=(t_tok // tm,), comms=comms,
        in_specs=[pl.BlockSpec((tm, d), lambda i: (i, 0)), pl.BlockSpec((d_mix, tm), lambda i: (0, i)),
                  pl.BlockSpec((d_mix, d), lambda i: (0, 0))],
        out_specs=[pl.BlockSpec((d_mix, tm), lambda i: (0, i)), pl.BlockSpec((d_mix, d), lambda i: (0, 0))],
        out_shape=[jax.ShapeDtypeStruct((d_mix, t_tok), F32), jax.ShapeDtypeStruct((d_mix, d), F32)],
        args=(dz1, cat_t, w_out_b))


def _mixer_bwd(dcat_t, h_t, cos_t, sin_t, w_spatial, b_spatial, vln_g, vln_b, sinks, band_bias, comms=()):
    t_tok = h_t.shape[1]
    nb = t_tok // BLK
    group = N_HEADS // N_KV_HEADS

    def body(sinks_ref, dcat_ref, u_ref, vg_ref, q_ref, kvc_ref, kvp_ref, cosc_ref, sinc_ref, cosp_ref, sinp_ref,
             wsp_ref, bsp_ref, g_ref, b_ref, bias_ref, dh_ref, dkvp_ref, gwsb_ref, gbsp_ref, gvln_ref, gsink_ref,
             dg_acc, db_acc, wm_scr, gws_ref):
        i = pl.program_id(0)

        @pl.when(i == 0)
        def _():
            gws_ref[...] = jnp.zeros_like(gws_ref)
            gbsp_ref[...] = jnp.zeros_like(gbsp_ref)
            gsink_ref[...] = jnp.zeros_like(gsink_ref)
            dg_acc[...] = jnp.zeros_like(dg_acc)
            db_acc[...] = jnp.zeros_like(db_acc)

        _mask_w_once(wsp_ref, wm_scr)

        g = g_ref[...]
        ua, ua_grad = _gelu_and_grad(u_ref[...])
        vv, vv_grad = _gelu_and_grad(vg_ref[...])
        vp, vhat, rstd = _ln_fwd_t(vv, g, b_ref[...])
        vpb = vp.astype(BF16)
        da = dcat_ref[0:D_GMLP, :]
        dmixed = da * ua
        dvp_parts = []
        for hh in range(N_HEADS):
            rows = slice(hh * HEAD_DIM, (hh + 1) * HEAD_DIM)
            mixed = _dot(vpb[rows], wm_scr[hh], NT) + bsp_ref[hh:hh + 1, :]
            dh_ref[COL_U + hh * HEAD_DIM:COL_U + (hh + 1) * HEAD_DIM, :] = da[rows] * mixed * ua_grad[rows]
            dm = dmixed[rows]
            dmb = dm.astype(BF16)
            gbsp_ref[hh:hh + 1, :] += jnp.sum(dm, axis=0, keepdims=True)
            gws_ref[hh] += _dot(dmb, vpb[rows], TN)
            dvp_parts.append(_dot(dmb, wm_scr[hh]))
        dvp = jnp.concatenate(dvp_parts, axis=0)
        dg_acc[...] += dvp * vhat
        db_acc[...] += dvp
        dh_ref[COL_V:COL_V + D_GMLP, :] = _ln_bwd_t(dvp, vhat, rstd, g) * vv_grad

        cosc, sinc, cosp, sinp = cosc_ref[...], sinc_ref[...], cosp_ref[...], sinp_ref[...]
        qr = (_rope_t(q_ref[...], cosc, sinc) * SCORE_SCALE).astype(BF16)
        k_t, k_n, v_t = _keys_values(kvc_ref[...], kvp_ref[...], cosc, sinc, cosp, sinp)
        v_n = jnp.concatenate([kvp_ref[D_KV:, :].T, kvc_ref[D_KV:, :].T], axis=0).astype(BF16)
        bias = _group_lanes([bias_ref[...]] * group)
        dk, dv, dq_parts = [], [], []
        for kv in range(N_KV_HEADS):
            heads = range(kv * group, (kv + 1) * group)
            kv_rows = slice(kv * HEAD_DIM, (kv + 1) * HEAD_DIM)
            qs = _group_lanes([qr[hh * HEAD_DIM:(hh + 1) * HEAD_DIM] for hh in heads])
            dos = _group_lanes([dcat_ref[D_GMLP + hh * HEAD_DIM:D_GMLP + (hh + 1) * HEAD_DIM, :] for hh in heads]).astype(BF16)
            sink = _group_lanes([jnp.full((1, BLK), sinks_ref[hh], F32) for hh in heads])
            p, p_sink = _softmax_sink_t(_dot(k_n, _pad_head(qs, kv)) + bias, sink)
            dp = _dot(v_n, _pad_head(dos, kv))
            delta = jnp.sum(p * dp, axis=0, keepdims=True)
            ds = (p * (dp - delta)).astype(BF16)
            dsink = p_sink * delta
            dq = _dot(k_t[kv_rows], ds) * SCORE_SCALE
            for j, hh in enumerate(heads):
                gsink_ref[hh:hh + 1, :] -= dsink[:, j * BLK:(j + 1) * BLK]
                dq_parts.append(dq[:, j * BLK:(j + 1) * BLK])
            dk.append(_dot(qs, ds, NT))
            dv.append(_dot(dos, p.astype(BF16), NT))
        dh_ref[COL_Q:COL_Q + D_ATTN, :] = _rope_t(jnp.concatenate(dq_parts, axis=0), cosc, sinc, bwd=True)
        dk_all = jnp.concatenate(dk, axis=0)
        dv_all = jnp.concatenate(dv, axis=0)
        dh_ref[COL_K:COL_K + D_KV, :] = _rope_t(dk_all[:, BLK:], cosc, sinc, bwd=True)
        dh_ref[COL_K + D_KV:COL_K + 2 * D_KV, :] = dv_all[:, BLK:]
        dkvp_ref[0:D_KV, :] = _rope_t(dk_all[:, :BLK], cosp, sinp, bwd=True)
        dkvp_ref[D_KV:2 * D_KV, :] = dv_all[:, :BLK]

        @pl.when(i == nb - 1)
        def _():
            causal = _causal()
            for hh in range(N_HEADS):
                gwsb_ref[hh] = jnp.where(causal, gws_ref[hh], 0.0).astype(BF16)
            gvln_ref[...] = jnp.zeros_like(gvln_ref)
            gvln_ref[0:1, :] = jnp.sum(dg_acc[...].T, axis=0, keepdims=True)
            gvln_ref[1:2, :] = jnp.sum(db_acc[...].T, axis=0, keepdims=True)

    full = lambda shape: pl.BlockSpec(shape, lambda i: (0,) * len(shape))
    return _carry(
        body, name="mixer_bwd", grid=(nb,), comms=comms,
        in_specs=[pl.BlockSpec(memory_space=pltpu.SMEM), pl.BlockSpec((D_GMLP + D_ATTN, BLK), lambda i: (0, i))]
        + _h_specs() + _table_specs()
        + [full((N_HEADS, BLK, BLK)), full((N_HEADS, BLK)), full((D_GMLP, 1)), full((D_GMLP, 1)), BIAS_SPEC],
        out_specs=[pl.BlockSpec((D_IN, BLK), lambda i: (0, i)),
                   pl.BlockSpec((2 * D_KV, BLK), lambda i: (0, (i + nb - 1) % nb)),
                   full((N_HEADS, BLK, BLK)), full((N_HEADS, BLK)), full((8, D_GMLP)), full((N_HEADS, LANES))],
        out_shape=[jax.ShapeDtypeStruct((D_IN, t_tok), F32), jax.ShapeDtypeStruct((2 * D_KV, t_tok), F32),
                   jax.ShapeDtypeStruct((N_HEADS, BLK, BLK), BF16), jax.ShapeDtypeStruct((N_HEADS, BLK), F32),
                   jax.ShapeDtypeStruct((8, D_GMLP), F32), jax.ShapeDtypeStruct((N_HEADS, LANES), F32)],
        scratch_shapes=[pltpu.VMEM((D_GMLP, BLK), F32), pltpu.VMEM((D_GMLP, BLK), F32), pltpu.VMEM((N_HEADS, BLK, BLK), BF16),
                        pltpu.VMEM((N_HEADS, BLK, BLK), F32)],
        args=(sinks, dcat_t, h_t, h_t, h_t, h_t, h_t, cos_t, sin_t, cos_t, sin_t, w_spatial, b_spatial, vln_g, vln_b, band_bias))


def _proj_in_dgrad(dh_t, dkvp_t, dz1, w_in_t):
    t_tok, d = dz1.shape
    d_in = dh_t.shape[0]
    tm = min(512, t_tok)

    def body(dh_ref, dkvp_ref, dz1_ref, w_ref, dx_ref, dhb_ref):
        dhb = jnp.concatenate([dh_ref[0:COL_K, :], dh_ref[COL_K:, :] + dkvp_ref[...]], axis=0).astype(BF16)
        dhb_ref[...] = dhb
        dx_ref[...] = ALPHA * dz1_ref[...] + _dot(dhb, w_ref[...], TN)

    return _carry(
        body, name="proj_in_dgrad", grid=(t_tok // tm,),
        in_specs=[pl.BlockSpec((d_in, tm), lambda i: (0, i)), pl.BlockSpec((2 * D_KV, tm), lambda i: (0, i)),
                  pl.BlockSpec((tm, d), lambda i: (i, 0)), pl.BlockSpec((d_in, d), lambda i: (0, 0))],
        out_specs=[pl.BlockSpec((tm, d), lambda i: (i, 0)), pl.BlockSpec((d_in, tm), lambda i: (0, i))],
        out_shape=[jax.ShapeDtypeStruct((t_tok, d), F32), jax.ShapeDtypeStruct((d_in, t_tok), BF16)],
        args=(dh_t, dkvp_t, dz1, w_in_t))[0]


def _proj_in_wgrad(dhb_t, xb, comms=()):
    t_tok, d = xb.shape
    d_in = dhb_t.shape[0]
    tm = min(1024, t_tok)

    def body(dhb_ref, xb_ref, gw_ref):
        @pl.when(pl.program_id(0) == 0)
        def _():
            gw_ref[...] = jnp.zeros_like(gw_ref)

        gw_ref[...] += _dot(dhb_ref[...], xb_ref[...])

    return _carry(
        body, name="proj_in_wgrad", grid=(t_tok // tm,), comms=comms,
        in_specs=[pl.BlockSpec((d_in, tm), lambda i: (0, i)), pl.BlockSpec((tm, d), lambda i: (i, 0))],
        out_specs=[pl.BlockSpec((d_in, d), lambda i: (0, 0))],
        out_shape=[jax.ShapeDtypeStruct((d_in, d), F32)],
        args=(dhb_t, xb))


def _adamw(w, g, m, v):
    m = ADAM_B1 * m + (1.0 - ADAM_B1) * g
    v = ADAM_B2 * v + (1.0 - ADAM_B2) * (g * g)
    m_hat = m / (1.0 - ADAM_B1 ** ADAM_STEP)
    v_hat = v / (1.0 - ADAM_B2 ** ADAM_STEP)
    delta = -ADAM_LR * (m_hat / (jnp.sqrt(v_hat) + ADAM_EPS) + ADAM_WD * w)
    return delta, m, v


def _row_tiled(name, own, recv, extra, n_out, finish, comms=()):
    r, c = own.shape
    recv = [] if recv is None else list(recv)
    k = max(len(recv), 1)
    n = max(k, -(-r // 512))
    tr, per = r // n, n // k
    blk = pl.BlockSpec((tr, c), lambda i: (i, 0))

    def body(own_ref, *refs):
        recv_refs, rest = refs[:len(recv)], refs[len(recv):]
        ins, outs = rest[:len(extra)], rest[len(extra):]

        def tile(recv_ref):
            g = own_ref[...]
            if recv_ref is not None:
                g = ((g + recv_ref[0].astype(F32)) + recv_ref[1].astype(F32)) + recv_ref[2].astype(F32)
            for o_ref, val in zip(outs, finish(g, *[a[...] for a in ins])):
                o_ref[...] = val

        if len(recv) <= 1:
            tile(recv_refs[0] if recv else None)
        else:
            for p in range(k):
                pl.when(pl.program_id(0) // per == p)(functools.partial(tile, recv_refs[p]))

    recv_specs = [pl.BlockSpec((3, tr, c), lambda i, p=p: (0, jnp.clip(i - p * per, 0, per - 1), 0)) for p in range(len(recv))]
    return _carry(
        body, name=name, grid=(n,), comms=comms,
        in_specs=[blk] + recv_specs + [blk] * len(extra),
        out_specs=[blk] * n_out, out_shape=[jax.ShapeDtypeStruct((r, c), F32)] * n_out,
        args=(own, *recv, *extra))


def _adamw_shard(name, own, recv, w, m, v, comms=()):
    def finish(g, w_t, m_t, v_t):
        return (g,) + _adamw(w_t, g, m_t, v_t)

    return _row_tiled(name, own, recv, (w, m, v), 4, finish, comms)


def _sum_partials(name, own, recv):
    return _row_tiled(name, own, recv, (), 1, lambda g: (g,))[0][0]


VEC_VLN, VEC_LN1G, VEC_LN1B, VEC_LN2G, VEC_LN2B, VEC_SINK, VEC_LOSS, VEC_BSP, VEC_ROWS = 0, 1, 2, 3, 4, 5, 6, 8, 16


def _adamw_small(parts_w, parts_vec, params):
    n = parts_w.shape[0]
    flat = [a for p in params for a in p]
    shapes = [p[0].shape for p in params]

    def grads(gw, gv):
        return [gw, gv[VEC_VLN:VEC_VLN + 1, 0:D_GMLP], gv[VEC_VLN:VEC_VLN + 1, D_GMLP:2 * D_GMLP],
                gv[VEC_BSP:VEC_BSP + N_HEADS, 0:BLK], gv[VEC_LN1G:VEC_LN1G + 1], gv[VEC_LN1B:VEC_LN1B + 1],
                gv[VEC_LN2G:VEC_LN2G + 1], gv[VEC_LN2B:VEC_LN2B + 1], gv[VEC_SINK:VEC_SINK + 1, 0:N_HEADS]]

    def body(pw_ref, pv_ref, *refs):
        ins, outs = refs[:len(flat)], refs[len(flat):]
        gw, gv = pw_ref[0].astype(F32), pv_ref[0]
        for k in range(1, n):
            gw, gv = gw + pw_ref[k].astype(F32), gv + pv_ref[k]
        for i, g in enumerate(grads(gw, gv)):
            w_ref, m_ref, v_ref = ins[3 * i:3 * i + 3]
            delta, m_new, v_new = _adamw(w_ref[...], g, m_ref[...], v_ref[...])
            for o_ref, val in zip(outs[4 * i:4 * i + 4], (g, delta, m_new, v_new)):
                o_ref[...] = val
        outs[-1][...] = gv[VEC_LOSS:VEC_LOSS + 1, 0:LANES]

    whole = lambda shape: pl.BlockSpec(shape, lambda i: (0,) * len(shape))
    res = _carry(
        body, name="adamw_small", grid=(1,),
        in_specs=[whole(parts_w.shape), whole(parts_vec.shape)] + [whole(a.shape) for a in flat],
        out_specs=[whole(s) for s in shapes for _ in range(4)] + [whole((1, LANES))],
        out_shape=[jax.ShapeDtypeStruct(s, F32) for s in shapes for _ in range(4)] + [jax.ShapeDtypeStruct((1, LANES), F32)],
        args=(parts_w, parts_vec, *flat))[0]
    return [res[4 * i:4 * i + 4] for i in range(len(params))], res[-1]


def _pair_sum(name, parts, recv, core_chip):
    _, r, c = parts.shape
    tr = r if r <= 512 else 512

    def body(cc_ref, a_ref, b_ref, wire_ref, own_ref):
        s = a_ref[...] + b_ref[...]
        wire_ref[...] = s.astype(BF16)

        @pl.when(pl.program_id(1) == cc_ref[1])
        def _():
            own_ref[...] = s

    return _carry(
        body, name=name, grid=(r // tr, 4), prefetch=(core_chip,),
        in_specs=[pl.BlockSpec((None, tr, c), lambda i, q, cc: (2 * q + cc[0], i, 0)),
                  pl.BlockSpec((None, tr, c), lambda i, q, cc: (q, i, 0))],
        out_specs=[pl.BlockSpec((None, tr, c), lambda i, q, cc: (q, i, 0)), pl.BlockSpec((tr, c), lambda i, q, cc: (i, 0))],
        out_shape=[jax.ShapeDtypeStruct((4, r, c), BF16), jax.ShapeDtypeStruct((r, c), F32)],
        args=(parts, recv))[0]


def kernel(x, positions, w_in, v_ln_g, v_ln_b, w_spatial, b_spatial, sinks, w_out, ln1_g, ln1_b, w_ff1, w_ff2, ln2_g, ln2_b, loss_target, m_w_in, m_v_ln_g, m_v_ln_b, m_w_spatial, m_b_spatial, m_sinks, m_w_out, m_ln1_g, m_ln1_b, m_w_ff1, m_w_ff2, m_ln2_g, m_ln2_b, v_w_in, v_v_ln_g, v_v_ln_b, v_w_spatial, v_b_spatial, v_sinks, v_w_out, v_ln1_g, v_ln1_b, v_w_ff1, v_w_ff2, v_ln2_g, v_ln2_b):
    _, t_tok, d = x.shape
    xi, yi, ci = _place()
    core_chip = jnp.stack([ci, 2 * xi + yi]).astype(jnp.int32)
    x2 = x.reshape(t_tok, d)
    target = loss_target.reshape(t_tok, d)
    inv_freq = ROPE_THETA ** (-jnp.arange(0, HEAD_DIM, 2, dtype=F32) / HEAD_DIM)
    wsp, bsp, sink_vec = w_spatial[0], b_spatial[0], sinks[0]
    vg_col, vb_col = v_ln_g.reshape(D_GMLP, 1), v_ln_b.reshape(D_GMLP, 1)
    big = {"in": w_in[0], "out": w_out[0], "ff1": w_ff1[0], "ff2": w_ff2[0]}
    half1, half2 = big["ff1"].shape[1] // 2, big["ff2"].shape[0] // 2
    w1_mine = [big["ff1"][:, :half1].astype(BF16), big["ff1"][:, half1:].astype(BF16)]
    w2_mine = [big["ff2"][:half2].astype(BF16), big["ff2"][half2:].astype(BF16)]

    (cos_t, sin_t), ((g_in,),) = _rope_tables(
        positions, jnp.tile(inv_freq, 2).reshape(HEAD_DIM, 1), comms=[_gather_comm([big["in"].T.astype(BF16)])])
    w_in_t = g_in.reshape(D_IN, d)
    (h_t, xb), ((g_out, w1_a),) = _proj_in(x2, w_in_t, comms=[_gather_comm([big["out"].astype(BF16), w1_mine[0]])])
    w_out_b = g_out.reshape(-1, d)
    band_bias = _band_bias()
    (cat_t,), ((w1_b, w2_a),) = _mixer_fwd(h_t, cos_t, sin_t, wsp, bsp, vg_col, vb_col, sink_vec, band_bias,
                                           comms=[_gather_comm([w1_mine[1], w2_mine[0]])])
    (z1,), ((w2_b,),) = _proj_out(cat_t, x2, w_out_b, comms=[_gather_comm([w2_mine[1]])])
    act_b, dpre_b, x1b, dz2b, dz1, stats = _ffn_fwd_bwd(z1, target, [w1_a, w1_b], [w2_a, w2_b], ln1_g, ln1_b, ln2_g, ln2_b)

    (p_ff1,), _ = _ffn_wgrad1(x1b, dpre_b, N_DEV)
    (p_ff2,), ((s_ff1,),) = _ffn_wgrad2(act_b, dz2b, N_DEV, comms=[_sibling_comm([p_ff1])])
    wire_ff1, own_ff1 = _pair_sum("pair_sum_ff1", p_ff1, s_ff1, core_chip)
    rows1 = wire_ff1.shape[1] // 2
    (dcat_t, gw_out), ((s_ff2,), (r_ff1a,)) = _proj_out_bwd(
        dz1, cat_t, w_out_b, comms=[_sibling_comm([p_ff2]), _chips_comm([wire_ff1], rows=(0, rows1))])
    wire_ff2, own_ff2 = _pair_sum("pair_sum_ff2", p_ff2, s_ff2, core_chip)
    p_out = gw_out.reshape(N_DEV, -1, d)
    (dh_t, dkvp_t, g_wsp, g_bsp, g_vln, g_sink), ((r_ff1b,), (r_ff2,), (s_out,)) = _mixer_bwd(
        dcat_t, h_t, cos_t, sin_t, wsp, bsp, vg_col, vb_col, sink_vec, band_bias,
        comms=[_chips_comm([wire_ff1], rows=(rows1, rows1)), _chips_comm([wire_ff2]), _sibling_comm([p_out])])
    wire_out, own_out = _pair_sum("pair_sum_out", p_out, s_out, core_chip)
    grad_x, dhb_t = _proj_in_dgrad(dh_t, dkvp_t, dz1, w_in_t)
    sink_row = jnp.pad(g_sink.sum(axis=1).reshape(1, N_HEADS), ((0, 0), (0, d - N_HEADS)))
    small_vec = jnp.concatenate([g_vln[0:2].reshape(1, d), stats[0:4], sink_row, stats[4:5], jnp.zeros((1, d), F32),
                                 jnp.pad(g_bsp, ((0, 0), (0, d - BLK)))], axis=0)
    (gw_in_t,), ((parts_w, parts_vec), (r_out,)) = _proj_in_wgrad(
        dhb_t, xb, comms=[_gather_comm([g_wsp.reshape(-1, BLK), small_vec]), _chips_comm([wire_out])])
    p_in = gw_in_t.reshape(N_DEV, -1, d)

    out_out, ((s_in,),) = _adamw_shard("adamw_out", own_out, [r_out], big["out"], m_w_out[0], v_w_out[0], comms=[_sibling_comm([p_in])])
    wire_in, own_in = _pair_sum("pair_sum_in", p_in, s_in, core_chip)
    rows_in = wire_in.shape[1] // 2
    ff1_out, ((r_in_a,),) = _adamw_shard("adamw_ff1", own_ff1, [r_ff1a, r_ff1b], big["ff1"], m_w_ff1[0], v_w_ff1[0],
                                         comms=[_chips_comm([wire_in], rows=(0, rows_in))])
    ff2_out, ((r_in_b,),) = _adamw_shard("adamw_ff2", own_ff2, [r_ff2], big["ff2"], m_w_ff2[0], v_w_ff2[0],
                                         comms=[_chips_comm([wire_in], rows=(rows_in, rows_in))])
    g_in_t = _sum_partials("sum_in", own_in, [r_in_a, r_in_b])
    in_out, _ = _adamw_shard("adamw_in", g_in_t.T, None, big["in"], m_w_in[0], v_w_in[0])
    small = [(w_spatial, m_w_spatial, v_w_spatial), (v_ln_g, m_v_ln_g, v_v_ln_g), (v_ln_b, m_v_ln_b, v_v_ln_b),
             (b_spatial, m_b_spatial, v_b_spatial), (ln1_g, m_ln1_g, v_ln1_g), (ln1_b, m_ln1_b, v_ln1_b),
             (ln2_g, m_ln2_g, v_ln2_g), (ln2_b, m_ln2_b, v_ln2_b), (sinks, m_sinks, v_sinks)]
    views = [(-1, BLK), None, None, (N_HEADS, BLK)] + [None] * 5
    small_res, loss_row = _adamw_small(parts_w, parts_vec, [
        tuple(a if vw is None else a.reshape(vw) for a in p) for p, vw in zip(small, views)])
    small_out = [[o.reshape(p[0].shape) for o in res] for res, p in zip(small_res, small)]
    loss = loss_row[0, 0]

    big_out = {0: in_out, 6: out_out, 9: ff1_out, 10: ff2_out}
    small_slot = {3: 0, 1: 1, 2: 2, 4: 3, 7: 4, 8: 5, 11: 6, 12: 7, 5: 8}
    outs = [loss, grad_x.reshape(x.shape)]
    for kind in range(4):
        for wi in range(13):
            outs.append(big_out[wi][kind][None] if wi in big_out else small_out[small_slot[wi]][kind])
    return tuple(outs)
```

```python
import functools
import math

import jax
import jax.numpy as jnp
from jax import lax
from jax.experimental import pallas as pl
from jax.experimental.pallas import tpu as pltpu

F32 = jnp.float32
BF16 = jnp.bfloat16
MESH = pl.DeviceIdType.MESH

HEAD_DIM = 64
N_HEADS = 8
N_KV_HEADS = 2
BLK = 128
D_GMLP = N_HEADS * HEAD_DIM
D_ATTN = N_HEADS * HEAD_DIM
D_KV = N_KV_HEADS * HEAD_DIM
D_IN = 2 * D_GMLP + D_ATTN + 2 * D_KV
COL_U, COL_V, COL_Q, COL_K = 0, D_GMLP, 2 * D_GMLP, 2 * D_GMLP + D_ATTN
ROPE_THETA = 10000.0
LN_EPS = 1e-5
ALPHA = 2.0 ** 0.25
NEG_INF = -1e30
SCORE_SCALE = 1.0 / math.sqrt(HEAD_DIM)
ADAM_LR, ADAM_B1, ADAM_B2, ADAM_EPS, ADAM_WD, ADAM_STEP = 0.001, 0.9, 0.999, 1e-08, 0.01, 10
N_DEV = 8
LANES = 128
VMEM_LIMIT = 56 * 1024 * 1024

NT = (((1,), (1,)), ((), ()))
TN = (((0,), (0,)), ((), ()))


def _params(*sem):
    return pltpu.CompilerParams(dimension_semantics=sem, vmem_limit_bytes=VMEM_LIMIT)


def _dot(a, b, dims=None):
    if dims is None:
        return jnp.dot(a, b, preferred_element_type=F32)
    return lax.dot_general(a, b, dims, preferred_element_type=F32)


def _mean(a):
    return jnp.mean(a, axis=-1, keepdims=True)


def _ln_fwd(z, g, b):
    zc = z - _mean(z)
    rstd = lax.rsqrt(_mean(zc * zc) + LN_EPS)
    xhat = zc * rstd
    return xhat * g + b, xhat, rstd


def _ln_bwd(dy, xhat, rstd, g):
    dxhat = dy * g
    return rstd * (dxhat - _mean(dxhat) - xhat * _mean(dxhat * xhat))


_GELU_C = math.sqrt(2.0 / math.pi)


def _gelu(x):
    t = jnp.tanh(_GELU_C * (x + 0.044715 * (x * x * x)))
    return 0.5 * x * (1.0 + t)


def _gelu_and_grad(x):
    x2 = x * x
    t = jnp.tanh(_GELU_C * (x + 0.044715 * (x2 * x)))
    hx, ht = 0.5 * x, 0.5 * (1.0 + t)
    return x * ht, ht + hx * (1.0 - t * t) * (_GELU_C * (1.0 + 3.0 * 0.044715 * x2))


def _mean0(a):
    return jnp.mean(a, axis=0, keepdims=True)


def _ln_fwd_t(z, g, b):
    zc = z - _mean0(z)
    rstd = lax.rsqrt(_mean0(zc * zc) + LN_EPS)
    xhat = zc * rstd
    return xhat * g + b, xhat, rstd


def _ln_bwd_t(dy, xhat, rstd, g):
    dxhat = dy * g
    return rstd * (dxhat - _mean0(dxhat) - xhat * _mean0(dxhat * xhat))


def _rope_t(t, cos, sin_signed, bwd=False):
    half = HEAD_DIM // 2
    outs = []
    for r in range(0, t.shape[0], HEAD_DIM):
        th = t[r:r + HEAD_DIM]
        sw = jnp.concatenate([th[half:], th[:half]], axis=0) * sin_signed
        outs.append(th * cos - sw if bwd else th * cos + sw)
    return jnp.concatenate(outs, axis=0)


ANY = pl.BlockSpec(memory_space=pl.ANY)


def _place():
    return lax.axis_index("x"), lax.axis_index("y"), lax.axis_index("c")


class _Comm:
    def __init__(self, ins, outs, sems, start, finish):
        self.ins, self.outs, self.sems, self.start, self.finish = ins, outs, sems, start, finish


def _gather_comm(arrs):
    n = len(arrs)

    def parts(ins, outs, sems):
        send_sems, recv_sems, local_sems = sems
        x, y, c = _place()
        me, sibling = (x, y, c), (x, y, 1 - c)
        chips = [(1 - x, y), (x, 1 - y), (1 - x, 1 - y)]

        def copy(a, k, block, to, src=None):
            px, py, pc = block
            dst = outs[a].at[4 * px + 2 * py + pc]
            return pltpu.make_async_remote_copy(
                src_ref=dst if src is None else src, dst_ref=dst,
                send_sem=send_sems.at[a, k], recv_sem=recv_sems.at[a, k], device_id=to, device_id_type=MESH)

        mine = [pltpu.make_async_copy(ins[a], outs[a].at[4 * x + 2 * y + c], local_sems.at[a]) for a in range(n)]
        first = []
        for a in range(n):
            first.append(copy(a, 0, me, sibling, src=ins[a]))
            first += [copy(a, 1 + j, me, (*chip, c), src=ins[a]) for j, chip in enumerate(chips)]
        return copy, mine, first, me, sibling, chips, c

    def start(ins, outs, sems):
        _, mine, first, *_ = parts(ins, outs, sems)
        for cp in mine + first:
            cp.start()

    def finish(ins, outs, sems):
        copy, mine, first, me, sibling, chips, c = parts(ins, outs, sems)
        passed = []
        for j, chip in enumerate(chips):
            for a in range(n):
                copy(a, 1 + j, (*chip, c), me).wait_recv()
                fwd = copy(a, 4 + j, (*chip, c), sibling)
                fwd.start()
                passed.append(fwd)
        for a in range(n):
            copy(a, 0, sibling, me).wait_recv()
        for j, chip in enumerate(chips):
            for a in range(n):
                copy(a, 4 + j, (*chip, 1 - c), me).wait_recv()
        for cp in first + passed:
            cp.wait_send()
        for cp in mine:
            cp.wait()

    return _Comm(list(arrs), [jax.ShapeDtypeStruct((N_DEV,) + a.shape, a.dtype) for a in arrs],
                 [pltpu.SemaphoreType.DMA((n, 7)), pltpu.SemaphoreType.DMA((n, 7)), pltpu.SemaphoreType.DMA((n,))],
                 start, finish)


def _sibling_comm(parts):
    n = len(parts)

    def copies(ins, outs, sems):
        x, y, c = _place()
        return [pltpu.make_async_remote_copy(
            src_ref=ins[a].at[2 * q + (1 - c)], dst_ref=outs[a].at[q],
            send_sem=sems[0].at[a, q], recv_sem=sems[1].at[a, q],
            device_id=(x, y, 1 - c), device_id_type=MESH) for a in range(n) for q in range(4)]

    return _Comm(list(parts), [jax.ShapeDtypeStruct((4,) + p.shape[1:], p.dtype) for p in parts],
                 [pltpu.SemaphoreType.DMA((n, 4)), pltpu.SemaphoreType.DMA((n, 4))],
                 lambda *r: [cp.start() for cp in copies(*r)], lambda *r: [cp.wait() for cp in copies(*r)])


def _chips_comm(chip_parts, rows=None):
    n = len(chip_parts)
    r0, nr = (0, None) if rows is None else rows

    def copies(ins, outs, sems):
        x, y, c = _place()
        chips = [(1 - x, y), (x, 1 - y), (1 - x, 1 - y)]
        src = lambda a, q: ins[a].at[q] if rows is None else ins[a].at[q, pl.ds(r0, nr)]
        return [pltpu.make_async_remote_copy(
            src_ref=src(a, 2 * px + py), dst_ref=outs[a].at[k],
            send_sem=sems[0].at[a, k], recv_sem=sems[1].at[a, k],
            device_id=(px, py, c), device_id_type=MESH) for a in range(n) for k, (px, py) in enumerate(chips)]

    shape = lambda p: (3,) + p.shape[1:] if rows is None else (3, nr) + p.shape[2:]
    return _Comm(list(chip_parts), [jax.ShapeDtypeStruct(shape(p), p.dtype) for p in chip_parts],
                 [pltpu.SemaphoreType.DMA((n, 3)), pltpu.SemaphoreType.DMA((n, 3))],
                 lambda *r: [cp.start() for cp in copies(*r)], lambda *r: [cp.wait() for cp in copies(*r)])


def _carry(body, *, name, grid, in_specs, out_specs, out_shape, args, comms=(), scratch_shapes=(), prefetch=()):
    n_pre, n_in, n_out, n_scr = len(prefetch), len(in_specs), len(out_specs), len(scratch_shapes)
    c_ins = [a for cm in comms for a in cm.ins]
    c_outs = [s for cm in comms for s in cm.outs]
    c_sems = [s for cm in comms for s in cm.sems]

    def wrapped(*refs):
        pre, refs = refs[:n_pre], refs[n_pre:]
        ins, refs = refs[:n_in], refs[n_in:]
        cins, refs = refs[:len(c_ins)], refs[len(c_ins):]
        outs, refs = refs[:n_out], refs[n_out:]
        couts, refs = refs[:len(c_outs)], refs[len(c_outs):]
        scr, sems = refs[:n_scr], refs[n_scr:]
        groups, i0, o0, s0 = [], 0, 0, 0
        for cm in comms:
            groups.append((cm, cins[i0:i0 + len(cm.ins)], couts[o0:o0 + len(cm.outs)], sems[s0:s0 + len(cm.sems)]))
            i0, o0, s0 = i0 + len(cm.ins), o0 + len(cm.outs), s0 + len(cm.sems)
        first = pl.program_id(0) == 0
        last = pl.program_id(0) == grid[0] - 1
        for ax in range(1, len(grid)):
            first = first & (pl.program_id(ax) == 0)
            last = last & (pl.program_id(ax) == grid[ax] - 1)
        if comms:
            @pl.when(first)
            def _():
                for cm, ci, co, cs in groups:
                    cm.start(ci, co, cs)
        body(*pre, *ins, *outs, *scr)
        if comms:
            @pl.when(last)
            def _():
                for cm, ci, co, cs in groups:
                    cm.finish(ci, co, cs)

    grid_spec = pltpu.PrefetchScalarGridSpec(
        num_scalar_prefetch=n_pre, grid=grid,
        in_specs=list(in_specs) + [ANY] * len(c_ins), out_specs=list(out_specs) + [ANY] * len(c_outs),
        scratch_shapes=list(scratch_shapes) + c_sems)
    res = pl.pallas_call(
        wrapped, name=name, grid_spec=grid_spec, out_shape=list(out_shape) + c_outs,
        compiler_params=_params(*(["arbitrary"] * len(grid))),
    )(*prefetch, *args, *c_ins)
    outs, rest, per_comm = res[:n_out], res[n_out:], []
    for cm in comms:
        per_comm.append(rest[:len(cm.outs)])
        rest = rest[len(cm.outs):]
    return outs, per_comm


def _rope_tables(pos_row, inv_freq_col, comms=()):
    t_tok = pos_row.shape[1]
    tm = min(512, t_tok)

    def body(pos_ref, invf_ref, cos_ref, sin_ref):
        ang = pos_ref[...].astype(F32) * invf_ref[...]
        row = lax.broadcasted_iota(jnp.int32, ang.shape, 0)
        cos_ref[...] = jnp.cos(ang)
        sin_ref[...] = jnp.sin(ang) * jnp.where(row < HEAD_DIM // 2, -1.0, 1.0)

    return _carry(
        body, name="rope_tables", grid=(t_tok // tm,), comms=comms,
        in_specs=[pl.BlockSpec((1, tm), lambda i: (0, i)), pl.BlockSpec((HEAD_DIM, 1), lambda i: (0, 0))],
        out_specs=[pl.BlockSpec((HEAD_DIM, tm), lambda i: (0, i))] * 2,
        out_shape=[jax.ShapeDtypeStruct((HEAD_DIM, t_tok), F32)] * 2,
        args=(pos_row, inv_freq_col))


def _proj_in(x2, w_in_t, comms=()):
    t_tok, d = x2.shape
    d_in = w_in_t.shape[0]
    tm = min(512, t_tok)

    def body(x_ref, w_ref, h_ref, xb_ref):
        xb = x_ref[...].astype(BF16)
        xb_ref[...] = xb
        h_ref[...] = _dot(w_ref[...], xb, NT)

    return _carry(
        body, name="proj_in", grid=(t_tok // tm,), comms=comms,
        in_specs=[pl.BlockSpec((tm, d), lambda i: (i, 0)), pl.BlockSpec((d_in, d), lambda i: (0, 0))],
        out_specs=[pl.BlockSpec((d_in, tm), lambda i: (0, i)), pl.BlockSpec((tm, d), lambda i: (i, 0))],
        out_shape=[jax.ShapeDtypeStruct((d_in, t_tok), F32), jax.ShapeDtypeStruct((t_tok, d), BF16)],
        args=(x2, w_in_t))


def _h_specs():
    kv_row = COL_K // (2 * D_KV)
    return [
        pl.BlockSpec((D_GMLP, BLK), lambda i: (0, i)),
        pl.BlockSpec((D_GMLP, BLK), lambda i: (1, i)),
        pl.BlockSpec((D_ATTN, BLK), lambda i: (2, i)),
        pl.BlockSpec((2 * D_KV, BLK), lambda i: (kv_row, i)),
        pl.BlockSpec((2 * D_KV, BLK), lambda i: (kv_row, jnp.maximum(i - 1, 0))),
    ]


def _table_specs():
    return [
        pl.BlockSpec((HEAD_DIM, BLK), lambda i: (0, i)),
        pl.BlockSpec((HEAD_DIM, BLK), lambda i: (0, i)),
        pl.BlockSpec((HEAD_DIM, BLK), lambda i: (0, jnp.maximum(i - 1, 0))),
        pl.BlockSpec((HEAD_DIM, BLK), lambda i: (0, jnp.maximum(i - 1, 0))),
    ]


def _band_bias():
    ki = lax.broadcasted_iota(jnp.int32, (2, 2 * BLK, BLK), 1)
    qi = lax.broadcasted_iota(jnp.int32, (2, 2 * BLK, BLK), 2)
    later = lax.broadcasted_iota(jnp.int32, (2, 2 * BLK, BLK), 0) > 0
    dist = qi + BLK - ki
    return jnp.where((dist >= 0) & (dist < BLK) & ((ki >= BLK) | later), 0.0, NEG_INF).astype(F32)


BIAS_SPEC = pl.BlockSpec((None, 2 * BLK, BLK), lambda i: (jnp.minimum(i, 1), 0, 0))


def _keys_values(kvc, kvp, cosc, sinc, cosp, sinp):
    kp, kc = _rope_t(kvp[:D_KV], cosp, sinp), _rope_t(kvc[:D_KV], cosc, sinc)
    k_t = jnp.concatenate([kp, kc], axis=1).astype(BF16)
    k_n = jnp.concatenate([kp.T, kc.T], axis=0).astype(BF16)
    v_t = jnp.concatenate([kvp[D_KV:], kvc[D_KV:]], axis=1).astype(BF16)
    return k_t, k_n, v_t


def _pad_head(th, kv):
    z = jnp.zeros_like(th)
    return jnp.concatenate([th, z] if kv == 0 else [z, th], axis=0)


def _group_lanes(parts):
    return jnp.concatenate(parts, axis=1)


def _softmax_sink_t(s, sink):
    m = jnp.maximum(jnp.max(s, axis=0, keepdims=True), sink)
    e = jnp.exp(s - m)
    es = jnp.exp(sink - m)
    r = 1.0 / (jnp.sum(e, axis=0, keepdims=True) + es)
    return e * r, es * r


def _causal():
    row = lax.broadcasted_iota(jnp.int32, (BLK, BLK), 0)
    col = lax.broadcasted_iota(jnp.int32, (BLK, BLK), 1)
    return row >= col


def _mask_w_once(wsp_ref, wm_scr):
    @pl.when(pl.program_id(0) == 0)
    def _():
        causal = _causal()
        for hh in range(N_HEADS):
            wm_scr[hh] = jnp.where(causal, wsp_ref[hh], 0.0).astype(BF16)


def _mixer_fwd(h_t, cos_t, sin_t, w_spatial, b_spatial, vln_g, vln_b, sinks, band_bias, comms=()):
    t_tok = h_t.shape[1]
    nb = t_tok // BLK
    group = N_HEADS // N_KV_HEADS

    def body(sinks_ref, u_ref, vg_ref, q_ref, kvc_ref, kvp_ref, cosc_ref, sinc_ref, cosp_ref, sinp_ref,
             wsp_ref, bsp_ref, g_ref, b_ref, bias_ref, cat_ref, wm_scr):
        _mask_w_once(wsp_ref, wm_scr)
        ua = _gelu(u_ref[...])
        vp, _, _ = _ln_fwd_t(_gelu(vg_ref[...]), g_ref[...], b_ref[...])
        vpb = vp.astype(BF16)
        for hh in range(N_HEADS):
            rows = slice(hh * HEAD_DIM, (hh + 1) * HEAD_DIM)
            mixed = _dot(vpb[rows], wm_scr[hh], NT) + bsp_ref[hh:hh + 1, :]
            cat_ref[rows, :] = (ua[rows] * mixed).astype(BF16)

        cosc, sinc = cosc_ref[...], sinc_ref[...]
        qr = (_rope_t(q_ref[...], cosc, sinc) * SCORE_SCALE).astype(BF16)
        _, k_n, v_t = _keys_values(kvc_ref[...], kvp_ref[...], cosc, sinc, cosp_ref[...], sinp_ref[...])
        bias = _group_lanes([bias_ref[...]] * group)
        for kv in range(N_KV_HEADS):
            heads = range(kv * group, (kv + 1) * group)
            qs = _group_lanes([qr[hh * HEAD_DIM:(hh + 1) * HEAD_DIM] for hh in heads])
            sink = _group_lanes([jnp.full((1, BLK), sinks_ref[hh], F32) for hh in heads])
            p, _ = _softmax_sink_t(_dot(k_n, _pad_head(qs, kv)) + bias, sink)
            o = _dot(v_t[kv * HEAD_DIM:(kv + 1) * HEAD_DIM], p.astype(BF16)).astype(BF16)
            for j, hh in enumerate(heads):
                cat_ref[D_GMLP + hh * HEAD_DIM:D_GMLP + (hh + 1) * HEAD_DIM, :] = o[:, j * BLK:(j + 1) * BLK]

    full = lambda shape: pl.BlockSpec(shape, lambda i: (0,) * len(shape))
    return _carry(
        body, name="mixer_fwd", grid=(nb,), comms=comms,
        in_specs=[pl.BlockSpec(memory_space=pltpu.SMEM)] + _h_specs() + _table_specs() + [
            full((N_HEADS, BLK, BLK)), full((N_HEADS, BLK)), full((D_GMLP, 1)), full((D_GMLP, 1)), BIAS_SPEC],
        out_specs=[pl.BlockSpec((D_GMLP + D_ATTN, BLK), lambda i: (0, i))],
        out_shape=[jax.ShapeDtypeStruct((D_GMLP + D_ATTN, t_tok), BF16)],
        scratch_shapes=[pltpu.VMEM((N_HEADS, BLK, BLK), BF16)],
        args=(sinks, h_t, h_t, h_t, h_t, h_t, cos_t, sin_t, cos_t, sin_t, w_spatial, b_spatial, vln_g, vln_b, band_bias))


def _proj_out(cat_t, x2, w_out_b, comms=()):
    t_tok, d = x2.shape
    tm = min(512, t_tok)

    def body(cat_ref, x_ref, w_ref, z_ref):
        z_ref[...] = ALPHA * x_ref[...] + _dot(cat_ref[...], w_ref[...], TN)

    return _carry(
        body, name="proj_out", grid=(t_tok // tm,), comms=comms,
        in_specs=[pl.BlockSpec((cat_t.shape[0], tm), lambda i: (0, i)), pl.BlockSpec((tm, d), lambda i: (i, 0)),
                  pl.BlockSpec(w_out_b.shape, lambda i: (0, 0))],
        out_specs=[pl.BlockSpec((tm, d), lambda i: (i, 0))],
        out_shape=[jax.ShapeDtypeStruct((t_tok, d), F32)],
        args=(cat_t, x2, w_out_b))


def _ffn_fwd_bwd(z1, target, w1_parts, w2_parts, ln1_g, ln1_b, ln2_g, ln2_b):
    t_tok, d = z1.shape
    n_part = len(w1_parts)
    n_chunk, _, fp = w1_parts[0].shape
    fc = n_part * fp
    f = n_chunk * fc
    tm = min(256, t_tok)

    def body(z1_ref, tgt_ref, *refs):
        w1_hbm, w2_hbm = refs[:n_part], refs[n_part:2 * n_part]
        (g1_ref, b1_ref, g2_ref, b2_ref, act_ref, dpre_ref, x1b_ref, dz2b_ref, dz1_ref, stats_ref,
         r_scr, w1_ref, w2_ref, w_sems) = refs[2 * n_part:]

        @pl.when(pl.program_id(0) == 0)
        def _():
            stats_ref[...] = jnp.zeros_like(stats_ref)
            loads = [pltpu.make_async_copy(w1_hbm[p], w1_ref.at[:, :, pl.ds(p * fp, fp)], w_sems.at[0, p]) for p in range(n_part)]
            loads += [pltpu.make_async_copy(w2_hbm[p], w2_ref.at[:, pl.ds(p * fp, fp), :], w_sems.at[1, p]) for p in range(n_part)]
            for cp in loads:
                cp.start()
            for cp in loads:
                cp.wait()

        g1, g2 = g1_ref[...], g2_ref[...]
        x1, xhat1, rstd1 = _ln_fwd(z1_ref[...], g1, b1_ref[...])
        x1b = x1.astype(BF16)
        x1b_ref[...] = x1b
        ff = jnp.zeros((tm, d), F32)
        for j in range(n_chunk):
            r = jnp.maximum(_dot(x1b, w1_ref[j]), 0.0)
            r_scr[:, j * fc:(j + 1) * fc] = r
            act = (r * r).astype(BF16)
            act_ref[:, j * fc:(j + 1) * fc] = act
            ff = ff + _dot(act, w2_ref[j])
        y, xhat2, rstd2 = _ln_fwd(ALPHA * x1 + ff, g2, b2_ref[...])
        diff = y - tgt_ref[...]
        loss = 0.5 * jnp.sum(jnp.sum(diff * diff, axis=-1, keepdims=True) / d, axis=0, keepdims=True)
        dy = diff / d
        dz2 = _ln_bwd(dy, xhat2, rstd2, g2)
        dz2b = dz2.astype(BF16)
        dz2b_ref[...] = dz2b
        dx1 = ALPHA * dz2
        for j in range(n_chunk):
            dpre = (_dot(dz2b, w2_ref[j], NT) * (2.0 * r_scr[:, j * fc:(j + 1) * fc])).astype(BF16)
            dpre_ref[:, j * fc:(j + 1) * fc] = dpre
            dx1 = dx1 + _dot(dpre, w1_ref[j], NT)
        dz1_ref[...] = _ln_bwd(dx1, xhat1, rstd1, g1)
        stats_ref[0:1, :] += jnp.sum(dx1 * xhat1, axis=0, keepdims=True)
        stats_ref[1:2, :] += jnp.sum(dx1, axis=0, keepdims=True)
        stats_ref[2:3, :] += jnp.sum(dy * xhat2, axis=0, keepdims=True)
        stats_ref[3:4, :] += jnp.sum(dy, axis=0, keepdims=True)
        stats_ref[4:5, :] += jnp.broadcast_to(loss, (1, d))

    tok = lambda w: pl.BlockSpec((tm, w), lambda i: (i, 0))
    vec = pl.BlockSpec((1, d), lambda i: (0, 0))
    return _carry(
        body, name="ffn_fwd_bwd", grid=(t_tok // tm,),
        in_specs=[tok(d), tok(d)] + [ANY] * (2 * n_part) + [vec, vec, vec, vec],
        out_specs=[tok(f), tok(f), tok(d), tok(d), tok(d), pl.BlockSpec((8, d), lambda i: (0, 0))],
        out_shape=[jax.ShapeDtypeStruct((t_tok, f), BF16), jax.ShapeDtypeStruct((t_tok, f), BF16),
                   jax.ShapeDtypeStruct((t_tok, d), BF16), jax.ShapeDtypeStruct((t_tok, d), BF16),
                   jax.ShapeDtypeStruct((t_tok, d), F32), jax.ShapeDtypeStruct((8, d), F32)],
        scratch_shapes=[pltpu.VMEM((tm, f), F32), pltpu.VMEM((n_chunk, d, fc), BF16), pltpu.VMEM((n_chunk, fc, d), BF16),
                        pltpu.SemaphoreType.DMA((2, n_part))],
        args=(z1, target, *w1_parts, *w2_parts, ln1_g, ln1_b, ln2_g, ln2_b))[0]


def _ffn_wgrad1(x1b, dpre_b, n_chunk, comms=()):
    t_tok, d = x1b.shape
    fc = dpre_b.shape[1] // n_chunk

    def body(x1_ref, dpre_ref, g_ref):
        g_ref[...] = _dot(x1_ref[...], dpre_ref[...], TN)

    return _carry(
        body, name="ffn_wgrad1", grid=(n_chunk,), comms=comms,
        in_specs=[pl.BlockSpec((t_tok, d), lambda j: (0, 0), pipeline_mode=pl.Buffered(1)),
                  pl.BlockSpec((t_tok, fc), lambda j: (0, j))],
        out_specs=[pl.BlockSpec((None, d, fc), lambda j: (j, 0, 0))],
        out_shape=[jax.ShapeDtypeStruct((n_chunk, d, fc), F32)],
        args=(x1b, dpre_b))


def _ffn_wgrad2(act_b, dz2b, n_chunk, comms=()):
    t_tok, d = dz2b.shape
    fc = act_b.shape[1] // n_chunk

    def body(act_ref, dz2_ref, g_ref):
        g_ref[...] = _dot(act_ref[...], dz2_ref[...], TN)

    return _carry(
        body, name="ffn_wgrad2", grid=(n_chunk,), comms=comms,
        in_specs=[pl.BlockSpec((t_tok, fc), lambda j: (0, j)),
                  pl.BlockSpec((t_tok, d), lambda j: (0, 0), pipeline_mode=pl.Buffered(1))],
        out_specs=[pl.BlockSpec((None, fc, d), lambda j: (j, 0, 0))],
        out_shape=[jax.ShapeDtypeStruct((n_chunk, fc, d), F32)],
        args=(act_b, dz2b))


def _proj_out_bwd(dz1, cat_t, w_out_b, comms=()):
    t_tok, d = dz1.shape
    d_mix = cat_t.shape[0]
    tm = min(512, t_tok)

    def body(dz1_ref, cat_ref, w_ref, dcat_ref, gw_ref):
        @pl.when(pl.program_id(0) == 0)
        def _():
            gw_ref[...] = jnp.zeros_like(gw_ref)

        dzb = dz1_ref[...].astype(BF16)
        dcat_ref[...] = _dot(w_ref[...], dzb, NT)
        gw_ref[...] += _dot(cat_ref[...], dzb)

    return _carry(
        body, name="proj_out_bwd", grid=(t_tok // tm,), comms=comms,
        in_specs=[pl.BlockSpec((tm, d), lambda i: (i, 0)), pl.BlockSpec((d_mix, tm), lambda i: (0, i)),
                  pl.BlockSpec((d_mix, d), lambda i: (0, 0))],
        out_specs=[pl.BlockSpec((d_mix, tm), lambda i: (0, i)), pl.BlockSpec((d_mix, d), lambda i: (0, 0))],
        out_shape=[jax.ShapeDtypeStruct((d_mix, t_tok), F32), jax.ShapeDtypeStruct((d_mix, d), F32)],
        args=(dz1, cat_t, w_out_b))


def _mixer_bwd(dcat_t, h_t, cos_t, sin_t, w_spatial, b_spatial, vln_g, vln_b, sinks, band_bias, comms=()):
    t_tok = h_t.shape[1]
    nb = t_tok // BLK
    group = N_HEADS // N_KV_HEADS

    def body(sinks_ref, dcat_ref, u_ref, vg_ref, q_ref, kvc_ref, kvp_ref, cosc_ref, sinc_ref, cosp_ref, sinp_ref,
             wsp_ref, bsp_ref, g_ref, b_ref, bias_ref, dh_ref, dkvp_ref, gwsb_ref, gbsp_ref, gvln_ref, gsink_ref,
             dg_acc, db_acc, wm_scr, gws_ref):
        i = pl.program_id(0)

        @pl.when(i == 0)
        def _():
            gws_ref[...] = jnp.zeros_like(gws_ref)
            gbsp_ref[...] = jnp.zeros_like(gbsp_ref)
            gsink_ref[...] = jnp.zeros_like(gsink_ref)
            dg_acc[...] = jnp.zeros_like(dg_acc)
            db_acc[...] = jnp.zeros_like(db_acc)

        _mask_w_once(wsp_ref, wm_scr)

        g = g_ref[...]
        ua, ua_grad = _gelu_and_grad(u_ref[...])
        vv, vv_grad = _gelu_and_grad(vg_ref[...])
        vp, vhat, rstd = _ln_fwd_t(vv, g, b_ref[...])
        vpb = vp.astype(BF16)
        da = dcat_ref[0:D_GMLP, :]
        dmixed = da * ua
        dvp_parts = []
        for hh in range(N_HEADS):
            rows = slice(hh * HEAD_DIM, (hh + 1) * HEAD_DIM)
            mixed = _dot(vpb[rows], wm_scr[hh], NT) + bsp_ref[hh:hh + 1, :]
            dh_ref[COL_U + hh * HEAD_DIM:COL_U + (hh + 1) * HEAD_DIM, :] = da[rows] * mixed * ua_grad[rows]
            dm = dmixed[rows]
            dmb = dm.astype(BF16)
            gbsp_ref[hh:hh + 1, :] += jnp.sum(dm, axis=0, keepdims=True)
            gws_ref[hh] += _dot(dmb, vpb[rows], TN)
            dvp_parts.append(_dot(dmb, wm_scr[hh]))
        dvp = jnp.concatenate(dvp_parts, axis=0)
        dg_acc[...] += dvp * vhat
        db_acc[...] += dvp
        dh_ref[COL_V:COL_V + D_GMLP, :] = _ln_bwd_t(dvp, vhat, rstd, g) * vv_grad

        cosc, sinc, cosp, sinp = cosc_ref[...], sinc_ref[...], cosp_ref[...], sinp_ref[...]
        qr = (_rope_t(q_ref[...], cosc, sinc) * SCORE_SCALE).astype(BF16)
        k_t, k_n, v_t = _keys_values(kvc_ref[...], kvp_ref[...], cosc, sinc, cosp, sinp)
        v_n = jnp.concatenate([kvp_ref[D_KV:, :].T, kvc_ref[D_KV:, :].T], axis=0).astype(BF16)
        bias = _group_lanes([bias_ref[...]] * group)
        dk, dv, dq_parts = [], [], []
        for kv in range(N_KV_HEADS):
            heads = range(kv * group, (kv + 1) * group)
            kv_rows = slice(kv * HEAD_DIM, (kv + 1) * HEAD_DIM)
            qs = _group_lanes([qr[hh * HEAD_DIM:(hh + 1) * HEAD_DIM] for hh in heads])
            dos = _group_lanes([dcat_ref[D_GMLP + hh * HEAD_DIM:D_GMLP + (hh + 1) * HEAD_DIM, :] for hh in heads]).astype(BF16)
            sink = _group_lanes([jnp.full((1, BLK), sinks_ref[hh], F32) for hh in heads])
            p, p_sink = _softmax_sink_t(_dot(k_n, _pad_head(qs, kv)) + bias, sink)
            dp = _dot(v_n, _pad_head(dos, kv))
            delta = jnp.sum(p * dp, axis=0, keepdims=True)
            ds = (p * (dp - delta)).astype(BF16)
            dsink = p_sink * delta
            dq = _dot(k_t[kv_rows], ds) * SCORE_SCALE
            for j, hh in enumerate(heads):
                gsink_ref[hh:hh + 1, :] -= dsink[:, j * BLK:(j + 1) * BLK]
                dq_parts.append(dq[:, j * BLK:(j + 1) * BLK])
            dk.append(_dot(qs, ds, NT))
            dv.append(_dot(dos, p.astype(BF16), NT))
        dh_ref[COL_Q:COL_Q + D_ATTN, :] = _rope_t(jnp.concatenate(dq_parts, axis=0), cosc, sinc, bwd=True)
        dk_all = jnp.concatenate(dk, axis=0)
        dv_all = jnp.concatenate(dv, axis=0)
        dh_ref[COL_K:COL_K + D_KV, :] = _rope_t(dk_all[:, BLK:], cosc, sinc, bwd=True)
        dh_ref[COL_K + D_KV:COL_K + 2 * D_KV, :] = dv_all[:, BLK:]
        dkvp_ref[0:D_KV, :] = _rope_t(dk_all[:, :BLK], cosp, sinp, bwd=True)
        dkvp_ref[D_KV:2 * D_KV, :] = dv_all[:, :BLK]

        @pl.when(i == nb - 1)
        def _():
            causal = _causal()
            for hh in range(N_HEADS):
                gwsb_ref[hh] = jnp.where(causal, gws_ref[hh], 0.0).astype(BF16)
            gvln_ref[...] = jnp.zeros_like(gvln_ref)
            gvln_ref[0:1, :] = jnp.sum(dg_acc[...].T, axis=0, keepdims=True)
            gvln_ref[1:2, :] = jnp.sum(db_acc[...].T, axis=0, keepdims=True)

    full = lambda shape: pl.BlockSpec(shape, lambda i: (0,) * len(shape))
    return _carry(
        body, name="mixer_bwd", grid=(nb,), comms=comms,
        in_specs=[pl.BlockSpec(memory_space=pltpu.SMEM), pl.BlockSpec((D_GMLP + D_ATTN, BLK), lambda i: (0, i))]
        + _h_specs() + _table_specs()
        + [full((N_HEADS, BLK, BLK)), full((N_HEADS, BLK)), full((D_GMLP, 1)), full((D_GMLP, 1)), BIAS_SPEC],
        out_specs=[pl.BlockSpec((D_IN, BLK), lambda i: (0, i)),
                   pl.BlockSpec((2 * D_KV, BLK), lambda i: (0, (i + nb - 1) % nb)),
                   full((N_HEADS, BLK, BLK)), full((N_HEADS, BLK)), full((8, D_GMLP)), full((N_HEADS, LANES))],
        out_shape=[jax.ShapeDtypeStruct((D_IN, t_tok), F32), jax.ShapeDtypeStruct((2 * D_KV, t_tok), F32),
                   jax.ShapeDtypeStruct((N_HEADS, BLK, BLK), BF16), jax.ShapeDtypeStruct((N_HEADS, BLK), F32),
                   jax.ShapeDtypeStruct((8, D_GMLP), F32), jax.ShapeDtypeStruct((N_HEADS, LANES), F32)],
        scratch_shapes=[pltpu.VMEM((D_GMLP, BLK), F32), pltpu.VMEM((D_GMLP, BLK), F32), pltpu.VMEM((N_HEADS, BLK, BLK), BF16),
                        pltpu.VMEM((N_HEADS, BLK, BLK), F32)],
        args=(sinks, dcat_t, h_t, h_t, h_t, h_t, h_t, cos_t, sin_t, cos_t, sin_t, w_spatial, b_spatial, vln_g, vln_b, band_bias))


def _proj_in_dgrad(dh_t, dkvp_t, dz1, w_in_t):
    t_tok, d = dz1.shape
    d_in = dh_t.shape[0]
    tm = min(512, t_tok)

    def body(dh_ref, dkvp_ref, dz1_ref, w_ref, dx_ref, dhb_ref):
        dhb = jnp.concatenate([dh_ref[0:COL_K, :], dh_ref[COL_K:, :] + dkvp_ref[...]], axis=0).astype(BF16)
        dhb_ref[...] = dhb
        dx_ref[...] = ALPHA * dz1_ref[...] + _dot(dhb, w_ref[...], TN)

    return _carry(
        body, name="proj_in_dgrad", grid=(t_tok // tm,),
        in_specs=[pl.BlockSpec((d_in, tm), lambda i: (0, i)), pl.BlockSpec((2 * D_KV, tm), lambda i: (0, i)),
                  pl.BlockSpec((tm, d), lambda i: (i, 0)), pl.BlockSpec((d_in, d), lambda i: (0, 0))],
        out_specs=[pl.BlockSpec((tm, d), lambda i: (i, 0)), pl.BlockSpec((d_in, tm), lambda i: (0, i))],
        out_shape=[jax.ShapeDtypeStruct((t_tok, d), F32), jax.ShapeDtypeStruct((d_in, t_tok), BF16)],
        args=(dh_t, dkvp_t, dz1, w_in_t))[0]


def _proj_in_wgrad(dhb_t, xb, comms=()):
    t_tok, d = xb.shape
    d_in = dhb_t.shape[0]
    tm = min(1024, t_tok)

    def body(dhb_ref, xb_ref, gw_ref):
        @pl.when(pl.program_id(0) == 0)
        def _():
            gw_ref[...] = jnp.zeros_like(gw_ref)

        gw_ref[...] += _dot(dhb_ref[...], xb_ref[...])

    return _carry(
        body, name="proj_in_wgrad", grid=(t_tok // tm,), comms=comms,
        in_specs=[pl.BlockSpec((d_in, tm), lambda i: (0, i)), pl.BlockSpec((tm, d), lambda i: (i, 0))],
        out_specs=[pl.BlockSpec((d_in, d), lambda i: (0, 0))],
        out_shape=[jax.ShapeDtypeStruct((d_in, d), F32)],
        args=(dhb_t, xb))


def _adamw(w, g, m, v):
    m = ADAM_B1 * m + (1.0 - ADAM_B1) * g
    v = ADAM_B2 * v + (1.0 - ADAM_B2) * (g * g)
    m_hat = m / (1.0 - ADAM_B1 ** ADAM_STEP)
    v_hat = v / (1.0 - ADAM_B2 ** ADAM_STEP)
    delta = -ADAM_LR * (m_hat / (jnp.sqrt(v_hat) + ADAM_EPS) + ADAM_WD * w)
    return delta, m, v


def _row_tiled(name, own, recv, extra, n_out, finish, comms=()):
    r, c = own.shape
    recv = [] if recv is None else list(recv)
    k = max(len(recv), 1)
    n = max(k, -(-r // 512))
    tr, per = r // n, n // k
    blk = pl.BlockSpec((tr, c), lambda i: (i, 0))

    def body(own_ref, *refs):
        recv_refs, rest = refs[:len(recv)], refs[len(recv):]
        ins, outs = rest[:len(extra)], rest[len(extra):]

        def tile(recv_ref):
            g = own_ref[...]
            if recv_ref is not None:
                g = ((g + recv_ref[0].astype(F32)) + recv_ref[1].astype(F32)) + recv_ref[2].astype(F32)
            for o_ref, val in zip(outs, finish(g, *[a[...] for a in ins])):
                o_ref[...] = val

        if len(recv) <= 1:
            tile(recv_refs[0] if recv else None)
        else:
            for p in range(k):
                pl.when(pl.program_id(0) // per == p)(functools.partial(tile, recv_refs[p]))

    recv_specs = [pl.BlockSpec((3, tr, c), lambda i, p=p: (0, jnp.clip(i - p * per, 0, per - 1), 0)) for p in range(len(recv))]
    return _carry(
        body, name=name, grid=(n,), comms=comms,
        in_specs=[blk] + recv_specs + [blk] * len(extra),
        out_specs=[blk] * n_out, out_shape=[jax.ShapeDtypeStruct((r, c), F32)] * n_out,
        args=(own, *recv, *extra))


def _adamw_shard(name, own, recv, w, m, v, comms=()):
    def finish(g, w_t, m_t, v_t):
        return (g,) + _adamw(w_t, g, m_t, v_t)

    return _row_tiled(name, own, recv, (w, m, v), 4, finish, comms)


def _sum_partials(name, own, recv):
    return _row_tiled(name, own, recv, (), 1, lambda g: (g,))[0][0]


VEC_VLN, VEC_LN1G, VEC_LN1B, VEC_LN2G, VEC_LN2B, VEC_SINK, VEC_LOSS, VEC_BSP, VEC_ROWS = 0, 1, 2, 3, 4, 5, 6, 8, 16


def _adamw_small(parts_w, parts_vec, params):
    n = parts_w.shape[0]
    flat = [a for p in params for a in p]
    shapes = [p[0].shape for p in params]

    def grads(gw, gv):
        return [gw, gv[VEC_VLN:VEC_VLN + 1, 0:D_GMLP], gv[VEC_VLN:VEC_VLN + 1, D_GMLP:2 * D_GMLP],
                gv[VEC_BSP:VEC_BSP + N_HEADS, 0:BLK], gv[VEC_LN1G:VEC_LN1G + 1], gv[VEC_LN1B:VEC_LN1B + 1],
                gv[VEC_LN2G:VEC_LN2G + 1], gv[VEC_LN2B:VEC_LN2B + 1], gv[VEC_SINK:VEC_SINK + 1, 0:N_HEADS]]

    def body(pw_ref, pv_ref, *refs):
        ins, outs = refs[:len(flat)], refs[len(flat):]
        gw, gv = pw_ref[0].astype(F32), pv_ref[0]
        for k in range(1, n):
            gw, gv = gw + pw_ref[k].astype(F32), gv + pv_ref[k]
        for i, g in enumerate(grads(gw, gv)):
            w_ref, m_ref, v_ref = ins[3 * i:3 * i + 3]
            delta, m_new, v_new = _adamw(w_ref[...], g, m_ref[...], v_ref[...])
            for o_ref, val in zip(outs[4 * i:4 * i + 4], (g, delta, m_new, v_new)):
                o_ref[...] = val
        outs[-1][...] = gv[VEC_LOSS:VEC_LOSS + 1, 0:LANES]

    whole = lambda shape: pl.BlockSpec(shape, lambda i: (0,) * len(shape))
    res = _carry(
        body, name="adamw_small", grid=(1,),
        in_specs=[whole(parts_w.shape), whole(parts_vec.shape)] + [whole(a.shape) for a in flat],
        out_specs=[whole(s) for s in shapes for _ in range(4)] + [whole((1, LANES))],
        out_shape=[jax.ShapeDtypeStruct(s, F32) for s in shapes for _ in range(4)] + [jax.ShapeDtypeStruct((1, LANES), F32)],
        args=(parts_w, parts_vec, *flat))[0]
    return [res[4 * i:4 * i + 4] for i in range(len(params))], res[-1]


def _pair_sum(name, parts, recv, core_chip):
    _, r, c = parts.shape
    tr = r if r <= 512 else 512

    def body(cc_ref, a_ref, b_ref, wire_ref, own_ref):
        s = a_ref[...] + b_ref[...]
        wire_ref[...] = s.astype(BF16)

        @pl.when(pl.program_id(1) == cc_ref[1])
        def _():
            own_ref[...] = s

    return _carry(
        body, name=name, grid=(r // tr, 4), prefetch=(core_chip,),
        in_specs=[pl.BlockSpec((None, tr, c), lambda i, q, cc: (2 * q + cc[0], i, 0)),
                  pl.BlockSpec((None, tr, c), lambda i, q, cc: (q, i, 0))],
        out_specs=[pl.BlockSpec((None, tr, c), lambda i, q, cc: (q, i, 0)), pl.BlockSpec((tr, c), lambda i, q, cc: (i, 0))],
        out_shape=[jax.ShapeDtypeStruct((4, r, c), BF16), jax.ShapeDtypeStruct((r, c), F32)],
        args=(parts, recv))[0]


def kernel(x, positions, w_in, v_ln_g, v_ln_b, w_spatial, b_spatial, sinks, w_out, ln1_g, ln1_b, w_ff1, w_ff2, ln2_g, ln2_b, loss_target, m_w_in, m_v_ln_g, m_v_ln_b, m_w_spatial, m_b_spatial, m_sinks, m_w_out, m_ln1_g, m_ln1_b, m_w_ff1, m_w_ff2, m_ln2_g, m_ln2_b, v_w_in, v_v_ln_g, v_v_ln_b, v_w_spatial, v_b_spatial, v_sinks, v_w_out, v_ln1_g, v_ln1_b, v_w_ff1, v_w_ff2, v_ln2_g, v_ln2_b):
    _, t_tok, d = x.shape
    xi, yi, ci = _place()
    core_chip = jnp.stack([ci, 2 * xi + yi]).astype(jnp.int32)
    x2 = x.reshape(t_tok, d)
    target = loss_target.reshape(t_tok, d)
    inv_freq = ROPE_THETA ** (-jnp.arange(0, HEAD_DIM, 2, dtype=F32) / HEAD_DIM)
    wsp, bsp, sink_vec = w_spatial[0], b_spatial[0], sinks[0]
    vg_col, vb_col = v_ln_g.reshape(D_GMLP, 1), v_ln_b.reshape(D_GMLP, 1)
    big = {"in": w_in[0], "out": w_out[0], "ff1": w_ff1[0], "ff2": w_ff2[0]}
    half1, half2 = big["ff1"].shape[1] // 2, big["ff2"].shape[0] // 2
    w1_mine = [big["ff1"][:, :half1].astype(BF16), big["ff1"][:, half1:].astype(BF16)]
    w2_mine = [big["ff2"][:half2].astype(BF16), big["ff2"][half2:].astype(BF16)]

    (cos_t, sin_t), ((g_in,),) = _rope_tables(
        positions, jnp.tile(inv_freq, 2).reshape(HEAD_DIM, 1), comms=[_gather_comm([big["in"].T.astype(BF16)])])
    w_in_t = g_in.reshape(D_IN, d)
    (h_t, xb), ((g_out, w1_a),) = _proj_in(x2, w_in_t, comms=[_gather_comm([big["out"].astype(BF16), w1_mine[0]])])
    w_out_b = g_out.reshape(-1, d)
    band_bias = _band_bias()
    (cat_t,), ((w1_b, w2_a),) = _mixer_fwd(h_t, cos_t, sin_t, wsp, bsp, vg_col, vb_col, sink_vec, band_bias,
                                           comms=[_gather_comm([w1_mine[1], w2_mine[0]])])
    (z1,), ((w2_b,),) = _proj_out(cat_t, x2, w_out_b, comms=[_gather_comm([w2_mine[1]])])
    act_b, dpre_b, x1b, dz2b, dz1, stats = _ffn_fwd_bwd(z1, target, [w1_a, w1_b], [w2_a, w2_b], ln1_g, ln1_b, ln2_g, ln2_b)

    (p_ff1,), _ = _ffn_wgrad1(x1b, dpre_b, N_DEV)
    (p_ff2,), ((s_ff1,),) = _ffn_wgrad2(act_b, dz2b, N_DEV, comms=[_sibling_comm([p_ff1])])
    wire_ff1, own_ff1 = _pair_sum("pair_sum_ff1", p_ff1, s_ff1, core_chip)
    rows1 = wire_ff1.shape[1] // 2
    (dcat_t, gw_out), ((s_ff2,), (r_ff1a,)) = _proj_out_bwd(
        dz1, cat_t, w_out_b, comms=[_sibling_comm([p_ff2]), _chips_comm([wire_ff1], rows=(0, rows1))])
    wire_ff2, own_ff2 = _pair_sum("pair_sum_ff2", p_ff2, s_ff2, core_chip)
    p_out = gw_out.reshape(N_DEV, -1, d)
    (dh_t, dkvp_t, g_wsp, g_bsp, g_vln, g_sink), ((r_ff1b,), (r_ff2,), (s_out,)) = _mixer_bwd(
        dcat_t, h_t, cos_t, sin_t, wsp, bsp, vg_col, vb_col, sink_vec, band_bias,
        comms=[_chips_comm([wire_ff1], rows=(rows1, rows1)), _chips_comm([wire_ff2]), _sibling_comm([p_out])])
    wire_out, own_out = _pair_sum("pair_sum_out", p_out, s_out, core_chip)
    grad_x, dhb_t = _proj_in_dgrad(dh_t, dkvp_t, dz1, w_in_t)
    sink_row = jnp.pad(g_sink.sum(axis=1).reshape(1, N_HEADS), ((0, 0), (0, d - N_HEADS)))
    small_vec = jnp.concatenate([g_vln[0:2].reshape(1, d), stats[0:4], sink_row, stats[4:5], jnp.zeros((1, d), F32),
                                 jnp.pad(g_bsp, ((0, 0), (0, d - BLK)))], axis=0)
    (gw_in_t,), ((parts_w, parts_vec), (r_out,)) = _proj_in_wgrad(
        dhb_t, xb, comms=[_gather_comm([g_wsp.reshape(-1, BLK), small_vec]), _chips_comm([wire_out])])
    p_in = gw_in_t.reshape(N_DEV, -1, d)

    out_out, ((s_in,),) = _adamw_shard("adamw_out", own_out, [r_out], big["out"], m_w_out[0], v_w_out[0], comms=[_sibling_comm([p_in])])
    wire_in, own_in = _pair_sum("pair_sum_in", p_in, s_in, core_chip)
    rows_in = wire_in.shape[1] // 2
    ff1_out, ((r_in_a,),) = _adamw_shard("adamw_ff1", own_ff1, [r_ff1a, r_ff1b], big["ff1"], m_w_ff1[0], v_w_ff1[0],
                                         comms=[_chips_comm([wire_in], rows=(0, rows_in))])
    ff2_out, ((r_in_b,),) = _adamw_shard("adamw_ff2", own_ff2, [r_ff2], big["ff2"], m_w_ff2[0], v_w_ff2[0],
                                         comms=[_chips_comm([wire_in], rows=(rows_in, rows_in))])
    g_in_t = _sum_partials("sum_in", own_in, [r_in_a, r_in_b])
    in_out, _ = _adamw_shard("adamw_in", g_in_t.T, None, big["in"], m_w_in[0], v_w_in[0])
    small = [(w_spatial, m_w_spatial, v_w_spatial), (v_ln_g, m_v_ln_g, v_v_ln_g), (v_ln_b, m_v_ln_b, v_v_ln_b),
             (b_spatial, m_b_spatial, v_b_spatial), (ln1_g, m_ln1_g, v_ln1_g), (ln1_b, m_ln1_b, v_ln1_b),
             (ln2_g, m_ln2_g, v_ln2_g), (ln2_b, m_ln2_b, v_ln2_b), (sinks, m_sinks, v_sinks)]
    views = [(-1, BLK), None, None, (N_HEADS, BLK)] + [None] * 5
    small_res, loss_row = _adamw_small(parts_w, parts_vec, [
        tuple(a if vw is None else a.reshape(vw) for a in p) for p, vw in zip(small, views)])
    small_out = [[o.reshape(p[0].shape) for o in res] for res, p in zip(small_res, small)]
    loss = loss_row[0, 0]

    big_out = {0: in_out, 6: out_out, 9: ff1_out, 10: ff2_out}
    small_slot = {3: 0, 1: 1, 2: 2, 4: 3, 7: 4, 8: 5, 11: 6, 12: 7, 5: 8}
    outs = [loss, grad_x.reshape(x.shape)]
    for kind in range(4):
        for wi in range(13):
            outs.append(big_out[wi][kind][None] if wi in big_out else small_out[small_slot[wi]][kind])
    return tuple(outs)
```

```python
import functools
import math

import jax
import jax.numpy as jnp
from jax import lax
from jax.experimental import pallas as pl
from jax.experimental.pallas import tpu as pltpu

F32 = jnp.float32
BF16 = jnp.bfloat16
MESH = pl.DeviceIdType.MESH

HEAD_DIM = 64
N_HEADS = 8
N_KV_HEADS = 2
BLK = 128
D_GMLP = N_HEADS * HEAD_DIM
D_ATTN = N_HEADS * HEAD_DIM
D_KV = N_KV_HEADS * HEAD_DIM
D_IN = 2 * D_GMLP + D_ATTN + 2 * D_KV
COL_U, COL_V, COL_Q, COL_K = 0, D_GMLP, 2 * D_GMLP, 2 * D_GMLP + D_ATTN
ROPE_THETA = 10000.0
LN_EPS = 1e-5
ALPHA = 2.0 ** 0.25
NEG_INF = -1e30
SCORE_SCALE = 1.0 / math.sqrt(HEAD_DIM)
ADAM_LR, ADAM_B1, ADAM_B2, ADAM_EPS, ADAM_WD, ADAM_STEP = 0.001, 0.9, 0.999, 1e-08, 0.01, 10
N_DEV = 8
LANES = 128
VMEM_LIMIT = 56 * 1024 * 1024

NT = (((1,), (1,)), ((), ()))
TN = (((0,), (0,)), ((), ()))


def _params(*sem):
    return pltpu.CompilerParams(dimension_semantics=sem, vmem_limit_bytes=VMEM_LIMIT)


def _dot(a, b, dims=None):
    if dims is None:
        return jnp.dot(a, b, preferred_element_type=F32)
    return lax.dot_general(a, b, dims, preferred_element_type=F32)


def _mean(a):
    return jnp.mean(a, axis=-1, keepdims=True)


def _ln_fwd(z, g, b):
    zc = z - _mean(z)
    rstd = lax.rsqrt(_mean(zc * zc) + LN_EPS)
    xhat = zc * rstd
    return xhat * g + b, xhat, rstd


def _ln_bwd(dy, xhat, rstd, g):
    dxhat = dy * g
    return rstd * (dxhat - _mean(dxhat) - xhat * _mean(dxhat * xhat))


_GELU_C = math.sqrt(2.0 / math.pi)


def _gelu(x):
    t = jnp.tanh(_GELU_C * (x + 0.044715 * (x * x * x)))
    return 0.5 * x * (1.0 + t)


def _gelu_and_grad(x):
    x2 = x * x
    t = jnp.tanh(_GELU_C * (x + 0.044715 * (x2 * x)))
    hx, ht = 0.5 * x, 0.5 * (1.0 + t)
    return x * ht, ht + hx * (1.0 - t * t) * (_GELU_C * (1.0 + 3.0 * 0.044715 * x2))


def _mean0(a):
    return jnp.mean(a, axis=0, keepdims=True)


def _ln_fwd_t(z, g, b):
    zc = z - _mean0(z)
    rstd = lax.rsqrt(_mean0(zc * zc) + LN_EPS)
    xhat = zc * rstd
    return xhat * g + b, xhat, rstd


def _ln_bwd_t(dy, xhat, rstd, g):
    dxhat = dy * g
    return rstd * (dxhat - _mean0(dxhat) - xhat * _mean0(dxhat * xhat))


def _rope_t(t, cos, sin_signed, bwd=False):
    half = HEAD_DIM // 2
    outs = []
    for r in range(0, t.shape[0], HEAD_DIM):
        th = t[r:r + HEAD_DIM]
        sw = jnp.concatenate([th[half:], th[:half]], axis=0) * sin_signed
        outs.append(th * cos - sw if bwd else th * cos + sw)
    return jnp.concatenate(outs, axis=0)


ANY = pl.BlockSpec(memory_space=pl.ANY)


def _place():
    return lax.axis_index("x"), lax.axis_index("y"), lax.axis_index("c")


class _Comm:
    def __init__(self, ins, outs, sems, start, finish):
        self.ins, self.outs, self.sems, self.start, self.finish = ins, outs, sems, start, finish


def _gather_comm(arrs):
    n = len(arrs)

    def parts(ins, outs, sems):
        send_sems, recv_sems, local_sems = sems
        x, y, c = _place()
        me, sibling = (x, y, c), (x, y, 1 - c)
        chips = [(1 - x, y), (x, 1 - y), (1 - x, 1 - y)]

        def copy(a, k, block, to, src=None):
            px, py, pc = block
            dst = outs[a].at[4 * px + 2 * py + pc]
            return pltpu.make_async_remote_copy(
                src_ref=dst if src is None else src, dst_ref=dst,
                send_sem=send_sems.at[a, k], recv_sem=recv_sems.at[a, k], device_id=to, device_id_type=MESH)

        mine = [pltpu.make_async_copy(ins[a], outs[a].at[4 * x + 2 * y + c], local_sems.at[a]) for a in range(n)]
        first = []
        for a in range(n):
            first.append(copy(a, 0, me, sibling, src=ins[a]))
            first += [copy(a, 1 + j, me, (*chip, c), src=ins[a]) for j, chip in enumerate(chips)]
        return copy, mine, first, me, sibling, chips, c

    def start(ins, outs, sems):
        _, mine, first, *_ = parts(ins, outs, sems)
        for cp in mine + first:
            cp.start()

    def finish(ins, outs, sems):
        copy, mine, first, me, sibling, chips, c = parts(ins, outs, sems)
        passed = []
        for j, chip in enumerate(chips):
            for a in range(n):
                copy(a, 1 + j, (*chip, c), me).wait_recv()
                fwd = copy(a, 4 + j, (*chip, c), sibling)
                fwd.start()
                passed.append(fwd)
        for a in range(n):
            copy(a, 0, sibling, me).wait_recv()
        for j, chip in enumerate(chips):
            for a in range(n):
                copy(a, 4 + j, (*chip, 1 - c), me).wait_recv()
        for cp in first + passed:
            cp.wait_send()
        for cp in mine:
            cp.wait()

    return _Comm(list(arrs), [jax.ShapeDtypeStruct((N_DEV,) + a.shape, a.dtype) for a in arrs],
                 [pltpu.SemaphoreType.DMA((n, 7)), pltpu.SemaphoreType.DMA((n, 7)), pltpu.SemaphoreType.DMA((n,))],
                 start, finish)


def _sibling_comm(parts):
    n = len(parts)

    def copies(ins, outs, sems):
        x, y, c = _place()
        return [pltpu.make_async_remote_copy(
            src_ref=ins[a].at[2 * q + (1 - c)], dst_ref=outs[a].at[q],
            send_sem=sems[0].at[a, q], recv_sem=sems[1].at[a, q],
            device_id=(x, y, 1 - c), device_id_type=MESH) for a in range(n) for q in range(4)]

    return _Comm(list(parts), [jax.ShapeDtypeStruct((4,) + p.shape[1:], p.dtype) for p in parts],
                 [pltpu.SemaphoreType.DMA((n, 4)), pltpu.SemaphoreType.DMA((n, 4))],
                 lambda *r: [cp.start() for cp in copies(*r)], lambda *r: [cp.wait() for cp in copies(*r)])


def _chips_comm(chip_parts, rows=None):
    n = len(chip_parts)
    r0, nr = (0, None) if rows is None else rows

    def copies(ins, outs, sems):
        x, y, c = _place()
        chips = [(1 - x, y), (x, 1 - y), (1 - x, 1 - y)]
        src = lambda a, q: ins[a].at[q] if rows is None else ins[a].at[q, pl.ds(r0, nr)]
        return [pltpu.make_async_remote_copy(
            src_ref=src(a, 2 * px + py), dst_ref=outs[a].at[k],
            send_sem=sems[0].at[a, k], recv_sem=sems[1].at[a, k],
            device_id=(px, py, c), device_id_type=MESH) for a in range(n) for k, (px, py) in enumerate(chips)]

    shape = lambda p: (3,) + p.shape[1:] if rows is None else (3, nr) + p.shape[2:]
    return _Comm(list(chip_parts), [jax.ShapeDtypeStruct(shape(p), p.dtype) for p in chip_parts],
                 [pltpu.SemaphoreType.DMA((n, 3)), pltpu.SemaphoreType.DMA((n, 3))],
                 lambda *r: [cp.start() for cp in copies(*r)], lambda *r: [cp.wait() for cp in copies(*r)])


def _carry(body, *, name, grid, in_specs, out_specs, out_shape, args, comms=(), scratch_shapes=(), prefetch=()):
    n_pre, n_in, n_out, n_scr = len(prefetch), len(in_specs), len(out_specs), len(scratch_shapes)
    c_ins = [a for cm in comms for a in cm.ins]
    c_outs = [s for cm in comms for s in cm.outs]
    c_sems = [s for cm in comms for s in cm.sems]

    def wrapped(*refs):
        pre, refs = refs[:n_pre], refs[n_pre:]
        ins, refs = refs[:n_in], refs[n_in:]
        cins, refs = refs[:len(c_ins)], refs[len(c_ins):]
        outs, refs = refs[:n_out], refs[n_out:]
        couts, refs = refs[:len(c_outs)], refs[len(c_outs):]
        scr, sems = refs[:n_scr], refs[n_scr:]
        groups, i0, o0, s0 = [], 0, 0, 0
        for cm in comms:
            groups.append((cm, cins[i0:i0 + len(cm.ins)], couts[o0:o0 + len(cm.outs)], sems[s0:s0 + len(cm.sems)]))
            i0, o0, s0 = i0 + len(cm.ins), o0 + len(cm.outs), s0 + len(cm.sems)
        first = pl.program_id(0) == 0
        last = pl.program_id(0) == grid[0] - 1
        for ax in range(1, len(grid)):
            first = first & (pl.program_id(ax) == 0)
            last = last & (pl.program_id(ax) == grid[ax] - 1)
        if comms:
            @pl.when(first)
            def _():
                for cm, ci, co, cs in groups:
                    cm.start(ci, co, cs)
        body(*pre, *ins, *outs, *scr)
        if comms:
            @pl.when(last)
            def _():
                for cm, ci, co, cs in groups:
                    cm.finish(ci, co, cs)

    grid_spec = pltpu.PrefetchScalarGridSpec(
        num_scalar_prefetch=n_pre, grid=grid,
        in_specs=list(in_specs) + [ANY] * len(c_ins), out_specs=list(out_specs) + [ANY] * len(c_outs),
        scratch_shapes=list(scratch_shapes) + c_sems)
    res = pl.pallas_call(
        wrapped, name=name, grid_spec=grid_spec, out_shape=list(out_shape) + c_outs,
        compiler_params=_params(*(["arbitrary"] * len(grid))),
    )(*prefetch, *args, *c_ins)
    outs, rest, per_comm = res[:n_out], res[n_out:], []
    for cm in comms:
        per_comm.append(rest[:len(cm.outs)])
        rest = rest[len(cm.outs):]
    return outs, per_comm


def _rope_tables(pos_row, inv_freq_col, comms=()):
    t_tok = pos_row.shape[1]
    tm = min(512, t_tok)

    def body(pos_ref, invf_ref, cos_ref, sin_ref):
        ang = pos_ref[...].astype(F32) * invf_ref[...]
        row = lax.broadcasted_iota(jnp.int32, ang.shape, 0)
        cos_ref[...] = jnp.cos(ang)
        sin_ref[...] = jnp.sin(ang) * jnp.where(row < HEAD_DIM // 2, -1.0, 1.0)

    return _carry(
        body, name="rope_tables", grid=(t_tok // tm,), comms=comms,
        in_specs=[pl.BlockSpec((1, tm), lambda i: (0, i)), pl.BlockSpec((HEAD_DIM, 1), lambda i: (0, 0))],
        out_specs=[pl.BlockSpec((HEAD_DIM, tm), lambda i: (0, i))] * 2,
        out_shape=[jax.ShapeDtypeStruct((HEAD_DIM, t_tok), F32)] * 2,
        args=(pos_row, inv_freq_col))


def _proj_in(x2, w_in_t, comms=()):
    t_tok, d = x2.shape
    d_in = w_in_t.shape[0]
    tm = min(512, t_tok)

    def body(x_ref, w_ref, h_ref, xb_ref):
        xb = x_ref[...].astype(BF16)
        xb_ref[...] = xb
        h_ref[...] = _dot(w_ref[...], xb, NT)

    return _carry(
        body, name="proj_in", grid=(t_tok // tm,), comms=comms,
        in_specs=[pl.BlockSpec((tm, d), lambda i: (i, 0)), pl.BlockSpec((d_in, d), lambda i: (0, 0))],
        out_specs=[pl.BlockSpec((d_in, tm), lambda i: (0, i)), pl.BlockSpec((tm, d), lambda i: (i, 0))],
        out_shape=[jax.ShapeDtypeStruct((d_in, t_tok), F32), jax.ShapeDtypeStruct((t_tok, d), BF16)],
        args=(x2, w_in_t))


def _h_specs():
    kv_row = COL_K // (2 * D_KV)
    return [
        pl.BlockSpec((D_GMLP, BLK), lambda i: (0, i)),
        pl.BlockSpec((D_GMLP, BLK), lambda i: (1, i)),
        pl.BlockSpec((D_ATTN, BLK), lambda i: (2, i)),
        pl.BlockSpec((2 * D_KV, BLK), lambda i: (kv_row, i)),
        pl.BlockSpec((2 * D_KV, BLK), lambda i: (kv_row, jnp.maximum(i - 1, 0))),
    ]


def _table_specs():
    return [
        pl.BlockSpec((HEAD_DIM, BLK), lambda i: (0, i)),
        pl.BlockSpec((HEAD_DIM, BLK), lambda i: (0, i)),
        pl.BlockSpec((HEAD_DIM, BLK), lambda i: (0, jnp.maximum(i - 1, 0))),
        pl.BlockSpec((HEAD_DIM, BLK), lambda i: (0, jnp.maximum(i - 1, 0))),
    ]


def _band_bias():
    ki = lax.broadcasted_iota(jnp.int32, (2, 2 * BLK, BLK), 1)
    qi = lax.broadcasted_iota(jnp.int32, (2, 2 * BLK, BLK), 2)
    later = lax.broadcasted_iota(jnp.int32, (2, 2 * BLK, BLK), 0) > 0
    dist = qi + BLK - ki
    return jnp.where((dist >= 0) & (dist < BLK) & ((ki >= BLK) | later), 0.0, NEG_INF).astype(F32)


BIAS_SPEC = pl.BlockSpec((None, 2 * BLK, BLK), lambda i: (jnp.minimum(i, 1), 0, 0))


def _keys_values(kvc, kvp, cosc, sinc, cosp, sinp):
    kp, kc = _rope_t(kvp[:D_KV], cosp, sinp), _rope_t(kvc[:D_KV], cosc, sinc)
    k_t = jnp.concatenate([kp, kc], axis=1).astype(BF16)
    k_n = jnp.concatenate([kp.T, kc.T], axis=0).astype(BF16)
    v_t = jnp.concatenate([kvp[D_KV:], kvc[D_KV:]], axis=1).astype(BF16)
    return k_t, k_n, v_t


def _pad_head(th, kv):
    z = jnp.zeros_like(th)
    return jnp.concatenate([th, z] if kv == 0 else [z, th], axis=0)


def _group_lanes(parts):
    return jnp.concatenate(parts, axis=1)


def _softmax_sink_t(s, sink):
    m = jnp.maximum(jnp.max(s, axis=0, keepdims=True), sink)
    e = jnp.exp(s - m)
    es = jnp.exp(sink - m)
    r = 1.0 / (jnp.sum(e, axis=0, keepdims=True) + es)
    return e * r, es * r


def _causal():
    row = lax.broadcasted_iota(jnp.int32, (BLK, BLK), 0)
    col = lax.broadcasted_iota(jnp.int32, (BLK, BLK), 1)
    return row >= col


def _mask_w_once(wsp_ref, wm_scr):
    @pl.when(pl.program_id(0) == 0)
    def _():
        causal = _causal()
        for hh in range(N_HEADS):
            wm_scr[hh] = jnp.where(causal, wsp_ref[hh], 0.0).astype(BF16)


def _mixer_fwd(h_t, cos_t, sin_t, w_spatial, b_spatial, vln_g, vln_b, sinks, band_bias, comms=()):
    t_tok = h_t.shape[1]
    nb = t_tok // BLK
    group = N_HEADS // N_KV_HEADS

    def body(sinks_ref, u_ref, vg_ref, q_ref, kvc_ref, kvp_ref, cosc_ref, sinc_ref, cosp_ref, sinp_ref,
             wsp_ref, bsp_ref, g_ref, b_ref, bias_ref, cat_ref, wm_scr):
        _mask_w_once(wsp_ref, wm_scr)
        ua = _gelu(u_ref[...])
        vp, _, _ = _ln_fwd_t(_gelu(vg_ref[...]), g_ref[...], b_ref[...])
        vpb = vp.astype(BF16)
        for hh in range(N_HEADS):
            rows = slice(hh * HEAD_DIM, (hh + 1) * HEAD_DIM)
            mixed = _dot(vpb[rows], wm_scr[hh], NT) + bsp_ref[hh:hh + 1, :]
            cat_ref[rows, :] = (ua[rows] * mixed).astype(BF16)

        cosc, sinc = cosc_ref[...], sinc_ref[...]
        qr = (_rope_t(q_ref[...], cosc, sinc) * SCORE_SCALE).astype(BF16)
        _, k_n, v_t = _keys_values(kvc_ref[...], kvp_ref[...], cosc, sinc, cosp_ref[...], sinp_ref[...])
        bias = _group_lanes([bias_ref[...]] * group)
        for kv in range(N_KV_HEADS):
            heads = range(kv * group, (kv + 1) * group)
            qs = _group_lanes([qr[hh * HEAD_DIM:(hh + 1) * HEAD_DIM] for hh in heads])
            sink = _group_lanes([jnp.full((1, BLK), sinks_ref[hh], F32) for hh in heads])
            p, _ = _softmax_sink_t(_dot(k_n, _pad_head(qs, kv)) + bias, sink)
            o = _dot(v_t[kv * HEAD_DIM:(kv + 1) * HEAD_DIM], p.astype(BF16)).astype(BF16)
            for j, hh in enumerate(heads):
                cat_ref[D_GMLP + hh * HEAD_DIM:D_GMLP + (hh + 1) * HEAD_DIM, :] = o[:, j * BLK:(j + 1) * BLK]

    full = lambda shape: pl.BlockSpec(shape, lambda i: (0,) * len(shape))
    return _carry(
        body, name="mixer_fwd", grid=(nb,), comms=comms,
        in_specs=[pl.BlockSpec(memory_space=pltpu.SMEM)] + _h_specs() + _table_specs() + [
            full((N_HEADS, BLK, BLK)), full((N_HEADS, BLK)), full((D_GMLP, 1)), full((D_GMLP, 1)), BIAS_SPEC],
        out_specs=[pl.BlockSpec((D_GMLP + D_ATTN, BLK), lambda i: (0, i))],
        out_shape=[jax.ShapeDtypeStruct((D_GMLP + D_ATTN, t_tok), BF16)],
        scratch_shapes=[pltpu.VMEM((N_HEADS, BLK, BLK), BF16)],
        args=(sinks, h_t, h_t, h_t, h_t, h_t, cos_t, sin_t, cos_t, sin_t, w_spatial, b_spatial, vln_g, vln_b, band_bias))


def _proj_out(cat_t, x2, w_out_b, comms=()):
    t_tok, d = x2.shape
    tm = min(512, t_tok)

    def body(cat_ref, x_ref, w_ref, z_ref):
        z_ref[...] = ALPHA * x_ref[...] + _dot(cat_ref[...], w_ref[...], TN)

    return _carry(
        body, name="proj_out", grid=(t_tok // tm,), comms=comms,
        in_specs=[pl.BlockSpec((cat_t.shape[0], tm), lambda i: (0, i)), pl.BlockSpec((tm, d), lambda i: (i, 0)),
                  pl.BlockSpec(w_out_b.shape, lambda i: (0, 0))],
        out_specs=[pl.BlockSpec((tm, d), lambda i: (i, 0))],
        out_shape=[jax.ShapeDtypeStruct((t_tok, d), F32)],
        args=(cat_t, x2, w_out_b))


def _ffn_fwd_bwd(z1, target, w1_parts, w2_parts, ln1_g, ln1_b, ln2_g, ln2_b):
    t_tok, d = z1.shape
    n_part = len(w1_parts)
    n_chunk, _, fp = w1_parts[0].shape
    fc = n_part * fp
    f = n_chunk * fc
    tm = min(256, t_tok)

    def body(z1_ref, tgt_ref, *refs):
        w1_hbm, w2_hbm = refs[:n_part], refs[n_part:2 * n_part]
        (g1_ref, b1_ref, g2_ref, b2_ref, act_ref, dpre_ref, x1b_ref, dz2b_ref, dz1_ref, stats_ref,
         r_scr, w1_ref, w2_ref, w_sems) = refs[2 * n_part:]

        @pl.when(pl.program_id(0) == 0)
        def _():
            stats_ref[...] = jnp.zeros_like(stats_ref)
            loads = [pltpu.make_async_copy(w1_hbm[p], w1_ref.at[:, :, pl.ds(p * fp, fp)], w_sems.at[0, p]) for p in range(n_part)]
            loads += [pltpu.make_async_copy(w2_hbm[p], w2_ref.at[:, pl.ds(p * fp, fp), :], w_sems.at[1, p]) for p in range(n_part)]
            for cp in loads:
                cp.start()
            for cp in loads:
                cp.wait()

        g1, g2 = g1_ref[...], g2_ref[...]
        x1, xhat1, rstd1 = _ln_fwd(z1_ref[...], g1, b1_ref[...])
        x1b = x1.astype(BF16)
        x1b_ref[...] = x1b
        ff = jnp.zeros((tm, d), F32)
        for j in range(n_chunk):
            r = jnp.maximum(_dot(x1b, w1_ref[j]), 0.0)
            r_scr[:, j * fc:(j + 1) * fc] = r
            act = (r * r).astype(BF16)
            act_ref[:, j * fc:(j + 1) * fc] = act
            ff = ff + _dot(act, w2_ref[j])
        y, xhat2, rstd2 = _ln_fwd(ALPHA * x1 + ff, g2, b2_ref[...])
        diff = y - tgt_ref[...]
        loss = 0.5 * jnp.sum(jnp.sum(diff * diff, axis=-1, keepdims=True) / d, axis=0, keepdims=True)
        dy = diff / d
        dz2 = _ln_bwd(dy, xhat2, rstd2, g2)
        dz2b = dz2.astype(BF16)
        dz2b_ref[...] = dz2b
        dx1 = ALPHA * dz2
        for j in range(n_chunk):
            dpre = (_dot(dz2b, w2_ref[j], NT) * (2.0 * r_scr[:, j * fc:(j + 1) * fc])).astype(BF16)
            dpre_ref[:, j * fc:(j + 1) * fc] = dpre
            dx1 = dx1 + _dot(dpre, w1_ref[j], NT)
        dz1_ref[...] = _ln_bwd(dx1, xhat1, rstd1, g1)
        stats_ref[0:1, :] += jnp.sum(dx1 * xhat1, axis=0, keepdims=True)
        stats_ref[1:2, :] += jnp.sum(dx1, axis=0, keepdims=True)
        stats_ref[2:3, :] += jnp.sum(dy * xhat2, axis=0, keepdims=True)
        stats_ref[3:4, :] += jnp.sum(dy, axis=0, keepdims=True)
        stats_ref[4:5, :] += jnp.broadcast_to(loss, (1, d))

    tok = lambda w: pl.BlockSpec((tm, w), lambda i: (i, 0))
    vec = pl.BlockSpec((1, d), lambda i: (0, 0))
    return _carry(
        body, name="ffn_fwd_bwd", grid=(t_tok // tm,),
        in_specs=[tok(d), tok(d)] + [ANY] * (2 * n_part) + [vec, vec, vec, vec],
        out_specs=[tok(f), tok(f), tok(d), tok(d), tok(d), pl.BlockSpec((8, d), lambda i: (0, 0))],
        out_shape=[jax.ShapeDtypeStruct((t_tok, f), BF16), jax.ShapeDtypeStruct((t_tok, f), BF16),
                   jax.ShapeDtypeStruct((t_tok, d), BF16), jax.ShapeDtypeStruct((t_tok, d), BF16),
                   jax.ShapeDtypeStruct((t_tok, d), F32), jax.ShapeDtypeStruct((8, d), F32)],
        scratch_shapes=[pltpu.VMEM((tm, f), F32), pltpu.VMEM((n_chunk, d, fc), BF16), pltpu.VMEM((n_chunk, fc, d), BF16),
                        pltpu.SemaphoreType.DMA((2, n_part))],
        args=(z1, target, *w1_parts, *w2_parts, ln1_g, ln1_b, ln2_g, ln2_b))[0]


def _ffn_wgrad1(x1b, dpre_b, n_chunk, comms=()):
    t_tok, d = x1b.shape
    fc = dpre_b.shape[1] // n_chunk

    def body(x1_ref, dpre_ref, g_ref):
        g_ref[...] = _dot(x1_ref[...], dpre_ref[...], TN)

    return _carry(
        body, name="ffn_wgrad1", grid=(n_chunk,), comms=comms,
        in_specs=[pl.BlockSpec((t_tok, d), lambda j: (0, 0), pipeline_mode=pl.Buffered(1)),
                  pl.BlockSpec((t_tok, fc), lambda j: (0, j))],
        out_specs=[pl.BlockSpec((None, d, fc), lambda j: (j, 0, 0))],
        out_shape=[jax.ShapeDtypeStruct((n_chunk, d, fc), F32)],
        args=(x1b, dpre_b))


def _ffn_wgrad2(act_b, dz2b, n_chunk, comms=()):
    t_tok, d = dz2b.shape
    fc = act_b.shape[1] // n_chunk

    def body(act_ref, dz2_ref, g_ref):
        g_ref[...] = _dot(act_ref[...], dz2_ref[...], TN)

    return _carry(
        body, name="ffn_wgrad2", grid=(n_chunk,), comms=comms,
        in_specs=[pl.BlockSpec((t_tok, fc), lambda j: (0, j)),
                  pl.BlockSpec((t_tok, d), lambda j: (0, 0), pipeline_mode=pl.Buffered(1))],
        out_specs=[pl.BlockSpec((None, fc, d), lambda j: (j, 0, 0))],
        out_shape=[jax.ShapeDtypeStruct((n_chunk, fc, d), F32)],
        args=(act_b, dz2b))


def _proj_out_bwd(dz1, cat_t, w_out_b, comms=()):
    t_tok, d = dz1.shape
    d_mix = cat_t.shape[0]
    tm = min(512, t_tok)

    def body(dz1_ref, cat_ref, w_ref, dcat_ref, gw_ref):
        @pl.when(pl.program_id(0) == 0)
        def _():
            gw_ref[...] = jnp.zeros_like(gw_ref)

        dzb = dz1_ref[...].astype(BF16)
        dcat_ref[...] = _dot(w_ref[...], dzb, NT)
        gw_ref[...] += _dot(cat_ref[...], dzb)

    return _carry(
        body, name="proj_out_bwd", grid=(t_tok // tm,), comms=comms,
        in_specs=[pl.BlockSpec((tm, d), lambda i: (i, 0)), pl.BlockSpec((d_mix, tm), lambda i: (0, i)),
                  pl.BlockSpec((d_mix, d), lambda i: (0, 0))],
        out_specs=[pl.BlockSpec((d_mix, tm), lambda i: (0, i)), pl.BlockSpec((d_mix, d), lambda i: (0, 0))],
        out_shape=[jax.ShapeDtypeStruct((d_mix, t_tok), F32), jax.ShapeDtypeStruct((d_mix, d), F32)],
        args=(dz1, cat_t, w_out_b))


def _mixer_bwd(dcat_t, h_t, cos_t, sin_t, w_spatial, b_spatial, vln_g, vln_b, sinks, band_bias, comms=()):
    t_tok = h_t.shape[1]
    nb = t_tok // BLK
    group = N_HEADS // N_KV_HEADS

    def body(sinks_ref, dcat_ref, u_ref, vg_ref, q_ref, kvc_ref, kvp_ref, cosc_ref, sinc_ref, cosp_ref, sinp_ref,
             wsp_ref, bsp_ref, g_ref, b_ref, bias_ref, dh_ref, dkvc_ref, dkvp_ref, gwsb_ref, gbsp_ref, gvln_ref, gsink_ref,
             dg_acc, db_acc, wm_scr, gws_ref):
        i = pl.program_id(0)

        @pl.when(i == 0)
        def _():
            gws_ref[...] = jnp.zeros_like(gws_ref)
            gbsp_ref[...] = jnp.zeros_like(gbsp_ref)
            gsink_ref[...] = jnp.zeros_like(gsink_ref)
            dg_acc[...] = jnp.zeros_like(dg_acc)
            db_acc[...] = jnp.zeros_like(db_acc)

        _mask_w_once(wsp_ref, wm_scr)

        g = g_ref[...]
        ua, ua_grad = _gelu_and_grad(u_ref[...])
        vv, vv_grad = _gelu_and_grad(vg_ref[...])
        vp, vhat, rstd = _ln_fwd_t(vv, g, b_ref[...])
        vpb = vp.astype(BF16)
        da = dcat_ref[0:D_GMLP, :]
        dmixed = da * ua
        dvp_parts = []
        for hh in range(N_HEADS):
            rows = slice(hh * HEAD_DIM, (hh + 1) * HEAD_DIM)
            mixed = _dot(vpb[rows], wm_scr[hh], NT) + bsp_ref[hh:hh + 1, :]
            dh_ref[COL_U + hh * HEAD_DIM:COL_U + (hh + 1) * HEAD_DIM, :] = (da[rows] * mixed * ua_grad[rows]).astype(BF16)
            dm = dmixed[rows]
            dmb = dm.astype(BF16)
            gbsp_ref[hh:hh + 1, :] += jnp.sum(dm, axis=0, keepdims=True)
            gws_ref[hh] += _dot(dmb, vpb[rows], TN)
            dvp_parts.append(_dot(dmb, wm_scr[hh]))
        dvp = jnp.concatenate(dvp_parts, axis=0)
        dg_acc[...] += dvp * vhat
        db_acc[...] += dvp
        dh_ref[COL_V:COL_V + D_GMLP, :] = (_ln_bwd_t(dvp, vhat, rstd, g) * vv_grad).astype(BF16)

        cosc, sinc, cosp, sinp = cosc_ref[...], sinc_ref[...], cosp_ref[...], sinp_ref[...]
        qr = (_rope_t(q_ref[...], cosc, sinc) * SCORE_SCALE).astype(BF16)
        k_t, k_n, v_t = _keys_values(kvc_ref[...], kvp_ref[...], cosc, sinc, cosp, sinp)
        v_n = jnp.concatenate([kvp_ref[D_KV:, :].T, kvc_ref[D_KV:, :].T], axis=0).astype(BF16)
        bias = _group_lanes([bias_ref[...]] * group)
        dk, dv, dq_parts = [], [], []
        for kv in range(N_KV_HEADS):
            heads = range(kv * group, (kv + 1) * group)
            kv_rows = slice(kv * HEAD_DIM, (kv + 1) * HEAD_DIM)
            qs = _group_lanes([qr[hh * HEAD_DIM:(hh + 1) * HEAD_DIM] for hh in heads])
            dos = _group_lanes([dcat_ref[D_GMLP + hh * HEAD_DIM:D_GMLP + (hh + 1) * HEAD_DIM, :] for hh in heads]).astype(BF16)
            sink = _group_lanes([jnp.full((1, BLK), sinks_ref[hh], F32) for hh in heads])
            p, p_sink = _softmax_sink_t(_dot(k_n, _pad_head(qs, kv)) + bias, sink)
            dp = _dot(v_n, _pad_head(dos, kv))
            delta = jnp.sum(p * dp, axis=0, keepdims=True)
            ds = (p * (dp - delta)).astype(BF16)
            dsink = p_sink * delta
            dq = _dot(k_t[kv_rows], ds) * SCORE_SCALE
            for j, hh in enumerate(heads):
                gsink_ref[hh:hh + 1, :] -= dsink[:, j * BLK:(j + 1) * BLK]
                dq_parts.append(dq[:, j * BLK:(j + 1) * BLK])
            dk.append(_dot(qs, ds, NT))
            dv.append(_dot(dos, p.astype(BF16), NT))
        dh_ref[COL_Q:COL_Q + D_ATTN, :] = _rope_t(jnp.concatenate(dq_parts, axis=0), cosc, sinc, bwd=True).astype(BF16)
        dk_all = jnp.concatenate(dk, axis=0)
        dv_all = jnp.concatenate(dv, axis=0)
        dkvc_ref[0:D_KV, :] = _rope_t(dk_all[:, BLK:], cosc, sinc, bwd=True)
        dkvc_ref[D_KV:2 * D_KV, :] = dv_all[:, BLK:]
        dkvp_ref[0:D_KV, :] = _rope_t(dk_all[:, :BLK], cosp, sinp, bwd=True)
        dkvp_ref[D_KV:2 * D_KV, :] = dv_all[:, :BLK]

        @pl.when(i == nb - 1)
        def _():
            causal = _causal()
            for hh in range(N_HEADS):
                gwsb_ref[hh] = jnp.where(causal, gws_ref[hh], 0.0).astype(BF16)
            gvln_ref[...] = jnp.zeros_like(gvln_ref)
            gvln_ref[0:1, :] = jnp.sum(dg_acc[...].T, axis=0, keepdims=True)
            gvln_ref[1:2, :] = jnp.sum(db_acc[...].T, axis=0, keepdims=True)

    full = lambda shape: pl.BlockSpec(shape, lambda i: (0,) * len(shape))
    return _carry(
        body, name="mixer_bwd", grid=(nb,), comms=comms,
        in_specs=[pl.BlockSpec(memory_space=pltpu.SMEM), pl.BlockSpec((D_GMLP + D_ATTN, BLK), lambda i: (0, i))]
        + _h_specs() + _table_specs()
        + [full((N_HEADS, BLK, BLK)), full((N_HEADS, BLK)), full((D_GMLP, 1)), full((D_GMLP, 1)), BIAS_SPEC],
        out_specs=[pl.BlockSpec((COL_K, BLK), lambda i: (0, i)), pl.BlockSpec((2 * D_KV, BLK), lambda i: (0, i)),
                   pl.BlockSpec((2 * D_KV, BLK), lambda i: (0, (i + nb - 1) % nb)),
                   full((N_HEADS, BLK, BLK)), full((N_HEADS, BLK)), full((8, D_GMLP)), full((N_HEADS, LANES))],
        out_shape=[jax.ShapeDtypeStruct((COL_K, t_tok), BF16), jax.ShapeDtypeStruct((2 * D_KV, t_tok), F32),
                   jax.ShapeDtypeStruct((2 * D_KV, t_tok), F32),
                   jax.ShapeDtypeStruct((N_HEADS, BLK, BLK), BF16), jax.ShapeDtypeStruct((N_HEADS, BLK), F32),
                   jax.ShapeDtypeStruct((8, D_GMLP), F32), jax.ShapeDtypeStruct((N_HEADS, LANES), F32)],
        scratch_shapes=[pltpu.VMEM((D_GMLP, BLK), F32), pltpu.VMEM((D_GMLP, BLK), F32), pltpu.VMEM((N_HEADS, BLK, BLK), BF16),
                        pltpu.VMEM((N_HEADS, BLK, BLK), F32)],
        args=(sinks, dcat_t, h_t, h_t, h_t, h_t, h_t, cos_t, sin_t, cos_t, sin_t, w_spatial, b_spatial, vln_g, vln_b, band_bias))


def _proj_in_wgrad(dh_b, dkvc_t, dkvp_t, xb, comms=()):
    t_tok, d = xb.shape
    d_main, d_kv = dh_b.shape[0], dkvc_t.shape[0]
    tm = min(1024, t_tok)

    def body(dh_ref, dkvc_ref, dkvp_ref, xb_ref, dkvb_ref, gw_ref):
        @pl.when(pl.program_id(0) == 0)
        def _():
            gw_ref[...] = jnp.zeros_like(gw_ref)

        dkvb = (dkvc_ref[...] + dkvp_ref[...]).astype(BF16)
        dkvb_ref[...] = dkvb
        gw_ref[0:d_main, :] += _dot(dh_ref[...], xb_ref[...])
        gw_ref[d_main:, :] += _dot(dkvb, xb_ref[...])

    tok = lambda rows: pl.BlockSpec((rows, tm), lambda i: (0, i))
    return _carry(
        body, name="proj_in_wgrad", grid=(t_tok // tm,), comms=comms,
        in_specs=[tok(d_main), tok(d_kv), tok(d_kv), pl.BlockSpec((tm, d), lambda i: (i, 0))],
        out_specs=[tok(d_kv), pl.BlockSpec((d_main + d_kv, d), lambda i: (0, 0))],
        out_shape=[jax.ShapeDtypeStruct((d_kv, t_tok), BF16), jax.ShapeDtypeStruct((d_main + d_kv, d), F32)],
        args=(dh_b, dkvc_t, dkvp_t, xb))


def _proj_in_dgrad(dh_b, dkv_b, dz1, w_in_t, comms=()):
    t_tok, d = dz1.shape
    d_main, d_kv = dh_b.shape[0], dkv_b.shape[0]
    tm = min(512, t_tok)

    def body(dh_ref, dkv_ref, dz1_ref, w_ref, dx_ref):
        dx_ref[...] = (ALPHA * dz1_ref[...] + _dot(dh_ref[...], w_ref[0:d_main, :], TN)
                       + _dot(dkv_ref[...], w_ref[d_main:, :], TN))

    return _carry(
        body, name="proj_in_dgrad", grid=(t_tok // tm,), comms=comms,
        in_specs=[pl.BlockSpec((d_main, tm), lambda i: (0, i)), pl.BlockSpec((d_kv, tm), lambda i: (0, i)),
                  pl.BlockSpec((tm, d), lambda i: (i, 0)), pl.BlockSpec((d_main + d_kv, d), lambda i: (0, 0))],
        out_specs=[pl.BlockSpec((tm, d), lambda i: (i, 0))],
        out_shape=[jax.ShapeDtypeStruct((t_tok, d), F32)],
        args=(dh_b, dkv_b, dz1, w_in_t))


def _adamw(w, g, m, v):
    m = ADAM_B1 * m + (1.0 - ADAM_B1) * g
    v = ADAM_B2 * v + (1.0 - ADAM_B2) * (g * g)
    m_hat = m / (1.0 - ADAM_B1 ** ADAM_STEP)
    v_hat = v / (1.0 - ADAM_B2 ** ADAM_STEP)
    delta = -ADAM_LR * (m_hat / (jnp.sqrt(v_hat) + ADAM_EPS) + ADAM_WD * w)
    return delta, m, v


def _row_tiled(name, own, recv, extra, n_out, finish, comms=()):
    r, c = own.shape
    recv = [] if recv is None else list(recv)
    k = max(len(recv), 1)
    n = max(k, -(-r // 512))
    tr, per = r // n, n // k
    blk = pl.BlockSpec((tr, c), lambda i: (i, 0))

    def body(own_ref, *refs):
        recv_refs, rest = refs[:len(recv)], refs[len(recv):]
        ins, outs = rest[:len(extra)], rest[len(extra):]

        def tile(recv_ref):
            g = own_ref[...]
            if recv_ref is not None:
                g = ((g + recv_ref[0].astype(F32)) + recv_ref[1].astype(F32)) + recv_ref[2].astype(F32)
            for o_ref, val in zip(outs, finish(g, *[a[...] for a in ins])):
                o_ref[...] = val

        if len(recv) <= 1:
            tile(recv_refs[0] if recv else None)
        else:
            for p in range(k):
                pl.when(pl.program_id(0) // per == p)(functools.partial(tile, recv_refs[p]))

    recv_specs = [pl.BlockSpec((3, tr, c), lambda i, p=p: (0, jnp.clip(i - p * per, 0, per - 1), 0)) for p in range(len(recv))]
    return _carry(
        body, name=name, grid=(n,), comms=comms,
        in_specs=[blk] + recv_specs + [blk] * len(extra),
        out_specs=[blk] * n_out, out_shape=[jax.ShapeDtypeStruct((r, c), F32)] * n_out,
        args=(own, *recv, *extra))


def _adamw_shard(name, own, recv, w, m, v, comms=()):
    def finish(g, w_t, m_t, v_t):
        return (g,) + _adamw(w_t, g, m_t, v_t)

    return _row_tiled(name, own, recv, (w, m, v), 4, finish, comms)


def _sum_partials(name, own, recv):
    return _row_tiled(name, own, recv, (), 1, lambda g: (g,))[0][0]


VEC_VLN, VEC_LN1G, VEC_LN1B, VEC_LN2G, VEC_LN2B, VEC_SINK, VEC_LOSS, VEC_BSP, VEC_ROWS = 0, 1, 2, 3, 4, 5, 6, 8, 16


def _adamw_small(parts_w, parts_vec, params):
    n = parts_w.shape[0]
    flat = [a for p in params for a in p]
    shapes = [p[0].shape for p in params]

    def grads(gw, gv):
        return [gw, gv[VEC_VLN:VEC_VLN + 1, 0:D_GMLP], gv[VEC_VLN:VEC_VLN + 1, D_GMLP:2 * D_GMLP],
                gv[VEC_BSP:VEC_BSP + N_HEADS, 0:BLK], gv[VEC_LN1G:VEC_LN1G + 1], gv[VEC_LN1B:VEC_LN1B + 1],
                gv[VEC_LN2G:VEC_LN2G + 1], gv[VEC_LN2B:VEC_LN2B + 1], gv[VEC_SINK:VEC_SINK + 1, 0:N_HEADS]]

    def body(pw_ref, pv_ref, *refs):
        ins, outs = refs[:len(flat)], refs[len(flat):]
        gw, gv = pw_ref[0].astype(F32), pv_ref[0]
        for k in range(1, n):
            gw, gv = gw + pw_ref[k].astype(F32), gv + pv_ref[k]
        for i, g in enumerate(grads(gw, gv)):
            w_ref, m_ref, v_ref = ins[3 * i:3 * i + 3]
            delta, m_new, v_new = _adamw(w_ref[...], g, m_ref[...], v_ref[...])
            for o_ref, val in zip(outs[4 * i:4 * i + 4], (g, delta, m_new, v_new)):
                o_ref[...] = val
        outs[-1][...] = gv[VEC_LOSS:VEC_LOSS + 1, 0:LANES]

    whole = lambda shape: pl.BlockSpec(shape, lambda i: (0,) * len(shape))
    res = _carry(
        body, name="adamw_small", grid=(1,),
        in_specs=[whole(parts_w.shape), whole(parts_vec.shape)] + [whole(a.shape) for a in flat],
        out_specs=[whole(s) for s in shapes for _ in range(4)] + [whole((1, LANES))],
        out_shape=[jax.ShapeDtypeStruct(s, F32) for s in shapes for _ in range(4)] + [jax.ShapeDtypeStruct((1, LANES), F32)],
        args=(parts_w, parts_vec, *flat))[0]
    return [res[4 * i:4 * i + 4] for i in range(len(params))], res[-1]


def _pair_sum(name, parts, recv, core_chip):
    _, r, c = parts.shape
    tr = r if r <= 512 else 512

    def body(cc_ref, a_ref, b_ref, wire_ref, own_ref):
        s = a_ref[...] + b_ref[...]
        wire_ref[...] = s.astype(BF16)

        @pl.when(pl.program_id(1) == cc_ref[1])
        def _():
            own_ref[...] = s

    return _carry(
        body, name=name, grid=(r // tr, 4), prefetch=(core_chip,),
        in_specs=[pl.BlockSpec((None, tr, c), lambda i, q, cc: (2 * q + cc[0], i, 0)),
                  pl.BlockSpec((None, tr, c), lambda i, q, cc: (q, i, 0))],
        out_specs=[pl.BlockSpec((None, tr, c), lambda i, q, cc: (q, i, 0)), pl.BlockSpec((tr, c), lambda i, q, cc: (i, 0))],
        out_shape=[jax.ShapeDtypeStruct((4, r, c), BF16), jax.ShapeDtypeStruct((r, c), F32)],
        args=(parts, recv))[0]


def kernel(x, positions, w_in, v_ln_g, v_ln_b, w_spatial, b_spatial, sinks, w_out, ln1_g, ln1_b, w_ff1, w_ff2, ln2_g, ln2_b, loss_target, m_w_in, m_v_ln_g, m_v_ln_b, m_w_spatial, m_b_spatial, m_sinks, m_w_out, m_ln1_g, m_ln1_b, m_w_ff1, m_w_ff2, m_ln2_g, m_ln2_b, v_w_in, v_v_ln_g, v_v_ln_b, v_w_spatial, v_b_spatial, v_sinks, v_w_out, v_ln1_g, v_ln1_b, v_w_ff1, v_w_ff2, v_ln2_g, v_ln2_b):
    _, t_tok, d = x.shape
    xi, yi, ci = _place()
    core_chip = jnp.stack([ci, 2 * xi + yi]).astype(jnp.int32)
    x2 = x.reshape(t_tok, d)
    target = loss_target.reshape(t_tok, d)
    inv_freq = ROPE_THETA ** (-jnp.arange(0, HEAD_DIM, 2, dtype=F32) / HEAD_DIM)
    wsp, bsp, sink_vec = w_spatial[0], b_spatial[0], sinks[0]
    vg_col, vb_col = v_ln_g.reshape(D_GMLP, 1), v_ln_b.reshape(D_GMLP, 1)
    big = {"in": w_in[0], "out": w_out[0], "ff1": w_ff1[0], "ff2": w_ff2[0]}
    half1, half2 = big["ff1"].shape[1] // 2, big["ff2"].shape[0] // 2
    w1_mine = [big["ff1"][:, :half1].astype(BF16), big["ff1"][:, half1:].astype(BF16)]
    w2_mine = [big["ff2"][:half2].astype(BF16), big["ff2"][half2:].astype(BF16)]

    (cos_t, sin_t), ((g_in,),) = _rope_tables(
        positions, jnp.tile(inv_freq, 2).reshape(HEAD_DIM, 1), comms=[_gather_comm([big["in"].T.astype(BF16)])])
    w_in_t = g_in.reshape(D_IN, d)
    (h_t, xb), ((g_out, w1_a),) = _proj_in(x2, w_in_t, comms=[_gather_comm([big["out"].astype(BF16), w1_mine[0]])])
    w_out_b = g_out.reshape(-1, d)
    band_bias = _band_bias()
    (cat_t,), ((w1_b, w2_a),) = _mixer_fwd(h_t, cos_t, sin_t, wsp, bsp, vg_col, vb_col, sink_vec, band_bias,
                                           comms=[_gather_comm([w1_mine[1], w2_mine[0]])])
    (z1,), ((w2_b,),) = _proj_out(cat_t, x2, w_out_b, comms=[_gather_comm([w2_mine[1]])])
    act_b, dpre_b, x1b, dz2b, dz1, stats = _ffn_fwd_bwd(z1, target, [w1_a, w1_b], [w2_a, w2_b], ln1_g, ln1_b, ln2_g, ln2_b)

    (p_ff1,), _ = _ffn_wgrad1(x1b, dpre_b, N_DEV)
    (p_ff2,), ((s_ff1,),) = _ffn_wgrad2(act_b, dz2b, N_DEV, comms=[_sibling_comm([p_ff1])])
    wire_ff1, own_ff1 = _pair_sum("pair_sum_ff1", p_ff1, s_ff1, core_chip)
    rows1 = wire_ff1.shape[1] // 2
    (dcat_t, gw_out), ((s_ff2,), (r_ff1a,)) = _proj_out_bwd(
        dz1, cat_t, w_out_b, comms=[_sibling_comm([p_ff2]), _chips_comm([wire_ff1], rows=(0, rows1))])
    wire_ff2, own_ff2 = _pair_sum("pair_sum_ff2", p_ff2, s_ff2, core_chip)
    p_out = gw_out.reshape(N_DEV, -1, d)
    (dh_b, dkvc_t, dkvp_t, g_wsp, g_bsp, g_vln, g_sink), ((r_ff1b,), (r_ff2,), (s_out,)) = _mixer_bwd(
        dcat_t, h_t, cos_t, sin_t, wsp, bsp, vg_col, vb_col, sink_vec, band_bias,
        comms=[_chips_comm([wire_ff1], rows=(rows1, rows1)), _chips_comm([wire_ff2]), _sibling_comm([p_out])])
    wire_out, own_out = _pair_sum("pair_sum_out", p_out, s_out, core_chip)
    sink_row = jnp.pad(g_sink.sum(axis=1).reshape(1, N_HEADS), ((0, 0), (0, d - N_HEADS)))
    small_vec = jnp.concatenate([g_vln[0:2].reshape(1, d), stats[0:4], sink_row, stats[4:5], jnp.zeros((1, d), F32),
                                 jnp.pad(g_bsp, ((0, 0), (0, d - BLK)))], axis=0)
    (dkv_b, gw_in_t), ((parts_w, parts_vec), (r_out,)) = _proj_in_wgrad(
        dh_b, dkvc_t, dkvp_t, xb, comms=[_gather_comm([g_wsp.reshape(-1, BLK), small_vec]), _chips_comm([wire_out])])
    p_in = gw_in_t.reshape(N_DEV, -1, d)

    out_out, ((s_in,),) = _adamw_shard("adamw_out", own_out, [r_out], big["out"], m_w_out[0], v_w_out[0], comms=[_sibling_comm([p_in])])
    wire_in, own_in = _pair_sum("pair_sum_in", p_in, s_in, core_chip)
    (grad_x,), ((r_in,),) = _proj_in_dgrad(dh_b, dkv_b, dz1, w_in_t, comms=[_chips_comm([wire_in])])
    ff1_out, _ = _adamw_shard("adamw_ff1", own_ff1, [r_ff1a, r_ff1b], big["ff1"], m_w_ff1[0], v_w_ff1[0])
    ff2_out, _ = _adamw_shard("adamw_ff2", own_ff2, [r_ff2], big["ff2"], m_w_ff2[0], v_w_ff2[0])
    g_in_t = _sum_partials("sum_in", own_in, [r_in])
    in_out, _ = _adamw_shard("adamw_in", g_in_t.T, None, big["in"], m_w_in[0], v_w_in[0])
    small = [(w_spatial, m_w_spatial, v_w_spatial), (v_ln_g, m_v_ln_g, v_v_ln_g), (v_ln_b, m_v_ln_b, v_v_ln_b),
             (b_spatial, m_b_spatial, v_b_spatial), (ln1_g, m_ln1_g, v_ln1_g), (ln1_b, m_ln1_b, v_ln1_b),
             (ln2_g, m_ln2_g, v_ln2_g), (ln2_b, m_ln2_b, v_ln2_b), (sinks, m_sinks, v_sinks)]
    views = [(-1, BLK), None, None, (N_HEADS, BLK)] + [None] * 5
    small_res, loss_row = _adamw_small(parts_w, parts_vec, [
        tuple(a if vw is None else a.reshape(vw) for a in p) for p, vw in zip(small, views)])
    small_out = [[o.reshape(p[0].shape) for o in res] for res, p in zip(small_res, small)]
    loss = loss_row[0, 0]

    big_out = {0: in_out, 6: out_out, 9: ff1_out, 10: ff2_out}
    small_slot = {3: 0, 1: 1, 2: 2, 4: 3, 7: 4, 8: 5, 11: 6, 12: 7, 5: 8}
    outs = [loss, grad_x.reshape(x.shape)]
    for kind in range(4):
        for wi in range(13):
            outs.append(big_out[wi][kind][None] if wi in big_out else small_out[small_slot[wi]][kind])
    return tuple(outs)
```

```python
import functools
import math

import jax
import jax.numpy as jnp
from jax import lax
from jax.experimental import pallas as pl
from jax.experimental.pallas import tpu as pltpu

F32 = jnp.float32
BF16 = jnp.bfloat16
MESH = pl.DeviceIdType.MESH

HEAD_DIM = 64
N_HEADS = 8
N_KV_HEADS = 2
BLK = 128
D_GMLP = N_HEADS * HEAD_DIM
D_ATTN = N_HEADS * HEAD_DIM
D_KV = N_KV_HEADS * HEAD_DIM
D_IN = 2 * D_GMLP + D_ATTN + 2 * D_KV
COL_U, COL_V, COL_Q, COL_K = 0, D_GMLP, 2 * D_GMLP, 2 * D_GMLP + D_ATTN
ROPE_THETA = 10000.0
LN_EPS = 1e-5
ALPHA = 2.0 ** 0.25
NEG_INF = -1e30
SCORE_SCALE = 1.0 / math.sqrt(HEAD_DIM)
ADAM_LR, ADAM_B1, ADAM_B2, ADAM_EPS, ADAM_WD, ADAM_STEP = 0.001, 0.9, 0.999, 1e-08, 0.01, 10
N_DEV = 8
LANES = 128
VMEM_LIMIT = 56 * 1024 * 1024
N_SUB = 1

NT = (((1,), (1,)), ((), ()))
TN = (((0,), (0,)), ((), ()))


def _params(*sem):
    return pltpu.CompilerParams(dimension_semantics=sem, vmem_limit_bytes=VMEM_LIMIT)


def _dot(a, b, dims=None):
    if dims is None:
        return jnp.dot(a, b, preferred_element_type=F32)
    return lax.dot_general(a, b, dims, preferred_element_type=F32)


def _mean(a):
    return jnp.mean(a, axis=-1, keepdims=True)


def _ln_fwd(z, g, b):
    zc = z - _mean(z)
    rstd = lax.rsqrt(_mean(zc * zc) + LN_EPS)
    xhat = zc * rstd
    return xhat * g + b, xhat, rstd


def _ln_bwd(dy, xhat, rstd, g):
    dxhat = dy * g
    return rstd * (dxhat - _mean(dxhat) - xhat * _mean(dxhat * xhat))


_GELU_C = math.sqrt(2.0 / math.pi)


def _gelu(x):
    t = jnp.tanh(_GELU_C * (x + 0.044715 * (x * x * x)))
    return 0.5 * x * (1.0 + t)


def _gelu_and_grad(x):
    x2 = x * x
    t = jnp.tanh(_GELU_C * (x + 0.044715 * (x2 * x)))
    hx, ht = 0.5 * x, 0.5 * (1.0 + t)
    return x * ht, ht + hx * (1.0 - t * t) * (_GELU_C * (1.0 + 3.0 * 0.044715 * x2))


def _mean0(a):
    return jnp.mean(a, axis=0, keepdims=True)


def _ln_fwd_t(z, g, b):
    zc = z - _mean0(z)
    rstd = lax.rsqrt(_mean0(zc * zc) + LN_EPS)
    xhat = zc * rstd
    return xhat * g + b, xhat, rstd


def _ln_bwd_t(dy, xhat, rstd, g):
    dxhat = dy * g
    return rstd * (dxhat - _mean0(dxhat) - xhat * _mean0(dxhat * xhat))


def _rope_t(t, cos, sin_signed, bwd=False):
    half = HEAD_DIM // 2
    outs = []
    for r in range(0, t.shape[0], HEAD_DIM):
        th = t[r:r + HEAD_DIM]
        sw = jnp.concatenate([th[half:], th[:half]], axis=0) * sin_signed
        outs.append(th * cos - sw if bwd else th * cos + sw)
    return jnp.concatenate(outs, axis=0)


ANY = pl.BlockSpec(memory_space=pl.ANY)


def _place():
    return lax.axis_index("x"), lax.axis_index("y"), lax.axis_index("c")


class _Comm:
    def __init__(self, ins, outs, sems, start, finish):
        self.ins, self.outs, self.sems, self.start, self.finish = ins, outs, sems, start, finish


def _gather_comm(arrs):
    n = len(arrs)

    def parts(ins, outs, sems):
        send_sems, recv_sems, local_sems = sems
        x, y, c = _place()
        me, sibling = (x, y, c), (x, y, 1 - c)
        chips = [(1 - x, y), (x, 1 - y), (1 - x, 1 - y)]

        def copy(a, k, block, to, src=None):
            px, py, pc = block
            dst = outs[a].at[4 * px + 2 * py + pc]
            return pltpu.make_async_remote_copy(
                src_ref=dst if src is None else src, dst_ref=dst,
                send_sem=send_sems.at[a, k], recv_sem=recv_sems.at[a, k], device_id=to, device_id_type=MESH)

        mine = [pltpu.make_async_copy(ins[a], outs[a].at[4 * x + 2 * y + c], local_sems.at[a]) for a in range(n)]
        first = []
        for a in range(n):
            first.append(copy(a, 0, me, sibling, src=ins[a]))
            first += [copy(a, 1 + j, me, (*chip, c), src=ins[a]) for j, chip in enumerate(chips)]
        return copy, mine, first, me, sibling, chips, c

    def start(ins, outs, sems):
        _, mine, first, *_ = parts(ins, outs, sems)
        for cp in mine + first:
            cp.start()

    def finish(ins, outs, sems):
        copy, mine, first, me, sibling, chips, c = parts(ins, outs, sems)
        passed = []
        for j, chip in enumerate(chips):
            for a in range(n):
                copy(a, 1 + j, (*chip, c), me).wait_recv()
                fwd = copy(a, 4 + j, (*chip, c), sibling)
                fwd.start()
                passed.append(fwd)
        for a in range(n):
            copy(a, 0, sibling, me).wait_recv()
        for j, chip in enumerate(chips):
            for a in range(n):
                copy(a, 4 + j, (*chip, 1 - c), me).wait_recv()
        for cp in first + passed:
            cp.wait_send()
        for cp in mine:
            cp.wait()

    return _Comm(list(arrs), [jax.ShapeDtypeStruct((N_DEV,) + a.shape, a.dtype) for a in arrs],
                 [pltpu.SemaphoreType.DMA((n, 7)), pltpu.SemaphoreType.DMA((n, 7)), pltpu.SemaphoreType.DMA((n,))],
                 start, finish)


def _sibling_comm(parts):
    n = len(parts)

    def copies(ins, outs, sems):
        x, y, c = _place()
        return [pltpu.make_async_remote_copy(
            src_ref=ins[a].at[2 * q + (1 - c)], dst_ref=outs[a].at[q],
            send_sem=sems[0].at[a, q], recv_sem=sems[1].at[a, q],
            device_id=(x, y, 1 - c), device_id_type=MESH) for a in range(n) for q in range(4)]

    return _Comm(list(parts), [jax.ShapeDtypeStruct((4,) + p.shape[1:], p.dtype) for p in parts],
                 [pltpu.SemaphoreType.DMA((n, 4)), pltpu.SemaphoreType.DMA((n, 4))],
                 lambda *r: [cp.start() for cp in copies(*r)], lambda *r: [cp.wait() for cp in copies(*r)])


def _chips_comm(chip_parts, rows=None):
    n = len(chip_parts)
    r0, nr = (0, None) if rows is None else rows

    def copies(ins, outs, sems):
        x, y, c = _place()
        chips = [(1 - x, y), (x, 1 - y), (1 - x, 1 - y)]
        src = lambda a, q: ins[a].at[q] if rows is None else ins[a].at[q, pl.ds(r0, nr)]
        return [pltpu.make_async_remote_copy(
            src_ref=src(a, 2 * px + py), dst_ref=outs[a].at[k],
            send_sem=sems[0].at[a, k], recv_sem=sems[1].at[a, k],
            device_id=(px, py, c), device_id_type=MESH) for a in range(n) for k, (px, py) in enumerate(chips)]

    shape = lambda p: (3,) + p.shape[1:] if rows is None else (3, nr) + p.shape[2:]
    return _Comm(list(chip_parts), [jax.ShapeDtypeStruct(shape(p), p.dtype) for p in chip_parts],
                 [pltpu.SemaphoreType.DMA((n, 3)), pltpu.SemaphoreType.DMA((n, 3))],
                 lambda *r: [cp.start() for cp in copies(*r)], lambda *r: [cp.wait() for cp in copies(*r)])


def _carry(body, *, name, grid, in_specs, out_specs, out_shape, args, comms=(), scratch_shapes=(), prefetch=()):
    n_pre, n_in, n_out, n_scr = len(prefetch), len(in_specs), len(out_specs), len(scratch_shapes)
    c_ins = [a for cm in comms for a in cm.ins]
    c_outs = [s for cm in comms for s in cm.outs]
    c_sems = [s for cm in comms for s in cm.sems]

    def wrapped(*refs):
        pre, refs = refs[:n_pre], refs[n_pre:]
        ins, refs = refs[:n_in], refs[n_in:]
        cins, refs = refs[:len(c_ins)], refs[len(c_ins):]
        outs, refs = refs[:n_out], refs[n_out:]
        couts, refs = refs[:len(c_outs)], refs[len(c_outs):]
        scr, sems = refs[:n_scr], refs[n_scr:]
        groups, i0, o0, s0 = [], 0, 0, 0
        for cm in comms:
            groups.append((cm, cins[i0:i0 + len(cm.ins)], couts[o0:o0 + len(cm.outs)], sems[s0:s0 + len(cm.sems)]))
            i0, o0, s0 = i0 + len(cm.ins), o0 + len(cm.outs), s0 + len(cm.sems)
        first = pl.program_id(0) == 0
        last = pl.program_id(0) == grid[0] - 1
        for ax in range(1, len(grid)):
            first = first & (pl.program_id(ax) == 0)
            last = last & (pl.program_id(ax) == grid[ax] - 1)
        if comms:
            @pl.when(first)
            def _():
                for cm, ci, co, cs in groups:
                    cm.start(ci, co, cs)
        body(*pre, *ins, *outs, *scr)
        if comms:
            @pl.when(last)
            def _():
                for cm, ci, co, cs in groups:
                    cm.finish(ci, co, cs)

    grid_spec = pltpu.PrefetchScalarGridSpec(
        num_scalar_prefetch=n_pre, grid=grid,
        in_specs=list(in_specs) + [ANY] * len(c_ins), out_specs=list(out_specs) + [ANY] * len(c_outs),
        scratch_shapes=list(scratch_shapes) + c_sems)
    res = pl.pallas_call(
        wrapped, name=name, grid_spec=grid_spec, out_shape=list(out_shape) + c_outs,
        compiler_params=_params(*(["arbitrary"] * len(grid))),
    )(*prefetch, *args, *c_ins)
    outs, rest, per_comm = res[:n_out], res[n_out:], []
    for cm in comms:
        per_comm.append(rest[:len(cm.outs)])
        rest = rest[len(cm.outs):]
    return outs, per_comm


def _rope_tables(pos_row, inv_freq_col, comms=()):
    t_tok = pos_row.shape[1]
    tm = min(512, t_tok)

    def body(pos_ref, invf_ref, cos_ref, sin_ref):
        ang = pos_ref[...].astype(F32) * invf_ref[...]
        row = lax.broadcasted_iota(jnp.int32, ang.shape, 0)
        cos_ref[...] = jnp.cos(ang)
        sin_ref[...] = jnp.sin(ang) * jnp.where(row < HEAD_DIM // 2, -1.0, 1.0)

    return _carry(
        body, name="rope_tables", grid=(t_tok // tm,), comms=comms,
        in_specs=[pl.BlockSpec((1, tm), lambda i: (0, i)), pl.BlockSpec((HEAD_DIM, 1), lambda i: (0, 0))],
        out_specs=[pl.BlockSpec((HEAD_DIM, tm), lambda i: (0, i))] * 2,
        out_shape=[jax.ShapeDtypeStruct((HEAD_DIM, t_tok), F32)] * 2,
        args=(pos_row, inv_freq_col))


def _proj_in(x2, w_in_t, comms=()):
    t_tok, d = x2.shape
    d_in = w_in_t.shape[0]
    tm = min(512, t_tok)

    def body(x_ref, w_ref, h_ref, xb_ref):
        xb = x_ref[...].astype(BF16)
        xb_ref[...] = xb
        h_ref[...] = _dot(w_ref[...], xb, NT)

    return _carry(
        body, name="proj_in", grid=(t_tok // tm,), comms=comms,
        in_specs=[pl.BlockSpec((tm, d), lambda i: (i, 0)), pl.BlockSpec((d_in, d), lambda i: (0, 0))],
        out_specs=[pl.BlockSpec((d_in, tm), lambda i: (0, i)), pl.BlockSpec((tm, d), lambda i: (i, 0))],
        out_shape=[jax.ShapeDtypeStruct((d_in, t_tok), F32), jax.ShapeDtypeStruct((t_tok, d), BF16)],
        args=(x2, w_in_t))


def _h_specs():
    kv_row = COL_K // (2 * D_KV)
    return [
        pl.BlockSpec((D_GMLP, BLK), lambda i: (0, i)),
        pl.BlockSpec((D_GMLP, BLK), lambda i: (1, i)),
        pl.BlockSpec((D_ATTN, BLK), lambda i: (2, i)),
        pl.BlockSpec((2 * D_KV, BLK), lambda i: (kv_row, i)),
        pl.BlockSpec((2 * D_KV, BLK), lambda i: (kv_row, jnp.maximum(i - 1, 0))),
    ]


def _table_specs():
    return [
        pl.BlockSpec((HEAD_DIM, BLK), lambda i: (0, i)),
        pl.BlockSpec((HEAD_DIM, BLK), lambda i: (0, i)),
        pl.BlockSpec((HEAD_DIM, BLK), lambda i: (0, jnp.maximum(i - 1, 0))),
        pl.BlockSpec((HEAD_DIM, BLK), lambda i: (0, jnp.maximum(i - 1, 0))),
    ]


def _band_bias():
    ki = lax.broadcasted_iota(jnp.int32, (2, 2 * BLK, BLK), 1)
    qi = lax.broadcasted_iota(jnp.int32, (2, 2 * BLK, BLK), 2)
    later = lax.broadcasted_iota(jnp.int32, (2, 2 * BLK, BLK), 0) > 0
    dist = qi + BLK - ki
    return jnp.where((dist >= 0) & (dist < BLK) & ((ki >= BLK) | later), 0.0, NEG_INF).astype(F32)


BIAS_SPEC = pl.BlockSpec((None, 2 * BLK, BLK), lambda i: (jnp.minimum(i, 1), 0, 0))


def _keys_values(kvc, kvp, cosc, sinc, cosp, sinp):
    kp, kc = _rope_t(kvp[:D_KV], cosp, sinp), _rope_t(kvc[:D_KV], cosc, sinc)
    k_t = jnp.concatenate([kp, kc], axis=1).astype(BF16)
    k_n = jnp.concatenate([kp.T, kc.T], axis=0).astype(BF16)
    v_t = jnp.concatenate([kvp[D_KV:], kvc[D_KV:]], axis=1).astype(BF16)
    return k_t, k_n, v_t


def _pad_head(th, kv):
    z = jnp.zeros_like(th)
    return jnp.concatenate([th, z] if kv == 0 else [z, th], axis=0)


def _group_lanes(parts):
    return jnp.concatenate(parts, axis=1)


def _softmax_sink_t(s, sink):
    m = jnp.maximum(jnp.max(s, axis=0, keepdims=True), sink)
    e = jnp.exp(s - m)
    es = jnp.exp(sink - m)
    r = 1.0 / (jnp.sum(e, axis=0, keepdims=True) + es)
    return e * r, es * r


def _causal():
    row = lax.broadcasted_iota(jnp.int32, (BLK, BLK), 0)
    col = lax.broadcasted_iota(jnp.int32, (BLK, BLK), 1)
    return row >= col


def _mask_w_once(wsp_ref, wm_scr):
    @pl.when(pl.program_id(0) == 0)
    def _():
        causal = _causal()
        for hh in range(N_HEADS):
            wm_scr[hh] = jnp.where(causal, wsp_ref[hh], 0.0).astype(BF16)


def _mixer_fwd(h_t, cos_t, sin_t, w_spatial, b_spatial, vln_g, vln_b, sinks, band_bias, comms=()):
    t_tok = h_t.shape[1]
    nb = t_tok // BLK
    group = N_HEADS // N_KV_HEADS

    def body(sinks_ref, u_ref, vg_ref, q_ref, kvc_ref, kvp_ref, cosc_ref, sinc_ref, cosp_ref, sinp_ref,
             wsp_ref, bsp_ref, g_ref, b_ref, bias_ref, cat_ref, wm_scr):
        _mask_w_once(wsp_ref, wm_scr)
        ua = _gelu(u_ref[...])
        vp, _, _ = _ln_fwd_t(_gelu(vg_ref[...]), g_ref[...], b_ref[...])
        vpb = vp.astype(BF16)
        for hh in range(N_HEADS):
            rows = slice(hh * HEAD_DIM, (hh + 1) * HEAD_DIM)
            mixed = _dot(vpb[rows], wm_scr[hh], NT) + bsp_ref[hh:hh + 1, :]
            cat_ref[rows, :] = (ua[rows] * mixed).astype(BF16)

        cosc, sinc = cosc_ref[...], sinc_ref[...]
        qr = (_rope_t(q_ref[...], cosc, sinc) * SCORE_SCALE).astype(BF16)
        _, k_n, v_t = _keys_values(kvc_ref[...], kvp_ref[...], cosc, sinc, cosp_ref[...], sinp_ref[...])
        bias = _group_lanes([bias_ref[...]] * group)
        for kv in range(N_KV_HEADS):
            heads = range(kv * group, (kv + 1) * group)
            qs = _group_lanes([qr[hh * HEAD_DIM:(hh + 1) * HEAD_DIM] for hh in heads])
            sink = _group_lanes([jnp.full((1, BLK), sinks_ref[hh], F32) for hh in heads])
            p, _ = _softmax_sink_t(_dot(k_n, _pad_head(qs, kv)) + bias, sink)
            o = _dot(v_t[kv * HEAD_DIM:(kv + 1) * HEAD_DIM], p.astype(BF16)).astype(BF16)
            for j, hh in enumerate(heads):
                cat_ref[D_GMLP + hh * HEAD_DIM:D_GMLP + (hh + 1) * HEAD_DIM, :] = o[:, j * BLK:(j + 1) * BLK]

    full = lambda shape: pl.BlockSpec(shape, lambda i: (0,) * len(shape))
    return _carry(
        body, name="mixer_fwd", grid=(nb,), comms=comms,
        in_specs=[pl.BlockSpec(memory_space=pltpu.SMEM)] + _h_specs() + _table_specs() + [
            full((N_HEADS, BLK, BLK)), full((N_HEADS, BLK)), full((D_GMLP, 1)), full((D_GMLP, 1)), BIAS_SPEC],
        out_specs=[pl.BlockSpec((D_GMLP + D_ATTN, BLK), lambda i: (0, i))],
        out_shape=[jax.ShapeDtypeStruct((D_GMLP + D_ATTN, t_tok), BF16)],
        scratch_shapes=[pltpu.VMEM((N_HEADS, BLK, BLK), BF16)],
        args=(sinks, h_t, h_t, h_t, h_t, h_t, cos_t, sin_t, cos_t, sin_t, w_spatial, b_spatial, vln_g, vln_b, band_bias))


def _proj_out(cat_t, x2, w_out_b, comms=()):
    t_tok, d = x2.shape
    tm = min(512, t_tok)

    def body(cat_ref, x_ref, w_ref, z_ref):
        z_ref[...] = ALPHA * x_ref[...] + _dot(cat_ref[...], w_ref[...], TN)

    return _carry(
        body, name="proj_out", grid=(t_tok // tm,), comms=comms,
        in_specs=[pl.BlockSpec((cat_t.shape[0], tm), lambda i: (0, i)), pl.BlockSpec((tm, d), lambda i: (i, 0)),
                  pl.BlockSpec(w_out_b.shape, lambda i: (0, 0))],
        out_specs=[pl.BlockSpec((tm, d), lambda i: (i, 0))],
        out_shape=[jax.ShapeDtypeStruct((t_tok, d), F32)],
        args=(cat_t, x2, w_out_b))


def _ffn_fwd_bwd(z1, target, w1_parts, w2_parts, ln1_g, ln1_b, ln2_g, ln2_b):
    t_tok, d = z1.shape
    n_part = len(w1_parts)
    n_chunk, _, fp = w1_parts[0].shape
    fc = n_part * fp
    f = n_chunk * fc
    tm = min(256, t_tok)

    def body(z1_ref, tgt_ref, *refs):
        w1_hbm, w2_hbm = refs[:n_part], refs[n_part:2 * n_part]
        (g1_ref, b1_ref, g2_ref, b2_ref, act_ref, dpre_ref, x1b_ref, dz2b_ref, dz1_ref, stats_ref,
         r_scr, w1_ref, w2_ref, w_sems) = refs[2 * n_part:]

        @pl.when(pl.program_id(0) == 0)
        def _():
            stats_ref[...] = jnp.zeros_like(stats_ref)
            loads = [pltpu.make_async_copy(w1_hbm[p], w1_ref.at[:, :, pl.ds(p * fp, fp)], w_sems.at[0, p]) for p in range(n_part)]
            loads += [pltpu.make_async_copy(w2_hbm[p], w2_ref.at[:, pl.ds(p * fp, fp), :], w_sems.at[1, p]) for p in range(n_part)]
            for cp in loads:
                cp.start()
            for cp in loads:
                cp.wait()

        g1, g2 = g1_ref[...], g2_ref[...]
        groups = [slice(s * (tm // N_SUB), (s + 1) * (tm // N_SUB)) for s in range(N_SUB)]
        st = [{} for _ in groups]
        for s, rows in zip(st, groups):
            s["x1"], s["xhat1"], s["rstd1"] = _ln_fwd(z1_ref[rows, :], g1, b1_ref[...])
            s["x1b"] = s["x1"].astype(BF16)
            x1b_ref[rows, :] = s["x1b"]
        for s, rows in zip(st, groups):
            ff = jnp.zeros((tm // N_SUB, d), F32)
            for j in range(n_chunk):
                r = jnp.maximum(_dot(s["x1b"], w1_ref[j]), 0.0)
                r_scr[rows, j * fc:(j + 1) * fc] = r
                act = (r * r).astype(BF16)
                act_ref[rows, j * fc:(j + 1) * fc] = act
                ff = ff + _dot(act, w2_ref[j])
            s["ff"] = ff
        stats = [jnp.zeros((1, d), F32) for _ in range(5)]
        for s, rows in zip(st, groups):
            y, s["xhat2"], rstd2 = _ln_fwd(ALPHA * s["x1"] + s["ff"], g2, b2_ref[...])
            diff = y - tgt_ref[rows, :]
            loss = 0.5 * jnp.sum(jnp.sum(diff * diff, axis=-1, keepdims=True) / d, axis=0, keepdims=True)
            s["dy"] = diff / d
            s["dz2"] = _ln_bwd(s["dy"], s["xhat2"], rstd2, g2)
            s["dz2b"] = s["dz2"].astype(BF16)
            dz2b_ref[rows, :] = s["dz2b"]
            stats[4] = stats[4] + jnp.broadcast_to(loss, (1, d))
        for s, rows in zip(st, groups):
            dx1 = ALPHA * s["dz2"]
            for j in range(n_chunk):
                dpre = (_dot(s["dz2b"], w2_ref[j], NT) * (2.0 * r_scr[rows, j * fc:(j + 1) * fc])).astype(BF16)
                dpre_ref[rows, j * fc:(j + 1) * fc] = dpre
                dx1 = dx1 + _dot(dpre, w1_ref[j], NT)
            s["dx1"] = dx1
        for s, rows in zip(st, groups):
            dz1_ref[rows, :] = _ln_bwd(s["dx1"], s["xhat1"], s["rstd1"], g1)
            stats[0] = stats[0] + jnp.sum(s["dx1"] * s["xhat1"], axis=0, keepdims=True)
            stats[1] = stats[1] + jnp.sum(s["dx1"], axis=0, keepdims=True)
            stats[2] = stats[2] + jnp.sum(s["dy"] * s["xhat2"], axis=0, keepdims=True)
            stats[3] = stats[3] + jnp.sum(s["dy"], axis=0, keepdims=True)
        for k in range(5):
            stats_ref[k:k + 1, :] += stats[k]

    tok = lambda w: pl.BlockSpec((tm, w), lambda i: (i, 0))
    vec = pl.BlockSpec((1, d), lambda i: (0, 0))
    return _carry(
        body, name="ffn_fwd_bwd", grid=(t_tok // tm,),
        in_specs=[tok(d), tok(d)] + [ANY] * (2 * n_part) + [vec, vec, vec, vec],
        out_specs=[tok(f), tok(f), tok(d), tok(d), tok(d), pl.BlockSpec((8, d), lambda i: (0, 0))],
        out_shape=[jax.ShapeDtypeStruct((t_tok, f), BF16), jax.ShapeDtypeStruct((t_tok, f), BF16),
                   jax.ShapeDtypeStruct((t_tok, d), BF16), jax.ShapeDtypeStruct((t_tok, d), BF16),
                   jax.ShapeDtypeStruct((t_tok, d), F32), jax.ShapeDtypeStruct((8, d), F32)],
        scratch_shapes=[pltpu.VMEM((tm, f), F32), pltpu.VMEM((n_chunk, d, fc), BF16), pltpu.VMEM((n_chunk, fc, d), BF16),
                        pltpu.SemaphoreType.DMA((2, n_part))],
        args=(z1, target, *w1_parts, *w2_parts, ln1_g, ln1_b, ln2_g, ln2_b))[0]


def _ffn_wgrad1(x1b, dpre_b, n_chunk, comms=()):
    t_tok, d = x1b.shape
    fc = dpre_b.shape[1] // n_chunk

    def body(x1_ref, dpre_ref, g_ref):
        g_ref[...] = _dot(x1_ref[...], dpre_ref[...], TN)

    return _carry(
        body, name="ffn_wgrad1", grid=(n_chunk,), comms=comms,
        in_specs=[pl.BlockSpec((t_tok, d), lambda j: (0, 0), pipeline_mode=pl.Buffered(1)),
                  pl.BlockSpec((t_tok, fc), lambda j: (0, j))],
        out_specs=[pl.BlockSpec((None, d, fc), lambda j: (j, 0, 0))],
        out_shape=[jax.ShapeDtypeStruct((n_chunk, d, fc), F32)],
        args=(x1b, dpre_b))


def _ffn_wgrad2(act_b, dz2b, n_chunk, comms=()):
    t_tok, d = dz2b.shape
    fc = act_b.shape[1] // n_chunk

    def body(act_ref, dz2_ref, g_ref):
        g_ref[...] = _dot(act_ref[...], dz2_ref[...], TN)

    return _carry(
        body, name="ffn_wgrad2", grid=(n_chunk,), comms=comms,
        in_specs=[pl.BlockSpec((t_tok, fc), lambda j: (0, j)),
                  pl.BlockSpec((t_tok, d), lambda j: (0, 0), pipeline_mode=pl.Buffered(1))],
        out_specs=[pl.BlockSpec((None, fc, d), lambda j: (j, 0, 0))],
        out_shape=[jax.ShapeDtypeStruct((n_chunk, fc, d), F32)],
        args=(act_b, dz2b))


def _proj_out_bwd(dz1, cat_t, w_out_b, comms=()):
    t_tok, d = dz1.shape
    d_mix = cat_t.shape[0]
    tm = min(512, t_tok)

    def body(dz1_ref, cat_ref, w_ref, dcat_ref, gw_ref):
        @pl.when(pl.program_id(0) == 0)
        def _():
            gw_ref[...] = jnp.zeros_like(gw_ref)

        dzb = dz1_ref[...].astype(BF16)
        dcat_ref[...] = _dot(w_ref[...], dzb, NT)
        gw_ref[...] += _dot(cat_ref[...], dzb)

    return _carry(
        body, name="proj_out_bwd", grid=(t_tok // tm,), comms=comms,
        in_specs=[pl.BlockSpec((tm, d), lambda i: (i, 0)), pl.BlockSpec((d_mix, tm), lambda i: (0, i)),
                  pl.BlockSpec((d_mix, d), lambda i: (0, 0))],
        out_specs=[pl.BlockSpec((d_mix, tm), lambda i: (0, i)), pl.BlockSpec((d_mix, d), lambda i: (0, 0))],
        out_shape=[jax.ShapeDtypeStruct((d_mix, t_tok), F32), jax.ShapeDtypeStruct((d_mix, d), F32)],
        args=(dz1, cat_t, w_out_b))


def _mixer_bwd(dcat_t, h_t, cos_t, sin_t, w_spatial, b_spatial, vln_g, vln_b, sinks, band_bias, comms=()):
    t_tok = h_t.shape[1]
    nb = t_tok // BLK
    group = N_HEADS // N_KV_HEADS

    def body(sinks_ref, dcat_ref, u_ref, vg_ref, q_ref, kvc_ref, kvp_ref, cosc_ref, sinc_ref, cosp_ref, sinp_ref,
             wsp_ref, bsp_ref, g_ref, b_ref, bias_ref, dh_ref, dkvc_ref, dkvp_ref, gwsb_ref, gbsp_ref, gvln_ref, gsink_ref,
             dg_acc, db_acc, wm_scr, gws_ref):
        i = pl.program_id(0)

        @pl.when(i == 0)
        def _():
            gws_ref[...] = jnp.zeros_like(gws_ref)
            gbsp_ref[...] = jnp.zeros_like(gbsp_ref)
            gsink_ref[...] = jnp.zeros_like(gsink_ref)
            dg_acc[...] = jnp.zeros_like(dg_acc)
            db_acc[...] = jnp.zeros_like(db_acc)

        _mask_w_once(wsp_ref, wm_scr)

        g = g_ref[...]
        ua, ua_grad = _gelu_and_grad(u_ref[...])
        vv, vv_grad = _gelu_and_grad(vg_ref[...])
        vp, vhat, rstd = _ln_fwd_t(vv, g, b_ref[...])
        vpb = vp.astype(BF16)
        da = dcat_ref[0:D_GMLP, :]
        dmixed = da * ua
        dvp_parts = []
        for hh in range(N_HEADS):
            rows = slice(hh * HEAD_DIM, (hh + 1) * HEAD_DIM)
            mixed = _dot(vpb[rows], wm_scr[hh], NT) + bsp_ref[hh:hh + 1, :]
            dh_ref[COL_U + hh * HEAD_DIM:COL_U + (hh + 1) * HEAD_DIM, :] = (da[rows] * mixed * ua_grad[rows]).astype(BF16)
            dm = dmixed[rows]
            dmb = dm.astype(BF16)
            gbsp_ref[hh:hh + 1, :] += jnp.sum(dm, axis=0, keepdims=True)
            gws_ref[hh] += _dot(dmb, vpb[rows], TN)
            dvp_parts.append(_dot(dmb, wm_scr[hh]))
        dvp = jnp.concatenate(dvp_parts, axis=0)
        dg_acc[...] += dvp * vhat
        db_acc[...] += dvp
        dh_ref[COL_V:COL_V + D_GMLP, :] = (_ln_bwd_t(dvp, vhat, rstd, g) * vv_grad).astype(BF16)

        cosc, sinc, cosp, sinp = cosc_ref[...], sinc_ref[...], cosp_ref[...], sinp_ref[...]
        qr = (_rope_t(q_ref[...], cosc, sinc) * SCORE_SCALE).astype(BF16)
        k_t, k_n, v_t = _keys_values(kvc_ref[...], kvp_ref[...], cosc, sinc, cosp, sinp)
        v_n = jnp.concatenate([kvp_ref[D_KV:, :].T, kvc_ref[D_KV:, :].T], axis=0).astype(BF16)
        bias = _group_lanes([bias_ref[...]] * group)
        dk, dv, dq_parts = [], [], []
        for kv in range(N_KV_HEADS):
            heads = range(kv * group, (kv + 1) * group)
            kv_rows = slice(kv * HEAD_DIM, (kv + 1) * HEAD_DIM)
            qs = _group_lanes([qr[hh * HEAD_DIM:(hh + 1) * HEAD_DIM] for hh in heads])
            dos = _group_lanes([dcat_ref[D_GMLP + hh * HEAD_DIM:D_GMLP + (hh + 1) * HEAD_DIM, :] for hh in heads]).astype(BF16)
            sink = _group_lanes([jnp.full((1, BLK), sinks_ref[hh], F32) for hh in heads])
            p, p_sink = _softmax_sink_t(_dot(k_n, _pad_head(qs, kv)) + bias, sink)
            dp = _dot(v_n, _pad_head(dos, kv))
            delta = jnp.sum(p * dp, axis=0, keepdims=True)
            ds = (p * (dp - delta)).astype(BF16)
            dsink = p_sink * delta
            dq = _dot(k_t[kv_rows], ds) * SCORE_SCALE
            for j, hh in enumerate(heads):
                gsink_ref[hh:hh + 1, :] -= dsink[:, j * BLK:(j + 1) * BLK]
                dq_parts.append(dq[:, j * BLK:(j + 1) * BLK])
            dk.append(_dot(qs, ds, NT))
            dv.append(_dot(dos, p.astype(BF16), NT))
        dh_ref[COL_Q:COL_Q + D_ATTN, :] = _rope_t(jnp.concatenate(dq_parts, axis=0), cosc, sinc, bwd=True).astype(BF16)
        dk_all = jnp.concatenate(dk, axis=0)
        dv_all = jnp.concatenate(dv, axis=0)
        dkvc_ref[0:D_KV, :] = _rope_t(dk_all[:, BLK:], cosc, sinc, bwd=True)
        dkvc_ref[D_KV:2 * D_KV, :] = dv_all[:, BLK:]
        dkvp_ref[0:D_KV, :] = _rope_t(dk_all[:, :BLK], cosp, sinp, bwd=True)
        dkvp_ref[D_KV:2 * D_KV, :] = dv_all[:, :BLK]

        @pl.when(i == nb - 1)
        def _():
            causal = _causal()
            for hh in range(N_HEADS):
                gwsb_ref[hh] = jnp.where(causal, gws_ref[hh], 0.0).astype(BF16)
            gvln_ref[...] = jnp.zeros_like(gvln_ref)
            gvln_ref[0:1, :] = jnp.sum(dg_acc[...].T, axis=0, keepdims=True)
            gvln_ref[1:2, :] = jnp.sum(db_acc[...].T, axis=0, keepdims=True)

    full = lambda shape: pl.BlockSpec(shape, lambda i: (0,) * len(shape))
    return _carry(
        body, name="mixer_bwd", grid=(nb,), comms=comms,
        in_specs=[pl.BlockSpec(memory_space=pltpu.SMEM), pl.BlockSpec((D_GMLP + D_ATTN, BLK), lambda i: (0, i))]
        + _h_specs() + _table_specs()
        + [full((N_HEADS, BLK, BLK)), full((N_HEADS, BLK)), full((D_GMLP, 1)), full((D_GMLP, 1)), BIAS_SPEC],
        out_specs=[pl.BlockSpec((COL_K, BLK), lambda i: (0, i)), pl.BlockSpec((2 * D_KV, BLK), lambda i: (0, i)),
                   pl.BlockSpec((2 * D_KV, BLK), lambda i: (0, (i + nb - 1) % nb)),
                   full((N_HEADS, BLK, BLK)), full((N_HEADS, BLK)), full((8, D_GMLP)), full((N_HEADS, LANES))],
        out_shape=[jax.ShapeDtypeStruct((COL_K, t_tok), BF16), jax.ShapeDtypeStruct((2 * D_KV, t_tok), F32),
                   jax.ShapeDtypeStruct((2 * D_KV, t_tok), F32),
                   jax.ShapeDtypeStruct((N_HEADS, BLK, BLK), BF16), jax.ShapeDtypeStruct((N_HEADS, BLK), F32),
                   jax.ShapeDtypeStruct((8, D_GMLP), F32), jax.ShapeDtypeStruct((N_HEADS, LANES), F32)],
        scratch_shapes=[pltpu.VMEM((D_GMLP, BLK), F32), pltpu.VMEM((D_GMLP, BLK), F32), pltpu.VMEM((N_HEADS, BLK, BLK), BF16),
                        pltpu.VMEM((N_HEADS, BLK, BLK), F32)],
        args=(sinks, dcat_t, h_t, h_t, h_t, h_t, h_t, cos_t, sin_t, cos_t, sin_t, w_spatial, b_spatial, vln_g, vln_b, band_bias))


def _proj_in_wgrad(dh_b, dkvc_t, dkvp_t, xb, comms=()):
    t_tok, d = xb.shape
    d_main, d_kv = dh_b.shape[0], dkvc_t.shape[0]
    tm = min(1024, t_tok)

    def body(dh_ref, dkvc_ref, dkvp_ref, xb_ref, dkvb_ref, gw_ref):
        @pl.when(pl.program_id(0) == 0)
        def _():
            gw_ref[...] = jnp.zeros_like(gw_ref)

        dkvb = (dkvc_ref[...] + dkvp_ref[...]).astype(BF16)
        dkvb_ref[...] = dkvb
        gw_ref[0:d_main, :] += _dot(dh_ref[...], xb_ref[...])
        gw_ref[d_main:, :] += _dot(dkvb, xb_ref[...])

    tok = lambda rows: pl.BlockSpec((rows, tm), lambda i: (0, i))
    return _carry(
        body, name="proj_in_wgrad", grid=(t_tok // tm,), comms=comms,
        in_specs=[tok(d_main), tok(d_kv), tok(d_kv), pl.BlockSpec((tm, d), lambda i: (i, 0))],
        out_specs=[tok(d_kv), pl.BlockSpec((d_main + d_kv, d), lambda i: (0, 0))],
        out_shape=[jax.ShapeDtypeStruct((d_kv, t_tok), BF16), jax.ShapeDtypeStruct((d_main + d_kv, d), F32)],
        args=(dh_b, dkvc_t, dkvp_t, xb))


def _proj_in_dgrad(dh_b, dkv_b, dz1, w_in_t, comms=()):
    t_tok, d = dz1.shape
    d_main, d_kv = dh_b.shape[0], dkv_b.shape[0]
    tm = min(512, t_tok)

    def body(dh_ref, dkv_ref, dz1_ref, w_ref, dx_ref):
        dx_ref[...] = (ALPHA * dz1_ref[...] + _dot(dh_ref[...], w_ref[0:d_main, :], TN)
                       + _dot(dkv_ref[...], w_ref[d_main:, :], TN))

    return _carry(
        body, name="proj_in_dgrad", grid=(t_tok // tm,), comms=comms,
        in_specs=[pl.BlockSpec((d_main, tm), lambda i: (0, i)), pl.BlockSpec((d_kv, tm), lambda i: (0, i)),
                  pl.BlockSpec((tm, d), lambda i: (i, 0)), pl.BlockSpec((d_main + d_kv, d), lambda i: (0, 0))],
        out_specs=[pl.BlockSpec((tm, d), lambda i: (i, 0))],
        out_shape=[jax.ShapeDtypeStruct((t_tok, d), F32)],
        args=(dh_b, dkv_b, dz1, w_in_t))


def _adamw(w, g, m, v):
    m = ADAM_B1 * m + (1.0 - ADAM_B1) * g
    v = ADAM_B2 * v + (1.0 - ADAM_B2) * (g * g)
    m_hat = m / (1.0 - ADAM_B1 ** ADAM_STEP)
    v_hat = v / (1.0 - ADAM_B2 ** ADAM_STEP)
    delta = -ADAM_LR * (m_hat / (jnp.sqrt(v_hat) + ADAM_EPS) + ADAM_WD * w)
    return delta, m, v


def _row_tiled(name, own, recv, extra, n_out, finish, comms=()):
    r, c = own.shape
    recv = [] if recv is None else list(recv)
    k = max(len(recv), 1)
    n = max(k, -(-r // 512))
    tr, per = r // n, n // k
    blk = pl.BlockSpec((tr, c), lambda i: (i, 0))

    def body(own_ref, *refs):
        recv_refs, rest = refs[:len(recv)], refs[len(recv):]
        ins, outs = rest[:len(extra)], rest[len(extra):]

        def tile(recv_ref):
            g = own_ref[...]
            if recv_ref is not None:
                g = ((g + recv_ref[0].astype(F32)) + recv_ref[1].astype(F32)) + recv_ref[2].astype(F32)
            for o_ref, val in zip(outs, finish(g, *[a[...] for a in ins])):
                o_ref[...] = val

        if len(recv) <= 1:
            tile(recv_refs[0] if recv else None)
        else:
            for p in range(k):
                pl.when(pl.program_id(0) // per == p)(functools.partial(tile, recv_refs[p]))

    recv_specs = [pl.BlockSpec((3, tr, c), lambda i, p=p: (0, jnp.clip(i - p * per, 0, per - 1), 0)) for p in range(len(recv))]
    return _carry(
        body, name=name, grid=(n,), comms=comms,
        in_specs=[blk] + recv_specs + [blk] * len(extra),
        out_specs=[blk] * n_out, out_shape=[jax.ShapeDtypeStruct((r, c), F32)] * n_out,
        args=(own, *recv, *extra))


def _adamw_shard(name, own, recv, w, m, v, comms=()):
    def finish(g, w_t, m_t, v_t):
        return (g,) + _adamw(w_t, g, m_t, v_t)

    return _row_tiled(name, own, recv, (w, m, v), 4, finish, comms)


VEC_VLN, VEC_LN1G, VEC_LN1B, VEC_LN2G, VEC_LN2B, VEC_SINK, VEC_LOSS, VEC_BSP, VEC_ROWS = 0, 1, 2, 3, 4, 5, 6, 8, 16


def _adamw_small(parts_w, parts_vec, params):
    n = parts_w.shape[0]
    flat = [a for p in params for a in p]
    shapes = [p[0].shape for p in params]

    def grads(gw, gv):
        return [gw, gv[VEC_VLN:VEC_VLN + 1, 0:D_GMLP], gv[VEC_VLN:VEC_VLN + 1, D_GMLP:2 * D_GMLP],
                gv[VEC_BSP:VEC_BSP + N_HEADS, 0:BLK], gv[VEC_LN1G:VEC_LN1G + 1], gv[VEC_LN1B:VEC_LN1B + 1],
                gv[VEC_LN2G:VEC_LN2G + 1], gv[VEC_LN2B:VEC_LN2B + 1], gv[VEC_SINK:VEC_SINK + 1, 0:N_HEADS]]

    def body(pw_ref, pv_ref, *refs):
        ins, outs = refs[:len(flat)], refs[len(flat):]
        gw, gv = pw_ref[0].astype(F32), pv_ref[0]
        for k in range(1, n):
            gw, gv = gw + pw_ref[k].astype(F32), gv + pv_ref[k]
        for i, g in enumerate(grads(gw, gv)):
            w_ref, m_ref, v_ref = ins[3 * i:3 * i + 3]
            delta, m_new, v_new = _adamw(w_ref[...], g, m_ref[...], v_ref[...])
            for o_ref, val in zip(outs[4 * i:4 * i + 4], (g, delta, m_new, v_new)):
                o_ref[...] = val
        outs[-1][...] = gv[VEC_LOSS:VEC_LOSS + 1, 0:LANES]

    whole = lambda shape: pl.BlockSpec(shape, lambda i: (0,) * len(shape))
    res = _carry(
        body, name="adamw_small", grid=(1,),
        in_specs=[whole(parts_w.shape), whole(parts_vec.shape)] + [whole(a.shape) for a in flat],
        out_specs=[whole(s) for s in shapes for _ in range(4)] + [whole((1, LANES))],
        out_shape=[jax.ShapeDtypeStruct(s, F32) for s in shapes for _ in range(4)] + [jax.ShapeDtypeStruct((1, LANES), F32)],
        args=(parts_w, parts_vec, *flat))[0]
    return [res[4 * i:4 * i + 4] for i in range(len(params))], res[-1]


def _pair_sum(name, parts, recv, core_chip, comms=()):
    _, r, c = parts.shape
    tr = r if r <= 512 else 512

    def body(cc_ref, a_ref, b_ref, wire_ref, own_ref):
        s = a_ref[...] + b_ref[...]
        wire_ref[...] = s.astype(BF16)

        @pl.when(pl.program_id(1) == cc_ref[1])
        def _():
            own_ref[...] = s

    return _carry(
        body, name=name, grid=(r // tr, 4), prefetch=(core_chip,), comms=comms,
        in_specs=[pl.BlockSpec((None, tr, c), lambda i, q, cc: (2 * q + cc[0], i, 0)),
                  pl.BlockSpec((None, tr, c), lambda i, q, cc: (q, i, 0))],
        out_specs=[pl.BlockSpec((None, tr, c), lambda i, q, cc: (q, i, 0)), pl.BlockSpec((tr, c), lambda i, q, cc: (i, 0))],
        out_shape=[jax.ShapeDtypeStruct((4, r, c), BF16), jax.ShapeDtypeStruct((r, c), F32)],
        args=(parts, recv))


def kernel(x, positions, w_in, v_ln_g, v_ln_b, w_spatial, b_spatial, sinks, w_out, ln1_g, ln1_b, w_ff1, w_ff2, ln2_g, ln2_b, loss_target, m_w_in, m_v_ln_g, m_v_ln_b, m_w_spatial, m_b_spatial, m_sinks, m_w_out, m_ln1_g, m_ln1_b, m_w_ff1, m_w_ff2, m_ln2_g, m_ln2_b, v_w_in, v_v_ln_g, v_v_ln_b, v_w_spatial, v_b_spatial, v_sinks, v_w_out, v_ln1_g, v_ln1_b, v_w_ff1, v_w_ff2, v_ln2_g, v_ln2_b):
    _, t_tok, d = x.shape
    xi, yi, ci = _place()
    core_chip = jnp.stack([ci, 2 * xi + yi]).astype(jnp.int32)
    x2 = x.reshape(t_tok, d)
    target = loss_target.reshape(t_tok, d)
    inv_freq = ROPE_THETA ** (-jnp.arange(0, HEAD_DIM, 2, dtype=F32) / HEAD_DIM)
    wsp, bsp, sink_vec = w_spatial[0], b_spatial[0], sinks[0]
    vg_col, vb_col = v_ln_g.reshape(D_GMLP, 1), v_ln_b.reshape(D_GMLP, 1)
    big = {"in": w_in[0], "out": w_out[0], "ff1": w_ff1[0], "ff2": w_ff2[0]}
    half1, half2 = big["ff1"].shape[1] // 2, big["ff2"].shape[0] // 2
    w1_mine = [big["ff1"][:, :half1].astype(BF16), big["ff1"][:, half1:].astype(BF16)]
    w2_mine = [big["ff2"][:half2].astype(BF16), big["ff2"][half2:].astype(BF16)]

    (cos_t, sin_t), ((g_in,),) = _rope_tables(
        positions, jnp.tile(inv_freq, 2).reshape(HEAD_DIM, 1), comms=[_gather_comm([big["in"].T.astype(BF16)])])
    w_in_t = g_in.reshape(D_IN, d)
    (h_t, xb), ((g_out, w1_a),) = _proj_in(x2, w_in_t, comms=[_gather_comm([big["out"].astype(BF16), w1_mine[0]])])
    w_out_b = g_out.reshape(-1, d)
    band_bias = _band_bias()
    (cat_t,), ((w1_b, w2_a),) = _mixer_fwd(h_t, cos_t, sin_t, wsp, bsp, vg_col, vb_col, sink_vec, band_bias,
                                           comms=[_gather_comm([w1_mine[1], w2_mine[0]])])
    (z1,), ((w2_b,),) = _proj_out(cat_t, x2, w_out_b, comms=[_gather_comm([w2_mine[1]])])
    act_b, dpre_b, x1b, dz2b, dz1, stats = _ffn_fwd_bwd(z1, target, [w1_a, w1_b], [w2_a, w2_b], ln1_g, ln1_b, ln2_g, ln2_b)

    (p_ff1,), _ = _ffn_wgrad1(x1b, dpre_b, N_DEV)
    (p_ff2,), ((s_ff1,),) = _ffn_wgrad2(act_b, dz2b, N_DEV, comms=[_sibling_comm([p_ff1])])
    (wire_ff1, own_ff1), _ = _pair_sum("pair_sum_ff1", p_ff1, s_ff1, core_chip)
    rows1 = wire_ff1.shape[1] // 2
    (dcat_t, gw_out), ((s_ff2,), (r_ff1a,)) = _proj_out_bwd(
        dz1, cat_t, w_out_b, comms=[_sibling_comm([p_ff2]), _chips_comm([wire_ff1], rows=(0, rows1))])
    p_out = gw_out.reshape(N_DEV, -1, d)
    (wire_ff2, own_ff2), ((s_out,),) = _pair_sum("pair_sum_ff2", p_ff2, s_ff2, core_chip, comms=[_sibling_comm([p_out])])
    (wire_out, own_out), _ = _pair_sum("pair_sum_out", p_out, s_out, core_chip)
    (dh_b, dkvc_t, dkvp_t, g_wsp, g_bsp, g_vln, g_sink), ((r_ff1b,), (r_ff2, r_out)) = _mixer_bwd(
        dcat_t, h_t, cos_t, sin_t, wsp, bsp, vg_col, vb_col, sink_vec, band_bias,
        comms=[_chips_comm([wire_ff1], rows=(rows1, rows1)), _chips_comm([wire_ff2, wire_out])])
    sink_row = jnp.pad(g_sink.sum(axis=1).reshape(1, N_HEADS), ((0, 0), (0, d - N_HEADS)))
    small_vec = jnp.concatenate([g_vln[0:2].reshape(1, d), stats[0:4], sink_row, stats[4:5], jnp.zeros((1, d), F32),
                                 jnp.pad(g_bsp, ((0, 0), (0, d - BLK)))], axis=0)
    (dkv_b, gw_in_t), ((parts_w, parts_vec),) = _proj_in_wgrad(
        dh_b, dkvc_t, dkvp_t, xb, comms=[_gather_comm([g_wsp.reshape(-1, BLK), small_vec])])
    p_in = gw_in_t.reshape(N_DEV, -1, d)

    out_out, ((s_in,),) = _adamw_shard("adamw_out", own_out, [r_out], big["out"], m_w_out[0], v_w_out[0], comms=[_sibling_comm([p_in])])
    (wire_in, own_in), _ = _pair_sum("pair_sum_in", p_in, s_in, core_chip)
    (grad_x,), ((r_in,),) = _proj_in_dgrad(dh_b, dkv_b, dz1, w_in_t, comms=[_chips_comm([wire_in])])
    ff1_out, _ = _adamw_shard("adamw_ff1", own_ff1, [r_ff1a, r_ff1b], big["ff1"], m_w_ff1[0], v_w_ff1[0])
    ff2_out, _ = _adamw_shard("adamw_ff2", own_ff2, [r_ff2], big["ff2"], m_w_ff2[0], v_w_ff2[0])
    in_out_t, _ = _adamw_shard("adamw_in", own_in, [r_in], big["in"].T, m_w_in[0].T, v_w_in[0].T)
    in_out = [o.T for o in in_out_t]
    small = [(w_spatial, m_w_spatial, v_w_spatial), (v_ln_g, m_v_ln_g, v_v_ln_g), (v_ln_b, m_v_ln_b, v_v_ln_b),
             (b_spatial, m_b_spatial, v_b_spatial), (ln1_g, m_ln1_g, v_ln1_g), (ln1_b, m_ln1_b, v_ln1_b),
             (ln2_g, m_ln2_g, v_ln2_g), (ln2_b, m_ln2_b, v_ln2_b), (sinks, m_sinks, v_sinks)]
    views = [(-1, BLK), None, None, (N_HEADS, BLK)] + [None] * 5
    small_res, loss_row = _adamw_small(parts_w, parts_vec, [
        tuple(a if vw is None else a.reshape(vw) for a in p) for p, vw in zip(small, views)])
    small_out = [[o.reshape(p[0].shape) for o in res] for res, p in zip(small_res, small)]
    loss = loss_row[0, 0]

    big_out = {0: in_out, 6: out_out, 9: ff1_out, 10: ff2_out}
    small_slot = {3: 0, 1: 1, 2: 2, 4: 3, 7: 4, 8: 5, 11: 6, 12: 7, 5: 8}
    outs = [loss, grad_x.reshape(x.shape)]
    for kind in range(4):
        for wi in range(13):
            outs.append(big_out[wi][kind][None] if wi in big_out else small_out[small_slot[wi]][kind])
    return tuple(outs)
```

```python
import functools
import math

import jax
import jax.numpy as jnp
from jax import lax
from jax.experimental import pallas as pl
from jax.experimental.pallas import tpu as pltpu

F32 = jnp.float32
BF16 = jnp.bfloat16
MESH = pl.DeviceIdType.MESH

HEAD_DIM = 64
N_HEADS = 8
N_KV_HEADS = 2
BLK = 128
D_GMLP = N_HEADS * HEAD_DIM
D_ATTN = N_HEADS * HEAD_DIM
D_KV = N_KV_HEADS * HEAD_DIM
D_IN = 2 * D_GMLP + D_ATTN + 2 * D_KV
COL_U, COL_V, COL_Q, COL_K = 0, D_GMLP, 2 * D_GMLP, 2 * D_GMLP + D_ATTN
ROPE_THETA = 10000.0
LN_EPS = 1e-5
ALPHA = 2.0 ** 0.25
NEG_INF = -1e30
SCORE_SCALE = 1.0 / math.sqrt(HEAD_DIM)
ADAM_LR, ADAM_B1, ADAM_B2, ADAM_EPS, ADAM_WD, ADAM_STEP = 0.001, 0.9, 0.999, 1e-08, 0.01, 10
N_DEV = 8
LANES = 128
VMEM_LIMIT = 56 * 1024 * 1024

NT = (((1,), (1,)), ((), ()))
TN = (((0,), (0,)), ((), ()))


def _params(*sem):
    return pltpu.CompilerParams(dimension_semantics=sem, vmem_limit_bytes=VMEM_LIMIT)


def _dot(a, b, dims=None):
    if dims is None:
        return jnp.dot(a, b, preferred_element_type=F32)
    return lax.dot_general(a, b, dims, preferred_element_type=F32)


def _mean(a):
    return jnp.mean(a, axis=-1, keepdims=True)


def _ln_fwd(z, g, b):
    zc = z - _mean(z)
    rstd = lax.rsqrt(_mean(zc * zc) + LN_EPS)
    xhat = zc * rstd
    return xhat * g + b, xhat, rstd


def _ln_bwd(dy, xhat, rstd, g):
    dxhat = dy * g
    return rstd * (dxhat - _mean(dxhat) - xhat * _mean(dxhat * xhat))


_GELU_C = math.sqrt(2.0 / math.pi)


def _gelu(x):
    t = jnp.tanh(_GELU_C * (x + 0.044715 * (x * x * x)))
    return 0.5 * x * (1.0 + t)


def _gelu_and_grad(x):
    x2 = x * x
    t = jnp.tanh(_GELU_C * (x + 0.044715 * (x2 * x)))
    hx, ht = 0.5 * x, 0.5 * (1.0 + t)
    return x * ht, ht + hx * (1.0 - t * t) * (_GELU_C * (1.0 + 3.0 * 0.044715 * x2))


def _mean0(a):
    return jnp.mean(a, axis=0, keepdims=True)


def _ln_fwd_t(z, g, b):
    zc = z - _mean0(z)
    rstd = lax.rsqrt(_mean0(zc * zc) + LN_EPS)
    xhat = zc * rstd
    return xhat * g + b, xhat, rstd


def _ln_bwd_t(dy, xhat, rstd, g):
    dxhat = dy * g
    return rstd * (dxhat - _mean0(dxhat) - xhat * _mean0(dxhat * xhat))


def _rope_t(t, cos, sin_signed, bwd=False):
    half = HEAD_DIM // 2
    outs = []
    for r in range(0, t.shape[0], HEAD_DIM):
        th = t[r:r + HEAD_DIM]
        sw = jnp.concatenate([th[half:], th[:half]], axis=0) * sin_signed
        outs.append(th * cos - sw if bwd else th * cos + sw)
    return jnp.concatenate(outs, axis=0)


ANY = pl.BlockSpec(memory_space=pl.ANY)


def _place():
    return lax.axis_index("x"), lax.axis_index("y"), lax.axis_index("c")


class _Comm:
    def __init__(self, ins, outs, sems, start, finish):
        self.ins, self.outs, self.sems, self.start, self.finish = ins, outs, sems, start, finish


def _gather_comm(arrs):
    n = len(arrs)

    def parts(ins, outs, sems):
        send_sems, recv_sems, local_sems = sems
        x, y, c = _place()
        me, sibling = (x, y, c), (x, y, 1 - c)
        chips = [(1 - x, y), (x, 1 - y), (1 - x, 1 - y)]

        def copy(a, k, block, to, src=None):
            px, py, pc = block
            dst = outs[a].at[4 * px + 2 * py + pc]
            return pltpu.make_async_remote_copy(
                src_ref=dst if src is None else src, dst_ref=dst,
                send_sem=send_sems.at[a, k], recv_sem=recv_sems.at[a, k], device_id=to, device_id_type=MESH)

        mine = [pltpu.make_async_copy(ins[a], outs[a].at[4 * x + 2 * y + c], local_sems.at[a]) for a in range(n)]
        first = []
        for a in range(n):
            first.append(copy(a, 0, me, sibling, src=ins[a]))
            first += [copy(a, 1 + j, me, (*chip, c), src=ins[a]) for j, chip in enumerate(chips)]
        return copy, mine, first, me, sibling, chips, c

    def start(ins, outs, sems):
        _, mine, first, *_ = parts(ins, outs, sems)
        for cp in mine + first:
            cp.start()

    def finish(ins, outs, sems):
        copy, mine, first, me, sibling, chips, c = parts(ins, outs, sems)
        passed = []
        for j, chip in enumerate(chips):
            for a in range(n):
                copy(a, 1 + j, (*chip, c), me).wait_recv()
                fwd = copy(a, 4 + j, (*chip, c), sibling)
                fwd.start()
                passed.append(fwd)
        for a in range(n):
            copy(a, 0, sibling, me).wait_recv()
        for j, chip in enumerate(chips):
            for a in range(n):
                copy(a, 4 + j, (*chip, 1 - c), me).wait_recv()
        for cp in first + passed:
            cp.wait_send()
        for cp in mine:
            cp.wait()

    return _Comm(list(arrs), [jax.ShapeDtypeStruct((N_DEV,) + a.shape, a.dtype) for a in arrs],
                 [pltpu.SemaphoreType.DMA((n, 7)), pltpu.SemaphoreType.DMA((n, 7)), pltpu.SemaphoreType.DMA((n,))],
                 start, finish)


def _sibling_comm(parts):
    n = len(parts)

    def copies(ins, outs, sems):
        x, y, c = _place()
        return [pltpu.make_async_remote_copy(
            src_ref=ins[a].at[2 * q + (1 - c)], dst_ref=outs[a].at[q],
            send_sem=sems[0].at[a, q], recv_sem=sems[1].at[a, q],
            device_id=(x, y, 1 - c), device_id_type=MESH) for a in range(n) for q in range(4)]

    return _Comm(list(parts), [jax.ShapeDtypeStruct((4,) + p.shape[1:], p.dtype) for p in parts],
                 [pltpu.SemaphoreType.DMA((n, 4)), pltpu.SemaphoreType.DMA((n, 4))],
                 lambda *r: [cp.start() for cp in copies(*r)], lambda *r: [cp.wait() for cp in copies(*r)])


def _chips_comm(chip_parts, rows=None):
    n = len(chip_parts)
    r0, nr = (0, None) if rows is None else rows

    def copies(ins, outs, sems):
        x, y, c = _place()
        chips = [(1 - x, y), (x, 1 - y), (1 - x, 1 - y)]
        src = lambda a, q: ins[a].at[q] if rows is None else ins[a].at[q, pl.ds(r0, nr)]
        return [pltpu.make_async_remote_copy(
            src_ref=src(a, 2 * px + py), dst_ref=outs[a].at[k],
            send_sem=sems[0].at[a, k], recv_sem=sems[1].at[a, k],
            device_id=(px, py, c), device_id_type=MESH) for a in range(n) for k, (px, py) in enumerate(chips)]

    shape = lambda p: (3,) + p.shape[1:] if rows is None else (3, nr) + p.shape[2:]
    return _Comm(list(chip_parts), [jax.ShapeDtypeStruct(shape(p), p.dtype) for p in chip_parts],
                 [pltpu.SemaphoreType.DMA((n, 3)), pltpu.SemaphoreType.DMA((n, 3))],
                 lambda *r: [cp.start() for cp in copies(*r)], lambda *r: [cp.wait() for cp in copies(*r)])


def _carry(body, *, name, grid, in_specs, out_specs, out_shape, args, comms=(), scratch_shapes=(), prefetch=()):
    n_pre, n_in, n_out, n_scr = len(prefetch), len(in_specs), len(out_specs), len(scratch_shapes)
    c_ins = [a for cm in comms for a in cm.ins]
    c_outs = [s for cm in comms for s in cm.outs]
    c_sems = [s for cm in comms for s in cm.sems]

    def wrapped(*refs):
        pre, refs = refs[:n_pre], refs[n_pre:]
        ins, refs = refs[:n_in], refs[n_in:]
        cins, refs = refs[:len(c_ins)], refs[len(c_ins):]
        outs, refs = refs[:n_out], refs[n_out:]
        couts, refs = refs[:len(c_outs)], refs[len(c_outs):]
        scr, sems = refs[:n_scr], refs[n_scr:]
        groups, i0, o0, s0 = [], 0, 0, 0
        for cm in comms:
            groups.append((cm, cins[i0:i0 + len(cm.ins)], couts[o0:o0 + len(cm.outs)], sems[s0:s0 + len(cm.sems)]))
            i0, o0, s0 = i0 + len(cm.ins), o0 + len(cm.outs), s0 + len(cm.sems)
        first = pl.program_id(0) == 0
        last = pl.program_id(0) == grid[0] - 1
        for ax in range(1, len(grid)):
            first = first & (pl.program_id(ax) == 0)
            last = last & (pl.program_id(ax) == grid[ax] - 1)
        if comms:
            @pl.when(first)
            def _():
                for cm, ci, co, cs in groups:
                    cm.start(ci, co, cs)
        body(*pre, *ins, *outs, *scr)
        if comms:
            @pl.when(last)
            def _():
                for cm, ci, co, cs in groups:
                    cm.finish(ci, co, cs)

    grid_spec = pltpu.PrefetchScalarGridSpec(
        num_scalar_prefetch=n_pre, grid=grid,
        in_specs=list(in_specs) + [ANY] * len(c_ins), out_specs=list(out_specs) + [ANY] * len(c_outs),
        scratch_shapes=list(scratch_shapes) + c_sems)
    res = pl.pallas_call(
        wrapped, name=name, grid_spec=grid_spec, out_shape=list(out_shape) + c_outs,
        compiler_params=_params(*(["arbitrary"] * len(grid))),
    )(*prefetch, *args, *c_ins)
    outs, rest, per_comm = res[:n_out], res[n_out:], []
    for cm in comms:
        per_comm.append(rest[:len(cm.outs)])
        rest = rest[len(cm.outs):]
    return outs, per_comm


def _rope_tables(pos_row, inv_freq_col, comms=()):
    t_tok = pos_row.shape[1]
    tm = min(512, t_tok)

    def body(pos_ref, invf_ref, cos_ref, sin_ref):
        ang = pos_ref[...].astype(F32) * invf_ref[...]
        row = lax.broadcasted_iota(jnp.int32, ang.shape, 0)
        cos_ref[...] = jnp.cos(ang)
        sin_ref[...] = jnp.sin(ang) * jnp.where(row < HEAD_DIM // 2, -1.0, 1.0)

    return _carry(
        body, name="rope_tables", grid=(t_tok // tm,), comms=comms,
        in_specs=[pl.BlockSpec((1, tm), lambda i: (0, i)), pl.BlockSpec((HEAD_DIM, 1), lambda i: (0, 0))],
        out_specs=[pl.BlockSpec((HEAD_DIM, tm), lambda i: (0, i))] * 2,
        out_shape=[jax.ShapeDtypeStruct((HEAD_DIM, t_tok), F32)] * 2,
        args=(pos_row, inv_freq_col))


def _proj_in(x2, w_in_t, comms=()):
    t_tok, d = x2.shape
    d_in = w_in_t.shape[0]
    tm = min(512, t_tok)

    def body(x_ref, w_ref, h_ref, xb_ref):
        xb = x_ref[...].astype(BF16)
        xb_ref[...] = xb
        h_ref[...] = _dot(w_ref[...], xb, NT)

    return _carry(
        body, name="proj_in", grid=(t_tok // tm,), comms=comms,
        in_specs=[pl.BlockSpec((tm, d), lambda i: (i, 0)), pl.BlockSpec((d_in, d), lambda i: (0, 0))],
        out_specs=[pl.BlockSpec((d_in, tm), lambda i: (0, i)), pl.BlockSpec((tm, d), lambda i: (i, 0))],
        out_shape=[jax.ShapeDtypeStruct((d_in, t_tok), F32), jax.ShapeDtypeStruct((t_tok, d), BF16)],
        args=(x2, w_in_t))


MIX_BLOCKS = 2
MIX_W = MIX_BLOCKS * BLK


def _prev_block(i):
    return jnp.maximum(MIX_BLOCKS * i - 1, 0)


def _h_specs():
    kv_row = COL_K // (2 * D_KV)
    return [
        pl.BlockSpec((D_GMLP, MIX_W), lambda i: (0, i)),
        pl.BlockSpec((D_GMLP, MIX_W), lambda i: (1, i)),
        pl.BlockSpec((D_ATTN, MIX_W), lambda i: (2, i)),
        pl.BlockSpec((2 * D_KV, MIX_W), lambda i: (kv_row, i)),
        pl.BlockSpec((2 * D_KV, BLK), lambda i: (kv_row, _prev_block(i))),
    ]


def _table_specs():
    return [
        pl.BlockSpec((HEAD_DIM, MIX_W), lambda i: (0, i)),
        pl.BlockSpec((HEAD_DIM, MIX_W), lambda i: (0, i)),
        pl.BlockSpec((HEAD_DIM, BLK), lambda i: (0, _prev_block(i))),
        pl.BlockSpec((HEAD_DIM, BLK), lambda i: (0, _prev_block(i))),
    ]


def _cols(b):
    return slice(b * BLK, (b + 1) * BLK)


def _block_inputs(b, i, kvc, kvp_ref, cos, sin, cosp_ref, sinp_ref, bias_ref):
    if b == 0:
        kv_prev, cos_prev, sin_prev, bias = kvp_ref[...], cosp_ref[...], sinp_ref[...], bias_ref[jnp.minimum(i, 1)]
    else:
        kv_prev, cos_prev, sin_prev, bias = kvc[:, _cols(b - 1)], cos[:, _cols(b - 1)], sin[:, _cols(b - 1)], bias_ref[1]
    return kvc[:, _cols(b)], kv_prev, cos[:, _cols(b)], sin[:, _cols(b)], cos_prev, sin_prev, bias


def _band_bias():
    ki = lax.broadcasted_iota(jnp.int32, (2, 2 * BLK, BLK), 1)
    qi = lax.broadcasted_iota(jnp.int32, (2, 2 * BLK, BLK), 2)
    later = lax.broadcasted_iota(jnp.int32, (2, 2 * BLK, BLK), 0) > 0
    dist = qi + BLK - ki
    return jnp.where((dist >= 0) & (dist < BLK) & ((ki >= BLK) | later), 0.0, NEG_INF).astype(F32)


BIAS_SPEC = pl.BlockSpec((2, 2 * BLK, BLK), lambda i: (0, 0, 0))


def _keys_values(kvc, kvp, cosc, sinc, cosp, sinp):
    kp, kc = _rope_t(kvp[:D_KV], cosp, sinp), _rope_t(kvc[:D_KV], cosc, sinc)
    k_t = jnp.concatenate([kp, kc], axis=1).astype(BF16)
    k_n = jnp.concatenate([kp.T, kc.T], axis=0).astype(BF16)
    v_t = jnp.concatenate([kvp[D_KV:], kvc[D_KV:]], axis=1).astype(BF16)
    return k_t, k_n, v_t


def _pad_head(th, kv):
    z = jnp.zeros_like(th)
    return jnp.concatenate([th, z] if kv == 0 else [z, th], axis=0)


def _group_lanes(parts):
    return jnp.concatenate(parts, axis=1)


def _softmax_sink_t(s, sink):
    m = jnp.maximum(jnp.max(s, axis=0, keepdims=True), sink)
    e = jnp.exp(s - m)
    es = jnp.exp(sink - m)
    r = 1.0 / (jnp.sum(e, axis=0, keepdims=True) + es)
    return e * r, es * r


def _causal():
    row = lax.broadcasted_iota(jnp.int32, (BLK, BLK), 0)
    col = lax.broadcasted_iota(jnp.int32, (BLK, BLK), 1)
    return row >= col


def _mask_w_once(wsp_ref, wm_scr):
    @pl.when(pl.program_id(0) == 0)
    def _():
        causal = _causal()
        for hh in range(N_HEADS):
            wm_scr[hh] = jnp.where(causal, wsp_ref[hh], 0.0).astype(BF16)


def _mixer_fwd(h_t, cos_t, sin_t, w_spatial, b_spatial, vln_g, vln_b, sinks, band_bias, comms=()):
    t_tok = h_t.shape[1]
    group = N_HEADS // N_KV_HEADS

    def body(sinks_ref, u_ref, vg_ref, q_ref, kvc_ref, kvp_ref, cos_ref, sin_ref, cosp_ref, sinp_ref,
             wsp_ref, bsp_ref, g_ref, b_ref, bias_ref, cat_ref, wm_scr):
        i = pl.program_id(0)
        _mask_w_once(wsp_ref, wm_scr)
        ua = _gelu(u_ref[...])
        vp, _, _ = _ln_fwd_t(_gelu(vg_ref[...]), g_ref[...], b_ref[...])
        vpb = vp.astype(BF16)
        for b in range(MIX_BLOCKS):
            for hh in range(N_HEADS):
                rows = slice(hh * HEAD_DIM, (hh + 1) * HEAD_DIM)
                mixed = _dot(vpb[rows, _cols(b)], wm_scr[hh], NT) + bsp_ref[hh:hh + 1, :]
                cat_ref[rows, _cols(b)] = (ua[rows, _cols(b)] * mixed).astype(BF16)

        kvc, cos, sin = kvc_ref[...], cos_ref[...], sin_ref[...]
        qr = (_rope_t(q_ref[...], cos, sin) * SCORE_SCALE).astype(BF16)
        sinks4 = [_group_lanes([jnp.full((1, BLK), sinks_ref[hh], F32) for hh in range(kv * group, (kv + 1) * group)])
                  for kv in range(N_KV_HEADS)]
        for b in range(MIX_BLOCKS):
            kv_cur, kv_prev, cosc, sinc, cosp, sinp, bias1 = _block_inputs(b, i, kvc, kvp_ref, cos, sin, cosp_ref, sinp_ref, bias_ref)
            _, k_n, v_t = _keys_values(kv_cur, kv_prev, cosc, sinc, cosp, sinp)
            bias = _group_lanes([bias1] * group)
            for kv in range(N_KV_HEADS):
                heads = range(kv * group, (kv + 1) * group)
                qs = _group_lanes([qr[hh * HEAD_DIM:(hh + 1) * HEAD_DIM, _cols(b)] for hh in heads])
                p, _ = _softmax_sink_t(_dot(k_n, _pad_head(qs, kv)) + bias, sinks4[kv])
                o = _dot(v_t[kv * HEAD_DIM:(kv + 1) * HEAD_DIM], p.astype(BF16)).astype(BF16)
                for j, hh in enumerate(heads):
                    cat_ref[D_GMLP + hh * HEAD_DIM:D_GMLP + (hh + 1) * HEAD_DIM, _cols(b)] = o[:, j * BLK:(j + 1) * BLK]

    full = lambda shape: pl.BlockSpec(shape, lambda i: (0,) * len(shape))
    return _carry(
        body, name="mixer_fwd", grid=(t_tok // MIX_W,), comms=comms,
        in_specs=[pl.BlockSpec(memory_space=pltpu.SMEM)] + _h_specs() + _table_specs() + [
            full((N_HEADS, BLK, BLK)), full((N_HEADS, BLK)), full((D_GMLP, 1)), full((D_GMLP, 1)), BIAS_SPEC],
        out_specs=[pl.BlockSpec((D_GMLP + D_ATTN, MIX_W), lambda i: (0, i))],
        out_shape=[jax.ShapeDtypeStruct((D_GMLP + D_ATTN, t_tok), BF16)],
        scratch_shapes=[pltpu.VMEM((N_HEADS, BLK, BLK), BF16)],
        args=(sinks, h_t, h_t, h_t, h_t, h_t, cos_t, sin_t, cos_t, sin_t, w_spatial, b_spatial, vln_g, vln_b, band_bias))


def _proj_out(cat_t, x2, w_out_b, ln1_g, ln1_b, comms=()):
    t_tok, d = x2.shape
    tm = min(512, t_tok)

    def body(cat_ref, x_ref, w_ref, g_ref, b_ref, xhat_ref, rstd_ref, x1b_ref):
        x1, xhat, rstd = _ln_fwd(ALPHA * x_ref[...] + _dot(cat_ref[...], w_ref[...], TN), g_ref[...], b_ref[...])
        xhat_ref[...] = xhat
        rstd_ref[...] = rstd
        x1b_ref[...] = x1.astype(BF16)

    tok = lambda w: pl.BlockSpec((tm, w), lambda i: (i, 0))
    vec = pl.BlockSpec((1, d), lambda i: (0, 0))
    return _carry(
        body, name="proj_out", grid=(t_tok // tm,), comms=comms,
        in_specs=[pl.BlockSpec((cat_t.shape[0], tm), lambda i: (0, i)), tok(d), pl.BlockSpec(w_out_b.shape, lambda i: (0, 0)), vec, vec],
        out_specs=[tok(d), tok(1), tok(d)],
        out_shape=[jax.ShapeDtypeStruct((t_tok, d), F32), jax.ShapeDtypeStruct((t_tok, 1), F32), jax.ShapeDtypeStruct((t_tok, d), BF16)],
        args=(cat_t, x2, w_out_b, ln1_g, ln1_b))


def _ffn_fwd_bwd(xhat1, rstd1, x1b, target, w1_parts, w2_parts, ln1_g, ln1_b, ln2_g, ln2_b):
    t_tok, d = xhat1.shape
    n_part = len(w1_parts)
    n_chunk, _, fp = w1_parts[0].shape
    fc = n_part * fp
    f = n_chunk * fc
    tm = min(256, t_tok)

    def body(xhat1_ref, rstd1_ref, x1b_ref, tgt_ref, *refs):
        w1_hbm, w2_hbm = refs[:n_part], refs[n_part:2 * n_part]
        (g1_ref, b1_ref, g2_ref, b2_ref, act_ref, dpre_ref, dz2b_ref, dz1_ref, stats_ref,
         r_scr, w1_ref, w2_ref, w_sems) = refs[2 * n_part:]

        @pl.when(pl.program_id(0) == 0)
        def _():
            stats_ref[...] = jnp.zeros_like(stats_ref)
            loads = [pltpu.make_async_copy(w1_hbm[p], w1_ref.at[:, :, pl.ds(p * fp, fp)], w_sems.at[0, p]) for p in range(n_part)]
            loads += [pltpu.make_async_copy(w2_hbm[p], w2_ref.at[:, pl.ds(p * fp, fp), :], w_sems.at[1, p]) for p in range(n_part)]
            for cp in loads:
                cp.start()
            for cp in loads:
                cp.wait()

        g1, g2 = g1_ref[...], g2_ref[...]
        xhat1, x1b = xhat1_ref[...], x1b_ref[...]
        ff = jnp.zeros((tm, d), F32)
        for j in range(n_chunk):
            r = jnp.maximum(_dot(x1b, w1_ref[j]), 0.0)
            r_scr[:, j * fc:(j + 1) * fc] = r
            act = (r * r).astype(BF16)
            act_ref[:, j * fc:(j + 1) * fc] = act
            ff = ff + _dot(act, w2_ref[j])
        y, xhat2, rstd2 = _ln_fwd(ALPHA * (xhat1 * g1 + b1_ref[...]) + ff, g2, b2_ref[...])
        diff = y - tgt_ref[...]
        loss = 0.5 * jnp.sum(jnp.sum(diff * diff, axis=-1, keepdims=True) / d, axis=0, keepdims=True)
        dy = diff / d
        dz2 = _ln_bwd(dy, xhat2, rstd2, g2)
        dz2b = dz2.astype(BF16)
        dz2b_ref[...] = dz2b
        dx1 = ALPHA * dz2
        for j in range(n_chunk):
            dpre = (_dot(dz2b, w2_ref[j], NT) * (2.0 * r_scr[:, j * fc:(j + 1) * fc])).astype(BF16)
            dpre_ref[:, j * fc:(j + 1) * fc] = dpre
            dx1 = dx1 + _dot(dpre, w1_ref[j], NT)
        dz1_ref[...] = _ln_bwd(dx1, xhat1, rstd1_ref[...], g1)
        stats_ref[0:1, :] += jnp.sum(dx1 * xhat1, axis=0, keepdims=True)
        stats_ref[1:2, :] += jnp.sum(dx1, axis=0, keepdims=True)
        stats_ref[2:3, :] += jnp.sum(dy * xhat2, axis=0, keepdims=True)
        stats_ref[3:4, :] += jnp.sum(dy, axis=0, keepdims=True)
        stats_ref[4:5, :] += jnp.broadcast_to(loss, (1, d))

    tok = lambda w: pl.BlockSpec((tm, w), lambda i: (i, 0))
    vec = pl.BlockSpec((1, d), lambda i: (0, 0))
    return _carry(
        body, name="ffn_fwd_bwd", grid=(t_tok // tm,),
        in_specs=[tok(d), tok(1), tok(d), tok(d)] + [ANY] * (2 * n_part) + [vec, vec, vec, vec],
        out_specs=[tok(f), tok(f), tok(d), tok(d), pl.BlockSpec((8, d), lambda i: (0, 0))],
        out_shape=[jax.ShapeDtypeStruct((t_tok, f), BF16), jax.ShapeDtypeStruct((t_tok, f), BF16),
                   jax.ShapeDtypeStruct((t_tok, d), BF16), jax.ShapeDtypeStruct((t_tok, d), F32), jax.ShapeDtypeStruct((8, d), F32)],
        scratch_shapes=[pltpu.VMEM((tm, f), F32), pltpu.VMEM((n_chunk, d, fc), BF16), pltpu.VMEM((n_chunk, fc, d), BF16),
                        pltpu.SemaphoreType.DMA((2, n_part))],
        args=(xhat1, rstd1, x1b, target, *w1_parts, *w2_parts, ln1_g, ln1_b, ln2_g, ln2_b))[0]


def _ffn_wgrad1(x1b, dpre_b, n_chunk, comms=()):
    t_tok, d = x1b.shape
    fc = dpre_b.shape[1] // n_chunk

    def body(x1_ref, dpre_ref, g_ref):
        g_ref[...] = _dot(x1_ref[...], dpre_ref[...], TN)

    return _carry(
        body, name="ffn_wgrad1", grid=(n_chunk,), comms=comms,
        in_specs=[pl.BlockSpec((t_tok, d), lambda j: (0, 0), pipeline_mode=pl.Buffered(1)),
                  pl.BlockSpec((t_tok, fc), lambda j: (0, j))],
        out_specs=[pl.BlockSpec((None, d, fc), lambda j: (j, 0, 0))],
        out_shape=[jax.ShapeDtypeStruct((n_chunk, d, fc), F32)],
        args=(x1b, dpre_b))


def _ffn_wgrad2(act_b, dz2b, n_chunk, comms=()):
    t_tok, d = dz2b.shape
    fc = act_b.shape[1] // n_chunk

    def body(act_ref, dz2_ref, g_ref):
        g_ref[...] = _dot(act_ref[...], dz2_ref[...], TN)

    return _carry(
        body, name="ffn_wgrad2", grid=(n_chunk,), comms=comms,
        in_specs=[pl.BlockSpec((t_tok, fc), lambda j: (0, j)),
                  pl.BlockSpec((t_tok, d), lambda j: (0, 0), pipeline_mode=pl.Buffered(1))],
        out_specs=[pl.BlockSpec((None, fc, d), lambda j: (j, 0, 0))],
        out_shape=[jax.ShapeDtypeStruct((n_chunk, fc, d), F32)],
        args=(act_b, dz2b))


def _proj_out_bwd(dz1, cat_t, w_out_b, comms=()):
    t_tok, d = dz1.shape
    d_mix = cat_t.shape[0]
    tm = min(512, t_tok)

    def body(dz1_ref, cat_ref, w_ref, dcat_ref, gw_ref):
        @pl.when(pl.program_id(0) == 0)
        def _():
            gw_ref[...] = jnp.zeros_like(gw_ref)

        dzb = dz1_ref[...].astype(BF16)
        dcat_ref[...] = _dot(w_ref[...], dzb, NT)
        gw_ref[...] += _dot(cat_ref[...], dzb)

    return _carry(
        body, name="proj_out_bwd", grid=(t_tok // tm,), comms=comms,
        in_specs=[pl.BlockSpec((tm, d), lambda i: (i, 0)), pl.BlockSpec((d_mix, tm), lambda i: (0, i)),
                  pl.BlockSpec((d_mix, d), lambda i: (0, 0))],
        out_specs=[pl.BlockSpec((d_mix, tm), lambda i: (0, i)), pl.BlockSpec((d_mix, d), lambda i: (0, 0))],
        out_shape=[jax.ShapeDtypeStruct((d_mix, t_tok), F32), jax.ShapeDtypeStruct((d_mix, d), F32)],
        args=(dz1, cat_t, w_out_b))


def _mixer_bwd(dcat_t, h_t, cos_t, sin_t, w_spatial, b_spatial, vln_g, vln_b, sinks, band_bias, comms=()):
    t_tok = h_t.shape[1]
    nb, n_step = t_tok // BLK, t_tok // MIX_W
    group = N_HEADS // N_KV_HEADS

    def body(sinks_ref, dcat_ref, u_ref, vg_ref, q_ref, kvc_ref, kvp_ref, cos_ref, sin_ref, cosp_ref, sinp_ref,
             wsp_ref, bsp_ref, g_ref, b_ref, bias_ref, dh_ref, dkvc_ref, dkvp_ref, gwsb_ref, gbsp_ref, gvln_ref, gsink_ref,
             dg_acc, db_acc, wm_scr, gws_ref):
        i = pl.program_id(0)

        @pl.when(i == 0)
        def _():
            gws_ref[...] = jnp.zeros_like(gws_ref)
            gbsp_ref[...] = jnp.zeros_like(gbsp_ref)
            gsink_ref[...] = jnp.zeros_like(gsink_ref)
            dg_acc[...] = jnp.zeros_like(dg_acc)
            db_acc[...] = jnp.zeros_like(db_acc)

        _mask_w_once(wsp_ref, wm_scr)

        g = g_ref[...]
        ua, ua_grad = _gelu_and_grad(u_ref[...])
        vv, vv_grad = _gelu_and_grad(vg_ref[...])
        vp, vhat, rstd = _ln_fwd_t(vv, g, b_ref[...])
        vpb = vp.astype(BF16)
        da = dcat_ref[0:D_GMLP, :]
        dmixed = da * ua
        dvp_blocks = []
        for b in range(MIX_BLOCKS):
            dvp_parts = []
            for hh in range(N_HEADS):
                rows = slice(hh * HEAD_DIM, (hh + 1) * HEAD_DIM)
                vpb_h = vpb[rows, _cols(b)]
                mixed = _dot(vpb_h, wm_scr[hh], NT) + bsp_ref[hh:hh + 1, :]
                dh_ref[COL_U + hh * HEAD_DIM:COL_U + (hh + 1) * HEAD_DIM, _cols(b)] = (
                    da[rows, _cols(b)] * mixed * ua_grad[rows, _cols(b)]).astype(BF16)
                dm = dmixed[rows, _cols(b)]
                dmb = dm.astype(BF16)
                gbsp_ref[hh:hh + 1, :] += jnp.sum(dm, axis=0, keepdims=True)
                gws_ref[hh] += _dot(dmb, vpb_h, TN)
                dvp_parts.append(_dot(dmb, wm_scr[hh]))
            dvp_blocks.append(jnp.concatenate(dvp_parts, axis=0))
        dvp = jnp.concatenate(dvp_blocks, axis=1)
        dgv, dbv = dvp * vhat, dvp
        for b in range(MIX_BLOCKS):
            dg_acc[...] += dgv[:, _cols(b)]
            db_acc[...] += dbv[:, _cols(b)]
        dh_ref[COL_V:COL_V + D_GMLP, :] = (_ln_bwd_t(dvp, vhat, rstd, g) * vv_grad).astype(BF16)

        kvc, cos, sin = kvc_ref[...], cos_ref[...], sin_ref[...]
        qr = (_rope_t(q_ref[...], cos, sin) * SCORE_SCALE).astype(BF16)
        sinks4 = [_group_lanes([jnp.full((1, BLK), sinks_ref[hh], F32) for hh in range(kv * group, (kv + 1) * group)])
                  for kv in range(N_KV_HEADS)]
        dq_blocks, dkv_cur, dkv_prev = [], [], []
        for b in range(MIX_BLOCKS):
            kv_cur, kv_prev, cosc, sinc, cosp, sinp, bias1 = _block_inputs(b, i, kvc, kvp_ref, cos, sin, cosp_ref, sinp_ref, bias_ref)
            k_t, k_n, v_t = _keys_values(kv_cur, kv_prev, cosc, sinc, cosp, sinp)
            v_n = jnp.concatenate([kv_prev[D_KV:].T, kv_cur[D_KV:].T], axis=0).astype(BF16)
            bias = _group_lanes([bias1] * group)
            dk, dv, dq_parts = [], [], []
            for kv in range(N_KV_HEADS):
                heads = range(kv * group, (kv + 1) * group)
                kv_rows = slice(kv * HEAD_DIM, (kv + 1) * HEAD_DIM)
                qs = _group_lanes([qr[hh * HEAD_DIM:(hh + 1) * HEAD_DIM, _cols(b)] for hh in heads])
                dos = _group_lanes([dcat_ref[D_GMLP + hh * HEAD_DIM:D_GMLP + (hh + 1) * HEAD_DIM, _cols(b)]
                                    for hh in heads]).astype(BF16)
                p, p_sink = _softmax_sink_t(_dot(k_n, _pad_head(qs, kv)) + bias, sinks4[kv])
                dp = _dot(v_n, _pad_head(dos, kv))
                delta = jnp.sum(p * dp, axis=0, keepdims=True)
                ds = (p * (dp - delta)).astype(BF16)
                dsink = p_sink * delta
                dq = _dot(k_t[kv_rows], ds) * SCORE_SCALE
                for j, hh in enumerate(heads):
                    gsink_ref[hh:hh + 1, :] -= dsink[:, j * BLK:(j + 1) * BLK]
                    dq_parts.append(dq[:, j * BLK:(j + 1) * BLK])
                dk.append(_dot(qs, ds, NT))
                dv.append(_dot(dos, p.astype(BF16), NT))
            dq_blocks.append(jnp.concatenate(dq_parts, axis=0))
            dk_all, dv_all = jnp.concatenate(dk, axis=0), jnp.concatenate(dv, axis=0)
            dkv_cur.append(jnp.concatenate([_rope_t(dk_all[:, BLK:], cosc, sinc, bwd=True), dv_all[:, BLK:]], axis=0))
            dkv_prev.append(jnp.concatenate([_rope_t(dk_all[:, :BLK], cosp, sinp, bwd=True), dv_all[:, :BLK]], axis=0))
        dh_ref[COL_Q:COL_Q + D_ATTN, :] = _rope_t(jnp.concatenate(dq_blocks, axis=1), cos, sin, bwd=True).astype(BF16)
        for b in range(MIX_BLOCKS):
            dkvc_ref[:, _cols(b)] = dkv_cur[b] + dkv_prev[b + 1] if b + 1 < MIX_BLOCKS else dkv_cur[b]
        dkvp_ref[...] = dkv_prev[0]

        @pl.when(i == n_step - 1)
        def _():
            causal = _causal()
            for hh in range(N_HEADS):
                gwsb_ref[hh] = jnp.where(causal, gws_ref[hh], 0.0).astype(BF16)
            gvln_ref[...] = jnp.zeros_like(gvln_ref)
            gvln_ref[0:1, :] = jnp.sum(dg_acc[...].T, axis=0, keepdims=True)
            gvln_ref[1:2, :] = jnp.sum(db_acc[...].T, axis=0, keepdims=True)

    full = lambda shape: pl.BlockSpec(shape, lambda i: (0,) * len(shape))
    return _carry(
        body, name="mixer_bwd", grid=(n_step,), comms=comms,
        in_specs=[pl.BlockSpec(memory_space=pltpu.SMEM), pl.BlockSpec((D_GMLP + D_ATTN, MIX_W), lambda i: (0, i))]
        + _h_specs() + _table_specs()
        + [full((N_HEADS, BLK, BLK)), full((N_HEADS, BLK)), full((D_GMLP, 1)), full((D_GMLP, 1)), BIAS_SPEC],
        out_specs=[pl.BlockSpec((COL_K, MIX_W), lambda i: (0, i)), pl.BlockSpec((2 * D_KV, MIX_W), lambda i: (0, i)),
                   pl.BlockSpec((2 * D_KV, BLK), lambda i: (0, (i + n_step - 1) % n_step)),
                   full((N_HEADS, BLK, BLK)), full((N_HEADS, BLK)), full((8, D_GMLP)), full((N_HEADS, LANES))],
        out_shape=[jax.ShapeDtypeStruct((COL_K, t_tok), BF16), jax.ShapeDtypeStruct((2 * D_KV, t_tok), F32),
                   jax.ShapeDtypeStruct((2 * D_KV, n_step * BLK), F32),
                   jax.ShapeDtypeStruct((N_HEADS, BLK, BLK), BF16), jax.ShapeDtypeStruct((N_HEADS, BLK), F32),
                   jax.ShapeDtypeStruct((8, D_GMLP), F32), jax.ShapeDtypeStruct((N_HEADS, LANES), F32)],
        scratch_shapes=[pltpu.VMEM((D_GMLP, BLK), F32), pltpu.VMEM((D_GMLP, BLK), F32), pltpu.VMEM((N_HEADS, BLK, BLK), BF16),
                        pltpu.VMEM((N_HEADS, BLK, BLK), F32)],
        args=(sinks, dcat_t, h_t, h_t, h_t, h_t, h_t, cos_t, sin_t, cos_t, sin_t, w_spatial, b_spatial, vln_g, vln_b, band_bias))


def _proj_in_wgrad(dh_b, dkvc_t, dkvp_t, xb, comms=()):
    t_tok, d = xb.shape
    d_main, d_kv = dh_b.shape[0], dkvc_t.shape[0]
    tm = min(1024, t_tok)

    def body(dh_ref, dkvc_ref, dkvp_ref, xb_ref, dkvb_ref, gw_ref):
        @pl.when(pl.program_id(0) == 0)
        def _():
            gw_ref[...] = jnp.zeros_like(gw_ref)

        for s in range(tm // MIX_W):
            last = slice((s + 1) * MIX_W - BLK, (s + 1) * MIX_W)
            dkvb_ref[:, s * MIX_W:(s + 1) * MIX_W - BLK] = dkvc_ref[:, s * MIX_W:(s + 1) * MIX_W - BLK].astype(BF16)
            dkvb_ref[:, last] = (dkvc_ref[:, last] + dkvp_ref[:, _cols(s)]).astype(BF16)
        gw_ref[0:d_main, :] += _dot(dh_ref[...], xb_ref[...])
        gw_ref[d_main:, :] += _dot(dkvb_ref[...], xb_ref[...])

    tok = lambda rows: pl.BlockSpec((rows, tm), lambda i: (0, i))
    return _carry(
        body, name="proj_in_wgrad", grid=(t_tok // tm,), comms=comms,
        in_specs=[tok(d_main), tok(d_kv), pl.BlockSpec((d_kv, tm // MIX_BLOCKS), lambda i: (0, i)),
                  pl.BlockSpec((tm, d), lambda i: (i, 0))],
        out_specs=[tok(d_kv), pl.BlockSpec((d_main + d_kv, d), lambda i: (0, 0))],
        out_shape=[jax.ShapeDtypeStruct((d_kv, t_tok), BF16), jax.ShapeDtypeStruct((d_main + d_kv, d), F32)],
        args=(dh_b, dkvc_t, dkvp_t, xb))


def _proj_in_dgrad(dh_b, dkv_b, dz1, w_in_t, comms=()):
    t_tok, d = dz1.shape
    d_main, d_kv = dh_b.shape[0], dkv_b.shape[0]
    tm = min(512, t_tok)

    def body(dh_ref, dkv_ref, dz1_ref, w_ref, dx_ref):
        dx_ref[...] = (ALPHA * dz1_ref[...] + _dot(dh_ref[...], w_ref[0:d_main, :], TN)
                       + _dot(dkv_ref[...], w_ref[d_main:, :], TN))

    return _carry(
        body, name="proj_in_dgrad", grid=(t_tok // tm,), comms=comms,
        in_specs=[pl.BlockSpec((d_main, tm), lambda i: (0, i)), pl.BlockSpec((d_kv, tm), lambda i: (0, i)),
                  pl.BlockSpec((tm, d), lambda i: (i, 0)), pl.BlockSpec((d_main + d_kv, d), lambda i: (0, 0))],
        out_specs=[pl.BlockSpec((tm, d), lambda i: (i, 0))],
        out_shape=[jax.ShapeDtypeStruct((t_tok, d), F32)],
        args=(dh_b, dkv_b, dz1, w_in_t))


def _adamw(w, g, m, v):
    m = ADAM_B1 * m + (1.0 - ADAM_B1) * g
    v = ADAM_B2 * v + (1.0 - ADAM_B2) * (g * g)
    m_hat = m / (1.0 - ADAM_B1 ** ADAM_STEP)
    v_hat = v / (1.0 - ADAM_B2 ** ADAM_STEP)
    delta = -ADAM_LR * (m_hat / (jnp.sqrt(v_hat) + ADAM_EPS) + ADAM_WD * w)
    return delta, m, v


def _row_tiled(name, own, recv, extra, n_out, finish, comms=()):
    r, c = own.shape
    recv = [] if recv is None else list(recv)
    k = max(len(recv), 1)
    n = max(k, -(-r // 512))
    tr, per = r // n, n // k
    blk = pl.BlockSpec((tr, c), lambda i: (i, 0))

    def body(own_ref, *refs):
        recv_refs, rest = refs[:len(recv)], refs[len(recv):]
        ins, outs = rest[:len(extra)], rest[len(extra):]

        def tile(recv_ref):
            g = own_ref[...]
            if recv_ref is not None:
                g = ((g + recv_ref[0].astype(F32)) + recv_ref[1].astype(F32)) + recv_ref[2].astype(F32)
            for o_ref, val in zip(outs, finish(g, *[a[...] for a in ins])):
                o_ref[...] = val

        if len(recv) <= 1:
            tile(recv_refs[0] if recv else None)
        else:
            for p in range(k):
                pl.when(pl.program_id(0) // per == p)(functools.partial(tile, recv_refs[p]))

    recv_specs = [pl.BlockSpec((3, tr, c), lambda i, p=p: (0, jnp.clip(i - p * per, 0, per - 1), 0)) for p in range(len(recv))]
    return _carry(
        body, name=name, grid=(n,), comms=comms,
        in_specs=[blk] + recv_specs + [blk] * len(extra),
        out_specs=[blk] * n_out, out_shape=[jax.ShapeDtypeStruct((r, c), F32)] * n_out,
        args=(own, *recv, *extra))


def _adamw_shard(name, own, recv, w, m, v, comms=()):
    def finish(g, w_t, m_t, v_t):
        return (g,) + _adamw(w_t, g, m_t, v_t)

    return _row_tiled(name, own, recv, (w, m, v), 4, finish, comms)


VEC_VLN, VEC_LN1G, VEC_LN1B, VEC_LN2G, VEC_LN2B, VEC_SINK, VEC_LOSS, VEC_BSP, VEC_ROWS = 0, 1, 2, 3, 4, 5, 6, 8, 16


def _adamw_small(parts_w, parts_vec, params):
    n = parts_w.shape[0]
    flat = [a for p in params for a in p]
    shapes = [p[0].shape for p in params]

    def grads(gw, gv):
        return [gw, gv[VEC_VLN:VEC_VLN + 1, 0:D_GMLP], gv[VEC_VLN:VEC_VLN + 1, D_GMLP:2 * D_GMLP],
                gv[VEC_BSP:VEC_BSP + N_HEADS, 0:BLK], gv[VEC_LN1G:VEC_LN1G + 1], gv[VEC_LN1B:VEC_LN1B + 1],
                gv[VEC_LN2G:VEC_LN2G + 1], gv[VEC_LN2B:VEC_LN2B + 1], gv[VEC_SINK:VEC_SINK + 1, 0:N_HEADS]]

    def body(pw_ref, pv_ref, *refs):
        ins, outs = refs[:len(flat)], refs[len(flat):]
        gw, gv = pw_ref[0].astype(F32), pv_ref[0]
        for k in range(1, n):
            gw, gv = gw + pw_ref[k].astype(F32), gv + pv_ref[k]
        for i, g in enumerate(grads(gw, gv)):
            w_ref, m_ref, v_ref = ins[3 * i:3 * i + 3]
            delta, m_new, v_new = _adamw(w_ref[...], g, m_ref[...], v_ref[...])
            for o_ref, val in zip(outs[4 * i:4 * i + 4], (g, delta, m_new, v_new)):
                o_ref[...] = val
        outs[-1][...] = gv[VEC_LOSS:VEC_LOSS + 1, 0:LANES]

    whole = lambda shape: pl.BlockSpec(shape, lambda i: (0,) * len(shape))
    res = _carry(
        body, name="adamw_small", grid=(1,),
        in_specs=[whole(parts_w.shape), whole(parts_vec.shape)] + [whole(a.shape) for a in flat],
        out_specs=[whole(s) for s in shapes for _ in range(4)] + [whole((1, LANES))],
        out_shape=[jax.ShapeDtypeStruct(s, F32) for s in shapes for _ in range(4)] + [jax.ShapeDtypeStruct((1, LANES), F32)],
        args=(parts_w, parts_vec, *flat))[0]
    return [res[4 * i:4 * i + 4] for i in range(len(params))], res[-1]


def _pair_sum(name, parts, recv, core_chip, comms=()):
    _, r, c = parts.shape
    tr = r if r <= 512 else 512

    def body(cc_ref, a_ref, b_ref, wire_ref, own_ref):
        s = a_ref[...] + b_ref[...]
        wire_ref[...] = s.astype(BF16)

        @pl.when(pl.program_id(1) == cc_ref[1])
        def _():
            own_ref[...] = s

    return _carry(
        body, name=name, grid=(r // tr, 4), prefetch=(core_chip,), comms=comms,
        in_specs=[pl.BlockSpec((None, tr, c), lambda i, q, cc: (2 * q + cc[0], i, 0)),
                  pl.BlockSpec((None, tr, c), lambda i, q, cc: (q, i, 0))],
        out_specs=[pl.BlockSpec((None, tr, c), lambda i, q, cc: (q, i, 0)), pl.BlockSpec((tr, c), lambda i, q, cc: (i, 0))],
        out_shape=[jax.ShapeDtypeStruct((4, r, c), BF16), jax.ShapeDtypeStruct((r, c), F32)],
        args=(parts, recv))


def kernel(x, positions, w_in, v_ln_g, v_ln_b, w_spatial, b_spatial, sinks, w_out, ln1_g, ln1_b, w_ff1, w_ff2, ln2_g, ln2_b, loss_target, m_w_in, m_v_ln_g, m_v_ln_b, m_w_spatial, m_b_spatial, m_sinks, m_w_out, m_ln1_g, m_ln1_b, m_w_ff1, m_w_ff2, m_ln2_g, m_ln2_b, v_w_in, v_v_ln_g, v_v_ln_b, v_w_spatial, v_b_spatial, v_sinks, v_w_out, v_ln1_g, v_ln1_b, v_w_ff1, v_w_ff2, v_ln2_g, v_ln2_b):
    _, t_tok, d = x.shape
    xi, yi, ci = _place()
    core_chip = jnp.stack([ci, 2 * xi + yi]).astype(jnp.int32)
    x2 = x.reshape(t_tok, d)
    target = loss_target.reshape(t_tok, d)
    inv_freq = ROPE_THETA ** (-jnp.arange(0, HEAD_DIM, 2, dtype=F32) / HEAD_DIM)
    wsp, bsp, sink_vec = w_spatial[0], b_spatial[0], sinks[0]
    vg_col, vb_col = v_ln_g.reshape(D_GMLP, 1), v_ln_b.reshape(D_GMLP, 1)
    big = {"in": w_in[0], "out": w_out[0], "ff1": w_ff1[0], "ff2": w_ff2[0]}
    half1, half2 = big["ff1"].shape[1] // 2, big["ff2"].shape[0] // 2
    w1_mine = [big["ff1"][:, :half1].astype(BF16), big["ff1"][:, half1:].astype(BF16)]
    w2_mine = [big["ff2"][:half2].astype(BF16), big["ff2"][half2:].astype(BF16)]

    (cos_t, sin_t), ((g_in,),) = _rope_tables(
        positions, jnp.tile(inv_freq, 2).reshape(HEAD_DIM, 1), comms=[_gather_comm([big["in"].T.astype(BF16)])])
    w_in_t = g_in.reshape(D_IN, d)
    (h_t, xb), ((g_out, w1_a),) = _proj_in(x2, w_in_t, comms=[_gather_comm([big["out"].astype(BF16), w1_mine[0]])])
    w_out_b = g_out.reshape(-1, d)
    band_bias = _band_bias()
    (cat_t,), ((w1_b, w2_a),) = _mixer_fwd(h_t, cos_t, sin_t, wsp, bsp, vg_col, vb_col, sink_vec, band_bias,
                                           comms=[_gather_comm([w1_mine[1], w2_mine[0]])])
    (xhat1, rstd1, x1b), ((w2_b,),) = _proj_out(cat_t, x2, w_out_b, ln1_g, ln1_b, comms=[_gather_comm([w2_mine[1]])])
    act_b, dpre_b, dz2b, dz1, stats = _ffn_fwd_bwd(xhat1, rstd1, x1b, target, [w1_a, w1_b], [w2_a, w2_b], ln1_g, ln1_b, ln2_g, ln2_b)

    (p_ff1,), _ = _ffn_wgrad1(x1b, dpre_b, N_DEV)
    (p_ff2,), ((s_ff1,),) = _ffn_wgrad2(act_b, dz2b, N_DEV, comms=[_sibling_comm([p_ff1])])
    (wire_ff1, own_ff1), _ = _pair_sum("pair_sum_ff1", p_ff1, s_ff1, core_chip)
    rows1 = wire_ff1.shape[1] // 2
    (dcat_t, gw_out), ((s_ff2,), (r_ff1a,)) = _proj_out_bwd(
        dz1, cat_t, w_out_b, comms=[_sibling_comm([p_ff2]), _chips_comm([wire_ff1], rows=(0, rows1))])
    p_out = gw_out.reshape(N_DEV, -1, d)
    (wire_ff2, own_ff2), ((s_out,),) = _pair_sum("pair_sum_ff2", p_ff2, s_ff2, core_chip, comms=[_sibling_comm([p_out])])
    (wire_out, own_out), _ = _pair_sum("pair_sum_out", p_out, s_out, core_chip)
    (dh_b, dkvc_t, dkvp_t, g_wsp, g_bsp, g_vln, g_sink), ((r_ff1b,), (r_ff2, r_out)) = _mixer_bwd(
        dcat_t, h_t, cos_t, sin_t, wsp, bsp, vg_col, vb_col, sink_vec, band_bias,
        comms=[_chips_comm([wire_ff1], rows=(rows1, rows1)), _chips_comm([wire_ff2, wire_out])])
    sink_row = jnp.pad(g_sink.sum(axis=1).reshape(1, N_HEADS), ((0, 0), (0, d - N_HEADS)))
    small_vec = jnp.concatenate([g_vln[0:2].reshape(1, d), stats[0:4], sink_row, stats[4:5], jnp.zeros((1, d), F32),
                                 jnp.pad(g_bsp, ((0, 0), (0, d - BLK)))], axis=0)
    (dkv_b, gw_in_t), ((parts_w, parts_vec),) = _proj_in_wgrad(
        dh_b, dkvc_t, dkvp_t, xb, comms=[_gather_comm([g_wsp.reshape(-1, BLK), small_vec])])
    p_in = gw_in_t.reshape(N_DEV, -1, d)

    out_out, ((s_in,),) = _adamw_shard("adamw_out", own_out, [r_out], big["out"], m_w_out[0], v_w_out[0], comms=[_sibling_comm([p_in])])
    (wire_in, own_in), _ = _pair_sum("pair_sum_in", p_in, s_in, core_chip)
    (grad_x,), ((r_in,),) = _proj_in_dgrad(dh_b, dkv_b, dz1, w_in_t, comms=[_chips_comm([wire_in])])
    ff1_out, _ = _adamw_shard("adamw_ff1", own_ff1, [r_ff1a, r_ff1b], big["ff1"], m_w_ff1[0], v_w_ff1[0])
    ff2_out, _ = _adamw_shard("adamw_ff2", own_ff2, [r_ff2], big["ff2"], m_w_ff2[0], v_w_ff2[0])
    in_out_t, _ = _adamw_shard("adamw_in", own_in, [r_in], big["in"].T, m_w_in[0].T, v_w_in[0].T)
    in_out = [o.T for o in in_out_t]
    small = [(w_spatial, m_w_spatial, v_w_spatial), (v_ln_g, m_v_ln_g, v_v_ln_g), (v_ln_b, m_v_ln_b, v_v_ln_b),
             (b_spatial, m_b_spatial, v_b_spatial), (ln1_g, m_ln1_g, v_ln1_g), (ln1_b, m_ln1_b, v_ln1_b),
             (ln2_g, m_ln2_g, v_ln2_g), (ln2_b, m_ln2_b, v_ln2_b), (sinks, m_sinks, v_sinks)]
    views = [(-1, BLK), None, None, (N_HEADS, BLK)] + [None] * 5
    small_res, loss_row = _adamw_small(parts_w, parts_vec, [
        tuple(a if vw is None else a.reshape(vw) for a in p) for p, vw in zip(small, views)])
    small_out = [[o.reshape(p[0].shape) for o in res] for res, p in zip(small_res, small)]
    loss = loss_row[0, 0]

    big_out = {0: in_out, 6: out_out, 9: ff1_out, 10: ff2_out}
    small_slot = {3: 0, 1: 1, 2: 2, 4: 3, 7: 4, 8: 5, 11: 6, 12: 7, 5: 8}
    outs = [loss, grad_x.reshape(x.shape)]
    for kind in range(4):
        for wi in range(13):
            outs.append(big_out[wi][kind][None] if wi in big_out else small_out[small_slot[wi]][kind])
    return tuple(outs)
```

```python
import functools
import math

import jax
import jax.numpy as jnp
from jax import lax
from jax.experimental import pallas as pl
from jax.experimental.pallas import tpu as pltpu

F32 = jnp.float32
BF16 = jnp.bfloat16
MESH = pl.DeviceIdType.MESH

HEAD_DIM = 64
N_HEADS = 8
N_KV_HEADS = 2
BLK = 128
D_GMLP = N_HEADS * HEAD_DIM
D_ATTN = N_HEADS * HEAD_DIM
D_KV = N_KV_HEADS * HEAD_DIM
D_IN = 2 * D_GMLP + D_ATTN + 2 * D_KV
COL_U, COL_V, COL_Q, COL_K = 0, D_GMLP, 2 * D_GMLP, 2 * D_GMLP + D_ATTN
ROPE_THETA = 10000.0
LN_EPS = 1e-5
ALPHA = 2.0 ** 0.25
NEG_INF = -1e30
SCORE_SCALE = 1.0 / math.sqrt(HEAD_DIM)
ADAM_LR, ADAM_B1, ADAM_B2, ADAM_EPS, ADAM_WD, ADAM_STEP = 0.001, 0.9, 0.999, 1e-08, 0.01, 10
N_DEV = 8
LANES = 128
VMEM_LIMIT = 56 * 1024 * 1024

NT = (((1,), (1,)), ((), ()))
TN = (((0,), (0,)), ((), ()))


def _params(*sem):
    return pltpu.CompilerParams(dimension_semantics=sem, vmem_limit_bytes=VMEM_LIMIT)


def _dot(a, b, dims=None):
    if dims is None:
        return jnp.dot(a, b, preferred_element_type=F32)
    return lax.dot_general(a, b, dims, preferred_element_type=F32)


def _mean(a):
    return jnp.mean(a, axis=-1, keepdims=True)


def _ln_fwd(z, g, b):
    zc = z - _mean(z)
    rstd = lax.rsqrt(_mean(zc * zc) + LN_EPS)
    xhat = zc * rstd
    return xhat * g + b, xhat, rstd


def _ln_bwd(dy, xhat, rstd, g):
    dxhat = dy * g
    return rstd * (dxhat - _mean(dxhat) - xhat * _mean(dxhat * xhat))


_GELU_C = math.sqrt(2.0 / math.pi)


def _gelu(x):
    t = jnp.tanh(_GELU_C * (x + 0.044715 * (x * x * x)))
    return 0.5 * x * (1.0 + t)


def _gelu_and_grad(x):
    x2 = x * x
    t = jnp.tanh(_GELU_C * (x + 0.044715 * (x2 * x)))
    hx, ht = 0.5 * x, 0.5 * (1.0 + t)
    return x * ht, ht + hx * (1.0 - t * t) * (_GELU_C * (1.0 + 3.0 * 0.044715 * x2))


def _mean0(a):
    return jnp.mean(a, axis=0, keepdims=True)


def _ln_fwd_t(z, g, b):
    zc = z - _mean0(z)
    rstd = lax.rsqrt(_mean0(zc * zc) + LN_EPS)
    xhat = zc * rstd
    return xhat * g + b, xhat, rstd


def _ln_bwd_t(dy, xhat, rstd, g):
    dxhat = dy * g
    return rstd * (dxhat - _mean0(dxhat) - xhat * _mean0(dxhat * xhat))


def _rope_t(t, cos, sin_signed, bwd=False):
    half = HEAD_DIM // 2
    outs = []
    for r in range(0, t.shape[0], HEAD_DIM):
        th = t[r:r + HEAD_DIM]
        sw = jnp.concatenate([th[half:], th[:half]], axis=0) * sin_signed
        outs.append(th * cos - sw if bwd else th * cos + sw)
    return jnp.concatenate(outs, axis=0)


ANY = pl.BlockSpec(memory_space=pl.ANY)


def _place():
    return lax.axis_index("x"), lax.axis_index("y"), lax.axis_index("c")


class _Comm:
    def __init__(self, ins, outs, sems, start, finish):
        self.ins, self.outs, self.sems, self.start, self.finish = ins, outs, sems, start, finish


def _gather_comm(arrs):
    n = len(arrs)

    def parts(ins, outs, sems):
        send_sems, recv_sems, local_sems = sems
        x, y, c = _place()
        me, sibling = (x, y, c), (x, y, 1 - c)
        chips = [(1 - x, y), (x, 1 - y), (1 - x, 1 - y)]

        def copy(a, k, block, to, src=None):
            px, py, pc = block
            dst = outs[a].at[4 * px + 2 * py + pc]
            return pltpu.make_async_remote_copy(
                src_ref=dst if src is None else src, dst_ref=dst,
                send_sem=send_sems.at[a, k], recv_sem=recv_sems.at[a, k], device_id=to, device_id_type=MESH)

        mine = [pltpu.make_async_copy(ins[a], outs[a].at[4 * x + 2 * y + c], local_sems.at[a]) for a in range(n)]
        first = []
        for a in range(n):
            first.append(copy(a, 0, me, sibling, src=ins[a]))
            first += [copy(a, 1 + j, me, (*chip, c), src=ins[a]) for j, chip in enumerate(chips)]
        return copy, mine, first, me, sibling, chips, c

    def start(ins, outs, sems):
        _, mine, first, *_ = parts(ins, outs, sems)
        for cp in mine + first:
            cp.start()

    def finish(ins, outs, sems):
        copy, mine, first, me, sibling, chips, c = parts(ins, outs, sems)
        passed = []
        for j, chip in enumerate(chips):
            for a in range(n):
                copy(a, 1 + j, (*chip, c), me).wait_recv()
                fwd = copy(a, 4 + j, (*chip, c), sibling)
                fwd.start()
                passed.append(fwd)
        for a in range(n):
            copy(a, 0, sibling, me).wait_recv()
        for j, chip in enumerate(chips):
            for a in range(n):
                copy(a, 4 + j, (*chip, 1 - c), me).wait_recv()
        for cp in first + passed:
            cp.wait_send()
        for cp in mine:
            cp.wait()

    return _Comm(list(arrs), [jax.ShapeDtypeStruct((N_DEV,) + a.shape, a.dtype) for a in arrs],
                 [pltpu.SemaphoreType.DMA((n, 7)), pltpu.SemaphoreType.DMA((n, 7)), pltpu.SemaphoreType.DMA((n,))],
                 start, finish)


def _sibling_comm(parts):
    n = len(parts)

    def copies(ins, outs, sems):
        x, y, c = _place()
        return [pltpu.make_async_remote_copy(
            src_ref=ins[a].at[2 * q + (1 - c)], dst_ref=outs[a].at[q],
            send_sem=sems[0].at[a, q], recv_sem=sems[1].at[a, q],
            device_id=(x, y, 1 - c), device_id_type=MESH) for a in range(n) for q in range(4)]

    return _Comm(list(parts), [jax.ShapeDtypeStruct((4,) + p.shape[1:], p.dtype) for p in parts],
                 [pltpu.SemaphoreType.DMA((n, 4)), pltpu.SemaphoreType.DMA((n, 4))],
                 lambda *r: [cp.start() for cp in copies(*r)], lambda *r: [cp.wait() for cp in copies(*r)])


def _chips_comm(chip_parts, rows=None):
    n = len(chip_parts)
    r0, nr = (0, None) if rows is None else rows

    def copies(ins, outs, sems):
        x, y, c = _place()
        chips = [(1 - x, y), (x, 1 - y), (1 - x, 1 - y)]
        src = lambda a, q: ins[a].at[q] if rows is None else ins[a].at[q, pl.ds(r0, nr)]
        return [pltpu.make_async_remote_copy(
            src_ref=src(a, 2 * px + py), dst_ref=outs[a].at[k],
            send_sem=sems[0].at[a, k], recv_sem=sems[1].at[a, k],
            device_id=(px, py, c), device_id_type=MESH) for a in range(n) for k, (px, py) in enumerate(chips)]

    shape = lambda p: (3,) + p.shape[1:] if rows is None else (3, nr) + p.shape[2:]
    return _Comm(list(chip_parts), [jax.ShapeDtypeStruct(shape(p), p.dtype) for p in chip_parts],
                 [pltpu.SemaphoreType.DMA((n, 3)), pltpu.SemaphoreType.DMA((n, 3))],
                 lambda *r: [cp.start() for cp in copies(*r)], lambda *r: [cp.wait() for cp in copies(*r)])


def _carry(body, *, name, grid, in_specs, out_specs, out_shape, args, comms=(), scratch_shapes=(), prefetch=()):
    n_pre, n_in, n_out, n_scr = len(prefetch), len(in_specs), len(out_specs), len(scratch_shapes)
    c_ins = [a for cm in comms for a in cm.ins]
    c_outs = [s for cm in comms for s in cm.outs]
    c_sems = [s for cm in comms for s in cm.sems]

    def wrapped(*refs):
        pre, refs = refs[:n_pre], refs[n_pre:]
        ins, refs = refs[:n_in], refs[n_in:]
        cins, refs = refs[:len(c_ins)], refs[len(c_ins):]
        outs, refs = refs[:n_out], refs[n_out:]
        couts, refs = refs[:len(c_outs)], refs[len(c_outs):]
        scr, sems = refs[:n_scr], refs[n_scr:]
        groups, i0, o0, s0 = [], 0, 0, 0
        for cm in comms:
            groups.append((cm, cins[i0:i0 + len(cm.ins)], couts[o0:o0 + len(cm.outs)], sems[s0:s0 + len(cm.sems)]))
            i0, o0, s0 = i0 + len(cm.ins), o0 + len(cm.outs), s0 + len(cm.sems)
        first = pl.program_id(0) == 0
        last = pl.program_id(0) == grid[0] - 1
        for ax in range(1, len(grid)):
            first = first & (pl.program_id(ax) == 0)
            last = last & (pl.program_id(ax) == grid[ax] - 1)
        if comms:
            @pl.when(first)
            def _():
                for cm, ci, co, cs in groups:
                    cm.start(ci, co, cs)
        body(*pre, *ins, *outs, *scr)
        if comms:
            @pl.when(last)
            def _():
                for cm, ci, co, cs in groups:
                    cm.finish(ci, co, cs)

    grid_spec = pltpu.PrefetchScalarGridSpec(
        num_scalar_prefetch=n_pre, grid=grid,
        in_specs=list(in_specs) + [ANY] * len(c_ins), out_specs=list(out_specs) + [ANY] * len(c_outs),
        scratch_shapes=list(scratch_shapes) + c_sems)
    res = pl.pallas_call(
        wrapped, name=name, grid_spec=grid_spec, out_shape=list(out_shape) + c_outs,
        compiler_params=_params(*(["arbitrary"] * len(grid))),
    )(*prefetch, *args, *c_ins)
    outs, rest, per_comm = res[:n_out], res[n_out:], []
    for cm in comms:
        per_comm.append(rest[:len(cm.outs)])
        rest = rest[len(cm.outs):]
    return outs, per_comm


def _rope_tables(pos_row, inv_freq_col, comms=()):
    t_tok = pos_row.shape[1]
    tm = min(512, t_tok)

    def body(pos_ref, invf_ref, cos_ref, sin_ref):
        ang = pos_ref[...].astype(F32) * invf_ref[...]
        row = lax.broadcasted_iota(jnp.int32, ang.shape, 0)
        cos_ref[...] = jnp.cos(ang)
        sin_ref[...] = jnp.sin(ang) * jnp.where(row < HEAD_DIM // 2, -1.0, 1.0)

    return _carry(
        body, name="rope_tables", grid=(t_tok // tm,), comms=comms,
        in_specs=[pl.BlockSpec((1, tm), lambda i: (0, i)), pl.BlockSpec((HEAD_DIM, 1), lambda i: (0, 0))],
        out_specs=[pl.BlockSpec((HEAD_DIM, tm), lambda i: (0, i))] * 2,
        out_shape=[jax.ShapeDtypeStruct((HEAD_DIM, t_tok), F32)] * 2,
        args=(pos_row, inv_freq_col))


def _proj_in(x2, w_in_t, comms=()):
    t_tok, d = x2.shape
    d_in = w_in_t.shape[0]
    tm = min(512, t_tok)

    def body(x_ref, w_ref, h_ref, xb_ref):
        xb = x_ref[...].astype(BF16)
        xb_ref[...] = xb
        h_ref[...] = _dot(w_ref[...], xb, NT)

    return _carry(
        body, name="proj_in", grid=(t_tok // tm,), comms=comms,
        in_specs=[pl.BlockSpec((tm, d), lambda i: (i, 0)), pl.BlockSpec((d_in, d), lambda i: (0, 0))],
        out_specs=[pl.BlockSpec((d_in, tm), lambda i: (0, i)), pl.BlockSpec((tm, d), lambda i: (i, 0))],
        out_shape=[jax.ShapeDtypeStruct((d_in, t_tok), F32), jax.ShapeDtypeStruct((t_tok, d), BF16)],
        args=(x2, w_in_t))


MIX_BLOCKS = 2
MIX_W = MIX_BLOCKS * BLK


def _prev_block(i):
    return jnp.maximum(MIX_BLOCKS * i - 1, 0)


def _h_specs():
    kv_row = COL_K // (2 * D_KV)
    return [
        pl.BlockSpec((D_GMLP, MIX_W), lambda i: (0, i)),
        pl.BlockSpec((D_GMLP, MIX_W), lambda i: (1, i)),
        pl.BlockSpec((D_ATTN, MIX_W), lambda i: (2, i)),
        pl.BlockSpec((2 * D_KV, MIX_W), lambda i: (kv_row, i)),
        pl.BlockSpec((2 * D_KV, BLK), lambda i: (kv_row, _prev_block(i))),
    ]


def _table_specs():
    return [
        pl.BlockSpec((HEAD_DIM, MIX_W), lambda i: (0, i)),
        pl.BlockSpec((HEAD_DIM, MIX_W), lambda i: (0, i)),
        pl.BlockSpec((HEAD_DIM, BLK), lambda i: (0, _prev_block(i))),
        pl.BlockSpec((HEAD_DIM, BLK), lambda i: (0, _prev_block(i))),
    ]


def _cols(b):
    return slice(b * BLK, (b + 1) * BLK)


def _block_inputs(b, i, kvc, kvp_ref, cos, sin, cosp_ref, sinp_ref, bias_ref):
    if b == 0:
        kv_prev, cos_prev, sin_prev, bias = kvp_ref[...], cosp_ref[...], sinp_ref[...], bias_ref[jnp.minimum(i, 1)]
    else:
        kv_prev, cos_prev, sin_prev, bias = kvc[:, _cols(b - 1)], cos[:, _cols(b - 1)], sin[:, _cols(b - 1)], bias_ref[1]
    return kvc[:, _cols(b)], kv_prev, cos[:, _cols(b)], sin[:, _cols(b)], cos_prev, sin_prev, bias


def _band_bias():
    ki = lax.broadcasted_iota(jnp.int32, (2, 2 * BLK, BLK), 1)
    qi = lax.broadcasted_iota(jnp.int32, (2, 2 * BLK, BLK), 2)
    later = lax.broadcasted_iota(jnp.int32, (2, 2 * BLK, BLK), 0) > 0
    dist = qi + BLK - ki
    return jnp.where((dist >= 0) & (dist < BLK) & ((ki >= BLK) | later), 0.0, NEG_INF).astype(F32)


BIAS_SPEC = pl.BlockSpec((2, 2 * BLK, BLK), lambda i: (0, 0, 0))


def _keys_values(kvc, kvp, cosc, sinc, cosp, sinp):
    kp, kc = _rope_t(kvp[:D_KV], cosp, sinp), _rope_t(kvc[:D_KV], cosc, sinc)
    k_t = jnp.concatenate([kp, kc], axis=1).astype(BF16)
    k_n = jnp.concatenate([kp.T, kc.T], axis=0).astype(BF16)
    v_t = jnp.concatenate([kvp[D_KV:], kvc[D_KV:]], axis=1).astype(BF16)
    return k_t, k_n, v_t


def _pad_head(th, kv):
    z = jnp.zeros_like(th)
    return jnp.concatenate([th, z] if kv == 0 else [z, th], axis=0)


def _group_lanes(parts):
    return jnp.concatenate(parts, axis=1)


def _softmax_sink_t(s, sink):
    m = jnp.maximum(jnp.max(s, axis=0, keepdims=True), sink)
    e = jnp.exp(s - m)
    es = jnp.exp(sink - m)
    r = 1.0 / (jnp.sum(e, axis=0, keepdims=True) + es)
    return e * r, es * r


def _causal():
    row = lax.broadcasted_iota(jnp.int32, (BLK, BLK), 0)
    col = lax.broadcasted_iota(jnp.int32, (BLK, BLK), 1)
    return row >= col


def _mask_w_once(wsp_ref, wm_scr):
    @pl.when(pl.program_id(0) == 0)
    def _():
        causal = _causal()
        for hh in range(N_HEADS):
            wm_scr[hh] = jnp.where(causal, wsp_ref[hh], 0.0).astype(BF16)


def _mixer_fwd(h_t, cos_t, sin_t, w_spatial, b_spatial, vln_g, vln_b, sinks, band_bias, comms=()):
    t_tok = h_t.shape[1]
    group = N_HEADS // N_KV_HEADS

    def body(sinks_ref, u_ref, vg_ref, q_ref, kvc_ref, kvp_ref, cos_ref, sin_ref, cosp_ref, sinp_ref,
             wsp_ref, bsp_ref, g_ref, b_ref, bias_ref, cat_ref, wm_scr):
        i = pl.program_id(0)
        _mask_w_once(wsp_ref, wm_scr)
        ua = _gelu(u_ref[...])
        vp, _, _ = _ln_fwd_t(_gelu(vg_ref[...]), g_ref[...], b_ref[...])
        vpb = vp.astype(BF16)
        for b in range(MIX_BLOCKS):
            for hh in range(N_HEADS):
                rows = slice(hh * HEAD_DIM, (hh + 1) * HEAD_DIM)
                mixed = _dot(vpb[rows, _cols(b)], wm_scr[hh], NT) + bsp_ref[hh:hh + 1, :]
                cat_ref[rows, _cols(b)] = (ua[rows, _cols(b)] * mixed).astype(BF16)

        kvc, cos, sin = kvc_ref[...], cos_ref[...], sin_ref[...]
        qr = (_rope_t(q_ref[...], cos, sin) * SCORE_SCALE).astype(BF16)
        sinks4 = [_group_lanes([jnp.full((1, BLK), sinks_ref[hh], F32) for hh in range(kv * group, (kv + 1) * group)])
                  for kv in range(N_KV_HEADS)]
        for b in range(MIX_BLOCKS):
            kv_cur, kv_prev, cosc, sinc, cosp, sinp, bias1 = _block_inputs(b, i, kvc, kvp_ref, cos, sin, cosp_ref, sinp_ref, bias_ref)
            _, k_n, v_t = _keys_values(kv_cur, kv_prev, cosc, sinc, cosp, sinp)
            bias = _group_lanes([bias1] * group)
            for kv in range(N_KV_HEADS):
                heads = range(kv * group, (kv + 1) * group)
                qs = _group_lanes([qr[hh * HEAD_DIM:(hh + 1) * HEAD_DIM, _cols(b)] for hh in heads])
                p, _ = _softmax_sink_t(_dot(k_n, _pad_head(qs, kv)) + bias, sinks4[kv])
                o = _dot(v_t[kv * HEAD_DIM:(kv + 1) * HEAD_DIM], p.astype(BF16)).astype(BF16)
                for j, hh in enumerate(heads):
                    cat_ref[D_GMLP + hh * HEAD_DIM:D_GMLP + (hh + 1) * HEAD_DIM, _cols(b)] = o[:, j * BLK:(j + 1) * BLK]

    full = lambda shape: pl.BlockSpec(shape, lambda i: (0,) * len(shape))
    return _carry(
        body, name="mixer_fwd", grid=(t_tok // MIX_W,), comms=comms,
        in_specs=[pl.BlockSpec(memory_space=pltpu.SMEM)] + _h_specs() + _table_specs() + [
            full((N_HEADS, BLK, BLK)), full((N_HEADS, BLK)), full((D_GMLP, 1)), full((D_GMLP, 1)), BIAS_SPEC],
        out_specs=[pl.BlockSpec((D_GMLP + D_ATTN, MIX_W), lambda i: (0, i))],
        out_shape=[jax.ShapeDtypeStruct((D_GMLP + D_ATTN, t_tok), BF16)],
        scratch_shapes=[pltpu.VMEM((N_HEADS, BLK, BLK), BF16)],
        args=(sinks, h_t, h_t, h_t, h_t, h_t, cos_t, sin_t, cos_t, sin_t, w_spatial, b_spatial, vln_g, vln_b, band_bias))


def _proj_out(cat_t, x2, w_out_b, ln1_g, ln1_b, comms=()):
    t_tok, d = x2.shape
    tm = min(512, t_tok)

    def body(cat_ref, x_ref, w_ref, g_ref, b_ref, xhat_ref, rstd_ref, x1b_ref):
        x1, xhat, rstd = _ln_fwd(ALPHA * x_ref[...] + _dot(cat_ref[...], w_ref[...], TN), g_ref[...], b_ref[...])
        xhat_ref[...] = xhat
        rstd_ref[...] = rstd
        x1b_ref[...] = x1.astype(BF16)

    tok = lambda w: pl.BlockSpec((tm, w), lambda i: (i, 0))
    vec = pl.BlockSpec((1, d), lambda i: (0, 0))
    return _carry(
        body, name="proj_out", grid=(t_tok // tm,), comms=comms,
        in_specs=[pl.BlockSpec((cat_t.shape[0], tm), lambda i: (0, i)), tok(d), pl.BlockSpec(w_out_b.shape, lambda i: (0, 0)), vec, vec],
        out_specs=[tok(d), tok(1), tok(d)],
        out_shape=[jax.ShapeDtypeStruct((t_tok, d), F32), jax.ShapeDtypeStruct((t_tok, 1), F32), jax.ShapeDtypeStruct((t_tok, d), BF16)],
        args=(cat_t, x2, w_out_b, ln1_g, ln1_b))


def _ffn_fwd_bwd(xhat1, rstd1, x1b, target, w1_parts, w2_parts, ln1_g, ln1_b, ln2_g, ln2_b):
    t_tok, d = xhat1.shape
    n_part = len(w1_parts)
    n_chunk, _, fp = w1_parts[0].shape
    fc = n_part * fp
    f = n_chunk * fc
    tm = min(256, t_tok)

    def body(xhat1_ref, rstd1_ref, x1b_ref, tgt_ref, *refs):
        w1_hbm, w2_hbm = refs[:n_part], refs[n_part:2 * n_part]
        (g1_ref, b1_ref, g2_ref, b2_ref, act_ref, dpre_ref, dz2b_ref, dz1_ref, stats_ref,
         r_scr, w1_ref, w2_ref, w_sems) = refs[2 * n_part:]

        @pl.when(pl.program_id(0) == 0)
        def _():
            stats_ref[...] = jnp.zeros_like(stats_ref)
            loads = [pltpu.make_async_copy(w1_hbm[p], w1_ref.at[:, :, pl.ds(p * fp, fp)], w_sems.at[0, p]) for p in range(n_part)]
            loads += [pltpu.make_async_copy(w2_hbm[p], w2_ref.at[:, pl.ds(p * fp, fp), :], w_sems.at[1, p]) for p in range(n_part)]
            for cp in loads:
                cp.start()
            for cp in loads:
                cp.wait()

        g1, g2 = g1_ref[...], g2_ref[...]
        xhat1, x1b = xhat1_ref[...], x1b_ref[...]
        ff = jnp.zeros((tm, d), F32)
        for j in range(n_chunk):
            r = jnp.maximum(_dot(x1b, w1_ref[j]), 0.0)
            r_scr[:, j * fc:(j + 1) * fc] = r
            act = (r * r).astype(BF16)
            act_ref[:, j * fc:(j + 1) * fc] = act
            ff = ff + _dot(act, w2_ref[j])
        y, xhat2, rstd2 = _ln_fwd(ALPHA * (xhat1 * g1 + b1_ref[...]) + ff, g2, b2_ref[...])
        diff = y - tgt_ref[...]
        loss = 0.5 * jnp.sum(jnp.sum(diff * diff, axis=-1, keepdims=True) / d, axis=0, keepdims=True)
        dy = diff / d
        dz2 = _ln_bwd(dy, xhat2, rstd2, g2)
        dz2b = dz2.astype(BF16)
        dz2b_ref[...] = dz2b
        dx1 = ALPHA * dz2
        for j in range(n_chunk):
            dpre = (_dot(dz2b, w2_ref[j], NT) * (2.0 * r_scr[:, j * fc:(j + 1) * fc])).astype(BF16)
            dpre_ref[:, j * fc:(j + 1) * fc] = dpre
            dx1 = dx1 + _dot(dpre, w1_ref[j], NT)
        dz1_ref[...] = _ln_bwd(dx1, xhat1, rstd1_ref[...], g1)
        stats_ref[0:1, :] += jnp.sum(dx1 * xhat1, axis=0, keepdims=True)
        stats_ref[1:2, :] += jnp.sum(dx1, axis=0, keepdims=True)
        stats_ref[2:3, :] += jnp.sum(dy * xhat2, axis=0, keepdims=True)
        stats_ref[3:4, :] += jnp.sum(dy, axis=0, keepdims=True)
        stats_ref[4:5, :] += jnp.broadcast_to(loss, (1, d))

    tok = lambda w: pl.BlockSpec((tm, w), lambda i: (i, 0))
    vec = pl.BlockSpec((1, d), lambda i: (0, 0))
    return _carry(
        body, name="ffn_fwd_bwd", grid=(t_tok // tm,),
        in_specs=[tok(d), tok(1), tok(d), tok(d)] + [ANY] * (2 * n_part) + [vec, vec, vec, vec],
        out_specs=[tok(f), tok(f), tok(d), tok(d), pl.BlockSpec((8, d), lambda i: (0, 0))],
        out_shape=[jax.ShapeDtypeStruct((t_tok, f), BF16), jax.ShapeDtypeStruct((t_tok, f), BF16),
                   jax.ShapeDtypeStruct((t_tok, d), BF16), jax.ShapeDtypeStruct((t_tok, d), F32), jax.ShapeDtypeStruct((8, d), F32)],
        scratch_shapes=[pltpu.VMEM((tm, f), F32), pltpu.VMEM((n_chunk, d, fc), BF16), pltpu.VMEM((n_chunk, fc, d), BF16),
                        pltpu.SemaphoreType.DMA((2, n_part))],
        args=(xhat1, rstd1, x1b, target, *w1_parts, *w2_parts, ln1_g, ln1_b, ln2_g, ln2_b))[0]


def _ffn_wgrad(name, lhs, rhs, chunk_lhs, core_chip, comms=()):
    t_tok = lhs.shape[0]
    half = N_DEV // 2
    fc = (lhs if chunk_lhs else rhs).shape[1] // N_DEV
    chunk = (fc, rhs.shape[1]) if chunk_lhs else (lhs.shape[1], fc)

    def shard(s, cc):
        return 2 * (s % half) + jnp.where(s < half, 1 - cc[0], cc[0])

    def body(cc_ref, lhs_ref, rhs_ref, g_ref, recv_ref, send_buf, send_sems, recv_sems):
        s = pl.program_id(0)
        x, y, c = _place()
        g = _dot(lhs_ref[...], rhs_ref[...], TN)
        g_ref[...] = g

        def send(q):
            return pltpu.make_async_remote_copy(
                src_ref=send_buf.at[q % 2], dst_ref=recv_ref.at[q], send_sem=send_sems.at[q], recv_sem=recv_sems.at[q],
                device_id=(x, y, 1 - c), device_id_type=MESH)

        for q in range(half):
            @pl.when(s == q)
            def _(q=q):
                if q >= 2:
                    send(q - 2).wait_send()
                send_buf[q % 2] = g
                send(q).start()

        @pl.when(s == N_DEV - 1)
        def _():
            for q in range(half - 2, half):
                send(q).wait_send()
            for q in range(half):
                send(q).wait_recv()

    resident = lambda a: pl.BlockSpec(a.shape, lambda s, cc: (0, 0), pipeline_mode=pl.Buffered(1))
    chunked = pl.BlockSpec((t_tok, fc), lambda s, cc: (0, shard(s, cc)))
    (parts, recv), per_comm = _carry(
        body, name=name, grid=(N_DEV,), comms=comms, prefetch=(core_chip,),
        in_specs=[chunked, resident(rhs)] if chunk_lhs else [resident(lhs), chunked],
        out_specs=[pl.BlockSpec((None,) + chunk, lambda s, cc: (shard(s, cc), 0, 0)), ANY],
        out_shape=[jax.ShapeDtypeStruct((N_DEV,) + chunk, F32), jax.ShapeDtypeStruct((half,) + chunk, F32)],
        scratch_shapes=[pltpu.VMEM((2,) + chunk, F32), pltpu.SemaphoreType.DMA((half,)), pltpu.SemaphoreType.DMA((half,))],
        args=(lhs, rhs))
    return parts, recv, per_comm


def _proj_out_bwd(dz1, cat_t, w_out_b, comms=()):
    t_tok, d = dz1.shape
    d_mix = cat_t.shape[0]
    tm = min(512, t_tok)

    def body(dz1_ref, cat_ref, w_ref, dcat_ref, gw_ref):
        @pl.when(pl.program_id(0) == 0)
        def _():
            gw_ref[...] = jnp.zeros_like(gw_ref)

        dzb = dz1_ref[...].astype(BF16)
        dcat_ref[...] = _dot(w_ref[...], dzb, NT)
        gw_ref[...] += _dot(cat_ref[...], dzb)

    return _carry(
        body, name="proj_out_bwd", grid=(t_tok // tm,), comms=comms,
        in_specs=[pl.BlockSpec((tm, d), lambda i: (i, 0)), pl.BlockSpec((d_mix, tm), lambda i: (0, i)),
                  pl.BlockSpec((d_mix, d), lambda i: (0, 0))],
        out_specs=[pl.BlockSpec((d_mix, tm), lambda i: (0, i)), pl.BlockSpec((d_mix, d), lambda i: (0, 0))],
        out_shape=[jax.ShapeDtypeStruct((d_mix, t_tok), F32), jax.ShapeDtypeStruct((d_mix, d), F32)],
        args=(dz1, cat_t, w_out_b))


def _mixer_bwd(dcat_t, h_t, cos_t, sin_t, w_spatial, b_spatial, vln_g, vln_b, sinks, band_bias, comms=()):
    t_tok = h_t.shape[1]
    nb, n_step = t_tok // BLK, t_tok // MIX_W
    group = N_HEADS // N_KV_HEADS

    def body(sinks_ref, dcat_ref, u_ref, vg_ref, q_ref, kvc_ref, kvp_ref, cos_ref, sin_ref, cosp_ref, sinp_ref,
             wsp_ref, bsp_ref, g_ref, b_ref, bias_ref, dh_ref, dkvc_ref, dkvp_ref, gwsb_ref, gbsp_ref, gvln_ref, gsink_ref,
             dg_acc, db_acc, wm_scr, gws_ref):
        i = pl.program_id(0)

        @pl.when(i == 0)
        def _():
            gws_ref[...] = jnp.zeros_like(gws_ref)
            gbsp_ref[...] = jnp.zeros_like(gbsp_ref)
            gsink_ref[...] = jnp.zeros_like(gsink_ref)
            dg_acc[...] = jnp.zeros_like(dg_acc)
            db_acc[...] = jnp.zeros_like(db_acc)

        _mask_w_once(wsp_ref, wm_scr)

        g = g_ref[...]
        ua, ua_grad = _gelu_and_grad(u_ref[...])
        vv, vv_grad = _gelu_and_grad(vg_ref[...])
        vp, vhat, rstd = _ln_fwd_t(vv, g, b_ref[...])
        vpb = vp.astype(BF16)
        da = dcat_ref[0:D_GMLP, :]
        dmixed = da * ua
        dvp_blocks = []
        for b in range(MIX_BLOCKS):
            dvp_parts = []
            for hh in range(N_HEADS):
                rows = slice(hh * HEAD_DIM, (hh + 1) * HEAD_DIM)
                vpb_h = vpb[rows, _cols(b)]
                mixed = _dot(vpb_h, wm_scr[hh], NT) + bsp_ref[hh:hh + 1, :]
                dh_ref[COL_U + hh * HEAD_DIM:COL_U + (hh + 1) * HEAD_DIM, _cols(b)] = (
                    da[rows, _cols(b)] * mixed * ua_grad[rows, _cols(b)]).astype(BF16)
                dm = dmixed[rows, _cols(b)]
                dmb = dm.astype(BF16)
                gbsp_ref[hh:hh + 1, :] += jnp.sum(dm, axis=0, keepdims=True)
                gws_ref[hh] += _dot(dmb, vpb_h, TN)
                dvp_parts.append(_dot(dmb, wm_scr[hh]))
            dvp_blocks.append(jnp.concatenate(dvp_parts, axis=0))
        dvp = jnp.concatenate(dvp_blocks, axis=1)
        dgv, dbv = dvp * vhat, dvp
        for b in range(MIX_BLOCKS):
            dg_acc[...] += dgv[:, _cols(b)]
            db_acc[...] += dbv[:, _cols(b)]
        dh_ref[COL_V:COL_V + D_GMLP, :] = (_ln_bwd_t(dvp, vhat, rstd, g) * vv_grad).astype(BF16)

        kvc, cos, sin = kvc_ref[...], cos_ref[...], sin_ref[...]
        qr = (_rope_t(q_ref[...], cos, sin) * SCORE_SCALE).astype(BF16)
        sinks4 = [_group_lanes([jnp.full((1, BLK), sinks_ref[hh], F32) for hh in range(kv * group, (kv + 1) * group)])
                  for kv in range(N_KV_HEADS)]
        dq_blocks, dkv_cur, dkv_prev = [], [], []
        for b in range(MIX_BLOCKS):
            kv_cur, kv_prev, cosc, sinc, cosp, sinp, bias1 = _block_inputs(b, i, kvc, kvp_ref, cos, sin, cosp_ref, sinp_ref, bias_ref)
            k_t, k_n, v_t = _keys_values(kv_cur, kv_prev, cosc, sinc, cosp, sinp)
            v_n = jnp.concatenate([kv_prev[D_KV:].T, kv_cur[D_KV:].T], axis=0).astype(BF16)
            bias = _group_lanes([bias1] * group)
            dk, dv, dq_parts = [], [], []
            for kv in range(N_KV_HEADS):
                heads = range(kv * group, (kv + 1) * group)
                kv_rows = slice(kv * HEAD_DIM, (kv + 1) * HEAD_DIM)
                qs = _group_lanes([qr[hh * HEAD_DIM:(hh + 1) * HEAD_DIM, _cols(b)] for hh in heads])
                dos = _group_lanes([dcat_ref[D_GMLP + hh * HEAD_DIM:D_GMLP + (hh + 1) * HEAD_DIM, _cols(b)]
                                    for hh in heads]).astype(BF16)
                p, p_sink = _softmax_sink_t(_dot(k_n, _pad_head(qs, kv)) + bias, sinks4[kv])
                dp = _dot(v_n, _pad_head(dos, kv))
                delta = jnp.sum(p * dp, axis=0, keepdims=True)
                ds = (p * (dp - delta)).astype(BF16)
                dsink = p_sink * delta
                dq = _dot(k_t[kv_rows], ds) * SCORE_SCALE
                for j, hh in enumerate(heads):
                    gsink_ref[hh:hh + 1, :] -= dsink[:, j * BLK:(j + 1) * BLK]
                    dq_parts.append(dq[:, j * BLK:(j + 1) * BLK])
                dk.append(_dot(qs, ds, NT))
                dv.append(_dot(dos, p.astype(BF16), NT))
            dq_blocks.append(jnp.concatenate(dq_parts, axis=0))
            dk_all, dv_all = jnp.concatenate(dk, axis=0), jnp.concatenate(dv, axis=0)
            dkv_cur.append(jnp.concatenate([_rope_t(dk_all[:, BLK:], cosc, sinc, bwd=True), dv_all[:, BLK:]], axis=0))
            dkv_prev.append(jnp.concatenate([_rope_t(dk_all[:, :BLK], cosp, sinp, bwd=True), dv_all[:, :BLK]], axis=0))
        dh_ref[COL_Q:COL_Q + D_ATTN, :] = _rope_t(jnp.concatenate(dq_blocks, axis=1), cos, sin, bwd=True).astype(BF16)
        for b in range(MIX_BLOCKS):
            dkvc_ref[:, _cols(b)] = dkv_cur[b] + dkv_prev[b + 1] if b + 1 < MIX_BLOCKS else dkv_cur[b]
        dkvp_ref[...] = dkv_prev[0]

        @pl.when(i == n_step - 1)
        def _():
            causal = _causal()
            for hh in range(N_HEADS):
                gwsb_ref[hh] = jnp.where(causal, gws_ref[hh], 0.0).astype(BF16)
            gvln_ref[...] = jnp.zeros_like(gvln_ref)
            gvln_ref[0:1, :] = jnp.sum(dg_acc[...].T, axis=0, keepdims=True)
            gvln_ref[1:2, :] = jnp.sum(db_acc[...].T, axis=0, keepdims=True)

    full = lambda shape: pl.BlockSpec(shape, lambda i: (0,) * len(shape))
    return _carry(
        body, name="mixer_bwd", grid=(n_step,), comms=comms,
        in_specs=[pl.BlockSpec(memory_space=pltpu.SMEM), pl.BlockSpec((D_GMLP + D_ATTN, MIX_W), lambda i: (0, i))]
        + _h_specs() + _table_specs()
        + [full((N_HEADS, BLK, BLK)), full((N_HEADS, BLK)), full((D_GMLP, 1)), full((D_GMLP, 1)), BIAS_SPEC],
        out_specs=[pl.BlockSpec((COL_K, MIX_W), lambda i: (0, i)), pl.BlockSpec((2 * D_KV, MIX_W), lambda i: (0, i)),
                   pl.BlockSpec((2 * D_KV, BLK), lambda i: (0, (i + n_step - 1) % n_step)),
                   full((N_HEADS, BLK, BLK)), full((N_HEADS, BLK)), full((8, D_GMLP)), full((N_HEADS, LANES))],
        out_shape=[jax.ShapeDtypeStruct((COL_K, t_tok), BF16), jax.ShapeDtypeStruct((2 * D_KV, t_tok), F32),
                   jax.ShapeDtypeStruct((2 * D_KV, n_step * BLK), F32),
                   jax.ShapeDtypeStruct((N_HEADS, BLK, BLK), BF16), jax.ShapeDtypeStruct((N_HEADS, BLK), F32),
                   jax.ShapeDtypeStruct((8, D_GMLP), F32), jax.ShapeDtypeStruct((N_HEADS, LANES), F32)],
        scratch_shapes=[pltpu.VMEM((D_GMLP, BLK), F32), pltpu.VMEM((D_GMLP, BLK), F32), pltpu.VMEM((N_HEADS, BLK, BLK), BF16),
                        pltpu.VMEM((N_HEADS, BLK, BLK), F32)],
        args=(sinks, dcat_t, h_t, h_t, h_t, h_t, h_t, cos_t, sin_t, cos_t, sin_t, w_spatial, b_spatial, vln_g, vln_b, band_bias))


def _proj_in_wgrad(dh_b, dkvc_t, dkvp_t, xb, comms=()):
    t_tok, d = xb.shape
    d_main, d_kv = dh_b.shape[0], dkvc_t.shape[0]
    tm = min(1024, t_tok)

    def body(dh_ref, dkvc_ref, dkvp_ref, xb_ref, dkvb_ref, gw_ref):
        @pl.when(pl.program_id(0) == 0)
        def _():
            gw_ref[...] = jnp.zeros_like(gw_ref)

        for s in range(tm // MIX_W):
            last = slice((s + 1) * MIX_W - BLK, (s + 1) * MIX_W)
            dkvb_ref[:, s * MIX_W:(s + 1) * MIX_W - BLK] = dkvc_ref[:, s * MIX_W:(s + 1) * MIX_W - BLK].astype(BF16)
            dkvb_ref[:, last] = (dkvc_ref[:, last] + dkvp_ref[:, _cols(s)]).astype(BF16)
        gw_ref[0:d_main, :] += _dot(dh_ref[...], xb_ref[...])
        gw_ref[d_main:, :] += _dot(dkvb_ref[...], xb_ref[...])

    tok = lambda rows: pl.BlockSpec((rows, tm), lambda i: (0, i))
    return _carry(
        body, name="proj_in_wgrad", grid=(t_tok // tm,), comms=comms,
        in_specs=[tok(d_main), tok(d_kv), pl.BlockSpec((d_kv, tm // MIX_BLOCKS), lambda i: (0, i)),
                  pl.BlockSpec((tm, d), lambda i: (i, 0))],
        out_specs=[tok(d_kv), pl.BlockSpec((d_main + d_kv, d), lambda i: (0, 0))],
        out_shape=[jax.ShapeDtypeStruct((d_kv, t_tok), BF16), jax.ShapeDtypeStruct((d_main + d_kv, d), F32)],
        args=(dh_b, dkvc_t, dkvp_t, xb))


def _proj_in_dgrad(dh_b, dkv_b, dz1, w_in_t, comms=()):
    t_tok, d = dz1.shape
    d_main, d_kv = dh_b.shape[0], dkv_b.shape[0]
    tm = min(512, t_tok)

    def body(dh_ref, dkv_ref, dz1_ref, w_ref, dx_ref):
        dx_ref[...] = (ALPHA * dz1_ref[...] + _dot(dh_ref[...], w_ref[0:d_main, :], TN)
                       + _dot(dkv_ref[...], w_ref[d_main:, :], TN))

    return _carry(
        body, name="proj_in_dgrad", grid=(t_tok // tm,), comms=comms,
        in_specs=[pl.BlockSpec((d_main, tm), lambda i: (0, i)), pl.BlockSpec((d_kv, tm), lambda i: (0, i)),
                  pl.BlockSpec((tm, d), lambda i: (i, 0)), pl.BlockSpec((d_main + d_kv, d), lambda i: (0, 0))],
        out_specs=[pl.BlockSpec((tm, d), lambda i: (i, 0))],
        out_shape=[jax.ShapeDtypeStruct((t_tok, d), F32)],
        args=(dh_b, dkv_b, dz1, w_in_t))


def _adamw(w, g, m, v):
    m = ADAM_B1 * m + (1.0 - ADAM_B1) * g
    v = ADAM_B2 * v + (1.0 - ADAM_B2) * (g * g)
    m_hat = m / (1.0 - ADAM_B1 ** ADAM_STEP)
    v_hat = v / (1.0 - ADAM_B2 ** ADAM_STEP)
    delta = -ADAM_LR * (m_hat / (jnp.sqrt(v_hat) + ADAM_EPS) + ADAM_WD * w)
    return delta, m, v


def _row_tiled(name, own, recv, extra, n_out, finish, comms=()):
    r, c = own.shape
    recv = [] if recv is None else list(recv)
    k = max(len(recv), 1)
    n = max(k, -(-r // 512))
    tr, per = r // n, n // k
    blk = pl.BlockSpec((tr, c), lambda i: (i, 0))

    def body(own_ref, *refs):
        recv_refs, rest = refs[:len(recv)], refs[len(recv):]
        ins, outs = rest[:len(extra)], rest[len(extra):]

        def tile(recv_ref):
            g = own_ref[...]
            if recv_ref is not None:
                g = ((g + recv_ref[0].astype(F32)) + recv_ref[1].astype(F32)) + recv_ref[2].astype(F32)
            for o_ref, val in zip(outs, finish(g, *[a[...] for a in ins])):
                o_ref[...] = val

        if len(recv) <= 1:
            tile(recv_refs[0] if recv else None)
        else:
            for p in range(k):
                pl.when(pl.program_id(0) // per == p)(functools.partial(tile, recv_refs[p]))

    recv_specs = [pl.BlockSpec((3, tr, c), lambda i, p=p: (0, jnp.clip(i - p * per, 0, per - 1), 0)) for p in range(len(recv))]
    return _carry(
        body, name=name, grid=(n,), comms=comms,
        in_specs=[blk] + recv_specs + [blk] * len(extra),
        out_specs=[blk] * n_out, out_shape=[jax.ShapeDtypeStruct((r, c), F32)] * n_out,
        args=(own, *recv, *extra))


def _adamw_shard(name, own, recv, w, m, v, comms=()):
    def finish(g, w_t, m_t, v_t):
        return (g,) + _adamw(w_t, g, m_t, v_t)

    return _row_tiled(name, own, recv, (w, m, v), 4, finish, comms)


VEC_VLN, VEC_LN1G, VEC_LN1B, VEC_LN2G, VEC_LN2B, VEC_SINK, VEC_LOSS, VEC_BSP, VEC_ROWS = 0, 1, 2, 3, 4, 5, 6, 8, 16


def _adamw_small(parts_w, parts_vec, params):
    n = parts_w.shape[0]
    flat = [a for p in params for a in p]
    shapes = [p[0].shape for p in params]

    def grads(gw, gv):
        return [gw, gv[VEC_VLN:VEC_VLN + 1, 0:D_GMLP], gv[VEC_VLN:VEC_VLN + 1, D_GMLP:2 * D_GMLP],
                gv[VEC_BSP:VEC_BSP + N_HEADS, 0:BLK], gv[VEC_LN1G:VEC_LN1G + 1], gv[VEC_LN1B:VEC_LN1B + 1],
                gv[VEC_LN2G:VEC_LN2G + 1], gv[VEC_LN2B:VEC_LN2B + 1], gv[VEC_SINK:VEC_SINK + 1, 0:N_HEADS]]

    def body(pw_ref, pv_ref, *refs):
        ins, outs = refs[:len(flat)], refs[len(flat):]
        gw, gv = pw_ref[0].astype(F32), pv_ref[0]
        for k in range(1, n):
            gw, gv = gw + pw_ref[k].astype(F32), gv + pv_ref[k]
        for i, g in enumerate(grads(gw, gv)):
            w_ref, m_ref, v_ref = ins[3 * i:3 * i + 3]
            delta, m_new, v_new = _adamw(w_ref[...], g, m_ref[...], v_ref[...])
            for o_ref, val in zip(outs[4 * i:4 * i + 4], (g, delta, m_new, v_new)):
                o_ref[...] = val
        outs[-1][...] = gv[VEC_LOSS:VEC_LOSS + 1, 0:LANES]

    whole = lambda shape: pl.BlockSpec(shape, lambda i: (0,) * len(shape))
    res = _carry(
        body, name="adamw_small", grid=(1,),
        in_specs=[whole(parts_w.shape), whole(parts_vec.shape)] + [whole(a.shape) for a in flat],
        out_specs=[whole(s) for s in shapes for _ in range(4)] + [whole((1, LANES))],
        out_shape=[jax.ShapeDtypeStruct(s, F32) for s in shapes for _ in range(4)] + [jax.ShapeDtypeStruct((1, LANES), F32)],
        args=(parts_w, parts_vec, *flat))[0]
    return [res[4 * i:4 * i + 4] for i in range(len(params))], res[-1]


def _pair_sum(name, parts, recv, core_chip, comms=()):
    _, r, c = parts.shape
    tr = r if r <= 512 else 512

    def body(cc_ref, a_ref, b_ref, wire_ref, own_ref):
        s = a_ref[...] + b_ref[...]
        wire_ref[...] = s.astype(BF16)

        @pl.when(pl.program_id(1) == cc_ref[1])
        def _():
            own_ref[...] = s

    return _carry(
        body, name=name, grid=(r // tr, 4), prefetch=(core_chip,), comms=comms,
        in_specs=[pl.BlockSpec((None, tr, c), lambda i, q, cc: (2 * q + cc[0], i, 0)),
                  pl.BlockSpec((None, tr, c), lambda i, q, cc: (q, i, 0))],
        out_specs=[pl.BlockSpec((None, tr, c), lambda i, q, cc: (q, i, 0)), pl.BlockSpec((tr, c), lambda i, q, cc: (i, 0))],
        out_shape=[jax.ShapeDtypeStruct((4, r, c), BF16), jax.ShapeDtypeStruct((r, c), F32)],
        args=(parts, recv))


def kernel(x, positions, w_in, v_ln_g, v_ln_b, w_spatial, b_spatial, sinks, w_out, ln1_g, ln1_b, w_ff1, w_ff2, ln2_g, ln2_b, loss_target, m_w_in, m_v_ln_g, m_v_ln_b, m_w_spatial, m_b_spatial, m_sinks, m_w_out, m_ln1_g, m_ln1_b, m_w_ff1, m_w_ff2, m_ln2_g, m_ln2_b, v_w_in, v_v_ln_g, v_v_ln_b, v_w_spatial, v_b_spatial, v_sinks, v_w_out, v_ln1_g, v_ln1_b, v_w_ff1, v_w_ff2, v_ln2_g, v_ln2_b):
    _, t_tok, d = x.shape
    xi, yi, ci = _place()
    core_chip = jnp.stack([ci, 2 * xi + yi]).astype(jnp.int32)
    x2 = x.reshape(t_tok, d)
    target = loss_target.reshape(t_tok, d)
    inv_freq = ROPE_THETA ** (-jnp.arange(0, HEAD_DIM, 2, dtype=F32) / HEAD_DIM)
    wsp, bsp, sink_vec = w_spatial[0], b_spatial[0], sinks[0]
    vg_col, vb_col = v_ln_g.reshape(D_GMLP, 1), v_ln_b.reshape(D_GMLP, 1)
    big = {"in": w_in[0], "out": w_out[0], "ff1": w_ff1[0], "ff2": w_ff2[0]}
    half1, half2 = big["ff1"].shape[1] // 2, big["ff2"].shape[0] // 2
    w1_mine = [big["ff1"][:, :half1].astype(BF16), big["ff1"][:, half1:].astype(BF16)]
    w2_mine = [big["ff2"][:half2].astype(BF16), big["ff2"][half2:].astype(BF16)]

    (cos_t, sin_t), ((g_in,),) = _rope_tables(
        positions, jnp.tile(inv_freq, 2).reshape(HEAD_DIM, 1), comms=[_gather_comm([big["in"].T.astype(BF16)])])
    w_in_t = g_in.reshape(D_IN, d)
    (h_t, xb), ((g_out, w1_a),) = _proj_in(x2, w_in_t, comms=[_gather_comm([big["out"].astype(BF16), w1_mine[0]])])
    w_out_b = g_out.reshape(-1, d)
    band_bias = _band_bias()
    (cat_t,), ((w1_b, w2_a),) = _mixer_fwd(h_t, cos_t, sin_t, wsp, bsp, vg_col, vb_col, sink_vec, band_bias,
                                           comms=[_gather_comm([w1_mine[1], w2_mine[0]])])
    (xhat1, rstd1, x1b), ((w2_b,),) = _proj_out(cat_t, x2, w_out_b, ln1_g, ln1_b, comms=[_gather_comm([w2_mine[1]])])
    act_b, dpre_b, dz2b, dz1, stats = _ffn_fwd_bwd(xhat1, rstd1, x1b, target, [w1_a, w1_b], [w2_a, w2_b], ln1_g, ln1_b, ln2_g, ln2_b)

    (dcat_t, gw_out), _ = _proj_out_bwd(dz1, cat_t, w_out_b)
    p_out = gw_out.reshape(N_DEV, -1, d)
    p_ff1, s_ff1, ((s_out,),) = _ffn_wgrad("ffn_wgrad1", x1b, dpre_b, False, core_chip, comms=[_sibling_comm([p_out])])
    (wire_ff1, own_ff1), _ = _pair_sum("pair_sum_ff1", p_ff1, s_ff1, core_chip)
    (wire_out, own_out), _ = _pair_sum("pair_sum_out", p_out, s_out, core_chip)
    p_ff2, s_ff2, ((r_ff1,),) = _ffn_wgrad("ffn_wgrad2", act_b, dz2b, True, core_chip, comms=[_chips_comm([wire_ff1])])
    (wire_ff2, own_ff2), _ = _pair_sum("pair_sum_ff2", p_ff2, s_ff2, core_chip)
    (dh_b, dkvc_t, dkvp_t, g_wsp, g_bsp, g_vln, g_sink), ((r_ff2, r_out),) = _mixer_bwd(
        dcat_t, h_t, cos_t, sin_t, wsp, bsp, vg_col, vb_col, sink_vec, band_bias,
        comms=[_chips_comm([wire_ff2, wire_out])])
    sink_row = jnp.pad(g_sink.sum(axis=1).reshape(1, N_HEADS), ((0, 0), (0, d - N_HEADS)))
    small_vec = jnp.concatenate([g_vln[0:2].reshape(1, d), stats[0:4], sink_row, stats[4:5], jnp.zeros((1, d), F32),
                                 jnp.pad(g_bsp, ((0, 0), (0, d - BLK)))], axis=0)
    (dkv_b, gw_in_t), ((parts_w, parts_vec),) = _proj_in_wgrad(
        dh_b, dkvc_t, dkvp_t, xb, comms=[_gather_comm([g_wsp.reshape(-1, BLK), small_vec])])
    p_in = gw_in_t.reshape(N_DEV, -1, d)

    out_out, ((s_in,),) = _adamw_shard("adamw_out", own_out, [r_out], big["out"], m_w_out[0], v_w_out[0], comms=[_sibling_comm([p_in])])
    (wire_in, own_in), _ = _pair_sum("pair_sum_in", p_in, s_in, core_chip)
    (grad_x,), ((r_in,),) = _proj_in_dgrad(dh_b, dkv_b, dz1, w_in_t, comms=[_chips_comm([wire_in])])
    ff1_out, _ = _adamw_shard("adamw_ff1", own_ff1, [r_ff1], big["ff1"], m_w_ff1[0], v_w_ff1[0])
    ff2_out, _ = _adamw_shard("adamw_ff2", own_ff2, [r_ff2], big["ff2"], m_w_ff2[0], v_w_ff2[0])
    in_out_t, _ = _adamw_shard("adamw_in", own_in, [r_in], big["in"].T, m_w_in[0].T, v_w_in[0].T)
    in_out = [o.T for o in in_out_t]
    small = [(w_spatial, m_w_spatial, v_w_spatial), (v_ln_g, m_v_ln_g, v_v_ln_g), (v_ln_b, m_v_ln_b, v_v_ln_b),
             (b_spatial, m_b_spatial, v_b_spatial), (ln1_g, m_ln1_g, v_ln1_g), (ln1_b, m_ln1_b, v_ln1_b),
             (ln2_g, m_ln2_g, v_ln2_g), (ln2_b, m_ln2_b, v_ln2_b), (sinks, m_sinks, v_sinks)]
    views = [(-1, BLK), None, None, (N_HEADS, BLK)] + [None] * 5
    small_res, loss_row = _adamw_small(parts_w, parts_vec, [
        tuple(a if vw is None else a.reshape(vw) for a in p) for p, vw in zip(small, views)])
    small_out = [[o.reshape(p[0].shape) for o in res] for res, p in zip(small_res, small)]
    loss = loss_row[0, 0]

    big_out = {0: in_out, 6: out_out, 9: ff1_out, 10: ff2_out}
    small_slot = {3: 0, 1: 1, 2: 2, 4: 3, 7: 4, 8: 5, 11: 6, 12: 7, 5: 8}
    outs = [loss, grad_x.reshape(x.shape)]
    for kind in range(4):
        for wi in range(13):
            outs.append(big_out[wi][kind][None] if wi in big_out else small_out[small_slot[wi]][kind])
    return tuple(outs)
```

```python
import functools
import math

import jax
import jax.numpy as jnp
from jax import lax
from jax.experimental import pallas as pl
from jax.experimental.pallas import tpu as pltpu

F32 = jnp.float32
BF16 = jnp.bfloat16
MESH = pl.DeviceIdType.MESH

HEAD_DIM = 64
N_HEADS = 8
N_KV_HEADS = 2
BLK = 128
D_GMLP = N_HEADS * HEAD_DIM
D_ATTN = N_HEADS * HEAD_DIM
D_KV = N_KV_HEADS * HEAD_DIM
D_IN = 2 * D_GMLP + D_ATTN + 2 * D_KV
COL_U, COL_V, COL_Q, COL_K = 0, D_GMLP, 2 * D_GMLP, 2 * D_GMLP + D_ATTN
ROPE_THETA = 10000.0
LN_EPS = 1e-5
ALPHA = 2.0 ** 0.25
NEG_INF = -1e30
SCORE_SCALE = 1.0 / math.sqrt(HEAD_DIM)
ADAM_LR, ADAM_B1, ADAM_B2, ADAM_EPS, ADAM_WD, ADAM_STEP = 0.001, 0.9, 0.999, 1e-08, 0.01, 10
N_DEV = 8
LANES = 128
VMEM_LIMIT = 56 * 1024 * 1024

NT = (((1,), (1,)), ((), ()))
TN = (((0,), (0,)), ((), ()))


def _params(*sem):
    return pltpu.CompilerParams(dimension_semantics=sem, vmem_limit_bytes=VMEM_LIMIT)


def _dot(a, b, dims=None):
    if dims is None:
        return jnp.dot(a, b, preferred_element_type=F32)
    return lax.dot_general(a, b, dims, preferred_element_type=F32)


def _mean(a):
    return jnp.mean(a, axis=-1, keepdims=True)


def _ln_fwd(z, g, b):
    zc = z - _mean(z)
    rstd = lax.rsqrt(_mean(zc * zc) + LN_EPS)
    xhat = zc * rstd
    return xhat * g + b, xhat, rstd


def _ln_bwd(dy, xhat, rstd, g):
    dxhat = dy * g
    return rstd * (dxhat - _mean(dxhat) - xhat * _mean(dxhat * xhat))


_GELU_C = math.sqrt(2.0 / math.pi)


def _gelu(x):
    t = jnp.tanh(_GELU_C * (x + 0.044715 * (x * x * x)))
    return 0.5 * x * (1.0 + t)


def _gelu_and_grad(x):
    x2 = x * x
    t = jnp.tanh(_GELU_C * (x + 0.044715 * (x2 * x)))
    hx, ht = 0.5 * x, 0.5 * (1.0 + t)
    return x * ht, ht + hx * (1.0 - t * t) * (_GELU_C * (1.0 + 3.0 * 0.044715 * x2))


def _mean0(a):
    return jnp.mean(a, axis=0, keepdims=True)


def _ln_fwd_t(z, g, b):
    zc = z - _mean0(z)
    rstd = lax.rsqrt(_mean0(zc * zc) + LN_EPS)
    xhat = zc * rstd
    return xhat * g + b, xhat, rstd


def _ln_bwd_t(dy, xhat, rstd, g):
    dxhat = dy * g
    return rstd * (dxhat - _mean0(dxhat) - xhat * _mean0(dxhat * xhat))


def _rope_t(t, cos, sin_signed, bwd=False):
    half = HEAD_DIM // 2
    outs = []
    for r in range(0, t.shape[0], HEAD_DIM):
        th = t[r:r + HEAD_DIM]
        sw = jnp.concatenate([th[half:], th[:half]], axis=0) * sin_signed
        outs.append(th * cos - sw if bwd else th * cos + sw)
    return jnp.concatenate(outs, axis=0)


ANY = pl.BlockSpec(memory_space=pl.ANY)


def _place():
    return lax.axis_index("x"), lax.axis_index("y"), lax.axis_index("c")


class _Comm:
    def __init__(self, ins, outs, sems, start, finish):
        self.ins, self.outs, self.sems, self.start, self.finish = ins, outs, sems, start, finish


def _gather_comm(arrs):
    n = len(arrs)

    def parts(ins, outs, sems):
        send_sems, recv_sems, local_sems = sems
        x, y, c = _place()
        me, sibling = (x, y, c), (x, y, 1 - c)
        chips = [(1 - x, y), (x, 1 - y), (1 - x, 1 - y)]

        def copy(a, k, block, to, src=None):
            px, py, pc = block
            dst = outs[a].at[4 * px + 2 * py + pc]
            return pltpu.make_async_remote_copy(
                src_ref=dst if src is None else src, dst_ref=dst,
                send_sem=send_sems.at[a, k], recv_sem=recv_sems.at[a, k], device_id=to, device_id_type=MESH)

        mine = [pltpu.make_async_copy(ins[a], outs[a].at[4 * x + 2 * y + c], local_sems.at[a]) for a in range(n)]
        first = []
        for a in range(n):
            first.append(copy(a, 0, me, sibling, src=ins[a]))
            first += [copy(a, 1 + j, me, (*chip, c), src=ins[a]) for j, chip in enumerate(chips)]
        return copy, mine, first, me, sibling, chips, c

    def start(ins, outs, sems):
        _, mine, first, *_ = parts(ins, outs, sems)
        for cp in mine + first:
            cp.start()

    def finish(ins, outs, sems):
        copy, mine, first, me, sibling, chips, c = parts(ins, outs, sems)
        passed = []
        for j, chip in enumerate(chips):
            for a in range(n):
                copy(a, 1 + j, (*chip, c), me).wait_recv()
                fwd = copy(a, 4 + j, (*chip, c), sibling)
                fwd.start()
                passed.append(fwd)
        for a in range(n):
            copy(a, 0, sibling, me).wait_recv()
        for j, chip in enumerate(chips):
            for a in range(n):
                copy(a, 4 + j, (*chip, 1 - c), me).wait_recv()
        for cp in first + passed:
            cp.wait_send()
        for cp in mine:
            cp.wait()

    return _Comm(list(arrs), [jax.ShapeDtypeStruct((N_DEV,) + a.shape, a.dtype) for a in arrs],
                 [pltpu.SemaphoreType.DMA((n, 7)), pltpu.SemaphoreType.DMA((n, 7)), pltpu.SemaphoreType.DMA((n,))],
                 start, finish)


def _sibling_comm(parts):
    n = len(parts)

    def copies(ins, outs, sems):
        x, y, c = _place()
        return [pltpu.make_async_remote_copy(
            src_ref=ins[a].at[2 * q + (1 - c)], dst_ref=outs[a].at[q],
            send_sem=sems[0].at[a, q], recv_sem=sems[1].at[a, q],
            device_id=(x, y, 1 - c), device_id_type=MESH) for a in range(n) for q in range(4)]

    return _Comm(list(parts), [jax.ShapeDtypeStruct((4,) + p.shape[1:], p.dtype) for p in parts],
                 [pltpu.SemaphoreType.DMA((n, 4)), pltpu.SemaphoreType.DMA((n, 4))],
                 lambda *r: [cp.start() for cp in copies(*r)], lambda *r: [cp.wait() for cp in copies(*r)])


def _chips_comm(chip_parts, rows=None):
    n = len(chip_parts)
    r0, nr = (0, None) if rows is None else rows

    def copies(ins, outs, sems):
        x, y, c = _place()
        chips = [(1 - x, y), (x, 1 - y), (1 - x, 1 - y)]
        src = lambda a, q: ins[a].at[q] if rows is None else ins[a].at[q, pl.ds(r0, nr)]
        return [pltpu.make_async_remote_copy(
            src_ref=src(a, 2 * px + py), dst_ref=outs[a].at[k],
            send_sem=sems[0].at[a, k], recv_sem=sems[1].at[a, k],
            device_id=(px, py, c), device_id_type=MESH) for a in range(n) for k, (px, py) in enumerate(chips)]

    shape = lambda p: (3,) + p.shape[1:] if rows is None else (3, nr) + p.shape[2:]
    return _Comm(list(chip_parts), [jax.ShapeDtypeStruct(shape(p), p.dtype) for p in chip_parts],
                 [pltpu.SemaphoreType.DMA((n, 3)), pltpu.SemaphoreType.DMA((n, 3))],
                 lambda *r: [cp.start() for cp in copies(*r)], lambda *r: [cp.wait() for cp in copies(*r)])


def _carry(body, *, name, grid, in_specs, out_specs, out_shape, args, comms=(), scratch_shapes=(), prefetch=()):
    n_pre, n_in, n_out, n_scr = len(prefetch), len(in_specs), len(out_specs), len(scratch_shapes)
    c_ins = [a for cm in comms for a in cm.ins]
    c_outs = [s for cm in comms for s in cm.outs]
    c_sems = [s for cm in comms for s in cm.sems]

    def wrapped(*refs):
        pre, refs = refs[:n_pre], refs[n_pre:]
        ins, refs = refs[:n_in], refs[n_in:]
        cins, refs = refs[:len(c_ins)], refs[len(c_ins):]
        outs, refs = refs[:n_out], refs[n_out:]
        couts, refs = refs[:len(c_outs)], refs[len(c_outs):]
        scr, sems = refs[:n_scr], refs[n_scr:]
        groups, i0, o0, s0 = [], 0, 0, 0
        for cm in comms:
            groups.append((cm, cins[i0:i0 + len(cm.ins)], couts[o0:o0 + len(cm.outs)], sems[s0:s0 + len(cm.sems)]))
            i0, o0, s0 = i0 + len(cm.ins), o0 + len(cm.outs), s0 + len(cm.sems)
        first = pl.program_id(0) == 0
        last = pl.program_id(0) == grid[0] - 1
        for ax in range(1, len(grid)):
            first = first & (pl.program_id(ax) == 0)
            last = last & (pl.program_id(ax) == grid[ax] - 1)
        if comms:
            @pl.when(first)
            def _():
                for cm, ci, co, cs in groups:
                    cm.start(ci, co, cs)
        body(*pre, *ins, *outs, *scr)
        if comms:
            @pl.when(last)
            def _():
                for cm, ci, co, cs in groups:
                    cm.finish(ci, co, cs)

    grid_spec = pltpu.PrefetchScalarGridSpec(
        num_scalar_prefetch=n_pre, grid=grid,
        in_specs=list(in_specs) + [ANY] * len(c_ins), out_specs=list(out_specs) + [ANY] * len(c_outs),
        scratch_shapes=list(scratch_shapes) + c_sems)
    res = pl.pallas_call(
        wrapped, name=name, grid_spec=grid_spec, out_shape=list(out_shape) + c_outs,
        compiler_params=_params(*(["arbitrary"] * len(grid))),
    )(*prefetch, *args, *c_ins)
    outs, rest, per_comm = res[:n_out], res[n_out:], []
    for cm in comms:
        per_comm.append(rest[:len(cm.outs)])
        rest = rest[len(cm.outs):]
    return outs, per_comm


def _rope_tables(pos_row, inv_freq_col, comms=()):
    t_tok = pos_row.shape[1]
    tm = min(512, t_tok)

    def body(pos_ref, invf_ref, cos_ref, sin_ref):
        ang = pos_ref[...].astype(F32) * invf_ref[...]
        row = lax.broadcasted_iota(jnp.int32, ang.shape, 0)
        cos_ref[...] = jnp.cos(ang)
        sin_ref[...] = jnp.sin(ang) * jnp.where(row < HEAD_DIM // 2, -1.0, 1.0)

    return _carry(
        body, name="rope_tables", grid=(t_tok // tm,), comms=comms,
        in_specs=[pl.BlockSpec((1, tm), lambda i: (0, i)), pl.BlockSpec((HEAD_DIM, 1), lambda i: (0, 0))],
        out_specs=[pl.BlockSpec((HEAD_DIM, tm), lambda i: (0, i))] * 2,
        out_shape=[jax.ShapeDtypeStruct((HEAD_DIM, t_tok), F32)] * 2,
        args=(pos_row, inv_freq_col))


def _proj_in(x2, w_in_t, comms=()):
    t_tok, d = x2.shape
    d_in = w_in_t.shape[0]
    tm = min(512, t_tok)

    def body(x_ref, w_ref, h_ref, xb_ref):
        xb = x_ref[...].astype(BF16)
        xb_ref[...] = xb
        h_ref[...] = _dot(w_ref[...], xb, NT)

    return _carry(
        body, name="proj_in", grid=(t_tok // tm,), comms=comms,
        in_specs=[pl.BlockSpec((tm, d), lambda i: (i, 0)), pl.BlockSpec((d_in, d), lambda i: (0, 0))],
        out_specs=[pl.BlockSpec((d_in, tm), lambda i: (0, i)), pl.BlockSpec((tm, d), lambda i: (i, 0))],
        out_shape=[jax.ShapeDtypeStruct((d_in, t_tok), F32), jax.ShapeDtypeStruct((t_tok, d), BF16)],
        args=(x2, w_in_t))


MIX_BLOCKS = 2
MIX_W = MIX_BLOCKS * BLK


def _prev_block(i):
    return jnp.maximum(MIX_BLOCKS * i - 1, 0)


def _h_specs():
    kv_row = COL_K // (2 * D_KV)
    return [
        pl.BlockSpec((D_GMLP, MIX_W), lambda i: (0, i)),
        pl.BlockSpec((D_GMLP, MIX_W), lambda i: (1, i)),
        pl.BlockSpec((D_ATTN, MIX_W), lambda i: (2, i)),
        pl.BlockSpec((2 * D_KV, MIX_W), lambda i: (kv_row, i)),
        pl.BlockSpec((2 * D_KV, BLK), lambda i: (kv_row, _prev_block(i))),
    ]


def _table_specs():
    return [
        pl.BlockSpec((HEAD_DIM, MIX_W), lambda i: (0, i)),
        pl.BlockSpec((HEAD_DIM, MIX_W), lambda i: (0, i)),
        pl.BlockSpec((HEAD_DIM, BLK), lambda i: (0, _prev_block(i))),
        pl.BlockSpec((HEAD_DIM, BLK), lambda i: (0, _prev_block(i))),
    ]


def _cols(b):
    return slice(b * BLK, (b + 1) * BLK)


def _block_inputs(b, i, kvc, kvp_ref, cos, sin, cosp_ref, sinp_ref, bias_ref):
    if b == 0:
        kv_prev, cos_prev, sin_prev, bias = kvp_ref[...], cosp_ref[...], sinp_ref[...], bias_ref[jnp.minimum(i, 1)]
    else:
        kv_prev, cos_prev, sin_prev, bias = kvc[:, _cols(b - 1)], cos[:, _cols(b - 1)], sin[:, _cols(b - 1)], bias_ref[1]
    return kvc[:, _cols(b)], kv_prev, cos[:, _cols(b)], sin[:, _cols(b)], cos_prev, sin_prev, bias


def _band_bias():
    ki = lax.broadcasted_iota(jnp.int32, (2, 2 * BLK, BLK), 1)
    qi = lax.broadcasted_iota(jnp.int32, (2, 2 * BLK, BLK), 2)
    later = lax.broadcasted_iota(jnp.int32, (2, 2 * BLK, BLK), 0) > 0
    dist = qi + BLK - ki
    return jnp.where((dist >= 0) & (dist < BLK) & ((ki >= BLK) | later), 0.0, NEG_INF).astype(F32)


BIAS_SPEC = pl.BlockSpec((2, 2 * BLK, BLK), lambda i: (0, 0, 0))


def _keys_values(kvc, kvp, cosc, sinc, cosp, sinp):
    kp, kc = _rope_t(kvp[:D_KV], cosp, sinp), _rope_t(kvc[:D_KV], cosc, sinc)
    k_t = jnp.concatenate([kp, kc], axis=1).astype(BF16)
    k_n = jnp.concatenate([kp.T, kc.T], axis=0).astype(BF16)
    v_t = jnp.concatenate([kvp[D_KV:], kvc[D_KV:]], axis=1).astype(BF16)
    return k_t, k_n, v_t


def _pad_head(th, kv):
    z = jnp.zeros_like(th)
    return jnp.concatenate([th, z] if kv == 0 else [z, th], axis=0)


def _group_lanes(parts):
    return jnp.concatenate(parts, axis=1)


def _softmax_sink_t(s, sink):
    m = jnp.maximum(jnp.max(s, axis=0, keepdims=True), sink)
    e = jnp.exp(s - m)
    es = jnp.exp(sink - m)
    r = 1.0 / (jnp.sum(e, axis=0, keepdims=True) + es)
    return e * r, es * r


def _causal():
    row = lax.broadcasted_iota(jnp.int32, (BLK, BLK), 0)
    col = lax.broadcasted_iota(jnp.int32, (BLK, BLK), 1)
    return row >= col


def _mask_w_once(wsp_ref, wm_scr):
    @pl.when(pl.program_id(0) == 0)
    def _():
        causal = _causal()
        for hh in range(N_HEADS):
            wm_scr[hh] = jnp.where(causal, wsp_ref[hh], 0.0).astype(BF16)


def _mixer_fwd(h_t, cos_t, sin_t, w_spatial, b_spatial, vln_g, vln_b, sinks, band_bias, comms=()):
    t_tok = h_t.shape[1]
    group = N_HEADS // N_KV_HEADS

    def body(sinks_ref, u_ref, vg_ref, q_ref, kvc_ref, kvp_ref, cos_ref, sin_ref, cosp_ref, sinp_ref,
             wsp_ref, bsp_ref, g_ref, b_ref, bias_ref, cat_ref, wm_scr):
        i = pl.program_id(0)
        _mask_w_once(wsp_ref, wm_scr)
        ua = _gelu(u_ref[...])
        vp, _, _ = _ln_fwd_t(_gelu(vg_ref[...]), g_ref[...], b_ref[...])
        vpb = vp.astype(BF16)
        for b in range(MIX_BLOCKS):
            for hh in range(N_HEADS):
                rows = slice(hh * HEAD_DIM, (hh + 1) * HEAD_DIM)
                mixed = _dot(vpb[rows, _cols(b)], wm_scr[hh], NT) + bsp_ref[hh:hh + 1, :]
                cat_ref[rows, _cols(b)] = (ua[rows, _cols(b)] * mixed).astype(BF16)

        kvc, cos, sin = kvc_ref[...], cos_ref[...], sin_ref[...]
        qr = (_rope_t(q_ref[...], cos, sin) * SCORE_SCALE).astype(BF16)
        sinks4 = [_group_lanes([jnp.full((1, BLK), sinks_ref[hh], F32) for hh in range(kv * group, (kv + 1) * group)])
                  for kv in range(N_KV_HEADS)]
        for b in range(MIX_BLOCKS):
            kv_cur, kv_prev, cosc, sinc, cosp, sinp, bias1 = _block_inputs(b, i, kvc, kvp_ref, cos, sin, cosp_ref, sinp_ref, bias_ref)
            _, k_n, v_t = _keys_values(kv_cur, kv_prev, cosc, sinc, cosp, sinp)
            bias = _group_lanes([bias1] * group)
            for kv in range(N_KV_HEADS):
                heads = range(kv * group, (kv + 1) * group)
                qs = _group_lanes([qr[hh * HEAD_DIM:(hh + 1) * HEAD_DIM, _cols(b)] for hh in heads])
                p, _ = _softmax_sink_t(_dot(k_n, _pad_head(qs, kv)) + bias, sinks4[kv])
                o = _dot(v_t[kv * HEAD_DIM:(kv + 1) * HEAD_DIM], p.astype(BF16)).astype(BF16)
                for j, hh in enumerate(heads):
                    cat_ref[D_GMLP + hh * HEAD_DIM:D_GMLP + (hh + 1) * HEAD_DIM, _cols(b)] = o[:, j * BLK:(j + 1) * BLK]

    full = lambda shape: pl.BlockSpec(shape, lambda i: (0,) * len(shape))
    return _carry(
        body, name="mixer_fwd", grid=(t_tok // MIX_W,), comms=comms,
        in_specs=[pl.BlockSpec(memory_space=pltpu.SMEM)] + _h_specs() + _table_specs() + [
            full((N_HEADS, BLK, BLK)), full((N_HEADS, BLK)), full((D_GMLP, 1)), full((D_GMLP, 1)), BIAS_SPEC],
        out_specs=[pl.BlockSpec((D_GMLP + D_ATTN, MIX_W), lambda i: (0, i))],
        out_shape=[jax.ShapeDtypeStruct((D_GMLP + D_ATTN, t_tok), BF16)],
        scratch_shapes=[pltpu.VMEM((N_HEADS, BLK, BLK), BF16)],
        args=(sinks, h_t, h_t, h_t, h_t, h_t, cos_t, sin_t, cos_t, sin_t, w_spatial, b_spatial, vln_g, vln_b, band_bias))


def _proj_out(cat_t, x2, w_out_b, ln1_g, ln1_b, comms=()):
    t_tok, d = x2.shape
    tm = min(512, t_tok)

    def body(cat_ref, x_ref, w_ref, g_ref, b_ref, xhat_ref, rstd_ref, x1b_ref):
        x1, xhat, rstd = _ln_fwd(ALPHA * x_ref[...] + _dot(cat_ref[...], w_ref[...], TN), g_ref[...], b_ref[...])
        xhat_ref[...] = xhat
        rstd_ref[...] = rstd
        x1b_ref[...] = x1.astype(BF16)

    tok = lambda w: pl.BlockSpec((tm, w), lambda i: (i, 0))
    vec = pl.BlockSpec((1, d), lambda i: (0, 0))
    return _carry(
        body, name="proj_out", grid=(t_tok // tm,), comms=comms,
        in_specs=[pl.BlockSpec((cat_t.shape[0], tm), lambda i: (0, i)), tok(d), pl.BlockSpec(w_out_b.shape, lambda i: (0, 0)), vec, vec],
        out_specs=[tok(d), tok(1), tok(d)],
        out_shape=[jax.ShapeDtypeStruct((t_tok, d), F32), jax.ShapeDtypeStruct((t_tok, 1), F32), jax.ShapeDtypeStruct((t_tok, d), BF16)],
        args=(cat_t, x2, w_out_b, ln1_g, ln1_b))


def _ffn_fwd_bwd(xhat1, rstd1, x1b, target, w1_parts, w2_parts, ln1_g, ln1_b, ln2_g, ln2_b):
    t_tok, d = xhat1.shape
    n_part = len(w1_parts)
    n_chunk, _, fp = w1_parts[0].shape
    fc = n_part * fp
    f = n_chunk * fc
    tm = min(256, t_tok)

    def body(xhat1_ref, rstd1_ref, x1b_ref, tgt_ref, *refs):
        w1_hbm, w2_hbm = refs[:n_part], refs[n_part:2 * n_part]
        (g1_ref, b1_ref, g2_ref, b2_ref, act_ref, dpre_ref, dz2b_ref, dz1_ref, stats_ref,
         r_scr, w1_ref, w2_ref, w_sems) = refs[2 * n_part:]

        @pl.when(pl.program_id(0) == 0)
        def _():
            stats_ref[...] = jnp.zeros_like(stats_ref)
            loads = [pltpu.make_async_copy(w1_hbm[p], w1_ref.at[:, :, pl.ds(p * fp, fp)], w_sems.at[0, p]) for p in range(n_part)]
            loads += [pltpu.make_async_copy(w2_hbm[p], w2_ref.at[:, pl.ds(p * fp, fp), :], w_sems.at[1, p]) for p in range(n_part)]
            for cp in loads:
                cp.start()
            for cp in loads:
                cp.wait()

        g1, g2 = g1_ref[...], g2_ref[...]
        xhat1, x1b = xhat1_ref[...], x1b_ref[...]
        ff = jnp.zeros((tm, d), F32)
        for j in range(n_chunk):
            r = jnp.maximum(_dot(x1b, w1_ref[j]), 0.0)
            r_scr[:, j * fc:(j + 1) * fc] = r
            act = (r * r).astype(BF16)
            act_ref[:, j * fc:(j + 1) * fc] = act
            ff = ff + _dot(act, w2_ref[j])
        y, xhat2, rstd2 = _ln_fwd(ALPHA * (xhat1 * g1 + b1_ref[...]) + ff, g2, b2_ref[...])
        diff = y - tgt_ref[...]
        loss = 0.5 * jnp.sum(jnp.sum(diff * diff, axis=-1, keepdims=True) / d, axis=0, keepdims=True)
        dy = diff / d
        dz2 = _ln_bwd(dy, xhat2, rstd2, g2)
        dz2b = dz2.astype(BF16)
        dz2b_ref[...] = dz2b
        dx1 = ALPHA * dz2
        for j in range(n_chunk):
            dpre = (_dot(dz2b, w2_ref[j], NT) * (2.0 * r_scr[:, j * fc:(j + 1) * fc])).astype(BF16)
            dpre_ref[:, j * fc:(j + 1) * fc] = dpre
            dx1 = dx1 + _dot(dpre, w1_ref[j], NT)
        dz1_ref[...] = _ln_bwd(dx1, xhat1, rstd1_ref[...], g1)
        stats_ref[0:1, :] += jnp.sum(dx1 * xhat1, axis=0, keepdims=True)
        stats_ref[1:2, :] += jnp.sum(dx1, axis=0, keepdims=True)
        stats_ref[2:3, :] += jnp.sum(dy * xhat2, axis=0, keepdims=True)
        stats_ref[3:4, :] += jnp.sum(dy, axis=0, keepdims=True)
        stats_ref[4:5, :] += jnp.broadcast_to(loss, (1, d))

    tok = lambda w: pl.BlockSpec((tm, w), lambda i: (i, 0))
    vec = pl.BlockSpec((1, d), lambda i: (0, 0))
    return _carry(
        body, name="ffn_fwd_bwd", grid=(t_tok // tm,),
        in_specs=[tok(d), tok(1), tok(d), tok(d)] + [ANY] * (2 * n_part) + [vec, vec, vec, vec],
        out_specs=[tok(f), tok(f), tok(d), tok(d), pl.BlockSpec((8, d), lambda i: (0, 0))],
        out_shape=[jax.ShapeDtypeStruct((t_tok, f), BF16), jax.ShapeDtypeStruct((t_tok, f), BF16),
                   jax.ShapeDtypeStruct((t_tok, d), BF16), jax.ShapeDtypeStruct((t_tok, d), F32), jax.ShapeDtypeStruct((8, d), F32)],
        scratch_shapes=[pltpu.VMEM((tm, f), F32), pltpu.VMEM((n_chunk, d, fc), BF16), pltpu.VMEM((n_chunk, fc, d), BF16),
                        pltpu.SemaphoreType.DMA((2, n_part))],
        args=(xhat1, rstd1, x1b, target, *w1_parts, *w2_parts, ln1_g, ln1_b, ln2_g, ln2_b))[0]


def _ffn_wgrad(name, lhs, rhs, chunk_lhs, core_chip, comms=()):
    t_tok = lhs.shape[0]
    half = N_DEV // 2
    fc = (lhs if chunk_lhs else rhs).shape[1] // N_DEV
    chunk = (fc, rhs.shape[1]) if chunk_lhs else (lhs.shape[1], fc)

    def shard(s, cc):
        return 2 * (s % half) + jnp.where(s < half, 1 - cc[0], cc[0])

    def body(cc_ref, lhs_ref, rhs_ref, wire_ref, own_ref, recv_ref, send_buf, got, send_sems, recv_sems, got_sem):
        s = pl.program_id(0)
        x, y, c = _place()
        g = _dot(lhs_ref[...], rhs_ref[...], TN)

        def send(q):
            return pltpu.make_async_remote_copy(
                src_ref=send_buf.at[q % 2], dst_ref=recv_ref.at[q], send_sem=send_sems.at[q], recv_sem=recv_sems.at[q],
                device_id=(x, y, 1 - c), device_id_type=MESH)

        for q in range(half):
            @pl.when(s == q)
            def _(q=q):
                if q >= 2:
                    send(q - 2).wait_send()
                send_buf[q % 2] = g
                send(q).start()

            @pl.when(s == half + q)
            def _(q=q):
                send(q).wait_recv()
                load = pltpu.make_async_copy(recv_ref.at[q], got, got_sem.at[0])
                load.start()
                load.wait()
                total = g + got[...]
                wire_ref[...] = total.astype(BF16)

                @pl.when(cc_ref[1] == q)
                def _():
                    own_ref[...] = total

        @pl.when(s == N_DEV - 1)
        def _():
            for q in range(half - 2, half):
                send(q).wait_send()

    resident = lambda a: pl.BlockSpec(a.shape, lambda s, cc: (0, 0), pipeline_mode=pl.Buffered(1))
    chunked = pl.BlockSpec((t_tok, fc), lambda s, cc: (0, shard(s, cc)))
    (wire, own, _), per_comm = _carry(
        body, name=name, grid=(N_DEV,), comms=comms, prefetch=(core_chip,),
        in_specs=[chunked, resident(rhs)] if chunk_lhs else [resident(lhs), chunked],
        out_specs=[pl.BlockSpec((None,) + chunk, lambda s, cc: (jnp.maximum(s - half, 0), 0, 0)),
                   pl.BlockSpec(chunk, lambda s, cc: (0, 0)), ANY],
        out_shape=[jax.ShapeDtypeStruct((half,) + chunk, BF16), jax.ShapeDtypeStruct(chunk, F32),
                   jax.ShapeDtypeStruct((half,) + chunk, F32)],
        scratch_shapes=[pltpu.VMEM((2,) + chunk, F32), pltpu.VMEM(chunk, F32), pltpu.SemaphoreType.DMA((half,)),
                        pltpu.SemaphoreType.DMA((half,)), pltpu.SemaphoreType.DMA((1,))],
        args=(lhs, rhs))
    return wire, own, per_comm


def _proj_out_bwd(dz1, cat_t, w_out_b, comms=()):
    t_tok, d = dz1.shape
    d_mix = cat_t.shape[0]
    tm = min(512, t_tok)

    def body(dz1_ref, cat_ref, w_ref, dcat_ref, gw_ref):
        @pl.when(pl.program_id(0) == 0)
        def _():
            gw_ref[...] = jnp.zeros_like(gw_ref)

        dzb = dz1_ref[...].astype(BF16)
        dcat_ref[...] = _dot(w_ref[...], dzb, NT)
        gw_ref[...] += _dot(cat_ref[...], dzb)

    return _carry(
        body, name="proj_out_bwd", grid=(t_tok // tm,), comms=comms,
        in_specs=[pl.BlockSpec((tm, d), lambda i: (i, 0)), pl.BlockSpec((d_mix, tm), lambda i: (0, i)),
                  pl.BlockSpec((d_mix, d), lambda i: (0, 0))],
        out_specs=[pl.BlockSpec((d_mix, tm), lambda i: (0, i)), pl.BlockSpec((d_mix, d), lambda i: (0, 0))],
        out_shape=[jax.ShapeDtypeStruct((d_mix, t_tok), F32), jax.ShapeDtypeStruct((d_mix, d), F32)],
        args=(dz1, cat_t, w_out_b))


def _mixer_bwd(dcat_t, h_t, cos_t, sin_t, w_spatial, b_spatial, vln_g, vln_b, sinks, band_bias, comms=()):
    t_tok = h_t.shape[1]
    nb, n_step = t_tok // BLK, t_tok // MIX_W
    group = N_HEADS // N_KV_HEADS

    def body(sinks_ref, dcat_ref, u_ref, vg_ref, q_ref, kvc_ref, kvp_ref, cos_ref, sin_ref, cosp_ref, sinp_ref,
             wsp_ref, bsp_ref, g_ref, b_ref, bias_ref, dh_ref, dkvc_ref, dkvp_ref, gwsb_ref, gbsp_ref, gvln_ref, gsink_ref,
             dg_acc, db_acc, wm_scr, gws_ref):
        i = pl.program_id(0)

        @pl.when(i == 0)
        def _():
            gws_ref[...] = jnp.zeros_like(gws_ref)
            gbsp_ref[...] = jnp.zeros_like(gbsp_ref)
            gsink_ref[...] = jnp.zeros_like(gsink_ref)
            dg_acc[...] = jnp.zeros_like(dg_acc)
            db_acc[...] = jnp.zeros_like(db_acc)

        _mask_w_once(wsp_ref, wm_scr)

        g = g_ref[...]
        ua, ua_grad = _gelu_and_grad(u_ref[...])
        vv, vv_grad = _gelu_and_grad(vg_ref[...])
        vp, vhat, rstd = _ln_fwd_t(vv, g, b_ref[...])
        vpb = vp.astype(BF16)
        da = dcat_ref[0:D_GMLP, :]
        dmixed = da * ua
        dvp_blocks = []
        for b in range(MIX_BLOCKS):
            dvp_parts = []
            for hh in range(N_HEADS):
                rows = slice(hh * HEAD_DIM, (hh + 1) * HEAD_DIM)
                vpb_h = vpb[rows, _cols(b)]
                mixed = _dot(vpb_h, wm_scr[hh], NT) + bsp_ref[hh:hh + 1, :]
                dh_ref[COL_U + hh * HEAD_DIM:COL_U + (hh + 1) * HEAD_DIM, _cols(b)] = (
                    da[rows, _cols(b)] * mixed * ua_grad[rows, _cols(b)]).astype(BF16)
                dm = dmixed[rows, _cols(b)]
                dmb = dm.astype(BF16)
                gbsp_ref[hh:hh + 1, :] += jnp.sum(dm, axis=0, keepdims=True)
                gws_ref[hh] += _dot(dmb, vpb_h, TN)
                dvp_parts.append(_dot(dmb, wm_scr[hh]))
            dvp_blocks.append(jnp.concatenate(dvp_parts, axis=0))
        dvp = jnp.concatenate(dvp_blocks, axis=1)
        dgv, dbv = dvp * vhat, dvp
        for b in range(MIX_BLOCKS):
            dg_acc[...] += dgv[:, _cols(b)]
            db_acc[...] += dbv[:, _cols(b)]
        dh_ref[COL_V:COL_V + D_GMLP, :] = (_ln_bwd_t(dvp, vhat, rstd, g) * vv_grad).astype(BF16)

        kvc, cos, sin = kvc_ref[...], cos_ref[...], sin_ref[...]
        qr = (_rope_t(q_ref[...], cos, sin) * SCORE_SCALE).astype(BF16)
        sinks4 = [_group_lanes([jnp.full((1, BLK), sinks_ref[hh], F32) for hh in range(kv * group, (kv + 1) * group)])
                  for kv in range(N_KV_HEADS)]
        dq_blocks, dkv_cur, dkv_prev = [], [], []
        for b in range(MIX_BLOCKS):
            kv_cur, kv_prev, cosc, sinc, cosp, sinp, bias1 = _block_inputs(b, i, kvc, kvp_ref, cos, sin, cosp_ref, sinp_ref, bias_ref)
            k_t, k_n, v_t = _keys_values(kv_cur, kv_prev, cosc, sinc, cosp, sinp)
            v_n = jnp.concatenate([kv_prev[D_KV:].T, kv_cur[D_KV:].T], axis=0).astype(BF16)
            bias = _group_lanes([bias1] * group)
            dk, dv, dq_parts = [], [], []
            for kv in range(N_KV_HEADS):
                heads = range(kv * group, (kv + 1) * group)
                kv_rows = slice(kv * HEAD_DIM, (kv + 1) * HEAD_DIM)
                qs = _group_lanes([qr[hh * HEAD_DIM:(hh + 1) * HEAD_DIM, _cols(b)] for hh in heads])
                dos = _group_lanes([dcat_ref[D_GMLP + hh * HEAD_DIM:D_GMLP + (hh + 1) * HEAD_DIM, _cols(b)]
                                    for hh in heads]).astype(BF16)
                p, p_sink = _softmax_sink_t(_dot(k_n, _pad_head(qs, kv)) + bias, sinks4[kv])
                dp = _dot(v_n, _pad_head(dos, kv))
                delta = jnp.sum(p * dp, axis=0, keepdims=True)
                ds = (p * (dp - delta)).astype(BF16)
                dsink = p_sink * delta
                dq = _dot(k_t[kv_rows], ds) * SCORE_SCALE
                for j, hh in enumerate(heads):
                    gsink_ref[hh:hh + 1, :] -= dsink[:, j * BLK:(j + 1) * BLK]
                    dq_parts.append(dq[:, j * BLK:(j + 1) * BLK])
                dk.append(_dot(qs, ds, NT))
                dv.append(_dot(dos, p.astype(BF16), NT))
            dq_blocks.append(jnp.concatenate(dq_parts, axis=0))
            dk_all, dv_all = jnp.concatenate(dk, axis=0), jnp.concatenate(dv, axis=0)
            dkv_cur.append(jnp.concatenate([_rope_t(dk_all[:, BLK:], cosc, sinc, bwd=True), dv_all[:, BLK:]], axis=0))
            dkv_prev.append(jnp.concatenate([_rope_t(dk_all[:, :BLK], cosp, sinp, bwd=True), dv_all[:, :BLK]], axis=0))
        dh_ref[COL_Q:COL_Q + D_ATTN, :] = _rope_t(jnp.concatenate(dq_blocks, axis=1), cos, sin, bwd=True).astype(BF16)
        for b in range(MIX_BLOCKS):
            dkvc_ref[:, _cols(b)] = dkv_cur[b] + dkv_prev[b + 1] if b + 1 < MIX_BLOCKS else dkv_cur[b]
        dkvp_ref[...] = dkv_prev[0]

        @pl.when(i == n_step - 1)
        def _():
            causal = _causal()
            for hh in range(N_HEADS):
                gwsb_ref[hh] = jnp.where(causal, gws_ref[hh], 0.0).astype(BF16)
            gvln_ref[...] = jnp.zeros_like(gvln_ref)
            gvln_ref[0:1, :] = jnp.sum(dg_acc[...].T, axis=0, keepdims=True)
            gvln_ref[1:2, :] = jnp.sum(db_acc[...].T, axis=0, keepdims=True)

    full = lambda shape: pl.BlockSpec(shape, lambda i: (0,) * len(shape))
    return _carry(
        body, name="mixer_bwd", grid=(n_step,), comms=comms,
        in_specs=[pl.BlockSpec(memory_space=pltpu.SMEM), pl.BlockSpec((D_GMLP + D_ATTN, MIX_W), lambda i: (0, i))]
        + _h_specs() + _table_specs()
        + [full((N_HEADS, BLK, BLK)), full((N_HEADS, BLK)), full((D_GMLP, 1)), full((D_GMLP, 1)), BIAS_SPEC],
        out_specs=[pl.BlockSpec((COL_K, MIX_W), lambda i: (0, i)), pl.BlockSpec((2 * D_KV, MIX_W), lambda i: (0, i)),
                   pl.BlockSpec((2 * D_KV, BLK), lambda i: (0, (i + n_step - 1) % n_step)),
                   full((N_HEADS, BLK, BLK)), full((N_HEADS, BLK)), full((8, D_GMLP)), full((N_HEADS, LANES))],
        out_shape=[jax.ShapeDtypeStruct((COL_K, t_tok), BF16), jax.ShapeDtypeStruct((2 * D_KV, t_tok), F32),
                   jax.ShapeDtypeStruct((2 * D_KV, n_step * BLK), F32),
                   jax.ShapeDtypeStruct((N_HEADS, BLK, BLK), BF16), jax.ShapeDtypeStruct((N_HEADS, BLK), F32),
                   jax.ShapeDtypeStruct((8, D_GMLP), F32), jax.ShapeDtypeStruct((N_HEADS, LANES), F32)],
        scratch_shapes=[pltpu.VMEM((D_GMLP, BLK), F32), pltpu.VMEM((D_GMLP, BLK), F32), pltpu.VMEM((N_HEADS, BLK, BLK), BF16),
                        pltpu.VMEM((N_HEADS, BLK, BLK), F32)],
        args=(sinks, dcat_t, h_t, h_t, h_t, h_t, h_t, cos_t, sin_t, cos_t, sin_t, w_spatial, b_spatial, vln_g, vln_b, band_bias))


def _proj_in_wgrad(dh_b, dkvc_t, dkvp_t, xb, comms=()):
    t_tok, d = xb.shape
    d_main, d_kv = dh_b.shape[0], dkvc_t.shape[0]
    tm = min(1024, t_tok)

    def body(dh_ref, dkvc_ref, dkvp_ref, xb_ref, dkvb_ref, gw_ref):
        @pl.when(pl.program_id(0) == 0)
        def _():
            gw_ref[...] = jnp.zeros_like(gw_ref)

        for s in range(tm // MIX_W):
            last = slice((s + 1) * MIX_W - BLK, (s + 1) * MIX_W)
            dkvb_ref[:, s * MIX_W:(s + 1) * MIX_W - BLK] = dkvc_ref[:, s * MIX_W:(s + 1) * MIX_W - BLK].astype(BF16)
            dkvb_ref[:, last] = (dkvc_ref[:, last] + dkvp_ref[:, _cols(s)]).astype(BF16)
        gw_ref[0:d_main, :] += _dot(dh_ref[...], xb_ref[...])
        gw_ref[d_main:, :] += _dot(dkvb_ref[...], xb_ref[...])

    tok = lambda rows: pl.BlockSpec((rows, tm), lambda i: (0, i))
    return _carry(
        body, name="proj_in_wgrad", grid=(t_tok // tm,), comms=comms,
        in_specs=[tok(d_main), tok(d_kv), pl.BlockSpec((d_kv, tm // MIX_BLOCKS), lambda i: (0, i)),
                  pl.BlockSpec((tm, d), lambda i: (i, 0))],
        out_specs=[tok(d_kv), pl.BlockSpec((d_main + d_kv, d), lambda i: (0, 0))],
        out_shape=[jax.ShapeDtypeStruct((d_kv, t_tok), BF16), jax.ShapeDtypeStruct((d_main + d_kv, d), F32)],
        args=(dh_b, dkvc_t, dkvp_t, xb))


def _proj_in_dgrad(dh_b, dkv_b, dz1, w_in_t, comms=()):
    t_tok, d = dz1.shape
    d_main, d_kv = dh_b.shape[0], dkv_b.shape[0]
    tm = min(512, t_tok)

    def body(dh_ref, dkv_ref, dz1_ref, w_ref, dx_ref):
        dx_ref[...] = (ALPHA * dz1_ref[...] + _dot(dh_ref[...], w_ref[0:d_main, :], TN)
                       + _dot(dkv_ref[...], w_ref[d_main:, :], TN))

    return _carry(
        body, name="proj_in_dgrad", grid=(t_tok // tm,), comms=comms,
        in_specs=[pl.BlockSpec((d_main, tm), lambda i: (0, i)), pl.BlockSpec((d_kv, tm), lambda i: (0, i)),
                  pl.BlockSpec((tm, d), lambda i: (i, 0)), pl.BlockSpec((d_main + d_kv, d), lambda i: (0, 0))],
        out_specs=[pl.BlockSpec((tm, d), lambda i: (i, 0))],
        out_shape=[jax.ShapeDtypeStruct((t_tok, d), F32)],
        args=(dh_b, dkv_b, dz1, w_in_t))


def _adamw(w, g, m, v):
    m = ADAM_B1 * m + (1.0 - ADAM_B1) * g
    v = ADAM_B2 * v + (1.0 - ADAM_B2) * (g * g)
    m_hat = m / (1.0 - ADAM_B1 ** ADAM_STEP)
    v_hat = v / (1.0 - ADAM_B2 ** ADAM_STEP)
    delta = -ADAM_LR * (m_hat / (jnp.sqrt(v_hat) + ADAM_EPS) + ADAM_WD * w)
    return delta, m, v


def _row_tiled(name, own, recv, extra, n_out, finish, comms=()):
    r, c = own.shape
    recv = [] if recv is None else list(recv)
    k = max(len(recv), 1)
    n = max(k, -(-r // 512))
    tr, per = r // n, n // k
    blk = pl.BlockSpec((tr, c), lambda i: (i, 0))

    def body(own_ref, *refs):
        recv_refs, rest = refs[:len(recv)], refs[len(recv):]
        ins, outs = rest[:len(extra)], rest[len(extra):]

        def tile(recv_ref):
            g = own_ref[...]
            if recv_ref is not None:
                g = ((g + recv_ref[0].astype(F32)) + recv_ref[1].astype(F32)) + recv_ref[2].astype(F32)
            for o_ref, val in zip(outs, finish(g, *[a[...] for a in ins])):
                o_ref[...] = val

        if len(recv) <= 1:
            tile(recv_refs[0] if recv else None)
        else:
            for p in range(k):
                pl.when(pl.program_id(0) // per == p)(functools.partial(tile, recv_refs[p]))

    recv_specs = [pl.BlockSpec((3, tr, c), lambda i, p=p: (0, jnp.clip(i - p * per, 0, per - 1), 0)) for p in range(len(recv))]
    return _carry(
        body, name=name, grid=(n,), comms=comms,
        in_specs=[blk] + recv_specs + [blk] * len(extra),
        out_specs=[blk] * n_out, out_shape=[jax.ShapeDtypeStruct((r, c), F32)] * n_out,
        args=(own, *recv, *extra))


def _adamw_shard(name, own, recv, w, m, v, comms=()):
    def finish(g, w_t, m_t, v_t):
        return (g,) + _adamw(w_t, g, m_t, v_t)

    return _row_tiled(name, own, recv, (w, m, v), 4, finish, comms)


VEC_VLN, VEC_LN1G, VEC_LN1B, VEC_LN2G, VEC_LN2B, VEC_SINK, VEC_LOSS, VEC_BSP, VEC_ROWS = 0, 1, 2, 3, 4, 5, 6, 8, 16


def _adamw_small(parts_w, parts_vec, params):
    n = parts_w.shape[0]
    flat = [a for p in params for a in p]
    shapes = [p[0].shape for p in params]

    def grads(gw, gv):
        return [gw, gv[VEC_VLN:VEC_VLN + 1, 0:D_GMLP], gv[VEC_VLN:VEC_VLN + 1, D_GMLP:2 * D_GMLP],
                gv[VEC_BSP:VEC_BSP + N_HEADS, 0:BLK], gv[VEC_LN1G:VEC_LN1G + 1], gv[VEC_LN1B:VEC_LN1B + 1],
                gv[VEC_LN2G:VEC_LN2G + 1], gv[VEC_LN2B:VEC_LN2B + 1], gv[VEC_SINK:VEC_SINK + 1, 0:N_HEADS]]

    def body(pw_ref, pv_ref, *refs):
        ins, outs = refs[:len(flat)], refs[len(flat):]
        gw, gv = pw_ref[0].astype(F32), pv_ref[0]
        for k in range(1, n):
            gw, gv = gw + pw_ref[k].astype(F32), gv + pv_ref[k]
        for i, g in enumerate(grads(gw, gv)):
            w_ref, m_ref, v_ref = ins[3 * i:3 * i + 3]
            delta, m_new, v_new = _adamw(w_ref[...], g, m_ref[...], v_ref[...])
            for o_ref, val in zip(outs[4 * i:4 * i + 4], (g, delta, m_new, v_new)):
                o_ref[...] = val
        outs[-1][...] = gv[VEC_LOSS:VEC_LOSS + 1, 0:LANES]

    whole = lambda shape: pl.BlockSpec(shape, lambda i: (0,) * len(shape))
    res = _carry(
        body, name="adamw_small", grid=(1,),
        in_specs=[whole(parts_w.shape), whole(parts_vec.shape)] + [whole(a.shape) for a in flat],
        out_specs=[whole(s) for s in shapes for _ in range(4)] + [whole((1, LANES))],
        out_shape=[jax.ShapeDtypeStruct(s, F32) for s in shapes for _ in range(4)] + [jax.ShapeDtypeStruct((1, LANES), F32)],
        args=(parts_w, parts_vec, *flat))[0]
    return [res[4 * i:4 * i + 4] for i in range(len(params))], res[-1]


def _pair_sum(name, parts, recv, core_chip, comms=()):
    _, r, c = parts.shape
    tr = r if r <= 512 else 512

    def body(cc_ref, a_ref, b_ref, wire_ref, own_ref):
        s = a_ref[...] + b_ref[...]
        wire_ref[...] = s.astype(BF16)

        @pl.when(pl.program_id(1) == cc_ref[1])
        def _():
            own_ref[...] = s

    return _carry(
        body, name=name, grid=(r // tr, 4), prefetch=(core_chip,), comms=comms,
        in_specs=[pl.BlockSpec((None, tr, c), lambda i, q, cc: (2 * q + cc[0], i, 0)),
                  pl.BlockSpec((None, tr, c), lambda i, q, cc: (q, i, 0))],
        out_specs=[pl.BlockSpec((None, tr, c), lambda i, q, cc: (q, i, 0)), pl.BlockSpec((tr, c), lambda i, q, cc: (i, 0))],
        out_shape=[jax.ShapeDtypeStruct((4, r, c), BF16), jax.ShapeDtypeStruct((r, c), F32)],
        args=(parts, recv))


def kernel(x, positions, w_in, v_ln_g, v_ln_b, w_spatial, b_spatial, sinks, w_out, ln1_g, ln1_b, w_ff1, w_ff2, ln2_g, ln2_b, loss_target, m_w_in, m_v_ln_g, m_v_ln_b, m_w_spatial, m_b_spatial, m_sinks, m_w_out, m_ln1_g, m_ln1_b, m_w_ff1, m_w_ff2, m_ln2_g, m_ln2_b, v_w_in, v_v_ln_g, v_v_ln_b, v_w_spatial, v_b_spatial, v_sinks, v_w_out, v_ln1_g, v_ln1_b, v_w_ff1, v_w_ff2, v_ln2_g, v_ln2_b):
    _, t_tok, d = x.shape
    xi, yi, ci = _place()
    core_chip = jnp.stack([ci, 2 * xi + yi]).astype(jnp.int32)
    x2 = x.reshape(t_tok, d)
    target = loss_target.reshape(t_tok, d)
    inv_freq = ROPE_THETA ** (-jnp.arange(0, HEAD_DIM, 2, dtype=F32) / HEAD_DIM)
    wsp, bsp, sink_vec = w_spatial[0], b_spatial[0], sinks[0]
    vg_col, vb_col = v_ln_g.reshape(D_GMLP, 1), v_ln_b.reshape(D_GMLP, 1)
    big = {"in": w_in[0], "out": w_out[0], "ff1": w_ff1[0], "ff2": w_ff2[0]}
    half1, half2 = big["ff1"].shape[1] // 2, big["ff2"].shape[0] // 2
    w1_mine = [big["ff1"][:, :half1].astype(BF16), big["ff1"][:, half1:].astype(BF16)]
    w2_mine = [big["ff2"][:half2].astype(BF16), big["ff2"][half2:].astype(BF16)]

    (cos_t, sin_t), ((g_in,),) = _rope_tables(
        positions, jnp.tile(inv_freq, 2).reshape(HEAD_DIM, 1), comms=[_gather_comm([big["in"].T.astype(BF16)])])
    w_in_t = g_in.reshape(D_IN, d)
    (h_t, xb), ((g_out, w1_a),) = _proj_in(x2, w_in_t, comms=[_gather_comm([big["out"].astype(BF16), w1_mine[0]])])
    w_out_b = g_out.reshape(-1, d)
    band_bias = _band_bias()
    (cat_t,), ((w1_b, w2_a),) = _mixer_fwd(h_t, cos_t, sin_t, wsp, bsp, vg_col, vb_col, sink_vec, band_bias,
                                           comms=[_gather_comm([w1_mine[1], w2_mine[0]])])
    (xhat1, rstd1, x1b), ((w2_b,),) = _proj_out(cat_t, x2, w_out_b, ln1_g, ln1_b, comms=[_gather_comm([w2_mine[1]])])
    act_b, dpre_b, dz2b, dz1, stats = _ffn_fwd_bwd(xhat1, rstd1, x1b, target, [w1_a, w1_b], [w2_a, w2_b], ln1_g, ln1_b, ln2_g, ln2_b)

    (dcat_t, gw_out), _ = _proj_out_bwd(dz1, cat_t, w_out_b)
    p_out = gw_out.reshape(N_DEV, -1, d)
    wire_ff1, own_ff1, ((s_out,),) = _ffn_wgrad("ffn_wgrad1", x1b, dpre_b, False, core_chip, comms=[_sibling_comm([p_out])])
    (wire_out, own_out), _ = _pair_sum("pair_sum_out", p_out, s_out, core_chip)
    wire_ff2, own_ff2, ((r_ff1,),) = _ffn_wgrad("ffn_wgrad2", act_b, dz2b, True, core_chip, comms=[_chips_comm([wire_ff1])])
    (dh_b, dkvc_t, dkvp_t, g_wsp, g_bsp, g_vln, g_sink), ((r_ff2, r_out),) = _mixer_bwd(
        dcat_t, h_t, cos_t, sin_t, wsp, bsp, vg_col, vb_col, sink_vec, band_bias,
        comms=[_chips_comm([wire_ff2, wire_out])])
    sink_row = jnp.pad(g_sink.sum(axis=1).reshape(1, N_HEADS), ((0, 0), (0, d - N_HEADS)))
    small_vec = jnp.concatenate([g_vln[0:2].reshape(1, d), stats[0:4], sink_row, stats[4:5], jnp.zeros((1, d), F32),
                                 jnp.pad(g_bsp, ((0, 0), (0, d - BLK)))], axis=0)
    (dkv_b, gw_in_t), ((parts_w, parts_vec),) = _proj_in_wgrad(
        dh_b, dkvc_t, dkvp_t, xb, comms=[_gather_comm([g_wsp.reshape(-1, BLK), small_vec])])
    p_in = gw_in_t.reshape(N_DEV, -1, d)

    out_out, ((s_in,),) = _adamw_shard("adamw_out", own_out, [r_out], big["out"], m_w_out[0], v_w_out[0], comms=[_sibling_comm([p_in])])
    (wire_in, own_in), _ = _pair_sum("pair_sum_in", p_in, s_in, core_chip)
    (grad_x,), ((r_in,),) = _proj_in_dgrad(dh_b, dkv_b, dz1, w_in_t, comms=[_chips_comm([wire_in])])
    ff1_out, _ = _adamw_shard("adamw_ff1", own_ff1, [r_ff1], big["ff1"], m_w_ff1[0], v_w_ff1[0])
    ff2_out, _ = _adamw_shard("adamw_ff2", own_ff2, [r_ff2], big["ff2"], m_w_ff2[0], v_w_ff2[0])
    in_out_t, _ = _adamw_shard("adamw_in", own_in, [r_in], big["in"].T, m_w_in[0].T, v_w_in[0].T)
    in_out = [o.T for o in in_out_t]
    small = [(w_spatial, m_w_spatial, v_w_spatial), (v_ln_g, m_v_ln_g, v_v_ln_g), (v_ln_b, m_v_ln_b, v_v_ln_b),
             (b_spatial, m_b_spatial, v_b_spatial), (ln1_g, m_ln1_g, v_ln1_g), (ln1_b, m_ln1_b, v_ln1_b),
             (ln2_g, m_ln2_g, v_ln2_g), (ln2_b, m_ln2_b, v_ln2_b), (sinks, m_sinks, v_sinks)]
    views = [(-1, BLK), None, None, (N_HEADS, BLK)] + [None] * 5
    small_res, loss_row = _adamw_small(parts_w, parts_vec, [
        tuple(a if vw is None else a.reshape(vw) for a in p) for p, vw in zip(small, views)])
    small_out = [[o.reshape(p[0].shape) for o in res] for res, p in zip(small_res, small)]
    loss = loss_row[0, 0]

    big_out = {0: in_out, 6: out_out, 9: ff1_out, 10: ff2_out}
    small_slot = {3: 0, 1: 1, 2: 2, 4: 3, 7: 4, 8: 5, 11: 6, 12: 7, 5: 8}
    outs = [loss, grad_x.reshape(x.shape)]
    for kind in range(4):
        for wi in range(13):
            outs.append(big_out[wi][kind][None] if wi in big_out else small_out[small_slot[wi]][kind])
    return tuple(outs)
```

```python
import functools
import math

import jax
import jax.numpy as jnp
from jax import lax
from jax.experimental import pallas as pl
from jax.experimental.pallas import tpu as pltpu

F32 = jnp.float32
BF16 = jnp.bfloat16
MESH = pl.DeviceIdType.MESH

HEAD_DIM = 64
N_HEADS = 8
N_KV_HEADS = 2
BLK = 128
D_GMLP = N_HEADS * HEAD_DIM
D_ATTN = N_HEADS * HEAD_DIM
D_KV = N_KV_HEADS * HEAD_DIM
D_IN = 2 * D_GMLP + D_ATTN + 2 * D_KV
COL_U, COL_V, COL_Q, COL_K = 0, D_GMLP, 2 * D_GMLP, 2 * D_GMLP + D_ATTN
ROPE_THETA = 10000.0
LN_EPS = 1e-5
ALPHA = 2.0 ** 0.25
NEG_INF = -1e30
SCORE_SCALE = 1.0 / math.sqrt(HEAD_DIM)
ADAM_LR, ADAM_B1, ADAM_B2, ADAM_EPS, ADAM_WD, ADAM_STEP = 0.001, 0.9, 0.999, 1e-08, 0.01, 10
N_DEV = 8
LANES = 128
VMEM_LIMIT = 56 * 1024 * 1024

NT = (((1,), (1,)), ((), ()))
TN = (((0,), (0,)), ((), ()))


def _params(*sem):
    return pltpu.CompilerParams(dimension_semantics=sem, vmem_limit_bytes=VMEM_LIMIT)


def _dot(a, b, dims=None):
    if dims is None:
        return jnp.dot(a, b, preferred_element_type=F32)
    return lax.dot_general(a, b, dims, preferred_element_type=F32)


def _mean(a):
    return jnp.mean(a, axis=-1, keepdims=True)


def _ln_fwd(z, g, b):
    zc = z - _mean(z)
    rstd = lax.rsqrt(_mean(zc * zc) + LN_EPS)
    xhat = zc * rstd
    return xhat * g + b, xhat, rstd


def _ln_bwd(dy, xhat, rstd, g):
    dxhat = dy * g
    return rstd * (dxhat - _mean(dxhat) - xhat * _mean(dxhat * xhat))


_GELU_C = math.sqrt(2.0 / math.pi)


def _gelu(x):
    t = jnp.tanh(_GELU_C * (x + 0.044715 * (x * x * x)))
    return 0.5 * x * (1.0 + t)


def _gelu_and_grad(x):
    x2 = x * x
    t = jnp.tanh(_GELU_C * (x + 0.044715 * (x2 * x)))
    hx, ht = 0.5 * x, 0.5 * (1.0 + t)
    return x * ht, ht + hx * (1.0 - t * t) * (_GELU_C * (1.0 + 3.0 * 0.044715 * x2))


def _mean0(a):
    return jnp.mean(a, axis=0, keepdims=True)


def _ln_fwd_t(z, g, b):
    zc = z - _mean0(z)
    rstd = lax.rsqrt(_mean0(zc * zc) + LN_EPS)
    xhat = zc * rstd
    return xhat * g + b, xhat, rstd


def _ln_bwd_t(dy, xhat, rstd, g):
    dxhat = dy * g
    return rstd * (dxhat - _mean0(dxhat) - xhat * _mean0(dxhat * xhat))


def _rope_t(t, cos, sin_signed, bwd=False):
    half = HEAD_DIM // 2
    outs = []
    for r in range(0, t.shape[0], HEAD_DIM):
        th = t[r:r + HEAD_DIM]
        sw = jnp.concatenate([th[half:], th[:half]], axis=0) * sin_signed
        outs.append(th * cos - sw if bwd else th * cos + sw)
    return jnp.concatenate(outs, axis=0)


ANY = pl.BlockSpec(memory_space=pl.ANY)


def _place():
    return lax.axis_index("x"), lax.axis_index("y"), lax.axis_index("c")


class _Comm:
    def __init__(self, ins, outs, sems, start, finish):
        self.ins, self.outs, self.sems, self.start, self.finish = ins, outs, sems, start, finish


def _gather_comm(arrs):
    n = len(arrs)

    def parts(ins, outs, sems):
        send_sems, recv_sems, local_sems = sems
        x, y, c = _place()
        me, sibling = (x, y, c), (x, y, 1 - c)
        chips = [(1 - x, y), (x, 1 - y), (1 - x, 1 - y)]

        def copy(a, k, block, to, src=None):
            px, py, pc = block
            dst = outs[a].at[4 * px + 2 * py + pc]
            return pltpu.make_async_remote_copy(
                src_ref=dst if src is None else src, dst_ref=dst,
                send_sem=send_sems.at[a, k], recv_sem=recv_sems.at[a, k], device_id=to, device_id_type=MESH)

        mine = [pltpu.make_async_copy(ins[a], outs[a].at[4 * x + 2 * y + c], local_sems.at[a]) for a in range(n)]
        first = []
        for a in range(n):
            first.append(copy(a, 0, me, sibling, src=ins[a]))
            first += [copy(a, 1 + j, me, (*chip, c), src=ins[a]) for j, chip in enumerate(chips)]
        return copy, mine, first, me, sibling, chips, c

    def start(ins, outs, sems):
        _, mine, first, *_ = parts(ins, outs, sems)
        for cp in mine + first:
            cp.start()

    def finish(ins, outs, sems):
        copy, mine, first, me, sibling, chips, c = parts(ins, outs, sems)
        passed = []
        for j, chip in enumerate(chips):
            for a in range(n):
                copy(a, 1 + j, (*chip, c), me).wait_recv()
                fwd = copy(a, 4 + j, (*chip, c), sibling)
                fwd.start()
                passed.append(fwd)
        for a in range(n):
            copy(a, 0, sibling, me).wait_recv()
        for j, chip in enumerate(chips):
            for a in range(n):
                copy(a, 4 + j, (*chip, 1 - c), me).wait_recv()
        for cp in first + passed:
            cp.wait_send()
        for cp in mine:
            cp.wait()

    return _Comm(list(arrs), [jax.ShapeDtypeStruct((N_DEV,) + a.shape, a.dtype) for a in arrs],
                 [pltpu.SemaphoreType.DMA((n, 7)), pltpu.SemaphoreType.DMA((n, 7)), pltpu.SemaphoreType.DMA((n,))],
                 start, finish)


def _sibling_comm(parts):
    n = len(parts)

    def copies(ins, outs, sems):
        x, y, c = _place()
        return [pltpu.make_async_remote_copy(
            src_ref=ins[a].at[2 * q + (1 - c)], dst_ref=outs[a].at[q],
            send_sem=sems[0].at[a, q], recv_sem=sems[1].at[a, q],
            device_id=(x, y, 1 - c), device_id_type=MESH) for a in range(n) for q in range(4)]

    return _Comm(list(parts), [jax.ShapeDtypeStruct((4,) + p.shape[1:], p.dtype) for p in parts],
                 [pltpu.SemaphoreType.DMA((n, 4)), pltpu.SemaphoreType.DMA((n, 4))],
                 lambda *r: [cp.start() for cp in copies(*r)], lambda *r: [cp.wait() for cp in copies(*r)])


def _chips_comm(chip_parts, rows=None):
    n = len(chip_parts)
    r0, nr = (0, None) if rows is None else rows

    def copies(ins, outs, sems):
        x, y, c = _place()
        chips = [(1 - x, y), (x, 1 - y), (1 - x, 1 - y)]
        src = lambda a, q: ins[a].at[q] if rows is None else ins[a].at[q, pl.ds(r0, nr)]
        return [pltpu.make_async_remote_copy(
            src_ref=src(a, 2 * px + py), dst_ref=outs[a].at[k],
            send_sem=sems[0].at[a, k], recv_sem=sems[1].at[a, k],
            device_id=(px, py, c), device_id_type=MESH) for a in range(n) for k, (px, py) in enumerate(chips)]

    shape = lambda p: (3,) + p.shape[1:] if rows is None else (3, nr) + p.shape[2:]
    return _Comm(list(chip_parts), [jax.ShapeDtypeStruct(shape(p), p.dtype) for p in chip_parts],
                 [pltpu.SemaphoreType.DMA((n, 3)), pltpu.SemaphoreType.DMA((n, 3))],
                 lambda *r: [cp.start() for cp in copies(*r)], lambda *r: [cp.wait() for cp in copies(*r)])


def _carry(body, *, name, grid, in_specs, out_specs, out_shape, args, comms=(), scratch_shapes=(), prefetch=()):
    n_pre, n_in, n_out, n_scr = len(prefetch), len(in_specs), len(out_specs), len(scratch_shapes)
    c_ins = [a for cm in comms for a in cm.ins]
    c_outs = [s for cm in comms for s in cm.outs]
    c_sems = [s for cm in comms for s in cm.sems]

    def wrapped(*refs):
        pre, refs = refs[:n_pre], refs[n_pre:]
        ins, refs = refs[:n_in], refs[n_in:]
        cins, refs = refs[:len(c_ins)], refs[len(c_ins):]
        outs, refs = refs[:n_out], refs[n_out:]
        couts, refs = refs[:len(c_outs)], refs[len(c_outs):]
        scr, sems = refs[:n_scr], refs[n_scr:]
        groups, i0, o0, s0 = [], 0, 0, 0
        for cm in comms:
            groups.append((cm, cins[i0:i0 + len(cm.ins)], couts[o0:o0 + len(cm.outs)], sems[s0:s0 + len(cm.sems)]))
            i0, o0, s0 = i0 + len(cm.ins), o0 + len(cm.outs), s0 + len(cm.sems)
        first = pl.program_id(0) == 0
        last = pl.program_id(0) == grid[0] - 1
        for ax in range(1, len(grid)):
            first = first & (pl.program_id(ax) == 0)
            last = last & (pl.program_id(ax) == grid[ax] - 1)
        if comms:
            @pl.when(first)
            def _():
                for cm, ci, co, cs in groups:
                    cm.start(ci, co, cs)
        body(*pre, *ins, *outs, *scr)
        if comms:
            @pl.when(last)
            def _():
                for cm, ci, co, cs in groups:
                    cm.finish(ci, co, cs)

    grid_spec = pltpu.PrefetchScalarGridSpec(
        num_scalar_prefetch=n_pre, grid=grid,
        in_specs=list(in_specs) + [ANY] * len(c_ins), out_specs=list(out_specs) + [ANY] * len(c_outs),
        scratch_shapes=list(scratch_shapes) + c_sems)
    res = pl.pallas_call(
        wrapped, name=name, grid_spec=grid_spec, out_shape=list(out_shape) + c_outs,
        compiler_params=_params(*(["arbitrary"] * len(grid))),
    )(*prefetch, *args, *c_ins)
    outs, rest, per_comm = res[:n_out], res[n_out:], []
    for cm in comms:
        per_comm.append(rest[:len(cm.outs)])
        rest = rest[len(cm.outs):]
    return outs, per_comm


def _rope_tables(pos_row, inv_freq_col, comms=()):
    t_tok = pos_row.shape[1]
    tm = min(512, t_tok)

    def body(pos_ref, invf_ref, cos_ref, sin_ref):
        ang = pos_ref[...].astype(F32) * invf_ref[...]
        row = lax.broadcasted_iota(jnp.int32, ang.shape, 0)
        cos_ref[...] = jnp.cos(ang)
        sin_ref[...] = jnp.sin(ang) * jnp.where(row < HEAD_DIM // 2, -1.0, 1.0)

    return _carry(
        body, name="rope_tables", grid=(t_tok // tm,), comms=comms,
        in_specs=[pl.BlockSpec((1, tm), lambda i: (0, i)), pl.BlockSpec((HEAD_DIM, 1), lambda i: (0, 0))],
        out_specs=[pl.BlockSpec((HEAD_DIM, tm), lambda i: (0, i))] * 2,
        out_shape=[jax.ShapeDtypeStruct((HEAD_DIM, t_tok), F32)] * 2,
        args=(pos_row, inv_freq_col))


def _proj_in(x2, w_in_t, comms=()):
    t_tok, d = x2.shape
    d_in = w_in_t.shape[0]
    tm = min(512, t_tok)

    def body(x_ref, w_ref, h_ref, xb_ref):
        xb = x_ref[...].astype(BF16)
        xb_ref[...] = xb
        h_ref[...] = _dot(w_ref[...], xb, NT)

    return _carry(
        body, name="proj_in", grid=(t_tok // tm,), comms=comms,
        in_specs=[pl.BlockSpec((tm, d), lambda i: (i, 0)), pl.BlockSpec((d_in, d), lambda i: (0, 0))],
        out_specs=[pl.BlockSpec((d_in, tm), lambda i: (0, i)), pl.BlockSpec((tm, d), lambda i: (i, 0))],
        out_shape=[jax.ShapeDtypeStruct((d_in, t_tok), F32), jax.ShapeDtypeStruct((t_tok, d), BF16)],
        args=(x2, w_in_t))


MIX_BLOCKS = 2
MIX_W = MIX_BLOCKS * BLK


def _prev_block(i):
    return jnp.maximum(MIX_BLOCKS * i - 1, 0)


def _h_specs():
    kv_row = COL_K // (2 * D_KV)
    return [
        pl.BlockSpec((D_GMLP, MIX_W), lambda i: (0, i)),
        pl.BlockSpec((D_GMLP, MIX_W), lambda i: (1, i)),
        pl.BlockSpec((D_ATTN, MIX_W), lambda i: (2, i)),
        pl.BlockSpec((2 * D_KV, MIX_W), lambda i: (kv_row, i)),
        pl.BlockSpec((2 * D_KV, BLK), lambda i: (kv_row, _prev_block(i))),
    ]


def _table_specs():
    return [
        pl.BlockSpec((HEAD_DIM, MIX_W), lambda i: (0, i)),
        pl.BlockSpec((HEAD_DIM, MIX_W), lambda i: (0, i)),
        pl.BlockSpec((HEAD_DIM, BLK), lambda i: (0, _prev_block(i))),
        pl.BlockSpec((HEAD_DIM, BLK), lambda i: (0, _prev_block(i))),
    ]


def _cols(b):
    return slice(b * BLK, (b + 1) * BLK)


def _block_inputs(b, i, kvc, kvp_ref, cos, sin, cosp_ref, sinp_ref, bias_ref):
    if b == 0:
        kv_prev, cos_prev, sin_prev, bias = kvp_ref[...], cosp_ref[...], sinp_ref[...], bias_ref[jnp.minimum(i, 1)]
    else:
        kv_prev, cos_prev, sin_prev, bias = kvc[:, _cols(b - 1)], cos[:, _cols(b - 1)], sin[:, _cols(b - 1)], bias_ref[1]
    return kvc[:, _cols(b)], kv_prev, cos[:, _cols(b)], sin[:, _cols(b)], cos_prev, sin_prev, bias


def _band_bias():
    ki = lax.broadcasted_iota(jnp.int32, (2, 2 * BLK, BLK), 1)
    qi = lax.broadcasted_iota(jnp.int32, (2, 2 * BLK, BLK), 2)
    later = lax.broadcasted_iota(jnp.int32, (2, 2 * BLK, BLK), 0) > 0
    dist = qi + BLK - ki
    return jnp.where((dist >= 0) & (dist < BLK) & ((ki >= BLK) | later), 0.0, NEG_INF).astype(F32)


BIAS_SPEC = pl.BlockSpec((2, 2 * BLK, BLK), lambda i: (0, 0, 0))


def _keys_values(kvc, kvp, cosc, sinc, cosp, sinp):
    kp, kc = _rope_t(kvp[:D_KV], cosp, sinp), _rope_t(kvc[:D_KV], cosc, sinc)
    k_t = jnp.concatenate([kp, kc], axis=1).astype(BF16)
    k_n = jnp.concatenate([kp.T, kc.T], axis=0).astype(BF16)
    v_t = jnp.concatenate([kvp[D_KV:], kvc[D_KV:]], axis=1).astype(BF16)
    return k_t, k_n, v_t


def _pad_head(th, kv):
    z = jnp.zeros_like(th)
    return jnp.concatenate([th, z] if kv == 0 else [z, th], axis=0)


def _group_lanes(parts):
    return jnp.concatenate(parts, axis=1)


def _softmax_sink_t(s, sink):
    m = jnp.maximum(jnp.max(s, axis=0, keepdims=True), sink)
    e = jnp.exp(s - m)
    es = jnp.exp(sink - m)
    r = 1.0 / (jnp.sum(e, axis=0, keepdims=True) + es)
    return e * r, es * r


def _causal():
    row = lax.broadcasted_iota(jnp.int32, (BLK, BLK), 0)
    col = lax.broadcasted_iota(jnp.int32, (BLK, BLK), 1)
    return row >= col


def _mask_w_once(wsp_ref, wm_scr):
    @pl.when(pl.program_id(0) == 0)
    def _():
        causal = _causal()
        for hh in range(N_HEADS):
            wm_scr[hh] = jnp.where(causal, wsp_ref[hh], 0.0).astype(BF16)


def _mixer_fwd(h_t, cos_t, sin_t, w_spatial, b_spatial, vln_g, vln_b, sinks, band_bias, comms=()):
    t_tok = h_t.shape[1]
    group = N_HEADS // N_KV_HEADS

    def body(sinks_ref, u_ref, vg_ref, q_ref, kvc_ref, kvp_ref, cos_ref, sin_ref, cosp_ref, sinp_ref,
             wsp_ref, bsp_ref, g_ref, b_ref, bias_ref, cat_ref, wm_scr):
        i = pl.program_id(0)
        _mask_w_once(wsp_ref, wm_scr)
        ua = _gelu(u_ref[...])
        vp, _, _ = _ln_fwd_t(_gelu(vg_ref[...]), g_ref[...], b_ref[...])
        vpb = vp.astype(BF16)
        for b in range(MIX_BLOCKS):
            for hh in range(N_HEADS):
                rows = slice(hh * HEAD_DIM, (hh + 1) * HEAD_DIM)
                mixed = _dot(vpb[rows, _cols(b)], wm_scr[hh], NT) + bsp_ref[hh:hh + 1, :]
                cat_ref[rows, _cols(b)] = (ua[rows, _cols(b)] * mixed).astype(BF16)

        kvc, cos, sin = kvc_ref[...], cos_ref[...], sin_ref[...]
        qr = (_rope_t(q_ref[...], cos, sin) * SCORE_SCALE).astype(BF16)
        sinks4 = [_group_lanes([jnp.full((1, BLK), sinks_ref[hh], F32) for hh in range(kv * group, (kv + 1) * group)])
                  for kv in range(N_KV_HEADS)]
        for b in range(MIX_BLOCKS):
            kv_cur, kv_prev, cosc, sinc, cosp, sinp, bias1 = _block_inputs(b, i, kvc, kvp_ref, cos, sin, cosp_ref, sinp_ref, bias_ref)
            _, k_n, v_t = _keys_values(kv_cur, kv_prev, cosc, sinc, cosp, sinp)
            bias = _group_lanes([bias1] * group)
            for kv in range(N_KV_HEADS):
                heads = range(kv * group, (kv + 1) * group)
                qs = _group_lanes([qr[hh * HEAD_DIM:(hh + 1) * HEAD_DIM, _cols(b)] for hh in heads])
                p, _ = _softmax_sink_t(_dot(k_n, _pad_head(qs, kv)) + bias, sinks4[kv])
                o = _dot(v_t[kv * HEAD_DIM:(kv + 1) * HEAD_DIM], p.astype(BF16)).astype(BF16)
                for j, hh in enumerate(heads):
                    cat_ref[D_GMLP + hh * HEAD_DIM:D_GMLP + (hh + 1) * HEAD_DIM, _cols(b)] = o[:, j * BLK:(j + 1) * BLK]

    full = lambda shape: pl.BlockSpec(shape, lambda i: (0,) * len(shape))
    return _carry(
        body, name="mixer_fwd", grid=(t_tok // MIX_W,), comms=comms,
        in_specs=[pl.BlockSpec(memory_space=pltpu.SMEM)] + _h_specs() + _table_specs() + [
            full((N_HEADS, BLK, BLK)), full((N_HEADS, BLK)), full((D_GMLP, 1)), full((D_GMLP, 1)), BIAS_SPEC],
        out_specs=[pl.BlockSpec((D_GMLP + D_ATTN, MIX_W), lambda i: (0, i))],
        out_shape=[jax.ShapeDtypeStruct((D_GMLP + D_ATTN, t_tok), BF16)],
        scratch_shapes=[pltpu.VMEM((N_HEADS, BLK, BLK), BF16)],
        args=(sinks, h_t, h_t, h_t, h_t, h_t, cos_t, sin_t, cos_t, sin_t, w_spatial, b_spatial, vln_g, vln_b, band_bias))


def _proj_out(cat_t, x2, w_out_b, ln1_g, ln1_b, comms=()):
    t_tok, d = x2.shape
    tm = min(512, t_tok)

    def body(cat_ref, x_ref, w_ref, g_ref, b_ref, xhat_ref, rstd_ref, x1b_ref):
        x1, xhat, rstd = _ln_fwd(ALPHA * x_ref[...] + _dot(cat_ref[...], w_ref[...], TN), g_ref[...], b_ref[...])
        xhat_ref[...] = xhat
        rstd_ref[...] = rstd
        x1b_ref[...] = x1.astype(BF16)

    tok = lambda w: pl.BlockSpec((tm, w), lambda i: (i, 0))
    vec = pl.BlockSpec((1, d), lambda i: (0, 0))
    return _carry(
        body, name="proj_out", grid=(t_tok // tm,), comms=comms,
        in_specs=[pl.BlockSpec((cat_t.shape[0], tm), lambda i: (0, i)), tok(d), pl.BlockSpec(w_out_b.shape, lambda i: (0, 0)), vec, vec],
        out_specs=[tok(d), tok(1), tok(d)],
        out_shape=[jax.ShapeDtypeStruct((t_tok, d), F32), jax.ShapeDtypeStruct((t_tok, 1), F32), jax.ShapeDtypeStruct((t_tok, d), BF16)],
        args=(cat_t, x2, w_out_b, ln1_g, ln1_b))


def _ffn_fwd_bwd(xhat1, rstd1, x1b, target, w1_parts, w2_parts, ln1_g, ln1_b, ln2_g, ln2_b):
    t_tok, d = xhat1.shape
    n_part = len(w1_parts)
    n_chunk, _, fp = w1_parts[0].shape
    fc = n_part * fp
    f = n_chunk * fc
    tm = min(256, t_tok)

    def body(xhat1_ref, rstd1_ref, x1b_ref, tgt_ref, *refs):
        w1_hbm, w2_hbm = refs[:n_part], refs[n_part:2 * n_part]
        (g1_ref, b1_ref, g2_ref, b2_ref, act_ref, dpre_ref, dz2b_ref, dz1_ref, stats_ref,
         r_scr, w1_ref, w2_ref, w_sems) = refs[2 * n_part:]

        @pl.when(pl.program_id(0) == 0)
        def _():
            stats_ref[...] = jnp.zeros_like(stats_ref)
            loads = [pltpu.make_async_copy(w1_hbm[p], w1_ref.at[:, :, pl.ds(p * fp, fp)], w_sems.at[0, p]) for p in range(n_part)]
            loads += [pltpu.make_async_copy(w2_hbm[p], w2_ref.at[:, pl.ds(p * fp, fp), :], w_sems.at[1, p]) for p in range(n_part)]
            for cp in loads:
                cp.start()
            for cp in loads:
                cp.wait()

        g1, g2 = g1_ref[...], g2_ref[...]
        xhat1, x1b = xhat1_ref[...], x1b_ref[...]
        ff = jnp.zeros((tm, d), F32)
        for j in range(n_chunk):
            r = jnp.maximum(_dot(x1b, w1_ref[j]), 0.0)
            r_scr[:, j * fc:(j + 1) * fc] = r
            act = (r * r).astype(BF16)
            act_ref[:, j * fc:(j + 1) * fc] = act
            ff = ff + _dot(act, w2_ref[j])
        y, xhat2, rstd2 = _ln_fwd(ALPHA * (xhat1 * g1 + b1_ref[...]) + ff, g2, b2_ref[...])
        diff = y - tgt_ref[...]
        loss = 0.5 * jnp.sum(jnp.sum(diff * diff, axis=-1, keepdims=True) / d, axis=0, keepdims=True)
        dy = diff / d
        dz2 = _ln_bwd(dy, xhat2, rstd2, g2)
        dz2b = dz2.astype(BF16)
        dz2b_ref[...] = dz2b
        dx1 = ALPHA * dz2
        for j in range(n_chunk):
            dpre = (_dot(dz2b, w2_ref[j], NT) * (2.0 * r_scr[:, j * fc:(j + 1) * fc])).astype(BF16)
            dpre_ref[:, j * fc:(j + 1) * fc] = dpre
            dx1 = dx1 + _dot(dpre, w1_ref[j], NT)
        dz1_ref[...] = _ln_bwd(dx1, xhat1, rstd1_ref[...], g1)
        stats_ref[0:1, :] += jnp.sum(dx1 * xhat1, axis=0, keepdims=True)
        stats_ref[1:2, :] += jnp.sum(dx1, axis=0, keepdims=True)
        stats_ref[2:3, :] += jnp.sum(dy * xhat2, axis=0, keepdims=True)
        stats_ref[3:4, :] += jnp.sum(dy, axis=0, keepdims=True)
        stats_ref[4:5, :] += jnp.broadcast_to(loss, (1, d))

    tok = lambda w: pl.BlockSpec((tm, w), lambda i: (i, 0))
    vec = pl.BlockSpec((1, d), lambda i: (0, 0))
    return _carry(
        body, name="ffn_fwd_bwd", grid=(t_tok // tm,),
        in_specs=[tok(d), tok(1), tok(d), tok(d)] + [ANY] * (2 * n_part) + [vec, vec, vec, vec],
        out_specs=[tok(f), tok(f), tok(d), tok(d), pl.BlockSpec((8, d), lambda i: (0, 0))],
        out_shape=[jax.ShapeDtypeStruct((t_tok, f), BF16), jax.ShapeDtypeStruct((t_tok, f), BF16),
                   jax.ShapeDtypeStruct((t_tok, d), BF16), jax.ShapeDtypeStruct((t_tok, d), F32), jax.ShapeDtypeStruct((8, d), F32)],
        scratch_shapes=[pltpu.VMEM((tm, f), F32), pltpu.VMEM((n_chunk, d, fc), BF16), pltpu.VMEM((n_chunk, fc, d), BF16),
                        pltpu.SemaphoreType.DMA((2, n_part))],
        args=(xhat1, rstd1, x1b, target, *w1_parts, *w2_parts, ln1_g, ln1_b, ln2_g, ln2_b))[0]


def _ffn_wgrad(name, lhs, rhs, chunk_lhs, core_chip, comms=()):
    t_tok = lhs.shape[0]
    half = N_DEV // 2
    fc = (lhs if chunk_lhs else rhs).shape[1] // N_DEV
    chunk = (fc, rhs.shape[1]) if chunk_lhs else (lhs.shape[1], fc)

    def shard(s, cc):
        return 2 * (s % half) + jnp.where(s < half, 1 - cc[0], cc[0])

    def body(cc_ref, lhs_ref, rhs_ref, wire_ref, own_ref, recv_ref, send_buf, got, send_sems, recv_sems, got_sem):
        s = pl.program_id(0)
        x, y, c = _place()
        partial = lambda: _dot(lhs_ref[...], rhs_ref[...], TN)

        def send(q):
            return pltpu.make_async_remote_copy(
                src_ref=send_buf.at[q % 2], dst_ref=recv_ref.at[q], send_sem=send_sems.at[q], recv_sem=recv_sems.at[q],
                device_id=(x, y, 1 - c), device_id_type=MESH)

        for q in range(half):
            @pl.when(s == q)
            def _(q=q):
                if q >= 2:
                    send(q - 2).wait_send()
                send_buf[q % 2] = partial()
                send(q).start()

            @pl.when(s == half + q)
            def _(q=q):
                send(q).wait_recv()
                load = pltpu.make_async_copy(recv_ref.at[q], got, got_sem.at[0])
                load.start()
                mine = partial()
                load.wait()
                total = mine + got[...]
                wire_ref[...] = total.astype(BF16)

                @pl.when(cc_ref[1] == q)
                def _():
                    own_ref[...] = total

        @pl.when(s == N_DEV - 1)
        def _():
            for q in range(half - 2, half):
                send(q).wait_send()

    resident = lambda a: pl.BlockSpec(a.shape, lambda s, cc: (0, 0), pipeline_mode=pl.Buffered(1))
    chunked = pl.BlockSpec((t_tok, fc), lambda s, cc: (0, shard(s, cc)))
    (wire, own, _), per_comm = _carry(
        body, name=name, grid=(N_DEV,), comms=comms, prefetch=(core_chip,),
        in_specs=[chunked, resident(rhs)] if chunk_lhs else [resident(lhs), chunked],
        out_specs=[pl.BlockSpec((None,) + chunk, lambda s, cc: (jnp.maximum(s - half, 0), 0, 0)),
                   pl.BlockSpec(chunk, lambda s, cc: (0, 0)), ANY],
        out_shape=[jax.ShapeDtypeStruct((half,) + chunk, BF16), jax.ShapeDtypeStruct(chunk, F32),
                   jax.ShapeDtypeStruct((half,) + chunk, F32)],
        scratch_shapes=[pltpu.VMEM((2,) + chunk, F32), pltpu.VMEM(chunk, F32), pltpu.SemaphoreType.DMA((half,)),
                        pltpu.SemaphoreType.DMA((half,)), pltpu.SemaphoreType.DMA((1,))],
        args=(lhs, rhs))
    return wire, own, per_comm


def _proj_out_bwd(dz1, cat_t, w_out_b, comms=()):
    t_tok, d = dz1.shape
    d_mix = cat_t.shape[0]
    tm = min(512, t_tok)

    def body(dz1_ref, cat_ref, w_ref, dcat_ref, gw_ref):
        @pl.when(pl.program_id(0) == 0)
        def _():
            gw_ref[...] = jnp.zeros_like(gw_ref)

        dzb = dz1_ref[...].astype(BF16)
        dcat_ref[...] = _dot(w_ref[...], dzb, NT)
        gw_ref[...] += _dot(cat_ref[...], dzb)

    return _carry(
        body, name="proj_out_bwd", grid=(t_tok // tm,), comms=comms,
        in_specs=[pl.BlockSpec((tm, d), lambda i: (i, 0)), pl.BlockSpec((d_mix, tm), lambda i: (0, i)),
                  pl.BlockSpec((d_mix, d), lambda i: (0, 0))],
        out_specs=[pl.BlockSpec((d_mix, tm), lambda i: (0, i)), pl.BlockSpec((d_mix, d), lambda i: (0, 0))],
        out_shape=[jax.ShapeDtypeStruct((d_mix, t_tok), F32), jax.ShapeDtypeStruct((d_mix, d), F32)],
        args=(dz1, cat_t, w_out_b))


def _mixer_bwd(dcat_t, h_t, cos_t, sin_t, w_spatial, b_spatial, vln_g, vln_b, sinks, band_bias, comms=()):
    t_tok = h_t.shape[1]
    nb, n_step = t_tok // BLK, t_tok // MIX_W
    group = N_HEADS // N_KV_HEADS

    def body(sinks_ref, dcat_ref, u_ref, vg_ref, q_ref, kvc_ref, kvp_ref, cos_ref, sin_ref, cosp_ref, sinp_ref,
             wsp_ref, bsp_ref, g_ref, b_ref, bias_ref, dh_ref, dkvc_ref, dkvp_ref, gwsb_ref, gbsp_ref, gvln_ref, gsink_ref,
             dg_acc, db_acc, wm_scr, gws_ref):
        i = pl.program_id(0)

        @pl.when(i == 0)
        def _():
            gws_ref[...] = jnp.zeros_like(gws_ref)
            gbsp_ref[...] = jnp.zeros_like(gbsp_ref)
            gsink_ref[...] = jnp.zeros_like(gsink_ref)
            dg_acc[...] = jnp.zeros_like(dg_acc)
            db_acc[...] = jnp.zeros_like(db_acc)

        _mask_w_once(wsp_ref, wm_scr)

        g = g_ref[...]
        ua, ua_grad = _gelu_and_grad(u_ref[...])
        vv, vv_grad = _gelu_and_grad(vg_ref[...])
        vp, vhat, rstd = _ln_fwd_t(vv, g, b_ref[...])
        vpb = vp.astype(BF16)
        da = dcat_ref[0:D_GMLP, :]
        dmixed = da * ua
        dvp_blocks = []
        for b in range(MIX_BLOCKS):
            dvp_parts = []
            for hh in range(N_HEADS):
                rows = slice(hh * HEAD_DIM, (hh + 1) * HEAD_DIM)
                vpb_h = vpb[rows, _cols(b)]
                mixed = _dot(vpb_h, wm_scr[hh], NT) + bsp_ref[hh:hh + 1, :]
                dh_ref[COL_U + hh * HEAD_DIM:COL_U + (hh + 1) * HEAD_DIM, _cols(b)] = (
                    da[rows, _cols(b)] * mixed * ua_grad[rows, _cols(b)]).astype(BF16)
                dm = dmixed[rows, _cols(b)]
                dmb = dm.astype(BF16)
                gbsp_ref[hh:hh + 1, :] += jnp.sum(dm, axis=0, keepdims=True)
                gws_ref[hh] += _dot(dmb, vpb_h, TN)
                dvp_parts.append(_dot(dmb, wm_scr[hh]))
            dvp_blocks.append(jnp.concatenate(dvp_parts, axis=0))
        dvp = jnp.concatenate(dvp_blocks, axis=1)
        dgv, dbv = dvp * vhat, dvp
        for b in range(MIX_BLOCKS):
            dg_acc[...] += dgv[:, _cols(b)]
            db_acc[...] += dbv[:, _cols(b)]
        dh_ref[COL_V:COL_V + D_GMLP, :] = (_ln_bwd_t(dvp, vhat, rstd, g) * vv_grad).astype(BF16)

        kvc, cos, sin = kvc_ref[...], cos_ref[...], sin_ref[...]
        qr = (_rope_t(q_ref[...], cos, sin) * SCORE_SCALE).astype(BF16)
        sinks4 = [_group_lanes([jnp.full((1, BLK), sinks_ref[hh], F32) for hh in range(kv * group, (kv + 1) * group)])
                  for kv in range(N_KV_HEADS)]
        dq_blocks, dkv_cur, dkv_prev = [], [], []
        for b in range(MIX_BLOCKS):
            kv_cur, kv_prev, cosc, sinc, cosp, sinp, bias1 = _block_inputs(b, i, kvc, kvp_ref, cos, sin, cosp_ref, sinp_ref, bias_ref)
            k_t, k_n, v_t = _keys_values(kv_cur, kv_prev, cosc, sinc, cosp, sinp)
            v_n = jnp.concatenate([kv_prev[D_KV:].T, kv_cur[D_KV:].T], axis=0).astype(BF16)
            bias = _group_lanes([bias1] * group)
            dk, dv, dq_parts = [], [], []
            for kv in range(N_KV_HEADS):
                heads = range(kv * group, (kv + 1) * group)
                kv_rows = slice(kv * HEAD_DIM, (kv + 1) * HEAD_DIM)
                qs = _group_lanes([qr[hh * HEAD_DIM:(hh + 1) * HEAD_DIM, _cols(b)] for hh in heads])
                dos = _group_lanes([dcat_ref[D_GMLP + hh * HEAD_DIM:D_GMLP + (hh + 1) * HEAD_DIM, _cols(b)]
                                    for hh in heads]).astype(BF16)
                p, p_sink = _softmax_sink_t(_dot(k_n, _pad_head(qs, kv)) + bias, sinks4[kv])
                dp = _dot(v_n, _pad_head(dos, kv))
                delta = jnp.sum(p * dp, axis=0, keepdims=True)
                ds = (p * (dp - delta)).astype(BF16)
                dsink = p_sink * delta
                dq = _dot(k_t[kv_rows], ds) * SCORE_SCALE
                for j, hh in enumerate(heads):
                    gsink_ref[hh:hh + 1, :] -= dsink[:, j * BLK:(j + 1) * BLK]
                    dq_parts.append(dq[:, j * BLK:(j + 1) * BLK])
                dk.append(_dot(qs, ds, NT))
                dv.append(_dot(dos, p.astype(BF16), NT))
            dq_blocks.append(jnp.concatenate(dq_parts, axis=0))
            dk_all, dv_all = jnp.concatenate(dk, axis=0), jnp.concatenate(dv, axis=0)
            dkv_cur.append(jnp.concatenate([_rope_t(dk_all[:, BLK:], cosc, sinc, bwd=True), dv_all[:, BLK:]], axis=0))
            dkv_prev.append(jnp.concatenate([_rope_t(dk_all[:, :BLK], cosp, sinp, bwd=True), dv_all[:, :BLK]], axis=0))
        dh_ref[COL_Q:COL_Q + D_ATTN, :] = _rope_t(jnp.concatenate(dq_blocks, axis=1), cos, sin, bwd=True).astype(BF16)
        for b in range(MIX_BLOCKS):
            dkvc_ref[:, _cols(b)] = dkv_cur[b] + dkv_prev[b + 1] if b + 1 < MIX_BLOCKS else dkv_cur[b]
        dkvp_ref[...] = dkv_prev[0]

        @pl.when(i == n_step - 1)
        def _():
            causal = _causal()
            for hh in range(N_HEADS):
                gwsb_ref[hh] = jnp.where(causal, gws_ref[hh], 0.0).astype(BF16)
            gvln_ref[...] = jnp.zeros_like(gvln_ref)
            gvln_ref[0:1, :] = jnp.sum(dg_acc[...].T, axis=0, keepdims=True)
            gvln_ref[1:2, :] = jnp.sum(db_acc[...].T, axis=0, keepdims=True)

    full = lambda shape: pl.BlockSpec(shape, lambda i: (0,) * len(shape))
    return _carry(
        body, name="mixer_bwd", grid=(n_step,), comms=comms,
        in_specs=[pl.BlockSpec(memory_space=pltpu.SMEM), pl.BlockSpec((D_GMLP + D_ATTN, MIX_W), lambda i: (0, i))]
        + _h_specs() + _table_specs()
        + [full((N_HEADS, BLK, BLK)), full((N_HEADS, BLK)), full((D_GMLP, 1)), full((D_GMLP, 1)), BIAS_SPEC],
        out_specs=[pl.BlockSpec((COL_K, MIX_W), lambda i: (0, i)), pl.BlockSpec((2 * D_KV, MIX_W), lambda i: (0, i)),
                   pl.BlockSpec((2 * D_KV, BLK), lambda i: (0, (i + n_step - 1) % n_step)),
                   full((N_HEADS, BLK, BLK)), full((N_HEADS, BLK)), full((8, D_GMLP)), full((N_HEADS, LANES))],
        out_shape=[jax.ShapeDtypeStruct((COL_K, t_tok), BF16), jax.ShapeDtypeStruct((2 * D_KV, t_tok), F32),
                   jax.ShapeDtypeStruct((2 * D_KV, n_step * BLK), F32),
                   jax.ShapeDtypeStruct((N_HEADS, BLK, BLK), BF16), jax.ShapeDtypeStruct((N_HEADS, BLK), F32),
                   jax.ShapeDtypeStruct((8, D_GMLP), F32), jax.ShapeDtypeStruct((N_HEADS, LANES), F32)],
        scratch_shapes=[pltpu.VMEM((D_GMLP, BLK), F32), pltpu.VMEM((D_GMLP, BLK), F32), pltpu.VMEM((N_HEADS, BLK, BLK), BF16),
                        pltpu.VMEM((N_HEADS, BLK, BLK), F32)],
        args=(sinks, dcat_t, h_t, h_t, h_t, h_t, h_t, cos_t, sin_t, cos_t, sin_t, w_spatial, b_spatial, vln_g, vln_b, band_bias))


def _proj_in_wgrad(dh_b, dkvc_t, dkvp_t, xb, comms=()):
    t_tok, d = xb.shape
    d_main, d_kv = dh_b.shape[0], dkvc_t.shape[0]
    tm = min(1024, t_tok)

    def body(dh_ref, dkvc_ref, dkvp_ref, xb_ref, dkvb_ref, gw_ref):
        @pl.when(pl.program_id(0) == 0)
        def _():
            gw_ref[...] = jnp.zeros_like(gw_ref)

        for s in range(tm // MIX_W):
            last = slice((s + 1) * MIX_W - BLK, (s + 1) * MIX_W)
            dkvb_ref[:, s * MIX_W:(s + 1) * MIX_W - BLK] = dkvc_ref[:, s * MIX_W:(s + 1) * MIX_W - BLK].astype(BF16)
            dkvb_ref[:, last] = (dkvc_ref[:, last] + dkvp_ref[:, _cols(s)]).astype(BF16)
        gw_ref[0:d_main, :] += _dot(dh_ref[...], xb_ref[...])
        gw_ref[d_main:, :] += _dot(dkvb_ref[...], xb_ref[...])

    tok = lambda rows: pl.BlockSpec((rows, tm), lambda i: (0, i))
    return _carry(
        body, name="proj_in_wgrad", grid=(t_tok // tm,), comms=comms,
        in_specs=[tok(d_main), tok(d_kv), pl.BlockSpec((d_kv, tm // MIX_BLOCKS), lambda i: (0, i)),
                  pl.BlockSpec((tm, d), lambda i: (i, 0))],
        out_specs=[tok(d_kv), pl.BlockSpec((d_main + d_kv, d), lambda i: (0, 0))],
        out_shape=[jax.ShapeDtypeStruct((d_kv, t_tok), BF16), jax.ShapeDtypeStruct((d_main + d_kv, d), F32)],
        args=(dh_b, dkvc_t, dkvp_t, xb))


def _proj_in_dgrad(dh_b, dkv_b, dz1, w_in_t, comms=()):
    t_tok, d = dz1.shape
    d_main, d_kv = dh_b.shape[0], dkv_b.shape[0]
    tm = min(512, t_tok)

    def body(dh_ref, dkv_ref, dz1_ref, w_ref, dx_ref):
        dx_ref[...] = (ALPHA * dz1_ref[...] + _dot(dh_ref[...], w_ref[0:d_main, :], TN)
                       + _dot(dkv_ref[...], w_ref[d_main:, :], TN))

    return _carry(
        body, name="proj_in_dgrad", grid=(t_tok // tm,), comms=comms,
        in_specs=[pl.BlockSpec((d_main, tm), lambda i: (0, i)), pl.BlockSpec((d_kv, tm), lambda i: (0, i)),
                  pl.BlockSpec((tm, d), lambda i: (i, 0)), pl.BlockSpec((d_main + d_kv, d), lambda i: (0, 0))],
        out_specs=[pl.BlockSpec((tm, d), lambda i: (i, 0))],
        out_shape=[jax.ShapeDtypeStruct((t_tok, d), F32)],
        args=(dh_b, dkv_b, dz1, w_in_t))


def _adamw(w, g, m, v):
    m = ADAM_B1 * m + (1.0 - ADAM_B1) * g
    v = ADAM_B2 * v + (1.0 - ADAM_B2) * (g * g)
    m_hat = m / (1.0 - ADAM_B1 ** ADAM_STEP)
    v_hat = v / (1.0 - ADAM_B2 ** ADAM_STEP)
    delta = -ADAM_LR * (m_hat / (jnp.sqrt(v_hat) + ADAM_EPS) + ADAM_WD * w)
    return delta, m, v


def _row_tiled(name, own, recv, extra, n_out, finish, comms=()):
    r, c = own.shape
    recv = [] if recv is None else list(recv)
    k = max(len(recv), 1)
    n = max(k, -(-r // 512))
    tr, per = r // n, n // k
    blk = pl.BlockSpec((tr, c), lambda i: (i, 0))

    def body(own_ref, *refs):
        recv_refs, rest = refs[:len(recv)], refs[len(recv):]
        ins, outs = rest[:len(extra)], rest[len(extra):]

        def tile(recv_ref):
            g = own_ref[...]
            if recv_ref is not None:
                g = ((g + recv_ref[0].astype(F32)) + recv_ref[1].astype(F32)) + recv_ref[2].astype(F32)
            for o_ref, val in zip(outs, finish(g, *[a[...] for a in ins])):
                o_ref[...] = val

        if len(recv) <= 1:
            tile(recv_refs[0] if recv else None)
        else:
            for p in range(k):
                pl.when(pl.program_id(0) // per == p)(functools.partial(tile, recv_refs[p]))

    recv_specs = [pl.BlockSpec((3, tr, c), lambda i, p=p: (0, jnp.clip(i - p * per, 0, per - 1), 0)) for p in range(len(recv))]
    return _carry(
        body, name=name, grid=(n,), comms=comms,
        in_specs=[blk] + recv_specs + [blk] * len(extra),
        out_specs=[blk] * n_out, out_shape=[jax.ShapeDtypeStruct((r, c), F32)] * n_out,
        args=(own, *recv, *extra))


def _adamw_shard(name, own, recv, w, m, v, comms=()):
    def finish(g, w_t, m_t, v_t):
        return (g,) + _adamw(w_t, g, m_t, v_t)

    return _row_tiled(name, own, recv, (w, m, v), 4, finish, comms)


VEC_VLN, VEC_LN1G, VEC_LN1B, VEC_LN2G, VEC_LN2B, VEC_SINK, VEC_LOSS, VEC_BSP, VEC_ROWS = 0, 1, 2, 3, 4, 5, 6, 8, 16


def _adamw_small(parts_w, parts_vec, params):
    n = parts_w.shape[0]
    flat = [a for p in params for a in p]
    shapes = [p[0].shape for p in params]

    def grads(gw, gv):
        return [gw, gv[VEC_VLN:VEC_VLN + 1, 0:D_GMLP], gv[VEC_VLN:VEC_VLN + 1, D_GMLP:2 * D_GMLP],
                gv[VEC_BSP:VEC_BSP + N_HEADS, 0:BLK], gv[VEC_LN1G:VEC_LN1G + 1], gv[VEC_LN1B:VEC_LN1B + 1],
                gv[VEC_LN2G:VEC_LN2G + 1], gv[VEC_LN2B:VEC_LN2B + 1], gv[VEC_SINK:VEC_SINK + 1, 0:N_HEADS]]

    def body(pw_ref, pv_ref, *refs):
        ins, outs = refs[:len(flat)], refs[len(flat):]
        gw, gv = pw_ref[0].astype(F32), pv_ref[0]
        for k in range(1, n):
            gw, gv = gw + pw_ref[k].astype(F32), gv + pv_ref[k]
        for i, g in enumerate(grads(gw, gv)):
            w_ref, m_ref, v_ref = ins[3 * i:3 * i + 3]
            delta, m_new, v_new = _adamw(w_ref[...], g, m_ref[...], v_ref[...])
            for o_ref, val in zip(outs[4 * i:4 * i + 4], (g, delta, m_new, v_new)):
                o_ref[...] = val
        outs[-1][...] = gv[VEC_LOSS:VEC_LOSS + 1, 0:LANES]

    whole = lambda shape: pl.BlockSpec(shape, lambda i: (0,) * len(shape))
    res = _carry(
        body, name="adamw_small", grid=(1,),
        in_specs=[whole(parts_w.shape), whole(parts_vec.shape)] + [whole(a.shape) for a in flat],
        out_specs=[whole(s) for s in shapes for _ in range(4)] + [whole((1, LANES))],
        out_shape=[jax.ShapeDtypeStruct(s, F32) for s in shapes for _ in range(4)] + [jax.ShapeDtypeStruct((1, LANES), F32)],
        args=(parts_w, parts_vec, *flat))[0]
    return [res[4 * i:4 * i + 4] for i in range(len(params))], res[-1]


def _pair_sum(name, parts, recv, core_chip, comms=()):
    _, r, c = parts.shape
    tr = r if r <= 512 else 512

    def body(cc_ref, a_ref, b_ref, wire_ref, own_ref):
        s = a_ref[...] + b_ref[...]
        wire_ref[...] = s.astype(BF16)

        @pl.when(pl.program_id(1) == cc_ref[1])
        def _():
            own_ref[...] = s

    return _carry(
        body, name=name, grid=(r // tr, 4), prefetch=(core_chip,), comms=comms,
        in_specs=[pl.BlockSpec((None, tr, c), lambda i, q, cc: (2 * q + cc[0], i, 0)),
                  pl.BlockSpec((None, tr, c), lambda i, q, cc: (q, i, 0))],
        out_specs=[pl.BlockSpec((None, tr, c), lambda i, q, cc: (q, i, 0)), pl.BlockSpec((tr, c), lambda i, q, cc: (i, 0))],
        out_shape=[jax.ShapeDtypeStruct((4, r, c), BF16), jax.ShapeDtypeStruct((r, c), F32)],
        args=(parts, recv))


def kernel(x, positions, w_in, v_ln_g, v_ln_b, w_spatial, b_spatial, sinks, w_out, ln1_g, ln1_b, w_ff1, w_ff2, ln2_g, ln2_b, loss_target, m_w_in, m_v_ln_g, m_v_ln_b, m_w_spatial, m_b_spatial, m_sinks, m_w_out, m_ln1_g, m_ln1_b, m_w_ff1, m_w_ff2, m_ln2_g, m_ln2_b, v_w_in, v_v_ln_g, v_v_ln_b, v_w_spatial, v_b_spatial, v_sinks, v_w_out, v_ln1_g, v_ln1_b, v_w_ff1, v_w_ff2, v_ln2_g, v_ln2_b):
    _, t_tok, d = x.shape
    xi, yi, ci = _place()
    core_chip = jnp.stack([ci, 2 * xi + yi]).astype(jnp.int32)
    x2 = x.reshape(t_tok, d)
    target = loss_target.reshape(t_tok, d)
    inv_freq = ROPE_THETA ** (-jnp.arange(0, HEAD_DIM, 2, dtype=F32) / HEAD_DIM)
    wsp, bsp, sink_vec = w_spatial[0], b_spatial[0], sinks[0]
    vg_col, vb_col = v_ln_g.reshape(D_GMLP, 1), v_ln_b.reshape(D_GMLP, 1)
    big = {"in": w_in[0], "out": w_out[0], "ff1": w_ff1[0], "ff2": w_ff2[0]}
    half1, half2 = big["ff1"].shape[1] // 2, big["ff2"].shape[0] // 2
    w1_mine = [big["ff1"][:, :half1].astype(BF16), big["ff1"][:, half1:].astype(BF16)]
    w2_mine = [big["ff2"][:half2].astype(BF16), big["ff2"][half2:].astype(BF16)]

    (cos_t, sin_t), ((g_in,),) = _rope_tables(
        positions, jnp.tile(inv_freq, 2).reshape(HEAD_DIM, 1), comms=[_gather_comm([big["in"].T.astype(BF16)])])
    w_in_t = g_in.reshape(D_IN, d)
    (h_t, xb), ((g_out, w1_a),) = _proj_in(x2, w_in_t, comms=[_gather_comm([big["out"].astype(BF16), w1_mine[0]])])
    w_out_b = g_out.reshape(-1, d)
    band_bias = _band_bias()
    (cat_t,), ((w1_b, w2_a),) = _mixer_fwd(h_t, cos_t, sin_t, wsp, bsp, vg_col, vb_col, sink_vec, band_bias,
                                           comms=[_gather_comm([w1_mine[1], w2_mine[0]])])
    (xhat1, rstd1, x1b), ((w2_b,),) = _proj_out(cat_t, x2, w_out_b, ln1_g, ln1_b, comms=[_gather_comm([w2_mine[1]])])
    act_b, dpre_b, dz2b, dz1, stats = _ffn_fwd_bwd(xhat1, rstd1, x1b, target, [w1_a, w1_b], [w2_a, w2_b], ln1_g, ln1_b, ln2_g, ln2_b)

    (dcat_t, gw_out), _ = _proj_out_bwd(dz1, cat_t, w_out_b)
    p_out = gw_out.reshape(N_DEV, -1, d)
    wire_ff1, own_ff1, ((s_out,),) = _ffn_wgrad("ffn_wgrad1", x1b, dpre_b, False, core_chip, comms=[_sibling_comm([p_out])])
    (wire_out, own_out), _ = _pair_sum("pair_sum_out", p_out, s_out, core_chip)
    wire_ff2, own_ff2, ((r_ff1,),) = _ffn_wgrad("ffn_wgrad2", act_b, dz2b, True, core_chip, comms=[_chips_comm([wire_ff1])])
    (dh_b, dkvc_t, dkvp_t, g_wsp, g_bsp, g_vln, g_sink), ((r_ff2, r_out),) = _mixer_bwd(
        dcat_t, h_t, cos_t, sin_t, wsp, bsp, vg_col, vb_col, sink_vec, band_bias,
        comms=[_chips_comm([wire_ff2, wire_out])])
    sink_row = jnp.pad(g_sink.sum(axis=1).reshape(1, N_HEADS), ((0, 0), (0, d - N_HEADS)))
    small_vec = jnp.concatenate([g_vln[0:2].reshape(1, d), stats[0:4], sink_row, stats[4:5], jnp.zeros((1, d), F32),
                                 jnp.pad(g_bsp, ((0, 0), (0, d - BLK)))], axis=0)
    (dkv_b, gw_in_t), ((parts_w, parts_vec),) = _proj_in_wgrad(
        dh_b, dkvc_t, dkvp_t, xb, comms=[_gather_comm([g_wsp.reshape(-1, BLK), small_vec])])
    p_in = gw_in_t.reshape(N_DEV, -1, d)

    out_out, ((s_in,),) = _adamw_shard("adamw_out", own_out, [r_out], big["out"], m_w_out[0], v_w_out[0], comms=[_sibling_comm([p_in])])
    (wire_in, own_in), _ = _pair_sum("pair_sum_in", p_in, s_in, core_chip)
    (grad_x,), ((r_in,),) = _proj_in_dgrad(dh_b, dkv_b, dz1, w_in_t, comms=[_chips_comm([wire_in])])
    ff1_out, _ = _adamw_shard("adamw_ff1", own_ff1, [r_ff1], big["ff1"], m_w_ff1[0], v_w_ff1[0])
    ff2_out, _ = _adamw_shard("adamw_ff2", own_ff2, [r_ff2], big["ff2"], m_w_ff2[0], v_w_ff2[0])
    in_out_t, _ = _adamw_shard("adamw_in", own_in, [r_in], big["in"].T, m_w_in[0].T, v_w_in[0].T)
    in_out = [o.T for o in in_out_t]
    small = [(w_spatial, m_w_spatial, v_w_spatial), (v_ln_g, m_v_ln_g, v_v_ln_g), (v_ln_b, m_v_ln_b, v_v_ln_b),
             (b_spatial, m_b_spatial, v_b_spatial), (ln1_g, m_ln1_g, v_ln1_g), (ln1_b, m_ln1_b, v_ln1_b),
             (ln2_g, m_ln2_g, v_ln2_g), (ln2_b, m_ln2_b, v_ln2_b), (sinks, m_sinks, v_sinks)]
    views = [(-1, BLK), None, None, (N_HEADS, BLK)] + [None] * 5
    small_res, loss_row = _adamw_small(parts_w, parts_vec, [
        tuple(a if vw is None else a.reshape(vw) for a in p) for p, vw in zip(small, views)])
    small_out = [[o.reshape(p[0].shape) for o in res] for res, p in zip(small_res, small)]
    loss = loss_row[0, 0]

    big_out = {0: in_out, 6: out_out, 9: ff1_out, 10: ff2_out}
    small_slot = {3: 0, 1: 1, 2: 2, 4: 3, 7: 4, 8: 5, 11: 6, 12: 7, 5: 8}
    outs = [loss, grad_x.reshape(x.shape)]
    for kind in range(4):
        for wi in range(13):
            outs.append(big_out[wi][kind][None] if wi in big_out else small_out[small_slot[wi]][kind])
    return tuple(outs)
```

```python
import functools
import math

import jax
import jax.numpy as jnp
from jax import lax
from jax.experimental import pallas as pl
from jax.experimental.pallas import tpu as pltpu

F32 = jnp.float32
BF16 = jnp.bfloat16
MESH = pl.DeviceIdType.MESH

HEAD_DIM = 64
N_HEADS = 8
N_KV_HEADS = 2
BLK = 128
D_GMLP = N_HEADS * HEAD_DIM
D_ATTN = N_HEADS * HEAD_DIM
D_KV = N_KV_HEADS * HEAD_DIM
D_IN = 2 * D_GMLP + D_ATTN + 2 * D_KV
COL_U, COL_V, COL_Q, COL_K = 0, D_GMLP, 2 * D_GMLP, 2 * D_GMLP + D_ATTN
ROPE_THETA = 10000.0
LN_EPS = 1e-5
ALPHA = 2.0 ** 0.25
NEG_INF = -1e30
SCORE_SCALE = 1.0 / math.sqrt(HEAD_DIM)
ADAM_LR, ADAM_B1, ADAM_B2, ADAM_EPS, ADAM_WD, ADAM_STEP = 0.001, 0.9, 0.999, 1e-08, 0.01, 10
N_DEV = 8
LANES = 128
VMEM_LIMIT = 56 * 1024 * 1024

NT = (((1,), (1,)), ((), ()))
TN = (((0,), (0,)), ((), ()))


def _params(*sem):
    return pltpu.CompilerParams(dimension_semantics=sem, vmem_limit_bytes=VMEM_LIMIT)


def _dot(a, b, dims=None):
    if dims is None:
        return jnp.dot(a, b, preferred_element_type=F32)
    return lax.dot_general(a, b, dims, preferred_element_type=F32)


def _mean(a):
    return jnp.mean(a, axis=-1, keepdims=True)


def _ln_fwd(z, g, b):
    zc = z - _mean(z)
    rstd = lax.rsqrt(_mean(zc * zc) + LN_EPS)
    xhat = zc * rstd
    return xhat * g + b, xhat, rstd


def _ln_bwd(dy, xhat, rstd, g):
    dxhat = dy * g
    return rstd * (dxhat - _mean(dxhat) - xhat * _mean(dxhat * xhat))


_GELU_C = math.sqrt(2.0 / math.pi)


def _gelu(x):
    t = jnp.tanh(_GELU_C * (x + 0.044715 * (x * x * x)))
    return 0.5 * x * (1.0 + t)


def _gelu_and_grad(x):
    x2 = x * x
    t = jnp.tanh(_GELU_C * (x + 0.044715 * (x2 * x)))
    hx, ht = 0.5 * x, 0.5 * (1.0 + t)
    return x * ht, ht + hx * (1.0 - t * t) * (_GELU_C * (1.0 + 3.0 * 0.044715 * x2))


def _mean0(a):
    return jnp.mean(a, axis=0, keepdims=True)


def _ln_fwd_t(z, g, b):
    zc = z - _mean0(z)
    rstd = lax.rsqrt(_mean0(zc * zc) + LN_EPS)
    xhat = zc * rstd
    return xhat * g + b, xhat, rstd


def _ln_bwd_t(dy, xhat, rstd, g):
    dxhat = dy * g
    return rstd * (dxhat - _mean0(dxhat) - xhat * _mean0(dxhat * xhat))


def _rope_t(t, cos, sin_signed, bwd=False):
    half = HEAD_DIM // 2
    outs = []
    for r in range(0, t.shape[0], HEAD_DIM):
        th = t[r:r + HEAD_DIM]
        sw = jnp.concatenate([th[half:], th[:half]], axis=0) * sin_signed
        outs.append(th * cos - sw if bwd else th * cos + sw)
    return jnp.concatenate(outs, axis=0)


ANY = pl.BlockSpec(memory_space=pl.ANY)


def _place():
    return lax.axis_index("x"), lax.axis_index("y"), lax.axis_index("c")


class _Comm:
    def __init__(self, ins, outs, sems, start, finish):
        self.ins, self.outs, self.sems, self.start, self.finish = ins, outs, sems, start, finish


def _gather_comm(arrs):
    n = len(arrs)

    def parts(ins, outs, sems):
        send_sems, recv_sems, local_sems = sems
        x, y, c = _place()
        me, sibling = (x, y, c), (x, y, 1 - c)
        chips = [(1 - x, y), (x, 1 - y), (1 - x, 1 - y)]

        def copy(a, k, block, to, src=None):
            px, py, pc = block
            dst = outs[a].at[4 * px + 2 * py + pc]
            return pltpu.make_async_remote_copy(
                src_ref=dst if src is None else src, dst_ref=dst,
                send_sem=send_sems.at[a, k], recv_sem=recv_sems.at[a, k], device_id=to, device_id_type=MESH)

        mine = [pltpu.make_async_copy(ins[a], outs[a].at[4 * x + 2 * y + c], local_sems.at[a]) for a in range(n)]
        first = []
        for a in range(n):
            first.append(copy(a, 0, me, sibling, src=ins[a]))
            first += [copy(a, 1 + j, me, (*chip, c), src=ins[a]) for j, chip in enumerate(chips)]
        return copy, mine, first, me, sibling, chips, c

    def start(ins, outs, sems):
        _, mine, first, *_ = parts(ins, outs, sems)
        for cp in mine + first:
            cp.start()

    def finish(ins, outs, sems):
        copy, mine, first, me, sibling, chips, c = parts(ins, outs, sems)
        passed = []
        for j, chip in enumerate(chips):
            for a in range(n):
                copy(a, 1 + j, (*chip, c), me).wait_recv()
                fwd = copy(a, 4 + j, (*chip, c), sibling)
                fwd.start()
                passed.append(fwd)
        for a in range(n):
            copy(a, 0, sibling, me).wait_recv()
        for j, chip in enumerate(chips):
            for a in range(n):
                copy(a, 4 + j, (*chip, 1 - c), me).wait_recv()
        for cp in first + passed:
            cp.wait_send()
        for cp in mine:
            cp.wait()

    return _Comm(list(arrs), [jax.ShapeDtypeStruct((N_DEV,) + a.shape, a.dtype) for a in arrs],
                 [pltpu.SemaphoreType.DMA((n, 7)), pltpu.SemaphoreType.DMA((n, 7)), pltpu.SemaphoreType.DMA((n,))],
                 start, finish)


def _sibling_comm(parts):
    n = len(parts)

    def copies(ins, outs, sems):
        x, y, c = _place()
        return [pltpu.make_async_remote_copy(
            src_ref=ins[a].at[2 * q + (1 - c)], dst_ref=outs[a].at[q],
            send_sem=sems[0].at[a, q], recv_sem=sems[1].at[a, q],
            device_id=(x, y, 1 - c), device_id_type=MESH) for a in range(n) for q in range(4)]

    return _Comm(list(parts), [jax.ShapeDtypeStruct((4,) + p.shape[1:], p.dtype) for p in parts],
                 [pltpu.SemaphoreType.DMA((n, 4)), pltpu.SemaphoreType.DMA((n, 4))],
                 lambda *r: [cp.start() for cp in copies(*r)], lambda *r: [cp.wait() for cp in copies(*r)])


def _chips_comm(chip_parts, rows=None):
    n = len(chip_parts)
    r0, nr = (0, None) if rows is None else rows

    def copies(ins, outs, sems):
        x, y, c = _place()
        chips = [(1 - x, y), (x, 1 - y), (1 - x, 1 - y)]
        src = lambda a, q: ins[a].at[q] if rows is None else ins[a].at[q, pl.ds(r0, nr)]
        return [pltpu.make_async_remote_copy(
            src_ref=src(a, 2 * px + py), dst_ref=outs[a].at[k],
            send_sem=sems[0].at[a, k], recv_sem=sems[1].at[a, k],
            device_id=(px, py, c), device_id_type=MESH) for a in range(n) for k, (px, py) in enumerate(chips)]

    shape = lambda p: (3,) + p.shape[1:] if rows is None else (3, nr) + p.shape[2:]
    return _Comm(list(chip_parts), [jax.ShapeDtypeStruct(shape(p), p.dtype) for p in chip_parts],
                 [pltpu.SemaphoreType.DMA((n, 3)), pltpu.SemaphoreType.DMA((n, 3))],
                 lambda *r: [cp.start() for cp in copies(*r)], lambda *r: [cp.wait() for cp in copies(*r)])


def _carry(body, *, name, grid, in_specs, out_specs, out_shape, args, comms=(), scratch_shapes=(), prefetch=()):
    n_pre, n_in, n_out, n_scr = len(prefetch), len(in_specs), len(out_specs), len(scratch_shapes)
    c_ins = [a for cm in comms for a in cm.ins]
    c_outs = [s for cm in comms for s in cm.outs]
    c_sems = [s for cm in comms for s in cm.sems]

    def wrapped(*refs):
        pre, refs = refs[:n_pre], refs[n_pre:]
        ins, refs = refs[:n_in], refs[n_in:]
        cins, refs = refs[:len(c_ins)], refs[len(c_ins):]
        outs, refs = refs[:n_out], refs[n_out:]
        couts, refs = refs[:len(c_outs)], refs[len(c_outs):]
        scr, sems = refs[:n_scr], refs[n_scr:]
        groups, i0, o0, s0 = [], 0, 0, 0
        for cm in comms:
            groups.append((cm, cins[i0:i0 + len(cm.ins)], couts[o0:o0 + len(cm.outs)], sems[s0:s0 + len(cm.sems)]))
            i0, o0, s0 = i0 + len(cm.ins), o0 + len(cm.outs), s0 + len(cm.sems)
        first = pl.program_id(0) == 0
        last = pl.program_id(0) == grid[0] - 1
        for ax in range(1, len(grid)):
            first = first & (pl.program_id(ax) == 0)
            last = last & (pl.program_id(ax) == grid[ax] - 1)
        if comms:
            @pl.when(first)
            def _():
                for cm, ci, co, cs in groups:
                    cm.start(ci, co, cs)
        body(*pre, *ins, *outs, *scr)
        if comms:
            @pl.when(last)
            def _():
                for cm, ci, co, cs in groups:
                    cm.finish(ci, co, cs)

    grid_spec = pltpu.PrefetchScalarGridSpec(
        num_scalar_prefetch=n_pre, grid=grid,
        in_specs=list(in_specs) + [ANY] * len(c_ins), out_specs=list(out_specs) + [ANY] * len(c_outs),
        scratch_shapes=list(scratch_shapes) + c_sems)
    res = pl.pallas_call(
        wrapped, name=name, grid_spec=grid_spec, out_shape=list(out_shape) + c_outs,
        compiler_params=_params(*(["arbitrary"] * len(grid))),
    )(*prefetch, *args, *c_ins)
    outs, rest, per_comm = res[:n_out], res[n_out:], []
    for cm in comms:
        per_comm.append(rest[:len(cm.outs)])
        rest = rest[len(cm.outs):]
    return outs, per_comm


def _rope_tables(pos_row, inv_freq_col, comms=()):
    t_tok = pos_row.shape[1]
    tm = min(512, t_tok)

    def body(pos_ref, invf_ref, cos_ref, sin_ref):
        ang = pos_ref[...].astype(F32) * invf_ref[...]
        row = lax.broadcasted_iota(jnp.int32, ang.shape, 0)
        cos_ref[...] = jnp.cos(ang)
        sin_ref[...] = jnp.sin(ang) * jnp.where(row < HEAD_DIM // 2, -1.0, 1.0)

    return _carry(
        body, name="rope_tables", grid=(t_tok // tm,), comms=comms,
        in_specs=[pl.BlockSpec((1, tm), lambda i: (0, i)), pl.BlockSpec((HEAD_DIM, 1), lambda i: (0, 0))],
        out_specs=[pl.BlockSpec((HEAD_DIM, tm), lambda i: (0, i))] * 2,
        out_shape=[jax.ShapeDtypeStruct((HEAD_DIM, t_tok), F32)] * 2,
        args=(pos_row, inv_freq_col))


def _proj_in(x2, w_in_t, comms=()):
    t_tok, d = x2.shape
    d_in = w_in_t.shape[0]
    tm = min(512, t_tok)

    def body(x_ref, w_ref, h_ref, xb_ref):
        xb = x_ref[...].astype(BF16)
        xb_ref[...] = xb
        h_ref[...] = _dot(w_ref[...], xb, NT)

    return _carry(
        body, name="proj_in", grid=(t_tok // tm,), comms=comms,
        in_specs=[pl.BlockSpec((tm, d), lambda i: (i, 0)), pl.BlockSpec((d_in, d), lambda i: (0, 0))],
        out_specs=[pl.BlockSpec((d_in, tm), lambda i: (0, i)), pl.BlockSpec((tm, d), lambda i: (i, 0))],
        out_shape=[jax.ShapeDtypeStruct((d_in, t_tok), F32), jax.ShapeDtypeStruct((t_tok, d), BF16)],
        args=(x2, w_in_t))


MIX_BLOCKS = 2
MIX_W = MIX_BLOCKS * BLK


def _prev_block(i):
    return jnp.maximum(MIX_BLOCKS * i - 1, 0)


def _h_specs():
    kv_row = COL_K // (2 * D_KV)
    return [
        pl.BlockSpec((D_GMLP, MIX_W), lambda i: (0, i)),
        pl.BlockSpec((D_GMLP, MIX_W), lambda i: (1, i)),
        pl.BlockSpec((D_ATTN, MIX_W), lambda i: (2, i)),
        pl.BlockSpec((2 * D_KV, MIX_W), lambda i: (kv_row, i)),
        pl.BlockSpec((2 * D_KV, BLK), lambda i: (kv_row, _prev_block(i))),
    ]


def _table_specs():
    return [
        pl.BlockSpec((HEAD_DIM, MIX_W), lambda i: (0, i)),
        pl.BlockSpec((HEAD_DIM, MIX_W), lambda i: (0, i)),
        pl.BlockSpec((HEAD_DIM, BLK), lambda i: (0, _prev_block(i))),
        pl.BlockSpec((HEAD_DIM, BLK), lambda i: (0, _prev_block(i))),
    ]


def _cols(b):
    return slice(b * BLK, (b + 1) * BLK)


def _block_inputs(b, i, kvc, kvp_ref, cos, sin, cosp_ref, sinp_ref, bias_ref):
    if b == 0:
        kv_prev, cos_prev, sin_prev, bias = kvp_ref[...], cosp_ref[...], sinp_ref[...], bias_ref[jnp.minimum(i, 1)]
    else:
        kv_prev, cos_prev, sin_prev, bias = kvc[:, _cols(b - 1)], cos[:, _cols(b - 1)], sin[:, _cols(b - 1)], bias_ref[1]
    return kvc[:, _cols(b)], kv_prev, cos[:, _cols(b)], sin[:, _cols(b)], cos_prev, sin_prev, bias


def _band_bias():
    ki = lax.broadcasted_iota(jnp.int32, (2, 2 * BLK, BLK), 1)
    qi = lax.broadcasted_iota(jnp.int32, (2, 2 * BLK, BLK), 2)
    later = lax.broadcasted_iota(jnp.int32, (2, 2 * BLK, BLK), 0) > 0
    dist = qi + BLK - ki
    return jnp.where((dist >= 0) & (dist < BLK) & ((ki >= BLK) | later), 0.0, NEG_INF).astype(F32)


BIAS_SPEC = pl.BlockSpec((2, 2 * BLK, BLK), lambda i: (0, 0, 0))


def _keys_values(kvc, kvp, cosc, sinc, cosp, sinp):
    kp, kc = _rope_t(kvp[:D_KV], cosp, sinp), _rope_t(kvc[:D_KV], cosc, sinc)
    k_t = jnp.concatenate([kp, kc], axis=1).astype(BF16)
    k_n = jnp.concatenate([kp.T, kc.T], axis=0).astype(BF16)
    v_t = jnp.concatenate([kvp[D_KV:], kvc[D_KV:]], axis=1).astype(BF16)
    return k_t, k_n, v_t


def _pad_head(th, kv):
    z = jnp.zeros_like(th)
    return jnp.concatenate([th, z] if kv == 0 else [z, th], axis=0)


def _group_lanes(parts):
    return jnp.concatenate(parts, axis=1)


def _softmax_sink_t(s, sink):
    m = jnp.maximum(jnp.max(s, axis=0, keepdims=True), sink)
    e = jnp.exp(s - m)
    es = jnp.exp(sink - m)
    r = 1.0 / (jnp.sum(e, axis=0, keepdims=True) + es)
    return e * r, es * r


def _causal():
    row = lax.broadcasted_iota(jnp.int32, (BLK, BLK), 0)
    col = lax.broadcasted_iota(jnp.int32, (BLK, BLK), 1)
    return row >= col


def _mask_w_once(wsp_ref, wm_scr):
    @pl.when(pl.program_id(0) == 0)
    def _():
        causal = _causal()
        for hh in range(N_HEADS):
            wm_scr[hh] = jnp.where(causal, wsp_ref[hh], 0.0).astype(BF16)


def _mixer_fwd(h_t, cos_t, sin_t, w_spatial, b_spatial, vln_g, vln_b, sinks, band_bias, comms=()):
    t_tok = h_t.shape[1]
    group = N_HEADS // N_KV_HEADS

    def body(sinks_ref, u_ref, vg_ref, q_ref, kvc_ref, kvp_ref, cos_ref, sin_ref, cosp_ref, sinp_ref,
             wsp_ref, bsp_ref, g_ref, b_ref, bias_ref, cat_ref, wm_scr):
        i = pl.program_id(0)
        _mask_w_once(wsp_ref, wm_scr)
        ua = _gelu(u_ref[...])
        vp, _, _ = _ln_fwd_t(_gelu(vg_ref[...]), g_ref[...], b_ref[...])
        vpb = vp.astype(BF16)
        for b in range(MIX_BLOCKS):
            for hh in range(N_HEADS):
                rows = slice(hh * HEAD_DIM, (hh + 1) * HEAD_DIM)
                mixed = _dot(vpb[rows, _cols(b)], wm_scr[hh], NT) + bsp_ref[hh:hh + 1, :]
                cat_ref[rows, _cols(b)] = (ua[rows, _cols(b)] * mixed).astype(BF16)

        kvc, cos, sin = kvc_ref[...], cos_ref[...], sin_ref[...]
        qr = (_rope_t(q_ref[...], cos, sin) * SCORE_SCALE).astype(BF16)
        sinks4 = [_group_lanes([jnp.full((1, BLK), sinks_ref[hh], F32) for hh in range(kv * group, (kv + 1) * group)])
                  for kv in range(N_KV_HEADS)]
        for b in range(MIX_BLOCKS):
            kv_cur, kv_prev, cosc, sinc, cosp, sinp, bias1 = _block_inputs(b, i, kvc, kvp_ref, cos, sin, cosp_ref, sinp_ref, bias_ref)
            _, k_n, v_t = _keys_values(kv_cur, kv_prev, cosc, sinc, cosp, sinp)
            bias = _group_lanes([bias1] * group)
            for kv in range(N_KV_HEADS):
                heads = range(kv * group, (kv + 1) * group)
                qs = _group_lanes([qr[hh * HEAD_DIM:(hh + 1) * HEAD_DIM, _cols(b)] for hh in heads])
                p, _ = _softmax_sink_t(_dot(k_n, _pad_head(qs, kv)) + bias, sinks4[kv])
                o = _dot(v_t[kv * HEAD_DIM:(kv + 1) * HEAD_DIM], p.astype(BF16)).astype(BF16)
                for j, hh in enumerate(heads):
                    cat_ref[D_GMLP + hh * HEAD_DIM:D_GMLP + (hh + 1) * HEAD_DIM, _cols(b)] = o[:, j * BLK:(j + 1) * BLK]

    full = lambda shape: pl.BlockSpec(shape, lambda i: (0,) * len(shape))
    return _carry(
        body, name="mixer_fwd", grid=(t_tok // MIX_W,), comms=comms,
        in_specs=[pl.BlockSpec(memory_space=pltpu.SMEM)] + _h_specs() + _table_specs() + [
            full((N_HEADS, BLK, BLK)), full((N_HEADS, BLK)), full((D_GMLP, 1)), full((D_GMLP, 1)), BIAS_SPEC],
        out_specs=[pl.BlockSpec((D_GMLP + D_ATTN, MIX_W), lambda i: (0, i))],
        out_shape=[jax.ShapeDtypeStruct((D_GMLP + D_ATTN, t_tok), BF16)],
        scratch_shapes=[pltpu.VMEM((N_HEADS, BLK, BLK), BF16)],
        args=(sinks, h_t, h_t, h_t, h_t, h_t, cos_t, sin_t, cos_t, sin_t, w_spatial, b_spatial, vln_g, vln_b, band_bias))


def _proj_out(cat_t, x2, w_out_b, ln1_g, ln1_b, comms=()):
    t_tok, d = x2.shape
    tm = min(512, t_tok)

    def body(cat_ref, x_ref, w_ref, g_ref, b_ref, xhat_ref, rstd_ref, x1b_ref):
        x1, xhat, rstd = _ln_fwd(ALPHA * x_ref[...] + _dot(cat_ref[...], w_ref[...], TN), g_ref[...], b_ref[...])
        xhat_ref[...] = xhat
        rstd_ref[...] = rstd
        x1b_ref[...] = x1.astype(BF16)

    tok = lambda w: pl.BlockSpec((tm, w), lambda i: (i, 0))
    vec = pl.BlockSpec((1, d), lambda i: (0, 0))
    return _carry(
        body, name="proj_out", grid=(t_tok // tm,), comms=comms,
        in_specs=[pl.BlockSpec((cat_t.shape[0], tm), lambda i: (0, i)), tok(d), pl.BlockSpec(w_out_b.shape, lambda i: (0, 0)), vec, vec],
        out_specs=[tok(d), tok(1), tok(d)],
        out_shape=[jax.ShapeDtypeStruct((t_tok, d), F32), jax.ShapeDtypeStruct((t_tok, 1), F32), jax.ShapeDtypeStruct((t_tok, d), BF16)],
        args=(cat_t, x2, w_out_b, ln1_g, ln1_b))


def _ffn_fwd_bwd(xhat1, rstd1, x1b, target, w1_parts, w2_parts, ln1_g, ln1_b, ln2_g, ln2_b):
    t_tok, d = xhat1.shape
    n_part = len(w1_parts)
    n_chunk, _, fp = w1_parts[0].shape
    fc = n_part * fp
    f = n_chunk * fc
    tm = min(256, t_tok)

    def body(xhat1_ref, rstd1_ref, x1b_ref, tgt_ref, *refs):
        w1_hbm, w2_hbm = refs[:n_part], refs[n_part:2 * n_part]
        (g1_ref, b1_ref, g2_ref, b2_ref, act_ref, dpre_ref, dz2b_ref, dz1_ref, stats_ref,
         r_scr, w1_ref, w2_ref, w_sems) = refs[2 * n_part:]

        @pl.when(pl.program_id(0) == 0)
        def _():
            stats_ref[...] = jnp.zeros_like(stats_ref)
            loads = [pltpu.make_async_copy(w1_hbm[p], w1_ref.at[:, :, pl.ds(p * fp, fp)], w_sems.at[0, p]) for p in range(n_part)]
            loads += [pltpu.make_async_copy(w2_hbm[p], w2_ref.at[:, pl.ds(p * fp, fp), :], w_sems.at[1, p]) for p in range(n_part)]
            for cp in loads:
                cp.start()
            for cp in loads:
                cp.wait()

        g1, g2 = g1_ref[...], g2_ref[...]
        xhat1, x1b = xhat1_ref[...], x1b_ref[...]
        ff = jnp.zeros((tm, d), F32)
        for j in range(n_chunk):
            r = jnp.maximum(_dot(x1b, w1_ref[j]), 0.0)
            r_scr[:, j * fc:(j + 1) * fc] = r
            act = (r * r).astype(BF16)
            act_ref[:, j * fc:(j + 1) * fc] = act
            ff = ff + _dot(act, w2_ref[j])
        y, xhat2, rstd2 = _ln_fwd(ALPHA * (xhat1 * g1 + b1_ref[...]) + ff, g2, b2_ref[...])
        diff = y - tgt_ref[...]
        loss = 0.5 * jnp.sum(jnp.sum(diff * diff, axis=-1, keepdims=True) / d, axis=0, keepdims=True)
        dy = diff / d
        dz2 = _ln_bwd(dy, xhat2, rstd2, g2)
        dz2b = dz2.astype(BF16)
        dz2b_ref[...] = dz2b
        dx1 = ALPHA * dz2
        for j in range(n_chunk):
            dpre = (_dot(dz2b, w2_ref[j], NT) * (2.0 * r_scr[:, j * fc:(j + 1) * fc])).astype(BF16)
            dpre_ref[:, j * fc:(j + 1) * fc] = dpre
            dx1 = dx1 + _dot(dpre, w1_ref[j], NT)
        dz1_ref[...] = _ln_bwd(dx1, xhat1, rstd1_ref[...], g1)
        stats_ref[0:1, :] += jnp.sum(dx1 * xhat1, axis=0, keepdims=True)
        stats_ref[1:2, :] += jnp.sum(dx1, axis=0, keepdims=True)
        stats_ref[2:3, :] += jnp.sum(dy * xhat2, axis=0, keepdims=True)
        stats_ref[3:4, :] += jnp.sum(dy, axis=0, keepdims=True)
        stats_ref[4:5, :] += jnp.broadcast_to(loss, (1, d))

    tok = lambda w: pl.BlockSpec((tm, w), lambda i: (i, 0))
    vec = pl.BlockSpec((1, d), lambda i: (0, 0))
    return _carry(
        body, name="ffn_fwd_bwd", grid=(t_tok // tm,),
        in_specs=[tok(d), tok(1), tok(d), tok(d)] + [ANY] * (2 * n_part) + [vec, vec, vec, vec],
        out_specs=[tok(f), tok(f), tok(d), tok(d), pl.BlockSpec((8, d), lambda i: (0, 0))],
        out_shape=[jax.ShapeDtypeStruct((t_tok, f), BF16), jax.ShapeDtypeStruct((t_tok, f), BF16),
                   jax.ShapeDtypeStruct((t_tok, d), BF16), jax.ShapeDtypeStruct((t_tok, d), F32), jax.ShapeDtypeStruct((8, d), F32)],
        scratch_shapes=[pltpu.VMEM((tm, f), F32), pltpu.VMEM((n_chunk, d, fc), BF16), pltpu.VMEM((n_chunk, fc, d), BF16),
                        pltpu.SemaphoreType.DMA((2, n_part))],
        args=(xhat1, rstd1, x1b, target, *w1_parts, *w2_parts, ln1_g, ln1_b, ln2_g, ln2_b))[0]


def _ffn_wgrad(name, lhs, rhs, chunk_lhs, core_chip, comms=()):
    t_tok = lhs.shape[0]
    half = N_DEV // 2
    fc = (lhs if chunk_lhs else rhs).shape[1] // N_DEV
    chunk = (fc, rhs.shape[1]) if chunk_lhs else (lhs.shape[1], fc)

    def shard(s, cc):
        return 2 * (s % half) + jnp.where(s < half, 1 - cc[0], cc[0])

    def body(cc_ref, lhs_ref, rhs_ref, wire_ref, own_ref, recv_ref, send_buf, got, send_sems, recv_sems, got_sem):
        s = pl.program_id(0)
        x, y, c = _place()
        def send(q):
            return pltpu.make_async_remote_copy(
                src_ref=send_buf.at[q % 2], dst_ref=recv_ref.at[q], send_sem=send_sems.at[q], recv_sem=recv_sems.at[q],
                device_id=(x, y, 1 - c), device_id_type=MESH)

        def load(q):
            return pltpu.make_async_copy(recv_ref.at[q], got, got_sem.at[0])

        @pl.when(s >= half)
        def _():
            send(s - half).wait_recv()
            load(s - half).start()

        g = _dot(lhs_ref[...], rhs_ref[...], TN)

        for q in range(half):
            @pl.when(s == q)
            def _(q=q):
                if q >= 2:
                    send(q - 2).wait_send()
                send_buf[q % 2] = g
                send(q).start()

            @pl.when(s == half + q)
            def _(q=q):
                load(q).wait()
                total = g + got[...]
                wire_ref[...] = total.astype(BF16)

                @pl.when(cc_ref[1] == q)
                def _():
                    own_ref[...] = total

        @pl.when(s == N_DEV - 1)
        def _():
            for q in range(half - 2, half):
                send(q).wait_send()

    resident = lambda a: pl.BlockSpec(a.shape, lambda s, cc: (0, 0), pipeline_mode=pl.Buffered(1))
    chunked = pl.BlockSpec((t_tok, fc), lambda s, cc: (0, shard(s, cc)))
    (wire, own, _), per_comm = _carry(
        body, name=name, grid=(N_DEV,), comms=comms, prefetch=(core_chip,),
        in_specs=[chunked, resident(rhs)] if chunk_lhs else [resident(lhs), chunked],
        out_specs=[pl.BlockSpec((None,) + chunk, lambda s, cc: (jnp.maximum(s - half, 0), 0, 0)),
                   pl.BlockSpec(chunk, lambda s, cc: (0, 0)), ANY],
        out_shape=[jax.ShapeDtypeStruct((half,) + chunk, BF16), jax.ShapeDtypeStruct(chunk, F32),
                   jax.ShapeDtypeStruct((half,) + chunk, F32)],
        scratch_shapes=[pltpu.VMEM((2,) + chunk, F32), pltpu.VMEM(chunk, F32), pltpu.SemaphoreType.DMA((half,)),
                        pltpu.SemaphoreType.DMA((half,)), pltpu.SemaphoreType.DMA((1,))],
        args=(lhs, rhs))
    return wire, own, per_comm


def _proj_out_bwd(dz1, cat_t, w_out_b, comms=()):
    t_tok, d = dz1.shape
    d_mix = cat_t.shape[0]
    tm = min(512, t_tok)

    def body(dz1_ref, cat_ref, w_ref, dcat_ref, gw_ref):
        @pl.when(pl.program_id(0) == 0)
        def _():
            gw_ref[...] = jnp.zeros_like(gw_ref)

        dzb = dz1_ref[...].astype(BF16)
        dcat_ref[...] = _dot(w_ref[...], dzb, NT)
        gw_ref[...] += _dot(cat_ref[...], dzb)

    return _carry(
        body, name="proj_out_bwd", grid=(t_tok // tm,), comms=comms,
        in_specs=[pl.BlockSpec((tm, d), lambda i: (i, 0)), pl.BlockSpec((d_mix, tm), lambda i: (0, i)),
                  pl.BlockSpec((d_mix, d), lambda i: (0, 0))],
        out_specs=[pl.BlockSpec((d_mix, tm), lambda i: (0, i)), pl.BlockSpec((d_mix, d), lambda i: (0, 0))],
        out_shape=[jax.ShapeDtypeStruct((d_mix, t_tok), F32), jax.ShapeDtypeStruct((d_mix, d), F32)],
        args=(dz1, cat_t, w_out_b))


def _mixer_bwd(dcat_t, h_t, cos_t, sin_t, w_spatial, b_spatial, vln_g, vln_b, sinks, band_bias, comms=()):
    t_tok = h_t.shape[1]
    nb, n_step = t_tok // BLK, t_tok // MIX_W
    group = N_HEADS // N_KV_HEADS

    def body(sinks_ref, dcat_ref, u_ref, vg_ref, q_ref, kvc_ref, kvp_ref, cos_ref, sin_ref, cosp_ref, sinp_ref,
             wsp_ref, bsp_ref, g_ref, b_ref, bias_ref, dh_ref, dkvc_ref, dkvp_ref, gwsb_ref, gbsp_ref, gvln_ref, gsink_ref,
             dg_acc, db_acc, wm_scr, gws_ref):
        i = pl.program_id(0)

        @pl.when(i == 0)
        def _():
            gws_ref[...] = jnp.zeros_like(gws_ref)
            gbsp_ref[...] = jnp.zeros_like(gbsp_ref)
            gsink_ref[...] = jnp.zeros_like(gsink_ref)
            dg_acc[...] = jnp.zeros_like(dg_acc)
            db_acc[...] = jnp.zeros_like(db_acc)

        _mask_w_once(wsp_ref, wm_scr)

        g = g_ref[...]
        ua, ua_grad = _gelu_and_grad(u_ref[...])
        vv, vv_grad = _gelu_and_grad(vg_ref[...])
        vp, vhat, rstd = _ln_fwd_t(vv, g, b_ref[...])
        vpb = vp.astype(BF16)
        da = dcat_ref[0:D_GMLP, :]
        dmixed = da * ua
        dvp_blocks = []
        for b in range(MIX_BLOCKS):
            dvp_parts = []
            for hh in range(N_HEADS):
                rows = slice(hh * HEAD_DIM, (hh + 1) * HEAD_DIM)
                vpb_h = vpb[rows, _cols(b)]
                mixed = _dot(vpb_h, wm_scr[hh], NT) + bsp_ref[hh:hh + 1, :]
                dh_ref[COL_U + hh * HEAD_DIM:COL_U + (hh + 1) * HEAD_DIM, _cols(b)] = (
                    da[rows, _cols(b)] * mixed * ua_grad[rows, _cols(b)]).astype(BF16)
                dm = dmixed[rows, _cols(b)]
                dmb = dm.astype(BF16)
                gbsp_ref[hh:hh + 1, :] += jnp.sum(dm, axis=0, keepdims=True)
                gws_ref[hh] += _dot(dmb, vpb_h, TN)
                dvp_parts.append(_dot(dmb, wm_scr[hh]))
            dvp_blocks.append(jnp.concatenate(dvp_parts, axis=0))
        dvp = jnp.concatenate(dvp_blocks, axis=1)
        dgv, dbv = dvp * vhat, dvp
        for b in range(MIX_BLOCKS):
            dg_acc[...] += dgv[:, _cols(b)]
            db_acc[...] += dbv[:, _cols(b)]
        dh_ref[COL_V:COL_V + D_GMLP, :] = (_ln_bwd_t(dvp, vhat, rstd, g) * vv_grad).astype(BF16)

        kvc, cos, sin = kvc_ref[...], cos_ref[...], sin_ref[...]
        qr = (_rope_t(q_ref[...], cos, sin) * SCORE_SCALE).astype(BF16)
        sinks4 = [_group_lanes([jnp.full((1, BLK), sinks_ref[hh], F32) for hh in range(kv * group, (kv + 1) * group)])
                  for kv in range(N_KV_HEADS)]
        dq_blocks, dkv_cur, dkv_prev = [], [], []
        for b in range(MIX_BLOCKS):
            kv_cur, kv_prev, cosc, sinc, cosp, sinp, bias1 = _block_inputs(b, i, kvc, kvp_ref, cos, sin, cosp_ref, sinp_ref, bias_ref)
            k_t, k_n, v_t = _keys_values(kv_cur, kv_prev, cosc, sinc, cosp, sinp)
            v_n = jnp.concatenate([kv_prev[D_KV:].T, kv_cur[D_KV:].T], axis=0).astype(BF16)
            bias = _group_lanes([bias1] * group)
            dk, dv, dq_parts = [], [], []
            for kv in range(N_KV_HEADS):
                heads = range(kv * group, (kv + 1) * group)
                kv_rows = slice(kv * HEAD_DIM, (kv + 1) * HEAD_DIM)
                qs = _group_lanes([qr[hh * HEAD_DIM:(hh + 1) * HEAD_DIM, _cols(b)] for hh in heads])
                dos = _group_lanes([dcat_ref[D_GMLP + hh * HEAD_DIM:D_GMLP + (hh + 1) * HEAD_DIM, _cols(b)]
                                    for hh in heads]).astype(BF16)
                p, p_sink = _softmax_sink_t(_dot(k_n, _pad_head(qs, kv)) + bias, sinks4[kv])
                dp = _dot(v_n, _pad_head(dos, kv))
                delta = jnp.sum(p * dp, axis=0, keepdims=True)
                ds = (p * (dp - delta)).astype(BF16)
                dsink = p_sink * delta
                dq = _dot(k_t[kv_rows], ds) * SCORE_SCALE
                for j, hh in enumerate(heads):
                    gsink_ref[hh:hh + 1, :] -= dsink[:, j * BLK:(j + 1) * BLK]
                    dq_parts.append(dq[:, j * BLK:(j + 1) * BLK])
                dk.append(_dot(qs, ds, NT))
                dv.append(_dot(dos, p.astype(BF16), NT))
            dq_blocks.append(jnp.concatenate(dq_parts, axis=0))
            dk_all, dv_all = jnp.concatenate(dk, axis=0), jnp.concatenate(dv, axis=0)
            dkv_cur.append(jnp.concatenate([_rope_t(dk_all[:, BLK:], cosc, sinc, bwd=True), dv_all[:, BLK:]], axis=0))
            dkv_prev.append(jnp.concatenate([_rope_t(dk_all[:, :BLK], cosp, sinp, bwd=True), dv_all[:, :BLK]], axis=0))
        dh_ref[COL_Q:COL_Q + D_ATTN, :] = _rope_t(jnp.concatenate(dq_blocks, axis=1), cos, sin, bwd=True).astype(BF16)
        for b in range(MIX_BLOCKS):
            dkvc_ref[:, _cols(b)] = dkv_cur[b] + dkv_prev[b + 1] if b + 1 < MIX_BLOCKS else dkv_cur[b]
        dkvp_ref[...] = dkv_prev[0]

        @pl.when(i == n_step - 1)
        def _():
            causal = _causal()
            for hh in range(N_HEADS):
                gwsb_ref[hh] = jnp.where(causal, gws_ref[hh], 0.0).astype(BF16)
            gvln_ref[...] = jnp.zeros_like(gvln_ref)
            gvln_ref[0:1, :] = jnp.sum(dg_acc[...].T, axis=0, keepdims=True)
            gvln_ref[1:2, :] = jnp.sum(db_acc[...].T, axis=0, keepdims=True)

    full = lambda shape: pl.BlockSpec(shape, lambda i: (0,) * len(shape))
    return _carry(
        body, name="mixer_bwd", grid=(n_step,), comms=comms,
        in_specs=[pl.BlockSpec(memory_space=pltpu.SMEM), pl.BlockSpec((D_GMLP + D_ATTN, MIX_W), lambda i: (0, i))]
        + _h_specs() + _table_specs()
        + [full((N_HEADS, BLK, BLK)), full((N_HEADS, BLK)), full((D_GMLP, 1)), full((D_GMLP, 1)), BIAS_SPEC],
        out_specs=[pl.BlockSpec((COL_K, MIX_W), lambda i: (0, i)), pl.BlockSpec((2 * D_KV, MIX_W), lambda i: (0, i)),
                   pl.BlockSpec((2 * D_KV, BLK), lambda i: (0, (i + n_step - 1) % n_step)),
                   full((N_HEADS, BLK, BLK)), full((N_HEADS, BLK)), full((8, D_GMLP)), full((N_HEADS, LANES))],
        out_shape=[jax.ShapeDtypeStruct((COL_K, t_tok), BF16), jax.ShapeDtypeStruct((2 * D_KV, t_tok), F32),
                   jax.ShapeDtypeStruct((2 * D_KV, n_step * BLK), F32),
                   jax.ShapeDtypeStruct((N_HEADS, BLK, BLK), BF16), jax.ShapeDtypeStruct((N_HEADS, BLK), F32),
                   jax.ShapeDtypeStruct((8, D_GMLP), F32), jax.ShapeDtypeStruct((N_HEADS, LANES), F32)],
        scratch_shapes=[pltpu.VMEM((D_GMLP, BLK), F32), pltpu.VMEM((D_GMLP, BLK), F32), pltpu.VMEM((N_HEADS, BLK, BLK), BF16),
                        pltpu.VMEM((N_HEADS, BLK, BLK), F32)],
        args=(sinks, dcat_t, h_t, h_t, h_t, h_t, h_t, cos_t, sin_t, cos_t, sin_t, w_spatial, b_spatial, vln_g, vln_b, band_bias))


def _proj_in_wgrad(dh_b, dkvc_t, dkvp_t, xb, comms=()):
    t_tok, d = xb.shape
    d_main, d_kv = dh_b.shape[0], dkvc_t.shape[0]
    tm = min(1024, t_tok)

    def body(dh_ref, dkvc_ref, dkvp_ref, xb_ref, dkvb_ref, gw_ref):
        @pl.when(pl.program_id(0) == 0)
        def _():
            gw_ref[...] = jnp.zeros_like(gw_ref)

        for s in range(tm // MIX_W):
            last = slice((s + 1) * MIX_W - BLK, (s + 1) * MIX_W)
            dkvb_ref[:, s * MIX_W:(s + 1) * MIX_W - BLK] = dkvc_ref[:, s * MIX_W:(s + 1) * MIX_W - BLK].astype(BF16)
            dkvb_ref[:, last] = (dkvc_ref[:, last] + dkvp_ref[:, _cols(s)]).astype(BF16)
        gw_ref[0:d_main, :] += _dot(dh_ref[...], xb_ref[...])
        gw_ref[d_main:, :] += _dot(dkvb_ref[...], xb_ref[...])

    tok = lambda rows: pl.BlockSpec((rows, tm), lambda i: (0, i))
    return _carry(
        body, name="proj_in_wgrad", grid=(t_tok // tm,), comms=comms,
        in_specs=[tok(d_main), tok(d_kv), pl.BlockSpec((d_kv, tm // MIX_BLOCKS), lambda i: (0, i)),
                  pl.BlockSpec((tm, d), lambda i: (i, 0))],
        out_specs=[tok(d_kv), pl.BlockSpec((d_main + d_kv, d), lambda i: (0, 0))],
        out_shape=[jax.ShapeDtypeStruct((d_kv, t_tok), BF16), jax.ShapeDtypeStruct((d_main + d_kv, d), F32)],
        args=(dh_b, dkvc_t, dkvp_t, xb))


def _proj_in_dgrad(dh_b, dkv_b, dz1, w_in_t, comms=()):
    t_tok, d = dz1.shape
    d_main, d_kv = dh_b.shape[0], dkv_b.shape[0]
    tm = min(512, t_tok)

    def body(dh_ref, dkv_ref, dz1_ref, w_ref, dx_ref):
        dx_ref[...] = (ALPHA * dz1_ref[...] + _dot(dh_ref[...], w_ref[0:d_main, :], TN)
                       + _dot(dkv_ref[...], w_ref[d_main:, :], TN))

    return _carry(
        body, name="proj_in_dgrad", grid=(t_tok // tm,), comms=comms,
        in_specs=[pl.BlockSpec((d_main, tm), lambda i: (0, i)), pl.BlockSpec((d_kv, tm), lambda i: (0, i)),
                  pl.BlockSpec((tm, d), lambda i: (i, 0)), pl.BlockSpec((d_main + d_kv, d), lambda i: (0, 0))],
        out_specs=[pl.BlockSpec((tm, d), lambda i: (i, 0))],
        out_shape=[jax.ShapeDtypeStruct((t_tok, d), F32)],
        args=(dh_b, dkv_b, dz1, w_in_t))


def _adamw(w, g, m, v):
    m = ADAM_B1 * m + (1.0 - ADAM_B1) * g
    v = ADAM_B2 * v + (1.0 - ADAM_B2) * (g * g)
    m_hat = m / (1.0 - ADAM_B1 ** ADAM_STEP)
    v_hat = v / (1.0 - ADAM_B2 ** ADAM_STEP)
    delta = -ADAM_LR * (m_hat / (jnp.sqrt(v_hat) + ADAM_EPS) + ADAM_WD * w)
    return delta, m, v


def _row_tiled(name, own, recv, extra, n_out, finish, comms=()):
    r, c = own.shape
    recv = [] if recv is None else list(recv)
    k = max(len(recv), 1)
    n = max(k, -(-r // 512))
    tr, per = r // n, n // k
    blk = pl.BlockSpec((tr, c), lambda i: (i, 0))

    def body(own_ref, *refs):
        recv_refs, rest = refs[:len(recv)], refs[len(recv):]
        ins, outs = rest[:len(extra)], rest[len(extra):]

        def tile(recv_ref):
            g = own_ref[...]
            if recv_ref is not None:
                g = ((g + recv_ref[0].astype(F32)) + recv_ref[1].astype(F32)) + recv_ref[2].astype(F32)
            for o_ref, val in zip(outs, finish(g, *[a[...] for a in ins])):
                o_ref[...] = val

        if len(recv) <= 1:
            tile(recv_refs[0] if recv else None)
        else:
            for p in range(k):
                pl.when(pl.program_id(0) // per == p)(functools.partial(tile, recv_refs[p]))

    recv_specs = [pl.BlockSpec((3, tr, c), lambda i, p=p: (0, jnp.clip(i - p * per, 0, per - 1), 0)) for p in range(len(recv))]
    return _carry(
        body, name=name, grid=(n,), comms=comms,
        in_specs=[blk] + recv_specs + [blk] * len(extra),
        out_specs=[blk] * n_out, out_shape=[jax.ShapeDtypeStruct((r, c), F32)] * n_out,
        args=(own, *recv, *extra))


def _adamw_shard(name, own, recv, w, m, v, comms=()):
    def finish(g, w_t, m_t, v_t):
        return (g,) + _adamw(w_t, g, m_t, v_t)

    return _row_tiled(name, own, recv, (w, m, v), 4, finish, comms)


VEC_VLN, VEC_LN1G, VEC_LN1B, VEC_LN2G, VEC_LN2B, VEC_SINK, VEC_LOSS, VEC_BSP, VEC_ROWS = 0, 1, 2, 3, 4, 5, 6, 8, 16


def _adamw_small(parts_w, parts_vec, params):
    n = parts_w.shape[0]
    flat = [a for p in params for a in p]
    shapes = [p[0].shape for p in params]

    def grads(gw, gv):
        return [gw, gv[VEC_VLN:VEC_VLN + 1, 0:D_GMLP], gv[VEC_VLN:VEC_VLN + 1, D_GMLP:2 * D_GMLP],
                gv[VEC_BSP:VEC_BSP + N_HEADS, 0:BLK], gv[VEC_LN1G:VEC_LN1G + 1], gv[VEC_LN1B:VEC_LN1B + 1],
                gv[VEC_LN2G:VEC_LN2G + 1], gv[VEC_LN2B:VEC_LN2B + 1], gv[VEC_SINK:VEC_SINK + 1, 0:N_HEADS]]

    def body(pw_ref, pv_ref, *refs):
        ins, outs = refs[:len(flat)], refs[len(flat):]
        gw, gv = pw_ref[0].astype(F32), pv_ref[0]
        for k in range(1, n):
            gw, gv = gw + pw_ref[k].astype(F32), gv + pv_ref[k]
        for i, g in enumerate(grads(gw, gv)):
            w_ref, m_ref, v_ref = ins[3 * i:3 * i + 3]
            delta, m_new, v_new = _adamw(w_ref[...], g, m_ref[...], v_ref[...])
            for o_ref, val in zip(outs[4 * i:4 * i + 4], (g, delta, m_new, v_new)):
                o_ref[...] = val
        outs[-1][...] = gv[VEC_LOSS:VEC_LOSS + 1, 0:LANES]

    whole = lambda shape: pl.BlockSpec(shape, lambda i: (0,) * len(shape))
    res = _carry(
        body, name="adamw_small", grid=(1,),
        in_specs=[whole(parts_w.shape), whole(parts_vec.shape)] + [whole(a.shape) for a in flat],
        out_specs=[whole(s) for s in shapes for _ in range(4)] + [whole((1, LANES))],
        out_shape=[jax.ShapeDtypeStruct(s, F32) for s in shapes for _ in range(4)] + [jax.ShapeDtypeStruct((1, LANES), F32)],
        args=(parts_w, parts_vec, *flat))[0]
    return [res[4 * i:4 * i + 4] for i in range(len(params))], res[-1]


def _pair_sum(name, parts, recv, core_chip, comms=()):
    _, r, c = parts.shape
    tr = r if r <= 512 else 512

    def body(cc_ref, a_ref, b_ref, wire_ref, own_ref):
        s = a_ref[...] + b_ref[...]
        wire_ref[...] = s.astype(BF16)

        @pl.when(pl.program_id(1) == cc_ref[1])
        def _():
            own_ref[...] = s

    return _carry(
        body, name=name, grid=(r // tr, 4), prefetch=(core_chip,), comms=comms,
        in_specs=[pl.BlockSpec((None, tr, c), lambda i, q, cc: (2 * q + cc[0], i, 0)),
                  pl.BlockSpec((None, tr, c), lambda i, q, cc: (q, i, 0))],
        out_specs=[pl.BlockSpec((None, tr, c), lambda i, q, cc: (q, i, 0)), pl.BlockSpec((tr, c), lambda i, q, cc: (i, 0))],
        out_shape=[jax.ShapeDtypeStruct((4, r, c), BF16), jax.ShapeDtypeStruct((r, c), F32)],
        args=(parts, recv))


def kernel(x, positions, w_in, v_ln_g, v_ln_b, w_spatial, b_spatial, sinks, w_out, ln1_g, ln1_b, w_ff1, w_ff2, ln2_g, ln2_b, loss_target, m_w_in, m_v_ln_g, m_v_ln_b, m_w_spatial, m_b_spatial, m_sinks, m_w_out, m_ln1_g, m_ln1_b, m_w_ff1, m_w_ff2, m_ln2_g, m_ln2_b, v_w_in, v_v_ln_g, v_v_ln_b, v_w_spatial, v_b_spatial, v_sinks, v_w_out, v_ln1_g, v_ln1_b, v_w_ff1, v_w_ff2, v_ln2_g, v_ln2_b):
    _, t_tok, d = x.shape
    xi, yi, ci = _place()
    core_chip = jnp.stack([ci, 2 * xi + yi]).astype(jnp.int32)
    x2 = x.reshape(t_tok, d)
    target = loss_target.reshape(t_tok, d)
    inv_freq = ROPE_THETA ** (-jnp.arange(0, HEAD_DIM, 2, dtype=F32) / HEAD_DIM)
    wsp, bsp, sink_vec = w_spatial[0], b_spatial[0], sinks[0]
    vg_col, vb_col = v_ln_g.reshape(D_GMLP, 1), v_ln_b.reshape(D_GMLP, 1)
    big = {"in": w_in[0], "out": w_out[0], "ff1": w_ff1[0], "ff2": w_ff2[0]}
    half1, half2 = big["ff1"].shape[1] // 2, big["ff2"].shape[0] // 2
    w1_mine = [big["ff1"][:, :half1].astype(BF16), big["ff1"][:, half1:].astype(BF16)]
    w2_mine = [big["ff2"][:half2].astype(BF16), big["ff2"][half2:].astype(BF16)]

    (cos_t, sin_t), ((g_in,),) = _rope_tables(
        positions, jnp.tile(inv_freq, 2).reshape(HEAD_DIM, 1), comms=[_gather_comm([big["in"].T.astype(BF16)])])
    w_in_t = g_in.reshape(D_IN, d)
    (h_t, xb), ((g_out, w1_a),) = _proj_in(x2, w_in_t, comms=[_gather_comm([big["out"].astype(BF16), w1_mine[0]])])
    w_out_b = g_out.reshape(-1, d)
    band_bias = _band_bias()
    (cat_t,), ((w1_b, w2_a),) = _mixer_fwd(h_t, cos_t, sin_t, wsp, bsp, vg_col, vb_col, sink_vec, band_bias,
                                           comms=[_gather_comm([w1_mine[1], w2_mine[0]])])
    (xhat1, rstd1, x1b), ((w2_b,),) = _proj_out(cat_t, x2, w_out_b, ln1_g, ln1_b, comms=[_gather_comm([w2_mine[1]])])
    act_b, dpre_b, dz2b, dz1, stats = _ffn_fwd_bwd(xhat1, rstd1, x1b, target, [w1_a, w1_b], [w2_a, w2_b], ln1_g, ln1_b, ln2_g, ln2_b)

    (dcat_t, gw_out), _ = _proj_out_bwd(dz1, cat_t, w_out_b)
    p_out = gw_out.reshape(N_DEV, -1, d)
    wire_ff1, own_ff1, ((s_out,),) = _ffn_wgrad("ffn_wgrad1", x1b, dpre_b, False, core_chip, comms=[_sibling_comm([p_out])])
    (wire_out, own_out), _ = _pair_sum("pair_sum_out", p_out, s_out, core_chip)
    wire_ff2, own_ff2, ((r_ff1,),) = _ffn_wgrad("ffn_wgrad2", act_b, dz2b, True, core_chip, comms=[_chips_comm([wire_ff1])])
    (dh_b, dkvc_t, dkvp_t, g_wsp, g_bsp, g_vln, g_sink), ((r_ff2, r_out),) = _mixer_bwd(
        dcat_t, h_t, cos_t, sin_t, wsp, bsp, vg_col, vb_col, sink_vec, band_bias,
        comms=[_chips_comm([wire_ff2, wire_out])])
    sink_row = jnp.pad(g_sink.sum(axis=1).reshape(1, N_HEADS), ((0, 0), (0, d - N_HEADS)))
    small_vec = jnp.concatenate([g_vln[0:2].reshape(1, d), stats[0:4], sink_row, stats[4:5], jnp.zeros((1, d), F32),
                                 jnp.pad(g_bsp, ((0, 0), (0, d - BLK)))], axis=0)
    (dkv_b, gw_in_t), ((parts_w, parts_vec),) = _proj_in_wgrad(
        dh_b, dkvc_t, dkvp_t, xb, comms=[_gather_comm([g_wsp.reshape(-1, BLK), small_vec])])
    p_in = gw_in_t.reshape(N_DEV, -1, d)

    out_out, ((s_in,),) = _adamw_shard("adamw_out", own_out, [r_out], big["out"], m_w_out[0], v_w_out[0], comms=[_sibling_comm([p_in])])
    (wire_in, own_in), _ = _pair_sum("pair_sum_in", p_in, s_in, core_chip)
    (grad_x,), ((r_in,),) = _proj_in_dgrad(dh_b, dkv_b, dz1, w_in_t, comms=[_chips_comm([wire_in])])
    ff1_out, _ = _adamw_shard("adamw_ff1", own_ff1, [r_ff1], big["ff1"], m_w_ff1[0], v_w_ff1[0])
    ff2_out, _ = _adamw_shard("adamw_ff2", own_ff2, [r_ff2], big["ff2"], m_w_ff2[0], v_w_ff2[0])
    in_out_t, _ = _adamw_shard("adamw_in", own_in, [r_in], big["in"].T, m_w_in[0].T, v_w_in[0].T)
    in_out = [o.T for o in in_out_t]
    small = [(w_spatial, m_w_spatial, v_w_spatial), (v_ln_g, m_v_ln_g, v_v_ln_g), (v_ln_b, m_v_ln_b, v_v_ln_b),
             (b_spatial, m_b_spatial, v_b_spatial), (ln1_g, m_ln1_g, v_ln1_g), (ln1_b, m_ln1_b, v_ln1_b),
             (ln2_g, m_ln2_g, v_ln2_g), (ln2_b, m_ln2_b, v_ln2_b), (sinks, m_sinks, v_sinks)]
    views = [(-1, BLK), None, None, (N_HEADS, BLK)] + [None] * 5
    small_res, loss_row = _adamw_small(parts_w, parts_vec, [
        tuple(a if vw is None else a.reshape(vw) for a in p) for p, vw in zip(small, views)])
    small_out = [[o.reshape(p[0].shape) for o in res] for res, p in zip(small_res, small)]
    loss = loss_row[0, 0]

    big_out = {0: in_out, 6: out_out, 9: ff1_out, 10: ff2_out}
    small_slot = {3: 0, 1: 1, 2: 2, 4: 3, 7: 4, 8: 5, 11: 6, 12: 7, 5: 8}
    outs = [loss, grad_x.reshape(x.shape)]
    for kind in range(4):
        for wi in range(13):
            outs.append(big_out[wi][kind][None] if wi in big_out else small_out[small_slot[wi]][kind])
    return tuple(outs)
```

```python
import functools
import math

import jax
import jax.numpy as jnp
from jax import lax
from jax.experimental import pallas as pl
from jax.experimental.pallas import tpu as pltpu

F32 = jnp.float32
BF16 = jnp.bfloat16
MESH = pl.DeviceIdType.MESH

HEAD_DIM = 64
N_HEADS = 8
N_KV_HEADS = 2
BLK = 128
D_GMLP = N_HEADS * HEAD_DIM
D_ATTN = N_HEADS * HEAD_DIM
D_KV = N_KV_HEADS * HEAD_DIM
D_IN = 2 * D_GMLP + D_ATTN + 2 * D_KV
COL_U, COL_V, COL_Q, COL_K = 0, D_GMLP, 2 * D_GMLP, 2 * D_GMLP + D_ATTN
ROPE_THETA = 10000.0
LN_EPS = 1e-5
ALPHA = 2.0 ** 0.25
NEG_INF = -1e30
SCORE_SCALE = 1.0 / math.sqrt(HEAD_DIM)
ADAM_LR, ADAM_B1, ADAM_B2, ADAM_EPS, ADAM_WD, ADAM_STEP = 0.001, 0.9, 0.999, 1e-08, 0.01, 10
N_DEV = 8
LANES = 128
VMEM_LIMIT = 62 * 1024 * 1024
FFN_ROWS = 256
FFN_GROUPS = 2

NT = (((1,), (1,)), ((), ()))
TN = (((0,), (0,)), ((), ()))


def _params(*sem):
    return pltpu.CompilerParams(dimension_semantics=sem, vmem_limit_bytes=VMEM_LIMIT)


def _dot(a, b, dims=None):
    if dims is None:
        return jnp.dot(a, b, preferred_element_type=F32)
    return lax.dot_general(a, b, dims, preferred_element_type=F32)


def _mean(a):
    return jnp.mean(a, axis=-1, keepdims=True)


def _ln_fwd(z, g, b):
    zc = z - _mean(z)
    rstd = lax.rsqrt(_mean(zc * zc) + LN_EPS)
    xhat = zc * rstd
    return xhat * g + b, xhat, rstd


def _ln_bwd(dy, xhat, rstd, g):
    dxhat = dy * g
    return rstd * (dxhat - _mean(dxhat) - xhat * _mean(dxhat * xhat))


_GELU_C = math.sqrt(2.0 / math.pi)


def _gelu(x):
    t = jnp.tanh(_GELU_C * (x + 0.044715 * (x * x * x)))
    return 0.5 * x * (1.0 + t)


def _gelu_and_grad(x):
    x2 = x * x
    t = jnp.tanh(_GELU_C * (x + 0.044715 * (x2 * x)))
    hx, ht = 0.5 * x, 0.5 * (1.0 + t)
    return x * ht, ht + hx * (1.0 - t * t) * (_GELU_C * (1.0 + 3.0 * 0.044715 * x2))


def _mean0(a):
    return jnp.mean(a, axis=0, keepdims=True)


def _ln_fwd_t(z, g, b):
    zc = z - _mean0(z)
    rstd = lax.rsqrt(_mean0(zc * zc) + LN_EPS)
    xhat = zc * rstd
    return xhat * g + b, xhat, rstd


def _ln_bwd_t(dy, xhat, rstd, g):
    dxhat = dy * g
    return rstd * (dxhat - _mean0(dxhat) - xhat * _mean0(dxhat * xhat))


def _rope_t(t, cos, sin_signed, bwd=False):
    half = HEAD_DIM // 2
    outs = []
    for r in range(0, t.shape[0], HEAD_DIM):
        th = t[r:r + HEAD_DIM]
        sw = jnp.concatenate([th[half:], th[:half]], axis=0) * sin_signed
        outs.append(th * cos - sw if bwd else th * cos + sw)
    return jnp.concatenate(outs, axis=0)


ANY = pl.BlockSpec(memory_space=pl.ANY)


def _place():
    return lax.axis_index("x"), lax.axis_index("y"), lax.axis_index("c")


class _Comm:
    def __init__(self, ins, outs, sems, start, finish):
        self.ins, self.outs, self.sems, self.start, self.finish = ins, outs, sems, start, finish


def _gather_comm(arrs):
    n = len(arrs)

    def parts(ins, outs, sems):
        send_sems, recv_sems, local_sems = sems
        x, y, c = _place()
        me, sibling = (x, y, c), (x, y, 1 - c)
        chips = [(1 - x, y), (x, 1 - y), (1 - x, 1 - y)]

        def copy(a, k, block, to, src=None):
            px, py, pc = block
            dst = outs[a].at[4 * px + 2 * py + pc]
            return pltpu.make_async_remote_copy(
                src_ref=dst if src is None else src, dst_ref=dst,
                send_sem=send_sems.at[a, k], recv_sem=recv_sems.at[a, k], device_id=to, device_id_type=MESH)

        mine = [pltpu.make_async_copy(ins[a], outs[a].at[4 * x + 2 * y + c], local_sems.at[a]) for a in range(n)]
        first = []
        for a in range(n):
            first.append(copy(a, 0, me, sibling, src=ins[a]))
            first += [copy(a, 1 + j, me, (*chip, c), src=ins[a]) for j, chip in enumerate(chips)]
        return copy, mine, first, me, sibling, chips, c

    def start(ins, outs, sems):
        _, mine, first, *_ = parts(ins, outs, sems)
        for cp in mine + first:
            cp.start()

    def finish(ins, outs, sems):
        copy, mine, first, me, sibling, chips, c = parts(ins, outs, sems)
        passed = []
        for j, chip in enumerate(chips):
            for a in range(n):
                copy(a, 1 + j, (*chip, c), me).wait_recv()
                fwd = copy(a, 4 + j, (*chip, c), sibling)
                fwd.start()
                passed.append(fwd)
        for a in range(n):
            copy(a, 0, sibling, me).wait_recv()
        for j, chip in enumerate(chips):
            for a in range(n):
                copy(a, 4 + j, (*chip, 1 - c), me).wait_recv()
        for cp in first + passed:
            cp.wait_send()
        for cp in mine:
            cp.wait()

    return _Comm(list(arrs), [jax.ShapeDtypeStruct((N_DEV,) + a.shape, a.dtype) for a in arrs],
                 [pltpu.SemaphoreType.DMA((n, 7)), pltpu.SemaphoreType.DMA((n, 7)), pltpu.SemaphoreType.DMA((n,))],
                 start, finish)


def _sibling_comm(parts):
    n = len(parts)

    def copies(ins, outs, sems):
        x, y, c = _place()
        return [pltpu.make_async_remote_copy(
            src_ref=ins[a].at[2 * q + (1 - c)], dst_ref=outs[a].at[q],
            send_sem=sems[0].at[a, q], recv_sem=sems[1].at[a, q],
            device_id=(x, y, 1 - c), device_id_type=MESH) for a in range(n) for q in range(4)]

    return _Comm(list(parts), [jax.ShapeDtypeStruct((4,) + p.shape[1:], p.dtype) for p in parts],
                 [pltpu.SemaphoreType.DMA((n, 4)), pltpu.SemaphoreType.DMA((n, 4))],
                 lambda *r: [cp.start() for cp in copies(*r)], lambda *r: [cp.wait() for cp in copies(*r)])


def _chips_comm(chip_parts, rows=None):
    n = len(chip_parts)
    r0, nr = (0, None) if rows is None else rows

    def copies(ins, outs, sems):
        x, y, c = _place()
        chips = [(1 - x, y), (x, 1 - y), (1 - x, 1 - y)]
        src = lambda a, q: ins[a].at[q] if rows is None else ins[a].at[q, pl.ds(r0, nr)]
        return [pltpu.make_async_remote_copy(
            src_ref=src(a, 2 * px + py), dst_ref=outs[a].at[k],
            send_sem=sems[0].at[a, k], recv_sem=sems[1].at[a, k],
            device_id=(px, py, c), device_id_type=MESH) for a in range(n) for k, (px, py) in enumerate(chips)]

    shape = lambda p: (3,) + p.shape[1:] if rows is None else (3, nr) + p.shape[2:]
    return _Comm(list(chip_parts), [jax.ShapeDtypeStruct(shape(p), p.dtype) for p in chip_parts],
                 [pltpu.SemaphoreType.DMA((n, 3)), pltpu.SemaphoreType.DMA((n, 3))],
                 lambda *r: [cp.start() for cp in copies(*r)], lambda *r: [cp.wait() for cp in copies(*r)])


def _carry(body, *, name, grid, in_specs, out_specs, out_shape, args, comms=(), scratch_shapes=(), prefetch=()):
    n_pre, n_in, n_out, n_scr = len(prefetch), len(in_specs), len(out_specs), len(scratch_shapes)
    c_ins = [a for cm in comms for a in cm.ins]
    c_outs = [s for cm in comms for s in cm.outs]
    c_sems = [s for cm in comms for s in cm.sems]

    def wrapped(*refs):
        pre, refs = refs[:n_pre], refs[n_pre:]
        ins, refs = refs[:n_in], refs[n_in:]
        cins, refs = refs[:len(c_ins)], refs[len(c_ins):]
        outs, refs = refs[:n_out], refs[n_out:]
        couts, refs = refs[:len(c_outs)], refs[len(c_outs):]
        scr, sems = refs[:n_scr], refs[n_scr:]
        groups, i0, o0, s0 = [], 0, 0, 0
        for cm in comms:
            groups.append((cm, cins[i0:i0 + len(cm.ins)], couts[o0:o0 + len(cm.outs)], sems[s0:s0 + len(cm.sems)]))
            i0, o0, s0 = i0 + len(cm.ins), o0 + len(cm.outs), s0 + len(cm.sems)
        first = pl.program_id(0) == 0
        last = pl.program_id(0) == grid[0] - 1
        for ax in range(1, len(grid)):
            first = first & (pl.program_id(ax) == 0)
            last = last & (pl.program_id(ax) == grid[ax] - 1)
        if comms:
            @pl.when(first)
            def _():
                for cm, ci, co, cs in groups:
                    cm.start(ci, co, cs)
        body(*pre, *ins, *outs, *scr)
        if comms:
            @pl.when(last)
            def _():
                for cm, ci, co, cs in groups:
                    cm.finish(ci, co, cs)

    grid_spec = pltpu.PrefetchScalarGridSpec(
        num_scalar_prefetch=n_pre, grid=grid,
        in_specs=list(in_specs) + [ANY] * len(c_ins), out_specs=list(out_specs) + [ANY] * len(c_outs),
        scratch_shapes=list(scratch_shapes) + c_sems)
    res = pl.pallas_call(
        wrapped, name=name, grid_spec=grid_spec, out_shape=list(out_shape) + c_outs,
        compiler_params=_params(*(["arbitrary"] * len(grid))),
    )(*prefetch, *args, *c_ins)
    outs, rest, per_comm = res[:n_out], res[n_out:], []
    for cm in comms:
        per_comm.append(rest[:len(cm.outs)])
        rest = rest[len(cm.outs):]
    return outs, per_comm


def _rope_tables(pos_row, inv_freq_col, comms=()):
    t_tok = pos_row.shape[1]
    tm = min(512, t_tok)

    def body(pos_ref, invf_ref, cos_ref, sin_ref):
        ang = pos_ref[...].astype(F32) * invf_ref[...]
        row = lax.broadcasted_iota(jnp.int32, ang.shape, 0)
        cos_ref[...] = jnp.cos(ang)
        sin_ref[...] = jnp.sin(ang) * jnp.where(row < HEAD_DIM // 2, -1.0, 1.0)

    return _carry(
        body, name="rope_tables", grid=(t_tok // tm,), comms=comms,
        in_specs=[pl.BlockSpec((1, tm), lambda i: (0, i)), pl.BlockSpec((HEAD_DIM, 1), lambda i: (0, 0))],
        out_specs=[pl.BlockSpec((HEAD_DIM, tm), lambda i: (0, i))] * 2,
        out_shape=[jax.ShapeDtypeStruct((HEAD_DIM, t_tok), F32)] * 2,
        args=(pos_row, inv_freq_col))


def _proj_in(x2, w_in_t, comms=()):
    t_tok, d = x2.shape
    d_in = w_in_t.shape[0]
    tm = min(512, t_tok)

    def body(x_ref, w_ref, h_ref, xb_ref):
        xb = x_ref[...].astype(BF16)
        xb_ref[...] = xb
        h_ref[...] = _dot(w_ref[...], xb, NT)

    return _carry(
        body, name="proj_in", grid=(t_tok // tm,), comms=comms,
        in_specs=[pl.BlockSpec((tm, d), lambda i: (i, 0)), pl.BlockSpec((d_in, d), lambda i: (0, 0))],
        out_specs=[pl.BlockSpec((d_in, tm), lambda i: (0, i)), pl.BlockSpec((tm, d), lambda i: (i, 0))],
        out_shape=[jax.ShapeDtypeStruct((d_in, t_tok), F32), jax.ShapeDtypeStruct((t_tok, d), BF16)],
        args=(x2, w_in_t))


MIX_BLOCKS = 2
MIX_W = MIX_BLOCKS * BLK


def _prev_block(i):
    return jnp.maximum(MIX_BLOCKS * i - 1, 0)


def _h_specs():
    kv_row = COL_K // (2 * D_KV)
    return [
        pl.BlockSpec((D_GMLP, MIX_W), lambda i: (0, i)),
        pl.BlockSpec((D_GMLP, MIX_W), lambda i: (1, i)),
        pl.BlockSpec((D_ATTN, MIX_W), lambda i: (2, i)),
        pl.BlockSpec((2 * D_KV, MIX_W), lambda i: (kv_row, i)),
        pl.BlockSpec((2 * D_KV, BLK), lambda i: (kv_row, _prev_block(i))),
    ]


def _table_specs():
    return [
        pl.BlockSpec((HEAD_DIM, MIX_W), lambda i: (0, i)),
        pl.BlockSpec((HEAD_DIM, MIX_W), lambda i: (0, i)),
        pl.BlockSpec((HEAD_DIM, BLK), lambda i: (0, _prev_block(i))),
        pl.BlockSpec((HEAD_DIM, BLK), lambda i: (0, _prev_block(i))),
    ]


def _cols(b):
    return slice(b * BLK, (b + 1) * BLK)


def _block_inputs(b, i, kvc, kvp_ref, cos, sin, cosp_ref, sinp_ref, bias_ref):
    if b == 0:
        kv_prev, cos_prev, sin_prev, bias = kvp_ref[...], cosp_ref[...], sinp_ref[...], bias_ref[jnp.minimum(i, 1)]
    else:
        kv_prev, cos_prev, sin_prev, bias = kvc[:, _cols(b - 1)], cos[:, _cols(b - 1)], sin[:, _cols(b - 1)], bias_ref[1]
    return kvc[:, _cols(b)], kv_prev, cos[:, _cols(b)], sin[:, _cols(b)], cos_prev, sin_prev, bias


def _band_bias():
    ki = lax.broadcasted_iota(jnp.int32, (2, 2 * BLK, BLK), 1)
    qi = lax.broadcasted_iota(jnp.int32, (2, 2 * BLK, BLK), 2)
    later = lax.broadcasted_iota(jnp.int32, (2, 2 * BLK, BLK), 0) > 0
    dist = qi + BLK - ki
    return jnp.where((dist >= 0) & (dist < BLK) & ((ki >= BLK) | later), 0.0, NEG_INF).astype(F32)


BIAS_SPEC = pl.BlockSpec((2, 2 * BLK, BLK), lambda i: (0, 0, 0))


def _keys_values(kvc, kvp, cosc, sinc, cosp, sinp):
    kp, kc = _rope_t(kvp[:D_KV], cosp, sinp), _rope_t(kvc[:D_KV], cosc, sinc)
    k_t = jnp.concatenate([kp, kc], axis=1).astype(BF16)
    k_n = jnp.concatenate([kp.T, kc.T], axis=0).astype(BF16)
    v_t = jnp.concatenate([kvp[D_KV:], kvc[D_KV:]], axis=1).astype(BF16)
    return k_t, k_n, v_t


def _pad_head(th, kv):
    z = jnp.zeros_like(th)
    return jnp.concatenate([th, z] if kv == 0 else [z, th], axis=0)


def _group_lanes(parts):
    return jnp.concatenate(parts, axis=1)


def _softmax_sink_t(s, sink):
    m = jnp.maximum(jnp.max(s, axis=0, keepdims=True), sink)
    e = jnp.exp(s - m)
    es = jnp.exp(sink - m)
    r = 1.0 / (jnp.sum(e, axis=0, keepdims=True) + es)
    return e * r, es * r


def _causal():
    row = lax.broadcasted_iota(jnp.int32, (BLK, BLK), 0)
    col = lax.broadcasted_iota(jnp.int32, (BLK, BLK), 1)
    return row >= col


def _mask_w_once(wsp_ref, wm_scr):
    @pl.when(pl.program_id(0) == 0)
    def _():
        causal = _causal()
        for hh in range(N_HEADS):
            wm_scr[hh] = jnp.where(causal, wsp_ref[hh], 0.0).astype(BF16)


def _mixer_fwd(h_t, cos_t, sin_t, w_spatial, b_spatial, vln_g, vln_b, sinks, band_bias, comms=()):
    t_tok = h_t.shape[1]
    group = N_HEADS // N_KV_HEADS

    def body(sinks_ref, u_ref, vg_ref, q_ref, kvc_ref, kvp_ref, cos_ref, sin_ref, cosp_ref, sinp_ref,
             wsp_ref, bsp_ref, g_ref, b_ref, bias_ref, cat_ref, wm_scr):
        i = pl.program_id(0)
        _mask_w_once(wsp_ref, wm_scr)
        ua = _gelu(u_ref[...])
        vp, _, _ = _ln_fwd_t(_gelu(vg_ref[...]), g_ref[...], b_ref[...])
        vpb = vp.astype(BF16)
        for b in range(MIX_BLOCKS):
            for hh in range(N_HEADS):
                rows = slice(hh * HEAD_DIM, (hh + 1) * HEAD_DIM)
                mixed = _dot(vpb[rows, _cols(b)], wm_scr[hh], NT) + bsp_ref[hh:hh + 1, :]
                cat_ref[rows, _cols(b)] = (ua[rows, _cols(b)] * mixed).astype(BF16)

        kvc, cos, sin = kvc_ref[...], cos_ref[...], sin_ref[...]
        qr = (_rope_t(q_ref[...], cos, sin) * SCORE_SCALE).astype(BF16)
        sinks4 = [_group_lanes([jnp.full((1, BLK), sinks_ref[hh], F32) for hh in range(kv * group, (kv + 1) * group)])
                  for kv in range(N_KV_HEADS)]
        for b in range(MIX_BLOCKS):
            kv_cur, kv_prev, cosc, sinc, cosp, sinp, bias1 = _block_inputs(b, i, kvc, kvp_ref, cos, sin, cosp_ref, sinp_ref, bias_ref)
            _, k_n, v_t = _keys_values(kv_cur, kv_prev, cosc, sinc, cosp, sinp)
            bias = _group_lanes([bias1] * group)
            for kv in range(N_KV_HEADS):
                heads = range(kv * group, (kv + 1) * group)
                qs = _group_lanes([qr[hh * HEAD_DIM:(hh + 1) * HEAD_DIM, _cols(b)] for hh in heads])
                p, _ = _softmax_sink_t(_dot(k_n, _pad_head(qs, kv)) + bias, sinks4[kv])
                o = _dot(v_t[kv * HEAD_DIM:(kv + 1) * HEAD_DIM], p.astype(BF16)).astype(BF16)
                for j, hh in enumerate(heads):
                    cat_ref[D_GMLP + hh * HEAD_DIM:D_GMLP + (hh + 1) * HEAD_DIM, _cols(b)] = o[:, j * BLK:(j + 1) * BLK]

    full = lambda shape: pl.BlockSpec(shape, lambda i: (0,) * len(shape))
    return _carry(
        body, name="mixer_fwd", grid=(t_tok // MIX_W,), comms=comms,
        in_specs=[pl.BlockSpec(memory_space=pltpu.SMEM)] + _h_specs() + _table_specs() + [
            full((N_HEADS, BLK, BLK)), full((N_HEADS, BLK)), full((D_GMLP, 1)), full((D_GMLP, 1)), BIAS_SPEC],
        out_specs=[pl.BlockSpec((D_GMLP + D_ATTN, MIX_W), lambda i: (0, i))],
        out_shape=[jax.ShapeDtypeStruct((D_GMLP + D_ATTN, t_tok), BF16)],
        scratch_shapes=[pltpu.VMEM((N_HEADS, BLK, BLK), BF16)],
        args=(sinks, h_t, h_t, h_t, h_t, h_t, cos_t, sin_t, cos_t, sin_t, w_spatial, b_spatial, vln_g, vln_b, band_bias))


def _proj_out(cat_t, x2, w_out_b, ln1_g, ln1_b, comms=()):
    t_tok, d = x2.shape
    tm = min(512, t_tok)

    def body(cat_ref, x_ref, w_ref, g_ref, b_ref, xhat_ref, rstd_ref, x1b_ref):
        x1, xhat, rstd = _ln_fwd(ALPHA * x_ref[...] + _dot(cat_ref[...], w_ref[...], TN), g_ref[...], b_ref[...])
        xhat_ref[...] = xhat
        rstd_ref[...] = rstd
        x1b_ref[...] = x1.astype(BF16)

    tok = lambda w: pl.BlockSpec((tm, w), lambda i: (i, 0))
    vec = pl.BlockSpec((1, d), lambda i: (0, 0))
    return _carry(
        body, name="proj_out", grid=(t_tok // tm,), comms=comms,
        in_specs=[pl.BlockSpec((cat_t.shape[0], tm), lambda i: (0, i)), tok(d), pl.BlockSpec(w_out_b.shape, lambda i: (0, 0)), vec, vec],
        out_specs=[tok(d), tok(1), tok(d)],
        out_shape=[jax.ShapeDtypeStruct((t_tok, d), F32), jax.ShapeDtypeStruct((t_tok, 1), F32), jax.ShapeDtypeStruct((t_tok, d), BF16)],
        args=(cat_t, x2, w_out_b, ln1_g, ln1_b))


def _ffn_fwd_bwd(xhat1, rstd1, x1b, target, w1_parts, w2_parts, ln1_g, ln1_b, ln2_g, ln2_b):
    t_tok, d = xhat1.shape
    n_part = len(w1_parts)
    n_chunk, _, fp = w1_parts[0].shape
    fc = n_part * fp
    f = n_chunk * fc
    tm = min(FFN_GROUPS * FFN_ROWS, t_tok)

    def body(xhat1_ref, rstd1_ref, x1b_ref, tgt_ref, *refs):
        w1_hbm, w2_hbm = refs[:n_part], refs[n_part:2 * n_part]
        (g1_ref, b1_ref, g2_ref, b2_ref, act_ref, dpre_ref, dz2b_ref, dz1_ref, stats_ref,
         r_scr, w1_ref, w2_ref, w_sems) = refs[2 * n_part:]

        @pl.when(pl.program_id(0) == 0)
        def _():
            stats_ref[...] = jnp.zeros_like(stats_ref)
            loads = [pltpu.make_async_copy(w1_hbm[p], w1_ref.at[:, :, pl.ds(p * fp, fp)], w_sems.at[0, p]) for p in range(n_part)]
            loads += [pltpu.make_async_copy(w2_hbm[p], w2_ref.at[:, pl.ds(p * fp, fp), :], w_sems.at[1, p]) for p in range(n_part)]
            for cp in loads:
                cp.start()
            for cp in loads:
                cp.wait()

        g1, g2 = g1_ref[...], g2_ref[...]
        groups = [slice(s * FFN_ROWS, (s + 1) * FFN_ROWS) for s in range(tm // FFN_ROWS)]
        st = [{"rows": rows} for rows in groups]
        stats = [jnp.zeros((1, d), F32) for _ in range(5)]

        def fwd(s):
            rows = s["rows"]
            s["xhat1"], s["x1b"], s["ff"] = xhat1_ref[rows, :], x1b_ref[rows, :], jnp.zeros((FFN_ROWS, d), F32)
            for j in range(n_chunk):
                r = jnp.maximum(_dot(s["x1b"], w1_ref[j]), 0.0)
                r_scr[rows, j * fc:(j + 1) * fc] = r
                act = (r * r).astype(BF16)
                act_ref[rows, j * fc:(j + 1) * fc] = act
                s["ff"] = s["ff"] + _dot(act, w2_ref[j])

        def mid(s):
            rows = s["rows"]
            y, s["xhat2"], rstd2 = _ln_fwd(ALPHA * (s["xhat1"] * g1 + b1_ref[...]) + s["ff"], g2, b2_ref[...])
            diff = y - tgt_ref[rows, :]
            loss = 0.5 * jnp.sum(jnp.sum(diff * diff, axis=-1, keepdims=True) / d, axis=0, keepdims=True)
            s["dy"] = diff * (1.0 / d)
            dz2 = _ln_bwd(s["dy"], s["xhat2"], rstd2, g2)
            s["dz2b"] = dz2.astype(BF16)
            dz2b_ref[rows, :] = s["dz2b"]
            s["dx1"] = ALPHA * dz2
            stats[4] = stats[4] + jnp.broadcast_to(loss, (1, d))

        def bwd(s):
            rows = s["rows"]
            for j in range(n_chunk):
                dpre = (_dot(s["dz2b"], w2_ref[j], NT) * (2.0 * r_scr[rows, j * fc:(j + 1) * fc])).astype(BF16)
                dpre_ref[rows, j * fc:(j + 1) * fc] = dpre
                s["dx1"] = s["dx1"] + _dot(dpre, w1_ref[j], NT)

        def end(s):
            dz1_ref[s["rows"], :] = _ln_bwd(s["dx1"], s["xhat1"], rstd1_ref[s["rows"], :], g1)
            stats[0] = stats[0] + jnp.sum(s["dx1"] * s["xhat1"], axis=0, keepdims=True)
            stats[1] = stats[1] + jnp.sum(s["dx1"], axis=0, keepdims=True)
            stats[2] = stats[2] + jnp.sum(s["dy"] * s["xhat2"], axis=0, keepdims=True)
            stats[3] = stats[3] + jnp.sum(s["dy"], axis=0, keepdims=True)

        for phase in (fwd, mid, bwd, end):
            for s in st:
                phase(s)
        for k in range(5):
            stats_ref[k:k + 1, :] += stats[k]

    tok = lambda w: pl.BlockSpec((tm, w), lambda i: (i, 0))
    vec = pl.BlockSpec((1, d), lambda i: (0, 0))
    return _carry(
        body, name="ffn_fwd_bwd", grid=(t_tok // tm,),
        in_specs=[tok(d), tok(1), tok(d), tok(d)] + [ANY] * (2 * n_part) + [vec, vec, vec, vec],
        out_specs=[tok(f), tok(f), tok(d), tok(d), pl.BlockSpec((8, d), lambda i: (0, 0))],
        out_shape=[jax.ShapeDtypeStruct((t_tok, f), BF16), jax.ShapeDtypeStruct((t_tok, f), BF16),
                   jax.ShapeDtypeStruct((t_tok, d), BF16), jax.ShapeDtypeStruct((t_tok, d), F32), jax.ShapeDtypeStruct((8, d), F32)],
        scratch_shapes=[pltpu.VMEM((tm, f), F32), pltpu.VMEM((n_chunk, d, fc), BF16), pltpu.VMEM((n_chunk, fc, d), BF16),
                        pltpu.SemaphoreType.DMA((2, n_part))],
        args=(xhat1, rstd1, x1b, target, *w1_parts, *w2_parts, ln1_g, ln1_b, ln2_g, ln2_b))[0]


def _ffn_wgrad(name, lhs, rhs, chunk_lhs, core_chip, comms=()):
    t_tok = lhs.shape[0]
    half = N_DEV // 2
    fc = (lhs if chunk_lhs else rhs).shape[1] // N_DEV
    chunk = (fc, rhs.shape[1]) if chunk_lhs else (lhs.shape[1], fc)

    def shard(s, cc):
        return 2 * (s % half) + jnp.where(s < half, 1 - cc[0], cc[0])

    def body(cc_ref, lhs_ref, rhs_ref, wire_ref, own_ref, recv_ref, send_buf, got, send_sems, recv_sems, got_sem):
        s = pl.program_id(0)
        x, y, c = _place()
        def send(q):
            return pltpu.make_async_remote_copy(
                src_ref=send_buf.at[q % 2], dst_ref=recv_ref.at[q], send_sem=send_sems.at[q], recv_sem=recv_sems.at[q],
                device_id=(x, y, 1 - c), device_id_type=MESH)

        def load(q):
            return pltpu.make_async_copy(recv_ref.at[q], got, got_sem.at[0])

        @pl.when(s >= half)
        def _():
            send(s - half).wait_recv()
            load(s - half).start()

        g = _dot(lhs_ref[...], rhs_ref[...], TN)

        for q in range(half):
            @pl.when(s == q)
            def _(q=q):
                if q >= 2:
                    send(q - 2).wait_send()
                send_buf[q % 2] = g
                send(q).start()

            @pl.when(s == half + q)
            def _(q=q):
                load(q).wait()
                total = g + got[...]
                wire_ref[...] = total.astype(BF16)

                @pl.when(cc_ref[1] == q)
                def _():
                    own_ref[...] = total

        @pl.when(s == N_DEV - 1)
        def _():
            for q in range(half - 2, half):
                send(q).wait_send()

    resident = lambda a: pl.BlockSpec(a.shape, lambda s, cc: (0, 0), pipeline_mode=pl.Buffered(1))
    chunked = pl.BlockSpec((t_tok, fc), lambda s, cc: (0, shard(s, cc)))
    (wire, own, _), per_comm = _carry(
        body, name=name, grid=(N_DEV,), comms=comms, prefetch=(core_chip,),
        in_specs=[chunked, resident(rhs)] if chunk_lhs else [resident(lhs), chunked],
        out_specs=[pl.BlockSpec((None,) + chunk, lambda s, cc: (jnp.maximum(s - half, 0), 0, 0)),
                   pl.BlockSpec(chunk, lambda s, cc: (0, 0)), ANY],
        out_shape=[jax.ShapeDtypeStruct((half,) + chunk, BF16), jax.ShapeDtypeStruct(chunk, F32),
                   jax.ShapeDtypeStruct((half,) + chunk, F32)],
        scratch_shapes=[pltpu.VMEM((2,) + chunk, F32), pltpu.VMEM(chunk, F32), pltpu.SemaphoreType.DMA((half,)),
                        pltpu.SemaphoreType.DMA((half,)), pltpu.SemaphoreType.DMA((1,))],
        args=(lhs, rhs))
    return wire, own, per_comm


def _proj_out_bwd(dz1, cat_t, w_out_b, comms=()):
    t_tok, d = dz1.shape
    d_mix = cat_t.shape[0]
    tm = min(512, t_tok)

    def body(dz1_ref, cat_ref, w_ref, dcat_ref, gw_ref):
        @pl.when(pl.program_id(0) == 0)
        def _():
            gw_ref[...] = jnp.zeros_like(gw_ref)

        dzb = dz1_ref[...].astype(BF16)
        dcat_ref[...] = _dot(w_ref[...], dzb, NT)
        gw_ref[...] += _dot(cat_ref[...], dzb)

    return _carry(
        body, name="proj_out_bwd", grid=(t_tok // tm,), comms=comms,
        in_specs=[pl.BlockSpec((tm, d), lambda i: (i, 0)), pl.BlockSpec((d_mix, tm), lambda i: (0, i)),
                  pl.BlockSpec((d_mix, d), lambda i: (0, 0))],
        out_specs=[pl.BlockSpec((d_mix, tm), lambda i: (0, i)), pl.BlockSpec((d_mix, d), lambda i: (0, 0))],
        out_shape=[jax.ShapeDtypeStruct((d_mix, t_tok), F32), jax.ShapeDtypeStruct((d_mix, d), F32)],
        args=(dz1, cat_t, w_out_b))


def _mixer_bwd(dcat_t, h_t, cos_t, sin_t, w_spatial, b_spatial, vln_g, vln_b, sinks, band_bias, comms=()):
    t_tok = h_t.shape[1]
    nb, n_step = t_tok // BLK, t_tok // MIX_W
    group = N_HEADS // N_KV_HEADS

    def body(sinks_ref, dcat_ref, u_ref, vg_ref, q_ref, kvc_ref, kvp_ref, cos_ref, sin_ref, cosp_ref, sinp_ref,
             wsp_ref, bsp_ref, g_ref, b_ref, bias_ref, dh_ref, dkvc_ref, dkvp_ref, gwsb_ref, gbsp_ref, gvln_ref, gsink_ref,
             dg_acc, db_acc, wm_scr, gws_ref):
        i = pl.program_id(0)

        @pl.when(i == 0)
        def _():
            gws_ref[...] = jnp.zeros_like(gws_ref)
            gbsp_ref[...] = jnp.zeros_like(gbsp_ref)
            gsink_ref[...] = jnp.zeros_like(gsink_ref)
            dg_acc[...] = jnp.zeros_like(dg_acc)
            db_acc[...] = jnp.zeros_like(db_acc)

        _mask_w_once(wsp_ref, wm_scr)

        g = g_ref[...]
        ua, ua_grad = _gelu_and_grad(u_ref[...])
        vv, vv_grad = _gelu_and_grad(vg_ref[...])
        vp, vhat, rstd = _ln_fwd_t(vv, g, b_ref[...])
        vpb = vp.astype(BF16)
        da = dcat_ref[0:D_GMLP, :]
        dmixed = da * ua
        dvp_blocks = []
        for b in range(MIX_BLOCKS):
            dvp_parts = []
            for hh in range(N_HEADS):
                rows = slice(hh * HEAD_DIM, (hh + 1) * HEAD_DIM)
                vpb_h = vpb[rows, _cols(b)]
                mixed = _dot(vpb_h, wm_scr[hh], NT) + bsp_ref[hh:hh + 1, :]
                dh_ref[COL_U + hh * HEAD_DIM:COL_U + (hh + 1) * HEAD_DIM, _cols(b)] = (
                    da[rows, _cols(b)] * mixed * ua_grad[rows, _cols(b)]).astype(BF16)
                dm = dmixed[rows, _cols(b)]
                dmb = dm.astype(BF16)
                gbsp_ref[hh:hh + 1, :] += jnp.sum(dm, axis=0, keepdims=True)
                gws_ref[hh] += _dot(dmb, vpb_h, TN)
                dvp_parts.append(_dot(dmb, wm_scr[hh]))
            dvp_blocks.append(jnp.concatenate(dvp_parts, axis=0))
        dvp = jnp.concatenate(dvp_blocks, axis=1)
        dgv, dbv = dvp * vhat, dvp
        for b in range(MIX_BLOCKS):
            dg_acc[...] += dgv[:, _cols(b)]
            db_acc[...] += dbv[:, _cols(b)]
        dh_ref[COL_V:COL_V + D_GMLP, :] = (_ln_bwd_t(dvp, vhat, rstd, g) * vv_grad).astype(BF16)

        kvc, cos, sin = kvc_ref[...], cos_ref[...], sin_ref[...]
        qr = (_rope_t(q_ref[...], cos, sin) * SCORE_SCALE).astype(BF16)
        sinks4 = [_group_lanes([jnp.full((1, BLK), sinks_ref[hh], F32) for hh in range(kv * group, (kv + 1) * group)])
                  for kv in range(N_KV_HEADS)]
        dq_blocks, dkv_cur, dkv_prev = [], [], []
        for b in range(MIX_BLOCKS):
            kv_cur, kv_prev, cosc, sinc, cosp, sinp, bias1 = _block_inputs(b, i, kvc, kvp_ref, cos, sin, cosp_ref, sinp_ref, bias_ref)
            k_t, k_n, v_t = _keys_values(kv_cur, kv_prev, cosc, sinc, cosp, sinp)
            v_n = jnp.concatenate([kv_prev[D_KV:].T, kv_cur[D_KV:].T], axis=0).astype(BF16)
            bias = _group_lanes([bias1] * group)
            dk, dv, dq_parts = [], [], []
            for kv in range(N_KV_HEADS):
                heads = range(kv * group, (kv + 1) * group)
                kv_rows = slice(kv * HEAD_DIM, (kv + 1) * HEAD_DIM)
                qs = _group_lanes([qr[hh * HEAD_DIM:(hh + 1) * HEAD_DIM, _cols(b)] for hh in heads])
                dos = _group_lanes([dcat_ref[D_GMLP + hh * HEAD_DIM:D_GMLP + (hh + 1) * HEAD_DIM, _cols(b)]
                                    for hh in heads]).astype(BF16)
                p, p_sink = _softmax_sink_t(_dot(k_n, _pad_head(qs, kv)) + bias, sinks4[kv])
                dp = _dot(v_n, _pad_head(dos, kv))
                delta = jnp.sum(p * dp, axis=0, keepdims=True)
                ds = (p * (dp - delta)).astype(BF16)
                dsink = p_sink * delta
                dq = _dot(k_t[kv_rows], ds) * SCORE_SCALE
                for j, hh in enumerate(heads):
                    gsink_ref[hh:hh + 1, :] -= dsink[:, j * BLK:(j + 1) * BLK]
                    dq_parts.append(dq[:, j * BLK:(j + 1) * BLK])
                dk.append(_dot(qs, ds, NT))
                dv.append(_dot(dos, p.astype(BF16), NT))
            dq_blocks.append(jnp.concatenate(dq_parts, axis=0))
            dk_all, dv_all = jnp.concatenate(dk, axis=0), jnp.concatenate(dv, axis=0)
            dkv_cur.append(jnp.concatenate([_rope_t(dk_all[:, BLK:], cosc, sinc, bwd=True), dv_all[:, BLK:]], axis=0))
            dkv_prev.append(jnp.concatenate([_rope_t(dk_all[:, :BLK], cosp, sinp, bwd=True), dv_all[:, :BLK]], axis=0))
        dh_ref[COL_Q:COL_Q + D_ATTN, :] = _rope_t(jnp.concatenate(dq_blocks, axis=1), cos, sin, bwd=True).astype(BF16)
        for b in range(MIX_BLOCKS):
            dkvc_ref[:, _cols(b)] = dkv_cur[b] + dkv_prev[b + 1] if b + 1 < MIX_BLOCKS else dkv_cur[b]
        dkvp_ref[...] = dkv_prev[0]

        @pl.when(i == n_step - 1)
        def _():
            causal = _causal()
            for hh in range(N_HEADS):
                gwsb_ref[hh] = jnp.where(causal, gws_ref[hh], 0.0).astype(BF16)
            gvln_ref[...] = jnp.zeros_like(gvln_ref)
            gvln_ref[0:1, :] = jnp.sum(dg_acc[...].T, axis=0, keepdims=True)
            gvln_ref[1:2, :] = jnp.sum(db_acc[...].T, axis=0, keepdims=True)

    full = lambda shape: pl.BlockSpec(shape, lambda i: (0,) * len(shape))
    return _carry(
        body, name="mixer_bwd", grid=(n_step,), comms=comms,
        in_specs=[pl.BlockSpec(memory_space=pltpu.SMEM), pl.BlockSpec((D_GMLP + D_ATTN, MIX_W), lambda i: (0, i))]
        + _h_specs() + _table_specs()
        + [full((N_HEADS, BLK, BLK)), full((N_HEADS, BLK)), full((D_GMLP, 1)), full((D_GMLP, 1)), BIAS_SPEC],
        out_specs=[pl.BlockSpec((COL_K, MIX_W), lambda i: (0, i)), pl.BlockSpec((2 * D_KV, MIX_W), lambda i: (0, i)),
                   pl.BlockSpec((2 * D_KV, BLK), lambda i: (0, (i + n_step - 1) % n_step)),
                   full((N_HEADS, BLK, BLK)), full((N_HEADS, BLK)), full((8, D_GMLP)), full((N_HEADS, LANES))],
        out_shape=[jax.ShapeDtypeStruct((COL_K, t_tok), BF16), jax.ShapeDtypeStruct((2 * D_KV, t_tok), F32),
                   jax.ShapeDtypeStruct((2 * D_KV, n_step * BLK), F32),
                   jax.ShapeDtypeStruct((N_HEADS, BLK, BLK), BF16), jax.ShapeDtypeStruct((N_HEADS, BLK), F32),
                   jax.ShapeDtypeStruct((8, D_GMLP), F32), jax.ShapeDtypeStruct((N_HEADS, LANES), F32)],
        scratch_shapes=[pltpu.VMEM((D_GMLP, BLK), F32), pltpu.VMEM((D_GMLP, BLK), F32), pltpu.VMEM((N_HEADS, BLK, BLK), BF16),
                        pltpu.VMEM((N_HEADS, BLK, BLK), F32)],
        args=(sinks, dcat_t, h_t, h_t, h_t, h_t, h_t, cos_t, sin_t, cos_t, sin_t, w_spatial, b_spatial, vln_g, vln_b, band_bias))


def _proj_in_wgrad(dh_b, dkvc_t, dkvp_t, xb, comms=()):
    t_tok, d = xb.shape
    d_main, d_kv = dh_b.shape[0], dkvc_t.shape[0]
    tm = min(1024, t_tok)

    def body(dh_ref, dkvc_ref, dkvp_ref, xb_ref, dkvb_ref, gw_ref):
        @pl.when(pl.program_id(0) == 0)
        def _():
            gw_ref[...] = jnp.zeros_like(gw_ref)

        for s in range(tm // MIX_W):
            last = slice((s + 1) * MIX_W - BLK, (s + 1) * MIX_W)
            dkvb_ref[:, s * MIX_W:(s + 1) * MIX_W - BLK] = dkvc_ref[:, s * MIX_W:(s + 1) * MIX_W - BLK].astype(BF16)
            dkvb_ref[:, last] = (dkvc_ref[:, last] + dkvp_ref[:, _cols(s)]).astype(BF16)
        gw_ref[0:d_main, :] += _dot(dh_ref[...], xb_ref[...])
        gw_ref[d_main:, :] += _dot(dkvb_ref[...], xb_ref[...])

    tok = lambda rows: pl.BlockSpec((rows, tm), lambda i: (0, i))
    return _carry(
        body, name="proj_in_wgrad", grid=(t_tok // tm,), comms=comms,
        in_specs=[tok(d_main), tok(d_kv), pl.BlockSpec((d_kv, tm // MIX_BLOCKS), lambda i: (0, i)),
                  pl.BlockSpec((tm, d), lambda i: (i, 0))],
        out_specs=[tok(d_kv), pl.BlockSpec((d_main + d_kv, d), lambda i: (0, 0))],
        out_shape=[jax.ShapeDtypeStruct((d_kv, t_tok), BF16), jax.ShapeDtypeStruct((d_main + d_kv, d), F32)],
        args=(dh_b, dkvc_t, dkvp_t, xb))


def _proj_in_dgrad(dh_b, dkv_b, dz1, w_in_t, comms=()):
    t_tok, d = dz1.shape
    d_main, d_kv = dh_b.shape[0], dkv_b.shape[0]
    tm = min(512, t_tok)

    def body(dh_ref, dkv_ref, dz1_ref, w_ref, dx_ref):
        dx_ref[...] = (ALPHA * dz1_ref[...] + _dot(dh_ref[...], w_ref[0:d_main, :], TN)
                       + _dot(dkv_ref[...], w_ref[d_main:, :], TN))

    return _carry(
        body, name="proj_in_dgrad", grid=(t_tok // tm,), comms=comms,
        in_specs=[pl.BlockSpec((d_main, tm), lambda i: (0, i)), pl.BlockSpec((d_kv, tm), lambda i: (0, i)),
                  pl.BlockSpec((tm, d), lambda i: (i, 0)), pl.BlockSpec((d_main + d_kv, d), lambda i: (0, 0))],
        out_specs=[pl.BlockSpec((tm, d), lambda i: (i, 0))],
        out_shape=[jax.ShapeDtypeStruct((t_tok, d), F32)],
        args=(dh_b, dkv_b, dz1, w_in_t))


def _adamw(w, g, m, v):
    m = ADAM_B1 * m + (1.0 - ADAM_B1) * g
    v = ADAM_B2 * v + (1.0 - ADAM_B2) * (g * g)
    m_hat = m / (1.0 - ADAM_B1 ** ADAM_STEP)
    v_hat = v / (1.0 - ADAM_B2 ** ADAM_STEP)
    delta = -ADAM_LR * (m_hat / (jnp.sqrt(v_hat) + ADAM_EPS) + ADAM_WD * w)
    return delta, m, v


def _row_tiled(name, own, recv, extra, n_out, finish, comms=()):
    r, c = own.shape
    recv = [] if recv is None else list(recv)
    k = max(len(recv), 1)
    n = max(k, -(-r // 512))
    tr, per = r // n, n // k
    blk = pl.BlockSpec((tr, c), lambda i: (i, 0))

    def body(own_ref, *refs):
        recv_refs, rest = refs[:len(recv)], refs[len(recv):]
        ins, outs = rest[:len(extra)], rest[len(extra):]

        def tile(recv_ref):
            g = own_ref[...]
            if recv_ref is not None:
                g = ((g + recv_ref[0].astype(F32)) + recv_ref[1].astype(F32)) + recv_ref[2].astype(F32)
            for o_ref, val in zip(outs, finish(g, *[a[...] for a in ins])):
                o_ref[...] = val

        if len(recv) <= 1:
            tile(recv_refs[0] if recv else None)
        else:
            for p in range(k):
                pl.when(pl.program_id(0) // per == p)(functools.partial(tile, recv_refs[p]))

    recv_specs = [pl.BlockSpec((3, tr, c), lambda i, p=p: (0, jnp.clip(i - p * per, 0, per - 1), 0)) for p in range(len(recv))]
    return _carry(
        body, name=name, grid=(n,), comms=comms,
        in_specs=[blk] + recv_specs + [blk] * len(extra),
        out_specs=[blk] * n_out, out_shape=[jax.ShapeDtypeStruct((r, c), F32)] * n_out,
        args=(own, *recv, *extra))


def _adamw_shard(name, own, recv, w, m, v, comms=()):
    def finish(g, w_t, m_t, v_t):
        return (g,) + _adamw(w_t, g, m_t, v_t)

    return _row_tiled(name, own, recv, (w, m, v), 4, finish, comms)


VEC_VLN, VEC_LN1G, VEC_LN1B, VEC_LN2G, VEC_LN2B, VEC_SINK, VEC_LOSS, VEC_BSP, VEC_ROWS = 0, 1, 2, 3, 4, 5, 6, 8, 16


def _adamw_small(parts_w, parts_vec, params):
    n = parts_w.shape[0]
    flat = [a for p in params for a in p]
    shapes = [p[0].shape for p in params]

    def grads(gw, gv):
        return [gw, gv[VEC_VLN:VEC_VLN + 1, 0:D_GMLP], gv[VEC_VLN:VEC_VLN + 1, D_GMLP:2 * D_GMLP],
                gv[VEC_BSP:VEC_BSP + N_HEADS, 0:BLK], gv[VEC_LN1G:VEC_LN1G + 1], gv[VEC_LN1B:VEC_LN1B + 1],
                gv[VEC_LN2G:VEC_LN2G + 1], gv[VEC_LN2B:VEC_LN2B + 1], gv[VEC_SINK:VEC_SINK + 1, 0:N_HEADS]]

    def body(pw_ref, pv_ref, *refs):
        ins, outs = refs[:len(flat)], refs[len(flat):]
        gw, gv = pw_ref[0].astype(F32), pv_ref[0]
        for k in range(1, n):
            gw, gv = gw + pw_ref[k].astype(F32), gv + pv_ref[k]
        for i, g in enumerate(grads(gw, gv)):
            w_ref, m_ref, v_ref = ins[3 * i:3 * i + 3]
            delta, m_new, v_new = _adamw(w_ref[...], g, m_ref[...], v_ref[...])
            for o_ref, val in zip(outs[4 * i:4 * i + 4], (g, delta, m_new, v_new)):
                o_ref[...] = val
        outs[-1][...] = gv[VEC_LOSS:VEC_LOSS + 1, 0:LANES]

    whole = lambda shape: pl.BlockSpec(shape, lambda i: (0,) * len(shape))
    res = _carry(
        body, name="adamw_small", grid=(1,),
        in_specs=[whole(parts_w.shape), whole(parts_vec.shape)] + [whole(a.shape) for a in flat],
        out_specs=[whole(s) for s in shapes for _ in range(4)] + [whole((1, LANES))],
        out_shape=[jax.ShapeDtypeStruct(s, F32) for s in shapes for _ in range(4)] + [jax.ShapeDtypeStruct((1, LANES), F32)],
        args=(parts_w, parts_vec, *flat))[0]
    return [res[4 * i:4 * i + 4] for i in range(len(params))], res[-1]


def _pair_sum(name, parts, recv, core_chip, comms=()):
    _, r, c = parts.shape
    tr = r if r <= 512 else 512

    def body(cc_ref, a_ref, b_ref, wire_ref, own_ref):
        s = a_ref[...] + b_ref[...]
        wire_ref[...] = s.astype(BF16)

        @pl.when(pl.program_id(1) == cc_ref[1])
        def _():
            own_ref[...] = s

    return _carry(
        body, name=name, grid=(r // tr, 4), prefetch=(core_chip,), comms=comms,
        in_specs=[pl.BlockSpec((None, tr, c), lambda i, q, cc: (2 * q + cc[0], i, 0)),
                  pl.BlockSpec((None, tr, c), lambda i, q, cc: (q, i, 0))],
        out_specs=[pl.BlockSpec((None, tr, c), lambda i, q, cc: (q, i, 0)), pl.BlockSpec((tr, c), lambda i, q, cc: (i, 0))],
        out_shape=[jax.ShapeDtypeStruct((4, r, c), BF16), jax.ShapeDtypeStruct((r, c), F32)],
        args=(parts, recv))


def kernel(x, positions, w_in, v_ln_g, v_ln_b, w_spatial, b_spatial, sinks, w_out, ln1_g, ln1_b, w_ff1, w_ff2, ln2_g, ln2_b, loss_target, m_w_in, m_v_ln_g, m_v_ln_b, m_w_spatial, m_b_spatial, m_sinks, m_w_out, m_ln1_g, m_ln1_b, m_w_ff1, m_w_ff2, m_ln2_g, m_ln2_b, v_w_in, v_v_ln_g, v_v_ln_b, v_w_spatial, v_b_spatial, v_sinks, v_w_out, v_ln1_g, v_ln1_b, v_w_ff1, v_w_ff2, v_ln2_g, v_ln2_b):
    _, t_tok, d = x.shape
    xi, yi, ci = _place()
    core_chip = jnp.stack([ci, 2 * xi + yi]).astype(jnp.int32)
    x2 = x.reshape(t_tok, d)
    target = loss_target.reshape(t_tok, d)
    inv_freq = ROPE_THETA ** (-jnp.arange(0, HEAD_DIM, 2, dtype=F32) / HEAD_DIM)
    wsp, bsp, sink_vec = w_spatial[0], b_spatial[0], sinks[0]
    vg_col, vb_col = v_ln_g.reshape(D_GMLP, 1), v_ln_b.reshape(D_GMLP, 1)
    big = {"in": w_in[0], "out": w_out[0], "ff1": w_ff1[0], "ff2": w_ff2[0]}
    half1, half2 = big["ff1"].shape[1] // 2, big["ff2"].shape[0] // 2
    w1_mine = [big["ff1"][:, :half1].astype(BF16), big["ff1"][:, half1:].astype(BF16)]
    w2_mine = [big["ff2"][:half2].astype(BF16), big["ff2"][half2:].astype(BF16)]

    (cos_t, sin_t), ((g_in,),) = _rope_tables(
        positions, jnp.tile(inv_freq, 2).reshape(HEAD_DIM, 1), comms=[_gather_comm([big["in"].T.astype(BF16)])])
    w_in_t = g_in.reshape(D_IN, d)
    (h_t, xb), ((g_out, w1_a),) = _proj_in(x2, w_in_t, comms=[_gather_comm([big["out"].astype(BF16), w1_mine[0]])])
    w_out_b = g_out.reshape(-1, d)
    band_bias = _band_bias()
    (cat_t,), ((w1_b, w2_a),) = _mixer_fwd(h_t, cos_t, sin_t, wsp, bsp, vg_col, vb_col, sink_vec, band_bias,
                                           comms=[_gather_comm([w1_mine[1], w2_mine[0]])])
    (xhat1, rstd1, x1b), ((w2_b,),) = _proj_out(cat_t, x2, w_out_b, ln1_g, ln1_b, comms=[_gather_comm([w2_mine[1]])])
    act_b, dpre_b, dz2b, dz1, stats = _ffn_fwd_bwd(xhat1, rstd1, x1b, target, [w1_a, w1_b], [w2_a, w2_b], ln1_g, ln1_b, ln2_g, ln2_b)

    (dcat_t, gw_out), _ = _proj_out_bwd(dz1, cat_t, w_out_b)
    p_out = gw_out.reshape(N_DEV, -1, d)
    wire_ff1, own_ff1, ((s_out,),) = _ffn_wgrad("ffn_wgrad1", x1b, dpre_b, False, core_chip, comms=[_sibling_comm([p_out])])
    (wire_out, own_out), _ = _pair_sum("pair_sum_out", p_out, s_out, core_chip)
    wire_ff2, own_ff2, ((r_ff1,),) = _ffn_wgrad("ffn_wgrad2", act_b, dz2b, True, core_chip, comms=[_chips_comm([wire_ff1])])
    (dh_b, dkvc_t, dkvp_t, g_wsp, g_bsp, g_vln, g_sink), ((r_ff2, r_out),) = _mixer_bwd(
        dcat_t, h_t, cos_t, sin_t, wsp, bsp, vg_col, vb_col, sink_vec, band_bias,
        comms=[_chips_comm([wire_ff2, wire_out])])
    sink_row = jnp.pad(g_sink.sum(axis=1).reshape(1, N_HEADS), ((0, 0), (0, d - N_HEADS)))
    small_vec = jnp.concatenate([g_vln[0:2].reshape(1, d), stats[0:4], sink_row, stats[4:5], jnp.zeros((1, d), F32),
                                 jnp.pad(g_bsp, ((0, 0), (0, d - BLK)))], axis=0)
    (dkv_b, gw_in_t), ((parts_w, parts_vec),) = _proj_in_wgrad(
        dh_b, dkvc_t, dkvp_t, xb, comms=[_gather_comm([g_wsp.reshape(-1, BLK), small_vec])])
    p_in = gw_in_t.reshape(N_DEV, -1, d)

    out_out, ((s_in,),) = _adamw_shard("adamw_out", own_out, [r_out], big["out"], m_w_out[0], v_w_out[0], comms=[_sibling_comm([p_in])])
    (wire_in, own_in), _ = _pair_sum("pair_sum_in", p_in, s_in, core_chip)
    (grad_x,), ((r_in,),) = _proj_in_dgrad(dh_b, dkv_b, dz1, w_in_t, comms=[_chips_comm([wire_in])])
    ff1_out, _ = _adamw_shard("adamw_ff1", own_ff1, [r_ff1], big["ff1"], m_w_ff1[0], v_w_ff1[0])
    ff2_out, _ = _adamw_shard("adamw_ff2", own_ff2, [r_ff2], big["ff2"], m_w_ff2[0], v_w_ff2[0])
    in_out_t, _ = _adamw_shard("adamw_in", own_in, [r_in], big["in"].T, m_w_in[0].T, v_w_in[0].T)
    in_out = [o.T for o in in_out_t]
    small = [(w_spatial, m_w_spatial, v_w_spatial), (v_ln_g, m_v_ln_g, v_v_ln_g), (v_ln_b, m_v_ln_b, v_v_ln_b),
             (b_spatial, m_b_spatial, v_b_spatial), (ln1_g, m_ln1_g, v_ln1_g), (ln1_b, m_ln1_b, v_ln1_b),
             (ln2_g, m_ln2_g, v_ln2_g), (ln2_b, m_ln2_b, v_ln2_b), (sinks, m_sinks, v_sinks)]
    views = [(-1, BLK), None, None, (N_HEADS, BLK)] + [None] * 5
    small_res, loss_row = _adamw_small(parts_w, parts_vec, [
        tuple(a if vw is None else a.reshape(vw) for a in p) for p, vw in zip(small, views)])
    small_out = [[o.reshape(p[0].shape) for o in res] for res, p in zip(small_res, small)]
    loss = loss_row[0, 0]

    big_out = {0: in_out, 6: out_out, 9: ff1_out, 10: ff2_out}
    small_slot = {3: 0, 1: 1, 2: 2, 4: 3, 7: 4, 8: 5, 11: 6, 12: 7, 5: 8}
    outs = [loss, grad_x.reshape(x.shape)]
    for kind in range(4):
        for wi in range(13):
            outs.append(big_out[wi][kind][None] if wi in big_out else small_out[small_slot[wi]][kind])
    return tuple(outs)
```

```python
import functools
import math

import jax
import jax.numpy as jnp
from jax import lax
from jax.experimental import pallas as pl
from jax.experimental.pallas import tpu as pltpu

F32 = jnp.float32
BF16 = jnp.bfloat16
MESH = pl.DeviceIdType.MESH

HEAD_DIM = 64
N_HEADS = 8
N_KV_HEADS = 2
BLK = 128
D_GMLP = N_HEADS * HEAD_DIM
D_ATTN = N_HEADS * HEAD_DIM
D_KV = N_KV_HEADS * HEAD_DIM
D_IN = 2 * D_GMLP + D_ATTN + 2 * D_KV
COL_U, COL_V, COL_Q, COL_K = 0, D_GMLP, 2 * D_GMLP, 2 * D_GMLP + D_ATTN
ROPE_THETA = 10000.0
LN_EPS = 1e-5
ALPHA = 2.0 ** 0.25
NEG_INF = -1e30
SCORE_SCALE = 1.0 / math.sqrt(HEAD_DIM)
ADAM_LR, ADAM_B1, ADAM_B2, ADAM_EPS, ADAM_WD, ADAM_STEP = 0.001, 0.9, 0.999, 1e-08, 0.01, 10
N_DEV = 8
LANES = 128
VMEM_LIMIT = 56 * 1024 * 1024
FFN_ROWS = 256

NT = (((1,), (1,)), ((), ()))
TN = (((0,), (0,)), ((), ()))


def _params(*sem):
    return pltpu.CompilerParams(dimension_semantics=sem, vmem_limit_bytes=VMEM_LIMIT)


def _dot(a, b, dims=None):
    if dims is None:
        return jnp.dot(a, b, preferred_element_type=F32)
    return lax.dot_general(a, b, dims, preferred_element_type=F32)


def _mean(a):
    return jnp.mean(a, axis=-1, keepdims=True)


def _ln_fwd(z, g, b):
    zc = z - _mean(z)
    rstd = lax.rsqrt(_mean(zc * zc) + LN_EPS)
    xhat = zc * rstd
    return xhat * g + b, xhat, rstd


def _ln_bwd(dy, xhat, rstd, g):
    dxhat = dy * g
    return rstd * (dxhat - _mean(dxhat) - xhat * _mean(dxhat * xhat))


_GELU_C = math.sqrt(2.0 / math.pi)


def _gelu(x):
    t = jnp.tanh(_GELU_C * (x + 0.044715 * (x * x * x)))
    return 0.5 * x * (1.0 + t)


def _gelu_and_grad(x):
    x2 = x * x
    t = jnp.tanh(_GELU_C * (x + 0.044715 * (x2 * x)))
    hx, ht = 0.5 * x, 0.5 * (1.0 + t)
    return x * ht, ht + hx * (1.0 - t * t) * (_GELU_C * (1.0 + 3.0 * 0.044715 * x2))


def _mean0(a):
    return jnp.mean(a, axis=0, keepdims=True)


def _ln_fwd_t(z, g, b):
    zc = z - _mean0(z)
    rstd = lax.rsqrt(_mean0(zc * zc) + LN_EPS)
    xhat = zc * rstd
    return xhat * g + b, xhat, rstd


def _ln_bwd_t(dy, xhat, rstd, g):
    dxhat = dy * g
    return rstd * (dxhat - _mean0(dxhat) - xhat * _mean0(dxhat * xhat))


def _rope_t(t, cos, sin_signed, bwd=False):
    half = HEAD_DIM // 2
    outs = []
    for r in range(0, t.shape[0], HEAD_DIM):
        th = t[r:r + HEAD_DIM]
        sw = jnp.concatenate([th[half:], th[:half]], axis=0) * sin_signed
        outs.append(th * cos - sw if bwd else th * cos + sw)
    return jnp.concatenate(outs, axis=0)


ANY = pl.BlockSpec(memory_space=pl.ANY)


def _place():
    return lax.axis_index("x"), lax.axis_index("y"), lax.axis_index("c")


class _Comm:
    def __init__(self, ins, outs, sems, start, finish):
        self.ins, self.outs, self.sems, self.start, self.finish = ins, outs, sems, start, finish


def _gather_comm(arrs):
    n = len(arrs)

    def parts(ins, outs, sems):
        send_sems, recv_sems, local_sems = sems
        x, y, c = _place()
        me, sibling = (x, y, c), (x, y, 1 - c)
        chips = [(1 - x, y), (x, 1 - y), (1 - x, 1 - y)]

        def copy(a, k, block, to, src=None):
            px, py, pc = block
            dst = outs[a].at[4 * px + 2 * py + pc]
            return pltpu.make_async_remote_copy(
                src_ref=dst if src is None else src, dst_ref=dst,
                send_sem=send_sems.at[a, k], recv_sem=recv_sems.at[a, k], device_id=to, device_id_type=MESH)

        mine = [pltpu.make_async_copy(ins[a], outs[a].at[4 * x + 2 * y + c], local_sems.at[a]) for a in range(n)]
        first = []
        for a in range(n):
            first.append(copy(a, 0, me, sibling, src=ins[a]))
            first += [copy(a, 1 + j, me, (*chip, c), src=ins[a]) for j, chip in enumerate(chips)]
        return copy, mine, first, me, sibling, chips, c

    def start(ins, outs, sems):
        _, mine, first, *_ = parts(ins, outs, sems)
        for cp in mine + first:
            cp.start()

    def finish(ins, outs, sems):
        copy, mine, first, me, sibling, chips, c = parts(ins, outs, sems)
        passed = []
        for j, chip in enumerate(chips):
            for a in range(n):
                copy(a, 1 + j, (*chip, c), me).wait_recv()
                fwd = copy(a, 4 + j, (*chip, c), sibling)
                fwd.start()
                passed.append(fwd)
        for a in range(n):
            copy(a, 0, sibling, me).wait_recv()
        for j, chip in enumerate(chips):
            for a in range(n):
                copy(a, 4 + j, (*chip, 1 - c), me).wait_recv()
        for cp in first + passed:
            cp.wait_send()
        for cp in mine:
            cp.wait()

    return _Comm(list(arrs), [jax.ShapeDtypeStruct((N_DEV,) + a.shape, a.dtype) for a in arrs],
                 [pltpu.SemaphoreType.DMA((n, 7)), pltpu.SemaphoreType.DMA((n, 7)), pltpu.SemaphoreType.DMA((n,))],
                 start, finish)


def _sibling_comm(parts):
    n = len(parts)

    def copies(ins, outs, sems):
        x, y, c = _place()
        return [pltpu.make_async_remote_copy(
            src_ref=ins[a].at[2 * q + (1 - c)], dst_ref=outs[a].at[q],
            send_sem=sems[0].at[a, q], recv_sem=sems[1].at[a, q],
            device_id=(x, y, 1 - c), device_id_type=MESH) for a in range(n) for q in range(4)]

    return _Comm(list(parts), [jax.ShapeDtypeStruct((4,) + p.shape[1:], p.dtype) for p in parts],
                 [pltpu.SemaphoreType.DMA((n, 4)), pltpu.SemaphoreType.DMA((n, 4))],
                 lambda *r: [cp.start() for cp in copies(*r)], lambda *r: [cp.wait() for cp in copies(*r)])


def _chips_comm(chip_parts, rows=None):
    n = len(chip_parts)
    r0, nr = (0, None) if rows is None else rows

    def copies(ins, outs, sems):
        x, y, c = _place()
        chips = [(1 - x, y), (x, 1 - y), (1 - x, 1 - y)]
        src = lambda a, q: ins[a].at[q] if rows is None else ins[a].at[q, pl.ds(r0, nr)]
        return [pltpu.make_async_remote_copy(
            src_ref=src(a, 2 * px + py), dst_ref=outs[a].at[k],
            send_sem=sems[0].at[a, k], recv_sem=sems[1].at[a, k],
            device_id=(px, py, c), device_id_type=MESH) for a in range(n) for k, (px, py) in enumerate(chips)]

    shape = lambda p: (3,) + p.shape[1:] if rows is None else (3, nr) + p.shape[2:]
    return _Comm(list(chip_parts), [jax.ShapeDtypeStruct(shape(p), p.dtype) for p in chip_parts],
                 [pltpu.SemaphoreType.DMA((n, 3)), pltpu.SemaphoreType.DMA((n, 3))],
                 lambda *r: [cp.start() for cp in copies(*r)], lambda *r: [cp.wait() for cp in copies(*r)])


def _carry(body, *, name, grid, in_specs, out_specs, out_shape, args, comms=(), scratch_shapes=(), prefetch=()):
    n_pre, n_in, n_out, n_scr = len(prefetch), len(in_specs), len(out_specs), len(scratch_shapes)
    c_ins = [a for cm in comms for a in cm.ins]
    c_outs = [s for cm in comms for s in cm.outs]
    c_sems = [s for cm in comms for s in cm.sems]

    def wrapped(*refs):
        pre, refs = refs[:n_pre], refs[n_pre:]
        ins, refs = refs[:n_in], refs[n_in:]
        cins, refs = refs[:len(c_ins)], refs[len(c_ins):]
        outs, refs = refs[:n_out], refs[n_out:]
        couts, refs = refs[:len(c_outs)], refs[len(c_outs):]
        scr, sems = refs[:n_scr], refs[n_scr:]
        groups, i0, o0, s0 = [], 0, 0, 0
        for cm in comms:
            groups.append((cm, cins[i0:i0 + len(cm.ins)], couts[o0:o0 + len(cm.outs)], sems[s0:s0 + len(cm.sems)]))
            i0, o0, s0 = i0 + len(cm.ins), o0 + len(cm.outs), s0 + len(cm.sems)
        first = pl.program_id(0) == 0
        last = pl.program_id(0) == grid[0] - 1
        for ax in range(1, len(grid)):
            first = first & (pl.program_id(ax) == 0)
            last = last & (pl.program_id(ax) == grid[ax] - 1)
        if comms:
            @pl.when(first)
            def _():
                for cm, ci, co, cs in groups:
                    cm.start(ci, co, cs)
        body(*pre, *ins, *outs, *scr)
        if comms:
            @pl.when(last)
            def _():
                for cm, ci, co, cs in groups:
                    cm.finish(ci, co, cs)

    grid_spec = pltpu.PrefetchScalarGridSpec(
        num_scalar_prefetch=n_pre, grid=grid,
        in_specs=list(in_specs) + [ANY] * len(c_ins), out_specs=list(out_specs) + [ANY] * len(c_outs),
        scratch_shapes=list(scratch_shapes) + c_sems)
    res = pl.pallas_call(
        wrapped, name=name, grid_spec=grid_spec, out_shape=list(out_shape) + c_outs,
        compiler_params=_params(*(["arbitrary"] * len(grid))),
    )(*prefetch, *args, *c_ins)
    outs, rest, per_comm = res[:n_out], res[n_out:], []
    for cm in comms:
        per_comm.append(rest[:len(cm.outs)])
        rest = rest[len(cm.outs):]
    return outs, per_comm


def _rope_tables(pos_row, inv_freq_col, comms=()):
    t_tok = pos_row.shape[1]
    tm = min(512, t_tok)

    def body(pos_ref, invf_ref, cos_ref, sin_ref):
        ang = pos_ref[...].astype(F32) * invf_ref[...]
        row = lax.broadcasted_iota(jnp.int32, ang.shape, 0)
        cos_ref[...] = jnp.cos(ang)
        sin_ref[...] = jnp.sin(ang) * jnp.where(row < HEAD_DIM // 2, -1.0, 1.0)

    return _carry(
        body, name="rope_tables", grid=(t_tok // tm,), comms=comms,
        in_specs=[pl.BlockSpec((1, tm), lambda i: (0, i)), pl.BlockSpec((HEAD_DIM, 1), lambda i: (0, 0))],
        out_specs=[pl.BlockSpec((HEAD_DIM, tm), lambda i: (0, i))] * 2,
        out_shape=[jax.ShapeDtypeStruct((HEAD_DIM, t_tok), F32)] * 2,
        args=(pos_row, inv_freq_col))


def _proj_in(x2, w_in_t, comms=()):
    t_tok, d = x2.shape
    d_in = w_in_t.shape[0]
    tm = min(512, t_tok)

    def body(x_ref, w_ref, h_ref, xb_ref):
        xb = x_ref[...].astype(BF16)
        xb_ref[...] = xb
        h_ref[...] = _dot(w_ref[...], xb, NT)

    return _carry(
        body, name="proj_in", grid=(t_tok // tm,), comms=comms,
        in_specs=[pl.BlockSpec((tm, d), lambda i: (i, 0)), pl.BlockSpec((d_in, d), lambda i: (0, 0))],
        out_specs=[pl.BlockSpec((d_in, tm), lambda i: (0, i)), pl.BlockSpec((tm, d), lambda i: (i, 0))],
        out_shape=[jax.ShapeDtypeStruct((d_in, t_tok), F32), jax.ShapeDtypeStruct((t_tok, d), BF16)],
        args=(x2, w_in_t))


MIX_BLOCKS = 2
MIX_W = MIX_BLOCKS * BLK


def _prev_block(i):
    return jnp.maximum(MIX_BLOCKS * i - 1, 0)


def _h_specs():
    kv_row = COL_K // (2 * D_KV)
    return [
        pl.BlockSpec((D_GMLP, MIX_W), lambda i: (0, i)),
        pl.BlockSpec((D_GMLP, MIX_W), lambda i: (1, i)),
        pl.BlockSpec((D_ATTN, MIX_W), lambda i: (2, i)),
        pl.BlockSpec((2 * D_KV, MIX_W), lambda i: (kv_row, i)),
        pl.BlockSpec((2 * D_KV, BLK), lambda i: (kv_row, _prev_block(i))),
    ]


def _table_specs():
    return [
        pl.BlockSpec((HEAD_DIM, MIX_W), lambda i: (0, i)),
        pl.BlockSpec((HEAD_DIM, MIX_W), lambda i: (0, i)),
        pl.BlockSpec((HEAD_DIM, BLK), lambda i: (0, _prev_block(i))),
        pl.BlockSpec((HEAD_DIM, BLK), lambda i: (0, _prev_block(i))),
    ]


def _cols(b):
    return slice(b * BLK, (b + 1) * BLK)


def _block_inputs(b, i, kvc, kvp_ref, cos, sin, cosp_ref, sinp_ref, bias_ref):
    if b == 0:
        kv_prev, cos_prev, sin_prev, bias = kvp_ref[...], cosp_ref[...], sinp_ref[...], bias_ref[jnp.minimum(i, 1)]
    else:
        kv_prev, cos_prev, sin_prev, bias = kvc[:, _cols(b - 1)], cos[:, _cols(b - 1)], sin[:, _cols(b - 1)], bias_ref[1]
    return kvc[:, _cols(b)], kv_prev, cos[:, _cols(b)], sin[:, _cols(b)], cos_prev, sin_prev, bias


def _band_bias():
    ki = lax.broadcasted_iota(jnp.int32, (2, 2 * BLK, BLK), 1)
    qi = lax.broadcasted_iota(jnp.int32, (2, 2 * BLK, BLK), 2)
    later = lax.broadcasted_iota(jnp.int32, (2, 2 * BLK, BLK), 0) > 0
    dist = qi + BLK - ki
    return jnp.where((dist >= 0) & (dist < BLK) & ((ki >= BLK) | later), 0.0, NEG_INF).astype(F32)


BIAS_SPEC = pl.BlockSpec((2, 2 * BLK, BLK), lambda i: (0, 0, 0))


def _keys_values(kvc, kvp, cosc, sinc, cosp, sinp):
    kp, kc = _rope_t(kvp[:D_KV], cosp, sinp), _rope_t(kvc[:D_KV], cosc, sinc)
    k_t = jnp.concatenate([kp, kc], axis=1).astype(BF16)
    k_n = jnp.concatenate([kp.T, kc.T], axis=0).astype(BF16)
    v_t = jnp.concatenate([kvp[D_KV:], kvc[D_KV:]], axis=1).astype(BF16)
    return k_t, k_n, v_t


def _pad_head(th, kv):
    z = jnp.zeros_like(th)
    return jnp.concatenate([th, z] if kv == 0 else [z, th], axis=0)


def _group_lanes(parts):
    return jnp.concatenate(parts, axis=1)


def _softmax_sink_t(s, sink):
    m = jnp.maximum(jnp.max(s, axis=0, keepdims=True), sink)
    e = jnp.exp(s - m)
    es = jnp.exp(sink - m)
    r = 1.0 / (jnp.sum(e, axis=0, keepdims=True) + es)
    return e * r, es * r


def _causal():
    row = lax.broadcasted_iota(jnp.int32, (BLK, BLK), 0)
    col = lax.broadcasted_iota(jnp.int32, (BLK, BLK), 1)
    return row >= col


def _mask_w_once(wsp_ref, wm_scr):
    @pl.when(pl.program_id(0) == 0)
    def _():
        causal = _causal()
        for hh in range(N_HEADS):
            wm_scr[hh] = jnp.where(causal, wsp_ref[hh], 0.0).astype(BF16)


def _mixer_fwd(h_t, cos_t, sin_t, w_spatial, b_spatial, vln_g, vln_b, sinks, band_bias, comms=()):
    t_tok = h_t.shape[1]
    group = N_HEADS // N_KV_HEADS

    def body(sinks_ref, u_ref, vg_ref, q_ref, kvc_ref, kvp_ref, cos_ref, sin_ref, cosp_ref, sinp_ref,
             wsp_ref, bsp_ref, g_ref, b_ref, bias_ref, cat_ref, wm_scr):
        i = pl.program_id(0)
        _mask_w_once(wsp_ref, wm_scr)
        ua = _gelu(u_ref[...])
        vp, _, _ = _ln_fwd_t(_gelu(vg_ref[...]), g_ref[...], b_ref[...])
        vpb = vp.astype(BF16)
        for b in range(MIX_BLOCKS):
            for hh in range(N_HEADS):
                rows = slice(hh * HEAD_DIM, (hh + 1) * HEAD_DIM)
                mixed = _dot(vpb[rows, _cols(b)], wm_scr[hh], NT) + bsp_ref[hh:hh + 1, :]
                cat_ref[rows, _cols(b)] = (ua[rows, _cols(b)] * mixed).astype(BF16)

        kvc, cos, sin = kvc_ref[...], cos_ref[...], sin_ref[...]
        qr = (_rope_t(q_ref[...], cos, sin) * SCORE_SCALE).astype(BF16)
        sinks4 = [_group_lanes([jnp.full((1, BLK), sinks_ref[hh], F32) for hh in range(kv * group, (kv + 1) * group)])
                  for kv in range(N_KV_HEADS)]
        for b in range(MIX_BLOCKS):
            kv_cur, kv_prev, cosc, sinc, cosp, sinp, bias1 = _block_inputs(b, i, kvc, kvp_ref, cos, sin, cosp_ref, sinp_ref, bias_ref)
            _, k_n, v_t = _keys_values(kv_cur, kv_prev, cosc, sinc, cosp, sinp)
            bias = _group_lanes([bias1] * group)
            for kv in range(N_KV_HEADS):
                heads = range(kv * group, (kv + 1) * group)
                qs = _group_lanes([qr[hh * HEAD_DIM:(hh + 1) * HEAD_DIM, _cols(b)] for hh in heads])
                p, _ = _softmax_sink_t(_dot(k_n, _pad_head(qs, kv)) + bias, sinks4[kv])
                o = _dot(v_t[kv * HEAD_DIM:(kv + 1) * HEAD_DIM], p.astype(BF16)).astype(BF16)
                for j, hh in enumerate(heads):
                    cat_ref[D_GMLP + hh * HEAD_DIM:D_GMLP + (hh + 1) * HEAD_DIM, _cols(b)] = o[:, j * BLK:(j + 1) * BLK]

    full = lambda shape: pl.BlockSpec(shape, lambda i: (0,) * len(shape))
    return _carry(
        body, name="mixer_fwd", grid=(t_tok // MIX_W,), comms=comms,
        in_specs=[pl.BlockSpec(memory_space=pltpu.SMEM)] + _h_specs() + _table_specs() + [
            full((N_HEADS, BLK, BLK)), full((N_HEADS, BLK)), full((D_GMLP, 1)), full((D_GMLP, 1)), BIAS_SPEC],
        out_specs=[pl.BlockSpec((D_GMLP + D_ATTN, MIX_W), lambda i: (0, i))],
        out_shape=[jax.ShapeDtypeStruct((D_GMLP + D_ATTN, t_tok), BF16)],
        scratch_shapes=[pltpu.VMEM((N_HEADS, BLK, BLK), BF16)],
        args=(sinks, h_t, h_t, h_t, h_t, h_t, cos_t, sin_t, cos_t, sin_t, w_spatial, b_spatial, vln_g, vln_b, band_bias))


def _proj_out(cat_t, x2, w_out_b, ln1_g, ln1_b, comms=()):
    t_tok, d = x2.shape
    tm = min(512, t_tok)

    def body(cat_ref, x_ref, w_ref, g_ref, b_ref, xhat_ref, rstd_ref, x1b_ref):
        x1, xhat, rstd = _ln_fwd(ALPHA * x_ref[...] + _dot(cat_ref[...], w_ref[...], TN), g_ref[...], b_ref[...])
        xhat_ref[...] = xhat
        rstd_ref[...] = rstd
        x1b_ref[...] = x1.astype(BF16)

    tok = lambda w: pl.BlockSpec((tm, w), lambda i: (i, 0))
    vec = pl.BlockSpec((1, d), lambda i: (0, 0))
    return _carry(
        body, name="proj_out", grid=(t_tok // tm,), comms=comms,
        in_specs=[pl.BlockSpec((cat_t.shape[0], tm), lambda i: (0, i)), tok(d), pl.BlockSpec(w_out_b.shape, lambda i: (0, 0)), vec, vec],
        out_specs=[tok(d), tok(1), tok(d)],
        out_shape=[jax.ShapeDtypeStruct((t_tok, d), F32), jax.ShapeDtypeStruct((t_tok, 1), F32), jax.ShapeDtypeStruct((t_tok, d), BF16)],
        args=(cat_t, x2, w_out_b, ln1_g, ln1_b))


def _ffn_fwd_bwd(xhat1, rstd1, x1b, target, w1_parts, w2_parts, ln1_g, ln1_b, ln2_g, ln2_b):
    t_tok, d = xhat1.shape
    n_part = len(w1_parts)
    n_chunk, _, fp = w1_parts[0].shape
    fc = n_part * fp
    f = n_chunk * fc
    tm = min(FFN_ROWS, t_tok)

    def body(xhat1_ref, rstd1_ref, x1b_ref, tgt_ref, *refs):
        w1_hbm, w2_hbm = refs[:n_part], refs[n_part:2 * n_part]
        (g1_ref, b1_ref, g2_ref, b2_ref, act_ref, dpre_ref, dz2b_ref, dz1_ref, stats_ref,
         r_scr, w1_ref, w2_ref, w_sems) = refs[2 * n_part:]

        @pl.when(pl.program_id(0) == 0)
        def _():
            stats_ref[...] = jnp.zeros_like(stats_ref)
            loads = [pltpu.make_async_copy(w1_hbm[p], w1_ref.at[:, :, pl.ds(p * fp, fp)], w_sems.at[0, p]) for p in range(n_part)]
            loads += [pltpu.make_async_copy(w2_hbm[p], w2_ref.at[:, pl.ds(p * fp, fp), :], w_sems.at[1, p]) for p in range(n_part)]
            for cp in loads:
                cp.start()
            for cp in loads:
                cp.wait()

        g1, g2 = g1_ref[...], g2_ref[...]
        xhat1, x1b = xhat1_ref[...], x1b_ref[...]
        ff = jnp.zeros((tm, d), F32)
        for j in range(n_chunk):
            r = jnp.maximum(_dot(x1b, w1_ref[j]), 0.0)
            r_scr[:, j * fc:(j + 1) * fc] = r
            act = (r * r).astype(BF16)
            act_ref[:, j * fc:(j + 1) * fc] = act
            ff = ff + _dot(act, w2_ref[j])
        y, xhat2, rstd2 = _ln_fwd(ALPHA * (xhat1 * g1 + b1_ref[...]) + ff, g2, b2_ref[...])
        diff = y - tgt_ref[...]
        loss = 0.5 * jnp.sum(jnp.sum(diff * diff, axis=-1, keepdims=True) / d, axis=0, keepdims=True)
        dy = diff / d
        dz2 = _ln_bwd(dy, xhat2, rstd2, g2)
        dz2b = dz2.astype(BF16)
        dz2b_ref[...] = dz2b
        dx1 = ALPHA * dz2
        for j in range(n_chunk):
            dpre = (_dot(dz2b, w2_ref[j], NT) * (2.0 * r_scr[:, j * fc:(j + 1) * fc])).astype(BF16)
            dpre_ref[:, j * fc:(j + 1) * fc] = dpre
            dx1 = dx1 + _dot(dpre, w1_ref[j], NT)
        dz1_ref[...] = _ln_bwd(dx1, xhat1, rstd1_ref[...], g1)
        stats_ref[0:1, :] += jnp.sum(dx1 * xhat1, axis=0, keepdims=True)
        stats_ref[1:2, :] += jnp.sum(dx1, axis=0, keepdims=True)
        stats_ref[2:3, :] += jnp.sum(dy * xhat2, axis=0, keepdims=True)
        stats_ref[3:4, :] += jnp.sum(dy, axis=0, keepdims=True)
        stats_ref[4:5, :] += jnp.broadcast_to(loss, (1, d))

    tok = lambda w: pl.BlockSpec((tm, w), lambda i: (i, 0))
    vec = pl.BlockSpec((1, d), lambda i: (0, 0))
    return _carry(
        body, name="ffn_fwd_bwd", grid=(t_tok // tm,),
        in_specs=[tok(d), tok(1), tok(d), tok(d)] + [ANY] * (2 * n_part) + [vec, vec, vec, vec],
        out_specs=[tok(f), tok(f), tok(d), tok(d), pl.BlockSpec((8, d), lambda i: (0, 0))],
        out_shape=[jax.ShapeDtypeStruct((t_tok, f), BF16), jax.ShapeDtypeStruct((t_tok, f), BF16),
                   jax.ShapeDtypeStruct((t_tok, d), BF16), jax.ShapeDtypeStruct((t_tok, d), F32), jax.ShapeDtypeStruct((8, d), F32)],
        scratch_shapes=[pltpu.VMEM((tm, f), F32), pltpu.VMEM((n_chunk, d, fc), BF16), pltpu.VMEM((n_chunk, fc, d), BF16),
                        pltpu.SemaphoreType.DMA((2, n_part))],
        args=(xhat1, rstd1, x1b, target, *w1_parts, *w2_parts, ln1_g, ln1_b, ln2_g, ln2_b))[0]


def _ffn_wgrad(name, lhs, rhs, chunk_lhs, core_chip, comms=()):
    t_tok = lhs.shape[0]
    half = N_DEV // 2
    fc = (lhs if chunk_lhs else rhs).shape[1] // N_DEV
    chunk = (fc, rhs.shape[1]) if chunk_lhs else (lhs.shape[1], fc)

    def shard(s, cc):
        return 2 * (s % half) + jnp.where(s < half, 1 - cc[0], cc[0])

    def body(cc_ref, lhs_ref, rhs_ref, wire_ref, own_ref, recv_ref, send_buf, got, send_sems, recv_sems, got_sem):
        s = pl.program_id(0)
        x, y, c = _place()
        def send(q):
            return pltpu.make_async_remote_copy(
                src_ref=send_buf.at[q % 2], dst_ref=recv_ref.at[q], send_sem=send_sems.at[q], recv_sem=recv_sems.at[q],
                device_id=(x, y, 1 - c), device_id_type=MESH)

        def load(q):
            return pltpu.make_async_copy(recv_ref.at[q], got, got_sem.at[0])

        @pl.when(s >= half)
        def _():
            send(s - half).wait_recv()
            load(s - half).start()

        g = _dot(lhs_ref[...], rhs_ref[...], TN)

        for q in range(half):
            @pl.when(s == q)
            def _(q=q):
                if q >= 2:
                    send(q - 2).wait_send()
                send_buf[q % 2] = g
                send(q).start()

            @pl.when(s == half + q)
            def _(q=q):
                load(q).wait()
                total = g + got[...]
                wire_ref[...] = total.astype(BF16)

                @pl.when(cc_ref[1] == q)
                def _():
                    own_ref[...] = total

        @pl.when(s == N_DEV - 1)
        def _():
            for q in range(half - 2, half):
                send(q).wait_send()

    resident = lambda a: pl.BlockSpec(a.shape, lambda s, cc: (0, 0), pipeline_mode=pl.Buffered(1))
    chunked = pl.BlockSpec((t_tok, fc), lambda s, cc: (0, shard(s, cc)))
    (wire, own, _), per_comm = _carry(
        body, name=name, grid=(N_DEV,), comms=comms, prefetch=(core_chip,),
        in_specs=[chunked, resident(rhs)] if chunk_lhs else [resident(lhs), chunked],
        out_specs=[pl.BlockSpec((None,) + chunk, lambda s, cc: (jnp.maximum(s - half, 0), 0, 0)),
                   pl.BlockSpec(chunk, lambda s, cc: (0, 0)), ANY],
        out_shape=[jax.ShapeDtypeStruct((half,) + chunk, BF16), jax.ShapeDtypeStruct(chunk, F32),
                   jax.ShapeDtypeStruct((half,) + chunk, F32)],
        scratch_shapes=[pltpu.VMEM((2,) + chunk, F32), pltpu.VMEM(chunk, F32), pltpu.SemaphoreType.DMA((half,)),
                        pltpu.SemaphoreType.DMA((half,)), pltpu.SemaphoreType.DMA((1,))],
        args=(lhs, rhs))
    return wire, own, per_comm


def _proj_out_bwd(dz1, cat_t, w_out_b, comms=()):
    t_tok, d = dz1.shape
    d_mix = cat_t.shape[0]
    tm = min(512, t_tok)

    def body(dz1_ref, cat_ref, w_ref, dcat_ref, gw_ref):
        @pl.when(pl.program_id(0) == 0)
        def _():
            gw_ref[...] = jnp.zeros_like(gw_ref)

        dzb = dz1_ref[...].astype(BF16)
        dcat_ref[...] = _dot(w_ref[...], dzb, NT)
        gw_ref[...] += _dot(cat_ref[...], dzb)

    return _carry(
        body, name="proj_out_bwd", grid=(t_tok // tm,), comms=comms,
        in_specs=[pl.BlockSpec((tm, d), lambda i: (i, 0)), pl.BlockSpec((d_mix, tm), lambda i: (0, i)),
                  pl.BlockSpec((d_mix, d), lambda i: (0, 0))],
        out_specs=[pl.BlockSpec((d_mix, tm), lambda i: (0, i)), pl.BlockSpec((d_mix, d), lambda i: (0, 0))],
        out_shape=[jax.ShapeDtypeStruct((d_mix, t_tok), F32), jax.ShapeDtypeStruct((d_mix, d), F32)],
        args=(dz1, cat_t, w_out_b))


def _mixer_bwd(dcat_t, h_t, cos_t, sin_t, w_spatial, b_spatial, vln_g, vln_b, sinks, band_bias, comms=()):
    t_tok = h_t.shape[1]
    nb, n_step = t_tok // BLK, t_tok // MIX_W
    group = N_HEADS // N_KV_HEADS

    def body(sinks_ref, dcat_ref, u_ref, vg_ref, q_ref, kvc_ref, kvp_ref, cos_ref, sin_ref, cosp_ref, sinp_ref,
             wsp_ref, bsp_ref, g_ref, b_ref, bias_ref, dh_ref, dkvc_ref, dkvp_ref, gwsb_ref, gbsp_ref, gvln_ref, gsink_ref,
             dg_acc, db_acc, wm_scr, gws_ref):
        i = pl.program_id(0)

        @pl.when(i == 0)
        def _():
            gws_ref[...] = jnp.zeros_like(gws_ref)
            gbsp_ref[...] = jnp.zeros_like(gbsp_ref)
            gsink_ref[...] = jnp.zeros_like(gsink_ref)
            dg_acc[...] = jnp.zeros_like(dg_acc)
            db_acc[...] = jnp.zeros_like(db_acc)

        _mask_w_once(wsp_ref, wm_scr)

        g = g_ref[...]
        ua, ua_grad = _gelu_and_grad(u_ref[...])
        vv, vv_grad = _gelu_and_grad(vg_ref[...])
        vp, vhat, rstd = _ln_fwd_t(vv, g, b_ref[...])
        vpb = vp.astype(BF16)
        da = dcat_ref[0:D_GMLP, :]
        dmixed = da * ua
        dvp_blocks = []
        for b in range(MIX_BLOCKS):
            dvp_parts = []
            for hh in range(N_HEADS):
                rows = slice(hh * HEAD_DIM, (hh + 1) * HEAD_DIM)
                vpb_h = vpb[rows, _cols(b)]
                mixed = _dot(vpb_h, wm_scr[hh], NT) + bsp_ref[hh:hh + 1, :]
                dh_ref[COL_U + hh * HEAD_DIM:COL_U + (hh + 1) * HEAD_DIM, _cols(b)] = (
                    da[rows, _cols(b)] * mixed * ua_grad[rows, _cols(b)]).astype(BF16)
                dm = dmixed[rows, _cols(b)]
                dmb = dm.astype(BF16)
                gbsp_ref[hh:hh + 1, :] += jnp.sum(dm, axis=0, keepdims=True)
                gws_ref[hh] += _dot(dmb, vpb_h, TN)
                dvp_parts.append(_dot(dmb, wm_scr[hh]))
            dvp_blocks.append(jnp.concatenate(dvp_parts, axis=0))
        dvp = jnp.concatenate(dvp_blocks, axis=1)
        dgv, dbv = dvp * vhat, dvp
        for b in range(MIX_BLOCKS):
            dg_acc[...] += dgv[:, _cols(b)]
            db_acc[...] += dbv[:, _cols(b)]
        dh_ref[COL_V:COL_V + D_GMLP, :] = (_ln_bwd_t(dvp, vhat, rstd, g) * vv_grad).astype(BF16)

        kvc, cos, sin = kvc_ref[...], cos_ref[...], sin_ref[...]
        qr = (_rope_t(q_ref[...], cos, sin) * SCORE_SCALE).astype(BF16)
        sinks4 = [_group_lanes([jnp.full((1, BLK), sinks_ref[hh], F32) for hh in range(kv * group, (kv + 1) * group)])
                  for kv in range(N_KV_HEADS)]
        dq_blocks, dkv_cur, dkv_prev = [], [], []
        for b in range(MIX_BLOCKS):
            kv_cur, kv_prev, cosc, sinc, cosp, sinp, bias1 = _block_inputs(b, i, kvc, kvp_ref, cos, sin, cosp_ref, sinp_ref, bias_ref)
            k_t, k_n, v_t = _keys_values(kv_cur, kv_prev, cosc, sinc, cosp, sinp)
            v_n = jnp.concatenate([kv_prev[D_KV:].T, kv_cur[D_KV:].T], axis=0).astype(BF16)
            bias = _group_lanes([bias1] * group)
            dk, dv, dq_parts = [], [], []
            for kv in range(N_KV_HEADS):
                heads = range(kv * group, (kv + 1) * group)
                kv_rows = slice(kv * HEAD_DIM, (kv + 1) * HEAD_DIM)
                qs = _group_lanes([qr[hh * HEAD_DIM:(hh + 1) * HEAD_DIM, _cols(b)] for hh in heads])
                dos = _group_lanes([dcat_ref[D_GMLP + hh * HEAD_DIM:D_GMLP + (hh + 1) * HEAD_DIM, _cols(b)]
                                    for hh in heads]).astype(BF16)
                p, p_sink = _softmax_sink_t(_dot(k_n, _pad_head(qs, kv)) + bias, sinks4[kv])
                dp = _dot(v_n, _pad_head(dos, kv))
                delta = jnp.sum(p * dp, axis=0, keepdims=True)
                ds = (p * (dp - delta)).astype(BF16)
                dsink = p_sink * delta
                dq = _dot(k_t[kv_rows], ds) * SCORE_SCALE
                for j, hh in enumerate(heads):
                    gsink_ref[hh:hh + 1, :] -= dsink[:, j * BLK:(j + 1) * BLK]
                    dq_parts.append(dq[:, j * BLK:(j + 1) * BLK])
                dk.append(_dot(qs, ds, NT))
                dv.append(_dot(dos, p.astype(BF16), NT))
            dq_blocks.append(jnp.concatenate(dq_parts, axis=0))
            dk_all, dv_all = jnp.concatenate(dk, axis=0), jnp.concatenate(dv, axis=0)
            dkv_cur.append(jnp.concatenate([_rope_t(dk_all[:, BLK:], cosc, sinc, bwd=True), dv_all[:, BLK:]], axis=0))
            dkv_prev.append(jnp.concatenate([_rope_t(dk_all[:, :BLK], cosp, sinp, bwd=True), dv_all[:, :BLK]], axis=0))
        dh_ref[COL_Q:COL_Q + D_ATTN, :] = _rope_t(jnp.concatenate(dq_blocks, axis=1), cos, sin, bwd=True).astype(BF16)
        for b in range(MIX_BLOCKS):
            dkvc_ref[:, _cols(b)] = dkv_cur[b] + dkv_prev[b + 1] if b + 1 < MIX_BLOCKS else dkv_cur[b]
        dkvp_ref[...] = dkv_prev[0]

        @pl.when(i == n_step - 1)
        def _():
            causal = _causal()
            for hh in range(N_HEADS):
                gwsb_ref[hh] = jnp.where(causal, gws_ref[hh], 0.0).astype(BF16)
            gvln_ref[...] = jnp.zeros_like(gvln_ref)
            gvln_ref[0:1, :] = jnp.sum(dg_acc[...].T, axis=0, keepdims=True)
            gvln_ref[1:2, :] = jnp.sum(db_acc[...].T, axis=0, keepdims=True)

    full = lambda shape: pl.BlockSpec(shape, lambda i: (0,) * len(shape))
    return _carry(
        body, name="mixer_bwd", grid=(n_step,), comms=comms,
        in_specs=[pl.BlockSpec(memory_space=pltpu.SMEM), pl.BlockSpec((D_GMLP + D_ATTN, MIX_W), lambda i: (0, i))]
        + _h_specs() + _table_specs()
        + [full((N_HEADS, BLK, BLK)), full((N_HEADS, BLK)), full((D_GMLP, 1)), full((D_GMLP, 1)), BIAS_SPEC],
        out_specs=[pl.BlockSpec((COL_K, MIX_W), lambda i: (0, i)), pl.BlockSpec((2 * D_KV, MIX_W), lambda i: (0, i)),
                   pl.BlockSpec((2 * D_KV, BLK), lambda i: (0, (i + n_step - 1) % n_step)),
                   full((N_HEADS, BLK, BLK)), full((N_HEADS, BLK)), full((8, D_GMLP)), full((N_HEADS, LANES))],
        out_shape=[jax.ShapeDtypeStruct((COL_K, t_tok), BF16), jax.ShapeDtypeStruct((2 * D_KV, t_tok), F32),
                   jax.ShapeDtypeStruct((2 * D_KV, n_step * BLK), F32),
                   jax.ShapeDtypeStruct((N_HEADS, BLK, BLK), BF16), jax.ShapeDtypeStruct((N_HEADS, BLK), F32),
                   jax.ShapeDtypeStruct((8, D_GMLP), F32), jax.ShapeDtypeStruct((N_HEADS, LANES), F32)],
        scratch_shapes=[pltpu.VMEM((D_GMLP, BLK), F32), pltpu.VMEM((D_GMLP, BLK), F32), pltpu.VMEM((N_HEADS, BLK, BLK), BF16),
                        pltpu.VMEM((N_HEADS, BLK, BLK), F32)],
        args=(sinks, dcat_t, h_t, h_t, h_t, h_t, h_t, cos_t, sin_t, cos_t, sin_t, w_spatial, b_spatial, vln_g, vln_b, band_bias))


def _proj_in_wgrad(dh_b, dkvc_t, dkvp_t, xb, comms=()):
    t_tok, d = xb.shape
    d_main, d_kv = dh_b.shape[0], dkvc_t.shape[0]
    tm = min(1024, t_tok)

    def body(dh_ref, dkvc_ref, dkvp_ref, xb_ref, dkvb_ref, gw_ref):
        @pl.when(pl.program_id(0) == 0)
        def _():
            gw_ref[...] = jnp.zeros_like(gw_ref)

        for s in range(tm // MIX_W):
            last = slice((s + 1) * MIX_W - BLK, (s + 1) * MIX_W)
            dkvb_ref[:, s * MIX_W:(s + 1) * MIX_W - BLK] = dkvc_ref[:, s * MIX_W:(s + 1) * MIX_W - BLK].astype(BF16)
            dkvb_ref[:, last] = (dkvc_ref[:, last] + dkvp_ref[:, _cols(s)]).astype(BF16)
        gw_ref[0:d_main, :] += _dot(dh_ref[...], xb_ref[...])
        gw_ref[d_main:, :] += _dot(dkvb_ref[...], xb_ref[...])

    tok = lambda rows: pl.BlockSpec((rows, tm), lambda i: (0, i))
    return _carry(
        body, name="proj_in_wgrad", grid=(t_tok // tm,), comms=comms,
        in_specs=[tok(d_main), tok(d_kv), pl.BlockSpec((d_kv, tm // MIX_BLOCKS), lambda i: (0, i)),
                  pl.BlockSpec((tm, d), lambda i: (i, 0))],
        out_specs=[tok(d_kv), pl.BlockSpec((d_main + d_kv, d), lambda i: (0, 0))],
        out_shape=[jax.ShapeDtypeStruct((d_kv, t_tok), BF16), jax.ShapeDtypeStruct((d_main + d_kv, d), F32)],
        args=(dh_b, dkvc_t, dkvp_t, xb))


def _proj_in_dgrad(dh_b, dkv_b, dz1, w_in_t, comms=()):
    t_tok, d = dz1.shape
    d_main, d_kv = dh_b.shape[0], dkv_b.shape[0]
    tm = min(512, t_tok)

    def body(dh_ref, dkv_ref, dz1_ref, w_ref, dx_ref):
        dx_ref[...] = (ALPHA * dz1_ref[...] + _dot(dh_ref[...], w_ref[0:d_main, :], TN)
                       + _dot(dkv_ref[...], w_ref[d_main:, :], TN))

    return _carry(
        body, name="proj_in_dgrad", grid=(t_tok // tm,), comms=comms,
        in_specs=[pl.BlockSpec((d_main, tm), lambda i: (0, i)), pl.BlockSpec((d_kv, tm), lambda i: (0, i)),
                  pl.BlockSpec((tm, d), lambda i: (i, 0)), pl.BlockSpec((d_main + d_kv, d), lambda i: (0, 0))],
        out_specs=[pl.BlockSpec((tm, d), lambda i: (i, 0))],
        out_shape=[jax.ShapeDtypeStruct((t_tok, d), F32)],
        args=(dh_b, dkv_b, dz1, w_in_t))


def _adamw(w, g, m, v):
    m = ADAM_B1 * m + (1.0 - ADAM_B1) * g
    v = ADAM_B2 * v + (1.0 - ADAM_B2) * (g * g)
    m_hat = m / (1.0 - ADAM_B1 ** ADAM_STEP)
    v_hat = v / (1.0 - ADAM_B2 ** ADAM_STEP)
    delta = -ADAM_LR * (m_hat / (jnp.sqrt(v_hat) + ADAM_EPS) + ADAM_WD * w)
    return delta, m, v


def _row_tiled(name, own, recv, extra, n_out, finish, comms=()):
    r, c = own.shape
    recv = [] if recv is None else list(recv)
    k = max(len(recv), 1)
    n = max(k, -(-r // 512))
    tr, per = r // n, n // k
    blk = pl.BlockSpec((tr, c), lambda i: (i, 0))

    def body(own_ref, *refs):
        recv_refs, rest = refs[:len(recv)], refs[len(recv):]
        ins, outs = rest[:len(extra)], rest[len(extra):]

        def tile(recv_ref):
            g = own_ref[...]
            if recv_ref is not None:
                g = ((g + recv_ref[0].astype(F32)) + recv_ref[1].astype(F32)) + recv_ref[2].astype(F32)
            for o_ref, val in zip(outs, finish(g, *[a[...] for a in ins])):
                o_ref[...] = val

        if len(recv) <= 1:
            tile(recv_refs[0] if recv else None)
        else:
            for p in range(k):
                pl.when(pl.program_id(0) // per == p)(functools.partial(tile, recv_refs[p]))

    recv_specs = [pl.BlockSpec((3, tr, c), lambda i, p=p: (0, jnp.clip(i - p * per, 0, per - 1), 0)) for p in range(len(recv))]
    return _carry(
        body, name=name, grid=(n,), comms=comms,
        in_specs=[blk] + recv_specs + [blk] * len(extra),
        out_specs=[blk] * n_out, out_shape=[jax.ShapeDtypeStruct((r, c), F32)] * n_out,
        args=(own, *recv, *extra))


def _adamw_shard(name, own, recv, w, m, v, comms=()):
    def finish(g, w_t, m_t, v_t):
        return (g,) + _adamw(w_t, g, m_t, v_t)

    return _row_tiled(name, own, recv, (w, m, v), 4, finish, comms)


VEC_VLN, VEC_LN1G, VEC_LN1B, VEC_LN2G, VEC_LN2B, VEC_SINK, VEC_LOSS, VEC_BSP, VEC_ROWS = 0, 1, 2, 3, 4, 5, 6, 8, 16


def _adamw_small(parts_w, parts_vec, params):
    n = parts_w.shape[0]
    flat = [a for p in params for a in p]
    shapes = [p[0].shape for p in params]

    def grads(gw, gv):
        return [gw, gv[VEC_VLN:VEC_VLN + 1, 0:D_GMLP], gv[VEC_VLN:VEC_VLN + 1, D_GMLP:2 * D_GMLP],
                gv[VEC_BSP:VEC_BSP + N_HEADS, 0:BLK], gv[VEC_LN1G:VEC_LN1G + 1], gv[VEC_LN1B:VEC_LN1B + 1],
                gv[VEC_LN2G:VEC_LN2G + 1], gv[VEC_LN2B:VEC_LN2B + 1], gv[VEC_SINK:VEC_SINK + 1, 0:N_HEADS]]

    def body(pw_ref, pv_ref, *refs):
        ins, outs = refs[:len(flat)], refs[len(flat):]
        gw, gv = pw_ref[0].astype(F32), pv_ref[0]
        for k in range(1, n):
            gw, gv = gw + pw_ref[k].astype(F32), gv + pv_ref[k]
        for i, g in enumerate(grads(gw, gv)):
            w_ref, m_ref, v_ref = ins[3 * i:3 * i + 3]
            delta, m_new, v_new = _adamw(w_ref[...], g, m_ref[...], v_ref[...])
            for o_ref, val in zip(outs[4 * i:4 * i + 4], (g, delta, m_new, v_new)):
                o_ref[...] = val
        outs[-1][...] = gv[VEC_LOSS:VEC_LOSS + 1, 0:LANES]

    whole = lambda shape: pl.BlockSpec(shape, lambda i: (0,) * len(shape))
    res = _carry(
        body, name="adamw_small", grid=(1,),
        in_specs=[whole(parts_w.shape), whole(parts_vec.shape)] + [whole(a.shape) for a in flat],
        out_specs=[whole(s) for s in shapes for _ in range(4)] + [whole((1, LANES))],
        out_shape=[jax.ShapeDtypeStruct(s, F32) for s in shapes for _ in range(4)] + [jax.ShapeDtypeStruct((1, LANES), F32)],
        args=(parts_w, parts_vec, *flat))[0]
    return [res[4 * i:4 * i + 4] for i in range(len(params))], res[-1]


def _pair_sum(name, parts, recv, core_chip, comms=()):
    _, r, c = parts.shape
    tr = r if r <= 512 else 512

    def body(cc_ref, a_ref, b_ref, wire_ref, own_ref):
        s = a_ref[...] + b_ref[...]
        wire_ref[...] = s.astype(BF16)

        @pl.when(pl.program_id(1) == cc_ref[1])
        def _():
            own_ref[...] = s

    return _carry(
        body, name=name, grid=(r // tr, 4), prefetch=(core_chip,), comms=comms,
        in_specs=[pl.BlockSpec((None, tr, c), lambda i, q, cc: (2 * q + cc[0], i, 0)),
                  pl.BlockSpec((None, tr, c), lambda i, q, cc: (q, i, 0))],
        out_specs=[pl.BlockSpec((None, tr, c), lambda i, q, cc: (q, i, 0)), pl.BlockSpec((tr, c), lambda i, q, cc: (i, 0))],
        out_shape=[jax.ShapeDtypeStruct((4, r, c), BF16), jax.ShapeDtypeStruct((r, c), F32)],
        args=(parts, recv))


def kernel(x, positions, w_in, v_ln_g, v_ln_b, w_spatial, b_spatial, sinks, w_out, ln1_g, ln1_b, w_ff1, w_ff2, ln2_g, ln2_b, loss_target, m_w_in, m_v_ln_g, m_v_ln_b, m_w_spatial, m_b_spatial, m_sinks, m_w_out, m_ln1_g, m_ln1_b, m_w_ff1, m_w_ff2, m_ln2_g, m_ln2_b, v_w_in, v_v_ln_g, v_v_ln_b, v_w_spatial, v_b_spatial, v_sinks, v_w_out, v_ln1_g, v_ln1_b, v_w_ff1, v_w_ff2, v_ln2_g, v_ln2_b):
    _, t_tok, d = x.shape
    xi, yi, ci = _place()
    core_chip = jnp.stack([ci, 2 * xi + yi]).astype(jnp.int32)
    x2 = x.reshape(t_tok, d)
    target = loss_target.reshape(t_tok, d)
    inv_freq = ROPE_THETA ** (-jnp.arange(0, HEAD_DIM, 2, dtype=F32) / HEAD_DIM)
    wsp, bsp, sink_vec = w_spatial[0], b_spatial[0], sinks[0]
    vg_col, vb_col = v_ln_g.reshape(D_GMLP, 1), v_ln_b.reshape(D_GMLP, 1)
    big = {"in": w_in[0], "out": w_out[0], "ff1": w_ff1[0], "ff2": w_ff2[0]}

    (cos_t, sin_t), ((g_in, g_out),) = _rope_tables(
        positions, jnp.tile(inv_freq, 2).reshape(HEAD_DIM, 1),
        comms=[_gather_comm([big["in"].T.astype(BF16), big["out"].astype(BF16)])])
    w_in_t, w_out_b = g_in.reshape(D_IN, d), g_out.reshape(-1, d)
    (h_t, xb), ((w1_b,),) = _proj_in(x2, w_in_t, comms=[_gather_comm([big["ff1"].astype(BF16)])])
    band_bias = _band_bias()
    (cat_t,), ((w2_b,),) = _mixer_fwd(h_t, cos_t, sin_t, wsp, bsp, vg_col, vb_col, sink_vec, band_bias,
                                      comms=[_gather_comm([big["ff2"].astype(BF16)])])
    (xhat1, rstd1, x1b), _ = _proj_out(cat_t, x2, w_out_b, ln1_g, ln1_b)
    act_b, dpre_b, dz2b, dz1, stats = _ffn_fwd_bwd(xhat1, rstd1, x1b, target, [w1_b], [w2_b], ln1_g, ln1_b, ln2_g, ln2_b)

    (dcat_t, gw_out), _ = _proj_out_bwd(dz1, cat_t, w_out_b)
    p_out = gw_out.reshape(N_DEV, -1, d)
    wire_ff1, own_ff1, ((s_out,),) = _ffn_wgrad("ffn_wgrad1", x1b, dpre_b, False, core_chip, comms=[_sibling_comm([p_out])])
    (wire_out, own_out), _ = _pair_sum("pair_sum_out", p_out, s_out, core_chip)
    wire_ff2, own_ff2, ((r_ff1,),) = _ffn_wgrad("ffn_wgrad2", act_b, dz2b, True, core_chip, comms=[_chips_comm([wire_ff1])])
    (dh_b, dkvc_t, dkvp_t, g_wsp, g_bsp, g_vln, g_sink), ((r_ff2, r_out),) = _mixer_bwd(
        dcat_t, h_t, cos_t, sin_t, wsp, bsp, vg_col, vb_col, sink_vec, band_bias,
        comms=[_chips_comm([wire_ff2, wire_out])])
    sink_row = jnp.pad(g_sink.sum(axis=1).reshape(1, N_HEADS), ((0, 0), (0, d - N_HEADS)))
    small_vec = jnp.concatenate([g_vln[0:2].reshape(1, d), stats[0:4], sink_row, stats[4:5], jnp.zeros((1, d), F32),
                                 jnp.pad(g_bsp, ((0, 0), (0, d - BLK)))], axis=0)
    (dkv_b, gw_in_t), ((parts_w, parts_vec),) = _proj_in_wgrad(
        dh_b, dkvc_t, dkvp_t, xb, comms=[_gather_comm([g_wsp.reshape(-1, BLK), small_vec])])
    p_in = gw_in_t.reshape(N_DEV, -1, d)

    out_out, ((s_in,),) = _adamw_shard("adamw_out", own_out, [r_out], big["out"], m_w_out[0], v_w_out[0], comms=[_sibling_comm([p_in])])
    (wire_in, own_in), _ = _pair_sum("pair_sum_in", p_in, s_in, core_chip)
    (grad_x,), ((r_in,),) = _proj_in_dgrad(dh_b, dkv_b, dz1, w_in_t, comms=[_chips_comm([wire_in])])
    ff1_out, _ = _adamw_shard("adamw_ff1", own_ff1, [r_ff1], big["ff1"], m_w_ff1[0], v_w_ff1[0])
    ff2_out, _ = _adamw_shard("adamw_ff2", own_ff2, [r_ff2], big["ff2"], m_w_ff2[0], v_w_ff2[0])
    in_out_t, _ = _adamw_shard("adamw_in", own_in, [r_in], big["in"].T, m_w_in[0].T, v_w_in[0].T)
    in_out = [o.T for o in in_out_t]
    small = [(w_spatial, m_w_spatial, v_w_spatial), (v_ln_g, m_v_ln_g, v_v_ln_g), (v_ln_b, m_v_ln_b, v_v_ln_b),
             (b_spatial, m_b_spatial, v_b_spatial), (ln1_g, m_ln1_g, v_ln1_g), (ln1_b, m_ln1_b, v_ln1_b),
             (ln2_g, m_ln2_g, v_ln2_g), (ln2_b, m_ln2_b, v_ln2_b), (sinks, m_sinks, v_sinks)]
    views = [(-1, BLK), None, None, (N_HEADS, BLK)] + [None] * 5
    small_res, loss_row = _adamw_small(parts_w, parts_vec, [
        tuple(a if vw is None else a.reshape(vw) for a in p) for p, vw in zip(small, views)])
    small_out = [[o.reshape(p[0].shape) for o in res] for res, p in zip(small_res, small)]
    loss = loss_row[0, 0]

    big_out = {0: in_out, 6: out_out, 9: ff1_out, 10: ff2_out}
    small_slot = {3: 0, 1: 1, 2: 2, 4: 3, 7: 4, 8: 5, 11: 6, 12: 7, 5: 8}
    outs = [loss, grad_x.reshape(x.shape)]
    for kind in range(4):
        for wi in range(13):
            outs.append(big_out[wi][kind][None] if wi in big_out else small_out[small_slot[wi]][kind])
    return tuple(outs)
```

```python
import functools
import math

import jax
import jax.numpy as jnp
from jax import lax
from jax.experimental import pallas as pl
from jax.experimental.pallas import tpu as pltpu

F32 = jnp.float32
BF16 = jnp.bfloat16
MESH = pl.DeviceIdType.MESH

HEAD_DIM = 64
N_HEADS = 8
N_KV_HEADS = 2
BLK = 128
D_GMLP = N_HEADS * HEAD_DIM
D_ATTN = N_HEADS * HEAD_DIM
D_KV = N_KV_HEADS * HEAD_DIM
D_IN = 2 * D_GMLP + D_ATTN + 2 * D_KV
COL_U, COL_V, COL_Q, COL_K = 0, D_GMLP, 2 * D_GMLP, 2 * D_GMLP + D_ATTN
ROPE_THETA = 10000.0
LN_EPS = 1e-5
ALPHA = 2.0 ** 0.25
NEG_INF = -1e30
SCORE_SCALE = 1.0 / math.sqrt(HEAD_DIM)
ADAM_LR, ADAM_B1, ADAM_B2, ADAM_EPS, ADAM_WD, ADAM_STEP = 0.001, 0.9, 0.999, 1e-08, 0.01, 10
N_DEV = 8
LANES = 128
VMEM_LIMIT = 56 * 1024 * 1024
FFN_ROWS = 256

NT = (((1,), (1,)), ((), ()))
TN = (((0,), (0,)), ((), ()))


def _params(*sem):
    return pltpu.CompilerParams(dimension_semantics=sem, vmem_limit_bytes=VMEM_LIMIT)


def _dot(a, b, dims=None):
    if dims is None:
        return jnp.dot(a, b, preferred_element_type=F32)
    return lax.dot_general(a, b, dims, preferred_element_type=F32)


def _mean(a):
    return jnp.mean(a, axis=-1, keepdims=True)


def _ln_fwd(z, g, b):
    zc = z - _mean(z)
    rstd = lax.rsqrt(_mean(zc * zc) + LN_EPS)
    xhat = zc * rstd
    return xhat * g + b, xhat, rstd


def _ln_bwd(dy, xhat, rstd, g):
    dxhat = dy * g
    return rstd * (dxhat - _mean(dxhat) - xhat * _mean(dxhat * xhat))


_GELU_C = math.sqrt(2.0 / math.pi)


def _gelu(x):
    t = jnp.tanh(_GELU_C * (x + 0.044715 * (x * x * x)))
    return 0.5 * x * (1.0 + t)


def _gelu_and_grad(x):
    x2 = x * x
    t = jnp.tanh(_GELU_C * (x + 0.044715 * (x2 * x)))
    hx, ht = 0.5 * x, 0.5 * (1.0 + t)
    return x * ht, ht + hx * (1.0 - t * t) * (_GELU_C * (1.0 + 3.0 * 0.044715 * x2))


def _mean0(a):
    return jnp.mean(a, axis=0, keepdims=True)


def _ln_fwd_t(z, g, b):
    zc = z - _mean0(z)
    rstd = lax.rsqrt(_mean0(zc * zc) + LN_EPS)
    xhat = zc * rstd
    return xhat * g + b, xhat, rstd


def _ln_bwd_t(dy, xhat, rstd, g):
    dxhat = dy * g
    return rstd * (dxhat - _mean0(dxhat) - xhat * _mean0(dxhat * xhat))


def _rope_t(t, cos, sin_signed, bwd=False):
    half = HEAD_DIM // 2
    outs = []
    for r in range(0, t.shape[0], HEAD_DIM):
        th = t[r:r + HEAD_DIM]
        sw = jnp.concatenate([th[half:], th[:half]], axis=0) * sin_signed
        outs.append(th * cos - sw if bwd else th * cos + sw)
    return jnp.concatenate(outs, axis=0)


ANY = pl.BlockSpec(memory_space=pl.ANY)


def _place():
    return lax.axis_index("x"), lax.axis_index("y"), lax.axis_index("c")


class _Comm:
    def __init__(self, ins, outs, sems, start, finish):
        self.ins, self.outs, self.sems, self.start, self.finish = ins, outs, sems, start, finish


def _gather_comm(arrs):
    n = len(arrs)

    def parts(ins, outs, sems):
        send_sems, recv_sems, local_sems = sems
        x, y, c = _place()
        me, sibling = (x, y, c), (x, y, 1 - c)
        chips = [(1 - x, y), (x, 1 - y), (1 - x, 1 - y)]

        def copy(a, k, block, to, src=None):
            px, py, pc = block
            dst = outs[a].at[4 * px + 2 * py + pc]
            return pltpu.make_async_remote_copy(
                src_ref=dst if src is None else src, dst_ref=dst,
                send_sem=send_sems.at[a, k], recv_sem=recv_sems.at[a, k], device_id=to, device_id_type=MESH)

        mine = [pltpu.make_async_copy(ins[a], outs[a].at[4 * x + 2 * y + c], local_sems.at[a]) for a in range(n)]
        first = []
        for a in range(n):
            first.append(copy(a, 0, me, sibling, src=ins[a]))
            first += [copy(a, 1 + j, me, (*chip, c), src=ins[a]) for j, chip in enumerate(chips)]
        return copy, mine, first, me, sibling, chips, c

    def start(ins, outs, sems):
        _, mine, first, *_ = parts(ins, outs, sems)
        for cp in mine + first:
            cp.start()

    def finish(ins, outs, sems):
        copy, mine, first, me, sibling, chips, c = parts(ins, outs, sems)
        passed = []
        for j, chip in enumerate(chips):
            for a in range(n):
                copy(a, 1 + j, (*chip, c), me).wait_recv()
                fwd = copy(a, 4 + j, (*chip, c), sibling)
                fwd.start()
                passed.append(fwd)
        for a in range(n):
            copy(a, 0, sibling, me).wait_recv()
        for j, chip in enumerate(chips):
            for a in range(n):
                copy(a, 4 + j, (*chip, 1 - c), me).wait_recv()
        for cp in first + passed:
            cp.wait_send()
        for cp in mine:
            cp.wait()

    return _Comm(list(arrs), [jax.ShapeDtypeStruct((N_DEV,) + a.shape, a.dtype) for a in arrs],
                 [pltpu.SemaphoreType.DMA((n, 7)), pltpu.SemaphoreType.DMA((n, 7)), pltpu.SemaphoreType.DMA((n,))],
                 start, finish)


def _sibling_comm(parts):
    n = len(parts)

    def copies(ins, outs, sems):
        x, y, c = _place()
        return [pltpu.make_async_remote_copy(
            src_ref=ins[a].at[2 * q + (1 - c)], dst_ref=outs[a].at[q],
            send_sem=sems[0].at[a, q], recv_sem=sems[1].at[a, q],
            device_id=(x, y, 1 - c), device_id_type=MESH) for a in range(n) for q in range(4)]

    return _Comm(list(parts), [jax.ShapeDtypeStruct((4,) + p.shape[1:], p.dtype) for p in parts],
                 [pltpu.SemaphoreType.DMA((n, 4)), pltpu.SemaphoreType.DMA((n, 4))],
                 lambda *r: [cp.start() for cp in copies(*r)], lambda *r: [cp.wait() for cp in copies(*r)])


def _chips_comm(chip_parts, rows=None):
    n = len(chip_parts)
    r0, nr = (0, None) if rows is None else rows

    def copies(ins, outs, sems):
        x, y, c = _place()
        chips = [(1 - x, y), (x, 1 - y), (1 - x, 1 - y)]
        src = lambda a, q: ins[a].at[q] if rows is None else ins[a].at[q, pl.ds(r0, nr)]
        return [pltpu.make_async_remote_copy(
            src_ref=src(a, 2 * px + py), dst_ref=outs[a].at[k],
            send_sem=sems[0].at[a, k], recv_sem=sems[1].at[a, k],
            device_id=(px, py, c), device_id_type=MESH) for a in range(n) for k, (px, py) in enumerate(chips)]

    shape = lambda p: (3,) + p.shape[1:] if rows is None else (3, nr) + p.shape[2:]
    return _Comm(list(chip_parts), [jax.ShapeDtypeStruct(shape(p), p.dtype) for p in chip_parts],
                 [pltpu.SemaphoreType.DMA((n, 3)), pltpu.SemaphoreType.DMA((n, 3))],
                 lambda *r: [cp.start() for cp in copies(*r)], lambda *r: [cp.wait() for cp in copies(*r)])


def _carry(body, *, name, grid, in_specs, out_specs, out_shape, args, comms=(), scratch_shapes=(), prefetch=()):
    n_pre, n_in, n_out, n_scr = len(prefetch), len(in_specs), len(out_specs), len(scratch_shapes)
    c_ins = [a for cm in comms for a in cm.ins]
    c_outs = [s for cm in comms for s in cm.outs]
    c_sems = [s for cm in comms for s in cm.sems]

    def wrapped(*refs):
        pre, refs = refs[:n_pre], refs[n_pre:]
        ins, refs = refs[:n_in], refs[n_in:]
        cins, refs = refs[:len(c_ins)], refs[len(c_ins):]
        outs, refs = refs[:n_out], refs[n_out:]
        couts, refs = refs[:len(c_outs)], refs[len(c_outs):]
        scr, sems = refs[:n_scr], refs[n_scr:]
        groups, i0, o0, s0 = [], 0, 0, 0
        for cm in comms:
            groups.append((cm, cins[i0:i0 + len(cm.ins)], couts[o0:o0 + len(cm.outs)], sems[s0:s0 + len(cm.sems)]))
            i0, o0, s0 = i0 + len(cm.ins), o0 + len(cm.outs), s0 + len(cm.sems)
        first = pl.program_id(0) == 0
        last = pl.program_id(0) == grid[0] - 1
        for ax in range(1, len(grid)):
            first = first & (pl.program_id(ax) == 0)
            last = last & (pl.program_id(ax) == grid[ax] - 1)
        if comms:
            @pl.when(first)
            def _():
                for cm, ci, co, cs in groups:
                    cm.start(ci, co, cs)
        body(*pre, *ins, *outs, *scr)
        if comms:
            @pl.when(last)
            def _():
                for cm, ci, co, cs in groups:
                    cm.finish(ci, co, cs)

    grid_spec = pltpu.PrefetchScalarGridSpec(
        num_scalar_prefetch=n_pre, grid=grid,
        in_specs=list(in_specs) + [ANY] * len(c_ins), out_specs=list(out_specs) + [ANY] * len(c_outs),
        scratch_shapes=list(scratch_shapes) + c_sems)
    res = pl.pallas_call(
        wrapped, name=name, grid_spec=grid_spec, out_shape=list(out_shape) + c_outs,
        compiler_params=_params(*(["arbitrary"] * len(grid))),
    )(*prefetch, *args, *c_ins)
    outs, rest, per_comm = res[:n_out], res[n_out:], []
    for cm in comms:
        per_comm.append(rest[:len(cm.outs)])
        rest = rest[len(cm.outs):]
    return outs, per_comm


def _rope_tables(pos_row, inv_freq_col, comms=()):
    t_tok = pos_row.shape[1]
    tm = min(512, t_tok)

    def body(pos_ref, invf_ref, cos_ref, sin_ref):
        ang = pos_ref[...].astype(F32) * invf_ref[...]
        row = lax.broadcasted_iota(jnp.int32, ang.shape, 0)
        cos_ref[...] = jnp.cos(ang)
        sin_ref[...] = jnp.sin(ang) * jnp.where(row < HEAD_DIM // 2, -1.0, 1.0)

    return _carry(
        body, name="rope_tables", grid=(t_tok // tm,), comms=comms,
        in_specs=[pl.BlockSpec((1, tm), lambda i: (0, i)), pl.BlockSpec((HEAD_DIM, 1), lambda i: (0, 0))],
        out_specs=[pl.BlockSpec((HEAD_DIM, tm), lambda i: (0, i))] * 2,
        out_shape=[jax.ShapeDtypeStruct((HEAD_DIM, t_tok), F32)] * 2,
        args=(pos_row, inv_freq_col))


def _proj_in(x2, w_in_t, comms=()):
    t_tok, d = x2.shape
    d_in = w_in_t.shape[0]
    tm = min(512, t_tok)

    def body(x_ref, w_ref, h_ref, xb_ref):
        xb = x_ref[...].astype(BF16)
        xb_ref[...] = xb
        h_ref[...] = _dot(w_ref[...], xb, NT)

    return _carry(
        body, name="proj_in", grid=(t_tok // tm,), comms=comms,
        in_specs=[pl.BlockSpec((tm, d), lambda i: (i, 0)), pl.BlockSpec((d_in, d), lambda i: (0, 0))],
        out_specs=[pl.BlockSpec((d_in, tm), lambda i: (0, i)), pl.BlockSpec((tm, d), lambda i: (i, 0))],
        out_shape=[jax.ShapeDtypeStruct((d_in, t_tok), F32), jax.ShapeDtypeStruct((t_tok, d), BF16)],
        args=(x2, w_in_t))


MIX_BLOCKS = 2
MIX_W = MIX_BLOCKS * BLK


def _prev_block(i):
    return jnp.maximum(MIX_BLOCKS * i - 1, 0)


def _h_specs():
    kv_row = COL_K // (2 * D_KV)
    return [
        pl.BlockSpec((D_GMLP, MIX_W), lambda i: (0, i)),
        pl.BlockSpec((D_GMLP, MIX_W), lambda i: (1, i)),
        pl.BlockSpec((D_ATTN, MIX_W), lambda i: (2, i)),
        pl.BlockSpec((2 * D_KV, MIX_W), lambda i: (kv_row, i)),
        pl.BlockSpec((2 * D_KV, BLK), lambda i: (kv_row, _prev_block(i))),
    ]


def _table_specs():
    return [
        pl.BlockSpec((HEAD_DIM, MIX_W), lambda i: (0, i)),
        pl.BlockSpec((HEAD_DIM, MIX_W), lambda i: (0, i)),
        pl.BlockSpec((HEAD_DIM, BLK), lambda i: (0, _prev_block(i))),
        pl.BlockSpec((HEAD_DIM, BLK), lambda i: (0, _prev_block(i))),
    ]


def _cols(b):
    return slice(b * BLK, (b + 1) * BLK)


def _block_inputs(b, i, kvc, kvp_ref, cos, sin, cosp_ref, sinp_ref, bias_ref):
    if b == 0:
        kv_prev, cos_prev, sin_prev, bias = kvp_ref[...], cosp_ref[...], sinp_ref[...], bias_ref[jnp.minimum(i, 1)]
    else:
        kv_prev, cos_prev, sin_prev, bias = kvc[:, _cols(b - 1)], cos[:, _cols(b - 1)], sin[:, _cols(b - 1)], bias_ref[1]
    return kvc[:, _cols(b)], kv_prev, cos[:, _cols(b)], sin[:, _cols(b)], cos_prev, sin_prev, bias


def _band_bias():
    ki = lax.broadcasted_iota(jnp.int32, (2, 2 * BLK, BLK), 1)
    qi = lax.broadcasted_iota(jnp.int32, (2, 2 * BLK, BLK), 2)
    later = lax.broadcasted_iota(jnp.int32, (2, 2 * BLK, BLK), 0) > 0
    dist = qi + BLK - ki
    return jnp.where((dist >= 0) & (dist < BLK) & ((ki >= BLK) | later), 0.0, NEG_INF).astype(F32)


BIAS_SPEC = pl.BlockSpec((2, 2 * BLK, BLK), lambda i: (0, 0, 0))


def _keys_values(kvc, kvp, cosc, sinc, cosp, sinp):
    kp, kc = _rope_t(kvp[:D_KV], cosp, sinp), _rope_t(kvc[:D_KV], cosc, sinc)
    k_t = jnp.concatenate([kp, kc], axis=1).astype(BF16)
    k_n = jnp.concatenate([kp.T, kc.T], axis=0).astype(BF16)
    v_t = jnp.concatenate([kvp[D_KV:], kvc[D_KV:]], axis=1).astype(BF16)
    return k_t, k_n, v_t


def _pad_head(th, kv):
    z = jnp.zeros_like(th)
    return jnp.concatenate([th, z] if kv == 0 else [z, th], axis=0)


def _group_lanes(parts):
    return jnp.concatenate(parts, axis=1)


def _softmax_sink_t(s, sink):
    m = jnp.maximum(jnp.max(s, axis=0, keepdims=True), sink)
    e = jnp.exp(s - m)
    es = jnp.exp(sink - m)
    r = 1.0 / (jnp.sum(e, axis=0, keepdims=True) + es)
    return e * r, es * r


def _causal():
    row = lax.broadcasted_iota(jnp.int32, (BLK, BLK), 0)
    col = lax.broadcasted_iota(jnp.int32, (BLK, BLK), 1)
    return row >= col


def _mask_w_once(wsp_ref, wm_scr):
    @pl.when(pl.program_id(0) == 0)
    def _():
        causal = _causal()
        for hh in range(N_HEADS):
            wm_scr[hh] = jnp.where(causal, wsp_ref[hh], 0.0).astype(BF16)


def _mixer_fwd(h_t, cos_t, sin_t, w_spatial, b_spatial, vln_g, vln_b, sinks, band_bias, comms=()):
    t_tok = h_t.shape[1]
    group = N_HEADS // N_KV_HEADS

    def body(sinks_ref, u_ref, vg_ref, q_ref, kvc_ref, kvp_ref, cos_ref, sin_ref, cosp_ref, sinp_ref,
             wsp_ref, bsp_ref, g_ref, b_ref, bias_ref, cat_ref, wm_scr):
        i = pl.program_id(0)
        _mask_w_once(wsp_ref, wm_scr)
        ua = _gelu(u_ref[...])
        vp, _, _ = _ln_fwd_t(_gelu(vg_ref[...]), g_ref[...], b_ref[...])
        vpb = vp.astype(BF16)
        for b in range(MIX_BLOCKS):
            for hh in range(N_HEADS):
                rows = slice(hh * HEAD_DIM, (hh + 1) * HEAD_DIM)
                mixed = _dot(vpb[rows, _cols(b)], wm_scr[hh], NT) + bsp_ref[hh:hh + 1, :]
                cat_ref[rows, _cols(b)] = (ua[rows, _cols(b)] * mixed).astype(BF16)

        kvc, cos, sin = kvc_ref[...], cos_ref[...], sin_ref[...]
        qr = (_rope_t(q_ref[...], cos, sin) * SCORE_SCALE).astype(BF16)
        sinks4 = [_group_lanes([jnp.full((1, BLK), sinks_ref[hh], F32) for hh in range(kv * group, (kv + 1) * group)])
                  for kv in range(N_KV_HEADS)]
        for b in range(MIX_BLOCKS):
            kv_cur, kv_prev, cosc, sinc, cosp, sinp, bias1 = _block_inputs(b, i, kvc, kvp_ref, cos, sin, cosp_ref, sinp_ref, bias_ref)
            _, k_n, v_t = _keys_values(kv_cur, kv_prev, cosc, sinc, cosp, sinp)
            bias = _group_lanes([bias1] * group)
            for kv in range(N_KV_HEADS):
                heads = range(kv * group, (kv + 1) * group)
                qs = _group_lanes([qr[hh * HEAD_DIM:(hh + 1) * HEAD_DIM, _cols(b)] for hh in heads])
                p, _ = _softmax_sink_t(_dot(k_n, _pad_head(qs, kv)) + bias, sinks4[kv])
                o = _dot(v_t[kv * HEAD_DIM:(kv + 1) * HEAD_DIM], p.astype(BF16)).astype(BF16)
                for j, hh in enumerate(heads):
                    cat_ref[D_GMLP + hh * HEAD_DIM:D_GMLP + (hh + 1) * HEAD_DIM, _cols(b)] = o[:, j * BLK:(j + 1) * BLK]

    full = lambda shape: pl.BlockSpec(shape, lambda i: (0,) * len(shape))
    return _carry(
        body, name="mixer_fwd", grid=(t_tok // MIX_W,), comms=comms,
        in_specs=[pl.BlockSpec(memory_space=pltpu.SMEM)] + _h_specs() + _table_specs() + [
            full((N_HEADS, BLK, BLK)), full((N_HEADS, BLK)), full((D_GMLP, 1)), full((D_GMLP, 1)), BIAS_SPEC],
        out_specs=[pl.BlockSpec((D_GMLP + D_ATTN, MIX_W), lambda i: (0, i))],
        out_shape=[jax.ShapeDtypeStruct((D_GMLP + D_ATTN, t_tok), BF16)],
        scratch_shapes=[pltpu.VMEM((N_HEADS, BLK, BLK), BF16)],
        args=(sinks, h_t, h_t, h_t, h_t, h_t, cos_t, sin_t, cos_t, sin_t, w_spatial, b_spatial, vln_g, vln_b, band_bias))


def _proj_out(cat_t, x2, w_out_b, ln1_g, ln1_b, comms=()):
    t_tok, d = x2.shape
    tm = min(512, t_tok)

    def body(cat_ref, x_ref, w_ref, g_ref, b_ref, xhat_ref, rstd_ref, x1b_ref):
        x1, xhat, rstd = _ln_fwd(ALPHA * x_ref[...] + _dot(cat_ref[...], w_ref[...], TN), g_ref[...], b_ref[...])
        xhat_ref[...] = xhat
        rstd_ref[...] = rstd
        x1b_ref[...] = x1.astype(BF16)

    tok = lambda w: pl.BlockSpec((tm, w), lambda i: (i, 0))
    vec = pl.BlockSpec((1, d), lambda i: (0, 0))
    return _carry(
        body, name="proj_out", grid=(t_tok // tm,), comms=comms,
        in_specs=[pl.BlockSpec((cat_t.shape[0], tm), lambda i: (0, i)), tok(d), pl.BlockSpec(w_out_b.shape, lambda i: (0, 0)), vec, vec],
        out_specs=[tok(d), tok(1), tok(d)],
        out_shape=[jax.ShapeDtypeStruct((t_tok, d), F32), jax.ShapeDtypeStruct((t_tok, 1), F32), jax.ShapeDtypeStruct((t_tok, d), BF16)],
        args=(cat_t, x2, w_out_b, ln1_g, ln1_b))


def _ffn_fwd_bwd(xhat1, rstd1, x1b, target, w1_parts, w2_parts, ln1_g, ln1_b, ln2_g, ln2_b):
    t_tok, d = xhat1.shape
    n_part = len(w1_parts)
    n_chunk, _, fp = w1_parts[0].shape
    f = n_chunk * n_part * fp
    tm = min(FFN_ROWS, t_tok)

    def body(xhat1_ref, rstd1_ref, x1b_ref, tgt_ref, *refs):
        w1_hbm, w2_hbm = refs[:n_part], refs[n_part:2 * n_part]
        (g1_ref, b1_ref, g2_ref, b2_ref, act_ref, dpre_ref, dz2b_ref, dz1_ref, stats_ref,
         r_scr, w1_ref, w2_ref, w_sems) = refs[2 * n_part:]

        @pl.when(pl.program_id(0) == 0)
        def _():
            stats_ref[...] = jnp.zeros_like(stats_ref)
            loads = []
            for j in range(n_chunk):
                for p in range(n_part):
                    units = pl.ds((j * n_part + p) * fp, fp)
                    loads.append(pltpu.make_async_copy(w1_hbm[p].at[j], w1_ref.at[:, units], w_sems.at[0, p, j]))
                    loads.append(pltpu.make_async_copy(w2_hbm[p].at[j], w2_ref.at[units, :], w_sems.at[1, p, j]))
            for cp in loads:
                cp.start()
            for cp in loads:
                cp.wait()

        g1, g2 = g1_ref[...], g2_ref[...]
        xhat1 = xhat1_ref[...]
        r_scr[...] = jnp.maximum(_dot(x1b_ref[...], w1_ref[...]), 0.0)
        r = r_scr[...]
        act = (r * r).astype(BF16)
        act_ref[...] = act
        ff = _dot(act, w2_ref[...])
        y, xhat2, rstd2 = _ln_fwd(ALPHA * (xhat1 * g1 + b1_ref[...]) + ff, g2, b2_ref[...])
        diff = y - tgt_ref[...]
        loss = 0.5 * jnp.sum(jnp.sum(diff * diff, axis=-1, keepdims=True) / d, axis=0, keepdims=True)
        dy = diff / d
        dz2 = _ln_bwd(dy, xhat2, rstd2, g2)
        dz2b = dz2.astype(BF16)
        dz2b_ref[...] = dz2b
        dpre = (_dot(dz2b, w2_ref[...], NT) * (2.0 * r_scr[...])).astype(BF16)
        dpre_ref[...] = dpre
        dx1 = ALPHA * dz2 + _dot(dpre, w1_ref[...], NT)
        dz1_ref[...] = _ln_bwd(dx1, xhat1, rstd1_ref[...], g1)
        stats_ref[0:1, :] += jnp.sum(dx1 * xhat1, axis=0, keepdims=True)
        stats_ref[1:2, :] += jnp.sum(dx1, axis=0, keepdims=True)
        stats_ref[2:3, :] += jnp.sum(dy * xhat2, axis=0, keepdims=True)
        stats_ref[3:4, :] += jnp.sum(dy, axis=0, keepdims=True)
        stats_ref[4:5, :] += jnp.broadcast_to(loss, (1, d))

    tok = lambda w: pl.BlockSpec((tm, w), lambda i: (i, 0))
    vec = pl.BlockSpec((1, d), lambda i: (0, 0))
    return _carry(
        body, name="ffn_fwd_bwd", grid=(t_tok // tm,),
        in_specs=[tok(d), tok(1), tok(d), tok(d)] + [ANY] * (2 * n_part) + [vec, vec, vec, vec],
        out_specs=[tok(f), tok(f), tok(d), tok(d), pl.BlockSpec((8, d), lambda i: (0, 0))],
        out_shape=[jax.ShapeDtypeStruct((t_tok, f), BF16), jax.ShapeDtypeStruct((t_tok, f), BF16),
                   jax.ShapeDtypeStruct((t_tok, d), BF16), jax.ShapeDtypeStruct((t_tok, d), F32), jax.ShapeDtypeStruct((8, d), F32)],
        scratch_shapes=[pltpu.VMEM((tm, f), F32), pltpu.VMEM((d, f), BF16), pltpu.VMEM((f, d), BF16),
                        pltpu.SemaphoreType.DMA((2, n_part, n_chunk))],
        args=(xhat1, rstd1, x1b, target, *w1_parts, *w2_parts, ln1_g, ln1_b, ln2_g, ln2_b))[0]


def _ffn_wgrad(name, lhs, rhs, chunk_lhs, core_chip, comms=()):
    t_tok = lhs.shape[0]
    half = N_DEV // 2
    fc = (lhs if chunk_lhs else rhs).shape[1] // N_DEV
    chunk = (fc, rhs.shape[1]) if chunk_lhs else (lhs.shape[1], fc)

    def shard(s, cc):
        return 2 * (s % half) + jnp.where(s < half, 1 - cc[0], cc[0])

    def body(cc_ref, lhs_ref, rhs_ref, wire_ref, own_ref, recv_ref, send_buf, got, send_sems, recv_sems, got_sem):
        s = pl.program_id(0)
        x, y, c = _place()
        def send(q):
            return pltpu.make_async_remote_copy(
                src_ref=send_buf.at[q % 2], dst_ref=recv_ref.at[q], send_sem=send_sems.at[q], recv_sem=recv_sems.at[q],
                device_id=(x, y, 1 - c), device_id_type=MESH)

        def load(q):
            return pltpu.make_async_copy(recv_ref.at[q], got, got_sem.at[0])

        @pl.when(s >= half)
        def _():
            send(s - half).wait_recv()
            load(s - half).start()

        g = _dot(lhs_ref[...], rhs_ref[...], TN)

        for q in range(half):
            @pl.when(s == q)
            def _(q=q):
                if q >= 2:
                    send(q - 2).wait_send()
                send_buf[q % 2] = g
                send(q).start()

            @pl.when(s == half + q)
            def _(q=q):
                load(q).wait()
                total = g + got[...]
                wire_ref[...] = total.astype(BF16)

                @pl.when(cc_ref[1] == q)
                def _():
                    own_ref[...] = total

        @pl.when(s == N_DEV - 1)
        def _():
            for q in range(half - 2, half):
                send(q).wait_send()

    resident = lambda a: pl.BlockSpec(a.shape, lambda s, cc: (0, 0), pipeline_mode=pl.Buffered(1))
    chunked = pl.BlockSpec((t_tok, fc), lambda s, cc: (0, shard(s, cc)))
    (wire, own, _), per_comm = _carry(
        body, name=name, grid=(N_DEV,), comms=comms, prefetch=(core_chip,),
        in_specs=[chunked, resident(rhs)] if chunk_lhs else [resident(lhs), chunked],
        out_specs=[pl.BlockSpec((None,) + chunk, lambda s, cc: (jnp.maximum(s - half, 0), 0, 0)),
                   pl.BlockSpec(chunk, lambda s, cc: (0, 0)), ANY],
        out_shape=[jax.ShapeDtypeStruct((half,) + chunk, BF16), jax.ShapeDtypeStruct(chunk, F32),
                   jax.ShapeDtypeStruct((half,) + chunk, F32)],
        scratch_shapes=[pltpu.VMEM((2,) + chunk, F32), pltpu.VMEM(chunk, F32), pltpu.SemaphoreType.DMA((half,)),
                        pltpu.SemaphoreType.DMA((half,)), pltpu.SemaphoreType.DMA((1,))],
        args=(lhs, rhs))
    return wire, own, per_comm


def _proj_out_bwd(dz1, cat_t, w_out_b, comms=()):
    t_tok, d = dz1.shape
    d_mix = cat_t.shape[0]
    tm = min(512, t_tok)

    def body(dz1_ref, cat_ref, w_ref, dcat_ref, gw_ref):
        @pl.when(pl.program_id(0) == 0)
        def _():
            gw_ref[...] = jnp.zeros_like(gw_ref)

        dzb = dz1_ref[...].astype(BF16)
        dcat_ref[...] = _dot(w_ref[...], dzb, NT)
        gw_ref[...] += _dot(cat_ref[...], dzb)

    return _carry(
        body, name="proj_out_bwd", grid=(t_tok // tm,), comms=comms,
        in_specs=[pl.BlockSpec((tm, d), lambda i: (i, 0)), pl.BlockSpec((d_mix, tm), lambda i: (0, i)),
                  pl.BlockSpec((d_mix, d), lambda i: (0, 0))],
        out_specs=[pl.BlockSpec((d_mix, tm), lambda i: (0, i)), pl.BlockSpec((d_mix, d), lambda i: (0, 0))],
        out_shape=[jax.ShapeDtypeStruct((d_mix, t_tok), F32), jax.ShapeDtypeStruct((d_mix, d), F32)],
        args=(dz1, cat_t, w_out_b))


def _mixer_bwd(dcat_t, h_t, cos_t, sin_t, w_spatial, b_spatial, vln_g, vln_b, sinks, band_bias, comms=()):
    t_tok = h_t.shape[1]
    nb, n_step = t_tok // BLK, t_tok // MIX_W
    group = N_HEADS // N_KV_HEADS

    def body(sinks_ref, dcat_ref, u_ref, vg_ref, q_ref, kvc_ref, kvp_ref, cos_ref, sin_ref, cosp_ref, sinp_ref,
             wsp_ref, bsp_ref, g_ref, b_ref, bias_ref, dh_ref, dkvc_ref, dkvp_ref, gwsb_ref, gbsp_ref, gvln_ref, gsink_ref,
             dg_acc, db_acc, wm_scr, gws_ref):
        i = pl.program_id(0)

        @pl.when(i == 0)
        def _():
            gws_ref[...] = jnp.zeros_like(gws_ref)
            gbsp_ref[...] = jnp.zeros_like(gbsp_ref)
            gsink_ref[...] = jnp.zeros_like(gsink_ref)
            dg_acc[...] = jnp.zeros_like(dg_acc)
            db_acc[...] = jnp.zeros_like(db_acc)

        _mask_w_once(wsp_ref, wm_scr)

        g = g_ref[...]
        ua, ua_grad = _gelu_and_grad(u_ref[...])
        vv, vv_grad = _gelu_and_grad(vg_ref[...])
        vp, vhat, rstd = _ln_fwd_t(vv, g, b_ref[...])
        vpb = vp.astype(BF16)
        da = dcat_ref[0:D_GMLP, :]
        dmixed = da * ua
        dvp_blocks = []
        for b in range(MIX_BLOCKS):
            dvp_parts = []
            for hh in range(N_HEADS):
                rows = slice(hh * HEAD_DIM, (hh + 1) * HEAD_DIM)
                vpb_h = vpb[rows, _cols(b)]
                mixed = _dot(vpb_h, wm_scr[hh], NT) + bsp_ref[hh:hh + 1, :]
                dh_ref[COL_U + hh * HEAD_DIM:COL_U + (hh + 1) * HEAD_DIM, _cols(b)] = (
                    da[rows, _cols(b)] * mixed * ua_grad[rows, _cols(b)]).astype(BF16)
                dm = dmixed[rows, _cols(b)]
                dmb = dm.astype(BF16)
                gbsp_ref[hh:hh + 1, :] += jnp.sum(dm, axis=0, keepdims=True)
                gws_ref[hh] += _dot(dmb, vpb_h, TN)
                dvp_parts.append(_dot(dmb, wm_scr[hh]))
            dvp_blocks.append(jnp.concatenate(dvp_parts, axis=0))
        dvp = jnp.concatenate(dvp_blocks, axis=1)
        dgv, dbv = dvp * vhat, dvp
        for b in range(MIX_BLOCKS):
            dg_acc[...] += dgv[:, _cols(b)]
            db_acc[...] += dbv[:, _cols(b)]
        dh_ref[COL_V:COL_V + D_GMLP, :] = (_ln_bwd_t(dvp, vhat, rstd, g) * vv_grad).astype(BF16)

        kvc, cos, sin = kvc_ref[...], cos_ref[...], sin_ref[...]
        qr = (_rope_t(q_ref[...], cos, sin) * SCORE_SCALE).astype(BF16)
        sinks4 = [_group_lanes([jnp.full((1, BLK), sinks_ref[hh], F32) for hh in range(kv * group, (kv + 1) * group)])
                  for kv in range(N_KV_HEADS)]
        dq_blocks, dkv_cur, dkv_prev = [], [], []
        for b in range(MIX_BLOCKS):
            kv_cur, kv_prev, cosc, sinc, cosp, sinp, bias1 = _block_inputs(b, i, kvc, kvp_ref, cos, sin, cosp_ref, sinp_ref, bias_ref)
            k_t, k_n, v_t = _keys_values(kv_cur, kv_prev, cosc, sinc, cosp, sinp)
            v_n = jnp.concatenate([kv_prev[D_KV:].T, kv_cur[D_KV:].T], axis=0).astype(BF16)
            bias = _group_lanes([bias1] * group)
            dk, dv, dq_parts = [], [], []
            for kv in range(N_KV_HEADS):
                heads = range(kv * group, (kv + 1) * group)
                kv_rows = slice(kv * HEAD_DIM, (kv + 1) * HEAD_DIM)
                qs = _group_lanes([qr[hh * HEAD_DIM:(hh + 1) * HEAD_DIM, _cols(b)] for hh in heads])
                dos = _group_lanes([dcat_ref[D_GMLP + hh * HEAD_DIM:D_GMLP + (hh + 1) * HEAD_DIM, _cols(b)]
                                    for hh in heads]).astype(BF16)
                p, p_sink = _softmax_sink_t(_dot(k_n, _pad_head(qs, kv)) + bias, sinks4[kv])
                dp = _dot(v_n, _pad_head(dos, kv))
                delta = jnp.sum(p * dp, axis=0, keepdims=True)
                ds = (p * (dp - delta)).astype(BF16)
                dsink = p_sink * delta
                dq = _dot(k_t[kv_rows], ds) * SCORE_SCALE
                for j, hh in enumerate(heads):
                    gsink_ref[hh:hh + 1, :] -= dsink[:, j * BLK:(j + 1) * BLK]
                    dq_parts.append(dq[:, j * BLK:(j + 1) * BLK])
                dk.append(_dot(qs, ds, NT))
                dv.append(_dot(dos, p.astype(BF16), NT))
            dq_blocks.append(jnp.concatenate(dq_parts, axis=0))
            dk_all, dv_all = jnp.concatenate(dk, axis=0), jnp.concatenate(dv, axis=0)
            dkv_cur.append(jnp.concatenate([_rope_t(dk_all[:, BLK:], cosc, sinc, bwd=True), dv_all[:, BLK:]], axis=0))
            dkv_prev.append(jnp.concatenate([_rope_t(dk_all[:, :BLK], cosp, sinp, bwd=True), dv_all[:, :BLK]], axis=0))
        dh_ref[COL_Q:COL_Q + D_ATTN, :] = _rope_t(jnp.concatenate(dq_blocks, axis=1), cos, sin, bwd=True).astype(BF16)
        for b in range(MIX_BLOCKS):
            dkvc_ref[:, _cols(b)] = dkv_cur[b] + dkv_prev[b + 1] if b + 1 < MIX_BLOCKS else dkv_cur[b]
        dkvp_ref[...] = dkv_prev[0]

        @pl.when(i == n_step - 1)
        def _():
            causal = _causal()
            for hh in range(N_HEADS):
                gwsb_ref[hh] = jnp.where(causal, gws_ref[hh], 0.0).astype(BF16)
            gvln_ref[...] = jnp.zeros_like(gvln_ref)
            gvln_ref[0:1, :] = jnp.sum(dg_acc[...].T, axis=0, keepdims=True)
            gvln_ref[1:2, :] = jnp.sum(db_acc[...].T, axis=0, keepdims=True)

    full = lambda shape: pl.BlockSpec(shape, lambda i: (0,) * len(shape))
    return _carry(
        body, name="mixer_bwd", grid=(n_step,), comms=comms,
        in_specs=[pl.BlockSpec(memory_space=pltpu.SMEM), pl.BlockSpec((D_GMLP + D_ATTN, MIX_W), lambda i: (0, i))]
        + _h_specs() + _table_specs()
        + [full((N_HEADS, BLK, BLK)), full((N_HEADS, BLK)), full((D_GMLP, 1)), full((D_GMLP, 1)), BIAS_SPEC],
        out_specs=[pl.BlockSpec((COL_K, MIX_W), lambda i: (0, i)), pl.BlockSpec((2 * D_KV, MIX_W), lambda i: (0, i)),
                   pl.BlockSpec((2 * D_KV, BLK), lambda i: (0, (i + n_step - 1) % n_step)),
                   full((N_HEADS, BLK, BLK)), full((N_HEADS, BLK)), full((8, D_GMLP)), full((N_HEADS, LANES))],
        out_shape=[jax.ShapeDtypeStruct((COL_K, t_tok), BF16), jax.ShapeDtypeStruct((2 * D_KV, t_tok), F32),
                   jax.ShapeDtypeStruct((2 * D_KV, n_step * BLK), F32),
                   jax.ShapeDtypeStruct((N_HEADS, BLK, BLK), BF16), jax.ShapeDtypeStruct((N_HEADS, BLK), F32),
                   jax.ShapeDtypeStruct((8, D_GMLP), F32), jax.ShapeDtypeStruct((N_HEADS, LANES), F32)],
        scratch_shapes=[pltpu.VMEM((D_GMLP, BLK), F32), pltpu.VMEM((D_GMLP, BLK), F32), pltpu.VMEM((N_HEADS, BLK, BLK), BF16),
                        pltpu.VMEM((N_HEADS, BLK, BLK), F32)],
        args=(sinks, dcat_t, h_t, h_t, h_t, h_t, h_t, cos_t, sin_t, cos_t, sin_t, w_spatial, b_spatial, vln_g, vln_b, band_bias))


def _proj_in_wgrad(dh_b, dkvc_t, dkvp_t, xb, comms=()):
    t_tok, d = xb.shape
    d_main, d_kv = dh_b.shape[0], dkvc_t.shape[0]
    tm = min(1024, t_tok)

    def body(dh_ref, dkvc_ref, dkvp_ref, xb_ref, dkvb_ref, gw_ref):
        @pl.when(pl.program_id(0) == 0)
        def _():
            gw_ref[...] = jnp.zeros_like(gw_ref)

        for s in range(tm // MIX_W):
            last = slice((s + 1) * MIX_W - BLK, (s + 1) * MIX_W)
            dkvb_ref[:, s * MIX_W:(s + 1) * MIX_W - BLK] = dkvc_ref[:, s * MIX_W:(s + 1) * MIX_W - BLK].astype(BF16)
            dkvb_ref[:, last] = (dkvc_ref[:, last] + dkvp_ref[:, _cols(s)]).astype(BF16)
        gw_ref[0:d_main, :] += _dot(dh_ref[...], xb_ref[...])
        gw_ref[d_main:, :] += _dot(dkvb_ref[...], xb_ref[...])

    tok = lambda rows: pl.BlockSpec((rows, tm), lambda i: (0, i))
    return _carry(
        body, name="proj_in_wgrad", grid=(t_tok // tm,), comms=comms,
        in_specs=[tok(d_main), tok(d_kv), pl.BlockSpec((d_kv, tm // MIX_BLOCKS), lambda i: (0, i)),
                  pl.BlockSpec((tm, d), lambda i: (i, 0))],
        out_specs=[tok(d_kv), pl.BlockSpec((d_main + d_kv, d), lambda i: (0, 0))],
        out_shape=[jax.ShapeDtypeStruct((d_kv, t_tok), BF16), jax.ShapeDtypeStruct((d_main + d_kv, d), F32)],
        args=(dh_b, dkvc_t, dkvp_t, xb))


def _proj_in_dgrad(dh_b, dkv_b, dz1, w_in_t, comms=()):
    t_tok, d = dz1.shape
    d_main, d_kv = dh_b.shape[0], dkv_b.shape[0]
    tm = min(512, t_tok)

    def body(dh_ref, dkv_ref, dz1_ref, w_ref, dx_ref):
        dx_ref[...] = (ALPHA * dz1_ref[...] + _dot(dh_ref[...], w_ref[0:d_main, :], TN)
                       + _dot(dkv_ref[...], w_ref[d_main:, :], TN))

    return _carry(
        body, name="proj_in_dgrad", grid=(t_tok // tm,), comms=comms,
        in_specs=[pl.BlockSpec((d_main, tm), lambda i: (0, i)), pl.BlockSpec((d_kv, tm), lambda i: (0, i)),
                  pl.BlockSpec((tm, d), lambda i: (i, 0)), pl.BlockSpec((d_main + d_kv, d), lambda i: (0, 0))],
        out_specs=[pl.BlockSpec((tm, d), lambda i: (i, 0))],
        out_shape=[jax.ShapeDtypeStruct((t_tok, d), F32)],
        args=(dh_b, dkv_b, dz1, w_in_t))


def _adamw(w, g, m, v):
    m = ADAM_B1 * m + (1.0 - ADAM_B1) * g
    v = ADAM_B2 * v + (1.0 - ADAM_B2) * (g * g)
    m_hat = m / (1.0 - ADAM_B1 ** ADAM_STEP)
    v_hat = v / (1.0 - ADAM_B2 ** ADAM_STEP)
    delta = -ADAM_LR * (m_hat / (jnp.sqrt(v_hat) + ADAM_EPS) + ADAM_WD * w)
    return delta, m, v


def _row_tiled(name, own, recv, extra, n_out, finish, comms=()):
    r, c = own.shape
    recv = [] if recv is None else list(recv)
    k = max(len(recv), 1)
    n = max(k, -(-r // 512))
    tr, per = r // n, n // k
    blk = pl.BlockSpec((tr, c), lambda i: (i, 0))

    def body(own_ref, *refs):
        recv_refs, rest = refs[:len(recv)], refs[len(recv):]
        ins, outs = rest[:len(extra)], rest[len(extra):]

        def tile(recv_ref):
            g = own_ref[...]
            if recv_ref is not None:
                g = ((g + recv_ref[0].astype(F32)) + recv_ref[1].astype(F32)) + recv_ref[2].astype(F32)
            for o_ref, val in zip(outs, finish(g, *[a[...] for a in ins])):
                o_ref[...] = val

        if len(recv) <= 1:
            tile(recv_refs[0] if recv else None)
        else:
            for p in range(k):
                pl.when(pl.program_id(0) // per == p)(functools.partial(tile, recv_refs[p]))

    recv_specs = [pl.BlockSpec((3, tr, c), lambda i, p=p: (0, jnp.clip(i - p * per, 0, per - 1), 0)) for p in range(len(recv))]
    return _carry(
        body, name=name, grid=(n,), comms=comms,
        in_specs=[blk] + recv_specs + [blk] * len(extra),
        out_specs=[blk] * n_out, out_shape=[jax.ShapeDtypeStruct((r, c), F32)] * n_out,
        args=(own, *recv, *extra))


def _adamw_shard(name, own, recv, w, m, v, comms=()):
    def finish(g, w_t, m_t, v_t):
        return (g,) + _adamw(w_t, g, m_t, v_t)

    return _row_tiled(name, own, recv, (w, m, v), 4, finish, comms)


VEC_VLN, VEC_LN1G, VEC_LN1B, VEC_LN2G, VEC_LN2B, VEC_SINK, VEC_LOSS, VEC_BSP, VEC_ROWS = 0, 1, 2, 3, 4, 5, 6, 8, 16


def _adamw_small(parts_w, parts_vec, params):
    n = parts_w.shape[0]
    flat = [a for p in params for a in p]
    shapes = [p[0].shape for p in params]

    def grads(gw, gv):
        return [gw, gv[VEC_VLN:VEC_VLN + 1, 0:D_GMLP], gv[VEC_VLN:VEC_VLN + 1, D_GMLP:2 * D_GMLP],
                gv[VEC_BSP:VEC_BSP + N_HEADS, 0:BLK], gv[VEC_LN1G:VEC_LN1G + 1], gv[VEC_LN1B:VEC_LN1B + 1],
                gv[VEC_LN2G:VEC_LN2G + 1], gv[VEC_LN2B:VEC_LN2B + 1], gv[VEC_SINK:VEC_SINK + 1, 0:N_HEADS]]

    def body(pw_ref, pv_ref, *refs):
        ins, outs = refs[:len(flat)], refs[len(flat):]
        gw, gv = pw_ref[0].astype(F32), pv_ref[0]
        for k in range(1, n):
            gw, gv = gw + pw_ref[k].astype(F32), gv + pv_ref[k]
        for i, g in enumerate(grads(gw, gv)):
            w_ref, m_ref, v_ref = ins[3 * i:3 * i + 3]
            delta, m_new, v_new = _adamw(w_ref[...], g, m_ref[...], v_ref[...])
            for o_ref, val in zip(outs[4 * i:4 * i + 4], (g, delta, m_new, v_new)):
                o_ref[...] = val
        outs[-1][...] = gv[VEC_LOSS:VEC_LOSS + 1, 0:LANES]

    whole = lambda shape: pl.BlockSpec(shape, lambda i: (0,) * len(shape))
    res = _carry(
        body, name="adamw_small", grid=(1,),
        in_specs=[whole(parts_w.shape), whole(parts_vec.shape)] + [whole(a.shape) for a in flat],
        out_specs=[whole(s) for s in shapes for _ in range(4)] + [whole((1, LANES))],
        out_shape=[jax.ShapeDtypeStruct(s, F32) for s in shapes for _ in range(4)] + [jax.ShapeDtypeStruct((1, LANES), F32)],
        args=(parts_w, parts_vec, *flat))[0]
    return [res[4 * i:4 * i + 4] for i in range(len(params))], res[-1]


def _pair_sum(name, parts, recv, core_chip, comms=()):
    _, r, c = parts.shape
    tr = r if r <= 512 else 512

    def body(cc_ref, a_ref, b_ref, wire_ref, own_ref):
        s = a_ref[...] + b_ref[...]
        wire_ref[...] = s.astype(BF16)

        @pl.when(pl.program_id(1) == cc_ref[1])
        def _():
            own_ref[...] = s

    return _carry(
        body, name=name, grid=(r // tr, 4), prefetch=(core_chip,), comms=comms,
        in_specs=[pl.BlockSpec((None, tr, c), lambda i, q, cc: (2 * q + cc[0], i, 0)),
                  pl.BlockSpec((None, tr, c), lambda i, q, cc: (q, i, 0))],
        out_specs=[pl.BlockSpec((None, tr, c), lambda i, q, cc: (q, i, 0)), pl.BlockSpec((tr, c), lambda i, q, cc: (i, 0))],
        out_shape=[jax.ShapeDtypeStruct((4, r, c), BF16), jax.ShapeDtypeStruct((r, c), F32)],
        args=(parts, recv))


def kernel(x, positions, w_in, v_ln_g, v_ln_b, w_spatial, b_spatial, sinks, w_out, ln1_g, ln1_b, w_ff1, w_ff2, ln2_g, ln2_b, loss_target, m_w_in, m_v_ln_g, m_v_ln_b, m_w_spatial, m_b_spatial, m_sinks, m_w_out, m_ln1_g, m_ln1_b, m_w_ff1, m_w_ff2, m_ln2_g, m_ln2_b, v_w_in, v_v_ln_g, v_v_ln_b, v_w_spatial, v_b_spatial, v_sinks, v_w_out, v_ln1_g, v_ln1_b, v_w_ff1, v_w_ff2, v_ln2_g, v_ln2_b):
    _, t_tok, d = x.shape
    xi, yi, ci = _place()
    core_chip = jnp.stack([ci, 2 * xi + yi]).astype(jnp.int32)
    x2 = x.reshape(t_tok, d)
    target = loss_target.reshape(t_tok, d)
    inv_freq = ROPE_THETA ** (-jnp.arange(0, HEAD_DIM, 2, dtype=F32) / HEAD_DIM)
    wsp, bsp, sink_vec = w_spatial[0], b_spatial[0], sinks[0]
    vg_col, vb_col = v_ln_g.reshape(D_GMLP, 1), v_ln_b.reshape(D_GMLP, 1)
    big = {"in": w_in[0], "out": w_out[0], "ff1": w_ff1[0], "ff2": w_ff2[0]}
    half1, half2 = big["ff1"].shape[1] // 2, big["ff2"].shape[0] // 2
    w1_mine = [big["ff1"][:, :half1].astype(BF16), big["ff1"][:, half1:].astype(BF16)]
    w2_mine = [big["ff2"][:half2].astype(BF16), big["ff2"][half2:].astype(BF16)]

    (cos_t, sin_t), ((g_in,),) = _rope_tables(
        positions, jnp.tile(inv_freq, 2).reshape(HEAD_DIM, 1), comms=[_gather_comm([big["in"].T.astype(BF16)])])
    w_in_t = g_in.reshape(D_IN, d)
    (h_t, xb), ((g_out, w1_a),) = _proj_in(x2, w_in_t, comms=[_gather_comm([big["out"].astype(BF16), w1_mine[0]])])
    w_out_b = g_out.reshape(-1, d)
    band_bias = _band_bias()
    (cat_t,), ((w1_b, w2_a),) = _mixer_fwd(h_t, cos_t, sin_t, wsp, bsp, vg_col, vb_col, sink_vec, band_bias,
                                           comms=[_gather_comm([w1_mine[1], w2_mine[0]])])
    (xhat1, rstd1, x1b), ((w2_b,),) = _proj_out(cat_t, x2, w_out_b, ln1_g, ln1_b, comms=[_gather_comm([w2_mine[1]])])
    act_b, dpre_b, dz2b, dz1, stats = _ffn_fwd_bwd(xhat1, rstd1, x1b, target, [w1_a, w1_b], [w2_a, w2_b], ln1_g, ln1_b, ln2_g, ln2_b)

    (dcat_t, gw_out), _ = _proj_out_bwd(dz1, cat_t, w_out_b)
    p_out = gw_out.reshape(N_DEV, -1, d)
    wire_ff1, own_ff1, ((s_out,),) = _ffn_wgrad("ffn_wgrad1", x1b, dpre_b, False, core_chip, comms=[_sibling_comm([p_out])])
    (wire_out, own_out), _ = _pair_sum("pair_sum_out", p_out, s_out, core_chip)
    wire_ff2, own_ff2, ((r_ff1,),) = _ffn_wgrad("ffn_wgrad2", act_b, dz2b, True, core_chip, comms=[_chips_comm([wire_ff1])])
    (dh_b, dkvc_t, dkvp_t, g_wsp, g_bsp, g_vln, g_sink), ((r_ff2, r_out),) = _mixer_bwd(
        dcat_t, h_t, cos_t, sin_t, wsp, bsp, vg_col, vb_col, sink_vec, band_bias,
        comms=[_chips_comm([wire_ff2, wire_out])])
    sink_row = jnp.pad(g_sink.sum(axis=1).reshape(1, N_HEADS), ((0, 0), (0, d - N_HEADS)))
    small_vec = jnp.concatenate([g_vln[0:2].reshape(1, d), stats[0:4], sink_row, stats[4:5], jnp.zeros((1, d), F32),
                                 jnp.pad(g_bsp, ((0, 0), (0, d - BLK)))], axis=0)
    (dkv_b, gw_in_t), ((parts_w, parts_vec),) = _proj_in_wgrad(
        dh_b, dkvc_t, dkvp_t, xb, comms=[_gather_comm([g_wsp.reshape(-1, BLK), small_vec])])
    p_in = gw_in_t.reshape(N_DEV, -1, d)

    out_out, ((s_in,),) = _adamw_shard("adamw_out", own_out, [r_out], big["out"], m_w_out[0], v_w_out[0], comms=[_sibling_comm([p_in])])
    (wire_in, own_in), _ = _pair_sum("pair_sum_in", p_in, s_in, core_chip)
    (grad_x,), ((r_in,),) = _proj_in_dgrad(dh_b, dkv_b, dz1, w_in_t, comms=[_chips_comm([wire_in])])
    ff1_out, _ = _adamw_shard("adamw_ff1", own_ff1, [r_ff1], big["ff1"], m_w_ff1[0], v_w_ff1[0])
    ff2_out, _ = _adamw_shard("adamw_ff2", own_ff2, [r_ff2], big["ff2"], m_w_ff2[0], v_w_ff2[0])
    in_out_t, _ = _adamw_shard("adamw_in", own_in, [r_in], big["in"].T, m_w_in[0].T, v_w_in[0].T)
    in_out = [o.T for o in in_out_t]
    small = [(w_spatial, m_w_spatial, v_w_spatial), (v_ln_g, m_v_ln_g, v_v_ln_g), (v_ln_b, m_v_ln_b, v_v_ln_b),
             (b_spatial, m_b_spatial, v_b_spatial), (ln1_g, m_ln1_g, v_ln1_g), (ln1_b, m_ln1_b, v_ln1_b),
             (ln2_g, m_ln2_g, v_ln2_g), (ln2_b, m_ln2_b, v_ln2_b), (sinks, m_sinks, v_sinks)]
    views = [(-1, BLK), None, None, (N_HEADS, BLK)] + [None] * 5
    small_res, loss_row = _adamw_small(parts_w, parts_vec, [
        tuple(a if vw is None else a.reshape(vw) for a in p) for p, vw in zip(small, views)])
    small_out = [[o.reshape(p[0].shape) for o in res] for res, p in zip(small_res, small)]
    loss = loss_row[0, 0]

    big_out = {0: in_out, 6: out_out, 9: ff1_out, 10: ff2_out}
    small_slot = {3: 0, 1: 1, 2: 2, 4: 3, 7: 4, 8: 5, 11: 6, 12: 7, 5: 8}
    outs = [loss, grad_x.reshape(x.shape)]
    for kind in range(4):
        for wi in range(13):
            outs.append(big_out[wi][kind][None] if wi in big_out else small_out[small_slot[wi]][kind])
    return tuple(outs)
```

```python
import functools
import math

import jax
import jax.numpy as jnp
from jax import lax
from jax.experimental import pallas as pl
from jax.experimental.pallas import tpu as pltpu

F32 = jnp.float32
BF16 = jnp.bfloat16
MESH = pl.DeviceIdType.MESH

HEAD_DIM = 64
N_HEADS = 8
N_KV_HEADS = 2
BLK = 128
D_GMLP = N_HEADS * HEAD_DIM
D_ATTN = N_HEADS * HEAD_DIM
D_KV = N_KV_HEADS * HEAD_DIM
D_IN = 2 * D_GMLP + D_ATTN + 2 * D_KV
COL_U, COL_V, COL_Q, COL_K = 0, D_GMLP, 2 * D_GMLP, 2 * D_GMLP + D_ATTN
ROPE_THETA = 10000.0
LN_EPS = 1e-5
ALPHA = 2.0 ** 0.25
NEG_INF = -1e30
SCORE_SCALE = 1.0 / math.sqrt(HEAD_DIM)
ADAM_LR, ADAM_B1, ADAM_B2, ADAM_EPS, ADAM_WD, ADAM_STEP = 0.001, 0.9, 0.999, 1e-08, 0.01, 10
N_DEV = 8
LANES = 128
VMEM_LIMIT = 56 * 1024 * 1024
FFN_ROWS = 256

NT = (((1,), (1,)), ((), ()))
TN = (((0,), (0,)), ((), ()))


def _params(*sem):
    return pltpu.CompilerParams(dimension_semantics=sem, vmem_limit_bytes=VMEM_LIMIT)


def _dot(a, b, dims=None):
    if dims is None:
        return jnp.dot(a, b, preferred_element_type=F32)
    return lax.dot_general(a, b, dims, preferred_element_type=F32)


def _mean(a):
    return jnp.mean(a, axis=-1, keepdims=True)


def _ln_fwd(z, g, b):
    zc = z - _mean(z)
    rstd = lax.rsqrt(_mean(zc * zc) + LN_EPS)
    xhat = zc * rstd
    return xhat * g + b, xhat, rstd


def _ln_bwd(dy, xhat, rstd, g):
    dxhat = dy * g
    return rstd * (dxhat - _mean(dxhat) - xhat * _mean(dxhat * xhat))


_GELU_C = math.sqrt(2.0 / math.pi)


def _gelu(x):
    t = jnp.tanh(_GELU_C * (x + 0.044715 * (x * x * x)))
    return 0.5 * x * (1.0 + t)


def _gelu_and_grad(x):
    x2 = x * x
    t = jnp.tanh(_GELU_C * (x + 0.044715 * (x2 * x)))
    hx, ht = 0.5 * x, 0.5 * (1.0 + t)
    return x * ht, ht + hx * (1.0 - t * t) * (_GELU_C * (1.0 + 3.0 * 0.044715 * x2))


def _mean0(a):
    return jnp.mean(a, axis=0, keepdims=True)


def _ln_fwd_t(z, g, b):
    zc = z - _mean0(z)
    rstd = lax.rsqrt(_mean0(zc * zc) + LN_EPS)
    xhat = zc * rstd
    return xhat * g + b, xhat, rstd


def _ln_bwd_t(dy, xhat, rstd, g):
    dxhat = dy * g
    return rstd * (dxhat - _mean0(dxhat) - xhat * _mean0(dxhat * xhat))


def _rope_t(t, cos, sin_signed, bwd=False):
    half = HEAD_DIM // 2
    outs = []
    for r in range(0, t.shape[0], HEAD_DIM):
        th = t[r:r + HEAD_DIM]
        sw = jnp.concatenate([th[half:], th[:half]], axis=0) * sin_signed
        outs.append(th * cos - sw if bwd else th * cos + sw)
    return jnp.concatenate(outs, axis=0)


ANY = pl.BlockSpec(memory_space=pl.ANY)


def _place():
    return lax.axis_index("x"), lax.axis_index("y"), lax.axis_index("c")


class _Comm:
    def __init__(self, ins, outs, sems, start, finish):
        self.ins, self.outs, self.sems, self.start, self.finish = ins, outs, sems, start, finish


def _gather_comm(arrs):
    n = len(arrs)

    def parts(ins, outs, sems):
        send_sems, recv_sems, local_sems = sems
        x, y, c = _place()
        me, sibling = (x, y, c), (x, y, 1 - c)
        chips = [(1 - x, y), (x, 1 - y), (1 - x, 1 - y)]

        def copy(a, k, block, to, src=None):
            px, py, pc = block
            dst = outs[a].at[4 * px + 2 * py + pc]
            return pltpu.make_async_remote_copy(
                src_ref=dst if src is None else src, dst_ref=dst,
                send_sem=send_sems.at[a, k], recv_sem=recv_sems.at[a, k], device_id=to, device_id_type=MESH)

        mine = [pltpu.make_async_copy(ins[a], outs[a].at[4 * x + 2 * y + c], local_sems.at[a]) for a in range(n)]
        first = []
        for a in range(n):
            first.append(copy(a, 0, me, sibling, src=ins[a]))
            first += [copy(a, 1 + j, me, (*chip, c), src=ins[a]) for j, chip in enumerate(chips)]
        return copy, mine, first, me, sibling, chips, c

    def start(ins, outs, sems):
        _, mine, first, *_ = parts(ins, outs, sems)
        for cp in mine + first:
            cp.start()

    def finish(ins, outs, sems):
        copy, mine, first, me, sibling, chips, c = parts(ins, outs, sems)
        passed = []
        for j, chip in enumerate(chips):
            for a in range(n):
                copy(a, 1 + j, (*chip, c), me).wait_recv()
                fwd = copy(a, 4 + j, (*chip, c), sibling)
                fwd.start()
                passed.append(fwd)
        for a in range(n):
            copy(a, 0, sibling, me).wait_recv()
        for j, chip in enumerate(chips):
            for a in range(n):
                copy(a, 4 + j, (*chip, 1 - c), me).wait_recv()
        for cp in first + passed:
            cp.wait_send()
        for cp in mine:
            cp.wait()

    return _Comm(list(arrs), [jax.ShapeDtypeStruct((N_DEV,) + a.shape, a.dtype) for a in arrs],
                 [pltpu.SemaphoreType.DMA((n, 7)), pltpu.SemaphoreType.DMA((n, 7)), pltpu.SemaphoreType.DMA((n,))],
                 start, finish)


def _sibling_comm(parts):
    n = len(parts)

    def copies(ins, outs, sems):
        x, y, c = _place()
        return [pltpu.make_async_remote_copy(
            src_ref=ins[a].at[2 * q + (1 - c)], dst_ref=outs[a].at[q],
            send_sem=sems[0].at[a, q], recv_sem=sems[1].at[a, q],
            device_id=(x, y, 1 - c), device_id_type=MESH) for a in range(n) for q in range(4)]

    return _Comm(list(parts), [jax.ShapeDtypeStruct((4,) + p.shape[1:], p.dtype) for p in parts],
                 [pltpu.SemaphoreType.DMA((n, 4)), pltpu.SemaphoreType.DMA((n, 4))],
                 lambda *r: [cp.start() for cp in copies(*r)], lambda *r: [cp.wait() for cp in copies(*r)])


def _chips_comm(chip_parts, rows=None):
    n = len(chip_parts)
    r0, nr = (0, None) if rows is None else rows

    def copies(ins, outs, sems):
        x, y, c = _place()
        chips = [(1 - x, y), (x, 1 - y), (1 - x, 1 - y)]
        src = lambda a, q: ins[a].at[q] if rows is None else ins[a].at[q, pl.ds(r0, nr)]
        return [pltpu.make_async_remote_copy(
            src_ref=src(a, 2 * px + py), dst_ref=outs[a].at[k],
            send_sem=sems[0].at[a, k], recv_sem=sems[1].at[a, k],
            device_id=(px, py, c), device_id_type=MESH) for a in range(n) for k, (px, py) in enumerate(chips)]

    shape = lambda p: (3,) + p.shape[1:] if rows is None else (3, nr) + p.shape[2:]
    return _Comm(list(chip_parts), [jax.ShapeDtypeStruct(shape(p), p.dtype) for p in chip_parts],
                 [pltpu.SemaphoreType.DMA((n, 3)), pltpu.SemaphoreType.DMA((n, 3))],
                 lambda *r: [cp.start() for cp in copies(*r)], lambda *r: [cp.wait() for cp in copies(*r)])


def _carry(body, *, name, grid, in_specs, out_specs, out_shape, args, comms=(), scratch_shapes=(), prefetch=()):
    n_pre, n_in, n_out, n_scr = len(prefetch), len(in_specs), len(out_specs), len(scratch_shapes)
    c_ins = [a for cm in comms for a in cm.ins]
    c_outs = [s for cm in comms for s in cm.outs]
    c_sems = [s for cm in comms for s in cm.sems]

    def wrapped(*refs):
        pre, refs = refs[:n_pre], refs[n_pre:]
        ins, refs = refs[:n_in], refs[n_in:]
        cins, refs = refs[:len(c_ins)], refs[len(c_ins):]
        outs, refs = refs[:n_out], refs[n_out:]
        couts, refs = refs[:len(c_outs)], refs[len(c_outs):]
        scr, sems = refs[:n_scr], refs[n_scr:]
        groups, i0, o0, s0 = [], 0, 0, 0
        for cm in comms:
            groups.append((cm, cins[i0:i0 + len(cm.ins)], couts[o0:o0 + len(cm.outs)], sems[s0:s0 + len(cm.sems)]))
            i0, o0, s0 = i0 + len(cm.ins), o0 + len(cm.outs), s0 + len(cm.sems)
        first = pl.program_id(0) == 0
        last = pl.program_id(0) == grid[0] - 1
        for ax in range(1, len(grid)):
            first = first & (pl.program_id(ax) == 0)
            last = last & (pl.program_id(ax) == grid[ax] - 1)
        if comms:
            @pl.when(first)
            def _():
                for cm, ci, co, cs in groups:
                    cm.start(ci, co, cs)
        body(*pre, *ins, *outs, *scr)
        if comms:
            @pl.when(last)
            def _():
                for cm, ci, co, cs in groups:
                    cm.finish(ci, co, cs)

    grid_spec = pltpu.PrefetchScalarGridSpec(
        num_scalar_prefetch=n_pre, grid=grid,
        in_specs=list(in_specs) + [ANY] * len(c_ins), out_specs=list(out_specs) + [ANY] * len(c_outs),
        scratch_shapes=list(scratch_shapes) + c_sems)
    res = pl.pallas_call(
        wrapped, name=name, grid_spec=grid_spec, out_shape=list(out_shape) + c_outs,
        compiler_params=_params(*(["arbitrary"] * len(grid))),
    )(*prefetch, *args, *c_ins)
    outs, rest, per_comm = res[:n_out], res[n_out:], []
    for cm in comms:
        per_comm.append(rest[:len(cm.outs)])
        rest = rest[len(cm.outs):]
    return outs, per_comm


def _rope_tables(pos_row, inv_freq_col, comms=()):
    t_tok = pos_row.shape[1]
    tm = min(512, t_tok)

    def body(pos_ref, invf_ref, cos_ref, sin_ref):
        ang = pos_ref[...].astype(F32) * invf_ref[...]
        row = lax.broadcasted_iota(jnp.int32, ang.shape, 0)
        cos_ref[...] = jnp.cos(ang)
        sin_ref[...] = jnp.sin(ang) * jnp.where(row < HEAD_DIM // 2, -1.0, 1.0)

    return _carry(
        body, name="rope_tables", grid=(t_tok // tm,), comms=comms,
        in_specs=[pl.BlockSpec((1, tm), lambda i: (0, i)), pl.BlockSpec((HEAD_DIM, 1), lambda i: (0, 0))],
        out_specs=[pl.BlockSpec((HEAD_DIM, tm), lambda i: (0, i))] * 2,
        out_shape=[jax.ShapeDtypeStruct((HEAD_DIM, t_tok), F32)] * 2,
        args=(pos_row, inv_freq_col))


def _proj_in(x2, w_in_t, comms=()):
    t_tok, d = x2.shape
    d_in = w_in_t.shape[0]
    tm = min(512, t_tok)

    def body(x_ref, w_ref, h_ref, xb_ref):
        xb = x_ref[...].astype(BF16)
        xb_ref[...] = xb
        h_ref[...] = _dot(w_ref[...], xb, NT)

    return _carry(
        body, name="proj_in", grid=(t_tok // tm,), comms=comms,
        in_specs=[pl.BlockSpec((tm, d), lambda i: (i, 0)), pl.BlockSpec((d_in, d), lambda i: (0, 0))],
        out_specs=[pl.BlockSpec((d_in, tm), lambda i: (0, i)), pl.BlockSpec((tm, d), lambda i: (i, 0))],
        out_shape=[jax.ShapeDtypeStruct((d_in, t_tok), F32), jax.ShapeDtypeStruct((t_tok, d), BF16)],
        args=(x2, w_in_t))


MIX_BLOCKS = 2
MIX_W = MIX_BLOCKS * BLK


def _prev_block(i):
    return jnp.maximum(MIX_BLOCKS * i - 1, 0)


def _h_specs():
    kv_row = COL_K // (2 * D_KV)
    return [
        pl.BlockSpec((D_GMLP, MIX_W), lambda i: (0, i)),
        pl.BlockSpec((D_GMLP, MIX_W), lambda i: (1, i)),
        pl.BlockSpec((D_ATTN, MIX_W), lambda i: (2, i)),
        pl.BlockSpec((2 * D_KV, MIX_W), lambda i: (kv_row, i)),
        pl.BlockSpec((2 * D_KV, BLK), lambda i: (kv_row, _prev_block(i))),
    ]


def _table_specs():
    return [
        pl.BlockSpec((HEAD_DIM, MIX_W), lambda i: (0, i)),
        pl.BlockSpec((HEAD_DIM, MIX_W), lambda i: (0, i)),
        pl.BlockSpec((HEAD_DIM, BLK), lambda i: (0, _prev_block(i))),
        pl.BlockSpec((HEAD_DIM, BLK), lambda i: (0, _prev_block(i))),
    ]


def _cols(b):
    return slice(b * BLK, (b + 1) * BLK)


def _block_inputs(b, i, kvc, kvp_ref, cos, sin, cosp_ref, sinp_ref, bias_ref):
    if b == 0:
        kv_prev, cos_prev, sin_prev, bias = kvp_ref[...], cosp_ref[...], sinp_ref[...], bias_ref[jnp.minimum(i, 1)]
    else:
        kv_prev, cos_prev, sin_prev, bias = kvc[:, _cols(b - 1)], cos[:, _cols(b - 1)], sin[:, _cols(b - 1)], bias_ref[1]
    return kvc[:, _cols(b)], kv_prev, cos[:, _cols(b)], sin[:, _cols(b)], cos_prev, sin_prev, bias


def _band_bias():
    ki = lax.broadcasted_iota(jnp.int32, (2, 2 * BLK, BLK), 1)
    qi = lax.broadcasted_iota(jnp.int32, (2, 2 * BLK, BLK), 2)
    later = lax.broadcasted_iota(jnp.int32, (2, 2 * BLK, BLK), 0) > 0
    dist = qi + BLK - ki
    return jnp.where((dist >= 0) & (dist < BLK) & ((ki >= BLK) | later), 0.0, NEG_INF).astype(F32)


BIAS_SPEC = pl.BlockSpec((2, 2 * BLK, BLK), lambda i: (0, 0, 0))


def _keys_values(kvc, kvp, cosc, sinc, cosp, sinp):
    kp, kc = _rope_t(kvp[:D_KV], cosp, sinp), _rope_t(kvc[:D_KV], cosc, sinc)
    k_t = jnp.concatenate([kp, kc], axis=1).astype(BF16)
    k_n = jnp.concatenate([kp.T, kc.T], axis=0).astype(BF16)
    v_t = jnp.concatenate([kvp[D_KV:], kvc[D_KV:]], axis=1).astype(BF16)
    return k_t, k_n, v_t


def _pad_head(th, kv):
    z = jnp.zeros_like(th)
    return jnp.concatenate([th, z] if kv == 0 else [z, th], axis=0)


def _group_lanes(parts):
    return jnp.concatenate(parts, axis=1)


def _softmax_sink_t(s, sink):
    m = jnp.maximum(jnp.max(s, axis=0, keepdims=True), sink)
    e = jnp.exp(s - m)
    es = jnp.exp(sink - m)
    r = 1.0 / (jnp.sum(e, axis=0, keepdims=True) + es)
    return e * r, es * r


def _causal():
    row = lax.broadcasted_iota(jnp.int32, (BLK, BLK), 0)
    col = lax.broadcasted_iota(jnp.int32, (BLK, BLK), 1)
    return row >= col


def _mask_w_once(wsp_ref, wm_scr):
    @pl.when(pl.program_id(0) == 0)
    def _():
        causal = _causal()
        for hh in range(N_HEADS):
            wm_scr[hh] = jnp.where(causal, wsp_ref[hh], 0.0).astype(BF16)


def _mixer_fwd(h_t, cos_t, sin_t, w_spatial, b_spatial, vln_g, vln_b, sinks, band_bias, comms=()):
    t_tok = h_t.shape[1]
    group = N_HEADS // N_KV_HEADS

    def body(sinks_ref, u_ref, vg_ref, q_ref, kvc_ref, kvp_ref, cos_ref, sin_ref, cosp_ref, sinp_ref,
             wsp_ref, bsp_ref, g_ref, b_ref, bias_ref, cat_ref, wm_scr):
        i = pl.program_id(0)
        _mask_w_once(wsp_ref, wm_scr)
        ua = _gelu(u_ref[...])
        vp, _, _ = _ln_fwd_t(_gelu(vg_ref[...]), g_ref[...], b_ref[...])
        vpb = vp.astype(BF16)
        for b in range(MIX_BLOCKS):
            for hh in range(N_HEADS):
                rows = slice(hh * HEAD_DIM, (hh + 1) * HEAD_DIM)
                mixed = _dot(vpb[rows, _cols(b)], wm_scr[hh], NT) + bsp_ref[hh:hh + 1, :]
                cat_ref[rows, _cols(b)] = (ua[rows, _cols(b)] * mixed).astype(BF16)

        kvc, cos, sin = kvc_ref[...], cos_ref[...], sin_ref[...]
        qr = (_rope_t(q_ref[...], cos, sin) * SCORE_SCALE).astype(BF16)
        sinks4 = [_group_lanes([jnp.full((1, BLK), sinks_ref[hh], F32) for hh in range(kv * group, (kv + 1) * group)])
                  for kv in range(N_KV_HEADS)]
        for b in range(MIX_BLOCKS):
            kv_cur, kv_prev, cosc, sinc, cosp, sinp, bias1 = _block_inputs(b, i, kvc, kvp_ref, cos, sin, cosp_ref, sinp_ref, bias_ref)
            _, k_n, v_t = _keys_values(kv_cur, kv_prev, cosc, sinc, cosp, sinp)
            bias = _group_lanes([bias1] * group)
            for kv in range(N_KV_HEADS):
                heads = range(kv * group, (kv + 1) * group)
                qs = _group_lanes([qr[hh * HEAD_DIM:(hh + 1) * HEAD_DIM, _cols(b)] for hh in heads])
                p, _ = _softmax_sink_t(_dot(k_n, _pad_head(qs, kv)) + bias, sinks4[kv])
                o = _dot(v_t[kv * HEAD_DIM:(kv + 1) * HEAD_DIM], p.astype(BF16)).astype(BF16)
                for j, hh in enumerate(heads):
                    cat_ref[D_GMLP + hh * HEAD_DIM:D_GMLP + (hh + 1) * HEAD_DIM, _cols(b)] = o[:, j * BLK:(j + 1) * BLK]

    full = lambda shape: pl.BlockSpec(shape, lambda i: (0,) * len(shape))
    return _carry(
        body, name="mixer_fwd", grid=(t_tok // MIX_W,), comms=comms,
        in_specs=[pl.BlockSpec(memory_space=pltpu.SMEM)] + _h_specs() + _table_specs() + [
            full((N_HEADS, BLK, BLK)), full((N_HEADS, BLK)), full((D_GMLP, 1)), full((D_GMLP, 1)), BIAS_SPEC],
        out_specs=[pl.BlockSpec((D_GMLP + D_ATTN, MIX_W), lambda i: (0, i))],
        out_shape=[jax.ShapeDtypeStruct((D_GMLP + D_ATTN, t_tok), BF16)],
        scratch_shapes=[pltpu.VMEM((N_HEADS, BLK, BLK), BF16)],
        args=(sinks, h_t, h_t, h_t, h_t, h_t, cos_t, sin_t, cos_t, sin_t, w_spatial, b_spatial, vln_g, vln_b, band_bias))


def _proj_out(cat_t, x2, w_out_b, ln1_g, ln1_b, w1_first, comms=()):
    t_tok, d = x2.shape
    n_chunk, _, fp = w1_first.shape
    tm = min(512, t_tok)

    def body(cat_ref, x_ref, w_ref, g_ref, b_ref, w1_hbm, xhat_ref, rstd_ref, x1b_ref, r_ref, w1_ref, w1_sems):
        @pl.when(pl.program_id(0) == 0)
        def _():
            loads = [pltpu.make_async_copy(w1_hbm.at[j], w1_ref.at[:, pl.ds(j * fp, fp)], w1_sems.at[j]) for j in range(n_chunk)]
            for cp in loads:
                cp.start()
            for cp in loads:
                cp.wait()

        x1, xhat, rstd = _ln_fwd(ALPHA * x_ref[...] + _dot(cat_ref[...], w_ref[...], TN), g_ref[...], b_ref[...])
        xhat_ref[...] = xhat
        rstd_ref[...] = rstd
        x1b = x1.astype(BF16)
        x1b_ref[...] = x1b
        r_ref[...] = jnp.maximum(_dot(x1b, w1_ref[...]), 0.0)

    tok = lambda w: pl.BlockSpec((tm, w), lambda i: (i, 0))
    vec = pl.BlockSpec((1, d), lambda i: (0, 0))
    return _carry(
        body, name="proj_out", grid=(t_tok // tm,), comms=comms,
        in_specs=[pl.BlockSpec((cat_t.shape[0], tm), lambda i: (0, i)), tok(d), pl.BlockSpec(w_out_b.shape, lambda i: (0, 0)),
                  vec, vec, ANY],
        out_specs=[tok(d), tok(1), tok(d), tok(n_chunk * fp)],
        out_shape=[jax.ShapeDtypeStruct((t_tok, d), F32), jax.ShapeDtypeStruct((t_tok, 1), F32),
                   jax.ShapeDtypeStruct((t_tok, d), BF16), jax.ShapeDtypeStruct((t_tok, n_chunk * fp), F32)],
        scratch_shapes=[pltpu.VMEM((d, n_chunk * fp), BF16), pltpu.SemaphoreType.DMA((n_chunk,))],
        args=(cat_t, x2, w_out_b, ln1_g, ln1_b, w1_first))


def _ffn_fwd_bwd(xhat1, rstd1, x1b, r_first, target, w1_parts, w2_parts, ln1_g, ln1_b, ln2_g, ln2_b):
    t_tok, d = xhat1.shape
    n_part = len(w1_parts)
    n_chunk, _, fp = w1_parts[0].shape
    f_run = n_chunk * fp
    f = n_part * f_run
    tm = min(FFN_ROWS, t_tok)

    def to_shard_order(ref, val):
        for j in range(n_chunk):
            for p in range(n_part):
                ref[:, (j * n_part + p) * fp:(j * n_part + p + 1) * fp] = val[:, p * f_run + j * fp:p * f_run + (j + 1) * fp]

    def body(xhat1_ref, rstd1_ref, x1b_ref, rf_ref, tgt_ref, *refs):
        w1_hbm, w2_hbm = refs[:n_part], refs[n_part:2 * n_part]
        (g1_ref, b1_ref, g2_ref, b2_ref, act_ref, dpre_ref, dz2b_ref, dz1_ref, stats_ref,
         r_scr, w1_ref, w2_ref, w_sems) = refs[2 * n_part:]

        @pl.when(pl.program_id(0) == 0)
        def _():
            stats_ref[...] = jnp.zeros_like(stats_ref)
            loads = []
            for j in range(n_chunk):
                for p in range(n_part):
                    units = pl.ds(p * f_run + j * fp, fp)
                    loads.append(pltpu.make_async_copy(w1_hbm[p].at[j], w1_ref.at[:, units], w_sems.at[0, p, j]))
                    loads.append(pltpu.make_async_copy(w2_hbm[p].at[j], w2_ref.at[units, :], w_sems.at[1, p, j]))
            for cp in loads:
                cp.start()
            for cp in loads:
                cp.wait()

        g1, g2 = g1_ref[...], g2_ref[...]
        xhat1 = xhat1_ref[...]
        r_scr[:, 0:f_run] = rf_ref[...]
        r_scr[:, f_run:] = jnp.maximum(_dot(x1b_ref[...], w1_ref[:, f_run:]), 0.0)
        r = r_scr[...]
        act = (r * r).astype(BF16)
        to_shard_order(act_ref, act)
        ff = _dot(act, w2_ref[...])
        y, xhat2, rstd2 = _ln_fwd(ALPHA * (xhat1 * g1 + b1_ref[...]) + ff, g2, b2_ref[...])
        diff = y - tgt_ref[...]
        loss = 0.5 * jnp.sum(jnp.sum(diff * diff, axis=-1, keepdims=True) / d, axis=0, keepdims=True)
        dy = diff / d
        dz2 = _ln_bwd(dy, xhat2, rstd2, g2)
        dz2b = dz2.astype(BF16)
        dz2b_ref[...] = dz2b
        dpre = (_dot(dz2b, w2_ref[...], NT) * (2.0 * r_scr[...])).astype(BF16)
        to_shard_order(dpre_ref, dpre)
        dx1 = ALPHA * dz2 + _dot(dpre, w1_ref[...], NT)
        dz1_ref[...] = _ln_bwd(dx1, xhat1, rstd1_ref[...], g1)
        stats_ref[0:1, :] += jnp.sum(dx1 * xhat1, axis=0, keepdims=True)
        stats_ref[1:2, :] += jnp.sum(dx1, axis=0, keepdims=True)
        stats_ref[2:3, :] += jnp.sum(dy * xhat2, axis=0, keepdims=True)
        stats_ref[3:4, :] += jnp.sum(dy, axis=0, keepdims=True)
        stats_ref[4:5, :] += jnp.broadcast_to(loss, (1, d))

    tok = lambda w: pl.BlockSpec((tm, w), lambda i: (i, 0))
    vec = pl.BlockSpec((1, d), lambda i: (0, 0))
    return _carry(
        body, name="ffn_fwd_bwd", grid=(t_tok // tm,),
        in_specs=[tok(d), tok(1), tok(d), tok(f_run), tok(d)] + [ANY] * (2 * n_part) + [vec, vec, vec, vec],
        out_specs=[tok(f), tok(f), tok(d), tok(d), pl.BlockSpec((8, d), lambda i: (0, 0))],
        out_shape=[jax.ShapeDtypeStruct((t_tok, f), BF16), jax.ShapeDtypeStruct((t_tok, f), BF16),
                   jax.ShapeDtypeStruct((t_tok, d), BF16), jax.ShapeDtypeStruct((t_tok, d), F32), jax.ShapeDtypeStruct((8, d), F32)],
        scratch_shapes=[pltpu.VMEM((tm, f), F32), pltpu.VMEM((d, f), BF16), pltpu.VMEM((f, d), BF16),
                        pltpu.SemaphoreType.DMA((2, n_part, n_chunk))],
        args=(xhat1, rstd1, x1b, r_first, target, *w1_parts, *w2_parts, ln1_g, ln1_b, ln2_g, ln2_b))[0]


def _ffn_wgrad(name, lhs, rhs, chunk_lhs, core_chip, comms=()):
    t_tok = lhs.shape[0]
    half = N_DEV // 2
    fc = (lhs if chunk_lhs else rhs).shape[1] // N_DEV
    chunk = (fc, rhs.shape[1]) if chunk_lhs else (lhs.shape[1], fc)

    def shard(s, cc):
        return 2 * (s % half) + jnp.where(s < half, 1 - cc[0], cc[0])

    def body(cc_ref, lhs_ref, rhs_ref, wire_ref, own_ref, recv_ref, send_buf, got, send_sems, recv_sems, got_sem):
        s = pl.program_id(0)
        x, y, c = _place()
        def send(q):
            return pltpu.make_async_remote_copy(
                src_ref=send_buf.at[q % 2], dst_ref=recv_ref.at[q], send_sem=send_sems.at[q], recv_sem=recv_sems.at[q],
                device_id=(x, y, 1 - c), device_id_type=MESH)

        def load(q):
            return pltpu.make_async_copy(recv_ref.at[q], got, got_sem.at[0])

        @pl.when(s >= half)
        def _():
            send(s - half).wait_recv()
            load(s - half).start()

        g = _dot(lhs_ref[...], rhs_ref[...], TN)

        for q in range(half):
            @pl.when(s == q)
            def _(q=q):
                if q >= 2:
                    send(q - 2).wait_send()
                send_buf[q % 2] = g
                send(q).start()

            @pl.when(s == half + q)
            def _(q=q):
                load(q).wait()
                total = g + got[...]
                wire_ref[...] = total.astype(BF16)

                @pl.when(cc_ref[1] == q)
                def _():
                    own_ref[...] = total

        @pl.when(s == N_DEV - 1)
        def _():
            for q in range(half - 2, half):
                send(q).wait_send()

    resident = lambda a: pl.BlockSpec(a.shape, lambda s, cc: (0, 0), pipeline_mode=pl.Buffered(1))
    chunked = pl.BlockSpec((t_tok, fc), lambda s, cc: (0, shard(s, cc)))
    (wire, own, _), per_comm = _carry(
        body, name=name, grid=(N_DEV,), comms=comms, prefetch=(core_chip,),
        in_specs=[chunked, resident(rhs)] if chunk_lhs else [resident(lhs), chunked],
        out_specs=[pl.BlockSpec((None,) + chunk, lambda s, cc: (jnp.maximum(s - half, 0), 0, 0)),
                   pl.BlockSpec(chunk, lambda s, cc: (0, 0)), ANY],
        out_shape=[jax.ShapeDtypeStruct((half,) + chunk, BF16), jax.ShapeDtypeStruct(chunk, F32),
                   jax.ShapeDtypeStruct((half,) + chunk, F32)],
        scratch_shapes=[pltpu.VMEM((2,) + chunk, F32), pltpu.VMEM(chunk, F32), pltpu.SemaphoreType.DMA((half,)),
                        pltpu.SemaphoreType.DMA((half,)), pltpu.SemaphoreType.DMA((1,))],
        args=(lhs, rhs))
    return wire, own, per_comm


def _proj_out_bwd(dz1, cat_t, w_out_b, comms=()):
    t_tok, d = dz1.shape
    d_mix = cat_t.shape[0]
    tm = min(512, t_tok)

    def body(dz1_ref, cat_ref, w_ref, dcat_ref, gw_ref):
        @pl.when(pl.program_id(0) == 0)
        def _():
            gw_ref[...] = jnp.zeros_like(gw_ref)

        dzb = dz1_ref[...].astype(BF16)
        dcat_ref[...] = _dot(w_ref[...], dzb, NT)
        gw_ref[...] += _dot(cat_ref[...], dzb)

    return _carry(
        body, name="proj_out_bwd", grid=(t_tok // tm,), comms=comms,
        in_specs=[pl.BlockSpec((tm, d), lambda i: (i, 0)), pl.BlockSpec((d_mix, tm), lambda i: (0, i)),
                  pl.BlockSpec((d_mix, d), lambda i: (0, 0))],
        out_specs=[pl.BlockSpec((d_mix, tm), lambda i: (0, i)), pl.BlockSpec((d_mix, d), lambda i: (0, 0))],
        out_shape=[jax.ShapeDtypeStruct((d_mix, t_tok), F32), jax.ShapeDtypeStruct((d_mix, d), F32)],
        args=(dz1, cat_t, w_out_b))


def _mixer_bwd(dcat_t, h_t, cos_t, sin_t, w_spatial, b_spatial, vln_g, vln_b, sinks, band_bias, comms=()):
    t_tok = h_t.shape[1]
    nb, n_step = t_tok // BLK, t_tok // MIX_W
    group = N_HEADS // N_KV_HEADS

    def body(sinks_ref, dcat_ref, u_ref, vg_ref, q_ref, kvc_ref, kvp_ref, cos_ref, sin_ref, cosp_ref, sinp_ref,
             wsp_ref, bsp_ref, g_ref, b_ref, bias_ref, dh_ref, dkvc_ref, dkvp_ref, gwsb_ref, gbsp_ref, gvln_ref, gsink_ref,
             dg_acc, db_acc, wm_scr, gws_ref):
        i = pl.program_id(0)

        @pl.when(i == 0)
        def _():
            gws_ref[...] = jnp.zeros_like(gws_ref)
            gbsp_ref[...] = jnp.zeros_like(gbsp_ref)
            gsink_ref[...] = jnp.zeros_like(gsink_ref)
            dg_acc[...] = jnp.zeros_like(dg_acc)
            db_acc[...] = jnp.zeros_like(db_acc)

        _mask_w_once(wsp_ref, wm_scr)

        g = g_ref[...]
        ua, ua_grad = _gelu_and_grad(u_ref[...])
        vv, vv_grad = _gelu_and_grad(vg_ref[...])
        vp, vhat, rstd = _ln_fwd_t(vv, g, b_ref[...])
        vpb = vp.astype(BF16)
        da = dcat_ref[0:D_GMLP, :]
        dmixed = da * ua
        dvp_blocks = []
        for b in range(MIX_BLOCKS):
            dvp_parts = []
            for hh in range(N_HEADS):
                rows = slice(hh * HEAD_DIM, (hh + 1) * HEAD_DIM)
                vpb_h = vpb[rows, _cols(b)]
                mixed = _dot(vpb_h, wm_scr[hh], NT) + bsp_ref[hh:hh + 1, :]
                dh_ref[COL_U + hh * HEAD_DIM:COL_U + (hh + 1) * HEAD_DIM, _cols(b)] = (
                    da[rows, _cols(b)] * mixed * ua_grad[rows, _cols(b)]).astype(BF16)
                dm = dmixed[rows, _cols(b)]
                dmb = dm.astype(BF16)
                gbsp_ref[hh:hh + 1, :] += jnp.sum(dm, axis=0, keepdims=True)
                gws_ref[hh] += _dot(dmb, vpb_h, TN)
                dvp_parts.append(_dot(dmb, wm_scr[hh]))
            dvp_blocks.append(jnp.concatenate(dvp_parts, axis=0))
        dvp = jnp.concatenate(dvp_blocks, axis=1)
        dgv, dbv = dvp * vhat, dvp
        for b in range(MIX_BLOCKS):
            dg_acc[...] += dgv[:, _cols(b)]
            db_acc[...] += dbv[:, _cols(b)]
        dh_ref[COL_V:COL_V + D_GMLP, :] = (_ln_bwd_t(dvp, vhat, rstd, g) * vv_grad).astype(BF16)

        kvc, cos, sin = kvc_ref[...], cos_ref[...], sin_ref[...]
        qr = (_rope_t(q_ref[...], cos, sin) * SCORE_SCALE).astype(BF16)
        sinks4 = [_group_lanes([jnp.full((1, BLK), sinks_ref[hh], F32) for hh in range(kv * group, (kv + 1) * group)])
                  for kv in range(N_KV_HEADS)]
        dq_blocks, dkv_cur, dkv_prev = [], [], []
        for b in range(MIX_BLOCKS):
            kv_cur, kv_prev, cosc, sinc, cosp, sinp, bias1 = _block_inputs(b, i, kvc, kvp_ref, cos, sin, cosp_ref, sinp_ref, bias_ref)
            k_t, k_n, v_t = _keys_values(kv_cur, kv_prev, cosc, sinc, cosp, sinp)
            v_n = jnp.concatenate([kv_prev[D_KV:].T, kv_cur[D_KV:].T], axis=0).astype(BF16)
            bias = _group_lanes([bias1] * group)
            dk, dv, dq_parts = [], [], []
            for kv in range(N_KV_HEADS):
                heads = range(kv * group, (kv + 1) * group)
                kv_rows = slice(kv * HEAD_DIM, (kv + 1) * HEAD_DIM)
                qs = _group_lanes([qr[hh * HEAD_DIM:(hh + 1) * HEAD_DIM, _cols(b)] for hh in heads])
                dos = _group_lanes([dcat_ref[D_GMLP + hh * HEAD_DIM:D_GMLP + (hh + 1) * HEAD_DIM, _cols(b)]
                                    for hh in heads]).astype(BF16)
                p, p_sink = _softmax_sink_t(_dot(k_n, _pad_head(qs, kv)) + bias, sinks4[kv])
                dp = _dot(v_n, _pad_head(dos, kv))
                delta = jnp.sum(p * dp, axis=0, keepdims=True)
                ds = (p * (dp - delta)).astype(BF16)
                dsink = p_sink * delta
                dq = _dot(k_t[kv_rows], ds) * SCORE_SCALE
                for j, hh in enumerate(heads):
                    gsink_ref[hh:hh + 1, :] -= dsink[:, j * BLK:(j + 1) * BLK]
                    dq_parts.append(dq[:, j * BLK:(j + 1) * BLK])
                dk.append(_dot(qs, ds, NT))
                dv.append(_dot(dos, p.astype(BF16), NT))
            dq_blocks.append(jnp.concatenate(dq_parts, axis=0))
            dk_all, dv_all = jnp.concatenate(dk, axis=0), jnp.concatenate(dv, axis=0)
            dkv_cur.append(jnp.concatenate([_rope_t(dk_all[:, BLK:], cosc, sinc, bwd=True), dv_all[:, BLK:]], axis=0))
            dkv_prev.append(jnp.concatenate([_rope_t(dk_all[:, :BLK], cosp, sinp, bwd=True), dv_all[:, :BLK]], axis=0))
        dh_ref[COL_Q:COL_Q + D_ATTN, :] = _rope_t(jnp.concatenate(dq_blocks, axis=1), cos, sin, bwd=True).astype(BF16)
        for b in range(MIX_BLOCKS):
            dkvc_ref[:, _cols(b)] = dkv_cur[b] + dkv_prev[b + 1] if b + 1 < MIX_BLOCKS else dkv_cur[b]
        dkvp_ref[...] = dkv_prev[0]

        @pl.when(i == n_step - 1)
        def _():
            causal = _causal()
            for hh in range(N_HEADS):
                gwsb_ref[hh] = jnp.where(causal, gws_ref[hh], 0.0).astype(BF16)
            gvln_ref[...] = jnp.zeros_like(gvln_ref)
            gvln_ref[0:1, :] = jnp.sum(dg_acc[...].T, axis=0, keepdims=True)
            gvln_ref[1:2, :] = jnp.sum(db_acc[...].T, axis=0, keepdims=True)

    full = lambda shape: pl.BlockSpec(shape, lambda i: (0,) * len(shape))
    return _carry(
        body, name="mixer_bwd", grid=(n_step,), comms=comms,
        in_specs=[pl.BlockSpec(memory_space=pltpu.SMEM), pl.BlockSpec((D_GMLP + D_ATTN, MIX_W), lambda i: (0, i))]
        + _h_specs() + _table_specs()
        + [full((N_HEADS, BLK, BLK)), full((N_HEADS, BLK)), full((D_GMLP, 1)), full((D_GMLP, 1)), BIAS_SPEC],
        out_specs=[pl.BlockSpec((COL_K, MIX_W), lambda i: (0, i)), pl.BlockSpec((2 * D_KV, MIX_W), lambda i: (0, i)),
                   pl.BlockSpec((2 * D_KV, BLK), lambda i: (0, (i + n_step - 1) % n_step)),
                   full((N_HEADS, BLK, BLK)), full((N_HEADS, BLK)), full((8, D_GMLP)), full((N_HEADS, LANES))],
        out_shape=[jax.ShapeDtypeStruct((COL_K, t_tok), BF16), jax.ShapeDtypeStruct((2 * D_KV, t_tok), F32),
                   jax.ShapeDtypeStruct((2 * D_KV, n_step * BLK), F32),
                   jax.ShapeDtypeStruct((N_HEADS, BLK, BLK), BF16), jax.ShapeDtypeStruct((N_HEADS, BLK), F32),
                   jax.ShapeDtypeStruct((8, D_GMLP), F32), jax.ShapeDtypeStruct((N_HEADS, LANES), F32)],
        scratch_shapes=[pltpu.VMEM((D_GMLP, BLK), F32), pltpu.VMEM((D_GMLP, BLK), F32), pltpu.VMEM((N_HEADS, BLK, BLK), BF16),
                        pltpu.VMEM((N_HEADS, BLK, BLK), F32)],
        args=(sinks, dcat_t, h_t, h_t, h_t, h_t, h_t, cos_t, sin_t, cos_t, sin_t, w_spatial, b_spatial, vln_g, vln_b, band_bias))


def _proj_in_wgrad(dh_b, dkvc_t, dkvp_t, xb, comms=()):
    t_tok, d = xb.shape
    d_main, d_kv = dh_b.shape[0], dkvc_t.shape[0]
    tm = min(1024, t_tok)

    def body(dh_ref, dkvc_ref, dkvp_ref, xb_ref, dkvb_ref, gw_ref):
        @pl.when(pl.program_id(0) == 0)
        def _():
            gw_ref[...] = jnp.zeros_like(gw_ref)

        for s in range(tm // MIX_W):
            last = slice((s + 1) * MIX_W - BLK, (s + 1) * MIX_W)
            dkvb_ref[:, s * MIX_W:(s + 1) * MIX_W - BLK] = dkvc_ref[:, s * MIX_W:(s + 1) * MIX_W - BLK].astype(BF16)
            dkvb_ref[:, last] = (dkvc_ref[:, last] + dkvp_ref[:, _cols(s)]).astype(BF16)
        gw_ref[0:d_main, :] += _dot(dh_ref[...], xb_ref[...])
        gw_ref[d_main:, :] += _dot(dkvb_ref[...], xb_ref[...])

    tok = lambda rows: pl.BlockSpec((rows, tm), lambda i: (0, i))
    return _carry(
        body, name="proj_in_wgrad", grid=(t_tok // tm,), comms=comms,
        in_specs=[tok(d_main), tok(d_kv), pl.BlockSpec((d_kv, tm // MIX_BLOCKS), lambda i: (0, i)),
                  pl.BlockSpec((tm, d), lambda i: (i, 0))],
        out_specs=[tok(d_kv), pl.BlockSpec((d_main + d_kv, d), lambda i: (0, 0))],
        out_shape=[jax.ShapeDtypeStruct((d_kv, t_tok), BF16), jax.ShapeDtypeStruct((d_main + d_kv, d), F32)],
        args=(dh_b, dkvc_t, dkvp_t, xb))


def _proj_in_dgrad(dh_b, dkv_b, dz1, w_in_t, comms=()):
    t_tok, d = dz1.shape
    d_main, d_kv = dh_b.shape[0], dkv_b.shape[0]
    tm = min(512, t_tok)

    def body(dh_ref, dkv_ref, dz1_ref, w_ref, dx_ref):
        dx_ref[...] = (ALPHA * dz1_ref[...] + _dot(dh_ref[...], w_ref[0:d_main, :], TN)
                       + _dot(dkv_ref[...], w_ref[d_main:, :], TN))

    return _carry(
        body, name="proj_in_dgrad", grid=(t_tok // tm,), comms=comms,
        in_specs=[pl.BlockSpec((d_main, tm), lambda i: (0, i)), pl.BlockSpec((d_kv, tm), lambda i: (0, i)),
                  pl.BlockSpec((tm, d), lambda i: (i, 0)), pl.BlockSpec((d_main + d_kv, d), lambda i: (0, 0))],
        out_specs=[pl.BlockSpec((tm, d), lambda i: (i, 0))],
        out_shape=[jax.ShapeDtypeStruct((t_tok, d), F32)],
        args=(dh_b, dkv_b, dz1, w_in_t))


def _adamw(w, g, m, v):
    m = ADAM_B1 * m + (1.0 - ADAM_B1) * g
    v = ADAM_B2 * v + (1.0 - ADAM_B2) * (g * g)
    m_hat = m / (1.0 - ADAM_B1 ** ADAM_STEP)
    v_hat = v / (1.0 - ADAM_B2 ** ADAM_STEP)
    delta = -ADAM_LR * (m_hat / (jnp.sqrt(v_hat) + ADAM_EPS) + ADAM_WD * w)
    return delta, m, v


def _row_tiled(name, own, recv, extra, n_out, finish, comms=()):
    r, c = own.shape
    recv = [] if recv is None else list(recv)
    k = max(len(recv), 1)
    n = max(k, -(-r // 512))
    tr, per = r // n, n // k
    blk = pl.BlockSpec((tr, c), lambda i: (i, 0))

    def body(own_ref, *refs):
        recv_refs, rest = refs[:len(recv)], refs[len(recv):]
        ins, outs = rest[:len(extra)], rest[len(extra):]

        def tile(recv_ref):
            g = own_ref[...]
            if recv_ref is not None:
                g = ((g + recv_ref[0].astype(F32)) + recv_ref[1].astype(F32)) + recv_ref[2].astype(F32)
            for o_ref, val in zip(outs, finish(g, *[a[...] for a in ins])):
                o_ref[...] = val

        if len(recv) <= 1:
            tile(recv_refs[0] if recv else None)
        else:
            for p in range(k):
                pl.when(pl.program_id(0) // per == p)(functools.partial(tile, recv_refs[p]))

    recv_specs = [pl.BlockSpec((3, tr, c), lambda i, p=p: (0, jnp.clip(i - p * per, 0, per - 1), 0)) for p in range(len(recv))]
    return _carry(
        body, name=name, grid=(n,), comms=comms,
        in_specs=[blk] + recv_specs + [blk] * len(extra),
        out_specs=[blk] * n_out, out_shape=[jax.ShapeDtypeStruct((r, c), F32)] * n_out,
        args=(own, *recv, *extra))


def _adamw_shard(name, own, recv, w, m, v, comms=()):
    def finish(g, w_t, m_t, v_t):
        return (g,) + _adamw(w_t, g, m_t, v_t)

    return _row_tiled(name, own, recv, (w, m, v), 4, finish, comms)


VEC_VLN, VEC_LN1G, VEC_LN1B, VEC_LN2G, VEC_LN2B, VEC_SINK, VEC_LOSS, VEC_BSP, VEC_ROWS = 0, 1, 2, 3, 4, 5, 6, 8, 16


def _adamw_small(parts_w, parts_vec, params):
    n = parts_w.shape[0]
    flat = [a for p in params for a in p]
    shapes = [p[0].shape for p in params]

    def grads(gw, gv):
        return [gw, gv[VEC_VLN:VEC_VLN + 1, 0:D_GMLP], gv[VEC_VLN:VEC_VLN + 1, D_GMLP:2 * D_GMLP],
                gv[VEC_BSP:VEC_BSP + N_HEADS, 0:BLK], gv[VEC_LN1G:VEC_LN1G + 1], gv[VEC_LN1B:VEC_LN1B + 1],
                gv[VEC_LN2G:VEC_LN2G + 1], gv[VEC_LN2B:VEC_LN2B + 1], gv[VEC_SINK:VEC_SINK + 1, 0:N_HEADS]]

    def body(pw_ref, pv_ref, *refs):
        ins, outs = refs[:len(flat)], refs[len(flat):]
        gw, gv = pw_ref[0].astype(F32), pv_ref[0]
        for k in range(1, n):
            gw, gv = gw + pw_ref[k].astype(F32), gv + pv_ref[k]
        for i, g in enumerate(grads(gw, gv)):
            w_ref, m_ref, v_ref = ins[3 * i:3 * i + 3]
            delta, m_new, v_new = _adamw(w_ref[...], g, m_ref[...], v_ref[...])
            for o_ref, val in zip(outs[4 * i:4 * i + 4], (g, delta, m_new, v_new)):
                o_ref[...] = val
        outs[-1][...] = gv[VEC_LOSS:VEC_LOSS + 1, 0:LANES]

    whole = lambda shape: pl.BlockSpec(shape, lambda i: (0,) * len(shape))
    res = _carry(
        body, name="adamw_small", grid=(1,),
        in_specs=[whole(parts_w.shape), whole(parts_vec.shape)] + [whole(a.shape) for a in flat],
        out_specs=[whole(s) for s in shapes for _ in range(4)] + [whole((1, LANES))],
        out_shape=[jax.ShapeDtypeStruct(s, F32) for s in shapes for _ in range(4)] + [jax.ShapeDtypeStruct((1, LANES), F32)],
        args=(parts_w, parts_vec, *flat))[0]
    return [res[4 * i:4 * i + 4] for i in range(len(params))], res[-1]


def _pair_sum(name, parts, recv, core_chip, comms=()):
    _, r, c = parts.shape
    tr = r if r <= 512 else 512

    def body(cc_ref, a_ref, b_ref, wire_ref, own_ref):
        s = a_ref[...] + b_ref[...]
        wire_ref[...] = s.astype(BF16)

        @pl.when(pl.program_id(1) == cc_ref[1])
        def _():
            own_ref[...] = s

    return _carry(
        body, name=name, grid=(r // tr, 4), prefetch=(core_chip,), comms=comms,
        in_specs=[pl.BlockSpec((None, tr, c), lambda i, q, cc: (2 * q + cc[0], i, 0)),
                  pl.BlockSpec((None, tr, c), lambda i, q, cc: (q, i, 0))],
        out_specs=[pl.BlockSpec((None, tr, c), lambda i, q, cc: (q, i, 0)), pl.BlockSpec((tr, c), lambda i, q, cc: (i, 0))],
        out_shape=[jax.ShapeDtypeStruct((4, r, c), BF16), jax.ShapeDtypeStruct((r, c), F32)],
        args=(parts, recv))


def kernel(x, positions, w_in, v_ln_g, v_ln_b, w_spatial, b_spatial, sinks, w_out, ln1_g, ln1_b, w_ff1, w_ff2, ln2_g, ln2_b, loss_target, m_w_in, m_v_ln_g, m_v_ln_b, m_w_spatial, m_b_spatial, m_sinks, m_w_out, m_ln1_g, m_ln1_b, m_w_ff1, m_w_ff2, m_ln2_g, m_ln2_b, v_w_in, v_v_ln_g, v_v_ln_b, v_w_spatial, v_b_spatial, v_sinks, v_w_out, v_ln1_g, v_ln1_b, v_w_ff1, v_w_ff2, v_ln2_g, v_ln2_b):
    _, t_tok, d = x.shape
    xi, yi, ci = _place()
    core_chip = jnp.stack([ci, 2 * xi + yi]).astype(jnp.int32)
    x2 = x.reshape(t_tok, d)
    target = loss_target.reshape(t_tok, d)
    inv_freq = ROPE_THETA ** (-jnp.arange(0, HEAD_DIM, 2, dtype=F32) / HEAD_DIM)
    wsp, bsp, sink_vec = w_spatial[0], b_spatial[0], sinks[0]
    vg_col, vb_col = v_ln_g.reshape(D_GMLP, 1), v_ln_b.reshape(D_GMLP, 1)
    big = {"in": w_in[0], "out": w_out[0], "ff1": w_ff1[0], "ff2": w_ff2[0]}
    half1, half2 = big["ff1"].shape[1] // 2, big["ff2"].shape[0] // 2
    w1_mine = [big["ff1"][:, :half1].astype(BF16), big["ff1"][:, half1:].astype(BF16)]
    w2_mine = [big["ff2"][:half2].astype(BF16), big["ff2"][half2:].astype(BF16)]

    (cos_t, sin_t), ((g_in,),) = _rope_tables(
        positions, jnp.tile(inv_freq, 2).reshape(HEAD_DIM, 1), comms=[_gather_comm([big["in"].T.astype(BF16)])])
    w_in_t = g_in.reshape(D_IN, d)
    (h_t, xb), ((g_out, w1_a),) = _proj_in(x2, w_in_t, comms=[_gather_comm([big["out"].astype(BF16), w1_mine[0]])])
    w_out_b = g_out.reshape(-1, d)
    band_bias = _band_bias()
    (cat_t,), ((w1_b, w2_a),) = _mixer_fwd(h_t, cos_t, sin_t, wsp, bsp, vg_col, vb_col, sink_vec, band_bias,
                                           comms=[_gather_comm([w1_mine[1], w2_mine[0]])])
    (xhat1, rstd1, x1b, r_first), ((w2_b,),) = _proj_out(cat_t, x2, w_out_b, ln1_g, ln1_b, w1_a, comms=[_gather_comm([w2_mine[1]])])
    act_b, dpre_b, dz2b, dz1, stats = _ffn_fwd_bwd(xhat1, rstd1, x1b, r_first, target, [w1_a, w1_b], [w2_a, w2_b],
                                                   ln1_g, ln1_b, ln2_g, ln2_b)

    (dcat_t, gw_out), _ = _proj_out_bwd(dz1, cat_t, w_out_b)
    p_out = gw_out.reshape(N_DEV, -1, d)
    wire_ff1, own_ff1, ((s_out,),) = _ffn_wgrad("ffn_wgrad1", x1b, dpre_b, False, core_chip, comms=[_sibling_comm([p_out])])
    (wire_out, own_out), _ = _pair_sum("pair_sum_out", p_out, s_out, core_chip)
    wire_ff2, own_ff2, ((r_ff1,),) = _ffn_wgrad("ffn_wgrad2", act_b, dz2b, True, core_chip, comms=[_chips_comm([wire_ff1])])
    (dh_b, dkvc_t, dkvp_t, g_wsp, g_bsp, g_vln, g_sink), ((r_ff2, r_out),) = _mixer_bwd(
        dcat_t, h_t, cos_t, sin_t, wsp, bsp, vg_col, vb_col, sink_vec, band_bias,
        comms=[_chips_comm([wire_ff2, wire_out])])
    sink_row = jnp.pad(g_sink.sum(axis=1).reshape(1, N_HEADS), ((0, 0), (0, d - N_HEADS)))
    small_vec = jnp.concatenate([g_vln[0:2].reshape(1, d), stats[0:4], sink_row, stats[4:5], jnp.zeros((1, d), F32),
                                 jnp.pad(g_bsp, ((0, 0), (0, d - BLK)))], axis=0)
    (dkv_b, gw_in_t), ((parts_w, parts_vec),) = _proj_in_wgrad(
        dh_b, dkvc_t, dkvp_t, xb, comms=[_gather_comm([g_wsp.reshape(-1, BLK), small_vec])])
    p_in = gw_in_t.reshape(N_DEV, -1, d)

    out_out, ((s_in,),) = _adamw_shard("adamw_out", own_out, [r_out], big["out"], m_w_out[0], v_w_out[0], comms=[_sibling_comm([p_in])])
    (wire_in, own_in), _ = _pair_sum("pair_sum_in", p_in, s_in, core_chip)
    (grad_x,), ((r_in,),) = _proj_in_dgrad(dh_b, dkv_b, dz1, w_in_t, comms=[_chips_comm([wire_in])])
    ff1_out, _ = _adamw_shard("adamw_ff1", own_ff1, [r_ff1], big["ff1"], m_w_ff1[0], v_w_ff1[0])
    ff2_out, _ = _adamw_shard("adamw_ff2", own_ff2, [r_ff2], big["ff2"], m_w_ff2[0], v_w_ff2[0])
    in_out_t, _ = _adamw_shard("adamw_in", own_in, [r_in], big["in"].T, m_w_in[0].T, v_w_in[0].T)
    in_out = [o.T for o in in_out_t]
    small = [(w_spatial, m_w_spatial, v_w_spatial), (v_ln_g, m_v_ln_g, v_v_ln_g), (v_ln_b, m_v_ln_b, v_v_ln_b),
             (b_spatial, m_b_spatial, v_b_spatial), (ln1_g, m_ln1_g, v_ln1_g), (ln1_b, m_ln1_b, v_ln1_b),
             (ln2_g, m_ln2_g, v_ln2_g), (ln2_b, m_ln2_b, v_ln2_b), (sinks, m_sinks, v_sinks)]
    views = [(-1, BLK), None, None, (N_HEADS, BLK)] + [None] * 5
    small_res, loss_row = _adamw_small(parts_w, parts_vec, [
        tuple(a if vw is None else a.reshape(vw) for a in p) for p, vw in zip(small, views)])
    small_out = [[o.reshape(p[0].shape) for o in res] for res, p in zip(small_res, small)]
    loss = loss_row[0, 0]

    big_out = {0: in_out, 6: out_out, 9: ff1_out, 10: ff2_out}
    small_slot = {3: 0, 1: 1, 2: 2, 4: 3, 7: 4, 8: 5, 11: 6, 12: 7, 5: 8}
    outs = [loss, grad_x.reshape(x.shape)]
    for kind in range(4):
        for wi in range(13):
            outs.append(big_out[wi][kind][None] if wi in big_out else small_out[small_slot[wi]][kind])
    return tuple(outs)
```

```python
import math

import jax
import jax.numpy as jnp
from jax import lax
from jax.experimental import pallas as pl
from jax.experimental.pallas import tpu as pltpu

F32 = jnp.float32
BF16 = jnp.bfloat16
MESH = pl.DeviceIdType.MESH

HEAD_DIM = 64
N_HEADS = 8
N_KV_HEADS = 2
BLK = 128
D_GMLP = N_HEADS * HEAD_DIM
D_ATTN = N_HEADS * HEAD_DIM
D_KV = N_KV_HEADS * HEAD_DIM
D_IN = 2 * D_GMLP + D_ATTN + 2 * D_KV
COL_U, COL_V, COL_Q, COL_K = 0, D_GMLP, 2 * D_GMLP, 2 * D_GMLP + D_ATTN
ROPE_THETA = 10000.0
LN_EPS = 1e-5
ALPHA = 2.0 ** 0.25
NEG_INF = -1e30
SCORE_SCALE = 1.0 / math.sqrt(HEAD_DIM)
ADAM_LR, ADAM_B1, ADAM_B2, ADAM_EPS, ADAM_WD, ADAM_STEP = 0.001, 0.9, 0.999, 1e-08, 0.01, 10
N_DEV = 8
LANES = 128
VMEM_LIMIT = 56 * 1024 * 1024
FFN_ROWS = 256

NT = (((1,), (1,)), ((), ()))
TN = (((0,), (0,)), ((), ()))


def _params(*sem):
    return pltpu.CompilerParams(dimension_semantics=sem, vmem_limit_bytes=VMEM_LIMIT)


def _dot(a, b, dims=None):
    if dims is None:
        return jnp.dot(a, b, preferred_element_type=F32)
    return lax.dot_general(a, b, dims, preferred_element_type=F32)


def _mean(a):
    return jnp.mean(a, axis=-1, keepdims=True)


def _ln_fwd(z, g, b):
    zc = z - _mean(z)
    rstd = lax.rsqrt(_mean(zc * zc) + LN_EPS)
    xhat = zc * rstd
    return xhat * g + b, xhat, rstd


def _ln_bwd(dy, xhat, rstd, g):
    dxhat = dy * g
    return rstd * (dxhat - _mean(dxhat) - xhat * _mean(dxhat * xhat))


_GELU_C = math.sqrt(2.0 / math.pi)


def _gelu(x):
    t = jnp.tanh(_GELU_C * (x + 0.044715 * (x * x * x)))
    return 0.5 * x * (1.0 + t)


def _gelu_and_grad(x):
    x2 = x * x
    t = jnp.tanh(_GELU_C * (x + 0.044715 * (x2 * x)))
    hx, ht = 0.5 * x, 0.5 * (1.0 + t)
    return x * ht, ht + hx * (1.0 - t * t) * (_GELU_C * (1.0 + 3.0 * 0.044715 * x2))


def _mean0(a):
    return jnp.mean(a, axis=0, keepdims=True)


def _ln_fwd_t(z, g, b):
    zc = z - _mean0(z)
    rstd = lax.rsqrt(_mean0(zc * zc) + LN_EPS)
    xhat = zc * rstd
    return xhat * g + b, xhat, rstd


def _ln_bwd_t(dy, xhat, rstd, g):
    dxhat = dy * g
    return rstd * (dxhat - _mean0(dxhat) - xhat * _mean0(dxhat * xhat))


def _rope_t(t, cos, sin_signed, bwd=False):
    half = HEAD_DIM // 2
    outs = []
    for r in range(0, t.shape[0], HEAD_DIM):
        th = t[r:r + HEAD_DIM]
        sw = jnp.concatenate([th[half:], th[:half]], axis=0) * sin_signed
        outs.append(th * cos - sw if bwd else th * cos + sw)
    return jnp.concatenate(outs, axis=0)


ANY = pl.BlockSpec(memory_space=pl.ANY)


def _place():
    return lax.axis_index("x"), lax.axis_index("y"), lax.axis_index("c")


class _Comm:
    def __init__(self, ins, outs, sems, start, finish):
        self.ins, self.outs, self.sems, self.start, self.finish = ins, outs, sems, start, finish


def _gather_comm(arrs):
    n = len(arrs)

    def parts(ins, outs, sems):
        send_sems, recv_sems, local_sems = sems
        x, y, c = _place()
        me, sibling = (x, y, c), (x, y, 1 - c)
        chips = [(1 - x, y), (x, 1 - y), (1 - x, 1 - y)]

        def copy(a, k, block, to, src=None):
            px, py, pc = block
            dst = outs[a].at[4 * px + 2 * py + pc]
            return pltpu.make_async_remote_copy(
                src_ref=dst if src is None else src, dst_ref=dst,
                send_sem=send_sems.at[a, k], recv_sem=recv_sems.at[a, k], device_id=to, device_id_type=MESH)

        mine = [pltpu.make_async_copy(ins[a], outs[a].at[4 * x + 2 * y + c], local_sems.at[a]) for a in range(n)]
        first = []
        for a in range(n):
            first.append(copy(a, 0, me, sibling, src=ins[a]))
            first += [copy(a, 1 + j, me, (*chip, c), src=ins[a]) for j, chip in enumerate(chips)]
        return copy, mine, first, me, sibling, chips, c

    def start(ins, outs, sems):
        _, mine, first, *_ = parts(ins, outs, sems)
        for cp in mine + first:
            cp.start()

    def finish(ins, outs, sems):
        copy, mine, first, me, sibling, chips, c = parts(ins, outs, sems)
        passed = []
        for j, chip in enumerate(chips):
            for a in range(n):
                copy(a, 1 + j, (*chip, c), me).wait_recv()
                fwd = copy(a, 4 + j, (*chip, c), sibling)
                fwd.start()
                passed.append(fwd)
        for a in range(n):
            copy(a, 0, sibling, me).wait_recv()
        for j, chip in enumerate(chips):
            for a in range(n):
                copy(a, 4 + j, (*chip, 1 - c), me).wait_recv()
        for cp in first + passed:
            cp.wait_send()
        for cp in mine:
            cp.wait()

    return _Comm(list(arrs), [jax.ShapeDtypeStruct((N_DEV,) + a.shape, a.dtype) for a in arrs],
                 [pltpu.SemaphoreType.DMA((n, 7)), pltpu.SemaphoreType.DMA((n, 7)), pltpu.SemaphoreType.DMA((n,))],
                 start, finish)


def _sibling_comm(parts):
    n = len(parts)

    def copies(ins, outs, sems):
        x, y, c = _place()
        return [pltpu.make_async_remote_copy(
            src_ref=ins[a].at[2 * q + (1 - c)], dst_ref=outs[a].at[q],
            send_sem=sems[0].at[a, q], recv_sem=sems[1].at[a, q],
            device_id=(x, y, 1 - c), device_id_type=MESH) for a in range(n) for q in range(4)]

    return _Comm(list(parts), [jax.ShapeDtypeStruct((4,) + p.shape[1:], p.dtype) for p in parts],
                 [pltpu.SemaphoreType.DMA((n, 4)), pltpu.SemaphoreType.DMA((n, 4))],
                 lambda *r: [cp.start() for cp in copies(*r)], lambda *r: [cp.wait() for cp in copies(*r)])


def _chips_comm(chip_parts, rows=None):
    n = len(chip_parts)
    r0, nr = (0, None) if rows is None else rows

    def copies(ins, outs, sems):
        x, y, c = _place()
        chips = [(1 - x, y), (x, 1 - y), (1 - x, 1 - y)]
        src = lambda a, q: ins[a].at[q] if rows is None else ins[a].at[q, pl.ds(r0, nr)]
        return [pltpu.make_async_remote_copy(
            src_ref=src(a, 2 * px + py), dst_ref=outs[a].at[k],
            send_sem=sems[0].at[a, k], recv_sem=sems[1].at[a, k],
            device_id=(px, py, c), device_id_type=MESH) for a in range(n) for k, (px, py) in enumerate(chips)]

    shape = lambda p: (3,) + p.shape[1:] if rows is None else (3, nr) + p.shape[2:]
    return _Comm(list(chip_parts), [jax.ShapeDtypeStruct(shape(p), p.dtype) for p in chip_parts],
                 [pltpu.SemaphoreType.DMA((n, 3)), pltpu.SemaphoreType.DMA((n, 3))],
                 lambda *r: [cp.start() for cp in copies(*r)], lambda *r: [cp.wait() for cp in copies(*r)])


def _carry(body, *, name, grid, in_specs, out_specs, out_shape, args, comms=(), scratch_shapes=(), prefetch=()):
    n_pre, n_in, n_out, n_scr = len(prefetch), len(in_specs), len(out_specs), len(scratch_shapes)
    c_ins = [a for cm in comms for a in cm.ins]
    c_outs = [s for cm in comms for s in cm.outs]
    c_sems = [s for cm in comms for s in cm.sems]

    def wrapped(*refs):
        pre, refs = refs[:n_pre], refs[n_pre:]
        ins, refs = refs[:n_in], refs[n_in:]
        cins, refs = refs[:len(c_ins)], refs[len(c_ins):]
        outs, refs = refs[:n_out], refs[n_out:]
        couts, refs = refs[:len(c_outs)], refs[len(c_outs):]
        scr, sems = refs[:n_scr], refs[n_scr:]
        groups, i0, o0, s0 = [], 0, 0, 0
        for cm in comms:
            groups.append((cm, cins[i0:i0 + len(cm.ins)], couts[o0:o0 + len(cm.outs)], sems[s0:s0 + len(cm.sems)]))
            i0, o0, s0 = i0 + len(cm.ins), o0 + len(cm.outs), s0 + len(cm.sems)
        first = pl.program_id(0) == 0
        last = pl.program_id(0) == grid[0] - 1
        for ax in range(1, len(grid)):
            first = first & (pl.program_id(ax) == 0)
            last = last & (pl.program_id(ax) == grid[ax] - 1)
        if comms:
            @pl.when(first)
            def _():
                for cm, ci, co, cs in groups:
                    cm.start(ci, co, cs)
        body(*pre, *ins, *outs, *scr)
        if comms:
            @pl.when(last)
            def _():
                for cm, ci, co, cs in groups:
                    cm.finish(ci, co, cs)

    grid_spec = pltpu.PrefetchScalarGridSpec(
        num_scalar_prefetch=n_pre, grid=grid,
        in_specs=list(in_specs) + [ANY] * len(c_ins), out_specs=list(out_specs) + [ANY] * len(c_outs),
        scratch_shapes=list(scratch_shapes) + c_sems)
    res = pl.pallas_call(
        wrapped, name=name, grid_spec=grid_spec, out_shape=list(out_shape) + c_outs,
        compiler_params=_params(*(["arbitrary"] * len(grid))),
    )(*prefetch, *args, *c_ins)
    outs, rest, per_comm = res[:n_out], res[n_out:], []
    for cm in comms:
        per_comm.append(rest[:len(cm.outs)])
        rest = rest[len(cm.outs):]
    return outs, per_comm


def _rope_tables(pos_row, inv_freq_col, comms=()):
    t_tok = pos_row.shape[1]
    tm = min(512, t_tok)

    def body(pos_ref, invf_ref, cos_ref, sin_ref):
        ang = pos_ref[...].astype(F32) * invf_ref[...]
        row = lax.broadcasted_iota(jnp.int32, ang.shape, 0)
        cos_ref[...] = jnp.cos(ang)
        sin_ref[...] = jnp.sin(ang) * jnp.where(row < HEAD_DIM // 2, -1.0, 1.0)

    return _carry(
        body, name="rope_tables", grid=(t_tok // tm,), comms=comms,
        in_specs=[pl.BlockSpec((1, tm), lambda i: (0, i)), pl.BlockSpec((HEAD_DIM, 1), lambda i: (0, 0))],
        out_specs=[pl.BlockSpec((HEAD_DIM, tm), lambda i: (0, i))] * 2,
        out_shape=[jax.ShapeDtypeStruct((HEAD_DIM, t_tok), F32)] * 2,
        args=(pos_row, inv_freq_col))


def _proj_in(x2, w_in_t, comms=()):
    t_tok, d = x2.shape
    d_in = w_in_t.shape[0]
    tm = min(512, t_tok)

    def body(x_ref, w_ref, h_ref, xb_ref):
        xb = x_ref[...].astype(BF16)
        xb_ref[...] = xb
        h_ref[...] = _dot(w_ref[...], xb, NT)

    return _carry(
        body, name="proj_in", grid=(t_tok // tm,), comms=comms,
        in_specs=[pl.BlockSpec((tm, d), lambda i: (i, 0)), pl.BlockSpec((d_in, d), lambda i: (0, 0))],
        out_specs=[pl.BlockSpec((d_in, tm), lambda i: (0, i)), pl.BlockSpec((tm, d), lambda i: (i, 0))],
        out_shape=[jax.ShapeDtypeStruct((d_in, t_tok), F32), jax.ShapeDtypeStruct((t_tok, d), BF16)],
        args=(x2, w_in_t))


MIX_BLOCKS = 2
MIX_W = MIX_BLOCKS * BLK


def _prev_block(i):
    return jnp.maximum(MIX_BLOCKS * i - 1, 0)


def _h_specs():
    kv_row = COL_K // (2 * D_KV)
    return [
        pl.BlockSpec((D_GMLP, MIX_W), lambda i: (0, i)),
        pl.BlockSpec((D_GMLP, MIX_W), lambda i: (1, i)),
        pl.BlockSpec((D_ATTN, MIX_W), lambda i: (2, i)),
        pl.BlockSpec((2 * D_KV, MIX_W), lambda i: (kv_row, i)),
        pl.BlockSpec((2 * D_KV, BLK), lambda i: (kv_row, _prev_block(i))),
    ]


def _table_specs():
    return [
        pl.BlockSpec((HEAD_DIM, MIX_W), lambda i: (0, i)),
        pl.BlockSpec((HEAD_DIM, MIX_W), lambda i: (0, i)),
        pl.BlockSpec((HEAD_DIM, BLK), lambda i: (0, _prev_block(i))),
        pl.BlockSpec((HEAD_DIM, BLK), lambda i: (0, _prev_block(i))),
    ]


def _cols(b):
    return slice(b * BLK, (b + 1) * BLK)


def _block_inputs(b, i, kvc, kvp_ref, cos, sin, cosp_ref, sinp_ref, bias_ref):
    if b == 0:
        kv_prev, cos_prev, sin_prev, bias = kvp_ref[...], cosp_ref[...], sinp_ref[...], bias_ref[jnp.minimum(i, 1)]
    else:
        kv_prev, cos_prev, sin_prev, bias = kvc[:, _cols(b - 1)], cos[:, _cols(b - 1)], sin[:, _cols(b - 1)], bias_ref[1]
    return kvc[:, _cols(b)], kv_prev, cos[:, _cols(b)], sin[:, _cols(b)], cos_prev, sin_prev, bias


def _band_bias():
    ki = lax.broadcasted_iota(jnp.int32, (2, 2 * BLK, BLK), 1)
    qi = lax.broadcasted_iota(jnp.int32, (2, 2 * BLK, BLK), 2)
    later = lax.broadcasted_iota(jnp.int32, (2, 2 * BLK, BLK), 0) > 0
    dist = qi + BLK - ki
    return jnp.where((dist >= 0) & (dist < BLK) & ((ki >= BLK) | later), 0.0, NEG_INF).astype(F32)


BIAS_SPEC = pl.BlockSpec((2, 2 * BLK, BLK), lambda i: (0, 0, 0))


def _keys_values(kvc, kvp, cosc, sinc, cosp, sinp):
    kp, kc = _rope_t(kvp[:D_KV], cosp, sinp), _rope_t(kvc[:D_KV], cosc, sinc)
    k_t = jnp.concatenate([kp, kc], axis=1).astype(BF16)
    k_n = jnp.concatenate([kp.T, kc.T], axis=0).astype(BF16)
    v_t = jnp.concatenate([kvp[D_KV:], kvc[D_KV:]], axis=1).astype(BF16)
    return k_t, k_n, v_t


def _pad_head(th, kv):
    z = jnp.zeros_like(th)
    return jnp.concatenate([th, z] if kv == 0 else [z, th], axis=0)


def _group_lanes(parts):
    return jnp.concatenate(parts, axis=1)


def _softmax_sink_t(s, sink):
    m = jnp.maximum(jnp.max(s, axis=0, keepdims=True), sink)
    e = jnp.exp(s - m)
    es = jnp.exp(sink - m)
    r = 1.0 / (jnp.sum(e, axis=0, keepdims=True) + es)
    return e * r, es * r


def _causal():
    row = lax.broadcasted_iota(jnp.int32, (BLK, BLK), 0)
    col = lax.broadcasted_iota(jnp.int32, (BLK, BLK), 1)
    return row >= col


def _mask_w_once(wsp_ref, wm_scr):
    @pl.when(pl.program_id(0) == 0)
    def _():
        causal = _causal()
        for hh in range(N_HEADS):
            wm_scr[hh] = jnp.where(causal, wsp_ref[hh], 0.0).astype(BF16)


def _mixer_fwd(h_t, cos_t, sin_t, w_spatial, b_spatial, vln_g, vln_b, sinks, band_bias, comms=()):
    t_tok = h_t.shape[1]
    group = N_HEADS // N_KV_HEADS

    def body(sinks_ref, u_ref, vg_ref, q_ref, kvc_ref, kvp_ref, cos_ref, sin_ref, cosp_ref, sinp_ref,
             wsp_ref, bsp_ref, g_ref, b_ref, bias_ref, cat_ref, wm_scr):
        i = pl.program_id(0)
        _mask_w_once(wsp_ref, wm_scr)
        ua = _gelu(u_ref[...])
        vp, _, _ = _ln_fwd_t(_gelu(vg_ref[...]), g_ref[...], b_ref[...])
        vpb = vp.astype(BF16)
        for b in range(MIX_BLOCKS):
            for hh in range(N_HEADS):
                rows = slice(hh * HEAD_DIM, (hh + 1) * HEAD_DIM)
                mixed = _dot(vpb[rows, _cols(b)], wm_scr[hh], NT) + bsp_ref[hh:hh + 1, :]
                cat_ref[rows, _cols(b)] = (ua[rows, _cols(b)] * mixed).astype(BF16)

        kvc, cos, sin = kvc_ref[...], cos_ref[...], sin_ref[...]
        qr = (_rope_t(q_ref[...], cos, sin) * SCORE_SCALE).astype(BF16)
        sinks4 = [_group_lanes([jnp.full((1, BLK), sinks_ref[hh], F32) for hh in range(kv * group, (kv + 1) * group)])
                  for kv in range(N_KV_HEADS)]
        for b in range(MIX_BLOCKS):
            kv_cur, kv_prev, cosc, sinc, cosp, sinp, bias1 = _block_inputs(b, i, kvc, kvp_ref, cos, sin, cosp_ref, sinp_ref, bias_ref)
            _, k_n, v_t = _keys_values(kv_cur, kv_prev, cosc, sinc, cosp, sinp)
            bias = _group_lanes([bias1] * group)
            for kv in range(N_KV_HEADS):
                heads = range(kv * group, (kv + 1) * group)
                qs = _group_lanes([qr[hh * HEAD_DIM:(hh + 1) * HEAD_DIM, _cols(b)] for hh in heads])
                p, _ = _softmax_sink_t(_dot(k_n, _pad_head(qs, kv)) + bias, sinks4[kv])
                o = _dot(v_t[kv * HEAD_DIM:(kv + 1) * HEAD_DIM], p.astype(BF16)).astype(BF16)
                for j, hh in enumerate(heads):
                    cat_ref[D_GMLP + hh * HEAD_DIM:D_GMLP + (hh + 1) * HEAD_DIM, _cols(b)] = o[:, j * BLK:(j + 1) * BLK]

    full = lambda shape: pl.BlockSpec(shape, lambda i: (0,) * len(shape))
    return _carry(
        body, name="mixer_fwd", grid=(t_tok // MIX_W,), comms=comms,
        in_specs=[pl.BlockSpec(memory_space=pltpu.SMEM)] + _h_specs() + _table_specs() + [
            full((N_HEADS, BLK, BLK)), full((N_HEADS, BLK)), full((D_GMLP, 1)), full((D_GMLP, 1)), BIAS_SPEC],
        out_specs=[pl.BlockSpec((D_GMLP + D_ATTN, MIX_W), lambda i: (0, i))],
        out_shape=[jax.ShapeDtypeStruct((D_GMLP + D_ATTN, t_tok), BF16)],
        scratch_shapes=[pltpu.VMEM((N_HEADS, BLK, BLK), BF16)],
        args=(sinks, h_t, h_t, h_t, h_t, h_t, cos_t, sin_t, cos_t, sin_t, w_spatial, b_spatial, vln_g, vln_b, band_bias))


def _proj_out(cat_t, x2, w_out_b, ln1_g, ln1_b, comms=()):
    t_tok, d = x2.shape
    tm = min(512, t_tok)

    def body(cat_ref, x_ref, w_ref, g_ref, b_ref, xhat_ref, rstd_ref, x1b_ref):
        x1, xhat, rstd = _ln_fwd(ALPHA * x_ref[...] + _dot(cat_ref[...], w_ref[...], TN), g_ref[...], b_ref[...])
        xhat_ref[...] = xhat
        rstd_ref[...] = rstd
        x1b_ref[...] = x1.astype(BF16)

    tok = lambda w: pl.BlockSpec((tm, w), lambda i: (i, 0))
    vec = pl.BlockSpec((1, d), lambda i: (0, 0))
    return _carry(
        body, name="proj_out", grid=(t_tok // tm,), comms=comms,
        in_specs=[pl.BlockSpec((cat_t.shape[0], tm), lambda i: (0, i)), tok(d), pl.BlockSpec(w_out_b.shape, lambda i: (0, 0)), vec, vec],
        out_specs=[tok(d), tok(1), tok(d)],
        out_shape=[jax.ShapeDtypeStruct((t_tok, d), F32), jax.ShapeDtypeStruct((t_tok, 1), F32), jax.ShapeDtypeStruct((t_tok, d), BF16)],
        args=(cat_t, x2, w_out_b, ln1_g, ln1_b))


def _ffn_fwd_bwd(xhat1, rstd1, x1b, target, w1_parts, w2_parts, ln1_g, ln1_b, ln2_g, ln2_b):
    t_tok, d = xhat1.shape
    n_part = len(w1_parts)
    n_chunk, _, fp = w1_parts[0].shape
    f = n_chunk * n_part * fp
    tm = min(FFN_ROWS, t_tok)

    def body(xhat1_ref, rstd1_ref, x1b_ref, tgt_ref, *refs):
        w1_hbm, w2_hbm = refs[:n_part], refs[n_part:2 * n_part]
        (g1_ref, b1_ref, g2_ref, b2_ref, act_ref, dpre_ref, dz2b_ref, dz1_ref, stats_ref,
         r_scr, w1_ref, w2_ref, w_sems) = refs[2 * n_part:]

        @pl.when(pl.program_id(0) == 0)
        def _():
            stats_ref[...] = jnp.zeros_like(stats_ref)
            loads = []
            for j in range(n_chunk):
                for p in range(n_part):
                    units = pl.ds((j * n_part + p) * fp, fp)
                    loads.append(pltpu.make_async_copy(w1_hbm[p].at[j], w1_ref.at[:, units], w_sems.at[0, p, j]))
                    loads.append(pltpu.make_async_copy(w2_hbm[p].at[j], w2_ref.at[units, :], w_sems.at[1, p, j]))
            for cp in loads:
                cp.start()
            for cp in loads:
                cp.wait()

        g1, g2 = g1_ref[...], g2_ref[...]
        xhat1 = xhat1_ref[...]
        r_scr[...] = jnp.maximum(_dot(x1b_ref[...], w1_ref[...]), 0.0)
        r = r_scr[...]
        act = (r * r).astype(BF16)
        act_ref[...] = act
        ff = _dot(act, w2_ref[...])
        y, xhat2, rstd2 = _ln_fwd(ALPHA * (xhat1 * g1 + b1_ref[...]) + ff, g2, b2_ref[...])
        diff = y - tgt_ref[...]
        loss = 0.5 * jnp.sum(jnp.sum(diff * diff, axis=-1, keepdims=True) / d, axis=0, keepdims=True)
        dy = diff / d
        dz2 = _ln_bwd(dy, xhat2, rstd2, g2)
        dz2b = dz2.astype(BF16)
        dz2b_ref[...] = dz2b
        dpre = (_dot(dz2b, w2_ref[...], NT) * (2.0 * r_scr[...])).astype(BF16)
        dpre_ref[...] = dpre
        dx1 = ALPHA * dz2 + _dot(dpre, w1_ref[...], NT)
        dz1_ref[...] = _ln_bwd(dx1, xhat1, rstd1_ref[...], g1)
        stats_ref[0:1, :] += jnp.sum(dx1 * xhat1, axis=0, keepdims=True)
        stats_ref[1:2, :] += jnp.sum(dx1, axis=0, keepdims=True)
        stats_ref[2:3, :] += jnp.sum(dy * xhat2, axis=0, keepdims=True)
        stats_ref[3:4, :] += jnp.sum(dy, axis=0, keepdims=True)
        stats_ref[4:5, :] += jnp.broadcast_to(loss, (1, d))

    tok = lambda w: pl.BlockSpec((tm, w), lambda i: (i, 0))
    vec = pl.BlockSpec((1, d), lambda i: (0, 0))
    return _carry(
        body, name="ffn_fwd_bwd", grid=(t_tok // tm,),
        in_specs=[tok(d), tok(1), tok(d), tok(d)] + [ANY] * (2 * n_part) + [vec, vec, vec, vec],
        out_specs=[tok(f), tok(f), tok(d), tok(d), pl.BlockSpec((8, d), lambda i: (0, 0))],
        out_shape=[jax.ShapeDtypeStruct((t_tok, f), BF16), jax.ShapeDtypeStruct((t_tok, f), BF16),
                   jax.ShapeDtypeStruct((t_tok, d), BF16), jax.ShapeDtypeStruct((t_tok, d), F32), jax.ShapeDtypeStruct((8, d), F32)],
        scratch_shapes=[pltpu.VMEM((tm, f), F32), pltpu.VMEM((d, f), BF16), pltpu.VMEM((f, d), BF16),
                        pltpu.SemaphoreType.DMA((2, n_part, n_chunk))],
        args=(xhat1, rstd1, x1b, target, *w1_parts, *w2_parts, ln1_g, ln1_b, ln2_g, ln2_b))[0]


def _ffn_wgrad(name, lhs, rhs, chunk_lhs, core_chip, comms=()):
    t_tok = lhs.shape[0]
    half = N_DEV // 2
    fc = (lhs if chunk_lhs else rhs).shape[1] // N_DEV
    chunk = (fc, rhs.shape[1]) if chunk_lhs else (lhs.shape[1], fc)

    def shard(s, cc):
        return 2 * (s % half) + jnp.where(s < half, 1 - cc[0], cc[0])

    def body(cc_ref, lhs_ref, rhs_ref, wire_ref, own_ref, recv_ref, send_buf, got, send_sems, recv_sems, got_sem):
        s = pl.program_id(0)
        x, y, c = _place()
        def send(q):
            return pltpu.make_async_remote_copy(
                src_ref=send_buf.at[q % 2], dst_ref=recv_ref.at[q], send_sem=send_sems.at[q], recv_sem=recv_sems.at[q],
                device_id=(x, y, 1 - c), device_id_type=MESH)

        def load(q):
            return pltpu.make_async_copy(recv_ref.at[q], got, got_sem.at[0])

        @pl.when(s >= half)
        def _():
            send(s - half).wait_recv()
            load(s - half).start()

        g = _dot(lhs_ref[...], rhs_ref[...], TN)

        for q in range(half):
            @pl.when(s == q)
            def _(q=q):
                if q >= 2:
                    send(q - 2).wait_send()
                send_buf[q % 2] = g
                send(q).start()

            @pl.when(s == half + q)
            def _(q=q):
                load(q).wait()
                total = g + got[...]
                wire_ref[...] = total.astype(BF16)

                @pl.when(cc_ref[1] == q)
                def _():
                    own_ref[...] = total

        @pl.when(s == N_DEV - 1)
        def _():
            for q in range(half - 2, half):
                send(q).wait_send()

    resident = lambda a: pl.BlockSpec(a.shape, lambda s, cc: (0, 0), pipeline_mode=pl.Buffered(1))
    chunked = pl.BlockSpec((t_tok, fc), lambda s, cc: (0, shard(s, cc)))
    (wire, own, _), per_comm = _carry(
        body, name=name, grid=(N_DEV,), comms=comms, prefetch=(core_chip,),
        in_specs=[chunked, resident(rhs)] if chunk_lhs else [resident(lhs), chunked],
        out_specs=[pl.BlockSpec((None,) + chunk, lambda s, cc: (jnp.maximum(s - half, 0), 0, 0)),
                   pl.BlockSpec(chunk, lambda s, cc: (0, 0)), ANY],
        out_shape=[jax.ShapeDtypeStruct((half,) + chunk, BF16), jax.ShapeDtypeStruct(chunk, F32),
                   jax.ShapeDtypeStruct((half,) + chunk, F32)],
        scratch_shapes=[pltpu.VMEM((2,) + chunk, F32), pltpu.VMEM(chunk, F32), pltpu.SemaphoreType.DMA((half,)),
                        pltpu.SemaphoreType.DMA((half,)), pltpu.SemaphoreType.DMA((1,))],
        args=(lhs, rhs))
    return wire, own, per_comm


def _proj_out_bwd(dz1, cat_t, w_out_b, comms=()):
    t_tok, d = dz1.shape
    d_mix = cat_t.shape[0]
    tm = min(512, t_tok)

    def body(dz1_ref, cat_ref, w_ref, dcat_ref, gw_ref):
        @pl.when(pl.program_id(0) == 0)
        def _():
            gw_ref[...] = jnp.zeros_like(gw_ref)

        dzb = dz1_ref[...].astype(BF16)
        dcat_ref[...] = _dot(w_ref[...], dzb, NT)
        gw_ref[...] += _dot(cat_ref[...], dzb)

    return _carry(
        body, name="proj_out_bwd", grid=(t_tok // tm,), comms=comms,
        in_specs=[pl.BlockSpec((tm, d), lambda i: (i, 0)), pl.BlockSpec((d_mix, tm), lambda i: (0, i)),
                  pl.BlockSpec((d_mix, d), lambda i: (0, 0))],
        out_specs=[pl.BlockSpec((d_mix, tm), lambda i: (0, i)), pl.BlockSpec((d_mix, d), lambda i: (0, 0))],
        out_shape=[jax.ShapeDtypeStruct((d_mix, t_tok), F32), jax.ShapeDtypeStruct((d_mix, d), F32)],
        args=(dz1, cat_t, w_out_b))


def _mixer_bwd(dcat_t, h_t, cos_t, sin_t, w_spatial, b_spatial, vln_g, vln_b, sinks, band_bias, comms=()):
    t_tok = h_t.shape[1]
    nb, n_step = t_tok // BLK, t_tok // MIX_W
    group = N_HEADS // N_KV_HEADS

    def body(sinks_ref, dcat_ref, u_ref, vg_ref, q_ref, kvc_ref, kvp_ref, cos_ref, sin_ref, cosp_ref, sinp_ref,
             wsp_ref, bsp_ref, g_ref, b_ref, bias_ref, dh_ref, dkvc_ref, dkvp_ref, gwsb_ref, gbsp_ref, gvln_ref, gsink_ref,
             dg_acc, db_acc, wm_scr, gws_ref):
        i = pl.program_id(0)

        @pl.when(i == 0)
        def _():
            gws_ref[...] = jnp.zeros_like(gws_ref)
            gbsp_ref[...] = jnp.zeros_like(gbsp_ref)
            gsink_ref[...] = jnp.zeros_like(gsink_ref)
            dg_acc[...] = jnp.zeros_like(dg_acc)
            db_acc[...] = jnp.zeros_like(db_acc)

        _mask_w_once(wsp_ref, wm_scr)

        g = g_ref[...]
        ua, ua_grad = _gelu_and_grad(u_ref[...])
        vv, vv_grad = _gelu_and_grad(vg_ref[...])
        vp, vhat, rstd = _ln_fwd_t(vv, g, b_ref[...])
        vpb = vp.astype(BF16)
        da = dcat_ref[0:D_GMLP, :]
        dmixed = da * ua
        dvp_blocks = []
        for b in range(MIX_BLOCKS):
            dvp_parts = []
            for hh in range(N_HEADS):
                rows = slice(hh * HEAD_DIM, (hh + 1) * HEAD_DIM)
                vpb_h = vpb[rows, _cols(b)]
                mixed = _dot(vpb_h, wm_scr[hh], NT) + bsp_ref[hh:hh + 1, :]
                dh_ref[COL_U + hh * HEAD_DIM:COL_U + (hh + 1) * HEAD_DIM, _cols(b)] = (
                    da[rows, _cols(b)] * mixed * ua_grad[rows, _cols(b)]).astype(BF16)
                dm = dmixed[rows, _cols(b)]
                dmb = dm.astype(BF16)
                gbsp_ref[hh:hh + 1, :] += jnp.sum(dm, axis=0, keepdims=True)
                gws_ref[hh] += _dot(dmb, vpb_h, TN)
                dvp_parts.append(_dot(dmb, wm_scr[hh]))
            dvp_blocks.append(jnp.concatenate(dvp_parts, axis=0))
        dvp = jnp.concatenate(dvp_blocks, axis=1)
        dgv, dbv = dvp * vhat, dvp
        for b in range(MIX_BLOCKS):
            dg_acc[...] += dgv[:, _cols(b)]
            db_acc[...] += dbv[:, _cols(b)]
        dh_ref[COL_V:COL_V + D_GMLP, :] = (_ln_bwd_t(dvp, vhat, rstd, g) * vv_grad).astype(BF16)

        kvc, cos, sin = kvc_ref[...], cos_ref[...], sin_ref[...]
        qr = (_rope_t(q_ref[...], cos, sin) * SCORE_SCALE).astype(BF16)
        sinks4 = [_group_lanes([jnp.full((1, BLK), sinks_ref[hh], F32) for hh in range(kv * group, (kv + 1) * group)])
                  for kv in range(N_KV_HEADS)]
        dq_blocks, dkv_cur, dkv_prev = [], [], []
        for b in range(MIX_BLOCKS):
            kv_cur, kv_prev, cosc, sinc, cosp, sinp, bias1 = _block_inputs(b, i, kvc, kvp_ref, cos, sin, cosp_ref, sinp_ref, bias_ref)
            k_t, k_n, v_t = _keys_values(kv_cur, kv_prev, cosc, sinc, cosp, sinp)
            v_n = jnp.concatenate([kv_prev[D_KV:].T, kv_cur[D_KV:].T], axis=0).astype(BF16)
            bias = _group_lanes([bias1] * group)
            dk, dv, dq_parts = [], [], []
            for kv in range(N_KV_HEADS):
                heads = range(kv * group, (kv + 1) * group)
                kv_rows = slice(kv * HEAD_DIM, (kv + 1) * HEAD_DIM)
                qs = _group_lanes([qr[hh * HEAD_DIM:(hh + 1) * HEAD_DIM, _cols(b)] for hh in heads])
                dos = _group_lanes([dcat_ref[D_GMLP + hh * HEAD_DIM:D_GMLP + (hh + 1) * HEAD_DIM, _cols(b)]
                                    for hh in heads]).astype(BF16)
                p, p_sink = _softmax_sink_t(_dot(k_n, _pad_head(qs, kv)) + bias, sinks4[kv])
                dp = _dot(v_n, _pad_head(dos, kv))
                delta = jnp.sum(p * dp, axis=0, keepdims=True)
                ds = (p * (dp - delta)).astype(BF16)
                dsink = p_sink * delta
                dq = _dot(k_t[kv_rows], ds) * SCORE_SCALE
                for j, hh in enumerate(heads):
                    gsink_ref[hh:hh + 1, :] -= dsink[:, j * BLK:(j + 1) * BLK]
                    dq_parts.append(dq[:, j * BLK:(j + 1) * BLK])
                dk.append(_dot(qs, ds, NT))
                dv.append(_dot(dos, p.astype(BF16), NT))
            dq_blocks.append(jnp.concatenate(dq_parts, axis=0))
            dk_all, dv_all = jnp.concatenate(dk, axis=0), jnp.concatenate(dv, axis=0)
            dkv_cur.append(jnp.concatenate([_rope_t(dk_all[:, BLK:], cosc, sinc, bwd=True), dv_all[:, BLK:]], axis=0))
            dkv_prev.append(jnp.concatenate([_rope_t(dk_all[:, :BLK], cosp, sinp, bwd=True), dv_all[:, :BLK]], axis=0))
        dh_ref[COL_Q:COL_Q + D_ATTN, :] = _rope_t(jnp.concatenate(dq_blocks, axis=1), cos, sin, bwd=True).astype(BF16)
        for b in range(MIX_BLOCKS):
            dkvc_ref[:, _cols(b)] = dkv_cur[b] + dkv_prev[b + 1] if b + 1 < MIX_BLOCKS else dkv_cur[b]
        dkvp_ref[...] = dkv_prev[0]

        @pl.when(i == n_step - 1)
        def _():
            causal = _causal()
            for hh in range(N_HEADS):
                gwsb_ref[hh] = jnp.where(causal, gws_ref[hh], 0.0).astype(BF16)
            gvln_ref[...] = jnp.zeros_like(gvln_ref)
            gvln_ref[0:1, :] = jnp.sum(dg_acc[...].T, axis=0, keepdims=True)
            gvln_ref[1:2, :] = jnp.sum(db_acc[...].T, axis=0, keepdims=True)

    full = lambda shape: pl.BlockSpec(shape, lambda i: (0,) * len(shape))
    return _carry(
        body, name="mixer_bwd", grid=(n_step,), comms=comms,
        in_specs=[pl.BlockSpec(memory_space=pltpu.SMEM), pl.BlockSpec((D_GMLP + D_ATTN, MIX_W), lambda i: (0, i))]
        + _h_specs() + _table_specs()
        + [full((N_HEADS, BLK, BLK)), full((N_HEADS, BLK)), full((D_GMLP, 1)), full((D_GMLP, 1)), BIAS_SPEC],
        out_specs=[pl.BlockSpec((COL_K, MIX_W), lambda i: (0, i)), pl.BlockSpec((2 * D_KV, MIX_W), lambda i: (0, i)),
                   pl.BlockSpec((2 * D_KV, BLK), lambda i: (0, (i + n_step - 1) % n_step)),
                   full((N_HEADS, BLK, BLK)), full((N_HEADS, BLK)), full((8, D_GMLP)), full((N_HEADS, LANES))],
        out_shape=[jax.ShapeDtypeStruct((COL_K, t_tok), BF16), jax.ShapeDtypeStruct((2 * D_KV, t_tok), F32),
                   jax.ShapeDtypeStruct((2 * D_KV, n_step * BLK), F32),
                   jax.ShapeDtypeStruct((N_HEADS, BLK, BLK), BF16), jax.ShapeDtypeStruct((N_HEADS, BLK), F32),
                   jax.ShapeDtypeStruct((8, D_GMLP), F32), jax.ShapeDtypeStruct((N_HEADS, LANES), F32)],
        scratch_shapes=[pltpu.VMEM((D_GMLP, BLK), F32), pltpu.VMEM((D_GMLP, BLK), F32), pltpu.VMEM((N_HEADS, BLK, BLK), BF16),
                        pltpu.VMEM((N_HEADS, BLK, BLK), F32)],
        args=(sinks, dcat_t, h_t, h_t, h_t, h_t, h_t, cos_t, sin_t, cos_t, sin_t, w_spatial, b_spatial, vln_g, vln_b, band_bias))


def _proj_in_wgrad(dh_b, dkvc_t, dkvp_t, xb, comms=()):
    t_tok, d = xb.shape
    d_main, d_kv = dh_b.shape[0], dkvc_t.shape[0]
    tm = min(1024, t_tok)

    def body(dh_ref, dkvc_ref, dkvp_ref, xb_ref, dkvb_ref, gw_ref):
        @pl.when(pl.program_id(0) == 0)
        def _():
            gw_ref[...] = jnp.zeros_like(gw_ref)

        for s in range(tm // MIX_W):
            last = slice((s + 1) * MIX_W - BLK, (s + 1) * MIX_W)
            dkvb_ref[:, s * MIX_W:(s + 1) * MIX_W - BLK] = dkvc_ref[:, s * MIX_W:(s + 1) * MIX_W - BLK].astype(BF16)
            dkvb_ref[:, last] = (dkvc_ref[:, last] + dkvp_ref[:, _cols(s)]).astype(BF16)
        gw_ref[0:d_main, :] += _dot(dh_ref[...], xb_ref[...])
        gw_ref[d_main:, :] += _dot(dkvb_ref[...], xb_ref[...])

    tok = lambda rows: pl.BlockSpec((rows, tm), lambda i: (0, i))
    return _carry(
        body, name="proj_in_wgrad", grid=(t_tok // tm,), comms=comms,
        in_specs=[tok(d_main), tok(d_kv), pl.BlockSpec((d_kv, tm // MIX_BLOCKS), lambda i: (0, i)),
                  pl.BlockSpec((tm, d), lambda i: (i, 0))],
        out_specs=[tok(d_kv), pl.BlockSpec((d_main + d_kv, d), lambda i: (0, 0))],
        out_shape=[jax.ShapeDtypeStruct((d_kv, t_tok), BF16), jax.ShapeDtypeStruct((d_main + d_kv, d), F32)],
        args=(dh_b, dkvc_t, dkvp_t, xb))


def _proj_in_dgrad(dh_b, dkv_b, dz1, w_in_t, comms=()):
    t_tok, d = dz1.shape
    d_main, d_kv = dh_b.shape[0], dkv_b.shape[0]
    tm = min(512, t_tok)

    def body(dh_ref, dkv_ref, dz1_ref, w_ref, dx_ref):
        dx_ref[...] = (ALPHA * dz1_ref[...] + _dot(dh_ref[...], w_ref[0:d_main, :], TN)
                       + _dot(dkv_ref[...], w_ref[d_main:, :], TN))

    return _carry(
        body, name="proj_in_dgrad", grid=(t_tok // tm,), comms=comms,
        in_specs=[pl.BlockSpec((d_main, tm), lambda i: (0, i)), pl.BlockSpec((d_kv, tm), lambda i: (0, i)),
                  pl.BlockSpec((tm, d), lambda i: (i, 0)), pl.BlockSpec((d_main + d_kv, d), lambda i: (0, 0))],
        out_specs=[pl.BlockSpec((tm, d), lambda i: (i, 0))],
        out_shape=[jax.ShapeDtypeStruct((t_tok, d), F32)],
        args=(dh_b, dkv_b, dz1, w_in_t))


def _adamw(w, g, m, v):
    m = ADAM_B1 * m + (1.0 - ADAM_B1) * g
    v = ADAM_B2 * v + (1.0 - ADAM_B2) * (g * g)
    m_hat = m / (1.0 - ADAM_B1 ** ADAM_STEP)
    v_hat = v / (1.0 - ADAM_B2 ** ADAM_STEP)
    delta = -ADAM_LR * (m_hat / (jnp.sqrt(v_hat) + ADAM_EPS) + ADAM_WD * w)
    return delta, m, v


ADAMW_STEPS = 2


def _adamw_shards(name, items, comms=()):
    def body(*refs):
        ins, outs = refs[:5 * len(items)], refs[5 * len(items):]
        for i in range(len(items)):
            own_ref, recv_ref, w_ref, m_ref, v_ref = ins[5 * i:5 * i + 5]
            g = ((own_ref[...] + recv_ref[0].astype(F32)) + recv_ref[1].astype(F32)) + recv_ref[2].astype(F32)
            for o_ref, val in zip(outs[4 * i:4 * i + 4], (g,) + _adamw(w_ref[...], g, m_ref[...], v_ref[...])):
                o_ref[...] = val

    in_specs, out_specs, out_shape, args = [], [], [], []
    for own, recv, w, m, v in items:
        r, c = own.shape
        blk = pl.BlockSpec((r // ADAMW_STEPS, c), lambda s: (s, 0))
        in_specs += [blk, pl.BlockSpec((3, r // ADAMW_STEPS, c), lambda s: (0, s, 0)), blk, blk, blk]
        out_specs += [blk] * 4
        out_shape += [jax.ShapeDtypeStruct((r, c), F32)] * 4
        args += [own, recv, w, m, v]
    res, per_comm = _carry(body, name=name, grid=(ADAMW_STEPS,), comms=comms, in_specs=in_specs, out_specs=out_specs,
                           out_shape=out_shape, args=args)
    return [res[4 * i:4 * i + 4] for i in range(len(items))], per_comm


VEC_VLN, VEC_LN1G, VEC_LN1B, VEC_LN2G, VEC_LN2B, VEC_SINK, VEC_LOSS, VEC_BSP, VEC_ROWS = 0, 1, 2, 3, 4, 5, 6, 8, 16


def _adamw_small(parts_w, parts_vec, params):
    n = parts_w.shape[0]
    flat = [a for p in params for a in p]
    shapes = [p[0].shape for p in params]

    def grads(gw, gv):
        return [gw, gv[VEC_VLN:VEC_VLN + 1, 0:D_GMLP], gv[VEC_VLN:VEC_VLN + 1, D_GMLP:2 * D_GMLP],
                gv[VEC_BSP:VEC_BSP + N_HEADS, 0:BLK], gv[VEC_LN1G:VEC_LN1G + 1], gv[VEC_LN1B:VEC_LN1B + 1],
                gv[VEC_LN2G:VEC_LN2G + 1], gv[VEC_LN2B:VEC_LN2B + 1], gv[VEC_SINK:VEC_SINK + 1, 0:N_HEADS]]

    def body(pw_ref, pv_ref, *refs):
        ins, outs = refs[:len(flat)], refs[len(flat):]
        gw, gv = pw_ref[0].astype(F32), pv_ref[0]
        for k in range(1, n):
            gw, gv = gw + pw_ref[k].astype(F32), gv + pv_ref[k]
        for i, g in enumerate(grads(gw, gv)):
            w_ref, m_ref, v_ref = ins[3 * i:3 * i + 3]
            delta, m_new, v_new = _adamw(w_ref[...], g, m_ref[...], v_ref[...])
            for o_ref, val in zip(outs[4 * i:4 * i + 4], (g, delta, m_new, v_new)):
                o_ref[...] = val
        outs[-1][...] = gv[VEC_LOSS:VEC_LOSS + 1, 0:LANES]

    whole = lambda shape: pl.BlockSpec(shape, lambda i: (0,) * len(shape))
    res = _carry(
        body, name="adamw_small", grid=(1,),
        in_specs=[whole(parts_w.shape), whole(parts_vec.shape)] + [whole(a.shape) for a in flat],
        out_specs=[whole(s) for s in shapes for _ in range(4)] + [whole((1, LANES))],
        out_shape=[jax.ShapeDtypeStruct(s, F32) for s in shapes for _ in range(4)] + [jax.ShapeDtypeStruct((1, LANES), F32)],
        args=(parts_w, parts_vec, *flat))[0]
    return [res[4 * i:4 * i + 4] for i in range(len(params))], res[-1]


def _pair_sum(name, parts, recv, core_chip, comms=()):
    _, r, c = parts.shape
    tr = r if r <= 512 else 512

    def body(cc_ref, a_ref, b_ref, wire_ref, own_ref):
        s = a_ref[...] + b_ref[...]
        wire_ref[...] = s.astype(BF16)

        @pl.when(pl.program_id(1) == cc_ref[1])
        def _():
            own_ref[...] = s

    return _carry(
        body, name=name, grid=(r // tr, 4), prefetch=(core_chip,), comms=comms,
        in_specs=[pl.BlockSpec((None, tr, c), lambda i, q, cc: (2 * q + cc[0], i, 0)),
                  pl.BlockSpec((None, tr, c), lambda i, q, cc: (q, i, 0))],
        out_specs=[pl.BlockSpec((None, tr, c), lambda i, q, cc: (q, i, 0)), pl.BlockSpec((tr, c), lambda i, q, cc: (i, 0))],
        out_shape=[jax.ShapeDtypeStruct((4, r, c), BF16), jax.ShapeDtypeStruct((r, c), F32)],
        args=(parts, recv))


def kernel(x, positions, w_in, v_ln_g, v_ln_b, w_spatial, b_spatial, sinks, w_out, ln1_g, ln1_b, w_ff1, w_ff2, ln2_g, ln2_b, loss_target, m_w_in, m_v_ln_g, m_v_ln_b, m_w_spatial, m_b_spatial, m_sinks, m_w_out, m_ln1_g, m_ln1_b, m_w_ff1, m_w_ff2, m_ln2_g, m_ln2_b, v_w_in, v_v_ln_g, v_v_ln_b, v_w_spatial, v_b_spatial, v_sinks, v_w_out, v_ln1_g, v_ln1_b, v_w_ff1, v_w_ff2, v_ln2_g, v_ln2_b):
    _, t_tok, d = x.shape
    xi, yi, ci = _place()
    core_chip = jnp.stack([ci, 2 * xi + yi]).astype(jnp.int32)
    x2 = x.reshape(t_tok, d)
    target = loss_target.reshape(t_tok, d)
    inv_freq = ROPE_THETA ** (-jnp.arange(0, HEAD_DIM, 2, dtype=F32) / HEAD_DIM)
    wsp, bsp, sink_vec = w_spatial[0], b_spatial[0], sinks[0]
    vg_col, vb_col = v_ln_g.reshape(D_GMLP, 1), v_ln_b.reshape(D_GMLP, 1)
    big = {"in": w_in[0], "out": w_out[0], "ff1": w_ff1[0], "ff2": w_ff2[0]}
    half1, half2 = big["ff1"].shape[1] // 2, big["ff2"].shape[0] // 2
    w1_mine = [big["ff1"][:, :half1].astype(BF16), big["ff1"][:, half1:].astype(BF16)]
    w2_mine = [big["ff2"][:half2].astype(BF16), big["ff2"][half2:].astype(BF16)]

    (cos_t, sin_t), ((g_in,),) = _rope_tables(
        positions, jnp.tile(inv_freq, 2).reshape(HEAD_DIM, 1), comms=[_gather_comm([big["in"].T.astype(BF16)])])
    w_in_t = g_in.reshape(D_IN, d)
    (h_t, xb), ((g_out, w1_a),) = _proj_in(x2, w_in_t, comms=[_gather_comm([big["out"].astype(BF16), w1_mine[0]])])
    w_out_b = g_out.reshape(-1, d)
    band_bias = _band_bias()
    (cat_t,), ((w1_b, w2_a),) = _mixer_fwd(h_t, cos_t, sin_t, wsp, bsp, vg_col, vb_col, sink_vec, band_bias,
                                           comms=[_gather_comm([w1_mine[1], w2_mine[0]])])
    (xhat1, rstd1, x1b), ((w2_b,),) = _proj_out(cat_t, x2, w_out_b, ln1_g, ln1_b, comms=[_gather_comm([w2_mine[1]])])
    act_b, dpre_b, dz2b, dz1, stats = _ffn_fwd_bwd(xhat1, rstd1, x1b, target, [w1_a, w1_b], [w2_a, w2_b], ln1_g, ln1_b, ln2_g, ln2_b)

    (dcat_t, gw_out), _ = _proj_out_bwd(dz1, cat_t, w_out_b)
    p_out = gw_out.reshape(N_DEV, -1, d)
    wire_ff1, own_ff1, ((s_out,),) = _ffn_wgrad("ffn_wgrad1", x1b, dpre_b, False, core_chip, comms=[_sibling_comm([p_out])])
    (wire_out, own_out), _ = _pair_sum("pair_sum_out", p_out, s_out, core_chip)
    wire_ff2, own_ff2, ((r_ff1,),) = _ffn_wgrad("ffn_wgrad2", act_b, dz2b, True, core_chip, comms=[_chips_comm([wire_ff1])])
    (dh_b, dkvc_t, dkvp_t, g_wsp, g_bsp, g_vln, g_sink), ((r_ff2, r_out),) = _mixer_bwd(
        dcat_t, h_t, cos_t, sin_t, wsp, bsp, vg_col, vb_col, sink_vec, band_bias,
        comms=[_chips_comm([wire_ff2, wire_out])])
    sink_row = jnp.pad(g_sink.sum(axis=1).reshape(1, N_HEADS), ((0, 0), (0, d - N_HEADS)))
    small_vec = jnp.concatenate([g_vln[0:2].reshape(1, d), stats[0:4], sink_row, stats[4:5], jnp.zeros((1, d), F32),
                                 jnp.pad(g_bsp, ((0, 0), (0, d - BLK)))], axis=0)
    (dkv_b, gw_in_t), ((parts_w, parts_vec),) = _proj_in_wgrad(
        dh_b, dkvc_t, dkvp_t, xb, comms=[_gather_comm([g_wsp.reshape(-1, BLK), small_vec])])
    p_in = gw_in_t.reshape(N_DEV, -1, d)

    (out_out,), ((s_in,),) = _adamw_shards("adamw_out", [(own_out, r_out, big["out"], m_w_out[0], v_w_out[0])],
                                           comms=[_sibling_comm([p_in])])
    (wire_in, own_in), _ = _pair_sum("pair_sum_in", p_in, s_in, core_chip)
    (grad_x,), ((r_in,),) = _proj_in_dgrad(dh_b, dkv_b, dz1, w_in_t, comms=[_chips_comm([wire_in])])
    (ff1_out, ff2_out, in_out_t), _ = _adamw_shards("adamw_big", [
        (own_ff1, r_ff1, big["ff1"], m_w_ff1[0], v_w_ff1[0]), (own_ff2, r_ff2, big["ff2"], m_w_ff2[0], v_w_ff2[0]),
        (own_in, r_in, big["in"].T, m_w_in[0].T, v_w_in[0].T)])
    in_out = [o.T for o in in_out_t]
    small = [(w_spatial, m_w_spatial, v_w_spatial), (v_ln_g, m_v_ln_g, v_v_ln_g), (v_ln_b, m_v_ln_b, v_v_ln_b),
             (b_spatial, m_b_spatial, v_b_spatial), (ln1_g, m_ln1_g, v_ln1_g), (ln1_b, m_ln1_b, v_ln1_b),
             (ln2_g, m_ln2_g, v_ln2_g), (ln2_b, m_ln2_b, v_ln2_b), (sinks, m_sinks, v_sinks)]
    views = [(-1, BLK), None, None, (N_HEADS, BLK)] + [None] * 5
    small_res, loss_row = _adamw_small(parts_w, parts_vec, [
        tuple(a if vw is None else a.reshape(vw) for a in p) for p, vw in zip(small, views)])
    small_out = [[o.reshape(p[0].shape) for o in res] for res, p in zip(small_res, small)]
    loss = loss_row[0, 0]

    big_out = {0: in_out, 6: out_out, 9: ff1_out, 10: ff2_out}
    small_slot = {3: 0, 1: 1, 2: 2, 4: 3, 7: 4, 8: 5, 11: 6, 12: 7, 5: 8}
    outs = [loss, grad_x.reshape(x.shape)]
    for kind in range(4):
        for wi in range(13):
            outs.append(big_out[wi][kind][None] if wi in big_out else small_out[small_slot[wi]][kind])
    return tuple(outs)
```

```python
import math

import jax
import jax.numpy as jnp
from jax import lax
from jax.experimental import pallas as pl
from jax.experimental.pallas import tpu as pltpu

F32 = jnp.float32
BF16 = jnp.bfloat16
MESH = pl.DeviceIdType.MESH

HEAD_DIM = 64
N_HEADS = 8
N_KV_HEADS = 2
BLK = 128
D_GMLP = N_HEADS * HEAD_DIM
D_ATTN = N_HEADS * HEAD_DIM
D_KV = N_KV_HEADS * HEAD_DIM
D_IN = 2 * D_GMLP + D_ATTN + 2 * D_KV
COL_U, COL_V, COL_Q, COL_K = 0, D_GMLP, 2 * D_GMLP, 2 * D_GMLP + D_ATTN
ROPE_THETA = 10000.0
LN_EPS = 1e-5
ALPHA = 2.0 ** 0.25
NEG_INF = -1e30
SCORE_SCALE = 1.0 / math.sqrt(HEAD_DIM)
ADAM_LR, ADAM_B1, ADAM_B2, ADAM_EPS, ADAM_WD, ADAM_STEP = 0.001, 0.9, 0.999, 1e-08, 0.01, 10
N_DEV = 8
LANES = 128
VMEM_LIMIT = 56 * 1024 * 1024
FFN_ROWS = 256

NT = (((1,), (1,)), ((), ()))
TN = (((0,), (0,)), ((), ()))


def _params(*sem):
    return pltpu.CompilerParams(dimension_semantics=sem, vmem_limit_bytes=VMEM_LIMIT)


def _dot(a, b, dims=None):
    if dims is None:
        return jnp.dot(a, b, preferred_element_type=F32)
    return lax.dot_general(a, b, dims, preferred_element_type=F32)


def _mean(a):
    return jnp.mean(a, axis=-1, keepdims=True)


def _ln_fwd(z, g, b):
    zc = z - _mean(z)
    rstd = lax.rsqrt(_mean(zc * zc) + LN_EPS)
    xhat = zc * rstd
    return xhat * g + b, xhat, rstd


def _ln_bwd(dy, xhat, rstd, g):
    dxhat = dy * g
    return rstd * (dxhat - _mean(dxhat) - xhat * _mean(dxhat * xhat))


_GELU_C = math.sqrt(2.0 / math.pi)


def _gelu(x):
    t = jnp.tanh(_GELU_C * (x + 0.044715 * (x * x * x)))
    return 0.5 * x * (1.0 + t)


def _gelu_and_grad(x):
    x2 = x * x
    t = jnp.tanh(_GELU_C * (x + 0.044715 * (x2 * x)))
    hx, ht = 0.5 * x, 0.5 * (1.0 + t)
    return x * ht, ht + hx * (1.0 - t * t) * (_GELU_C * (1.0 + 3.0 * 0.044715 * x2))


def _mean0(a):
    return jnp.mean(a, axis=0, keepdims=True)


def _ln_fwd_t(z, g, b):
    zc = z - _mean0(z)
    rstd = lax.rsqrt(_mean0(zc * zc) + LN_EPS)
    xhat = zc * rstd
    return xhat * g + b, xhat, rstd


def _ln_bwd_t(dy, xhat, rstd, g):
    dxhat = dy * g
    return rstd * (dxhat - _mean0(dxhat) - xhat * _mean0(dxhat * xhat))


def _rope_t(t, cos, sin_signed, bwd=False):
    half = HEAD_DIM // 2
    outs = []
    for r in range(0, t.shape[0], HEAD_DIM):
        th = t[r:r + HEAD_DIM]
        sw = jnp.concatenate([th[half:], th[:half]], axis=0) * sin_signed
        outs.append(th * cos - sw if bwd else th * cos + sw)
    return jnp.concatenate(outs, axis=0)


ANY = pl.BlockSpec(memory_space=pl.ANY)


def _place():
    return lax.axis_index("x"), lax.axis_index("y"), lax.axis_index("c")


class _Comm:
    def __init__(self, ins, outs, sems, start, finish):
        self.ins, self.outs, self.sems, self.start, self.finish = ins, outs, sems, start, finish


def _gather_comm(arrs):
    n = len(arrs)

    def parts(ins, outs, sems):
        send_sems, recv_sems, local_sems = sems
        x, y, c = _place()
        me, sibling = (x, y, c), (x, y, 1 - c)
        chips = [(1 - x, y), (x, 1 - y), (1 - x, 1 - y)]

        def copy(a, k, block, to, src=None):
            px, py, pc = block
            dst = outs[a].at[4 * px + 2 * py + pc]
            return pltpu.make_async_remote_copy(
                src_ref=dst if src is None else src, dst_ref=dst,
                send_sem=send_sems.at[a, k], recv_sem=recv_sems.at[a, k], device_id=to, device_id_type=MESH)

        mine = [pltpu.make_async_copy(ins[a], outs[a].at[4 * x + 2 * y + c], local_sems.at[a]) for a in range(n)]
        first = []
        for a in range(n):
            first.append(copy(a, 0, me, sibling, src=ins[a]))
            first += [copy(a, 1 + j, me, (*chip, c), src=ins[a]) for j, chip in enumerate(chips)]
        return copy, mine, first, me, sibling, chips, c

    def start(ins, outs, sems):
        _, mine, first, *_ = parts(ins, outs, sems)
        for cp in mine + first:
            cp.start()

    def finish(ins, outs, sems):
        copy, mine, first, me, sibling, chips, c = parts(ins, outs, sems)
        passed = []
        for j, chip in enumerate(chips):
            for a in range(n):
                copy(a, 1 + j, (*chip, c), me).wait_recv()
                fwd = copy(a, 4 + j, (*chip, c), sibling)
                fwd.start()
                passed.append(fwd)
        for a in range(n):
            copy(a, 0, sibling, me).wait_recv()
        for j, chip in enumerate(chips):
            for a in range(n):
                copy(a, 4 + j, (*chip, 1 - c), me).wait_recv()
        for cp in first + passed:
            cp.wait_send()
        for cp in mine:
            cp.wait()

    return _Comm(list(arrs), [jax.ShapeDtypeStruct((N_DEV,) + a.shape, a.dtype) for a in arrs],
                 [pltpu.SemaphoreType.DMA((n, 7)), pltpu.SemaphoreType.DMA((n, 7)), pltpu.SemaphoreType.DMA((n,))],
                 start, finish)


def _sibling_comm(parts):
    n = len(parts)

    def copies(ins, outs, sems):
        x, y, c = _place()
        return [pltpu.make_async_remote_copy(
            src_ref=ins[a].at[2 * q + (1 - c)], dst_ref=outs[a].at[q],
            send_sem=sems[0].at[a, q], recv_sem=sems[1].at[a, q],
            device_id=(x, y, 1 - c), device_id_type=MESH) for a in range(n) for q in range(4)]

    return _Comm(list(parts), [jax.ShapeDtypeStruct((4,) + p.shape[1:], p.dtype) for p in parts],
                 [pltpu.SemaphoreType.DMA((n, 4)), pltpu.SemaphoreType.DMA((n, 4))],
                 lambda *r: [cp.start() for cp in copies(*r)], lambda *r: [cp.wait() for cp in copies(*r)])


def _chips_comm(chip_parts, rows=None):
    n = len(chip_parts)
    r0, nr = (0, None) if rows is None else rows

    def copies(ins, outs, sems):
        x, y, c = _place()
        chips = [(1 - x, y), (x, 1 - y), (1 - x, 1 - y)]
        src = lambda a, q: ins[a].at[q] if rows is None else ins[a].at[q, pl.ds(r0, nr)]
        return [pltpu.make_async_remote_copy(
            src_ref=src(a, 2 * px + py), dst_ref=outs[a].at[k],
            send_sem=sems[0].at[a, k], recv_sem=sems[1].at[a, k],
            device_id=(px, py, c), device_id_type=MESH) for a in range(n) for k, (px, py) in enumerate(chips)]

    shape = lambda p: (3,) + p.shape[1:] if rows is None else (3, nr) + p.shape[2:]
    return _Comm(list(chip_parts), [jax.ShapeDtypeStruct(shape(p), p.dtype) for p in chip_parts],
                 [pltpu.SemaphoreType.DMA((n, 3)), pltpu.SemaphoreType.DMA((n, 3))],
                 lambda *r: [cp.start() for cp in copies(*r)], lambda *r: [cp.wait() for cp in copies(*r)])


def _carry(body, *, name, grid, in_specs, out_specs, out_shape, args, comms=(), scratch_shapes=(), prefetch=()):
    n_pre, n_in, n_out, n_scr = len(prefetch), len(in_specs), len(out_specs), len(scratch_shapes)
    c_ins = [a for cm in comms for a in cm.ins]
    c_outs = [s for cm in comms for s in cm.outs]
    c_sems = [s for cm in comms for s in cm.sems]

    def wrapped(*refs):
        pre, refs = refs[:n_pre], refs[n_pre:]
        ins, refs = refs[:n_in], refs[n_in:]
        cins, refs = refs[:len(c_ins)], refs[len(c_ins):]
        outs, refs = refs[:n_out], refs[n_out:]
        couts, refs = refs[:len(c_outs)], refs[len(c_outs):]
        scr, sems = refs[:n_scr], refs[n_scr:]
        groups, i0, o0, s0 = [], 0, 0, 0
        for cm in comms:
            groups.append((cm, cins[i0:i0 + len(cm.ins)], couts[o0:o0 + len(cm.outs)], sems[s0:s0 + len(cm.sems)]))
            i0, o0, s0 = i0 + len(cm.ins), o0 + len(cm.outs), s0 + len(cm.sems)
        first = pl.program_id(0) == 0
        last = pl.program_id(0) == grid[0] - 1
        for ax in range(1, len(grid)):
            first = first & (pl.program_id(ax) == 0)
            last = last & (pl.program_id(ax) == grid[ax] - 1)
        if comms:
            @pl.when(first)
            def _():
                for cm, ci, co, cs in groups:
                    cm.start(ci, co, cs)
        body(*pre, *ins, *outs, *scr)
        if comms:
            @pl.when(last)
            def _():
                for cm, ci, co, cs in groups:
                    cm.finish(ci, co, cs)

    grid_spec = pltpu.PrefetchScalarGridSpec(
        num_scalar_prefetch=n_pre, grid=grid,
        in_specs=list(in_specs) + [ANY] * len(c_ins), out_specs=list(out_specs) + [ANY] * len(c_outs),
        scratch_shapes=list(scratch_shapes) + c_sems)
    res = pl.pallas_call(
        wrapped, name=name, grid_spec=grid_spec, out_shape=list(out_shape) + c_outs,
        compiler_params=_params(*(["arbitrary"] * len(grid))),
    )(*prefetch, *args, *c_ins)
    outs, rest, per_comm = res[:n_out], res[n_out:], []
    for cm in comms:
        per_comm.append(rest[:len(cm.outs)])
        rest = rest[len(cm.outs):]
    return outs, per_comm


def _rope_tables(pos_row, inv_freq_col, comms=()):
    t_tok = pos_row.shape[1]
    tm = min(512, t_tok)

    def body(pos_ref, invf_ref, cos_ref, sin_ref):
        ang = pos_ref[...].astype(F32) * invf_ref[...]
        row = lax.broadcasted_iota(jnp.int32, ang.shape, 0)
        cos_ref[...] = jnp.cos(ang)
        sin_ref[...] = jnp.sin(ang) * jnp.where(row < HEAD_DIM // 2, -1.0, 1.0)

    return _carry(
        body, name="rope_tables", grid=(t_tok // tm,), comms=comms,
        in_specs=[pl.BlockSpec((1, tm), lambda i: (0, i)), pl.BlockSpec((HEAD_DIM, 1), lambda i: (0, 0))],
        out_specs=[pl.BlockSpec((HEAD_DIM, tm), lambda i: (0, i))] * 2,
        out_shape=[jax.ShapeDtypeStruct((HEAD_DIM, t_tok), F32)] * 2,
        args=(pos_row, inv_freq_col))


def _proj_in(x2, w_in_t, comms=()):
    t_tok, d = x2.shape
    d_in = w_in_t.shape[0]
    tm = min(512, t_tok)

    def body(x_ref, w_ref, h_ref, xb_ref):
        xb = x_ref[...].astype(BF16)
        xb_ref[...] = xb
        h_ref[...] = _dot(w_ref[...], xb, NT)

    return _carry(
        body, name="proj_in", grid=(t_tok // tm,), comms=comms,
        in_specs=[pl.BlockSpec((tm, d), lambda i: (i, 0)), pl.BlockSpec((d_in, d), lambda i: (0, 0))],
        out_specs=[pl.BlockSpec((d_in, tm), lambda i: (0, i)), pl.BlockSpec((tm, d), lambda i: (i, 0))],
        out_shape=[jax.ShapeDtypeStruct((d_in, t_tok), F32), jax.ShapeDtypeStruct((t_tok, d), BF16)],
        args=(x2, w_in_t))


MIX_BLOCKS = 2
MIX_W = MIX_BLOCKS * BLK


def _prev_block(i):
    return jnp.maximum(MIX_BLOCKS * i - 1, 0)


def _h_specs():
    kv_row = COL_K // (2 * D_KV)
    return [
        pl.BlockSpec((D_GMLP, MIX_W), lambda i: (0, i)),
        pl.BlockSpec((D_GMLP, MIX_W), lambda i: (1, i)),
        pl.BlockSpec((D_ATTN, MIX_W), lambda i: (2, i)),
        pl.BlockSpec((2 * D_KV, MIX_W), lambda i: (kv_row, i)),
        pl.BlockSpec((2 * D_KV, BLK), lambda i: (kv_row, _prev_block(i))),
    ]


def _table_specs():
    return [
        pl.BlockSpec((HEAD_DIM, MIX_W), lambda i: (0, i)),
        pl.BlockSpec((HEAD_DIM, MIX_W), lambda i: (0, i)),
        pl.BlockSpec((HEAD_DIM, BLK), lambda i: (0, _prev_block(i))),
        pl.BlockSpec((HEAD_DIM, BLK), lambda i: (0, _prev_block(i))),
    ]


def _cols(b):
    return slice(b * BLK, (b + 1) * BLK)


LSE_ROWS = 8
LSE_SPEC = pl.BlockSpec((LSE_ROWS, D_ATTN), lambda i: (i, 0))


def _block_inputs(b, i, kvc, kvp_ref, cos, sin, cosp_ref, sinp_ref, bias_ref):
    if b == 0:
        kv_prev, cos_prev, sin_prev, bias = kvp_ref[...], cosp_ref[...], sinp_ref[...], bias_ref[jnp.minimum(i, 1)]
    else:
        kv_prev, cos_prev, sin_prev, bias = kvc[:, _cols(b - 1)], cos[:, _cols(b - 1)], sin[:, _cols(b - 1)], bias_ref[1]
    return kvc[:, _cols(b)], kv_prev, cos[:, _cols(b)], sin[:, _cols(b)], cos_prev, sin_prev, bias


def _band_bias():
    ki = lax.broadcasted_iota(jnp.int32, (2, 2 * BLK, BLK), 1)
    qi = lax.broadcasted_iota(jnp.int32, (2, 2 * BLK, BLK), 2)
    later = lax.broadcasted_iota(jnp.int32, (2, 2 * BLK, BLK), 0) > 0
    dist = qi + BLK - ki
    return jnp.where((dist >= 0) & (dist < BLK) & ((ki >= BLK) | later), 0.0, NEG_INF).astype(F32)


BIAS_SPEC = pl.BlockSpec((2, 2 * BLK, BLK), lambda i: (0, 0, 0))


def _keys_values(kvc, kvp, cosc, sinc, cosp, sinp):
    kp, kc = _rope_t(kvp[:D_KV], cosp, sinp), _rope_t(kvc[:D_KV], cosc, sinc)
    k_t = jnp.concatenate([kp, kc], axis=1).astype(BF16)
    k_n = jnp.concatenate([kp.T, kc.T], axis=0).astype(BF16)
    v_t = jnp.concatenate([kvp[D_KV:], kvc[D_KV:]], axis=1).astype(BF16)
    return k_t, k_n, v_t


def _pad_head(th, kv):
    z = jnp.zeros_like(th)
    return jnp.concatenate([th, z] if kv == 0 else [z, th], axis=0)


def _group_lanes(parts):
    return jnp.concatenate(parts, axis=1)


def _softmax_sink_t(s, sink):
    m = jnp.maximum(jnp.max(s, axis=0, keepdims=True), sink)
    e = jnp.exp(s - m)
    denom = jnp.sum(e, axis=0, keepdims=True) + jnp.exp(sink - m)
    return e * (1.0 / denom), m + jnp.log(denom)


def _causal():
    row = lax.broadcasted_iota(jnp.int32, (BLK, BLK), 0)
    col = lax.broadcasted_iota(jnp.int32, (BLK, BLK), 1)
    return row >= col


def _mask_w_once(wsp_ref, wm_scr):
    @pl.when(pl.program_id(0) == 0)
    def _():
        causal = _causal()
        for hh in range(N_HEADS):
            wm_scr[hh] = jnp.where(causal, wsp_ref[hh], 0.0).astype(BF16)


def _mixer_fwd(h_t, cos_t, sin_t, w_spatial, b_spatial, vln_g, vln_b, sinks, band_bias, comms=()):
    t_tok = h_t.shape[1]
    group = N_HEADS // N_KV_HEADS

    def body(sinks_ref, u_ref, vg_ref, q_ref, kvc_ref, kvp_ref, cos_ref, sin_ref, cosp_ref, sinp_ref,
             wsp_ref, bsp_ref, g_ref, b_ref, bias_ref, cat_ref, lse_ref, wm_scr):
        i = pl.program_id(0)
        _mask_w_once(wsp_ref, wm_scr)
        lse_ref[...] = jnp.zeros_like(lse_ref)
        ua = _gelu(u_ref[...])
        vp, _, _ = _ln_fwd_t(_gelu(vg_ref[...]), g_ref[...], b_ref[...])
        vpb = vp.astype(BF16)
        for b in range(MIX_BLOCKS):
            for hh in range(N_HEADS):
                rows = slice(hh * HEAD_DIM, (hh + 1) * HEAD_DIM)
                mixed = _dot(vpb[rows, _cols(b)], wm_scr[hh], NT) + bsp_ref[hh:hh + 1, :]
                cat_ref[rows, _cols(b)] = (ua[rows, _cols(b)] * mixed).astype(BF16)

        kvc, cos, sin = kvc_ref[...], cos_ref[...], sin_ref[...]
        qr = (_rope_t(q_ref[...], cos, sin) * SCORE_SCALE).astype(BF16)
        sinks4 = [_group_lanes([jnp.full((1, BLK), sinks_ref[hh], F32) for hh in range(kv * group, (kv + 1) * group)])
                  for kv in range(N_KV_HEADS)]
        for b in range(MIX_BLOCKS):
            kv_cur, kv_prev, cosc, sinc, cosp, sinp, bias1 = _block_inputs(b, i, kvc, kvp_ref, cos, sin, cosp_ref, sinp_ref, bias_ref)
            _, k_n, v_t = _keys_values(kv_cur, kv_prev, cosc, sinc, cosp, sinp)
            bias = _group_lanes([bias1] * group)
            for kv in range(N_KV_HEADS):
                heads = range(kv * group, (kv + 1) * group)
                qs = _group_lanes([qr[hh * HEAD_DIM:(hh + 1) * HEAD_DIM, _cols(b)] for hh in heads])
                p, lse = _softmax_sink_t(_dot(k_n, _pad_head(qs, kv)) + bias, sinks4[kv])
                lse_ref[b * N_KV_HEADS + kv:b * N_KV_HEADS + kv + 1, :] = lse
                o = _dot(v_t[kv * HEAD_DIM:(kv + 1) * HEAD_DIM], p.astype(BF16)).astype(BF16)
                for j, hh in enumerate(heads):
                    cat_ref[D_GMLP + hh * HEAD_DIM:D_GMLP + (hh + 1) * HEAD_DIM, _cols(b)] = o[:, j * BLK:(j + 1) * BLK]

    full = lambda shape: pl.BlockSpec(shape, lambda i: (0,) * len(shape))
    return _carry(
        body, name="mixer_fwd", grid=(t_tok // MIX_W,), comms=comms,
        in_specs=[pl.BlockSpec(memory_space=pltpu.SMEM)] + _h_specs() + _table_specs() + [
            full((N_HEADS, BLK, BLK)), full((N_HEADS, BLK)), full((D_GMLP, 1)), full((D_GMLP, 1)), BIAS_SPEC],
        out_specs=[pl.BlockSpec((D_GMLP + D_ATTN, MIX_W), lambda i: (0, i)), LSE_SPEC],
        out_shape=[jax.ShapeDtypeStruct((D_GMLP + D_ATTN, t_tok), BF16),
                   jax.ShapeDtypeStruct((t_tok // MIX_W * LSE_ROWS, D_ATTN), F32)],
        scratch_shapes=[pltpu.VMEM((N_HEADS, BLK, BLK), BF16)],
        args=(sinks, h_t, h_t, h_t, h_t, h_t, cos_t, sin_t, cos_t, sin_t, w_spatial, b_spatial, vln_g, vln_b, band_bias))


def _proj_out(cat_t, x2, w_out_b, ln1_g, ln1_b, comms=()):
    t_tok, d = x2.shape
    tm = min(512, t_tok)

    def body(cat_ref, x_ref, w_ref, g_ref, b_ref, xhat_ref, rstd_ref, x1b_ref):
        x1, xhat, rstd = _ln_fwd(ALPHA * x_ref[...] + _dot(cat_ref[...], w_ref[...], TN), g_ref[...], b_ref[...])
        xhat_ref[...] = xhat
        rstd_ref[...] = rstd
        x1b_ref[...] = x1.astype(BF16)

    tok = lambda w: pl.BlockSpec((tm, w), lambda i: (i, 0))
    vec = pl.BlockSpec((1, d), lambda i: (0, 0))
    return _carry(
        body, name="proj_out", grid=(t_tok // tm,), comms=comms,
        in_specs=[pl.BlockSpec((cat_t.shape[0], tm), lambda i: (0, i)), tok(d), pl.BlockSpec(w_out_b.shape, lambda i: (0, 0)), vec, vec],
        out_specs=[tok(d), tok(1), tok(d)],
        out_shape=[jax.ShapeDtypeStruct((t_tok, d), F32), jax.ShapeDtypeStruct((t_tok, 1), F32), jax.ShapeDtypeStruct((t_tok, d), BF16)],
        args=(cat_t, x2, w_out_b, ln1_g, ln1_b))


def _ffn_fwd_bwd(xhat1, rstd1, x1b, target, w1_parts, w2_parts, ln1_g, ln1_b, ln2_g, ln2_b):
    t_tok, d = xhat1.shape
    n_part = len(w1_parts)
    n_chunk, _, fp = w1_parts[0].shape
    f = n_chunk * n_part * fp
    tm = min(FFN_ROWS, t_tok)

    def body(xhat1_ref, rstd1_ref, x1b_ref, tgt_ref, *refs):
        w1_hbm, w2_hbm = refs[:n_part], refs[n_part:2 * n_part]
        (g1_ref, b1_ref, g2_ref, b2_ref, act_ref, dpre_ref, dz2b_ref, dz1_ref, stats_ref,
         r_scr, w1_ref, w2_ref, w_sems) = refs[2 * n_part:]

        @pl.when(pl.program_id(0) == 0)
        def _():
            stats_ref[...] = jnp.zeros_like(stats_ref)
            loads = []
            for j in range(n_chunk):
                for p in range(n_part):
                    units = pl.ds((j * n_part + p) * fp, fp)
                    loads.append(pltpu.make_async_copy(w1_hbm[p].at[j], w1_ref.at[:, units], w_sems.at[0, p, j]))
                    loads.append(pltpu.make_async_copy(w2_hbm[p].at[j], w2_ref.at[units, :], w_sems.at[1, p, j]))
            for cp in loads:
                cp.start()
            for cp in loads:
                cp.wait()

        g1, g2 = g1_ref[...], g2_ref[...]
        xhat1 = xhat1_ref[...]
        r_scr[...] = jnp.maximum(_dot(x1b_ref[...], w1_ref[...]), 0.0)
        r = r_scr[...]
        act = (r * r).astype(BF16)
        act_ref[...] = act
        ff = _dot(act, w2_ref[...])
        y, xhat2, rstd2 = _ln_fwd(ALPHA * (xhat1 * g1 + b1_ref[...]) + ff, g2, b2_ref[...])
        diff = y - tgt_ref[...]
        loss = 0.5 * jnp.sum(jnp.sum(diff * diff, axis=-1, keepdims=True) / d, axis=0, keepdims=True)
        dy = diff / d
        dz2 = _ln_bwd(dy, xhat2, rstd2, g2)
        dz2b = dz2.astype(BF16)
        dz2b_ref[...] = dz2b
        dpre = (_dot(dz2b, w2_ref[...], NT) * (2.0 * r_scr[...])).astype(BF16)
        dpre_ref[...] = dpre
        dx1 = ALPHA * dz2 + _dot(dpre, w1_ref[...], NT)
        dz1_ref[...] = _ln_bwd(dx1, xhat1, rstd1_ref[...], g1)
        stats_ref[0:1, :] += jnp.sum(dx1 * xhat1, axis=0, keepdims=True)
        stats_ref[1:2, :] += jnp.sum(dx1, axis=0, keepdims=True)
        stats_ref[2:3, :] += jnp.sum(dy * xhat2, axis=0, keepdims=True)
        stats_ref[3:4, :] += jnp.sum(dy, axis=0, keepdims=True)
        stats_ref[4:5, :] += jnp.broadcast_to(loss, (1, d))

    tok = lambda w: pl.BlockSpec((tm, w), lambda i: (i, 0))
    vec = pl.BlockSpec((1, d), lambda i: (0, 0))
    return _carry(
        body, name="ffn_fwd_bwd", grid=(t_tok // tm,),
        in_specs=[tok(d), tok(1), tok(d), tok(d)] + [ANY] * (2 * n_part) + [vec, vec, vec, vec],
        out_specs=[tok(f), tok(f), tok(d), tok(d), pl.BlockSpec((8, d), lambda i: (0, 0))],
        out_shape=[jax.ShapeDtypeStruct((t_tok, f), BF16), jax.ShapeDtypeStruct((t_tok, f), BF16),
                   jax.ShapeDtypeStruct((t_tok, d), BF16), jax.ShapeDtypeStruct((t_tok, d), F32), jax.ShapeDtypeStruct((8, d), F32)],
        scratch_shapes=[pltpu.VMEM((tm, f), F32), pltpu.VMEM((d, f), BF16), pltpu.VMEM((f, d), BF16),
                        pltpu.SemaphoreType.DMA((2, n_part, n_chunk))],
        args=(xhat1, rstd1, x1b, target, *w1_parts, *w2_parts, ln1_g, ln1_b, ln2_g, ln2_b))[0]


def _ffn_wgrad(name, lhs, rhs, chunk_lhs, core_chip, comms=()):
    t_tok = lhs.shape[0]
    half = N_DEV // 2
    fc = (lhs if chunk_lhs else rhs).shape[1] // N_DEV
    chunk = (fc, rhs.shape[1]) if chunk_lhs else (lhs.shape[1], fc)

    def shard(s, cc):
        return 2 * (s % half) + jnp.where(s < half, 1 - cc[0], cc[0])

    def body(cc_ref, lhs_ref, rhs_ref, wire_ref, own_ref, recv_ref, send_buf, got, send_sems, recv_sems, got_sem):
        s = pl.program_id(0)
        x, y, c = _place()
        def send(q):
            return pltpu.make_async_remote_copy(
                src_ref=send_buf.at[q % 2], dst_ref=recv_ref.at[q], send_sem=send_sems.at[q], recv_sem=recv_sems.at[q],
                device_id=(x, y, 1 - c), device_id_type=MESH)

        def load(q):
            return pltpu.make_async_copy(recv_ref.at[q], got, got_sem.at[0])

        @pl.when(s >= half)
        def _():
            send(s - half).wait_recv()
            load(s - half).start()

        g = _dot(lhs_ref[...], rhs_ref[...], TN)

        for q in range(half):
            @pl.when(s == q)
            def _(q=q):
                if q >= 2:
                    send(q - 2).wait_send()
                send_buf[q % 2] = g
                send(q).start()

            @pl.when(s == half + q)
            def _(q=q):
                load(q).wait()
                total = g + got[...]
                wire_ref[...] = total.astype(BF16)

                @pl.when(cc_ref[1] == q)
                def _():
                    own_ref[...] = total

        @pl.when(s == N_DEV - 1)
        def _():
            for q in range(half - 2, half):
                send(q).wait_send()

    resident = lambda a: pl.BlockSpec(a.shape, lambda s, cc: (0, 0), pipeline_mode=pl.Buffered(1))
    chunked = pl.BlockSpec((t_tok, fc), lambda s, cc: (0, shard(s, cc)))
    (wire, own, _), per_comm = _carry(
        body, name=name, grid=(N_DEV,), comms=comms, prefetch=(core_chip,),
        in_specs=[chunked, resident(rhs)] if chunk_lhs else [resident(lhs), chunked],
        out_specs=[pl.BlockSpec((None,) + chunk, lambda s, cc: (jnp.maximum(s - half, 0), 0, 0)),
                   pl.BlockSpec(chunk, lambda s, cc: (0, 0)), ANY],
        out_shape=[jax.ShapeDtypeStruct((half,) + chunk, BF16), jax.ShapeDtypeStruct(chunk, F32),
                   jax.ShapeDtypeStruct((half,) + chunk, F32)],
        scratch_shapes=[pltpu.VMEM((2,) + chunk, F32), pltpu.VMEM(chunk, F32), pltpu.SemaphoreType.DMA((half,)),
                        pltpu.SemaphoreType.DMA((half,)), pltpu.SemaphoreType.DMA((1,))],
        args=(lhs, rhs))
    return wire, own, per_comm


def _proj_out_bwd(dz1, cat_t, w_out_b, comms=()):
    t_tok, d = dz1.shape
    d_mix = cat_t.shape[0]
    tm = min(512, t_tok)

    def body(dz1_ref, cat_ref, w_ref, dcat_ref, gw_ref):
        @pl.when(pl.program_id(0) == 0)
        def _():
            gw_ref[...] = jnp.zeros_like(gw_ref)

        dzb = dz1_ref[...].astype(BF16)
        dcat_ref[...] = _dot(w_ref[...], dzb, NT)
        gw_ref[...] += _dot(cat_ref[...], dzb)

    return _carry(
        body, name="proj_out_bwd", grid=(t_tok // tm,), comms=comms,
        in_specs=[pl.BlockSpec((tm, d), lambda i: (i, 0)), pl.BlockSpec((d_mix, tm), lambda i: (0, i)),
                  pl.BlockSpec((d_mix, d), lambda i: (0, 0))],
        out_specs=[pl.BlockSpec((d_mix, tm), lambda i: (0, i)), pl.BlockSpec((d_mix, d), lambda i: (0, 0))],
        out_shape=[jax.ShapeDtypeStruct((d_mix, t_tok), F32), jax.ShapeDtypeStruct((d_mix, d), F32)],
        args=(dz1, cat_t, w_out_b))


def _mixer_bwd(dcat_t, h_t, cos_t, sin_t, w_spatial, b_spatial, vln_g, vln_b, sinks, band_bias, lse, comms=()):
    t_tok = h_t.shape[1]
    nb, n_step = t_tok // BLK, t_tok // MIX_W
    group = N_HEADS // N_KV_HEADS

    def body(sinks_ref, dcat_ref, u_ref, vg_ref, q_ref, kvc_ref, kvp_ref, cos_ref, sin_ref, cosp_ref, sinp_ref,
             wsp_ref, bsp_ref, g_ref, b_ref, bias_ref, lse_ref, dh_ref, dkvc_ref, dkvp_ref, gwsb_ref, gbsp_ref, gvln_ref, gsink_ref,
             dg_acc, db_acc, wm_scr, gws_ref):
        i = pl.program_id(0)

        @pl.when(i == 0)
        def _():
            gws_ref[...] = jnp.zeros_like(gws_ref)
            gbsp_ref[...] = jnp.zeros_like(gbsp_ref)
            gsink_ref[...] = jnp.zeros_like(gsink_ref)
            dg_acc[...] = jnp.zeros_like(dg_acc)
            db_acc[...] = jnp.zeros_like(db_acc)

        _mask_w_once(wsp_ref, wm_scr)

        g = g_ref[...]
        ua, ua_grad = _gelu_and_grad(u_ref[...])
        vv, vv_grad = _gelu_and_grad(vg_ref[...])
        vp, vhat, rstd = _ln_fwd_t(vv, g, b_ref[...])
        vpb = vp.astype(BF16)
        da = dcat_ref[0:D_GMLP, :]
        dmixed = da * ua
        dvp_blocks = []
        for b in range(MIX_BLOCKS):
            dvp_parts = []
            for hh in range(N_HEADS):
                rows = slice(hh * HEAD_DIM, (hh + 1) * HEAD_DIM)
                vpb_h = vpb[rows, _cols(b)]
                mixed = _dot(vpb_h, wm_scr[hh], NT) + bsp_ref[hh:hh + 1, :]
                dh_ref[COL_U + hh * HEAD_DIM:COL_U + (hh + 1) * HEAD_DIM, _cols(b)] = (
                    da[rows, _cols(b)] * mixed * ua_grad[rows, _cols(b)]).astype(BF16)
                dm = dmixed[rows, _cols(b)]
                dmb = dm.astype(BF16)
                gbsp_ref[hh:hh + 1, :] += jnp.sum(dm, axis=0, keepdims=True)
                gws_ref[hh] += _dot(dmb, vpb_h, TN)
                dvp_parts.append(_dot(dmb, wm_scr[hh]))
            dvp_blocks.append(jnp.concatenate(dvp_parts, axis=0))
        dvp = jnp.concatenate(dvp_blocks, axis=1)
        dgv, dbv = dvp * vhat, dvp
        for b in range(MIX_BLOCKS):
            dg_acc[...] += dgv[:, _cols(b)]
            db_acc[...] += dbv[:, _cols(b)]
        dh_ref[COL_V:COL_V + D_GMLP, :] = (_ln_bwd_t(dvp, vhat, rstd, g) * vv_grad).astype(BF16)

        kvc, cos, sin = kvc_ref[...], cos_ref[...], sin_ref[...]
        qr = (_rope_t(q_ref[...], cos, sin) * SCORE_SCALE).astype(BF16)
        sinks4 = [_group_lanes([jnp.full((1, BLK), sinks_ref[hh], F32) for hh in range(kv * group, (kv + 1) * group)])
                  for kv in range(N_KV_HEADS)]
        dq_blocks, dkv_cur, dkv_prev = [], [], []
        for b in range(MIX_BLOCKS):
            kv_cur, kv_prev, cosc, sinc, cosp, sinp, bias1 = _block_inputs(b, i, kvc, kvp_ref, cos, sin, cosp_ref, sinp_ref, bias_ref)
            k_t, k_n, v_t = _keys_values(kv_cur, kv_prev, cosc, sinc, cosp, sinp)
            v_n = jnp.concatenate([kv_prev[D_KV:].T, kv_cur[D_KV:].T], axis=0).astype(BF16)
            bias = _group_lanes([bias1] * group)
            dk, dv, dq_parts = [], [], []
            for kv in range(N_KV_HEADS):
                heads = range(kv * group, (kv + 1) * group)
                kv_rows = slice(kv * HEAD_DIM, (kv + 1) * HEAD_DIM)
                qs = _group_lanes([qr[hh * HEAD_DIM:(hh + 1) * HEAD_DIM, _cols(b)] for hh in heads])
                dos = _group_lanes([dcat_ref[D_GMLP + hh * HEAD_DIM:D_GMLP + (hh + 1) * HEAD_DIM, _cols(b)]
                                    for hh in heads]).astype(BF16)
                lse_g = lse_ref[b * N_KV_HEADS + kv:b * N_KV_HEADS + kv + 1, :]
                p = jnp.exp(_dot(k_n, _pad_head(qs, kv)) + bias - lse_g)
                p_sink = jnp.exp(sinks4[kv] - lse_g)
                dp = _dot(v_n, _pad_head(dos, kv))
                delta = jnp.sum(p * dp, axis=0, keepdims=True)
                ds = (p * (dp - delta)).astype(BF16)
                dsink = p_sink * delta
                dq = _dot(k_t[kv_rows], ds) * SCORE_SCALE
                for j, hh in enumerate(heads):
                    gsink_ref[hh:hh + 1, :] -= dsink[:, j * BLK:(j + 1) * BLK]
                    dq_parts.append(dq[:, j * BLK:(j + 1) * BLK])
                dk.append(_dot(qs, ds, NT))
                dv.append(_dot(dos, p.astype(BF16), NT))
            dq_blocks.append(jnp.concatenate(dq_parts, axis=0))
            dk_all, dv_all = jnp.concatenate(dk, axis=0), jnp.concatenate(dv, axis=0)
            dkv_cur.append(jnp.concatenate([_rope_t(dk_all[:, BLK:], cosc, sinc, bwd=True), dv_all[:, BLK:]], axis=0))
            dkv_prev.append(jnp.concatenate([_rope_t(dk_all[:, :BLK], cosp, sinp, bwd=True), dv_all[:, :BLK]], axis=0))
        dh_ref[COL_Q:COL_Q + D_ATTN, :] = _rope_t(jnp.concatenate(dq_blocks, axis=1), cos, sin, bwd=True).astype(BF16)
        for b in range(MIX_BLOCKS):
            dkvc_ref[:, _cols(b)] = dkv_cur[b] + dkv_prev[b + 1] if b + 1 < MIX_BLOCKS else dkv_cur[b]
        dkvp_ref[...] = dkv_prev[0]

        @pl.when(i == n_step - 1)
        def _():
            causal = _causal()
            for hh in range(N_HEADS):
                gwsb_ref[hh] = jnp.where(causal, gws_ref[hh], 0.0).astype(BF16)
            gvln_ref[...] = jnp.zeros_like(gvln_ref)
            gvln_ref[0:1, :] = jnp.sum(dg_acc[...].T, axis=0, keepdims=True)
            gvln_ref[1:2, :] = jnp.sum(db_acc[...].T, axis=0, keepdims=True)

    full = lambda shape: pl.BlockSpec(shape, lambda i: (0,) * len(shape))
    return _carry(
        body, name="mixer_bwd", grid=(n_step,), comms=comms,
        in_specs=[pl.BlockSpec(memory_space=pltpu.SMEM), pl.BlockSpec((D_GMLP + D_ATTN, MIX_W), lambda i: (0, i))]
        + _h_specs() + _table_specs()
        + [full((N_HEADS, BLK, BLK)), full((N_HEADS, BLK)), full((D_GMLP, 1)), full((D_GMLP, 1)), BIAS_SPEC, LSE_SPEC],
        out_specs=[pl.BlockSpec((COL_K, MIX_W), lambda i: (0, i)), pl.BlockSpec((2 * D_KV, MIX_W), lambda i: (0, i)),
                   pl.BlockSpec((2 * D_KV, BLK), lambda i: (0, (i + n_step - 1) % n_step)),
                   full((N_HEADS, BLK, BLK)), full((N_HEADS, BLK)), full((8, D_GMLP)), full((N_HEADS, LANES))],
        out_shape=[jax.ShapeDtypeStruct((COL_K, t_tok), BF16), jax.ShapeDtypeStruct((2 * D_KV, t_tok), F32),
                   jax.ShapeDtypeStruct((2 * D_KV, n_step * BLK), F32),
                   jax.ShapeDtypeStruct((N_HEADS, BLK, BLK), BF16), jax.ShapeDtypeStruct((N_HEADS, BLK), F32),
                   jax.ShapeDtypeStruct((8, D_GMLP), F32), jax.ShapeDtypeStruct((N_HEADS, LANES), F32)],
        scratch_shapes=[pltpu.VMEM((D_GMLP, BLK), F32), pltpu.VMEM((D_GMLP, BLK), F32), pltpu.VMEM((N_HEADS, BLK, BLK), BF16),
                        pltpu.VMEM((N_HEADS, BLK, BLK), F32)],
        args=(sinks, dcat_t, h_t, h_t, h_t, h_t, h_t, cos_t, sin_t, cos_t, sin_t, w_spatial, b_spatial, vln_g, vln_b, band_bias, lse))


def _proj_in_wgrad(dh_b, dkvc_t, dkvp_t, xb, comms=()):
    t_tok, d = xb.shape
    d_main, d_kv = dh_b.shape[0], dkvc_t.shape[0]
    tm = min(1024, t_tok)

    def body(dh_ref, dkvc_ref, dkvp_ref, xb_ref, dkvb_ref, gw_ref):
        @pl.when(pl.program_id(0) == 0)
        def _():
            gw_ref[...] = jnp.zeros_like(gw_ref)

        for s in range(tm // MIX_W):
            last = slice((s + 1) * MIX_W - BLK, (s + 1) * MIX_W)
            dkvb_ref[:, s * MIX_W:(s + 1) * MIX_W - BLK] = dkvc_ref[:, s * MIX_W:(s + 1) * MIX_W - BLK].astype(BF16)
            dkvb_ref[:, last] = (dkvc_ref[:, last] + dkvp_ref[:, _cols(s)]).astype(BF16)
        gw_ref[0:d_main, :] += _dot(dh_ref[...], xb_ref[...])
        gw_ref[d_main:, :] += _dot(dkvb_ref[...], xb_ref[...])

    tok = lambda rows: pl.BlockSpec((rows, tm), lambda i: (0, i))
    return _carry(
        body, name="proj_in_wgrad", grid=(t_tok // tm,), comms=comms,
        in_specs=[tok(d_main), tok(d_kv), pl.BlockSpec((d_kv, tm // MIX_BLOCKS), lambda i: (0, i)),
                  pl.BlockSpec((tm, d), lambda i: (i, 0))],
        out_specs=[tok(d_kv), pl.BlockSpec((d_main + d_kv, d), lambda i: (0, 0))],
        out_shape=[jax.ShapeDtypeStruct((d_kv, t_tok), BF16), jax.ShapeDtypeStruct((d_main + d_kv, d), F32)],
        args=(dh_b, dkvc_t, dkvp_t, xb))


def _proj_in_dgrad(dh_b, dkv_b, dz1, w_in_t, comms=()):
    t_tok, d = dz1.shape
    d_main, d_kv = dh_b.shape[0], dkv_b.shape[0]
    tm = min(512, t_tok)

    def body(dh_ref, dkv_ref, dz1_ref, w_ref, dx_ref):
        dx_ref[...] = (ALPHA * dz1_ref[...] + _dot(dh_ref[...], w_ref[0:d_main, :], TN)
                       + _dot(dkv_ref[...], w_ref[d_main:, :], TN))

    return _carry(
        body, name="proj_in_dgrad", grid=(t_tok // tm,), comms=comms,
        in_specs=[pl.BlockSpec((d_main, tm), lambda i: (0, i)), pl.BlockSpec((d_kv, tm), lambda i: (0, i)),
                  pl.BlockSpec((tm, d), lambda i: (i, 0)), pl.BlockSpec((d_main + d_kv, d), lambda i: (0, 0))],
        out_specs=[pl.BlockSpec((tm, d), lambda i: (i, 0))],
        out_shape=[jax.ShapeDtypeStruct((t_tok, d), F32)],
        args=(dh_b, dkv_b, dz1, w_in_t))


def _adamw(w, g, m, v):
    m = ADAM_B1 * m + (1.0 - ADAM_B1) * g
    v = ADAM_B2 * v + (1.0 - ADAM_B2) * (g * g)
    m_hat = m / (1.0 - ADAM_B1 ** ADAM_STEP)
    v_hat = v / (1.0 - ADAM_B2 ** ADAM_STEP)
    delta = -ADAM_LR * (m_hat / (jnp.sqrt(v_hat) + ADAM_EPS) + ADAM_WD * w)
    return delta, m, v


ADAMW_STEPS = 2


def _adamw_shards(name, items, comms=()):
    def body(*refs):
        ins, outs = refs[:5 * len(items)], refs[5 * len(items):]
        for i in range(len(items)):
            own_ref, recv_ref, w_ref, m_ref, v_ref = ins[5 * i:5 * i + 5]
            g = ((own_ref[...] + recv_ref[0].astype(F32)) + recv_ref[1].astype(F32)) + recv_ref[2].astype(F32)
            for o_ref, val in zip(outs[4 * i:4 * i + 4], (g,) + _adamw(w_ref[...], g, m_ref[...], v_ref[...])):
                o_ref[...] = val

    in_specs, out_specs, out_shape, args = [], [], [], []
    for own, recv, w, m, v in items:
        r, c = own.shape
        blk = pl.BlockSpec((r // ADAMW_STEPS, c), lambda s: (s, 0))
        in_specs += [blk, pl.BlockSpec((3, r // ADAMW_STEPS, c), lambda s: (0, s, 0)), blk, blk, blk]
        out_specs += [blk] * 4
        out_shape += [jax.ShapeDtypeStruct((r, c), F32)] * 4
        args += [own, recv, w, m, v]
    res, per_comm = _carry(body, name=name, grid=(ADAMW_STEPS,), comms=comms, in_specs=in_specs, out_specs=out_specs,
                           out_shape=out_shape, args=args)
    return [res[4 * i:4 * i + 4] for i in range(len(items))], per_comm


VEC_VLN, VEC_LN1G, VEC_LN1B, VEC_LN2G, VEC_LN2B, VEC_SINK, VEC_LOSS, VEC_BSP, VEC_ROWS = 0, 1, 2, 3, 4, 5, 6, 8, 16


def _adamw_small(parts_w, parts_vec, params):
    n = parts_w.shape[0]
    flat = [a for p in params for a in p]
    shapes = [p[0].shape for p in params]

    def grads(gw, gv):
        return [gw, gv[VEC_VLN:VEC_VLN + 1, 0:D_GMLP], gv[VEC_VLN:VEC_VLN + 1, D_GMLP:2 * D_GMLP],
                gv[VEC_BSP:VEC_BSP + N_HEADS, 0:BLK], gv[VEC_LN1G:VEC_LN1G + 1], gv[VEC_LN1B:VEC_LN1B + 1],
                gv[VEC_LN2G:VEC_LN2G + 1], gv[VEC_LN2B:VEC_LN2B + 1], gv[VEC_SINK:VEC_SINK + 1, 0:N_HEADS]]

    def body(pw_ref, pv_ref, *refs):
        ins, outs = refs[:len(flat)], refs[len(flat):]
        gw, gv = pw_ref[0].astype(F32), pv_ref[0]
        for k in range(1, n):
            gw, gv = gw + pw_ref[k].astype(F32), gv + pv_ref[k]
        for i, g in enumerate(grads(gw, gv)):
            w_ref, m_ref, v_ref = ins[3 * i:3 * i + 3]
            delta, m_new, v_new = _adamw(w_ref[...], g, m_ref[...], v_ref[...])
            for o_ref, val in zip(outs[4 * i:4 * i + 4], (g, delta, m_new, v_new)):
                o_ref[...] = val
        outs[-1][...] = gv[VEC_LOSS:VEC_LOSS + 1, 0:LANES]

    whole = lambda shape: pl.BlockSpec(shape, lambda i: (0,) * len(shape))
    res = _carry(
        body, name="adamw_small", grid=(1,),
        in_specs=[whole(parts_w.shape), whole(parts_vec.shape)] + [whole(a.shape) for a in flat],
        out_specs=[whole(s) for s in shapes for _ in range(4)] + [whole((1, LANES))],
        out_shape=[jax.ShapeDtypeStruct(s, F32) for s in shapes for _ in range(4)] + [jax.ShapeDtypeStruct((1, LANES), F32)],
        args=(parts_w, parts_vec, *flat))[0]
    return [res[4 * i:4 * i + 4] for i in range(len(params))], res[-1]


def _pair_sum(name, parts, recv, core_chip, comms=()):
    _, r, c = parts.shape
    tr = r if r <= 512 else 512

    def body(cc_ref, a_ref, b_ref, wire_ref, own_ref):
        s = a_ref[...] + b_ref[...]
        wire_ref[...] = s.astype(BF16)

        @pl.when(pl.program_id(1) == cc_ref[1])
        def _():
            own_ref[...] = s

    return _carry(
        body, name=name, grid=(r // tr, 4), prefetch=(core_chip,), comms=comms,
        in_specs=[pl.BlockSpec((None, tr, c), lambda i, q, cc: (2 * q + cc[0], i, 0)),
                  pl.BlockSpec((None, tr, c), lambda i, q, cc: (q, i, 0))],
        out_specs=[pl.BlockSpec((None, tr, c), lambda i, q, cc: (q, i, 0)), pl.BlockSpec((tr, c), lambda i, q, cc: (i, 0))],
        out_shape=[jax.ShapeDtypeStruct((4, r, c), BF16), jax.ShapeDtypeStruct((r, c), F32)],
        args=(parts, recv))


def kernel(x, positions, w_in, v_ln_g, v_ln_b, w_spatial, b_spatial, sinks, w_out, ln1_g, ln1_b, w_ff1, w_ff2, ln2_g, ln2_b, loss_target, m_w_in, m_v_ln_g, m_v_ln_b, m_w_spatial, m_b_spatial, m_sinks, m_w_out, m_ln1_g, m_ln1_b, m_w_ff1, m_w_ff2, m_ln2_g, m_ln2_b, v_w_in, v_v_ln_g, v_v_ln_b, v_w_spatial, v_b_spatial, v_sinks, v_w_out, v_ln1_g, v_ln1_b, v_w_ff1, v_w_ff2, v_ln2_g, v_ln2_b):
    _, t_tok, d = x.shape
    xi, yi, ci = _place()
    core_chip = jnp.stack([ci, 2 * xi + yi]).astype(jnp.int32)
    x2 = x.reshape(t_tok, d)
    target = loss_target.reshape(t_tok, d)
    inv_freq = ROPE_THETA ** (-jnp.arange(0, HEAD_DIM, 2, dtype=F32) / HEAD_DIM)
    wsp, bsp, sink_vec = w_spatial[0], b_spatial[0], sinks[0]
    vg_col, vb_col = v_ln_g.reshape(D_GMLP, 1), v_ln_b.reshape(D_GMLP, 1)
    big = {"in": w_in[0], "out": w_out[0], "ff1": w_ff1[0], "ff2": w_ff2[0]}
    half1, half2 = big["ff1"].shape[1] // 2, big["ff2"].shape[0] // 2
    w1_mine = [big["ff1"][:, :half1].astype(BF16), big["ff1"][:, half1:].astype(BF16)]
    w2_mine = [big["ff2"][:half2].astype(BF16), big["ff2"][half2:].astype(BF16)]

    (cos_t, sin_t), ((g_in,),) = _rope_tables(
        positions, jnp.tile(inv_freq, 2).reshape(HEAD_DIM, 1), comms=[_gather_comm([big["in"].T.astype(BF16)])])
    w_in_t = g_in.reshape(D_IN, d)
    (h_t, xb), ((g_out, w1_a),) = _proj_in(x2, w_in_t, comms=[_gather_comm([big["out"].astype(BF16), w1_mine[0]])])
    w_out_b = g_out.reshape(-1, d)
    band_bias = _band_bias()
    (cat_t, lse), ((w1_b, w2_a),) = _mixer_fwd(h_t, cos_t, sin_t, wsp, bsp, vg_col, vb_col, sink_vec, band_bias,
                                                comms=[_gather_comm([w1_mine[1], w2_mine[0]])])
    (xhat1, rstd1, x1b), ((w2_b,),) = _proj_out(cat_t, x2, w_out_b, ln1_g, ln1_b, comms=[_gather_comm([w2_mine[1]])])
    act_b, dpre_b, dz2b, dz1, stats = _ffn_fwd_bwd(xhat1, rstd1, x1b, target, [w1_a, w1_b], [w2_a, w2_b], ln1_g, ln1_b, ln2_g, ln2_b)

    (dcat_t, gw_out), _ = _proj_out_bwd(dz1, cat_t, w_out_b)
    p_out = gw_out.reshape(N_DEV, -1, d)
    wire_ff1, own_ff1, ((s_out,),) = _ffn_wgrad("ffn_wgrad1", x1b, dpre_b, False, core_chip, comms=[_sibling_comm([p_out])])
    (wire_out, own_out), _ = _pair_sum("pair_sum_out", p_out, s_out, core_chip)
    wire_ff2, own_ff2, ((r_ff1,),) = _ffn_wgrad("ffn_wgrad2", act_b, dz2b, True, core_chip, comms=[_chips_comm([wire_ff1])])
    (dh_b, dkvc_t, dkvp_t, g_wsp, g_bsp, g_vln, g_sink), ((r_ff2, r_out),) = _mixer_bwd(
        dcat_t, h_t, cos_t, sin_t, wsp, bsp, vg_col, vb_col, sink_vec, band_bias, lse,
        comms=[_chips_comm([wire_ff2, wire_out])])
    sink_row = jnp.pad(g_sink.sum(axis=1).reshape(1, N_HEADS), ((0, 0), (0, d - N_HEADS)))
    small_vec = jnp.concatenate([g_vln[0:2].reshape(1, d), stats[0:4], sink_row, stats[4:5], jnp.zeros((1, d), F32),
                                 jnp.pad(g_bsp, ((0, 0), (0, d - BLK)))], axis=0)
    (dkv_b, gw_in_t), ((parts_w, parts_vec),) = _proj_in_wgrad(
        dh_b, dkvc_t, dkvp_t, xb, comms=[_gather_comm([g_wsp.reshape(-1, BLK), small_vec])])
    p_in = gw_in_t.reshape(N_DEV, -1, d)

    (out_out,), ((s_in,),) = _adamw_shards("adamw_out", [(own_out, r_out, big["out"], m_w_out[0], v_w_out[0])],
                                           comms=[_sibling_comm([p_in])])
    (wire_in, own_in), _ = _pair_sum("pair_sum_in", p_in, s_in, core_chip)
    (grad_x,), ((r_in,),) = _proj_in_dgrad(dh_b, dkv_b, dz1, w_in_t, comms=[_chips_comm([wire_in])])
    (ff1_out, ff2_out, in_out_t), _ = _adamw_shards("adamw_big", [
        (own_ff1, r_ff1, big["ff1"], m_w_ff1[0], v_w_ff1[0]), (own_ff2, r_ff2, big["ff2"], m_w_ff2[0], v_w_ff2[0]),
        (own_in, r_in, big["in"].T, m_w_in[0].T, v_w_in[0].T)])
    in_out = [o.T for o in in_out_t]
    small = [(w_spatial, m_w_spatial, v_w_spatial), (v_ln_g, m_v_ln_g, v_v_ln_g), (v_ln_b, m_v_ln_b, v_v_ln_b),
             (b_spatial, m_b_spatial, v_b_spatial), (ln1_g, m_ln1_g, v_ln1_g), (ln1_b, m_ln1_b, v_ln1_b),
             (ln2_g, m_ln2_g, v_ln2_g), (ln2_b, m_ln2_b, v_ln2_b), (sinks, m_sinks, v_sinks)]
    views = [(-1, BLK), None, None, (N_HEADS, BLK)] + [None] * 5
    small_res, loss_row = _adamw_small(parts_w, parts_vec, [
        tuple(a if vw is None else a.reshape(vw) for a in p) for p, vw in zip(small, views)])
    small_out = [[o.reshape(p[0].shape) for o in res] for res, p in zip(small_res, small)]
    loss = loss_row[0, 0]

    big_out = {0: in_out, 6: out_out, 9: ff1_out, 10: ff2_out}
    small_slot = {3: 0, 1: 1, 2: 2, 4: 3, 7: 4, 8: 5, 11: 6, 12: 7, 5: 8}
    outs = [loss, grad_x.reshape(x.shape)]
    for kind in range(4):
        for wi in range(13):
            outs.append(big_out[wi][kind][None] if wi in big_out else small_out[small_slot[wi]][kind])
    return tuple(outs)
```

```python
import math

import jax
import jax.numpy as jnp
from jax import lax
from jax.experimental import pallas as pl
from jax.experimental.pallas import tpu as pltpu

F32 = jnp.float32
BF16 = jnp.bfloat16
MESH = pl.DeviceIdType.MESH

HEAD_DIM = 64
N_HEADS = 8
N_KV_HEADS = 2
BLK = 128
D_GMLP = N_HEADS * HEAD_DIM
D_ATTN = N_HEADS * HEAD_DIM
D_KV = N_KV_HEADS * HEAD_DIM
D_IN = 2 * D_GMLP + D_ATTN + 2 * D_KV
COL_U, COL_V, COL_Q, COL_K = 0, D_GMLP, 2 * D_GMLP, 2 * D_GMLP + D_ATTN
ROPE_THETA = 10000.0
LN_EPS = 1e-5
ALPHA = 2.0 ** 0.25
NEG_INF = -1e30
SCORE_SCALE = 1.0 / math.sqrt(HEAD_DIM)
ADAM_LR, ADAM_B1, ADAM_B2, ADAM_EPS, ADAM_WD, ADAM_STEP = 0.001, 0.9, 0.999, 1e-08, 0.01, 10
N_DEV = 8
LANES = 128
VMEM_LIMIT = 56 * 1024 * 1024
FFN_ROWS = 256

NT = (((1,), (1,)), ((), ()))
TN = (((0,), (0,)), ((), ()))


def _params(*sem):
    return pltpu.CompilerParams(dimension_semantics=sem, vmem_limit_bytes=VMEM_LIMIT)


def _dot(a, b, dims=None):
    if dims is None:
        return jnp.dot(a, b, preferred_element_type=F32)
    return lax.dot_general(a, b, dims, preferred_element_type=F32)


def _mean(a):
    return jnp.mean(a, axis=-1, keepdims=True)


def _ln_fwd(z, g, b):
    zc = z - _mean(z)
    rstd = lax.rsqrt(_mean(zc * zc) + LN_EPS)
    xhat = zc * rstd
    return xhat * g + b, xhat, rstd


def _ln_bwd(dy, xhat, rstd, g):
    dxhat = dy * g
    return rstd * (dxhat - _mean(dxhat) - xhat * _mean(dxhat * xhat))


_GELU_C = math.sqrt(2.0 / math.pi)


def _gelu(x):
    t = jnp.tanh(_GELU_C * (x + 0.044715 * (x * x * x)))
    return 0.5 * x * (1.0 + t)


def _gelu_and_grad(x):
    x2 = x * x
    t = jnp.tanh(_GELU_C * (x + 0.044715 * (x2 * x)))
    hx, ht = 0.5 * x, 0.5 * (1.0 + t)
    return x * ht, ht + hx * (1.0 - t * t) * (_GELU_C * (1.0 + 3.0 * 0.044715 * x2))


def _mean0(a):
    return jnp.mean(a, axis=0, keepdims=True)


def _ln_fwd_t(z, g, b):
    zc = z - _mean0(z)
    rstd = lax.rsqrt(_mean0(zc * zc) + LN_EPS)
    xhat = zc * rstd
    return xhat * g + b, xhat, rstd


def _ln_bwd_t(dy, xhat, rstd, g):
    dxhat = dy * g
    return rstd * (dxhat - _mean0(dxhat) - xhat * _mean0(dxhat * xhat))


def _rope_t(t, cos, sin_signed, bwd=False):
    half = HEAD_DIM // 2
    outs = []
    for r in range(0, t.shape[0], HEAD_DIM):
        th = t[r:r + HEAD_DIM]
        sw = jnp.concatenate([th[half:], th[:half]], axis=0) * sin_signed
        outs.append(th * cos - sw if bwd else th * cos + sw)
    return jnp.concatenate(outs, axis=0)


ANY = pl.BlockSpec(memory_space=pl.ANY)


def _place():
    return lax.axis_index("x"), lax.axis_index("y"), lax.axis_index("c")


class _Comm:
    def __init__(self, ins, outs, sems, start, finish):
        self.ins, self.outs, self.sems, self.start, self.finish = ins, outs, sems, start, finish


def _gather_comm(arrs):
    n = len(arrs)

    def parts(ins, outs, sems):
        send_sems, recv_sems, local_sems = sems
        x, y, c = _place()
        me, sibling = (x, y, c), (x, y, 1 - c)
        chips = [(1 - x, y), (x, 1 - y), (1 - x, 1 - y)]

        def copy(a, k, block, to, src=None):
            px, py, pc = block
            dst = outs[a].at[4 * px + 2 * py + pc]
            return pltpu.make_async_remote_copy(
                src_ref=dst if src is None else src, dst_ref=dst,
                send_sem=send_sems.at[a, k], recv_sem=recv_sems.at[a, k], device_id=to, device_id_type=MESH)

        mine = [pltpu.make_async_copy(ins[a], outs[a].at[4 * x + 2 * y + c], local_sems.at[a]) for a in range(n)]
        first = []
        for a in range(n):
            first.append(copy(a, 0, me, sibling, src=ins[a]))
            first += [copy(a, 1 + j, me, (*chip, c), src=ins[a]) for j, chip in enumerate(chips)]
        return copy, mine, first, me, sibling, chips, c

    def start(ins, outs, sems):
        _, mine, first, *_ = parts(ins, outs, sems)
        for cp in mine + first:
            cp.start()

    def finish(ins, outs, sems):
        copy, mine, first, me, sibling, chips, c = parts(ins, outs, sems)
        passed = []
        for j, chip in enumerate(chips):
            for a in range(n):
                copy(a, 1 + j, (*chip, c), me).wait_recv()
                fwd = copy(a, 4 + j, (*chip, c), sibling)
                fwd.start()
                passed.append(fwd)
        for a in range(n):
            copy(a, 0, sibling, me).wait_recv()
        for j, chip in enumerate(chips):
            for a in range(n):
                copy(a, 4 + j, (*chip, 1 - c), me).wait_recv()
        for cp in first + passed:
            cp.wait_send()
        for cp in mine:
            cp.wait()

    return _Comm(list(arrs), [jax.ShapeDtypeStruct((N_DEV,) + a.shape, a.dtype) for a in arrs],
                 [pltpu.SemaphoreType.DMA((n, 7)), pltpu.SemaphoreType.DMA((n, 7)), pltpu.SemaphoreType.DMA((n,))],
                 start, finish)


def _sibling_comm(parts):
    n = len(parts)

    def copies(ins, outs, sems):
        x, y, c = _place()
        return [pltpu.make_async_remote_copy(
            src_ref=ins[a].at[2 * q + (1 - c)], dst_ref=outs[a].at[q],
            send_sem=sems[0].at[a, q], recv_sem=sems[1].at[a, q],
            device_id=(x, y, 1 - c), device_id_type=MESH) for a in range(n) for q in range(4)]

    return _Comm(list(parts), [jax.ShapeDtypeStruct((4,) + p.shape[1:], p.dtype) for p in parts],
                 [pltpu.SemaphoreType.DMA((n, 4)), pltpu.SemaphoreType.DMA((n, 4))],
                 lambda *r: [cp.start() for cp in copies(*r)], lambda *r: [cp.wait() for cp in copies(*r)])


def _chips_comm(chip_parts, rows=None):
    n = len(chip_parts)
    r0, nr = (0, None) if rows is None else rows

    def copies(ins, outs, sems):
        x, y, c = _place()
        chips = [(1 - x, y), (x, 1 - y), (1 - x, 1 - y)]
        src = lambda a, q: ins[a].at[q] if rows is None else ins[a].at[q, pl.ds(r0, nr)]
        return [pltpu.make_async_remote_copy(
            src_ref=src(a, 2 * px + py), dst_ref=outs[a].at[k],
            send_sem=sems[0].at[a, k], recv_sem=sems[1].at[a, k],
            device_id=(px, py, c), device_id_type=MESH) for a in range(n) for k, (px, py) in enumerate(chips)]

    shape = lambda p: (3,) + p.shape[1:] if rows is None else (3, nr) + p.shape[2:]
    return _Comm(list(chip_parts), [jax.ShapeDtypeStruct(shape(p), p.dtype) for p in chip_parts],
                 [pltpu.SemaphoreType.DMA((n, 3)), pltpu.SemaphoreType.DMA((n, 3))],
                 lambda *r: [cp.start() for cp in copies(*r)], lambda *r: [cp.wait() for cp in copies(*r)])


def _carry(body, *, name, grid, in_specs, out_specs, out_shape, args, comms=(), scratch_shapes=(), prefetch=()):
    n_pre, n_in, n_out, n_scr = len(prefetch), len(in_specs), len(out_specs), len(scratch_shapes)
    c_ins = [a for cm in comms for a in cm.ins]
    c_outs = [s for cm in comms for s in cm.outs]
    c_sems = [s for cm in comms for s in cm.sems]

    def wrapped(*refs):
        pre, refs = refs[:n_pre], refs[n_pre:]
        ins, refs = refs[:n_in], refs[n_in:]
        cins, refs = refs[:len(c_ins)], refs[len(c_ins):]
        outs, refs = refs[:n_out], refs[n_out:]
        couts, refs = refs[:len(c_outs)], refs[len(c_outs):]
        scr, sems = refs[:n_scr], refs[n_scr:]
        groups, i0, o0, s0 = [], 0, 0, 0
        for cm in comms:
            groups.append((cm, cins[i0:i0 + len(cm.ins)], couts[o0:o0 + len(cm.outs)], sems[s0:s0 + len(cm.sems)]))
            i0, o0, s0 = i0 + len(cm.ins), o0 + len(cm.outs), s0 + len(cm.sems)
        first = pl.program_id(0) == 0
        last = pl.program_id(0) == grid[0] - 1
        for ax in range(1, len(grid)):
            first = first & (pl.program_id(ax) == 0)
            last = last & (pl.program_id(ax) == grid[ax] - 1)
        if comms:
            @pl.when(first)
            def _():
                for cm, ci, co, cs in groups:
                    cm.start(ci, co, cs)
        body(*pre, *ins, *outs, *scr)
        if comms:
            @pl.when(last)
            def _():
                for cm, ci, co, cs in groups:
                    cm.finish(ci, co, cs)

    grid_spec = pltpu.PrefetchScalarGridSpec(
        num_scalar_prefetch=n_pre, grid=grid,
        in_specs=list(in_specs) + [ANY] * len(c_ins), out_specs=list(out_specs) + [ANY] * len(c_outs),
        scratch_shapes=list(scratch_shapes) + c_sems)
    res = pl.pallas_call(
        wrapped, name=name, grid_spec=grid_spec, out_shape=list(out_shape) + c_outs,
        compiler_params=_params(*(["arbitrary"] * len(grid))),
    )(*prefetch, *args, *c_ins)
    outs, rest, per_comm = res[:n_out], res[n_out:], []
    for cm in comms:
        per_comm.append(rest[:len(cm.outs)])
        rest = rest[len(cm.outs):]
    return outs, per_comm


def _rope_tables(pos_row, inv_freq_col, comms=()):
    t_tok = pos_row.shape[1]
    tm = min(512, t_tok)

    def body(pos_ref, invf_ref, cos_ref, sin_ref):
        ang = pos_ref[...].astype(F32) * invf_ref[...]
        row = lax.broadcasted_iota(jnp.int32, ang.shape, 0)
        cos_ref[...] = jnp.cos(ang)
        sin_ref[...] = jnp.sin(ang) * jnp.where(row < HEAD_DIM // 2, -1.0, 1.0)

    return _carry(
        body, name="rope_tables", grid=(t_tok // tm,), comms=comms,
        in_specs=[pl.BlockSpec((1, tm), lambda i: (0, i)), pl.BlockSpec((HEAD_DIM, 1), lambda i: (0, 0))],
        out_specs=[pl.BlockSpec((HEAD_DIM, tm), lambda i: (0, i))] * 2,
        out_shape=[jax.ShapeDtypeStruct((HEAD_DIM, t_tok), F32)] * 2,
        args=(pos_row, inv_freq_col))


def _proj_in(x2, w_in_t, comms=()):
    t_tok, d = x2.shape
    d_in = w_in_t.shape[0]
    tm = min(512, t_tok)

    def body(x_ref, w_ref, h_ref, xb_ref):
        xb = x_ref[...].astype(BF16)
        xb_ref[...] = xb
        h_ref[...] = _dot(w_ref[...], xb, NT)

    return _carry(
        body, name="proj_in", grid=(t_tok // tm,), comms=comms,
        in_specs=[pl.BlockSpec((tm, d), lambda i: (i, 0)), pl.BlockSpec((d_in, d), lambda i: (0, 0))],
        out_specs=[pl.BlockSpec((d_in, tm), lambda i: (0, i)), pl.BlockSpec((tm, d), lambda i: (i, 0))],
        out_shape=[jax.ShapeDtypeStruct((d_in, t_tok), F32), jax.ShapeDtypeStruct((t_tok, d), BF16)],
        args=(x2, w_in_t))


MIX_BLOCKS = 2
MIX_W = MIX_BLOCKS * BLK


def _prev_block(i):
    return jnp.maximum(MIX_BLOCKS * i - 1, 0)


def _h_specs():
    kv_row = COL_K // (2 * D_KV)
    return [
        pl.BlockSpec((D_GMLP, MIX_W), lambda i: (0, i)),
        pl.BlockSpec((D_GMLP, MIX_W), lambda i: (1, i)),
        pl.BlockSpec((D_ATTN, MIX_W), lambda i: (2, i)),
        pl.BlockSpec((2 * D_KV, MIX_W), lambda i: (kv_row, i)),
        pl.BlockSpec((2 * D_KV, BLK), lambda i: (kv_row, _prev_block(i))),
    ]


def _table_specs():
    return [
        pl.BlockSpec((HEAD_DIM, MIX_W), lambda i: (0, i)),
        pl.BlockSpec((HEAD_DIM, MIX_W), lambda i: (0, i)),
        pl.BlockSpec((HEAD_DIM, BLK), lambda i: (0, _prev_block(i))),
        pl.BlockSpec((HEAD_DIM, BLK), lambda i: (0, _prev_block(i))),
    ]


def _cols(b):
    return slice(b * BLK, (b + 1) * BLK)


LSE_ROWS = 8
LSE_SPEC = pl.BlockSpec((LSE_ROWS, D_ATTN), lambda i: (i, 0))


def _block_inputs(b, i, kvc, kvp_ref, cos, sin, cosp_ref, sinp_ref, bias_ref):
    if b == 0:
        kv_prev, cos_prev, sin_prev, bias = kvp_ref[...], cosp_ref[...], sinp_ref[...], bias_ref[jnp.minimum(i, 1)]
    else:
        kv_prev, cos_prev, sin_prev, bias = kvc[:, _cols(b - 1)], cos[:, _cols(b - 1)], sin[:, _cols(b - 1)], bias_ref[1]
    return kvc[:, _cols(b)], kv_prev, cos[:, _cols(b)], sin[:, _cols(b)], cos_prev, sin_prev, bias


def _band_bias():
    ki = lax.broadcasted_iota(jnp.int32, (2, 2 * BLK, BLK), 1)
    qi = lax.broadcasted_iota(jnp.int32, (2, 2 * BLK, BLK), 2)
    later = lax.broadcasted_iota(jnp.int32, (2, 2 * BLK, BLK), 0) > 0
    dist = qi + BLK - ki
    return jnp.where((dist >= 0) & (dist < BLK) & ((ki >= BLK) | later), 0.0, NEG_INF).astype(F32)


BIAS_SPEC = pl.BlockSpec((2, 2 * BLK, BLK), lambda i: (0, 0, 0))


def _keys_values(kvc, kvp, cosc, sinc, cosp, sinp):
    kp, kc = _rope_t(kvp[:D_KV], cosp, sinp), _rope_t(kvc[:D_KV], cosc, sinc)
    k_t = jnp.concatenate([kp, kc], axis=1).astype(BF16)
    k_n = jnp.concatenate([kp.T, kc.T], axis=0).astype(BF16)
    v_t = jnp.concatenate([kvp[D_KV:], kvc[D_KV:]], axis=1).astype(BF16)
    return k_t, k_n, v_t


def _pad_head(th, kv):
    z = jnp.zeros_like(th)
    return jnp.concatenate([th, z] if kv == 0 else [z, th], axis=0)


def _group_lanes(parts):
    return jnp.concatenate(parts, axis=1)


def _softmax_sink_t(s, sink):
    m = jnp.maximum(jnp.max(s, axis=0, keepdims=True), sink)
    e = jnp.exp(s - m)
    denom = jnp.sum(e, axis=0, keepdims=True) + jnp.exp(sink - m)
    return e * (1.0 / denom), m + jnp.log(denom)


def _causal():
    row = lax.broadcasted_iota(jnp.int32, (BLK, BLK), 0)
    col = lax.broadcasted_iota(jnp.int32, (BLK, BLK), 1)
    return row >= col


def _mask_w_once(wsp_ref, wm_scr):
    @pl.when(pl.program_id(0) == 0)
    def _():
        causal = _causal()
        for hh in range(N_HEADS):
            wm_scr[hh] = jnp.where(causal, wsp_ref[hh], 0.0).astype(BF16)


def _mixer_fwd(h_t, cos_t, sin_t, w_spatial, b_spatial, vln_g, vln_b, sinks, band_bias, comms=()):
    t_tok = h_t.shape[1]
    group = N_HEADS // N_KV_HEADS

    def body(sinks_ref, u_ref, vg_ref, q_ref, kvc_ref, kvp_ref, cos_ref, sin_ref, cosp_ref, sinp_ref,
             wsp_ref, bsp_ref, g_ref, b_ref, bias_ref, cat_ref, lse_ref, wm_scr):
        i = pl.program_id(0)
        _mask_w_once(wsp_ref, wm_scr)
        lse_ref[...] = jnp.zeros_like(lse_ref)
        ua = _gelu(u_ref[...])
        vp, _, _ = _ln_fwd_t(_gelu(vg_ref[...]), g_ref[...], b_ref[...])
        vpb = vp.astype(BF16)
        for b in range(MIX_BLOCKS):
            for hh in range(N_HEADS):
                rows = slice(hh * HEAD_DIM, (hh + 1) * HEAD_DIM)
                mixed = _dot(vpb[rows, _cols(b)], wm_scr[hh], NT) + bsp_ref[hh:hh + 1, :]
                cat_ref[rows, _cols(b)] = (ua[rows, _cols(b)] * mixed).astype(BF16)

        kvc, cos, sin = kvc_ref[...], cos_ref[...], sin_ref[...]
        qr = (_rope_t(q_ref[...], cos, sin) * SCORE_SCALE).astype(BF16)
        sinks4 = [_group_lanes([jnp.full((1, BLK), sinks_ref[hh], F32) for hh in range(kv * group, (kv + 1) * group)])
                  for kv in range(N_KV_HEADS)]
        for b in range(MIX_BLOCKS):
            kv_cur, kv_prev, cosc, sinc, cosp, sinp, bias1 = _block_inputs(b, i, kvc, kvp_ref, cos, sin, cosp_ref, sinp_ref, bias_ref)
            _, k_n, v_t = _keys_values(kv_cur, kv_prev, cosc, sinc, cosp, sinp)
            bias = _group_lanes([bias1] * group)
            for kv in range(N_KV_HEADS):
                heads = range(kv * group, (kv + 1) * group)
                qs = _group_lanes([qr[hh * HEAD_DIM:(hh + 1) * HEAD_DIM, _cols(b)] for hh in heads])
                p, lse = _softmax_sink_t(_dot(k_n, _pad_head(qs, kv)) + bias, sinks4[kv])
                lse_ref[b * N_KV_HEADS + kv:b * N_KV_HEADS + kv + 1, :] = lse
                o = _dot(v_t[kv * HEAD_DIM:(kv + 1) * HEAD_DIM], p.astype(BF16)).astype(BF16)
                for j, hh in enumerate(heads):
                    cat_ref[D_GMLP + hh * HEAD_DIM:D_GMLP + (hh + 1) * HEAD_DIM, _cols(b)] = o[:, j * BLK:(j + 1) * BLK]

    full = lambda shape: pl.BlockSpec(shape, lambda i: (0,) * len(shape))
    return _carry(
        body, name="mixer_fwd", grid=(t_tok // MIX_W,), comms=comms,
        in_specs=[pl.BlockSpec(memory_space=pltpu.SMEM)] + _h_specs() + _table_specs() + [
            full((N_HEADS, BLK, BLK)), full((N_HEADS, BLK)), full((D_GMLP, 1)), full((D_GMLP, 1)), BIAS_SPEC],
        out_specs=[pl.BlockSpec((D_GMLP + D_ATTN, MIX_W), lambda i: (0, i)), LSE_SPEC],
        out_shape=[jax.ShapeDtypeStruct((D_GMLP + D_ATTN, t_tok), BF16),
                   jax.ShapeDtypeStruct((t_tok // MIX_W * LSE_ROWS, D_ATTN), F32)],
        scratch_shapes=[pltpu.VMEM((N_HEADS, BLK, BLK), BF16)],
        args=(sinks, h_t, h_t, h_t, h_t, h_t, cos_t, sin_t, cos_t, sin_t, w_spatial, b_spatial, vln_g, vln_b, band_bias))


def _proj_out(cat_t, x2, w_out_b, ln1_g, ln1_b, comms=()):
    t_tok, d = x2.shape
    tm = min(512, t_tok)

    def body(cat_ref, x_ref, w_ref, g_ref, b_ref, xhat_ref, rstd_ref, x1b_ref):
        x1, xhat, rstd = _ln_fwd(ALPHA * x_ref[...] + _dot(cat_ref[...], w_ref[...], TN), g_ref[...], b_ref[...])
        xhat_ref[...] = xhat
        rstd_ref[...] = rstd
        x1b_ref[...] = x1.astype(BF16)

    tok = lambda w: pl.BlockSpec((tm, w), lambda i: (i, 0))
    vec = pl.BlockSpec((1, d), lambda i: (0, 0))
    return _carry(
        body, name="proj_out", grid=(t_tok // tm,), comms=comms,
        in_specs=[pl.BlockSpec((cat_t.shape[0], tm), lambda i: (0, i)), tok(d), pl.BlockSpec(w_out_b.shape, lambda i: (0, 0)), vec, vec],
        out_specs=[tok(d), tok(1), tok(d)],
        out_shape=[jax.ShapeDtypeStruct((t_tok, d), F32), jax.ShapeDtypeStruct((t_tok, 1), F32), jax.ShapeDtypeStruct((t_tok, d), BF16)],
        args=(cat_t, x2, w_out_b, ln1_g, ln1_b))


def _ffn_fwd_bwd(xhat1, rstd1, x1b, target, w1_parts, w2_parts, ln1_g, ln1_b, ln2_g, ln2_b):
    t_tok, d = xhat1.shape
    n_part = len(w1_parts)
    n_chunk, _, fp = w1_parts[0].shape
    f = n_chunk * n_part * fp
    tm = min(FFN_ROWS, t_tok)

    def body(xhat1_ref, rstd1_ref, x1b_ref, tgt_ref, *refs):
        w1_hbm, w2_hbm = refs[:n_part], refs[n_part:2 * n_part]
        (g1_ref, b1_ref, g2_ref, b2_ref, act_ref, dpre_ref, dz2b_ref, dz1_ref, stats_ref,
         r_scr, w1_ref, w2_ref, w_sems) = refs[2 * n_part:]

        @pl.when(pl.program_id(0) == 0)
        def _():
            stats_ref[...] = jnp.zeros_like(stats_ref)
            loads = []
            for j in range(n_chunk):
                for p in range(n_part):
                    units = pl.ds((j * n_part + p) * fp, fp)
                    loads.append(pltpu.make_async_copy(w1_hbm[p].at[j], w1_ref.at[:, units], w_sems.at[0, p, j]))
                    loads.append(pltpu.make_async_copy(w2_hbm[p].at[j], w2_ref.at[units, :], w_sems.at[1, p, j]))
            for cp in loads:
                cp.start()
            for cp in loads:
                cp.wait()

        g1, g2 = g1_ref[...], g2_ref[...]
        xhat1 = xhat1_ref[...]
        r_scr[...] = jnp.maximum(_dot(x1b_ref[...], w1_ref[...]), 0.0)
        r = r_scr[...]
        act = (r * r).astype(BF16)
        act_ref[...] = act
        ff = _dot(act, w2_ref[...])
        y, xhat2, rstd2 = _ln_fwd(ALPHA * (xhat1 * g1 + b1_ref[...]) + ff, g2, b2_ref[...])
        diff = y - tgt_ref[...]
        loss = 0.5 * jnp.sum(jnp.sum(diff * diff, axis=-1, keepdims=True) / d, axis=0, keepdims=True)
        dy = diff / d
        dz2 = _ln_bwd(dy, xhat2, rstd2, g2)
        dz2b = dz2.astype(BF16)
        dz2b_ref[...] = dz2b
        dpre = (_dot(dz2b, w2_ref[...], NT) * (2.0 * r_scr[...])).astype(BF16)
        dpre_ref[...] = dpre
        dx1 = ALPHA * dz2 + _dot(dpre, w1_ref[...], NT)
        dz1_ref[...] = _ln_bwd(dx1, xhat1, rstd1_ref[...], g1)
        stats_ref[0:1, :] += jnp.sum(dx1 * xhat1, axis=0, keepdims=True)
        stats_ref[1:2, :] += jnp.sum(dx1, axis=0, keepdims=True)
        stats_ref[2:3, :] += jnp.sum(dy * xhat2, axis=0, keepdims=True)
        stats_ref[3:4, :] += jnp.sum(dy, axis=0, keepdims=True)
        stats_ref[4:5, :] += jnp.broadcast_to(loss, (1, d))

    tok = lambda w: pl.BlockSpec((tm, w), lambda i: (i, 0))
    vec = pl.BlockSpec((1, d), lambda i: (0, 0))
    return _carry(
        body, name="ffn_fwd_bwd", grid=(t_tok // tm,),
        in_specs=[tok(d), tok(1), tok(d), tok(d)] + [ANY] * (2 * n_part) + [vec, vec, vec, vec],
        out_specs=[tok(f), tok(f), tok(d), tok(d), pl.BlockSpec((8, d), lambda i: (0, 0))],
        out_shape=[jax.ShapeDtypeStruct((t_tok, f), BF16), jax.ShapeDtypeStruct((t_tok, f), BF16),
                   jax.ShapeDtypeStruct((t_tok, d), BF16), jax.ShapeDtypeStruct((t_tok, d), F32), jax.ShapeDtypeStruct((8, d), F32)],
        scratch_shapes=[pltpu.VMEM((tm, f), F32), pltpu.VMEM((d, f), BF16), pltpu.VMEM((f, d), BF16),
                        pltpu.SemaphoreType.DMA((2, n_part, n_chunk))],
        args=(xhat1, rstd1, x1b, target, *w1_parts, *w2_parts, ln1_g, ln1_b, ln2_g, ln2_b))[0]


def _ffn_wgrad(name, lhs, rhs, chunk_lhs, core_chip, comms=()):
    t_tok = lhs.shape[0]
    half = N_DEV // 2
    fc = (lhs if chunk_lhs else rhs).shape[1] // N_DEV
    chunk = (fc, rhs.shape[1]) if chunk_lhs else (lhs.shape[1], fc)

    def shard(s, cc):
        return 2 * (s % half) + jnp.where(s < half, 1 - cc[0], cc[0])

    def body(cc_ref, lhs_ref, rhs_ref, wire_ref, own_ref, recv_ref, send_buf, got, send_sems, recv_sems, got_sem):
        s = pl.program_id(0)
        x, y, c = _place()
        def send(q):
            return pltpu.make_async_remote_copy(
                src_ref=send_buf.at[q % 2], dst_ref=recv_ref.at[q], send_sem=send_sems.at[q], recv_sem=recv_sems.at[q],
                device_id=(x, y, 1 - c), device_id_type=MESH)

        def load(q):
            return pltpu.make_async_copy(recv_ref.at[q], got, got_sem.at[0])

        @pl.when(s >= half)
        def _():
            send(s - half).wait_recv()
            load(s - half).start()

        g = _dot(lhs_ref[...], rhs_ref[...], TN)

        for q in range(half):
            @pl.when(s == q)
            def _(q=q):
                if q >= 2:
                    send(q - 2).wait_send()
                send_buf[q % 2] = g
                send(q).start()

            @pl.when(s == half + q)
            def _(q=q):
                load(q).wait()
                total = g + got[...]
                wire_ref[...] = total.astype(BF16)

                @pl.when(cc_ref[1] == q)
                def _():
                    own_ref[...] = total

        @pl.when(s == N_DEV - 1)
        def _():
            for q in range(half - 2, half):
                send(q).wait_send()

    resident = lambda a: pl.BlockSpec(a.shape, lambda s, cc: (0, 0), pipeline_mode=pl.Buffered(1))
    chunked = pl.BlockSpec((t_tok, fc), lambda s, cc: (0, shard(s, cc)))
    (wire, own, _), per_comm = _carry(
        body, name=name, grid=(N_DEV,), comms=comms, prefetch=(core_chip,),
        in_specs=[chunked, resident(rhs)] if chunk_lhs else [resident(lhs), chunked],
        out_specs=[pl.BlockSpec((None,) + chunk, lambda s, cc: (jnp.maximum(s - half, 0), 0, 0)),
                   pl.BlockSpec(chunk, lambda s, cc: (0, 0)), ANY],
        out_shape=[jax.ShapeDtypeStruct((half,) + chunk, BF16), jax.ShapeDtypeStruct(chunk, F32),
                   jax.ShapeDtypeStruct((half,) + chunk, F32)],
        scratch_shapes=[pltpu.VMEM((2,) + chunk, F32), pltpu.VMEM(chunk, F32), pltpu.SemaphoreType.DMA((half,)),
                        pltpu.SemaphoreType.DMA((half,)), pltpu.SemaphoreType.DMA((1,))],
        args=(lhs, rhs))
    return wire, own, per_comm


def _proj_out_bwd(dz1, cat_t, w_out_b, comms=()):
    t_tok, d = dz1.shape
    d_mix = cat_t.shape[0]
    tm = min(512, t_tok)

    def body(dz1_ref, cat_ref, w_ref, dcat_ref, gw_ref):
        @pl.when(pl.program_id(0) == 0)
        def _():
            gw_ref[...] = jnp.zeros_like(gw_ref)

        dzb = dz1_ref[...].astype(BF16)
        dcat_ref[...] = _dot(w_ref[...], dzb, NT)
        gw_ref[...] += _dot(cat_ref[...], dzb)

    return _carry(
        body, name="proj_out_bwd", grid=(t_tok // tm,), comms=comms,
        in_specs=[pl.BlockSpec((tm, d), lambda i: (i, 0)), pl.BlockSpec((d_mix, tm), lambda i: (0, i)),
                  pl.BlockSpec((d_mix, d), lambda i: (0, 0))],
        out_specs=[pl.BlockSpec((d_mix, tm), lambda i: (0, i)), pl.BlockSpec((d_mix, d), lambda i: (0, 0))],
        out_shape=[jax.ShapeDtypeStruct((d_mix, t_tok), F32), jax.ShapeDtypeStruct((d_mix, d), F32)],
        args=(dz1, cat_t, w_out_b))


def _mixer_bwd(dcat_t, h_t, cos_t, sin_t, w_spatial, b_spatial, vln_g, vln_b, sinks, band_bias, lse, comms=()):
    t_tok = h_t.shape[1]
    nb, n_step = t_tok // BLK, t_tok // MIX_W
    group = N_HEADS // N_KV_HEADS

    def body(sinks_ref, dcat_ref, u_ref, vg_ref, q_ref, kvc_ref, kvp_ref, cos_ref, sin_ref, cosp_ref, sinp_ref,
             wsp_ref, bsp_ref, g_ref, b_ref, bias_ref, lse_ref, dh_ref, dkvc_ref, dkvp_ref, gwsb_ref, gbsp_ref, gvln_ref, gsink_ref,
             dg_acc, db_acc, wm_scr, gws_ref):
        i = pl.program_id(0)

        @pl.when(i == 0)
        def _():
            gws_ref[...] = jnp.zeros_like(gws_ref)
            gbsp_ref[...] = jnp.zeros_like(gbsp_ref)
            gsink_ref[...] = jnp.zeros_like(gsink_ref)
            dg_acc[...] = jnp.zeros_like(dg_acc)
            db_acc[...] = jnp.zeros_like(db_acc)

        _mask_w_once(wsp_ref, wm_scr)

        g = g_ref[...]
        ua, ua_grad = _gelu_and_grad(u_ref[...])
        vv, vv_grad = _gelu_and_grad(vg_ref[...])
        vp, vhat, rstd = _ln_fwd_t(vv, g, b_ref[...])
        vpb = vp.astype(BF16)
        da = dcat_ref[0:D_GMLP, :]
        dmixed = da * ua
        dvp_blocks = []
        for b in range(MIX_BLOCKS):
            dvp_parts = []
            for hh in range(N_HEADS):
                rows = slice(hh * HEAD_DIM, (hh + 1) * HEAD_DIM)
                vpb_h = vpb[rows, _cols(b)]
                mixed = _dot(vpb_h, wm_scr[hh], NT) + bsp_ref[hh:hh + 1, :]
                dh_ref[COL_U + hh * HEAD_DIM:COL_U + (hh + 1) * HEAD_DIM, _cols(b)] = (
                    da[rows, _cols(b)] * mixed * ua_grad[rows, _cols(b)]).astype(BF16)
                dm = dmixed[rows, _cols(b)]
                dmb = dm.astype(BF16)
                gbsp_ref[hh:hh + 1, :] += jnp.sum(dm, axis=0, keepdims=True)
                gws_ref[hh] += _dot(dmb, vpb_h, TN)
                dvp_parts.append(_dot(dmb, wm_scr[hh]))
            dvp_blocks.append(jnp.concatenate(dvp_parts, axis=0))
        dvp = jnp.concatenate(dvp_blocks, axis=1)
        dgv, dbv = dvp * vhat, dvp
        for b in range(MIX_BLOCKS):
            dg_acc[...] += dgv[:, _cols(b)]
            db_acc[...] += dbv[:, _cols(b)]
        dh_ref[COL_V:COL_V + D_GMLP, :] = (_ln_bwd_t(dvp, vhat, rstd, g) * vv_grad).astype(BF16)

        kvc, cos, sin = kvc_ref[...], cos_ref[...], sin_ref[...]
        qr = (_rope_t(q_ref[...], cos, sin) * SCORE_SCALE).astype(BF16)
        sinks4 = [_group_lanes([jnp.full((1, BLK), sinks_ref[hh], F32) for hh in range(kv * group, (kv + 1) * group)])
                  for kv in range(N_KV_HEADS)]
        dq_blocks, dkv_cur, dkv_prev = [], [], []
        for b in range(MIX_BLOCKS):
            kv_cur, kv_prev, cosc, sinc, cosp, sinp, bias1 = _block_inputs(b, i, kvc, kvp_ref, cos, sin, cosp_ref, sinp_ref, bias_ref)
            k_t, k_n, v_t = _keys_values(kv_cur, kv_prev, cosc, sinc, cosp, sinp)
            v_n = jnp.concatenate([kv_prev[D_KV:].T, kv_cur[D_KV:].T], axis=0).astype(BF16)
            bias = _group_lanes([bias1] * group)
            dk, dv, dq_parts = [], [], []
            for kv in range(N_KV_HEADS):
                heads = range(kv * group, (kv + 1) * group)
                kv_rows = slice(kv * HEAD_DIM, (kv + 1) * HEAD_DIM)
                qs = _group_lanes([qr[hh * HEAD_DIM:(hh + 1) * HEAD_DIM, _cols(b)] for hh in heads])
                dos = _group_lanes([dcat_ref[D_GMLP + hh * HEAD_DIM:D_GMLP + (hh + 1) * HEAD_DIM, _cols(b)]
                                    for hh in heads]).astype(BF16)
                lse_g = lse_ref[b * N_KV_HEADS + kv:b * N_KV_HEADS + kv + 1, :]
                p = jnp.exp(_dot(k_n, _pad_head(qs, kv)) + bias - lse_g)
                p_sink = jnp.exp(sinks4[kv] - lse_g)
                dp = _dot(v_n, _pad_head(dos, kv))
                delta = jnp.sum(p * dp, axis=0, keepdims=True)
                ds = (p * (dp - delta)).astype(BF16)
                dsink = p_sink * delta
                dq = _dot(k_t[kv_rows], ds) * SCORE_SCALE
                for j, hh in enumerate(heads):
                    gsink_ref[hh:hh + 1, :] -= dsink[:, j * BLK:(j + 1) * BLK]
                    dq_parts.append(dq[:, j * BLK:(j + 1) * BLK])
                dk.append(_dot(qs, ds, NT))
                dv.append(_dot(dos, p.astype(BF16), NT))
            dq_blocks.append(jnp.concatenate(dq_parts, axis=0))
            dk_all, dv_all = jnp.concatenate(dk, axis=0), jnp.concatenate(dv, axis=0)
            dkv_cur.append(jnp.concatenate([_rope_t(dk_all[:, BLK:], cosc, sinc, bwd=True), dv_all[:, BLK:]], axis=0))
            dkv_prev.append(jnp.concatenate([_rope_t(dk_all[:, :BLK], cosp, sinp, bwd=True), dv_all[:, :BLK]], axis=0))
        dh_ref[COL_Q:COL_Q + D_ATTN, :] = _rope_t(jnp.concatenate(dq_blocks, axis=1), cos, sin, bwd=True).astype(BF16)
        for b in range(MIX_BLOCKS):
            dkvc_ref[:, _cols(b)] = dkv_cur[b] + dkv_prev[b + 1] if b + 1 < MIX_BLOCKS else dkv_cur[b]
        dkvp_ref[...] = dkv_prev[0]

        @pl.when(i == n_step - 1)
        def _():
            causal = _causal()
            for hh in range(N_HEADS):
                gwsb_ref[hh] = jnp.where(causal, gws_ref[hh], 0.0).astype(BF16)
            gvln_ref[...] = jnp.zeros_like(gvln_ref)
            gvln_ref[0:1, :] = jnp.sum(dg_acc[...].T, axis=0, keepdims=True)
            gvln_ref[1:2, :] = jnp.sum(db_acc[...].T, axis=0, keepdims=True)

    full = lambda shape: pl.BlockSpec(shape, lambda i: (0,) * len(shape))
    return _carry(
        body, name="mixer_bwd", grid=(n_step,), comms=comms,
        in_specs=[pl.BlockSpec(memory_space=pltpu.SMEM), pl.BlockSpec((D_GMLP + D_ATTN, MIX_W), lambda i: (0, i))]
        + _h_specs() + _table_specs()
        + [full((N_HEADS, BLK, BLK)), full((N_HEADS, BLK)), full((D_GMLP, 1)), full((D_GMLP, 1)), BIAS_SPEC, LSE_SPEC],
        out_specs=[pl.BlockSpec((COL_K, MIX_W), lambda i: (0, i)), pl.BlockSpec((2 * D_KV, MIX_W), lambda i: (0, i)),
                   pl.BlockSpec((2 * D_KV, BLK), lambda i: (0, (i + n_step - 1) % n_step)),
                   full((N_HEADS, BLK, BLK)), full((N_HEADS, BLK)), full((8, D_GMLP)), full((N_HEADS, LANES))],
        out_shape=[jax.ShapeDtypeStruct((COL_K, t_tok), BF16), jax.ShapeDtypeStruct((2 * D_KV, t_tok), F32),
                   jax.ShapeDtypeStruct((2 * D_KV, n_step * BLK), F32),
                   jax.ShapeDtypeStruct((N_HEADS, BLK, BLK), BF16), jax.ShapeDtypeStruct((N_HEADS, BLK), F32),
                   jax.ShapeDtypeStruct((8, D_GMLP), F32), jax.ShapeDtypeStruct((N_HEADS, LANES), F32)],
        scratch_shapes=[pltpu.VMEM((D_GMLP, BLK), F32), pltpu.VMEM((D_GMLP, BLK), F32), pltpu.VMEM((N_HEADS, BLK, BLK), BF16),
                        pltpu.VMEM((N_HEADS, BLK, BLK), F32)],
        args=(sinks, dcat_t, h_t, h_t, h_t, h_t, h_t, cos_t, sin_t, cos_t, sin_t, w_spatial, b_spatial, vln_g, vln_b, band_bias, lse))


def _proj_in_wgrad(dh_b, dkvc_t, dkvp_t, xb, comms=()):
    t_tok, d = xb.shape
    d_main, d_kv = dh_b.shape[0], dkvc_t.shape[0]
    tm = min(1024, t_tok)

    def body(dh_ref, dkvc_ref, dkvp_ref, xb_ref, dkvb_ref, gw_ref):
        @pl.when(pl.program_id(0) == 0)
        def _():
            gw_ref[...] = jnp.zeros_like(gw_ref)

        for s in range(tm // MIX_W):
            last = slice((s + 1) * MIX_W - BLK, (s + 1) * MIX_W)
            dkvb_ref[:, s * MIX_W:(s + 1) * MIX_W - BLK] = dkvc_ref[:, s * MIX_W:(s + 1) * MIX_W - BLK].astype(BF16)
            dkvb_ref[:, last] = (dkvc_ref[:, last] + dkvp_ref[:, _cols(s)]).astype(BF16)
        gw_ref[0:d_main, :] += _dot(dh_ref[...], xb_ref[...])
        gw_ref[d_main:, :] += _dot(dkvb_ref[...], xb_ref[...])

    tok = lambda rows: pl.BlockSpec((rows, tm), lambda i: (0, i))
    return _carry(
        body, name="proj_in_wgrad", grid=(t_tok // tm,), comms=comms,
        in_specs=[tok(d_main), tok(d_kv), pl.BlockSpec((d_kv, tm // MIX_BLOCKS), lambda i: (0, i)),
                  pl.BlockSpec((tm, d), lambda i: (i, 0))],
        out_specs=[tok(d_kv), pl.BlockSpec((d_main + d_kv, d), lambda i: (0, 0))],
        out_shape=[jax.ShapeDtypeStruct((d_kv, t_tok), BF16), jax.ShapeDtypeStruct((d_main + d_kv, d), F32)],
        args=(dh_b, dkvc_t, dkvp_t, xb))


def _proj_in_dgrad(dh_b, dkv_b, dz1, w_in_t, comms=()):
    t_tok, d = dz1.shape
    d_main, d_kv = dh_b.shape[0], dkv_b.shape[0]
    tm = min(512, t_tok)

    def body(dh_ref, dkv_ref, dz1_ref, w_ref, dx_ref):
        dx_ref[...] = (ALPHA * dz1_ref[...] + _dot(dh_ref[...], w_ref[0:d_main, :], TN)
                       + _dot(dkv_ref[...], w_ref[d_main:, :], TN))

    return _carry(
        body, name="proj_in_dgrad", grid=(t_tok // tm,), comms=comms,
        in_specs=[pl.BlockSpec((d_main, tm), lambda i: (0, i)), pl.BlockSpec((d_kv, tm), lambda i: (0, i)),
                  pl.BlockSpec((tm, d), lambda i: (i, 0)), pl.BlockSpec((d_main + d_kv, d), lambda i: (0, 0))],
        out_specs=[pl.BlockSpec((tm, d), lambda i: (i, 0))],
        out_shape=[jax.ShapeDtypeStruct((t_tok, d), F32)],
        args=(dh_b, dkv_b, dz1, w_in_t))


def _adamw(w, g, m, v):
    m = ADAM_B1 * m + (1.0 - ADAM_B1) * g
    v = ADAM_B2 * v + (1.0 - ADAM_B2) * (g * g)
    m_hat = m / (1.0 - ADAM_B1 ** ADAM_STEP)
    v_hat = v / (1.0 - ADAM_B2 ** ADAM_STEP)
    delta = -ADAM_LR * (m_hat / (jnp.sqrt(v_hat) + ADAM_EPS) + ADAM_WD * w)
    return delta, m, v


ADAMW_STEPS = 4
BF16_ROWS = 16


def _adamw_shards(name, items, comms=(), rider=None):
    n_in, n_out = 5 * len(items), 4 * len(items)
    n_rin = len(rider["args"]) if rider else 0

    def body(*refs):
        ins, rins, outs, routs = refs[:n_in], refs[n_in:n_in + n_rin], refs[n_in + n_rin:n_in + n_rin + n_out], refs[n_in + n_rin + n_out:]
        for i in range(len(items)):
            own_ref, recv_ref, w_ref, m_ref, v_ref = ins[5 * i:5 * i + 5]
            g = ((own_ref[...] + recv_ref[0].astype(F32)) + recv_ref[1].astype(F32)) + recv_ref[2].astype(F32)
            for o_ref, val in zip(outs[4 * i:4 * i + 4], (g,) + _adamw(w_ref[...], g, m_ref[...], v_ref[...])):
                o_ref[...] = val
        if rider:
            pl.when(pl.program_id(0) == 0)(lambda: rider["body"](rins, routs))

    in_specs, out_specs, out_shape, args = [], [], [], []
    for own, recv, w, m, v in items:
        r, c = own.shape
        tiles = ADAMW_STEPS
        while (r // tiles) % BF16_ROWS:
            tiles //= 2
        blk = pl.BlockSpec((r // tiles, c), lambda s, k=ADAMW_STEPS // tiles: (s // k, 0))
        in_specs += [blk, pl.BlockSpec((3, r // tiles, c), lambda s, k=ADAMW_STEPS // tiles: (0, s // k, 0)), blk, blk, blk]
        out_specs += [blk] * 4
        out_shape += [jax.ShapeDtypeStruct((r, c), F32)] * 4
        args += [own, recv, w, m, v]
    if rider:
        in_specs, out_specs = in_specs + rider["in_specs"], out_specs + rider["out_specs"]
        out_shape, args = out_shape + rider["out_shape"], args + rider["args"]
    res, per_comm = _carry(body, name=name, grid=(ADAMW_STEPS,), comms=comms, in_specs=in_specs, out_specs=out_specs,
                           out_shape=out_shape, args=args)
    return [res[4 * i:4 * i + 4] for i in range(len(items))], res[n_out:], per_comm


VEC_VLN, VEC_LN1G, VEC_LN1B, VEC_LN2G, VEC_LN2B, VEC_SINK, VEC_LOSS, VEC_BSP, VEC_ROWS = 0, 1, 2, 3, 4, 5, 6, 8, 16


def _adamw_small(parts_w, parts_vec, params):
    n = parts_w.shape[0]
    flat = [a for p in params for a in p]
    shapes = [p[0].shape for p in params]

    def grads(gw, gv):
        return [gw, gv[VEC_VLN:VEC_VLN + 1, 0:D_GMLP], gv[VEC_VLN:VEC_VLN + 1, D_GMLP:2 * D_GMLP],
                gv[VEC_BSP:VEC_BSP + N_HEADS, 0:BLK], gv[VEC_LN1G:VEC_LN1G + 1], gv[VEC_LN1B:VEC_LN1B + 1],
                gv[VEC_LN2G:VEC_LN2G + 1], gv[VEC_LN2B:VEC_LN2B + 1], gv[VEC_SINK:VEC_SINK + 1, 0:N_HEADS]]

    def body(ins, outs):
        (pw_ref, pv_ref), ins = ins[:2], ins[2:]
        gw, gv = pw_ref[0].astype(F32), pv_ref[0]
        for k in range(1, n):
            gw, gv = gw + pw_ref[k].astype(F32), gv + pv_ref[k]
        for i, g in enumerate(grads(gw, gv)):
            w_ref, m_ref, v_ref = ins[3 * i:3 * i + 3]
            delta, m_new, v_new = _adamw(w_ref[...], g, m_ref[...], v_ref[...])
            for o_ref, val in zip(outs[4 * i:4 * i + 4], (g, delta, m_new, v_new)):
                o_ref[...] = val
        outs[-1][...] = gv[VEC_LOSS:VEC_LOSS + 1, 0:LANES]

    whole = lambda shape, **kw: pl.BlockSpec(shape, lambda i: (0,) * len(shape), **kw)
    once = dict(pipeline_mode=pl.Buffered(1))
    return dict(
        body=body, args=[parts_w, parts_vec, *flat],
        in_specs=[whole(parts_w.shape, **once), whole(parts_vec.shape, **once)] + [whole(a.shape, **once) for a in flat],
        out_specs=[whole(s) for s in shapes for _ in range(4)] + [whole((1, LANES))],
        out_shape=[jax.ShapeDtypeStruct(s, F32) for s in shapes for _ in range(4)] + [jax.ShapeDtypeStruct((1, LANES), F32)])


def _pair_sum(name, parts, recv, core_chip, comms=()):
    _, r, c = parts.shape
    tr = r if r <= 512 else 512

    def body(cc_ref, a_ref, b_ref, wire_ref, own_ref):
        s = a_ref[...] + b_ref[...]
        wire_ref[...] = s.astype(BF16)

        @pl.when(pl.program_id(1) == cc_ref[1])
        def _():
            own_ref[...] = s

    return _carry(
        body, name=name, grid=(r // tr, 4), prefetch=(core_chip,), comms=comms,
        in_specs=[pl.BlockSpec((None, tr, c), lambda i, q, cc: (2 * q + cc[0], i, 0)),
                  pl.BlockSpec((None, tr, c), lambda i, q, cc: (q, i, 0))],
        out_specs=[pl.BlockSpec((None, tr, c), lambda i, q, cc: (q, i, 0)), pl.BlockSpec((tr, c), lambda i, q, cc: (i, 0))],
        out_shape=[jax.ShapeDtypeStruct((4, r, c), BF16), jax.ShapeDtypeStruct((r, c), F32)],
        args=(parts, recv))


def kernel(x, positions, w_in, v_ln_g, v_ln_b, w_spatial, b_spatial, sinks, w_out, ln1_g, ln1_b, w_ff1, w_ff2, ln2_g, ln2_b, loss_target, m_w_in, m_v_ln_g, m_v_ln_b, m_w_spatial, m_b_spatial, m_sinks, m_w_out, m_ln1_g, m_ln1_b, m_w_ff1, m_w_ff2, m_ln2_g, m_ln2_b, v_w_in, v_v_ln_g, v_v_ln_b, v_w_spatial, v_b_spatial, v_sinks, v_w_out, v_ln1_g, v_ln1_b, v_w_ff1, v_w_ff2, v_ln2_g, v_ln2_b):
    _, t_tok, d = x.shape
    xi, yi, ci = _place()
    core_chip = jnp.stack([ci, 2 * xi + yi]).astype(jnp.int32)
    x2 = x.reshape(t_tok, d)
    target = loss_target.reshape(t_tok, d)
    inv_freq = ROPE_THETA ** (-jnp.arange(0, HEAD_DIM, 2, dtype=F32) / HEAD_DIM)
    wsp, bsp, sink_vec = w_spatial[0], b_spatial[0], sinks[0]
    vg_col, vb_col = v_ln_g.reshape(D_GMLP, 1), v_ln_b.reshape(D_GMLP, 1)
    big = {"in": w_in[0], "out": w_out[0], "ff1": w_ff1[0], "ff2": w_ff2[0]}
    half1, half2 = big["ff1"].shape[1] // 2, big["ff2"].shape[0] // 2
    w1_mine = [big["ff1"][:, :half1].astype(BF16), big["ff1"][:, half1:].astype(BF16)]
    w2_mine = [big["ff2"][:half2].astype(BF16), big["ff2"][half2:].astype(BF16)]

    (cos_t, sin_t), ((g_in,),) = _rope_tables(
        positions, jnp.tile(inv_freq, 2).reshape(HEAD_DIM, 1), comms=[_gather_comm([big["in"].T.astype(BF16)])])
    w_in_t = g_in.reshape(D_IN, d)
    (h_t, xb), ((g_out, w1_a),) = _proj_in(x2, w_in_t, comms=[_gather_comm([big["out"].astype(BF16), w1_mine[0]])])
    w_out_b = g_out.reshape(-1, d)
    band_bias = _band_bias()
    (cat_t, lse), ((w1_b, w2_a),) = _mixer_fwd(h_t, cos_t, sin_t, wsp, bsp, vg_col, vb_col, sink_vec, band_bias,
                                                comms=[_gather_comm([w1_mine[1], w2_mine[0]])])
    (xhat1, rstd1, x1b), ((w2_b,),) = _proj_out(cat_t, x2, w_out_b, ln1_g, ln1_b, comms=[_gather_comm([w2_mine[1]])])
    act_b, dpre_b, dz2b, dz1, stats = _ffn_fwd_bwd(xhat1, rstd1, x1b, target, [w1_a, w1_b], [w2_a, w2_b], ln1_g, ln1_b, ln2_g, ln2_b)

    (dcat_t, gw_out), _ = _proj_out_bwd(dz1, cat_t, w_out_b)
    p_out = gw_out.reshape(N_DEV, -1, d)
    wire_ff1, own_ff1, ((s_out,),) = _ffn_wgrad("ffn_wgrad1", x1b, dpre_b, False, core_chip, comms=[_sibling_comm([p_out])])
    (wire_out, own_out), _ = _pair_sum("pair_sum_out", p_out, s_out, core_chip)
    wire_ff2, own_ff2, ((r_ff1,),) = _ffn_wgrad("ffn_wgrad2", act_b, dz2b, True, core_chip, comms=[_chips_comm([wire_ff1])])
    (dh_b, dkvc_t, dkvp_t, g_wsp, g_bsp, g_vln, g_sink), ((r_ff2, r_out),) = _mixer_bwd(
        dcat_t, h_t, cos_t, sin_t, wsp, bsp, vg_col, vb_col, sink_vec, band_bias, lse,
        comms=[_chips_comm([wire_ff2, wire_out])])
    sink_row = jnp.pad(g_sink.sum(axis=1).reshape(1, N_HEADS), ((0, 0), (0, d - N_HEADS)))
    small_vec = jnp.concatenate([g_vln[0:2].reshape(1, d), stats[0:4], sink_row, stats[4:5], jnp.zeros((1, d), F32),
                                 jnp.pad(g_bsp, ((0, 0), (0, d - BLK)))], axis=0)
    (dkv_b, gw_in_t), ((parts_w, parts_vec),) = _proj_in_wgrad(
        dh_b, dkvc_t, dkvp_t, xb, comms=[_gather_comm([g_wsp.reshape(-1, BLK), small_vec])])
    p_in = gw_in_t.reshape(N_DEV, -1, d)

    (out_out,), _, ((s_in,),) = _adamw_shards("adamw_out", [(own_out, r_out, big["out"], m_w_out[0], v_w_out[0])],
                                              comms=[_sibling_comm([p_in])])
    (wire_in, own_in), _ = _pair_sum("pair_sum_in", p_in, s_in, core_chip)
    (grad_x,), ((r_in,),) = _proj_in_dgrad(dh_b, dkv_b, dz1, w_in_t, comms=[_chips_comm([wire_in])])
    small = [(w_spatial, m_w_spatial, v_w_spatial), (v_ln_g, m_v_ln_g, v_v_ln_g), (v_ln_b, m_v_ln_b, v_v_ln_b),
             (b_spatial, m_b_spatial, v_b_spatial), (ln1_g, m_ln1_g, v_ln1_g), (ln1_b, m_ln1_b, v_ln1_b),
             (ln2_g, m_ln2_g, v_ln2_g), (ln2_b, m_ln2_b, v_ln2_b), (sinks, m_sinks, v_sinks)]
    views = [(-1, BLK), None, None, (N_HEADS, BLK)] + [None] * 5
    small_update = _adamw_small(parts_w, parts_vec, [
        tuple(a if vw is None else a.reshape(vw) for a in p) for p, vw in zip(small, views)])
    (ff1_out, ff2_out, in_out_t), small_res, _ = _adamw_shards("adamw_all", [
        (own_ff1, r_ff1, big["ff1"], m_w_ff1[0], v_w_ff1[0]), (own_ff2, r_ff2, big["ff2"], m_w_ff2[0], v_w_ff2[0]),
        (own_in, r_in, big["in"].T, m_w_in[0].T, v_w_in[0].T)], rider=small_update)
    in_out = [o.T for o in in_out_t]
    small_out = [[o.reshape(p[0].shape) for o in small_res[4 * i:4 * i + 4]] for i, p in enumerate(small)]
    loss = small_res[-1][0, 0]

    big_out = {0: in_out, 6: out_out, 9: ff1_out, 10: ff2_out}
    small_slot = {3: 0, 1: 1, 2: 2, 4: 3, 7: 4, 8: 5, 11: 6, 12: 7, 5: 8}
    outs = [loss, grad_x.reshape(x.shape)]
    for kind in range(4):
        for wi in range(13):
            outs.append(big_out[wi][kind][None] if wi in big_out else small_out[small_slot[wi]][kind])
    return tuple(outs)
```

```python
import math

import jax
import jax.numpy as jnp
from jax import lax
from jax.experimental import pallas as pl
from jax.experimental.pallas import tpu as pltpu

F32 = jnp.float32
BF16 = jnp.bfloat16
MESH = pl.DeviceIdType.MESH

HEAD_DIM = 64
N_HEADS = 8
N_KV_HEADS = 2
BLK = 128
D_GMLP = N_HEADS * HEAD_DIM
D_ATTN = N_HEADS * HEAD_DIM
D_KV = N_KV_HEADS * HEAD_DIM
D_IN = 2 * D_GMLP + D_ATTN + 2 * D_KV
COL_U, COL_V, COL_Q, COL_K = 0, D_GMLP, 2 * D_GMLP, 2 * D_GMLP + D_ATTN
ROPE_THETA = 10000.0
LN_EPS = 1e-5
ALPHA = 2.0 ** 0.25
NEG_INF = -1e30
SCORE_SCALE = 1.0 / math.sqrt(HEAD_DIM)
ADAM_LR, ADAM_B1, ADAM_B2, ADAM_EPS, ADAM_WD, ADAM_STEP = 0.001, 0.9, 0.999, 1e-08, 0.01, 10
N_DEV = 8
LANES = 128
VMEM_LIMIT = 56 * 1024 * 1024
FFN_ROWS = 256

NT = (((1,), (1,)), ((), ()))
TN = (((0,), (0,)), ((), ()))


def _params(*sem):
    return pltpu.CompilerParams(dimension_semantics=sem, vmem_limit_bytes=VMEM_LIMIT)


def _dot(a, b, dims=None):
    if dims is None:
        return jnp.dot(a, b, preferred_element_type=F32)
    return lax.dot_general(a, b, dims, preferred_element_type=F32)


def _mean(a):
    return jnp.mean(a, axis=-1, keepdims=True)


def _ln_fwd(z, g, b):
    zc = z - _mean(z)
    rstd = lax.rsqrt(_mean(zc * zc) + LN_EPS)
    xhat = zc * rstd
    return xhat * g + b, xhat, rstd


def _ln_bwd(dy, xhat, rstd, g):
    dxhat = dy * g
    return rstd * (dxhat - _mean(dxhat) - xhat * _mean(dxhat * xhat))


_GELU_C = math.sqrt(2.0 / math.pi)


def _gelu(x):
    t = jnp.tanh(_GELU_C * (x + 0.044715 * (x * x * x)))
    return 0.5 * x * (1.0 + t)


def _gelu_and_grad(x):
    x2 = x * x
    t = jnp.tanh(_GELU_C * (x + 0.044715 * (x2 * x)))
    hx, ht = 0.5 * x, 0.5 * (1.0 + t)
    return x * ht, ht + hx * (1.0 - t * t) * (_GELU_C * (1.0 + 3.0 * 0.044715 * x2))


def _mean0(a):
    return jnp.mean(a, axis=0, keepdims=True)


def _ln_fwd_t(z, g, b):
    zc = z - _mean0(z)
    rstd = lax.rsqrt(_mean0(zc * zc) + LN_EPS)
    xhat = zc * rstd
    return xhat * g + b, xhat, rstd


def _ln_bwd_t(dy, xhat, rstd, g):
    dxhat = dy * g
    return rstd * (dxhat - _mean0(dxhat) - xhat * _mean0(dxhat * xhat))


def _rope_t(t, cos, sin_signed, bwd=False):
    half = HEAD_DIM // 2
    outs = []
    for r in range(0, t.shape[0], HEAD_DIM):
        th = t[r:r + HEAD_DIM]
        sw = jnp.concatenate([th[half:], th[:half]], axis=0) * sin_signed
        outs.append(th * cos - sw if bwd else th * cos + sw)
    return jnp.concatenate(outs, axis=0)


ANY = pl.BlockSpec(memory_space=pl.ANY)
GATHER_PIECES = 4
BF16_ROWS = 16


def _place():
    return lax.axis_index("x"), lax.axis_index("y"), lax.axis_index("c")


class _Comm:
    def __init__(self, ins, outs, sems, start, finish):
        self.ins, self.outs, self.sems, self.start, self.finish = ins, outs, sems, start, finish


def _gather_comm(arrs):
    n = len(arrs)
    pieces = []
    for a, arr in enumerate(arrs):
        k = GATHER_PIECES
        while arr.shape[0] % (k * BF16_ROWS):
            k //= 2
        pieces += [(a, p * (arr.shape[0] // k), arr.shape[0] // k) for p in range(k)]

    def parts(ins, outs, sems):
        send_sems, recv_sems, local_sems = sems
        x, y, c = _place()
        me, sibling = (x, y, c), (x, y, 1 - c)
        chips = [(1 - x, y), (x, 1 - y), (1 - x, 1 - y)]

        def copy(u, k, block, to, local=False):
            a, r0, nr = pieces[u]
            px, py, pc = block
            dst = outs[a].at[4 * px + 2 * py + pc, pl.ds(r0, nr)]
            return pltpu.make_async_remote_copy(
                src_ref=ins[a].at[pl.ds(r0, nr)] if local else dst, dst_ref=dst,
                send_sem=send_sems.at[u, k], recv_sem=recv_sems.at[u, k], device_id=to, device_id_type=MESH)

        mine = [pltpu.make_async_copy(ins[a], outs[a].at[4 * x + 2 * y + c], local_sems.at[a]) for a in range(n)]
        first = []
        for u in range(len(pieces)):
            first.append(copy(u, 0, me, sibling, local=True))
            first += [copy(u, 1 + j, me, (*chip, c), local=True) for j, chip in enumerate(chips)]
        return copy, mine, first, me, sibling, chips, c

    def start(ins, outs, sems):
        _, mine, first, *_ = parts(ins, outs, sems)
        for cp in mine + first:
            cp.start()

    def finish(ins, outs, sems):
        copy, mine, first, me, sibling, chips, c = parts(ins, outs, sems)
        passed = []
        for u in range(len(pieces)):
            for j, chip in enumerate(chips):
                copy(u, 1 + j, (*chip, c), me).wait_recv()
                fwd = copy(u, 4 + j, (*chip, c), sibling)
                fwd.start()
                passed.append(fwd)
        for u in range(len(pieces)):
            copy(u, 0, sibling, me).wait_recv()
            for j, chip in enumerate(chips):
                copy(u, 4 + j, (*chip, 1 - c), me).wait_recv()
        for cp in first + passed:
            cp.wait_send()
        for cp in mine:
            cp.wait()

    return _Comm(list(arrs), [jax.ShapeDtypeStruct((N_DEV,) + a.shape, a.dtype) for a in arrs],
                 [pltpu.SemaphoreType.DMA((len(pieces), 7)), pltpu.SemaphoreType.DMA((len(pieces), 7)),
                  pltpu.SemaphoreType.DMA((n,))], start, finish)


def _sibling_comm(parts):
    n = len(parts)

    def copies(ins, outs, sems):
        x, y, c = _place()
        return [pltpu.make_async_remote_copy(
            src_ref=ins[a].at[2 * q + (1 - c)], dst_ref=outs[a].at[q],
            send_sem=sems[0].at[a, q], recv_sem=sems[1].at[a, q],
            device_id=(x, y, 1 - c), device_id_type=MESH) for a in range(n) for q in range(4)]

    return _Comm(list(parts), [jax.ShapeDtypeStruct((4,) + p.shape[1:], p.dtype) for p in parts],
                 [pltpu.SemaphoreType.DMA((n, 4)), pltpu.SemaphoreType.DMA((n, 4))],
                 lambda *r: [cp.start() for cp in copies(*r)], lambda *r: [cp.wait() for cp in copies(*r)])


def _chips_comm(chip_parts, rows=None):
    n = len(chip_parts)
    r0, nr = (0, None) if rows is None else rows

    def copies(ins, outs, sems):
        x, y, c = _place()
        chips = [(1 - x, y), (x, 1 - y), (1 - x, 1 - y)]
        src = lambda a, q: ins[a].at[q] if rows is None else ins[a].at[q, pl.ds(r0, nr)]
        return [pltpu.make_async_remote_copy(
            src_ref=src(a, 2 * px + py), dst_ref=outs[a].at[k],
            send_sem=sems[0].at[a, k], recv_sem=sems[1].at[a, k],
            device_id=(px, py, c), device_id_type=MESH) for a in range(n) for k, (px, py) in enumerate(chips)]

    shape = lambda p: (3,) + p.shape[1:] if rows is None else (3, nr) + p.shape[2:]
    return _Comm(list(chip_parts), [jax.ShapeDtypeStruct(shape(p), p.dtype) for p in chip_parts],
                 [pltpu.SemaphoreType.DMA((n, 3)), pltpu.SemaphoreType.DMA((n, 3))],
                 lambda *r: [cp.start() for cp in copies(*r)], lambda *r: [cp.wait() for cp in copies(*r)])


def _carry(body, *, name, grid, in_specs, out_specs, out_shape, args, comms=(), scratch_shapes=(), prefetch=()):
    n_pre, n_in, n_out, n_scr = len(prefetch), len(in_specs), len(out_specs), len(scratch_shapes)
    c_ins = [a for cm in comms for a in cm.ins]
    c_outs = [s for cm in comms for s in cm.outs]
    c_sems = [s for cm in comms for s in cm.sems]

    def wrapped(*refs):
        pre, refs = refs[:n_pre], refs[n_pre:]
        ins, refs = refs[:n_in], refs[n_in:]
        cins, refs = refs[:len(c_ins)], refs[len(c_ins):]
        outs, refs = refs[:n_out], refs[n_out:]
        couts, refs = refs[:len(c_outs)], refs[len(c_outs):]
        scr, sems = refs[:n_scr], refs[n_scr:]
        groups, i0, o0, s0 = [], 0, 0, 0
        for cm in comms:
            groups.append((cm, cins[i0:i0 + len(cm.ins)], couts[o0:o0 + len(cm.outs)], sems[s0:s0 + len(cm.sems)]))
            i0, o0, s0 = i0 + len(cm.ins), o0 + len(cm.outs), s0 + len(cm.sems)
        first = pl.program_id(0) == 0
        last = pl.program_id(0) == grid[0] - 1
        for ax in range(1, len(grid)):
            first = first & (pl.program_id(ax) == 0)
            last = last & (pl.program_id(ax) == grid[ax] - 1)
        if comms:
            @pl.when(first)
            def _():
                for cm, ci, co, cs in groups:
                    cm.start(ci, co, cs)
        body(*pre, *ins, *outs, *scr)
        if comms:
            @pl.when(last)
            def _():
                for cm, ci, co, cs in groups:
                    cm.finish(ci, co, cs)

    grid_spec = pltpu.PrefetchScalarGridSpec(
        num_scalar_prefetch=n_pre, grid=grid,
        in_specs=list(in_specs) + [ANY] * len(c_ins), out_specs=list(out_specs) + [ANY] * len(c_outs),
        scratch_shapes=list(scratch_shapes) + c_sems)
    res = pl.pallas_call(
        wrapped, name=name, grid_spec=grid_spec, out_shape=list(out_shape) + c_outs,
        compiler_params=_params(*(["arbitrary"] * len(grid))),
    )(*prefetch, *args, *c_ins)
    outs, rest, per_comm = res[:n_out], res[n_out:], []
    for cm in comms:
        per_comm.append(rest[:len(cm.outs)])
        rest = rest[len(cm.outs):]
    return outs, per_comm


def _rope_tables(pos_row, inv_freq_col, comms=()):
    t_tok = pos_row.shape[1]
    tm = min(512, t_tok)

    def body(pos_ref, invf_ref, cos_ref, sin_ref):
        ang = pos_ref[...].astype(F32) * invf_ref[...]
        row = lax.broadcasted_iota(jnp.int32, ang.shape, 0)
        cos_ref[...] = jnp.cos(ang)
        sin_ref[...] = jnp.sin(ang) * jnp.where(row < HEAD_DIM // 2, -1.0, 1.0)

    return _carry(
        body, name="rope_tables", grid=(t_tok // tm,), comms=comms,
        in_specs=[pl.BlockSpec((1, tm), lambda i: (0, i)), pl.BlockSpec((HEAD_DIM, 1), lambda i: (0, 0))],
        out_specs=[pl.BlockSpec((HEAD_DIM, tm), lambda i: (0, i))] * 2,
        out_shape=[jax.ShapeDtypeStruct((HEAD_DIM, t_tok), F32)] * 2,
        args=(pos_row, inv_freq_col))


def _proj_in(x2, w_in_t, comms=()):
    t_tok, d = x2.shape
    d_in = w_in_t.shape[0]
    tm = min(512, t_tok)

    def body(x_ref, w_ref, h_ref, xb_ref):
        xb = x_ref[...].astype(BF16)
        xb_ref[...] = xb
        h_ref[...] = _dot(w_ref[...], xb, NT)

    return _carry(
        body, name="proj_in", grid=(t_tok // tm,), comms=comms,
        in_specs=[pl.BlockSpec((tm, d), lambda i: (i, 0)), pl.BlockSpec((d_in, d), lambda i: (0, 0))],
        out_specs=[pl.BlockSpec((d_in, tm), lambda i: (0, i)), pl.BlockSpec((tm, d), lambda i: (i, 0))],
        out_shape=[jax.ShapeDtypeStruct((d_in, t_tok), F32), jax.ShapeDtypeStruct((t_tok, d), BF16)],
        args=(x2, w_in_t))


MIX_BLOCKS = 2
MIX_W = MIX_BLOCKS * BLK


def _prev_block(i):
    return jnp.maximum(MIX_BLOCKS * i - 1, 0)


def _h_specs():
    kv_row = COL_K // (2 * D_KV)
    return [
        pl.BlockSpec((D_GMLP, MIX_W), lambda i: (0, i)),
        pl.BlockSpec((D_GMLP, MIX_W), lambda i: (1, i)),
        pl.BlockSpec((D_ATTN, MIX_W), lambda i: (2, i)),
        pl.BlockSpec((2 * D_KV, MIX_W), lambda i: (kv_row, i)),
        pl.BlockSpec((2 * D_KV, BLK), lambda i: (kv_row, _prev_block(i))),
    ]


def _table_specs():
    return [
        pl.BlockSpec((HEAD_DIM, MIX_W), lambda i: (0, i)),
        pl.BlockSpec((HEAD_DIM, MIX_W), lambda i: (0, i)),
        pl.BlockSpec((HEAD_DIM, BLK), lambda i: (0, _prev_block(i))),
        pl.BlockSpec((HEAD_DIM, BLK), lambda i: (0, _prev_block(i))),
    ]


def _cols(b):
    return slice(b * BLK, (b + 1) * BLK)


LSE_ROWS = 8
LSE_SPEC = pl.BlockSpec((LSE_ROWS, D_ATTN), lambda i: (i, 0))


def _block_inputs(b, i, kvc, kvp_ref, cos, sin, cosp_ref, sinp_ref, bias_ref):
    if b == 0:
        kv_prev, cos_prev, sin_prev, bias = kvp_ref[...], cosp_ref[...], sinp_ref[...], bias_ref[jnp.minimum(i, 1)]
    else:
        kv_prev, cos_prev, sin_prev, bias = kvc[:, _cols(b - 1)], cos[:, _cols(b - 1)], sin[:, _cols(b - 1)], bias_ref[1]
    return kvc[:, _cols(b)], kv_prev, cos[:, _cols(b)], sin[:, _cols(b)], cos_prev, sin_prev, bias


def _band_bias():
    ki = lax.broadcasted_iota(jnp.int32, (2, 2 * BLK, BLK), 1)
    qi = lax.broadcasted_iota(jnp.int32, (2, 2 * BLK, BLK), 2)
    later = lax.broadcasted_iota(jnp.int32, (2, 2 * BLK, BLK), 0) > 0
    dist = qi + BLK - ki
    return jnp.where((dist >= 0) & (dist < BLK) & ((ki >= BLK) | later), 0.0, NEG_INF).astype(F32)


BIAS_SPEC = pl.BlockSpec((2, 2 * BLK, BLK), lambda i: (0, 0, 0))


def _keys_values(kvc, kvp, cosc, sinc, cosp, sinp):
    kp, kc = _rope_t(kvp[:D_KV], cosp, sinp), _rope_t(kvc[:D_KV], cosc, sinc)
    k_t = jnp.concatenate([kp, kc], axis=1).astype(BF16)
    k_n = jnp.concatenate([kp.T, kc.T], axis=0).astype(BF16)
    v_t = jnp.concatenate([kvp[D_KV:], kvc[D_KV:]], axis=1).astype(BF16)
    return k_t, k_n, v_t


def _pad_head(th, kv):
    z = jnp.zeros_like(th)
    return jnp.concatenate([th, z] if kv == 0 else [z, th], axis=0)


def _group_lanes(parts):
    return jnp.concatenate(parts, axis=1)


def _softmax_sink_t(s, sink):
    m = jnp.maximum(jnp.max(s, axis=0, keepdims=True), sink)
    e = jnp.exp(s - m)
    denom = jnp.sum(e, axis=0, keepdims=True) + jnp.exp(sink - m)
    return e * (1.0 / denom), m + jnp.log(denom)


def _causal():
    row = lax.broadcasted_iota(jnp.int32, (BLK, BLK), 0)
    col = lax.broadcasted_iota(jnp.int32, (BLK, BLK), 1)
    return row >= col


def _mask_w_once(wsp_ref, wm_scr):
    @pl.when(pl.program_id(0) == 0)
    def _():
        causal = _causal()
        for hh in range(N_HEADS):
            wm_scr[hh] = jnp.where(causal, wsp_ref[hh], 0.0).astype(BF16)


def _mixer_fwd(h_t, cos_t, sin_t, w_spatial, b_spatial, vln_g, vln_b, sinks, band_bias, comms=()):
    t_tok = h_t.shape[1]
    group = N_HEADS // N_KV_HEADS

    def body(sinks_ref, u_ref, vg_ref, q_ref, kvc_ref, kvp_ref, cos_ref, sin_ref, cosp_ref, sinp_ref,
             wsp_ref, bsp_ref, g_ref, b_ref, bias_ref, cat_ref, lse_ref, wm_scr):
        i = pl.program_id(0)
        _mask_w_once(wsp_ref, wm_scr)
        lse_ref[...] = jnp.zeros_like(lse_ref)
        ua = _gelu(u_ref[...])
        vp, _, _ = _ln_fwd_t(_gelu(vg_ref[...]), g_ref[...], b_ref[...])
        vpb = vp.astype(BF16)
        for b in range(MIX_BLOCKS):
            for hh in range(N_HEADS):
                rows = slice(hh * HEAD_DIM, (hh + 1) * HEAD_DIM)
                mixed = _dot(vpb[rows, _cols(b)], wm_scr[hh], NT) + bsp_ref[hh:hh + 1, :]
                cat_ref[rows, _cols(b)] = (ua[rows, _cols(b)] * mixed).astype(BF16)

        kvc, cos, sin = kvc_ref[...], cos_ref[...], sin_ref[...]
        qr = (_rope_t(q_ref[...], cos, sin) * SCORE_SCALE).astype(BF16)
        sinks4 = [_group_lanes([jnp.full((1, BLK), sinks_ref[hh], F32) for hh in range(kv * group, (kv + 1) * group)])
                  for kv in range(N_KV_HEADS)]
        for b in range(MIX_BLOCKS):
            kv_cur, kv_prev, cosc, sinc, cosp, sinp, bias1 = _block_inputs(b, i, kvc, kvp_ref, cos, sin, cosp_ref, sinp_ref, bias_ref)
            _, k_n, v_t = _keys_values(kv_cur, kv_prev, cosc, sinc, cosp, sinp)
            bias = _group_lanes([bias1] * group)
            for kv in range(N_KV_HEADS):
                heads = range(kv * group, (kv + 1) * group)
                qs = _group_lanes([qr[hh * HEAD_DIM:(hh + 1) * HEAD_DIM, _cols(b)] for hh in heads])
                p, lse = _softmax_sink_t(_dot(k_n, _pad_head(qs, kv)) + bias, sinks4[kv])
                lse_ref[b * N_KV_HEADS + kv:b * N_KV_HEADS + kv + 1, :] = lse
                o = _dot(v_t[kv * HEAD_DIM:(kv + 1) * HEAD_DIM], p.astype(BF16)).astype(BF16)
                for j, hh in enumerate(heads):
                    cat_ref[D_GMLP + hh * HEAD_DIM:D_GMLP + (hh + 1) * HEAD_DIM, _cols(b)] = o[:, j * BLK:(j + 1) * BLK]

    full = lambda shape: pl.BlockSpec(shape, lambda i: (0,) * len(shape))
    return _carry(
        body, name="mixer_fwd", grid=(t_tok // MIX_W,), comms=comms,
        in_specs=[pl.BlockSpec(memory_space=pltpu.SMEM)] + _h_specs() + _table_specs() + [
            full((N_HEADS, BLK, BLK)), full((N_HEADS, BLK)), full((D_GMLP, 1)), full((D_GMLP, 1)), BIAS_SPEC],
        out_specs=[pl.BlockSpec((D_GMLP + D_ATTN, MIX_W), lambda i: (0, i)), LSE_SPEC],
        out_shape=[jax.ShapeDtypeStruct((D_GMLP + D_ATTN, t_tok), BF16),
                   jax.ShapeDtypeStruct((t_tok // MIX_W * LSE_ROWS, D_ATTN), F32)],
        scratch_shapes=[pltpu.VMEM((N_HEADS, BLK, BLK), BF16)],
        args=(sinks, h_t, h_t, h_t, h_t, h_t, cos_t, sin_t, cos_t, sin_t, w_spatial, b_spatial, vln_g, vln_b, band_bias))


def _proj_out(cat_t, x2, w_out_b, ln1_g, ln1_b, comms=()):
    t_tok, d = x2.shape
    tm = min(512, t_tok)

    def body(cat_ref, x_ref, w_ref, g_ref, b_ref, xhat_ref, rstd_ref, x1b_ref):
        x1, xhat, rstd = _ln_fwd(ALPHA * x_ref[...] + _dot(cat_ref[...], w_ref[...], TN), g_ref[...], b_ref[...])
        xhat_ref[...] = xhat
        rstd_ref[...] = rstd
        x1b_ref[...] = x1.astype(BF16)

    tok = lambda w: pl.BlockSpec((tm, w), lambda i: (i, 0))
    vec = pl.BlockSpec((1, d), lambda i: (0, 0))
    return _carry(
        body, name="proj_out", grid=(t_tok // tm,), comms=comms,
        in_specs=[pl.BlockSpec((cat_t.shape[0], tm), lambda i: (0, i)), tok(d), pl.BlockSpec(w_out_b.shape, lambda i: (0, 0)), vec, vec],
        out_specs=[tok(d), tok(1), tok(d)],
        out_shape=[jax.ShapeDtypeStruct((t_tok, d), F32), jax.ShapeDtypeStruct((t_tok, 1), F32), jax.ShapeDtypeStruct((t_tok, d), BF16)],
        args=(cat_t, x2, w_out_b, ln1_g, ln1_b))


def _ffn_fwd_bwd(xhat1, rstd1, x1b, target, w1_parts, w2_parts, ln1_g, ln1_b, ln2_g, ln2_b):
    t_tok, d = xhat1.shape
    n_part = len(w1_parts)
    n_chunk, _, fp = w1_parts[0].shape
    f = n_chunk * n_part * fp
    tm = min(FFN_ROWS, t_tok)

    def body(xhat1_ref, rstd1_ref, x1b_ref, tgt_ref, *refs):
        w1_hbm, w2_hbm = refs[:n_part], refs[n_part:2 * n_part]
        (g1_ref, b1_ref, g2_ref, b2_ref, act_ref, dpre_ref, dz2b_ref, dz1_ref, stats_ref,
         r_scr, w1_ref, w2_ref, w_sems) = refs[2 * n_part:]

        @pl.when(pl.program_id(0) == 0)
        def _():
            stats_ref[...] = jnp.zeros_like(stats_ref)
            loads = []
            for j in range(n_chunk):
                for p in range(n_part):
                    units = pl.ds((j * n_part + p) * fp, fp)
                    loads.append(pltpu.make_async_copy(w1_hbm[p].at[j], w1_ref.at[:, units], w_sems.at[0, p, j]))
                    loads.append(pltpu.make_async_copy(w2_hbm[p].at[j], w2_ref.at[units, :], w_sems.at[1, p, j]))
            for cp in loads:
                cp.start()
            for cp in loads:
                cp.wait()

        g1, g2 = g1_ref[...], g2_ref[...]
        xhat1 = xhat1_ref[...]
        r_scr[...] = jnp.maximum(_dot(x1b_ref[...], w1_ref[...]), 0.0)
        r = r_scr[...]
        act = (r * r).astype(BF16)
        act_ref[...] = act
        ff = _dot(act, w2_ref[...])
        y, xhat2, rstd2 = _ln_fwd(ALPHA * (xhat1 * g1 + b1_ref[...]) + ff, g2, b2_ref[...])
        diff = y - tgt_ref[...]
        loss = 0.5 * jnp.sum(jnp.sum(diff * diff, axis=-1, keepdims=True) / d, axis=0, keepdims=True)
        dy = diff / d
        dz2 = _ln_bwd(dy, xhat2, rstd2, g2)
        dz2b = dz2.astype(BF16)
        dz2b_ref[...] = dz2b
        dpre = (_dot(dz2b, w2_ref[...], NT) * (2.0 * r_scr[...])).astype(BF16)
        dpre_ref[...] = dpre
        dx1 = ALPHA * dz2 + _dot(dpre, w1_ref[...], NT)
        dz1_ref[...] = _ln_bwd(dx1, xhat1, rstd1_ref[...], g1)
        stats_ref[0:1, :] += jnp.sum(dx1 * xhat1, axis=0, keepdims=True)
        stats_ref[1:2, :] += jnp.sum(dx1, axis=0, keepdims=True)
        stats_ref[2:3, :] += jnp.sum(dy * xhat2, axis=0, keepdims=True)
        stats_ref[3:4, :] += jnp.sum(dy, axis=0, keepdims=True)
        stats_ref[4:5, :] += jnp.broadcast_to(loss, (1, d))

    tok = lambda w: pl.BlockSpec((tm, w), lambda i: (i, 0))
    vec = pl.BlockSpec((1, d), lambda i: (0, 0))
    return _carry(
        body, name="ffn_fwd_bwd", grid=(t_tok // tm,),
        in_specs=[tok(d), tok(1), tok(d), tok(d)] + [ANY] * (2 * n_part) + [vec, vec, vec, vec],
        out_specs=[tok(f), tok(f), tok(d), tok(d), pl.BlockSpec((8, d), lambda i: (0, 0))],
        out_shape=[jax.ShapeDtypeStruct((t_tok, f), BF16), jax.ShapeDtypeStruct((t_tok, f), BF16),
                   jax.ShapeDtypeStruct((t_tok, d), BF16), jax.ShapeDtypeStruct((t_tok, d), F32), jax.ShapeDtypeStruct((8, d), F32)],
        scratch_shapes=[pltpu.VMEM((tm, f), F32), pltpu.VMEM((d, f), BF16), pltpu.VMEM((f, d), BF16),
                        pltpu.SemaphoreType.DMA((2, n_part, n_chunk))],
        args=(xhat1, rstd1, x1b, target, *w1_parts, *w2_parts, ln1_g, ln1_b, ln2_g, ln2_b))[0]


def _ffn_wgrad(name, lhs, rhs, chunk_lhs, core_chip, comms=()):
    t_tok = lhs.shape[0]
    half = N_DEV // 2
    fc = (lhs if chunk_lhs else rhs).shape[1] // N_DEV
    chunk = (fc, rhs.shape[1]) if chunk_lhs else (lhs.shape[1], fc)

    def shard(s, cc):
        return 2 * (s % half) + jnp.where(s < half, 1 - cc[0], cc[0])

    def body(cc_ref, lhs_ref, rhs_ref, wire_ref, own_ref, recv_ref, send_buf, got, send_sems, recv_sems, got_sem):
        s = pl.program_id(0)
        x, y, c = _place()
        def send(q):
            return pltpu.make_async_remote_copy(
                src_ref=send_buf.at[q % 2], dst_ref=recv_ref.at[q], send_sem=send_sems.at[q], recv_sem=recv_sems.at[q],
                device_id=(x, y, 1 - c), device_id_type=MESH)

        def load(q):
            return pltpu.make_async_copy(recv_ref.at[q], got, got_sem.at[0])

        @pl.when(s >= half)
        def _():
            send(s - half).wait_recv()
            load(s - half).start()

        g = _dot(lhs_ref[...], rhs_ref[...], TN)

        for q in range(half):
            @pl.when(s == q)
            def _(q=q):
                if q >= 2:
                    send(q - 2).wait_send()
                send_buf[q % 2] = g
                send(q).start()

            @pl.when(s == half + q)
            def _(q=q):
                load(q).wait()
                total = g + got[...]
                wire_ref[...] = total.astype(BF16)

                @pl.when(cc_ref[1] == q)
                def _():
                    own_ref[...] = total

        @pl.when(s == N_DEV - 1)
        def _():
            for q in range(half - 2, half):
                send(q).wait_send()

    resident = lambda a: pl.BlockSpec(a.shape, lambda s, cc: (0, 0), pipeline_mode=pl.Buffered(1))
    chunked = pl.BlockSpec((t_tok, fc), lambda s, cc: (0, shard(s, cc)))
    (wire, own, _), per_comm = _carry(
        body, name=name, grid=(N_DEV,), comms=comms, prefetch=(core_chip,),
        in_specs=[chunked, resident(rhs)] if chunk_lhs else [resident(lhs), chunked],
        out_specs=[pl.BlockSpec((None,) + chunk, lambda s, cc: (jnp.maximum(s - half, 0), 0, 0)),
                   pl.BlockSpec(chunk, lambda s, cc: (0, 0)), ANY],
        out_shape=[jax.ShapeDtypeStruct((half,) + chunk, BF16), jax.ShapeDtypeStruct(chunk, F32),
                   jax.ShapeDtypeStruct((half,) + chunk, F32)],
        scratch_shapes=[pltpu.VMEM((2,) + chunk, F32), pltpu.VMEM(chunk, F32), pltpu.SemaphoreType.DMA((half,)),
                        pltpu.SemaphoreType.DMA((half,)), pltpu.SemaphoreType.DMA((1,))],
        args=(lhs, rhs))
    return wire, own, per_comm


def _proj_out_bwd(dz1, cat_t, w_out_b, comms=()):
    t_tok, d = dz1.shape
    d_mix = cat_t.shape[0]
    tm = min(512, t_tok)

    def body(dz1_ref, cat_ref, w_ref, dcat_ref, gw_ref):
        @pl.when(pl.program_id(0) == 0)
        def _():
            gw_ref[...] = jnp.zeros_like(gw_ref)

        dzb = dz1_ref[...].astype(BF16)
        dcat_ref[...] = _dot(w_ref[...], dzb, NT)
        gw_ref[...] += _dot(cat_ref[...], dzb)

    return _carry(
        body, name="proj_out_bwd", grid=(t_tok // tm,), comms=comms,
        in_specs=[pl.BlockSpec((tm, d), lambda i: (i, 0)), pl.BlockSpec((d_mix, tm), lambda i: (0, i)),
                  pl.BlockSpec((d_mix, d), lambda i: (0, 0))],
        out_specs=[pl.BlockSpec((d_mix, tm), lambda i: (0, i)), pl.BlockSpec((d_mix, d), lambda i: (0, 0))],
        out_shape=[jax.ShapeDtypeStruct((d_mix, t_tok), F32), jax.ShapeDtypeStruct((d_mix, d), F32)],
        args=(dz1, cat_t, w_out_b))


def _mixer_bwd(dcat_t, h_t, cos_t, sin_t, w_spatial, b_spatial, vln_g, vln_b, sinks, band_bias, lse, comms=()):
    t_tok = h_t.shape[1]
    nb, n_step = t_tok // BLK, t_tok // MIX_W
    group = N_HEADS // N_KV_HEADS

    def body(sinks_ref, dcat_ref, u_ref, vg_ref, q_ref, kvc_ref, kvp_ref, cos_ref, sin_ref, cosp_ref, sinp_ref,
             wsp_ref, bsp_ref, g_ref, b_ref, bias_ref, lse_ref, dh_ref, dkvc_ref, dkvp_ref, gwsb_ref, gbsp_ref, gvln_ref, gsink_ref,
             dg_acc, db_acc, wm_scr, gws_ref):
        i = pl.program_id(0)

        @pl.when(i == 0)
        def _():
            gws_ref[...] = jnp.zeros_like(gws_ref)
            gbsp_ref[...] = jnp.zeros_like(gbsp_ref)
            gsink_ref[...] = jnp.zeros_like(gsink_ref)
            dg_acc[...] = jnp.zeros_like(dg_acc)
            db_acc[...] = jnp.zeros_like(db_acc)

        _mask_w_once(wsp_ref, wm_scr)

        g = g_ref[...]
        ua, ua_grad = _gelu_and_grad(u_ref[...])
        vv, vv_grad = _gelu_and_grad(vg_ref[...])
        vp, vhat, rstd = _ln_fwd_t(vv, g, b_ref[...])
        vpb = vp.astype(BF16)
        da = dcat_ref[0:D_GMLP, :]
        dmixed = da * ua
        dvp_blocks = []
        for b in range(MIX_BLOCKS):
            dvp_parts = []
            for hh in range(N_HEADS):
                rows = slice(hh * HEAD_DIM, (hh + 1) * HEAD_DIM)
                vpb_h = vpb[rows, _cols(b)]
                mixed = _dot(vpb_h, wm_scr[hh], NT) + bsp_ref[hh:hh + 1, :]
                dh_ref[COL_U + hh * HEAD_DIM:COL_U + (hh + 1) * HEAD_DIM, _cols(b)] = (
                    da[rows, _cols(b)] * mixed * ua_grad[rows, _cols(b)]).astype(BF16)
                dm = dmixed[rows, _cols(b)]
                dmb = dm.astype(BF16)
                gbsp_ref[hh:hh + 1, :] += jnp.sum(dm, axis=0, keepdims=True)
                gws_ref[hh] += _dot(dmb, vpb_h, TN)
                dvp_parts.append(_dot(dmb, wm_scr[hh]))
            dvp_blocks.append(jnp.concatenate(dvp_parts, axis=0))
        dvp = jnp.concatenate(dvp_blocks, axis=1)
        dgv, dbv = dvp * vhat, dvp
        for b in range(MIX_BLOCKS):
            dg_acc[...] += dgv[:, _cols(b)]
            db_acc[...] += dbv[:, _cols(b)]
        dh_ref[COL_V:COL_V + D_GMLP, :] = (_ln_bwd_t(dvp, vhat, rstd, g) * vv_grad).astype(BF16)

        kvc, cos, sin = kvc_ref[...], cos_ref[...], sin_ref[...]
        qr = (_rope_t(q_ref[...], cos, sin) * SCORE_SCALE).astype(BF16)
        sinks4 = [_group_lanes([jnp.full((1, BLK), sinks_ref[hh], F32) for hh in range(kv * group, (kv + 1) * group)])
                  for kv in range(N_KV_HEADS)]
        dq_blocks, dkv_cur, dkv_prev = [], [], []
        for b in range(MIX_BLOCKS):
            kv_cur, kv_prev, cosc, sinc, cosp, sinp, bias1 = _block_inputs(b, i, kvc, kvp_ref, cos, sin, cosp_ref, sinp_ref, bias_ref)
            k_t, k_n, v_t = _keys_values(kv_cur, kv_prev, cosc, sinc, cosp, sinp)
            v_n = jnp.concatenate([kv_prev[D_KV:].T, kv_cur[D_KV:].T], axis=0).astype(BF16)
            bias = _group_lanes([bias1] * group)
            dk, dv, dq_parts = [], [], []
            for kv in range(N_KV_HEADS):
                heads = range(kv * group, (kv + 1) * group)
                kv_rows = slice(kv * HEAD_DIM, (kv + 1) * HEAD_DIM)
                qs = _group_lanes([qr[hh * HEAD_DIM:(hh + 1) * HEAD_DIM, _cols(b)] for hh in heads])
                dos = _group_lanes([dcat_ref[D_GMLP + hh * HEAD_DIM:D_GMLP + (hh + 1) * HEAD_DIM, _cols(b)]
                                    for hh in heads]).astype(BF16)
                lse_g = lse_ref[b * N_KV_HEADS + kv:b * N_KV_HEADS + kv + 1, :]
                p = jnp.exp(_dot(k_n, _pad_head(qs, kv)) + bias - lse_g)
                p_sink = jnp.exp(sinks4[kv] - lse_g)
                dp = _dot(v_n, _pad_head(dos, kv))
                delta = jnp.sum(p * dp, axis=0, keepdims=True)
                ds = (p * (dp - delta)).astype(BF16)
                dsink = p_sink * delta
                dq = _dot(k_t[kv_rows], ds) * SCORE_SCALE
                for j, hh in enumerate(heads):
                    gsink_ref[hh:hh + 1, :] -= dsink[:, j * BLK:(j + 1) * BLK]
                    dq_parts.append(dq[:, j * BLK:(j + 1) * BLK])
                dk.append(_dot(qs, ds, NT))
                dv.append(_dot(dos, p.astype(BF16), NT))
            dq_blocks.append(jnp.concatenate(dq_parts, axis=0))
            dk_all, dv_all = jnp.concatenate(dk, axis=0), jnp.concatenate(dv, axis=0)
            dkv_cur.append(jnp.concatenate([_rope_t(dk_all[:, BLK:], cosc, sinc, bwd=True), dv_all[:, BLK:]], axis=0))
            dkv_prev.append(jnp.concatenate([_rope_t(dk_all[:, :BLK], cosp, sinp, bwd=True), dv_all[:, :BLK]], axis=0))
        dh_ref[COL_Q:COL_Q + D_ATTN, :] = _rope_t(jnp.concatenate(dq_blocks, axis=1), cos, sin, bwd=True).astype(BF16)
        for b in range(MIX_BLOCKS):
            dkvc_ref[:, _cols(b)] = dkv_cur[b] + dkv_prev[b + 1] if b + 1 < MIX_BLOCKS else dkv_cur[b]
        dkvp_ref[...] = dkv_prev[0]

        @pl.when(i == n_step - 1)
        def _():
            causal = _causal()
            for hh in range(N_HEADS):
                gwsb_ref[hh] = jnp.where(causal, gws_ref[hh], 0.0).astype(BF16)
            gvln_ref[...] = jnp.zeros_like(gvln_ref)
            gvln_ref[0:1, :] = jnp.sum(dg_acc[...].T, axis=0, keepdims=True)
            gvln_ref[1:2, :] = jnp.sum(db_acc[...].T, axis=0, keepdims=True)

    full = lambda shape: pl.BlockSpec(shape, lambda i: (0,) * len(shape))
    return _carry(
        body, name="mixer_bwd", grid=(n_step,), comms=comms,
        in_specs=[pl.BlockSpec(memory_space=pltpu.SMEM), pl.BlockSpec((D_GMLP + D_ATTN, MIX_W), lambda i: (0, i))]
        + _h_specs() + _table_specs()
        + [full((N_HEADS, BLK, BLK)), full((N_HEADS, BLK)), full((D_GMLP, 1)), full((D_GMLP, 1)), BIAS_SPEC, LSE_SPEC],
        out_specs=[pl.BlockSpec((COL_K, MIX_W), lambda i: (0, i)), pl.BlockSpec((2 * D_KV, MIX_W), lambda i: (0, i)),
                   pl.BlockSpec((2 * D_KV, BLK), lambda i: (0, (i + n_step - 1) % n_step)),
                   full((N_HEADS, BLK, BLK)), full((N_HEADS, BLK)), full((8, D_GMLP)), full((N_HEADS, LANES))],
        out_shape=[jax.ShapeDtypeStruct((COL_K, t_tok), BF16), jax.ShapeDtypeStruct((2 * D_KV, t_tok), F32),
                   jax.ShapeDtypeStruct((2 * D_KV, n_step * BLK), F32),
                   jax.ShapeDtypeStruct((N_HEADS, BLK, BLK), BF16), jax.ShapeDtypeStruct((N_HEADS, BLK), F32),
                   jax.ShapeDtypeStruct((8, D_GMLP), F32), jax.ShapeDtypeStruct((N_HEADS, LANES), F32)],
        scratch_shapes=[pltpu.VMEM((D_GMLP, BLK), F32), pltpu.VMEM((D_GMLP, BLK), F32), pltpu.VMEM((N_HEADS, BLK, BLK), BF16),
                        pltpu.VMEM((N_HEADS, BLK, BLK), F32)],
        args=(sinks, dcat_t, h_t, h_t, h_t, h_t, h_t, cos_t, sin_t, cos_t, sin_t, w_spatial, b_spatial, vln_g, vln_b, band_bias, lse))


def _proj_in_wgrad(dh_b, dkvc_t, dkvp_t, xb, comms=()):
    t_tok, d = xb.shape
    d_main, d_kv = dh_b.shape[0], dkvc_t.shape[0]
    tm = min(1024, t_tok)

    def body(dh_ref, dkvc_ref, dkvp_ref, xb_ref, dkvb_ref, gw_ref):
        @pl.when(pl.program_id(0) == 0)
        def _():
            gw_ref[...] = jnp.zeros_like(gw_ref)

        for s in range(tm // MIX_W):
            last = slice((s + 1) * MIX_W - BLK, (s + 1) * MIX_W)
            dkvb_ref[:, s * MIX_W:(s + 1) * MIX_W - BLK] = dkvc_ref[:, s * MIX_W:(s + 1) * MIX_W - BLK].astype(BF16)
            dkvb_ref[:, last] = (dkvc_ref[:, last] + dkvp_ref[:, _cols(s)]).astype(BF16)
        gw_ref[0:d_main, :] += _dot(dh_ref[...], xb_ref[...])
        gw_ref[d_main:, :] += _dot(dkvb_ref[...], xb_ref[...])

    tok = lambda rows: pl.BlockSpec((rows, tm), lambda i: (0, i))
    return _carry(
        body, name="proj_in_wgrad", grid=(t_tok // tm,), comms=comms,
        in_specs=[tok(d_main), tok(d_kv), pl.BlockSpec((d_kv, tm // MIX_BLOCKS), lambda i: (0, i)),
                  pl.BlockSpec((tm, d), lambda i: (i, 0))],
        out_specs=[tok(d_kv), pl.BlockSpec((d_main + d_kv, d), lambda i: (0, 0))],
        out_shape=[jax.ShapeDtypeStruct((d_kv, t_tok), BF16), jax.ShapeDtypeStruct((d_main + d_kv, d), F32)],
        args=(dh_b, dkvc_t, dkvp_t, xb))


def _proj_in_dgrad(dh_b, dkv_b, dz1, w_in_t, comms=()):
    t_tok, d = dz1.shape
    d_main, d_kv = dh_b.shape[0], dkv_b.shape[0]
    tm = min(512, t_tok)

    def body(dh_ref, dkv_ref, dz1_ref, w_ref, dx_ref):
        dx_ref[...] = (ALPHA * dz1_ref[...] + _dot(dh_ref[...], w_ref[0:d_main, :], TN)
                       + _dot(dkv_ref[...], w_ref[d_main:, :], TN))

    return _carry(
        body, name="proj_in_dgrad", grid=(t_tok // tm,), comms=comms,
        in_specs=[pl.BlockSpec((d_main, tm), lambda i: (0, i)), pl.BlockSpec((d_kv, tm), lambda i: (0, i)),
                  pl.BlockSpec((tm, d), lambda i: (i, 0)), pl.BlockSpec((d_main + d_kv, d), lambda i: (0, 0))],
        out_specs=[pl.BlockSpec((tm, d), lambda i: (i, 0))],
        out_shape=[jax.ShapeDtypeStruct((t_tok, d), F32)],
        args=(dh_b, dkv_b, dz1, w_in_t))


def _adamw(w, g, m, v):
    m = ADAM_B1 * m + (1.0 - ADAM_B1) * g
    v = ADAM_B2 * v + (1.0 - ADAM_B2) * (g * g)
    m_hat = m / (1.0 - ADAM_B1 ** ADAM_STEP)
    v_hat = v / (1.0 - ADAM_B2 ** ADAM_STEP)
    delta = -ADAM_LR * (m_hat / (jnp.sqrt(v_hat) + ADAM_EPS) + ADAM_WD * w)
    return delta, m, v


ADAMW_STEPS = 4


def _adamw_shards(name, items, comms=(), rider=None):
    n_in, n_out = 5 * len(items), 4 * len(items)
    n_rin = len(rider["args"]) if rider else 0

    def body(*refs):
        ins, rins, outs, routs = refs[:n_in], refs[n_in:n_in + n_rin], refs[n_in + n_rin:n_in + n_rin + n_out], refs[n_in + n_rin + n_out:]
        for i in range(len(items)):
            own_ref, recv_ref, w_ref, m_ref, v_ref = ins[5 * i:5 * i + 5]
            g = ((own_ref[...] + recv_ref[0].astype(F32)) + recv_ref[1].astype(F32)) + recv_ref[2].astype(F32)
            for o_ref, val in zip(outs[4 * i:4 * i + 4], (g,) + _adamw(w_ref[...], g, m_ref[...], v_ref[...])):
                o_ref[...] = val
        if rider:
            pl.when(pl.program_id(0) == 0)(lambda: rider["body"](rins, routs))

    in_specs, out_specs, out_shape, args = [], [], [], []
    for own, recv, w, m, v in items:
        r, c = own.shape
        tiles = ADAMW_STEPS
        while (r // tiles) % BF16_ROWS:
            tiles //= 2
        blk = pl.BlockSpec((r // tiles, c), lambda s, k=ADAMW_STEPS // tiles: (s // k, 0))
        in_specs += [blk, pl.BlockSpec((3, r // tiles, c), lambda s, k=ADAMW_STEPS // tiles: (0, s // k, 0)), blk, blk, blk]
        out_specs += [blk] * 4
        out_shape += [jax.ShapeDtypeStruct((r, c), F32)] * 4
        args += [own, recv, w, m, v]
    if rider:
        in_specs, out_specs = in_specs + rider["in_specs"], out_specs + rider["out_specs"]
        out_shape, args = out_shape + rider["out_shape"], args + rider["args"]
    res, per_comm = _carry(body, name=name, grid=(ADAMW_STEPS,), comms=comms, in_specs=in_specs, out_specs=out_specs,
                           out_shape=out_shape, args=args)
    return [res[4 * i:4 * i + 4] for i in range(len(items))], res[n_out:], per_comm


VEC_VLN, VEC_LN1G, VEC_LN1B, VEC_LN2G, VEC_LN2B, VEC_SINK, VEC_LOSS, VEC_BSP, VEC_ROWS = 0, 1, 2, 3, 4, 5, 6, 8, 16


def _adamw_small(parts_w, parts_vec, params):
    n = parts_w.shape[0]
    flat = [a for p in params for a in p]
    shapes = [p[0].shape for p in params]

    def grads(gw, gv):
        return [gw, gv[VEC_VLN:VEC_VLN + 1, 0:D_GMLP], gv[VEC_VLN:VEC_VLN + 1, D_GMLP:2 * D_GMLP],
                gv[VEC_BSP:VEC_BSP + N_HEADS, 0:BLK], gv[VEC_LN1G:VEC_LN1G + 1], gv[VEC_LN1B:VEC_LN1B + 1],
                gv[VEC_LN2G:VEC_LN2G + 1], gv[VEC_LN2B:VEC_LN2B + 1], gv[VEC_SINK:VEC_SINK + 1, 0:N_HEADS]]

    def body(ins, outs):
        (pw_ref, pv_ref), ins = ins[:2], ins[2:]
        gw, gv = pw_ref[0].astype(F32), pv_ref[0]
        for k in range(1, n):
            gw, gv = gw + pw_ref[k].astype(F32), gv + pv_ref[k]
        for i, g in enumerate(grads(gw, gv)):
            w_ref, m_ref, v_ref = ins[3 * i:3 * i + 3]
            delta, m_new, v_new = _adamw(w_ref[...], g, m_ref[...], v_ref[...])
            for o_ref, val in zip(outs[4 * i:4 * i + 4], (g, delta, m_new, v_new)):
                o_ref[...] = val
        outs[-1][...] = gv[VEC_LOSS:VEC_LOSS + 1, 0:LANES]

    whole = lambda shape, **kw: pl.BlockSpec(shape, lambda i: (0,) * len(shape), **kw)
    once = dict(pipeline_mode=pl.Buffered(1))
    return dict(
        body=body, args=[parts_w, parts_vec, *flat],
        in_specs=[whole(parts_w.shape, **once), whole(parts_vec.shape, **once)] + [whole(a.shape, **once) for a in flat],
        out_specs=[whole(s) for s in shapes for _ in range(4)] + [whole((1, LANES))],
        out_shape=[jax.ShapeDtypeStruct(s, F32) for s in shapes for _ in range(4)] + [jax.ShapeDtypeStruct((1, LANES), F32)])


def _pair_sum(name, parts, recv, core_chip, comms=()):
    _, r, c = parts.shape
    tr = r if r <= 512 else 512

    def body(cc_ref, a_ref, b_ref, wire_ref, own_ref):
        s = a_ref[...] + b_ref[...]
        wire_ref[...] = s.astype(BF16)

        @pl.when(pl.program_id(1) == cc_ref[1])
        def _():
            own_ref[...] = s

    return _carry(
        body, name=name, grid=(r // tr, 4), prefetch=(core_chip,), comms=comms,
        in_specs=[pl.BlockSpec((None, tr, c), lambda i, q, cc: (2 * q + cc[0], i, 0)),
                  pl.BlockSpec((None, tr, c), lambda i, q, cc: (q, i, 0))],
        out_specs=[pl.BlockSpec((None, tr, c), lambda i, q, cc: (q, i, 0)), pl.BlockSpec((tr, c), lambda i, q, cc: (i, 0))],
        out_shape=[jax.ShapeDtypeStruct((4, r, c), BF16), jax.ShapeDtypeStruct((r, c), F32)],
        args=(parts, recv))


def kernel(x, positions, w_in, v_ln_g, v_ln_b, w_spatial, b_spatial, sinks, w_out, ln1_g, ln1_b, w_ff1, w_ff2, ln2_g, ln2_b, loss_target, m_w_in, m_v_ln_g, m_v_ln_b, m_w_spatial, m_b_spatial, m_sinks, m_w_out, m_ln1_g, m_ln1_b, m_w_ff1, m_w_ff2, m_ln2_g, m_ln2_b, v_w_in, v_v_ln_g, v_v_ln_b, v_w_spatial, v_b_spatial, v_sinks, v_w_out, v_ln1_g, v_ln1_b, v_w_ff1, v_w_ff2, v_ln2_g, v_ln2_b):
    _, t_tok, d = x.shape
    xi, yi, ci = _place()
    core_chip = jnp.stack([ci, 2 * xi + yi]).astype(jnp.int32)
    x2 = x.reshape(t_tok, d)
    target = loss_target.reshape(t_tok, d)
    inv_freq = ROPE_THETA ** (-jnp.arange(0, HEAD_DIM, 2, dtype=F32) / HEAD_DIM)
    wsp, bsp, sink_vec = w_spatial[0], b_spatial[0], sinks[0]
    vg_col, vb_col = v_ln_g.reshape(D_GMLP, 1), v_ln_b.reshape(D_GMLP, 1)
    big = {"in": w_in[0], "out": w_out[0], "ff1": w_ff1[0], "ff2": w_ff2[0]}
    half1, half2 = big["ff1"].shape[1] // 2, big["ff2"].shape[0] // 2
    w1_mine = [big["ff1"][:, :half1].astype(BF16), big["ff1"][:, half1:].astype(BF16)]
    w2_mine = [big["ff2"][:half2].astype(BF16), big["ff2"][half2:].astype(BF16)]

    (cos_t, sin_t), ((g_in,),) = _rope_tables(
        positions, jnp.tile(inv_freq, 2).reshape(HEAD_DIM, 1), comms=[_gather_comm([big["in"].T.astype(BF16)])])
    w_in_t = g_in.reshape(D_IN, d)
    (h_t, xb), ((g_out, w1_a),) = _proj_in(x2, w_in_t, comms=[_gather_comm([big["out"].astype(BF16), w1_mine[0]])])
    w_out_b = g_out.reshape(-1, d)
    band_bias = _band_bias()
    (cat_t, lse), ((w1_b, w2_a),) = _mixer_fwd(h_t, cos_t, sin_t, wsp, bsp, vg_col, vb_col, sink_vec, band_bias,
                                                comms=[_gather_comm([w1_mine[1], w2_mine[0]])])
    (xhat1, rstd1, x1b), ((w2_b,),) = _proj_out(cat_t, x2, w_out_b, ln1_g, ln1_b, comms=[_gather_comm([w2_mine[1]])])
    act_b, dpre_b, dz2b, dz1, stats = _ffn_fwd_bwd(xhat1, rstd1, x1b, target, [w1_a, w1_b], [w2_a, w2_b], ln1_g, ln1_b, ln2_g, ln2_b)

    (dcat_t, gw_out), _ = _proj_out_bwd(dz1, cat_t, w_out_b)
    p_out = gw_out.reshape(N_DEV, -1, d)
    wire_ff1, own_ff1, ((s_out,),) = _ffn_wgrad("ffn_wgrad1", x1b, dpre_b, False, core_chip, comms=[_sibling_comm([p_out])])
    (wire_out, own_out), _ = _pair_sum("pair_sum_out", p_out, s_out, core_chip)
    wire_ff2, own_ff2, ((r_ff1,),) = _ffn_wgrad("ffn_wgrad2", act_b, dz2b, True, core_chip, comms=[_chips_comm([wire_ff1])])
    (dh_b, dkvc_t, dkvp_t, g_wsp, g_bsp, g_vln, g_sink), ((r_ff2, r_out),) = _mixer_bwd(
        dcat_t, h_t, cos_t, sin_t, wsp, bsp, vg_col, vb_col, sink_vec, band_bias, lse,
        comms=[_chips_comm([wire_ff2, wire_out])])
    sink_row = jnp.pad(g_sink.sum(axis=1).reshape(1, N_HEADS), ((0, 0), (0, d - N_HEADS)))
    small_vec = jnp.concatenate([g_vln[0:2].reshape(1, d), stats[0:4], sink_row, stats[4:5], jnp.zeros((1, d), F32),
                                 jnp.pad(g_bsp, ((0, 0), (0, d - BLK)))], axis=0)
    (dkv_b, gw_in_t), ((parts_w, parts_vec),) = _proj_in_wgrad(
        dh_b, dkvc_t, dkvp_t, xb, comms=[_gather_comm([g_wsp.reshape(-1, BLK), small_vec])])
    p_in = gw_in_t.reshape(N_DEV, -1, d)

    (out_out,), _, ((s_in,),) = _adamw_shards("adamw_out", [(own_out, r_out, big["out"], m_w_out[0], v_w_out[0])],
                                              comms=[_sibling_comm([p_in])])
    (wire_in, own_in), _ = _pair_sum("pair_sum_in", p_in, s_in, core_chip)
    (grad_x,), ((r_in,),) = _proj_in_dgrad(dh_b, dkv_b, dz1, w_in_t, comms=[_chips_comm([wire_in])])
    small = [(w_spatial, m_w_spatial, v_w_spatial), (v_ln_g, m_v_ln_g, v_v_ln_g), (v_ln_b, m_v_ln_b, v_v_ln_b),
             (b_spatial, m_b_spatial, v_b_spatial), (ln1_g, m_ln1_g, v_ln1_g), (ln1_b, m_ln1_b, v_ln1_b),
             (ln2_g, m_ln2_g, v_ln2_g), (ln2_b, m_ln2_b, v_ln2_b), (sinks, m_sinks, v_sinks)]
    views = [(-1, BLK), None, None, (N_HEADS, BLK)] + [None] * 5
    small_update = _adamw_small(parts_w, parts_vec, [
        tuple(a if vw is None else a.reshape(vw) for a in p) for p, vw in zip(small, views)])
    (ff1_out, ff2_out, in_out_t), small_res, _ = _adamw_shards("adamw_all", [
        (own_ff1, r_ff1, big["ff1"], m_w_ff1[0], v_w_ff1[0]), (own_ff2, r_ff2, big["ff2"], m_w_ff2[0], v_w_ff2[0]),
        (own_in, r_in, big["in"].T, m_w_in[0].T, v_w_in[0].T)], rider=small_update)
    in_out = [o.T for o in in_out_t]
    small_out = [[o.reshape(p[0].shape) for o in small_res[4 * i:4 * i + 4]] for i, p in enumerate(small)]
    loss = small_res[-1][0, 0]

    big_out = {0: in_out, 6: out_out, 9: ff1_out, 10: ff2_out}
    small_slot = {3: 0, 1: 1, 2: 2, 4: 3, 7: 4, 8: 5, 11: 6, 12: 7, 5: 8}
    outs = [loss, grad_x.reshape(x.shape)]
    for kind in range(4):
        for wi in range(13):
            outs.append(big_out[wi][kind][None] if wi in big_out else small_out[small_slot[wi]][kind])
    return tuple(outs)
```

```python
import math

import jax
import jax.numpy as jnp
from jax import lax
from jax.experimental import pallas as pl
from jax.experimental.pallas import tpu as pltpu

F32 = jnp.float32
BF16 = jnp.bfloat16
MESH = pl.DeviceIdType.MESH

HEAD_DIM = 64
N_HEADS = 8
N_KV_HEADS = 2
BLK = 128
D_GMLP = N_HEADS * HEAD_DIM
D_ATTN = N_HEADS * HEAD_DIM
D_KV = N_KV_HEADS * HEAD_DIM
D_IN = 2 * D_GMLP + D_ATTN + 2 * D_KV
COL_U, COL_V, COL_Q, COL_K = 0, D_GMLP, 2 * D_GMLP, 2 * D_GMLP + D_ATTN
ROPE_THETA = 10000.0
LN_EPS = 1e-5
ALPHA = 2.0 ** 0.25
NEG_INF = -1e30
SCORE_SCALE = 1.0 / math.sqrt(HEAD_DIM)
ADAM_LR, ADAM_B1, ADAM_B2, ADAM_EPS, ADAM_WD, ADAM_STEP = 0.001, 0.9, 0.999, 1e-08, 0.01, 10
N_DEV = 8
LANES = 128
VMEM_LIMIT = 56 * 1024 * 1024
FFN_ROWS = 256

NT = (((1,), (1,)), ((), ()))
TN = (((0,), (0,)), ((), ()))


def _params(*sem):
    return pltpu.CompilerParams(dimension_semantics=sem, vmem_limit_bytes=VMEM_LIMIT)


def _dot(a, b, dims=None):
    if dims is None:
        return jnp.dot(a, b, preferred_element_type=F32)
    return lax.dot_general(a, b, dims, preferred_element_type=F32)


def _mean(a):
    return jnp.mean(a, axis=-1, keepdims=True)


def _ln_fwd(z, g, b):
    zc = z - _mean(z)
    rstd = lax.rsqrt(_mean(zc * zc) + LN_EPS)
    xhat = zc * rstd
    return xhat * g + b, xhat, rstd


def _ln_bwd(dy, xhat, rstd, g):
    dxhat = dy * g
    return rstd * (dxhat - _mean(dxhat) - xhat * _mean(dxhat * xhat))


_GELU_C = math.sqrt(2.0 / math.pi)


def _gelu(x):
    t = jnp.tanh(_GELU_C * (x + 0.044715 * (x * x * x)))
    return 0.5 * x * (1.0 + t)


def _gelu_and_grad(x):
    x2 = x * x
    t = jnp.tanh(_GELU_C * (x + 0.044715 * (x2 * x)))
    hx, ht = 0.5 * x, 0.5 * (1.0 + t)
    return x * ht, ht + hx * (1.0 - t * t) * (_GELU_C * (1.0 + 3.0 * 0.044715 * x2))


def _mean0(a):
    return jnp.mean(a, axis=0, keepdims=True)


def _ln_fwd_t(z, g, b):
    zc = z - _mean0(z)
    rstd = lax.rsqrt(_mean0(zc * zc) + LN_EPS)
    xhat = zc * rstd
    return xhat * g + b, xhat, rstd


def _ln_bwd_t(dy, xhat, rstd, g):
    dxhat = dy * g
    return rstd * (dxhat - _mean0(dxhat) - xhat * _mean0(dxhat * xhat))


def _rope_t(t, cos, sin_signed, bwd=False):
    half = HEAD_DIM // 2
    outs = []
    for r in range(0, t.shape[0], HEAD_DIM):
        th = t[r:r + HEAD_DIM]
        sw = jnp.concatenate([th[half:], th[:half]], axis=0) * sin_signed
        outs.append(th * cos - sw if bwd else th * cos + sw)
    return jnp.concatenate(outs, axis=0)


ANY = pl.BlockSpec(memory_space=pl.ANY)
GATHER_PIECES = 4
BF16_ROWS = 16


def _place():
    return lax.axis_index("x"), lax.axis_index("y"), lax.axis_index("c")


class _Comm:
    def __init__(self, ins, outs, sems, start, finish):
        self.ins, self.outs, self.sems, self.start, self.finish = ins, outs, sems, start, finish


def _gather_comm(arrs):
    n = len(arrs)
    pieces = []
    for a, arr in enumerate(arrs):
        k = GATHER_PIECES
        while arr.shape[0] % (k * BF16_ROWS):
            k //= 2
        pieces += [(a, p * (arr.shape[0] // k), arr.shape[0] // k) for p in range(k)]

    def parts(ins, outs, sems):
        send_sems, recv_sems, local_sems = sems
        x, y, c = _place()
        me, sibling = (x, y, c), (x, y, 1 - c)
        chips = [(1 - x, y), (x, 1 - y), (1 - x, 1 - y)]

        def copy(u, k, block, to, local=False):
            a, r0, nr = pieces[u]
            px, py, pc = block
            dst = outs[a].at[4 * px + 2 * py + pc, pl.ds(r0, nr)]
            return pltpu.make_async_remote_copy(
                src_ref=ins[a].at[pl.ds(r0, nr)] if local else dst, dst_ref=dst,
                send_sem=send_sems.at[u, k], recv_sem=recv_sems.at[u, k], device_id=to, device_id_type=MESH)

        mine = [pltpu.make_async_copy(ins[a], outs[a].at[4 * x + 2 * y + c], local_sems.at[a]) for a in range(n)]
        first = []
        for u in range(len(pieces)):
            first.append(copy(u, 0, me, sibling, local=True))
            first += [copy(u, 1 + j, me, (*chip, c), local=True) for j, chip in enumerate(chips)]
        return copy, mine, first, me, sibling, chips, c

    def start(ins, outs, sems):
        _, mine, first, *_ = parts(ins, outs, sems)
        for cp in mine + first:
            cp.start()

    def finish(ins, outs, sems):
        copy, mine, first, me, sibling, chips, c = parts(ins, outs, sems)
        passed = []
        for u in range(len(pieces)):
            for j, chip in enumerate(chips):
                copy(u, 1 + j, (*chip, c), me).wait_recv()
                fwd = copy(u, 4 + j, (*chip, c), sibling)
                fwd.start()
                passed.append(fwd)
        for u in range(len(pieces)):
            copy(u, 0, sibling, me).wait_recv()
            for j, chip in enumerate(chips):
                copy(u, 4 + j, (*chip, 1 - c), me).wait_recv()
        for cp in first + passed:
            cp.wait_send()
        for cp in mine:
            cp.wait()

    return _Comm(list(arrs), [jax.ShapeDtypeStruct((N_DEV,) + a.shape, a.dtype) for a in arrs],
                 [pltpu.SemaphoreType.DMA((len(pieces), 7)), pltpu.SemaphoreType.DMA((len(pieces), 7)),
                  pltpu.SemaphoreType.DMA((n,))], start, finish)


def _sibling_comm(parts):
    n = len(parts)

    def copies(ins, outs, sems):
        x, y, c = _place()
        return [pltpu.make_async_remote_copy(
            src_ref=ins[a].at[2 * q + (1 - c)], dst_ref=outs[a].at[q],
            send_sem=sems[0].at[a, q], recv_sem=sems[1].at[a, q],
            device_id=(x, y, 1 - c), device_id_type=MESH) for a in range(n) for q in range(4)]

    return _Comm(list(parts), [jax.ShapeDtypeStruct((4,) + p.shape[1:], p.dtype) for p in parts],
                 [pltpu.SemaphoreType.DMA((n, 4)), pltpu.SemaphoreType.DMA((n, 4))],
                 lambda *r: [cp.start() for cp in copies(*r)], lambda *r: [cp.wait() for cp in copies(*r)])


def _chips_comm(chip_parts, rows=None):
    n = len(chip_parts)
    r0, nr = (0, None) if rows is None else rows

    def copies(ins, outs, sems):
        x, y, c = _place()
        chips = [(1 - x, y), (x, 1 - y), (1 - x, 1 - y)]
        src = lambda a, q: ins[a].at[q] if rows is None else ins[a].at[q, pl.ds(r0, nr)]
        return [pltpu.make_async_remote_copy(
            src_ref=src(a, 2 * px + py), dst_ref=outs[a].at[k],
            send_sem=sems[0].at[a, k], recv_sem=sems[1].at[a, k],
            device_id=(px, py, c), device_id_type=MESH) for a in range(n) for k, (px, py) in enumerate(chips)]

    shape = lambda p: (3,) + p.shape[1:] if rows is None else (3, nr) + p.shape[2:]
    return _Comm(list(chip_parts), [jax.ShapeDtypeStruct(shape(p), p.dtype) for p in chip_parts],
                 [pltpu.SemaphoreType.DMA((n, 3)), pltpu.SemaphoreType.DMA((n, 3))],
                 lambda *r: [cp.start() for cp in copies(*r)], lambda *r: [cp.wait() for cp in copies(*r)])


def _carry(body, *, name, grid, in_specs, out_specs, out_shape, args, comms=(), scratch_shapes=(), prefetch=()):
    n_pre, n_in, n_out, n_scr = len(prefetch), len(in_specs), len(out_specs), len(scratch_shapes)
    c_ins = [a for cm in comms for a in cm.ins]
    c_outs = [s for cm in comms for s in cm.outs]
    c_sems = [s for cm in comms for s in cm.sems]

    def wrapped(*refs):
        pre, refs = refs[:n_pre], refs[n_pre:]
        ins, refs = refs[:n_in], refs[n_in:]
        cins, refs = refs[:len(c_ins)], refs[len(c_ins):]
        outs, refs = refs[:n_out], refs[n_out:]
        couts, refs = refs[:len(c_outs)], refs[len(c_outs):]
        scr, sems = refs[:n_scr], refs[n_scr:]
        groups, i0, o0, s0 = [], 0, 0, 0
        for cm in comms:
            groups.append((cm, cins[i0:i0 + len(cm.ins)], couts[o0:o0 + len(cm.outs)], sems[s0:s0 + len(cm.sems)]))
            i0, o0, s0 = i0 + len(cm.ins), o0 + len(cm.outs), s0 + len(cm.sems)
        first = pl.program_id(0) == 0
        last = pl.program_id(0) == grid[0] - 1
        for ax in range(1, len(grid)):
            first = first & (pl.program_id(ax) == 0)
            last = last & (pl.program_id(ax) == grid[ax] - 1)
        if comms:
            @pl.when(first)
            def _():
                for cm, ci, co, cs in groups:
                    cm.start(ci, co, cs)
        body(*pre, *ins, *outs, *scr)
        if comms:
            @pl.when(last)
            def _():
                for cm, ci, co, cs in groups:
                    cm.finish(ci, co, cs)

    grid_spec = pltpu.PrefetchScalarGridSpec(
        num_scalar_prefetch=n_pre, grid=grid,
        in_specs=list(in_specs) + [ANY] * len(c_ins), out_specs=list(out_specs) + [ANY] * len(c_outs),
        scratch_shapes=list(scratch_shapes) + c_sems)
    res = pl.pallas_call(
        wrapped, name=name, grid_spec=grid_spec, out_shape=list(out_shape) + c_outs,
        compiler_params=_params(*(["arbitrary"] * len(grid))),
    )(*prefetch, *args, *c_ins)
    outs, rest, per_comm = res[:n_out], res[n_out:], []
    for cm in comms:
        per_comm.append(rest[:len(cm.outs)])
        rest = rest[len(cm.outs):]
    return outs, per_comm


def _rope_tables(pos_row, inv_freq_col, comms=()):
    t_tok = pos_row.shape[1]
    tm = min(512, t_tok)

    def body(pos_ref, invf_ref, cos_ref, sin_ref):
        ang = pos_ref[...].astype(F32) * invf_ref[...]
        row = lax.broadcasted_iota(jnp.int32, ang.shape, 0)
        cos_ref[...] = jnp.cos(ang)
        sin_ref[...] = jnp.sin(ang) * jnp.where(row < HEAD_DIM // 2, -1.0, 1.0)

    return _carry(
        body, name="rope_tables", grid=(t_tok // tm,), comms=comms,
        in_specs=[pl.BlockSpec((1, tm), lambda i: (0, i)), pl.BlockSpec((HEAD_DIM, 1), lambda i: (0, 0))],
        out_specs=[pl.BlockSpec((HEAD_DIM, tm), lambda i: (0, i))] * 2,
        out_shape=[jax.ShapeDtypeStruct((HEAD_DIM, t_tok), F32)] * 2,
        args=(pos_row, inv_freq_col))


def _proj_in(x2, w_in_t, comms=()):
    t_tok, d = x2.shape
    d_in = w_in_t.shape[0]
    tm = min(512, t_tok)

    def body(x_ref, w_ref, h_ref, xb_ref):
        xb = x_ref[...].astype(BF16)
        xb_ref[...] = xb
        h_ref[...] = _dot(w_ref[...], xb, NT)

    return _carry(
        body, name="proj_in", grid=(t_tok // tm,), comms=comms,
        in_specs=[pl.BlockSpec((tm, d), lambda i: (i, 0)), pl.BlockSpec((d_in, d), lambda i: (0, 0))],
        out_specs=[pl.BlockSpec((d_in, tm), lambda i: (0, i)), pl.BlockSpec((tm, d), lambda i: (i, 0))],
        out_shape=[jax.ShapeDtypeStruct((d_in, t_tok), F32), jax.ShapeDtypeStruct((t_tok, d), BF16)],
        args=(x2, w_in_t))


MIX_BLOCKS = 2
MIX_W = MIX_BLOCKS * BLK


def _prev_block(i):
    return jnp.maximum(MIX_BLOCKS * i - 1, 0)


def _h_specs():
    kv_row = COL_K // (2 * D_KV)
    return [
        pl.BlockSpec((D_GMLP, MIX_W), lambda i: (0, i)),
        pl.BlockSpec((D_GMLP, MIX_W), lambda i: (1, i)),
        pl.BlockSpec((D_ATTN, MIX_W), lambda i: (2, i)),
        pl.BlockSpec((2 * D_KV, MIX_W), lambda i: (kv_row, i)),
        pl.BlockSpec((2 * D_KV, BLK), lambda i: (kv_row, _prev_block(i))),
    ]


def _table_specs():
    return [
        pl.BlockSpec((HEAD_DIM, MIX_W), lambda i: (0, i)),
        pl.BlockSpec((HEAD_DIM, MIX_W), lambda i: (0, i)),
        pl.BlockSpec((HEAD_DIM, BLK), lambda i: (0, _prev_block(i))),
        pl.BlockSpec((HEAD_DIM, BLK), lambda i: (0, _prev_block(i))),
    ]


def _cols(b):
    return slice(b * BLK, (b + 1) * BLK)


LSE_ROWS = 8
LSE_SPEC = pl.BlockSpec((LSE_ROWS, D_ATTN), lambda i: (i, 0))


def _block_inputs(b, i, kvc, kvp_ref, cos, sin, cosp_ref, sinp_ref, bias_ref):
    if b == 0:
        kv_prev, cos_prev, sin_prev, bias = kvp_ref[...], cosp_ref[...], sinp_ref[...], bias_ref[jnp.minimum(i, 1)]
    else:
        kv_prev, cos_prev, sin_prev, bias = kvc[:, _cols(b - 1)], cos[:, _cols(b - 1)], sin[:, _cols(b - 1)], bias_ref[1]
    return kvc[:, _cols(b)], kv_prev, cos[:, _cols(b)], sin[:, _cols(b)], cos_prev, sin_prev, bias


def _band_bias():
    ki = lax.broadcasted_iota(jnp.int32, (2, 2 * BLK, BLK), 1)
    qi = lax.broadcasted_iota(jnp.int32, (2, 2 * BLK, BLK), 2)
    later = lax.broadcasted_iota(jnp.int32, (2, 2 * BLK, BLK), 0) > 0
    dist = qi + BLK - ki
    return jnp.where((dist >= 0) & (dist < BLK) & ((ki >= BLK) | later), 0.0, NEG_INF).astype(F32)


BIAS_SPEC = pl.BlockSpec((2, 2 * BLK, BLK), lambda i: (0, 0, 0))


def _keys_values(kvc, kvp, cosc, sinc, cosp, sinp):
    kp, kc = _rope_t(kvp[:D_KV], cosp, sinp), _rope_t(kvc[:D_KV], cosc, sinc)
    k_t = jnp.concatenate([kp, kc], axis=1).astype(BF16)
    k_n = jnp.concatenate([kp.T, kc.T], axis=0).astype(BF16)
    v_t = jnp.concatenate([kvp[D_KV:], kvc[D_KV:]], axis=1).astype(BF16)
    return k_t, k_n, v_t


def _pad_head(th, kv):
    z = jnp.zeros_like(th)
    return jnp.concatenate([th, z] if kv == 0 else [z, th], axis=0)


def _group_lanes(parts):
    return jnp.concatenate(parts, axis=1)


def _softmax_sink_t(s, sink):
    m = jnp.maximum(jnp.max(s, axis=0, keepdims=True), sink)
    e = jnp.exp(s - m)
    denom = jnp.sum(e, axis=0, keepdims=True) + jnp.exp(sink - m)
    return e * (1.0 / denom), m + jnp.log(denom)


def _causal():
    row = lax.broadcasted_iota(jnp.int32, (BLK, BLK), 0)
    col = lax.broadcasted_iota(jnp.int32, (BLK, BLK), 1)
    return row >= col


def _mask_w_once(wsp_ref, wm_scr):
    @pl.when(pl.program_id(0) == 0)
    def _():
        causal = _causal()
        for hh in range(N_HEADS):
            wm_scr[hh] = jnp.where(causal, wsp_ref[hh], 0.0).astype(BF16)


def _mixer_fwd(h_t, cos_t, sin_t, w_spatial, b_spatial, vln_g, vln_b, sinks, band_bias, comms=()):
    t_tok = h_t.shape[1]
    group = N_HEADS // N_KV_HEADS

    def body(sinks_ref, u_ref, vg_ref, q_ref, kvc_ref, kvp_ref, cos_ref, sin_ref, cosp_ref, sinp_ref,
             wsp_ref, bsp_ref, g_ref, b_ref, bias_ref, cat_ref, lse_ref, wm_scr):
        i = pl.program_id(0)
        _mask_w_once(wsp_ref, wm_scr)
        lse_ref[...] = jnp.zeros_like(lse_ref)
        ua = _gelu(u_ref[...])
        vp, _, _ = _ln_fwd_t(_gelu(vg_ref[...]), g_ref[...], b_ref[...])
        vpb = vp.astype(BF16)
        for b in range(MIX_BLOCKS):
            for hh in range(N_HEADS):
                rows = slice(hh * HEAD_DIM, (hh + 1) * HEAD_DIM)
                mixed = _dot(vpb[rows, _cols(b)], wm_scr[hh], NT) + bsp_ref[hh:hh + 1, :]
                cat_ref[rows, _cols(b)] = (ua[rows, _cols(b)] * mixed).astype(BF16)

        kvc, cos, sin = kvc_ref[...], cos_ref[...], sin_ref[...]
        qr = (_rope_t(q_ref[...], cos, sin) * SCORE_SCALE).astype(BF16)
        sinks4 = [_group_lanes([jnp.full((1, BLK), sinks_ref[hh], F32) for hh in range(kv * group, (kv + 1) * group)])
                  for kv in range(N_KV_HEADS)]
        for b in range(MIX_BLOCKS):
            kv_cur, kv_prev, cosc, sinc, cosp, sinp, bias1 = _block_inputs(b, i, kvc, kvp_ref, cos, sin, cosp_ref, sinp_ref, bias_ref)
            _, k_n, v_t = _keys_values(kv_cur, kv_prev, cosc, sinc, cosp, sinp)
            bias = _group_lanes([bias1] * group)
            for kv in range(N_KV_HEADS):
                heads = range(kv * group, (kv + 1) * group)
                qs = _group_lanes([qr[hh * HEAD_DIM:(hh + 1) * HEAD_DIM, _cols(b)] for hh in heads])
                p, lse = _softmax_sink_t(_dot(k_n, _pad_head(qs, kv)) + bias, sinks4[kv])
                lse_ref[b * N_KV_HEADS + kv:b * N_KV_HEADS + kv + 1, :] = lse
                o = _dot(v_t[kv * HEAD_DIM:(kv + 1) * HEAD_DIM], p.astype(BF16)).astype(BF16)
                for j, hh in enumerate(heads):
                    cat_ref[D_GMLP + hh * HEAD_DIM:D_GMLP + (hh + 1) * HEAD_DIM, _cols(b)] = o[:, j * BLK:(j + 1) * BLK]

    full = lambda shape: pl.BlockSpec(shape, lambda i: (0,) * len(shape))
    return _carry(
        body, name="mixer_fwd", grid=(t_tok // MIX_W,), comms=comms,
        in_specs=[pl.BlockSpec(memory_space=pltpu.SMEM)] + _h_specs() + _table_specs() + [
            full((N_HEADS, BLK, BLK)), full((N_HEADS, BLK)), full((D_GMLP, 1)), full((D_GMLP, 1)), BIAS_SPEC],
        out_specs=[pl.BlockSpec((D_GMLP + D_ATTN, MIX_W), lambda i: (0, i)), LSE_SPEC],
        out_shape=[jax.ShapeDtypeStruct((D_GMLP + D_ATTN, t_tok), BF16),
                   jax.ShapeDtypeStruct((t_tok // MIX_W * LSE_ROWS, D_ATTN), F32)],
        scratch_shapes=[pltpu.VMEM((N_HEADS, BLK, BLK), BF16)],
        args=(sinks, h_t, h_t, h_t, h_t, h_t, cos_t, sin_t, cos_t, sin_t, w_spatial, b_spatial, vln_g, vln_b, band_bias))


def _proj_out(cat_t, x2, w_out_b, ln1_g, ln1_b, comms=()):
    t_tok, d = x2.shape
    tm = min(512, t_tok)

    def body(cat_ref, x_ref, w_ref, g_ref, b_ref, xhat_ref, rstd_ref, x1b_ref):
        x1, xhat, rstd = _ln_fwd(ALPHA * x_ref[...] + _dot(cat_ref[...], w_ref[...], TN), g_ref[...], b_ref[...])
        xhat_ref[...] = xhat
        rstd_ref[...] = rstd
        x1b_ref[...] = x1.astype(BF16)

    tok = lambda w: pl.BlockSpec((tm, w), lambda i: (i, 0))
    vec = pl.BlockSpec((1, d), lambda i: (0, 0))
    return _carry(
        body, name="proj_out", grid=(t_tok // tm,), comms=comms,
        in_specs=[pl.BlockSpec((cat_t.shape[0], tm), lambda i: (0, i)), tok(d), pl.BlockSpec(w_out_b.shape, lambda i: (0, 0)), vec, vec],
        out_specs=[tok(d), tok(1), tok(d)],
        out_shape=[jax.ShapeDtypeStruct((t_tok, d), F32), jax.ShapeDtypeStruct((t_tok, 1), F32), jax.ShapeDtypeStruct((t_tok, d), BF16)],
        args=(cat_t, x2, w_out_b, ln1_g, ln1_b))


def _ffn_fwd_bwd(xhat1, rstd1, x1b, target, w1_parts, w2_parts, ln1_g, ln1_b, ln2_g, ln2_b):
    t_tok, d = xhat1.shape
    n_part = len(w1_parts)
    n_chunk, _, fp = w1_parts[0].shape
    f = n_chunk * n_part * fp
    tm = min(FFN_ROWS, t_tok)

    def body(xhat1_ref, rstd1_ref, x1b_ref, tgt_ref, *refs):
        w1_hbm, w2_hbm = refs[:n_part], refs[n_part:2 * n_part]
        (g1_ref, b1_ref, g2_ref, b2_ref, act_ref, dpre_ref, dz2b_ref, dz1_ref, stats_ref,
         r_scr, w1_ref, w2_ref, w_sems) = refs[2 * n_part:]

        @pl.when(pl.program_id(0) == 0)
        def _():
            stats_ref[...] = jnp.zeros_like(stats_ref)
            loads = []
            for j in range(n_chunk):
                for p in range(n_part):
                    units = pl.ds((j * n_part + p) * fp, fp)
                    loads.append(pltpu.make_async_copy(w1_hbm[p].at[j], w1_ref.at[:, units], w_sems.at[0, p, j]))
                    loads.append(pltpu.make_async_copy(w2_hbm[p].at[j], w2_ref.at[units, :], w_sems.at[1, p, j]))
            for cp in loads:
                cp.start()
            for cp in loads:
                cp.wait()

        g1, g2 = g1_ref[...], g2_ref[...]
        xhat1 = xhat1_ref[...]
        r_scr[...] = jnp.maximum(_dot(x1b_ref[...], w1_ref[...]), 0.0)
        r = r_scr[...]
        act = (r * r).astype(BF16)
        act_ref[...] = act
        ff = _dot(act, w2_ref[...])
        y, xhat2, rstd2 = _ln_fwd(ALPHA * (xhat1 * g1 + b1_ref[...]) + ff, g2, b2_ref[...])
        diff = y - tgt_ref[...]
        loss = 0.5 * jnp.sum(jnp.sum(diff * diff, axis=-1, keepdims=True) / d, axis=0, keepdims=True)
        dy = diff / d
        dz2 = _ln_bwd(dy, xhat2, rstd2, g2)
        dz2b = dz2.astype(BF16)
        dz2b_ref[...] = dz2b
        dpre = (_dot(dz2b, w2_ref[...], NT) * (2.0 * r_scr[...])).astype(BF16)
        dpre_ref[...] = dpre
        dx1 = ALPHA * dz2 + _dot(dpre, w1_ref[...], NT)
        dz1_ref[...] = _ln_bwd(dx1, xhat1, rstd1_ref[...], g1)
        stats_ref[0:1, :] += jnp.sum(dx1 * xhat1, axis=0, keepdims=True)
        stats_ref[1:2, :] += jnp.sum(dx1, axis=0, keepdims=True)
        stats_ref[2:3, :] += jnp.sum(dy * xhat2, axis=0, keepdims=True)
        stats_ref[3:4, :] += jnp.sum(dy, axis=0, keepdims=True)
        stats_ref[4:5, :] += jnp.broadcast_to(loss, (1, d))

    tok = lambda w: pl.BlockSpec((tm, w), lambda i: (i, 0))
    vec = pl.BlockSpec((1, d), lambda i: (0, 0))
    return _carry(
        body, name="ffn_fwd_bwd", grid=(t_tok // tm,),
        in_specs=[tok(d), tok(1), tok(d), tok(d)] + [ANY] * (2 * n_part) + [vec, vec, vec, vec],
        out_specs=[tok(f), tok(f), tok(d), tok(d), pl.BlockSpec((8, d), lambda i: (0, 0))],
        out_shape=[jax.ShapeDtypeStruct((t_tok, f), BF16), jax.ShapeDtypeStruct((t_tok, f), BF16),
                   jax.ShapeDtypeStruct((t_tok, d), BF16), jax.ShapeDtypeStruct((t_tok, d), F32), jax.ShapeDtypeStruct((8, d), F32)],
        scratch_shapes=[pltpu.VMEM((tm, f), F32), pltpu.VMEM((d, f), BF16), pltpu.VMEM((f, d), BF16),
                        pltpu.SemaphoreType.DMA((2, n_part, n_chunk))],
        args=(xhat1, rstd1, x1b, target, *w1_parts, *w2_parts, ln1_g, ln1_b, ln2_g, ln2_b))[0]


def _wgrad_shard(s, cc):
    half = N_DEV // 2
    return 2 * (s % half) + jnp.where(s < half, 1 - cc[0], cc[0])


def _wgrad_pair_sum(name, product, chunk, in_specs, args, core_chip, comms=(), scratch_shapes=()):
    half = N_DEV // 2
    n_in = len(in_specs)

    def body(cc_ref, *refs):
        ins, (wire_ref, own_ref, recv_ref, send_buf, got, send_sems, recv_sems, got_sem) = refs[:n_in], refs[n_in:n_in + 8]
        s = pl.program_id(0)
        x, y, c = _place()
        def send(q):
            return pltpu.make_async_remote_copy(
                src_ref=send_buf.at[q % 2], dst_ref=recv_ref.at[q], send_sem=send_sems.at[q], recv_sem=recv_sems.at[q],
                device_id=(x, y, 1 - c), device_id_type=MESH)

        def load(q):
            return pltpu.make_async_copy(recv_ref.at[q], got, got_sem.at[0])

        @pl.when(s >= half)
        def _():
            send(s - half).wait_recv()
            load(s - half).start()

        g = product(_wgrad_shard(s, cc_ref), *ins, *refs[n_in + 8:])

        for q in range(half):
            @pl.when(s == q)
            def _(q=q):
                if q >= 2:
                    send(q - 2).wait_send()
                send_buf[q % 2] = g
                send(q).start()

            @pl.when(s == half + q)
            def _(q=q):
                load(q).wait()
                total = g + got[...]
                wire_ref[...] = total.astype(BF16)

                @pl.when(cc_ref[1] == q)
                def _():
                    own_ref[...] = total

        @pl.when(s == N_DEV - 1)
        def _():
            for q in range(half - 2, half):
                send(q).wait_send()

    (wire, own, _), per_comm = _carry(
        body, name=name, grid=(N_DEV,), comms=comms, prefetch=(core_chip,), in_specs=in_specs,
        out_specs=[pl.BlockSpec((None,) + chunk, lambda s, cc: (jnp.maximum(s - half, 0), 0, 0)),
                   pl.BlockSpec(chunk, lambda s, cc: (0, 0)), ANY],
        out_shape=[jax.ShapeDtypeStruct((half,) + chunk, BF16), jax.ShapeDtypeStruct(chunk, F32),
                   jax.ShapeDtypeStruct((half,) + chunk, F32)],
        scratch_shapes=[pltpu.VMEM((2,) + chunk, F32), pltpu.VMEM(chunk, F32), pltpu.SemaphoreType.DMA((half,)),
                        pltpu.SemaphoreType.DMA((half,)), pltpu.SemaphoreType.DMA((1,)), *scratch_shapes],
        args=args)
    return wire, own, per_comm


def _resident(a):
    return pl.BlockSpec(a.shape, lambda s, cc: (0,) * a.ndim, pipeline_mode=pl.Buffered(1))


def _ffn_wgrad(name, lhs, rhs, chunk_lhs, core_chip, comms=()):
    t_tok = lhs.shape[0]
    fc = (lhs if chunk_lhs else rhs).shape[1] // N_DEV
    chunked = pl.BlockSpec((t_tok, fc), lambda s, cc: (0, _wgrad_shard(s, cc)))

    def product(shard, lhs_ref, rhs_ref):
        return _dot(lhs_ref[...], rhs_ref[...], TN)

    return _wgrad_pair_sum(
        name, product, (fc, rhs.shape[1]) if chunk_lhs else (lhs.shape[1], fc),
        [chunked, _resident(rhs)] if chunk_lhs else [_resident(lhs), chunked], (lhs, rhs), core_chip, comms)


def _proj_out_bwd(dz1, cat_t, w_out_b, comms=()):
    t_tok, d = dz1.shape
    d_mix = cat_t.shape[0]
    tm = min(512, t_tok)

    def body(dz1_ref, cat_ref, w_ref, dcat_ref, gw_ref):
        @pl.when(pl.program_id(0) == 0)
        def _():
            gw_ref[...] = jnp.zeros_like(gw_ref)

        dzb = dz1_ref[...].astype(BF16)
        dcat_ref[...] = _dot(w_ref[...], dzb, NT)
        gw_ref[...] += _dot(cat_ref[...], dzb)

    return _carry(
        body, name="proj_out_bwd", grid=(t_tok // tm,), comms=comms,
        in_specs=[pl.BlockSpec((tm, d), lambda i: (i, 0)), pl.BlockSpec((d_mix, tm), lambda i: (0, i)),
                  pl.BlockSpec((d_mix, d), lambda i: (0, 0))],
        out_specs=[pl.BlockSpec((d_mix, tm), lambda i: (0, i)), pl.BlockSpec((d_mix, d), lambda i: (0, 0))],
        out_shape=[jax.ShapeDtypeStruct((d_mix, t_tok), F32), jax.ShapeDtypeStruct((d_mix, d), F32)],
        args=(dz1, cat_t, w_out_b))


def _mixer_bwd(dcat_t, h_t, cos_t, sin_t, w_spatial, b_spatial, vln_g, vln_b, sinks, band_bias, lse, comms=()):
    t_tok = h_t.shape[1]
    nb, n_step = t_tok // BLK, t_tok // MIX_W
    group = N_HEADS // N_KV_HEADS

    def body(sinks_ref, dcat_ref, u_ref, vg_ref, q_ref, kvc_ref, kvp_ref, cos_ref, sin_ref, cosp_ref, sinp_ref,
             wsp_ref, bsp_ref, g_ref, b_ref, bias_ref, lse_ref, dh_ref, dkvc_ref, dkvp_ref, gwsb_ref, gbsp_ref, gvln_ref, gsink_ref,
             dg_acc, db_acc, wm_scr, gws_ref):
        i = pl.program_id(0)

        @pl.when(i == 0)
        def _():
            gws_ref[...] = jnp.zeros_like(gws_ref)
            gbsp_ref[...] = jnp.zeros_like(gbsp_ref)
            gsink_ref[...] = jnp.zeros_like(gsink_ref)
            dg_acc[...] = jnp.zeros_like(dg_acc)
            db_acc[...] = jnp.zeros_like(db_acc)

        _mask_w_once(wsp_ref, wm_scr)

        g = g_ref[...]
        ua, ua_grad = _gelu_and_grad(u_ref[...])
        vv, vv_grad = _gelu_and_grad(vg_ref[...])
        vp, vhat, rstd = _ln_fwd_t(vv, g, b_ref[...])
        vpb = vp.astype(BF16)
        da = dcat_ref[0:D_GMLP, :]
        dmixed = da * ua
        dvp_blocks = []
        for b in range(MIX_BLOCKS):
            dvp_parts = []
            for hh in range(N_HEADS):
                rows = slice(hh * HEAD_DIM, (hh + 1) * HEAD_DIM)
                vpb_h = vpb[rows, _cols(b)]
                mixed = _dot(vpb_h, wm_scr[hh], NT) + bsp_ref[hh:hh + 1, :]
                dh_ref[COL_U + hh * HEAD_DIM:COL_U + (hh + 1) * HEAD_DIM, _cols(b)] = (
                    da[rows, _cols(b)] * mixed * ua_grad[rows, _cols(b)]).astype(BF16)
                dm = dmixed[rows, _cols(b)]
                dmb = dm.astype(BF16)
                gbsp_ref[hh:hh + 1, :] += jnp.sum(dm, axis=0, keepdims=True)
                gws_ref[hh] += _dot(dmb, vpb_h, TN)
                dvp_parts.append(_dot(dmb, wm_scr[hh]))
            dvp_blocks.append(jnp.concatenate(dvp_parts, axis=0))
        dvp = jnp.concatenate(dvp_blocks, axis=1)
        dgv, dbv = dvp * vhat, dvp
        for b in range(MIX_BLOCKS):
            dg_acc[...] += dgv[:, _cols(b)]
            db_acc[...] += dbv[:, _cols(b)]
        dh_ref[COL_V:COL_V + D_GMLP, :] = (_ln_bwd_t(dvp, vhat, rstd, g) * vv_grad).astype(BF16)

        kvc, cos, sin = kvc_ref[...], cos_ref[...], sin_ref[...]
        qr = (_rope_t(q_ref[...], cos, sin) * SCORE_SCALE).astype(BF16)
        sinks4 = [_group_lanes([jnp.full((1, BLK), sinks_ref[hh], F32) for hh in range(kv * group, (kv + 1) * group)])
                  for kv in range(N_KV_HEADS)]
        dq_blocks, dkv_cur, dkv_prev = [], [], []
        for b in range(MIX_BLOCKS):
            kv_cur, kv_prev, cosc, sinc, cosp, sinp, bias1 = _block_inputs(b, i, kvc, kvp_ref, cos, sin, cosp_ref, sinp_ref, bias_ref)
            k_t, k_n, v_t = _keys_values(kv_cur, kv_prev, cosc, sinc, cosp, sinp)
            v_n = jnp.concatenate([kv_prev[D_KV:].T, kv_cur[D_KV:].T], axis=0).astype(BF16)
            bias = _group_lanes([bias1] * group)
            dk, dv, dq_parts = [], [], []
            for kv in range(N_KV_HEADS):
                heads = range(kv * group, (kv + 1) * group)
                kv_rows = slice(kv * HEAD_DIM, (kv + 1) * HEAD_DIM)
                qs = _group_lanes([qr[hh * HEAD_DIM:(hh + 1) * HEAD_DIM, _cols(b)] for hh in heads])
                dos = _group_lanes([dcat_ref[D_GMLP + hh * HEAD_DIM:D_GMLP + (hh + 1) * HEAD_DIM, _cols(b)]
                                    for hh in heads]).astype(BF16)
                lse_g = lse_ref[b * N_KV_HEADS + kv:b * N_KV_HEADS + kv + 1, :]
                p = jnp.exp(_dot(k_n, _pad_head(qs, kv)) + bias - lse_g)
                p_sink = jnp.exp(sinks4[kv] - lse_g)
                dp = _dot(v_n, _pad_head(dos, kv))
                delta = jnp.sum(p * dp, axis=0, keepdims=True)
                ds = (p * (dp - delta)).astype(BF16)
                dsink = p_sink * delta
                dq = _dot(k_t[kv_rows], ds) * SCORE_SCALE
                for j, hh in enumerate(heads):
                    gsink_ref[hh:hh + 1, :] -= dsink[:, j * BLK:(j + 1) * BLK]
                    dq_parts.append(dq[:, j * BLK:(j + 1) * BLK])
                dk.append(_dot(qs, ds, NT))
                dv.append(_dot(dos, p.astype(BF16), NT))
            dq_blocks.append(jnp.concatenate(dq_parts, axis=0))
            dk_all, dv_all = jnp.concatenate(dk, axis=0), jnp.concatenate(dv, axis=0)
            dkv_cur.append(jnp.concatenate([_rope_t(dk_all[:, BLK:], cosc, sinc, bwd=True), dv_all[:, BLK:]], axis=0))
            dkv_prev.append(jnp.concatenate([_rope_t(dk_all[:, :BLK], cosp, sinp, bwd=True), dv_all[:, :BLK]], axis=0))
        dh_ref[COL_Q:COL_Q + D_ATTN, :] = _rope_t(jnp.concatenate(dq_blocks, axis=1), cos, sin, bwd=True).astype(BF16)
        for b in range(MIX_BLOCKS):
            dkvc_ref[:, _cols(b)] = dkv_cur[b] + dkv_prev[b + 1] if b + 1 < MIX_BLOCKS else dkv_cur[b]
        dkvp_ref[...] = dkv_prev[0]

        @pl.when(i == n_step - 1)
        def _():
            causal = _causal()
            for hh in range(N_HEADS):
                gwsb_ref[hh] = jnp.where(causal, gws_ref[hh], 0.0).astype(BF16)
            gvln_ref[...] = jnp.zeros_like(gvln_ref)
            gvln_ref[0:1, :] = jnp.sum(dg_acc[...].T, axis=0, keepdims=True)
            gvln_ref[1:2, :] = jnp.sum(db_acc[...].T, axis=0, keepdims=True)

    full = lambda shape: pl.BlockSpec(shape, lambda i: (0,) * len(shape))
    return _carry(
        body, name="mixer_bwd", grid=(n_step,), comms=comms,
        in_specs=[pl.BlockSpec(memory_space=pltpu.SMEM), pl.BlockSpec((D_GMLP + D_ATTN, MIX_W), lambda i: (0, i))]
        + _h_specs() + _table_specs()
        + [full((N_HEADS, BLK, BLK)), full((N_HEADS, BLK)), full((D_GMLP, 1)), full((D_GMLP, 1)), BIAS_SPEC, LSE_SPEC],
        out_specs=[pl.BlockSpec((COL_K, MIX_W), lambda i: (0, i)), pl.BlockSpec((2 * D_KV, MIX_W), lambda i: (0, i)),
                   pl.BlockSpec((2 * D_KV, BLK), lambda i: (0, (i + n_step - 1) % n_step)),
                   full((N_HEADS, BLK, BLK)), full((N_HEADS, BLK)), full((8, D_GMLP)), full((N_HEADS, LANES))],
        out_shape=[jax.ShapeDtypeStruct((COL_K, t_tok), BF16), jax.ShapeDtypeStruct((2 * D_KV, t_tok), F32),
                   jax.ShapeDtypeStruct((2 * D_KV, n_step * BLK), F32),
                   jax.ShapeDtypeStruct((N_HEADS, BLK, BLK), BF16), jax.ShapeDtypeStruct((N_HEADS, BLK), F32),
                   jax.ShapeDtypeStruct((8, D_GMLP), F32), jax.ShapeDtypeStruct((N_HEADS, LANES), F32)],
        scratch_shapes=[pltpu.VMEM((D_GMLP, BLK), F32), pltpu.VMEM((D_GMLP, BLK), F32), pltpu.VMEM((N_HEADS, BLK, BLK), BF16),
                        pltpu.VMEM((N_HEADS, BLK, BLK), F32)],
        args=(sinks, dcat_t, h_t, h_t, h_t, h_t, h_t, cos_t, sin_t, cos_t, sin_t, w_spatial, b_spatial, vln_g, vln_b, band_bias, lse))


def _dkv_rows(dkvc_ref, dkvp_ref, width, store):
    for s in range(width // MIX_W):
        rest, last = slice(s * MIX_W, (s + 1) * MIX_W - BLK), slice((s + 1) * MIX_W - BLK, (s + 1) * MIX_W)
        store(rest, dkvc_ref[:, rest].astype(BF16))
        store(last, (dkvc_ref[:, last] + dkvp_ref[:, _cols(s)]).astype(BF16))


def _proj_in_wgrad(dh_b, dkvc_t, dkvp_t, xb, core_chip, comms=()):
    t_tok, d = xb.shape
    d_main, d_kv = dh_b.shape[0], dkvc_t.shape[0]
    rows = (d_main + d_kv) // N_DEV
    whole, cut = d_main // rows, d_main % rows

    def product(shard, dh_ref, dkvc_ref, dkvp_ref, xb_ref, dht_scr, sems):
        copies = [pltpu.make_async_copy(dh_ref.at[j * rows:(j + 1) * rows], dht_scr.at[j], sems.at[j]) for j in range(whole)]
        copies.append(pltpu.make_async_copy(dh_ref.at[whole * rows:d_main], dht_scr.at[whole, 0:cut], sems.at[whole]))

        @pl.when(pl.program_id(0) == 0)
        def _():
            for cp in copies:
                cp.start()

            def store(cols, val):
                dht_scr[whole, cut:rows, cols] = val[0:rows - cut]
                dht_scr[whole + 1, :, cols] = val[rows - cut:]

            _dkv_rows(dkvc_ref, dkvp_ref, t_tok, store)
            for cp in copies:
                cp.wait()

        return _dot(dht_scr[shard], xb_ref[...])

    return _wgrad_pair_sum(
        "proj_in_wgrad", product, (rows, d), [ANY, _resident(dkvc_t), _resident(dkvp_t), _resident(xb)],
        (dh_b, dkvc_t, dkvp_t, xb), core_chip, comms,
        scratch_shapes=[pltpu.VMEM((N_DEV, rows, t_tok), BF16), pltpu.SemaphoreType.DMA((whole + 1,))])


def _proj_in_dgrad(dh_b, dkvc_t, dkvp_t, dz1, w_in_t, comms=()):
    t_tok, d = dz1.shape
    d_main, d_kv = dh_b.shape[0], dkvc_t.shape[0]
    tm = min(512, t_tok)

    def body(dh_ref, dkvc_ref, dkvp_ref, dz1_ref, w_ref, dx_ref, dkv_scr):
        def store(cols, val):
            dkv_scr[:, cols] = val

        _dkv_rows(dkvc_ref, dkvp_ref, tm, store)
        dx_ref[...] = (ALPHA * dz1_ref[...] + _dot(dh_ref[...], w_ref[0:d_main, :], TN)
                       + _dot(dkv_scr[...], w_ref[d_main:, :], TN))

    return _carry(
        body, name="proj_in_dgrad", grid=(t_tok // tm,), comms=comms,
        in_specs=[pl.BlockSpec((d_main, tm), lambda i: (0, i)), pl.BlockSpec((d_kv, tm), lambda i: (0, i)),
                  pl.BlockSpec((d_kv, tm // MIX_BLOCKS), lambda i: (0, i)),
                  pl.BlockSpec((tm, d), lambda i: (i, 0)), pl.BlockSpec((d_main + d_kv, d), lambda i: (0, 0))],
        out_specs=[pl.BlockSpec((tm, d), lambda i: (i, 0))],
        out_shape=[jax.ShapeDtypeStruct((t_tok, d), F32)],
        scratch_shapes=[pltpu.VMEM((d_kv, tm), BF16)],
        args=(dh_b, dkvc_t, dkvp_t, dz1, w_in_t))


def _adamw(w, g, m, v):
    m = ADAM_B1 * m + (1.0 - ADAM_B1) * g
    v = ADAM_B2 * v + (1.0 - ADAM_B2) * (g * g)
    m_hat = m / (1.0 - ADAM_B1 ** ADAM_STEP)
    v_hat = v / (1.0 - ADAM_B2 ** ADAM_STEP)
    delta = -ADAM_LR * (m_hat / (jnp.sqrt(v_hat) + ADAM_EPS) + ADAM_WD * w)
    return delta, m, v


ADAMW_STEPS = 4


def _adamw_shards(name, items, comms=(), rider=None):
    n_in, n_out = 5 * len(items), 4 * len(items)
    n_rin = len(rider["args"]) if rider else 0

    def body(*refs):
        ins, rins, outs, routs = refs[:n_in], refs[n_in:n_in + n_rin], refs[n_in + n_rin:n_in + n_rin + n_out], refs[n_in + n_rin + n_out:]
        for i in range(len(items)):
            own_ref, recv_ref, w_ref, m_ref, v_ref = ins[5 * i:5 * i + 5]
            g = ((own_ref[...] + recv_ref[0].astype(F32)) + recv_ref[1].astype(F32)) + recv_ref[2].astype(F32)
            for o_ref, val in zip(outs[4 * i:4 * i + 4], (g,) + _adamw(w_ref[...], g, m_ref[...], v_ref[...])):
                o_ref[...] = val
        if rider:
            pl.when(pl.program_id(0) == 0)(lambda: rider["body"](rins, routs))

    in_specs, out_specs, out_shape, args = [], [], [], []
    for own, recv, w, m, v in items:
        r, c = own.shape
        tiles = ADAMW_STEPS
        while (r // tiles) % BF16_ROWS:
            tiles //= 2
        blk = pl.BlockSpec((r // tiles, c), lambda s, k=ADAMW_STEPS // tiles: (s // k, 0))
        in_specs += [blk, pl.BlockSpec((3, r // tiles, c), lambda s, k=ADAMW_STEPS // tiles: (0, s // k, 0)), blk, blk, blk]
        out_specs += [blk] * 4
        out_shape += [jax.ShapeDtypeStruct((r, c), F32)] * 4
        args += [own, recv, w, m, v]
    if rider:
        in_specs, out_specs = in_specs + rider["in_specs"], out_specs + rider["out_specs"]
        out_shape, args = out_shape + rider["out_shape"], args + rider["args"]
    res, per_comm = _carry(body, name=name, grid=(ADAMW_STEPS,), comms=comms, in_specs=in_specs, out_specs=out_specs,
                           out_shape=out_shape, args=args)
    return [res[4 * i:4 * i + 4] for i in range(len(items))], res[n_out:], per_comm


VEC_VLN, VEC_LN1G, VEC_LN1B, VEC_LN2G, VEC_LN2B, VEC_SINK, VEC_LOSS, VEC_BSP, VEC_ROWS = 0, 1, 2, 3, 4, 5, 6, 8, 16


def _adamw_small(parts_w, parts_vec, params):
    n = parts_w.shape[0]
    flat = [a for p in params for a in p]
    shapes = [p[0].shape for p in params]

    def grads(gw, gv):
        return [gw, gv[VEC_VLN:VEC_VLN + 1, 0:D_GMLP], gv[VEC_VLN:VEC_VLN + 1, D_GMLP:2 * D_GMLP],
                gv[VEC_BSP:VEC_BSP + N_HEADS, 0:BLK], gv[VEC_LN1G:VEC_LN1G + 1], gv[VEC_LN1B:VEC_LN1B + 1],
                gv[VEC_LN2G:VEC_LN2G + 1], gv[VEC_LN2B:VEC_LN2B + 1], gv[VEC_SINK:VEC_SINK + 1, 0:N_HEADS]]

    def body(ins, outs):
        (pw_ref, pv_ref), ins = ins[:2], ins[2:]
        gw, gv = pw_ref[0].astype(F32), pv_ref[0]
        for k in range(1, n):
            gw, gv = gw + pw_ref[k].astype(F32), gv + pv_ref[k]
        for i, g in enumerate(grads(gw, gv)):
            w_ref, m_ref, v_ref = ins[3 * i:3 * i + 3]
            delta, m_new, v_new = _adamw(w_ref[...], g, m_ref[...], v_ref[...])
            for o_ref, val in zip(outs[4 * i:4 * i + 4], (g, delta, m_new, v_new)):
                o_ref[...] = val
        outs[-1][...] = gv[VEC_LOSS:VEC_LOSS + 1, 0:LANES]

    whole = lambda shape, **kw: pl.BlockSpec(shape, lambda i: (0,) * len(shape), **kw)
    once = dict(pipeline_mode=pl.Buffered(1))
    return dict(
        body=body, args=[parts_w, parts_vec, *flat],
        in_specs=[whole(parts_w.shape, **once), whole(parts_vec.shape, **once)] + [whole(a.shape, **once) for a in flat],
        out_specs=[whole(s) for s in shapes for _ in range(4)] + [whole((1, LANES))],
        out_shape=[jax.ShapeDtypeStruct(s, F32) for s in shapes for _ in range(4)] + [jax.ShapeDtypeStruct((1, LANES), F32)])


def _pair_sum(name, parts, recv, core_chip, comms=()):
    _, r, c = parts.shape
    tr = r if r <= 512 else 512

    def body(cc_ref, a_ref, b_ref, wire_ref, own_ref):
        s = a_ref[...] + b_ref[...]
        wire_ref[...] = s.astype(BF16)

        @pl.when(pl.program_id(1) == cc_ref[1])
        def _():
            own_ref[...] = s

    return _carry(
        body, name=name, grid=(r // tr, 4), prefetch=(core_chip,), comms=comms,
        in_specs=[pl.BlockSpec((None, tr, c), lambda i, q, cc: (2 * q + cc[0], i, 0)),
                  pl.BlockSpec((None, tr, c), lambda i, q, cc: (q, i, 0))],
        out_specs=[pl.BlockSpec((None, tr, c), lambda i, q, cc: (q, i, 0)), pl.BlockSpec((tr, c), lambda i, q, cc: (i, 0))],
        out_shape=[jax.ShapeDtypeStruct((4, r, c), BF16), jax.ShapeDtypeStruct((r, c), F32)],
        args=(parts, recv))


def kernel(x, positions, w_in, v_ln_g, v_ln_b, w_spatial, b_spatial, sinks, w_out, ln1_g, ln1_b, w_ff1, w_ff2, ln2_g, ln2_b, loss_target, m_w_in, m_v_ln_g, m_v_ln_b, m_w_spatial, m_b_spatial, m_sinks, m_w_out, m_ln1_g, m_ln1_b, m_w_ff1, m_w_ff2, m_ln2_g, m_ln2_b, v_w_in, v_v_ln_g, v_v_ln_b, v_w_spatial, v_b_spatial, v_sinks, v_w_out, v_ln1_g, v_ln1_b, v_w_ff1, v_w_ff2, v_ln2_g, v_ln2_b):
    _, t_tok, d = x.shape
    xi, yi, ci = _place()
    core_chip = jnp.stack([ci, 2 * xi + yi]).astype(jnp.int32)
    x2 = x.reshape(t_tok, d)
    target = loss_target.reshape(t_tok, d)
    inv_freq = ROPE_THETA ** (-jnp.arange(0, HEAD_DIM, 2, dtype=F32) / HEAD_DIM)
    wsp, bsp, sink_vec = w_spatial[0], b_spatial[0], sinks[0]
    vg_col, vb_col = v_ln_g.reshape(D_GMLP, 1), v_ln_b.reshape(D_GMLP, 1)
    big = {"in": w_in[0], "out": w_out[0], "ff1": w_ff1[0], "ff2": w_ff2[0]}
    half1, half2 = big["ff1"].shape[1] // 2, big["ff2"].shape[0] // 2
    w1_mine = [big["ff1"][:, :half1].astype(BF16), big["ff1"][:, half1:].astype(BF16)]
    w2_mine = [big["ff2"][:half2].astype(BF16), big["ff2"][half2:].astype(BF16)]

    (cos_t, sin_t), ((g_in,),) = _rope_tables(
        positions, jnp.tile(inv_freq, 2).reshape(HEAD_DIM, 1), comms=[_gather_comm([big["in"].T.astype(BF16)])])
    w_in_t = g_in.reshape(D_IN, d)
    (h_t, xb), ((g_out, w1_a),) = _proj_in(x2, w_in_t, comms=[_gather_comm([big["out"].astype(BF16), w1_mine[0]])])
    w_out_b = g_out.reshape(-1, d)
    band_bias = _band_bias()
    (cat_t, lse), ((w1_b, w2_a),) = _mixer_fwd(h_t, cos_t, sin_t, wsp, bsp, vg_col, vb_col, sink_vec, band_bias,
                                                comms=[_gather_comm([w1_mine[1], w2_mine[0]])])
    (xhat1, rstd1, x1b), ((w2_b,),) = _proj_out(cat_t, x2, w_out_b, ln1_g, ln1_b, comms=[_gather_comm([w2_mine[1]])])
    act_b, dpre_b, dz2b, dz1, stats = _ffn_fwd_bwd(xhat1, rstd1, x1b, target, [w1_a, w1_b], [w2_a, w2_b], ln1_g, ln1_b, ln2_g, ln2_b)

    (dcat_t, gw_out), _ = _proj_out_bwd(dz1, cat_t, w_out_b)
    p_out = gw_out.reshape(N_DEV, -1, d)
    wire_ff1, own_ff1, ((s_out,),) = _ffn_wgrad("ffn_wgrad1", x1b, dpre_b, False, core_chip, comms=[_sibling_comm([p_out])])
    (wire_out, own_out), _ = _pair_sum("pair_sum_out", p_out, s_out, core_chip)
    wire_ff2, own_ff2, ((r_ff1,),) = _ffn_wgrad("ffn_wgrad2", act_b, dz2b, True, core_chip, comms=[_chips_comm([wire_ff1])])
    (dh_b, dkvc_t, dkvp_t, g_wsp, g_bsp, g_vln, g_sink), ((r_ff2, r_out),) = _mixer_bwd(
        dcat_t, h_t, cos_t, sin_t, wsp, bsp, vg_col, vb_col, sink_vec, band_bias, lse,
        comms=[_chips_comm([wire_ff2, wire_out])])
    sink_row = jnp.pad(g_sink.sum(axis=1).reshape(1, N_HEADS), ((0, 0), (0, d - N_HEADS)))
    small_vec = jnp.concatenate([g_vln[0:2].reshape(1, d), stats[0:4], sink_row, stats[4:5], jnp.zeros((1, d), F32),
                                 jnp.pad(g_bsp, ((0, 0), (0, d - BLK)))], axis=0)
    wire_in, own_in, ((parts_w, parts_vec),) = _proj_in_wgrad(
        dh_b, dkvc_t, dkvp_t, xb, core_chip, comms=[_gather_comm([g_wsp.reshape(-1, BLK), small_vec])])
    (grad_x,), _ = _proj_in_dgrad(dh_b, dkvc_t, dkvp_t, dz1, w_in_t)

    (out_out,), _, ((r_in,),) = _adamw_shards("adamw_out", [(own_out, r_out, big["out"], m_w_out[0], v_w_out[0])],
                                              comms=[_chips_comm([wire_in])])
    small = [(w_spatial, m_w_spatial, v_w_spatial), (v_ln_g, m_v_ln_g, v_v_ln_g), (v_ln_b, m_v_ln_b, v_v_ln_b),
             (b_spatial, m_b_spatial, v_b_spatial), (ln1_g, m_ln1_g, v_ln1_g), (ln1_b, m_ln1_b, v_ln1_b),
             (ln2_g, m_ln2_g, v_ln2_g), (ln2_b, m_ln2_b, v_ln2_b), (sinks, m_sinks, v_sinks)]
    views = [(-1, BLK), None, None, (N_HEADS, BLK)] + [None] * 5
    small_update = _adamw_small(parts_w, parts_vec, [
        tuple(a if vw is None else a.reshape(vw) for a in p) for p, vw in zip(small, views)])
    (ff1_out, ff2_out, in_out_t), small_res, _ = _adamw_shards("adamw_all", [
        (own_ff1, r_ff1, big["ff1"], m_w_ff1[0], v_w_ff1[0]), (own_ff2, r_ff2, big["ff2"], m_w_ff2[0], v_w_ff2[0]),
        (own_in, r_in, big["in"].T, m_w_in[0].T, v_w_in[0].T)], rider=small_update)
    in_out = [o.T for o in in_out_t]
    small_out = [[o.reshape(p[0].shape) for o in small_res[4 * i:4 * i + 4]] for i, p in enumerate(small)]
    loss = small_res[-1][0, 0]

    big_out = {0: in_out, 6: out_out, 9: ff1_out, 10: ff2_out}
    small_slot = {3: 0, 1: 1, 2: 2, 4: 3, 7: 4, 8: 5, 11: 6, 12: 7, 5: 8}
    outs = [loss, grad_x.reshape(x.shape)]
    for kind in range(4):
        for wi in range(13):
            outs.append(big_out[wi][kind][None] if wi in big_out else small_out[small_slot[wi]][kind])
    return tuple(outs)
```

```python
import math

import jax
import jax.numpy as jnp
from jax import lax
from jax.experimental import pallas as pl
from jax.experimental.pallas import tpu as pltpu

F32 = jnp.float32
BF16 = jnp.bfloat16
MESH = pl.DeviceIdType.MESH

HEAD_DIM = 64
N_HEADS = 8
N_KV_HEADS = 2
BLK = 128
D_GMLP = N_HEADS * HEAD_DIM
D_ATTN = N_HEADS * HEAD_DIM
D_KV = N_KV_HEADS * HEAD_DIM
D_IN = 2 * D_GMLP + D_ATTN + 2 * D_KV
COL_U, COL_V, COL_Q, COL_K = 0, D_GMLP, 2 * D_GMLP, 2 * D_GMLP + D_ATTN
ROPE_THETA = 10000.0
LN_EPS = 1e-5
ALPHA = 2.0 ** 0.25
NEG_INF = -1e30
SCORE_SCALE = 1.0 / math.sqrt(HEAD_DIM)
ADAM_LR, ADAM_B1, ADAM_B2, ADAM_EPS, ADAM_WD, ADAM_STEP = 0.001, 0.9, 0.999, 1e-08, 0.01, 10
N_DEV = 8
LANES = 128
VMEM_LIMIT = 56 * 1024 * 1024
FFN_ROWS = 256

NT = (((1,), (1,)), ((), ()))
TN = (((0,), (0,)), ((), ()))


def _params(*sem):
    return pltpu.CompilerParams(dimension_semantics=sem, vmem_limit_bytes=VMEM_LIMIT)


def _dot(a, b, dims=None):
    if dims is None:
        return jnp.dot(a, b, preferred_element_type=F32)
    return lax.dot_general(a, b, dims, preferred_element_type=F32)


def _mean(a):
    return jnp.mean(a, axis=-1, keepdims=True)


def _ln_fwd(z, g, b):
    zc = z - _mean(z)
    rstd = lax.rsqrt(_mean(zc * zc) + LN_EPS)
    xhat = zc * rstd
    return xhat * g + b, xhat, rstd


def _ln_bwd(dy, xhat, rstd, g):
    dxhat = dy * g
    return rstd * (dxhat - _mean(dxhat) - xhat * _mean(dxhat * xhat))


_GELU_C = math.sqrt(2.0 / math.pi)


def _gelu(x):
    t = jnp.tanh(_GELU_C * (x + 0.044715 * (x * x * x)))
    return 0.5 * x * (1.0 + t)


def _gelu_and_grad(x):
    x2 = x * x
    t = jnp.tanh(_GELU_C * (x + 0.044715 * (x2 * x)))
    hx, ht = 0.5 * x, 0.5 * (1.0 + t)
    return x * ht, ht + hx * (1.0 - t * t) * (_GELU_C * (1.0 + 3.0 * 0.044715 * x2))


def _mean0(a):
    return jnp.mean(a, axis=0, keepdims=True)


def _ln_fwd_t(z, g, b):
    zc = z - _mean0(z)
    rstd = lax.rsqrt(_mean0(zc * zc) + LN_EPS)
    xhat = zc * rstd
    return xhat * g + b, xhat, rstd


def _ln_bwd_t(dy, xhat, rstd, g):
    dxhat = dy * g
    return rstd * (dxhat - _mean0(dxhat) - xhat * _mean0(dxhat * xhat))


def _rope_t(t, cos, sin_signed, bwd=False):
    half = HEAD_DIM // 2
    outs = []
    for r in range(0, t.shape[0], HEAD_DIM):
        th = t[r:r + HEAD_DIM]
        sw = jnp.concatenate([th[half:], th[:half]], axis=0) * sin_signed
        outs.append(th * cos - sw if bwd else th * cos + sw)
    return jnp.concatenate(outs, axis=0)


ANY = pl.BlockSpec(memory_space=pl.ANY)
GATHER_PIECES = 4
BF16_ROWS = 16


def _place():
    return lax.axis_index("x"), lax.axis_index("y"), lax.axis_index("c")


class _Comm:
    def __init__(self, ins, outs, sems, start, finish):
        self.ins, self.outs, self.sems, self.start, self.finish = ins, outs, sems, start, finish


def _gather_comm(arrs):
    n = len(arrs)
    pieces = []
    for a, arr in enumerate(arrs):
        k = GATHER_PIECES
        while arr.shape[0] % (k * BF16_ROWS):
            k //= 2
        pieces += [(a, p * (arr.shape[0] // k), arr.shape[0] // k) for p in range(k)]

    def parts(ins, outs, sems):
        send_sems, recv_sems, local_sems = sems
        x, y, c = _place()
        me, sibling = (x, y, c), (x, y, 1 - c)
        chips = [(1 - x, y), (x, 1 - y), (1 - x, 1 - y)]

        def copy(u, k, block, to, local=False):
            a, r0, nr = pieces[u]
            px, py, pc = block
            dst = outs[a].at[4 * px + 2 * py + pc, pl.ds(r0, nr)]
            return pltpu.make_async_remote_copy(
                src_ref=ins[a].at[pl.ds(r0, nr)] if local else dst, dst_ref=dst,
                send_sem=send_sems.at[u, k], recv_sem=recv_sems.at[u, k], device_id=to, device_id_type=MESH)

        mine = [pltpu.make_async_copy(ins[a], outs[a].at[4 * x + 2 * y + c], local_sems.at[a]) for a in range(n)]
        first = []
        for u in range(len(pieces)):
            first.append(copy(u, 0, me, sibling, local=True))
            first += [copy(u, 1 + j, me, (*chip, c), local=True) for j, chip in enumerate(chips)]
        return copy, mine, first, me, sibling, chips, c

    def start(ins, outs, sems):
        _, mine, first, *_ = parts(ins, outs, sems)
        for cp in mine + first:
            cp.start()

    def finish(ins, outs, sems):
        copy, mine, first, me, sibling, chips, c = parts(ins, outs, sems)
        passed = []
        for u in range(len(pieces)):
            for j, chip in enumerate(chips):
                copy(u, 1 + j, (*chip, c), me).wait_recv()
                fwd = copy(u, 4 + j, (*chip, c), sibling)
                fwd.start()
                passed.append(fwd)
        for u in range(len(pieces)):
            copy(u, 0, sibling, me).wait_recv()
            for j, chip in enumerate(chips):
                copy(u, 4 + j, (*chip, 1 - c), me).wait_recv()
        for cp in first + passed:
            cp.wait_send()
        for cp in mine:
            cp.wait()

    return _Comm(list(arrs), [jax.ShapeDtypeStruct((N_DEV,) + a.shape, a.dtype) for a in arrs],
                 [pltpu.SemaphoreType.DMA((len(pieces), 7)), pltpu.SemaphoreType.DMA((len(pieces), 7)),
                  pltpu.SemaphoreType.DMA((n,))], start, finish)


def _sibling_comm(parts):
    n = len(parts)

    def copies(ins, outs, sems):
        x, y, c = _place()
        return [pltpu.make_async_remote_copy(
            src_ref=ins[a].at[2 * q + (1 - c)], dst_ref=outs[a].at[q],
            send_sem=sems[0].at[a, q], recv_sem=sems[1].at[a, q],
            device_id=(x, y, 1 - c), device_id_type=MESH) for a in range(n) for q in range(4)]

    return _Comm(list(parts), [jax.ShapeDtypeStruct((4,) + p.shape[1:], p.dtype) for p in parts],
                 [pltpu.SemaphoreType.DMA((n, 4)), pltpu.SemaphoreType.DMA((n, 4))],
                 lambda *r: [cp.start() for cp in copies(*r)], lambda *r: [cp.wait() for cp in copies(*r)])


def _chips_comm(chip_parts, rows=None):
    n = len(chip_parts)
    r0, nr = (0, None) if rows is None else rows

    def copies(ins, outs, sems):
        x, y, c = _place()
        chips = [(1 - x, y), (x, 1 - y), (1 - x, 1 - y)]
        src = lambda a, q: ins[a].at[q] if rows is None else ins[a].at[q, pl.ds(r0, nr)]
        return [pltpu.make_async_remote_copy(
            src_ref=src(a, 2 * px + py), dst_ref=outs[a].at[k],
            send_sem=sems[0].at[a, k], recv_sem=sems[1].at[a, k],
            device_id=(px, py, c), device_id_type=MESH) for a in range(n) for k, (px, py) in enumerate(chips)]

    shape = lambda p: (3,) + p.shape[1:] if rows is None else (3, nr) + p.shape[2:]
    return _Comm(list(chip_parts), [jax.ShapeDtypeStruct(shape(p), p.dtype) for p in chip_parts],
                 [pltpu.SemaphoreType.DMA((n, 3)), pltpu.SemaphoreType.DMA((n, 3))],
                 lambda *r: [cp.start() for cp in copies(*r)], lambda *r: [cp.wait() for cp in copies(*r)])


def _carry(body, *, name, grid, in_specs, out_specs, out_shape, args, comms=(), scratch_shapes=(), prefetch=()):
    n_pre, n_in, n_out, n_scr = len(prefetch), len(in_specs), len(out_specs), len(scratch_shapes)
    c_ins = [a for cm in comms for a in cm.ins]
    c_outs = [s for cm in comms for s in cm.outs]
    c_sems = [s for cm in comms for s in cm.sems]

    def wrapped(*refs):
        pre, refs = refs[:n_pre], refs[n_pre:]
        ins, refs = refs[:n_in], refs[n_in:]
        cins, refs = refs[:len(c_ins)], refs[len(c_ins):]
        outs, refs = refs[:n_out], refs[n_out:]
        couts, refs = refs[:len(c_outs)], refs[len(c_outs):]
        scr, sems = refs[:n_scr], refs[n_scr:]
        groups, i0, o0, s0 = [], 0, 0, 0
        for cm in comms:
            groups.append((cm, cins[i0:i0 + len(cm.ins)], couts[o0:o0 + len(cm.outs)], sems[s0:s0 + len(cm.sems)]))
            i0, o0, s0 = i0 + len(cm.ins), o0 + len(cm.outs), s0 + len(cm.sems)
        first = pl.program_id(0) == 0
        last = pl.program_id(0) == grid[0] - 1
        for ax in range(1, len(grid)):
            first = first & (pl.program_id(ax) == 0)
            last = last & (pl.program_id(ax) == grid[ax] - 1)
        if comms:
            @pl.when(first)
            def _():
                for cm, ci, co, cs in groups:
                    cm.start(ci, co, cs)
        body(*pre, *ins, *outs, *scr)
        if comms:
            @pl.when(last)
            def _():
                for cm, ci, co, cs in groups:
                    cm.finish(ci, co, cs)

    grid_spec = pltpu.PrefetchScalarGridSpec(
        num_scalar_prefetch=n_pre, grid=grid,
        in_specs=list(in_specs) + [ANY] * len(c_ins), out_specs=list(out_specs) + [ANY] * len(c_outs),
        scratch_shapes=list(scratch_shapes) + c_sems)
    res = pl.pallas_call(
        wrapped, name=name, grid_spec=grid_spec, out_shape=list(out_shape) + c_outs,
        compiler_params=_params(*(["arbitrary"] * len(grid))),
    )(*prefetch, *args, *c_ins)
    outs, rest, per_comm = res[:n_out], res[n_out:], []
    for cm in comms:
        per_comm.append(rest[:len(cm.outs)])
        rest = rest[len(cm.outs):]
    return outs, per_comm


def _rope_tables(pos_row, inv_freq_col, comms=()):
    t_tok = pos_row.shape[1]
    tm = min(512, t_tok)

    def body(pos_ref, invf_ref, cos_ref, sin_ref):
        ang = pos_ref[...].astype(F32) * invf_ref[...]
        row = lax.broadcasted_iota(jnp.int32, ang.shape, 0)
        cos_ref[...] = jnp.cos(ang)
        sin_ref[...] = jnp.sin(ang) * jnp.where(row < HEAD_DIM // 2, -1.0, 1.0)

    return _carry(
        body, name="rope_tables", grid=(t_tok // tm,), comms=comms,
        in_specs=[pl.BlockSpec((1, tm), lambda i: (0, i)), pl.BlockSpec((HEAD_DIM, 1), lambda i: (0, 0))],
        out_specs=[pl.BlockSpec((HEAD_DIM, tm), lambda i: (0, i))] * 2,
        out_shape=[jax.ShapeDtypeStruct((HEAD_DIM, t_tok), F32)] * 2,
        args=(pos_row, inv_freq_col))


def _proj_in(x2, w_in_t, comms=()):
    t_tok, d = x2.shape
    d_in = w_in_t.shape[0]
    tm = min(512, t_tok)

    def body(x_ref, w_ref, h_ref, xb_ref):
        xb = x_ref[...].astype(BF16)
        xb_ref[...] = xb
        h_ref[...] = _dot(w_ref[...], xb, NT)

    return _carry(
        body, name="proj_in", grid=(t_tok // tm,), comms=comms,
        in_specs=[pl.BlockSpec((tm, d), lambda i: (i, 0)), pl.BlockSpec((d_in, d), lambda i: (0, 0))],
        out_specs=[pl.BlockSpec((d_in, tm), lambda i: (0, i)), pl.BlockSpec((tm, d), lambda i: (i, 0))],
        out_shape=[jax.ShapeDtypeStruct((d_in, t_tok), F32), jax.ShapeDtypeStruct((t_tok, d), BF16)],
        args=(x2, w_in_t))


MIX_BLOCKS = 2
MIX_W = MIX_BLOCKS * BLK


def _prev_block(i):
    return jnp.maximum(MIX_BLOCKS * i - 1, 0)


def _h_specs():
    kv_row = COL_K // (2 * D_KV)
    return [
        pl.BlockSpec((D_GMLP, MIX_W), lambda i: (0, i)),
        pl.BlockSpec((D_GMLP, MIX_W), lambda i: (1, i)),
        pl.BlockSpec((D_ATTN, MIX_W), lambda i: (2, i)),
        pl.BlockSpec((2 * D_KV, MIX_W), lambda i: (kv_row, i)),
        pl.BlockSpec((2 * D_KV, BLK), lambda i: (kv_row, _prev_block(i))),
    ]


def _table_specs():
    return [
        pl.BlockSpec((HEAD_DIM, MIX_W), lambda i: (0, i)),
        pl.BlockSpec((HEAD_DIM, MIX_W), lambda i: (0, i)),
        pl.BlockSpec((HEAD_DIM, BLK), lambda i: (0, _prev_block(i))),
        pl.BlockSpec((HEAD_DIM, BLK), lambda i: (0, _prev_block(i))),
    ]


def _cols(b):
    return slice(b * BLK, (b + 1) * BLK)


LSE_ROWS = 8
LSE_SPEC = pl.BlockSpec((LSE_ROWS, D_ATTN), lambda i: (i, 0))


def _block_inputs(b, i, kvc, kvp_ref, cos, sin, cosp_ref, sinp_ref, bias_ref):
    if b == 0:
        kv_prev, cos_prev, sin_prev, bias = kvp_ref[...], cosp_ref[...], sinp_ref[...], bias_ref[jnp.minimum(i, 1)]
    else:
        kv_prev, cos_prev, sin_prev, bias = kvc[:, _cols(b - 1)], cos[:, _cols(b - 1)], sin[:, _cols(b - 1)], bias_ref[1]
    return kvc[:, _cols(b)], kv_prev, cos[:, _cols(b)], sin[:, _cols(b)], cos_prev, sin_prev, bias


def _band_bias():
    ki = lax.broadcasted_iota(jnp.int32, (2, 2 * BLK, BLK), 1)
    qi = lax.broadcasted_iota(jnp.int32, (2, 2 * BLK, BLK), 2)
    later = lax.broadcasted_iota(jnp.int32, (2, 2 * BLK, BLK), 0) > 0
    dist = qi + BLK - ki
    return jnp.where((dist >= 0) & (dist < BLK) & ((ki >= BLK) | later), 0.0, NEG_INF).astype(F32)


BIAS_SPEC = pl.BlockSpec((2, 2 * BLK, BLK), lambda i: (0, 0, 0))


def _keys_values(kvc, kvp, cosc, sinc, cosp, sinp):
    kp, kc = _rope_t(kvp[:D_KV], cosp, sinp), _rope_t(kvc[:D_KV], cosc, sinc)
    k_t = jnp.concatenate([kp, kc], axis=1).astype(BF16)
    k_n = jnp.concatenate([kp.T, kc.T], axis=0).astype(BF16)
    v_t = jnp.concatenate([kvp[D_KV:], kvc[D_KV:]], axis=1).astype(BF16)
    return k_t, k_n, v_t


def _pad_head(th, kv):
    z = jnp.zeros_like(th)
    return jnp.concatenate([th, z] if kv == 0 else [z, th], axis=0)


def _group_lanes(parts):
    return jnp.concatenate(parts, axis=1)


def _softmax_sink_t(s, sink):
    m = jnp.maximum(jnp.max(s, axis=0, keepdims=True), sink)
    e = jnp.exp(s - m)
    denom = jnp.sum(e, axis=0, keepdims=True) + jnp.exp(sink - m)
    return e * (1.0 / denom), m + jnp.log(denom)


def _causal():
    row = lax.broadcasted_iota(jnp.int32, (BLK, BLK), 0)
    col = lax.broadcasted_iota(jnp.int32, (BLK, BLK), 1)
    return row >= col


def _mask_w_once(wsp_ref, wm_scr):
    @pl.when(pl.program_id(0) == 0)
    def _():
        causal = _causal()
        for hh in range(N_HEADS):
            wm_scr[hh] = jnp.where(causal, wsp_ref[hh], 0.0).astype(BF16)


def _mixer_fwd(h_t, cos_t, sin_t, w_spatial, b_spatial, vln_g, vln_b, sinks, band_bias, comms=()):
    t_tok = h_t.shape[1]
    group = N_HEADS // N_KV_HEADS

    def body(sinks_ref, u_ref, vg_ref, q_ref, kvc_ref, kvp_ref, cos_ref, sin_ref, cosp_ref, sinp_ref,
             wsp_ref, bsp_ref, g_ref, b_ref, bias_ref, cat_ref, lse_ref, wm_scr):
        i = pl.program_id(0)
        _mask_w_once(wsp_ref, wm_scr)
        lse_ref[...] = jnp.zeros_like(lse_ref)
        ua = _gelu(u_ref[...])
        vp, _, _ = _ln_fwd_t(_gelu(vg_ref[...]), g_ref[...], b_ref[...])
        vpb = vp.astype(BF16)
        for b in range(MIX_BLOCKS):
            for hh in range(N_HEADS):
                rows = slice(hh * HEAD_DIM, (hh + 1) * HEAD_DIM)
                mixed = _dot(vpb[rows, _cols(b)], wm_scr[hh], NT) + bsp_ref[hh:hh + 1, :]
                cat_ref[rows, _cols(b)] = (ua[rows, _cols(b)] * mixed).astype(BF16)

        kvc, cos, sin = kvc_ref[...], cos_ref[...], sin_ref[...]
        qr = (_rope_t(q_ref[...], cos, sin) * SCORE_SCALE).astype(BF16)
        sinks4 = [_group_lanes([jnp.full((1, BLK), sinks_ref[hh], F32) for hh in range(kv * group, (kv + 1) * group)])
                  for kv in range(N_KV_HEADS)]
        for b in range(MIX_BLOCKS):
            kv_cur, kv_prev, cosc, sinc, cosp, sinp, bias1 = _block_inputs(b, i, kvc, kvp_ref, cos, sin, cosp_ref, sinp_ref, bias_ref)
            _, k_n, v_t = _keys_values(kv_cur, kv_prev, cosc, sinc, cosp, sinp)
            bias = _group_lanes([bias1] * group)
            for kv in range(N_KV_HEADS):
                heads = range(kv * group, (kv + 1) * group)
                qs = _group_lanes([qr[hh * HEAD_DIM:(hh + 1) * HEAD_DIM, _cols(b)] for hh in heads])
                p, lse = _softmax_sink_t(_dot(k_n, _pad_head(qs, kv)) + bias, sinks4[kv])
                lse_ref[b * N_KV_HEADS + kv:b * N_KV_HEADS + kv + 1, :] = lse
                o = _dot(v_t[kv * HEAD_DIM:(kv + 1) * HEAD_DIM], p.astype(BF16)).astype(BF16)
                for j, hh in enumerate(heads):
                    cat_ref[D_GMLP + hh * HEAD_DIM:D_GMLP + (hh + 1) * HEAD_DIM, _cols(b)] = o[:, j * BLK:(j + 1) * BLK]

    full = lambda shape: pl.BlockSpec(shape, lambda i: (0,) * len(shape))
    return _carry(
        body, name="mixer_fwd", grid=(t_tok // MIX_W,), comms=comms,
        in_specs=[pl.BlockSpec(memory_space=pltpu.SMEM)] + _h_specs() + _table_specs() + [
            full((N_HEADS, BLK, BLK)), full((N_HEADS, BLK)), full((D_GMLP, 1)), full((D_GMLP, 1)), BIAS_SPEC],
        out_specs=[pl.BlockSpec((D_GMLP + D_ATTN, MIX_W), lambda i: (0, i)), LSE_SPEC],
        out_shape=[jax.ShapeDtypeStruct((D_GMLP + D_ATTN, t_tok), BF16),
                   jax.ShapeDtypeStruct((t_tok // MIX_W * LSE_ROWS, D_ATTN), F32)],
        scratch_shapes=[pltpu.VMEM((N_HEADS, BLK, BLK), BF16)],
        args=(sinks, h_t, h_t, h_t, h_t, h_t, cos_t, sin_t, cos_t, sin_t, w_spatial, b_spatial, vln_g, vln_b, band_bias))


def _proj_out(cat_t, x2, w_out_b, ln1_g, ln1_b, comms=()):
    t_tok, d = x2.shape
    tm = min(512, t_tok)

    def body(cat_ref, x_ref, w_ref, g_ref, b_ref, xhat_ref, rstd_ref, x1b_ref):
        x1, xhat, rstd = _ln_fwd(ALPHA * x_ref[...] + _dot(cat_ref[...], w_ref[...], TN), g_ref[...], b_ref[...])
        xhat_ref[...] = xhat
        rstd_ref[...] = rstd
        x1b_ref[...] = x1.astype(BF16)

    tok = lambda w: pl.BlockSpec((tm, w), lambda i: (i, 0))
    vec = pl.BlockSpec((1, d), lambda i: (0, 0))
    return _carry(
        body, name="proj_out", grid=(t_tok // tm,), comms=comms,
        in_specs=[pl.BlockSpec((cat_t.shape[0], tm), lambda i: (0, i)), tok(d), pl.BlockSpec(w_out_b.shape, lambda i: (0, 0)), vec, vec],
        out_specs=[tok(d), tok(1), tok(d)],
        out_shape=[jax.ShapeDtypeStruct((t_tok, d), F32), jax.ShapeDtypeStruct((t_tok, 1), F32), jax.ShapeDtypeStruct((t_tok, d), BF16)],
        args=(cat_t, x2, w_out_b, ln1_g, ln1_b))


def _ffn_fwd_bwd(xhat1, rstd1, x1b, target, w1_parts, w2_parts, ln1_g, ln1_b, ln2_g, ln2_b):
    t_tok, d = xhat1.shape
    n_part = len(w1_parts)
    n_chunk, _, fp = w1_parts[0].shape
    f = n_chunk * n_part * fp
    tm = min(FFN_ROWS, t_tok)

    def body(xhat1_ref, rstd1_ref, x1b_ref, tgt_ref, *refs):
        w1_hbm, w2_hbm = refs[:n_part], refs[n_part:2 * n_part]
        (g1_ref, b1_ref, g2_ref, b2_ref, act_ref, dpre_ref, dz2b_ref, dz1_ref, stats_ref,
         r_scr, w1_ref, w2_ref, w_sems) = refs[2 * n_part:]

        @pl.when(pl.program_id(0) == 0)
        def _():
            stats_ref[...] = jnp.zeros_like(stats_ref)
            loads = []
            for j in range(n_chunk):
                for p in range(n_part):
                    units = pl.ds((j * n_part + p) * fp, fp)
                    loads.append(pltpu.make_async_copy(w1_hbm[p].at[j], w1_ref.at[:, units], w_sems.at[0, p, j]))
                    loads.append(pltpu.make_async_copy(w2_hbm[p].at[j], w2_ref.at[units, :], w_sems.at[1, p, j]))
            for cp in loads:
                cp.start()
            for cp in loads:
                cp.wait()

        g1, g2 = g1_ref[...], g2_ref[...]
        xhat1 = xhat1_ref[...]
        r_scr[...] = jnp.maximum(_dot(x1b_ref[...], w1_ref[...]), 0.0)
        r = r_scr[...]
        act = (r * r).astype(BF16)
        act_ref[...] = act
        ff = _dot(act, w2_ref[...])
        y, xhat2, rstd2 = _ln_fwd(ALPHA * (xhat1 * g1 + b1_ref[...]) + ff, g2, b2_ref[...])
        diff = y - tgt_ref[...]
        loss = 0.5 * jnp.sum(jnp.sum(diff * diff, axis=-1, keepdims=True) / d, axis=0, keepdims=True)
        dy = diff / d
        dz2 = _ln_bwd(dy, xhat2, rstd2, g2)
        dz2b = dz2.astype(BF16)
        dz2b_ref[...] = dz2b
        dpre = (_dot(dz2b, w2_ref[...], NT) * (2.0 * r_scr[...])).astype(BF16)
        dpre_ref[...] = dpre
        dx1 = ALPHA * dz2 + _dot(dpre, w1_ref[...], NT)
        dz1_ref[...] = _ln_bwd(dx1, xhat1, rstd1_ref[...], g1)
        stats_ref[0:1, :] += jnp.sum(dx1 * xhat1, axis=0, keepdims=True)
        stats_ref[1:2, :] += jnp.sum(dx1, axis=0, keepdims=True)
        stats_ref[2:3, :] += jnp.sum(dy * xhat2, axis=0, keepdims=True)
        stats_ref[3:4, :] += jnp.sum(dy, axis=0, keepdims=True)
        stats_ref[4:5, :] += jnp.broadcast_to(loss, (1, d))

    tok = lambda w: pl.BlockSpec((tm, w), lambda i: (i, 0))
    vec = pl.BlockSpec((1, d), lambda i: (0, 0))
    return _carry(
        body, name="ffn_fwd_bwd", grid=(t_tok // tm,),
        in_specs=[tok(d), tok(1), tok(d), tok(d)] + [ANY] * (2 * n_part) + [vec, vec, vec, vec],
        out_specs=[tok(f), tok(f), tok(d), tok(d), pl.BlockSpec((8, d), lambda i: (0, 0))],
        out_shape=[jax.ShapeDtypeStruct((t_tok, f), BF16), jax.ShapeDtypeStruct((t_tok, f), BF16),
                   jax.ShapeDtypeStruct((t_tok, d), BF16), jax.ShapeDtypeStruct((t_tok, d), F32), jax.ShapeDtypeStruct((8, d), F32)],
        scratch_shapes=[pltpu.VMEM((tm, f), F32), pltpu.VMEM((d, f), BF16), pltpu.VMEM((f, d), BF16),
                        pltpu.SemaphoreType.DMA((2, n_part, n_chunk))],
        args=(xhat1, rstd1, x1b, target, *w1_parts, *w2_parts, ln1_g, ln1_b, ln2_g, ln2_b))[0]


def _wgrad_shard(s, cc, own_last=False):
    half = N_DEV // 2
    q = jnp.bitwise_xor(cc[1], (s + 1) % half) if own_last else s % half
    return 2 * q + jnp.where(s < half, 1 - cc[0], cc[0])


def _wgrad_pair_sum(name, product, chunk, in_specs, args, core_chip, comms=(), scratch_shapes=(), to_chips=False):
    half = N_DEV // 2
    n_in, n_scr = len(in_specs), 8 if to_chips else 5

    def body(cc_ref, *refs):
        ins, (first_ref, own_ref, recv_ref, send_buf, got, send_sems, recv_sems, got_sem) = refs[:n_in], refs[n_in:n_in + 8]
        s = pl.program_id(0)
        x, y, c = _place()
        def send(q):
            return pltpu.make_async_remote_copy(
                src_ref=send_buf.at[q % 2], dst_ref=recv_ref.at[q], send_sem=send_sems.at[q], recv_sem=recv_sems.at[q],
                device_id=(x, y, 1 - c), device_id_type=MESH)

        def load(q):
            return pltpu.make_async_copy(recv_ref.at[q], got, got_sem.at[0])

        chip_refs = refs[n_in + 8:n_in + 3 + n_scr]

        def to_chip(k):
            wire_buf, chip_send_sems, chip_recv_sems = chip_refs
            flip_x, flip_y = (k + 1) // 2, (k + 1) % 2
            return pltpu.make_async_remote_copy(
                src_ref=wire_buf.at[k], dst_ref=first_ref.at[k], send_sem=chip_send_sems.at[k], recv_sem=chip_recv_sems.at[k],
                device_id=(1 - x if flip_x else x, 1 - y if flip_y else y, c), device_id_type=MESH)

        @pl.when(s >= half)
        def _():
            send(s - half).wait_recv()
            load(s - half).start()

        g = product(_wgrad_shard(s, cc_ref, to_chips), *ins, *refs[n_in + 3 + n_scr:])

        for q in range(half):
            @pl.when(s == q)
            def _(q=q):
                if q >= 2:
                    send(q - 2).wait_send()
                send_buf[q % 2] = g
                send(q).start()

            @pl.when(s == half + q)
            def _(q=q):
                load(q).wait()
                total = g + got[...]
                if to_chips and q < half - 1:
                    chip_refs[0][q] = total.astype(BF16)
                    to_chip(q).start()
                elif to_chips:
                    own_ref[...] = total
                else:
                    first_ref[...] = total.astype(BF16)

                    @pl.when(cc_ref[1] == q)
                    def _():
                        own_ref[...] = total

        @pl.when(s == N_DEV - 1)
        def _():
            for q in range(half - 2, half):
                send(q).wait_send()
            if to_chips:
                for k in range(half - 1):
                    to_chip(k).wait()

    if to_chips:
        first_spec, first_shape = ANY, jax.ShapeDtypeStruct((half - 1,) + chunk, BF16)
        chip_scratch = [pltpu.VMEM((half - 1,) + chunk, BF16), pltpu.SemaphoreType.DMA((half - 1,)),
                        pltpu.SemaphoreType.DMA((half - 1,))]
    else:
        first_spec = pl.BlockSpec((None,) + chunk, lambda s, cc: (jnp.maximum(s - half, 0), 0, 0))
        first_shape, chip_scratch = jax.ShapeDtypeStruct((half,) + chunk, BF16), []
    (first, own, _), per_comm = _carry(
        body, name=name, grid=(N_DEV,), comms=comms, prefetch=(core_chip,), in_specs=in_specs,
        out_specs=[first_spec, pl.BlockSpec(chunk, lambda s, cc: (0, 0)), ANY],
        out_shape=[first_shape, jax.ShapeDtypeStruct(chunk, F32), jax.ShapeDtypeStruct((half,) + chunk, F32)],
        scratch_shapes=[pltpu.VMEM((2,) + chunk, F32), pltpu.VMEM(chunk, F32), pltpu.SemaphoreType.DMA((half,)),
                        pltpu.SemaphoreType.DMA((half,)), pltpu.SemaphoreType.DMA((1,)), *chip_scratch, *scratch_shapes],
        args=args)
    return first, own, per_comm


def _resident(a):
    return pl.BlockSpec(a.shape, lambda s, cc: (0,) * a.ndim, pipeline_mode=pl.Buffered(1))


def _ffn_wgrad(name, lhs, rhs, chunk_lhs, core_chip, comms=()):
    t_tok = lhs.shape[0]
    fc = (lhs if chunk_lhs else rhs).shape[1] // N_DEV
    chunked = pl.BlockSpec((t_tok, fc), lambda s, cc: (0, _wgrad_shard(s, cc)))

    def product(shard, lhs_ref, rhs_ref):
        return _dot(lhs_ref[...], rhs_ref[...], TN)

    return _wgrad_pair_sum(
        name, product, (fc, rhs.shape[1]) if chunk_lhs else (lhs.shape[1], fc),
        [chunked, _resident(rhs)] if chunk_lhs else [_resident(lhs), chunked], (lhs, rhs), core_chip, comms)


def _proj_out_bwd(dz1, cat_t, w_out_b, comms=()):
    t_tok, d = dz1.shape
    d_mix = cat_t.shape[0]
    tm = min(512, t_tok)

    def body(dz1_ref, cat_ref, w_ref, dcat_ref, gw_ref):
        @pl.when(pl.program_id(0) == 0)
        def _():
            gw_ref[...] = jnp.zeros_like(gw_ref)

        dzb = dz1_ref[...].astype(BF16)
        dcat_ref[...] = _dot(w_ref[...], dzb, NT)
        gw_ref[...] += _dot(cat_ref[...], dzb)

    return _carry(
        body, name="proj_out_bwd", grid=(t_tok // tm,), comms=comms,
        in_specs=[pl.BlockSpec((tm, d), lambda i: (i, 0)), pl.BlockSpec((d_mix, tm), lambda i: (0, i)),
                  pl.BlockSpec((d_mix, d), lambda i: (0, 0))],
        out_specs=[pl.BlockSpec((d_mix, tm), lambda i: (0, i)), pl.BlockSpec((d_mix, d), lambda i: (0, 0))],
        out_shape=[jax.ShapeDtypeStruct((d_mix, t_tok), F32), jax.ShapeDtypeStruct((d_mix, d), F32)],
        args=(dz1, cat_t, w_out_b))


def _mixer_bwd(dcat_t, h_t, cos_t, sin_t, w_spatial, b_spatial, vln_g, vln_b, sinks, band_bias, lse, comms=()):
    t_tok = h_t.shape[1]
    nb, n_step = t_tok // BLK, t_tok // MIX_W
    group = N_HEADS // N_KV_HEADS

    def body(sinks_ref, dcat_ref, u_ref, vg_ref, q_ref, kvc_ref, kvp_ref, cos_ref, sin_ref, cosp_ref, sinp_ref,
             wsp_ref, bsp_ref, g_ref, b_ref, bias_ref, lse_ref, dh_ref, dkvc_ref, dkvp_ref, gwsb_ref, gbsp_ref, gvln_ref, gsink_ref,
             dg_acc, db_acc, wm_scr, gws_ref):
        i = pl.program_id(0)

        @pl.when(i == 0)
        def _():
            gws_ref[...] = jnp.zeros_like(gws_ref)
            gbsp_ref[...] = jnp.zeros_like(gbsp_ref)
            gsink_ref[...] = jnp.zeros_like(gsink_ref)
            dg_acc[...] = jnp.zeros_like(dg_acc)
            db_acc[...] = jnp.zeros_like(db_acc)

        _mask_w_once(wsp_ref, wm_scr)

        g = g_ref[...]
        ua, ua_grad = _gelu_and_grad(u_ref[...])
        vv, vv_grad = _gelu_and_grad(vg_ref[...])
        vp, vhat, rstd = _ln_fwd_t(vv, g, b_ref[...])
        vpb = vp.astype(BF16)
        da = dcat_ref[0:D_GMLP, :]
        dmixed = da * ua
        dvp_blocks = []
        for b in range(MIX_BLOCKS):
            dvp_parts = []
            for hh in range(N_HEADS):
                rows = slice(hh * HEAD_DIM, (hh + 1) * HEAD_DIM)
                vpb_h = vpb[rows, _cols(b)]
                mixed = _dot(vpb_h, wm_scr[hh], NT) + bsp_ref[hh:hh + 1, :]
                dh_ref[COL_U + hh * HEAD_DIM:COL_U + (hh + 1) * HEAD_DIM, _cols(b)] = (
                    da[rows, _cols(b)] * mixed * ua_grad[rows, _cols(b)]).astype(BF16)
                dm = dmixed[rows, _cols(b)]
                dmb = dm.astype(BF16)
                gbsp_ref[hh:hh + 1, :] += jnp.sum(dm, axis=0, keepdims=True)
                gws_ref[hh] += _dot(dmb, vpb_h, TN)
                dvp_parts.append(_dot(dmb, wm_scr[hh]))
            dvp_blocks.append(jnp.concatenate(dvp_parts, axis=0))
        dvp = jnp.concatenate(dvp_blocks, axis=1)
        dgv, dbv = dvp * vhat, dvp
        for b in range(MIX_BLOCKS):
            dg_acc[...] += dgv[:, _cols(b)]
            db_acc[...] += dbv[:, _cols(b)]
        dh_ref[COL_V:COL_V + D_GMLP, :] = (_ln_bwd_t(dvp, vhat, rstd, g) * vv_grad).astype(BF16)

        kvc, cos, sin = kvc_ref[...], cos_ref[...], sin_ref[...]
        qr = (_rope_t(q_ref[...], cos, sin) * SCORE_SCALE).astype(BF16)
        sinks4 = [_group_lanes([jnp.full((1, BLK), sinks_ref[hh], F32) for hh in range(kv * group, (kv + 1) * group)])
                  for kv in range(N_KV_HEADS)]
        dq_blocks, dkv_cur, dkv_prev = [], [], []
        for b in range(MIX_BLOCKS):
            kv_cur, kv_prev, cosc, sinc, cosp, sinp, bias1 = _block_inputs(b, i, kvc, kvp_ref, cos, sin, cosp_ref, sinp_ref, bias_ref)
            k_t, k_n, v_t = _keys_values(kv_cur, kv_prev, cosc, sinc, cosp, sinp)
            v_n = jnp.concatenate([kv_prev[D_KV:].T, kv_cur[D_KV:].T], axis=0).astype(BF16)
            bias = _group_lanes([bias1] * group)
            dk, dv, dq_parts = [], [], []
            for kv in range(N_KV_HEADS):
                heads = range(kv * group, (kv + 1) * group)
                kv_rows = slice(kv * HEAD_DIM, (kv + 1) * HEAD_DIM)
                qs = _group_lanes([qr[hh * HEAD_DIM:(hh + 1) * HEAD_DIM, _cols(b)] for hh in heads])
                dos = _group_lanes([dcat_ref[D_GMLP + hh * HEAD_DIM:D_GMLP + (hh + 1) * HEAD_DIM, _cols(b)]
                                    for hh in heads]).astype(BF16)
                lse_g = lse_ref[b * N_KV_HEADS + kv:b * N_KV_HEADS + kv + 1, :]
                p = jnp.exp(_dot(k_n, _pad_head(qs, kv)) + bias - lse_g)
                p_sink = jnp.exp(sinks4[kv] - lse_g)
                dp = _dot(v_n, _pad_head(dos, kv))
                delta = jnp.sum(p * dp, axis=0, keepdims=True)
                ds = (p * (dp - delta)).astype(BF16)
                dsink = p_sink * delta
                dq = _dot(k_t[kv_rows], ds) * SCORE_SCALE
                for j, hh in enumerate(heads):
                    gsink_ref[hh:hh + 1, :] -= dsink[:, j * BLK:(j + 1) * BLK]
                    dq_parts.append(dq[:, j * BLK:(j + 1) * BLK])
                dk.append(_dot(qs, ds, NT))
                dv.append(_dot(dos, p.astype(BF16), NT))
            dq_blocks.append(jnp.concatenate(dq_parts, axis=0))
            dk_all, dv_all = jnp.concatenate(dk, axis=0), jnp.concatenate(dv, axis=0)
            dkv_cur.append(jnp.concatenate([_rope_t(dk_all[:, BLK:], cosc, sinc, bwd=True), dv_all[:, BLK:]], axis=0))
            dkv_prev.append(jnp.concatenate([_rope_t(dk_all[:, :BLK], cosp, sinp, bwd=True), dv_all[:, :BLK]], axis=0))
        dh_ref[COL_Q:COL_Q + D_ATTN, :] = _rope_t(jnp.concatenate(dq_blocks, axis=1), cos, sin, bwd=True).astype(BF16)
        for b in range(MIX_BLOCKS):
            dkvc_ref[:, _cols(b)] = dkv_cur[b] + dkv_prev[b + 1] if b + 1 < MIX_BLOCKS else dkv_cur[b]
        dkvp_ref[...] = dkv_prev[0]

        @pl.when(i == n_step - 1)
        def _():
            causal = _causal()
            for hh in range(N_HEADS):
                gwsb_ref[hh] = jnp.where(causal, gws_ref[hh], 0.0).astype(BF16)
            gvln_ref[...] = jnp.zeros_like(gvln_ref)
            gvln_ref[0:1, :] = jnp.sum(dg_acc[...].T, axis=0, keepdims=True)
            gvln_ref[1:2, :] = jnp.sum(db_acc[...].T, axis=0, keepdims=True)

    full = lambda shape: pl.BlockSpec(shape, lambda i: (0,) * len(shape))
    return _carry(
        body, name="mixer_bwd", grid=(n_step,), comms=comms,
        in_specs=[pl.BlockSpec(memory_space=pltpu.SMEM), pl.BlockSpec((D_GMLP + D_ATTN, MIX_W), lambda i: (0, i))]
        + _h_specs() + _table_specs()
        + [full((N_HEADS, BLK, BLK)), full((N_HEADS, BLK)), full((D_GMLP, 1)), full((D_GMLP, 1)), BIAS_SPEC, LSE_SPEC],
        out_specs=[pl.BlockSpec((COL_K, MIX_W), lambda i: (0, i)), pl.BlockSpec((2 * D_KV, MIX_W), lambda i: (0, i)),
                   pl.BlockSpec((2 * D_KV, BLK), lambda i: (0, (i + n_step - 1) % n_step)),
                   full((N_HEADS, BLK, BLK)), full((N_HEADS, BLK)), full((8, D_GMLP)), full((N_HEADS, LANES))],
        out_shape=[jax.ShapeDtypeStruct((COL_K, t_tok), BF16), jax.ShapeDtypeStruct((2 * D_KV, t_tok), F32),
                   jax.ShapeDtypeStruct((2 * D_KV, n_step * BLK), F32),
                   jax.ShapeDtypeStruct((N_HEADS, BLK, BLK), BF16), jax.ShapeDtypeStruct((N_HEADS, BLK), F32),
                   jax.ShapeDtypeStruct((8, D_GMLP), F32), jax.ShapeDtypeStruct((N_HEADS, LANES), F32)],
        scratch_shapes=[pltpu.VMEM((D_GMLP, BLK), F32), pltpu.VMEM((D_GMLP, BLK), F32), pltpu.VMEM((N_HEADS, BLK, BLK), BF16),
                        pltpu.VMEM((N_HEADS, BLK, BLK), F32)],
        args=(sinks, dcat_t, h_t, h_t, h_t, h_t, h_t, cos_t, sin_t, cos_t, sin_t, w_spatial, b_spatial, vln_g, vln_b, band_bias, lse))


def _dkv_rows(dkvc_ref, dkvp_ref, width, store):
    for s in range(width // MIX_W):
        rest, last = slice(s * MIX_W, (s + 1) * MIX_W - BLK), slice((s + 1) * MIX_W - BLK, (s + 1) * MIX_W)
        store(rest, dkvc_ref[:, rest].astype(BF16))
        store(last, (dkvc_ref[:, last] + dkvp_ref[:, _cols(s)]).astype(BF16))


def _proj_in_wgrad(dh_b, dkvc_t, dkvp_t, xb, core_chip, comms=()):
    t_tok, d = xb.shape
    d_main, d_kv = dh_b.shape[0], dkvc_t.shape[0]
    rows = (d_main + d_kv) // N_DEV
    whole, cut = d_main // rows, d_main % rows

    def product(shard, dh_ref, dkvc_ref, dkvp_ref, xb_ref, dht_scr, sems):
        copies = [pltpu.make_async_copy(dh_ref.at[j * rows:(j + 1) * rows], dht_scr.at[j], sems.at[j]) for j in range(whole)]
        copies.append(pltpu.make_async_copy(dh_ref.at[whole * rows:d_main], dht_scr.at[whole, 0:cut], sems.at[whole]))

        @pl.when(pl.program_id(0) == 0)
        def _():
            for cp in copies:
                cp.start()

            def store(cols, val):
                dht_scr[whole, cut:rows, cols] = val[0:rows - cut]
                dht_scr[whole + 1, :, cols] = val[rows - cut:]

            _dkv_rows(dkvc_ref, dkvp_ref, t_tok, store)
            for cp in copies:
                cp.wait()

        return _dot(dht_scr[shard], xb_ref[...])

    return _wgrad_pair_sum(
        "proj_in_wgrad", product, (rows, d), [ANY, _resident(dkvc_t), _resident(dkvp_t), _resident(xb)],
        (dh_b, dkvc_t, dkvp_t, xb), core_chip, comms, to_chips=True,
        scratch_shapes=[pltpu.VMEM((N_DEV, rows, t_tok), BF16), pltpu.SemaphoreType.DMA((whole + 1,))])


def _proj_in_dgrad(dh_b, dkvc_t, dkvp_t, dz1, w_in_t, comms=()):
    t_tok, d = dz1.shape
    d_main, d_kv = dh_b.shape[0], dkvc_t.shape[0]
    tm = min(512, t_tok)

    def body(dh_ref, dkvc_ref, dkvp_ref, dz1_ref, w_ref, dx_ref, dkv_scr):
        def store(cols, val):
            dkv_scr[:, cols] = val

        _dkv_rows(dkvc_ref, dkvp_ref, tm, store)
        dx_ref[...] = (ALPHA * dz1_ref[...] + _dot(dh_ref[...], w_ref[0:d_main, :], TN)
                       + _dot(dkv_scr[...], w_ref[d_main:, :], TN))

    return _carry(
        body, name="proj_in_dgrad", grid=(t_tok // tm,), comms=comms,
        in_specs=[pl.BlockSpec((d_main, tm), lambda i: (0, i)), pl.BlockSpec((d_kv, tm), lambda i: (0, i)),
                  pl.BlockSpec((d_kv, tm // MIX_BLOCKS), lambda i: (0, i)),
                  pl.BlockSpec((tm, d), lambda i: (i, 0)), pl.BlockSpec((d_main + d_kv, d), lambda i: (0, 0))],
        out_specs=[pl.BlockSpec((tm, d), lambda i: (i, 0))],
        out_shape=[jax.ShapeDtypeStruct((t_tok, d), F32)],
        scratch_shapes=[pltpu.VMEM((d_kv, tm), BF16)],
        args=(dh_b, dkvc_t, dkvp_t, dz1, w_in_t))


def _adamw(w, g, m, v):
    m = ADAM_B1 * m + (1.0 - ADAM_B1) * g
    v = ADAM_B2 * v + (1.0 - ADAM_B2) * (g * g)
    m_hat = m / (1.0 - ADAM_B1 ** ADAM_STEP)
    v_hat = v / (1.0 - ADAM_B2 ** ADAM_STEP)
    delta = -ADAM_LR * (m_hat / (jnp.sqrt(v_hat) + ADAM_EPS) + ADAM_WD * w)
    return delta, m, v


ADAMW_STEPS = 4


def _adamw_shards(name, items, comms=(), rider=None):
    n_in, n_out = 5 * len(items), 4 * len(items)
    n_rin = len(rider["args"]) if rider else 0

    def body(*refs):
        ins, rins, outs, routs = refs[:n_in], refs[n_in:n_in + n_rin], refs[n_in + n_rin:n_in + n_rin + n_out], refs[n_in + n_rin + n_out:]
        for i in range(len(items)):
            own_ref, recv_ref, w_ref, m_ref, v_ref = ins[5 * i:5 * i + 5]
            g = ((own_ref[...] + recv_ref[0].astype(F32)) + recv_ref[1].astype(F32)) + recv_ref[2].astype(F32)
            for o_ref, val in zip(outs[4 * i:4 * i + 4], (g,) + _adamw(w_ref[...], g, m_ref[...], v_ref[...])):
                o_ref[...] = val
        if rider:
            pl.when(pl.program_id(0) == 0)(lambda: rider["body"](rins, routs))

    in_specs, out_specs, out_shape, args = [], [], [], []
    for own, recv, w, m, v in items:
        r, c = own.shape
        tiles = ADAMW_STEPS
        while (r // tiles) % BF16_ROWS:
            tiles //= 2
        blk = pl.BlockSpec((r // tiles, c), lambda s, k=ADAMW_STEPS // tiles: (s // k, 0))
        in_specs += [blk, pl.BlockSpec((3, r // tiles, c), lambda s, k=ADAMW_STEPS // tiles: (0, s // k, 0)), blk, blk, blk]
        out_specs += [blk] * 4
        out_shape += [jax.ShapeDtypeStruct((r, c), F32)] * 4
        args += [own, recv, w, m, v]
    if rider:
        in_specs, out_specs = in_specs + rider["in_specs"], out_specs + rider["out_specs"]
        out_shape, args = out_shape + rider["out_shape"], args + rider["args"]
    res, per_comm = _carry(body, name=name, grid=(ADAMW_STEPS,), comms=comms, in_specs=in_specs, out_specs=out_specs,
                           out_shape=out_shape, args=args)
    return [res[4 * i:4 * i + 4] for i in range(len(items))], res[n_out:], per_comm


VEC_VLN, VEC_LN1G, VEC_LN1B, VEC_LN2G, VEC_LN2B, VEC_SINK, VEC_LOSS, VEC_BSP, VEC_ROWS = 0, 1, 2, 3, 4, 5, 6, 8, 16


def _adamw_small(parts_w, parts_vec, params):
    n = parts_w.shape[0]
    flat = [a for p in params for a in p]
    shapes = [p[0].shape for p in params]

    def grads(gw, gv):
        return [gw, gv[VEC_VLN:VEC_VLN + 1, 0:D_GMLP], gv[VEC_VLN:VEC_VLN + 1, D_GMLP:2 * D_GMLP],
                gv[VEC_BSP:VEC_BSP + N_HEADS, 0:BLK], gv[VEC_LN1G:VEC_LN1G + 1], gv[VEC_LN1B:VEC_LN1B + 1],
                gv[VEC_LN2G:VEC_LN2G + 1], gv[VEC_LN2B:VEC_LN2B + 1], gv[VEC_SINK:VEC_SINK + 1, 0:N_HEADS]]

    def body(ins, outs):
        (pw_ref, pv_ref), ins = ins[:2], ins[2:]
        gw, gv = pw_ref[0].astype(F32), pv_ref[0]
        for k in range(1, n):
            gw, gv = gw + pw_ref[k].astype(F32), gv + pv_ref[k]
        for i, g in enumerate(grads(gw, gv)):
            w_ref, m_ref, v_ref = ins[3 * i:3 * i + 3]
            delta, m_new, v_new = _adamw(w_ref[...], g, m_ref[...], v_ref[...])
            for o_ref, val in zip(outs[4 * i:4 * i + 4], (g, delta, m_new, v_new)):
                o_ref[...] = val
        outs[-1][...] = gv[VEC_LOSS:VEC_LOSS + 1, 0:LANES]

    whole = lambda shape, **kw: pl.BlockSpec(shape, lambda i: (0,) * len(shape), **kw)
    once = dict(pipeline_mode=pl.Buffered(1))
    return dict(
        body=body, args=[parts_w, parts_vec, *flat],
        in_specs=[whole(parts_w.shape, **once), whole(parts_vec.shape, **once)] + [whole(a.shape, **once) for a in flat],
        out_specs=[whole(s) for s in shapes for _ in range(4)] + [whole((1, LANES))],
        out_shape=[jax.ShapeDtypeStruct(s, F32) for s in shapes for _ in range(4)] + [jax.ShapeDtypeStruct((1, LANES), F32)])


def _pair_sum(name, parts, recv, core_chip, comms=()):
    _, r, c = parts.shape
    tr = r if r <= 512 else 512

    def body(cc_ref, a_ref, b_ref, wire_ref, own_ref):
        s = a_ref[...] + b_ref[...]
        wire_ref[...] = s.astype(BF16)

        @pl.when(pl.program_id(1) == cc_ref[1])
        def _():
            own_ref[...] = s

    return _carry(
        body, name=name, grid=(r // tr, 4), prefetch=(core_chip,), comms=comms,
        in_specs=[pl.BlockSpec((None, tr, c), lambda i, q, cc: (2 * q + cc[0], i, 0)),
                  pl.BlockSpec((None, tr, c), lambda i, q, cc: (q, i, 0))],
        out_specs=[pl.BlockSpec((None, tr, c), lambda i, q, cc: (q, i, 0)), pl.BlockSpec((tr, c), lambda i, q, cc: (i, 0))],
        out_shape=[jax.ShapeDtypeStruct((4, r, c), BF16), jax.ShapeDtypeStruct((r, c), F32)],
        args=(parts, recv))


def kernel(x, positions, w_in, v_ln_g, v_ln_b, w_spatial, b_spatial, sinks, w_out, ln1_g, ln1_b, w_ff1, w_ff2, ln2_g, ln2_b, loss_target, m_w_in, m_v_ln_g, m_v_ln_b, m_w_spatial, m_b_spatial, m_sinks, m_w_out, m_ln1_g, m_ln1_b, m_w_ff1, m_w_ff2, m_ln2_g, m_ln2_b, v_w_in, v_v_ln_g, v_v_ln_b, v_w_spatial, v_b_spatial, v_sinks, v_w_out, v_ln1_g, v_ln1_b, v_w_ff1, v_w_ff2, v_ln2_g, v_ln2_b):
    _, t_tok, d = x.shape
    xi, yi, ci = _place()
    core_chip = jnp.stack([ci, 2 * xi + yi]).astype(jnp.int32)
    x2 = x.reshape(t_tok, d)
    target = loss_target.reshape(t_tok, d)
    inv_freq = ROPE_THETA ** (-jnp.arange(0, HEAD_DIM, 2, dtype=F32) / HEAD_DIM)
    wsp, bsp, sink_vec = w_spatial[0], b_spatial[0], sinks[0]
    vg_col, vb_col = v_ln_g.reshape(D_GMLP, 1), v_ln_b.reshape(D_GMLP, 1)
    big = {"in": w_in[0], "out": w_out[0], "ff1": w_ff1[0], "ff2": w_ff2[0]}
    half1, half2 = big["ff1"].shape[1] // 2, big["ff2"].shape[0] // 2
    w1_mine = [big["ff1"][:, :half1].astype(BF16), big["ff1"][:, half1:].astype(BF16)]
    w2_mine = [big["ff2"][:half2].astype(BF16), big["ff2"][half2:].astype(BF16)]

    (cos_t, sin_t), ((g_in,),) = _rope_tables(
        positions, jnp.tile(inv_freq, 2).reshape(HEAD_DIM, 1), comms=[_gather_comm([big["in"].T.astype(BF16)])])
    w_in_t = g_in.reshape(D_IN, d)
    (h_t, xb), ((g_out, w1_a),) = _proj_in(x2, w_in_t, comms=[_gather_comm([big["out"].astype(BF16), w1_mine[0]])])
    w_out_b = g_out.reshape(-1, d)
    band_bias = _band_bias()
    (cat_t, lse), ((w1_b, w2_a),) = _mixer_fwd(h_t, cos_t, sin_t, wsp, bsp, vg_col, vb_col, sink_vec, band_bias,
                                                comms=[_gather_comm([w1_mine[1], w2_mine[0]])])
    (xhat1, rstd1, x1b), ((w2_b,),) = _proj_out(cat_t, x2, w_out_b, ln1_g, ln1_b, comms=[_gather_comm([w2_mine[1]])])
    act_b, dpre_b, dz2b, dz1, stats = _ffn_fwd_bwd(xhat1, rstd1, x1b, target, [w1_a, w1_b], [w2_a, w2_b], ln1_g, ln1_b, ln2_g, ln2_b)

    (dcat_t, gw_out), _ = _proj_out_bwd(dz1, cat_t, w_out_b)
    p_out = gw_out.reshape(N_DEV, -1, d)
    wire_ff1, own_ff1, ((s_out,),) = _ffn_wgrad("ffn_wgrad1", x1b, dpre_b, False, core_chip, comms=[_sibling_comm([p_out])])
    (wire_out, own_out), _ = _pair_sum("pair_sum_out", p_out, s_out, core_chip)
    wire_ff2, own_ff2, ((r_ff1,),) = _ffn_wgrad("ffn_wgrad2", act_b, dz2b, True, core_chip, comms=[_chips_comm([wire_ff1])])
    (dh_b, dkvc_t, dkvp_t, g_wsp, g_bsp, g_vln, g_sink), ((r_ff2, r_out),) = _mixer_bwd(
        dcat_t, h_t, cos_t, sin_t, wsp, bsp, vg_col, vb_col, sink_vec, band_bias, lse,
        comms=[_chips_comm([wire_ff2, wire_out])])
    sink_row = jnp.pad(g_sink.sum(axis=1).reshape(1, N_HEADS), ((0, 0), (0, d - N_HEADS)))
    small_vec = jnp.concatenate([g_vln[0:2].reshape(1, d), stats[0:4], sink_row, stats[4:5], jnp.zeros((1, d), F32),
                                 jnp.pad(g_bsp, ((0, 0), (0, d - BLK)))], axis=0)
    r_in, own_in, ((parts_w, parts_vec),) = _proj_in_wgrad(
        dh_b, dkvc_t, dkvp_t, xb, core_chip, comms=[_gather_comm([g_wsp.reshape(-1, BLK), small_vec])])
    (grad_x,), _ = _proj_in_dgrad(dh_b, dkvc_t, dkvp_t, dz1, w_in_t)
    small = [(w_spatial, m_w_spatial, v_w_spatial), (v_ln_g, m_v_ln_g, v_v_ln_g), (v_ln_b, m_v_ln_b, v_v_ln_b),
             (b_spatial, m_b_spatial, v_b_spatial), (ln1_g, m_ln1_g, v_ln1_g), (ln1_b, m_ln1_b, v_ln1_b),
             (ln2_g, m_ln2_g, v_ln2_g), (ln2_b, m_ln2_b, v_ln2_b), (sinks, m_sinks, v_sinks)]
    views = [(-1, BLK), None, None, (N_HEADS, BLK)] + [None] * 5
    small_update = _adamw_small(parts_w, parts_vec, [
        tuple(a if vw is None else a.reshape(vw) for a in p) for p, vw in zip(small, views)])
    (out_out, ff1_out, ff2_out, in_out_t), small_res, _ = _adamw_shards("adamw_all", [
        (own_out, r_out, big["out"], m_w_out[0], v_w_out[0]),
        (own_ff1, r_ff1, big["ff1"], m_w_ff1[0], v_w_ff1[0]), (own_ff2, r_ff2, big["ff2"], m_w_ff2[0], v_w_ff2[0]),
        (own_in, r_in, big["in"].T, m_w_in[0].T, v_w_in[0].T)], rider=small_update)
    in_out = [o.T for o in in_out_t]
    small_out = [[o.reshape(p[0].shape) for o in small_res[4 * i:4 * i + 4]] for i, p in enumerate(small)]
    loss = small_res[-1][0, 0]

    big_out = {0: in_out, 6: out_out, 9: ff1_out, 10: ff2_out}
    small_slot = {3: 0, 1: 1, 2: 2, 4: 3, 7: 4, 8: 5, 11: 6, 12: 7, 5: 8}
    outs = [loss, grad_x.reshape(x.shape)]
    for kind in range(4):
        for wi in range(13):
            outs.append(big_out[wi][kind][None] if wi in big_out else small_out[small_slot[wi]][kind])
    return tuple(outs)
```

```python
import math

import jax
import jax.numpy as jnp
from jax import lax
from jax.experimental import pallas as pl
from jax.experimental.pallas import tpu as pltpu

F32 = jnp.float32
BF16 = jnp.bfloat16
MESH = pl.DeviceIdType.MESH

HEAD_DIM = 64
N_HEADS = 8
N_KV_HEADS = 2
BLK = 128
D_GMLP = N_HEADS * HEAD_DIM
D_ATTN = N_HEADS * HEAD_DIM
D_KV = N_KV_HEADS * HEAD_DIM
D_IN = 2 * D_GMLP + D_ATTN + 2 * D_KV
COL_U, COL_V, COL_Q, COL_K = 0, D_GMLP, 2 * D_GMLP, 2 * D_GMLP + D_ATTN
ROPE_THETA = 10000.0
LN_EPS = 1e-5
ALPHA = 2.0 ** 0.25
NEG_INF = -1e30
SCORE_SCALE = 1.0 / math.sqrt(HEAD_DIM)
ADAM_LR, ADAM_B1, ADAM_B2, ADAM_EPS, ADAM_WD, ADAM_STEP = 0.001, 0.9, 0.999, 1e-08, 0.01, 10
N_DEV = 8
LANES = 128
VMEM_LIMIT = 56 * 1024 * 1024
FFN_ROWS = 256

NT = (((1,), (1,)), ((), ()))
TN = (((0,), (0,)), ((), ()))


def _params(*sem):
    return pltpu.CompilerParams(dimension_semantics=sem, vmem_limit_bytes=VMEM_LIMIT)


def _dot(a, b, dims=None):
    if dims is None:
        return jnp.dot(a, b, preferred_element_type=F32)
    return lax.dot_general(a, b, dims, preferred_element_type=F32)


def _mean(a):
    return jnp.mean(a, axis=-1, keepdims=True)


def _ln_fwd(z, g, b):
    zc = z - _mean(z)
    rstd = lax.rsqrt(_mean(zc * zc) + LN_EPS)
    xhat = zc * rstd
    return xhat * g + b, xhat, rstd


def _ln_bwd(dy, xhat, rstd, g):
    dxhat = dy * g
    return rstd * (dxhat - _mean(dxhat) - xhat * _mean(dxhat * xhat))


_GELU_C = math.sqrt(2.0 / math.pi)


def _gelu(x):
    t = jnp.tanh(_GELU_C * (x + 0.044715 * (x * x * x)))
    return 0.5 * x * (1.0 + t)


def _gelu_and_grad(x):
    x2 = x * x
    t = jnp.tanh(_GELU_C * (x + 0.044715 * (x2 * x)))
    hx, ht = 0.5 * x, 0.5 * (1.0 + t)
    return x * ht, ht + hx * (1.0 - t * t) * (_GELU_C * (1.0 + 3.0 * 0.044715 * x2))


def _mean0(a):
    return jnp.mean(a, axis=0, keepdims=True)


def _ln_fwd_t(z, g, b):
    zc = z - _mean0(z)
    rstd = lax.rsqrt(_mean0(zc * zc) + LN_EPS)
    xhat = zc * rstd
    return xhat * g + b, xhat, rstd


def _ln_bwd_t(dy, xhat, rstd, g):
    dxhat = dy * g
    return rstd * (dxhat - _mean0(dxhat) - xhat * _mean0(dxhat * xhat))


def _rope_t(t, cos, sin_signed, bwd=False):
    half = HEAD_DIM // 2
    outs = []
    for r in range(0, t.shape[0], HEAD_DIM):
        th = t[r:r + HEAD_DIM]
        sw = jnp.concatenate([th[half:], th[:half]], axis=0) * sin_signed
        outs.append(th * cos - sw if bwd else th * cos + sw)
    return jnp.concatenate(outs, axis=0)


ANY = pl.BlockSpec(memory_space=pl.ANY)
GATHER_PIECES = 4
BF16_ROWS = 16


def _place():
    return lax.axis_index("x"), lax.axis_index("y"), lax.axis_index("c")


class _Comm:
    def __init__(self, ins, outs, sems, start, finish):
        self.ins, self.outs, self.sems, self.start, self.finish = ins, outs, sems, start, finish


def _gather_comm(arrs):
    n = len(arrs)
    pieces = []
    for a, arr in enumerate(arrs):
        k = GATHER_PIECES
        while arr.shape[0] % (k * BF16_ROWS):
            k //= 2
        pieces += [(a, p * (arr.shape[0] // k), arr.shape[0] // k) for p in range(k)]

    def parts(ins, outs, sems):
        send_sems, recv_sems, local_sems = sems
        x, y, c = _place()
        me, sibling = (x, y, c), (x, y, 1 - c)
        chips = [(1 - x, y), (x, 1 - y), (1 - x, 1 - y)]

        def copy(u, k, block, to, local=False):
            a, r0, nr = pieces[u]
            px, py, pc = block
            dst = outs[a].at[4 * px + 2 * py + pc, pl.ds(r0, nr)]
            return pltpu.make_async_remote_copy(
                src_ref=ins[a].at[pl.ds(r0, nr)] if local else dst, dst_ref=dst,
                send_sem=send_sems.at[u, k], recv_sem=recv_sems.at[u, k], device_id=to, device_id_type=MESH)

        mine = [pltpu.make_async_copy(ins[a], outs[a].at[4 * x + 2 * y + c], local_sems.at[a]) for a in range(n)]
        first = []
        for u in range(len(pieces)):
            first.append(copy(u, 0, me, sibling, local=True))
            first += [copy(u, 1 + j, me, (*chip, c), local=True) for j, chip in enumerate(chips)]
        return copy, mine, first, me, sibling, chips, c

    def start(ins, outs, sems):
        _, mine, first, *_ = parts(ins, outs, sems)
        for cp in mine + first:
            cp.start()

    def finish(ins, outs, sems):
        copy, mine, first, me, sibling, chips, c = parts(ins, outs, sems)
        passed = []
        for u in range(len(pieces)):
            for j, chip in enumerate(chips):
                copy(u, 1 + j, (*chip, c), me).wait_recv()
                fwd = copy(u, 4 + j, (*chip, c), sibling)
                fwd.start()
                passed.append(fwd)
        for u in range(len(pieces)):
            copy(u, 0, sibling, me).wait_recv()
            for j, chip in enumerate(chips):
                copy(u, 4 + j, (*chip, 1 - c), me).wait_recv()
        for cp in first + passed:
            cp.wait_send()
        for cp in mine:
            cp.wait()

    return _Comm(list(arrs), [jax.ShapeDtypeStruct((N_DEV,) + a.shape, a.dtype) for a in arrs],
                 [pltpu.SemaphoreType.DMA((len(pieces), 7)), pltpu.SemaphoreType.DMA((len(pieces), 7)),
                  pltpu.SemaphoreType.DMA((n,))], start, finish)


def _sibling_comm(parts):
    n = len(parts)

    def copies(ins, outs, sems):
        x, y, c = _place()
        return [pltpu.make_async_remote_copy(
            src_ref=ins[a].at[2 * q + (1 - c)], dst_ref=outs[a].at[q],
            send_sem=sems[0].at[a, q], recv_sem=sems[1].at[a, q],
            device_id=(x, y, 1 - c), device_id_type=MESH) for a in range(n) for q in range(4)]

    return _Comm(list(parts), [jax.ShapeDtypeStruct((4,) + p.shape[1:], p.dtype) for p in parts],
                 [pltpu.SemaphoreType.DMA((n, 4)), pltpu.SemaphoreType.DMA((n, 4))],
                 lambda *r: [cp.start() for cp in copies(*r)], lambda *r: [cp.wait() for cp in copies(*r)])


def _chips_comm(chip_parts, rows=None):
    n = len(chip_parts)
    r0, nr = (0, None) if rows is None else rows

    def copies(ins, outs, sems):
        x, y, c = _place()
        chips = [(1 - x, y), (x, 1 - y), (1 - x, 1 - y)]
        src = lambda a, q: ins[a].at[q] if rows is None else ins[a].at[q, pl.ds(r0, nr)]
        return [pltpu.make_async_remote_copy(
            src_ref=src(a, 2 * px + py), dst_ref=outs[a].at[k],
            send_sem=sems[0].at[a, k], recv_sem=sems[1].at[a, k],
            device_id=(px, py, c), device_id_type=MESH) for a in range(n) for k, (px, py) in enumerate(chips)]

    shape = lambda p: (3,) + p.shape[1:] if rows is None else (3, nr) + p.shape[2:]
    return _Comm(list(chip_parts), [jax.ShapeDtypeStruct(shape(p), p.dtype) for p in chip_parts],
                 [pltpu.SemaphoreType.DMA((n, 3)), pltpu.SemaphoreType.DMA((n, 3))],
                 lambda *r: [cp.start() for cp in copies(*r)], lambda *r: [cp.wait() for cp in copies(*r)])


def _carry(body, *, name, grid, in_specs, out_specs, out_shape, args, comms=(), scratch_shapes=(), prefetch=()):
    n_pre, n_in, n_out, n_scr = len(prefetch), len(in_specs), len(out_specs), len(scratch_shapes)
    c_ins = [a for cm in comms for a in cm.ins]
    c_outs = [s for cm in comms for s in cm.outs]
    c_sems = [s for cm in comms for s in cm.sems]

    def wrapped(*refs):
        pre, refs = refs[:n_pre], refs[n_pre:]
        ins, refs = refs[:n_in], refs[n_in:]
        cins, refs = refs[:len(c_ins)], refs[len(c_ins):]
        outs, refs = refs[:n_out], refs[n_out:]
        couts, refs = refs[:len(c_outs)], refs[len(c_outs):]
        scr, sems = refs[:n_scr], refs[n_scr:]
        groups, i0, o0, s0 = [], 0, 0, 0
        for cm in comms:
            groups.append((cm, cins[i0:i0 + len(cm.ins)], couts[o0:o0 + len(cm.outs)], sems[s0:s0 + len(cm.sems)]))
            i0, o0, s0 = i0 + len(cm.ins), o0 + len(cm.outs), s0 + len(cm.sems)
        first = pl.program_id(0) == 0
        last = pl.program_id(0) == grid[0] - 1
        for ax in range(1, len(grid)):
            first = first & (pl.program_id(ax) == 0)
            last = last & (pl.program_id(ax) == grid[ax] - 1)
        if comms:
            @pl.when(first)
            def _():
                for cm, ci, co, cs in groups:
                    cm.start(ci, co, cs)
        body(*pre, *ins, *outs, *scr)
        if comms:
            @pl.when(last)
            def _():
                for cm, ci, co, cs in groups:
                    cm.finish(ci, co, cs)

    grid_spec = pltpu.PrefetchScalarGridSpec(
        num_scalar_prefetch=n_pre, grid=grid,
        in_specs=list(in_specs) + [ANY] * len(c_ins), out_specs=list(out_specs) + [ANY] * len(c_outs),
        scratch_shapes=list(scratch_shapes) + c_sems)
    res = pl.pallas_call(
        wrapped, name=name, grid_spec=grid_spec, out_shape=list(out_shape) + c_outs,
        compiler_params=_params(*(["arbitrary"] * len(grid))),
    )(*prefetch, *args, *c_ins)
    outs, rest, per_comm = res[:n_out], res[n_out:], []
    for cm in comms:
        per_comm.append(rest[:len(cm.outs)])
        rest = rest[len(cm.outs):]
    return outs, per_comm


def _rope_tables(pos_row, inv_freq_col, comms=()):
    t_tok = pos_row.shape[1]
    tm = min(512, t_tok)

    def body(pos_ref, invf_ref, cos_ref, sin_ref):
        ang = pos_ref[...].astype(F32) * invf_ref[...]
        row = lax.broadcasted_iota(jnp.int32, ang.shape, 0)
        cos_ref[...] = jnp.cos(ang)
        sin_ref[...] = jnp.sin(ang) * jnp.where(row < HEAD_DIM // 2, -1.0, 1.0)

    return _carry(
        body, name="rope_tables", grid=(t_tok // tm,), comms=comms,
        in_specs=[pl.BlockSpec((1, tm), lambda i: (0, i)), pl.BlockSpec((HEAD_DIM, 1), lambda i: (0, 0))],
        out_specs=[pl.BlockSpec((HEAD_DIM, tm), lambda i: (0, i))] * 2,
        out_shape=[jax.ShapeDtypeStruct((HEAD_DIM, t_tok), F32)] * 2,
        args=(pos_row, inv_freq_col))


def _proj_in(x2, w_in_t, comms=()):
    t_tok, d = x2.shape
    d_in = w_in_t.shape[0]
    tm = min(512, t_tok)

    def body(x_ref, w_ref, h_ref, xb_ref):
        xb = x_ref[...].astype(BF16)
        xb_ref[...] = xb
        h_ref[...] = _dot(w_ref[...], xb, NT)

    return _carry(
        body, name="proj_in", grid=(t_tok // tm,), comms=comms,
        in_specs=[pl.BlockSpec((tm, d), lambda i: (i, 0)), pl.BlockSpec((d_in, d), lambda i: (0, 0))],
        out_specs=[pl.BlockSpec((d_in, tm), lambda i: (0, i)), pl.BlockSpec((tm, d), lambda i: (i, 0))],
        out_shape=[jax.ShapeDtypeStruct((d_in, t_tok), F32), jax.ShapeDtypeStruct((t_tok, d), BF16)],
        args=(x2, w_in_t))


MIX_BLOCKS = 2
MIX_W = MIX_BLOCKS * BLK


def _prev_block(i):
    return jnp.maximum(MIX_BLOCKS * i - 1, 0)


def _h_specs():
    kv_row = COL_K // (2 * D_KV)
    return [
        pl.BlockSpec((D_GMLP, MIX_W), lambda i: (0, i)),
        pl.BlockSpec((D_GMLP, MIX_W), lambda i: (1, i)),
        pl.BlockSpec((D_ATTN, MIX_W), lambda i: (2, i)),
        pl.BlockSpec((2 * D_KV, MIX_W), lambda i: (kv_row, i)),
        pl.BlockSpec((2 * D_KV, BLK), lambda i: (kv_row, _prev_block(i))),
    ]


def _table_specs():
    return [
        pl.BlockSpec((HEAD_DIM, MIX_W), lambda i: (0, i)),
        pl.BlockSpec((HEAD_DIM, MIX_W), lambda i: (0, i)),
        pl.BlockSpec((HEAD_DIM, BLK), lambda i: (0, _prev_block(i))),
        pl.BlockSpec((HEAD_DIM, BLK), lambda i: (0, _prev_block(i))),
    ]


def _cols(b):
    return slice(b * BLK, (b + 1) * BLK)


LSE_ROWS = 8
LSE_SPEC = pl.BlockSpec((LSE_ROWS, D_ATTN), lambda i: (i, 0))


def _block_inputs(b, i, kvc, kvp_ref, cos, sin, cosp_ref, sinp_ref, bias_ref):
    if b == 0:
        kv_prev, cos_prev, sin_prev, bias = kvp_ref[...], cosp_ref[...], sinp_ref[...], bias_ref[jnp.minimum(i, 1)]
    else:
        kv_prev, cos_prev, sin_prev, bias = kvc[:, _cols(b - 1)], cos[:, _cols(b - 1)], sin[:, _cols(b - 1)], bias_ref[1]
    return kvc[:, _cols(b)], kv_prev, cos[:, _cols(b)], sin[:, _cols(b)], cos_prev, sin_prev, bias


def _band_bias():
    ki = lax.broadcasted_iota(jnp.int32, (2, 2 * BLK, BLK), 1)
    qi = lax.broadcasted_iota(jnp.int32, (2, 2 * BLK, BLK), 2)
    later = lax.broadcasted_iota(jnp.int32, (2, 2 * BLK, BLK), 0) > 0
    dist = qi + BLK - ki
    return jnp.where((dist >= 0) & (dist < BLK) & ((ki >= BLK) | later), 0.0, NEG_INF).astype(F32)


BIAS_SPEC = pl.BlockSpec((2, 2 * BLK, BLK), lambda i: (0, 0, 0))


def _keys_values(kvc, kvp, cosc, sinc, cosp, sinp):
    kp, kc = _rope_t(kvp[:D_KV], cosp, sinp), _rope_t(kvc[:D_KV], cosc, sinc)
    k_t = jnp.concatenate([kp, kc], axis=1).astype(BF16)
    k_n = jnp.concatenate([kp.T, kc.T], axis=0).astype(BF16)
    v_t = jnp.concatenate([kvp[D_KV:], kvc[D_KV:]], axis=1).astype(BF16)
    return k_t, k_n, v_t


def _pad_head(th, kv):
    z = jnp.zeros_like(th)
    return jnp.concatenate([th, z] if kv == 0 else [z, th], axis=0)


def _group_lanes(parts):
    return jnp.concatenate(parts, axis=1)


def _softmax_sink_t(s, sink):
    m = jnp.maximum(jnp.max(s, axis=0, keepdims=True), sink)
    e = jnp.exp(s - m)
    denom = jnp.sum(e, axis=0, keepdims=True) + jnp.exp(sink - m)
    return e * (1.0 / denom), m + jnp.log(denom)


def _causal():
    row = lax.broadcasted_iota(jnp.int32, (BLK, BLK), 0)
    col = lax.broadcasted_iota(jnp.int32, (BLK, BLK), 1)
    return row >= col


def _mask_w_once(wsp_ref, wm_scr):
    @pl.when(pl.program_id(0) == 0)
    def _():
        causal = _causal()
        for hh in range(N_HEADS):
            wm_scr[hh] = jnp.where(causal, wsp_ref[hh], 0.0).astype(BF16)


def _mixer_fwd(h_t, cos_t, sin_t, w_spatial, b_spatial, vln_g, vln_b, sinks, band_bias, comms=()):
    t_tok = h_t.shape[1]
    group = N_HEADS // N_KV_HEADS

    def body(sinks_ref, u_ref, vg_ref, q_ref, kvc_ref, kvp_ref, cos_ref, sin_ref, cosp_ref, sinp_ref,
             wsp_ref, bsp_ref, g_ref, b_ref, bias_ref, cat_ref, lse_ref, wm_scr):
        i = pl.program_id(0)
        _mask_w_once(wsp_ref, wm_scr)
        lse_ref[...] = jnp.zeros_like(lse_ref)
        ua = _gelu(u_ref[...])
        vp, _, _ = _ln_fwd_t(_gelu(vg_ref[...]), g_ref[...], b_ref[...])
        vpb = vp.astype(BF16)
        for b in range(MIX_BLOCKS):
            for hh in range(N_HEADS):
                rows = slice(hh * HEAD_DIM, (hh + 1) * HEAD_DIM)
                mixed = _dot(vpb[rows, _cols(b)], wm_scr[hh], NT) + bsp_ref[hh:hh + 1, :]
                cat_ref[rows, _cols(b)] = (ua[rows, _cols(b)] * mixed).astype(BF16)

        kvc, cos, sin = kvc_ref[...], cos_ref[...], sin_ref[...]
        qr = (_rope_t(q_ref[...], cos, sin) * SCORE_SCALE).astype(BF16)
        sinks4 = [_group_lanes([jnp.full((1, BLK), sinks_ref[hh], F32) for hh in range(kv * group, (kv + 1) * group)])
                  for kv in range(N_KV_HEADS)]
        for b in range(MIX_BLOCKS):
            kv_cur, kv_prev, cosc, sinc, cosp, sinp, bias1 = _block_inputs(b, i, kvc, kvp_ref, cos, sin, cosp_ref, sinp_ref, bias_ref)
            _, k_n, v_t = _keys_values(kv_cur, kv_prev, cosc, sinc, cosp, sinp)
            bias = _group_lanes([bias1] * group)
            for kv in range(N_KV_HEADS):
                heads = range(kv * group, (kv + 1) * group)
                qs = _group_lanes([qr[hh * HEAD_DIM:(hh + 1) * HEAD_DIM, _cols(b)] for hh in heads])
                p, lse = _softmax_sink_t(_dot(k_n, _pad_head(qs, kv)) + bias, sinks4[kv])
                lse_ref[b * N_KV_HEADS + kv:b * N_KV_HEADS + kv + 1, :] = lse
                o = _dot(v_t[kv * HEAD_DIM:(kv + 1) * HEAD_DIM], p.astype(BF16)).astype(BF16)
                for j, hh in enumerate(heads):
                    cat_ref[D_GMLP + hh * HEAD_DIM:D_GMLP + (hh + 1) * HEAD_DIM, _cols(b)] = o[:, j * BLK:(j + 1) * BLK]

    full = lambda shape: pl.BlockSpec(shape, lambda i: (0,) * len(shape))
    return _carry(
        body, name="mixer_fwd", grid=(t_tok // MIX_W,), comms=comms,
        in_specs=[pl.BlockSpec(memory_space=pltpu.SMEM)] + _h_specs() + _table_specs() + [
            full((N_HEADS, BLK, BLK)), full((N_HEADS, BLK)), full((D_GMLP, 1)), full((D_GMLP, 1)), BIAS_SPEC],
        out_specs=[pl.BlockSpec((D_GMLP + D_ATTN, MIX_W), lambda i: (0, i)), LSE_SPEC],
        out_shape=[jax.ShapeDtypeStruct((D_GMLP + D_ATTN, t_tok), BF16),
                   jax.ShapeDtypeStruct((t_tok // MIX_W * LSE_ROWS, D_ATTN), F32)],
        scratch_shapes=[pltpu.VMEM((N_HEADS, BLK, BLK), BF16)],
        args=(sinks, h_t, h_t, h_t, h_t, h_t, cos_t, sin_t, cos_t, sin_t, w_spatial, b_spatial, vln_g, vln_b, band_bias))


def _proj_out(cat_t, x2, w_out_b, ln1_g, ln1_b, comms=()):
    t_tok, d = x2.shape
    tm = min(512, t_tok)

    def body(cat_ref, x_ref, w_ref, g_ref, b_ref, xhat_ref, rstd_ref, x1b_ref):
        x1, xhat, rstd = _ln_fwd(ALPHA * x_ref[...] + _dot(cat_ref[...], w_ref[...], TN), g_ref[...], b_ref[...])
        xhat_ref[...] = xhat
        rstd_ref[...] = rstd
        x1b_ref[...] = x1.astype(BF16)

    tok = lambda w: pl.BlockSpec((tm, w), lambda i: (i, 0))
    vec = pl.BlockSpec((1, d), lambda i: (0, 0))
    return _carry(
        body, name="proj_out", grid=(t_tok // tm,), comms=comms,
        in_specs=[pl.BlockSpec((cat_t.shape[0], tm), lambda i: (0, i)), tok(d), pl.BlockSpec(w_out_b.shape, lambda i: (0, 0)), vec, vec],
        out_specs=[tok(d), tok(1), tok(d)],
        out_shape=[jax.ShapeDtypeStruct((t_tok, d), F32), jax.ShapeDtypeStruct((t_tok, 1), F32), jax.ShapeDtypeStruct((t_tok, d), BF16)],
        args=(cat_t, x2, w_out_b, ln1_g, ln1_b))


def _ffn_fwd_bwd(xhat1, rstd1, x1b, target, w1_parts, w2_parts, ln1_g, ln1_b, ln2_g, ln2_b):
    t_tok, d = xhat1.shape
    n_part = len(w1_parts)
    n_chunk, _, fp = w1_parts[0].shape
    f = n_chunk * n_part * fp
    tm = min(FFN_ROWS, t_tok)

    def body(xhat1_ref, rstd1_ref, x1b_ref, tgt_ref, *refs):
        w1_hbm, w2_hbm = refs[:n_part], refs[n_part:2 * n_part]
        (g1_ref, b1_ref, g2_ref, b2_ref, act_ref, dpre_ref, dz2b_ref, dz1_ref, stats_ref,
         r_scr, w1_ref, w2_ref, w_sems) = refs[2 * n_part:]

        @pl.when(pl.program_id(0) == 0)
        def _():
            stats_ref[...] = jnp.zeros_like(stats_ref)
            loads = []
            for j in range(n_chunk):
                for p in range(n_part):
                    units = pl.ds((j * n_part + p) * fp, fp)
                    loads.append(pltpu.make_async_copy(w1_hbm[p].at[j], w1_ref.at[:, units], w_sems.at[0, p, j]))
                    loads.append(pltpu.make_async_copy(w2_hbm[p].at[j], w2_ref.at[units, :], w_sems.at[1, p, j]))
            for cp in loads:
                cp.start()
            for cp in loads:
                cp.wait()

        g1, g2 = g1_ref[...], g2_ref[...]
        xhat1 = xhat1_ref[...]
        r_scr[...] = jnp.maximum(_dot(x1b_ref[...], w1_ref[...]), 0.0)
        r = r_scr[...]
        act = (r * r).astype(BF16)
        act_ref[...] = act
        ff = _dot(act, w2_ref[...])
        y, xhat2, rstd2 = _ln_fwd(ALPHA * (xhat1 * g1 + b1_ref[...]) + ff, g2, b2_ref[...])
        diff = y - tgt_ref[...]
        loss = 0.5 * jnp.sum(jnp.sum(diff * diff, axis=-1, keepdims=True) / d, axis=0, keepdims=True)
        dy = diff / d
        dz2 = _ln_bwd(dy, xhat2, rstd2, g2)
        dz2b = dz2.astype(BF16)
        dz2b_ref[...] = dz2b
        dpre = (_dot(dz2b, w2_ref[...], NT) * (2.0 * r_scr[...])).astype(BF16)
        dpre_ref[...] = dpre
        dx1 = ALPHA * dz2 + _dot(dpre, w1_ref[...], NT)
        dz1_ref[...] = _ln_bwd(dx1, xhat1, rstd1_ref[...], g1)
        stats_ref[0:1, :] += jnp.sum(dx1 * xhat1, axis=0, keepdims=True)
        stats_ref[1:2, :] += jnp.sum(dx1, axis=0, keepdims=True)
        stats_ref[2:3, :] += jnp.sum(dy * xhat2, axis=0, keepdims=True)
        stats_ref[3:4, :] += jnp.sum(dy, axis=0, keepdims=True)
        stats_ref[4:5, :] += jnp.broadcast_to(loss, (1, d))

    tok = lambda w: pl.BlockSpec((tm, w), lambda i: (i, 0))
    vec = pl.BlockSpec((1, d), lambda i: (0, 0))
    return _carry(
        body, name="ffn_fwd_bwd", grid=(t_tok // tm,),
        in_specs=[tok(d), tok(1), tok(d), tok(d)] + [ANY] * (2 * n_part) + [vec, vec, vec, vec],
        out_specs=[tok(f), tok(f), tok(d), tok(d), pl.BlockSpec((8, d), lambda i: (0, 0))],
        out_shape=[jax.ShapeDtypeStruct((t_tok, f), BF16), jax.ShapeDtypeStruct((t_tok, f), BF16),
                   jax.ShapeDtypeStruct((t_tok, d), BF16), jax.ShapeDtypeStruct((t_tok, d), F32), jax.ShapeDtypeStruct((8, d), F32)],
        scratch_shapes=[pltpu.VMEM((tm, f), F32), pltpu.VMEM((d, f), BF16), pltpu.VMEM((f, d), BF16),
                        pltpu.SemaphoreType.DMA((2, n_part, n_chunk))],
        args=(xhat1, rstd1, x1b, target, *w1_parts, *w2_parts, ln1_g, ln1_b, ln2_g, ln2_b))[0]


WGRAD_STEPS = [(True, 0), (True, 1), (True, 2), (True, 3), (False, 0), (False, 1), (False, 2), (False, 3)]
WGRAD_STEPS_TO_CHIPS = [(True, 0), (True, 1), (False, 0), (True, 2), (False, 1), (True, 3), (False, 2), (False, 3)]
CHIP_FLIPS = [3, 1, 2, 0]


def _pick(table, s):
    out = table[-1]
    for i in range(len(table) - 2, -1, -1):
        out = jnp.where(s == i, table[i], out)
    return out


def _wgrad_shard(s, cc, to_chips=False):
    steps = WGRAD_STEPS_TO_CHIPS if to_chips else WGRAD_STEPS
    k = _pick([k for _, k in steps], s)
    q = jnp.bitwise_xor(cc[1], _pick(CHIP_FLIPS, k)) if to_chips else k
    return 2 * q + jnp.where(_pick([int(sibling) for sibling, _ in steps], s) == 1, 1 - cc[0], cc[0])


def _wgrad_pair_sum(name, product, chunk, in_specs, args, core_chip, comms=(), scratch_shapes=(), to_chips=False):
    half = N_DEV // 2
    n_in, n_scr = len(in_specs), 8 if to_chips else 5
    steps = WGRAD_STEPS_TO_CHIPS if to_chips else WGRAD_STEPS

    def body(cc_ref, *refs):
        ins, (first_ref, own_ref, recv_ref, send_buf, got, send_sems, recv_sems, got_sem) = refs[:n_in], refs[n_in:n_in + 8]
        s = pl.program_id(0)
        x, y, c = _place()
        def send(q):
            return pltpu.make_async_remote_copy(
                src_ref=send_buf.at[q % 2], dst_ref=recv_ref.at[q], send_sem=send_sems.at[q], recv_sem=recv_sems.at[q],
                device_id=(x, y, 1 - c), device_id_type=MESH)

        def load(q):
            return pltpu.make_async_copy(recv_ref.at[q], got, got_sem.at[0])

        chip_refs = refs[n_in + 8:n_in + 3 + n_scr]

        def to_chip(k):
            wire_buf, chip_send_sems, chip_recv_sems = chip_refs
            flip_x, flip_y = CHIP_FLIPS[k] // 2, CHIP_FLIPS[k] % 2
            return pltpu.make_async_remote_copy(
                src_ref=wire_buf.at[k], dst_ref=first_ref.at[k], send_sem=chip_send_sems.at[k], recv_sem=chip_recv_sems.at[k],
                device_id=(1 - x if flip_x else x, 1 - y if flip_y else y, c), device_id_type=MESH)

        for step, (sibling, q) in enumerate(steps):
            if not sibling:
                @pl.when(s == step)
                def _(q=q):
                    send(q).wait_recv()
                    load(q).start()

        g = product(_wgrad_shard(s, cc_ref, to_chips), *ins, *refs[n_in + 3 + n_scr:])

        for step, (sibling, q) in enumerate(steps):
            @pl.when(s == step)
            def _(sibling=sibling, q=q):
                if sibling:
                    if q >= 2:
                        send(q - 2).wait_send()
                    send_buf[q % 2] = g
                    send(q).start()
                    return
                load(q).wait()
                total = g + got[...]
                if to_chips and q < half - 1:
                    chip_refs[0][q] = total.astype(BF16)
                    to_chip(q).start()
                elif to_chips:
                    own_ref[...] = total
                else:
                    first_ref[...] = total.astype(BF16)

                    @pl.when(cc_ref[1] == q)
                    def _():
                        own_ref[...] = total

        @pl.when(s == N_DEV - 1)
        def _():
            for q in range(half - 2, half):
                send(q).wait_send()
            if to_chips:
                for k in range(half - 1):
                    to_chip(k).wait()

    if to_chips:
        first_spec, first_shape = ANY, jax.ShapeDtypeStruct((half - 1,) + chunk, BF16)
        chip_scratch = [pltpu.VMEM((half - 1,) + chunk, BF16), pltpu.SemaphoreType.DMA((half - 1,)),
                        pltpu.SemaphoreType.DMA((half - 1,))]
    else:
        first_spec = pl.BlockSpec((None,) + chunk, lambda s, cc: (jnp.maximum(s - half, 0), 0, 0))
        first_shape, chip_scratch = jax.ShapeDtypeStruct((half,) + chunk, BF16), []
    (first, own, _), per_comm = _carry(
        body, name=name, grid=(N_DEV,), comms=comms, prefetch=(core_chip,), in_specs=in_specs,
        out_specs=[first_spec, pl.BlockSpec(chunk, lambda s, cc: (0, 0)), ANY],
        out_shape=[first_shape, jax.ShapeDtypeStruct(chunk, F32), jax.ShapeDtypeStruct((half,) + chunk, F32)],
        scratch_shapes=[pltpu.VMEM((2,) + chunk, F32), pltpu.VMEM(chunk, F32), pltpu.SemaphoreType.DMA((half,)),
                        pltpu.SemaphoreType.DMA((half,)), pltpu.SemaphoreType.DMA((1,)), *chip_scratch, *scratch_shapes],
        args=args)
    return first, own, per_comm


def _resident(a):
    return pl.BlockSpec(a.shape, lambda s, cc: (0,) * a.ndim, pipeline_mode=pl.Buffered(1))


def _ffn_wgrad(name, lhs, rhs, chunk_lhs, core_chip, comms=()):
    t_tok = lhs.shape[0]
    fc = (lhs if chunk_lhs else rhs).shape[1] // N_DEV
    chunked = pl.BlockSpec((t_tok, fc), lambda s, cc: (0, _wgrad_shard(s, cc)))

    def product(shard, lhs_ref, rhs_ref):
        return _dot(lhs_ref[...], rhs_ref[...], TN)

    return _wgrad_pair_sum(
        name, product, (fc, rhs.shape[1]) if chunk_lhs else (lhs.shape[1], fc),
        [chunked, _resident(rhs)] if chunk_lhs else [_resident(lhs), chunked], (lhs, rhs), core_chip, comms)


def _proj_out_bwd(dz1, cat_t, w_out_b, comms=()):
    t_tok, d = dz1.shape
    d_mix = cat_t.shape[0]
    tm = min(512, t_tok)

    def body(dz1_ref, cat_ref, w_ref, dcat_ref, gw_ref):
        @pl.when(pl.program_id(0) == 0)
        def _():
            gw_ref[...] = jnp.zeros_like(gw_ref)

        dzb = dz1_ref[...].astype(BF16)
        dcat_ref[...] = _dot(w_ref[...], dzb, NT)
        gw_ref[...] += _dot(cat_ref[...], dzb)

    return _carry(
        body, name="proj_out_bwd", grid=(t_tok // tm,), comms=comms,
        in_specs=[pl.BlockSpec((tm, d), lambda i: (i, 0)), pl.BlockSpec((d_mix, tm), lambda i: (0, i)),
                  pl.BlockSpec((d_mix, d), lambda i: (0, 0))],
        out_specs=[pl.BlockSpec((d_mix, tm), lambda i: (0, i)), pl.BlockSpec((d_mix, d), lambda i: (0, 0))],
        out_shape=[jax.ShapeDtypeStruct((d_mix, t_tok), F32), jax.ShapeDtypeStruct((d_mix, d), F32)],
        args=(dz1, cat_t, w_out_b))


def _mixer_bwd(dcat_t, h_t, cos_t, sin_t, w_spatial, b_spatial, vln_g, vln_b, sinks, band_bias, lse, comms=()):
    t_tok = h_t.shape[1]
    nb, n_step = t_tok // BLK, t_tok // MIX_W
    group = N_HEADS // N_KV_HEADS

    def body(sinks_ref, dcat_ref, u_ref, vg_ref, q_ref, kvc_ref, kvp_ref, cos_ref, sin_ref, cosp_ref, sinp_ref,
             wsp_ref, bsp_ref, g_ref, b_ref, bias_ref, lse_ref, dh_ref, dkvc_ref, dkvp_ref, gwsb_ref, gbsp_ref, gvln_ref, gsink_ref,
             dg_acc, db_acc, wm_scr, gws_ref):
        i = pl.program_id(0)

        @pl.when(i == 0)
        def _():
            gws_ref[...] = jnp.zeros_like(gws_ref)
            gbsp_ref[...] = jnp.zeros_like(gbsp_ref)
            gsink_ref[...] = jnp.zeros_like(gsink_ref)
            dg_acc[...] = jnp.zeros_like(dg_acc)
            db_acc[...] = jnp.zeros_like(db_acc)

        _mask_w_once(wsp_ref, wm_scr)

        g = g_ref[...]
        ua, ua_grad = _gelu_and_grad(u_ref[...])
        vv, vv_grad = _gelu_and_grad(vg_ref[...])
        vp, vhat, rstd = _ln_fwd_t(vv, g, b_ref[...])
        vpb = vp.astype(BF16)
        da = dcat_ref[0:D_GMLP, :]
        dmixed = da * ua
        dvp_blocks = []
        for b in range(MIX_BLOCKS):
            dvp_parts = []
            for hh in range(N_HEADS):
                rows = slice(hh * HEAD_DIM, (hh + 1) * HEAD_DIM)
                vpb_h = vpb[rows, _cols(b)]
                mixed = _dot(vpb_h, wm_scr[hh], NT) + bsp_ref[hh:hh + 1, :]
                dh_ref[COL_U + hh * HEAD_DIM:COL_U + (hh + 1) * HEAD_DIM, _cols(b)] = (
                    da[rows, _cols(b)] * mixed * ua_grad[rows, _cols(b)]).astype(BF16)
                dm = dmixed[rows, _cols(b)]
                dmb = dm.astype(BF16)
                gbsp_ref[hh:hh + 1, :] += jnp.sum(dm, axis=0, keepdims=True)
                gws_ref[hh] += _dot(dmb, vpb_h, TN)
                dvp_parts.append(_dot(dmb, wm_scr[hh]))
            dvp_blocks.append(jnp.concatenate(dvp_parts, axis=0))
        dvp = jnp.concatenate(dvp_blocks, axis=1)
        dgv, dbv = dvp * vhat, dvp
        for b in range(MIX_BLOCKS):
            dg_acc[...] += dgv[:, _cols(b)]
            db_acc[...] += dbv[:, _cols(b)]
        dh_ref[COL_V:COL_V + D_GMLP, :] = (_ln_bwd_t(dvp, vhat, rstd, g) * vv_grad).astype(BF16)

        kvc, cos, sin = kvc_ref[...], cos_ref[...], sin_ref[...]
        qr = (_rope_t(q_ref[...], cos, sin) * SCORE_SCALE).astype(BF16)
        sinks4 = [_group_lanes([jnp.full((1, BLK), sinks_ref[hh], F32) for hh in range(kv * group, (kv + 1) * group)])
                  for kv in range(N_KV_HEADS)]
        dq_blocks, dkv_cur, dkv_prev = [], [], []
        for b in range(MIX_BLOCKS):
            kv_cur, kv_prev, cosc, sinc, cosp, sinp, bias1 = _block_inputs(b, i, kvc, kvp_ref, cos, sin, cosp_ref, sinp_ref, bias_ref)
            k_t, k_n, v_t = _keys_values(kv_cur, kv_prev, cosc, sinc, cosp, sinp)
            v_n = jnp.concatenate([kv_prev[D_KV:].T, kv_cur[D_KV:].T], axis=0).astype(BF16)
            bias = _group_lanes([bias1] * group)
            dk, dv, dq_parts = [], [], []
            for kv in range(N_KV_HEADS):
                heads = range(kv * group, (kv + 1) * group)
                kv_rows = slice(kv * HEAD_DIM, (kv + 1) * HEAD_DIM)
                qs = _group_lanes([qr[hh * HEAD_DIM:(hh + 1) * HEAD_DIM, _cols(b)] for hh in heads])
                dos = _group_lanes([dcat_ref[D_GMLP + hh * HEAD_DIM:D_GMLP + (hh + 1) * HEAD_DIM, _cols(b)]
                                    for hh in heads]).astype(BF16)
                lse_g = lse_ref[b * N_KV_HEADS + kv:b * N_KV_HEADS + kv + 1, :]
                p = jnp.exp(_dot(k_n, _pad_head(qs, kv)) + bias - lse_g)
                p_sink = jnp.exp(sinks4[kv] - lse_g)
                dp = _dot(v_n, _pad_head(dos, kv))
                delta = jnp.sum(p * dp, axis=0, keepdims=True)
                ds = (p * (dp - delta)).astype(BF16)
                dsink = p_sink * delta
                dq = _dot(k_t[kv_rows], ds) * SCORE_SCALE
                for j, hh in enumerate(heads):
                    gsink_ref[hh:hh + 1, :] -= dsink[:, j * BLK:(j + 1) * BLK]
                    dq_parts.append(dq[:, j * BLK:(j + 1) * BLK])
                dk.append(_dot(qs, ds, NT))
                dv.append(_dot(dos, p.astype(BF16), NT))
            dq_blocks.append(jnp.concatenate(dq_parts, axis=0))
            dk_all, dv_all = jnp.concatenate(dk, axis=0), jnp.concatenate(dv, axis=0)
            dkv_cur.append(jnp.concatenate([_rope_t(dk_all[:, BLK:], cosc, sinc, bwd=True), dv_all[:, BLK:]], axis=0))
            dkv_prev.append(jnp.concatenate([_rope_t(dk_all[:, :BLK], cosp, sinp, bwd=True), dv_all[:, :BLK]], axis=0))
        dh_ref[COL_Q:COL_Q + D_ATTN, :] = _rope_t(jnp.concatenate(dq_blocks, axis=1), cos, sin, bwd=True).astype(BF16)
        for b in range(MIX_BLOCKS):
            dkvc_ref[:, _cols(b)] = dkv_cur[b] + dkv_prev[b + 1] if b + 1 < MIX_BLOCKS else dkv_cur[b]
        dkvp_ref[...] = dkv_prev[0]

        @pl.when(i == n_step - 1)
        def _():
            causal = _causal()
            for hh in range(N_HEADS):
                gwsb_ref[hh] = jnp.where(causal, gws_ref[hh], 0.0).astype(BF16)
            gvln_ref[...] = jnp.zeros_like(gvln_ref)
            gvln_ref[0:1, :] = jnp.sum(dg_acc[...].T, axis=0, keepdims=True)
            gvln_ref[1:2, :] = jnp.sum(db_acc[...].T, axis=0, keepdims=True)

    full = lambda shape: pl.BlockSpec(shape, lambda i: (0,) * len(shape))
    return _carry(
        body, name="mixer_bwd", grid=(n_step,), comms=comms,
        in_specs=[pl.BlockSpec(memory_space=pltpu.SMEM), pl.BlockSpec((D_GMLP + D_ATTN, MIX_W), lambda i: (0, i))]
        + _h_specs() + _table_specs()
        + [full((N_HEADS, BLK, BLK)), full((N_HEADS, BLK)), full((D_GMLP, 1)), full((D_GMLP, 1)), BIAS_SPEC, LSE_SPEC],
        out_specs=[pl.BlockSpec((COL_K, MIX_W), lambda i: (0, i)), pl.BlockSpec((2 * D_KV, MIX_W), lambda i: (0, i)),
                   pl.BlockSpec((2 * D_KV, BLK), lambda i: (0, (i + n_step - 1) % n_step)),
                   full((N_HEADS, BLK, BLK)), full((N_HEADS, BLK)), full((8, D_GMLP)), full((N_HEADS, LANES))],
        out_shape=[jax.ShapeDtypeStruct((COL_K, t_tok), BF16), jax.ShapeDtypeStruct((2 * D_KV, t_tok), F32),
                   jax.ShapeDtypeStruct((2 * D_KV, n_step * BLK), F32),
                   jax.ShapeDtypeStruct((N_HEADS, BLK, BLK), BF16), jax.ShapeDtypeStruct((N_HEADS, BLK), F32),
                   jax.ShapeDtypeStruct((8, D_GMLP), F32), jax.ShapeDtypeStruct((N_HEADS, LANES), F32)],
        scratch_shapes=[pltpu.VMEM((D_GMLP, BLK), F32), pltpu.VMEM((D_GMLP, BLK), F32), pltpu.VMEM((N_HEADS, BLK, BLK), BF16),
                        pltpu.VMEM((N_HEADS, BLK, BLK), F32)],
        args=(sinks, dcat_t, h_t, h_t, h_t, h_t, h_t, cos_t, sin_t, cos_t, sin_t, w_spatial, b_spatial, vln_g, vln_b, band_bias, lse))


def _dkv_rows(dkvc_ref, dkvp_ref, width, store):
    for s in range(width // MIX_W):
        rest, last = slice(s * MIX_W, (s + 1) * MIX_W - BLK), slice((s + 1) * MIX_W - BLK, (s + 1) * MIX_W)
        store(rest, dkvc_ref[:, rest].astype(BF16))
        store(last, (dkvc_ref[:, last] + dkvp_ref[:, _cols(s)]).astype(BF16))


def _proj_in_wgrad(dh_b, dkvc_t, dkvp_t, xb, core_chip, comms=()):
    t_tok, d = xb.shape
    d_main, d_kv = dh_b.shape[0], dkvc_t.shape[0]
    rows = (d_main + d_kv) // N_DEV
    whole, cut = d_main // rows, d_main % rows

    def product(shard, dh_ref, dkvc_ref, dkvp_ref, xb_ref, dht_scr, sems):
        copies = [pltpu.make_async_copy(dh_ref.at[j * rows:(j + 1) * rows], dht_scr.at[j], sems.at[j]) for j in range(whole)]
        copies.append(pltpu.make_async_copy(dh_ref.at[whole * rows:d_main], dht_scr.at[whole, 0:cut], sems.at[whole]))

        @pl.when(pl.program_id(0) == 0)
        def _():
            for cp in copies:
                cp.start()

            def store(cols, val):
                dht_scr[whole, cut:rows, cols] = val[0:rows - cut]
                dht_scr[whole + 1, :, cols] = val[rows - cut:]

            _dkv_rows(dkvc_ref, dkvp_ref, t_tok, store)
            for cp in copies:
                cp.wait()

        return _dot(dht_scr[shard], xb_ref[...])

    return _wgrad_pair_sum(
        "proj_in_wgrad", product, (rows, d), [ANY, _resident(dkvc_t), _resident(dkvp_t), _resident(xb)],
        (dh_b, dkvc_t, dkvp_t, xb), core_chip, comms, to_chips=True,
        scratch_shapes=[pltpu.VMEM((N_DEV, rows, t_tok), BF16), pltpu.SemaphoreType.DMA((whole + 1,))])


def _proj_in_dgrad(dh_b, dkvc_t, dkvp_t, dz1, w_in_t, comms=()):
    t_tok, d = dz1.shape
    d_main, d_kv = dh_b.shape[0], dkvc_t.shape[0]
    tm = min(512, t_tok)

    def body(dh_ref, dkvc_ref, dkvp_ref, dz1_ref, w_ref, dx_ref, dkv_scr):
        def store(cols, val):
            dkv_scr[:, cols] = val

        _dkv_rows(dkvc_ref, dkvp_ref, tm, store)
        dx_ref[...] = (ALPHA * dz1_ref[...] + _dot(dh_ref[...], w_ref[0:d_main, :], TN)
                       + _dot(dkv_scr[...], w_ref[d_main:, :], TN))

    return _carry(
        body, name="proj_in_dgrad", grid=(t_tok // tm,), comms=comms,
        in_specs=[pl.BlockSpec((d_main, tm), lambda i: (0, i)), pl.BlockSpec((d_kv, tm), lambda i: (0, i)),
                  pl.BlockSpec((d_kv, tm // MIX_BLOCKS), lambda i: (0, i)),
                  pl.BlockSpec((tm, d), lambda i: (i, 0)), pl.BlockSpec((d_main + d_kv, d), lambda i: (0, 0))],
        out_specs=[pl.BlockSpec((tm, d), lambda i: (i, 0))],
        out_shape=[jax.ShapeDtypeStruct((t_tok, d), F32)],
        scratch_shapes=[pltpu.VMEM((d_kv, tm), BF16)],
        args=(dh_b, dkvc_t, dkvp_t, dz1, w_in_t))


def _adamw(w, g, m, v):
    m = ADAM_B1 * m + (1.0 - ADAM_B1) * g
    v = ADAM_B2 * v + (1.0 - ADAM_B2) * (g * g)
    m_hat = m / (1.0 - ADAM_B1 ** ADAM_STEP)
    v_hat = v / (1.0 - ADAM_B2 ** ADAM_STEP)
    delta = -ADAM_LR * (m_hat / (jnp.sqrt(v_hat) + ADAM_EPS) + ADAM_WD * w)
    return delta, m, v


ADAMW_STEPS = 4


def _adamw_shards(name, items, comms=(), rider=None):
    n_in, n_out = 5 * len(items), 4 * len(items)
    n_rin = len(rider["args"]) if rider else 0

    def body(*refs):
        ins, rins, outs, routs = refs[:n_in], refs[n_in:n_in + n_rin], refs[n_in + n_rin:n_in + n_rin + n_out], refs[n_in + n_rin + n_out:]
        for i in range(len(items)):
            own_ref, recv_ref, w_ref, m_ref, v_ref = ins[5 * i:5 * i + 5]
            g = ((own_ref[...] + recv_ref[0].astype(F32)) + recv_ref[1].astype(F32)) + recv_ref[2].astype(F32)
            for o_ref, val in zip(outs[4 * i:4 * i + 4], (g,) + _adamw(w_ref[...], g, m_ref[...], v_ref[...])):
                o_ref[...] = val
        if rider:
            pl.when(pl.program_id(0) == 0)(lambda: rider["body"](rins, routs))

    in_specs, out_specs, out_shape, args = [], [], [], []
    for own, recv, w, m, v in items:
        r, c = own.shape
        tiles = ADAMW_STEPS
        while (r // tiles) % BF16_ROWS:
            tiles //= 2
        blk = pl.BlockSpec((r // tiles, c), lambda s, k=ADAMW_STEPS // tiles: (s // k, 0))
        in_specs += [blk, pl.BlockSpec((3, r // tiles, c), lambda s, k=ADAMW_STEPS // tiles: (0, s // k, 0)), blk, blk, blk]
        out_specs += [blk] * 4
        out_shape += [jax.ShapeDtypeStruct((r, c), F32)] * 4
        args += [own, recv, w, m, v]
    if rider:
        in_specs, out_specs = in_specs + rider["in_specs"], out_specs + rider["out_specs"]
        out_shape, args = out_shape + rider["out_shape"], args + rider["args"]
    res, per_comm = _carry(body, name=name, grid=(ADAMW_STEPS,), comms=comms, in_specs=in_specs, out_specs=out_specs,
                           out_shape=out_shape, args=args)
    return [res[4 * i:4 * i + 4] for i in range(len(items))], res[n_out:], per_comm


VEC_VLN, VEC_LN1G, VEC_LN1B, VEC_LN2G, VEC_LN2B, VEC_SINK, VEC_LOSS, VEC_BSP, VEC_ROWS = 0, 1, 2, 3, 4, 5, 6, 8, 16


def _adamw_small(parts_w, parts_vec, params):
    n = parts_w.shape[0]
    flat = [a for p in params for a in p]
    shapes = [p[0].shape for p in params]

    def grads(gw, gv):
        return [gw, gv[VEC_VLN:VEC_VLN + 1, 0:D_GMLP], gv[VEC_VLN:VEC_VLN + 1, D_GMLP:2 * D_GMLP],
                gv[VEC_BSP:VEC_BSP + N_HEADS, 0:BLK], gv[VEC_LN1G:VEC_LN1G + 1], gv[VEC_LN1B:VEC_LN1B + 1],
                gv[VEC_LN2G:VEC_LN2G + 1], gv[VEC_LN2B:VEC_LN2B + 1], gv[VEC_SINK:VEC_SINK + 1, 0:N_HEADS]]

    def body(ins, outs):
        (pw_ref, pv_ref), ins = ins[:2], ins[2:]
        gw, gv = pw_ref[0].astype(F32), pv_ref[0]
        for k in range(1, n):
            gw, gv = gw + pw_ref[k].astype(F32), gv + pv_ref[k]
        for i, g in enumerate(grads(gw, gv)):
            w_ref, m_ref, v_ref = ins[3 * i:3 * i + 3]
            delta, m_new, v_new = _adamw(w_ref[...], g, m_ref[...], v_ref[...])
            for o_ref, val in zip(outs[4 * i:4 * i + 4], (g, delta, m_new, v_new)):
                o_ref[...] = val
        outs[-1][...] = gv[VEC_LOSS:VEC_LOSS + 1, 0:LANES]

    whole = lambda shape, **kw: pl.BlockSpec(shape, lambda i: (0,) * len(shape), **kw)
    once = dict(pipeline_mode=pl.Buffered(1))
    return dict(
        body=body, args=[parts_w, parts_vec, *flat],
        in_specs=[whole(parts_w.shape, **once), whole(parts_vec.shape, **once)] + [whole(a.shape, **once) for a in flat],
        out_specs=[whole(s) for s in shapes for _ in range(4)] + [whole((1, LANES))],
        out_shape=[jax.ShapeDtypeStruct(s, F32) for s in shapes for _ in range(4)] + [jax.ShapeDtypeStruct((1, LANES), F32)])


def _pair_sum(name, parts, recv, core_chip, comms=()):
    _, r, c = parts.shape
    tr = r if r <= 512 else 512

    def body(cc_ref, a_ref, b_ref, wire_ref, own_ref):
        s = a_ref[...] + b_ref[...]
        wire_ref[...] = s.astype(BF16)

        @pl.when(pl.program_id(1) == cc_ref[1])
        def _():
            own_ref[...] = s

    return _carry(
        body, name=name, grid=(r // tr, 4), prefetch=(core_chip,), comms=comms,
        in_specs=[pl.BlockSpec((None, tr, c), lambda i, q, cc: (2 * q + cc[0], i, 0)),
                  pl.BlockSpec((None, tr, c), lambda i, q, cc: (q, i, 0))],
        out_specs=[pl.BlockSpec((None, tr, c), lambda i, q, cc: (q, i, 0)), pl.BlockSpec((tr, c), lambda i, q, cc: (i, 0))],
        out_shape=[jax.ShapeDtypeStruct((4, r, c), BF16), jax.ShapeDtypeStruct((r, c), F32)],
        args=(parts, recv))


def kernel(x, positions, w_in, v_ln_g, v_ln_b, w_spatial, b_spatial, sinks, w_out, ln1_g, ln1_b, w_ff1, w_ff2, ln2_g, ln2_b, loss_target, m_w_in, m_v_ln_g, m_v_ln_b, m_w_spatial, m_b_spatial, m_sinks, m_w_out, m_ln1_g, m_ln1_b, m_w_ff1, m_w_ff2, m_ln2_g, m_ln2_b, v_w_in, v_v_ln_g, v_v_ln_b, v_w_spatial, v_b_spatial, v_sinks, v_w_out, v_ln1_g, v_ln1_b, v_w_ff1, v_w_ff2, v_ln2_g, v_ln2_b):
    _, t_tok, d = x.shape
    xi, yi, ci = _place()
    core_chip = jnp.stack([ci, 2 * xi + yi]).astype(jnp.int32)
    x2 = x.reshape(t_tok, d)
    target = loss_target.reshape(t_tok, d)
    inv_freq = ROPE_THETA ** (-jnp.arange(0, HEAD_DIM, 2, dtype=F32) / HEAD_DIM)
    wsp, bsp, sink_vec = w_spatial[0], b_spatial[0], sinks[0]
    vg_col, vb_col = v_ln_g.reshape(D_GMLP, 1), v_ln_b.reshape(D_GMLP, 1)
    big = {"in": w_in[0], "out": w_out[0], "ff1": w_ff1[0], "ff2": w_ff2[0]}
    half1, half2 = big["ff1"].shape[1] // 2, big["ff2"].shape[0] // 2
    w1_mine = [big["ff1"][:, :half1].astype(BF16), big["ff1"][:, half1:].astype(BF16)]
    w2_mine = [big["ff2"][:half2].astype(BF16), big["ff2"][half2:].astype(BF16)]

    (cos_t, sin_t), ((g_in,),) = _rope_tables(
        positions, jnp.tile(inv_freq, 2).reshape(HEAD_DIM, 1), comms=[_gather_comm([big["in"].T.astype(BF16)])])
    w_in_t = g_in.reshape(D_IN, d)
    (h_t, xb), ((g_out, w1_a),) = _proj_in(x2, w_in_t, comms=[_gather_comm([big["out"].astype(BF16), w1_mine[0]])])
    w_out_b = g_out.reshape(-1, d)
    band_bias = _band_bias()
    (cat_t, lse), ((w1_b, w2_a),) = _mixer_fwd(h_t, cos_t, sin_t, wsp, bsp, vg_col, vb_col, sink_vec, band_bias,
                                                comms=[_gather_comm([w1_mine[1], w2_mine[0]])])
    (xhat1, rstd1, x1b), ((w2_b,),) = _proj_out(cat_t, x2, w_out_b, ln1_g, ln1_b, comms=[_gather_comm([w2_mine[1]])])
    act_b, dpre_b, dz2b, dz1, stats = _ffn_fwd_bwd(xhat1, rstd1, x1b, target, [w1_a, w1_b], [w2_a, w2_b], ln1_g, ln1_b, ln2_g, ln2_b)

    (dcat_t, gw_out), _ = _proj_out_bwd(dz1, cat_t, w_out_b)
    p_out = gw_out.reshape(N_DEV, -1, d)
    wire_ff1, own_ff1, ((s_out,),) = _ffn_wgrad("ffn_wgrad1", x1b, dpre_b, False, core_chip, comms=[_sibling_comm([p_out])])
    (wire_out, own_out), _ = _pair_sum("pair_sum_out", p_out, s_out, core_chip)
    wire_ff2, own_ff2, ((r_ff1,),) = _ffn_wgrad("ffn_wgrad2", act_b, dz2b, True, core_chip, comms=[_chips_comm([wire_ff1])])
    (dh_b, dkvc_t, dkvp_t, g_wsp, g_bsp, g_vln, g_sink), ((r_ff2, r_out),) = _mixer_bwd(
        dcat_t, h_t, cos_t, sin_t, wsp, bsp, vg_col, vb_col, sink_vec, band_bias, lse,
        comms=[_chips_comm([wire_ff2, wire_out])])
    sink_row = jnp.pad(g_sink.sum(axis=1).reshape(1, N_HEADS), ((0, 0), (0, d - N_HEADS)))
    small_vec = jnp.concatenate([g_vln[0:2].reshape(1, d), stats[0:4], sink_row, stats[4:5], jnp.zeros((1, d), F32),
                                 jnp.pad(g_bsp, ((0, 0), (0, d - BLK)))], axis=0)
    r_in, own_in, ((parts_w, parts_vec),) = _proj_in_wgrad(
        dh_b, dkvc_t, dkvp_t, xb, core_chip, comms=[_gather_comm([g_wsp.reshape(-1, BLK), small_vec])])
    (grad_x,), _ = _proj_in_dgrad(dh_b, dkvc_t, dkvp_t, dz1, w_in_t)
    small = [(w_spatial, m_w_spatial, v_w_spatial), (v_ln_g, m_v_ln_g, v_v_ln_g), (v_ln_b, m_v_ln_b, v_v_ln_b),
             (b_spatial, m_b_spatial, v_b_spatial), (ln1_g, m_ln1_g, v_ln1_g), (ln1_b, m_ln1_b, v_ln1_b),
             (ln2_g, m_ln2_g, v_ln2_g), (ln2_b, m_ln2_b, v_ln2_b), (sinks, m_sinks, v_sinks)]
    views = [(-1, BLK), None, None, (N_HEADS, BLK)] + [None] * 5
    small_update = _adamw_small(parts_w, parts_vec, [
        tuple(a if vw is None else a.reshape(vw) for a in p) for p, vw in zip(small, views)])
    (out_out, ff1_out, ff2_out, in_out_t), small_res, _ = _adamw_shards("adamw_all", [
        (own_out, r_out, big["out"], m_w_out[0], v_w_out[0]),
        (own_ff1, r_ff1, big["ff1"], m_w_ff1[0], v_w_ff1[0]), (own_ff2, r_ff2, big["ff2"], m_w_ff2[0], v_w_ff2[0]),
        (own_in, r_in, big["in"].T, m_w_in[0].T, v_w_in[0].T)], rider=small_update)
    in_out = [o.T for o in in_out_t]
    small_out = [[o.reshape(p[0].shape) for o in small_res[4 * i:4 * i + 4]] for i, p in enumerate(small)]
    loss = small_res[-1][0, 0]

    big_out = {0: in_out, 6: out_out, 9: ff1_out, 10: ff2_out}
    small_slot = {3: 0, 1: 1, 2: 2, 4: 3, 7: 4, 8: 5, 11: 6, 12: 7, 5: 8}
    outs = [loss, grad_x.reshape(x.shape)]
    for kind in range(4):
        for wi in range(13):
            outs.append(big_out[wi][kind][None] if wi in big_out else small_out[small_slot[wi]][kind])
    return tuple(outs)
```

```python
import math

import jax
import jax.numpy as jnp
from jax import lax
from jax.experimental import pallas as pl
from jax.experimental.pallas import tpu as pltpu

F32 = jnp.float32
BF16 = jnp.bfloat16
MESH = pl.DeviceIdType.MESH

HEAD_DIM = 64
N_HEADS = 8
N_KV_HEADS = 2
BLK = 128
D_GMLP = N_HEADS * HEAD_DIM
D_ATTN = N_HEADS * HEAD_DIM
D_KV = N_KV_HEADS * HEAD_DIM
D_IN = 2 * D_GMLP + D_ATTN + 2 * D_KV
COL_U, COL_V, COL_Q, COL_K = 0, D_GMLP, 2 * D_GMLP, 2 * D_GMLP + D_ATTN
ROPE_THETA = 10000.0
LN_EPS = 1e-5
ALPHA = 2.0 ** 0.25
NEG_INF = -1e30
SCORE_SCALE = 1.0 / math.sqrt(HEAD_DIM)
ADAM_LR, ADAM_B1, ADAM_B2, ADAM_EPS, ADAM_WD, ADAM_STEP = 0.001, 0.9, 0.999, 1e-08, 0.01, 10
N_DEV = 8
LANES = 128
VMEM_LIMIT = 56 * 1024 * 1024
FFN_ROWS = 256

NT = (((1,), (1,)), ((), ()))
TN = (((0,), (0,)), ((), ()))


def _params(*sem):
    return pltpu.CompilerParams(dimension_semantics=sem, vmem_limit_bytes=VMEM_LIMIT)


def _dot(a, b, dims=None):
    if dims is None:
        return jnp.dot(a, b, preferred_element_type=F32)
    return lax.dot_general(a, b, dims, preferred_element_type=F32)


def _mean(a):
    return jnp.mean(a, axis=-1, keepdims=True)


def _ln_fwd(z, g, b):
    zc = z - _mean(z)
    rstd = lax.rsqrt(_mean(zc * zc) + LN_EPS)
    xhat = zc * rstd
    return xhat * g + b, xhat, rstd


def _ln_bwd(dy, xhat, rstd, g):
    dxhat = dy * g
    return rstd * (dxhat - _mean(dxhat) - xhat * _mean(dxhat * xhat))


_GELU_C = math.sqrt(2.0 / math.pi)


def _gelu(x):
    t = jnp.tanh(_GELU_C * (x + 0.044715 * (x * x * x)))
    return 0.5 * x * (1.0 + t)


def _gelu_and_grad(x):
    x2 = x * x
    t = jnp.tanh(_GELU_C * (x + 0.044715 * (x2 * x)))
    hx, ht = 0.5 * x, 0.5 * (1.0 + t)
    return x * ht, ht + hx * (1.0 - t * t) * (_GELU_C * (1.0 + 3.0 * 0.044715 * x2))


def _mean0(a):
    return jnp.mean(a, axis=0, keepdims=True)


def _ln_fwd_t(z, g, b):
    zc = z - _mean0(z)
    rstd = lax.rsqrt(_mean0(zc * zc) + LN_EPS)
    xhat = zc * rstd
    return xhat * g + b, xhat, rstd


def _ln_bwd_t(dy, xhat, rstd, g):
    dxhat = dy * g
    return rstd * (dxhat - _mean0(dxhat) - xhat * _mean0(dxhat * xhat))


def _rope_t(t, cos, sin_signed, bwd=False):
    half = HEAD_DIM // 2
    outs = []
    for r in range(0, t.shape[0], HEAD_DIM):
        th = t[r:r + HEAD_DIM]
        sw = jnp.concatenate([th[half:], th[:half]], axis=0) * sin_signed
        outs.append(th * cos - sw if bwd else th * cos + sw)
    return jnp.concatenate(outs, axis=0)


ANY = pl.BlockSpec(memory_space=pl.ANY)
GATHER_PIECES = 4
BF16_ROWS = 16


def _place():
    return lax.axis_index("x"), lax.axis_index("y"), lax.axis_index("c")


class _Comm:
    def __init__(self, ins, outs, sems, start, finish):
        self.ins, self.outs, self.sems, self.start, self.finish = ins, outs, sems, start, finish


def _gather_comm(arrs):
    n = len(arrs)
    pieces = []
    for a, arr in enumerate(arrs):
        k = GATHER_PIECES
        while arr.shape[0] % (k * BF16_ROWS):
            k //= 2
        pieces += [(a, p * (arr.shape[0] // k), arr.shape[0] // k) for p in range(k)]

    def parts(ins, outs, sems):
        send_sems, recv_sems, local_sems = sems
        x, y, c = _place()
        me, sibling = (x, y, c), (x, y, 1 - c)
        chips = [(1 - x, y), (x, 1 - y), (1 - x, 1 - y)]

        def copy(u, k, block, to, local=False):
            a, r0, nr = pieces[u]
            px, py, pc = block
            dst = outs[a].at[4 * px + 2 * py + pc, pl.ds(r0, nr)]
            return pltpu.make_async_remote_copy(
                src_ref=ins[a].at[pl.ds(r0, nr)] if local else dst, dst_ref=dst,
                send_sem=send_sems.at[u, k], recv_sem=recv_sems.at[u, k], device_id=to, device_id_type=MESH)

        mine = [pltpu.make_async_copy(ins[a], outs[a].at[4 * x + 2 * y + c], local_sems.at[a]) for a in range(n)]
        first = []
        for u in range(len(pieces)):
            first.append(copy(u, 0, me, sibling, local=True))
            first += [copy(u, 1 + j, me, (*chip, c), local=True) for j, chip in enumerate(chips)]
        return copy, mine, first, me, sibling, chips, c

    def start(ins, outs, sems):
        _, mine, first, *_ = parts(ins, outs, sems)
        for cp in mine + first:
            cp.start()

    def finish(ins, outs, sems):
        copy, mine, first, me, sibling, chips, c = parts(ins, outs, sems)
        passed = []
        for u in range(len(pieces)):
            for j, chip in enumerate(chips):
                copy(u, 1 + j, (*chip, c), me).wait_recv()
                fwd = copy(u, 4 + j, (*chip, c), sibling)
                fwd.start()
                passed.append(fwd)
        for u in range(len(pieces)):
            copy(u, 0, sibling, me).wait_recv()
            for j, chip in enumerate(chips):
                copy(u, 4 + j, (*chip, 1 - c), me).wait_recv()
        for cp in first + passed:
            cp.wait_send()
        for cp in mine:
            cp.wait()

    return _Comm(list(arrs), [jax.ShapeDtypeStruct((N_DEV,) + a.shape, a.dtype) for a in arrs],
                 [pltpu.SemaphoreType.DMA((len(pieces), 7)), pltpu.SemaphoreType.DMA((len(pieces), 7)),
                  pltpu.SemaphoreType.DMA((n,))], start, finish)


def _sibling_comm(parts):
    n = len(parts)

    def copies(ins, outs, sems):
        x, y, c = _place()
        return [pltpu.make_async_remote_copy(
            src_ref=ins[a].at[2 * q + (1 - c)], dst_ref=outs[a].at[q],
            send_sem=sems[0].at[a, q], recv_sem=sems[1].at[a, q],
            device_id=(x, y, 1 - c), device_id_type=MESH) for a in range(n) for q in range(4)]

    return _Comm(list(parts), [jax.ShapeDtypeStruct((4,) + p.shape[1:], p.dtype) for p in parts],
                 [pltpu.SemaphoreType.DMA((n, 4)), pltpu.SemaphoreType.DMA((n, 4))],
                 lambda *r: [cp.start() for cp in copies(*r)], lambda *r: [cp.wait() for cp in copies(*r)])


def _chips_comm(chip_parts, rows=None):
    n = len(chip_parts)
    r0, nr = (0, None) if rows is None else rows

    def copies(ins, outs, sems):
        x, y, c = _place()
        chips = [(1 - x, y), (x, 1 - y), (1 - x, 1 - y)]
        src = lambda a, q: ins[a].at[q] if rows is None else ins[a].at[q, pl.ds(r0, nr)]
        return [pltpu.make_async_remote_copy(
            src_ref=src(a, 2 * px + py), dst_ref=outs[a].at[k],
            send_sem=sems[0].at[a, k], recv_sem=sems[1].at[a, k],
            device_id=(px, py, c), device_id_type=MESH) for a in range(n) for k, (px, py) in enumerate(chips)]

    shape = lambda p: (3,) + p.shape[1:] if rows is None else (3, nr) + p.shape[2:]
    return _Comm(list(chip_parts), [jax.ShapeDtypeStruct(shape(p), p.dtype) for p in chip_parts],
                 [pltpu.SemaphoreType.DMA((n, 3)), pltpu.SemaphoreType.DMA((n, 3))],
                 lambda *r: [cp.start() for cp in copies(*r)], lambda *r: [cp.wait() for cp in copies(*r)])


def _flips_comm(sums, first):
    m = sums.shape[0]

    def copies(ins, outs, sems):
        return [pltpu.make_async_remote_copy(
            src_ref=ins[0].at[j], dst_ref=outs[0].at[j], send_sem=sems[0].at[j], recv_sem=sems[1].at[j],
            device_id=_flipped(first + j), device_id_type=MESH) for j in range(m)]

    return _Comm([sums], [jax.ShapeDtypeStruct(sums.shape, sums.dtype)],
                 [pltpu.SemaphoreType.DMA((m,)), pltpu.SemaphoreType.DMA((m,))],
                 lambda *r: [cp.start() for cp in copies(*r)], lambda *r: [cp.wait() for cp in copies(*r)])


def _carry(body, *, name, grid, in_specs, out_specs, out_shape, args, comms=(), scratch_shapes=(), prefetch=()):
    n_pre, n_in, n_out, n_scr = len(prefetch), len(in_specs), len(out_specs), len(scratch_shapes)
    c_ins = [a for cm in comms for a in cm.ins]
    c_outs = [s for cm in comms for s in cm.outs]
    c_sems = [s for cm in comms for s in cm.sems]

    def wrapped(*refs):
        pre, refs = refs[:n_pre], refs[n_pre:]
        ins, refs = refs[:n_in], refs[n_in:]
        cins, refs = refs[:len(c_ins)], refs[len(c_ins):]
        outs, refs = refs[:n_out], refs[n_out:]
        couts, refs = refs[:len(c_outs)], refs[len(c_outs):]
        scr, sems = refs[:n_scr], refs[n_scr:]
        groups, i0, o0, s0 = [], 0, 0, 0
        for cm in comms:
            groups.append((cm, cins[i0:i0 + len(cm.ins)], couts[o0:o0 + len(cm.outs)], sems[s0:s0 + len(cm.sems)]))
            i0, o0, s0 = i0 + len(cm.ins), o0 + len(cm.outs), s0 + len(cm.sems)
        first = pl.program_id(0) == 0
        last = pl.program_id(0) == grid[0] - 1
        for ax in range(1, len(grid)):
            first = first & (pl.program_id(ax) == 0)
            last = last & (pl.program_id(ax) == grid[ax] - 1)
        if comms:
            @pl.when(first)
            def _():
                for cm, ci, co, cs in groups:
                    cm.start(ci, co, cs)
        body(*pre, *ins, *outs, *scr)
        if comms:
            @pl.when(last)
            def _():
                for cm, ci, co, cs in groups:
                    cm.finish(ci, co, cs)

    grid_spec = pltpu.PrefetchScalarGridSpec(
        num_scalar_prefetch=n_pre, grid=grid,
        in_specs=list(in_specs) + [ANY] * len(c_ins), out_specs=list(out_specs) + [ANY] * len(c_outs),
        scratch_shapes=list(scratch_shapes) + c_sems)
    res = pl.pallas_call(
        wrapped, name=name, grid_spec=grid_spec, out_shape=list(out_shape) + c_outs,
        compiler_params=_params(*(["arbitrary"] * len(grid))),
    )(*prefetch, *args, *c_ins)
    outs, rest, per_comm = res[:n_out], res[n_out:], []
    for cm in comms:
        per_comm.append(rest[:len(cm.outs)])
        rest = rest[len(cm.outs):]
    return outs, per_comm


def _rope_tables(pos_row, inv_freq_col, comms=()):
    t_tok = pos_row.shape[1]
    tm = min(512, t_tok)

    def body(pos_ref, invf_ref, cos_ref, sin_ref):
        ang = pos_ref[...].astype(F32) * invf_ref[...]
        row = lax.broadcasted_iota(jnp.int32, ang.shape, 0)
        cos_ref[...] = jnp.cos(ang)
        sin_ref[...] = jnp.sin(ang) * jnp.where(row < HEAD_DIM // 2, -1.0, 1.0)

    return _carry(
        body, name="rope_tables", grid=(t_tok // tm,), comms=comms,
        in_specs=[pl.BlockSpec((1, tm), lambda i: (0, i)), pl.BlockSpec((HEAD_DIM, 1), lambda i: (0, 0))],
        out_specs=[pl.BlockSpec((HEAD_DIM, tm), lambda i: (0, i))] * 2,
        out_shape=[jax.ShapeDtypeStruct((HEAD_DIM, t_tok), F32)] * 2,
        args=(pos_row, inv_freq_col))


def _proj_in(x2, w_in_t, comms=()):
    t_tok, d = x2.shape
    d_in = w_in_t.shape[0]
    tm = min(512, t_tok)

    def body(x_ref, w_ref, h_ref, xb_ref):
        xb = x_ref[...].astype(BF16)
        xb_ref[...] = xb
        h_ref[...] = _dot(w_ref[...], xb, NT)

    return _carry(
        body, name="proj_in", grid=(t_tok // tm,), comms=comms,
        in_specs=[pl.BlockSpec((tm, d), lambda i: (i, 0)), pl.BlockSpec((d_in, d), lambda i: (0, 0))],
        out_specs=[pl.BlockSpec((d_in, tm), lambda i: (0, i)), pl.BlockSpec((tm, d), lambda i: (i, 0))],
        out_shape=[jax.ShapeDtypeStruct((d_in, t_tok), F32), jax.ShapeDtypeStruct((t_tok, d), BF16)],
        args=(x2, w_in_t))


MIX_BLOCKS = 2
MIX_W = MIX_BLOCKS * BLK


def _prev_block(i):
    return jnp.maximum(MIX_BLOCKS * i - 1, 0)


def _h_specs():
    kv_row = COL_K // (2 * D_KV)
    return [
        pl.BlockSpec((D_GMLP, MIX_W), lambda i: (0, i)),
        pl.BlockSpec((D_GMLP, MIX_W), lambda i: (1, i)),
        pl.BlockSpec((D_ATTN, MIX_W), lambda i: (2, i)),
        pl.BlockSpec((2 * D_KV, MIX_W), lambda i: (kv_row, i)),
        pl.BlockSpec((2 * D_KV, BLK), lambda i: (kv_row, _prev_block(i))),
    ]


def _table_specs():
    return [
        pl.BlockSpec((HEAD_DIM, MIX_W), lambda i: (0, i)),
        pl.BlockSpec((HEAD_DIM, MIX_W), lambda i: (0, i)),
        pl.BlockSpec((HEAD_DIM, BLK), lambda i: (0, _prev_block(i))),
        pl.BlockSpec((HEAD_DIM, BLK), lambda i: (0, _prev_block(i))),
    ]


def _cols(b):
    return slice(b * BLK, (b + 1) * BLK)


LSE_ROWS = 8
LSE_SPEC = pl.BlockSpec((LSE_ROWS, D_ATTN), lambda i: (i, 0))


def _block_inputs(b, i, kvc, kvp_ref, cos, sin, cosp_ref, sinp_ref, bias_ref):
    if b == 0:
        kv_prev, cos_prev, sin_prev, bias = kvp_ref[...], cosp_ref[...], sinp_ref[...], bias_ref[jnp.minimum(i, 1)]
    else:
        kv_prev, cos_prev, sin_prev, bias = kvc[:, _cols(b - 1)], cos[:, _cols(b - 1)], sin[:, _cols(b - 1)], bias_ref[1]
    return kvc[:, _cols(b)], kv_prev, cos[:, _cols(b)], sin[:, _cols(b)], cos_prev, sin_prev, bias


def _band_bias():
    ki = lax.broadcasted_iota(jnp.int32, (2, 2 * BLK, BLK), 1)
    qi = lax.broadcasted_iota(jnp.int32, (2, 2 * BLK, BLK), 2)
    later = lax.broadcasted_iota(jnp.int32, (2, 2 * BLK, BLK), 0) > 0
    dist = qi + BLK - ki
    return jnp.where((dist >= 0) & (dist < BLK) & ((ki >= BLK) | later), 0.0, NEG_INF).astype(F32)


BIAS_SPEC = pl.BlockSpec((2, 2 * BLK, BLK), lambda i: (0, 0, 0))


def _keys_values(kvc, kvp, cosc, sinc, cosp, sinp):
    kp, kc = _rope_t(kvp[:D_KV], cosp, sinp), _rope_t(kvc[:D_KV], cosc, sinc)
    k_t = jnp.concatenate([kp, kc], axis=1).astype(BF16)
    k_n = jnp.concatenate([kp.T, kc.T], axis=0).astype(BF16)
    v_t = jnp.concatenate([kvp[D_KV:], kvc[D_KV:]], axis=1).astype(BF16)
    return k_t, k_n, v_t


def _pad_head(th, kv):
    z = jnp.zeros_like(th)
    return jnp.concatenate([th, z] if kv == 0 else [z, th], axis=0)


def _group_lanes(parts):
    return jnp.concatenate(parts, axis=1)


def _softmax_sink_t(s, sink):
    m = jnp.maximum(jnp.max(s, axis=0, keepdims=True), sink)
    e = jnp.exp(s - m)
    denom = jnp.sum(e, axis=0, keepdims=True) + jnp.exp(sink - m)
    return e * (1.0 / denom), m + jnp.log(denom)


def _causal():
    row = lax.broadcasted_iota(jnp.int32, (BLK, BLK), 0)
    col = lax.broadcasted_iota(jnp.int32, (BLK, BLK), 1)
    return row >= col


def _mask_w_once(wsp_ref, wm_scr):
    @pl.when(pl.program_id(0) == 0)
    def _():
        causal = _causal()
        for hh in range(N_HEADS):
            wm_scr[hh] = jnp.where(causal, wsp_ref[hh], 0.0).astype(BF16)


def _mixer_fwd(h_t, cos_t, sin_t, w_spatial, b_spatial, vln_g, vln_b, sinks, band_bias, comms=()):
    t_tok = h_t.shape[1]
    group = N_HEADS // N_KV_HEADS

    def body(sinks_ref, u_ref, vg_ref, q_ref, kvc_ref, kvp_ref, cos_ref, sin_ref, cosp_ref, sinp_ref,
             wsp_ref, bsp_ref, g_ref, b_ref, bias_ref, cat_ref, lse_ref, wm_scr):
        i = pl.program_id(0)
        _mask_w_once(wsp_ref, wm_scr)
        lse_ref[...] = jnp.zeros_like(lse_ref)
        ua = _gelu(u_ref[...])
        vp, _, _ = _ln_fwd_t(_gelu(vg_ref[...]), g_ref[...], b_ref[...])
        vpb = vp.astype(BF16)
        for b in range(MIX_BLOCKS):
            for hh in range(N_HEADS):
                rows = slice(hh * HEAD_DIM, (hh + 1) * HEAD_DIM)
                mixed = _dot(vpb[rows, _cols(b)], wm_scr[hh], NT) + bsp_ref[hh:hh + 1, :]
                cat_ref[rows, _cols(b)] = (ua[rows, _cols(b)] * mixed).astype(BF16)

        kvc, cos, sin = kvc_ref[...], cos_ref[...], sin_ref[...]
        qr = (_rope_t(q_ref[...], cos, sin) * SCORE_SCALE).astype(BF16)
        sinks4 = [_group_lanes([jnp.full((1, BLK), sinks_ref[hh], F32) for hh in range(kv * group, (kv + 1) * group)])
                  for kv in range(N_KV_HEADS)]
        for b in range(MIX_BLOCKS):
            kv_cur, kv_prev, cosc, sinc, cosp, sinp, bias1 = _block_inputs(b, i, kvc, kvp_ref, cos, sin, cosp_ref, sinp_ref, bias_ref)
            _, k_n, v_t = _keys_values(kv_cur, kv_prev, cosc, sinc, cosp, sinp)
            bias = _group_lanes([bias1] * group)
            for kv in range(N_KV_HEADS):
                heads = range(kv * group, (kv + 1) * group)
                qs = _group_lanes([qr[hh * HEAD_DIM:(hh + 1) * HEAD_DIM, _cols(b)] for hh in heads])
                p, lse = _softmax_sink_t(_dot(k_n, _pad_head(qs, kv)) + bias, sinks4[kv])
                lse_ref[b * N_KV_HEADS + kv:b * N_KV_HEADS + kv + 1, :] = lse
                o = _dot(v_t[kv * HEAD_DIM:(kv + 1) * HEAD_DIM], p.astype(BF16)).astype(BF16)
                for j, hh in enumerate(heads):
                    cat_ref[D_GMLP + hh * HEAD_DIM:D_GMLP + (hh + 1) * HEAD_DIM, _cols(b)] = o[:, j * BLK:(j + 1) * BLK]

    full = lambda shape: pl.BlockSpec(shape, lambda i: (0,) * len(shape))
    return _carry(
        body, name="mixer_fwd", grid=(t_tok // MIX_W,), comms=comms,
        in_specs=[pl.BlockSpec(memory_space=pltpu.SMEM)] + _h_specs() + _table_specs() + [
            full((N_HEADS, BLK, BLK)), full((N_HEADS, BLK)), full((D_GMLP, 1)), full((D_GMLP, 1)), BIAS_SPEC],
        out_specs=[pl.BlockSpec((D_GMLP + D_ATTN, MIX_W), lambda i: (0, i)), LSE_SPEC],
        out_shape=[jax.ShapeDtypeStruct((D_GMLP + D_ATTN, t_tok), BF16),
                   jax.ShapeDtypeStruct((t_tok // MIX_W * LSE_ROWS, D_ATTN), F32)],
        scratch_shapes=[pltpu.VMEM((N_HEADS, BLK, BLK), BF16)],
        args=(sinks, h_t, h_t, h_t, h_t, h_t, cos_t, sin_t, cos_t, sin_t, w_spatial, b_spatial, vln_g, vln_b, band_bias))


def _proj_out(cat_t, x2, w_out_b, ln1_g, ln1_b, comms=()):
    t_tok, d = x2.shape
    tm = min(512, t_tok)

    def body(cat_ref, x_ref, w_ref, g_ref, b_ref, xhat_ref, rstd_ref, x1b_ref):
        x1, xhat, rstd = _ln_fwd(ALPHA * x_ref[...] + _dot(cat_ref[...], w_ref[...], TN), g_ref[...], b_ref[...])
        xhat_ref[...] = xhat
        rstd_ref[...] = rstd
        x1b_ref[...] = x1.astype(BF16)

    tok = lambda w: pl.BlockSpec((tm, w), lambda i: (i, 0))
    vec = pl.BlockSpec((1, d), lambda i: (0, 0))
    return _carry(
        body, name="proj_out", grid=(t_tok // tm,), comms=comms,
        in_specs=[pl.BlockSpec((cat_t.shape[0], tm), lambda i: (0, i)), tok(d), pl.BlockSpec(w_out_b.shape, lambda i: (0, 0)), vec, vec],
        out_specs=[tok(d), tok(1), tok(d)],
        out_shape=[jax.ShapeDtypeStruct((t_tok, d), F32), jax.ShapeDtypeStruct((t_tok, 1), F32), jax.ShapeDtypeStruct((t_tok, d), BF16)],
        args=(cat_t, x2, w_out_b, ln1_g, ln1_b))


def _ffn_fwd_bwd(xhat1, rstd1, x1b, target, w1_parts, w2_parts, ln1_g, ln1_b, ln2_g, ln2_b):
    t_tok, d = xhat1.shape
    n_part = len(w1_parts)
    n_chunk, _, fp = w1_parts[0].shape
    f = n_chunk * n_part * fp
    tm = min(FFN_ROWS, t_tok)

    def body(xhat1_ref, rstd1_ref, x1b_ref, tgt_ref, *refs):
        w1_hbm, w2_hbm = refs[:n_part], refs[n_part:2 * n_part]
        (g1_ref, b1_ref, g2_ref, b2_ref, act_ref, dpre_ref, dz2b_ref, dz1_ref, stats_ref,
         r_scr, w1_ref, w2_ref, w_sems) = refs[2 * n_part:]

        @pl.when(pl.program_id(0) == 0)
        def _():
            stats_ref[...] = jnp.zeros_like(stats_ref)
            loads = []
            for j in range(n_chunk):
                for p in range(n_part):
                    units = pl.ds((j * n_part + p) * fp, fp)
                    loads.append(pltpu.make_async_copy(w1_hbm[p].at[j], w1_ref.at[:, units], w_sems.at[0, p, j]))
                    loads.append(pltpu.make_async_copy(w2_hbm[p].at[j], w2_ref.at[units, :], w_sems.at[1, p, j]))
            for cp in loads:
                cp.start()
            for cp in loads:
                cp.wait()

        g1, g2 = g1_ref[...], g2_ref[...]
        xhat1 = xhat1_ref[...]
        r_scr[...] = jnp.maximum(_dot(x1b_ref[...], w1_ref[...]), 0.0)
        r = r_scr[...]
        act = (r * r).astype(BF16)
        act_ref[...] = act
        ff = _dot(act, w2_ref[...])
        y, xhat2, rstd2 = _ln_fwd(ALPHA * (xhat1 * g1 + b1_ref[...]) + ff, g2, b2_ref[...])
        diff = y - tgt_ref[...]
        loss = 0.5 * jnp.sum(jnp.sum(diff * diff, axis=-1, keepdims=True) / d, axis=0, keepdims=True)
        dy = diff / d
        dz2 = _ln_bwd(dy, xhat2, rstd2, g2)
        dz2b = dz2.astype(BF16)
        dz2b_ref[...] = dz2b
        dpre = (_dot(dz2b, w2_ref[...], NT) * (2.0 * r_scr[...])).astype(BF16)
        dpre_ref[...] = dpre
        dx1 = ALPHA * dz2 + _dot(dpre, w1_ref[...], NT)
        dz1_ref[...] = _ln_bwd(dx1, xhat1, rstd1_ref[...], g1)
        stats_ref[0:1, :] += jnp.sum(dx1 * xhat1, axis=0, keepdims=True)
        stats_ref[1:2, :] += jnp.sum(dx1, axis=0, keepdims=True)
        stats_ref[2:3, :] += jnp.sum(dy * xhat2, axis=0, keepdims=True)
        stats_ref[3:4, :] += jnp.sum(dy, axis=0, keepdims=True)
        stats_ref[4:5, :] += jnp.broadcast_to(loss, (1, d))

    tok = lambda w: pl.BlockSpec((tm, w), lambda i: (i, 0))
    vec = pl.BlockSpec((1, d), lambda i: (0, 0))
    return _carry(
        body, name="ffn_fwd_bwd", grid=(t_tok // tm,),
        in_specs=[tok(d), tok(1), tok(d), tok(d)] + [ANY] * (2 * n_part) + [vec, vec, vec, vec],
        out_specs=[tok(f), tok(f), tok(d), tok(d), pl.BlockSpec((8, d), lambda i: (0, 0))],
        out_shape=[jax.ShapeDtypeStruct((t_tok, f), BF16), jax.ShapeDtypeStruct((t_tok, f), BF16),
                   jax.ShapeDtypeStruct((t_tok, d), BF16), jax.ShapeDtypeStruct((t_tok, d), F32), jax.ShapeDtypeStruct((8, d), F32)],
        scratch_shapes=[pltpu.VMEM((tm, f), F32), pltpu.VMEM((d, f), BF16), pltpu.VMEM((f, d), BF16),
                        pltpu.SemaphoreType.DMA((2, n_part, n_chunk))],
        args=(xhat1, rstd1, x1b, target, *w1_parts, *w2_parts, ln1_g, ln1_b, ln2_g, ln2_b))[0]


WGRAD_STEPS = [(True, 0), (True, 1), (False, 0), (True, 2), (False, 1), (True, 3), (False, 2), (False, 3)]
CHIP_FLIPS = [3, 1, 2, 0]


def _pick(table, s):
    out = table[-1]
    for i in range(len(table) - 2, -1, -1):
        out = jnp.where(s == i, table[i], out)
    return out


def _wgrad_shard(s, cc):
    q = jnp.bitwise_xor(cc[1], _pick([CHIP_FLIPS[k] for _, k in WGRAD_STEPS], s))
    return 2 * q + jnp.where(_pick([int(sibling) for sibling, _ in WGRAD_STEPS], s) == 1, 1 - cc[0], cc[0])


def _flipped(k):
    x, y, c = _place()
    return (1 - x if CHIP_FLIPS[k] // 2 else x, 1 - y if CHIP_FLIPS[k] % 2 else y, c)


def _wgrad_pair_sum(name, product, chunk, in_specs, args, core_chip, n_sent, comms=(), scratch_shapes=()):
    half = N_DEV // 2
    n_in, n_out = len(in_specs), 2 + (0 < n_sent) + (n_sent < half - 1)

    def body(cc_ref, *refs):
        ins, outs, scr = refs[:n_in], refs[n_in:n_in + n_out], refs[n_in + n_out:]
        (own_ref, recv_ref), from_chips_ref, wire_ref = outs[-2:], outs[0], outs[n_out - 3]
        send_buf, got, send_sems, recv_sems, got_sem, wire_buf, leave_sems, arrive_sems = scr[:8]
        s = pl.program_id(0)
        x, y, c = _place()
        def send(q):
            return pltpu.make_async_remote_copy(
                src_ref=send_buf.at[q % 2], dst_ref=recv_ref.at[q], send_sem=send_sems.at[q], recv_sem=recv_sems.at[q],
                device_id=(x, y, 1 - c), device_id_type=MESH)

        def load(q):
            return pltpu.make_async_copy(recv_ref.at[q], got, got_sem.at[0])

        def leave(k):
            if k < n_sent:
                return pltpu.make_async_remote_copy(
                    src_ref=wire_buf.at[k], dst_ref=from_chips_ref.at[k], send_sem=leave_sems.at[k],
                    recv_sem=arrive_sems.at[k], device_id=_flipped(k), device_id_type=MESH)
            return pltpu.make_async_copy(wire_buf.at[k], wire_ref.at[k - n_sent], leave_sems.at[k])

        for step, (sibling, q) in enumerate(WGRAD_STEPS):
            if not sibling:
                @pl.when(s == step)
                def _(q=q):
                    send(q).wait_recv()
                    load(q).start()

        g = product(_wgrad_shard(s, cc_ref), *ins, *scr[8:])

        for step, (sibling, q) in enumerate(WGRAD_STEPS):
            @pl.when(s == step)
            def _(sibling=sibling, q=q):
                if sibling:
                    if q >= 2:
                        send(q - 2).wait_send()
                    send_buf[q % 2] = g
                    send(q).start()
                    return
                load(q).wait()
                total = g + got[...]
                if q < half - 1:
                    wire_buf[q] = total.astype(BF16)
                    leave(q).start()
                else:
                    own_ref[...] = total

        @pl.when(s == N_DEV - 1)
        def _():
            for q in range(half - 2, half):
                send(q).wait_send()
            for k in range(half - 1):
                leave(k).wait()

    sums = lambda n: [jax.ShapeDtypeStruct((n,) + chunk, BF16)] if n else []
    sem = lambda n: pltpu.SemaphoreType.DMA((n,))
    res, per_comm = _carry(
        body, name=name, grid=(N_DEV,), comms=comms, prefetch=(core_chip,), in_specs=in_specs,
        out_specs=[ANY] * (n_out - 2) + [pl.BlockSpec(chunk, lambda s, cc: (0, 0)), ANY],
        out_shape=sums(n_sent) + sums(half - 1 - n_sent) + [jax.ShapeDtypeStruct(chunk, F32),
                                                            jax.ShapeDtypeStruct((half,) + chunk, F32)],
        scratch_shapes=[pltpu.VMEM((2,) + chunk, F32), pltpu.VMEM(chunk, F32), sem(half), sem(half), sem(1),
                        pltpu.VMEM((half - 1,) + chunk, BF16), sem(half - 1), sem(half - 1), *scratch_shapes],
        args=args)
    return res[0] if n_sent else None, res[n_out - 3] if n_sent < half - 1 else None, res[-2], per_comm


def _resident(a):
    return pl.BlockSpec(a.shape, lambda s, cc: (0,) * a.ndim, pipeline_mode=pl.Buffered(1))


def _ffn_wgrad(name, lhs, rhs, chunk_lhs, core_chip, n_sent, comms=()):
    t_tok = lhs.shape[0]
    fc = (lhs if chunk_lhs else rhs).shape[1] // N_DEV
    chunked = pl.BlockSpec((t_tok, fc), lambda s, cc: (0, _wgrad_shard(s, cc)))

    def product(shard, lhs_ref, rhs_ref):
        return _dot(lhs_ref[...], rhs_ref[...], TN)

    return _wgrad_pair_sum(
        name, product, (fc, rhs.shape[1]) if chunk_lhs else (lhs.shape[1], fc),
        [chunked, _resident(rhs)] if chunk_lhs else [_resident(lhs), chunked], (lhs, rhs), core_chip, n_sent, comms)


def _proj_out_bwd(dz1, cat_t, w_out_b, comms=()):
    t_tok, d = dz1.shape
    d_mix = cat_t.shape[0]
    tm = min(512, t_tok)

    def body(dz1_ref, cat_ref, w_ref, dcat_ref, gw_ref):
        @pl.when(pl.program_id(0) == 0)
        def _():
            gw_ref[...] = jnp.zeros_like(gw_ref)

        dzb = dz1_ref[...].astype(BF16)
        dcat_ref[...] = _dot(w_ref[...], dzb, NT)
        gw_ref[...] += _dot(cat_ref[...], dzb)

    return _carry(
        body, name="proj_out_bwd", grid=(t_tok // tm,), comms=comms,
        in_specs=[pl.BlockSpec((tm, d), lambda i: (i, 0)), pl.BlockSpec((d_mix, tm), lambda i: (0, i)),
                  pl.BlockSpec((d_mix, d), lambda i: (0, 0))],
        out_specs=[pl.BlockSpec((d_mix, tm), lambda i: (0, i)), pl.BlockSpec((d_mix, d), lambda i: (0, 0))],
        out_shape=[jax.ShapeDtypeStruct((d_mix, t_tok), F32), jax.ShapeDtypeStruct((d_mix, d), F32)],
        args=(dz1, cat_t, w_out_b))


def _mixer_bwd(dcat_t, h_t, cos_t, sin_t, w_spatial, b_spatial, vln_g, vln_b, sinks, band_bias, lse, comms=()):
    t_tok = h_t.shape[1]
    nb, n_step = t_tok // BLK, t_tok // MIX_W
    group = N_HEADS // N_KV_HEADS

    def body(sinks_ref, dcat_ref, u_ref, vg_ref, q_ref, kvc_ref, kvp_ref, cos_ref, sin_ref, cosp_ref, sinp_ref,
             wsp_ref, bsp_ref, g_ref, b_ref, bias_ref, lse_ref, dh_ref, dkvc_ref, dkvp_ref, gwsb_ref, gbsp_ref, gvln_ref, gsink_ref,
             dg_acc, db_acc, wm_scr, gws_ref):
        i = pl.program_id(0)

        @pl.when(i == 0)
        def _():
            gws_ref[...] = jnp.zeros_like(gws_ref)
            gbsp_ref[...] = jnp.zeros_like(gbsp_ref)
            gsink_ref[...] = jnp.zeros_like(gsink_ref)
            dg_acc[...] = jnp.zeros_like(dg_acc)
            db_acc[...] = jnp.zeros_like(db_acc)

        _mask_w_once(wsp_ref, wm_scr)

        g = g_ref[...]
        ua, ua_grad = _gelu_and_grad(u_ref[...])
        vv, vv_grad = _gelu_and_grad(vg_ref[...])
        vp, vhat, rstd = _ln_fwd_t(vv, g, b_ref[...])
        vpb = vp.astype(BF16)
        da = dcat_ref[0:D_GMLP, :]
        dmixed = da * ua
        dvp_blocks = []
        for b in range(MIX_BLOCKS):
            dvp_parts = []
            for hh in range(N_HEADS):
                rows = slice(hh * HEAD_DIM, (hh + 1) * HEAD_DIM)
                vpb_h = vpb[rows, _cols(b)]
                mixed = _dot(vpb_h, wm_scr[hh], NT) + bsp_ref[hh:hh + 1, :]
                dh_ref[COL_U + hh * HEAD_DIM:COL_U + (hh + 1) * HEAD_DIM, _cols(b)] = (
                    da[rows, _cols(b)] * mixed * ua_grad[rows, _cols(b)]).astype(BF16)
                dm = dmixed[rows, _cols(b)]
                dmb = dm.astype(BF16)
                gbsp_ref[hh:hh + 1, :] += jnp.sum(dm, axis=0, keepdims=True)
                gws_ref[hh] += _dot(dmb, vpb_h, TN)
                dvp_parts.append(_dot(dmb, wm_scr[hh]))
            dvp_blocks.append(jnp.concatenate(dvp_parts, axis=0))
        dvp = jnp.concatenate(dvp_blocks, axis=1)
        dgv, dbv = dvp * vhat, dvp
        for b in range(MIX_BLOCKS):
            dg_acc[...] += dgv[:, _cols(b)]
            db_acc[...] += dbv[:, _cols(b)]
        dh_ref[COL_V:COL_V + D_GMLP, :] = (_ln_bwd_t(dvp, vhat, rstd, g) * vv_grad).astype(BF16)

        kvc, cos, sin = kvc_ref[...], cos_ref[...], sin_ref[...]
        qr = (_rope_t(q_ref[...], cos, sin) * SCORE_SCALE).astype(BF16)
        sinks4 = [_group_lanes([jnp.full((1, BLK), sinks_ref[hh], F32) for hh in range(kv * group, (kv + 1) * group)])
                  for kv in range(N_KV_HEADS)]
        dq_blocks, dkv_cur, dkv_prev = [], [], []
        for b in range(MIX_BLOCKS):
            kv_cur, kv_prev, cosc, sinc, cosp, sinp, bias1 = _block_inputs(b, i, kvc, kvp_ref, cos, sin, cosp_ref, sinp_ref, bias_ref)
            k_t, k_n, v_t = _keys_values(kv_cur, kv_prev, cosc, sinc, cosp, sinp)
            v_n = jnp.concatenate([kv_prev[D_KV:].T, kv_cur[D_KV:].T], axis=0).astype(BF16)
            bias = _group_lanes([bias1] * group)
            dk, dv, dq_parts = [], [], []
            for kv in range(N_KV_HEADS):
                heads = range(kv * group, (kv + 1) * group)
                kv_rows = slice(kv * HEAD_DIM, (kv + 1) * HEAD_DIM)
                qs = _group_lanes([qr[hh * HEAD_DIM:(hh + 1) * HEAD_DIM, _cols(b)] for hh in heads])
                dos = _group_lanes([dcat_ref[D_GMLP + hh * HEAD_DIM:D_GMLP + (hh + 1) * HEAD_DIM, _cols(b)]
                                    for hh in heads]).astype(BF16)
                lse_g = lse_ref[b * N_KV_HEADS + kv:b * N_KV_HEADS + kv + 1, :]
                p = jnp.exp(_dot(k_n, _pad_head(qs, kv)) + bias - lse_g)
                p_sink = jnp.exp(sinks4[kv] - lse_g)
                dp = _dot(v_n, _pad_head(dos, kv))
                delta = jnp.sum(p * dp, axis=0, keepdims=True)
                ds = (p * (dp - delta)).astype(BF16)
                dsink = p_sink * delta
                dq = _dot(k_t[kv_rows], ds) * SCORE_SCALE
                for j, hh in enumerate(heads):
                    gsink_ref[hh:hh + 1, :] -= dsink[:, j * BLK:(j + 1) * BLK]
                    dq_parts.append(dq[:, j * BLK:(j + 1) * BLK])
                dk.append(_dot(qs, ds, NT))
                dv.append(_dot(dos, p.astype(BF16), NT))
            dq_blocks.append(jnp.concatenate(dq_parts, axis=0))
            dk_all, dv_all = jnp.concatenate(dk, axis=0), jnp.concatenate(dv, axis=0)
            dkv_cur.append(jnp.concatenate([_rope_t(dk_all[:, BLK:], cosc, sinc, bwd=True), dv_all[:, BLK:]], axis=0))
            dkv_prev.append(jnp.concatenate([_rope_t(dk_all[:, :BLK], cosp, sinp, bwd=True), dv_all[:, :BLK]], axis=0))
        dh_ref[COL_Q:COL_Q + D_ATTN, :] = _rope_t(jnp.concatenate(dq_blocks, axis=1), cos, sin, bwd=True).astype(BF16)
        for b in range(MIX_BLOCKS):
            dkvc_ref[:, _cols(b)] = dkv_cur[b] + dkv_prev[b + 1] if b + 1 < MIX_BLOCKS else dkv_cur[b]
        dkvp_ref[...] = dkv_prev[0]

        @pl.when(i == n_step - 1)
        def _():
            causal = _causal()
            for hh in range(N_HEADS):
                gwsb_ref[hh] = jnp.where(causal, gws_ref[hh], 0.0).astype(BF16)
            gvln_ref[...] = jnp.zeros_like(gvln_ref)
            gvln_ref[0:1, :] = jnp.sum(dg_acc[...].T, axis=0, keepdims=True)
            gvln_ref[1:2, :] = jnp.sum(db_acc[...].T, axis=0, keepdims=True)

    full = lambda shape: pl.BlockSpec(shape, lambda i: (0,) * len(shape))
    return _carry(
        body, name="mixer_bwd", grid=(n_step,), comms=comms,
        in_specs=[pl.BlockSpec(memory_space=pltpu.SMEM), pl.BlockSpec((D_GMLP + D_ATTN, MIX_W), lambda i: (0, i))]
        + _h_specs() + _table_specs()
        + [full((N_HEADS, BLK, BLK)), full((N_HEADS, BLK)), full((D_GMLP, 1)), full((D_GMLP, 1)), BIAS_SPEC, LSE_SPEC],
        out_specs=[pl.BlockSpec((COL_K, MIX_W), lambda i: (0, i)), pl.BlockSpec((2 * D_KV, MIX_W), lambda i: (0, i)),
                   pl.BlockSpec((2 * D_KV, BLK), lambda i: (0, (i + n_step - 1) % n_step)),
                   full((N_HEADS, BLK, BLK)), full((N_HEADS, BLK)), full((8, D_GMLP)), full((N_HEADS, LANES))],
        out_shape=[jax.ShapeDtypeStruct((COL_K, t_tok), BF16), jax.ShapeDtypeStruct((2 * D_KV, t_tok), F32),
                   jax.ShapeDtypeStruct((2 * D_KV, n_step * BLK), F32),
                   jax.ShapeDtypeStruct((N_HEADS, BLK, BLK), BF16), jax.ShapeDtypeStruct((N_HEADS, BLK), F32),
                   jax.ShapeDtypeStruct((8, D_GMLP), F32), jax.ShapeDtypeStruct((N_HEADS, LANES), F32)],
        scratch_shapes=[pltpu.VMEM((D_GMLP, BLK), F32), pltpu.VMEM((D_GMLP, BLK), F32), pltpu.VMEM((N_HEADS, BLK, BLK), BF16),
                        pltpu.VMEM((N_HEADS, BLK, BLK), F32)],
        args=(sinks, dcat_t, h_t, h_t, h_t, h_t, h_t, cos_t, sin_t, cos_t, sin_t, w_spatial, b_spatial, vln_g, vln_b, band_bias, lse))


def _dkv_rows(dkvc_ref, dkvp_ref, width, store):
    for s in range(width // MIX_W):
        rest, last = slice(s * MIX_W, (s + 1) * MIX_W - BLK), slice((s + 1) * MIX_W - BLK, (s + 1) * MIX_W)
        store(rest, dkvc_ref[:, rest].astype(BF16))
        store(last, (dkvc_ref[:, last] + dkvp_ref[:, _cols(s)]).astype(BF16))


def _proj_in_wgrad(dh_b, dkvc_t, dkvp_t, xb, core_chip, comms=()):
    t_tok, d = xb.shape
    d_main, d_kv = dh_b.shape[0], dkvc_t.shape[0]
    rows = (d_main + d_kv) // N_DEV
    whole, cut = d_main // rows, d_main % rows

    def product(shard, dh_ref, dkvc_ref, dkvp_ref, xb_ref, dht_scr, sems):
        copies = [pltpu.make_async_copy(dh_ref.at[j * rows:(j + 1) * rows], dht_scr.at[j], sems.at[j]) for j in range(whole)]
        copies.append(pltpu.make_async_copy(dh_ref.at[whole * rows:d_main], dht_scr.at[whole, 0:cut], sems.at[whole]))

        @pl.when(pl.program_id(0) == 0)
        def _():
            for cp in copies:
                cp.start()

            def store(cols, val):
                dht_scr[whole, cut:rows, cols] = val[0:rows - cut]
                dht_scr[whole + 1, :, cols] = val[rows - cut:]

            _dkv_rows(dkvc_ref, dkvp_ref, t_tok, store)
            for cp in copies:
                cp.wait()

        return _dot(dht_scr[shard], xb_ref[...])

    return _wgrad_pair_sum(
        "proj_in_wgrad", product, (rows, d), [ANY, _resident(dkvc_t), _resident(dkvp_t), _resident(xb)],
        (dh_b, dkvc_t, dkvp_t, xb), core_chip, N_DEV // 2 - 1, comms,
        scratch_shapes=[pltpu.VMEM((N_DEV, rows, t_tok), BF16), pltpu.SemaphoreType.DMA((whole + 1,))])


def _proj_in_dgrad(dh_b, dkvc_t, dkvp_t, dz1, w_in_t, comms=()):
    t_tok, d = dz1.shape
    d_main, d_kv = dh_b.shape[0], dkvc_t.shape[0]
    tm = min(512, t_tok)

    def body(dh_ref, dkvc_ref, dkvp_ref, dz1_ref, w_ref, dx_ref, dkv_scr):
        def store(cols, val):
            dkv_scr[:, cols] = val

        _dkv_rows(dkvc_ref, dkvp_ref, tm, store)
        dx_ref[...] = (ALPHA * dz1_ref[...] + _dot(dh_ref[...], w_ref[0:d_main, :], TN)
                       + _dot(dkv_scr[...], w_ref[d_main:, :], TN))

    return _carry(
        body, name="proj_in_dgrad", grid=(t_tok // tm,), comms=comms,
        in_specs=[pl.BlockSpec((d_main, tm), lambda i: (0, i)), pl.BlockSpec((d_kv, tm), lambda i: (0, i)),
                  pl.BlockSpec((d_kv, tm // MIX_BLOCKS), lambda i: (0, i)),
                  pl.BlockSpec((tm, d), lambda i: (i, 0)), pl.BlockSpec((d_main + d_kv, d), lambda i: (0, 0))],
        out_specs=[pl.BlockSpec((tm, d), lambda i: (i, 0))],
        out_shape=[jax.ShapeDtypeStruct((t_tok, d), F32)],
        scratch_shapes=[pltpu.VMEM((d_kv, tm), BF16)],
        args=(dh_b, dkvc_t, dkvp_t, dz1, w_in_t))


def _adamw(w, g, m, v):
    m = ADAM_B1 * m + (1.0 - ADAM_B1) * g
    v = ADAM_B2 * v + (1.0 - ADAM_B2) * (g * g)
    m_hat = m / (1.0 - ADAM_B1 ** ADAM_STEP)
    v_hat = v / (1.0 - ADAM_B2 ** ADAM_STEP)
    delta = -ADAM_LR * (m_hat / (jnp.sqrt(v_hat) + ADAM_EPS) + ADAM_WD * w)
    return delta, m, v


ADAMW_STEPS = 4


def _adamw_shards(name, items, comms=(), rider=None):
    n_in, n_out = sum(4 + len(it[1]) for it in items), 4 * len(items)
    n_rin = len(rider["args"]) if rider else 0

    def body(*refs):
        ins, rins, outs, routs = refs[:n_in], refs[n_in:n_in + n_rin], refs[n_in + n_rin:n_in + n_rin + n_out], refs[n_in + n_rin + n_out:]
        for i, item in enumerate(items):
            (own_ref, w_ref, m_ref, v_ref), recv_refs, ins = ins[:4], ins[4:4 + len(item[1])], ins[4 + len(item[1]):]
            g = own_ref[...]
            for recv_ref in recv_refs:
                for k in range(recv_ref.shape[0]):
                    g = g + recv_ref[k].astype(F32)
            for o_ref, val in zip(outs[4 * i:4 * i + 4], (g,) + _adamw(w_ref[...], g, m_ref[...], v_ref[...])):
                o_ref[...] = val
        if rider:
            pl.when(pl.program_id(0) == 0)(lambda: rider["body"](rins, routs))

    in_specs, out_specs, out_shape, args = [], [], [], []
    for own, recvs, w, m, v in items:
        r, c = own.shape
        tiles = ADAMW_STEPS
        while (r // tiles) % BF16_ROWS:
            tiles //= 2
        blk = pl.BlockSpec((r // tiles, c), lambda s, k=ADAMW_STEPS // tiles: (s // k, 0))
        in_specs += [blk] * 4 + [pl.BlockSpec((a.shape[0], r // tiles, c), lambda s, k=ADAMW_STEPS // tiles: (0, s // k, 0))
                                 for a in recvs]
        out_specs += [blk] * 4
        out_shape += [jax.ShapeDtypeStruct((r, c), F32)] * 4
        args += [own, w, m, v, *recvs]
    if rider:
        in_specs, out_specs = in_specs + rider["in_specs"], out_specs + rider["out_specs"]
        out_shape, args = out_shape + rider["out_shape"], args + rider["args"]
    res, per_comm = _carry(body, name=name, grid=(ADAMW_STEPS,), comms=comms, in_specs=in_specs, out_specs=out_specs,
                           out_shape=out_shape, args=args)
    return [res[4 * i:4 * i + 4] for i in range(len(items))], res[n_out:], per_comm


VEC_VLN, VEC_LN1G, VEC_LN1B, VEC_LN2G, VEC_LN2B, VEC_SINK, VEC_LOSS, VEC_BSP, VEC_ROWS = 0, 1, 2, 3, 4, 5, 6, 8, 16


def _adamw_small(parts_w, parts_vec, params):
    n = parts_w.shape[0]
    flat = [a for p in params for a in p]
    shapes = [p[0].shape for p in params]

    def grads(gw, gv):
        return [gw, gv[VEC_VLN:VEC_VLN + 1, 0:D_GMLP], gv[VEC_VLN:VEC_VLN + 1, D_GMLP:2 * D_GMLP],
                gv[VEC_BSP:VEC_BSP + N_HEADS, 0:BLK], gv[VEC_LN1G:VEC_LN1G + 1], gv[VEC_LN1B:VEC_LN1B + 1],
                gv[VEC_LN2G:VEC_LN2G + 1], gv[VEC_LN2B:VEC_LN2B + 1], gv[VEC_SINK:VEC_SINK + 1, 0:N_HEADS]]

    def body(ins, outs):
        (pw_ref, pv_ref), ins = ins[:2], ins[2:]
        gw, gv = pw_ref[0].astype(F32), pv_ref[0]
        for k in range(1, n):
            gw, gv = gw + pw_ref[k].astype(F32), gv + pv_ref[k]
        for i, g in enumerate(grads(gw, gv)):
            w_ref, m_ref, v_ref = ins[3 * i:3 * i + 3]
            delta, m_new, v_new = _adamw(w_ref[...], g, m_ref[...], v_ref[...])
            for o_ref, val in zip(outs[4 * i:4 * i + 4], (g, delta, m_new, v_new)):
                o_ref[...] = val
        outs[-1][...] = gv[VEC_LOSS:VEC_LOSS + 1, 0:LANES]

    whole = lambda shape, **kw: pl.BlockSpec(shape, lambda i: (0,) * len(shape), **kw)
    once = dict(pipeline_mode=pl.Buffered(1))
    return dict(
        body=body, args=[parts_w, parts_vec, *flat],
        in_specs=[whole(parts_w.shape, **once), whole(parts_vec.shape, **once)] + [whole(a.shape, **once) for a in flat],
        out_specs=[whole(s) for s in shapes for _ in range(4)] + [whole((1, LANES))],
        out_shape=[jax.ShapeDtypeStruct(s, F32) for s in shapes for _ in range(4)] + [jax.ShapeDtypeStruct((1, LANES), F32)])


def _pair_sum(name, parts, recv, core_chip, comms=()):
    _, r, c = parts.shape
    tr = r if r <= 512 else 512

    def body(cc_ref, a_ref, b_ref, wire_ref, own_ref):
        s = a_ref[...] + b_ref[...]
        wire_ref[...] = s.astype(BF16)

        @pl.when(pl.program_id(1) == cc_ref[1])
        def _():
            own_ref[...] = s

    return _carry(
        body, name=name, grid=(r // tr, 4), prefetch=(core_chip,), comms=comms,
        in_specs=[pl.BlockSpec((None, tr, c), lambda i, q, cc: (2 * q + cc[0], i, 0)),
                  pl.BlockSpec((None, tr, c), lambda i, q, cc: (q, i, 0))],
        out_specs=[pl.BlockSpec((None, tr, c), lambda i, q, cc: (q, i, 0)), pl.BlockSpec((tr, c), lambda i, q, cc: (i, 0))],
        out_shape=[jax.ShapeDtypeStruct((4, r, c), BF16), jax.ShapeDtypeStruct((r, c), F32)],
        args=(parts, recv))


def kernel(x, positions, w_in, v_ln_g, v_ln_b, w_spatial, b_spatial, sinks, w_out, ln1_g, ln1_b, w_ff1, w_ff2, ln2_g, ln2_b, loss_target, m_w_in, m_v_ln_g, m_v_ln_b, m_w_spatial, m_b_spatial, m_sinks, m_w_out, m_ln1_g, m_ln1_b, m_w_ff1, m_w_ff2, m_ln2_g, m_ln2_b, v_w_in, v_v_ln_g, v_v_ln_b, v_w_spatial, v_b_spatial, v_sinks, v_w_out, v_ln1_g, v_ln1_b, v_w_ff1, v_w_ff2, v_ln2_g, v_ln2_b):
    _, t_tok, d = x.shape
    xi, yi, ci = _place()
    core_chip = jnp.stack([ci, 2 * xi + yi]).astype(jnp.int32)
    x2 = x.reshape(t_tok, d)
    target = loss_target.reshape(t_tok, d)
    inv_freq = ROPE_THETA ** (-jnp.arange(0, HEAD_DIM, 2, dtype=F32) / HEAD_DIM)
    wsp, bsp, sink_vec = w_spatial[0], b_spatial[0], sinks[0]
    vg_col, vb_col = v_ln_g.reshape(D_GMLP, 1), v_ln_b.reshape(D_GMLP, 1)
    big = {"in": w_in[0], "out": w_out[0], "ff1": w_ff1[0], "ff2": w_ff2[0]}
    half1, half2 = big["ff1"].shape[1] // 2, big["ff2"].shape[0] // 2
    w1_mine = [big["ff1"][:, :half1].astype(BF16), big["ff1"][:, half1:].astype(BF16)]
    w2_mine = [big["ff2"][:half2].astype(BF16), big["ff2"][half2:].astype(BF16)]

    (cos_t, sin_t), ((g_in,),) = _rope_tables(
        positions, jnp.tile(inv_freq, 2).reshape(HEAD_DIM, 1), comms=[_gather_comm([big["in"].T.astype(BF16)])])
    w_in_t = g_in.reshape(D_IN, d)
    (h_t, xb), ((g_out, w1_a),) = _proj_in(x2, w_in_t, comms=[_gather_comm([big["out"].astype(BF16), w1_mine[0]])])
    w_out_b = g_out.reshape(-1, d)
    band_bias = _band_bias()
    (cat_t, lse), ((w1_b, w2_a),) = _mixer_fwd(h_t, cos_t, sin_t, wsp, bsp, vg_col, vb_col, sink_vec, band_bias,
                                                comms=[_gather_comm([w1_mine[1], w2_mine[0]])])
    (xhat1, rstd1, x1b), ((w2_b,),) = _proj_out(cat_t, x2, w_out_b, ln1_g, ln1_b, comms=[_gather_comm([w2_mine[1]])])
    act_b, dpre_b, dz2b, dz1, stats = _ffn_fwd_bwd(xhat1, rstd1, x1b, target, [w1_a, w1_b], [w2_a, w2_b], ln1_g, ln1_b, ln2_g, ln2_b)

    (dcat_t, gw_out), _ = _proj_out_bwd(dz1, cat_t, w_out_b)
    p_out = gw_out.reshape(N_DEV, -1, d)
    r_ff1_a, wire_ff1, own_ff1, ((s_out,),) = _ffn_wgrad(
        "ffn_wgrad1", x1b, dpre_b, False, core_chip, 1, comms=[_sibling_comm([p_out])])
    (wire_out, own_out), _ = _pair_sum("pair_sum_out", p_out, s_out, core_chip)
    _, wire_ff2, own_ff2, ((r_ff1_b,),) = _ffn_wgrad(
        "ffn_wgrad2", act_b, dz2b, True, core_chip, 0, comms=[_flips_comm(wire_ff1, 1)])
    (dh_b, dkvc_t, dkvp_t, g_wsp, g_bsp, g_vln, g_sink), ((r_ff2,), (r_out,)) = _mixer_bwd(
        dcat_t, h_t, cos_t, sin_t, wsp, bsp, vg_col, vb_col, sink_vec, band_bias, lse,
        comms=[_flips_comm(wire_ff2, 0), _chips_comm([wire_out])])
    sink_row = jnp.pad(g_sink.sum(axis=1).reshape(1, N_HEADS), ((0, 0), (0, d - N_HEADS)))
    small_vec = jnp.concatenate([g_vln[0:2].reshape(1, d), stats[0:4], sink_row, stats[4:5], jnp.zeros((1, d), F32),
                                 jnp.pad(g_bsp, ((0, 0), (0, d - BLK)))], axis=0)
    r_in, _, own_in, ((parts_w, parts_vec),) = _proj_in_wgrad(
        dh_b, dkvc_t, dkvp_t, xb, core_chip, comms=[_gather_comm([g_wsp.reshape(-1, BLK), small_vec])])
    (grad_x,), _ = _proj_in_dgrad(dh_b, dkvc_t, dkvp_t, dz1, w_in_t)
    small = [(w_spatial, m_w_spatial, v_w_spatial), (v_ln_g, m_v_ln_g, v_v_ln_g), (v_ln_b, m_v_ln_b, v_v_ln_b),
             (b_spatial, m_b_spatial, v_b_spatial), (ln1_g, m_ln1_g, v_ln1_g), (ln1_b, m_ln1_b, v_ln1_b),
             (ln2_g, m_ln2_g, v_ln2_g), (ln2_b, m_ln2_b, v_ln2_b), (sinks, m_sinks, v_sinks)]
    views = [(-1, BLK), None, None, (N_HEADS, BLK)] + [None] * 5
    small_update = _adamw_small(parts_w, parts_vec, [
        tuple(a if vw is None else a.reshape(vw) for a in p) for p, vw in zip(small, views)])
    (out_out, ff1_out, ff2_out, in_out_t), small_res, _ = _adamw_shards("adamw_all", [
        (own_out, [r_out], big["out"], m_w_out[0], v_w_out[0]),
        (own_ff1, [r_ff1_a, r_ff1_b], big["ff1"], m_w_ff1[0], v_w_ff1[0]),
        (own_ff2, [r_ff2], big["ff2"], m_w_ff2[0], v_w_ff2[0]),
        (own_in, [r_in], big["in"].T, m_w_in[0].T, v_w_in[0].T)], rider=small_update)
    in_out = [o.T for o in in_out_t]
    small_out = [[o.reshape(p[0].shape) for o in small_res[4 * i:4 * i + 4]] for i, p in enumerate(small)]
    loss = small_res[-1][0, 0]

    big_out = {0: in_out, 6: out_out, 9: ff1_out, 10: ff2_out}
    small_slot = {3: 0, 1: 1, 2: 2, 4: 3, 7: 4, 8: 5, 11: 6, 12: 7, 5: 8}
    outs = [loss, grad_x.reshape(x.shape)]
    for kind in range(4):
        for wi in range(13):
            outs.append(big_out[wi][kind][None] if wi in big_out else small_out[small_slot[wi]][kind])
    return tuple(outs)
```

```python
import math

import jax
import jax.numpy as jnp
from jax import lax
from jax.experimental import pallas as pl
from jax.experimental.pallas import tpu as pltpu

F32 = jnp.float32
BF16 = jnp.bfloat16
MESH = pl.DeviceIdType.MESH

HEAD_DIM = 64
N_HEADS = 8
N_KV_HEADS = 2
BLK = 128
D_GMLP = N_HEADS * HEAD_DIM
D_ATTN = N_HEADS * HEAD_DIM
D_KV = N_KV_HEADS * HEAD_DIM
D_IN = 2 * D_GMLP + D_ATTN + 2 * D_KV
COL_U, COL_V, COL_Q, COL_K = 0, D_GMLP, 2 * D_GMLP, 2 * D_GMLP + D_ATTN
ROPE_THETA = 10000.0
LN_EPS = 1e-5
ALPHA = 2.0 ** 0.25
NEG_INF = -1e30
SCORE_SCALE = 1.0 / math.sqrt(HEAD_DIM)
ADAM_LR, ADAM_B1, ADAM_B2, ADAM_EPS, ADAM_WD, ADAM_STEP = 0.001, 0.9, 0.999, 1e-08, 0.01, 10
N_DEV = 8
LANES = 128
VMEM_LIMIT = 56 * 1024 * 1024
FFN_ROWS = 256

NT = (((1,), (1,)), ((), ()))
TN = (((0,), (0,)), ((), ()))


def _params(*sem):
    return pltpu.CompilerParams(dimension_semantics=sem, vmem_limit_bytes=VMEM_LIMIT)


def _dot(a, b, dims=None):
    if dims is None:
        return jnp.dot(a, b, preferred_element_type=F32)
    return lax.dot_general(a, b, dims, preferred_element_type=F32)


def _mean(a):
    return jnp.mean(a, axis=-1, keepdims=True)


def _ln_fwd(z, g, b):
    zc = z - _mean(z)
    rstd = lax.rsqrt(_mean(zc * zc) + LN_EPS)
    xhat = zc * rstd
    return xhat * g + b, xhat, rstd


def _ln_bwd(dy, xhat, rstd, g):
    dxhat = dy * g
    return rstd * (dxhat - _mean(dxhat) - xhat * _mean(dxhat * xhat))


_GELU_C = math.sqrt(2.0 / math.pi)


def _gelu(x):
    t = jnp.tanh(_GELU_C * (x + 0.044715 * (x * x * x)))
    return 0.5 * x * (1.0 + t)


def _gelu_and_grad(x):
    x2 = x * x
    t = jnp.tanh(_GELU_C * (x + 0.044715 * (x2 * x)))
    hx, ht = 0.5 * x, 0.5 * (1.0 + t)
    return x * ht, ht + hx * (1.0 - t * t) * (_GELU_C * (1.0 + 3.0 * 0.044715 * x2))


def _mean0(a):
    return jnp.mean(a, axis=0, keepdims=True)


def _ln_fwd_t(z, g, b):
    zc = z - _mean0(z)
    rstd = lax.rsqrt(_mean0(zc * zc) + LN_EPS)
    xhat = zc * rstd
    return xhat * g + b, xhat, rstd


def _ln_bwd_t(dy, xhat, rstd, g):
    dxhat = dy * g
    return rstd * (dxhat - _mean0(dxhat) - xhat * _mean0(dxhat * xhat))


def _rope_t(t, cos, sin_signed, bwd=False):
    half = HEAD_DIM // 2
    outs = []
    for r in range(0, t.shape[0], HEAD_DIM):
        th = t[r:r + HEAD_DIM]
        sw = jnp.concatenate([th[half:], th[:half]], axis=0) * sin_signed
        outs.append(th * cos - sw if bwd else th * cos + sw)
    return jnp.concatenate(outs, axis=0)


ANY = pl.BlockSpec(memory_space=pl.ANY)
GATHER_PIECES = 4
BF16_ROWS = 16


def _place():
    return lax.axis_index("x"), lax.axis_index("y"), lax.axis_index("c")


class _Comm:
    def __init__(self, ins, outs, sems, start, finish):
        self.ins, self.outs, self.sems, self.start, self.finish = ins, outs, sems, start, finish


def _gather_comm(arrs):
    n = len(arrs)
    pieces = []
    for a, arr in enumerate(arrs):
        k = GATHER_PIECES
        while arr.shape[0] % (k * BF16_ROWS):
            k //= 2
        pieces += [(a, p * (arr.shape[0] // k), arr.shape[0] // k) for p in range(k)]

    def parts(ins, outs, sems):
        send_sems, recv_sems, local_sems = sems
        x, y, c = _place()
        me, sibling = (x, y, c), (x, y, 1 - c)
        chips = [(1 - x, y), (x, 1 - y), (1 - x, 1 - y)]

        def copy(u, k, block, to, local=False):
            a, r0, nr = pieces[u]
            px, py, pc = block
            dst = outs[a].at[4 * px + 2 * py + pc, pl.ds(r0, nr)]
            return pltpu.make_async_remote_copy(
                src_ref=ins[a].at[pl.ds(r0, nr)] if local else dst, dst_ref=dst,
                send_sem=send_sems.at[u, k], recv_sem=recv_sems.at[u, k], device_id=to, device_id_type=MESH)

        mine = [pltpu.make_async_copy(ins[a], outs[a].at[4 * x + 2 * y + c], local_sems.at[a]) for a in range(n)]
        first = []
        for u in range(len(pieces)):
            first.append(copy(u, 0, me, sibling, local=True))
            first += [copy(u, 1 + j, me, (*chip, c), local=True) for j, chip in enumerate(chips)]
        return copy, mine, first, me, sibling, chips, c

    def start(ins, outs, sems):
        _, mine, first, *_ = parts(ins, outs, sems)
        for cp in mine + first:
            cp.start()

    def finish(ins, outs, sems):
        copy, mine, first, me, sibling, chips, c = parts(ins, outs, sems)
        passed = []
        for u in range(len(pieces)):
            for j, chip in enumerate(chips):
                copy(u, 1 + j, (*chip, c), me).wait_recv()
                fwd = copy(u, 4 + j, (*chip, c), sibling)
                fwd.start()
                passed.append(fwd)
        for u in range(len(pieces)):
            copy(u, 0, sibling, me).wait_recv()
            for j, chip in enumerate(chips):
                copy(u, 4 + j, (*chip, 1 - c), me).wait_recv()
        for cp in first + passed:
            cp.wait_send()
        for cp in mine:
            cp.wait()

    return _Comm(list(arrs), [jax.ShapeDtypeStruct((N_DEV,) + a.shape, a.dtype) for a in arrs],
                 [pltpu.SemaphoreType.DMA((len(pieces), 7)), pltpu.SemaphoreType.DMA((len(pieces), 7)),
                  pltpu.SemaphoreType.DMA((n,))], start, finish)


def _sibling_comm(parts):
    n = len(parts)

    def copies(ins, outs, sems):
        x, y, c = _place()
        return [pltpu.make_async_remote_copy(
            src_ref=ins[a].at[2 * q + (1 - c)], dst_ref=outs[a].at[q],
            send_sem=sems[0].at[a, q], recv_sem=sems[1].at[a, q],
            device_id=(x, y, 1 - c), device_id_type=MESH) for a in range(n) for q in range(4)]

    return _Comm(list(parts), [jax.ShapeDtypeStruct((4,) + p.shape[1:], p.dtype) for p in parts],
                 [pltpu.SemaphoreType.DMA((n, 4)), pltpu.SemaphoreType.DMA((n, 4))],
                 lambda *r: [cp.start() for cp in copies(*r)], lambda *r: [cp.wait() for cp in copies(*r)])


def _chips_comm(chip_parts):
    n = len(chip_parts)

    def copies(ins, outs, sems):
        x, y, c = _place()
        chips = [(1 - x, y), (x, 1 - y), (1 - x, 1 - y)]
        return [pltpu.make_async_remote_copy(
            src_ref=ins[a].at[2 * px + py], dst_ref=outs[a].at[k],
            send_sem=sems[0].at[a, k], recv_sem=sems[1].at[a, k],
            device_id=(px, py, c), device_id_type=MESH) for a in range(n) for k, (px, py) in enumerate(chips)]

    return _Comm(list(chip_parts), [jax.ShapeDtypeStruct((3,) + p.shape[1:], p.dtype) for p in chip_parts],
                 [pltpu.SemaphoreType.DMA((n, 3)), pltpu.SemaphoreType.DMA((n, 3))],
                 lambda *r: [cp.start() for cp in copies(*r)], lambda *r: [cp.wait() for cp in copies(*r)])


def _flips_comm(sums, first):
    m = sums.shape[0]

    def copies(ins, outs, sems):
        return [pltpu.make_async_remote_copy(
            src_ref=ins[0].at[j], dst_ref=outs[0].at[j], send_sem=sems[0].at[j], recv_sem=sems[1].at[j],
            device_id=_flipped(first + j), device_id_type=MESH) for j in range(m)]

    return _Comm([sums], [jax.ShapeDtypeStruct(sums.shape, sums.dtype)],
                 [pltpu.SemaphoreType.DMA((m,)), pltpu.SemaphoreType.DMA((m,))],
                 lambda *r: [cp.start() for cp in copies(*r)], lambda *r: [cp.wait() for cp in copies(*r)])


def _carry(body, *, name, grid, in_specs, out_specs, out_shape, args, comms=(), scratch_shapes=(), prefetch=()):
    n_pre, n_in, n_out, n_scr = len(prefetch), len(in_specs), len(out_specs), len(scratch_shapes)
    c_ins = [a for cm in comms for a in cm.ins]
    c_outs = [s for cm in comms for s in cm.outs]
    c_sems = [s for cm in comms for s in cm.sems]

    def wrapped(*refs):
        pre, refs = refs[:n_pre], refs[n_pre:]
        ins, refs = refs[:n_in], refs[n_in:]
        cins, refs = refs[:len(c_ins)], refs[len(c_ins):]
        outs, refs = refs[:n_out], refs[n_out:]
        couts, refs = refs[:len(c_outs)], refs[len(c_outs):]
        scr, sems = refs[:n_scr], refs[n_scr:]
        groups, i0, o0, s0 = [], 0, 0, 0
        for cm in comms:
            groups.append((cm, cins[i0:i0 + len(cm.ins)], couts[o0:o0 + len(cm.outs)], sems[s0:s0 + len(cm.sems)]))
            i0, o0, s0 = i0 + len(cm.ins), o0 + len(cm.outs), s0 + len(cm.sems)
        first = pl.program_id(0) == 0
        last = pl.program_id(0) == grid[0] - 1
        for ax in range(1, len(grid)):
            first = first & (pl.program_id(ax) == 0)
            last = last & (pl.program_id(ax) == grid[ax] - 1)
        if comms:
            @pl.when(first)
            def _():
                for cm, ci, co, cs in groups:
                    cm.start(ci, co, cs)
        body(*pre, *ins, *outs, *scr)
        if comms:
            @pl.when(last)
            def _():
                for cm, ci, co, cs in groups:
                    cm.finish(ci, co, cs)

    grid_spec = pltpu.PrefetchScalarGridSpec(
        num_scalar_prefetch=n_pre, grid=grid,
        in_specs=list(in_specs) + [ANY] * len(c_ins), out_specs=list(out_specs) + [ANY] * len(c_outs),
        scratch_shapes=list(scratch_shapes) + c_sems)
    res = pl.pallas_call(
        wrapped, name=name, grid_spec=grid_spec, out_shape=list(out_shape) + c_outs,
        compiler_params=_params(*(["arbitrary"] * len(grid))),
    )(*prefetch, *args, *c_ins)
    outs, rest, per_comm = res[:n_out], res[n_out:], []
    for cm in comms:
        per_comm.append(rest[:len(cm.outs)])
        rest = rest[len(cm.outs):]
    return outs, per_comm


def _rope_tables(pos_row, inv_freq_col, comms=()):
    t_tok = pos_row.shape[1]
    tm = min(512, t_tok)

    def body(pos_ref, invf_ref, cos_ref, sin_ref):
        ang = pos_ref[...].astype(F32) * invf_ref[...]
        row = lax.broadcasted_iota(jnp.int32, ang.shape, 0)
        cos_ref[...] = jnp.cos(ang)
        sin_ref[...] = jnp.sin(ang) * jnp.where(row < HEAD_DIM // 2, -1.0, 1.0)

    return _carry(
        body, name="rope_tables", grid=(t_tok // tm,), comms=comms,
        in_specs=[pl.BlockSpec((1, tm), lambda i: (0, i)), pl.BlockSpec((HEAD_DIM, 1), lambda i: (0, 0))],
        out_specs=[pl.BlockSpec((HEAD_DIM, tm), lambda i: (0, i))] * 2,
        out_shape=[jax.ShapeDtypeStruct((HEAD_DIM, t_tok), F32)] * 2,
        args=(pos_row, inv_freq_col))


def _proj_in(x2, w_in_t, comms=()):
    t_tok, d = x2.shape
    d_in = w_in_t.shape[0]
    tm = min(512, t_tok)

    def body(x_ref, w_ref, h_ref, xb_ref):
        xb = x_ref[...].astype(BF16)
        xb_ref[...] = xb
        h_ref[...] = _dot(w_ref[...], xb, NT)

    return _carry(
        body, name="proj_in", grid=(t_tok // tm,), comms=comms,
        in_specs=[pl.BlockSpec((tm, d), lambda i: (i, 0)), pl.BlockSpec((d_in, d), lambda i: (0, 0))],
        out_specs=[pl.BlockSpec((d_in, tm), lambda i: (0, i)), pl.BlockSpec((tm, d), lambda i: (i, 0))],
        out_shape=[jax.ShapeDtypeStruct((d_in, t_tok), F32), jax.ShapeDtypeStruct((t_tok, d), BF16)],
        args=(x2, w_in_t))


MIX_BLOCKS = 2
MIX_W = MIX_BLOCKS * BLK


def _prev_block(i):
    return jnp.maximum(MIX_BLOCKS * i - 1, 0)


def _h_specs():
    kv_row = COL_K // (2 * D_KV)
    return [
        pl.BlockSpec((D_GMLP, MIX_W), lambda i: (0, i)),
        pl.BlockSpec((D_GMLP, MIX_W), lambda i: (1, i)),
        pl.BlockSpec((D_ATTN, MIX_W), lambda i: (2, i)),
        pl.BlockSpec((2 * D_KV, MIX_W), lambda i: (kv_row, i)),
        pl.BlockSpec((2 * D_KV, BLK), lambda i: (kv_row, _prev_block(i))),
    ]


def _table_specs():
    return [
        pl.BlockSpec((HEAD_DIM, MIX_W), lambda i: (0, i)),
        pl.BlockSpec((HEAD_DIM, MIX_W), lambda i: (0, i)),
        pl.BlockSpec((HEAD_DIM, BLK), lambda i: (0, _prev_block(i))),
        pl.BlockSpec((HEAD_DIM, BLK), lambda i: (0, _prev_block(i))),
    ]


def _cols(b):
    return slice(b * BLK, (b + 1) * BLK)


LSE_ROWS = 8
LSE_SPEC = pl.BlockSpec((LSE_ROWS, D_ATTN), lambda i: (i, 0))


def _block_inputs(b, i, kvc, kvp_ref, cos, sin, cosp_ref, sinp_ref, bias_ref):
    if b == 0:
        kv_prev, cos_prev, sin_prev, bias = kvp_ref[...], cosp_ref[...], sinp_ref[...], bias_ref[jnp.minimum(i, 1)]
    else:
        kv_prev, cos_prev, sin_prev, bias = kvc[:, _cols(b - 1)], cos[:, _cols(b - 1)], sin[:, _cols(b - 1)], bias_ref[1]
    return kvc[:, _cols(b)], kv_prev, cos[:, _cols(b)], sin[:, _cols(b)], cos_prev, sin_prev, bias


def _band_bias():
    ki = lax.broadcasted_iota(jnp.int32, (2, 2 * BLK, BLK), 1)
    qi = lax.broadcasted_iota(jnp.int32, (2, 2 * BLK, BLK), 2)
    later = lax.broadcasted_iota(jnp.int32, (2, 2 * BLK, BLK), 0) > 0
    dist = qi + BLK - ki
    return jnp.where((dist >= 0) & (dist < BLK) & ((ki >= BLK) | later), 0.0, NEG_INF).astype(F32)


BIAS_SPEC = pl.BlockSpec((2, 2 * BLK, BLK), lambda i: (0, 0, 0))


def _keys_values(kvc, kvp, cosc, sinc, cosp, sinp):
    kp, kc = _rope_t(kvp[:D_KV], cosp, sinp), _rope_t(kvc[:D_KV], cosc, sinc)
    k_t = jnp.concatenate([kp, kc], axis=1).astype(BF16)
    k_n = jnp.concatenate([kp.T, kc.T], axis=0).astype(BF16)
    v_t = jnp.concatenate([kvp[D_KV:], kvc[D_KV:]], axis=1).astype(BF16)
    return k_t, k_n, v_t


def _pad_head(th, kv):
    z = jnp.zeros_like(th)
    return jnp.concatenate([th, z] if kv == 0 else [z, th], axis=0)


def _group_lanes(parts):
    return jnp.concatenate(parts, axis=1)


def _softmax_sink_t(s, sink):
    m = jnp.maximum(jnp.max(s, axis=0, keepdims=True), sink)
    e = jnp.exp(s - m)
    denom = jnp.sum(e, axis=0, keepdims=True) + jnp.exp(sink - m)
    return e * (1.0 / denom), m + jnp.log(denom)


def _causal():
    row = lax.broadcasted_iota(jnp.int32, (BLK, BLK), 0)
    col = lax.broadcasted_iota(jnp.int32, (BLK, BLK), 1)
    return row >= col


def _mask_w_once(wsp_ref, wm_scr):
    @pl.when(pl.program_id(0) == 0)
    def _():
        causal = _causal()
        for hh in range(N_HEADS):
            wm_scr[hh] = jnp.where(causal, wsp_ref[hh], 0.0).astype(BF16)


def _mixer_fwd(h_t, cos_t, sin_t, w_spatial, b_spatial, vln_g, vln_b, sinks, band_bias, comms=()):
    t_tok = h_t.shape[1]
    group = N_HEADS // N_KV_HEADS

    def body(sinks_ref, u_ref, vg_ref, q_ref, kvc_ref, kvp_ref, cos_ref, sin_ref, cosp_ref, sinp_ref,
             wsp_ref, bsp_ref, g_ref, b_ref, bias_ref, cat_ref, lse_ref, wm_scr):
        i = pl.program_id(0)
        _mask_w_once(wsp_ref, wm_scr)
        lse_ref[...] = jnp.zeros_like(lse_ref)
        ua = _gelu(u_ref[...])
        vp, _, _ = _ln_fwd_t(_gelu(vg_ref[...]), g_ref[...], b_ref[...])
        vpb = vp.astype(BF16)
        for b in range(MIX_BLOCKS):
            for hh in range(N_HEADS):
                rows = slice(hh * HEAD_DIM, (hh + 1) * HEAD_DIM)
                mixed = _dot(vpb[rows, _cols(b)], wm_scr[hh], NT) + bsp_ref[hh:hh + 1, :]
                cat_ref[rows, _cols(b)] = (ua[rows, _cols(b)] * mixed).astype(BF16)

        kvc, cos, sin = kvc_ref[...], cos_ref[...], sin_ref[...]
        qr = (_rope_t(q_ref[...], cos, sin) * SCORE_SCALE).astype(BF16)
        sinks4 = [_group_lanes([jnp.full((1, BLK), sinks_ref[hh], F32) for hh in range(kv * group, (kv + 1) * group)])
                  for kv in range(N_KV_HEADS)]
        for b in range(MIX_BLOCKS):
            kv_cur, kv_prev, cosc, sinc, cosp, sinp, bias1 = _block_inputs(b, i, kvc, kvp_ref, cos, sin, cosp_ref, sinp_ref, bias_ref)
            _, k_n, v_t = _keys_values(kv_cur, kv_prev, cosc, sinc, cosp, sinp)
            bias = _group_lanes([bias1] * group)
            for kv in range(N_KV_HEADS):
                heads = range(kv * group, (kv + 1) * group)
                qs = _group_lanes([qr[hh * HEAD_DIM:(hh + 1) * HEAD_DIM, _cols(b)] for hh in heads])
                p, lse = _softmax_sink_t(_dot(k_n, _pad_head(qs, kv)) + bias, sinks4[kv])
                lse_ref[b * N_KV_HEADS + kv:b * N_KV_HEADS + kv + 1, :] = lse
                o = _dot(v_t[kv * HEAD_DIM:(kv + 1) * HEAD_DIM], p.astype(BF16)).astype(BF16)
                for j, hh in enumerate(heads):
                    cat_ref[D_GMLP + hh * HEAD_DIM:D_GMLP + (hh + 1) * HEAD_DIM, _cols(b)] = o[:, j * BLK:(j + 1) * BLK]

    full = lambda shape: pl.BlockSpec(shape, lambda i: (0,) * len(shape))
    return _carry(
        body, name="mixer_fwd", grid=(t_tok // MIX_W,), comms=comms,
        in_specs=[pl.BlockSpec(memory_space=pltpu.SMEM)] + _h_specs() + _table_specs() + [
            full((N_HEADS, BLK, BLK)), full((N_HEADS, BLK)), full((D_GMLP, 1)), full((D_GMLP, 1)), BIAS_SPEC],
        out_specs=[pl.BlockSpec((D_GMLP + D_ATTN, MIX_W), lambda i: (0, i)), LSE_SPEC],
        out_shape=[jax.ShapeDtypeStruct((D_GMLP + D_ATTN, t_tok), BF16),
                   jax.ShapeDtypeStruct((t_tok // MIX_W * LSE_ROWS, D_ATTN), F32)],
        scratch_shapes=[pltpu.VMEM((N_HEADS, BLK, BLK), BF16)],
        args=(sinks, h_t, h_t, h_t, h_t, h_t, cos_t, sin_t, cos_t, sin_t, w_spatial, b_spatial, vln_g, vln_b, band_bias))


def _proj_out(cat_t, x2, w_out_b, ln1_g, ln1_b, comms=()):
    t_tok, d = x2.shape
    tm = min(512, t_tok)

    def body(cat_ref, x_ref, w_ref, g_ref, b_ref, xhat_ref, rstd_ref, x1b_ref):
        x1, xhat, rstd = _ln_fwd(ALPHA * x_ref[...] + _dot(cat_ref[...], w_ref[...], TN), g_ref[...], b_ref[...])
        xhat_ref[...] = xhat
        rstd_ref[...] = rstd
        x1b_ref[...] = x1.astype(BF16)

    tok = lambda w: pl.BlockSpec((tm, w), lambda i: (i, 0))
    vec = pl.BlockSpec((1, d), lambda i: (0, 0))
    return _carry(
        body, name="proj_out", grid=(t_tok // tm,), comms=comms,
        in_specs=[pl.BlockSpec((cat_t.shape[0], tm), lambda i: (0, i)), tok(d), pl.BlockSpec(w_out_b.shape, lambda i: (0, 0)), vec, vec],
        out_specs=[tok(d), tok(1), tok(d)],
        out_shape=[jax.ShapeDtypeStruct((t_tok, d), F32), jax.ShapeDtypeStruct((t_tok, 1), F32), jax.ShapeDtypeStruct((t_tok, d), BF16)],
        args=(cat_t, x2, w_out_b, ln1_g, ln1_b))


def _ffn_fwd_bwd(xhat1, rstd1, x1b, target, w1_parts, w2_parts, ln1_g, ln1_b, ln2_g, ln2_b):
    t_tok, d = xhat1.shape
    n_part = len(w1_parts)
    n_chunk, _, fp = w1_parts[0].shape
    f = n_chunk * n_part * fp
    tm = min(FFN_ROWS, t_tok)

    def body(xhat1_ref, rstd1_ref, x1b_ref, tgt_ref, *refs):
        w1_hbm, w2_hbm = refs[:n_part], refs[n_part:2 * n_part]
        (g1_ref, b1_ref, g2_ref, b2_ref, act_ref, dpre_ref, dz2b_ref, dz1_ref, stats_ref,
         r_scr, w1_ref, w2_ref, w_sems) = refs[2 * n_part:]

        @pl.when(pl.program_id(0) == 0)
        def _():
            stats_ref[...] = jnp.zeros_like(stats_ref)
            loads = []
            for j in range(n_chunk):
                for p in range(n_part):
                    units = pl.ds((j * n_part + p) * fp, fp)
                    loads.append(pltpu.make_async_copy(w1_hbm[p].at[j], w1_ref.at[:, units], w_sems.at[0, p, j]))
                    loads.append(pltpu.make_async_copy(w2_hbm[p].at[j], w2_ref.at[units, :], w_sems.at[1, p, j]))
            for cp in loads:
                cp.start()
            for cp in loads:
                cp.wait()

        g1, g2 = g1_ref[...], g2_ref[...]
        xhat1 = xhat1_ref[...]
        r_scr[...] = jnp.maximum(_dot(x1b_ref[...], w1_ref[...]), 0.0)
        r = r_scr[...]
        act = (r * r).astype(BF16)
        act_ref[...] = act
        ff = _dot(act, w2_ref[...])
        y, xhat2, rstd2 = _ln_fwd(ALPHA * (xhat1 * g1 + b1_ref[...]) + ff, g2, b2_ref[...])
        diff = y - tgt_ref[...]
        loss = 0.5 * jnp.sum(jnp.sum(diff * diff, axis=-1, keepdims=True) / d, axis=0, keepdims=True)
        dy = diff / d
        dz2 = _ln_bwd(dy, xhat2, rstd2, g2)
        dz2b = dz2.astype(BF16)
        dz2b_ref[...] = dz2b
        dpre = (_dot(dz2b, w2_ref[...], NT) * (2.0 * r_scr[...])).astype(BF16)
        dpre_ref[...] = dpre
        dx1 = ALPHA * dz2 + _dot(dpre, w1_ref[...], NT)
        dz1_ref[...] = _ln_bwd(dx1, xhat1, rstd1_ref[...], g1)
        stats_ref[0:1, :] += jnp.sum(dx1 * xhat1, axis=0, keepdims=True)
        stats_ref[1:2, :] += jnp.sum(dx1, axis=0, keepdims=True)
        stats_ref[2:3, :] += jnp.sum(dy * xhat2, axis=0, keepdims=True)
        stats_ref[3:4, :] += jnp.sum(dy, axis=0, keepdims=True)
        stats_ref[4:5, :] += jnp.broadcast_to(loss, (1, d))

    tok = lambda w: pl.BlockSpec((tm, w), lambda i: (i, 0))
    vec = pl.BlockSpec((1, d), lambda i: (0, 0))
    return _carry(
        body, name="ffn_fwd_bwd", grid=(t_tok // tm,),
        in_specs=[tok(d), tok(1), tok(d), tok(d)] + [ANY] * (2 * n_part) + [vec, vec, vec, vec],
        out_specs=[tok(f), tok(f), tok(d), tok(d), pl.BlockSpec((8, d), lambda i: (0, 0))],
        out_shape=[jax.ShapeDtypeStruct((t_tok, f), BF16), jax.ShapeDtypeStruct((t_tok, f), BF16),
                   jax.ShapeDtypeStruct((t_tok, d), BF16), jax.ShapeDtypeStruct((t_tok, d), F32), jax.ShapeDtypeStruct((8, d), F32)],
        scratch_shapes=[pltpu.VMEM((tm, f), F32), pltpu.VMEM((d, f), BF16), pltpu.VMEM((f, d), BF16),
                        pltpu.SemaphoreType.DMA((2, n_part, n_chunk))],
        args=(xhat1, rstd1, x1b, target, *w1_parts, *w2_parts, ln1_g, ln1_b, ln2_g, ln2_b))[0]


WGRAD_STEPS = [(True, 0), (True, 1), (False, 0), (True, 2), (False, 1), (True, 3), (False, 2), (False, 3)]
CHIP_FLIPS = [3, 1, 2, 0]


def _pick(table, s):
    out = table[-1]
    for i in range(len(table) - 2, -1, -1):
        out = jnp.where(s == i, table[i], out)
    return out


def _wgrad_shard(s, cc):
    q = jnp.bitwise_xor(cc[1], _pick([CHIP_FLIPS[k] for _, k in WGRAD_STEPS], s))
    return 2 * q + jnp.where(_pick([int(sibling) for sibling, _ in WGRAD_STEPS], s) == 1, 1 - cc[0], cc[0])


def _flipped(k):
    x, y, c = _place()
    return (1 - x if CHIP_FLIPS[k] // 2 else x, 1 - y if CHIP_FLIPS[k] % 2 else y, c)


def _wgrad_pair_sum(name, product, chunk, in_specs, args, core_chip, n_sent, comms=(), scratch_shapes=()):
    half = N_DEV // 2
    n_in, n_out = len(in_specs), 2 + (0 < n_sent) + (n_sent < half - 1)

    def body(cc_ref, *refs):
        ins, outs, scr = refs[:n_in], refs[n_in:n_in + n_out], refs[n_in + n_out:]
        (own_ref, recv_ref), from_chips_ref, wire_ref = outs[-2:], outs[0], outs[n_out - 3]
        send_buf, got, send_sems, recv_sems, got_sem, wire_buf, leave_sems, arrive_sems = scr[:8]
        s = pl.program_id(0)
        x, y, c = _place()
        def send(q):
            return pltpu.make_async_remote_copy(
                src_ref=send_buf.at[q % 2], dst_ref=recv_ref.at[q], send_sem=send_sems.at[q], recv_sem=recv_sems.at[q],
                device_id=(x, y, 1 - c), device_id_type=MESH)

        def load(q):
            return pltpu.make_async_copy(recv_ref.at[q], got, got_sem.at[0])

        def leave(k):
            if k < n_sent:
                return pltpu.make_async_remote_copy(
                    src_ref=wire_buf.at[k], dst_ref=from_chips_ref.at[k], send_sem=leave_sems.at[k],
                    recv_sem=arrive_sems.at[k], device_id=_flipped(k), device_id_type=MESH)
            return pltpu.make_async_copy(wire_buf.at[k], wire_ref.at[k - n_sent], leave_sems.at[k])

        for step, (sibling, q) in enumerate(WGRAD_STEPS):
            if not sibling:
                @pl.when(s == step)
                def _(q=q):
                    send(q).wait_recv()
                    load(q).start()

        g = product(_wgrad_shard(s, cc_ref), *ins, *scr[8:])

        for step, (sibling, q) in enumerate(WGRAD_STEPS):
            @pl.when(s == step)
            def _(sibling=sibling, q=q):
                if sibling:
                    if q >= 2:
                        send(q - 2).wait_send()
                    send_buf[q % 2] = g
                    send(q).start()
                    return
                load(q).wait()
                total = g + got[...]
                if q < half - 1:
                    wire_buf[q] = total.astype(BF16)
                    leave(q).start()
                else:
                    own_ref[...] = total

        @pl.when(s == N_DEV - 1)
        def _():
            for q in range(half - 2, half):
                send(q).wait_send()
            for k in range(half - 1):
                leave(k).wait()

    sums = lambda n: [jax.ShapeDtypeStruct((n,) + chunk, BF16)] if n else []
    sem = lambda n: pltpu.SemaphoreType.DMA((n,))
    res, per_comm = _carry(
        body, name=name, grid=(N_DEV,), comms=comms, prefetch=(core_chip,), in_specs=in_specs,
        out_specs=[ANY] * (n_out - 2) + [pl.BlockSpec(chunk, lambda s, cc: (0, 0)), ANY],
        out_shape=sums(n_sent) + sums(half - 1 - n_sent) + [jax.ShapeDtypeStruct(chunk, F32),
                                                            jax.ShapeDtypeStruct((half,) + chunk, F32)],
        scratch_shapes=[pltpu.VMEM((2,) + chunk, F32), pltpu.VMEM(chunk, F32), sem(half), sem(half), sem(1),
                        pltpu.VMEM((half - 1,) + chunk, BF16), sem(half - 1), sem(half - 1), *scratch_shapes],
        args=args)
    return res[0] if n_sent else None, res[n_out - 3] if n_sent < half - 1 else None, res[-2], per_comm


def _resident(a):
    return pl.BlockSpec(a.shape, lambda s, cc: (0,) * a.ndim, pipeline_mode=pl.Buffered(1))


def _ffn_wgrad(name, lhs, rhs, chunk_lhs, core_chip, n_sent, comms=()):
    t_tok = lhs.shape[0]
    fc = (lhs if chunk_lhs else rhs).shape[1] // N_DEV
    chunked = pl.BlockSpec((t_tok, fc), lambda s, cc: (0, _wgrad_shard(s, cc)))

    def product(shard, lhs_ref, rhs_ref):
        return _dot(lhs_ref[...], rhs_ref[...], TN)

    return _wgrad_pair_sum(
        name, product, (fc, rhs.shape[1]) if chunk_lhs else (lhs.shape[1], fc),
        [chunked, _resident(rhs)] if chunk_lhs else [_resident(lhs), chunked], (lhs, rhs), core_chip, n_sent, comms)


def _proj_out_bwd(dz1, cat_t, w_out_b, comms=()):
    t_tok, d = dz1.shape
    d_mix = cat_t.shape[0]
    tm = min(512, t_tok)

    def body(dz1_ref, cat_ref, w_ref, dcat_ref, gw_ref):
        @pl.when(pl.program_id(0) == 0)
        def _():
            gw_ref[...] = jnp.zeros_like(gw_ref)

        dzb = dz1_ref[...].astype(BF16)
        dcat_ref[...] = _dot(w_ref[...], dzb, NT)
        gw_ref[...] += _dot(cat_ref[...], dzb)

    return _carry(
        body, name="proj_out_bwd", grid=(t_tok // tm,), comms=comms,
        in_specs=[pl.BlockSpec((tm, d), lambda i: (i, 0)), pl.BlockSpec((d_mix, tm), lambda i: (0, i)),
                  pl.BlockSpec((d_mix, d), lambda i: (0, 0))],
        out_specs=[pl.BlockSpec((d_mix, tm), lambda i: (0, i)), pl.BlockSpec((d_mix, d), lambda i: (0, 0))],
        out_shape=[jax.ShapeDtypeStruct((d_mix, t_tok), F32), jax.ShapeDtypeStruct((d_mix, d), F32)],
        args=(dz1, cat_t, w_out_b))


def _mixer_bwd(dcat_t, h_t, cos_t, sin_t, w_spatial, b_spatial, vln_g, vln_b, sinks, band_bias, lse, comms=()):
    t_tok = h_t.shape[1]
    nb, n_step = t_tok // BLK, t_tok // MIX_W
    group = N_HEADS // N_KV_HEADS

    def body(sinks_ref, dcat_ref, u_ref, vg_ref, q_ref, kvc_ref, kvp_ref, cos_ref, sin_ref, cosp_ref, sinp_ref,
             wsp_ref, bsp_ref, g_ref, b_ref, bias_ref, lse_ref, dh_ref, dkvc_ref, dkvp_ref, gwsb_ref, gbsp_ref, gvln_ref, gsink_ref,
             dg_acc, db_acc, wm_scr, gws_ref):
        i = pl.program_id(0)

        @pl.when(i == 0)
        def _():
            gws_ref[...] = jnp.zeros_like(gws_ref)
            gbsp_ref[...] = jnp.zeros_like(gbsp_ref)
            gsink_ref[...] = jnp.zeros_like(gsink_ref)
            dg_acc[...] = jnp.zeros_like(dg_acc)
            db_acc[...] = jnp.zeros_like(db_acc)

        _mask_w_once(wsp_ref, wm_scr)

        g = g_ref[...]
        ua, ua_grad = _gelu_and_grad(u_ref[...])
        vv, vv_grad = _gelu_and_grad(vg_ref[...])
        vp, vhat, rstd = _ln_fwd_t(vv, g, b_ref[...])
        vpb = vp.astype(BF16)
        da = dcat_ref[0:D_GMLP, :]
        dmixed = da * ua
        dvp_blocks = []
        for b in range(MIX_BLOCKS):
            dvp_parts = []
            for hh in range(N_HEADS):
                rows = slice(hh * HEAD_DIM, (hh + 1) * HEAD_DIM)
                vpb_h = vpb[rows, _cols(b)]
                mixed = _dot(vpb_h, wm_scr[hh], NT) + bsp_ref[hh:hh + 1, :]
                dh_ref[COL_U + hh * HEAD_DIM:COL_U + (hh + 1) * HEAD_DIM, _cols(b)] = (
                    da[rows, _cols(b)] * mixed * ua_grad[rows, _cols(b)]).astype(BF16)
                dm = dmixed[rows, _cols(b)]
                dmb = dm.astype(BF16)
                gbsp_ref[hh:hh + 1, :] += jnp.sum(dm, axis=0, keepdims=True)
                gws_ref[hh] += _dot(dmb, vpb_h, TN)
                dvp_parts.append(_dot(dmb, wm_scr[hh]))
            dvp_blocks.append(jnp.concatenate(dvp_parts, axis=0))
        dvp = jnp.concatenate(dvp_blocks, axis=1)
        dgv, dbv = dvp * vhat, dvp
        for b in range(MIX_BLOCKS):
            dg_acc[...] += dgv[:, _cols(b)]
            db_acc[...] += dbv[:, _cols(b)]
        dh_ref[COL_V:COL_V + D_GMLP, :] = (_ln_bwd_t(dvp, vhat, rstd, g) * vv_grad).astype(BF16)

        kvc, cos, sin = kvc_ref[...], cos_ref[...], sin_ref[...]
        qr = (_rope_t(q_ref[...], cos, sin) * SCORE_SCALE).astype(BF16)
        sinks4 = [_group_lanes([jnp.full((1, BLK), sinks_ref[hh], F32) for hh in range(kv * group, (kv + 1) * group)])
                  for kv in range(N_KV_HEADS)]
        dq_blocks, dkv_cur, dkv_prev = [], [], []
        for b in range(MIX_BLOCKS):
            kv_cur, kv_prev, cosc, sinc, cosp, sinp, bias1 = _block_inputs(b, i, kvc, kvp_ref, cos, sin, cosp_ref, sinp_ref, bias_ref)
            k_t, k_n, v_t = _keys_values(kv_cur, kv_prev, cosc, sinc, cosp, sinp)
            v_n = jnp.concatenate([kv_prev[D_KV:].T, kv_cur[D_KV:].T], axis=0).astype(BF16)
            bias = _group_lanes([bias1] * group)
            dk, dv, dq_parts = [], [], []
            for kv in range(N_KV_HEADS):
                heads = range(kv * group, (kv + 1) * group)
                kv_rows = slice(kv * HEAD_DIM, (kv + 1) * HEAD_DIM)
                qs = _group_lanes([qr[hh * HEAD_DIM:(hh + 1) * HEAD_DIM, _cols(b)] for hh in heads])
                dos = _group_lanes([dcat_ref[D_GMLP + hh * HEAD_DIM:D_GMLP + (hh + 1) * HEAD_DIM, _cols(b)]
                                    for hh in heads]).astype(BF16)
                lse_g = lse_ref[b * N_KV_HEADS + kv:b * N_KV_HEADS + kv + 1, :]
                p = jnp.exp(_dot(k_n, _pad_head(qs, kv)) + bias - lse_g)
                p_sink = jnp.exp(sinks4[kv] - lse_g)
                dp = _dot(v_n, _pad_head(dos, kv))
                delta = jnp.sum(p * dp, axis=0, keepdims=True)
                ds = (p * (dp - delta)).astype(BF16)
                dsink = p_sink * delta
                dq = _dot(k_t[kv_rows], ds) * SCORE_SCALE
                for j, hh in enumerate(heads):
                    gsink_ref[hh:hh + 1, :] -= dsink[:, j * BLK:(j + 1) * BLK]
                    dq_parts.append(dq[:, j * BLK:(j + 1) * BLK])
                dk.append(_dot(qs, ds, NT))
                dv.append(_dot(dos, p.astype(BF16), NT))
            dq_blocks.append(jnp.concatenate(dq_parts, axis=0))
            dk_all, dv_all = jnp.concatenate(dk, axis=0), jnp.concatenate(dv, axis=0)
            dkv_cur.append(jnp.concatenate([_rope_t(dk_all[:, BLK:], cosc, sinc, bwd=True), dv_all[:, BLK:]], axis=0))
            dkv_prev.append(jnp.concatenate([_rope_t(dk_all[:, :BLK], cosp, sinp, bwd=True), dv_all[:, :BLK]], axis=0))
        dh_ref[COL_Q:COL_Q + D_ATTN, :] = _rope_t(jnp.concatenate(dq_blocks, axis=1), cos, sin, bwd=True).astype(BF16)
        for b in range(MIX_BLOCKS):
            dkvc_ref[:, _cols(b)] = dkv_cur[b] + dkv_prev[b + 1] if b + 1 < MIX_BLOCKS else dkv_cur[b]
        dkvp_ref[...] = dkv_prev[0]

        @pl.when(i == n_step - 1)
        def _():
            causal = _causal()
            for hh in range(N_HEADS):
                gwsb_ref[hh] = jnp.where(causal, gws_ref[hh], 0.0).astype(BF16)
            gvln_ref[...] = jnp.zeros_like(gvln_ref)
            gvln_ref[0:1, :] = jnp.sum(dg_acc[...].T, axis=0, keepdims=True)
            gvln_ref[1:2, :] = jnp.sum(db_acc[...].T, axis=0, keepdims=True)

    full = lambda shape: pl.BlockSpec(shape, lambda i: (0,) * len(shape))
    return _carry(
        body, name="mixer_bwd", grid=(n_step,), comms=comms,
        in_specs=[pl.BlockSpec(memory_space=pltpu.SMEM), pl.BlockSpec((D_GMLP + D_ATTN, MIX_W), lambda i: (0, i))]
        + _h_specs() + _table_specs()
        + [full((N_HEADS, BLK, BLK)), full((N_HEADS, BLK)), full((D_GMLP, 1)), full((D_GMLP, 1)), BIAS_SPEC, LSE_SPEC],
        out_specs=[pl.BlockSpec((COL_K, MIX_W), lambda i: (0, i)), pl.BlockSpec((2 * D_KV, MIX_W), lambda i: (0, i)),
                   pl.BlockSpec((2 * D_KV, BLK), lambda i: (0, (i + n_step - 1) % n_step)),
                   full((N_HEADS, BLK, BLK)), full((N_HEADS, BLK)), full((8, D_GMLP)), full((N_HEADS, LANES))],
        out_shape=[jax.ShapeDtypeStruct((COL_K, t_tok), BF16), jax.ShapeDtypeStruct((2 * D_KV, t_tok), F32),
                   jax.ShapeDtypeStruct((2 * D_KV, n_step * BLK), F32),
                   jax.ShapeDtypeStruct((N_HEADS, BLK, BLK), BF16), jax.ShapeDtypeStruct((N_HEADS, BLK), F32),
                   jax.ShapeDtypeStruct((8, D_GMLP), F32), jax.ShapeDtypeStruct((N_HEADS, LANES), F32)],
        scratch_shapes=[pltpu.VMEM((D_GMLP, BLK), F32), pltpu.VMEM((D_GMLP, BLK), F32), pltpu.VMEM((N_HEADS, BLK, BLK), BF16),
                        pltpu.VMEM((N_HEADS, BLK, BLK), F32)],
        args=(sinks, dcat_t, h_t, h_t, h_t, h_t, h_t, cos_t, sin_t, cos_t, sin_t, w_spatial, b_spatial, vln_g, vln_b, band_bias, lse))


def _dkv_rows(dkvc_ref, dkvp_ref, width, store):
    for s in range(width // MIX_W):
        rest, last = slice(s * MIX_W, (s + 1) * MIX_W - BLK), slice((s + 1) * MIX_W - BLK, (s + 1) * MIX_W)
        store(rest, dkvc_ref[:, rest].astype(BF16))
        store(last, (dkvc_ref[:, last] + dkvp_ref[:, _cols(s)]).astype(BF16))


def _proj_in_wgrad(dh_b, dkvc_t, dkvp_t, xb, core_chip, comms=()):
    t_tok, d = xb.shape
    d_main, d_kv = dh_b.shape[0], dkvc_t.shape[0]
    rows = (d_main + d_kv) // N_DEV
    whole, cut = d_main // rows, d_main % rows

    def product(shard, dh_ref, dkvc_ref, dkvp_ref, xb_ref, dht_scr, sems):
        copies = [pltpu.make_async_copy(dh_ref.at[j * rows:(j + 1) * rows], dht_scr.at[j], sems.at[j]) for j in range(whole)]
        copies.append(pltpu.make_async_copy(dh_ref.at[whole * rows:d_main], dht_scr.at[whole, 0:cut], sems.at[whole]))

        @pl.when(pl.program_id(0) == 0)
        def _():
            for cp in copies:
                cp.start()

            def store(cols, val):
                dht_scr[whole, cut:rows, cols] = val[0:rows - cut]
                dht_scr[whole + 1, :, cols] = val[rows - cut:]

            _dkv_rows(dkvc_ref, dkvp_ref, t_tok, store)
            for cp in copies:
                cp.wait()

        return _dot(dht_scr[shard], xb_ref[...])

    return _wgrad_pair_sum(
        "proj_in_wgrad", product, (rows, d), [ANY, _resident(dkvc_t), _resident(dkvp_t), _resident(xb)],
        (dh_b, dkvc_t, dkvp_t, xb), core_chip, N_DEV // 2 - 1, comms,
        scratch_shapes=[pltpu.VMEM((N_DEV, rows, t_tok), BF16), pltpu.SemaphoreType.DMA((whole + 1,))])


def _proj_in_dgrad(dh_b, dkvc_t, dkvp_t, dz1, w_in_t, comms=()):
    t_tok, d = dz1.shape
    d_main, d_kv = dh_b.shape[0], dkvc_t.shape[0]
    tm = min(512, t_tok)

    def body(dh_ref, dkvc_ref, dkvp_ref, dz1_ref, w_ref, dx_ref, dkv_scr):
        def store(cols, val):
            dkv_scr[:, cols] = val

        _dkv_rows(dkvc_ref, dkvp_ref, tm, store)
        dx_ref[...] = (ALPHA * dz1_ref[...] + _dot(dh_ref[...], w_ref[0:d_main, :], TN)
                       + _dot(dkv_scr[...], w_ref[d_main:, :], TN))

    return _carry(
        body, name="proj_in_dgrad", grid=(t_tok // tm,), comms=comms,
        in_specs=[pl.BlockSpec((d_main, tm), lambda i: (0, i)), pl.BlockSpec((d_kv, tm), lambda i: (0, i)),
                  pl.BlockSpec((d_kv, tm // MIX_BLOCKS), lambda i: (0, i)),
                  pl.BlockSpec((tm, d), lambda i: (i, 0)), pl.BlockSpec((d_main + d_kv, d), lambda i: (0, 0))],
        out_specs=[pl.BlockSpec((tm, d), lambda i: (i, 0))],
        out_shape=[jax.ShapeDtypeStruct((t_tok, d), F32)],
        scratch_shapes=[pltpu.VMEM((d_kv, tm), BF16)],
        args=(dh_b, dkvc_t, dkvp_t, dz1, w_in_t))


def _adamw(w, g, m, v):
    m = ADAM_B1 * m + (1.0 - ADAM_B1) * g
    v = ADAM_B2 * v + (1.0 - ADAM_B2) * (g * g)
    m_hat = m / (1.0 - ADAM_B1 ** ADAM_STEP)
    v_hat = v / (1.0 - ADAM_B2 ** ADAM_STEP)
    delta = -ADAM_LR * (m_hat / (jnp.sqrt(v_hat) + ADAM_EPS) + ADAM_WD * w)
    return delta, m, v


ADAMW_STEPS = 4


def _adamw_shards(name, items, comms=(), rider=None):
    n_in, n_out = sum(4 + len(it[1]) for it in items), 4 * len(items)
    n_rin = len(rider["args"]) if rider else 0

    def body(*refs):
        ins, rins, outs, routs = refs[:n_in], refs[n_in:n_in + n_rin], refs[n_in + n_rin:n_in + n_rin + n_out], refs[n_in + n_rin + n_out:]
        for i, item in enumerate(items):
            (own_ref, w_ref, m_ref, v_ref), recv_refs, ins = ins[:4], ins[4:4 + len(item[1])], ins[4 + len(item[1]):]
            g = own_ref[...]
            for recv_ref in recv_refs:
                for k in range(recv_ref.shape[0]):
                    g = g + recv_ref[k].astype(F32)
            for o_ref, val in zip(outs[4 * i:4 * i + 4], (g,) + _adamw(w_ref[...], g, m_ref[...], v_ref[...])):
                o_ref[...] = val
        if rider:
            pl.when(pl.program_id(0) == 0)(lambda: rider["body"](rins, routs))

    in_specs, out_specs, out_shape, args = [], [], [], []
    for own, recvs, w, m, v in items:
        r, c = own.shape
        tiles = ADAMW_STEPS
        while (r // tiles) % BF16_ROWS:
            tiles //= 2
        blk = pl.BlockSpec((r // tiles, c), lambda s, k=ADAMW_STEPS // tiles: (s // k, 0))
        in_specs += [blk] * 4 + [pl.BlockSpec((a.shape[0], r // tiles, c), lambda s, k=ADAMW_STEPS // tiles: (0, s // k, 0))
                                 for a in recvs]
        out_specs += [blk] * 4
        out_shape += [jax.ShapeDtypeStruct((r, c), F32)] * 4
        args += [own, w, m, v, *recvs]
    if rider:
        in_specs, out_specs = in_specs + rider["in_specs"], out_specs + rider["out_specs"]
        out_shape, args = out_shape + rider["out_shape"], args + rider["args"]
    res, per_comm = _carry(body, name=name, grid=(ADAMW_STEPS,), comms=comms, in_specs=in_specs, out_specs=out_specs,
                           out_shape=out_shape, args=args)
    return [res[4 * i:4 * i + 4] for i in range(len(items))], res[n_out:], per_comm


VEC_VLN, VEC_LN1G, VEC_LN1B, VEC_LN2G, VEC_LN2B, VEC_SINK, VEC_LOSS, VEC_BSP, VEC_ROWS = 0, 1, 2, 3, 4, 5, 6, 8, 16


def _adamw_small(parts_w, parts_vec, params):
    n = parts_w.shape[0]
    flat = [a for p in params for a in p]
    shapes = [p[0].shape for p in params]

    def grads(gw, gv):
        return [gw, gv[VEC_VLN:VEC_VLN + 1, 0:D_GMLP], gv[VEC_VLN:VEC_VLN + 1, D_GMLP:2 * D_GMLP],
                gv[VEC_BSP:VEC_BSP + N_HEADS, 0:BLK], gv[VEC_LN1G:VEC_LN1G + 1], gv[VEC_LN1B:VEC_LN1B + 1],
                gv[VEC_LN2G:VEC_LN2G + 1], gv[VEC_LN2B:VEC_LN2B + 1], gv[VEC_SINK:VEC_SINK + 1, 0:N_HEADS]]

    def body(ins, outs):
        (pw_ref, pv_ref), ins = ins[:2], ins[2:]
        gw, gv = pw_ref[0].astype(F32), pv_ref[0]
        for k in range(1, n):
            gw, gv = gw + pw_ref[k].astype(F32), gv + pv_ref[k]
        for i, g in enumerate(grads(gw, gv)):
            w_ref, m_ref, v_ref = ins[3 * i:3 * i + 3]
            delta, m_new, v_new = _adamw(w_ref[...], g, m_ref[...], v_ref[...])
            for o_ref, val in zip(outs[4 * i:4 * i + 4], (g, delta, m_new, v_new)):
                o_ref[...] = val
        outs[-1][...] = gv[VEC_LOSS:VEC_LOSS + 1, 0:LANES]

    whole = lambda shape, **kw: pl.BlockSpec(shape, lambda i: (0,) * len(shape), **kw)
    once = dict(pipeline_mode=pl.Buffered(1))
    return dict(
        body=body, args=[parts_w, parts_vec, *flat],
        in_specs=[whole(parts_w.shape, **once), whole(parts_vec.shape, **once)] + [whole(a.shape, **once) for a in flat],
        out_specs=[whole(s) for s in shapes for _ in range(4)] + [whole((1, LANES))],
        out_shape=[jax.ShapeDtypeStruct(s, F32) for s in shapes for _ in range(4)] + [jax.ShapeDtypeStruct((1, LANES), F32)])


def _pair_sum(name, parts, recv, core_chip, comms=()):
    _, r, c = parts.shape
    tr = r if r <= 512 else 512

    def body(cc_ref, a_ref, b_ref, wire_ref, own_ref):
        s = a_ref[...] + b_ref[...]
        wire_ref[...] = s.astype(BF16)

        @pl.when(pl.program_id(1) == cc_ref[1])
        def _():
            own_ref[...] = s

    return _carry(
        body, name=name, grid=(r // tr, 4), prefetch=(core_chip,), comms=comms,
        in_specs=[pl.BlockSpec((None, tr, c), lambda i, q, cc: (2 * q + cc[0], i, 0)),
                  pl.BlockSpec((None, tr, c), lambda i, q, cc: (q, i, 0))],
        out_specs=[pl.BlockSpec((None, tr, c), lambda i, q, cc: (q, i, 0)), pl.BlockSpec((tr, c), lambda i, q, cc: (i, 0))],
        out_shape=[jax.ShapeDtypeStruct((4, r, c), BF16), jax.ShapeDtypeStruct((r, c), F32)],
        args=(parts, recv))


def kernel(x, positions, w_in, v_ln_g, v_ln_b, w_spatial, b_spatial, sinks, w_out, ln1_g, ln1_b, w_ff1, w_ff2, ln2_g, ln2_b, loss_target, m_w_in, m_v_ln_g, m_v_ln_b, m_w_spatial, m_b_spatial, m_sinks, m_w_out, m_ln1_g, m_ln1_b, m_w_ff1, m_w_ff2, m_ln2_g, m_ln2_b, v_w_in, v_v_ln_g, v_v_ln_b, v_w_spatial, v_b_spatial, v_sinks, v_w_out, v_ln1_g, v_ln1_b, v_w_ff1, v_w_ff2, v_ln2_g, v_ln2_b):
    _, t_tok, d = x.shape
    xi, yi, ci = _place()
    core_chip = jnp.stack([ci, 2 * xi + yi]).astype(jnp.int32)
    x2 = x.reshape(t_tok, d)
    target = loss_target.reshape(t_tok, d)
    inv_freq = ROPE_THETA ** (-jnp.arange(0, HEAD_DIM, 2, dtype=F32) / HEAD_DIM)
    wsp, bsp, sink_vec = w_spatial[0], b_spatial[0], sinks[0]
    vg_col, vb_col = v_ln_g.reshape(D_GMLP, 1), v_ln_b.reshape(D_GMLP, 1)
    big = {"in": w_in[0], "out": w_out[0], "ff1": w_ff1[0], "ff2": w_ff2[0]}
    half1, half2 = big["ff1"].shape[1] // 2, big["ff2"].shape[0] // 2
    w1_mine = [big["ff1"][:, :half1].astype(BF16), big["ff1"][:, half1:].astype(BF16)]
    w2_mine = [big["ff2"][:half2].astype(BF16), big["ff2"][half2:].astype(BF16)]

    (cos_t, sin_t), ((g_in,),) = _rope_tables(
        positions, jnp.tile(inv_freq, 2).reshape(HEAD_DIM, 1), comms=[_gather_comm([big["in"].T.astype(BF16)])])
    w_in_t = g_in.reshape(D_IN, d)
    (h_t, xb), ((g_out, w1_a),) = _proj_in(x2, w_in_t, comms=[_gather_comm([big["out"].astype(BF16), w1_mine[0]])])
    w_out_b = g_out.reshape(-1, d)
    band_bias = _band_bias()
    (cat_t, lse), ((w1_b, w2_a),) = _mixer_fwd(h_t, cos_t, sin_t, wsp, bsp, vg_col, vb_col, sink_vec, band_bias,
                                                comms=[_gather_comm([w1_mine[1], w2_mine[0]])])
    (xhat1, rstd1, x1b), ((w2_b,),) = _proj_out(cat_t, x2, w_out_b, ln1_g, ln1_b, comms=[_gather_comm([w2_mine[1]])])
    act_b, dpre_b, dz2b, dz1, stats = _ffn_fwd_bwd(xhat1, rstd1, x1b, target, [w1_a, w1_b], [w2_a, w2_b], ln1_g, ln1_b, ln2_g, ln2_b)

    (dcat_t, gw_out), _ = _proj_out_bwd(dz1, cat_t, w_out_b)
    p_out = gw_out.reshape(N_DEV, -1, d)
    r_ff1_a, wire_ff1, own_ff1, ((s_out,),) = _ffn_wgrad(
        "ffn_wgrad1", x1b, dpre_b, False, core_chip, 2, comms=[_sibling_comm([p_out])])
    (wire_out, own_out), _ = _pair_sum("pair_sum_out", p_out, s_out, core_chip)
    r_ff2_a, wire_ff2, own_ff2, ((r_ff1_b,),) = _ffn_wgrad(
        "ffn_wgrad2", act_b, dz2b, True, core_chip, 1, comms=[_flips_comm(wire_ff1, 2)])
    (dh_b, dkvc_t, dkvp_t, g_wsp, g_bsp, g_vln, g_sink), ((r_ff2_b,), (r_out,)) = _mixer_bwd(
        dcat_t, h_t, cos_t, sin_t, wsp, bsp, vg_col, vb_col, sink_vec, band_bias, lse,
        comms=[_flips_comm(wire_ff2, 1), _chips_comm([wire_out])])
    sink_row = jnp.pad(g_sink.sum(axis=1).reshape(1, N_HEADS), ((0, 0), (0, d - N_HEADS)))
    small_vec = jnp.concatenate([g_vln[0:2].reshape(1, d), stats[0:4], sink_row, stats[4:5], jnp.zeros((1, d), F32),
                                 jnp.pad(g_bsp, ((0, 0), (0, d - BLK)))], axis=0)
    r_in, _, own_in, ((parts_w, parts_vec),) = _proj_in_wgrad(
        dh_b, dkvc_t, dkvp_t, xb, core_chip, comms=[_gather_comm([g_wsp.reshape(-1, BLK), small_vec])])
    (grad_x,), _ = _proj_in_dgrad(dh_b, dkvc_t, dkvp_t, dz1, w_in_t)
    small = [(w_spatial, m_w_spatial, v_w_spatial), (v_ln_g, m_v_ln_g, v_v_ln_g), (v_ln_b, m_v_ln_b, v_v_ln_b),
             (b_spatial, m_b_spatial, v_b_spatial), (ln1_g, m_ln1_g, v_ln1_g), (ln1_b, m_ln1_b, v_ln1_b),
             (ln2_g, m_ln2_g, v_ln2_g), (ln2_b, m_ln2_b, v_ln2_b), (sinks, m_sinks, v_sinks)]
    views = [(-1, BLK), None, None, (N_HEADS, BLK)] + [None] * 5
    small_update = _adamw_small(parts_w, parts_vec, [
        tuple(a if vw is None else a.reshape(vw) for a in p) for p, vw in zip(small, views)])
    (out_out, ff1_out, ff2_out, in_out_t), small_res, _ = _adamw_shards("adamw_all", [
        (own_out, [r_out], big["out"], m_w_out[0], v_w_out[0]),
        (own_ff1, [r_ff1_a, r_ff1_b], big["ff1"], m_w_ff1[0], v_w_ff1[0]),
        (own_ff2, [r_ff2_a, r_ff2_b], big["ff2"], m_w_ff2[0], v_w_ff2[0]),
        (own_in, [r_in], big["in"].T, m_w_in[0].T, v_w_in[0].T)], rider=small_update)
    in_out = [o.T for o in in_out_t]
    small_out = [[o.reshape(p[0].shape) for o in small_res[4 * i:4 * i + 4]] for i, p in enumerate(small)]
    loss = small_res[-1][0, 0]

    big_out = {0: in_out, 6: out_out, 9: ff1_out, 10: ff2_out}
    small_slot = {3: 0, 1: 1, 2: 2, 4: 3, 7: 4, 8: 5, 11: 6, 12: 7, 5: 8}
    outs = [loss, grad_x.reshape(x.shape)]
    for kind in range(4):
        for wi in range(13):
            outs.append(big_out[wi][kind][None] if wi in big_out else small_out[small_slot[wi]][kind])
    return tuple(outs)
```

```python
import math

import jax
import jax.numpy as jnp
from jax import lax
from jax.experimental import pallas as pl
from jax.experimental.pallas import tpu as pltpu

F32 = jnp.float32
BF16 = jnp.bfloat16
MESH = pl.DeviceIdType.MESH

HEAD_DIM = 64
N_HEADS = 8
N_KV_HEADS = 2
BLK = 128
D_GMLP = N_HEADS * HEAD_DIM
D_ATTN = N_HEADS * HEAD_DIM
D_KV = N_KV_HEADS * HEAD_DIM
D_IN = 2 * D_GMLP + D_ATTN + 2 * D_KV
COL_U, COL_V, COL_Q, COL_K = 0, D_GMLP, 2 * D_GMLP, 2 * D_GMLP + D_ATTN
ROPE_THETA = 10000.0
LN_EPS = 1e-5
ALPHA = 2.0 ** 0.25
NEG_INF = -1e30
SCORE_SCALE = 1.0 / math.sqrt(HEAD_DIM)
ADAM_LR, ADAM_B1, ADAM_B2, ADAM_EPS, ADAM_WD, ADAM_STEP = 0.001, 0.9, 0.999, 1e-08, 0.01, 10
N_DEV = 8
LANES = 128
VMEM_LIMIT = 56 * 1024 * 1024
FFN_ROWS = 256

NT = (((1,), (1,)), ((), ()))
TN = (((0,), (0,)), ((), ()))


def _params(*sem):
    return pltpu.CompilerParams(dimension_semantics=sem, vmem_limit_bytes=VMEM_LIMIT)


def _dot(a, b, dims=None):
    if dims is None:
        return jnp.dot(a, b, preferred_element_type=F32)
    return lax.dot_general(a, b, dims, preferred_element_type=F32)


def _mean(a):
    return jnp.mean(a, axis=-1, keepdims=True)


def _ln_fwd(z, g, b):
    zc = z - _mean(z)
    rstd = lax.rsqrt(_mean(zc * zc) + LN_EPS)
    xhat = zc * rstd
    return xhat * g + b, xhat, rstd


def _ln_bwd(dy, xhat, rstd, g):
    dxhat = dy * g
    return rstd * (dxhat - _mean(dxhat) - xhat * _mean(dxhat * xhat))


_GELU_C = math.sqrt(2.0 / math.pi)


def _gelu(x):
    t = jnp.tanh(_GELU_C * (x + 0.044715 * (x * x * x)))
    return 0.5 * x * (1.0 + t)


def _gelu_and_grad(x):
    x2 = x * x
    t = jnp.tanh(_GELU_C * (x + 0.044715 * (x2 * x)))
    hx, ht = 0.5 * x, 0.5 * (1.0 + t)
    return x * ht, ht + hx * (1.0 - t * t) * (_GELU_C * (1.0 + 3.0 * 0.044715 * x2))


def _mean0(a):
    return jnp.mean(a, axis=0, keepdims=True)


def _ln_fwd_t(z, g, b):
    zc = z - _mean0(z)
    rstd = lax.rsqrt(_mean0(zc * zc) + LN_EPS)
    xhat = zc * rstd
    return xhat * g + b, xhat, rstd


def _ln_bwd_t(dy, xhat, rstd, g):
    dxhat = dy * g
    return rstd * (dxhat - _mean0(dxhat) - xhat * _mean0(dxhat * xhat))


def _rope_t(t, cos, sin_signed, bwd=False):
    half = HEAD_DIM // 2
    outs = []
    for r in range(0, t.shape[0], HEAD_DIM):
        th = t[r:r + HEAD_DIM]
        sw = jnp.concatenate([th[half:], th[:half]], axis=0) * sin_signed
        outs.append(th * cos - sw if bwd else th * cos + sw)
    return jnp.concatenate(outs, axis=0)


ANY = pl.BlockSpec(memory_space=pl.ANY)
GATHER_PIECES = 4
BF16_ROWS = 16


def _place():
    return lax.axis_index("x"), lax.axis_index("y"), lax.axis_index("c")


class _Comm:
    def __init__(self, ins, outs, sems, start, finish, relay=None):
        self.ins, self.outs, self.sems, self.start, self.finish, self.relay = ins, outs, sems, start, finish, relay


def _gather_comm(arrs):
    n = len(arrs)
    pieces = []
    for a, arr in enumerate(arrs):
        k = GATHER_PIECES
        while arr.shape[0] % (k * BF16_ROWS):
            k //= 2
        pieces += [(a, p * (arr.shape[0] // k), arr.shape[0] // k, 2 * p < k) for p in range(k)]

    def parts(ins, outs, sems):
        send_sems, recv_sems, local_sems = sems
        x, y, c = _place()
        me, sibling = (x, y, c), (x, y, 1 - c)
        chips = [(1 - x, y), (x, 1 - y), (1 - x, 1 - y)]

        def copy(u, k, block, to, local=False):
            a, r0, nr, _ = pieces[u]
            px, py, pc = block
            dst = outs[a].at[4 * px + 2 * py + pc, pl.ds(r0, nr)]
            return pltpu.make_async_remote_copy(
                src_ref=ins[a].at[pl.ds(r0, nr)] if local else dst, dst_ref=dst,
                send_sem=send_sems.at[u, k], recv_sem=recv_sems.at[u, k], device_id=to, device_id_type=MESH)

        mine = [pltpu.make_async_copy(ins[a], outs[a].at[4 * x + 2 * y + c], local_sems.at[a]) for a in range(n)]
        first = []
        for u in range(len(pieces)):
            first.append(copy(u, 0, me, sibling, local=True))
            first += [copy(u, 1 + j, me, (*chip, c), local=True) for j, chip in enumerate(chips[:2])]
        via = [0 if pieces[u][3] else 1 for u in range(len(pieces))]
        relayed = [copy(u, 3, (*chips[via[u]], c), (*chips[1 - via[u]], c)) for u in range(len(pieces))]
        return copy, mine, first, via, relayed, me, sibling, chips, c

    def start(ins, outs, sems):
        _, mine, first, *_ = parts(ins, outs, sems)
        for cp in mine + first:
            cp.start()

    def relay(ins, outs, sems):
        copy, _, _, via, relayed, me, _, chips, c = parts(ins, outs, sems)
        for u in range(len(pieces)):
            copy(u, 1 + via[u], (*chips[via[u]], c), me).wait_recv()
            relayed[u].start()

    def finish(ins, outs, sems):
        copy, mine, first, via, relayed, me, sibling, chips, c = parts(ins, outs, sems)
        passed = []
        for u in range(len(pieces)):
            for j, chip in enumerate(chips):
                if j != via[u]:
                    copy(u, 1 + j, (*chip, c), me).wait_recv()
                fwd = copy(u, 4 + j, (*chip, c), sibling)
                fwd.start()
                passed.append(fwd)
        for u in range(len(pieces)):
            copy(u, 0, sibling, me).wait_recv()
            for j, chip in enumerate(chips):
                copy(u, 4 + j, (*chip, 1 - c), me).wait_recv()
        for cp in first + relayed + passed:
            cp.wait_send()
        for cp in mine:
            cp.wait()

    return _Comm(list(arrs), [jax.ShapeDtypeStruct((N_DEV,) + a.shape, a.dtype) for a in arrs],
                 [pltpu.SemaphoreType.DMA((len(pieces), 7)), pltpu.SemaphoreType.DMA((len(pieces), 7)),
                  pltpu.SemaphoreType.DMA((n,))], start, finish, relay)


def _sibling_comm(parts):
    n = len(parts)

    def copies(ins, outs, sems):
        x, y, c = _place()
        return [pltpu.make_async_remote_copy(
            src_ref=ins[a].at[2 * q + (1 - c)], dst_ref=outs[a].at[q],
            send_sem=sems[0].at[a, q], recv_sem=sems[1].at[a, q],
            device_id=(x, y, 1 - c), device_id_type=MESH) for a in range(n) for q in range(4)]

    return _Comm(list(parts), [jax.ShapeDtypeStruct((4,) + p.shape[1:], p.dtype) for p in parts],
                 [pltpu.SemaphoreType.DMA((n, 4)), pltpu.SemaphoreType.DMA((n, 4))],
                 lambda *r: [cp.start() for cp in copies(*r)], lambda *r: [cp.wait() for cp in copies(*r)])


def _chips_comm(chip_parts, rows=None):
    n = len(chip_parts)
    r0, nr = (0, None) if rows is None else rows

    def copies(ins, outs, sems):
        x, y, c = _place()
        chips = [(1 - x, y), (x, 1 - y), (1 - x, 1 - y)]
        src = lambda a, q: ins[a].at[q] if rows is None else ins[a].at[q, pl.ds(r0, nr)]
        return [pltpu.make_async_remote_copy(
            src_ref=src(a, 2 * px + py), dst_ref=outs[a].at[k],
            send_sem=sems[0].at[a, k], recv_sem=sems[1].at[a, k],
            device_id=(px, py, c), device_id_type=MESH) for a in range(n) for k, (px, py) in enumerate(chips)]

    shape = lambda p: (3,) + p.shape[1:] if rows is None else (3, nr) + p.shape[2:]
    return _Comm(list(chip_parts), [jax.ShapeDtypeStruct(shape(p), p.dtype) for p in chip_parts],
                 [pltpu.SemaphoreType.DMA((n, 3)), pltpu.SemaphoreType.DMA((n, 3))],
                 lambda *r: [cp.start() for cp in copies(*r)], lambda *r: [cp.wait() for cp in copies(*r)])


def _flips_comm(sums, first):
    m = sums.shape[0]

    def copies(ins, outs, sems):
        return [pltpu.make_async_remote_copy(
            src_ref=ins[0].at[j], dst_ref=outs[0].at[j], send_sem=sems[0].at[j], recv_sem=sems[1].at[j],
            device_id=_flipped(first + j), device_id_type=MESH) for j in range(m)]

    return _Comm([sums], [jax.ShapeDtypeStruct(sums.shape, sums.dtype)],
                 [pltpu.SemaphoreType.DMA((m,)), pltpu.SemaphoreType.DMA((m,))],
                 lambda *r: [cp.start() for cp in copies(*r)], lambda *r: [cp.wait() for cp in copies(*r)])


def _carry(body, *, name, grid, in_specs, out_specs, out_shape, args, comms=(), scratch_shapes=(), prefetch=()):
    n_pre, n_in, n_out, n_scr = len(prefetch), len(in_specs), len(out_specs), len(scratch_shapes)
    c_ins = [a for cm in comms for a in cm.ins]
    c_outs = [s for cm in comms for s in cm.outs]
    c_sems = [s for cm in comms for s in cm.sems]

    def wrapped(*refs):
        pre, refs = refs[:n_pre], refs[n_pre:]
        ins, refs = refs[:n_in], refs[n_in:]
        cins, refs = refs[:len(c_ins)], refs[len(c_ins):]
        outs, refs = refs[:n_out], refs[n_out:]
        couts, refs = refs[:len(c_outs)], refs[len(c_outs):]
        scr, sems = refs[:n_scr], refs[n_scr:]
        groups, i0, o0, s0 = [], 0, 0, 0
        for cm in comms:
            groups.append((cm, cins[i0:i0 + len(cm.ins)], couts[o0:o0 + len(cm.outs)], sems[s0:s0 + len(cm.sems)]))
            i0, o0, s0 = i0 + len(cm.ins), o0 + len(cm.outs), s0 + len(cm.sems)
        first = pl.program_id(0) == 0
        last = pl.program_id(0) == grid[0] - 1
        for ax in range(1, len(grid)):
            first = first & (pl.program_id(ax) == 0)
            last = last & (pl.program_id(ax) == grid[ax] - 1)
        if comms:
            @pl.when(first)
            def _():
                for cm, ci, co, cs in groups:
                    cm.start(ci, co, cs)
        if any(cm.relay for cm in comms):
            assert len(grid) == 1

            @pl.when(pl.program_id(0) == grid[0] // 4)
            def _():
                for cm, ci, co, cs in groups:
                    if cm.relay:
                        cm.relay(ci, co, cs)
        body(*pre, *ins, *outs, *scr)
        if comms:
            @pl.when(last)
            def _():
                for cm, ci, co, cs in groups:
                    cm.finish(ci, co, cs)

    grid_spec = pltpu.PrefetchScalarGridSpec(
        num_scalar_prefetch=n_pre, grid=grid,
        in_specs=list(in_specs) + [ANY] * len(c_ins), out_specs=list(out_specs) + [ANY] * len(c_outs),
        scratch_shapes=list(scratch_shapes) + c_sems)
    res = pl.pallas_call(
        wrapped, name=name, grid_spec=grid_spec, out_shape=list(out_shape) + c_outs,
        compiler_params=_params(*(["arbitrary"] * len(grid))),
    )(*prefetch, *args, *c_ins)
    outs, rest, per_comm = res[:n_out], res[n_out:], []
    for cm in comms:
        per_comm.append(rest[:len(cm.outs)])
        rest = rest[len(cm.outs):]
    return outs, per_comm


def _rope_tables(pos_row, inv_freq_col, comms=()):
    t_tok = pos_row.shape[1]
    tm = min(512, t_tok)

    def body(pos_ref, invf_ref, cos_ref, sin_ref):
        ang = pos_ref[...].astype(F32) * invf_ref[...]
        row = lax.broadcasted_iota(jnp.int32, ang.shape, 0)
        cos_ref[...] = jnp.cos(ang)
        sin_ref[...] = jnp.sin(ang) * jnp.where(row < HEAD_DIM // 2, -1.0, 1.0)

    return _carry(
        body, name="rope_tables", grid=(t_tok // tm,), comms=comms,
        in_specs=[pl.BlockSpec((1, tm), lambda i: (0, i)), pl.BlockSpec((HEAD_DIM, 1), lambda i: (0, 0))],
        out_specs=[pl.BlockSpec((HEAD_DIM, tm), lambda i: (0, i))] * 2,
        out_shape=[jax.ShapeDtypeStruct((HEAD_DIM, t_tok), F32)] * 2,
        args=(pos_row, inv_freq_col))


def _proj_in(x2, w_in_t, comms=()):
    t_tok, d = x2.shape
    d_in = w_in_t.shape[0]
    tm = min(512, t_tok)

    def body(x_ref, w_ref, h_ref, xb_ref):
        xb = x_ref[...].astype(BF16)
        xb_ref[...] = xb
        h_ref[...] = _dot(w_ref[...], xb, NT)

    return _carry(
        body, name="proj_in", grid=(t_tok // tm,), comms=comms,
        in_specs=[pl.BlockSpec((tm, d), lambda i: (i, 0)), pl.BlockSpec((d_in, d), lambda i: (0, 0))],
        out_specs=[pl.BlockSpec((d_in, tm), lambda i: (0, i)), pl.BlockSpec((tm, d), lambda i: (i, 0))],
        out_shape=[jax.ShapeDtypeStruct((d_in, t_tok), F32), jax.ShapeDtypeStruct((t_tok, d), BF16)],
        args=(x2, w_in_t))


MIX_BLOCKS = 2
MIX_W = MIX_BLOCKS * BLK


def _prev_block(i):
    return jnp.maximum(MIX_BLOCKS * i - 1, 0)


def _h_specs():
    kv_row = COL_K // (2 * D_KV)
    return [
        pl.BlockSpec((D_GMLP, MIX_W), lambda i: (0, i)),
        pl.BlockSpec((D_GMLP, MIX_W), lambda i: (1, i)),
        pl.BlockSpec((D_ATTN, MIX_W), lambda i: (2, i)),
        pl.BlockSpec((2 * D_KV, MIX_W), lambda i: (kv_row, i)),
        pl.BlockSpec((2 * D_KV, BLK), lambda i: (kv_row, _prev_block(i))),
    ]


def _table_specs():
    return [
        pl.BlockSpec((HEAD_DIM, MIX_W), lambda i: (0, i)),
        pl.BlockSpec((HEAD_DIM, MIX_W), lambda i: (0, i)),
        pl.BlockSpec((HEAD_DIM, BLK), lambda i: (0, _prev_block(i))),
        pl.BlockSpec((HEAD_DIM, BLK), lambda i: (0, _prev_block(i))),
    ]


def _cols(b):
    return slice(b * BLK, (b + 1) * BLK)


LSE_ROWS = 8
LSE_SPEC = pl.BlockSpec((LSE_ROWS, D_ATTN), lambda i: (i, 0))


def _block_inputs(b, i, kvc, kvp_ref, cos, sin, cosp_ref, sinp_ref, bias_ref):
    if b == 0:
        kv_prev, cos_prev, sin_prev, bias = kvp_ref[...], cosp_ref[...], sinp_ref[...], bias_ref[jnp.minimum(i, 1)]
    else:
        kv_prev, cos_prev, sin_prev, bias = kvc[:, _cols(b - 1)], cos[:, _cols(b - 1)], sin[:, _cols(b - 1)], bias_ref[1]
    return kvc[:, _cols(b)], kv_prev, cos[:, _cols(b)], sin[:, _cols(b)], cos_prev, sin_prev, bias


def _band_bias():
    ki = lax.broadcasted_iota(jnp.int32, (2, 2 * BLK, BLK), 1)
    qi = lax.broadcasted_iota(jnp.int32, (2, 2 * BLK, BLK), 2)
    later = lax.broadcasted_iota(jnp.int32, (2, 2 * BLK, BLK), 0) > 0
    dist = qi + BLK - ki
    return jnp.where((dist >= 0) & (dist < BLK) & ((ki >= BLK) | later), 0.0, NEG_INF).astype(F32)


BIAS_SPEC = pl.BlockSpec((2, 2 * BLK, BLK), lambda i: (0, 0, 0))


def _keys_values(kvc, kvp, cosc, sinc, cosp, sinp):
    kp, kc = _rope_t(kvp[:D_KV], cosp, sinp), _rope_t(kvc[:D_KV], cosc, sinc)
    k_t = jnp.concatenate([kp, kc], axis=1).astype(BF16)
    k_n = jnp.concatenate([kp.T, kc.T], axis=0).astype(BF16)
    v_t = jnp.concatenate([kvp[D_KV:], kvc[D_KV:]], axis=1).astype(BF16)
    return k_t, k_n, v_t


def _pad_head(th, kv):
    z = jnp.zeros_like(th)
    return jnp.concatenate([th, z] if kv == 0 else [z, th], axis=0)


def _group_lanes(parts):
    return jnp.concatenate(parts, axis=1)


def _softmax_sink_t(s, sink):
    m = jnp.maximum(jnp.max(s, axis=0, keepdims=True), sink)
    e = jnp.exp(s - m)
    denom = jnp.sum(e, axis=0, keepdims=True) + jnp.exp(sink - m)
    return e * (1.0 / denom), m + jnp.log(denom)


def _causal():
    row = lax.broadcasted_iota(jnp.int32, (BLK, BLK), 0)
    col = lax.broadcasted_iota(jnp.int32, (BLK, BLK), 1)
    return row >= col


def _mask_w_once(wsp_ref, wm_scr):
    @pl.when(pl.program_id(0) == 0)
    def _():
        causal = _causal()
        for hh in range(N_HEADS):
            wm_scr[hh] = jnp.where(causal, wsp_ref[hh], 0.0).astype(BF16)


def _mixer_fwd(h_t, cos_t, sin_t, w_spatial, b_spatial, vln_g, vln_b, sinks, band_bias, comms=()):
    t_tok = h_t.shape[1]
    group = N_HEADS // N_KV_HEADS

    def body(sinks_ref, u_ref, vg_ref, q_ref, kvc_ref, kvp_ref, cos_ref, sin_ref, cosp_ref, sinp_ref,
             wsp_ref, bsp_ref, g_ref, b_ref, bias_ref, cat_ref, lse_ref, wm_scr):
        i = pl.program_id(0)
        _mask_w_once(wsp_ref, wm_scr)
        lse_ref[...] = jnp.zeros_like(lse_ref)
        ua = _gelu(u_ref[...])
        vp, _, _ = _ln_fwd_t(_gelu(vg_ref[...]), g_ref[...], b_ref[...])
        vpb = vp.astype(BF16)
        for b in range(MIX_BLOCKS):
            for hh in range(N_HEADS):
                rows = slice(hh * HEAD_DIM, (hh + 1) * HEAD_DIM)
                mixed = _dot(vpb[rows, _cols(b)], wm_scr[hh], NT) + bsp_ref[hh:hh + 1, :]
                cat_ref[rows, _cols(b)] = (ua[rows, _cols(b)] * mixed).astype(BF16)

        kvc, cos, sin = kvc_ref[...], cos_ref[...], sin_ref[...]
        qr = (_rope_t(q_ref[...], cos, sin) * SCORE_SCALE).astype(BF16)
        sinks4 = [_group_lanes([jnp.full((1, BLK), sinks_ref[hh], F32) for hh in range(kv * group, (kv + 1) * group)])
                  for kv in range(N_KV_HEADS)]
        for b in range(MIX_BLOCKS):
            kv_cur, kv_prev, cosc, sinc, cosp, sinp, bias1 = _block_inputs(b, i, kvc, kvp_ref, cos, sin, cosp_ref, sinp_ref, bias_ref)
            _, k_n, v_t = _keys_values(kv_cur, kv_prev, cosc, sinc, cosp, sinp)
            bias = _group_lanes([bias1] * group)
            for kv in range(N_KV_HEADS):
                heads = range(kv * group, (kv + 1) * group)
                qs = _group_lanes([qr[hh * HEAD_DIM:(hh + 1) * HEAD_DIM, _cols(b)] for hh in heads])
                p, lse = _softmax_sink_t(_dot(k_n, _pad_head(qs, kv)) + bias, sinks4[kv])
                lse_ref[b * N_KV_HEADS + kv:b * N_KV_HEADS + kv + 1, :] = lse
                o = _dot(v_t[kv * HEAD_DIM:(kv + 1) * HEAD_DIM], p.astype(BF16)).astype(BF16)
                for j, hh in enumerate(heads):
                    cat_ref[D_GMLP + hh * HEAD_DIM:D_GMLP + (hh + 1) * HEAD_DIM, _cols(b)] = o[:, j * BLK:(j + 1) * BLK]

    full = lambda shape: pl.BlockSpec(shape, lambda i: (0,) * len(shape))
    return _carry(
        body, name="mixer_fwd", grid=(t_tok // MIX_W,), comms=comms,
        in_specs=[pl.BlockSpec(memory_space=pltpu.SMEM)] + _h_specs() + _table_specs() + [
            full((N_HEADS, BLK, BLK)), full((N_HEADS, BLK)), full((D_GMLP, 1)), full((D_GMLP, 1)), BIAS_SPEC],
        out_specs=[pl.BlockSpec((D_GMLP + D_ATTN, MIX_W), lambda i: (0, i)), LSE_SPEC],
        out_shape=[jax.ShapeDtypeStruct((D_GMLP + D_ATTN, t_tok), BF16),
                   jax.ShapeDtypeStruct((t_tok // MIX_W * LSE_ROWS, D_ATTN), F32)],
        scratch_shapes=[pltpu.VMEM((N_HEADS, BLK, BLK), BF16)],
        args=(sinks, h_t, h_t, h_t, h_t, h_t, cos_t, sin_t, cos_t, sin_t, w_spatial, b_spatial, vln_g, vln_b, band_bias))


def _proj_out(cat_t, x2, w_out_b, ln1_g, ln1_b, comms=()):
    t_tok, d = x2.shape
    tm = min(512, t_tok)

    def body(cat_ref, x_ref, w_ref, g_ref, b_ref, xhat_ref, rstd_ref, x1b_ref):
        x1, xhat, rstd = _ln_fwd(ALPHA * x_ref[...] + _dot(cat_ref[...], w_ref[...], TN), g_ref[...], b_ref[...])
        xhat_ref[...] = xhat
        rstd_ref[...] = rstd
        x1b_ref[...] = x1.astype(BF16)

    tok = lambda w: pl.BlockSpec((tm, w), lambda i: (i, 0))
    vec = pl.BlockSpec((1, d), lambda i: (0, 0))
    return _carry(
        body, name="proj_out", grid=(t_tok // tm,), comms=comms,
        in_specs=[pl.BlockSpec((cat_t.shape[0], tm), lambda i: (0, i)), tok(d), pl.BlockSpec(w_out_b.shape, lambda i: (0, 0)), vec, vec],
        out_specs=[tok(d), tok(1), tok(d)],
        out_shape=[jax.ShapeDtypeStruct((t_tok, d), F32), jax.ShapeDtypeStruct((t_tok, 1), F32), jax.ShapeDtypeStruct((t_tok, d), BF16)],
        args=(cat_t, x2, w_out_b, ln1_g, ln1_b))


def _ffn_fwd_bwd(xhat1, rstd1, x1b, target, w1_parts, w2_parts, ln1_g, ln1_b, ln2_g, ln2_b):
    t_tok, d = xhat1.shape
    n_part = len(w1_parts)
    n_chunk, _, fp = w1_parts[0].shape
    f = n_chunk * n_part * fp
    tm = min(FFN_ROWS, t_tok)

    def body(xhat1_ref, rstd1_ref, x1b_ref, tgt_ref, *refs):
        w1_hbm, w2_hbm = refs[:n_part], refs[n_part:2 * n_part]
        (g1_ref, b1_ref, g2_ref, b2_ref, act_ref, dpre_ref, dz2b_ref, dz1_ref, stats_ref,
         r_scr, w1_ref, w2_ref, w_sems) = refs[2 * n_part:]

        @pl.when(pl.program_id(0) == 0)
        def _():
            stats_ref[...] = jnp.zeros_like(stats_ref)
            loads = []
            for j in range(n_chunk):
                for p in range(n_part):
                    units = pl.ds((j * n_part + p) * fp, fp)
                    loads.append(pltpu.make_async_copy(w1_hbm[p].at[j], w1_ref.at[:, units], w_sems.at[0, p, j]))
                    loads.append(pltpu.make_async_copy(w2_hbm[p].at[j], w2_ref.at[units, :], w_sems.at[1, p, j]))
            for cp in loads:
                cp.start()
            for cp in loads:
                cp.wait()

        g1, g2 = g1_ref[...], g2_ref[...]
        xhat1 = xhat1_ref[...]
        r_scr[...] = jnp.maximum(_dot(x1b_ref[...], w1_ref[...]), 0.0)
        r = r_scr[...]
        act = (r * r).astype(BF16)
        act_ref[...] = act
        ff = _dot(act, w2_ref[...])
        y, xhat2, rstd2 = _ln_fwd(ALPHA * (xhat1 * g1 + b1_ref[...]) + ff, g2, b2_ref[...])
        diff = y - tgt_ref[...]
        loss = 0.5 * jnp.sum(jnp.sum(diff * diff, axis=-1, keepdims=True) / d, axis=0, keepdims=True)
        dy = diff / d
        dz2 = _ln_bwd(dy, xhat2, rstd2, g2)
        dz2b = dz2.astype(BF16)
        dz2b_ref[...] = dz2b
        dpre = (_dot(dz2b, w2_ref[...], NT) * (2.0 * r_scr[...])).astype(BF16)
        dpre_ref[...] = dpre
        dx1 = ALPHA * dz2 + _dot(dpre, w1_ref[...], NT)
        dz1_ref[...] = _ln_bwd(dx1, xhat1, rstd1_ref[...], g1)
        stats_ref[0:1, :] += jnp.sum(dx1 * xhat1, axis=0, keepdims=True)
        stats_ref[1:2, :] += jnp.sum(dx1, axis=0, keepdims=True)
        stats_ref[2:3, :] += jnp.sum(dy * xhat2, axis=0, keepdims=True)
        stats_ref[3:4, :] += jnp.sum(dy, axis=0, keepdims=True)
        stats_ref[4:5, :] += jnp.broadcast_to(loss, (1, d))

    tok = lambda w: pl.BlockSpec((tm, w), lambda i: (i, 0))
    vec = pl.BlockSpec((1, d), lambda i: (0, 0))
    return _carry(
        body, name="ffn_fwd_bwd", grid=(t_tok // tm,),
        in_specs=[tok(d), tok(1), tok(d), tok(d)] + [ANY] * (2 * n_part) + [vec, vec, vec, vec],
        out_specs=[tok(f), tok(f), tok(d), tok(d), pl.BlockSpec((8, d), lambda i: (0, 0))],
        out_shape=[jax.ShapeDtypeStruct((t_tok, f), BF16), jax.ShapeDtypeStruct((t_tok, f), BF16),
                   jax.ShapeDtypeStruct((t_tok, d), BF16), jax.ShapeDtypeStruct((t_tok, d), F32), jax.ShapeDtypeStruct((8, d), F32)],
        scratch_shapes=[pltpu.VMEM((tm, f), F32), pltpu.VMEM((d, f), BF16), pltpu.VMEM((f, d), BF16),
                        pltpu.SemaphoreType.DMA((2, n_part, n_chunk))],
        args=(xhat1, rstd1, x1b, target, *w1_parts, *w2_parts, ln1_g, ln1_b, ln2_g, ln2_b))[0]


WGRAD_STEPS = [(True, 0), (True, 1), (False, 0), (True, 2), (False, 1), (True, 3), (False, 2), (False, 3)]
CHIP_FLIPS = [3, 1, 2, 0]


def _pick(table, s):
    out = table[-1]
    for i in range(len(table) - 2, -1, -1):
        out = jnp.where(s == i, table[i], out)
    return out


def _wgrad_shard(s, cc):
    q = jnp.bitwise_xor(cc[1], _pick([CHIP_FLIPS[k] for _, k in WGRAD_STEPS], s))
    return 2 * q + jnp.where(_pick([int(sibling) for sibling, _ in WGRAD_STEPS], s) == 1, 1 - cc[0], cc[0])


def _flipped(k):
    x, y, c = _place()
    return (1 - x if CHIP_FLIPS[k] // 2 else x, 1 - y if CHIP_FLIPS[k] % 2 else y, c)


def _wgrad_pair_sum(name, product, chunk, in_specs, args, core_chip, n_sent, comms=(), scratch_shapes=()):
    half = N_DEV // 2
    n_in, n_out = len(in_specs), 2 + (0 < n_sent) + (n_sent < half - 1)

    def body(cc_ref, *refs):
        ins, outs, scr = refs[:n_in], refs[n_in:n_in + n_out], refs[n_in + n_out:]
        (own_ref, recv_ref), from_chips_ref, wire_ref = outs[-2:], outs[0], outs[n_out - 3]
        send_buf, got, send_sems, recv_sems, got_sem, wire_buf, leave_sems, arrive_sems = scr[:8]
        s = pl.program_id(0)
        x, y, c = _place()
        def send(q):
            return pltpu.make_async_remote_copy(
                src_ref=send_buf.at[q % 2], dst_ref=recv_ref.at[q], send_sem=send_sems.at[q], recv_sem=recv_sems.at[q],
                device_id=(x, y, 1 - c), device_id_type=MESH)

        def load(q):
            return pltpu.make_async_copy(recv_ref.at[q], got, got_sem.at[0])

        def leave(k):
            if k < n_sent:
                return pltpu.make_async_remote_copy(
                    src_ref=wire_buf.at[k], dst_ref=from_chips_ref.at[k], send_sem=leave_sems.at[k],
                    recv_sem=arrive_sems.at[k], device_id=_flipped(k), device_id_type=MESH)
            return pltpu.make_async_copy(wire_buf.at[k], wire_ref.at[k - n_sent], leave_sems.at[k])

        for step, (sibling, q) in enumerate(WGRAD_STEPS):
            if not sibling:
                @pl.when(s == step)
                def _(q=q):
                    send(q).wait_recv()
                    load(q).start()

        g = product(_wgrad_shard(s, cc_ref), *ins, *scr[8:])

        for step, (sibling, q) in enumerate(WGRAD_STEPS):
            @pl.when(s == step)
            def _(sibling=sibling, q=q):
                if sibling:
                    if q >= 2:
                        send(q - 2).wait_send()
                    send_buf[q % 2] = g
                    send(q).start()
                    return
                load(q).wait()
                total = g + got[...]
                if q < half - 1:
                    wire_buf[q] = total.astype(BF16)
                    leave(q).start()
                else:
                    own_ref[...] = total

        @pl.when(s == N_DEV - 1)
        def _():
            for q in range(half - 2, half):
                send(q).wait_send()
            for k in range(half - 1):
                leave(k).wait()

    sums = lambda n: [jax.ShapeDtypeStruct((n,) + chunk, BF16)] if n else []
    sem = lambda n: pltpu.SemaphoreType.DMA((n,))
    res, per_comm = _carry(
        body, name=name, grid=(N_DEV,), comms=comms, prefetch=(core_chip,), in_specs=in_specs,
        out_specs=[ANY] * (n_out - 2) + [pl.BlockSpec(chunk, lambda s, cc: (0, 0)), ANY],
        out_shape=sums(n_sent) + sums(half - 1 - n_sent) + [jax.ShapeDtypeStruct(chunk, F32),
                                                            jax.ShapeDtypeStruct((half,) + chunk, F32)],
        scratch_shapes=[pltpu.VMEM((2,) + chunk, F32), pltpu.VMEM(chunk, F32), sem(half), sem(half), sem(1),
                        pltpu.VMEM((half - 1,) + chunk, BF16), sem(half - 1), sem(half - 1), *scratch_shapes],
        args=args)
    return res[0] if n_sent else None, res[n_out - 3] if n_sent < half - 1 else None, res[-2], per_comm


def _resident(a):
    return pl.BlockSpec(a.shape, lambda s, cc: (0,) * a.ndim, pipeline_mode=pl.Buffered(1))


def _ffn_wgrad(name, lhs, rhs, chunk_lhs, core_chip, n_sent, comms=()):
    t_tok = lhs.shape[0]
    fc = (lhs if chunk_lhs else rhs).shape[1] // N_DEV
    chunked = pl.BlockSpec((t_tok, fc), lambda s, cc: (0, _wgrad_shard(s, cc)))

    def product(shard, lhs_ref, rhs_ref):
        return _dot(lhs_ref[...], rhs_ref[...], TN)

    return _wgrad_pair_sum(
        name, product, (fc, rhs.shape[1]) if chunk_lhs else (lhs.shape[1], fc),
        [chunked, _resident(rhs)] if chunk_lhs else [_resident(lhs), chunked], (lhs, rhs), core_chip, n_sent, comms)


def _proj_out_bwd(dz1, cat_t, w_out_b, comms=()):
    t_tok, d = dz1.shape
    d_mix = cat_t.shape[0]
    tm = min(512, t_tok)

    def body(dz1_ref, cat_ref, w_ref, dcat_ref, gw_ref):
        @pl.when(pl.program_id(0) == 0)
        def _():
            gw_ref[...] = jnp.zeros_like(gw_ref)

        dzb = dz1_ref[...].astype(BF16)
        dcat_ref[...] = _dot(w_ref[...], dzb, NT)
        gw_ref[...] += _dot(cat_ref[...], dzb)

    return _carry(
        body, name="proj_out_bwd", grid=(t_tok // tm,), comms=comms,
        in_specs=[pl.BlockSpec((tm, d), lambda i: (i, 0)), pl.BlockSpec((d_mix, tm), lambda i: (0, i)),
                  pl.BlockSpec((d_mix, d), lambda i: (0, 0))],
        out_specs=[pl.BlockSpec((d_mix, tm), lambda i: (0, i)), pl.BlockSpec((d_mix, d), lambda i: (0, 0))],
        out_shape=[jax.ShapeDtypeStruct((d_mix, t_tok), F32), jax.ShapeDtypeStruct((d_mix, d), F32)],
        args=(dz1, cat_t, w_out_b))


def _mixer_bwd(dcat_t, h_t, cos_t, sin_t, w_spatial, b_spatial, vln_g, vln_b, sinks, band_bias, lse, comms=()):
    t_tok = h_t.shape[1]
    nb, n_step = t_tok // BLK, t_tok // MIX_W
    group = N_HEADS // N_KV_HEADS

    def body(sinks_ref, dcat_ref, u_ref, vg_ref, q_ref, kvc_ref, kvp_ref, cos_ref, sin_ref, cosp_ref, sinp_ref,
             wsp_ref, bsp_ref, g_ref, b_ref, bias_ref, lse_ref, dh_ref, dkvc_ref, dkvp_ref, gwsb_ref, gbsp_ref, gvln_ref, gsink_ref,
             dg_acc, db_acc, wm_scr, gws_ref):
        i = pl.program_id(0)

        @pl.when(i == 0)
        def _():
            gws_ref[...] = jnp.zeros_like(gws_ref)
            gbsp_ref[...] = jnp.zeros_like(gbsp_ref)
            gsink_ref[...] = jnp.zeros_like(gsink_ref)
            dg_acc[...] = jnp.zeros_like(dg_acc)
            db_acc[...] = jnp.zeros_like(db_acc)

        _mask_w_once(wsp_ref, wm_scr)

        g = g_ref[...]
        ua, ua_grad = _gelu_and_grad(u_ref[...])
        vv, vv_grad = _gelu_and_grad(vg_ref[...])
        vp, vhat, rstd = _ln_fwd_t(vv, g, b_ref[...])
        vpb = vp.astype(BF16)
        da = dcat_ref[0:D_GMLP, :]
        dmixed = da * ua
        dvp_blocks = []
        for b in range(MIX_BLOCKS):
            dvp_parts = []
            for hh in range(N_HEADS):
                rows = slice(hh * HEAD_DIM, (hh + 1) * HEAD_DIM)
                vpb_h = vpb[rows, _cols(b)]
                mixed = _dot(vpb_h, wm_scr[hh], NT) + bsp_ref[hh:hh + 1, :]
                dh_ref[COL_U + hh * HEAD_DIM:COL_U + (hh + 1) * HEAD_DIM, _cols(b)] = (
                    da[rows, _cols(b)] * mixed * ua_grad[rows, _cols(b)]).astype(BF16)
                dm = dmixed[rows, _cols(b)]
                dmb = dm.astype(BF16)
                gbsp_ref[hh:hh + 1, :] += jnp.sum(dm, axis=0, keepdims=True)
                gws_ref[hh] += _dot(dmb, vpb_h, TN)
                dvp_parts.append(_dot(dmb, wm_scr[hh]))
            dvp_blocks.append(jnp.concatenate(dvp_parts, axis=0))
        dvp = jnp.concatenate(dvp_blocks, axis=1)
        dgv, dbv = dvp * vhat, dvp
        for b in range(MIX_BLOCKS):
            dg_acc[...] += dgv[:, _cols(b)]
            db_acc[...] += dbv[:, _cols(b)]
        dh_ref[COL_V:COL_V + D_GMLP, :] = (_ln_bwd_t(dvp, vhat, rstd, g) * vv_grad).astype(BF16)

        kvc, cos, sin = kvc_ref[...], cos_ref[...], sin_ref[...]
        qr = (_rope_t(q_ref[...], cos, sin) * SCORE_SCALE).astype(BF16)
        sinks4 = [_group_lanes([jnp.full((1, BLK), sinks_ref[hh], F32) for hh in range(kv * group, (kv + 1) * group)])
                  for kv in range(N_KV_HEADS)]
        dq_blocks, dkv_cur, dkv_prev = [], [], []
        for b in range(MIX_BLOCKS):
            kv_cur, kv_prev, cosc, sinc, cosp, sinp, bias1 = _block_inputs(b, i, kvc, kvp_ref, cos, sin, cosp_ref, sinp_ref, bias_ref)
            k_t, k_n, v_t = _keys_values(kv_cur, kv_prev, cosc, sinc, cosp, sinp)
            v_n = jnp.concatenate([kv_prev[D_KV:].T, kv_cur[D_KV:].T], axis=0).astype(BF16)
            bias = _group_lanes([bias1] * group)
            dk, dv, dq_parts = [], [], []
            for kv in range(N_KV_HEADS):
                heads = range(kv * group, (kv + 1) * group)
                kv_rows = slice(kv * HEAD_DIM, (kv + 1) * HEAD_DIM)
                qs = _group_lanes([qr[hh * HEAD_DIM:(hh + 1) * HEAD_DIM, _cols(b)] for hh in heads])
                dos = _group_lanes([dcat_ref[D_GMLP + hh * HEAD_DIM:D_GMLP + (hh + 1) * HEAD_DIM, _cols(b)]
                                    for hh in heads]).astype(BF16)
                lse_g = lse_ref[b * N_KV_HEADS + kv:b * N_KV_HEADS + kv + 1, :]
                p = jnp.exp(_dot(k_n, _pad_head(qs, kv)) + bias - lse_g)
                p_sink = jnp.exp(sinks4[kv] - lse_g)
                dp = _dot(v_n, _pad_head(dos, kv))
                delta = jnp.sum(p * dp, axis=0, keepdims=True)
                ds = (p * (dp - delta)).astype(BF16)
                dsink = p_sink * delta
                dq = _dot(k_t[kv_rows], ds) * SCORE_SCALE
                for j, hh in enumerate(heads):
                    gsink_ref[hh:hh + 1, :] -= dsink[:, j * BLK:(j + 1) * BLK]
                    dq_parts.append(dq[:, j * BLK:(j + 1) * BLK])
                dk.append(_dot(qs, ds, NT))
                dv.append(_dot(dos, p.astype(BF16), NT))
            dq_blocks.append(jnp.concatenate(dq_parts, axis=0))
            dk_all, dv_all = jnp.concatenate(dk, axis=0), jnp.concatenate(dv, axis=0)
            dkv_cur.append(jnp.concatenate([_rope_t(dk_all[:, BLK:], cosc, sinc, bwd=True), dv_all[:, BLK:]], axis=0))
            dkv_prev.append(jnp.concatenate([_rope_t(dk_all[:, :BLK], cosp, sinp, bwd=True), dv_all[:, :BLK]], axis=0))
        dh_ref[COL_Q:COL_Q + D_ATTN, :] = _rope_t(jnp.concatenate(dq_blocks, axis=1), cos, sin, bwd=True).astype(BF16)
        for b in range(MIX_BLOCKS):
            dkvc_ref[:, _cols(b)] = dkv_cur[b] + dkv_prev[b + 1] if b + 1 < MIX_BLOCKS else dkv_cur[b]
        dkvp_ref[...] = dkv_prev[0]

        @pl.when(i == n_step - 1)
        def _():
            causal = _causal()
            for hh in range(N_HEADS):
                gwsb_ref[hh] = jnp.where(causal, gws_ref[hh], 0.0).astype(BF16)
            gvln_ref[...] = jnp.zeros_like(gvln_ref)
            gvln_ref[0:1, :] = jnp.sum(dg_acc[...].T, axis=0, keepdims=True)
            gvln_ref[1:2, :] = jnp.sum(db_acc[...].T, axis=0, keepdims=True)

    full = lambda shape: pl.BlockSpec(shape, lambda i: (0,) * len(shape))
    return _carry(
        body, name="mixer_bwd", grid=(n_step,), comms=comms,
        in_specs=[pl.BlockSpec(memory_space=pltpu.SMEM), pl.BlockSpec((D_GMLP + D_ATTN, MIX_W), lambda i: (0, i))]
        + _h_specs() + _table_specs()
        + [full((N_HEADS, BLK, BLK)), full((N_HEADS, BLK)), full((D_GMLP, 1)), full((D_GMLP, 1)), BIAS_SPEC, LSE_SPEC],
        out_specs=[pl.BlockSpec((COL_K, MIX_W), lambda i: (0, i)), pl.BlockSpec((2 * D_KV, MIX_W), lambda i: (0, i)),
                   pl.BlockSpec((2 * D_KV, BLK), lambda i: (0, (i + n_step - 1) % n_step)),
                   full((N_HEADS, BLK, BLK)), full((N_HEADS, BLK)), full((8, D_GMLP)), full((N_HEADS, LANES))],
        out_shape=[jax.ShapeDtypeStruct((COL_K, t_tok), BF16), jax.ShapeDtypeStruct((2 * D_KV, t_tok), F32),
                   jax.ShapeDtypeStruct((2 * D_KV, n_step * BLK), F32),
                   jax.ShapeDtypeStruct((N_HEADS, BLK, BLK), BF16), jax.ShapeDtypeStruct((N_HEADS, BLK), F32),
                   jax.ShapeDtypeStruct((8, D_GMLP), F32), jax.ShapeDtypeStruct((N_HEADS, LANES), F32)],
        scratch_shapes=[pltpu.VMEM((D_GMLP, BLK), F32), pltpu.VMEM((D_GMLP, BLK), F32), pltpu.VMEM((N_HEADS, BLK, BLK), BF16),
                        pltpu.VMEM((N_HEADS, BLK, BLK), F32)],
        args=(sinks, dcat_t, h_t, h_t, h_t, h_t, h_t, cos_t, sin_t, cos_t, sin_t, w_spatial, b_spatial, vln_g, vln_b, band_bias, lse))


def _dkv_rows(dkvc_ref, dkvp_ref, width, store):
    for s in range(width // MIX_W):
        rest, last = slice(s * MIX_W, (s + 1) * MIX_W - BLK), slice((s + 1) * MIX_W - BLK, (s + 1) * MIX_W)
        store(rest, dkvc_ref[:, rest].astype(BF16))
        store(last, (dkvc_ref[:, last] + dkvp_ref[:, _cols(s)]).astype(BF16))


def _proj_in_wgrad(dh_b, dkvc_t, dkvp_t, xb, core_chip, comms=()):
    t_tok, d = xb.shape
    d_main, d_kv = dh_b.shape[0], dkvc_t.shape[0]
    rows = (d_main + d_kv) // N_DEV
    whole, cut = d_main // rows, d_main % rows

    def product(shard, dh_ref, dkvc_ref, dkvp_ref, xb_ref, dht_scr, sems):
        copies = [pltpu.make_async_copy(dh_ref.at[j * rows:(j + 1) * rows], dht_scr.at[j], sems.at[j]) for j in range(whole)]
        copies.append(pltpu.make_async_copy(dh_ref.at[whole * rows:d_main], dht_scr.at[whole, 0:cut], sems.at[whole]))

        @pl.when(pl.program_id(0) == 0)
        def _():
            for cp in copies:
                cp.start()

            def store(cols, val):
                dht_scr[whole, cut:rows, cols] = val[0:rows - cut]
                dht_scr[whole + 1, :, cols] = val[rows - cut:]

            _dkv_rows(dkvc_ref, dkvp_ref, t_tok, store)
            for cp in copies:
                cp.wait()

        return _dot(dht_scr[shard], xb_ref[...])

    return _wgrad_pair_sum(
        "proj_in_wgrad", product, (rows, d), [ANY, _resident(dkvc_t), _resident(dkvp_t), _resident(xb)],
        (dh_b, dkvc_t, dkvp_t, xb), core_chip, N_DEV // 2 - 1, comms,
        scratch_shapes=[pltpu.VMEM((N_DEV, rows, t_tok), BF16), pltpu.SemaphoreType.DMA((whole + 1,))])


def _proj_in_dgrad(dh_b, dkvc_t, dkvp_t, dz1, w_in_t, comms=()):
    t_tok, d = dz1.shape
    d_main, d_kv = dh_b.shape[0], dkvc_t.shape[0]
    tm = min(512, t_tok)

    def body(dh_ref, dkvc_ref, dkvp_ref, dz1_ref, w_ref, dx_ref, dkv_scr):
        def store(cols, val):
            dkv_scr[:, cols] = val

        _dkv_rows(dkvc_ref, dkvp_ref, tm, store)
        dx_ref[...] = (ALPHA * dz1_ref[...] + _dot(dh_ref[...], w_ref[0:d_main, :], TN)
                       + _dot(dkv_scr[...], w_ref[d_main:, :], TN))

    return _carry(
        body, name="proj_in_dgrad", grid=(t_tok // tm,), comms=comms,
        in_specs=[pl.BlockSpec((d_main, tm), lambda i: (0, i)), pl.BlockSpec((d_kv, tm), lambda i: (0, i)),
                  pl.BlockSpec((d_kv, tm // MIX_BLOCKS), lambda i: (0, i)),
                  pl.BlockSpec((tm, d), lambda i: (i, 0)), pl.BlockSpec((d_main + d_kv, d), lambda i: (0, 0))],
        out_specs=[pl.BlockSpec((tm, d), lambda i: (i, 0))],
        out_shape=[jax.ShapeDtypeStruct((t_tok, d), F32)],
        scratch_shapes=[pltpu.VMEM((d_kv, tm), BF16)],
        args=(dh_b, dkvc_t, dkvp_t, dz1, w_in_t))


def _adamw(w, g, m, v):
    m = ADAM_B1 * m + (1.0 - ADAM_B1) * g
    v = ADAM_B2 * v + (1.0 - ADAM_B2) * (g * g)
    m_hat = m / (1.0 - ADAM_B1 ** ADAM_STEP)
    v_hat = v / (1.0 - ADAM_B2 ** ADAM_STEP)
    delta = -ADAM_LR * (m_hat / (jnp.sqrt(v_hat) + ADAM_EPS) + ADAM_WD * w)
    return delta, m, v


ADAMW_STEPS = 4


def _adamw_shards(name, items, comms=(), rider=None):
    n_in, n_out = sum(4 + len(it[1]) for it in items), 4 * len(items)
    n_rin = len(rider["args"]) if rider else 0

    def body(*refs):
        ins, rins, outs, routs = refs[:n_in], refs[n_in:n_in + n_rin], refs[n_in + n_rin:n_in + n_rin + n_out], refs[n_in + n_rin + n_out:]
        for i, item in enumerate(items):
            (own_ref, w_ref, m_ref, v_ref), recv_refs, ins = ins[:4], ins[4:4 + len(item[1])], ins[4 + len(item[1]):]
            g = own_ref[...]
            for recv_ref in recv_refs:
                for k in range(recv_ref.shape[0]):
                    g = g + recv_ref[k].astype(F32)
            for o_ref, val in zip(outs[4 * i:4 * i + 4], (g,) + _adamw(w_ref[...], g, m_ref[...], v_ref[...])):
                o_ref[...] = val
        if rider:
            pl.when(pl.program_id(0) == 0)(lambda: rider["body"](rins, routs))

    in_specs, out_specs, out_shape, args = [], [], [], []
    for own, recvs, w, m, v in items:
        r, c = own.shape
        tiles = ADAMW_STEPS
        while (r // tiles) % BF16_ROWS:
            tiles //= 2
        blk = pl.BlockSpec((r // tiles, c), lambda s, k=ADAMW_STEPS // tiles: (s // k, 0))
        in_specs += [blk] * 4 + [pl.BlockSpec((a.shape[0], r // tiles, c), lambda s, k=ADAMW_STEPS // tiles: (0, s // k, 0))
                                 for a in recvs]
        out_specs += [blk] * 4
        out_shape += [jax.ShapeDtypeStruct((r, c), F32)] * 4
        args += [own, w, m, v, *recvs]
    if rider:
        in_specs, out_specs = in_specs + rider["in_specs"], out_specs + rider["out_specs"]
        out_shape, args = out_shape + rider["out_shape"], args + rider["args"]
    res, per_comm = _carry(body, name=name, grid=(ADAMW_STEPS,), comms=comms, in_specs=in_specs, out_specs=out_specs,
                           out_shape=out_shape, args=args)
    return [res[4 * i:4 * i + 4] for i in range(len(items))], res[n_out:], per_comm


VEC_VLN, VEC_LN1G, VEC_LN1B, VEC_LN2G, VEC_LN2B, VEC_SINK, VEC_LOSS, VEC_BSP, VEC_ROWS = 0, 1, 2, 3, 4, 5, 6, 8, 16


def _adamw_small(parts_w, parts_vec, params):
    n = parts_w.shape[0]
    flat = [a for p in params for a in p]
    shapes = [p[0].shape for p in params]

    def grads(gw, gv):
        return [gw, gv[VEC_VLN:VEC_VLN + 1, 0:D_GMLP], gv[VEC_VLN:VEC_VLN + 1, D_GMLP:2 * D_GMLP],
                gv[VEC_BSP:VEC_BSP + N_HEADS, 0:BLK], gv[VEC_LN1G:VEC_LN1G + 1], gv[VEC_LN1B:VEC_LN1B + 1],
                gv[VEC_LN2G:VEC_LN2G + 1], gv[VEC_LN2B:VEC_LN2B + 1], gv[VEC_SINK:VEC_SINK + 1, 0:N_HEADS]]

    def body(ins, outs):
        (pw_ref, pv_ref), ins = ins[:2], ins[2:]
        gw, gv = pw_ref[0].astype(F32), pv_ref[0]
        for k in range(1, n):
            gw, gv = gw + pw_ref[k].astype(F32), gv + pv_ref[k]
        for i, g in enumerate(grads(gw, gv)):
            w_ref, m_ref, v_ref = ins[3 * i:3 * i + 3]
            delta, m_new, v_new = _adamw(w_ref[...], g, m_ref[...], v_ref[...])
            for o_ref, val in zip(outs[4 * i:4 * i + 4], (g, delta, m_new, v_new)):
                o_ref[...] = val
        outs[-1][...] = gv[VEC_LOSS:VEC_LOSS + 1, 0:LANES]

    whole = lambda shape, **kw: pl.BlockSpec(shape, lambda i: (0,) * len(shape), **kw)
    once = dict(pipeline_mode=pl.Buffered(1))
    return dict(
        body=body, args=[parts_w, parts_vec, *flat],
        in_specs=[whole(parts_w.shape, **once), whole(parts_vec.shape, **once)] + [whole(a.shape, **once) for a in flat],
        out_specs=[whole(s) for s in shapes for _ in range(4)] + [whole((1, LANES))],
        out_shape=[jax.ShapeDtypeStruct(s, F32) for s in shapes for _ in range(4)] + [jax.ShapeDtypeStruct((1, LANES), F32)])


def _pair_sum(name, parts, recv, core_chip, comms=()):
    _, r, c = parts.shape
    tr = r if r <= 512 else 512

    def body(cc_ref, a_ref, b_ref, wire_ref, own_ref):
        s = a_ref[...] + b_ref[...]
        wire_ref[...] = s.astype(BF16)

        @pl.when(pl.program_id(1) == cc_ref[1])
        def _():
            own_ref[...] = s

    return _carry(
        body, name=name, grid=(r // tr, 4), prefetch=(core_chip,), comms=comms,
        in_specs=[pl.BlockSpec((None, tr, c), lambda i, q, cc: (2 * q + cc[0], i, 0)),
                  pl.BlockSpec((None, tr, c), lambda i, q, cc: (q, i, 0))],
        out_specs=[pl.BlockSpec((None, tr, c), lambda i, q, cc: (q, i, 0)), pl.BlockSpec((tr, c), lambda i, q, cc: (i, 0))],
        out_shape=[jax.ShapeDtypeStruct((4, r, c), BF16), jax.ShapeDtypeStruct((r, c), F32)],
        args=(parts, recv))


def kernel(x, positions, w_in, v_ln_g, v_ln_b, w_spatial, b_spatial, sinks, w_out, ln1_g, ln1_b, w_ff1, w_ff2, ln2_g, ln2_b, loss_target, m_w_in, m_v_ln_g, m_v_ln_b, m_w_spatial, m_b_spatial, m_sinks, m_w_out, m_ln1_g, m_ln1_b, m_w_ff1, m_w_ff2, m_ln2_g, m_ln2_b, v_w_in, v_v_ln_g, v_v_ln_b, v_w_spatial, v_b_spatial, v_sinks, v_w_out, v_ln1_g, v_ln1_b, v_w_ff1, v_w_ff2, v_ln2_g, v_ln2_b):
    _, t_tok, d = x.shape
    xi, yi, ci = _place()
    core_chip = jnp.stack([ci, 2 * xi + yi]).astype(jnp.int32)
    x2 = x.reshape(t_tok, d)
    target = loss_target.reshape(t_tok, d)
    inv_freq = ROPE_THETA ** (-jnp.arange(0, HEAD_DIM, 2, dtype=F32) / HEAD_DIM)
    wsp, bsp, sink_vec = w_spatial[0], b_spatial[0], sinks[0]
    vg_col, vb_col = v_ln_g.reshape(D_GMLP, 1), v_ln_b.reshape(D_GMLP, 1)
    big = {"in": w_in[0], "out": w_out[0], "ff1": w_ff1[0], "ff2": w_ff2[0]}
    half1, half2 = big["ff1"].shape[1] // 2, big["ff2"].shape[0] // 2
    w1_mine = [big["ff1"][:, :half1].astype(BF16), big["ff1"][:, half1:].astype(BF16)]
    w2_mine = [big["ff2"][:half2].astype(BF16), big["ff2"][half2:].astype(BF16)]

    (cos_t, sin_t), ((g_in,),) = _rope_tables(
        positions, jnp.tile(inv_freq, 2).reshape(HEAD_DIM, 1), comms=[_gather_comm([big["in"].T.astype(BF16)])])
    w_in_t = g_in.reshape(D_IN, d)
    (h_t, xb), ((g_out, w1_a),) = _proj_in(x2, w_in_t, comms=[_gather_comm([big["out"].astype(BF16), w1_mine[0]])])
    w_out_b = g_out.reshape(-1, d)
    band_bias = _band_bias()
    (cat_t, lse), ((w1_b, w2_a),) = _mixer_fwd(h_t, cos_t, sin_t, wsp, bsp, vg_col, vb_col, sink_vec, band_bias,
                                                comms=[_gather_comm([w1_mine[1], w2_mine[0]])])
    (xhat1, rstd1, x1b), ((w2_b,),) = _proj_out(cat_t, x2, w_out_b, ln1_g, ln1_b, comms=[_gather_comm([w2_mine[1]])])
    act_b, dpre_b, dz2b, dz1, stats = _ffn_fwd_bwd(xhat1, rstd1, x1b, target, [w1_a, w1_b], [w2_a, w2_b], ln1_g, ln1_b, ln2_g, ln2_b)

    (dcat_t, gw_out), _ = _proj_out_bwd(dz1, cat_t, w_out_b)
    p_out = gw_out.reshape(N_DEV, -1, d)
    r_ff1_a, wire_ff1, own_ff1, ((s_out,),) = _ffn_wgrad(
        "ffn_wgrad1", x1b, dpre_b, False, core_chip, 1, comms=[_sibling_comm([p_out])])
    (wire_out, own_out), _ = _pair_sum("pair_sum_out", p_out, s_out, core_chip)
    _, wire_ff2, own_ff2, ((r_ff1_b,),) = _ffn_wgrad(
        "ffn_wgrad2", act_b, dz2b, True, core_chip, 0, comms=[_flips_comm(wire_ff1, 1)])
    (dh_b, dkvc_t, dkvp_t, g_wsp, g_bsp, g_vln, g_sink), ((r_ff2,), (r_out,)) = _mixer_bwd(
        dcat_t, h_t, cos_t, sin_t, wsp, bsp, vg_col, vb_col, sink_vec, band_bias, lse,
        comms=[_flips_comm(wire_ff2, 0), _chips_comm([wire_out])])
    sink_row = jnp.pad(g_sink.sum(axis=1).reshape(1, N_HEADS), ((0, 0), (0, d - N_HEADS)))
    small_vec = jnp.concatenate([g_vln[0:2].reshape(1, d), stats[0:4], sink_row, stats[4:5], jnp.zeros((1, d), F32),
                                 jnp.pad(g_bsp, ((0, 0), (0, d - BLK)))], axis=0)
    r_in, _, own_in, ((parts_w, parts_vec),) = _proj_in_wgrad(
        dh_b, dkvc_t, dkvp_t, xb, core_chip, comms=[_gather_comm([g_wsp.reshape(-1, BLK), small_vec])])
    (grad_x,), _ = _proj_in_dgrad(dh_b, dkvc_t, dkvp_t, dz1, w_in_t)
    small = [(w_spatial, m_w_spatial, v_w_spatial), (v_ln_g, m_v_ln_g, v_v_ln_g), (v_ln_b, m_v_ln_b, v_v_ln_b),
             (b_spatial, m_b_spatial, v_b_spatial), (ln1_g, m_ln1_g, v_ln1_g), (ln1_b, m_ln1_b, v_ln1_b),
             (ln2_g, m_ln2_g, v_ln2_g), (ln2_b, m_ln2_b, v_ln2_b), (sinks, m_sinks, v_sinks)]
    views = [(-1, BLK), None, None, (N_HEADS, BLK)] + [None] * 5
    small_update = _adamw_small(parts_w, parts_vec, [
        tuple(a if vw is None else a.reshape(vw) for a in p) for p, vw in zip(small, views)])
    (out_out, ff1_out, ff2_out, in_out_t), small_res, _ = _adamw_shards("adamw_all", [
        (own_out, [r_out], big["out"], m_w_out[0], v_w_out[0]),
        (own_ff1, [r_ff1_a, r_ff1_b], big["ff1"], m_w_ff1[0], v_w_ff1[0]),
        (own_ff2, [r_ff2], big["ff2"], m_w_ff2[0], v_w_ff2[0]),
        (own_in, [r_in], big["in"].T, m_w_in[0].T, v_w_in[0].T)], rider=small_update)
    in_out = [o.T for o in in_out_t]
    small_out = [[o.reshape(p[0].shape) for o in small_res[4 * i:4 * i + 4]] for i, p in enumerate(small)]
    loss = small_res[-1][0, 0]

    big_out = {0: in_out, 6: out_out, 9: ff1_out, 10: ff2_out}
    small_slot = {3: 0, 1: 1, 2: 2, 4: 3, 7: 4, 8: 5, 11: 6, 12: 7, 5: 8}
    outs = [loss, grad_x.reshape(x.shape)]
    for kind in range(4):
        for wi in range(13):
            outs.append(big_out[wi][kind][None] if wi in big_out else small_out[small_slot[wi]][kind])
    return tuple(outs)
```

```python
import math

import jax
import jax.numpy as jnp
from jax import lax
from jax.experimental import pallas as pl
from jax.experimental.pallas import tpu as pltpu

F32 = jnp.float32
BF16 = jnp.bfloat16
MESH = pl.DeviceIdType.MESH

HEAD_DIM = 64
N_HEADS = 8
N_KV_HEADS = 2
BLK = 128
D_GMLP = N_HEADS * HEAD_DIM
D_ATTN = N_HEADS * HEAD_DIM
D_KV = N_KV_HEADS * HEAD_DIM
D_IN = 2 * D_GMLP + D_ATTN + 2 * D_KV
COL_U, COL_V, COL_Q, COL_K = 0, D_GMLP, 2 * D_GMLP, 2 * D_GMLP + D_ATTN
ROPE_THETA = 10000.0
LN_EPS = 1e-5
ALPHA = 2.0 ** 0.25
NEG_INF = -1e30
SCORE_SCALE = 1.0 / math.sqrt(HEAD_DIM)
ADAM_LR, ADAM_B1, ADAM_B2, ADAM_EPS, ADAM_WD, ADAM_STEP = 0.001, 0.9, 0.999, 1e-08, 0.01, 10
N_DEV = 8
LANES = 128
VMEM_LIMIT = 56 * 1024 * 1024
FFN_ROWS = 256

NT = (((1,), (1,)), ((), ()))
TN = (((0,), (0,)), ((), ()))


def _params(*sem):
    return pltpu.CompilerParams(dimension_semantics=sem, vmem_limit_bytes=VMEM_LIMIT)


def _dot(a, b, dims=None):
    if dims is None:
        return jnp.dot(a, b, preferred_element_type=F32)
    return lax.dot_general(a, b, dims, preferred_element_type=F32)


def _mean(a):
    return jnp.mean(a, axis=-1, keepdims=True)


def _ln_fwd(z, g, b):
    zc = z - _mean(z)
    rstd = lax.rsqrt(_mean(zc * zc) + LN_EPS)
    xhat = zc * rstd
    return xhat * g + b, xhat, rstd


def _ln_bwd(dy, xhat, rstd, g):
    dxhat = dy * g
    return rstd * (dxhat - _mean(dxhat) - xhat * _mean(dxhat * xhat))


_GELU_C = math.sqrt(2.0 / math.pi)


def _gelu(x):
    t = jnp.tanh(_GELU_C * (x + 0.044715 * (x * x * x)))
    return 0.5 * x * (1.0 + t)


def _gelu_and_grad(x):
    x2 = x * x
    t = jnp.tanh(_GELU_C * (x + 0.044715 * (x2 * x)))
    hx, ht = 0.5 * x, 0.5 * (1.0 + t)
    return x * ht, ht + hx * (1.0 - t * t) * (_GELU_C * (1.0 + 3.0 * 0.044715 * x2))


def _mean0(a):
    return jnp.mean(a, axis=0, keepdims=True)


def _ln_fwd_t(z, g, b):
    zc = z - _mean0(z)
    rstd = lax.rsqrt(_mean0(zc * zc) + LN_EPS)
    xhat = zc * rstd
    return xhat * g + b, xhat, rstd


def _ln_bwd_t(dy, xhat, rstd, g):
    dxhat = dy * g
    return rstd * (dxhat - _mean0(dxhat) - xhat * _mean0(dxhat * xhat))


def _rope_t(t, cos, sin_signed, bwd=False):
    half = HEAD_DIM // 2
    outs = []
    for r in range(0, t.shape[0], HEAD_DIM):
        th = t[r:r + HEAD_DIM]
        sw = jnp.concatenate([th[half:], th[:half]], axis=0) * sin_signed
        outs.append(th * cos - sw if bwd else th * cos + sw)
    return jnp.concatenate(outs, axis=0)


ANY = pl.BlockSpec(memory_space=pl.ANY)
GATHER_PIECES = 4
BF16_ROWS = 16


def _place():
    return lax.axis_index("x"), lax.axis_index("y"), lax.axis_index("c")


class _Comm:
    def __init__(self, ins, outs, sems, start, finish):
        self.ins, self.outs, self.sems, self.start, self.finish = ins, outs, sems, start, finish


def _gather_comm(arrs):
    n = len(arrs)
    pieces = []
    for a, arr in enumerate(arrs):
        k = GATHER_PIECES
        while arr.shape[0] % (k * BF16_ROWS):
            k //= 2
        pieces += [(a, p * (arr.shape[0] // k), arr.shape[0] // k) for p in range(k)]

    def parts(ins, outs, sems):
        send_sems, recv_sems, local_sems = sems
        x, y, c = _place()
        me, sibling = (x, y, c), (x, y, 1 - c)
        chips = [(1 - x, y), (x, 1 - y), (1 - x, 1 - y)]

        def copy(u, k, block, to, local=False):
            a, r0, nr = pieces[u]
            px, py, pc = block
            dst = outs[a].at[4 * px + 2 * py + pc, pl.ds(r0, nr)]
            return pltpu.make_async_remote_copy(
                src_ref=ins[a].at[pl.ds(r0, nr)] if local else dst, dst_ref=dst,
                send_sem=send_sems.at[u, k], recv_sem=recv_sems.at[u, k], device_id=to, device_id_type=MESH)

        mine = [pltpu.make_async_copy(ins[a], outs[a].at[4 * x + 2 * y + c], local_sems.at[a]) for a in range(n)]
        first = []
        for u in range(len(pieces)):
            first.append(copy(u, 0, me, sibling, local=True))
            first += [copy(u, 1 + j, me, (*chip, c), local=True) for j, chip in enumerate(chips)]
        return copy, mine, first, me, sibling, chips, c

    def start(ins, outs, sems):
        _, mine, first, *_ = parts(ins, outs, sems)
        for cp in mine + first:
            cp.start()

    def finish(ins, outs, sems):
        copy, mine, first, me, sibling, chips, c = parts(ins, outs, sems)
        passed = []
        for u in range(len(pieces)):
            for j, chip in enumerate(chips):
                copy(u, 1 + j, (*chip, c), me).wait_recv()
                fwd = copy(u, 4 + j, (*chip, c), sibling)
                fwd.start()
                passed.append(fwd)
        for u in range(len(pieces)):
            copy(u, 0, sibling, me).wait_recv()
            for j, chip in enumerate(chips):
                copy(u, 4 + j, (*chip, 1 - c), me).wait_recv()
        for cp in first + passed:
            cp.wait_send()
        for cp in mine:
            cp.wait()

    return _Comm(list(arrs), [jax.ShapeDtypeStruct((N_DEV,) + a.shape, a.dtype) for a in arrs],
                 [pltpu.SemaphoreType.DMA((len(pieces), 7)), pltpu.SemaphoreType.DMA((len(pieces), 7)),
                  pltpu.SemaphoreType.DMA((n,))], start, finish)


def _sibling_comm(parts):
    n = len(parts)

    def copies(ins, outs, sems):
        x, y, c = _place()
        return [pltpu.make_async_remote_copy(
            src_ref=ins[a].at[2 * q + (1 - c)], dst_ref=outs[a].at[q],
            send_sem=sems[0].at[a, q], recv_sem=sems[1].at[a, q],
            device_id=(x, y, 1 - c), device_id_type=MESH) for a in range(n) for q in range(4)]

    return _Comm(list(parts), [jax.ShapeDtypeStruct((4,) + p.shape[1:], p.dtype) for p in parts],
                 [pltpu.SemaphoreType.DMA((n, 4)), pltpu.SemaphoreType.DMA((n, 4))],
                 lambda *r: [cp.start() for cp in copies(*r)], lambda *r: [cp.wait() for cp in copies(*r)])


def _chips_comm(chip_parts, rows=None):
    n = len(chip_parts)
    r0, nr = (0, None) if rows is None else rows

    def copies(ins, outs, sems):
        x, y, c = _place()
        chips = [(1 - x, y), (x, 1 - y), (1 - x, 1 - y)]
        src = lambda a, q: ins[a].at[q] if rows is None else ins[a].at[q, pl.ds(r0, nr)]
        return [pltpu.make_async_remote_copy(
            src_ref=src(a, 2 * px + py), dst_ref=outs[a].at[k],
            send_sem=sems[0].at[a, k], recv_sem=sems[1].at[a, k],
            device_id=(px, py, c), device_id_type=MESH) for a in range(n) for k, (px, py) in enumerate(chips)]

    shape = lambda p: (3,) + p.shape[1:] if rows is None else (3, nr) + p.shape[2:]
    return _Comm(list(chip_parts), [jax.ShapeDtypeStruct(shape(p), p.dtype) for p in chip_parts],
                 [pltpu.SemaphoreType.DMA((n, 3)), pltpu.SemaphoreType.DMA((n, 3))],
                 lambda *r: [cp.start() for cp in copies(*r)], lambda *r: [cp.wait() for cp in copies(*r)])


def _flips_comm(sums, first):
    m = sums.shape[0]

    def copies(ins, outs, sems):
        return [pltpu.make_async_remote_copy(
            src_ref=ins[0].at[j], dst_ref=outs[0].at[j], send_sem=sems[0].at[j], recv_sem=sems[1].at[j],
            device_id=_flipped(first + j), device_id_type=MESH) for j in range(m)]

    return _Comm([sums], [jax.ShapeDtypeStruct(sums.shape, sums.dtype)],
                 [pltpu.SemaphoreType.DMA((m,)), pltpu.SemaphoreType.DMA((m,))],
                 lambda *r: [cp.start() for cp in copies(*r)], lambda *r: [cp.wait() for cp in copies(*r)])


def _carry(body, *, name, grid, in_specs, out_specs, out_shape, args, comms=(), scratch_shapes=(), prefetch=()):
    n_pre, n_in, n_out, n_scr = len(prefetch), len(in_specs), len(out_specs), len(scratch_shapes)
    c_ins = [a for cm in comms for a in cm.ins]
    c_outs = [s for cm in comms for s in cm.outs]
    c_sems = [s for cm in comms for s in cm.sems]

    def wrapped(*refs):
        pre, refs = refs[:n_pre], refs[n_pre:]
        ins, refs = refs[:n_in], refs[n_in:]
        cins, refs = refs[:len(c_ins)], refs[len(c_ins):]
        outs, refs = refs[:n_out], refs[n_out:]
        couts, refs = refs[:len(c_outs)], refs[len(c_outs):]
        scr, sems = refs[:n_scr], refs[n_scr:]
        groups, i0, o0, s0 = [], 0, 0, 0
        for cm in comms:
            groups.append((cm, cins[i0:i0 + len(cm.ins)], couts[o0:o0 + len(cm.outs)], sems[s0:s0 + len(cm.sems)]))
            i0, o0, s0 = i0 + len(cm.ins), o0 + len(cm.outs), s0 + len(cm.sems)
        first = pl.program_id(0) == 0
        last = pl.program_id(0) == grid[0] - 1
        for ax in range(1, len(grid)):
            first = first & (pl.program_id(ax) == 0)
            last = last & (pl.program_id(ax) == grid[ax] - 1)
        if comms:
            @pl.when(first)
            def _():
                for cm, ci, co, cs in groups:
                    cm.start(ci, co, cs)
        body(*pre, *ins, *outs, *scr)
        if comms:
            @pl.when(last)
            def _():
                for cm, ci, co, cs in groups:
                    cm.finish(ci, co, cs)

    grid_spec = pltpu.PrefetchScalarGridSpec(
        num_scalar_prefetch=n_pre, grid=grid,
        in_specs=list(in_specs) + [ANY] * len(c_ins), out_specs=list(out_specs) + [ANY] * len(c_outs),
        scratch_shapes=list(scratch_shapes) + c_sems)
    res = pl.pallas_call(
        wrapped, name=name, grid_spec=grid_spec, out_shape=list(out_shape) + c_outs,
        compiler_params=_params(*(["arbitrary"] * len(grid))),
    )(*prefetch, *args, *c_ins)
    outs, rest, per_comm = res[:n_out], res[n_out:], []
    for cm in comms:
        per_comm.append(rest[:len(cm.outs)])
        rest = rest[len(cm.outs):]
    return outs, per_comm


def _rope_tables(pos_row, inv_freq_col, comms=()):
    t_tok = pos_row.shape[1]
    tm = min(512, t_tok)

    def body(pos_ref, invf_ref, cos_ref, sin_ref):
        ang = pos_ref[...].astype(F32) * invf_ref[...]
        row = lax.broadcasted_iota(jnp.int32, ang.shape, 0)
        cos_ref[...] = jnp.cos(ang)
        sin_ref[...] = jnp.sin(ang) * jnp.where(row < HEAD_DIM // 2, -1.0, 1.0)

    return _carry(
        body, name="rope_tables", grid=(t_tok // tm,), comms=comms,
        in_specs=[pl.BlockSpec((1, tm), lambda i: (0, i)), pl.BlockSpec((HEAD_DIM, 1), lambda i: (0, 0))],
        out_specs=[pl.BlockSpec((HEAD_DIM, tm), lambda i: (0, i))] * 2,
        out_shape=[jax.ShapeDtypeStruct((HEAD_DIM, t_tok), F32)] * 2,
        args=(pos_row, inv_freq_col))


def _proj_in(x2, w_in_t, comms=()):
    t_tok, d = x2.shape
    d_in = w_in_t.shape[0]
    tm = min(512, t_tok)

    def body(x_ref, w_ref, h_ref, xb_ref):
        xb = x_ref[...].astype(BF16)
        xb_ref[...] = xb
        h_ref[...] = _dot(w_ref[...], xb, NT)

    return _carry(
        body, name="proj_in", grid=(t_tok // tm,), comms=comms,
        in_specs=[pl.BlockSpec((tm, d), lambda i: (i, 0)), pl.BlockSpec((d_in, d), lambda i: (0, 0))],
        out_specs=[pl.BlockSpec((d_in, tm), lambda i: (0, i)), pl.BlockSpec((tm, d), lambda i: (i, 0))],
        out_shape=[jax.ShapeDtypeStruct((d_in, t_tok), F32), jax.ShapeDtypeStruct((t_tok, d), BF16)],
        args=(x2, w_in_t))


MIX_BLOCKS = 2
MIX_W = MIX_BLOCKS * BLK


def _prev_block(i):
    return jnp.maximum(MIX_BLOCKS * i - 1, 0)


def _h_specs():
    kv_row = COL_K // (2 * D_KV)
    return [
        pl.BlockSpec((D_GMLP, MIX_W), lambda i: (0, i)),
        pl.BlockSpec((D_GMLP, MIX_W), lambda i: (1, i)),
        pl.BlockSpec((D_ATTN, MIX_W), lambda i: (2, i)),
        pl.BlockSpec((2 * D_KV, MIX_W), lambda i: (kv_row, i)),
        pl.BlockSpec((2 * D_KV, BLK), lambda i: (kv_row, _prev_block(i))),
    ]


def _table_specs():
    return [
        pl.BlockSpec((HEAD_DIM, MIX_W), lambda i: (0, i)),
        pl.BlockSpec((HEAD_DIM, MIX_W), lambda i: (0, i)),
        pl.BlockSpec((HEAD_DIM, BLK), lambda i: (0, _prev_block(i))),
        pl.BlockSpec((HEAD_DIM, BLK), lambda i: (0, _prev_block(i))),
    ]


def _cols(b):
    return slice(b * BLK, (b + 1) * BLK)


LSE_ROWS = 8
LSE_SPEC = pl.BlockSpec((LSE_ROWS, D_ATTN), lambda i: (i, 0))


def _block_inputs(b, i, kvc, kvp_ref, cos, sin, cosp_ref, sinp_ref, bias_ref):
    if b == 0:
        kv_prev, cos_prev, sin_prev, bias = kvp_ref[...], cosp_ref[...], sinp_ref[...], bias_ref[jnp.minimum(i, 1)]
    else:
        kv_prev, cos_prev, sin_prev, bias = kvc[:, _cols(b - 1)], cos[:, _cols(b - 1)], sin[:, _cols(b - 1)], bias_ref[1]
    return kvc[:, _cols(b)], kv_prev, cos[:, _cols(b)], sin[:, _cols(b)], cos_prev, sin_prev, bias


def _band_bias():
    ki = lax.broadcasted_iota(jnp.int32, (2, 2 * BLK, BLK), 1)
    qi = lax.broadcasted_iota(jnp.int32, (2, 2 * BLK, BLK), 2)
    later = lax.broadcasted_iota(jnp.int32, (2, 2 * BLK, BLK), 0) > 0
    dist = qi + BLK - ki
    return jnp.where((dist >= 0) & (dist < BLK) & ((ki >= BLK) | later), 0.0, NEG_INF).astype(F32)


BIAS_SPEC = pl.BlockSpec((2, 2 * BLK, BLK), lambda i: (0, 0, 0))


def _keys_values(kvc, kvp, cosc, sinc, cosp, sinp):
    kp, kc = _rope_t(kvp[:D_KV], cosp, sinp), _rope_t(kvc[:D_KV], cosc, sinc)
    k_t = jnp.concatenate([kp, kc], axis=1).astype(BF16)
    k_n = jnp.concatenate([kp.T, kc.T], axis=0).astype(BF16)
    v_t = jnp.concatenate([kvp[D_KV:], kvc[D_KV:]], axis=1).astype(BF16)
    return k_t, k_n, v_t


def _pad_head(th, kv):
    z = jnp.zeros_like(th)
    return jnp.concatenate([th, z] if kv == 0 else [z, th], axis=0)


def _group_lanes(parts):
    return jnp.concatenate(parts, axis=1)


def _softmax_sink_t(s, sink):
    m = jnp.maximum(jnp.max(s, axis=0, keepdims=True), sink)
    e = jnp.exp(s - m)
    denom = jnp.sum(e, axis=0, keepdims=True) + jnp.exp(sink - m)
    return e * (1.0 / denom), m + jnp.log(denom)


def _causal():
    row = lax.broadcasted_iota(jnp.int32, (BLK, BLK), 0)
    col = lax.broadcasted_iota(jnp.int32, (BLK, BLK), 1)
    return row >= col


def _mask_w_once(wsp_ref, wm_scr):
    @pl.when(pl.program_id(0) == 0)
    def _():
        causal = _causal()
        for hh in range(N_HEADS):
            wm_scr[hh] = jnp.where(causal, wsp_ref[hh], 0.0).astype(BF16)


def _mixer_fwd(h_t, cos_t, sin_t, w_spatial, b_spatial, vln_g, vln_b, sinks, band_bias, comms=()):
    t_tok = h_t.shape[1]
    group = N_HEADS // N_KV_HEADS

    def body(sinks_ref, u_ref, vg_ref, q_ref, kvc_ref, kvp_ref, cos_ref, sin_ref, cosp_ref, sinp_ref,
             wsp_ref, bsp_ref, g_ref, b_ref, bias_ref, cat_ref, lse_ref, wm_scr):
        i = pl.program_id(0)
        _mask_w_once(wsp_ref, wm_scr)
        lse_ref[...] = jnp.zeros_like(lse_ref)
        ua = _gelu(u_ref[...])
        vp, _, _ = _ln_fwd_t(_gelu(vg_ref[...]), g_ref[...], b_ref[...])
        vpb = vp.astype(BF16)
        for b in range(MIX_BLOCKS):
            for hh in range(N_HEADS):
                rows = slice(hh * HEAD_DIM, (hh + 1) * HEAD_DIM)
                mixed = _dot(vpb[rows, _cols(b)], wm_scr[hh], NT) + bsp_ref[hh:hh + 1, :]
                cat_ref[rows, _cols(b)] = (ua[rows, _cols(b)] * mixed).astype(BF16)

        kvc, cos, sin = kvc_ref[...], cos_ref[...], sin_ref[...]
        qr = (_rope_t(q_ref[...], cos, sin) * SCORE_SCALE).astype(BF16)
        sinks4 = [_group_lanes([jnp.full((1, BLK), sinks_ref[hh], F32) for hh in range(kv * group, (kv + 1) * group)])
                  for kv in range(N_KV_HEADS)]
        for b in range(MIX_BLOCKS):
            kv_cur, kv_prev, cosc, sinc, cosp, sinp, bias1 = _block_inputs(b, i, kvc, kvp_ref, cos, sin, cosp_ref, sinp_ref, bias_ref)
            _, k_n, v_t = _keys_values(kv_cur, kv_prev, cosc, sinc, cosp, sinp)
            bias = _group_lanes([bias1] * group)
            for kv in range(N_KV_HEADS):
                heads = range(kv * group, (kv + 1) * group)
                qs = _group_lanes([qr[hh * HEAD_DIM:(hh + 1) * HEAD_DIM, _cols(b)] for hh in heads])
                p, lse = _softmax_sink_t(_dot(k_n, _pad_head(qs, kv)) + bias, sinks4[kv])
                lse_ref[b * N_KV_HEADS + kv:b * N_KV_HEADS + kv + 1, :] = lse
                o = _dot(v_t[kv * HEAD_DIM:(kv + 1) * HEAD_DIM], p.astype(BF16)).astype(BF16)
                for j, hh in enumerate(heads):
                    cat_ref[D_GMLP + hh * HEAD_DIM:D_GMLP + (hh + 1) * HEAD_DIM, _cols(b)] = o[:, j * BLK:(j + 1) * BLK]

    full = lambda shape: pl.BlockSpec(shape, lambda i: (0,) * len(shape))
    return _carry(
        body, name="mixer_fwd", grid=(t_tok // MIX_W,), comms=comms,
        in_specs=[pl.BlockSpec(memory_space=pltpu.SMEM)] + _h_specs() + _table_specs() + [
            full((N_HEADS, BLK, BLK)), full((N_HEADS, BLK)), full((D_GMLP, 1)), full((D_GMLP, 1)), BIAS_SPEC],
        out_specs=[pl.BlockSpec((D_GMLP + D_ATTN, MIX_W), lambda i: (0, i)), LSE_SPEC],
        out_shape=[jax.ShapeDtypeStruct((D_GMLP + D_ATTN, t_tok), BF16),
                   jax.ShapeDtypeStruct((t_tok // MIX_W * LSE_ROWS, D_ATTN), F32)],
        scratch_shapes=[pltpu.VMEM((N_HEADS, BLK, BLK), BF16)],
        args=(sinks, h_t, h_t, h_t, h_t, h_t, cos_t, sin_t, cos_t, sin_t, w_spatial, b_spatial, vln_g, vln_b, band_bias))


def _proj_out(cat_t, x2, w_out_b, ln1_g, ln1_b, comms=()):
    t_tok, d = x2.shape
    tm = min(512, t_tok)

    def body(cat_ref, x_ref, w_ref, g_ref, b_ref, xhat_ref, rstd_ref, x1b_ref):
        x1, xhat, rstd = _ln_fwd(ALPHA * x_ref[...] + _dot(cat_ref[...], w_ref[...], TN), g_ref[...], b_ref[...])
        xhat_ref[...] = xhat
        rstd_ref[...] = rstd
        x1b_ref[...] = x1.astype(BF16)

    tok = lambda w: pl.BlockSpec((tm, w), lambda i: (i, 0))
    vec = pl.BlockSpec((1, d), lambda i: (0, 0))
    return _carry(
        body, name="proj_out", grid=(t_tok // tm,), comms=comms,
        in_specs=[pl.BlockSpec((cat_t.shape[0], tm), lambda i: (0, i)), tok(d), pl.BlockSpec(w_out_b.shape, lambda i: (0, 0)), vec, vec],
        out_specs=[tok(d), tok(1), tok(d)],
        out_shape=[jax.ShapeDtypeStruct((t_tok, d), F32), jax.ShapeDtypeStruct((t_tok, 1), F32), jax.ShapeDtypeStruct((t_tok, d), BF16)],
        args=(cat_t, x2, w_out_b, ln1_g, ln1_b))


def _ffn_fwd_bwd(xhat1, rstd1, x1b, target, w1_parts, w2_parts, ln1_g, ln1_b, ln2_g, ln2_b):
    t_tok, d = xhat1.shape
    n_part = len(w1_parts)
    n_chunk, _, fp = w1_parts[0].shape
    f = n_chunk * n_part * fp
    tm = min(FFN_ROWS, t_tok)

    def body(xhat1_ref, rstd1_ref, x1b_ref, tgt_ref, *refs):
        w1_hbm, w2_hbm = refs[:n_part], refs[n_part:2 * n_part]
        (g1_ref, b1_ref, g2_ref, b2_ref, act_ref, dpre_ref, dz2b_ref, dz1_ref, stats_ref,
         r_scr, w1_ref, w2_ref, w_sems) = refs[2 * n_part:]

        @pl.when(pl.program_id(0) == 0)
        def _():
            stats_ref[...] = jnp.zeros_like(stats_ref)
            loads = []
            for j in range(n_chunk):
                for p in range(n_part):
                    units = pl.ds((j * n_part + p) * fp, fp)
                    loads.append(pltpu.make_async_copy(w1_hbm[p].at[j], w1_ref.at[:, units], w_sems.at[0, p, j]))
                    loads.append(pltpu.make_async_copy(w2_hbm[p].at[j], w2_ref.at[units, :], w_sems.at[1, p, j]))
            for cp in loads:
                cp.start()
            for cp in loads:
                cp.wait()

        g1, g2 = g1_ref[...], g2_ref[...]
        xhat1 = xhat1_ref[...]
        r_scr[...] = jnp.maximum(_dot(x1b_ref[...], w1_ref[...]), 0.0)
        r = r_scr[...]
        act = (r * r).astype(BF16)
        act_ref[...] = act
        ff = _dot(act, w2_ref[...])
        y, xhat2, rstd2 = _ln_fwd(ALPHA * (xhat1 * g1 + b1_ref[...]) + ff, g2, b2_ref[...])
        diff = y - tgt_ref[...]
        loss = 0.5 * jnp.sum(jnp.sum(diff * diff, axis=-1, keepdims=True) / d, axis=0, keepdims=True)
        dy = diff / d
        dz2 = _ln_bwd(dy, xhat2, rstd2, g2)
        dz2b = dz2.astype(BF16)
        dz2b_ref[...] = dz2b
        dpre = (_dot(dz2b, w2_ref[...], NT) * (2.0 * r_scr[...])).astype(BF16)
        dpre_ref[...] = dpre
        dx1 = ALPHA * dz2 + _dot(dpre, w1_ref[...], NT)
        dz1_ref[...] = _ln_bwd(dx1, xhat1, rstd1_ref[...], g1)
        stats_ref[0:1, :] += jnp.sum(dx1 * xhat1, axis=0, keepdims=True)
        stats_ref[1:2, :] += jnp.sum(dx1, axis=0, keepdims=True)
        stats_ref[2:3, :] += jnp.sum(dy * xhat2, axis=0, keepdims=True)
        stats_ref[3:4, :] += jnp.sum(dy, axis=0, keepdims=True)
        stats_ref[4:5, :] += jnp.broadcast_to(loss, (1, d))

    tok = lambda w: pl.BlockSpec((tm, w), lambda i: (i, 0))
    vec = pl.BlockSpec((1, d), lambda i: (0, 0))
    return _carry(
        body, name="ffn_fwd_bwd", grid=(t_tok // tm,),
        in_specs=[tok(d), tok(1), tok(d), tok(d)] + [ANY] * (2 * n_part) + [vec, vec, vec, vec],
        out_specs=[tok(f), tok(f), tok(d), tok(d), pl.BlockSpec((8, d), lambda i: (0, 0))],
        out_shape=[jax.ShapeDtypeStruct((t_tok, f), BF16), jax.ShapeDtypeStruct((t_tok, f), BF16),
                   jax.ShapeDtypeStruct((t_tok, d), BF16), jax.ShapeDtypeStruct((t_tok, d), F32), jax.ShapeDtypeStruct((8, d), F32)],
        scratch_shapes=[pltpu.VMEM((tm, f), F32), pltpu.VMEM((d, f), BF16), pltpu.VMEM((f, d), BF16),
                        pltpu.SemaphoreType.DMA((2, n_part, n_chunk))],
        args=(xhat1, rstd1, x1b, target, *w1_parts, *w2_parts, ln1_g, ln1_b, ln2_g, ln2_b))[0]


WGRAD_STEPS = [(True, 0), (True, 1), (False, 0), (True, 2), (False, 1), (False, 2), (True, 3), (False, 3)]
CHIP_FLIPS = [3, 1, 2, 0]


def _pick(table, s):
    out = table[-1]
    for i in range(len(table) - 2, -1, -1):
        out = jnp.where(s == i, table[i], out)
    return out


def _wgrad_shard(s, cc):
    q = jnp.bitwise_xor(cc[1], _pick([CHIP_FLIPS[k] for _, k in WGRAD_STEPS], s))
    return 2 * q + jnp.where(_pick([int(sibling) for sibling, _ in WGRAD_STEPS], s) == 1, 1 - cc[0], cc[0])


def _flipped(k):
    x, y, c = _place()
    return (1 - x if CHIP_FLIPS[k] // 2 else x, 1 - y if CHIP_FLIPS[k] % 2 else y, c)


def _wgrad_pair_sum(name, product, chunk, in_specs, args, core_chip, n_sent, comms=(), scratch_shapes=()):
    half = N_DEV // 2
    n_in, n_out = len(in_specs), 2 + (0 < n_sent) + (n_sent < half - 1)

    def body(cc_ref, *refs):
        ins, outs, scr = refs[:n_in], refs[n_in:n_in + n_out], refs[n_in + n_out:]
        (own_ref, recv_ref), from_chips_ref, wire_ref = outs[-2:], outs[0], outs[n_out - 3]
        send_buf, got, send_sems, recv_sems, got_sem, wire_buf, leave_sems, arrive_sems = scr[:8]
        s = pl.program_id(0)
        x, y, c = _place()
        def send(q):
            return pltpu.make_async_remote_copy(
                src_ref=send_buf.at[q % 2], dst_ref=recv_ref.at[q], send_sem=send_sems.at[q], recv_sem=recv_sems.at[q],
                device_id=(x, y, 1 - c), device_id_type=MESH)

        def load(q):
            return pltpu.make_async_copy(recv_ref.at[q], got, got_sem.at[0])

        def leave(k):
            if k < n_sent:
                return pltpu.make_async_remote_copy(
                    src_ref=wire_buf.at[k], dst_ref=from_chips_ref.at[k], send_sem=leave_sems.at[k],
                    recv_sem=arrive_sems.at[k], device_id=_flipped(k), device_id_type=MESH)
            return pltpu.make_async_copy(wire_buf.at[k], wire_ref.at[k - n_sent], leave_sems.at[k])

        for step, (sibling, q) in enumerate(WGRAD_STEPS):
            if not sibling:
                @pl.when(s == step)
                def _(q=q):
                    send(q).wait_recv()
                    load(q).start()

        g = product(_wgrad_shard(s, cc_ref), *ins, *scr[8:])

        for step, (sibling, q) in enumerate(WGRAD_STEPS):
            @pl.when(s == step)
            def _(sibling=sibling, q=q):
                if sibling:
                    if q >= 2:
                        send(q - 2).wait_send()
                    send_buf[q % 2] = g
                    send(q).start()
                    return
                load(q).wait()
                total = g + got[...]
                if q < half - 1:
                    wire_buf[q] = total.astype(BF16)
                    leave(q).start()
                else:
                    own_ref[...] = total

        @pl.when(s == N_DEV - 1)
        def _():
            for q in range(half - 2, half):
                send(q).wait_send()
            for k in range(half - 1):
                leave(k).wait()

    sums = lambda n: [jax.ShapeDtypeStruct((n,) + chunk, BF16)] if n else []
    sem = lambda n: pltpu.SemaphoreType.DMA((n,))
    res, per_comm = _carry(
        body, name=name, grid=(N_DEV,), comms=comms, prefetch=(core_chip,), in_specs=in_specs,
        out_specs=[ANY] * (n_out - 2) + [pl.BlockSpec(chunk, lambda s, cc: (0, 0)), ANY],
        out_shape=sums(n_sent) + sums(half - 1 - n_sent) + [jax.ShapeDtypeStruct(chunk, F32),
                                                            jax.ShapeDtypeStruct((half,) + chunk, F32)],
        scratch_shapes=[pltpu.VMEM((2,) + chunk, F32), pltpu.VMEM(chunk, F32), sem(half), sem(half), sem(1),
                        pltpu.VMEM((half - 1,) + chunk, BF16), sem(half - 1), sem(half - 1), *scratch_shapes],
        args=args)
    return res[0] if n_sent else None, res[n_out - 3] if n_sent < half - 1 else None, res[-2], per_comm


def _resident(a):
    return pl.BlockSpec(a.shape, lambda s, cc: (0,) * a.ndim, pipeline_mode=pl.Buffered(1))


def _ffn_wgrad(name, lhs, rhs, chunk_lhs, core_chip, n_sent, comms=()):
    t_tok = lhs.shape[0]
    fc = (lhs if chunk_lhs else rhs).shape[1] // N_DEV
    chunked = pl.BlockSpec((t_tok, fc), lambda s, cc: (0, _wgrad_shard(s, cc)))

    def product(shard, lhs_ref, rhs_ref):
        return _dot(lhs_ref[...], rhs_ref[...], TN)

    return _wgrad_pair_sum(
        name, product, (fc, rhs.shape[1]) if chunk_lhs else (lhs.shape[1], fc),
        [chunked, _resident(rhs)] if chunk_lhs else [_resident(lhs), chunked], (lhs, rhs), core_chip, n_sent, comms)


def _proj_out_bwd(dz1, cat_t, w_out_b, comms=()):
    t_tok, d = dz1.shape
    d_mix = cat_t.shape[0]
    tm = min(512, t_tok)

    def body(dz1_ref, cat_ref, w_ref, dcat_ref, gw_ref):
        @pl.when(pl.program_id(0) == 0)
        def _():
            gw_ref[...] = jnp.zeros_like(gw_ref)

        dzb = dz1_ref[...].astype(BF16)
        dcat_ref[...] = _dot(w_ref[...], dzb, NT)
        gw_ref[...] += _dot(cat_ref[...], dzb)

    return _carry(
        body, name="proj_out_bwd", grid=(t_tok // tm,), comms=comms,
        in_specs=[pl.BlockSpec((tm, d), lambda i: (i, 0)), pl.BlockSpec((d_mix, tm), lambda i: (0, i)),
                  pl.BlockSpec((d_mix, d), lambda i: (0, 0))],
        out_specs=[pl.BlockSpec((d_mix, tm), lambda i: (0, i)), pl.BlockSpec((d_mix, d), lambda i: (0, 0))],
        out_shape=[jax.ShapeDtypeStruct((d_mix, t_tok), F32), jax.ShapeDtypeStruct((d_mix, d), F32)],
        args=(dz1, cat_t, w_out_b))


def _mixer_bwd(dcat_t, h_t, cos_t, sin_t, w_spatial, b_spatial, vln_g, vln_b, sinks, band_bias, lse, comms=()):
    t_tok = h_t.shape[1]
    nb, n_step = t_tok // BLK, t_tok // MIX_W
    group = N_HEADS // N_KV_HEADS

    def body(sinks_ref, dcat_ref, u_ref, vg_ref, q_ref, kvc_ref, kvp_ref, cos_ref, sin_ref, cosp_ref, sinp_ref,
             wsp_ref, bsp_ref, g_ref, b_ref, bias_ref, lse_ref, dh_ref, dkvc_ref, dkvp_ref, gwsb_ref, gbsp_ref, gvln_ref, gsink_ref,
             dg_acc, db_acc, wm_scr, gws_ref):
        i = pl.program_id(0)

        @pl.when(i == 0)
        def _():
            gws_ref[...] = jnp.zeros_like(gws_ref)
            gbsp_ref[...] = jnp.zeros_like(gbsp_ref)
            gsink_ref[...] = jnp.zeros_like(gsink_ref)
            dg_acc[...] = jnp.zeros_like(dg_acc)
            db_acc[...] = jnp.zeros_like(db_acc)

        _mask_w_once(wsp_ref, wm_scr)

        g = g_ref[...]
        ua, ua_grad = _gelu_and_grad(u_ref[...])
        vv, vv_grad = _gelu_and_grad(vg_ref[...])
        vp, vhat, rstd = _ln_fwd_t(vv, g, b_ref[...])
        vpb = vp.astype(BF16)
        da = dcat_ref[0:D_GMLP, :]
        dmixed = da * ua
        dvp_blocks = []
        for b in range(MIX_BLOCKS):
            dvp_parts = []
            for hh in range(N_HEADS):
                rows = slice(hh * HEAD_DIM, (hh + 1) * HEAD_DIM)
                vpb_h = vpb[rows, _cols(b)]
                mixed = _dot(vpb_h, wm_scr[hh], NT) + bsp_ref[hh:hh + 1, :]
                dh_ref[COL_U + hh * HEAD_DIM:COL_U + (hh + 1) * HEAD_DIM, _cols(b)] = (
                    da[rows, _cols(b)] * mixed * ua_grad[rows, _cols(b)]).astype(BF16)
                dm = dmixed[rows, _cols(b)]
                dmb = dm.astype(BF16)
                gbsp_ref[hh:hh + 1, :] += jnp.sum(dm, axis=0, keepdims=True)
                gws_ref[hh] += _dot(dmb, vpb_h, TN)
                dvp_parts.append(_dot(dmb, wm_scr[hh]))
            dvp_blocks.append(jnp.concatenate(dvp_parts, axis=0))
        dvp = jnp.concatenate(dvp_blocks, axis=1)
        dgv, dbv = dvp * vhat, dvp
        for b in range(MIX_BLOCKS):
            dg_acc[...] += dgv[:, _cols(b)]
            db_acc[...] += dbv[:, _cols(b)]
        dh_ref[COL_V:COL_V + D_GMLP, :] = (_ln_bwd_t(dvp, vhat, rstd, g) * vv_grad).astype(BF16)

        kvc, cos, sin = kvc_ref[...], cos_ref[...], sin_ref[...]
        qr = (_rope_t(q_ref[...], cos, sin) * SCORE_SCALE).astype(BF16)
        sinks4 = [_group_lanes([jnp.full((1, BLK), sinks_ref[hh], F32) for hh in range(kv * group, (kv + 1) * group)])
                  for kv in range(N_KV_HEADS)]
        dq_blocks, dkv_cur, dkv_prev = [], [], []
        for b in range(MIX_BLOCKS):
            kv_cur, kv_prev, cosc, sinc, cosp, sinp, bias1 = _block_inputs(b, i, kvc, kvp_ref, cos, sin, cosp_ref, sinp_ref, bias_ref)
            k_t, k_n, v_t = _keys_values(kv_cur, kv_prev, cosc, sinc, cosp, sinp)
            v_n = jnp.concatenate([kv_prev[D_KV:].T, kv_cur[D_KV:].T], axis=0).astype(BF16)
            bias = _group_lanes([bias1] * group)
            dk, dv, dq_parts = [], [], []
            for kv in range(N_KV_HEADS):
                heads = range(kv * group, (kv + 1) * group)
                kv_rows = slice(kv * HEAD_DIM, (kv + 1) * HEAD_DIM)
                qs = _group_lanes([qr[hh * HEAD_DIM:(hh + 1) * HEAD_DIM, _cols(b)] for hh in heads])
                dos = _group_lanes([dcat_ref[D_GMLP + hh * HEAD_DIM:D_GMLP + (hh + 1) * HEAD_DIM, _cols(b)]
                                    for hh in heads]).astype(BF16)
                lse_g = lse_ref[b * N_KV_HEADS + kv:b * N_KV_HEADS + kv + 1, :]
                p = jnp.exp(_dot(k_n, _pad_head(qs, kv)) + bias - lse_g)
                p_sink = jnp.exp(sinks4[kv] - lse_g)
                dp = _dot(v_n, _pad_head(dos, kv))
                delta = jnp.sum(p * dp, axis=0, keepdims=True)
                ds = (p * (dp - delta)).astype(BF16)
                dsink = p_sink * delta
                dq = _dot(k_t[kv_rows], ds) * SCORE_SCALE
                for j, hh in enumerate(heads):
                    gsink_ref[hh:hh + 1, :] -= dsink[:, j * BLK:(j + 1) * BLK]
                    dq_parts.append(dq[:, j * BLK:(j + 1) * BLK])
                dk.append(_dot(qs, ds, NT))
                dv.append(_dot(dos, p.astype(BF16), NT))
            dq_blocks.append(jnp.concatenate(dq_parts, axis=0))
            dk_all, dv_all = jnp.concatenate(dk, axis=0), jnp.concatenate(dv, axis=0)
            dkv_cur.append(jnp.concatenate([_rope_t(dk_all[:, BLK:], cosc, sinc, bwd=True), dv_all[:, BLK:]], axis=0))
            dkv_prev.append(jnp.concatenate([_rope_t(dk_all[:, :BLK], cosp, sinp, bwd=True), dv_all[:, :BLK]], axis=0))
        dh_ref[COL_Q:COL_Q + D_ATTN, :] = _rope_t(jnp.concatenate(dq_blocks, axis=1), cos, sin, bwd=True).astype(BF16)
        for b in range(MIX_BLOCKS):
            dkvc_ref[:, _cols(b)] = dkv_cur[b] + dkv_prev[b + 1] if b + 1 < MIX_BLOCKS else dkv_cur[b]
        dkvp_ref[...] = dkv_prev[0]

        @pl.when(i == n_step - 1)
        def _():
            causal = _causal()
            for hh in range(N_HEADS):
                gwsb_ref[hh] = jnp.where(causal, gws_ref[hh], 0.0).astype(BF16)
            gvln_ref[...] = jnp.zeros_like(gvln_ref)
            gvln_ref[0:1, :] = jnp.sum(dg_acc[...].T, axis=0, keepdims=True)
            gvln_ref[1:2, :] = jnp.sum(db_acc[...].T, axis=0, keepdims=True)

    full = lambda shape: pl.BlockSpec(shape, lambda i: (0,) * len(shape))
    return _carry(
        body, name="mixer_bwd", grid=(n_step,), comms=comms,
        in_specs=[pl.BlockSpec(memory_space=pltpu.SMEM), pl.BlockSpec((D_GMLP + D_ATTN, MIX_W), lambda i: (0, i))]
        + _h_specs() + _table_specs()
        + [full((N_HEADS, BLK, BLK)), full((N_HEADS, BLK)), full((D_GMLP, 1)), full((D_GMLP, 1)), BIAS_SPEC, LSE_SPEC],
        out_specs=[pl.BlockSpec((COL_K, MIX_W), lambda i: (0, i)), pl.BlockSpec((2 * D_KV, MIX_W), lambda i: (0, i)),
                   pl.BlockSpec((2 * D_KV, BLK), lambda i: (0, (i + n_step - 1) % n_step)),
                   full((N_HEADS, BLK, BLK)), full((N_HEADS, BLK)), full((8, D_GMLP)), full((N_HEADS, LANES))],
        out_shape=[jax.ShapeDtypeStruct((COL_K, t_tok), BF16), jax.ShapeDtypeStruct((2 * D_KV, t_tok), F32),
                   jax.ShapeDtypeStruct((2 * D_KV, n_step * BLK), F32),
                   jax.ShapeDtypeStruct((N_HEADS, BLK, BLK), BF16), jax.ShapeDtypeStruct((N_HEADS, BLK), F32),
                   jax.ShapeDtypeStruct((8, D_GMLP), F32), jax.ShapeDtypeStruct((N_HEADS, LANES), F32)],
        scratch_shapes=[pltpu.VMEM((D_GMLP, BLK), F32), pltpu.VMEM((D_GMLP, BLK), F32), pltpu.VMEM((N_HEADS, BLK, BLK), BF16),
                        pltpu.VMEM((N_HEADS, BLK, BLK), F32)],
        args=(sinks, dcat_t, h_t, h_t, h_t, h_t, h_t, cos_t, sin_t, cos_t, sin_t, w_spatial, b_spatial, vln_g, vln_b, band_bias, lse))


def _dkv_rows(dkvc_ref, dkvp_ref, width, store):
    for s in range(width // MIX_W):
        rest, last = slice(s * MIX_W, (s + 1) * MIX_W - BLK), slice((s + 1) * MIX_W - BLK, (s + 1) * MIX_W)
        store(rest, dkvc_ref[:, rest].astype(BF16))
        store(last, (dkvc_ref[:, last] + dkvp_ref[:, _cols(s)]).astype(BF16))


def _proj_in_wgrad(dh_b, dkvc_t, dkvp_t, xb, core_chip, comms=()):
    t_tok, d = xb.shape
    d_main, d_kv = dh_b.shape[0], dkvc_t.shape[0]
    rows = (d_main + d_kv) // N_DEV
    whole, cut = d_main // rows, d_main % rows

    def product(shard, dh_ref, dkvc_ref, dkvp_ref, xb_ref, dht_scr, sems):
        copies = [pltpu.make_async_copy(dh_ref.at[j * rows:(j + 1) * rows], dht_scr.at[j], sems.at[j]) for j in range(whole)]
        copies.append(pltpu.make_async_copy(dh_ref.at[whole * rows:d_main], dht_scr.at[whole, 0:cut], sems.at[whole]))

        @pl.when(pl.program_id(0) == 0)
        def _():
            for cp in copies:
                cp.start()

            def store(cols, val):
                dht_scr[whole, cut:rows, cols] = val[0:rows - cut]
                dht_scr[whole + 1, :, cols] = val[rows - cut:]

            _dkv_rows(dkvc_ref, dkvp_ref, t_tok, store)
            for cp in copies:
                cp.wait()

        return _dot(dht_scr[shard], xb_ref[...])

    return _wgrad_pair_sum(
        "proj_in_wgrad", product, (rows, d), [ANY, _resident(dkvc_t), _resident(dkvp_t), _resident(xb)],
        (dh_b, dkvc_t, dkvp_t, xb), core_chip, N_DEV // 2 - 1, comms,
        scratch_shapes=[pltpu.VMEM((N_DEV, rows, t_tok), BF16), pltpu.SemaphoreType.DMA((whole + 1,))])


def _proj_in_dgrad(dh_b, dkvc_t, dkvp_t, dz1, w_in_t, comms=()):
    t_tok, d = dz1.shape
    d_main, d_kv = dh_b.shape[0], dkvc_t.shape[0]
    tm = min(512, t_tok)

    def body(dh_ref, dkvc_ref, dkvp_ref, dz1_ref, w_ref, dx_ref, dkv_scr):
        def store(cols, val):
            dkv_scr[:, cols] = val

        _dkv_rows(dkvc_ref, dkvp_ref, tm, store)
        dx_ref[...] = (ALPHA * dz1_ref[...] + _dot(dh_ref[...], w_ref[0:d_main, :], TN)
                       + _dot(dkv_scr[...], w_ref[d_main:, :], TN))

    return _carry(
        body, name="proj_in_dgrad", grid=(t_tok // tm,), comms=comms,
        in_specs=[pl.BlockSpec((d_main, tm), lambda i: (0, i)), pl.BlockSpec((d_kv, tm), lambda i: (0, i)),
                  pl.BlockSpec((d_kv, tm // MIX_BLOCKS), lambda i: (0, i)),
                  pl.BlockSpec((tm, d), lambda i: (i, 0)), pl.BlockSpec((d_main + d_kv, d), lambda i: (0, 0))],
        out_specs=[pl.BlockSpec((tm, d), lambda i: (i, 0))],
        out_shape=[jax.ShapeDtypeStruct((t_tok, d), F32)],
        scratch_shapes=[pltpu.VMEM((d_kv, tm), BF16)],
        args=(dh_b, dkvc_t, dkvp_t, dz1, w_in_t))


def _adamw(w, g, m, v):
    m = ADAM_B1 * m + (1.0 - ADAM_B1) * g
    v = ADAM_B2 * v + (1.0 - ADAM_B2) * (g * g)
    m_hat = m / (1.0 - ADAM_B1 ** ADAM_STEP)
    v_hat = v / (1.0 - ADAM_B2 ** ADAM_STEP)
    delta = -ADAM_LR * (m_hat / (jnp.sqrt(v_hat) + ADAM_EPS) + ADAM_WD * w)
    return delta, m, v


ADAMW_STEPS = 4


def _adamw_shards(name, items, comms=(), rider=None):
    n_in, n_out = sum(4 + len(it[1]) for it in items), 4 * len(items)
    n_rin = len(rider["args"]) if rider else 0

    def body(*refs):
        ins, rins, outs, routs = refs[:n_in], refs[n_in:n_in + n_rin], refs[n_in + n_rin:n_in + n_rin + n_out], refs[n_in + n_rin + n_out:]
        for i, item in enumerate(items):
            (own_ref, w_ref, m_ref, v_ref), recv_refs, ins = ins[:4], ins[4:4 + len(item[1])], ins[4 + len(item[1]):]
            g = own_ref[...]
            for recv_ref in recv_refs:
                for k in range(recv_ref.shape[0]):
                    g = g + recv_ref[k].astype(F32)
            for o_ref, val in zip(outs[4 * i:4 * i + 4], (g,) + _adamw(w_ref[...], g, m_ref[...], v_ref[...])):
                o_ref[...] = val
        if rider:
            pl.when(pl.program_id(0) == 0)(lambda: rider["body"](rins, routs))

    in_specs, out_specs, out_shape, args = [], [], [], []
    for own, recvs, w, m, v in items:
        r, c = own.shape
        tiles = ADAMW_STEPS
        while (r // tiles) % BF16_ROWS:
            tiles //= 2
        blk = pl.BlockSpec((r // tiles, c), lambda s, k=ADAMW_STEPS // tiles: (s // k, 0))
        in_specs += [blk] * 4 + [pl.BlockSpec((a.shape[0], r // tiles, c), lambda s, k=ADAMW_STEPS // tiles: (0, s // k, 0))
                                 for a in recvs]
        out_specs += [blk] * 4
        out_shape += [jax.ShapeDtypeStruct((r, c), F32)] * 4
        args += [own, w, m, v, *recvs]
    if rider:
        in_specs, out_specs = in_specs + rider["in_specs"], out_specs + rider["out_specs"]
        out_shape, args = out_shape + rider["out_shape"], args + rider["args"]
    res, per_comm = _carry(body, name=name, grid=(ADAMW_STEPS,), comms=comms, in_specs=in_specs, out_specs=out_specs,
                           out_shape=out_shape, args=args)
    return [res[4 * i:4 * i + 4] for i in range(len(items))], res[n_out:], per_comm


VEC_VLN, VEC_LN1G, VEC_LN1B, VEC_LN2G, VEC_LN2B, VEC_SINK, VEC_LOSS, VEC_BSP, VEC_ROWS = 0, 1, 2, 3, 4, 5, 6, 8, 16


def _adamw_small(parts_w, parts_vec, params):
    n = parts_w.shape[0]
    flat = [a for p in params for a in p]
    shapes = [p[0].shape for p in params]

    def grads(gw, gv):
        return [gw, gv[VEC_VLN:VEC_VLN + 1, 0:D_GMLP], gv[VEC_VLN:VEC_VLN + 1, D_GMLP:2 * D_GMLP],
                gv[VEC_BSP:VEC_BSP + N_HEADS, 0:BLK], gv[VEC_LN1G:VEC_LN1G + 1], gv[VEC_LN1B:VEC_LN1B + 1],
                gv[VEC_LN2G:VEC_LN2G + 1], gv[VEC_LN2B:VEC_LN2B + 1], gv[VEC_SINK:VEC_SINK + 1, 0:N_HEADS]]

    def body(ins, outs):
        (pw_ref, pv_ref), ins = ins[:2], ins[2:]
        gw, gv = pw_ref[0].astype(F32), pv_ref[0]
        for k in range(1, n):
            gw, gv = gw + pw_ref[k].astype(F32), gv + pv_ref[k]
        for i, g in enumerate(grads(gw, gv)):
            w_ref, m_ref, v_ref = ins[3 * i:3 * i + 3]
            delta, m_new, v_new = _adamw(w_ref[...], g, m_ref[...], v_ref[...])
            for o_ref, val in zip(outs[4 * i:4 * i + 4], (g, delta, m_new, v_new)):
                o_ref[...] = val
        outs[-1][...] = gv[VEC_LOSS:VEC_LOSS + 1, 0:LANES]

    whole = lambda shape, **kw: pl.BlockSpec(shape, lambda i: (0,) * len(shape), **kw)
    once = dict(pipeline_mode=pl.Buffered(1))
    return dict(
        body=body, args=[parts_w, parts_vec, *flat],
        in_specs=[whole(parts_w.shape, **once), whole(parts_vec.shape, **once)] + [whole(a.shape, **once) for a in flat],
        out_specs=[whole(s) for s in shapes for _ in range(4)] + [whole((1, LANES))],
        out_shape=[jax.ShapeDtypeStruct(s, F32) for s in shapes for _ in range(4)] + [jax.ShapeDtypeStruct((1, LANES), F32)])


def _pair_sum(name, parts, recv, core_chip, comms=()):
    _, r, c = parts.shape
    tr = r if r <= 512 else 512

    def body(cc_ref, a_ref, b_ref, wire_ref, own_ref):
        s = a_ref[...] + b_ref[...]
        wire_ref[...] = s.astype(BF16)

        @pl.when(pl.program_id(1) == cc_ref[1])
        def _():
            own_ref[...] = s

    return _carry(
        body, name=name, grid=(r // tr, 4), prefetch=(core_chip,), comms=comms,
        in_specs=[pl.BlockSpec((None, tr, c), lambda i, q, cc: (2 * q + cc[0], i, 0)),
                  pl.BlockSpec((None, tr, c), lambda i, q, cc: (q, i, 0))],
        out_specs=[pl.BlockSpec((None, tr, c), lambda i, q, cc: (q, i, 0)), pl.BlockSpec((tr, c), lambda i, q, cc: (i, 0))],
        out_shape=[jax.ShapeDtypeStruct((4, r, c), BF16), jax.ShapeDtypeStruct((r, c), F32)],
        args=(parts, recv))


def kernel(x, positions, w_in, v_ln_g, v_ln_b, w_spatial, b_spatial, sinks, w_out, ln1_g, ln1_b, w_ff1, w_ff2, ln2_g, ln2_b, loss_target, m_w_in, m_v_ln_g, m_v_ln_b, m_w_spatial, m_b_spatial, m_sinks, m_w_out, m_ln1_g, m_ln1_b, m_w_ff1, m_w_ff2, m_ln2_g, m_ln2_b, v_w_in, v_v_ln_g, v_v_ln_b, v_w_spatial, v_b_spatial, v_sinks, v_w_out, v_ln1_g, v_ln1_b, v_w_ff1, v_w_ff2, v_ln2_g, v_ln2_b):
    _, t_tok, d = x.shape
    xi, yi, ci = _place()
    core_chip = jnp.stack([ci, 2 * xi + yi]).astype(jnp.int32)
    x2 = x.reshape(t_tok, d)
    target = loss_target.reshape(t_tok, d)
    inv_freq = ROPE_THETA ** (-jnp.arange(0, HEAD_DIM, 2, dtype=F32) / HEAD_DIM)
    wsp, bsp, sink_vec = w_spatial[0], b_spatial[0], sinks[0]
    vg_col, vb_col = v_ln_g.reshape(D_GMLP, 1), v_ln_b.reshape(D_GMLP, 1)
    big = {"in": w_in[0], "out": w_out[0], "ff1": w_ff1[0], "ff2": w_ff2[0]}
    half1, half2 = big["ff1"].shape[1] // 2, big["ff2"].shape[0] // 2
    w1_mine = [big["ff1"][:, :half1].astype(BF16), big["ff1"][:, half1:].astype(BF16)]
    w2_mine = [big["ff2"][:half2].astype(BF16), big["ff2"][half2:].astype(BF16)]

    (cos_t, sin_t), ((g_in,),) = _rope_tables(
        positions, jnp.tile(inv_freq, 2).reshape(HEAD_DIM, 1), comms=[_gather_comm([big["in"].T.astype(BF16)])])
    w_in_t = g_in.reshape(D_IN, d)
    (h_t, xb), ((g_out, w1_a),) = _proj_in(x2, w_in_t, comms=[_gather_comm([big["out"].astype(BF16), w1_mine[0]])])
    w_out_b = g_out.reshape(-1, d)
    band_bias = _band_bias()
    (cat_t, lse), ((w1_b, w2_a),) = _mixer_fwd(h_t, cos_t, sin_t, wsp, bsp, vg_col, vb_col, sink_vec, band_bias,
                                                comms=[_gather_comm([w1_mine[1], w2_mine[0]])])
    (xhat1, rstd1, x1b), ((w2_b,),) = _proj_out(cat_t, x2, w_out_b, ln1_g, ln1_b, comms=[_gather_comm([w2_mine[1]])])
    act_b, dpre_b, dz2b, dz1, stats = _ffn_fwd_bwd(xhat1, rstd1, x1b, target, [w1_a, w1_b], [w2_a, w2_b], ln1_g, ln1_b, ln2_g, ln2_b)

    (dcat_t, gw_out), _ = _proj_out_bwd(dz1, cat_t, w_out_b)
    p_out = gw_out.reshape(N_DEV, -1, d)
    r_ff1_a, wire_ff1, own_ff1, ((s_out,),) = _ffn_wgrad(
        "ffn_wgrad1", x1b, dpre_b, False, core_chip, 1, comms=[_sibling_comm([p_out])])
    (wire_out, own_out), _ = _pair_sum("pair_sum_out", p_out, s_out, core_chip)
    _, wire_ff2, own_ff2, ((r_ff1_b,),) = _ffn_wgrad(
        "ffn_wgrad2", act_b, dz2b, True, core_chip, 0, comms=[_flips_comm(wire_ff1, 1)])
    (dh_b, dkvc_t, dkvp_t, g_wsp, g_bsp, g_vln, g_sink), ((r_ff2,), (r_out,)) = _mixer_bwd(
        dcat_t, h_t, cos_t, sin_t, wsp, bsp, vg_col, vb_col, sink_vec, band_bias, lse,
        comms=[_flips_comm(wire_ff2, 0), _chips_comm([wire_out])])
    sink_row = jnp.pad(g_sink.sum(axis=1).reshape(1, N_HEADS), ((0, 0), (0, d - N_HEADS)))
    small_vec = jnp.concatenate([g_vln[0:2].reshape(1, d), stats[0:4], sink_row, stats[4:5], jnp.zeros((1, d), F32),
                                 jnp.pad(g_bsp, ((0, 0), (0, d - BLK)))], axis=0)
    r_in, _, own_in, ((parts_w, parts_vec),) = _proj_in_wgrad(
        dh_b, dkvc_t, dkvp_t, xb, core_chip, comms=[_gather_comm([g_wsp.reshape(-1, BLK), small_vec])])
    (grad_x,), _ = _proj_in_dgrad(dh_b, dkvc_t, dkvp_t, dz1, w_in_t)
    small = [(w_spatial, m_w_spatial, v_w_spatial), (v_ln_g, m_v_ln_g, v_v_ln_g), (v_ln_b, m_v_ln_b, v_v_ln_b),
             (b_spatial, m_b_spatial, v_b_spatial), (ln1_g, m_ln1_g, v_ln1_g), (ln1_b, m_ln1_b, v_ln1_b),
             (ln2_g, m_ln2_g, v_ln2_g), (ln2_b, m_ln2_b, v_ln2_b), (sinks, m_sinks, v_sinks)]
    views = [(-1, BLK), None, None, (N_HEADS, BLK)] + [None] * 5
    small_update = _adamw_small(parts_w, parts_vec, [
        tuple(a if vw is None else a.reshape(vw) for a in p) for p, vw in zip(small, views)])
    (out_out, ff1_out, ff2_out, in_out_t), small_res, _ = _adamw_shards("adamw_all", [
        (own_out, [r_out], big["out"], m_w_out[0], v_w_out[0]),
        (own_ff1, [r_ff1_a, r_ff1_b], big["ff1"], m_w_ff1[0], v_w_ff1[0]),
        (own_ff2, [r_ff2], big["ff2"], m_w_ff2[0], v_w_ff2[0]),
        (own_in, [r_in], big["in"].T, m_w_in[0].T, v_w_in[0].T)], rider=small_update)
    in_out = [o.T for o in in_out_t]
    small_out = [[o.reshape(p[0].shape) for o in small_res[4 * i:4 * i + 4]] for i, p in enumerate(small)]
    loss = small_res[-1][0, 0]

    big_out = {0: in_out, 6: out_out, 9: ff1_out, 10: ff2_out}
    small_slot = {3: 0, 1: 1, 2: 2, 4: 3, 7: 4, 8: 5, 11: 6, 12: 7, 5: 8}
    outs = [loss, grad_x.reshape(x.shape)]
    for kind in range(4):
        for wi in range(13):
            outs.append(big_out[wi][kind][None] if wi in big_out else small_out[small_slot[wi]][kind])
    return tuple(outs)
```

```python
import math

import jax
import jax.numpy as jnp
from jax import lax
from jax.experimental import pallas as pl
from jax.experimental.pallas import tpu as pltpu

F32 = jnp.float32
BF16 = jnp.bfloat16
MESH = pl.DeviceIdType.MESH

HEAD_DIM = 64
N_HEADS = 8
N_KV_HEADS = 2
BLK = 128
D_GMLP = N_HEADS * HEAD_DIM
D_ATTN = N_HEADS * HEAD_DIM
D_KV = N_KV_HEADS * HEAD_DIM
D_IN = 2 * D_GMLP + D_ATTN + 2 * D_KV
COL_U, COL_V, COL_Q, COL_K = 0, D_GMLP, 2 * D_GMLP, 2 * D_GMLP + D_ATTN
ROPE_THETA = 10000.0
LN_EPS = 1e-5
ALPHA = 2.0 ** 0.25
NEG_INF = -1e30
SCORE_SCALE = 1.0 / math.sqrt(HEAD_DIM)
ADAM_LR, ADAM_B1, ADAM_B2, ADAM_EPS, ADAM_WD, ADAM_STEP = 0.001, 0.9, 0.999, 1e-08, 0.01, 10
N_DEV = 8
LANES = 128
VMEM_LIMIT = 56 * 1024 * 1024
FFN_ROWS = 256

NT = (((1,), (1,)), ((), ()))
TN = (((0,), (0,)), ((), ()))


def _params(*sem):
    return pltpu.CompilerParams(dimension_semantics=sem, vmem_limit_bytes=VMEM_LIMIT)


def _dot(a, b, dims=None):
    if dims is None:
        return jnp.dot(a, b, preferred_element_type=F32)
    return lax.dot_general(a, b, dims, preferred_element_type=F32)


def _mean(a):
    return jnp.mean(a, axis=-1, keepdims=True)


def _ln_fwd(z, g, b):
    zc = z - _mean(z)
    rstd = lax.rsqrt(_mean(zc * zc) + LN_EPS)
    xhat = zc * rstd
    return xhat * g + b, xhat, rstd


def _ln_bwd(dy, xhat, rstd, g):
    dxhat = dy * g
    return rstd * (dxhat - _mean(dxhat) - xhat * _mean(dxhat * xhat))


_GELU_C = math.sqrt(2.0 / math.pi)


def _gelu(x):
    t = jnp.tanh(_GELU_C * (x + 0.044715 * (x * x * x)))
    return 0.5 * x * (1.0 + t)


def _gelu_and_grad(x):
    x2 = x * x
    t = jnp.tanh(_GELU_C * (x + 0.044715 * (x2 * x)))
    hx, ht = 0.5 * x, 0.5 * (1.0 + t)
    return x * ht, ht + hx * (1.0 - t * t) * (_GELU_C * (1.0 + 3.0 * 0.044715 * x2))


def _mean0(a):
    return jnp.mean(a, axis=0, keepdims=True)


def _ln_fwd_t(z, g, b):
    zc = z - _mean0(z)
    rstd = lax.rsqrt(_mean0(zc * zc) + LN_EPS)
    xhat = zc * rstd
    return xhat * g + b, xhat, rstd


def _ln_bwd_t(dy, xhat, rstd, g):
    dxhat = dy * g
    return rstd * (dxhat - _mean0(dxhat) - xhat * _mean0(dxhat * xhat))


def _rope_t(t, cos, sin_signed, bwd=False):
    half = HEAD_DIM // 2
    outs = []
    for r in range(0, t.shape[0], HEAD_DIM):
        th = t[r:r + HEAD_DIM]
        sw = jnp.concatenate([th[half:], th[:half]], axis=0) * sin_signed
        outs.append(th * cos - sw if bwd else th * cos + sw)
    return jnp.concatenate(outs, axis=0)


ANY = pl.BlockSpec(memory_space=pl.ANY)
GATHER_PIECES = 2
BF16_ROWS = 16


def _place():
    return lax.axis_index("x"), lax.axis_index("y"), lax.axis_index("c")


class _Comm:
    def __init__(self, ins, outs, sems, start, finish):
        self.ins, self.outs, self.sems, self.start, self.finish = ins, outs, sems, start, finish


def _gather_comm(arrs):
    n = len(arrs)
    pieces = []
    for a, arr in enumerate(arrs):
        k = GATHER_PIECES
        while arr.shape[0] % (k * BF16_ROWS):
            k //= 2
        pieces += [(a, p * (arr.shape[0] // k), arr.shape[0] // k) for p in range(k)]

    def parts(ins, outs, sems):
        send_sems, recv_sems, local_sems = sems
        x, y, c = _place()
        me, sibling = (x, y, c), (x, y, 1 - c)
        chips = [(1 - x, y), (x, 1 - y), (1 - x, 1 - y)]

        def copy(u, k, block, to, local=False):
            a, r0, nr = pieces[u]
            px, py, pc = block
            dst = outs[a].at[4 * px + 2 * py + pc, pl.ds(r0, nr)]
            return pltpu.make_async_remote_copy(
                src_ref=ins[a].at[pl.ds(r0, nr)] if local else dst, dst_ref=dst,
                send_sem=send_sems.at[u, k], recv_sem=recv_sems.at[u, k], device_id=to, device_id_type=MESH)

        mine = [pltpu.make_async_copy(ins[a], outs[a].at[4 * x + 2 * y + c], local_sems.at[a]) for a in range(n)]
        first = []
        for u in range(len(pieces)):
            first.append(copy(u, 0, me, sibling, local=True))
            first += [copy(u, 1 + j, me, (*chip, c), local=True) for j, chip in enumerate(chips)]
        return copy, mine, first, me, sibling, chips, c

    def start(ins, outs, sems):
        _, mine, first, *_ = parts(ins, outs, sems)
        for cp in mine + first:
            cp.start()

    def finish(ins, outs, sems):
        copy, mine, first, me, sibling, chips, c = parts(ins, outs, sems)
        passed = []
        for u in range(len(pieces)):
            for j, chip in enumerate(chips):
                copy(u, 1 + j, (*chip, c), me).wait_recv()
                fwd = copy(u, 4 + j, (*chip, c), sibling)
                fwd.start()
                passed.append(fwd)
        for u in range(len(pieces)):
            copy(u, 0, sibling, me).wait_recv()
            for j, chip in enumerate(chips):
                copy(u, 4 + j, (*chip, 1 - c), me).wait_recv()
        for cp in first + passed:
            cp.wait_send()
        for cp in mine:
            cp.wait()

    return _Comm(list(arrs), [jax.ShapeDtypeStruct((N_DEV,) + a.shape, a.dtype) for a in arrs],
                 [pltpu.SemaphoreType.DMA((len(pieces), 7)), pltpu.SemaphoreType.DMA((len(pieces), 7)),
                  pltpu.SemaphoreType.DMA((n,))], start, finish)


def _sibling_comm(parts):
    n = len(parts)

    def copies(ins, outs, sems):
        x, y, c = _place()
        return [pltpu.make_async_remote_copy(
            src_ref=ins[a].at[2 * q + (1 - c)], dst_ref=outs[a].at[q],
            send_sem=sems[0].at[a, q], recv_sem=sems[1].at[a, q],
            device_id=(x, y, 1 - c), device_id_type=MESH) for a in range(n) for q in range(4)]

    return _Comm(list(parts), [jax.ShapeDtypeStruct((4,) + p.shape[1:], p.dtype) for p in parts],
                 [pltpu.SemaphoreType.DMA((n, 4)), pltpu.SemaphoreType.DMA((n, 4))],
                 lambda *r: [cp.start() for cp in copies(*r)], lambda *r: [cp.wait() for cp in copies(*r)])


def _chips_comm(chip_parts, rows=None):
    n = len(chip_parts)
    r0, nr = (0, None) if rows is None else rows

    def copies(ins, outs, sems):
        x, y, c = _place()
        chips = [(1 - x, y), (x, 1 - y), (1 - x, 1 - y)]
        src = lambda a, q: ins[a].at[q] if rows is None else ins[a].at[q, pl.ds(r0, nr)]
        return [pltpu.make_async_remote_copy(
            src_ref=src(a, 2 * px + py), dst_ref=outs[a].at[k],
            send_sem=sems[0].at[a, k], recv_sem=sems[1].at[a, k],
            device_id=(px, py, c), device_id_type=MESH) for a in range(n) for k, (px, py) in enumerate(chips)]

    shape = lambda p: (3,) + p.shape[1:] if rows is None else (3, nr) + p.shape[2:]
    return _Comm(list(chip_parts), [jax.ShapeDtypeStruct(shape(p), p.dtype) for p in chip_parts],
                 [pltpu.SemaphoreType.DMA((n, 3)), pltpu.SemaphoreType.DMA((n, 3))],
                 lambda *r: [cp.start() for cp in copies(*r)], lambda *r: [cp.wait() for cp in copies(*r)])


def _flips_comm(sums, first):
    m = sums.shape[0]

    def copies(ins, outs, sems):
        return [pltpu.make_async_remote_copy(
            src_ref=ins[0].at[j], dst_ref=outs[0].at[j], send_sem=sems[0].at[j], recv_sem=sems[1].at[j],
            device_id=_flipped(first + j), device_id_type=MESH) for j in range(m)]

    return _Comm([sums], [jax.ShapeDtypeStruct(sums.shape, sums.dtype)],
                 [pltpu.SemaphoreType.DMA((m,)), pltpu.SemaphoreType.DMA((m,))],
                 lambda *r: [cp.start() for cp in copies(*r)], lambda *r: [cp.wait() for cp in copies(*r)])


def _carry(body, *, name, grid, in_specs, out_specs, out_shape, args, comms=(), scratch_shapes=(), prefetch=()):
    n_pre, n_in, n_out, n_scr = len(prefetch), len(in_specs), len(out_specs), len(scratch_shapes)
    c_ins = [a for cm in comms for a in cm.ins]
    c_outs = [s for cm in comms for s in cm.outs]
    c_sems = [s for cm in comms for s in cm.sems]

    def wrapped(*refs):
        pre, refs = refs[:n_pre], refs[n_pre:]
        ins, refs = refs[:n_in], refs[n_in:]
        cins, refs = refs[:len(c_ins)], refs[len(c_ins):]
        outs, refs = refs[:n_out], refs[n_out:]
        couts, refs = refs[:len(c_outs)], refs[len(c_outs):]
        scr, sems = refs[:n_scr], refs[n_scr:]
        groups, i0, o0, s0 = [], 0, 0, 0
        for cm in comms:
            groups.append((cm, cins[i0:i0 + len(cm.ins)], couts[o0:o0 + len(cm.outs)], sems[s0:s0 + len(cm.sems)]))
            i0, o0, s0 = i0 + len(cm.ins), o0 + len(cm.outs), s0 + len(cm.sems)
        first = pl.program_id(0) == 0
        last = pl.program_id(0) == grid[0] - 1
        for ax in range(1, len(grid)):
            first = first & (pl.program_id(ax) == 0)
            last = last & (pl.program_id(ax) == grid[ax] - 1)
        if comms:
            @pl.when(first)
            def _():
                for cm, ci, co, cs in groups:
                    cm.start(ci, co, cs)
        body(*pre, *ins, *outs, *scr)
        if comms:
            @pl.when(last)
            def _():
                for cm, ci, co, cs in groups:
                    cm.finish(ci, co, cs)

    grid_spec = pltpu.PrefetchScalarGridSpec(
        num_scalar_prefetch=n_pre, grid=grid,
        in_specs=list(in_specs) + [ANY] * len(c_ins), out_specs=list(out_specs) + [ANY] * len(c_outs),
        scratch_shapes=list(scratch_shapes) + c_sems)
    res = pl.pallas_call(
        wrapped, name=name, grid_spec=grid_spec, out_shape=list(out_shape) + c_outs,
        compiler_params=_params(*(["arbitrary"] * len(grid))),
    )(*prefetch, *args, *c_ins)
    outs, rest, per_comm = res[:n_out], res[n_out:], []
    for cm in comms:
        per_comm.append(rest[:len(cm.outs)])
        rest = rest[len(cm.outs):]
    return outs, per_comm


def _rope_tables(pos_row, inv_freq_col, comms=()):
    t_tok = pos_row.shape[1]
    tm = min(512, t_tok)

    def body(pos_ref, invf_ref, cos_ref, sin_ref):
        ang = pos_ref[...].astype(F32) * invf_ref[...]
        row = lax.broadcasted_iota(jnp.int32, ang.shape, 0)
        cos_ref[...] = jnp.cos(ang)
        sin_ref[...] = jnp.sin(ang) * jnp.where(row < HEAD_DIM // 2, -1.0, 1.0)

    return _carry(
        body, name="rope_tables", grid=(t_tok // tm,), comms=comms,
        in_specs=[pl.BlockSpec((1, tm), lambda i: (0, i)), pl.BlockSpec((HEAD_DIM, 1), lambda i: (0, 0))],
        out_specs=[pl.BlockSpec((HEAD_DIM, tm), lambda i: (0, i))] * 2,
        out_shape=[jax.ShapeDtypeStruct((HEAD_DIM, t_tok), F32)] * 2,
        args=(pos_row, inv_freq_col))


def _proj_in(x2, w_in_t, comms=()):
    t_tok, d = x2.shape
    d_in = w_in_t.shape[0]
    tm = min(512, t_tok)

    def body(x_ref, w_ref, h_ref, xb_ref):
        xb = x_ref[...].astype(BF16)
        xb_ref[...] = xb
        h_ref[...] = _dot(w_ref[...], xb, NT)

    return _carry(
        body, name="proj_in", grid=(t_tok // tm,), comms=comms,
        in_specs=[pl.BlockSpec((tm, d), lambda i: (i, 0)), pl.BlockSpec((d_in, d), lambda i: (0, 0))],
        out_specs=[pl.BlockSpec((d_in, tm), lambda i: (0, i)), pl.BlockSpec((tm, d), lambda i: (i, 0))],
        out_shape=[jax.ShapeDtypeStruct((d_in, t_tok), F32), jax.ShapeDtypeStruct((t_tok, d), BF16)],
        args=(x2, w_in_t))


MIX_BLOCKS = 2
MIX_W = MIX_BLOCKS * BLK


def _prev_block(i):
    return jnp.maximum(MIX_BLOCKS * i - 1, 0)


def _h_specs():
    kv_row = COL_K // (2 * D_KV)
    return [
        pl.BlockSpec((D_GMLP, MIX_W), lambda i: (0, i)),
        pl.BlockSpec((D_GMLP, MIX_W), lambda i: (1, i)),
        pl.BlockSpec((D_ATTN, MIX_W), lambda i: (2, i)),
        pl.BlockSpec((2 * D_KV, MIX_W), lambda i: (kv_row, i)),
        pl.BlockSpec((2 * D_KV, BLK), lambda i: (kv_row, _prev_block(i))),
    ]


def _table_specs():
    return [
        pl.BlockSpec((HEAD_DIM, MIX_W), lambda i: (0, i)),
        pl.BlockSpec((HEAD_DIM, MIX_W), lambda i: (0, i)),
        pl.BlockSpec((HEAD_DIM, BLK), lambda i: (0, _prev_block(i))),
        pl.BlockSpec((HEAD_DIM, BLK), lambda i: (0, _prev_block(i))),
    ]


def _cols(b):
    return slice(b * BLK, (b + 1) * BLK)


LSE_ROWS = 8
LSE_SPEC = pl.BlockSpec((LSE_ROWS, D_ATTN), lambda i: (i, 0))


def _block_inputs(b, i, kvc, kvp_ref, cos, sin, cosp_ref, sinp_ref, bias_ref):
    if b == 0:
        kv_prev, cos_prev, sin_prev, bias = kvp_ref[...], cosp_ref[...], sinp_ref[...], bias_ref[jnp.minimum(i, 1)]
    else:
        kv_prev, cos_prev, sin_prev, bias = kvc[:, _cols(b - 1)], cos[:, _cols(b - 1)], sin[:, _cols(b - 1)], bias_ref[1]
    return kvc[:, _cols(b)], kv_prev, cos[:, _cols(b)], sin[:, _cols(b)], cos_prev, sin_prev, bias


def _band_bias():
    ki = lax.broadcasted_iota(jnp.int32, (2, 2 * BLK, BLK), 1)
    qi = lax.broadcasted_iota(jnp.int32, (2, 2 * BLK, BLK), 2)
    later = lax.broadcasted_iota(jnp.int32, (2, 2 * BLK, BLK), 0) > 0
    dist = qi + BLK - ki
    return jnp.where((dist >= 0) & (dist < BLK) & ((ki >= BLK) | later), 0.0, NEG_INF).astype(F32)


BIAS_SPEC = pl.BlockSpec((2, 2 * BLK, BLK), lambda i: (0, 0, 0))


def _keys_values(kvc, kvp, cosc, sinc, cosp, sinp):
    kp, kc = _rope_t(kvp[:D_KV], cosp, sinp), _rope_t(kvc[:D_KV], cosc, sinc)
    k_t = jnp.concatenate([kp, kc], axis=1).astype(BF16)
    k_n = jnp.concatenate([kp.T, kc.T], axis=0).astype(BF16)
    v_t = jnp.concatenate([kvp[D_KV:], kvc[D_KV:]], axis=1).astype(BF16)
    return k_t, k_n, v_t


def _pad_head(th, kv):
    z = jnp.zeros_like(th)
    return jnp.concatenate([th, z] if kv == 0 else [z, th], axis=0)


def _group_lanes(parts):
    return jnp.concatenate(parts, axis=1)


def _softmax_sink_t(s, sink):
    m = jnp.maximum(jnp.max(s, axis=0, keepdims=True), sink)
    e = jnp.exp(s - m)
    denom = jnp.sum(e, axis=0, keepdims=True) + jnp.exp(sink - m)
    return e * (1.0 / denom), m + jnp.log(denom)


def _causal():
    row = lax.broadcasted_iota(jnp.int32, (BLK, BLK), 0)
    col = lax.broadcasted_iota(jnp.int32, (BLK, BLK), 1)
    return row >= col


def _mask_w_once(wsp_ref, wm_scr):
    @pl.when(pl.program_id(0) == 0)
    def _():
        causal = _causal()
        for hh in range(N_HEADS):
            wm_scr[hh] = jnp.where(causal, wsp_ref[hh], 0.0).astype(BF16)


def _mixer_fwd(h_t, cos_t, sin_t, w_spatial, b_spatial, vln_g, vln_b, sinks, band_bias, comms=()):
    t_tok = h_t.shape[1]
    group = N_HEADS // N_KV_HEADS

    def body(sinks_ref, u_ref, vg_ref, q_ref, kvc_ref, kvp_ref, cos_ref, sin_ref, cosp_ref, sinp_ref,
             wsp_ref, bsp_ref, g_ref, b_ref, bias_ref, cat_ref, lse_ref, wm_scr):
        i = pl.program_id(0)
        _mask_w_once(wsp_ref, wm_scr)
        lse_ref[...] = jnp.zeros_like(lse_ref)
        ua = _gelu(u_ref[...])
        vp, _, _ = _ln_fwd_t(_gelu(vg_ref[...]), g_ref[...], b_ref[...])
        vpb = vp.astype(BF16)
        for b in range(MIX_BLOCKS):
            for hh in range(N_HEADS):
                rows = slice(hh * HEAD_DIM, (hh + 1) * HEAD_DIM)
                mixed = _dot(vpb[rows, _cols(b)], wm_scr[hh], NT) + bsp_ref[hh:hh + 1, :]
                cat_ref[rows, _cols(b)] = (ua[rows, _cols(b)] * mixed).astype(BF16)

        kvc, cos, sin = kvc_ref[...], cos_ref[...], sin_ref[...]
        qr = (_rope_t(q_ref[...], cos, sin) * SCORE_SCALE).astype(BF16)
        sinks4 = [_group_lanes([jnp.full((1, BLK), sinks_ref[hh], F32) for hh in range(kv * group, (kv + 1) * group)])
                  for kv in range(N_KV_HEADS)]
        for b in range(MIX_BLOCKS):
            kv_cur, kv_prev, cosc, sinc, cosp, sinp, bias1 = _block_inputs(b, i, kvc, kvp_ref, cos, sin, cosp_ref, sinp_ref, bias_ref)
            _, k_n, v_t = _keys_values(kv_cur, kv_prev, cosc, sinc, cosp, sinp)
            bias = _group_lanes([bias1] * group)
            for kv in range(N_KV_HEADS):
                heads = range(kv * group, (kv + 1) * group)
                qs = _group_lanes([qr[hh * HEAD_DIM:(hh + 1) * HEAD_DIM, _cols(b)] for hh in heads])
                p, lse = _softmax_sink_t(_dot(k_n, _pad_head(qs, kv)) + bias, sinks4[kv])
                lse_ref[b * N_KV_HEADS + kv:b * N_KV_HEADS + kv + 1, :] = lse
                o = _dot(v_t[kv * HEAD_DIM:(kv + 1) * HEAD_DIM], p.astype(BF16)).astype(BF16)
                for j, hh in enumerate(heads):
                    cat_ref[D_GMLP + hh * HEAD_DIM:D_GMLP + (hh + 1) * HEAD_DIM, _cols(b)] = o[:, j * BLK:(j + 1) * BLK]

    full = lambda shape: pl.BlockSpec(shape, lambda i: (0,) * len(shape))
    return _carry(
        body, name="mixer_fwd", grid=(t_tok // MIX_W,), comms=comms,
        in_specs=[pl.BlockSpec(memory_space=pltpu.SMEM)] + _h_specs() + _table_specs() + [
            full((N_HEADS, BLK, BLK)), full((N_HEADS, BLK)), full((D_GMLP, 1)), full((D_GMLP, 1)), BIAS_SPEC],
        out_specs=[pl.BlockSpec((D_GMLP + D_ATTN, MIX_W), lambda i: (0, i)), LSE_SPEC],
        out_shape=[jax.ShapeDtypeStruct((D_GMLP + D_ATTN, t_tok), BF16),
                   jax.ShapeDtypeStruct((t_tok // MIX_W * LSE_ROWS, D_ATTN), F32)],
        scratch_shapes=[pltpu.VMEM((N_HEADS, BLK, BLK), BF16)],
        args=(sinks, h_t, h_t, h_t, h_t, h_t, cos_t, sin_t, cos_t, sin_t, w_spatial, b_spatial, vln_g, vln_b, band_bias))


def _proj_out(cat_t, x2, w_out_b, ln1_g, ln1_b, comms=()):
    t_tok, d = x2.shape
    tm = min(512, t_tok)

    def body(cat_ref, x_ref, w_ref, g_ref, b_ref, xhat_ref, rstd_ref, x1b_ref):
        x1, xhat, rstd = _ln_fwd(ALPHA * x_ref[...] + _dot(cat_ref[...], w_ref[...], TN), g_ref[...], b_ref[...])
        xhat_ref[...] = xhat
        rstd_ref[...] = rstd
        x1b_ref[...] = x1.astype(BF16)

    tok = lambda w: pl.BlockSpec((tm, w), lambda i: (i, 0))
    vec = pl.BlockSpec((1, d), lambda i: (0, 0))
    return _carry(
        body, name="proj_out", grid=(t_tok // tm,), comms=comms,
        in_specs=[pl.BlockSpec((cat_t.shape[0], tm), lambda i: (0, i)), tok(d), pl.BlockSpec(w_out_b.shape, lambda i: (0, 0)), vec, vec],
        out_specs=[tok(d), tok(1), tok(d)],
        out_shape=[jax.ShapeDtypeStruct((t_tok, d), F32), jax.ShapeDtypeStruct((t_tok, 1), F32), jax.ShapeDtypeStruct((t_tok, d), BF16)],
        args=(cat_t, x2, w_out_b, ln1_g, ln1_b))


def _ffn_fwd_bwd(xhat1, rstd1, x1b, target, w1_parts, w2_parts, ln1_g, ln1_b, ln2_g, ln2_b):
    t_tok, d = xhat1.shape
    n_part = len(w1_parts)
    n_chunk, _, fp = w1_parts[0].shape
    f = n_chunk * n_part * fp
    tm = min(FFN_ROWS, t_tok)

    def body(xhat1_ref, rstd1_ref, x1b_ref, tgt_ref, *refs):
        w1_hbm, w2_hbm = refs[:n_part], refs[n_part:2 * n_part]
        (g1_ref, b1_ref, g2_ref, b2_ref, act_ref, dpre_ref, dz2b_ref, dz1_ref, stats_ref,
         r_scr, w1_ref, w2_ref, w_sems) = refs[2 * n_part:]

        @pl.when(pl.program_id(0) == 0)
        def _():
            stats_ref[...] = jnp.zeros_like(stats_ref)
            loads = []
            for j in range(n_chunk):
                for p in range(n_part):
                    units = pl.ds((j * n_part + p) * fp, fp)
                    loads.append(pltpu.make_async_copy(w1_hbm[p].at[j], w1_ref.at[:, units], w_sems.at[0, p, j]))
                    loads.append(pltpu.make_async_copy(w2_hbm[p].at[j], w2_ref.at[units, :], w_sems.at[1, p, j]))
            for cp in loads:
                cp.start()
            for cp in loads:
                cp.wait()

        g1, g2 = g1_ref[...], g2_ref[...]
        xhat1 = xhat1_ref[...]
        r_scr[...] = jnp.maximum(_dot(x1b_ref[...], w1_ref[...]), 0.0)
        r = r_scr[...]
        act = (r * r).astype(BF16)
        act_ref[...] = act
        ff = _dot(act, w2_ref[...])
        y, xhat2, rstd2 = _ln_fwd(ALPHA * (xhat1 * g1 + b1_ref[...]) + ff, g2, b2_ref[...])
        diff = y - tgt_ref[...]
        loss = 0.5 * jnp.sum(jnp.sum(diff * diff, axis=-1, keepdims=True) / d, axis=0, keepdims=True)
        dy = diff / d
        dz2 = _ln_bwd(dy, xhat2, rstd2, g2)
        dz2b = dz2.astype(BF16)
        dz2b_ref[...] = dz2b
        dpre = (_dot(dz2b, w2_ref[...], NT) * (2.0 * r_scr[...])).astype(BF16)
        dpre_ref[...] = dpre
        dx1 = ALPHA * dz2 + _dot(dpre, w1_ref[...], NT)
        dz1_ref[...] = _ln_bwd(dx1, xhat1, rstd1_ref[...], g1)
        stats_ref[0:1, :] += jnp.sum(dx1 * xhat1, axis=0, keepdims=True)
        stats_ref[1:2, :] += jnp.sum(dx1, axis=0, keepdims=True)
        stats_ref[2:3, :] += jnp.sum(dy * xhat2, axis=0, keepdims=True)
        stats_ref[3:4, :] += jnp.sum(dy, axis=0, keepdims=True)
        stats_ref[4:5, :] += jnp.broadcast_to(loss, (1, d))

    tok = lambda w: pl.BlockSpec((tm, w), lambda i: (i, 0))
    vec = pl.BlockSpec((1, d), lambda i: (0, 0))
    return _carry(
        body, name="ffn_fwd_bwd", grid=(t_tok // tm,),
        in_specs=[tok(d), tok(1), tok(d), tok(d)] + [ANY] * (2 * n_part) + [vec, vec, vec, vec],
        out_specs=[tok(f), tok(f), tok(d), tok(d), pl.BlockSpec((8, d), lambda i: (0, 0))],
        out_shape=[jax.ShapeDtypeStruct((t_tok, f), BF16), jax.ShapeDtypeStruct((t_tok, f), BF16),
                   jax.ShapeDtypeStruct((t_tok, d), BF16), jax.ShapeDtypeStruct((t_tok, d), F32), jax.ShapeDtypeStruct((8, d), F32)],
        scratch_shapes=[pltpu.VMEM((tm, f), F32), pltpu.VMEM((d, f), BF16), pltpu.VMEM((f, d), BF16),
                        pltpu.SemaphoreType.DMA((2, n_part, n_chunk))],
        args=(xhat1, rstd1, x1b, target, *w1_parts, *w2_parts, ln1_g, ln1_b, ln2_g, ln2_b))[0]


WGRAD_STEPS = [(True, 0), (True, 1), (False, 0), (True, 2), (False, 1), (True, 3), (False, 2), (False, 3)]
CHIP_FLIPS = [3, 1, 2, 0]


def _pick(table, s):
    out = table[-1]
    for i in range(len(table) - 2, -1, -1):
        out = jnp.where(s == i, table[i], out)
    return out


def _wgrad_shard(s, cc):
    q = jnp.bitwise_xor(cc[1], _pick([CHIP_FLIPS[k] for _, k in WGRAD_STEPS], s))
    return 2 * q + jnp.where(_pick([int(sibling) for sibling, _ in WGRAD_STEPS], s) == 1, 1 - cc[0], cc[0])


def _flipped(k):
    x, y, c = _place()
    return (1 - x if CHIP_FLIPS[k] // 2 else x, 1 - y if CHIP_FLIPS[k] % 2 else y, c)


def _wgrad_pair_sum(name, product, chunk, in_specs, args, core_chip, n_sent, comms=(), scratch_shapes=()):
    half = N_DEV // 2
    n_in, n_out = len(in_specs), 2 + (0 < n_sent) + (n_sent < half - 1)

    def body(cc_ref, *refs):
        ins, outs, scr = refs[:n_in], refs[n_in:n_in + n_out], refs[n_in + n_out:]
        (own_ref, recv_ref), from_chips_ref, wire_ref = outs[-2:], outs[0], outs[n_out - 3]
        send_buf, got, send_sems, recv_sems, got_sem, wire_buf, leave_sems, arrive_sems = scr[:8]
        s = pl.program_id(0)
        x, y, c = _place()
        def send(q):
            return pltpu.make_async_remote_copy(
                src_ref=send_buf.at[q % 2], dst_ref=recv_ref.at[q], send_sem=send_sems.at[q], recv_sem=recv_sems.at[q],
                device_id=(x, y, 1 - c), device_id_type=MESH)

        def load(q):
            return pltpu.make_async_copy(recv_ref.at[q], got, got_sem.at[0])

        def leave(k):
            if k < n_sent:
                return pltpu.make_async_remote_copy(
                    src_ref=wire_buf.at[k], dst_ref=from_chips_ref.at[k], send_sem=leave_sems.at[k],
                    recv_sem=arrive_sems.at[k], device_id=_flipped(k), device_id_type=MESH)
            return pltpu.make_async_copy(wire_buf.at[k], wire_ref.at[k - n_sent], leave_sems.at[k])

        for step, (sibling, q) in enumerate(WGRAD_STEPS):
            if not sibling:
                @pl.when(s == step)
                def _(q=q):
                    send(q).wait_recv()
                    load(q).start()

        g = product(_wgrad_shard(s, cc_ref), *ins, *scr[8:])

        for step, (sibling, q) in enumerate(WGRAD_STEPS):
            @pl.when(s == step)
            def _(sibling=sibling, q=q):
                if sibling:
                    if q >= 2:
                        send(q - 2).wait_send()
                    send_buf[q % 2] = g
                    send(q).start()
                    return
                load(q).wait()
                total = g + got[...]
                if q < half - 1:
                    wire_buf[q] = total.astype(BF16)
                    leave(q).start()
                else:
                    own_ref[...] = total

        @pl.when(s == N_DEV - 1)
        def _():
            for q in range(half - 2, half):
                send(q).wait_send()
            for k in range(half - 1):
                leave(k).wait()

    sums = lambda n: [jax.ShapeDtypeStruct((n,) + chunk, BF16)] if n else []
    sem = lambda n: pltpu.SemaphoreType.DMA((n,))
    res, per_comm = _carry(
        body, name=name, grid=(N_DEV,), comms=comms, prefetch=(core_chip,), in_specs=in_specs,
        out_specs=[ANY] * (n_out - 2) + [pl.BlockSpec(chunk, lambda s, cc: (0, 0)), ANY],
        out_shape=sums(n_sent) + sums(half - 1 - n_sent) + [jax.ShapeDtypeStruct(chunk, F32),
                                                            jax.ShapeDtypeStruct((half,) + chunk, F32)],
        scratch_shapes=[pltpu.VMEM((2,) + chunk, F32), pltpu.VMEM(chunk, F32), sem(half), sem(half), sem(1),
                        pltpu.VMEM((half - 1,) + chunk, BF16), sem(half - 1), sem(half - 1), *scratch_shapes],
        args=args)
    return res[0] if n_sent else None, res[n_out - 3] if n_sent < half - 1 else None, res[-2], per_comm


def _resident(a):
    return pl.BlockSpec(a.shape, lambda s, cc: (0,) * a.ndim, pipeline_mode=pl.Buffered(1))


def _ffn_wgrad(name, lhs, rhs, chunk_lhs, core_chip, n_sent, comms=()):
    t_tok = lhs.shape[0]
    fc = (lhs if chunk_lhs else rhs).shape[1] // N_DEV
    chunked = pl.BlockSpec((t_tok, fc), lambda s, cc: (0, _wgrad_shard(s, cc)))

    def product(shard, lhs_ref, rhs_ref):
        return _dot(lhs_ref[...], rhs_ref[...], TN)

    return _wgrad_pair_sum(
        name, product, (fc, rhs.shape[1]) if chunk_lhs else (lhs.shape[1], fc),
        [chunked, _resident(rhs)] if chunk_lhs else [_resident(lhs), chunked], (lhs, rhs), core_chip, n_sent, comms)


def _proj_out_bwd(dz1, cat_t, w_out_b, comms=()):
    t_tok, d = dz1.shape
    d_mix = cat_t.shape[0]
    tm = min(512, t_tok)

    def body(dz1_ref, cat_ref, w_ref, dcat_ref, gw_ref):
        @pl.when(pl.program_id(0) == 0)
        def _():
            gw_ref[...] = jnp.zeros_like(gw_ref)

        dzb = dz1_ref[...].astype(BF16)
        dcat_ref[...] = _dot(w_ref[...], dzb, NT)
        gw_ref[...] += _dot(cat_ref[...], dzb)

    return _carry(
        body, name="proj_out_bwd", grid=(t_tok // tm,), comms=comms,
        in_specs=[pl.BlockSpec((tm, d), lambda i: (i, 0)), pl.BlockSpec((d_mix, tm), lambda i: (0, i)),
                  pl.BlockSpec((d_mix, d), lambda i: (0, 0))],
        out_specs=[pl.BlockSpec((d_mix, tm), lambda i: (0, i)), pl.BlockSpec((d_mix, d), lambda i: (0, 0))],
        out_shape=[jax.ShapeDtypeStruct((d_mix, t_tok), F32), jax.ShapeDtypeStruct((d_mix, d), F32)],
        args=(dz1, cat_t, w_out_b))


def _mixer_bwd(dcat_t, h_t, cos_t, sin_t, w_spatial, b_spatial, vln_g, vln_b, sinks, band_bias, lse, comms=()):
    t_tok = h_t.shape[1]
    nb, n_step = t_tok // BLK, t_tok // MIX_W
    group = N_HEADS // N_KV_HEADS

    def body(sinks_ref, dcat_ref, u_ref, vg_ref, q_ref, kvc_ref, kvp_ref, cos_ref, sin_ref, cosp_ref, sinp_ref,
             wsp_ref, bsp_ref, g_ref, b_ref, bias_ref, lse_ref, dh_ref, dkvc_ref, dkvp_ref, gwsb_ref, gbsp_ref, gvln_ref, gsink_ref,
             dg_acc, db_acc, wm_scr, gws_ref):
        i = pl.program_id(0)

        @pl.when(i == 0)
        def _():
            gws_ref[...] = jnp.zeros_like(gws_ref)
            gbsp_ref[...] = jnp.zeros_like(gbsp_ref)
            gsink_ref[...] = jnp.zeros_like(gsink_ref)
            dg_acc[...] = jnp.zeros_like(dg_acc)
            db_acc[...] = jnp.zeros_like(db_acc)

        _mask_w_once(wsp_ref, wm_scr)

        g = g_ref[...]
        ua, ua_grad = _gelu_and_grad(u_ref[...])
        vv, vv_grad = _gelu_and_grad(vg_ref[...])
        vp, vhat, rstd = _ln_fwd_t(vv, g, b_ref[...])
        vpb = vp.astype(BF16)
        da = dcat_ref[0:D_GMLP, :]
        dmixed = da * ua
        dvp_blocks = []
        for b in range(MIX_BLOCKS):
            dvp_parts = []
            for hh in range(N_HEADS):
                rows = slice(hh * HEAD_DIM, (hh + 1) * HEAD_DIM)
                vpb_h = vpb[rows, _cols(b)]
                mixed = _dot(vpb_h, wm_scr[hh], NT) + bsp_ref[hh:hh + 1, :]
                dh_ref[COL_U + hh * HEAD_DIM:COL_U + (hh + 1) * HEAD_DIM, _cols(b)] = (
                    da[rows, _cols(b)] * mixed * ua_grad[rows, _cols(b)]).astype(BF16)
                dm = dmixed[rows, _cols(b)]
                dmb = dm.astype(BF16)
                gbsp_ref[hh:hh + 1, :] += jnp.sum(dm, axis=0, keepdims=True)
                gws_ref[hh] += _dot(dmb, vpb_h, TN)
                dvp_parts.append(_dot(dmb, wm_scr[hh]))
            dvp_blocks.append(jnp.concatenate(dvp_parts, axis=0))
        dvp = jnp.concatenate(dvp_blocks, axis=1)
        dgv, dbv = dvp * vhat, dvp
        for b in range(MIX_BLOCKS):
            dg_acc[...] += dgv[:, _cols(b)]
            db_acc[...] += dbv[:, _cols(b)]
        dh_ref[COL_V:COL_V + D_GMLP, :] = (_ln_bwd_t(dvp, vhat, rstd, g) * vv_grad).astype(BF16)

        kvc, cos, sin = kvc_ref[...], cos_ref[...], sin_ref[...]
        qr = (_rope_t(q_ref[...], cos, sin) * SCORE_SCALE).astype(BF16)
        sinks4 = [_group_lanes([jnp.full((1, BLK), sinks_ref[hh], F32) for hh in range(kv * group, (kv + 1) * group)])
                  for kv in range(N_KV_HEADS)]
        dq_blocks, dkv_cur, dkv_prev = [], [], []
        for b in range(MIX_BLOCKS):
            kv_cur, kv_prev, cosc, sinc, cosp, sinp, bias1 = _block_inputs(b, i, kvc, kvp_ref, cos, sin, cosp_ref, sinp_ref, bias_ref)
            k_t, k_n, v_t = _keys_values(kv_cur, kv_prev, cosc, sinc, cosp, sinp)
            v_n = jnp.concatenate([kv_prev[D_KV:].T, kv_cur[D_KV:].T], axis=0).astype(BF16)
            bias = _group_lanes([bias1] * group)
            dk, dv, dq_parts = [], [], []
            for kv in range(N_KV_HEADS):
                heads = range(kv * group, (kv + 1) * group)
                kv_rows = slice(kv * HEAD_DIM, (kv + 1) * HEAD_DIM)
                qs = _group_lanes([qr[hh * HEAD_DIM:(hh + 1) * HEAD_DIM, _cols(b)] for hh in heads])
                dos = _group_lanes([dcat_ref[D_GMLP + hh * HEAD_DIM:D_GMLP + (hh + 1) * HEAD_DIM, _cols(b)]
                                    for hh in heads]).astype(BF16)
                lse_g = lse_ref[b * N_KV_HEADS + kv:b * N_KV_HEADS + kv + 1, :]
                p = jnp.exp(_dot(k_n, _pad_head(qs, kv)) + bias - lse_g)
                p_sink = jnp.exp(sinks4[kv] - lse_g)
                dp = _dot(v_n, _pad_head(dos, kv))
                delta = jnp.sum(p * dp, axis=0, keepdims=True)
                ds = (p * (dp - delta)).astype(BF16)
                dsink = p_sink * delta
                dq = _dot(k_t[kv_rows], ds) * SCORE_SCALE
                for j, hh in enumerate(heads):
                    gsink_ref[hh:hh + 1, :] -= dsink[:, j * BLK:(j + 1) * BLK]
                    dq_parts.append(dq[:, j * BLK:(j + 1) * BLK])
                dk.append(_dot(qs, ds, NT))
                dv.append(_dot(dos, p.astype(BF16), NT))
            dq_blocks.append(jnp.concatenate(dq_parts, axis=0))
            dk_all, dv_all = jnp.concatenate(dk, axis=0), jnp.concatenate(dv, axis=0)
            dkv_cur.append(jnp.concatenate([_rope_t(dk_all[:, BLK:], cosc, sinc, bwd=True), dv_all[:, BLK:]], axis=0))
            dkv_prev.append(jnp.concatenate([_rope_t(dk_all[:, :BLK], cosp, sinp, bwd=True), dv_all[:, :BLK]], axis=0))
        dh_ref[COL_Q:COL_Q + D_ATTN, :] = _rope_t(jnp.concatenate(dq_blocks, axis=1), cos, sin, bwd=True).astype(BF16)
        for b in range(MIX_BLOCKS):
            dkvc_ref[:, _cols(b)] = dkv_cur[b] + dkv_prev[b + 1] if b + 1 < MIX_BLOCKS else dkv_cur[b]
        dkvp_ref[...] = dkv_prev[0]

        @pl.when(i == n_step - 1)
        def _():
            causal = _causal()
            for hh in range(N_HEADS):
                gwsb_ref[hh] = jnp.where(causal, gws_ref[hh], 0.0).astype(BF16)
            gvln_ref[...] = jnp.zeros_like(gvln_ref)
            gvln_ref[0:1, :] = jnp.sum(dg_acc[...].T, axis=0, keepdims=True)
            gvln_ref[1:2, :] = jnp.sum(db_acc[...].T, axis=0, keepdims=True)

    full = lambda shape: pl.BlockSpec(shape, lambda i: (0,) * len(shape))
    return _carry(
        body, name="mixer_bwd", grid=(n_step,), comms=comms,
        in_specs=[pl.BlockSpec(memory_space=pltpu.SMEM), pl.BlockSpec((D_GMLP + D_ATTN, MIX_W), lambda i: (0, i))]
        + _h_specs() + _table_specs()
        + [full((N_HEADS, BLK, BLK)), full((N_HEADS, BLK)), full((D_GMLP, 1)), full((D_GMLP, 1)), BIAS_SPEC, LSE_SPEC],
        out_specs=[pl.BlockSpec((COL_K, MIX_W), lambda i: (0, i)), pl.BlockSpec((2 * D_KV, MIX_W), lambda i: (0, i)),
                   pl.BlockSpec((2 * D_KV, BLK), lambda i: (0, (i + n_step - 1) % n_step)),
                   full((N_HEADS, BLK, BLK)), full((N_HEADS, BLK)), full((8, D_GMLP)), full((N_HEADS, LANES))],
        out_shape=[jax.ShapeDtypeStruct((COL_K, t_tok), BF16), jax.ShapeDtypeStruct((2 * D_KV, t_tok), F32),
                   jax.ShapeDtypeStruct((2 * D_KV, n_step * BLK), F32),
                   jax.ShapeDtypeStruct((N_HEADS, BLK, BLK), BF16), jax.ShapeDtypeStruct((N_HEADS, BLK), F32),
                   jax.ShapeDtypeStruct((8, D_GMLP), F32), jax.ShapeDtypeStruct((N_HEADS, LANES), F32)],
        scratch_shapes=[pltpu.VMEM((D_GMLP, BLK), F32), pltpu.VMEM((D_GMLP, BLK), F32), pltpu.VMEM((N_HEADS, BLK, BLK), BF16),
                        pltpu.VMEM((N_HEADS, BLK, BLK), F32)],
        args=(sinks, dcat_t, h_t, h_t, h_t, h_t, h_t, cos_t, sin_t, cos_t, sin_t, w_spatial, b_spatial, vln_g, vln_b, band_bias, lse))


def _dkv_rows(dkvc_ref, dkvp_ref, width, store):
    for s in range(width // MIX_W):
        rest, last = slice(s * MIX_W, (s + 1) * MIX_W - BLK), slice((s + 1) * MIX_W - BLK, (s + 1) * MIX_W)
        store(rest, dkvc_ref[:, rest].astype(BF16))
        store(last, (dkvc_ref[:, last] + dkvp_ref[:, _cols(s)]).astype(BF16))


def _proj_in_wgrad(dh_b, dkvc_t, dkvp_t, xb, core_chip, comms=()):
    t_tok, d = xb.shape
    d_main, d_kv = dh_b.shape[0], dkvc_t.shape[0]
    rows = (d_main + d_kv) // N_DEV
    whole, cut = d_main // rows, d_main % rows

    def product(shard, dh_ref, dkvc_ref, dkvp_ref, xb_ref, dht_scr, sems):
        copies = [pltpu.make_async_copy(dh_ref.at[j * rows:(j + 1) * rows], dht_scr.at[j], sems.at[j]) for j in range(whole)]
        copies.append(pltpu.make_async_copy(dh_ref.at[whole * rows:d_main], dht_scr.at[whole, 0:cut], sems.at[whole]))

        @pl.when(pl.program_id(0) == 0)
        def _():
            for cp in copies:
                cp.start()

            def store(cols, val):
                dht_scr[whole, cut:rows, cols] = val[0:rows - cut]
                dht_scr[whole + 1, :, cols] = val[rows - cut:]

            _dkv_rows(dkvc_ref, dkvp_ref, t_tok, store)
            for cp in copies:
                cp.wait()

        return _dot(dht_scr[shard], xb_ref[...])

    return _wgrad_pair_sum(
        "proj_in_wgrad", product, (rows, d), [ANY, _resident(dkvc_t), _resident(dkvp_t), _resident(xb)],
        (dh_b, dkvc_t, dkvp_t, xb), core_chip, N_DEV // 2 - 1, comms,
        scratch_shapes=[pltpu.VMEM((N_DEV, rows, t_tok), BF16), pltpu.SemaphoreType.DMA((whole + 1,))])


def _proj_in_dgrad(dh_b, dkvc_t, dkvp_t, dz1, w_in_t, comms=()):
    t_tok, d = dz1.shape
    d_main, d_kv = dh_b.shape[0], dkvc_t.shape[0]
    tm = min(512, t_tok)

    def body(dh_ref, dkvc_ref, dkvp_ref, dz1_ref, w_ref, dx_ref, dkv_scr):
        def store(cols, val):
            dkv_scr[:, cols] = val

        _dkv_rows(dkvc_ref, dkvp_ref, tm, store)
        dx_ref[...] = (ALPHA * dz1_ref[...] + _dot(dh_ref[...], w_ref[0:d_main, :], TN)
                       + _dot(dkv_scr[...], w_ref[d_main:, :], TN))

    return _carry(
        body, name="proj_in_dgrad", grid=(t_tok // tm,), comms=comms,
        in_specs=[pl.BlockSpec((d_main, tm), lambda i: (0, i)), pl.BlockSpec((d_kv, tm), lambda i: (0, i)),
                  pl.BlockSpec((d_kv, tm // MIX_BLOCKS), lambda i: (0, i)),
                  pl.BlockSpec((tm, d), lambda i: (i, 0)), pl.BlockSpec((d_main + d_kv, d), lambda i: (0, 0))],
        out_specs=[pl.BlockSpec((tm, d), lambda i: (i, 0))],
        out_shape=[jax.ShapeDtypeStruct((t_tok, d), F32)],
        scratch_shapes=[pltpu.VMEM((d_kv, tm), BF16)],
        args=(dh_b, dkvc_t, dkvp_t, dz1, w_in_t))


def _adamw(w, g, m, v):
    m = ADAM_B1 * m + (1.0 - ADAM_B1) * g
    v = ADAM_B2 * v + (1.0 - ADAM_B2) * (g * g)
    m_hat = m / (1.0 - ADAM_B1 ** ADAM_STEP)
    v_hat = v / (1.0 - ADAM_B2 ** ADAM_STEP)
    delta = -ADAM_LR * (m_hat / (jnp.sqrt(v_hat) + ADAM_EPS) + ADAM_WD * w)
    return delta, m, v


ADAMW_STEPS = 4


def _adamw_shards(name, items, comms=(), rider=None):
    n_in, n_out = sum(4 + len(it[1]) for it in items), 4 * len(items)
    n_rin = len(rider["args"]) if rider else 0

    def body(*refs):
        ins, rins, outs, routs = refs[:n_in], refs[n_in:n_in + n_rin], refs[n_in + n_rin:n_in + n_rin + n_out], refs[n_in + n_rin + n_out:]
        for i, item in enumerate(items):
            (own_ref, w_ref, m_ref, v_ref), recv_refs, ins = ins[:4], ins[4:4 + len(item[1])], ins[4 + len(item[1]):]
            g = own_ref[...]
            for recv_ref in recv_refs:
                for k in range(recv_ref.shape[0]):
                    g = g + recv_ref[k].astype(F32)
            for o_ref, val in zip(outs[4 * i:4 * i + 4], (g,) + _adamw(w_ref[...], g, m_ref[...], v_ref[...])):
                o_ref[...] = val
        if rider:
            pl.when(pl.program_id(0) == 0)(lambda: rider["body"](rins, routs))

    in_specs, out_specs, out_shape, args = [], [], [], []
    for own, recvs, w, m, v in items:
        r, c = own.shape
        tiles = ADAMW_STEPS
        while (r // tiles) % BF16_ROWS:
            tiles //= 2
        blk = pl.BlockSpec((r // tiles, c), lambda s, k=ADAMW_STEPS // tiles: (s // k, 0))
        in_specs += [blk] * 4 + [pl.BlockSpec((a.shape[0], r // tiles, c), lambda s, k=ADAMW_STEPS // tiles: (0, s // k, 0))
                                 for a in recvs]
        out_specs += [blk] * 4
        out_shape += [jax.ShapeDtypeStruct((r, c), F32)] * 4
        args += [own, w, m, v, *recvs]
    if rider:
        in_specs, out_specs = in_specs + rider["in_specs"], out_specs + rider["out_specs"]
        out_shape, args = out_shape + rider["out_shape"], args + rider["args"]
    res, per_comm = _carry(body, name=name, grid=(ADAMW_STEPS,), comms=comms, in_specs=in_specs, out_specs=out_specs,
                           out_shape=out_shape, args=args)
    return [res[4 * i:4 * i + 4] for i in range(len(items))], res[n_out:], per_comm


VEC_VLN, VEC_LN1G, VEC_LN1B, VEC_LN2G, VEC_LN2B, VEC_SINK, VEC_LOSS, VEC_BSP, VEC_ROWS = 0, 1, 2, 3, 4, 5, 6, 8, 16


def _adamw_small(parts_w, parts_vec, params):
    n = parts_w.shape[0]
    flat = [a for p in params for a in p]
    shapes = [p[0].shape for p in params]

    def grads(gw, gv):
        return [gw, gv[VEC_VLN:VEC_VLN + 1, 0:D_GMLP], gv[VEC_VLN:VEC_VLN + 1, D_GMLP:2 * D_GMLP],
                gv[VEC_BSP:VEC_BSP + N_HEADS, 0:BLK], gv[VEC_LN1G:VEC_LN1G + 1], gv[VEC_LN1B:VEC_LN1B + 1],
                gv[VEC_LN2G:VEC_LN2G + 1], gv[VEC_LN2B:VEC_LN2B + 1], gv[VEC_SINK:VEC_SINK + 1, 0:N_HEADS]]

    def body(ins, outs):
        (pw_ref, pv_ref), ins = ins[:2], ins[2:]
        gw, gv = pw_ref[0].astype(F32), pv_ref[0]
        for k in range(1, n):
            gw, gv = gw + pw_ref[k].astype(F32), gv + pv_ref[k]
        for i, g in enumerate(grads(gw, gv)):
            w_ref, m_ref, v_ref = ins[3 * i:3 * i + 3]
            delta, m_new, v_new = _adamw(w_ref[...], g, m_ref[...], v_ref[...])
            for o_ref, val in zip(outs[4 * i:4 * i + 4], (g, delta, m_new, v_new)):
                o_ref[...] = val
        outs[-1][...] = gv[VEC_LOSS:VEC_LOSS + 1, 0:LANES]

    whole = lambda shape, **kw: pl.BlockSpec(shape, lambda i: (0,) * len(shape), **kw)
    once = dict(pipeline_mode=pl.Buffered(1))
    return dict(
        body=body, args=[parts_w, parts_vec, *flat],
        in_specs=[whole(parts_w.shape, **once), whole(parts_vec.shape, **once)] + [whole(a.shape, **once) for a in flat],
        out_specs=[whole(s) for s in shapes for _ in range(4)] + [whole((1, LANES))],
        out_shape=[jax.ShapeDtypeStruct(s, F32) for s in shapes for _ in range(4)] + [jax.ShapeDtypeStruct((1, LANES), F32)])


def _pair_sum(name, parts, recv, core_chip, comms=()):
    _, r, c = parts.shape
    tr = r if r <= 512 else 512

    def body(cc_ref, a_ref, b_ref, wire_ref, own_ref):
        s = a_ref[...] + b_ref[...]
        wire_ref[...] = s.astype(BF16)

        @pl.when(pl.program_id(1) == cc_ref[1])
        def _():
            own_ref[...] = s

    return _carry(
        body, name=name, grid=(r // tr, 4), prefetch=(core_chip,), comms=comms,
        in_specs=[pl.BlockSpec((None, tr, c), lambda i, q, cc: (2 * q + cc[0], i, 0)),
                  pl.BlockSpec((None, tr, c), lambda i, q, cc: (q, i, 0))],
        out_specs=[pl.BlockSpec((None, tr, c), lambda i, q, cc: (q, i, 0)), pl.BlockSpec((tr, c), lambda i, q, cc: (i, 0))],
        out_shape=[jax.ShapeDtypeStruct((4, r, c), BF16), jax.ShapeDtypeStruct((r, c), F32)],
        args=(parts, recv))


def kernel(x, positions, w_in, v_ln_g, v_ln_b, w_spatial, b_spatial, sinks, w_out, ln1_g, ln1_b, w_ff1, w_ff2, ln2_g, ln2_b, loss_target, m_w_in, m_v_ln_g, m_v_ln_b, m_w_spatial, m_b_spatial, m_sinks, m_w_out, m_ln1_g, m_ln1_b, m_w_ff1, m_w_ff2, m_ln2_g, m_ln2_b, v_w_in, v_v_ln_g, v_v_ln_b, v_w_spatial, v_b_spatial, v_sinks, v_w_out, v_ln1_g, v_ln1_b, v_w_ff1, v_w_ff2, v_ln2_g, v_ln2_b):
    _, t_tok, d = x.shape
    xi, yi, ci = _place()
    core_chip = jnp.stack([ci, 2 * xi + yi]).astype(jnp.int32)
    x2 = x.reshape(t_tok, d)
    target = loss_target.reshape(t_tok, d)
    inv_freq = ROPE_THETA ** (-jnp.arange(0, HEAD_DIM, 2, dtype=F32) / HEAD_DIM)
    wsp, bsp, sink_vec = w_spatial[0], b_spatial[0], sinks[0]
    vg_col, vb_col = v_ln_g.reshape(D_GMLP, 1), v_ln_b.reshape(D_GMLP, 1)
    big = {"in": w_in[0], "out": w_out[0], "ff1": w_ff1[0], "ff2": w_ff2[0]}
    half1, half2 = big["ff1"].shape[1] // 2, big["ff2"].shape[0] // 2
    w1_mine = [big["ff1"][:, :half1].astype(BF16), big["ff1"][:, half1:].astype(BF16)]
    w2_mine = [big["ff2"][:half2].astype(BF16), big["ff2"][half2:].astype(BF16)]

    (cos_t, sin_t), ((g_in,),) = _rope_tables(
        positions, jnp.tile(inv_freq, 2).reshape(HEAD_DIM, 1), comms=[_gather_comm([big["in"].T.astype(BF16)])])
    w_in_t = g_in.reshape(D_IN, d)
    (h_t, xb), ((g_out, w1_a),) = _proj_in(x2, w_in_t, comms=[_gather_comm([big["out"].astype(BF16), w1_mine[0]])])
    w_out_b = g_out.reshape(-1, d)
    band_bias = _band_bias()
    (cat_t, lse), ((w1_b, w2_a),) = _mixer_fwd(h_t, cos_t, sin_t, wsp, bsp, vg_col, vb_col, sink_vec, band_bias,
                                                comms=[_gather_comm([w1_mine[1], w2_mine[0]])])
    (xhat1, rstd1, x1b), ((w2_b,),) = _proj_out(cat_t, x2, w_out_b, ln1_g, ln1_b, comms=[_gather_comm([w2_mine[1]])])
    act_b, dpre_b, dz2b, dz1, stats = _ffn_fwd_bwd(xhat1, rstd1, x1b, target, [w1_a, w1_b], [w2_a, w2_b], ln1_g, ln1_b, ln2_g, ln2_b)

    (dcat_t, gw_out), _ = _proj_out_bwd(dz1, cat_t, w_out_b)
    p_out = gw_out.reshape(N_DEV, -1, d)
    r_ff1_a, wire_ff1, own_ff1, ((s_out,),) = _ffn_wgrad(
        "ffn_wgrad1", x1b, dpre_b, False, core_chip, 1, comms=[_sibling_comm([p_out])])
    (wire_out, own_out), _ = _pair_sum("pair_sum_out", p_out, s_out, core_chip)
    _, wire_ff2, own_ff2, ((r_ff1_b,),) = _ffn_wgrad(
        "ffn_wgrad2", act_b, dz2b, True, core_chip, 0, comms=[_flips_comm(wire_ff1, 1)])
    (dh_b, dkvc_t, dkvp_t, g_wsp, g_bsp, g_vln, g_sink), ((r_ff2,), (r_out,)) = _mixer_bwd(
        dcat_t, h_t, cos_t, sin_t, wsp, bsp, vg_col, vb_col, sink_vec, band_bias, lse,
        comms=[_flips_comm(wire_ff2, 0), _chips_comm([wire_out])])
    sink_row = jnp.pad(g_sink.sum(axis=1).reshape(1, N_HEADS), ((0, 0), (0, d - N_HEADS)))
    small_vec = jnp.concatenate([g_vln[0:2].reshape(1, d), stats[0:4], sink_row, stats[4:5], jnp.zeros((1, d), F32),
                                 jnp.pad(g_bsp, ((0, 0), (0, d - BLK)))], axis=0)
    r_in, _, own_in, ((parts_w, parts_vec),) = _proj_in_wgrad(
        dh_b, dkvc_t, dkvp_t, xb, core_chip, comms=[_gather_comm([g_wsp.reshape(-1, BLK), small_vec])])
    (grad_x,), _ = _proj_in_dgrad(dh_b, dkvc_t, dkvp_t, dz1, w_in_t)
    small = [(w_spatial, m_w_spatial, v_w_spatial), (v_ln_g, m_v_ln_g, v_v_ln_g), (v_ln_b, m_v_ln_b, v_v_ln_b),
             (b_spatial, m_b_spatial, v_b_spatial), (ln1_g, m_ln1_g, v_ln1_g), (ln1_b, m_ln1_b, v_ln1_b),
             (ln2_g, m_ln2_g, v_ln2_g), (ln2_b, m_ln2_b, v_ln2_b), (sinks, m_sinks, v_sinks)]
    views = [(-1, BLK), None, None, (N_HEADS, BLK)] + [None] * 5
    small_update = _adamw_small(parts_w, parts_vec, [
        tuple(a if vw is None else a.reshape(vw) for a in p) for p, vw in zip(small, views)])
    (out_out, ff1_out, ff2_out, in_out_t), small_res, _ = _adamw_shards("adamw_all", [
        (own_out, [r_out], big["out"], m_w_out[0], v_w_out[0]),
        (own_ff1, [r_ff1_a, r_ff1_b], big["ff1"], m_w_ff1[0], v_w_ff1[0]),
        (own_ff2, [r_ff2], big["ff2"], m_w_ff2[0], v_w_ff2[0]),
        (own_in, [r_in], big["in"].T, m_w_in[0].T, v_w_in[0].T)], rider=small_update)
    in_out = [o.T for o in in_out_t]
    small_out = [[o.reshape(p[0].shape) for o in small_res[4 * i:4 * i + 4]] for i, p in enumerate(small)]
    loss = small_res[-1][0, 0]

    big_out = {0: in_out, 6: out_out, 9: ff1_out, 10: ff2_out}
    small_slot = {3: 0, 1: 1, 2: 2, 4: 3, 7: 4, 8: 5, 11: 6, 12: 7, 5: 8}
    outs = [loss, grad_x.reshape(x.shape)]
    for kind in range(4):
        for wi in range(13):
            outs.append(big_out[wi][kind][None] if wi in big_out else small_out[small_slot[wi]][kind])
    return tuple(outs)
```

```python
import math

import jax
import jax.numpy as jnp
from jax import lax
from jax.experimental import pallas as pl
from jax.experimental.pallas import tpu as pltpu

F32 = jnp.float32
BF16 = jnp.bfloat16
MESH = pl.DeviceIdType.MESH

HEAD_DIM = 64
N_HEADS = 8
N_KV_HEADS = 2
BLK = 128
D_GMLP = N_HEADS * HEAD_DIM
D_ATTN = N_HEADS * HEAD_DIM
D_KV = N_KV_HEADS * HEAD_DIM
D_IN = 2 * D_GMLP + D_ATTN + 2 * D_KV
COL_U, COL_V, COL_Q, COL_K = 0, D_GMLP, 2 * D_GMLP, 2 * D_GMLP + D_ATTN
ROPE_THETA = 10000.0
LN_EPS = 1e-5
ALPHA = 2.0 ** 0.25
NEG_INF = -1e30
SCORE_SCALE = 1.0 / math.sqrt(HEAD_DIM)
ADAM_LR, ADAM_B1, ADAM_B2, ADAM_EPS, ADAM_WD, ADAM_STEP = 0.001, 0.9, 0.999, 1e-08, 0.01, 10
N_DEV = 8
LANES = 128
VMEM_LIMIT = 56 * 1024 * 1024
FFN_ROWS = 256

NT = (((1,), (1,)), ((), ()))
TN = (((0,), (0,)), ((), ()))


def _params(*sem):
    return pltpu.CompilerParams(dimension_semantics=sem, vmem_limit_bytes=VMEM_LIMIT)


def _dot(a, b, dims=None):
    if dims is None:
        return jnp.dot(a, b, preferred_element_type=F32)
    return lax.dot_general(a, b, dims, preferred_element_type=F32)


def _mean(a):
    return jnp.mean(a, axis=-1, keepdims=True)


def _ln_fwd(z, g, b):
    zc = z - _mean(z)
    rstd = lax.rsqrt(_mean(zc * zc) + LN_EPS)
    xhat = zc * rstd
    return xhat * g + b, xhat, rstd


def _ln_bwd(dy, xhat, rstd, g):
    dxhat = dy * g
    return rstd * (dxhat - _mean(dxhat) - xhat * _mean(dxhat * xhat))


_GELU_C = math.sqrt(2.0 / math.pi)


def _gelu(x):
    t = jnp.tanh(_GELU_C * (x + 0.044715 * (x * x * x)))
    return 0.5 * x * (1.0 + t)


def _gelu_and_grad(x):
    x2 = x * x
    t = jnp.tanh(_GELU_C * (x + 0.044715 * (x2 * x)))
    hx, ht = 0.5 * x, 0.5 * (1.0 + t)
    return x * ht, ht + hx * (1.0 - t * t) * (_GELU_C * (1.0 + 3.0 * 0.044715 * x2))


def _mean0(a):
    return jnp.mean(a, axis=0, keepdims=True)


def _ln_fwd_t(z, g, b):
    zc = z - _mean0(z)
    rstd = lax.rsqrt(_mean0(zc * zc) + LN_EPS)
    xhat = zc * rstd
    return xhat * g + b, xhat, rstd


def _ln_bwd_t(dy, xhat, rstd, g):
    dxhat = dy * g
    return rstd * (dxhat - _mean0(dxhat) - xhat * _mean0(dxhat * xhat))


def _rope_t(t, cos, sin_signed, bwd=False):
    half = HEAD_DIM // 2
    outs = []
    for r in range(0, t.shape[0], HEAD_DIM):
        th = t[r:r + HEAD_DIM]
        sw = jnp.concatenate([th[half:], th[:half]], axis=0) * sin_signed
        outs.append(th * cos - sw if bwd else th * cos + sw)
    return jnp.concatenate(outs, axis=0)


ANY = pl.BlockSpec(memory_space=pl.ANY)
GATHER_PIECES = 2
BF16_ROWS = 16


def _place():
    return lax.axis_index("x"), lax.axis_index("y"), lax.axis_index("c")


class _Comm:
    def __init__(self, ins, outs, sems, start, finish):
        self.ins, self.outs, self.sems, self.start, self.finish = ins, outs, sems, start, finish


def _gather_comm(arrs):
    n = len(arrs)
    pieces = []
    for a, arr in enumerate(arrs):
        k = GATHER_PIECES
        while arr.shape[0] % (k * BF16_ROWS):
            k //= 2
        pieces += [(a, p * (arr.shape[0] // k), arr.shape[0] // k) for p in range(k)]

    def parts(ins, outs, sems):
        send_sems, recv_sems, local_sems = sems
        x, y, c = _place()
        me, sibling = (x, y, c), (x, y, 1 - c)
        chips = [(1 - x, y), (x, 1 - y), (1 - x, 1 - y)]

        def copy(u, k, block, to, local=False):
            a, r0, nr = pieces[u]
            px, py, pc = block
            dst = outs[a].at[4 * px + 2 * py + pc, pl.ds(r0, nr)]
            return pltpu.make_async_remote_copy(
                src_ref=ins[a].at[pl.ds(r0, nr)] if local else dst, dst_ref=dst,
                send_sem=send_sems.at[u, k], recv_sem=recv_sems.at[u, k], device_id=to, device_id_type=MESH)

        mine = [pltpu.make_async_copy(ins[a], outs[a].at[4 * x + 2 * y + c], local_sems.at[a]) for a in range(n)]
        first = []
        for u in range(len(pieces)):
            first.append(copy(u, 0, me, sibling, local=True))
            first += [copy(u, 1 + j, me, (*chip, c), local=True) for j, chip in enumerate(chips)]
        return copy, mine, first, me, sibling, chips, c

    def start(ins, outs, sems):
        _, mine, first, *_ = parts(ins, outs, sems)
        for cp in mine + first:
            cp.start()

    def finish(ins, outs, sems):
        copy, mine, first, me, sibling, chips, c = parts(ins, outs, sems)
        passed = []
        for u in range(len(pieces)):
            for j, chip in enumerate(chips):
                copy(u, 1 + j, (*chip, c), me).wait_recv()
                fwd = copy(u, 4 + j, (*chip, c), sibling)
                fwd.start()
                passed.append(fwd)
        for u in range(len(pieces)):
            copy(u, 0, sibling, me).wait_recv()
            for j, chip in enumerate(chips):
                copy(u, 4 + j, (*chip, 1 - c), me).wait_recv()
        for cp in first + passed:
            cp.wait_send()
        for cp in mine:
            cp.wait()

    return _Comm(list(arrs), [jax.ShapeDtypeStruct((N_DEV,) + a.shape, a.dtype) for a in arrs],
                 [pltpu.SemaphoreType.DMA((len(pieces), 7)), pltpu.SemaphoreType.DMA((len(pieces), 7)),
                  pltpu.SemaphoreType.DMA((n,))], start, finish)


def _sibling_comm(parts):
    n = len(parts)

    def copies(ins, outs, sems):
        x, y, c = _place()
        return [pltpu.make_async_remote_copy(
            src_ref=ins[a].at[2 * q + (1 - c)], dst_ref=outs[a].at[q],
            send_sem=sems[0].at[a, q], recv_sem=sems[1].at[a, q],
            device_id=(x, y, 1 - c), device_id_type=MESH) for a in range(n) for q in range(4)]

    return _Comm(list(parts), [jax.ShapeDtypeStruct((4,) + p.shape[1:], p.dtype) for p in parts],
                 [pltpu.SemaphoreType.DMA((n, 4)), pltpu.SemaphoreType.DMA((n, 4))],
                 lambda *r: [cp.start() for cp in copies(*r)], lambda *r: [cp.wait() for cp in copies(*r)])


def _chips_comm(chip_parts, rows=None):
    n = len(chip_parts)
    r0, nr = (0, None) if rows is None else rows

    def copies(ins, outs, sems):
        x, y, c = _place()
        chips = [(1 - x, y), (x, 1 - y), (1 - x, 1 - y)]
        src = lambda a, q: ins[a].at[q] if rows is None else ins[a].at[q, pl.ds(r0, nr)]
        return [pltpu.make_async_remote_copy(
            src_ref=src(a, 2 * px + py), dst_ref=outs[a].at[k],
            send_sem=sems[0].at[a, k], recv_sem=sems[1].at[a, k],
            device_id=(px, py, c), device_id_type=MESH) for a in range(n) for k, (px, py) in enumerate(chips)]

    shape = lambda p: (3,) + p.shape[1:] if rows is None else (3, nr) + p.shape[2:]
    return _Comm(list(chip_parts), [jax.ShapeDtypeStruct(shape(p), p.dtype) for p in chip_parts],
                 [pltpu.SemaphoreType.DMA((n, 3)), pltpu.SemaphoreType.DMA((n, 3))],
                 lambda *r: [cp.start() for cp in copies(*r)], lambda *r: [cp.wait() for cp in copies(*r)])


def _flips_comm(sums, first):
    m = sums.shape[0]

    def copies(ins, outs, sems):
        return [pltpu.make_async_remote_copy(
            src_ref=ins[0].at[j], dst_ref=outs[0].at[j], send_sem=sems[0].at[j], recv_sem=sems[1].at[j],
            device_id=_flipped(first + j), device_id_type=MESH) for j in range(m)]

    return _Comm([sums], [jax.ShapeDtypeStruct(sums.shape, sums.dtype)],
                 [pltpu.SemaphoreType.DMA((m,)), pltpu.SemaphoreType.DMA((m,))],
                 lambda *r: [cp.start() for cp in copies(*r)], lambda *r: [cp.wait() for cp in copies(*r)])


def _carry(body, *, name, grid, in_specs, out_specs, out_shape, args, comms=(), scratch_shapes=(), prefetch=()):
    n_pre, n_in, n_out, n_scr = len(prefetch), len(in_specs), len(out_specs), len(scratch_shapes)
    c_ins = [a for cm in comms for a in cm.ins]
    c_outs = [s for cm in comms for s in cm.outs]
    c_sems = [s for cm in comms for s in cm.sems]

    def wrapped(*refs):
        pre, refs = refs[:n_pre], refs[n_pre:]
        ins, refs = refs[:n_in], refs[n_in:]
        cins, refs = refs[:len(c_ins)], refs[len(c_ins):]
        outs, refs = refs[:n_out], refs[n_out:]
        couts, refs = refs[:len(c_outs)], refs[len(c_outs):]
        scr, sems = refs[:n_scr], refs[n_scr:]
        groups, i0, o0, s0 = [], 0, 0, 0
        for cm in comms:
            groups.append((cm, cins[i0:i0 + len(cm.ins)], couts[o0:o0 + len(cm.outs)], sems[s0:s0 + len(cm.sems)]))
            i0, o0, s0 = i0 + len(cm.ins), o0 + len(cm.outs), s0 + len(cm.sems)
        first = pl.program_id(0) == 0
        last = pl.program_id(0) == grid[0] - 1
        for ax in range(1, len(grid)):
            first = first & (pl.program_id(ax) == 0)
            last = last & (pl.program_id(ax) == grid[ax] - 1)
        if comms:
            @pl.when(first)
            def _():
                for cm, ci, co, cs in groups:
                    cm.start(ci, co, cs)
        body(*pre, *ins, *outs, *scr)
        if comms:
            @pl.when(last)
            def _():
                for cm, ci, co, cs in groups:
                    cm.finish(ci, co, cs)

    grid_spec = pltpu.PrefetchScalarGridSpec(
        num_scalar_prefetch=n_pre, grid=grid,
        in_specs=list(in_specs) + [ANY] * len(c_ins), out_specs=list(out_specs) + [ANY] * len(c_outs),
        scratch_shapes=list(scratch_shapes) + c_sems)
    res = pl.pallas_call(
        wrapped, name=name, grid_spec=grid_spec, out_shape=list(out_shape) + c_outs,
        compiler_params=_params(*(["arbitrary"] * len(grid))),
    )(*prefetch, *args, *c_ins)
    outs, rest, per_comm = res[:n_out], res[n_out:], []
    for cm in comms:
        per_comm.append(rest[:len(cm.outs)])
        rest = rest[len(cm.outs):]
    return outs, per_comm


def _rope_tables(pos_row, inv_freq_col, comms=()):
    t_tok = pos_row.shape[1]
    tm = min(512, t_tok)

    def body(pos_ref, invf_ref, cos_ref, sin_ref):
        ang = pos_ref[...].astype(F32) * invf_ref[...]
        row = lax.broadcasted_iota(jnp.int32, ang.shape, 0)
        cos_ref[...] = jnp.cos(ang)
        sin_ref[...] = jnp.sin(ang) * jnp.where(row < HEAD_DIM // 2, -1.0, 1.0)

    return _carry(
        body, name="rope_tables", grid=(t_tok // tm,), comms=comms,
        in_specs=[pl.BlockSpec((1, tm), lambda i: (0, i)), pl.BlockSpec((HEAD_DIM, 1), lambda i: (0, 0))],
        out_specs=[pl.BlockSpec((HEAD_DIM, tm), lambda i: (0, i))] * 2,
        out_shape=[jax.ShapeDtypeStruct((HEAD_DIM, t_tok), F32)] * 2,
        args=(pos_row, inv_freq_col))


def _proj_in(x2, w_in_t, comms=()):
    t_tok, d = x2.shape
    d_in = w_in_t.shape[0]
    tm = min(512, t_tok)

    def body(x_ref, w_ref, h_ref, xb_ref):
        xb = x_ref[...].astype(BF16)
        xb_ref[...] = xb
        h_ref[...] = _dot(w_ref[...], xb, NT)

    return _carry(
        body, name="proj_in", grid=(t_tok // tm,), comms=comms,
        in_specs=[pl.BlockSpec((tm, d), lambda i: (i, 0)), pl.BlockSpec((d_in, d), lambda i: (0, 0))],
        out_specs=[pl.BlockSpec((d_in, tm), lambda i: (0, i)), pl.BlockSpec((tm, d), lambda i: (i, 0))],
        out_shape=[jax.ShapeDtypeStruct((d_in, t_tok), F32), jax.ShapeDtypeStruct((t_tok, d), BF16)],
        args=(x2, w_in_t))


MIX_BLOCKS = 2
MIX_W = MIX_BLOCKS * BLK


def _prev_block(i):
    return jnp.maximum(MIX_BLOCKS * i - 1, 0)


def _h_specs():
    kv_row = COL_K // (2 * D_KV)
    return [
        pl.BlockSpec((D_GMLP, MIX_W), lambda i: (0, i)),
        pl.BlockSpec((D_GMLP, MIX_W), lambda i: (1, i)),
        pl.BlockSpec((D_ATTN, MIX_W), lambda i: (2, i)),
        pl.BlockSpec((2 * D_KV, MIX_W), lambda i: (kv_row, i)),
        pl.BlockSpec((2 * D_KV, BLK), lambda i: (kv_row, _prev_block(i))),
    ]


def _table_specs():
    return [
        pl.BlockSpec((HEAD_DIM, MIX_W), lambda i: (0, i)),
        pl.BlockSpec((HEAD_DIM, MIX_W), lambda i: (0, i)),
        pl.BlockSpec((HEAD_DIM, BLK), lambda i: (0, _prev_block(i))),
        pl.BlockSpec((HEAD_DIM, BLK), lambda i: (0, _prev_block(i))),
    ]


def _cols(b):
    return slice(b * BLK, (b + 1) * BLK)


LSE_ROWS = 8
LSE_SPEC = pl.BlockSpec((LSE_ROWS, D_ATTN), lambda i: (i, 0))


def _block_inputs(b, i, kvc, kvp_ref, cos, sin, cosp_ref, sinp_ref, bias_ref):
    if b == 0:
        kv_prev, cos_prev, sin_prev, bias = kvp_ref[...], cosp_ref[...], sinp_ref[...], bias_ref[jnp.minimum(i, 1)]
    else:
        kv_prev, cos_prev, sin_prev, bias = kvc[:, _cols(b - 1)], cos[:, _cols(b - 1)], sin[:, _cols(b - 1)], bias_ref[1]
    return kvc[:, _cols(b)], kv_prev, cos[:, _cols(b)], sin[:, _cols(b)], cos_prev, sin_prev, bias


def _band_bias():
    ki = lax.broadcasted_iota(jnp.int32, (2, 2 * BLK, BLK), 1)
    qi = lax.broadcasted_iota(jnp.int32, (2, 2 * BLK, BLK), 2)
    later = lax.broadcasted_iota(jnp.int32, (2, 2 * BLK, BLK), 0) > 0
    dist = qi + BLK - ki
    return jnp.where((dist >= 0) & (dist < BLK) & ((ki >= BLK) | later), 0.0, NEG_INF).astype(F32)


BIAS_SPEC = pl.BlockSpec((2, 2 * BLK, BLK), lambda i: (0, 0, 0))


def _keys_values(kvc, kvp, cosc, sinc, cosp, sinp):
    kp, kc = _rope_t(kvp[:D_KV], cosp, sinp), _rope_t(kvc[:D_KV], cosc, sinc)
    k_t = jnp.concatenate([kp, kc], axis=1).astype(BF16)
    k_n = jnp.concatenate([kp.T, kc.T], axis=0).astype(BF16)
    v_t = jnp.concatenate([kvp[D_KV:], kvc[D_KV:]], axis=1).astype(BF16)
    return k_t, k_n, v_t


def _pad_head(th, kv):
    z = jnp.zeros_like(th)
    return jnp.concatenate([th, z] if kv == 0 else [z, th], axis=0)


def _group_lanes(parts):
    return jnp.concatenate(parts, axis=1)


def _softmax_sink_t(s, sink):
    m = jnp.maximum(jnp.max(s, axis=0, keepdims=True), sink)
    e = jnp.exp(s - m)
    denom = jnp.sum(e, axis=0, keepdims=True) + jnp.exp(sink - m)
    return e * (1.0 / denom), m + jnp.log(denom)


def _causal():
    row = lax.broadcasted_iota(jnp.int32, (BLK, BLK), 0)
    col = lax.broadcasted_iota(jnp.int32, (BLK, BLK), 1)
    return row >= col


def _mask_w_once(wsp_ref, wm_scr):
    @pl.when(pl.program_id(0) == 0)
    def _():
        causal = _causal()
        for hh in range(N_HEADS):
            wm_scr[hh] = jnp.where(causal, wsp_ref[hh], 0.0).astype(BF16)


def _mixer_fwd(h_t, cos_t, sin_t, w_spatial, b_spatial, vln_g, vln_b, sinks, band_bias, comms=()):
    t_tok = h_t.shape[1]
    group = N_HEADS // N_KV_HEADS

    def body(sinks_ref, u_ref, vg_ref, q_ref, kvc_ref, kvp_ref, cos_ref, sin_ref, cosp_ref, sinp_ref,
             wsp_ref, bsp_ref, g_ref, b_ref, bias_ref, cat_ref, lse_ref, wm_scr):
        i = pl.program_id(0)
        _mask_w_once(wsp_ref, wm_scr)
        lse_ref[...] = jnp.zeros_like(lse_ref)
        ua = _gelu(u_ref[...])
        vp, _, _ = _ln_fwd_t(_gelu(vg_ref[...]), g_ref[...], b_ref[...])
        vpb = vp.astype(BF16)
        for b in range(MIX_BLOCKS):
            for hh in range(N_HEADS):
                rows = slice(hh * HEAD_DIM, (hh + 1) * HEAD_DIM)
                mixed = _dot(vpb[rows, _cols(b)], wm_scr[hh], NT) + bsp_ref[hh:hh + 1, :]
                cat_ref[rows, _cols(b)] = (ua[rows, _cols(b)] * mixed).astype(BF16)

        kvc, cos, sin = kvc_ref[...], cos_ref[...], sin_ref[...]
        qr = (_rope_t(q_ref[...], cos, sin) * SCORE_SCALE).astype(BF16)
        sinks4 = [_group_lanes([jnp.full((1, BLK), sinks_ref[hh], F32) for hh in range(kv * group, (kv + 1) * group)])
                  for kv in range(N_KV_HEADS)]
        for b in range(MIX_BLOCKS):
            kv_cur, kv_prev, cosc, sinc, cosp, sinp, bias1 = _block_inputs(b, i, kvc, kvp_ref, cos, sin, cosp_ref, sinp_ref, bias_ref)
            _, k_n, v_t = _keys_values(kv_cur, kv_prev, cosc, sinc, cosp, sinp)
            bias = _group_lanes([bias1] * group)
            for kv in range(N_KV_HEADS):
                heads = range(kv * group, (kv + 1) * group)
                qs = _group_lanes([qr[hh * HEAD_DIM:(hh + 1) * HEAD_DIM, _cols(b)] for hh in heads])
                p, lse = _softmax_sink_t(_dot(k_n, _pad_head(qs, kv)) + bias, sinks4[kv])
                lse_ref[b * N_KV_HEADS + kv:b * N_KV_HEADS + kv + 1, :] = lse
                o = _dot(v_t[kv * HEAD_DIM:(kv + 1) * HEAD_DIM], p.astype(BF16)).astype(BF16)
                for j, hh in enumerate(heads):
                    cat_ref[D_GMLP + hh * HEAD_DIM:D_GMLP + (hh + 1) * HEAD_DIM, _cols(b)] = o[:, j * BLK:(j + 1) * BLK]

    full = lambda shape: pl.BlockSpec(shape, lambda i: (0,) * len(shape))
    return _carry(
        body, name="mixer_fwd", grid=(t_tok // MIX_W,), comms=comms,
        in_specs=[pl.BlockSpec(memory_space=pltpu.SMEM)] + _h_specs() + _table_specs() + [
            full((N_HEADS, BLK, BLK)), full((N_HEADS, BLK)), full((D_GMLP, 1)), full((D_GMLP, 1)), BIAS_SPEC],
        out_specs=[pl.BlockSpec((D_GMLP + D_ATTN, MIX_W), lambda i: (0, i)), LSE_SPEC],
        out_shape=[jax.ShapeDtypeStruct((D_GMLP + D_ATTN, t_tok), BF16),
                   jax.ShapeDtypeStruct((t_tok // MIX_W * LSE_ROWS, D_ATTN), F32)],
        scratch_shapes=[pltpu.VMEM((N_HEADS, BLK, BLK), BF16)],
        args=(sinks, h_t, h_t, h_t, h_t, h_t, cos_t, sin_t, cos_t, sin_t, w_spatial, b_spatial, vln_g, vln_b, band_bias))


def _proj_out(cat_t, x2, w_out_b, ln1_g, ln1_b, comms=()):
    t_tok, d = x2.shape
    tm = min(512, t_tok)

    def body(cat_ref, x_ref, w_ref, g_ref, b_ref, xhat_ref, rstd_ref, x1b_ref):
        x1, xhat, rstd = _ln_fwd(ALPHA * x_ref[...] + _dot(cat_ref[...], w_ref[...], TN), g_ref[...], b_ref[...])
        xhat_ref[...] = xhat
        rstd_ref[...] = rstd
        x1b_ref[...] = x1.astype(BF16)

    tok = lambda w: pl.BlockSpec((tm, w), lambda i: (i, 0))
    vec = pl.BlockSpec((1, d), lambda i: (0, 0))
    return _carry(
        body, name="proj_out", grid=(t_tok // tm,), comms=comms,
        in_specs=[pl.BlockSpec((cat_t.shape[0], tm), lambda i: (0, i)), tok(d), pl.BlockSpec(w_out_b.shape, lambda i: (0, 0)), vec, vec],
        out_specs=[tok(d), tok(1), tok(d)],
        out_shape=[jax.ShapeDtypeStruct((t_tok, d), F32), jax.ShapeDtypeStruct((t_tok, 1), F32), jax.ShapeDtypeStruct((t_tok, d), BF16)],
        args=(cat_t, x2, w_out_b, ln1_g, ln1_b))


def _ffn_fwd_bwd(xhat1, rstd1, x1b, target, w1_parts, w2_parts, ln1_g, ln1_b, ln2_g, ln2_b):
    t_tok, d = xhat1.shape
    n_part = len(w1_parts)
    n_chunk, _, fp = w1_parts[0].shape
    f = n_chunk * n_part * fp
    tm = min(FFN_ROWS, t_tok)

    def body(xhat1_ref, rstd1_ref, x1b_ref, tgt_ref, *refs):
        w1_hbm, w2_hbm = refs[:n_part], refs[n_part:2 * n_part]
        (g1_ref, b1_ref, g2_ref, b2_ref, act_ref, dpre_ref, dz2b_ref, dz1_ref, stats_ref,
         r_scr, w1_ref, w2_ref, w_sems) = refs[2 * n_part:]

        @pl.when(pl.program_id(0) == 0)
        def _():
            stats_ref[...] = jnp.zeros_like(stats_ref)
            loads = []
            for j in range(n_chunk):
                for p in range(n_part):
                    units = pl.ds((j * n_part + p) * fp, fp)
                    loads.append(pltpu.make_async_copy(w1_hbm[p].at[j], w1_ref.at[:, units], w_sems.at[0, p, j]))
                    loads.append(pltpu.make_async_copy(w2_hbm[p].at[j], w2_ref.at[units, :], w_sems.at[1, p, j]))
            for cp in loads:
                cp.start()
            for cp in loads:
                cp.wait()

        g1, g2 = g1_ref[...], g2_ref[...]
        xhat1 = xhat1_ref[...]
        r_scr[...] = jnp.maximum(_dot(x1b_ref[...], w1_ref[...]), 0.0)
        r = r_scr[...]
        act = (r * r).astype(BF16)
        act_ref[...] = act
        ff = _dot(act, w2_ref[...])
        y, xhat2, rstd2 = _ln_fwd(ALPHA * (xhat1 * g1 + b1_ref[...]) + ff, g2, b2_ref[...])
        diff = y - tgt_ref[...]
        loss = 0.5 * jnp.sum(jnp.sum(diff * diff, axis=-1, keepdims=True) / d, axis=0, keepdims=True)
        dy = diff / d
        dz2 = _ln_bwd(dy, xhat2, rstd2, g2)
        dz2b = dz2.astype(BF16)
        dz2b_ref[...] = dz2b
        dpre = (_dot(dz2b, w2_ref[...], NT) * (2.0 * r_scr[...])).astype(BF16)
        dpre_ref[...] = dpre
        dx1 = ALPHA * dz2 + _dot(dpre, w1_ref[...], NT)
        dz1_ref[...] = _ln_bwd(dx1, xhat1, rstd1_ref[...], g1)
        stats_ref[0:1, :] += jnp.sum(dx1 * xhat1, axis=0, keepdims=True)
        stats_ref[1:2, :] += jnp.sum(dx1, axis=0, keepdims=True)
        stats_ref[2:3, :] += jnp.sum(dy * xhat2, axis=0, keepdims=True)
        stats_ref[3:4, :] += jnp.sum(dy, axis=0, keepdims=True)
        stats_ref[4:5, :] += jnp.broadcast_to(loss, (1, d))

    tok = lambda w: pl.BlockSpec((tm, w), lambda i: (i, 0))
    vec = pl.BlockSpec((1, d), lambda i: (0, 0))
    return _carry(
        body, name="ffn_fwd_bwd", grid=(t_tok // tm,),
        in_specs=[tok(d), tok(1), tok(d), tok(d)] + [ANY] * (2 * n_part) + [vec, vec, vec, vec],
        out_specs=[tok(f), tok(f), tok(d), tok(d), pl.BlockSpec((8, d), lambda i: (0, 0))],
        out_shape=[jax.ShapeDtypeStruct((t_tok, f), BF16), jax.ShapeDtypeStruct((t_tok, f), BF16),
                   jax.ShapeDtypeStruct((t_tok, d), BF16), jax.ShapeDtypeStruct((t_tok, d), F32), jax.ShapeDtypeStruct((8, d), F32)],
        scratch_shapes=[pltpu.VMEM((tm, f), F32), pltpu.VMEM((d, f), BF16), pltpu.VMEM((f, d), BF16),
                        pltpu.SemaphoreType.DMA((2, n_part, n_chunk))],
        args=(xhat1, rstd1, x1b, target, *w1_parts, *w2_parts, ln1_g, ln1_b, ln2_g, ln2_b))[0]


WGRAD_STEPS = [(True, 0), (True, 1), (False, 0), (True, 2), (False, 1), (True, 3), (False, 2), (False, 3)]
CHIP_FLIPS = [3, 1, 2, 0]


def _pick(table, s):
    out = table[-1]
    for i in range(len(table) - 2, -1, -1):
        out = jnp.where(s == i, table[i], out)
    return out


def _wgrad_shard(s, cc):
    q = jnp.bitwise_xor(cc[1], _pick([CHIP_FLIPS[k] for _, k in WGRAD_STEPS], s))
    return 2 * q + jnp.where(_pick([int(sibling) for sibling, _ in WGRAD_STEPS], s) == 1, 1 - cc[0], cc[0])


def _flipped(k):
    x, y, c = _place()
    return (1 - x if CHIP_FLIPS[k] // 2 else x, 1 - y if CHIP_FLIPS[k] % 2 else y, c)


def _wgrad_pair_sum(name, product, chunk, in_specs, args, core_chip, n_sent, comms=(), scratch_shapes=(), after=None):
    half = N_DEV // 2
    n_in, n_out = len(in_specs), 2 + (0 < n_sent) + (n_sent < half - 1)
    n_ain, n_aout, n_steps = (len(after["in_specs"]), len(after["out_specs"]), N_DEV + after["steps"]) if after else (0, 0, N_DEV)

    def body(cc_ref, *refs):
        ins, after_ins, refs = refs[:n_in], refs[n_in:n_in + n_ain], refs[n_in + n_ain:]
        outs, after_outs, scr = refs[:n_out], refs[n_out:n_out + n_aout], refs[n_out + n_aout:]
        extra = scr[8:-1] if after else scr[8:]
        (own_ref, recv_ref), from_chips_ref, wire_ref = outs[-2:], outs[0], outs[n_out - 3]
        send_buf, got, send_sems, recv_sems, got_sem, wire_buf, leave_sems, arrive_sems = scr[:8]
        s = pl.program_id(0)
        x, y, c = _place()
        def send(q):
            return pltpu.make_async_remote_copy(
                src_ref=send_buf.at[q % 2], dst_ref=recv_ref.at[q], send_sem=send_sems.at[q], recv_sem=recv_sems.at[q],
                device_id=(x, y, 1 - c), device_id_type=MESH)

        def load(q):
            return pltpu.make_async_copy(recv_ref.at[q], got, got_sem.at[0])

        def leave(k):
            if k < n_sent:
                return pltpu.make_async_remote_copy(
                    src_ref=wire_buf.at[k], dst_ref=from_chips_ref.at[k], send_sem=leave_sems.at[k],
                    recv_sem=arrive_sems.at[k], device_id=_flipped(k), device_id_type=MESH)
            return pltpu.make_async_copy(wire_buf.at[k], wire_ref.at[k - n_sent], leave_sems.at[k])

        for step, (sibling, q) in enumerate(WGRAD_STEPS):
            if not sibling:
                @pl.when(s == step)
                def _(q=q):
                    send(q).wait_recv()
                    load(q).start()

        if after:
            @pl.when(s < N_DEV)
            def _():
                scr[-1][...] = product(_wgrad_shard(s, cc_ref), *ins, *extra)

            @pl.when(s >= N_DEV)
            def _():
                after["body"](s - N_DEV, *after_ins, *after_outs, *extra)
            chunk_now = lambda: scr[-1][...]
        else:
            g = product(_wgrad_shard(s, cc_ref), *ins, *extra)
            chunk_now = lambda: g

        for step, (sibling, q) in enumerate(WGRAD_STEPS):
            @pl.when(s == step)
            def _(sibling=sibling, q=q):
                if sibling:
                    if q >= 2:
                        send(q - 2).wait_send()
                    send_buf[q % 2] = chunk_now()
                    send(q).start()
                    return
                load(q).wait()
                total = chunk_now() + got[...]
                if q < half - 1:
                    wire_buf[q] = total.astype(BF16)
                    leave(q).start()
                else:
                    own_ref[...] = total

        @pl.when(s == n_steps - 1)
        def _():
            for q in range(half - 2, half):
                send(q).wait_send()
            for k in range(half - 1):
                leave(k).wait()

    sums = lambda n: [jax.ShapeDtypeStruct((n,) + chunk, BF16)] if n else []
    sem = lambda n: pltpu.SemaphoreType.DMA((n,))
    more = after or dict(in_specs=[], out_specs=[], out_shape=[], args=())
    res, per_comm = _carry(
        body, name=name, grid=(n_steps,), comms=comms, prefetch=(core_chip,), in_specs=list(in_specs) + more["in_specs"],
        out_specs=[ANY] * (n_out - 2) + [pl.BlockSpec(chunk, lambda s, cc: (0, 0)), ANY] + more["out_specs"],
        out_shape=sums(n_sent) + sums(half - 1 - n_sent) + [jax.ShapeDtypeStruct(chunk, F32),
                                                            jax.ShapeDtypeStruct((half,) + chunk, F32)] + more["out_shape"],
        scratch_shapes=[pltpu.VMEM((2,) + chunk, F32), pltpu.VMEM(chunk, F32), sem(half), sem(half), sem(1),
                        pltpu.VMEM((half - 1,) + chunk, BF16), sem(half - 1), sem(half - 1), *scratch_shapes,
                        *([pltpu.VMEM(chunk, F32)] if after else [])],
        args=tuple(args) + tuple(more["args"]))
    own = res[n_out - 2]
    first = res[0] if n_sent else None, res[n_out - 3] if n_sent < half - 1 else None
    return (*first, own, per_comm, res[n_out:]) if after else (*first, own, per_comm)


def _resident(a):
    return pl.BlockSpec(a.shape, lambda s, cc: (0,) * a.ndim, pipeline_mode=pl.Buffered(1))


def _ffn_wgrad(name, lhs, rhs, chunk_lhs, core_chip, n_sent, comms=()):
    t_tok = lhs.shape[0]
    fc = (lhs if chunk_lhs else rhs).shape[1] // N_DEV
    chunked = pl.BlockSpec((t_tok, fc), lambda s, cc: (0, _wgrad_shard(s, cc)))

    def product(shard, lhs_ref, rhs_ref):
        return _dot(lhs_ref[...], rhs_ref[...], TN)

    return _wgrad_pair_sum(
        name, product, (fc, rhs.shape[1]) if chunk_lhs else (lhs.shape[1], fc),
        [chunked, _resident(rhs)] if chunk_lhs else [_resident(lhs), chunked], (lhs, rhs), core_chip, n_sent, comms)


def _proj_out_bwd(dz1, cat_t, w_out_b, comms=()):
    t_tok, d = dz1.shape
    d_mix = cat_t.shape[0]
    tm = min(512, t_tok)

    def body(dz1_ref, cat_ref, w_ref, dcat_ref, gw_ref):
        @pl.when(pl.program_id(0) == 0)
        def _():
            gw_ref[...] = jnp.zeros_like(gw_ref)

        dzb = dz1_ref[...].astype(BF16)
        dcat_ref[...] = _dot(w_ref[...], dzb, NT)
        gw_ref[...] += _dot(cat_ref[...], dzb)

    return _carry(
        body, name="proj_out_bwd", grid=(t_tok // tm,), comms=comms,
        in_specs=[pl.BlockSpec((tm, d), lambda i: (i, 0)), pl.BlockSpec((d_mix, tm), lambda i: (0, i)),
                  pl.BlockSpec((d_mix, d), lambda i: (0, 0))],
        out_specs=[pl.BlockSpec((d_mix, tm), lambda i: (0, i)), pl.BlockSpec((d_mix, d), lambda i: (0, 0))],
        out_shape=[jax.ShapeDtypeStruct((d_mix, t_tok), F32), jax.ShapeDtypeStruct((d_mix, d), F32)],
        args=(dz1, cat_t, w_out_b))


def _mixer_bwd(dcat_t, h_t, cos_t, sin_t, w_spatial, b_spatial, vln_g, vln_b, sinks, band_bias, lse, comms=()):
    t_tok = h_t.shape[1]
    nb, n_step = t_tok // BLK, t_tok // MIX_W
    group = N_HEADS // N_KV_HEADS

    def body(sinks_ref, dcat_ref, u_ref, vg_ref, q_ref, kvc_ref, kvp_ref, cos_ref, sin_ref, cosp_ref, sinp_ref,
             wsp_ref, bsp_ref, g_ref, b_ref, bias_ref, lse_ref, dh_ref, dkvc_ref, dkvp_ref, gwsb_ref, gbsp_ref, gvln_ref, gsink_ref,
             dg_acc, db_acc, wm_scr, gws_ref):
        i = pl.program_id(0)

        @pl.when(i == 0)
        def _():
            gws_ref[...] = jnp.zeros_like(gws_ref)
            gbsp_ref[...] = jnp.zeros_like(gbsp_ref)
            gsink_ref[...] = jnp.zeros_like(gsink_ref)
            dg_acc[...] = jnp.zeros_like(dg_acc)
            db_acc[...] = jnp.zeros_like(db_acc)

        _mask_w_once(wsp_ref, wm_scr)

        g = g_ref[...]
        ua, ua_grad = _gelu_and_grad(u_ref[...])
        vv, vv_grad = _gelu_and_grad(vg_ref[...])
        vp, vhat, rstd = _ln_fwd_t(vv, g, b_ref[...])
        vpb = vp.astype(BF16)
        da = dcat_ref[0:D_GMLP, :]
        dmixed = da * ua
        dvp_blocks = []
        for b in range(MIX_BLOCKS):
            dvp_parts = []
            for hh in range(N_HEADS):
                rows = slice(hh * HEAD_DIM, (hh + 1) * HEAD_DIM)
                vpb_h = vpb[rows, _cols(b)]
                mixed = _dot(vpb_h, wm_scr[hh], NT) + bsp_ref[hh:hh + 1, :]
                dh_ref[COL_U + hh * HEAD_DIM:COL_U + (hh + 1) * HEAD_DIM, _cols(b)] = (
                    da[rows, _cols(b)] * mixed * ua_grad[rows, _cols(b)]).astype(BF16)
                dm = dmixed[rows, _cols(b)]
                dmb = dm.astype(BF16)
                gbsp_ref[hh:hh + 1, :] += jnp.sum(dm, axis=0, keepdims=True)
                gws_ref[hh] += _dot(dmb, vpb_h, TN)
                dvp_parts.append(_dot(dmb, wm_scr[hh]))
            dvp_blocks.append(jnp.concatenate(dvp_parts, axis=0))
        dvp = jnp.concatenate(dvp_blocks, axis=1)
        dgv, dbv = dvp * vhat, dvp
        for b in range(MIX_BLOCKS):
            dg_acc[...] += dgv[:, _cols(b)]
            db_acc[...] += dbv[:, _cols(b)]
        dh_ref[COL_V:COL_V + D_GMLP, :] = (_ln_bwd_t(dvp, vhat, rstd, g) * vv_grad).astype(BF16)

        kvc, cos, sin = kvc_ref[...], cos_ref[...], sin_ref[...]
        qr = (_rope_t(q_ref[...], cos, sin) * SCORE_SCALE).astype(BF16)
        sinks4 = [_group_lanes([jnp.full((1, BLK), sinks_ref[hh], F32) for hh in range(kv * group, (kv + 1) * group)])
                  for kv in range(N_KV_HEADS)]
        dq_blocks, dkv_cur, dkv_prev = [], [], []
        for b in range(MIX_BLOCKS):
            kv_cur, kv_prev, cosc, sinc, cosp, sinp, bias1 = _block_inputs(b, i, kvc, kvp_ref, cos, sin, cosp_ref, sinp_ref, bias_ref)
            k_t, k_n, v_t = _keys_values(kv_cur, kv_prev, cosc, sinc, cosp, sinp)
            v_n = jnp.concatenate([kv_prev[D_KV:].T, kv_cur[D_KV:].T], axis=0).astype(BF16)
            bias = _group_lanes([bias1] * group)
            dk, dv, dq_parts = [], [], []
            for kv in range(N_KV_HEADS):
                heads = range(kv * group, (kv + 1) * group)
                kv_rows = slice(kv * HEAD_DIM, (kv + 1) * HEAD_DIM)
                qs = _group_lanes([qr[hh * HEAD_DIM:(hh + 1) * HEAD_DIM, _cols(b)] for hh in heads])
                dos = _group_lanes([dcat_ref[D_GMLP + hh * HEAD_DIM:D_GMLP + (hh + 1) * HEAD_DIM, _cols(b)]
                                    for hh in heads]).astype(BF16)
                lse_g = lse_ref[b * N_KV_HEADS + kv:b * N_KV_HEADS + kv + 1, :]
                p = jnp.exp(_dot(k_n, _pad_head(qs, kv)) + bias - lse_g)
                p_sink = jnp.exp(sinks4[kv] - lse_g)
                dp = _dot(v_n, _pad_head(dos, kv))
                delta = jnp.sum(p * dp, axis=0, keepdims=True)
                ds = (p * (dp - delta)).astype(BF16)
                dsink = p_sink * delta
                dq = _dot(k_t[kv_rows], ds) * SCORE_SCALE
                for j, hh in enumerate(heads):
                    gsink_ref[hh:hh + 1, :] -= dsink[:, j * BLK:(j + 1) * BLK]
                    dq_parts.append(dq[:, j * BLK:(j + 1) * BLK])
                dk.append(_dot(qs, ds, NT))
                dv.append(_dot(dos, p.astype(BF16), NT))
            dq_blocks.append(jnp.concatenate(dq_parts, axis=0))
            dk_all, dv_all = jnp.concatenate(dk, axis=0), jnp.concatenate(dv, axis=0)
            dkv_cur.append(jnp.concatenate([_rope_t(dk_all[:, BLK:], cosc, sinc, bwd=True), dv_all[:, BLK:]], axis=0))
            dkv_prev.append(jnp.concatenate([_rope_t(dk_all[:, :BLK], cosp, sinp, bwd=True), dv_all[:, :BLK]], axis=0))
        dh_ref[COL_Q:COL_Q + D_ATTN, :] = _rope_t(jnp.concatenate(dq_blocks, axis=1), cos, sin, bwd=True).astype(BF16)
        for b in range(MIX_BLOCKS):
            dkvc_ref[:, _cols(b)] = dkv_cur[b] + dkv_prev[b + 1] if b + 1 < MIX_BLOCKS else dkv_cur[b]
        dkvp_ref[...] = dkv_prev[0]

        @pl.when(i == n_step - 1)
        def _():
            causal = _causal()
            for hh in range(N_HEADS):
                gwsb_ref[hh] = jnp.where(causal, gws_ref[hh], 0.0).astype(BF16)
            gvln_ref[...] = jnp.zeros_like(gvln_ref)
            gvln_ref[0:1, :] = jnp.sum(dg_acc[...].T, axis=0, keepdims=True)
            gvln_ref[1:2, :] = jnp.sum(db_acc[...].T, axis=0, keepdims=True)

    full = lambda shape: pl.BlockSpec(shape, lambda i: (0,) * len(shape))
    return _carry(
        body, name="mixer_bwd", grid=(n_step,), comms=comms,
        in_specs=[pl.BlockSpec(memory_space=pltpu.SMEM), pl.BlockSpec((D_GMLP + D_ATTN, MIX_W), lambda i: (0, i))]
        + _h_specs() + _table_specs()
        + [full((N_HEADS, BLK, BLK)), full((N_HEADS, BLK)), full((D_GMLP, 1)), full((D_GMLP, 1)), BIAS_SPEC, LSE_SPEC],
        out_specs=[pl.BlockSpec((COL_K, MIX_W), lambda i: (0, i)), pl.BlockSpec((2 * D_KV, MIX_W), lambda i: (0, i)),
                   pl.BlockSpec((2 * D_KV, BLK), lambda i: (0, (i + n_step - 1) % n_step)),
                   full((N_HEADS, BLK, BLK)), full((N_HEADS, BLK)), full((8, D_GMLP)), full((N_HEADS, LANES))],
        out_shape=[jax.ShapeDtypeStruct((COL_K, t_tok), BF16), jax.ShapeDtypeStruct((2 * D_KV, t_tok), F32),
                   jax.ShapeDtypeStruct((2 * D_KV, n_step * BLK), F32),
                   jax.ShapeDtypeStruct((N_HEADS, BLK, BLK), BF16), jax.ShapeDtypeStruct((N_HEADS, BLK), F32),
                   jax.ShapeDtypeStruct((8, D_GMLP), F32), jax.ShapeDtypeStruct((N_HEADS, LANES), F32)],
        scratch_shapes=[pltpu.VMEM((D_GMLP, BLK), F32), pltpu.VMEM((D_GMLP, BLK), F32), pltpu.VMEM((N_HEADS, BLK, BLK), BF16),
                        pltpu.VMEM((N_HEADS, BLK, BLK), F32)],
        args=(sinks, dcat_t, h_t, h_t, h_t, h_t, h_t, cos_t, sin_t, cos_t, sin_t, w_spatial, b_spatial, vln_g, vln_b, band_bias, lse))


def _dkv_rows(dkvc_ref, dkvp_ref, width, store):
    for s in range(width // MIX_W):
        rest, last = slice(s * MIX_W, (s + 1) * MIX_W - BLK), slice((s + 1) * MIX_W - BLK, (s + 1) * MIX_W)
        store(rest, dkvc_ref[:, rest].astype(BF16))
        store(last, (dkvc_ref[:, last] + dkvp_ref[:, _cols(s)]).astype(BF16))


def _proj_in_wgrad(dh_b, dkvc_t, dkvp_t, xb, dz1, w_in_t, core_chip, comms=()):
    t_tok, d = xb.shape
    d_main, d_kv = dh_b.shape[0], dkvc_t.shape[0]
    rows = (d_main + d_kv) // N_DEV
    whole, cut = d_main // rows, d_main % rows

    def product(shard, dh_ref, dkvc_ref, dkvp_ref, xb_ref, dht_scr, sems):
        copies = [pltpu.make_async_copy(dh_ref.at[j * rows:(j + 1) * rows], dht_scr.at[j], sems.at[j]) for j in range(whole)]
        copies.append(pltpu.make_async_copy(dh_ref.at[whole * rows:d_main], dht_scr.at[whole, 0:cut], sems.at[whole]))

        @pl.when(pl.program_id(0) == 0)
        def _():
            for cp in copies:
                cp.start()

            def store(cols, val):
                dht_scr[whole, cut:rows, cols] = val[0:rows - cut]
                dht_scr[whole + 1, :, cols] = val[rows - cut:]

            _dkv_rows(dkvc_ref, dkvp_ref, t_tok, store)
            for cp in copies:
                cp.wait()

        return _dot(dht_scr[shard], xb_ref[...])

    tm = min(256, t_tok)

    def dgrad(i, dz1_ref, w_ref, dx_ref, dht_scr, sems):
        dh = dht_scr[:, :, pl.ds(pl.multiple_of(i * tm, tm), tm)].reshape(d_main + d_kv, tm)
        dx_ref[...] = ALPHA * dz1_ref[...] + _dot(dh, w_ref[...], TN)

    tile = pl.BlockSpec((tm, d), lambda s, cc: (jnp.maximum(s - N_DEV, 0), 0))
    after = dict(steps=t_tok // tm, body=dgrad, in_specs=[tile, _resident(w_in_t)], out_specs=[tile],
                 out_shape=[jax.ShapeDtypeStruct((t_tok, d), F32)], args=(dz1, w_in_t))
    return _wgrad_pair_sum(
        "proj_in_wgrad", product, (rows, d), [ANY, _resident(dkvc_t), _resident(dkvp_t), _resident(xb)],
        (dh_b, dkvc_t, dkvp_t, xb), core_chip, N_DEV // 2 - 1, comms, after=after,
        scratch_shapes=[pltpu.VMEM((N_DEV, rows, t_tok), BF16), pltpu.SemaphoreType.DMA((whole + 1,))])


def _proj_in_dgrad(dh_b, dkvc_t, dkvp_t, dz1, w_in_t, comms=()):
    t_tok, d = dz1.shape
    d_main, d_kv = dh_b.shape[0], dkvc_t.shape[0]
    tm = min(512, t_tok)

    def body(dh_ref, dkvc_ref, dkvp_ref, dz1_ref, w_ref, dx_ref, dkv_scr):
        def store(cols, val):
            dkv_scr[:, cols] = val

        _dkv_rows(dkvc_ref, dkvp_ref, tm, store)
        dx_ref[...] = (ALPHA * dz1_ref[...] + _dot(dh_ref[...], w_ref[0:d_main, :], TN)
                       + _dot(dkv_scr[...], w_ref[d_main:, :], TN))

    return _carry(
        body, name="proj_in_dgrad", grid=(t_tok // tm,), comms=comms,
        in_specs=[pl.BlockSpec((d_main, tm), lambda i: (0, i)), pl.BlockSpec((d_kv, tm), lambda i: (0, i)),
                  pl.BlockSpec((d_kv, tm // MIX_BLOCKS), lambda i: (0, i)),
                  pl.BlockSpec((tm, d), lambda i: (i, 0)), pl.BlockSpec((d_main + d_kv, d), lambda i: (0, 0))],
        out_specs=[pl.BlockSpec((tm, d), lambda i: (i, 0))],
        out_shape=[jax.ShapeDtypeStruct((t_tok, d), F32)],
        scratch_shapes=[pltpu.VMEM((d_kv, tm), BF16)],
        args=(dh_b, dkvc_t, dkvp_t, dz1, w_in_t))


def _adamw(w, g, m, v):
    m = ADAM_B1 * m + (1.0 - ADAM_B1) * g
    v = ADAM_B2 * v + (1.0 - ADAM_B2) * (g * g)
    m_hat = m / (1.0 - ADAM_B1 ** ADAM_STEP)
    v_hat = v / (1.0 - ADAM_B2 ** ADAM_STEP)
    delta = -ADAM_LR * (m_hat / (jnp.sqrt(v_hat) + ADAM_EPS) + ADAM_WD * w)
    return delta, m, v


ADAMW_STEPS = 4


def _adamw_shards(name, items, comms=(), rider=None):
    n_in, n_out = sum(4 + len(it[1]) for it in items), 4 * len(items)
    n_rin = len(rider["args"]) if rider else 0

    def body(*refs):
        ins, rins, outs, routs = refs[:n_in], refs[n_in:n_in + n_rin], refs[n_in + n_rin:n_in + n_rin + n_out], refs[n_in + n_rin + n_out:]
        for i, item in enumerate(items):
            (own_ref, w_ref, m_ref, v_ref), recv_refs, ins = ins[:4], ins[4:4 + len(item[1])], ins[4 + len(item[1]):]
            g = own_ref[...]
            for recv_ref in recv_refs:
                for k in range(recv_ref.shape[0]):
                    g = g + recv_ref[k].astype(F32)
            for o_ref, val in zip(outs[4 * i:4 * i + 4], (g,) + _adamw(w_ref[...], g, m_ref[...], v_ref[...])):
                o_ref[...] = val
        if rider:
            pl.when(pl.program_id(0) == 0)(lambda: rider["body"](rins, routs))

    in_specs, out_specs, out_shape, args = [], [], [], []
    for own, recvs, w, m, v in items:
        r, c = own.shape
        tiles = ADAMW_STEPS
        while (r // tiles) % BF16_ROWS:
            tiles //= 2
        blk = pl.BlockSpec((r // tiles, c), lambda s, k=ADAMW_STEPS // tiles: (s // k, 0))
        in_specs += [blk] * 4 + [pl.BlockSpec((a.shape[0], r // tiles, c), lambda s, k=ADAMW_STEPS // tiles: (0, s // k, 0))
                                 for a in recvs]
        out_specs += [blk] * 4
        out_shape += [jax.ShapeDtypeStruct((r, c), F32)] * 4
        args += [own, w, m, v, *recvs]
    if rider:
        in_specs, out_specs = in_specs + rider["in_specs"], out_specs + rider["out_specs"]
        out_shape, args = out_shape + rider["out_shape"], args + rider["args"]
    res, per_comm = _carry(body, name=name, grid=(ADAMW_STEPS,), comms=comms, in_specs=in_specs, out_specs=out_specs,
                           out_shape=out_shape, args=args)
    return [res[4 * i:4 * i + 4] for i in range(len(items))], res[n_out:], per_comm


VEC_VLN, VEC_LN1G, VEC_LN1B, VEC_LN2G, VEC_LN2B, VEC_SINK, VEC_LOSS, VEC_BSP, VEC_ROWS = 0, 1, 2, 3, 4, 5, 6, 8, 16


def _adamw_small(parts_w, parts_vec, params):
    n = parts_w.shape[0]
    flat = [a for p in params for a in p]
    shapes = [p[0].shape for p in params]

    def grads(gw, gv):
        return [gw, gv[VEC_VLN:VEC_VLN + 1, 0:D_GMLP], gv[VEC_VLN:VEC_VLN + 1, D_GMLP:2 * D_GMLP],
                gv[VEC_BSP:VEC_BSP + N_HEADS, 0:BLK], gv[VEC_LN1G:VEC_LN1G + 1], gv[VEC_LN1B:VEC_LN1B + 1],
                gv[VEC_LN2G:VEC_LN2G + 1], gv[VEC_LN2B:VEC_LN2B + 1], gv[VEC_SINK:VEC_SINK + 1, 0:N_HEADS]]

    def body(ins, outs):
        (pw_ref, pv_ref), ins = ins[:2], ins[2:]
        gw, gv = pw_ref[0].astype(F32), pv_ref[0]
        for k in range(1, n):
            gw, gv = gw + pw_ref[k].astype(F32), gv + pv_ref[k]
        for i, g in enumerate(grads(gw, gv)):
            w_ref, m_ref, v_ref = ins[3 * i:3 * i + 3]
            delta, m_new, v_new = _adamw(w_ref[...], g, m_ref[...], v_ref[...])
            for o_ref, val in zip(outs[4 * i:4 * i + 4], (g, delta, m_new, v_new)):
                o_ref[...] = val
        outs[-1][...] = gv[VEC_LOSS:VEC_LOSS + 1, 0:LANES]

    whole = lambda shape, **kw: pl.BlockSpec(shape, lambda i: (0,) * len(shape), **kw)
    once = dict(pipeline_mode=pl.Buffered(1))
    return dict(
        body=body, args=[parts_w, parts_vec, *flat],
        in_specs=[whole(parts_w.shape, **once), whole(parts_vec.shape, **once)] + [whole(a.shape, **once) for a in flat],
        out_specs=[whole(s) for s in shapes for _ in range(4)] + [whole((1, LANES))],
        out_shape=[jax.ShapeDtypeStruct(s, F32) for s in shapes for _ in range(4)] + [jax.ShapeDtypeStruct((1, LANES), F32)])


def _pair_sum(name, parts, recv, core_chip, comms=()):
    _, r, c = parts.shape
    tr = r if r <= 512 else 512

    def body(cc_ref, a_ref, b_ref, wire_ref, own_ref):
        s = a_ref[...] + b_ref[...]
        wire_ref[...] = s.astype(BF16)

        @pl.when(pl.program_id(1) == cc_ref[1])
        def _():
            own_ref[...] = s

    return _carry(
        body, name=name, grid=(r // tr, 4), prefetch=(core_chip,), comms=comms,
        in_specs=[pl.BlockSpec((None, tr, c), lambda i, q, cc: (2 * q + cc[0], i, 0)),
                  pl.BlockSpec((None, tr, c), lambda i, q, cc: (q, i, 0))],
        out_specs=[pl.BlockSpec((None, tr, c), lambda i, q, cc: (q, i, 0)), pl.BlockSpec((tr, c), lambda i, q, cc: (i, 0))],
        out_shape=[jax.ShapeDtypeStruct((4, r, c), BF16), jax.ShapeDtypeStruct((r, c), F32)],
        args=(parts, recv))


def kernel(x, positions, w_in, v_ln_g, v_ln_b, w_spatial, b_spatial, sinks, w_out, ln1_g, ln1_b, w_ff1, w_ff2, ln2_g, ln2_b, loss_target, m_w_in, m_v_ln_g, m_v_ln_b, m_w_spatial, m_b_spatial, m_sinks, m_w_out, m_ln1_g, m_ln1_b, m_w_ff1, m_w_ff2, m_ln2_g, m_ln2_b, v_w_in, v_v_ln_g, v_v_ln_b, v_w_spatial, v_b_spatial, v_sinks, v_w_out, v_ln1_g, v_ln1_b, v_w_ff1, v_w_ff2, v_ln2_g, v_ln2_b):
    _, t_tok, d = x.shape
    xi, yi, ci = _place()
    core_chip = jnp.stack([ci, 2 * xi + yi]).astype(jnp.int32)
    x2 = x.reshape(t_tok, d)
    target = loss_target.reshape(t_tok, d)
    inv_freq = ROPE_THETA ** (-jnp.arange(0, HEAD_DIM, 2, dtype=F32) / HEAD_DIM)
    wsp, bsp, sink_vec = w_spatial[0], b_spatial[0], sinks[0]
    vg_col, vb_col = v_ln_g.reshape(D_GMLP, 1), v_ln_b.reshape(D_GMLP, 1)
    big = {"in": w_in[0], "out": w_out[0], "ff1": w_ff1[0], "ff2": w_ff2[0]}
    half1, half2 = big["ff1"].shape[1] // 2, big["ff2"].shape[0] // 2
    w1_mine = [big["ff1"][:, :half1].astype(BF16), big["ff1"][:, half1:].astype(BF16)]
    w2_mine = [big["ff2"][:half2].astype(BF16), big["ff2"][half2:].astype(BF16)]

    (cos_t, sin_t), ((g_in,),) = _rope_tables(
        positions, jnp.tile(inv_freq, 2).reshape(HEAD_DIM, 1), comms=[_gather_comm([big["in"].T.astype(BF16)])])
    w_in_t = g_in.reshape(D_IN, d)
    (h_t, xb), ((g_out, w1_a),) = _proj_in(x2, w_in_t, comms=[_gather_comm([big["out"].astype(BF16), w1_mine[0]])])
    w_out_b = g_out.reshape(-1, d)
    band_bias = _band_bias()
    (cat_t, lse), ((w1_b, w2_a),) = _mixer_fwd(h_t, cos_t, sin_t, wsp, bsp, vg_col, vb_col, sink_vec, band_bias,
                                                comms=[_gather_comm([w1_mine[1], w2_mine[0]])])
    (xhat1, rstd1, x1b), ((w2_b,),) = _proj_out(cat_t, x2, w_out_b, ln1_g, ln1_b, comms=[_gather_comm([w2_mine[1]])])
    act_b, dpre_b, dz2b, dz1, stats = _ffn_fwd_bwd(xhat1, rstd1, x1b, target, [w1_a, w1_b], [w2_a, w2_b], ln1_g, ln1_b, ln2_g, ln2_b)

    (dcat_t, gw_out), _ = _proj_out_bwd(dz1, cat_t, w_out_b)
    p_out = gw_out.reshape(N_DEV, -1, d)
    r_ff1_a, wire_ff1, own_ff1, ((s_out,),) = _ffn_wgrad(
        "ffn_wgrad1", x1b, dpre_b, False, core_chip, 1, comms=[_sibling_comm([p_out])])
    (wire_out, own_out), _ = _pair_sum("pair_sum_out", p_out, s_out, core_chip)
    _, wire_ff2, own_ff2, ((r_ff1_b,),) = _ffn_wgrad(
        "ffn_wgrad2", act_b, dz2b, True, core_chip, 0, comms=[_flips_comm(wire_ff1, 1)])
    (dh_b, dkvc_t, dkvp_t, g_wsp, g_bsp, g_vln, g_sink), ((r_ff2,), (r_out,)) = _mixer_bwd(
        dcat_t, h_t, cos_t, sin_t, wsp, bsp, vg_col, vb_col, sink_vec, band_bias, lse,
        comms=[_flips_comm(wire_ff2, 0), _chips_comm([wire_out])])
    sink_row = jnp.pad(g_sink.sum(axis=1).reshape(1, N_HEADS), ((0, 0), (0, d - N_HEADS)))
    small_vec = jnp.concatenate([g_vln[0:2].reshape(1, d), stats[0:4], sink_row, stats[4:5], jnp.zeros((1, d), F32),
                                 jnp.pad(g_bsp, ((0, 0), (0, d - BLK)))], axis=0)
    r_in, _, own_in, ((parts_w, parts_vec),), (grad_x,) = _proj_in_wgrad(
        dh_b, dkvc_t, dkvp_t, xb, dz1, w_in_t, core_chip, comms=[_gather_comm([g_wsp.reshape(-1, BLK), small_vec])])
    small = [(w_spatial, m_w_spatial, v_w_spatial), (v_ln_g, m_v_ln_g, v_v_ln_g), (v_ln_b, m_v_ln_b, v_v_ln_b),
             (b_spatial, m_b_spatial, v_b_spatial), (ln1_g, m_ln1_g, v_ln1_g), (ln1_b, m_ln1_b, v_ln1_b),
             (ln2_g, m_ln2_g, v_ln2_g), (ln2_b, m_ln2_b, v_ln2_b), (sinks, m_sinks, v_sinks)]
    views = [(-1, BLK), None, None, (N_HEADS, BLK)] + [None] * 5
    small_update = _adamw_small(parts_w, parts_vec, [
        tuple(a if vw is None else a.reshape(vw) for a in p) for p, vw in zip(small, views)])
    (out_out, ff1_out, ff2_out, in_out_t), small_res, _ = _adamw_shards("adamw_all", [
        (own_out, [r_out], big["out"], m_w_out[0], v_w_out[0]),
        (own_ff1, [r_ff1_a, r_ff1_b], big["ff1"], m_w_ff1[0], v_w_ff1[0]),
        (own_ff2, [r_ff2], big["ff2"], m_w_ff2[0], v_w_ff2[0]),
        (own_in, [r_in], big["in"].T, m_w_in[0].T, v_w_in[0].T)], rider=small_update)
    in_out = [o.T for o in in_out_t]
    small_out = [[o.reshape(p[0].shape) for o in small_res[4 * i:4 * i + 4]] for i, p in enumerate(small)]
    loss = small_res[-1][0, 0]

    big_out = {0: in_out, 6: out_out, 9: ff1_out, 10: ff2_out}
    small_slot = {3: 0, 1: 1, 2: 2, 4: 3, 7: 4, 8: 5, 11: 6, 12: 7, 5: 8}
    outs = [loss, grad_x.reshape(x.shape)]
    for kind in range(4):
        for wi in range(13):
            outs.append(big_out[wi][kind][None] if wi in big_out else small_out[small_slot[wi]][kind])
    return tuple(outs)
```

```python
import math

import jax
import jax.numpy as jnp
from jax import lax
from jax.experimental import pallas as pl
from jax.experimental.pallas import tpu as pltpu

F32 = jnp.float32
BF16 = jnp.bfloat16
MESH = pl.DeviceIdType.MESH

HEAD_DIM = 64
N_HEADS = 8
N_KV_HEADS = 2
BLK = 128
D_GMLP = N_HEADS * HEAD_DIM
D_ATTN = N_HEADS * HEAD_DIM
D_KV = N_KV_HEADS * HEAD_DIM
D_IN = 2 * D_GMLP + D_ATTN + 2 * D_KV
COL_U, COL_V, COL_Q, COL_K = 0, D_GMLP, 2 * D_GMLP, 2 * D_GMLP + D_ATTN
ROPE_THETA = 10000.0
LN_EPS = 1e-5
ALPHA = 2.0 ** 0.25
NEG_INF = -1e30
SCORE_SCALE = 1.0 / math.sqrt(HEAD_DIM)
ADAM_LR, ADAM_B1, ADAM_B2, ADAM_EPS, ADAM_WD, ADAM_STEP = 0.001, 0.9, 0.999, 1e-08, 0.01, 10
N_DEV = 8
LANES = 128
VMEM_LIMIT = 56 * 1024 * 1024
FFN_ROWS = 256

NT = (((1,), (1,)), ((), ()))
TN = (((0,), (0,)), ((), ()))


def _params(*sem):
    return pltpu.CompilerParams(dimension_semantics=sem, vmem_limit_bytes=VMEM_LIMIT)


def _dot(a, b, dims=None):
    if dims is None:
        return jnp.dot(a, b, preferred_element_type=F32)
    return lax.dot_general(a, b, dims, preferred_element_type=F32)


def _mean(a):
    return jnp.mean(a, axis=-1, keepdims=True)


def _ln_fwd(z, g, b):
    zc = z - _mean(z)
    rstd = lax.rsqrt(_mean(zc * zc) + LN_EPS)
    xhat = zc * rstd
    return xhat * g + b, xhat, rstd


def _ln_bwd(dy, xhat, rstd, g):
    dxhat = dy * g
    return rstd * (dxhat - _mean(dxhat) - xhat * _mean(dxhat * xhat))


_GELU_C = math.sqrt(2.0 / math.pi)


def _gelu(x):
    t = jnp.tanh(_GELU_C * (x + 0.044715 * (x * x * x)))
    return 0.5 * x * (1.0 + t)


def _gelu_and_grad(x):
    x2 = x * x
    t = jnp.tanh(_GELU_C * (x + 0.044715 * (x2 * x)))
    hx, ht = 0.5 * x, 0.5 * (1.0 + t)
    return x * ht, ht + hx * (1.0 - t * t) * (_GELU_C * (1.0 + 3.0 * 0.044715 * x2))


def _mean0(a):
    return jnp.mean(a, axis=0, keepdims=True)


def _ln_fwd_t(z, g, b):
    zc = z - _mean0(z)
    rstd = lax.rsqrt(_mean0(zc * zc) + LN_EPS)
    xhat = zc * rstd
    return xhat * g + b, xhat, rstd


def _ln_bwd_t(dy, xhat, rstd, g):
    dxhat = dy * g
    return rstd * (dxhat - _mean0(dxhat) - xhat * _mean0(dxhat * xhat))


def _rope_t(t, cos, sin_signed, bwd=False):
    half = HEAD_DIM // 2
    outs = []
    for r in range(0, t.shape[0], HEAD_DIM):
        th = t[r:r + HEAD_DIM]
        sw = jnp.concatenate([th[half:], th[:half]], axis=0) * sin_signed
        outs.append(th * cos - sw if bwd else th * cos + sw)
    return jnp.concatenate(outs, axis=0)


ANY = pl.BlockSpec(memory_space=pl.ANY)
GATHER_PIECES = 2
BF16_ROWS = 16


def _place():
    return lax.axis_index("x"), lax.axis_index("y"), lax.axis_index("c")


class _Comm:
    def __init__(self, ins, outs, sems, start, finish):
        self.ins, self.outs, self.sems, self.start, self.finish = ins, outs, sems, start, finish


def _gather_comm(arrs):
    n = len(arrs)
    pieces = []
    for a, arr in enumerate(arrs):
        k = GATHER_PIECES
        while arr.shape[0] % (k * BF16_ROWS):
            k //= 2
        pieces += [(a, p * (arr.shape[0] // k), arr.shape[0] // k) for p in range(k)]

    def parts(ins, outs, sems):
        send_sems, recv_sems, local_sems = sems
        x, y, c = _place()
        me, sibling = (x, y, c), (x, y, 1 - c)
        chips = [(1 - x, y), (x, 1 - y), (1 - x, 1 - y)]

        def copy(u, k, block, to, local=False):
            a, r0, nr = pieces[u]
            px, py, pc = block
            dst = outs[a].at[4 * px + 2 * py + pc, pl.ds(r0, nr)]
            return pltpu.make_async_remote_copy(
                src_ref=ins[a].at[pl.ds(r0, nr)] if local else dst, dst_ref=dst,
                send_sem=send_sems.at[u, k], recv_sem=recv_sems.at[u, k], device_id=to, device_id_type=MESH)

        mine = [pltpu.make_async_copy(ins[a], outs[a].at[4 * x + 2 * y + c], local_sems.at[a]) for a in range(n)]
        first = []
        for u in range(len(pieces)):
            first.append(copy(u, 0, me, sibling, local=True))
            first += [copy(u, 1 + j, me, (*chip, c), local=True) for j, chip in enumerate(chips)]
        return copy, mine, first, me, sibling, chips, c

    def start(ins, outs, sems):
        _, mine, first, *_ = parts(ins, outs, sems)
        for cp in mine + first:
            cp.start()

    def finish(ins, outs, sems):
        copy, mine, first, me, sibling, chips, c = parts(ins, outs, sems)
        passed = []
        for u in range(len(pieces)):
            for j, chip in enumerate(chips):
                copy(u, 1 + j, (*chip, c), me).wait_recv()
                fwd = copy(u, 4 + j, (*chip, c), sibling)
                fwd.start()
                passed.append(fwd)
        for u in range(len(pieces)):
            copy(u, 0, sibling, me).wait_recv()
            for j, chip in enumerate(chips):
                copy(u, 4 + j, (*chip, 1 - c), me).wait_recv()
        for cp in first + passed:
            cp.wait_send()
        for cp in mine:
            cp.wait()

    return _Comm(list(arrs), [jax.ShapeDtypeStruct((N_DEV,) + a.shape, a.dtype) for a in arrs],
                 [pltpu.SemaphoreType.DMA((len(pieces), 7)), pltpu.SemaphoreType.DMA((len(pieces), 7)),
                  pltpu.SemaphoreType.DMA((n,))], start, finish)


def _sibling_comm(parts):
    n = len(parts)

    def copies(ins, outs, sems):
        x, y, c = _place()
        return [pltpu.make_async_remote_copy(
            src_ref=ins[a].at[2 * q + (1 - c)], dst_ref=outs[a].at[q],
            send_sem=sems[0].at[a, q], recv_sem=sems[1].at[a, q],
            device_id=(x, y, 1 - c), device_id_type=MESH) for a in range(n) for q in range(4)]

    return _Comm(list(parts), [jax.ShapeDtypeStruct((4,) + p.shape[1:], p.dtype) for p in parts],
                 [pltpu.SemaphoreType.DMA((n, 4)), pltpu.SemaphoreType.DMA((n, 4))],
                 lambda *r: [cp.start() for cp in copies(*r)], lambda *r: [cp.wait() for cp in copies(*r)])


def _chips_comm(chip_parts, rows=None):
    n = len(chip_parts)
    r0, nr = (0, None) if rows is None else rows

    def copies(ins, outs, sems):
        x, y, c = _place()
        chips = [(1 - x, y), (x, 1 - y), (1 - x, 1 - y)]
        src = lambda a, q: ins[a].at[q] if rows is None else ins[a].at[q, pl.ds(r0, nr)]
        return [pltpu.make_async_remote_copy(
            src_ref=src(a, 2 * px + py), dst_ref=outs[a].at[k],
            send_sem=sems[0].at[a, k], recv_sem=sems[1].at[a, k],
            device_id=(px, py, c), device_id_type=MESH) for a in range(n) for k, (px, py) in enumerate(chips)]

    shape = lambda p: (3,) + p.shape[1:] if rows is None else (3, nr) + p.shape[2:]
    return _Comm(list(chip_parts), [jax.ShapeDtypeStruct(shape(p), p.dtype) for p in chip_parts],
                 [pltpu.SemaphoreType.DMA((n, 3)), pltpu.SemaphoreType.DMA((n, 3))],
                 lambda *r: [cp.start() for cp in copies(*r)], lambda *r: [cp.wait() for cp in copies(*r)])


def _flips_comm(sums, first):
    m = sums.shape[0]

    def copies(ins, outs, sems):
        return [pltpu.make_async_remote_copy(
            src_ref=ins[0].at[j], dst_ref=outs[0].at[j], send_sem=sems[0].at[j], recv_sem=sems[1].at[j],
            device_id=_flipped(first + j), device_id_type=MESH) for j in range(m)]

    return _Comm([sums], [jax.ShapeDtypeStruct(sums.shape, sums.dtype)],
                 [pltpu.SemaphoreType.DMA((m,)), pltpu.SemaphoreType.DMA((m,))],
                 lambda *r: [cp.start() for cp in copies(*r)], lambda *r: [cp.wait() for cp in copies(*r)])


def _carry(body, *, name, grid, in_specs, out_specs, out_shape, args, comms=(), scratch_shapes=(), prefetch=()):
    n_pre, n_in, n_out, n_scr = len(prefetch), len(in_specs), len(out_specs), len(scratch_shapes)
    c_ins = [a for cm in comms for a in cm.ins]
    c_outs = [s for cm in comms for s in cm.outs]
    c_sems = [s for cm in comms for s in cm.sems]

    def wrapped(*refs):
        pre, refs = refs[:n_pre], refs[n_pre:]
        ins, refs = refs[:n_in], refs[n_in:]
        cins, refs = refs[:len(c_ins)], refs[len(c_ins):]
        outs, refs = refs[:n_out], refs[n_out:]
        couts, refs = refs[:len(c_outs)], refs[len(c_outs):]
        scr, sems = refs[:n_scr], refs[n_scr:]
        groups, i0, o0, s0 = [], 0, 0, 0
        for cm in comms:
            groups.append((cm, cins[i0:i0 + len(cm.ins)], couts[o0:o0 + len(cm.outs)], sems[s0:s0 + len(cm.sems)]))
            i0, o0, s0 = i0 + len(cm.ins), o0 + len(cm.outs), s0 + len(cm.sems)
        first = pl.program_id(0) == 0
        last = pl.program_id(0) == grid[0] - 1
        for ax in range(1, len(grid)):
            first = first & (pl.program_id(ax) == 0)
            last = last & (pl.program_id(ax) == grid[ax] - 1)
        if comms:
            @pl.when(first)
            def _():
                for cm, ci, co, cs in groups:
                    cm.start(ci, co, cs)
        body(*pre, *ins, *outs, *scr)
        if comms:
            @pl.when(last)
            def _():
                for cm, ci, co, cs in groups:
                    cm.finish(ci, co, cs)

    grid_spec = pltpu.PrefetchScalarGridSpec(
        num_scalar_prefetch=n_pre, grid=grid,
        in_specs=list(in_specs) + [ANY] * len(c_ins), out_specs=list(out_specs) + [ANY] * len(c_outs),
        scratch_shapes=list(scratch_shapes) + c_sems)
    res = pl.pallas_call(
        wrapped, name=name, grid_spec=grid_spec, out_shape=list(out_shape) + c_outs,
        compiler_params=_params(*(["arbitrary"] * len(grid))),
    )(*prefetch, *args, *c_ins)
    outs, rest, per_comm = res[:n_out], res[n_out:], []
    for cm in comms:
        per_comm.append(rest[:len(cm.outs)])
        rest = rest[len(cm.outs):]
    return outs, per_comm


def _rope_tables(pos_row, inv_freq_col, comms=()):
    t_tok = pos_row.shape[1]
    tm = min(512, t_tok)

    def body(pos_ref, invf_ref, cos_ref, sin_ref):
        ang = pos_ref[...].astype(F32) * invf_ref[...]
        row = lax.broadcasted_iota(jnp.int32, ang.shape, 0)
        cos_ref[...] = jnp.cos(ang)
        sin_ref[...] = jnp.sin(ang) * jnp.where(row < HEAD_DIM // 2, -1.0, 1.0)

    return _carry(
        body, name="rope_tables", grid=(t_tok // tm,), comms=comms,
        in_specs=[pl.BlockSpec((1, tm), lambda i: (0, i)), pl.BlockSpec((HEAD_DIM, 1), lambda i: (0, 0))],
        out_specs=[pl.BlockSpec((HEAD_DIM, tm), lambda i: (0, i))] * 2,
        out_shape=[jax.ShapeDtypeStruct((HEAD_DIM, t_tok), F32)] * 2,
        args=(pos_row, inv_freq_col))


def _proj_in(x2, w_in_t, comms=()):
    t_tok, d = x2.shape
    d_in = w_in_t.shape[0]
    tm = min(512, t_tok)

    def body(x_ref, w_ref, h_ref, xb_ref):
        xb = x_ref[...].astype(BF16)
        xb_ref[...] = xb
        h_ref[...] = _dot(w_ref[...], xb, NT)

    return _carry(
        body, name="proj_in", grid=(t_tok // tm,), comms=comms,
        in_specs=[pl.BlockSpec((tm, d), lambda i: (i, 0)), pl.BlockSpec((d_in, d), lambda i: (0, 0))],
        out_specs=[pl.BlockSpec((d_in, tm), lambda i: (0, i)), pl.BlockSpec((tm, d), lambda i: (i, 0))],
        out_shape=[jax.ShapeDtypeStruct((d_in, t_tok), F32), jax.ShapeDtypeStruct((t_tok, d), BF16)],
        args=(x2, w_in_t))


MIX_BLOCKS = 2
MIX_W = MIX_BLOCKS * BLK


def _prev_block(i):
    return jnp.maximum(MIX_BLOCKS * i - 1, 0)


def _h_specs():
    kv_row = COL_K // (2 * D_KV)
    return [
        pl.BlockSpec((D_GMLP, MIX_W), lambda i: (0, i)),
        pl.BlockSpec((D_GMLP, MIX_W), lambda i: (1, i)),
        pl.BlockSpec((D_ATTN, MIX_W), lambda i: (2, i)),
        pl.BlockSpec((2 * D_KV, MIX_W), lambda i: (kv_row, i)),
        pl.BlockSpec((2 * D_KV, BLK), lambda i: (kv_row, _prev_block(i))),
    ]


def _table_specs():
    return [
        pl.BlockSpec((HEAD_DIM, MIX_W), lambda i: (0, i)),
        pl.BlockSpec((HEAD_DIM, MIX_W), lambda i: (0, i)),
        pl.BlockSpec((HEAD_DIM, BLK), lambda i: (0, _prev_block(i))),
        pl.BlockSpec((HEAD_DIM, BLK), lambda i: (0, _prev_block(i))),
    ]


def _cols(b):
    return slice(b * BLK, (b + 1) * BLK)


LSE_ROWS = 8
LSE_SPEC = pl.BlockSpec((LSE_ROWS, D_ATTN), lambda i: (i, 0))


def _block_inputs(b, i, kvc, kvp_ref, cos, sin, cosp_ref, sinp_ref, bias_ref):
    if b == 0:
        kv_prev, cos_prev, sin_prev, bias = kvp_ref[...], cosp_ref[...], sinp_ref[...], bias_ref[jnp.minimum(i, 1)]
    else:
        kv_prev, cos_prev, sin_prev, bias = kvc[:, _cols(b - 1)], cos[:, _cols(b - 1)], sin[:, _cols(b - 1)], bias_ref[1]
    return kvc[:, _cols(b)], kv_prev, cos[:, _cols(b)], sin[:, _cols(b)], cos_prev, sin_prev, bias


def _band_bias():
    ki = lax.broadcasted_iota(jnp.int32, (2, 2 * BLK, BLK), 1)
    qi = lax.broadcasted_iota(jnp.int32, (2, 2 * BLK, BLK), 2)
    later = lax.broadcasted_iota(jnp.int32, (2, 2 * BLK, BLK), 0) > 0
    dist = qi + BLK - ki
    return jnp.where((dist >= 0) & (dist < BLK) & ((ki >= BLK) | later), 0.0, NEG_INF).astype(F32)


BIAS_SPEC = pl.BlockSpec((2, 2 * BLK, BLK), lambda i: (0, 0, 0))


def _keys_values(kvc, kvp, cosc, sinc, cosp, sinp):
    kp, kc = _rope_t(kvp[:D_KV], cosp, sinp), _rope_t(kvc[:D_KV], cosc, sinc)
    k_t = jnp.concatenate([kp, kc], axis=1).astype(BF16)
    k_n = jnp.concatenate([kp.T, kc.T], axis=0).astype(BF16)
    v_t = jnp.concatenate([kvp[D_KV:], kvc[D_KV:]], axis=1).astype(BF16)
    return k_t, k_n, v_t


def _pad_head(th, kv):
    z = jnp.zeros_like(th)
    return jnp.concatenate([th, z] if kv == 0 else [z, th], axis=0)


def _group_lanes(parts):
    return jnp.concatenate(parts, axis=1)


def _softmax_sink_t(s, sink):
    m = jnp.maximum(jnp.max(s, axis=0, keepdims=True), sink)
    e = jnp.exp(s - m)
    denom = jnp.sum(e, axis=0, keepdims=True) + jnp.exp(sink - m)
    return e * (1.0 / denom), m + jnp.log(denom)


def _causal():
    row = lax.broadcasted_iota(jnp.int32, (BLK, BLK), 0)
    col = lax.broadcasted_iota(jnp.int32, (BLK, BLK), 1)
    return row >= col


def _mask_w_once(wsp_ref, wm_scr):
    @pl.when(pl.program_id(0) == 0)
    def _():
        causal = _causal()
        for hh in range(N_HEADS):
            wm_scr[hh] = jnp.where(causal, wsp_ref[hh], 0.0).astype(BF16)


def _mixer_fwd(h_t, cos_t, sin_t, w_spatial, b_spatial, vln_g, vln_b, sinks, band_bias, comms=()):
    t_tok = h_t.shape[1]
    group = N_HEADS // N_KV_HEADS

    def body(sinks_ref, u_ref, vg_ref, q_ref, kvc_ref, kvp_ref, cos_ref, sin_ref, cosp_ref, sinp_ref,
             wsp_ref, bsp_ref, g_ref, b_ref, bias_ref, cat_ref, lse_ref, wm_scr):
        i = pl.program_id(0)
        _mask_w_once(wsp_ref, wm_scr)
        lse_ref[...] = jnp.zeros_like(lse_ref)
        ua = _gelu(u_ref[...])
        vp, _, _ = _ln_fwd_t(_gelu(vg_ref[...]), g_ref[...], b_ref[...])
        vpb = vp.astype(BF16)
        for b in range(MIX_BLOCKS):
            for hh in range(N_HEADS):
                rows = slice(hh * HEAD_DIM, (hh + 1) * HEAD_DIM)
                mixed = _dot(vpb[rows, _cols(b)], wm_scr[hh], NT) + bsp_ref[hh:hh + 1, :]
                cat_ref[rows, _cols(b)] = (ua[rows, _cols(b)] * mixed).astype(BF16)

        kvc, cos, sin = kvc_ref[...], cos_ref[...], sin_ref[...]
        qr = (_rope_t(q_ref[...], cos, sin) * SCORE_SCALE).astype(BF16)
        sinks4 = [_group_lanes([jnp.full((1, BLK), sinks_ref[hh], F32) for hh in range(kv * group, (kv + 1) * group)])
                  for kv in range(N_KV_HEADS)]
        for b in range(MIX_BLOCKS):
            kv_cur, kv_prev, cosc, sinc, cosp, sinp, bias1 = _block_inputs(b, i, kvc, kvp_ref, cos, sin, cosp_ref, sinp_ref, bias_ref)
            _, k_n, v_t = _keys_values(kv_cur, kv_prev, cosc, sinc, cosp, sinp)
            bias = _group_lanes([bias1] * group)
            for kv in range(N_KV_HEADS):
                heads = range(kv * group, (kv + 1) * group)
                qs = _group_lanes([qr[hh * HEAD_DIM:(hh + 1) * HEAD_DIM, _cols(b)] for hh in heads])
                p, lse = _softmax_sink_t(_dot(k_n, _pad_head(qs, kv)) + bias, sinks4[kv])
                lse_ref[b * N_KV_HEADS + kv:b * N_KV_HEADS + kv + 1, :] = lse
                o = _dot(v_t[kv * HEAD_DIM:(kv + 1) * HEAD_DIM], p.astype(BF16)).astype(BF16)
                for j, hh in enumerate(heads):
                    cat_ref[D_GMLP + hh * HEAD_DIM:D_GMLP + (hh + 1) * HEAD_DIM, _cols(b)] = o[:, j * BLK:(j + 1) * BLK]

    full = lambda shape: pl.BlockSpec(shape, lambda i: (0,) * len(shape))
    return _carry(
        body, name="mixer_fwd", grid=(t_tok // MIX_W,), comms=comms,
        in_specs=[pl.BlockSpec(memory_space=pltpu.SMEM)] + _h_specs() + _table_specs() + [
            full((N_HEADS, BLK, BLK)), full((N_HEADS, BLK)), full((D_GMLP, 1)), full((D_GMLP, 1)), BIAS_SPEC],
        out_specs=[pl.BlockSpec((D_GMLP + D_ATTN, MIX_W), lambda i: (0, i)), LSE_SPEC],
        out_shape=[jax.ShapeDtypeStruct((D_GMLP + D_ATTN, t_tok), BF16),
                   jax.ShapeDtypeStruct((t_tok // MIX_W * LSE_ROWS, D_ATTN), F32)],
        scratch_shapes=[pltpu.VMEM((N_HEADS, BLK, BLK), BF16)],
        args=(sinks, h_t, h_t, h_t, h_t, h_t, cos_t, sin_t, cos_t, sin_t, w_spatial, b_spatial, vln_g, vln_b, band_bias))


def _proj_out(cat_t, x2, w_out_b, ln1_g, ln1_b, comms=()):
    t_tok, d = x2.shape
    tm = min(512, t_tok)

    def body(cat_ref, x_ref, w_ref, g_ref, b_ref, xhat_ref, rstd_ref, x1b_ref):
        x1, xhat, rstd = _ln_fwd(ALPHA * x_ref[...] + _dot(cat_ref[...], w_ref[...], TN), g_ref[...], b_ref[...])
        xhat_ref[...] = xhat
        rstd_ref[...] = rstd
        x1b_ref[...] = x1.astype(BF16)

    tok = lambda w: pl.BlockSpec((tm, w), lambda i: (i, 0))
    vec = pl.BlockSpec((1, d), lambda i: (0, 0))
    return _carry(
        body, name="proj_out", grid=(t_tok // tm,), comms=comms,
        in_specs=[pl.BlockSpec((cat_t.shape[0], tm), lambda i: (0, i)), tok(d), pl.BlockSpec(w_out_b.shape, lambda i: (0, 0)), vec, vec],
        out_specs=[tok(d), tok(1), tok(d)],
        out_shape=[jax.ShapeDtypeStruct((t_tok, d), F32), jax.ShapeDtypeStruct((t_tok, 1), F32), jax.ShapeDtypeStruct((t_tok, d), BF16)],
        args=(cat_t, x2, w_out_b, ln1_g, ln1_b))


def _ffn_fwd_bwd(xhat1, rstd1, x1b, target, w1_parts, w2_parts, ln1_g, ln1_b, ln2_g, ln2_b):
    t_tok, d = xhat1.shape
    n_part = len(w1_parts)
    n_chunk, _, fp = w1_parts[0].shape
    f = n_chunk * n_part * fp
    tm = min(FFN_ROWS, t_tok)

    def body(xhat1_ref, rstd1_ref, x1b_ref, tgt_ref, *refs):
        w1_hbm, w2_hbm = refs[:n_part], refs[n_part:2 * n_part]
        (g1_ref, b1_ref, g2_ref, b2_ref, act_ref, dpre_ref, dz2b_ref, dz1_ref, stats_ref,
         r_scr, w1_ref, w2_ref, w_sems) = refs[2 * n_part:]

        @pl.when(pl.program_id(0) == 0)
        def _():
            stats_ref[...] = jnp.zeros_like(stats_ref)
            loads = []
            for j in range(n_chunk):
                for p in range(n_part):
                    units = pl.ds((j * n_part + p) * fp, fp)
                    loads.append(pltpu.make_async_copy(w1_hbm[p].at[j], w1_ref.at[:, units], w_sems.at[0, p, j]))
                    loads.append(pltpu.make_async_copy(w2_hbm[p].at[j], w2_ref.at[units, :], w_sems.at[1, p, j]))
            for cp in loads:
                cp.start()
            for cp in loads:
                cp.wait()

        g1, g2 = g1_ref[...], g2_ref[...]
        xhat1 = xhat1_ref[...]
        r_scr[...] = jnp.maximum(_dot(x1b_ref[...], w1_ref[...]), 0.0)
        r = r_scr[...]
        act = (r * r).astype(BF16)
        act_ref[...] = act
        ff = _dot(act, w2_ref[...])
        y, xhat2, rstd2 = _ln_fwd(ALPHA * (xhat1 * g1 + b1_ref[...]) + ff, g2, b2_ref[...])
        diff = y - tgt_ref[...]
        loss = 0.5 * jnp.sum(jnp.sum(diff * diff, axis=-1, keepdims=True) / d, axis=0, keepdims=True)
        dy = diff / d
        dz2 = _ln_bwd(dy, xhat2, rstd2, g2)
        dz2b = dz2.astype(BF16)
        dz2b_ref[...] = dz2b
        dpre = (_dot(dz2b, w2_ref[...], NT) * (2.0 * r_scr[...])).astype(BF16)
        dpre_ref[...] = dpre
        dx1 = ALPHA * dz2 + _dot(dpre, w1_ref[...], NT)
        dz1_ref[...] = _ln_bwd(dx1, xhat1, rstd1_ref[...], g1)
        stats_ref[0:1, :] += jnp.sum(dx1 * xhat1, axis=0, keepdims=True)
        stats_ref[1:2, :] += jnp.sum(dx1, axis=0, keepdims=True)
        stats_ref[2:3, :] += jnp.sum(dy * xhat2, axis=0, keepdims=True)
        stats_ref[3:4, :] += jnp.sum(dy, axis=0, keepdims=True)
        stats_ref[4:5, :] += jnp.broadcast_to(loss, (1, d))

    tok = lambda w: pl.BlockSpec((tm, w), lambda i: (i, 0))
    vec = pl.BlockSpec((1, d), lambda i: (0, 0))
    return _carry(
        body, name="ffn_fwd_bwd", grid=(t_tok // tm,),
        in_specs=[tok(d), tok(1), tok(d), tok(d)] + [ANY] * (2 * n_part) + [vec, vec, vec, vec],
        out_specs=[tok(f), tok(f), tok(d), tok(d), pl.BlockSpec((8, d), lambda i: (0, 0))],
        out_shape=[jax.ShapeDtypeStruct((t_tok, f), BF16), jax.ShapeDtypeStruct((t_tok, f), BF16),
                   jax.ShapeDtypeStruct((t_tok, d), BF16), jax.ShapeDtypeStruct((t_tok, d), F32), jax.ShapeDtypeStruct((8, d), F32)],
        scratch_shapes=[pltpu.VMEM((tm, f), F32), pltpu.VMEM((d, f), BF16), pltpu.VMEM((f, d), BF16),
                        pltpu.SemaphoreType.DMA((2, n_part, n_chunk))],
        args=(xhat1, rstd1, x1b, target, *w1_parts, *w2_parts, ln1_g, ln1_b, ln2_g, ln2_b))[0]


WGRAD_STEPS = [(True, 0), (True, 1), (False, 0), (True, 2), (False, 1), (True, 3), (False, 2), (False, 3)]
CHIP_FLIPS = [3, 1, 2, 0]


def _pick(table, s):
    out = table[-1]
    for i in range(len(table) - 2, -1, -1):
        out = jnp.where(s == i, table[i], out)
    return out


def _wgrad_shard(s, cc):
    q = jnp.bitwise_xor(cc[1], _pick([CHIP_FLIPS[k] for _, k in WGRAD_STEPS], s))
    return 2 * q + jnp.where(_pick([int(sibling) for sibling, _ in WGRAD_STEPS], s) == 1, 1 - cc[0], cc[0])


def _flipped(k):
    x, y, c = _place()
    return (1 - x if CHIP_FLIPS[k] // 2 else x, 1 - y if CHIP_FLIPS[k] % 2 else y, c)


def _wgrad_pair_sum(name, product, chunk, in_specs, args, core_chip, n_sent, comms=(), scratch_shapes=(), rider=None):
    half = N_DEV // 2
    n_in, n_out = len(in_specs), 2 + (0 < n_sent) + (n_sent < half - 1)
    more = rider or dict(in_specs=[], out_specs=[], out_shape=[], args=[])
    n_rin, n_rout = len(more["in_specs"]), len(more["out_specs"])

    def body(cc_ref, *refs):
        ins, rins, refs = refs[:n_in], refs[n_in:n_in + n_rin], refs[n_in + n_rin:]
        outs, routs, scr = refs[:n_out], refs[n_out:n_out + n_rout], refs[n_out + n_rout:]
        if rider:
            pl.when(pl.program_id(0) == 0)(lambda: rider["body"](cc_ref, rins, routs))
        (own_ref, recv_ref), from_chips_ref, wire_ref = outs[-2:], outs[0], outs[n_out - 3]
        send_buf, got, send_sems, recv_sems, got_sem, wire_buf, leave_sems, arrive_sems = scr[:8]
        s = pl.program_id(0)
        x, y, c = _place()
        def send(q):
            return pltpu.make_async_remote_copy(
                src_ref=send_buf.at[q % 2], dst_ref=recv_ref.at[q], send_sem=send_sems.at[q], recv_sem=recv_sems.at[q],
                device_id=(x, y, 1 - c), device_id_type=MESH)

        def load(q):
            return pltpu.make_async_copy(recv_ref.at[q], got, got_sem.at[0])

        def leave(k):
            if k < n_sent:
                return pltpu.make_async_remote_copy(
                    src_ref=wire_buf.at[k], dst_ref=from_chips_ref.at[k], send_sem=leave_sems.at[k],
                    recv_sem=arrive_sems.at[k], device_id=_flipped(k), device_id_type=MESH)
            return pltpu.make_async_copy(wire_buf.at[k], wire_ref.at[k - n_sent], leave_sems.at[k])

        for step, (sibling, q) in enumerate(WGRAD_STEPS):
            if not sibling:
                @pl.when(s == step)
                def _(q=q):
                    send(q).wait_recv()
                    load(q).start()

        g = product(_wgrad_shard(s, cc_ref), *ins, *scr[8:])

        for step, (sibling, q) in enumerate(WGRAD_STEPS):
            @pl.when(s == step)
            def _(sibling=sibling, q=q):
                if sibling:
                    if q >= 2:
                        send(q - 2).wait_send()
                    send_buf[q % 2] = g
                    send(q).start()
                    return
                load(q).wait()
                total = g + got[...]
                if q < half - 1:
                    wire_buf[q] = total.astype(BF16)
                    leave(q).start()
                else:
                    own_ref[...] = total

        @pl.when(s == N_DEV - 1)
        def _():
            for q in range(half - 2, half):
                send(q).wait_send()
            for k in range(half - 1):
                leave(k).wait()

    sums = lambda n: [jax.ShapeDtypeStruct((n,) + chunk, BF16)] if n else []
    sem = lambda n: pltpu.SemaphoreType.DMA((n,))
    res, per_comm = _carry(
        body, name=name, grid=(N_DEV,), comms=comms, prefetch=(core_chip,), in_specs=list(in_specs) + more["in_specs"],
        out_specs=[ANY] * (n_out - 2) + [pl.BlockSpec(chunk, lambda s, cc: (0, 0)), ANY] + more["out_specs"],
        out_shape=sums(n_sent) + sums(half - 1 - n_sent) + [jax.ShapeDtypeStruct(chunk, F32),
                                                            jax.ShapeDtypeStruct((half,) + chunk, F32)] + more["out_shape"],
        scratch_shapes=[pltpu.VMEM((2,) + chunk, F32), pltpu.VMEM(chunk, F32), sem(half), sem(half), sem(1),
                        pltpu.VMEM((half - 1,) + chunk, BF16), sem(half - 1), sem(half - 1), *scratch_shapes],
        args=list(args) + more["args"])
    first = res[0] if n_sent else None, res[n_out - 3] if n_sent < half - 1 else None
    return (*first, res[n_out - 2], per_comm, res[n_out:]) if rider else (*first, res[n_out - 2], per_comm)


def _resident(a):
    return pl.BlockSpec(a.shape, lambda s, cc: (0,) * a.ndim, pipeline_mode=pl.Buffered(1))


def _pair_sum_rider(parts, recv):
    _, r, c = parts.shape

    def body(cc_ref, ins, outs):
        (parts_ref, recv_ref), (wire_ref, own_ref) = ins, outs
        for q in range(N_DEV // 2):
            total = parts_ref[2 * q + cc_ref[0]] + recv_ref[q]
            wire_ref[q] = total.astype(BF16)

            @pl.when(cc_ref[1] == q)
            def _():
                own_ref[...] = total

    whole = lambda shape: pl.BlockSpec(shape, lambda s, cc: (0,) * len(shape))
    return dict(body=body, args=[parts, recv], in_specs=[_resident(parts), _resident(recv)],
                out_specs=[whole((N_DEV // 2, r, c)), whole((r, c))],
                out_shape=[jax.ShapeDtypeStruct((N_DEV // 2, r, c), BF16), jax.ShapeDtypeStruct((r, c), F32)])


def _ffn_wgrad(name, lhs, rhs, chunk_lhs, core_chip, n_sent, comms=(), rider=None):
    t_tok = lhs.shape[0]
    fc = (lhs if chunk_lhs else rhs).shape[1] // N_DEV
    chunked = pl.BlockSpec((t_tok, fc), lambda s, cc: (0, _wgrad_shard(s, cc)))

    def product(shard, lhs_ref, rhs_ref):
        return _dot(lhs_ref[...], rhs_ref[...], TN)

    return _wgrad_pair_sum(
        name, product, (fc, rhs.shape[1]) if chunk_lhs else (lhs.shape[1], fc),
        [chunked, _resident(rhs)] if chunk_lhs else [_resident(lhs), chunked], (lhs, rhs), core_chip, n_sent, comms,
        rider=rider)


def _proj_out_bwd(dz1, cat_t, w_out_b, comms=()):
    t_tok, d = dz1.shape
    d_mix = cat_t.shape[0]
    tm = min(512, t_tok)

    def body(dz1_ref, cat_ref, w_ref, dcat_ref, gw_ref):
        @pl.when(pl.program_id(0) == 0)
        def _():
            gw_ref[...] = jnp.zeros_like(gw_ref)

        dzb = dz1_ref[...].astype(BF16)
        dcat_ref[...] = _dot(w_ref[...], dzb, NT)
        gw_ref[...] += _dot(cat_ref[...], dzb)

    return _carry(
        body, name="proj_out_bwd", grid=(t_tok // tm,), comms=comms,
        in_specs=[pl.BlockSpec((tm, d), lambda i: (i, 0)), pl.BlockSpec((d_mix, tm), lambda i: (0, i)),
                  pl.BlockSpec((d_mix, d), lambda i: (0, 0))],
        out_specs=[pl.BlockSpec((d_mix, tm), lambda i: (0, i)), pl.BlockSpec((d_mix, d), lambda i: (0, 0))],
        out_shape=[jax.ShapeDtypeStruct((d_mix, t_tok), F32), jax.ShapeDtypeStruct((d_mix, d), F32)],
        args=(dz1, cat_t, w_out_b))


def _mixer_bwd(dcat_t, h_t, cos_t, sin_t, w_spatial, b_spatial, vln_g, vln_b, sinks, band_bias, lse, comms=()):
    t_tok = h_t.shape[1]
    nb, n_step = t_tok // BLK, t_tok // MIX_W
    group = N_HEADS // N_KV_HEADS

    def body(sinks_ref, dcat_ref, u_ref, vg_ref, q_ref, kvc_ref, kvp_ref, cos_ref, sin_ref, cosp_ref, sinp_ref,
             wsp_ref, bsp_ref, g_ref, b_ref, bias_ref, lse_ref, dh_ref, dkvc_ref, dkvp_ref, gwsb_ref, gbsp_ref, gvln_ref, gsink_ref,
             dg_acc, db_acc, wm_scr, gws_ref):
        i = pl.program_id(0)

        @pl.when(i == 0)
        def _():
            gws_ref[...] = jnp.zeros_like(gws_ref)
            gbsp_ref[...] = jnp.zeros_like(gbsp_ref)
            gsink_ref[...] = jnp.zeros_like(gsink_ref)
            dg_acc[...] = jnp.zeros_like(dg_acc)
            db_acc[...] = jnp.zeros_like(db_acc)

        _mask_w_once(wsp_ref, wm_scr)

        g = g_ref[...]
        ua, ua_grad = _gelu_and_grad(u_ref[...])
        vv, vv_grad = _gelu_and_grad(vg_ref[...])
        vp, vhat, rstd = _ln_fwd_t(vv, g, b_ref[...])
        vpb = vp.astype(BF16)
        da = dcat_ref[0:D_GMLP, :]
        dmixed = da * ua
        dvp_blocks = []
        for b in range(MIX_BLOCKS):
            dvp_parts = []
            for hh in range(N_HEADS):
                rows = slice(hh * HEAD_DIM, (hh + 1) * HEAD_DIM)
                vpb_h = vpb[rows, _cols(b)]
                mixed = _dot(vpb_h, wm_scr[hh], NT) + bsp_ref[hh:hh + 1, :]
                dh_ref[COL_U + hh * HEAD_DIM:COL_U + (hh + 1) * HEAD_DIM, _cols(b)] = (
                    da[rows, _cols(b)] * mixed * ua_grad[rows, _cols(b)]).astype(BF16)
                dm = dmixed[rows, _cols(b)]
                dmb = dm.astype(BF16)
                gbsp_ref[hh:hh + 1, :] += jnp.sum(dm, axis=0, keepdims=True)
                gws_ref[hh] += _dot(dmb, vpb_h, TN)
                dvp_parts.append(_dot(dmb, wm_scr[hh]))
            dvp_blocks.append(jnp.concatenate(dvp_parts, axis=0))
        dvp = jnp.concatenate(dvp_blocks, axis=1)
        dgv, dbv = dvp * vhat, dvp
        for b in range(MIX_BLOCKS):
            dg_acc[...] += dgv[:, _cols(b)]
            db_acc[...] += dbv[:, _cols(b)]
        dh_ref[COL_V:COL_V + D_GMLP, :] = (_ln_bwd_t(dvp, vhat, rstd, g) * vv_grad).astype(BF16)

        kvc, cos, sin = kvc_ref[...], cos_ref[...], sin_ref[...]
        qr = (_rope_t(q_ref[...], cos, sin) * SCORE_SCALE).astype(BF16)
        sinks4 = [_group_lanes([jnp.full((1, BLK), sinks_ref[hh], F32) for hh in range(kv * group, (kv + 1) * group)])
                  for kv in range(N_KV_HEADS)]
        dq_blocks, dkv_cur, dkv_prev = [], [], []
        for b in range(MIX_BLOCKS):
            kv_cur, kv_prev, cosc, sinc, cosp, sinp, bias1 = _block_inputs(b, i, kvc, kvp_ref, cos, sin, cosp_ref, sinp_ref, bias_ref)
            k_t, k_n, v_t = _keys_values(kv_cur, kv_prev, cosc, sinc, cosp, sinp)
            v_n = jnp.concatenate([kv_prev[D_KV:].T, kv_cur[D_KV:].T], axis=0).astype(BF16)
            bias = _group_lanes([bias1] * group)
            dk, dv, dq_parts = [], [], []
            for kv in range(N_KV_HEADS):
                heads = range(kv * group, (kv + 1) * group)
                kv_rows = slice(kv * HEAD_DIM, (kv + 1) * HEAD_DIM)
                qs = _group_lanes([qr[hh * HEAD_DIM:(hh + 1) * HEAD_DIM, _cols(b)] for hh in heads])
                dos = _group_lanes([dcat_ref[D_GMLP + hh * HEAD_DIM:D_GMLP + (hh + 1) * HEAD_DIM, _cols(b)]
                                    for hh in heads]).astype(BF16)
                lse_g = lse_ref[b * N_KV_HEADS + kv:b * N_KV_HEADS + kv + 1, :]
                p = jnp.exp(_dot(k_n, _pad_head(qs, kv)) + bias - lse_g)
                p_sink = jnp.exp(sinks4[kv] - lse_g)
                dp = _dot(v_n, _pad_head(dos, kv))
                delta = jnp.sum(p * dp, axis=0, keepdims=True)
                ds = (p * (dp - delta)).astype(BF16)
                dsink = p_sink * delta
                dq = _dot(k_t[kv_rows], ds) * SCORE_SCALE
                for j, hh in enumerate(heads):
                    gsink_ref[hh:hh + 1, :] -= dsink[:, j * BLK:(j + 1) * BLK]
                    dq_parts.append(dq[:, j * BLK:(j + 1) * BLK])
                dk.append(_dot(qs, ds, NT))
                dv.append(_dot(dos, p.astype(BF16), NT))
            dq_blocks.append(jnp.concatenate(dq_parts, axis=0))
            dk_all, dv_all = jnp.concatenate(dk, axis=0), jnp.concatenate(dv, axis=0)
            dkv_cur.append(jnp.concatenate([_rope_t(dk_all[:, BLK:], cosc, sinc, bwd=True), dv_all[:, BLK:]], axis=0))
            dkv_prev.append(jnp.concatenate([_rope_t(dk_all[:, :BLK], cosp, sinp, bwd=True), dv_all[:, :BLK]], axis=0))
        dh_ref[COL_Q:COL_Q + D_ATTN, :] = _rope_t(jnp.concatenate(dq_blocks, axis=1), cos, sin, bwd=True).astype(BF16)
        for b in range(MIX_BLOCKS):
            dkvc_ref[:, _cols(b)] = dkv_cur[b] + dkv_prev[b + 1] if b + 1 < MIX_BLOCKS else dkv_cur[b]
        dkvp_ref[...] = dkv_prev[0]

        @pl.when(i == n_step - 1)
        def _():
            causal = _causal()
            for hh in range(N_HEADS):
                gwsb_ref[hh] = jnp.where(causal, gws_ref[hh], 0.0).astype(BF16)
            gvln_ref[...] = jnp.zeros_like(gvln_ref)
            gvln_ref[0:1, :] = jnp.sum(dg_acc[...].T, axis=0, keepdims=True)
            gvln_ref[1:2, :] = jnp.sum(db_acc[...].T, axis=0, keepdims=True)

    full = lambda shape: pl.BlockSpec(shape, lambda i: (0,) * len(shape))
    return _carry(
        body, name="mixer_bwd", grid=(n_step,), comms=comms,
        in_specs=[pl.BlockSpec(memory_space=pltpu.SMEM), pl.BlockSpec((D_GMLP + D_ATTN, MIX_W), lambda i: (0, i))]
        + _h_specs() + _table_specs()
        + [full((N_HEADS, BLK, BLK)), full((N_HEADS, BLK)), full((D_GMLP, 1)), full((D_GMLP, 1)), BIAS_SPEC, LSE_SPEC],
        out_specs=[pl.BlockSpec((COL_K, MIX_W), lambda i: (0, i)), pl.BlockSpec((2 * D_KV, MIX_W), lambda i: (0, i)),
                   pl.BlockSpec((2 * D_KV, BLK), lambda i: (0, (i + n_step - 1) % n_step)),
                   full((N_HEADS, BLK, BLK)), full((N_HEADS, BLK)), full((8, D_GMLP)), full((N_HEADS, LANES))],
        out_shape=[jax.ShapeDtypeStruct((COL_K, t_tok), BF16), jax.ShapeDtypeStruct((2 * D_KV, t_tok), F32),
                   jax.ShapeDtypeStruct((2 * D_KV, n_step * BLK), F32),
                   jax.ShapeDtypeStruct((N_HEADS, BLK, BLK), BF16), jax.ShapeDtypeStruct((N_HEADS, BLK), F32),
                   jax.ShapeDtypeStruct((8, D_GMLP), F32), jax.ShapeDtypeStruct((N_HEADS, LANES), F32)],
        scratch_shapes=[pltpu.VMEM((D_GMLP, BLK), F32), pltpu.VMEM((D_GMLP, BLK), F32), pltpu.VMEM((N_HEADS, BLK, BLK), BF16),
                        pltpu.VMEM((N_HEADS, BLK, BLK), F32)],
        args=(sinks, dcat_t, h_t, h_t, h_t, h_t, h_t, cos_t, sin_t, cos_t, sin_t, w_spatial, b_spatial, vln_g, vln_b, band_bias, lse))


def _dkv_rows(dkvc_ref, dkvp_ref, width, store):
    for s in range(width // MIX_W):
        rest, last = slice(s * MIX_W, (s + 1) * MIX_W - BLK), slice((s + 1) * MIX_W - BLK, (s + 1) * MIX_W)
        store(rest, dkvc_ref[:, rest].astype(BF16))
        store(last, (dkvc_ref[:, last] + dkvp_ref[:, _cols(s)]).astype(BF16))


def _proj_in_wgrad(dh_b, dkvc_t, dkvp_t, xb, core_chip, comms=()):
    t_tok, d = xb.shape
    d_main, d_kv = dh_b.shape[0], dkvc_t.shape[0]
    rows = (d_main + d_kv) // N_DEV
    whole, cut = d_main // rows, d_main % rows

    def product(shard, dh_ref, dkvc_ref, dkvp_ref, xb_ref, dht_scr, sems):
        copies = [pltpu.make_async_copy(dh_ref.at[j * rows:(j + 1) * rows], dht_scr.at[j], sems.at[j]) for j in range(whole)]
        copies.append(pltpu.make_async_copy(dh_ref.at[whole * rows:d_main], dht_scr.at[whole, 0:cut], sems.at[whole]))

        @pl.when(pl.program_id(0) == 0)
        def _():
            for cp in copies:
                cp.start()

            def store(cols, val):
                dht_scr[whole, cut:rows, cols] = val[0:rows - cut]
                dht_scr[whole + 1, :, cols] = val[rows - cut:]

            _dkv_rows(dkvc_ref, dkvp_ref, t_tok, store)
            for cp in copies:
                cp.wait()

        return _dot(dht_scr[shard], xb_ref[...])

    return _wgrad_pair_sum(
        "proj_in_wgrad", product, (rows, d), [ANY, _resident(dkvc_t), _resident(dkvp_t), _resident(xb)],
        (dh_b, dkvc_t, dkvp_t, xb), core_chip, N_DEV // 2 - 1, comms,
        scratch_shapes=[pltpu.VMEM((N_DEV, rows, t_tok), BF16), pltpu.SemaphoreType.DMA((whole + 1,))])


def _proj_in_dgrad(dh_b, dkvc_t, dkvp_t, dz1, w_in_t, comms=()):
    t_tok, d = dz1.shape
    d_main, d_kv = dh_b.shape[0], dkvc_t.shape[0]
    tm = min(512, t_tok)

    def body(dh_ref, dkvc_ref, dkvp_ref, dz1_ref, w_ref, dx_ref, dkv_scr):
        def store(cols, val):
            dkv_scr[:, cols] = val

        _dkv_rows(dkvc_ref, dkvp_ref, tm, store)
        dx_ref[...] = (ALPHA * dz1_ref[...] + _dot(dh_ref[...], w_ref[0:d_main, :], TN)
                       + _dot(dkv_scr[...], w_ref[d_main:, :], TN))

    return _carry(
        body, name="proj_in_dgrad", grid=(t_tok // tm,), comms=comms,
        in_specs=[pl.BlockSpec((d_main, tm), lambda i: (0, i)), pl.BlockSpec((d_kv, tm), lambda i: (0, i)),
                  pl.BlockSpec((d_kv, tm // MIX_BLOCKS), lambda i: (0, i)),
                  pl.BlockSpec((tm, d), lambda i: (i, 0)), pl.BlockSpec((d_main + d_kv, d), lambda i: (0, 0))],
        out_specs=[pl.BlockSpec((tm, d), lambda i: (i, 0))],
        out_shape=[jax.ShapeDtypeStruct((t_tok, d), F32)],
        scratch_shapes=[pltpu.VMEM((d_kv, tm), BF16)],
        args=(dh_b, dkvc_t, dkvp_t, dz1, w_in_t))


def _adamw(w, g, m, v):
    m = ADAM_B1 * m + (1.0 - ADAM_B1) * g
    v = ADAM_B2 * v + (1.0 - ADAM_B2) * (g * g)
    m_hat = m / (1.0 - ADAM_B1 ** ADAM_STEP)
    v_hat = v / (1.0 - ADAM_B2 ** ADAM_STEP)
    delta = -ADAM_LR * (m_hat / (jnp.sqrt(v_hat) + ADAM_EPS) + ADAM_WD * w)
    return delta, m, v


ADAMW_STEPS = 4


def _adamw_shards(name, items, comms=(), rider=None):
    n_in, n_out = sum(4 + len(it[1]) for it in items), 4 * len(items)
    n_rin = len(rider["args"]) if rider else 0

    def body(*refs):
        ins, rins, outs, routs = refs[:n_in], refs[n_in:n_in + n_rin], refs[n_in + n_rin:n_in + n_rin + n_out], refs[n_in + n_rin + n_out:]
        for i, item in enumerate(items):
            (own_ref, w_ref, m_ref, v_ref), recv_refs, ins = ins[:4], ins[4:4 + len(item[1])], ins[4 + len(item[1]):]
            g = own_ref[...]
            for recv_ref in recv_refs:
                for k in range(recv_ref.shape[0]):
                    g = g + recv_ref[k].astype(F32)
            for o_ref, val in zip(outs[4 * i:4 * i + 4], (g,) + _adamw(w_ref[...], g, m_ref[...], v_ref[...])):
                o_ref[...] = val
        if rider:
            pl.when(pl.program_id(0) == 0)(lambda: rider["body"](rins, routs))

    in_specs, out_specs, out_shape, args = [], [], [], []
    for own, recvs, w, m, v in items:
        r, c = own.shape
        tiles = ADAMW_STEPS
        while (r // tiles) % BF16_ROWS:
            tiles //= 2
        blk = pl.BlockSpec((r // tiles, c), lambda s, k=ADAMW_STEPS // tiles: (s // k, 0))
        in_specs += [blk] * 4 + [pl.BlockSpec((a.shape[0], r // tiles, c), lambda s, k=ADAMW_STEPS // tiles: (0, s // k, 0))
                                 for a in recvs]
        out_specs += [blk] * 4
        out_shape += [jax.ShapeDtypeStruct((r, c), F32)] * 4
        args += [own, w, m, v, *recvs]
    if rider:
        in_specs, out_specs = in_specs + rider["in_specs"], out_specs + rider["out_specs"]
        out_shape, args = out_shape + rider["out_shape"], args + rider["args"]
    res, per_comm = _carry(body, name=name, grid=(ADAMW_STEPS,), comms=comms, in_specs=in_specs, out_specs=out_specs,
                           out_shape=out_shape, args=args)
    return [res[4 * i:4 * i + 4] for i in range(len(items))], res[n_out:], per_comm


VEC_VLN, VEC_LN1G, VEC_LN1B, VEC_LN2G, VEC_LN2B, VEC_SINK, VEC_LOSS, VEC_BSP, VEC_ROWS = 0, 1, 2, 3, 4, 5, 6, 8, 16


def _adamw_small(parts_w, parts_vec, params):
    n = parts_w.shape[0]
    flat = [a for p in params for a in p]
    shapes = [p[0].shape for p in params]

    def grads(gw, gv):
        return [gw, gv[VEC_VLN:VEC_VLN + 1, 0:D_GMLP], gv[VEC_VLN:VEC_VLN + 1, D_GMLP:2 * D_GMLP],
                gv[VEC_BSP:VEC_BSP + N_HEADS, 0:BLK], gv[VEC_LN1G:VEC_LN1G + 1], gv[VEC_LN1B:VEC_LN1B + 1],
                gv[VEC_LN2G:VEC_LN2G + 1], gv[VEC_LN2B:VEC_LN2B + 1], gv[VEC_SINK:VEC_SINK + 1, 0:N_HEADS]]

    def body(ins, outs):
        (pw_ref, pv_ref), ins = ins[:2], ins[2:]
        gw, gv = pw_ref[0].astype(F32), pv_ref[0]
        for k in range(1, n):
            gw, gv = gw + pw_ref[k].astype(F32), gv + pv_ref[k]
        for i, g in enumerate(grads(gw, gv)):
            w_ref, m_ref, v_ref = ins[3 * i:3 * i + 3]
            delta, m_new, v_new = _adamw(w_ref[...], g, m_ref[...], v_ref[...])
            for o_ref, val in zip(outs[4 * i:4 * i + 4], (g, delta, m_new, v_new)):
                o_ref[...] = val
        outs[-1][...] = gv[VEC_LOSS:VEC_LOSS + 1, 0:LANES]

    whole = lambda shape, **kw: pl.BlockSpec(shape, lambda i: (0,) * len(shape), **kw)
    once = dict(pipeline_mode=pl.Buffered(1))
    return dict(
        body=body, args=[parts_w, parts_vec, *flat],
        in_specs=[whole(parts_w.shape, **once), whole(parts_vec.shape, **once)] + [whole(a.shape, **once) for a in flat],
        out_specs=[whole(s) for s in shapes for _ in range(4)] + [whole((1, LANES))],
        out_shape=[jax.ShapeDtypeStruct(s, F32) for s in shapes for _ in range(4)] + [jax.ShapeDtypeStruct((1, LANES), F32)])


def _pair_sum(name, parts, recv, core_chip, comms=()):
    _, r, c = parts.shape
    tr = r if r <= 512 else 512

    def body(cc_ref, a_ref, b_ref, wire_ref, own_ref):
        s = a_ref[...] + b_ref[...]
        wire_ref[...] = s.astype(BF16)

        @pl.when(pl.program_id(1) == cc_ref[1])
        def _():
            own_ref[...] = s

    return _carry(
        body, name=name, grid=(r // tr, 4), prefetch=(core_chip,), comms=comms,
        in_specs=[pl.BlockSpec((None, tr, c), lambda i, q, cc: (2 * q + cc[0], i, 0)),
                  pl.BlockSpec((None, tr, c), lambda i, q, cc: (q, i, 0))],
        out_specs=[pl.BlockSpec((None, tr, c), lambda i, q, cc: (q, i, 0)), pl.BlockSpec((tr, c), lambda i, q, cc: (i, 0))],
        out_shape=[jax.ShapeDtypeStruct((4, r, c), BF16), jax.ShapeDtypeStruct((r, c), F32)],
        args=(parts, recv))


def kernel(x, positions, w_in, v_ln_g, v_ln_b, w_spatial, b_spatial, sinks, w_out, ln1_g, ln1_b, w_ff1, w_ff2, ln2_g, ln2_b, loss_target, m_w_in, m_v_ln_g, m_v_ln_b, m_w_spatial, m_b_spatial, m_sinks, m_w_out, m_ln1_g, m_ln1_b, m_w_ff1, m_w_ff2, m_ln2_g, m_ln2_b, v_w_in, v_v_ln_g, v_v_ln_b, v_w_spatial, v_b_spatial, v_sinks, v_w_out, v_ln1_g, v_ln1_b, v_w_ff1, v_w_ff2, v_ln2_g, v_ln2_b):
    _, t_tok, d = x.shape
    xi, yi, ci = _place()
    core_chip = jnp.stack([ci, 2 * xi + yi]).astype(jnp.int32)
    x2 = x.reshape(t_tok, d)
    target = loss_target.reshape(t_tok, d)
    inv_freq = ROPE_THETA ** (-jnp.arange(0, HEAD_DIM, 2, dtype=F32) / HEAD_DIM)
    wsp, bsp, sink_vec = w_spatial[0], b_spatial[0], sinks[0]
    vg_col, vb_col = v_ln_g.reshape(D_GMLP, 1), v_ln_b.reshape(D_GMLP, 1)
    big = {"in": w_in[0], "out": w_out[0], "ff1": w_ff1[0], "ff2": w_ff2[0]}
    half1, half2 = big["ff1"].shape[1] // 2, big["ff2"].shape[0] // 2
    w1_mine = [big["ff1"][:, :half1].astype(BF16), big["ff1"][:, half1:].astype(BF16)]
    w2_mine = [big["ff2"][:half2].astype(BF16), big["ff2"][half2:].astype(BF16)]

    (cos_t, sin_t), ((g_in,),) = _rope_tables(
        positions, jnp.tile(inv_freq, 2).reshape(HEAD_DIM, 1), comms=[_gather_comm([big["in"].T.astype(BF16)])])
    w_in_t = g_in.reshape(D_IN, d)
    (h_t, xb), ((g_out, w1_a),) = _proj_in(x2, w_in_t, comms=[_gather_comm([big["out"].astype(BF16), w1_mine[0]])])
    w_out_b = g_out.reshape(-1, d)
    band_bias = _band_bias()
    (cat_t, lse), ((w1_b, w2_a),) = _mixer_fwd(h_t, cos_t, sin_t, wsp, bsp, vg_col, vb_col, sink_vec, band_bias,
                                                comms=[_gather_comm([w1_mine[1], w2_mine[0]])])
    (xhat1, rstd1, x1b), ((w2_b,),) = _proj_out(cat_t, x2, w_out_b, ln1_g, ln1_b, comms=[_gather_comm([w2_mine[1]])])
    act_b, dpre_b, dz2b, dz1, stats = _ffn_fwd_bwd(xhat1, rstd1, x1b, target, [w1_a, w1_b], [w2_a, w2_b], ln1_g, ln1_b, ln2_g, ln2_b)

    (dcat_t, gw_out), _ = _proj_out_bwd(dz1, cat_t, w_out_b)
    p_out = gw_out.reshape(N_DEV, -1, d)
    r_ff1_a, wire_ff1, own_ff1, ((s_out,),) = _ffn_wgrad(
        "ffn_wgrad1", x1b, dpre_b, False, core_chip, 1, comms=[_sibling_comm([p_out])])
    _, wire_ff2, own_ff2, ((r_ff1_b,),), (wire_out, own_out) = _ffn_wgrad(
        "ffn_wgrad2", act_b, dz2b, True, core_chip, 0, comms=[_flips_comm(wire_ff1, 1)],
        rider=_pair_sum_rider(p_out, s_out))
    (dh_b, dkvc_t, dkvp_t, g_wsp, g_bsp, g_vln, g_sink), ((r_ff2,), (r_out,)) = _mixer_bwd(
        dcat_t, h_t, cos_t, sin_t, wsp, bsp, vg_col, vb_col, sink_vec, band_bias, lse,
        comms=[_flips_comm(wire_ff2, 0), _chips_comm([wire_out])])
    sink_row = jnp.pad(g_sink.sum(axis=1).reshape(1, N_HEADS), ((0, 0), (0, d - N_HEADS)))
    small_vec = jnp.concatenate([g_vln[0:2].reshape(1, d), stats[0:4], sink_row, stats[4:5], jnp.zeros((1, d), F32),
                                 jnp.pad(g_bsp, ((0, 0), (0, d - BLK)))], axis=0)
    r_in, _, own_in, ((parts_w, parts_vec),) = _proj_in_wgrad(
        dh_b, dkvc_t, dkvp_t, xb, core_chip, comms=[_gather_comm([g_wsp.reshape(-1, BLK), small_vec])])
    (grad_x,), _ = _proj_in_dgrad(dh_b, dkvc_t, dkvp_t, dz1, w_in_t)
    small = [(w_spatial, m_w_spatial, v_w_spatial), (v_ln_g, m_v_ln_g, v_v_ln_g), (v_ln_b, m_v_ln_b, v_v_ln_b),
             (b_spatial, m_b_spatial, v_b_spatial), (ln1_g, m_ln1_g, v_ln1_g), (ln1_b, m_ln1_b, v_ln1_b),
             (ln2_g, m_ln2_g, v_ln2_g), (ln2_b, m_ln2_b, v_ln2_b), (sinks, m_sinks, v_sinks)]
    views = [(-1, BLK), None, None, (N_HEADS, BLK)] + [None] * 5
    small_update = _adamw_small(parts_w, parts_vec, [
        tuple(a if vw is None else a.reshape(vw) for a in p) for p, vw in zip(small, views)])
    (out_out, ff1_out, ff2_out, in_out_t), small_res, _ = _adamw_shards("adamw_all", [
        (own_out, [r_out], big["out"], m_w_out[0], v_w_out[0]),
        (own_ff1, [r_ff1_a, r_ff1_b], big["ff1"], m_w_ff1[0], v_w_ff1[0]),
        (own_ff2, [r_ff2], big["ff2"], m_w_ff2[0], v_w_ff2[0]),
        (own_in, [r_in], big["in"].T, m_w_in[0].T, v_w_in[0].T)], rider=small_update)
    in_out = [o.T for o in in_out_t]
    small_out = [[o.reshape(p[0].shape) for o in small_res[4 * i:4 * i + 4]] for i, p in enumerate(small)]
    loss = small_res[-1][0, 0]

    big_out = {0: in_out, 6: out_out, 9: ff1_out, 10: ff2_out}
    small_slot = {3: 0, 1: 1, 2: 2, 4: 3, 7: 4, 8: 5, 11: 6, 12: 7, 5: 8}
    outs = [loss, grad_x.reshape(x.shape)]
    for kind in range(4):
        for wi in range(13):
            outs.append(big_out[wi][kind][None] if wi in big_out else small_out[small_slot[wi]][kind])
    return tuple(outs)
```

```python
import math

import jax
import jax.numpy as jnp
from jax import lax
from jax.experimental import pallas as pl
from jax.experimental.pallas import tpu as pltpu

F32 = jnp.float32
BF16 = jnp.bfloat16
MESH = pl.DeviceIdType.MESH

HEAD_DIM = 64
N_HEADS = 8
N_KV_HEADS = 2
BLK = 128
D_GMLP = N_HEADS * HEAD_DIM
D_ATTN = N_HEADS * HEAD_DIM
D_KV = N_KV_HEADS * HEAD_DIM
D_IN = 2 * D_GMLP + D_ATTN + 2 * D_KV
COL_U, COL_V, COL_Q, COL_K = 0, D_GMLP, 2 * D_GMLP, 2 * D_GMLP + D_ATTN
ROPE_THETA = 10000.0
LN_EPS = 1e-5
ALPHA = 2.0 ** 0.25
NEG_INF = -1e30
SCORE_SCALE = 1.0 / math.sqrt(HEAD_DIM)
ADAM_LR, ADAM_B1, ADAM_B2, ADAM_EPS, ADAM_WD, ADAM_STEP = 0.001, 0.9, 0.999, 1e-08, 0.01, 10
N_DEV = 8
LANES = 128
VMEM_LIMIT = 56 * 1024 * 1024
FFN_ROWS = 256

NT = (((1,), (1,)), ((), ()))
TN = (((0,), (0,)), ((), ()))


def _params(*sem):
    return pltpu.CompilerParams(dimension_semantics=sem, vmem_limit_bytes=VMEM_LIMIT)


def _dot(a, b, dims=None):
    if dims is None:
        return jnp.dot(a, b, preferred_element_type=F32)
    return lax.dot_general(a, b, dims, preferred_element_type=F32)


def _mean(a):
    return jnp.mean(a, axis=-1, keepdims=True)


def _ln_fwd(z, g, b):
    zc = z - _mean(z)
    rstd = lax.rsqrt(_mean(zc * zc) + LN_EPS)
    xhat = zc * rstd
    return xhat * g + b, xhat, rstd


def _ln_bwd(dy, xhat, rstd, g):
    dxhat = dy * g
    return rstd * (dxhat - _mean(dxhat) - xhat * _mean(dxhat * xhat))


_GELU_C = math.sqrt(2.0 / math.pi)


def _gelu(x):
    t = jnp.tanh(_GELU_C * (x + 0.044715 * (x * x * x)))
    return 0.5 * x * (1.0 + t)


def _gelu_and_grad(x):
    x2 = x * x
    t = jnp.tanh(_GELU_C * (x + 0.044715 * (x2 * x)))
    hx, ht = 0.5 * x, 0.5 * (1.0 + t)
    return x * ht, ht + hx * (1.0 - t * t) * (_GELU_C * (1.0 + 3.0 * 0.044715 * x2))


def _mean0(a):
    return jnp.mean(a, axis=0, keepdims=True)


def _ln_fwd_t(z, g, b):
    zc = z - _mean0(z)
    rstd = lax.rsqrt(_mean0(zc * zc) + LN_EPS)
    xhat = zc * rstd
    return xhat * g + b, xhat, rstd


def _ln_bwd_t(dy, xhat, rstd, g):
    dxhat = dy * g
    return rstd * (dxhat - _mean0(dxhat) - xhat * _mean0(dxhat * xhat))


def _rope_t(t, cos, sin_signed, bwd=False):
    half = HEAD_DIM // 2
    outs = []
    for r in range(0, t.shape[0], HEAD_DIM):
        th = t[r:r + HEAD_DIM]
        sw = jnp.concatenate([th[half:], th[:half]], axis=0) * sin_signed
        outs.append(th * cos - sw if bwd else th * cos + sw)
    return jnp.concatenate(outs, axis=0)


ANY = pl.BlockSpec(memory_space=pl.ANY)
GATHER_PIECES = 2
BF16_ROWS = 16


def _place():
    return lax.axis_index("x"), lax.axis_index("y"), lax.axis_index("c")


class _Comm:
    def __init__(self, ins, outs, sems, start, finish):
        self.ins, self.outs, self.sems, self.start, self.finish = ins, outs, sems, start, finish


def _gather_comm(arrs):
    n = len(arrs)
    pieces = []
    for a, arr in enumerate(arrs):
        k = GATHER_PIECES
        while arr.shape[0] % (k * BF16_ROWS):
            k //= 2
        pieces += [(a, p * (arr.shape[0] // k), arr.shape[0] // k) for p in range(k)]

    def parts(ins, outs, sems):
        send_sems, recv_sems, local_sems = sems
        x, y, c = _place()
        me, sibling = (x, y, c), (x, y, 1 - c)
        chips = [(1 - x, y), (x, 1 - y), (1 - x, 1 - y)]

        def copy(u, k, block, to, local=False):
            a, r0, nr = pieces[u]
            px, py, pc = block
            dst = outs[a].at[4 * px + 2 * py + pc, pl.ds(r0, nr)]
            return pltpu.make_async_remote_copy(
                src_ref=ins[a].at[pl.ds(r0, nr)] if local else dst, dst_ref=dst,
                send_sem=send_sems.at[u, k], recv_sem=recv_sems.at[u, k], device_id=to, device_id_type=MESH)

        mine = [pltpu.make_async_copy(ins[a], outs[a].at[4 * x + 2 * y + c], local_sems.at[a]) for a in range(n)]
        first = []
        for u in range(len(pieces)):
            first.append(copy(u, 0, me, sibling, local=True))
            first += [copy(u, 1 + j, me, (*chip, c), local=True) for j, chip in enumerate(chips)]
        return copy, mine, first, me, sibling, chips, c

    def start(ins, outs, sems):
        _, mine, first, *_ = parts(ins, outs, sems)
        for cp in mine + first:
            cp.start()

    def finish(ins, outs, sems):
        copy, mine, first, me, sibling, chips, c = parts(ins, outs, sems)
        passed = []
        for u in range(len(pieces)):
            for j, chip in enumerate(chips):
                copy(u, 1 + j, (*chip, c), me).wait_recv()
                fwd = copy(u, 4 + j, (*chip, c), sibling)
                fwd.start()
                passed.append(fwd)
        for u in range(len(pieces)):
            copy(u, 0, sibling, me).wait_recv()
            for j, chip in enumerate(chips):
                copy(u, 4 + j, (*chip, 1 - c), me).wait_recv()
        for cp in first + passed:
            cp.wait_send()
        for cp in mine:
            cp.wait()

    return _Comm(list(arrs), [jax.ShapeDtypeStruct((N_DEV,) + a.shape, a.dtype) for a in arrs],
                 [pltpu.SemaphoreType.DMA((len(pieces), 7)), pltpu.SemaphoreType.DMA((len(pieces), 7)),
                  pltpu.SemaphoreType.DMA((n,))], start, finish)


def _sibling_comm(parts):
    n = len(parts)

    def copies(ins, outs, sems):
        x, y, c = _place()
        return [pltpu.make_async_remote_copy(
            src_ref=ins[a].at[2 * q + (1 - c)], dst_ref=outs[a].at[q],
            send_sem=sems[0].at[a, q], recv_sem=sems[1].at[a, q],
            device_id=(x, y, 1 - c), device_id_type=MESH) for a in range(n) for q in range(4)]

    return _Comm(list(parts), [jax.ShapeDtypeStruct((4,) + p.shape[1:], p.dtype) for p in parts],
                 [pltpu.SemaphoreType.DMA((n, 4)), pltpu.SemaphoreType.DMA((n, 4))],
                 lambda *r: [cp.start() for cp in copies(*r)], lambda *r: [cp.wait() for cp in copies(*r)])


def _chips_comm(chip_parts, rows=None):
    n = len(chip_parts)
    r0, nr = (0, None) if rows is None else rows

    def copies(ins, outs, sems):
        x, y, c = _place()
        chips = [(1 - x, y), (x, 1 - y), (1 - x, 1 - y)]
        src = lambda a, q: ins[a].at[q] if rows is None else ins[a].at[q, pl.ds(r0, nr)]
        return [pltpu.make_async_remote_copy(
            src_ref=src(a, 2 * px + py), dst_ref=outs[a].at[k],
            send_sem=sems[0].at[a, k], recv_sem=sems[1].at[a, k],
            device_id=(px, py, c), device_id_type=MESH) for a in range(n) for k, (px, py) in enumerate(chips)]

    shape = lambda p: (3,) + p.shape[1:] if rows is None else (3, nr) + p.shape[2:]
    return _Comm(list(chip_parts), [jax.ShapeDtypeStruct(shape(p), p.dtype) for p in chip_parts],
                 [pltpu.SemaphoreType.DMA((n, 3)), pltpu.SemaphoreType.DMA((n, 3))],
                 lambda *r: [cp.start() for cp in copies(*r)], lambda *r: [cp.wait() for cp in copies(*r)])


def _flips_comm(sums, first):
    m = sums.shape[0]

    def copies(ins, outs, sems):
        return [pltpu.make_async_remote_copy(
            src_ref=ins[0].at[j], dst_ref=outs[0].at[j], send_sem=sems[0].at[j], recv_sem=sems[1].at[j],
            device_id=_flipped(first + j), device_id_type=MESH) for j in range(m)]

    return _Comm([sums], [jax.ShapeDtypeStruct(sums.shape, sums.dtype)],
                 [pltpu.SemaphoreType.DMA((m,)), pltpu.SemaphoreType.DMA((m,))],
                 lambda *r: [cp.start() for cp in copies(*r)], lambda *r: [cp.wait() for cp in copies(*r)])


def _carry(body, *, name, grid, in_specs, out_specs, out_shape, args, comms=(), scratch_shapes=(), prefetch=()):
    n_pre, n_in, n_out, n_scr = len(prefetch), len(in_specs), len(out_specs), len(scratch_shapes)
    c_ins = [a for cm in comms for a in cm.ins]
    c_outs = [s for cm in comms for s in cm.outs]
    c_sems = [s for cm in comms for s in cm.sems]

    def wrapped(*refs):
        pre, refs = refs[:n_pre], refs[n_pre:]
        ins, refs = refs[:n_in], refs[n_in:]
        cins, refs = refs[:len(c_ins)], refs[len(c_ins):]
        outs, refs = refs[:n_out], refs[n_out:]
        couts, refs = refs[:len(c_outs)], refs[len(c_outs):]
        scr, sems = refs[:n_scr], refs[n_scr:]
        groups, i0, o0, s0 = [], 0, 0, 0
        for cm in comms:
            groups.append((cm, cins[i0:i0 + len(cm.ins)], couts[o0:o0 + len(cm.outs)], sems[s0:s0 + len(cm.sems)]))
            i0, o0, s0 = i0 + len(cm.ins), o0 + len(cm.outs), s0 + len(cm.sems)
        first = pl.program_id(0) == 0
        last = pl.program_id(0) == grid[0] - 1
        for ax in range(1, len(grid)):
            first = first & (pl.program_id(ax) == 0)
            last = last & (pl.program_id(ax) == grid[ax] - 1)
        if comms:
            @pl.when(first)
            def _():
                for cm, ci, co, cs in groups:
                    cm.start(ci, co, cs)
        body(*pre, *ins, *outs, *scr)
        if comms:
            @pl.when(last)
            def _():
                for cm, ci, co, cs in groups:
                    cm.finish(ci, co, cs)

    grid_spec = pltpu.PrefetchScalarGridSpec(
        num_scalar_prefetch=n_pre, grid=grid,
        in_specs=list(in_specs) + [ANY] * len(c_ins), out_specs=list(out_specs) + [ANY] * len(c_outs),
        scratch_shapes=list(scratch_shapes) + c_sems)
    res = pl.pallas_call(
        wrapped, name=name, grid_spec=grid_spec, out_shape=list(out_shape) + c_outs,
        compiler_params=_params(*(["arbitrary"] * len(grid))),
    )(*prefetch, *args, *c_ins)
    outs, rest, per_comm = res[:n_out], res[n_out:], []
    for cm in comms:
        per_comm.append(rest[:len(cm.outs)])
        rest = rest[len(cm.outs):]
    return outs, per_comm


def _rope_tables(pos_row, inv_freq_col, comms=()):
    t_tok = pos_row.shape[1]
    tm = min(512, t_tok)

    def body(pos_ref, invf_ref, cos_ref, sin_ref):
        ang = pos_ref[...].astype(F32) * invf_ref[...]
        row = lax.broadcasted_iota(jnp.int32, ang.shape, 0)
        cos_ref[...] = jnp.cos(ang)
        sin_ref[...] = jnp.sin(ang) * jnp.where(row < HEAD_DIM // 2, -1.0, 1.0)

    return _carry(
        body, name="rope_tables", grid=(t_tok // tm,), comms=comms,
        in_specs=[pl.BlockSpec((1, tm), lambda i: (0, i)), pl.BlockSpec((HEAD_DIM, 1), lambda i: (0, 0))],
        out_specs=[pl.BlockSpec((HEAD_DIM, tm), lambda i: (0, i))] * 2,
        out_shape=[jax.ShapeDtypeStruct((HEAD_DIM, t_tok), F32)] * 2,
        args=(pos_row, inv_freq_col))


def _proj_in(x2, w_in_t, comms=()):
    t_tok, d = x2.shape
    d_in = w_in_t.shape[0]
    tm = min(512, t_tok)

    def body(x_ref, w_ref, h_ref, xb_ref):
        xb = x_ref[...].astype(BF16)
        xb_ref[...] = xb
        h_ref[...] = _dot(w_ref[...], xb, NT)

    return _carry(
        body, name="proj_in", grid=(t_tok // tm,), comms=comms,
        in_specs=[pl.BlockSpec((tm, d), lambda i: (i, 0)), pl.BlockSpec((d_in, d), lambda i: (0, 0))],
        out_specs=[pl.BlockSpec((d_in, tm), lambda i: (0, i)), pl.BlockSpec((tm, d), lambda i: (i, 0))],
        out_shape=[jax.ShapeDtypeStruct((d_in, t_tok), F32), jax.ShapeDtypeStruct((t_tok, d), BF16)],
        args=(x2, w_in_t))


MIX_BLOCKS = 2
MIX_W = MIX_BLOCKS * BLK


def _prev_block(i):
    return jnp.maximum(MIX_BLOCKS * i - 1, 0)


def _h_specs():
    kv_row = COL_K // (2 * D_KV)
    return [
        pl.BlockSpec((D_GMLP, MIX_W), lambda i: (0, i)),
        pl.BlockSpec((D_GMLP, MIX_W), lambda i: (1, i)),
        pl.BlockSpec((D_ATTN, MIX_W), lambda i: (2, i)),
        pl.BlockSpec((2 * D_KV, MIX_W), lambda i: (kv_row, i)),
        pl.BlockSpec((2 * D_KV, BLK), lambda i: (kv_row, _prev_block(i))),
    ]


def _table_specs():
    return [
        pl.BlockSpec((HEAD_DIM, MIX_W), lambda i: (0, i)),
        pl.BlockSpec((HEAD_DIM, MIX_W), lambda i: (0, i)),
        pl.BlockSpec((HEAD_DIM, BLK), lambda i: (0, _prev_block(i))),
        pl.BlockSpec((HEAD_DIM, BLK), lambda i: (0, _prev_block(i))),
    ]


def _cols(b):
    return slice(b * BLK, (b + 1) * BLK)


LSE_ROWS = 8
LSE_SPEC = pl.BlockSpec((LSE_ROWS, D_ATTN), lambda i: (i, 0))


def _block_inputs(b, i, kvc, kvp_ref, cos, sin, cosp_ref, sinp_ref, bias_ref):
    if b == 0:
        kv_prev, cos_prev, sin_prev, bias = kvp_ref[...], cosp_ref[...], sinp_ref[...], bias_ref[jnp.minimum(i, 1)]
    else:
        kv_prev, cos_prev, sin_prev, bias = kvc[:, _cols(b - 1)], cos[:, _cols(b - 1)], sin[:, _cols(b - 1)], bias_ref[1]
    return kvc[:, _cols(b)], kv_prev, cos[:, _cols(b)], sin[:, _cols(b)], cos_prev, sin_prev, bias


def _band_bias():
    ki = lax.broadcasted_iota(jnp.int32, (2, 2 * BLK, BLK), 1)
    qi = lax.broadcasted_iota(jnp.int32, (2, 2 * BLK, BLK), 2)
    later = lax.broadcasted_iota(jnp.int32, (2, 2 * BLK, BLK), 0) > 0
    dist = qi + BLK - ki
    return jnp.where((dist >= 0) & (dist < BLK) & ((ki >= BLK) | later), 0.0, NEG_INF).astype(F32)


BIAS_SPEC = pl.BlockSpec((2, 2 * BLK, BLK), lambda i: (0, 0, 0))


def _keys_values(kvc, kvp, cosc, sinc, cosp, sinp):
    kp, kc = _rope_t(kvp[:D_KV], cosp, sinp), _rope_t(kvc[:D_KV], cosc, sinc)
    k_t = jnp.concatenate([kp, kc], axis=1).astype(BF16)
    k_n = jnp.concatenate([kp.T, kc.T], axis=0).astype(BF16)
    v_t = jnp.concatenate([kvp[D_KV:], kvc[D_KV:]], axis=1).astype(BF16)
    return k_t, k_n, v_t


def _pad_head(th, kv):
    z = jnp.zeros_like(th)
    return jnp.concatenate([th, z] if kv == 0 else [z, th], axis=0)


def _group_lanes(parts):
    return jnp.concatenate(parts, axis=1)


def _softmax_sink_t(s, sink):
    m = jnp.maximum(jnp.max(s, axis=0, keepdims=True), sink)
    e = jnp.exp(s - m)
    denom = jnp.sum(e, axis=0, keepdims=True) + jnp.exp(sink - m)
    return e * (1.0 / denom), m + jnp.log(denom)


def _causal():
    row = lax.broadcasted_iota(jnp.int32, (BLK, BLK), 0)
    col = lax.broadcasted_iota(jnp.int32, (BLK, BLK), 1)
    return row >= col


def _mask_w_once(wsp_ref, wm_scr):
    @pl.when(pl.program_id(0) == 0)
    def _():
        causal = _causal()
        for hh in range(N_HEADS):
            wm_scr[hh] = jnp.where(causal, wsp_ref[hh], 0.0).astype(BF16)


def _mixer_fwd(h_t, cos_t, sin_t, w_spatial, b_spatial, vln_g, vln_b, sinks, band_bias, comms=()):
    t_tok = h_t.shape[1]
    group = N_HEADS // N_KV_HEADS

    def body(sinks_ref, u_ref, vg_ref, q_ref, kvc_ref, kvp_ref, cos_ref, sin_ref, cosp_ref, sinp_ref,
             wsp_ref, bsp_ref, g_ref, b_ref, bias_ref, cat_ref, lse_ref, wm_scr):
        i = pl.program_id(0)
        _mask_w_once(wsp_ref, wm_scr)
        lse_ref[...] = jnp.zeros_like(lse_ref)
        ua = _gelu(u_ref[...])
        vp, _, _ = _ln_fwd_t(_gelu(vg_ref[...]), g_ref[...], b_ref[...])
        vpb = vp.astype(BF16)
        for b in range(MIX_BLOCKS):
            for hh in range(N_HEADS):
                rows = slice(hh * HEAD_DIM, (hh + 1) * HEAD_DIM)
                mixed = _dot(vpb[rows, _cols(b)], wm_scr[hh], NT) + bsp_ref[hh:hh + 1, :]
                cat_ref[rows, _cols(b)] = (ua[rows, _cols(b)] * mixed).astype(BF16)

        kvc, cos, sin = kvc_ref[...], cos_ref[...], sin_ref[...]
        qr = (_rope_t(q_ref[...], cos, sin) * SCORE_SCALE).astype(BF16)
        sinks4 = [_group_lanes([jnp.full((1, BLK), sinks_ref[hh], F32) for hh in range(kv * group, (kv + 1) * group)])
                  for kv in range(N_KV_HEADS)]
        for b in range(MIX_BLOCKS):
            kv_cur, kv_prev, cosc, sinc, cosp, sinp, bias1 = _block_inputs(b, i, kvc, kvp_ref, cos, sin, cosp_ref, sinp_ref, bias_ref)
            _, k_n, v_t = _keys_values(kv_cur, kv_prev, cosc, sinc, cosp, sinp)
            bias = _group_lanes([bias1] * group)
            for kv in range(N_KV_HEADS):
                heads = range(kv * group, (kv + 1) * group)
                qs = _group_lanes([qr[hh * HEAD_DIM:(hh + 1) * HEAD_DIM, _cols(b)] for hh in heads])
                p, lse = _softmax_sink_t(_dot(k_n, _pad_head(qs, kv)) + bias, sinks4[kv])
                lse_ref[b * N_KV_HEADS + kv:b * N_KV_HEADS + kv + 1, :] = lse
                o = _dot(v_t[kv * HEAD_DIM:(kv + 1) * HEAD_DIM], p.astype(BF16)).astype(BF16)
                for j, hh in enumerate(heads):
                    cat_ref[D_GMLP + hh * HEAD_DIM:D_GMLP + (hh + 1) * HEAD_DIM, _cols(b)] = o[:, j * BLK:(j + 1) * BLK]

    full = lambda shape: pl.BlockSpec(shape, lambda i: (0,) * len(shape))
    return _carry(
        body, name="mixer_fwd", grid=(t_tok // MIX_W,), comms=comms,
        in_specs=[pl.BlockSpec(memory_space=pltpu.SMEM)] + _h_specs() + _table_specs() + [
            full((N_HEADS, BLK, BLK)), full((N_HEADS, BLK)), full((D_GMLP, 1)), full((D_GMLP, 1)), BIAS_SPEC],
        out_specs=[pl.BlockSpec((D_GMLP + D_ATTN, MIX_W), lambda i: (0, i)), LSE_SPEC],
        out_shape=[jax.ShapeDtypeStruct((D_GMLP + D_ATTN, t_tok), BF16),
                   jax.ShapeDtypeStruct((t_tok // MIX_W * LSE_ROWS, D_ATTN), F32)],
        scratch_shapes=[pltpu.VMEM((N_HEADS, BLK, BLK), BF16)],
        args=(sinks, h_t, h_t, h_t, h_t, h_t, cos_t, sin_t, cos_t, sin_t, w_spatial, b_spatial, vln_g, vln_b, band_bias))


def _proj_out(cat_t, x2, w_out_b, ln1_g, ln1_b, comms=()):
    t_tok, d = x2.shape
    tm = min(512, t_tok)

    def body(cat_ref, x_ref, w_ref, g_ref, b_ref, xhat_ref, rstd_ref, x1b_ref):
        x1, xhat, rstd = _ln_fwd(ALPHA * x_ref[...] + _dot(cat_ref[...], w_ref[...], TN), g_ref[...], b_ref[...])
        xhat_ref[...] = xhat
        rstd_ref[...] = rstd
        x1b_ref[...] = x1.astype(BF16)

    tok = lambda w: pl.BlockSpec((tm, w), lambda i: (i, 0))
    vec = pl.BlockSpec((1, d), lambda i: (0, 0))
    return _carry(
        body, name="proj_out", grid=(t_tok // tm,), comms=comms,
        in_specs=[pl.BlockSpec((cat_t.shape[0], tm), lambda i: (0, i)), tok(d), pl.BlockSpec(w_out_b.shape, lambda i: (0, 0)), vec, vec],
        out_specs=[tok(d), tok(1), tok(d)],
        out_shape=[jax.ShapeDtypeStruct((t_tok, d), F32), jax.ShapeDtypeStruct((t_tok, 1), F32), jax.ShapeDtypeStruct((t_tok, d), BF16)],
        args=(cat_t, x2, w_out_b, ln1_g, ln1_b))


def _ffn_fwd_bwd(xhat1, rstd1, x1b, target, w1_parts, w2_parts, ln1_g, ln1_b, ln2_g, ln2_b):
    t_tok, d = xhat1.shape
    n_part = len(w1_parts)
    n_chunk, _, fp = w1_parts[0].shape
    f = n_chunk * n_part * fp
    tm = min(FFN_ROWS, t_tok)

    def body(xhat1_ref, rstd1_ref, x1b_ref, tgt_ref, *refs):
        w1_hbm, w2_hbm = refs[:n_part], refs[n_part:2 * n_part]
        (g1_ref, b1_ref, g2_ref, b2_ref, act_ref, dpre_ref, dz2b_ref, dz1_ref, stats_ref,
         r_scr, w1_ref, w2_ref, w_sems) = refs[2 * n_part:]

        @pl.when(pl.program_id(0) == 0)
        def _():
            stats_ref[...] = jnp.zeros_like(stats_ref)
            loads = []
            for j in range(n_chunk):
                for p in range(n_part):
                    units = pl.ds((j * n_part + p) * fp, fp)
                    loads.append(pltpu.make_async_copy(w1_hbm[p].at[j], w1_ref.at[:, units], w_sems.at[0, p, j]))
                    loads.append(pltpu.make_async_copy(w2_hbm[p].at[j], w2_ref.at[units, :], w_sems.at[1, p, j]))
            for cp in loads:
                cp.start()
            for cp in loads:
                cp.wait()

        g1, g2 = g1_ref[...], g2_ref[...]
        xhat1 = xhat1_ref[...]
        r_scr[...] = jnp.maximum(_dot(x1b_ref[...], w1_ref[...]), 0.0)
        r = r_scr[...]
        act = (r * r).astype(BF16)
        act_ref[...] = act
        ff = _dot(act, w2_ref[...])
        y, xhat2, rstd2 = _ln_fwd(ALPHA * (xhat1 * g1 + b1_ref[...]) + ff, g2, b2_ref[...])
        diff = y - tgt_ref[...]
        loss = 0.5 * jnp.sum(jnp.sum(diff * diff, axis=-1, keepdims=True) / d, axis=0, keepdims=True)
        dy = diff / d
        dz2 = _ln_bwd(dy, xhat2, rstd2, g2)
        dz2b = dz2.astype(BF16)
        dz2b_ref[...] = dz2b
        dpre = (_dot(dz2b, w2_ref[...], NT) * (2.0 * r_scr[...])).astype(BF16)
        dpre_ref[...] = dpre
        dx1 = ALPHA * dz2 + _dot(dpre, w1_ref[...], NT)
        dz1_ref[...] = _ln_bwd(dx1, xhat1, rstd1_ref[...], g1)
        stats_ref[0:1, :] += jnp.sum(dx1 * xhat1, axis=0, keepdims=True)
        stats_ref[1:2, :] += jnp.sum(dx1, axis=0, keepdims=True)
        stats_ref[2:3, :] += jnp.sum(dy * xhat2, axis=0, keepdims=True)
        stats_ref[3:4, :] += jnp.sum(dy, axis=0, keepdims=True)
        stats_ref[4:5, :] += jnp.broadcast_to(loss, (1, d))

    tok = lambda w: pl.BlockSpec((tm, w), lambda i: (i, 0))
    vec = pl.BlockSpec((1, d), lambda i: (0, 0))
    return _carry(
        body, name="ffn_fwd_bwd", grid=(t_tok // tm,),
        in_specs=[tok(d), tok(1), tok(d), tok(d)] + [ANY] * (2 * n_part) + [vec, vec, vec, vec],
        out_specs=[tok(f), tok(f), tok(d), tok(d), pl.BlockSpec((8, d), lambda i: (0, 0))],
        out_shape=[jax.ShapeDtypeStruct((t_tok, f), BF16), jax.ShapeDtypeStruct((t_tok, f), BF16),
                   jax.ShapeDtypeStruct((t_tok, d), BF16), jax.ShapeDtypeStruct((t_tok, d), F32), jax.ShapeDtypeStruct((8, d), F32)],
        scratch_shapes=[pltpu.VMEM((tm, f), F32), pltpu.VMEM((d, f), BF16), pltpu.VMEM((f, d), BF16),
                        pltpu.SemaphoreType.DMA((2, n_part, n_chunk))],
        args=(xhat1, rstd1, x1b, target, *w1_parts, *w2_parts, ln1_g, ln1_b, ln2_g, ln2_b))[0]


WGRAD_STEPS = [(True, 0), (True, 1), (False, 0), (True, 2), (False, 1), (True, 3), (False, 2), (False, 3)]
CHIP_FLIPS = [3, 1, 2, 0]


def _pick(table, s):
    out = table[-1]
    for i in range(len(table) - 2, -1, -1):
        out = jnp.where(s == i, table[i], out)
    return out


def _wgrad_shard(s, cc):
    q = jnp.bitwise_xor(cc[1], _pick([CHIP_FLIPS[k] for _, k in WGRAD_STEPS], s))
    return 2 * q + jnp.where(_pick([int(sibling) for sibling, _ in WGRAD_STEPS], s) == 1, 1 - cc[0], cc[0])


def _flipped(k):
    x, y, c = _place()
    return (1 - x if CHIP_FLIPS[k] // 2 else x, 1 - y if CHIP_FLIPS[k] % 2 else y, c)


def _wgrad_pair_sum(name, product, chunk, in_specs, args, core_chip, n_sent, comms=(), scratch_shapes=(), rider=None):
    half = N_DEV // 2
    n_in, n_out = len(in_specs), 2 + (0 < n_sent) + (n_sent < half - 1)
    more = rider or dict(in_specs=[], out_specs=[], out_shape=[], args=[])
    n_rin, n_rout = len(more["in_specs"]), len(more["out_specs"])

    def body(cc_ref, *refs):
        ins, rins, refs = refs[:n_in], refs[n_in:n_in + n_rin], refs[n_in + n_rin:]
        outs, routs, scr = refs[:n_out], refs[n_out:n_out + n_rout], refs[n_out + n_rout:]
        if rider:
            pl.when(pl.program_id(0) == 0)(lambda: rider["body"](cc_ref, rins, routs))
        (own_ref, recv_ref), from_chips_ref, wire_ref = outs[-2:], outs[0], outs[n_out - 3]
        send_buf, got, send_sems, recv_sems, got_sem, wire_buf, leave_sems, arrive_sems = scr[:8]
        s = pl.program_id(0)
        x, y, c = _place()
        def send(q):
            return pltpu.make_async_remote_copy(
                src_ref=send_buf.at[q % 2], dst_ref=recv_ref.at[q], send_sem=send_sems.at[q], recv_sem=recv_sems.at[q],
                device_id=(x, y, 1 - c), device_id_type=MESH)

        def load(q):
            return pltpu.make_async_copy(recv_ref.at[q], got, got_sem.at[0])

        def leave(k):
            if k < n_sent:
                return pltpu.make_async_remote_copy(
                    src_ref=wire_buf.at[k], dst_ref=from_chips_ref.at[k], send_sem=leave_sems.at[k],
                    recv_sem=arrive_sems.at[k], device_id=_flipped(k), device_id_type=MESH)
            return pltpu.make_async_copy(wire_buf.at[k], wire_ref.at[k - n_sent], leave_sems.at[k])

        for step, (sibling, q) in enumerate(WGRAD_STEPS):
            if not sibling:
                @pl.when(s == step)
                def _(q=q):
                    send(q).wait_recv()
                    load(q).start()

        g = product(_wgrad_shard(s, cc_ref), *ins, *scr[8:])

        for step, (sibling, q) in enumerate(WGRAD_STEPS):
            @pl.when(s == step)
            def _(sibling=sibling, q=q):
                if sibling:
                    if q >= 2:
                        send(q - 2).wait_send()
                    send_buf[q % 2] = g
                    send(q).start()
                    return
                load(q).wait()
                total = g + got[...]
                if q < half - 1:
                    wire_buf[q] = total.astype(BF16)
                    leave(q).start()
                else:
                    own_ref[...] = total

        @pl.when(s == N_DEV - 1)
        def _():
            for q in range(half - 2, half):
                send(q).wait_send()
            for k in range(half - 1):
                leave(k).wait()

    sums = lambda n: [jax.ShapeDtypeStruct((n,) + chunk, BF16)] if n else []
    sem = lambda n: pltpu.SemaphoreType.DMA((n,))
    res, per_comm = _carry(
        body, name=name, grid=(N_DEV,), comms=comms, prefetch=(core_chip,), in_specs=list(in_specs) + more["in_specs"],
        out_specs=[ANY] * (n_out - 2) + [pl.BlockSpec(chunk, lambda s, cc: (0, 0)), ANY] + more["out_specs"],
        out_shape=sums(n_sent) + sums(half - 1 - n_sent) + [jax.ShapeDtypeStruct(chunk, F32),
                                                            jax.ShapeDtypeStruct((half,) + chunk, F32)] + more["out_shape"],
        scratch_shapes=[pltpu.VMEM((2,) + chunk, F32), pltpu.VMEM(chunk, F32), sem(half), sem(half), sem(1),
                        pltpu.VMEM((half - 1,) + chunk, BF16), sem(half - 1), sem(half - 1), *scratch_shapes],
        args=list(args) + more["args"])
    first = res[0] if n_sent else None, res[n_out - 3] if n_sent < half - 1 else None
    return (*first, res[n_out - 2], per_comm, res[n_out:]) if rider else (*first, res[n_out - 2], per_comm)


def _resident(a):
    return pl.BlockSpec(a.shape, lambda s, cc: (0,) * a.ndim, pipeline_mode=pl.Buffered(1))


def _pair_sum_rider(parts, recv):
    _, r, c = parts.shape
    by_chip = parts.reshape(N_DEV // 2, 2, r, c)

    def body(cc_ref, ins, outs):
        (parts_ref, recv_ref), (wire_ref, own_ref) = ins, outs
        for q in range(N_DEV // 2):
            total = parts_ref[q] + recv_ref[q]
            wire_ref[q] = total.astype(BF16)

            @pl.when(cc_ref[1] == q)
            def _():
                own_ref[...] = total

    whole = lambda shape: pl.BlockSpec(shape, lambda s, cc: (0,) * len(shape))
    mine = pl.BlockSpec((N_DEV // 2, None, r, c), lambda s, cc: (0, cc[0], 0, 0), pipeline_mode=pl.Buffered(1))
    return dict(body=body, args=[by_chip, recv], in_specs=[mine, _resident(recv)],
                out_specs=[whole((N_DEV // 2, r, c)), whole((r, c))],
                out_shape=[jax.ShapeDtypeStruct((N_DEV // 2, r, c), BF16), jax.ShapeDtypeStruct((r, c), F32)])


def _ffn_wgrad(name, lhs, rhs, chunk_lhs, core_chip, n_sent, comms=(), rider=None):
    t_tok = lhs.shape[0]
    fc = (lhs if chunk_lhs else rhs).shape[1] // N_DEV
    chunked = pl.BlockSpec((t_tok, fc), lambda s, cc: (0, _wgrad_shard(s, cc)))

    def product(shard, lhs_ref, rhs_ref):
        return _dot(lhs_ref[...], rhs_ref[...], TN)

    return _wgrad_pair_sum(
        name, product, (fc, rhs.shape[1]) if chunk_lhs else (lhs.shape[1], fc),
        [chunked, _resident(rhs)] if chunk_lhs else [_resident(lhs), chunked], (lhs, rhs), core_chip, n_sent, comms,
        rider=rider)


def _proj_out_bwd(dz1, cat_t, w_out_b, comms=()):
    t_tok, d = dz1.shape
    d_mix = cat_t.shape[0]
    tm = min(512, t_tok)

    def body(dz1_ref, cat_ref, w_ref, dcat_ref, gw_ref):
        @pl.when(pl.program_id(0) == 0)
        def _():
            gw_ref[...] = jnp.zeros_like(gw_ref)

        dzb = dz1_ref[...].astype(BF16)
        dcat_ref[...] = _dot(w_ref[...], dzb, NT)
        gw_ref[...] += _dot(cat_ref[...], dzb)

    return _carry(
        body, name="proj_out_bwd", grid=(t_tok // tm,), comms=comms,
        in_specs=[pl.BlockSpec((tm, d), lambda i: (i, 0)), pl.BlockSpec((d_mix, tm), lambda i: (0, i)),
                  pl.BlockSpec((d_mix, d), lambda i: (0, 0))],
        out_specs=[pl.BlockSpec((d_mix, tm), lambda i: (0, i)), pl.BlockSpec((d_mix, d), lambda i: (0, 0))],
        out_shape=[jax.ShapeDtypeStruct((d_mix, t_tok), F32), jax.ShapeDtypeStruct((d_mix, d), F32)],
        args=(dz1, cat_t, w_out_b))


def _mixer_bwd(dcat_t, h_t, cos_t, sin_t, w_spatial, b_spatial, vln_g, vln_b, sinks, band_bias, lse, comms=()):
    t_tok = h_t.shape[1]
    nb, n_step = t_tok // BLK, t_tok // MIX_W
    group = N_HEADS // N_KV_HEADS

    def body(sinks_ref, dcat_ref, u_ref, vg_ref, q_ref, kvc_ref, kvp_ref, cos_ref, sin_ref, cosp_ref, sinp_ref,
             wsp_ref, bsp_ref, g_ref, b_ref, bias_ref, lse_ref, dh_ref, dkvc_ref, dkvp_ref, gwsb_ref, gbsp_ref, gvln_ref, gsink_ref,
             dg_acc, db_acc, wm_scr, gws_ref):
        i = pl.program_id(0)

        @pl.when(i == 0)
        def _():
            gws_ref[...] = jnp.zeros_like(gws_ref)
            gbsp_ref[...] = jnp.zeros_like(gbsp_ref)
            gsink_ref[...] = jnp.zeros_like(gsink_ref)
            dg_acc[...] = jnp.zeros_like(dg_acc)
            db_acc[...] = jnp.zeros_like(db_acc)

        _mask_w_once(wsp_ref, wm_scr)

        g = g_ref[...]
        ua, ua_grad = _gelu_and_grad(u_ref[...])
        vv, vv_grad = _gelu_and_grad(vg_ref[...])
        vp, vhat, rstd = _ln_fwd_t(vv, g, b_ref[...])
        vpb = vp.astype(BF16)
        da = dcat_ref[0:D_GMLP, :]
        dmixed = da * ua
        dvp_blocks = []
        for b in range(MIX_BLOCKS):
            dvp_parts = []
            for hh in range(N_HEADS):
                rows = slice(hh * HEAD_DIM, (hh + 1) * HEAD_DIM)
                vpb_h = vpb[rows, _cols(b)]
                mixed = _dot(vpb_h, wm_scr[hh], NT) + bsp_ref[hh:hh + 1, :]
                dh_ref[COL_U + hh * HEAD_DIM:COL_U + (hh + 1) * HEAD_DIM, _cols(b)] = (
                    da[rows, _cols(b)] * mixed * ua_grad[rows, _cols(b)]).astype(BF16)
                dm = dmixed[rows, _cols(b)]
                dmb = dm.astype(BF16)
                gbsp_ref[hh:hh + 1, :] += jnp.sum(dm, axis=0, keepdims=True)
                gws_ref[hh] += _dot(dmb, vpb_h, TN)
                dvp_parts.append(_dot(dmb, wm_scr[hh]))
            dvp_blocks.append(jnp.concatenate(dvp_parts, axis=0))
        dvp = jnp.concatenate(dvp_blocks, axis=1)
        dgv, dbv = dvp * vhat, dvp
        for b in range(MIX_BLOCKS):
            dg_acc[...] += dgv[:, _cols(b)]
            db_acc[...] += dbv[:, _cols(b)]
        dh_ref[COL_V:COL_V + D_GMLP, :] = (_ln_bwd_t(dvp, vhat, rstd, g) * vv_grad).astype(BF16)

        kvc, cos, sin = kvc_ref[...], cos_ref[...], sin_ref[...]
        qr = (_rope_t(q_ref[...], cos, sin) * SCORE_SCALE).astype(BF16)
        sinks4 = [_group_lanes([jnp.full((1, BLK), sinks_ref[hh], F32) for hh in range(kv * group, (kv + 1) * group)])
                  for kv in range(N_KV_HEADS)]
        dq_blocks, dkv_cur, dkv_prev = [], [], []
        for b in range(MIX_BLOCKS):
            kv_cur, kv_prev, cosc, sinc, cosp, sinp, bias1 = _block_inputs(b, i, kvc, kvp_ref, cos, sin, cosp_ref, sinp_ref, bias_ref)
            k_t, k_n, v_t = _keys_values(kv_cur, kv_prev, cosc, sinc, cosp, sinp)
            v_n = jnp.concatenate([kv_prev[D_KV:].T, kv_cur[D_KV:].T], axis=0).astype(BF16)
            bias = _group_lanes([bias1] * group)
            dk, dv, dq_parts = [], [], []
            for kv in range(N_KV_HEADS):
                heads = range(kv * group, (kv + 1) * group)
                kv_rows = slice(kv * HEAD_DIM, (kv + 1) * HEAD_DIM)
                qs = _group_lanes([qr[hh * HEAD_DIM:(hh + 1) * HEAD_DIM, _cols(b)] for hh in heads])
                dos = _group_lanes([dcat_ref[D_GMLP + hh * HEAD_DIM:D_GMLP + (hh + 1) * HEAD_DIM, _cols(b)]
                                    for hh in heads]).astype(BF16)
                lse_g = lse_ref[b * N_KV_HEADS + kv:b * N_KV_HEADS + kv + 1, :]
                p = jnp.exp(_dot(k_n, _pad_head(qs, kv)) + bias - lse_g)
                p_sink = jnp.exp(sinks4[kv] - lse_g)
                dp = _dot(v_n, _pad_head(dos, kv))
                delta = jnp.sum(p * dp, axis=0, keepdims=True)
                ds = (p * (dp - delta)).astype(BF16)
                dsink = p_sink * delta
                dq = _dot(k_t[kv_rows], ds) * SCORE_SCALE
                for j, hh in enumerate(heads):
                    gsink_ref[hh:hh + 1, :] -= dsink[:, j * BLK:(j + 1) * BLK]
                    dq_parts.append(dq[:, j * BLK:(j + 1) * BLK])
                dk.append(_dot(qs, ds, NT))
                dv.append(_dot(dos, p.astype(BF16), NT))
            dq_blocks.append(jnp.concatenate(dq_parts, axis=0))
            dk_all, dv_all = jnp.concatenate(dk, axis=0), jnp.concatenate(dv, axis=0)
            dkv_cur.append(jnp.concatenate([_rope_t(dk_all[:, BLK:], cosc, sinc, bwd=True), dv_all[:, BLK:]], axis=0))
            dkv_prev.append(jnp.concatenate([_rope_t(dk_all[:, :BLK], cosp, sinp, bwd=True), dv_all[:, :BLK]], axis=0))
        dh_ref[COL_Q:COL_Q + D_ATTN, :] = _rope_t(jnp.concatenate(dq_blocks, axis=1), cos, sin, bwd=True).astype(BF16)
        for b in range(MIX_BLOCKS):
            dkvc_ref[:, _cols(b)] = dkv_cur[b] + dkv_prev[b + 1] if b + 1 < MIX_BLOCKS else dkv_cur[b]
        dkvp_ref[...] = dkv_prev[0]

        @pl.when(i == n_step - 1)
        def _():
            causal = _causal()
            for hh in range(N_HEADS):
                gwsb_ref[hh] = jnp.where(causal, gws_ref[hh], 0.0).astype(BF16)
            gvln_ref[...] = jnp.zeros_like(gvln_ref)
            gvln_ref[0:1, :] = jnp.sum(dg_acc[...].T, axis=0, keepdims=True)
            gvln_ref[1:2, :] = jnp.sum(db_acc[...].T, axis=0, keepdims=True)

    full = lambda shape: pl.BlockSpec(shape, lambda i: (0,) * len(shape))
    return _carry(
        body, name="mixer_bwd", grid=(n_step,), comms=comms,
        in_specs=[pl.BlockSpec(memory_space=pltpu.SMEM), pl.BlockSpec((D_GMLP + D_ATTN, MIX_W), lambda i: (0, i))]
        + _h_specs() + _table_specs()
        + [full((N_HEADS, BLK, BLK)), full((N_HEADS, BLK)), full((D_GMLP, 1)), full((D_GMLP, 1)), BIAS_SPEC, LSE_SPEC],
        out_specs=[pl.BlockSpec((COL_K, MIX_W), lambda i: (0, i)), pl.BlockSpec((2 * D_KV, MIX_W), lambda i: (0, i)),
                   pl.BlockSpec((2 * D_KV, BLK), lambda i: (0, (i + n_step - 1) % n_step)),
                   full((N_HEADS, BLK, BLK)), full((N_HEADS, BLK)), full((8, D_GMLP)), full((N_HEADS, LANES))],
        out_shape=[jax.ShapeDtypeStruct((COL_K, t_tok), BF16), jax.ShapeDtypeStruct((2 * D_KV, t_tok), F32),
                   jax.ShapeDtypeStruct((2 * D_KV, n_step * BLK), F32),
                   jax.ShapeDtypeStruct((N_HEADS, BLK, BLK), BF16), jax.ShapeDtypeStruct((N_HEADS, BLK), F32),
                   jax.ShapeDtypeStruct((8, D_GMLP), F32), jax.ShapeDtypeStruct((N_HEADS, LANES), F32)],
        scratch_shapes=[pltpu.VMEM((D_GMLP, BLK), F32), pltpu.VMEM((D_GMLP, BLK), F32), pltpu.VMEM((N_HEADS, BLK, BLK), BF16),
                        pltpu.VMEM((N_HEADS, BLK, BLK), F32)],
        args=(sinks, dcat_t, h_t, h_t, h_t, h_t, h_t, cos_t, sin_t, cos_t, sin_t, w_spatial, b_spatial, vln_g, vln_b, band_bias, lse))


def _dkv_rows(dkvc_ref, dkvp_ref, width, store):
    for s in range(width // MIX_W):
        rest, last = slice(s * MIX_W, (s + 1) * MIX_W - BLK), slice((s + 1) * MIX_W - BLK, (s + 1) * MIX_W)
        store(rest, dkvc_ref[:, rest].astype(BF16))
        store(last, (dkvc_ref[:, last] + dkvp_ref[:, _cols(s)]).astype(BF16))


def _proj_in_wgrad(dh_b, dkvc_t, dkvp_t, xb, core_chip, comms=()):
    t_tok, d = xb.shape
    d_main, d_kv = dh_b.shape[0], dkvc_t.shape[0]
    rows = (d_main + d_kv) // N_DEV
    whole, cut = d_main // rows, d_main % rows

    def product(shard, dh_ref, dkvc_ref, dkvp_ref, xb_ref, dht_scr, sems):
        copies = [pltpu.make_async_copy(dh_ref.at[j * rows:(j + 1) * rows], dht_scr.at[j], sems.at[j]) for j in range(whole)]
        copies.append(pltpu.make_async_copy(dh_ref.at[whole * rows:d_main], dht_scr.at[whole, 0:cut], sems.at[whole]))

        @pl.when(pl.program_id(0) == 0)
        def _():
            for cp in copies:
                cp.start()

            def store(cols, val):
                dht_scr[whole, cut:rows, cols] = val[0:rows - cut]
                dht_scr[whole + 1, :, cols] = val[rows - cut:]

            _dkv_rows(dkvc_ref, dkvp_ref, t_tok, store)
            for cp in copies:
                cp.wait()

        return _dot(dht_scr[shard], xb_ref[...])

    return _wgrad_pair_sum(
        "proj_in_wgrad", product, (rows, d), [ANY, _resident(dkvc_t), _resident(dkvp_t), _resident(xb)],
        (dh_b, dkvc_t, dkvp_t, xb), core_chip, N_DEV // 2 - 1, comms,
        scratch_shapes=[pltpu.VMEM((N_DEV, rows, t_tok), BF16), pltpu.SemaphoreType.DMA((whole + 1,))])


def _proj_in_dgrad(dh_b, dkvc_t, dkvp_t, dz1, w_in_t, comms=()):
    t_tok, d = dz1.shape
    d_main, d_kv = dh_b.shape[0], dkvc_t.shape[0]
    tm = min(512, t_tok)

    def body(dh_ref, dkvc_ref, dkvp_ref, dz1_ref, w_ref, dx_ref, dkv_scr):
        def store(cols, val):
            dkv_scr[:, cols] = val

        _dkv_rows(dkvc_ref, dkvp_ref, tm, store)
        dx_ref[...] = (ALPHA * dz1_ref[...] + _dot(dh_ref[...], w_ref[0:d_main, :], TN)
                       + _dot(dkv_scr[...], w_ref[d_main:, :], TN))

    return _carry(
        body, name="proj_in_dgrad", grid=(t_tok // tm,), comms=comms,
        in_specs=[pl.BlockSpec((d_main, tm), lambda i: (0, i)), pl.BlockSpec((d_kv, tm), lambda i: (0, i)),
                  pl.BlockSpec((d_kv, tm // MIX_BLOCKS), lambda i: (0, i)),
                  pl.BlockSpec((tm, d), lambda i: (i, 0)), pl.BlockSpec((d_main + d_kv, d), lambda i: (0, 0))],
        out_specs=[pl.BlockSpec((tm, d), lambda i: (i, 0))],
        out_shape=[jax.ShapeDtypeStruct((t_tok, d), F32)],
        scratch_shapes=[pltpu.VMEM((d_kv, tm), BF16)],
        args=(dh_b, dkvc_t, dkvp_t, dz1, w_in_t))


def _adamw(w, g, m, v):
    m = ADAM_B1 * m + (1.0 - ADAM_B1) * g
    v = ADAM_B2 * v + (1.0 - ADAM_B2) * (g * g)
    m_hat = m / (1.0 - ADAM_B1 ** ADAM_STEP)
    v_hat = v / (1.0 - ADAM_B2 ** ADAM_STEP)
    delta = -ADAM_LR * (m_hat / (jnp.sqrt(v_hat) + ADAM_EPS) + ADAM_WD * w)
    return delta, m, v


ADAMW_STEPS = 4


def _adamw_shards(name, items, comms=(), rider=None):
    n_in, n_out = sum(4 + len(it[1]) for it in items), 4 * len(items)
    n_rin = len(rider["args"]) if rider else 0

    def body(*refs):
        ins, rins, outs, routs = refs[:n_in], refs[n_in:n_in + n_rin], refs[n_in + n_rin:n_in + n_rin + n_out], refs[n_in + n_rin + n_out:]
        for i, item in enumerate(items):
            (own_ref, w_ref, m_ref, v_ref), recv_refs, ins = ins[:4], ins[4:4 + len(item[1])], ins[4 + len(item[1]):]
            g = own_ref[...]
            for recv_ref in recv_refs:
                for k in range(recv_ref.shape[0]):
                    g = g + recv_ref[k].astype(F32)
            for o_ref, val in zip(outs[4 * i:4 * i + 4], (g,) + _adamw(w_ref[...], g, m_ref[...], v_ref[...])):
                o_ref[...] = val
        if rider:
            pl.when(pl.program_id(0) == 0)(lambda: rider["body"](rins, routs))

    in_specs, out_specs, out_shape, args = [], [], [], []
    for own, recvs, w, m, v in items:
        r, c = own.shape
        tiles = ADAMW_STEPS
        while (r // tiles) % BF16_ROWS:
            tiles //= 2
        blk = pl.BlockSpec((r // tiles, c), lambda s, k=ADAMW_STEPS // tiles: (s // k, 0))
        in_specs += [blk] * 4 + [pl.BlockSpec((a.shape[0], r // tiles, c), lambda s, k=ADAMW_STEPS // tiles: (0, s // k, 0))
                                 for a in recvs]
        out_specs += [blk] * 4
        out_shape += [jax.ShapeDtypeStruct((r, c), F32)] * 4
        args += [own, w, m, v, *recvs]
    if rider:
        in_specs, out_specs = in_specs + rider["in_specs"], out_specs + rider["out_specs"]
        out_shape, args = out_shape + rider["out_shape"], args + rider["args"]
    res, per_comm = _carry(body, name=name, grid=(ADAMW_STEPS,), comms=comms, in_specs=in_specs, out_specs=out_specs,
                           out_shape=out_shape, args=args)
    return [res[4 * i:4 * i + 4] for i in range(len(items))], res[n_out:], per_comm


VEC_VLN, VEC_LN1G, VEC_LN1B, VEC_LN2G, VEC_LN2B, VEC_SINK, VEC_LOSS, VEC_BSP, VEC_ROWS = 0, 1, 2, 3, 4, 5, 6, 8, 16


def _adamw_small(parts_w, parts_vec, params):
    n = parts_w.shape[0]
    flat = [a for p in params for a in p]
    shapes = [p[0].shape for p in params]

    def grads(gw, gv):
        return [gw, gv[VEC_VLN:VEC_VLN + 1, 0:D_GMLP], gv[VEC_VLN:VEC_VLN + 1, D_GMLP:2 * D_GMLP],
                gv[VEC_BSP:VEC_BSP + N_HEADS, 0:BLK], gv[VEC_LN1G:VEC_LN1G + 1], gv[VEC_LN1B:VEC_LN1B + 1],
                gv[VEC_LN2G:VEC_LN2G + 1], gv[VEC_LN2B:VEC_LN2B + 1], gv[VEC_SINK:VEC_SINK + 1, 0:N_HEADS]]

    def body(ins, outs):
        (pw_ref, pv_ref), ins = ins[:2], ins[2:]
        gw, gv = pw_ref[0].astype(F32), pv_ref[0]
        for k in range(1, n):
            gw, gv = gw + pw_ref[k].astype(F32), gv + pv_ref[k]
        for i, g in enumerate(grads(gw, gv)):
            w_ref, m_ref, v_ref = ins[3 * i:3 * i + 3]
            delta, m_new, v_new = _adamw(w_ref[...], g, m_ref[...], v_ref[...])
            for o_ref, val in zip(outs[4 * i:4 * i + 4], (g, delta, m_new, v_new)):
                o_ref[...] = val
        outs[-1][...] = gv[VEC_LOSS:VEC_LOSS + 1, 0:LANES]

    whole = lambda shape, **kw: pl.BlockSpec(shape, lambda i: (0,) * len(shape), **kw)
    once = dict(pipeline_mode=pl.Buffered(1))
    return dict(
        body=body, args=[parts_w, parts_vec, *flat],
        in_specs=[whole(parts_w.shape, **once), whole(parts_vec.shape, **once)] + [whole(a.shape, **once) for a in flat],
        out_specs=[whole(s) for s in shapes for _ in range(4)] + [whole((1, LANES))],
        out_shape=[jax.ShapeDtypeStruct(s, F32) for s in shapes for _ in range(4)] + [jax.ShapeDtypeStruct((1, LANES), F32)])


def _pair_sum(name, parts, recv, core_chip, comms=()):
    _, r, c = parts.shape
    tr = r if r <= 512 else 512

    def body(cc_ref, a_ref, b_ref, wire_ref, own_ref):
        s = a_ref[...] + b_ref[...]
        wire_ref[...] = s.astype(BF16)

        @pl.when(pl.program_id(1) == cc_ref[1])
        def _():
            own_ref[...] = s

    return _carry(
        body, name=name, grid=(r // tr, 4), prefetch=(core_chip,), comms=comms,
        in_specs=[pl.BlockSpec((None, tr, c), lambda i, q, cc: (2 * q + cc[0], i, 0)),
                  pl.BlockSpec((None, tr, c), lambda i, q, cc: (q, i, 0))],
        out_specs=[pl.BlockSpec((None, tr, c), lambda i, q, cc: (q, i, 0)), pl.BlockSpec((tr, c), lambda i, q, cc: (i, 0))],
        out_shape=[jax.ShapeDtypeStruct((4, r, c), BF16), jax.ShapeDtypeStruct((r, c), F32)],
        args=(parts, recv))


def kernel(x, positions, w_in, v_ln_g, v_ln_b, w_spatial, b_spatial, sinks, w_out, ln1_g, ln1_b, w_ff1, w_ff2, ln2_g, ln2_b, loss_target, m_w_in, m_v_ln_g, m_v_ln_b, m_w_spatial, m_b_spatial, m_sinks, m_w_out, m_ln1_g, m_ln1_b, m_w_ff1, m_w_ff2, m_ln2_g, m_ln2_b, v_w_in, v_v_ln_g, v_v_ln_b, v_w_spatial, v_b_spatial, v_sinks, v_w_out, v_ln1_g, v_ln1_b, v_w_ff1, v_w_ff2, v_ln2_g, v_ln2_b):
    _, t_tok, d = x.shape
    xi, yi, ci = _place()
    core_chip = jnp.stack([ci, 2 * xi + yi]).astype(jnp.int32)
    x2 = x.reshape(t_tok, d)
    target = loss_target.reshape(t_tok, d)
    inv_freq = ROPE_THETA ** (-jnp.arange(0, HEAD_DIM, 2, dtype=F32) / HEAD_DIM)
    wsp, bsp, sink_vec = w_spatial[0], b_spatial[0], sinks[0]
    vg_col, vb_col = v_ln_g.reshape(D_GMLP, 1), v_ln_b.reshape(D_GMLP, 1)
    big = {"in": w_in[0], "out": w_out[0], "ff1": w_ff1[0], "ff2": w_ff2[0]}
    half1, half2 = big["ff1"].shape[1] // 2, big["ff2"].shape[0] // 2
    w1_mine = [big["ff1"][:, :half1].astype(BF16), big["ff1"][:, half1:].astype(BF16)]
    w2_mine = [big["ff2"][:half2].astype(BF16), big["ff2"][half2:].astype(BF16)]

    (cos_t, sin_t), ((g_in,),) = _rope_tables(
        positions, jnp.tile(inv_freq, 2).reshape(HEAD_DIM, 1), comms=[_gather_comm([big["in"].T.astype(BF16)])])
    w_in_t = g_in.reshape(D_IN, d)
    (h_t, xb), ((g_out, w1_a),) = _proj_in(x2, w_in_t, comms=[_gather_comm([big["out"].astype(BF16), w1_mine[0]])])
    w_out_b = g_out.reshape(-1, d)
    band_bias = _band_bias()
    (cat_t, lse), ((w1_b, w2_a),) = _mixer_fwd(h_t, cos_t, sin_t, wsp, bsp, vg_col, vb_col, sink_vec, band_bias,
                                                comms=[_gather_comm([w1_mine[1], w2_mine[0]])])
    (xhat1, rstd1, x1b), ((w2_b,),) = _proj_out(cat_t, x2, w_out_b, ln1_g, ln1_b, comms=[_gather_comm([w2_mine[1]])])
    act_b, dpre_b, dz2b, dz1, stats = _ffn_fwd_bwd(xhat1, rstd1, x1b, target, [w1_a, w1_b], [w2_a, w2_b], ln1_g, ln1_b, ln2_g, ln2_b)

    (dcat_t, gw_out), _ = _proj_out_bwd(dz1, cat_t, w_out_b)
    p_out = gw_out.reshape(N_DEV, -1, d)
    r_ff1_a, wire_ff1, own_ff1, ((s_out,),) = _ffn_wgrad(
        "ffn_wgrad1", x1b, dpre_b, False, core_chip, 1, comms=[_sibling_comm([p_out])])
    _, wire_ff2, own_ff2, ((r_ff1_b,),), (wire_out, own_out) = _ffn_wgrad(
        "ffn_wgrad2", act_b, dz2b, True, core_chip, 0, comms=[_flips_comm(wire_ff1, 1)],
        rider=_pair_sum_rider(p_out, s_out))
    (dh_b, dkvc_t, dkvp_t, g_wsp, g_bsp, g_vln, g_sink), ((r_ff2,), (r_out,)) = _mixer_bwd(
        dcat_t, h_t, cos_t, sin_t, wsp, bsp, vg_col, vb_col, sink_vec, band_bias, lse,
        comms=[_flips_comm(wire_ff2, 0), _chips_comm([wire_out])])
    sink_row = jnp.pad(g_sink.sum(axis=1).reshape(1, N_HEADS), ((0, 0), (0, d - N_HEADS)))
    small_vec = jnp.concatenate([g_vln[0:2].reshape(1, d), stats[0:4], sink_row, stats[4:5], jnp.zeros((1, d), F32),
                                 jnp.pad(g_bsp, ((0, 0), (0, d - BLK)))], axis=0)
    r_in, _, own_in, ((parts_w, parts_vec),) = _proj_in_wgrad(
        dh_b, dkvc_t, dkvp_t, xb, core_chip, comms=[_gather_comm([g_wsp.reshape(-1, BLK), small_vec])])
    (grad_x,), _ = _proj_in_dgrad(dh_b, dkvc_t, dkvp_t, dz1, w_in_t)
    small = [(w_spatial, m_w_spatial, v_w_spatial), (v_ln_g, m_v_ln_g, v_v_ln_g), (v_ln_b, m_v_ln_b, v_v_ln_b),
             (b_spatial, m_b_spatial, v_b_spatial), (ln1_g, m_ln1_g, v_ln1_g), (ln1_b, m_ln1_b, v_ln1_b),
             (ln2_g, m_ln2_g, v_ln2_g), (ln2_b, m_ln2_b, v_ln2_b), (sinks, m_sinks, v_sinks)]
    views = [(-1, BLK), None, None, (N_HEADS, BLK)] + [None] * 5
    small_update = _adamw_small(parts_w, parts_vec, [
        tuple(a if vw is None else a.reshape(vw) for a in p) for p, vw in zip(small, views)])
    (out_out, ff1_out, ff2_out, in_out_t), small_res, _ = _adamw_shards("adamw_all", [
        (own_out, [r_out], big["out"], m_w_out[0], v_w_out[0]),
        (own_ff1, [r_ff1_a, r_ff1_b], big["ff1"], m_w_ff1[0], v_w_ff1[0]),
        (own_ff2, [r_ff2], big["ff2"], m_w_ff2[0], v_w_ff2[0]),
        (own_in, [r_in], big["in"].T, m_w_in[0].T, v_w_in[0].T)], rider=small_update)
    in_out = [o.T for o in in_out_t]
    small_out = [[o.reshape(p[0].shape) for o in small_res[4 * i:4 * i + 4]] for i, p in enumerate(small)]
    loss = small_res[-1][0, 0]

    big_out = {0: in_out, 6: out_out, 9: ff1_out, 10: ff2_out}
    small_slot = {3: 0, 1: 1, 2: 2, 4: 3, 7: 4, 8: 5, 11: 6, 12: 7, 5: 8}
    outs = [loss, grad_x.reshape(x.shape)]
    for kind in range(4):
        for wi in range(13):
            outs.append(big_out[wi][kind][None] if wi in big_out else small_out[small_slot[wi]][kind])
    return tuple(outs)
```

```python
import math

import jax
import jax.numpy as jnp
from jax import lax
from jax.experimental import pallas as pl
from jax.experimental.pallas import tpu as pltpu

F32 = jnp.float32
BF16 = jnp.bfloat16
MESH = pl.DeviceIdType.MESH

HEAD_DIM = 64
N_HEADS = 8
N_KV_HEADS = 2
BLK = 128
D_GMLP = N_HEADS * HEAD_DIM
D_ATTN = N_HEADS * HEAD_DIM
D_KV = N_KV_HEADS * HEAD_DIM
D_IN = 2 * D_GMLP + D_ATTN + 2 * D_KV
COL_U, COL_V, COL_Q, COL_K = 0, D_GMLP, 2 * D_GMLP, 2 * D_GMLP + D_ATTN
ROPE_THETA = 10000.0
LN_EPS = 1e-5
ALPHA = 2.0 ** 0.25
NEG_INF = -1e30
SCORE_SCALE = 1.0 / math.sqrt(HEAD_DIM)
ADAM_LR, ADAM_B1, ADAM_B2, ADAM_EPS, ADAM_WD, ADAM_STEP = 0.001, 0.9, 0.999, 1e-08, 0.01, 10
N_DEV = 8
LANES = 128
VMEM_LIMIT = 56 * 1024 * 1024
FFN_ROWS = 256

NT = (((1,), (1,)), ((), ()))
TN = (((0,), (0,)), ((), ()))


def _params(*sem):
    return pltpu.CompilerParams(dimension_semantics=sem, vmem_limit_bytes=VMEM_LIMIT)


def _dot(a, b, dims=None):
    if dims is None:
        return jnp.dot(a, b, preferred_element_type=F32)
    return lax.dot_general(a, b, dims, preferred_element_type=F32)


def _mean(a):
    return jnp.mean(a, axis=-1, keepdims=True)


def _ln_fwd(z, g, b):
    zc = z - _mean(z)
    rstd = lax.rsqrt(_mean(zc * zc) + LN_EPS)
    xhat = zc * rstd
    return xhat * g + b, xhat, rstd


def _ln_bwd(dy, xhat, rstd, g):
    dxhat = dy * g
    return rstd * (dxhat - _mean(dxhat) - xhat * _mean(dxhat * xhat))


_GELU_C = math.sqrt(2.0 / math.pi)


def _gelu(x):
    t = jnp.tanh(_GELU_C * (x + 0.044715 * (x * x * x)))
    return 0.5 * x * (1.0 + t)


def _gelu_and_grad(x):
    x2 = x * x
    t = jnp.tanh(_GELU_C * (x + 0.044715 * (x2 * x)))
    hx, ht = 0.5 * x, 0.5 * (1.0 + t)
    return x * ht, ht + hx * (1.0 - t * t) * (_GELU_C * (1.0 + 3.0 * 0.044715 * x2))


def _mean0(a):
    return jnp.mean(a, axis=0, keepdims=True)


def _ln_fwd_t(z, g, b):
    zc = z - _mean0(z)
    rstd = lax.rsqrt(_mean0(zc * zc) + LN_EPS)
    xhat = zc * rstd
    return xhat * g + b, xhat, rstd


def _ln_bwd_t(dy, xhat, rstd, g):
    dxhat = dy * g
    return rstd * (dxhat - _mean0(dxhat) - xhat * _mean0(dxhat * xhat))


def _rope_t(t, cos, sin_signed, bwd=False):
    half = HEAD_DIM // 2
    outs = []
    for r in range(0, t.shape[0], HEAD_DIM):
        th = t[r:r + HEAD_DIM]
        sw = jnp.concatenate([th[half:], th[:half]], axis=0) * sin_signed
        outs.append(th * cos - sw if bwd else th * cos + sw)
    return jnp.concatenate(outs, axis=0)


ANY = pl.BlockSpec(memory_space=pl.ANY)
GATHER_PIECES = 2
BF16_ROWS = 16


def _place():
    return lax.axis_index("x"), lax.axis_index("y"), lax.axis_index("c")


class _Comm:
    def __init__(self, ins, outs, sems, start, finish):
        self.ins, self.outs, self.sems, self.start, self.finish = ins, outs, sems, start, finish


def _gather_comm(arrs):
    n = len(arrs)
    pieces = []
    for a, arr in enumerate(arrs):
        k = GATHER_PIECES
        while arr.shape[0] % (k * BF16_ROWS):
            k //= 2
        pieces += [(a, p * (arr.shape[0] // k), arr.shape[0] // k) for p in range(k)]

    def parts(ins, outs, sems):
        send_sems, recv_sems, local_sems = sems
        x, y, c = _place()
        me, sibling = (x, y, c), (x, y, 1 - c)
        chips = [(1 - x, y), (x, 1 - y), (1 - x, 1 - y)]

        def copy(u, k, block, to, local=False):
            a, r0, nr = pieces[u]
            px, py, pc = block
            dst = outs[a].at[4 * px + 2 * py + pc, pl.ds(r0, nr)]
            return pltpu.make_async_remote_copy(
                src_ref=ins[a].at[pl.ds(r0, nr)] if local else dst, dst_ref=dst,
                send_sem=send_sems.at[u, k], recv_sem=recv_sems.at[u, k], device_id=to, device_id_type=MESH)

        mine = [pltpu.make_async_copy(ins[a], outs[a].at[4 * x + 2 * y + c], local_sems.at[a]) for a in range(n)]
        first = []
        for u in range(len(pieces)):
            first.append(copy(u, 0, me, sibling, local=True))
            first += [copy(u, 1 + j, me, (*chip, c), local=True) for j, chip in enumerate(chips)]
        return copy, mine, first, me, sibling, chips, c

    def start(ins, outs, sems):
        _, mine, first, *_ = parts(ins, outs, sems)
        for cp in mine + first:
            cp.start()

    def finish(ins, outs, sems):
        copy, mine, first, me, sibling, chips, c = parts(ins, outs, sems)
        passed = []
        for u in range(len(pieces)):
            for j, chip in enumerate(chips):
                copy(u, 1 + j, (*chip, c), me).wait_recv()
                fwd = copy(u, 4 + j, (*chip, c), sibling)
                fwd.start()
                passed.append(fwd)
        for u in range(len(pieces)):
            copy(u, 0, sibling, me).wait_recv()
            for j, chip in enumerate(chips):
                copy(u, 4 + j, (*chip, 1 - c), me).wait_recv()
        for cp in first + passed:
            cp.wait_send()
        for cp in mine:
            cp.wait()

    return _Comm(list(arrs), [jax.ShapeDtypeStruct((N_DEV,) + a.shape, a.dtype) for a in arrs],
                 [pltpu.SemaphoreType.DMA((len(pieces), 7)), pltpu.SemaphoreType.DMA((len(pieces), 7)),
                  pltpu.SemaphoreType.DMA((n,))], start, finish)


def _sibling_comm(parts):
    n = len(parts)

    def copies(ins, outs, sems):
        x, y, c = _place()
        return [pltpu.make_async_remote_copy(
            src_ref=ins[a].at[2 * q + (1 - c)], dst_ref=outs[a].at[q],
            send_sem=sems[0].at[a, q], recv_sem=sems[1].at[a, q],
            device_id=(x, y, 1 - c), device_id_type=MESH) for a in range(n) for q in range(4)]

    return _Comm(list(parts), [jax.ShapeDtypeStruct((4,) + p.shape[1:], p.dtype) for p in parts],
                 [pltpu.SemaphoreType.DMA((n, 4)), pltpu.SemaphoreType.DMA((n, 4))],
                 lambda *r: [cp.start() for cp in copies(*r)], lambda *r: [cp.wait() for cp in copies(*r)])


def _chips_comm(chip_parts):
    n = len(chip_parts)

    def copies(ins, outs, sems):
        x, y, c = _place()
        chips = [(1 - x, y), (x, 1 - y), (1 - x, 1 - y)]
        return [pltpu.make_async_remote_copy(
            src_ref=ins[a].at[2 * px + py], dst_ref=outs[a].at[k],
            send_sem=sems[0].at[a, k], recv_sem=sems[1].at[a, k],
            device_id=(px, py, c), device_id_type=MESH) for a in range(n) for k, (px, py) in enumerate(chips)]

    return _Comm(list(chip_parts), [jax.ShapeDtypeStruct((3,) + p.shape[1:], p.dtype) for p in chip_parts],
                 [pltpu.SemaphoreType.DMA((n, 3)), pltpu.SemaphoreType.DMA((n, 3))],
                 lambda *r: [cp.start() for cp in copies(*r)], lambda *r: [cp.wait() for cp in copies(*r)])


def _flips_comm(sums, first):
    m = sums.shape[0]

    def copies(ins, outs, sems):
        return [pltpu.make_async_remote_copy(
            src_ref=ins[0].at[j], dst_ref=outs[0].at[j], send_sem=sems[0].at[j], recv_sem=sems[1].at[j],
            device_id=_flipped(first + j), device_id_type=MESH) for j in range(m)]

    return _Comm([sums], [jax.ShapeDtypeStruct(sums.shape, sums.dtype)],
                 [pltpu.SemaphoreType.DMA((m,)), pltpu.SemaphoreType.DMA((m,))],
                 lambda *r: [cp.start() for cp in copies(*r)], lambda *r: [cp.wait() for cp in copies(*r)])


def _carry(body, *, name, grid, in_specs, out_specs, out_shape, args, comms=(), scratch_shapes=(), prefetch=()):
    n_pre, n_in, n_out, n_scr = len(prefetch), len(in_specs), len(out_specs), len(scratch_shapes)
    c_ins = [a for cm in comms for a in cm.ins]
    c_outs = [s for cm in comms for s in cm.outs]
    c_sems = [s for cm in comms for s in cm.sems]

    def wrapped(*refs):
        pre, refs = refs[:n_pre], refs[n_pre:]
        ins, refs = refs[:n_in], refs[n_in:]
        cins, refs = refs[:len(c_ins)], refs[len(c_ins):]
        outs, refs = refs[:n_out], refs[n_out:]
        couts, refs = refs[:len(c_outs)], refs[len(c_outs):]
        scr, sems = refs[:n_scr], refs[n_scr:]
        groups, i0, o0, s0 = [], 0, 0, 0
        for cm in comms:
            groups.append((cm, cins[i0:i0 + len(cm.ins)], couts[o0:o0 + len(cm.outs)], sems[s0:s0 + len(cm.sems)]))
            i0, o0, s0 = i0 + len(cm.ins), o0 + len(cm.outs), s0 + len(cm.sems)
        first = pl.program_id(0) == 0
        last = pl.program_id(0) == grid[0] - 1
        for ax in range(1, len(grid)):
            first = first & (pl.program_id(ax) == 0)
            last = last & (pl.program_id(ax) == grid[ax] - 1)
        if comms:
            @pl.when(first)
            def _():
                for cm, ci, co, cs in groups:
                    cm.start(ci, co, cs)
        body(*pre, *ins, *outs, *scr)
        if comms:
            @pl.when(last)
            def _():
                for cm, ci, co, cs in groups:
                    cm.finish(ci, co, cs)

    grid_spec = pltpu.PrefetchScalarGridSpec(
        num_scalar_prefetch=n_pre, grid=grid,
        in_specs=list(in_specs) + [ANY] * len(c_ins), out_specs=list(out_specs) + [ANY] * len(c_outs),
        scratch_shapes=list(scratch_shapes) + c_sems)
    res = pl.pallas_call(
        wrapped, name=name, grid_spec=grid_spec, out_shape=list(out_shape) + c_outs,
        compiler_params=_params(*(["arbitrary"] * len(grid))),
    )(*prefetch, *args, *c_ins)
    outs, rest, per_comm = res[:n_out], res[n_out:], []
    for cm in comms:
        per_comm.append(rest[:len(cm.outs)])
        rest = rest[len(cm.outs):]
    return outs, per_comm


def _rope_tables(pos_row, inv_freq_col, comms=()):
    t_tok = pos_row.shape[1]
    tm = min(512, t_tok)

    def body(pos_ref, invf_ref, cos_ref, sin_ref):
        ang = pos_ref[...].astype(F32) * invf_ref[...]
        row = lax.broadcasted_iota(jnp.int32, ang.shape, 0)
        cos_ref[...] = jnp.cos(ang)
        sin_ref[...] = jnp.sin(ang) * jnp.where(row < HEAD_DIM // 2, -1.0, 1.0)

    return _carry(
        body, name="rope_tables", grid=(t_tok // tm,), comms=comms,
        in_specs=[pl.BlockSpec((1, tm), lambda i: (0, i)), pl.BlockSpec((HEAD_DIM, 1), lambda i: (0, 0))],
        out_specs=[pl.BlockSpec((HEAD_DIM, tm), lambda i: (0, i))] * 2,
        out_shape=[jax.ShapeDtypeStruct((HEAD_DIM, t_tok), F32)] * 2,
        args=(pos_row, inv_freq_col))


def _proj_in(x2, w_in_t, comms=()):
    t_tok, d = x2.shape
    d_in = w_in_t.shape[0]
    tm = min(512, t_tok)

    def body(x_ref, w_ref, h_ref, xb_ref):
        xb = x_ref[...].astype(BF16)
        xb_ref[...] = xb
        h_ref[...] = _dot(w_ref[...], xb, NT)

    return _carry(
        body, name="proj_in", grid=(t_tok // tm,), comms=comms,
        in_specs=[pl.BlockSpec((tm, d), lambda i: (i, 0)), pl.BlockSpec((d_in, d), lambda i: (0, 0))],
        out_specs=[pl.BlockSpec((d_in, tm), lambda i: (0, i)), pl.BlockSpec((tm, d), lambda i: (i, 0))],
        out_shape=[jax.ShapeDtypeStruct((d_in, t_tok), F32), jax.ShapeDtypeStruct((t_tok, d), BF16)],
        args=(x2, w_in_t))


MIX_BLOCKS = 2
MIX_W = MIX_BLOCKS * BLK


def _prev_block(i):
    return jnp.maximum(MIX_BLOCKS * i - 1, 0)


def _h_specs():
    kv_row = COL_K // (2 * D_KV)
    return [
        pl.BlockSpec((D_GMLP, MIX_W), lambda i: (0, i)),
        pl.BlockSpec((D_GMLP, MIX_W), lambda i: (1, i)),
        pl.BlockSpec((D_ATTN, MIX_W), lambda i: (2, i)),
        pl.BlockSpec((2 * D_KV, MIX_W), lambda i: (kv_row, i)),
        pl.BlockSpec((2 * D_KV, BLK), lambda i: (kv_row, _prev_block(i))),
    ]


def _table_specs():
    return [
        pl.BlockSpec((HEAD_DIM, MIX_W), lambda i: (0, i)),
        pl.BlockSpec((HEAD_DIM, MIX_W), lambda i: (0, i)),
        pl.BlockSpec((HEAD_DIM, BLK), lambda i: (0, _prev_block(i))),
        pl.BlockSpec((HEAD_DIM, BLK), lambda i: (0, _prev_block(i))),
    ]


def _cols(b):
    return slice(b * BLK, (b + 1) * BLK)


LSE_ROWS = 8
LSE_SPEC = pl.BlockSpec((LSE_ROWS, D_ATTN), lambda i: (i, 0))


def _block_inputs(b, i, kvc, kvp_ref, cos, sin, cosp_ref, sinp_ref, bias_ref):
    if b == 0:
        kv_prev, cos_prev, sin_prev, bias = kvp_ref[...], cosp_ref[...], sinp_ref[...], bias_ref[jnp.minimum(i, 1)]
    else:
        kv_prev, cos_prev, sin_prev, bias = kvc[:, _cols(b - 1)], cos[:, _cols(b - 1)], sin[:, _cols(b - 1)], bias_ref[1]
    return kvc[:, _cols(b)], kv_prev, cos[:, _cols(b)], sin[:, _cols(b)], cos_prev, sin_prev, bias


def _band_bias():
    ki = lax.broadcasted_iota(jnp.int32, (2, 2 * BLK, BLK), 1)
    qi = lax.broadcasted_iota(jnp.int32, (2, 2 * BLK, BLK), 2)
    later = lax.broadcasted_iota(jnp.int32, (2, 2 * BLK, BLK), 0) > 0
    dist = qi + BLK - ki
    return jnp.where((dist >= 0) & (dist < BLK) & ((ki >= BLK) | later), 0.0, NEG_INF).astype(F32)


BIAS_SPEC = pl.BlockSpec((2, 2 * BLK, BLK), lambda i: (0, 0, 0))


def _keys_values(kvc, kvp, cosc, sinc, cosp, sinp):
    kp, kc = _rope_t(kvp[:D_KV], cosp, sinp), _rope_t(kvc[:D_KV], cosc, sinc)
    k_t = jnp.concatenate([kp, kc], axis=1).astype(BF16)
    k_n = jnp.concatenate([kp.T, kc.T], axis=0).astype(BF16)
    v_t = jnp.concatenate([kvp[D_KV:], kvc[D_KV:]], axis=1).astype(BF16)
    return k_t, k_n, v_t


def _pad_head(th, kv):
    z = jnp.zeros_like(th)
    return jnp.concatenate([th, z] if kv == 0 else [z, th], axis=0)


def _group_lanes(parts):
    return jnp.concatenate(parts, axis=1)


def _softmax_sink_t(s, sink):
    m = jnp.maximum(jnp.max(s, axis=0, keepdims=True), sink)
    e = jnp.exp(s - m)
    denom = jnp.sum(e, axis=0, keepdims=True) + jnp.exp(sink - m)
    return e * (1.0 / denom), m + jnp.log(denom)


def _causal():
    row = lax.broadcasted_iota(jnp.int32, (BLK, BLK), 0)
    col = lax.broadcasted_iota(jnp.int32, (BLK, BLK), 1)
    return row >= col


def _mask_w_once(wsp_ref, wm_scr):
    @pl.when(pl.program_id(0) == 0)
    def _():
        causal = _causal()
        for hh in range(N_HEADS):
            wm_scr[hh] = jnp.where(causal, wsp_ref[hh], 0.0).astype(BF16)


def _mixer_fwd(h_t, cos_t, sin_t, w_spatial, b_spatial, vln_g, vln_b, sinks, band_bias, comms=()):
    t_tok = h_t.shape[1]
    group = N_HEADS // N_KV_HEADS

    def body(sinks_ref, u_ref, vg_ref, q_ref, kvc_ref, kvp_ref, cos_ref, sin_ref, cosp_ref, sinp_ref,
             wsp_ref, bsp_ref, g_ref, b_ref, bias_ref, cat_ref, lse_ref, wm_scr):
        i = pl.program_id(0)
        _mask_w_once(wsp_ref, wm_scr)
        lse_ref[...] = jnp.zeros_like(lse_ref)
        ua = _gelu(u_ref[...])
        vp, _, _ = _ln_fwd_t(_gelu(vg_ref[...]), g_ref[...], b_ref[...])
        vpb = vp.astype(BF16)
        for b in range(MIX_BLOCKS):
            for hh in range(N_HEADS):
                rows = slice(hh * HEAD_DIM, (hh + 1) * HEAD_DIM)
                mixed = _dot(vpb[rows, _cols(b)], wm_scr[hh], NT) + bsp_ref[hh:hh + 1, :]
                cat_ref[rows, _cols(b)] = (ua[rows, _cols(b)] * mixed).astype(BF16)

        kvc, cos, sin = kvc_ref[...], cos_ref[...], sin_ref[...]
        qr = (_rope_t(q_ref[...], cos, sin) * SCORE_SCALE).astype(BF16)
        sinks4 = [_group_lanes([jnp.full((1, BLK), sinks_ref[hh], F32) for hh in range(kv * group, (kv + 1) * group)])
                  for kv in range(N_KV_HEADS)]
        for b in range(MIX_BLOCKS):
            kv_cur, kv_prev, cosc, sinc, cosp, sinp, bias1 = _block_inputs(b, i, kvc, kvp_ref, cos, sin, cosp_ref, sinp_ref, bias_ref)
            _, k_n, v_t = _keys_values(kv_cur, kv_prev, cosc, sinc, cosp, sinp)
            bias = _group_lanes([bias1] * group)
            for kv in range(N_KV_HEADS):
                heads = range(kv * group, (kv + 1) * group)
                qs = _group_lanes([qr[hh * HEAD_DIM:(hh + 1) * HEAD_DIM, _cols(b)] for hh in heads])
                p, lse = _softmax_sink_t(_dot(k_n, _pad_head(qs, kv)) + bias, sinks4[kv])
                lse_ref[b * N_KV_HEADS + kv:b * N_KV_HEADS + kv + 1, :] = lse
                o = _dot(v_t[kv * HEAD_DIM:(kv + 1) * HEAD_DIM], p.astype(BF16)).astype(BF16)
                for j, hh in enumerate(heads):
                    cat_ref[D_GMLP + hh * HEAD_DIM:D_GMLP + (hh + 1) * HEAD_DIM, _cols(b)] = o[:, j * BLK:(j + 1) * BLK]

    full = lambda shape: pl.BlockSpec(shape, lambda i: (0,) * len(shape))
    return _carry(
        body, name="mixer_fwd", grid=(t_tok // MIX_W,), comms=comms,
        in_specs=[pl.BlockSpec(memory_space=pltpu.SMEM)] + _h_specs() + _table_specs() + [
            full((N_HEADS, BLK, BLK)), full((N_HEADS, BLK)), full((D_GMLP, 1)), full((D_GMLP, 1)), BIAS_SPEC],
        out_specs=[pl.BlockSpec((D_GMLP + D_ATTN, MIX_W), lambda i: (0, i)), LSE_SPEC],
        out_shape=[jax.ShapeDtypeStruct((D_GMLP + D_ATTN, t_tok), BF16),
                   jax.ShapeDtypeStruct((t_tok // MIX_W * LSE_ROWS, D_ATTN), F32)],
        scratch_shapes=[pltpu.VMEM((N_HEADS, BLK, BLK), BF16)],
        args=(sinks, h_t, h_t, h_t, h_t, h_t, cos_t, sin_t, cos_t, sin_t, w_spatial, b_spatial, vln_g, vln_b, band_bias))


def _proj_out(cat_t, x2, w_out_b, ln1_g, ln1_b, comms=()):
    t_tok, d = x2.shape
    tm = min(512, t_tok)

    def body(cat_ref, x_ref, w_ref, g_ref, b_ref, xhat_ref, rstd_ref, x1b_ref):
        x1, xhat, rstd = _ln_fwd(ALPHA * x_ref[...] + _dot(cat_ref[...], w_ref[...], TN), g_ref[...], b_ref[...])
        xhat_ref[...] = xhat
        rstd_ref[...] = rstd
        x1b_ref[...] = x1.astype(BF16)

    tok = lambda w: pl.BlockSpec((tm, w), lambda i: (i, 0))
    vec = pl.BlockSpec((1, d), lambda i: (0, 0))
    return _carry(
        body, name="proj_out", grid=(t_tok // tm,), comms=comms,
        in_specs=[pl.BlockSpec((cat_t.shape[0], tm), lambda i: (0, i)), tok(d), pl.BlockSpec(w_out_b.shape, lambda i: (0, 0)), vec, vec],
        out_specs=[tok(d), tok(1), tok(d)],
        out_shape=[jax.ShapeDtypeStruct((t_tok, d), F32), jax.ShapeDtypeStruct((t_tok, 1), F32), jax.ShapeDtypeStruct((t_tok, d), BF16)],
        args=(cat_t, x2, w_out_b, ln1_g, ln1_b))


def _ffn_fwd_bwd(xhat1, rstd1, x1b, target, w1_parts, w2_parts, ln1_g, ln1_b, ln2_g, ln2_b):
    t_tok, d = xhat1.shape
    n_part = len(w1_parts)
    n_chunk, _, fp = w1_parts[0].shape
    f = n_chunk * n_part * fp
    tm = min(FFN_ROWS, t_tok)

    def body(xhat1_ref, rstd1_ref, x1b_ref, tgt_ref, *refs):
        w1_hbm, w2_hbm = refs[:n_part], refs[n_part:2 * n_part]
        (g1_ref, b1_ref, g2_ref, b2_ref, act_ref, dpre_ref, dz2b_ref, dz1_ref, stats_ref,
         r_scr, w1_ref, w2_ref, w_sems) = refs[2 * n_part:]

        loads = [[], []]
        for j in range(n_chunk):
            for p in range(n_part):
                units = pl.ds((j * n_part + p) * fp, fp)
                loads[0].append(pltpu.make_async_copy(w1_hbm[p].at[j], w1_ref.at[:, units], w_sems.at[0, p, j]))
                loads[1].append(pltpu.make_async_copy(w2_hbm[p].at[j], w2_ref.at[units, :], w_sems.at[1, p, j]))

        @pl.when(pl.program_id(0) == 0)
        def _():
            stats_ref[...] = jnp.zeros_like(stats_ref)
            for cp in loads[0] + loads[1]:
                cp.start()
            for cp in loads[0]:
                cp.wait()

        g1, g2 = g1_ref[...], g2_ref[...]
        xhat1 = xhat1_ref[...]
        r_scr[...] = jnp.maximum(_dot(x1b_ref[...], w1_ref[...]), 0.0)
        r = r_scr[...]
        act = (r * r).astype(BF16)
        act_ref[...] = act

        @pl.when(pl.program_id(0) == 0)
        def _():
            for cp in loads[1]:
                cp.wait()

        ff = _dot(act, w2_ref[...])
        y, xhat2, rstd2 = _ln_fwd(ALPHA * (xhat1 * g1 + b1_ref[...]) + ff, g2, b2_ref[...])
        diff = y - tgt_ref[...]
        loss = 0.5 * jnp.sum(jnp.sum(diff * diff, axis=-1, keepdims=True) / d, axis=0, keepdims=True)
        dy = diff / d
        dz2 = _ln_bwd(dy, xhat2, rstd2, g2)
        dz2b = dz2.astype(BF16)
        dz2b_ref[...] = dz2b
        dpre = (_dot(dz2b, w2_ref[...], NT) * (2.0 * r_scr[...])).astype(BF16)
        dpre_ref[...] = dpre
        dx1 = ALPHA * dz2 + _dot(dpre, w1_ref[...], NT)
        dz1_ref[...] = _ln_bwd(dx1, xhat1, rstd1_ref[...], g1)
        stats_ref[0:1, :] += jnp.sum(dx1 * xhat1, axis=0, keepdims=True)
        stats_ref[1:2, :] += jnp.sum(dx1, axis=0, keepdims=True)
        stats_ref[2:3, :] += jnp.sum(dy * xhat2, axis=0, keepdims=True)
        stats_ref[3:4, :] += jnp.sum(dy, axis=0, keepdims=True)
        stats_ref[4:5, :] += jnp.broadcast_to(loss, (1, d))

    tok = lambda w: pl.BlockSpec((tm, w), lambda i: (i, 0))
    vec = pl.BlockSpec((1, d), lambda i: (0, 0))
    return _carry(
        body, name="ffn_fwd_bwd", grid=(t_tok // tm,),
        in_specs=[tok(d), tok(1), tok(d), tok(d)] + [ANY] * (2 * n_part) + [vec, vec, vec, vec],
        out_specs=[tok(f), tok(f), tok(d), tok(d), pl.BlockSpec((8, d), lambda i: (0, 0))],
        out_shape=[jax.ShapeDtypeStruct((t_tok, f), BF16), jax.ShapeDtypeStruct((t_tok, f), BF16),
                   jax.ShapeDtypeStruct((t_tok, d), BF16), jax.ShapeDtypeStruct((t_tok, d), F32), jax.ShapeDtypeStruct((8, d), F32)],
        scratch_shapes=[pltpu.VMEM((tm, f), F32), pltpu.VMEM((d, f), BF16), pltpu.VMEM((f, d), BF16),
                        pltpu.SemaphoreType.DMA((2, n_part, n_chunk))],
        args=(xhat1, rstd1, x1b, target, *w1_parts, *w2_parts, ln1_g, ln1_b, ln2_g, ln2_b))[0]


WGRAD_STEPS = [(True, 0), (True, 1), (False, 0), (True, 2), (False, 1), (True, 3), (False, 2), (False, 3)]
CHIP_FLIPS = [3, 1, 2, 0]


def _pick(table, s):
    out = table[-1]
    for i in range(len(table) - 2, -1, -1):
        out = jnp.where(s == i, table[i], out)
    return out


def _wgrad_shard(s, cc):
    q = jnp.bitwise_xor(cc[1], _pick([CHIP_FLIPS[k] for _, k in WGRAD_STEPS], s))
    return 2 * q + jnp.where(_pick([int(sibling) for sibling, _ in WGRAD_STEPS], s) == 1, 1 - cc[0], cc[0])


def _flipped(k):
    x, y, c = _place()
    return (1 - x if CHIP_FLIPS[k] // 2 else x, 1 - y if CHIP_FLIPS[k] % 2 else y, c)


def _wgrad_pair_sum(name, product, chunk, in_specs, args, core_chip, n_sent, comms=(), scratch_shapes=(), rider=None):
    half = N_DEV // 2
    n_in, n_out = len(in_specs), 2 + (0 < n_sent) + (n_sent < half - 1)
    more = rider or dict(in_specs=[], out_specs=[], out_shape=[], args=[])
    n_rin, n_rout = len(more["in_specs"]), len(more["out_specs"])

    def body(cc_ref, *refs):
        ins, rins, refs = refs[:n_in], refs[n_in:n_in + n_rin], refs[n_in + n_rin:]
        outs, routs, scr = refs[:n_out], refs[n_out:n_out + n_rout], refs[n_out + n_rout:]
        if rider:
            pl.when(pl.program_id(0) == 0)(lambda: rider["body"](cc_ref, rins, routs))
        (own_ref, recv_ref), from_chips_ref, wire_ref = outs[-2:], outs[0], outs[n_out - 3]
        send_buf, got, send_sems, recv_sems, got_sem, wire_buf, leave_sems, arrive_sems = scr[:8]
        s = pl.program_id(0)
        x, y, c = _place()
        def send(q):
            return pltpu.make_async_remote_copy(
                src_ref=send_buf.at[q % 2], dst_ref=recv_ref.at[q], send_sem=send_sems.at[q], recv_sem=recv_sems.at[q],
                device_id=(x, y, 1 - c), device_id_type=MESH)

        def load(q):
            return pltpu.make_async_copy(recv_ref.at[q], got, got_sem.at[0])

        def leave(k):
            if k < n_sent:
                return pltpu.make_async_remote_copy(
                    src_ref=wire_buf.at[k], dst_ref=from_chips_ref.at[k], send_sem=leave_sems.at[k],
                    recv_sem=arrive_sems.at[k], device_id=_flipped(k), device_id_type=MESH)
            return pltpu.make_async_copy(wire_buf.at[k], wire_ref.at[k - n_sent], leave_sems.at[k])

        for step, (sibling, q) in enumerate(WGRAD_STEPS):
            if not sibling:
                @pl.when(s == step)
                def _(q=q):
                    send(q).wait_recv()
                    load(q).start()

        g = product(_wgrad_shard(s, cc_ref), *ins, *scr[8:])

        for step, (sibling, q) in enumerate(WGRAD_STEPS):
            @pl.when(s == step)
            def _(sibling=sibling, q=q):
                if sibling:
                    if q >= 2:
                        send(q - 2).wait_send()
                    send_buf[q % 2] = g
                    send(q).start()
                    return
                load(q).wait()
                total = g + got[...]
                if q < half - 1:
                    wire_buf[q] = total.astype(BF16)
                    leave(q).start()
                else:
                    own_ref[...] = total

        @pl.when(s == N_DEV - 1)
        def _():
            for q in range(half - 2, half):
                send(q).wait_send()
            for k in range(half - 1):
                leave(k).wait()

    sums = lambda n: [jax.ShapeDtypeStruct((n,) + chunk, BF16)] if n else []
    sem = lambda n: pltpu.SemaphoreType.DMA((n,))
    res, per_comm = _carry(
        body, name=name, grid=(N_DEV,), comms=comms, prefetch=(core_chip,), in_specs=list(in_specs) + more["in_specs"],
        out_specs=[ANY] * (n_out - 2) + [pl.BlockSpec(chunk, lambda s, cc: (0, 0)), ANY] + more["out_specs"],
        out_shape=sums(n_sent) + sums(half - 1 - n_sent) + [jax.ShapeDtypeStruct(chunk, F32),
                                                            jax.ShapeDtypeStruct((half,) + chunk, F32)] + more["out_shape"],
        scratch_shapes=[pltpu.VMEM((2,) + chunk, F32), pltpu.VMEM(chunk, F32), sem(half), sem(half), sem(1),
                        pltpu.VMEM((half - 1,) + chunk, BF16), sem(half - 1), sem(half - 1), *scratch_shapes],
        args=list(args) + more["args"])
    first = res[0] if n_sent else None, res[n_out - 3] if n_sent < half - 1 else None
    return (*first, res[n_out - 2], per_comm, res[n_out:]) if rider else (*first, res[n_out - 2], per_comm)


def _resident(a):
    return pl.BlockSpec(a.shape, lambda s, cc: (0,) * a.ndim, pipeline_mode=pl.Buffered(1))


def _pair_sum_rider(parts, recv):
    _, r, c = parts.shape
    by_chip = parts.reshape(N_DEV // 2, 2, r, c)

    def body(cc_ref, ins, outs):
        (parts_ref, recv_ref), (wire_ref, own_ref) = ins, outs
        for q in range(N_DEV // 2):
            total = parts_ref[q] + recv_ref[q]
            wire_ref[q] = total.astype(BF16)

            @pl.when(cc_ref[1] == q)
            def _():
                own_ref[...] = total

    whole = lambda shape: pl.BlockSpec(shape, lambda s, cc: (0,) * len(shape))
    mine = pl.BlockSpec((N_DEV // 2, None, r, c), lambda s, cc: (0, cc[0], 0, 0), pipeline_mode=pl.Buffered(1))
    return dict(body=body, args=[by_chip, recv], in_specs=[mine, _resident(recv)],
                out_specs=[whole((N_DEV // 2, r, c)), whole((r, c))],
                out_shape=[jax.ShapeDtypeStruct((N_DEV // 2, r, c), BF16), jax.ShapeDtypeStruct((r, c), F32)])


def _ffn_wgrad(name, lhs, rhs, chunk_lhs, core_chip, n_sent, comms=(), rider=None):
    t_tok = lhs.shape[0]
    fc = (lhs if chunk_lhs else rhs).shape[1] // N_DEV
    chunked = pl.BlockSpec((t_tok, fc), lambda s, cc: (0, _wgrad_shard(s, cc)))

    def product(shard, lhs_ref, rhs_ref):
        return _dot(lhs_ref[...], rhs_ref[...], TN)

    return _wgrad_pair_sum(
        name, product, (fc, rhs.shape[1]) if chunk_lhs else (lhs.shape[1], fc),
        [chunked, _resident(rhs)] if chunk_lhs else [_resident(lhs), chunked], (lhs, rhs), core_chip, n_sent, comms,
        rider=rider)


def _proj_out_bwd(dz1, cat_t, w_out_b, comms=()):
    t_tok, d = dz1.shape
    d_mix = cat_t.shape[0]
    tm = min(512, t_tok)

    def body(dz1_ref, cat_ref, w_ref, dcat_ref, gw_ref):
        @pl.when(pl.program_id(0) == 0)
        def _():
            gw_ref[...] = jnp.zeros_like(gw_ref)

        dzb = dz1_ref[...].astype(BF16)
        dcat_ref[...] = _dot(w_ref[...], dzb, NT)
        gw_ref[...] += _dot(cat_ref[...], dzb)

    return _carry(
        body, name="proj_out_bwd", grid=(t_tok // tm,), comms=comms,
        in_specs=[pl.BlockSpec((tm, d), lambda i: (i, 0)), pl.BlockSpec((d_mix, tm), lambda i: (0, i)),
                  pl.BlockSpec((d_mix, d), lambda i: (0, 0))],
        out_specs=[pl.BlockSpec((d_mix, tm), lambda i: (0, i)), pl.BlockSpec((d_mix, d), lambda i: (0, 0))],
        out_shape=[jax.ShapeDtypeStruct((d_mix, t_tok), F32), jax.ShapeDtypeStruct((d_mix, d), F32)],
        args=(dz1, cat_t, w_out_b))


def _mixer_bwd(dcat_t, h_t, cos_t, sin_t, w_spatial, b_spatial, vln_g, vln_b, sinks, band_bias, lse, comms=()):
    t_tok = h_t.shape[1]
    nb, n_step = t_tok // BLK, t_tok // MIX_W
    group = N_HEADS // N_KV_HEADS

    def body(sinks_ref, dcat_ref, u_ref, vg_ref, q_ref, kvc_ref, kvp_ref, cos_ref, sin_ref, cosp_ref, sinp_ref,
             wsp_ref, bsp_ref, g_ref, b_ref, bias_ref, lse_ref, dh_ref, dkvc_ref, dkvp_ref, gwsb_ref, gbsp_ref, gvln_ref, gsink_ref,
             dg_acc, db_acc, wm_scr, gws_ref):
        i = pl.program_id(0)

        @pl.when(i == 0)
        def _():
            gws_ref[...] = jnp.zeros_like(gws_ref)
            gbsp_ref[...] = jnp.zeros_like(gbsp_ref)
            gsink_ref[...] = jnp.zeros_like(gsink_ref)
            dg_acc[...] = jnp.zeros_like(dg_acc)
            db_acc[...] = jnp.zeros_like(db_acc)

        _mask_w_once(wsp_ref, wm_scr)

        g = g_ref[...]
        ua, ua_grad = _gelu_and_grad(u_ref[...])
        vv, vv_grad = _gelu_and_grad(vg_ref[...])
        vp, vhat, rstd = _ln_fwd_t(vv, g, b_ref[...])
        vpb = vp.astype(BF16)
        da = dcat_ref[0:D_GMLP, :]
        dmixed = da * ua
        dvp_blocks = []
        for b in range(MIX_BLOCKS):
            dvp_parts = []
            for hh in range(N_HEADS):
                rows = slice(hh * HEAD_DIM, (hh + 1) * HEAD_DIM)
                vpb_h = vpb[rows, _cols(b)]
                mixed = _dot(vpb_h, wm_scr[hh], NT) + bsp_ref[hh:hh + 1, :]
                dh_ref[COL_U + hh * HEAD_DIM:COL_U + (hh + 1) * HEAD_DIM, _cols(b)] = (
                    da[rows, _cols(b)] * mixed * ua_grad[rows, _cols(b)]).astype(BF16)
                dm = dmixed[rows, _cols(b)]
                dmb = dm.astype(BF16)
                gbsp_ref[hh:hh + 1, :] += jnp.sum(dm, axis=0, keepdims=True)
                gws_ref[hh] += _dot(dmb, vpb_h, TN)
                dvp_parts.append(_dot(dmb, wm_scr[hh]))
            dvp_blocks.append(jnp.concatenate(dvp_parts, axis=0))
        dvp = jnp.concatenate(dvp_blocks, axis=1)
        dgv, dbv = dvp * vhat, dvp
        for b in range(MIX_BLOCKS):
            dg_acc[...] += dgv[:, _cols(b)]
            db_acc[...] += dbv[:, _cols(b)]
        dh_ref[COL_V:COL_V + D_GMLP, :] = (_ln_bwd_t(dvp, vhat, rstd, g) * vv_grad).astype(BF16)

        kvc, cos, sin = kvc_ref[...], cos_ref[...], sin_ref[...]
        qr = (_rope_t(q_ref[...], cos, sin) * SCORE_SCALE).astype(BF16)
        sinks4 = [_group_lanes([jnp.full((1, BLK), sinks_ref[hh], F32) for hh in range(kv * group, (kv + 1) * group)])
                  for kv in range(N_KV_HEADS)]
        dq_blocks, dkv_cur, dkv_prev = [], [], []
        for b in range(MIX_BLOCKS):
            kv_cur, kv_prev, cosc, sinc, cosp, sinp, bias1 = _block_inputs(b, i, kvc, kvp_ref, cos, sin, cosp_ref, sinp_ref, bias_ref)
            k_t, k_n, v_t = _keys_values(kv_cur, kv_prev, cosc, sinc, cosp, sinp)
            v_n = jnp.concatenate([kv_prev[D_KV:].T, kv_cur[D_KV:].T], axis=0).astype(BF16)
            bias = _group_lanes([bias1] * group)
            dk, dv, dq_parts = [], [], []
            for kv in range(N_KV_HEADS):
                heads = range(kv * group, (kv + 1) * group)
                kv_rows = slice(kv * HEAD_DIM, (kv + 1) * HEAD_DIM)
                qs = _group_lanes([qr[hh * HEAD_DIM:(hh + 1) * HEAD_DIM, _cols(b)] for hh in heads])
                dos = _group_lanes([dcat_ref[D_GMLP + hh * HEAD_DIM:D_GMLP + (hh + 1) * HEAD_DIM, _cols(b)]
                                    for hh in heads]).astype(BF16)
                lse_g = lse_ref[b * N_KV_HEADS + kv:b * N_KV_HEADS + kv + 1, :]
                p = jnp.exp(_dot(k_n, _pad_head(qs, kv)) + bias - lse_g)
                p_sink = jnp.exp(sinks4[kv] - lse_g)
                dp = _dot(v_n, _pad_head(dos, kv))
                delta = jnp.sum(p * dp, axis=0, keepdims=True)
                ds = (p * (dp - delta)).astype(BF16)
                dsink = p_sink * delta
                dq = _dot(k_t[kv_rows], ds) * SCORE_SCALE
                for j, hh in enumerate(heads):
                    gsink_ref[hh:hh + 1, :] -= dsink[:, j * BLK:(j + 1) * BLK]
                    dq_parts.append(dq[:, j * BLK:(j + 1) * BLK])
                dk.append(_dot(qs, ds, NT))
                dv.append(_dot(dos, p.astype(BF16), NT))
            dq_blocks.append(jnp.concatenate(dq_parts, axis=0))
            dk_all, dv_all = jnp.concatenate(dk, axis=0), jnp.concatenate(dv, axis=0)
            dkv_cur.append(jnp.concatenate([_rope_t(dk_all[:, BLK:], cosc, sinc, bwd=True), dv_all[:, BLK:]], axis=0))
            dkv_prev.append(jnp.concatenate([_rope_t(dk_all[:, :BLK], cosp, sinp, bwd=True), dv_all[:, :BLK]], axis=0))
        dh_ref[COL_Q:COL_Q + D_ATTN, :] = _rope_t(jnp.concatenate(dq_blocks, axis=1), cos, sin, bwd=True).astype(BF16)
        for b in range(MIX_BLOCKS):
            dkvc_ref[:, _cols(b)] = dkv_cur[b] + dkv_prev[b + 1] if b + 1 < MIX_BLOCKS else dkv_cur[b]
        dkvp_ref[...] = dkv_prev[0]

        @pl.when(i == n_step - 1)
        def _():
            causal = _causal()
            for hh in range(N_HEADS):
                gwsb_ref[hh] = jnp.where(causal, gws_ref[hh], 0.0).astype(BF16)
            gvln_ref[...] = jnp.zeros_like(gvln_ref)
            gvln_ref[0:1, :] = jnp.sum(dg_acc[...].T, axis=0, keepdims=True)
            gvln_ref[1:2, :] = jnp.sum(db_acc[...].T, axis=0, keepdims=True)

    full = lambda shape: pl.BlockSpec(shape, lambda i: (0,) * len(shape))
    return _carry(
        body, name="mixer_bwd", grid=(n_step,), comms=comms,
        in_specs=[pl.BlockSpec(memory_space=pltpu.SMEM), pl.BlockSpec((D_GMLP + D_ATTN, MIX_W), lambda i: (0, i))]
        + _h_specs() + _table_specs()
        + [full((N_HEADS, BLK, BLK)), full((N_HEADS, BLK)), full((D_GMLP, 1)), full((D_GMLP, 1)), BIAS_SPEC, LSE_SPEC],
        out_specs=[pl.BlockSpec((COL_K, MIX_W), lambda i: (0, i)), pl.BlockSpec((2 * D_KV, MIX_W), lambda i: (0, i)),
                   pl.BlockSpec((2 * D_KV, BLK), lambda i: (0, (i + n_step - 1) % n_step)),
                   full((N_HEADS, BLK, BLK)), full((N_HEADS, BLK)), full((8, D_GMLP)), full((N_HEADS, LANES))],
        out_shape=[jax.ShapeDtypeStruct((COL_K, t_tok), BF16), jax.ShapeDtypeStruct((2 * D_KV, t_tok), F32),
                   jax.ShapeDtypeStruct((2 * D_KV, n_step * BLK), F32),
                   jax.ShapeDtypeStruct((N_HEADS, BLK, BLK), BF16), jax.ShapeDtypeStruct((N_HEADS, BLK), F32),
                   jax.ShapeDtypeStruct((8, D_GMLP), F32), jax.ShapeDtypeStruct((N_HEADS, LANES), F32)],
        scratch_shapes=[pltpu.VMEM((D_GMLP, BLK), F32), pltpu.VMEM((D_GMLP, BLK), F32), pltpu.VMEM((N_HEADS, BLK, BLK), BF16),
                        pltpu.VMEM((N_HEADS, BLK, BLK), F32)],
        args=(sinks, dcat_t, h_t, h_t, h_t, h_t, h_t, cos_t, sin_t, cos_t, sin_t, w_spatial, b_spatial, vln_g, vln_b, band_bias, lse))


def _dkv_rows(dkvc_ref, dkvp_ref, width, store):
    for s in range(width // MIX_W):
        rest, last = slice(s * MIX_W, (s + 1) * MIX_W - BLK), slice((s + 1) * MIX_W - BLK, (s + 1) * MIX_W)
        store(rest, dkvc_ref[:, rest].astype(BF16))
        store(last, (dkvc_ref[:, last] + dkvp_ref[:, _cols(s)]).astype(BF16))


def _proj_in_wgrad(dh_b, dkvc_t, dkvp_t, xb, core_chip, comms=()):
    t_tok, d = xb.shape
    d_main, d_kv = dh_b.shape[0], dkvc_t.shape[0]
    rows = (d_main + d_kv) // N_DEV
    whole, cut = d_main // rows, d_main % rows

    def product(shard, dh_ref, dkvc_ref, dkvp_ref, xb_ref, dht_scr, sems):
        copies = [pltpu.make_async_copy(dh_ref.at[j * rows:(j + 1) * rows], dht_scr.at[j], sems.at[j]) for j in range(whole)]
        copies.append(pltpu.make_async_copy(dh_ref.at[whole * rows:d_main], dht_scr.at[whole, 0:cut], sems.at[whole]))

        @pl.when(pl.program_id(0) == 0)
        def _():
            for cp in copies:
                cp.start()

            def store(cols, val):
                dht_scr[whole, cut:rows, cols] = val[0:rows - cut]
                dht_scr[whole + 1, :, cols] = val[rows - cut:]

            _dkv_rows(dkvc_ref, dkvp_ref, t_tok, store)
            for cp in copies:
                cp.wait()

        return _dot(dht_scr[shard], xb_ref[...])

    return _wgrad_pair_sum(
        "proj_in_wgrad", product, (rows, d), [ANY, _resident(dkvc_t), _resident(dkvp_t), _resident(xb)],
        (dh_b, dkvc_t, dkvp_t, xb), core_chip, N_DEV // 2 - 1, comms,
        scratch_shapes=[pltpu.VMEM((N_DEV, rows, t_tok), BF16), pltpu.SemaphoreType.DMA((whole + 1,))])


def _proj_in_dgrad(dh_b, dkvc_t, dkvp_t, dz1, w_in_t, comms=()):
    t_tok, d = dz1.shape
    d_main, d_kv = dh_b.shape[0], dkvc_t.shape[0]
    tm = min(512, t_tok)

    def body(dh_ref, dkvc_ref, dkvp_ref, dz1_ref, w_ref, dx_ref, dkv_scr):
        def store(cols, val):
            dkv_scr[:, cols] = val

        _dkv_rows(dkvc_ref, dkvp_ref, tm, store)
        dx_ref[...] = (ALPHA * dz1_ref[...] + _dot(dh_ref[...], w_ref[0:d_main, :], TN)
                       + _dot(dkv_scr[...], w_ref[d_main:, :], TN))

    return _carry(
        body, name="proj_in_dgrad", grid=(t_tok // tm,), comms=comms,
        in_specs=[pl.BlockSpec((d_main, tm), lambda i: (0, i)), pl.BlockSpec((d_kv, tm), lambda i: (0, i)),
                  pl.BlockSpec((d_kv, tm // MIX_BLOCKS), lambda i: (0, i)),
                  pl.BlockSpec((tm, d), lambda i: (i, 0)), pl.BlockSpec((d_main + d_kv, d), lambda i: (0, 0))],
        out_specs=[pl.BlockSpec((tm, d), lambda i: (i, 0))],
        out_shape=[jax.ShapeDtypeStruct((t_tok, d), F32)],
        scratch_shapes=[pltpu.VMEM((d_kv, tm), BF16)],
        args=(dh_b, dkvc_t, dkvp_t, dz1, w_in_t))


def _adamw(w, g, m, v):
    m = ADAM_B1 * m + (1.0 - ADAM_B1) * g
    v = ADAM_B2 * v + (1.0 - ADAM_B2) * (g * g)
    m_hat = m / (1.0 - ADAM_B1 ** ADAM_STEP)
    v_hat = v / (1.0 - ADAM_B2 ** ADAM_STEP)
    delta = -ADAM_LR * (m_hat / (jnp.sqrt(v_hat) + ADAM_EPS) + ADAM_WD * w)
    return delta, m, v


ADAMW_STEPS = 4


def _adamw_shards(name, items, comms=(), rider=None):
    n_in, n_out = sum(4 + len(it[1]) for it in items), 4 * len(items)
    n_rin = len(rider["args"]) if rider else 0

    def body(*refs):
        ins, rins, outs, routs = refs[:n_in], refs[n_in:n_in + n_rin], refs[n_in + n_rin:n_in + n_rin + n_out], refs[n_in + n_rin + n_out:]
        for i, item in enumerate(items):
            (own_ref, w_ref, m_ref, v_ref), recv_refs, ins = ins[:4], ins[4:4 + len(item[1])], ins[4 + len(item[1]):]
            g = own_ref[...]
            for recv_ref in recv_refs:
                for k in range(recv_ref.shape[0]):
                    g = g + recv_ref[k].astype(F32)
            for o_ref, val in zip(outs[4 * i:4 * i + 4], (g,) + _adamw(w_ref[...], g, m_ref[...], v_ref[...])):
                o_ref[...] = val
        if rider:
            pl.when(pl.program_id(0) == 0)(lambda: rider["body"](rins, routs))

    in_specs, out_specs, out_shape, args = [], [], [], []
    for own, recvs, w, m, v in items:
        r, c = own.shape
        tiles = ADAMW_STEPS
        while (r // tiles) % BF16_ROWS:
            tiles //= 2
        blk = pl.BlockSpec((r // tiles, c), lambda s, k=ADAMW_STEPS // tiles: (s // k, 0))
        in_specs += [blk] * 4 + [pl.BlockSpec((a.shape[0], r // tiles, c), lambda s, k=ADAMW_STEPS // tiles: (0, s // k, 0))
                                 for a in recvs]
        out_specs += [blk] * 4
        out_shape += [jax.ShapeDtypeStruct((r, c), F32)] * 4
        args += [own, w, m, v, *recvs]
    if rider:
        in_specs, out_specs = in_specs + rider["in_specs"], out_specs + rider["out_specs"]
        out_shape, args = out_shape + rider["out_shape"], args + rider["args"]
    res, per_comm = _carry(body, name=name, grid=(ADAMW_STEPS,), comms=comms, in_specs=in_specs, out_specs=out_specs,
                           out_shape=out_shape, args=args)
    return [res[4 * i:4 * i + 4] for i in range(len(items))], res[n_out:], per_comm


VEC_VLN, VEC_LN1G, VEC_LN1B, VEC_LN2G, VEC_LN2B, VEC_SINK, VEC_LOSS, VEC_BSP, VEC_ROWS = 0, 1, 2, 3, 4, 5, 6, 8, 16


def _adamw_small(parts_w, parts_vec, params):
    n = parts_w.shape[0]
    flat = [a for p in params for a in p]
    shapes = [p[0].shape for p in params]

    def grads(gw, gv):
        return [gw, gv[VEC_VLN:VEC_VLN + 1, 0:D_GMLP], gv[VEC_VLN:VEC_VLN + 1, D_GMLP:2 * D_GMLP],
                gv[VEC_BSP:VEC_BSP + N_HEADS, 0:BLK], gv[VEC_LN1G:VEC_LN1G + 1], gv[VEC_LN1B:VEC_LN1B + 1],
                gv[VEC_LN2G:VEC_LN2G + 1], gv[VEC_LN2B:VEC_LN2B + 1], gv[VEC_SINK:VEC_SINK + 1, 0:N_HEADS]]

    def body(ins, outs):
        (pw_ref, pv_ref), ins = ins[:2], ins[2:]
        gw, gv = pw_ref[0].astype(F32), pv_ref[0]
        for k in range(1, n):
            gw, gv = gw + pw_ref[k].astype(F32), gv + pv_ref[k]
        for i, g in enumerate(grads(gw, gv)):
            w_ref, m_ref, v_ref = ins[3 * i:3 * i + 3]
            delta, m_new, v_new = _adamw(w_ref[...], g, m_ref[...], v_ref[...])
            for o_ref, val in zip(outs[4 * i:4 * i + 4], (g, delta, m_new, v_new)):
                o_ref[...] = val
        outs[-1][...] = gv[VEC_LOSS:VEC_LOSS + 1, 0:LANES]

    whole = lambda shape, **kw: pl.BlockSpec(shape, lambda i: (0,) * len(shape), **kw)
    once = dict(pipeline_mode=pl.Buffered(1))
    return dict(
        body=body, args=[parts_w, parts_vec, *flat],
        in_specs=[whole(parts_w.shape, **once), whole(parts_vec.shape, **once)] + [whole(a.shape, **once) for a in flat],
        out_specs=[whole(s) for s in shapes for _ in range(4)] + [whole((1, LANES))],
        out_shape=[jax.ShapeDtypeStruct(s, F32) for s in shapes for _ in range(4)] + [jax.ShapeDtypeStruct((1, LANES), F32)])


def kernel(x, positions, w_in, v_ln_g, v_ln_b, w_spatial, b_spatial, sinks, w_out, ln1_g, ln1_b, w_ff1, w_ff2, ln2_g, ln2_b, loss_target, m_w_in, m_v_ln_g, m_v_ln_b, m_w_spatial, m_b_spatial, m_sinks, m_w_out, m_ln1_g, m_ln1_b, m_w_ff1, m_w_ff2, m_ln2_g, m_ln2_b, v_w_in, v_v_ln_g, v_v_ln_b, v_w_spatial, v_b_spatial, v_sinks, v_w_out, v_ln1_g, v_ln1_b, v_w_ff1, v_w_ff2, v_ln2_g, v_ln2_b):
    _, t_tok, d = x.shape
    xi, yi, ci = _place()
    core_chip = jnp.stack([ci, 2 * xi + yi]).astype(jnp.int32)
    x2 = x.reshape(t_tok, d)
    target = loss_target.reshape(t_tok, d)
    inv_freq = ROPE_THETA ** (-jnp.arange(0, HEAD_DIM, 2, dtype=F32) / HEAD_DIM)
    wsp, bsp, sink_vec = w_spatial[0], b_spatial[0], sinks[0]
    vg_col, vb_col = v_ln_g.reshape(D_GMLP, 1), v_ln_b.reshape(D_GMLP, 1)
    big = {"in": w_in[0], "out": w_out[0], "ff1": w_ff1[0], "ff2": w_ff2[0]}
    half1, half2 = big["ff1"].shape[1] // 2, big["ff2"].shape[0] // 2
    w1_mine = [big["ff1"][:, :half1].astype(BF16), big["ff1"][:, half1:].astype(BF16)]
    w2_mine = [big["ff2"][:half2].astype(BF16), big["ff2"][half2:].astype(BF16)]

    (cos_t, sin_t), ((g_in,),) = _rope_tables(
        positions, jnp.tile(inv_freq, 2).reshape(HEAD_DIM, 1), comms=[_gather_comm([big["in"].T.astype(BF16)])])
    w_in_t = g_in.reshape(D_IN, d)
    (h_t, xb), ((g_out, w1_a),) = _proj_in(x2, w_in_t, comms=[_gather_comm([big["out"].astype(BF16), w1_mine[0]])])
    w_out_b = g_out.reshape(-1, d)
    band_bias = _band_bias()
    (cat_t, lse), ((w1_b, w2_a),) = _mixer_fwd(h_t, cos_t, sin_t, wsp, bsp, vg_col, vb_col, sink_vec, band_bias,
                                                comms=[_gather_comm([w1_mine[1], w2_mine[0]])])
    (xhat1, rstd1, x1b), ((w2_b,),) = _proj_out(cat_t, x2, w_out_b, ln1_g, ln1_b, comms=[_gather_comm([w2_mine[1]])])
    act_b, dpre_b, dz2b, dz1, stats = _ffn_fwd_bwd(xhat1, rstd1, x1b, target, [w1_a, w1_b], [w2_a, w2_b], ln1_g, ln1_b, ln2_g, ln2_b)

    (dcat_t, gw_out), _ = _proj_out_bwd(dz1, cat_t, w_out_b)
    p_out = gw_out.reshape(N_DEV, -1, d)
    r_ff1_a, wire_ff1, own_ff1, ((s_out,),) = _ffn_wgrad(
        "ffn_wgrad1", x1b, dpre_b, False, core_chip, 1, comms=[_sibling_comm([p_out])])
    _, wire_ff2, own_ff2, ((r_ff1_b,),), (wire_out, own_out) = _ffn_wgrad(
        "ffn_wgrad2", act_b, dz2b, True, core_chip, 0, comms=[_flips_comm(wire_ff1, 1)],
        rider=_pair_sum_rider(p_out, s_out))
    (dh_b, dkvc_t, dkvp_t, g_wsp, g_bsp, g_vln, g_sink), ((r_ff2,), (r_out,)) = _mixer_bwd(
        dcat_t, h_t, cos_t, sin_t, wsp, bsp, vg_col, vb_col, sink_vec, band_bias, lse,
        comms=[_flips_comm(wire_ff2, 0), _chips_comm([wire_out])])
    sink_row = jnp.pad(g_sink.sum(axis=1).reshape(1, N_HEADS), ((0, 0), (0, d - N_HEADS)))
    small_vec = jnp.concatenate([g_vln[0:2].reshape(1, d), stats[0:4], sink_row, stats[4:5], jnp.zeros((1, d), F32),
                                 jnp.pad(g_bsp, ((0, 0), (0, d - BLK)))], axis=0)
    r_in, _, own_in, ((parts_w, parts_vec),) = _proj_in_wgrad(
        dh_b, dkvc_t, dkvp_t, xb, core_chip, comms=[_gather_comm([g_wsp.reshape(-1, BLK), small_vec])])
    (grad_x,), _ = _proj_in_dgrad(dh_b, dkvc_t, dkvp_t, dz1, w_in_t)
    small = [(w_spatial, m_w_spatial, v_w_spatial), (v_ln_g, m_v_ln_g, v_v_ln_g), (v_ln_b, m_v_ln_b, v_v_ln_b),
             (b_spatial, m_b_spatial, v_b_spatial), (ln1_g, m_ln1_g, v_ln1_g), (ln1_b, m_ln1_b, v_ln1_b),
             (ln2_g, m_ln2_g, v_ln2_g), (ln2_b, m_ln2_b, v_ln2_b), (sinks, m_sinks, v_sinks)]
    views = [(-1, BLK), None, None, (N_HEADS, BLK)] + [None] * 5
    small_update = _adamw_small(parts_w, parts_vec, [
        tuple(a if vw is None else a.reshape(vw) for a in p) for p, vw in zip(small, views)])
    (out_out, ff1_out, ff2_out, in_out_t), small_res, _ = _adamw_shards("adamw_all", [
        (own_out, [r_out], big["out"], m_w_out[0], v_w_out[0]),
        (own_ff1, [r_ff1_a, r_ff1_b], big["ff1"], m_w_ff1[0], v_w_ff1[0]),
        (own_ff2, [r_ff2], big["ff2"], m_w_ff2[0], v_w_ff2[0]),
        (own_in, [r_in], big["in"].T, m_w_in[0].T, v_w_in[0].T)], rider=small_update)
    in_out = [o.T for o in in_out_t]
    small_out = [[o.reshape(p[0].shape) for o in small_res[4 * i:4 * i + 4]] for i, p in enumerate(small)]
    loss = small_res[-1][0, 0]

    big_out = {0: in_out, 6: out_out, 9: ff1_out, 10: ff2_out}
    small_slot = {3: 0, 1: 1, 2: 2, 4: 3, 7: 4, 8: 5, 11: 6, 12: 7, 5: 8}
    outs = [loss, grad_x.reshape(x.shape)]
    for kind in range(4):
        for wi in range(13):
            outs.append(big_out[wi][kind][None] if wi in big_out else small_out[small_slot[wi]][kind])
    return tuple(outs)
```

```python
import math

import jax
import jax.numpy as jnp
from jax import lax
from jax.experimental import pallas as pl
from jax.experimental.pallas import tpu as pltpu

F32 = jnp.float32
BF16 = jnp.bfloat16
MESH = pl.DeviceIdType.MESH

HEAD_DIM = 64
N_HEADS = 8
N_KV_HEADS = 2
BLK = 128
D_GMLP = N_HEADS * HEAD_DIM
D_ATTN = N_HEADS * HEAD_DIM
D_KV = N_KV_HEADS * HEAD_DIM
D_IN = 2 * D_GMLP + D_ATTN + 2 * D_KV
COL_U, COL_V, COL_Q, COL_K = 0, D_GMLP, 2 * D_GMLP, 2 * D_GMLP + D_ATTN
ROPE_THETA = 10000.0
LN_EPS = 1e-5
ALPHA = 2.0 ** 0.25
NEG_INF = -1e30
SCORE_SCALE = 1.0 / math.sqrt(HEAD_DIM)
ADAM_LR, ADAM_B1, ADAM_B2, ADAM_EPS, ADAM_WD, ADAM_STEP = 0.001, 0.9, 0.999, 1e-08, 0.01, 10
N_DEV = 8
LANES = 128
VMEM_LIMIT = 56 * 1024 * 1024
FFN_ROWS = 256

NT = (((1,), (1,)), ((), ()))
TN = (((0,), (0,)), ((), ()))


def _params(*sem):
    return pltpu.CompilerParams(dimension_semantics=sem, vmem_limit_bytes=VMEM_LIMIT)


def _dot(a, b, dims=None):
    if dims is None:
        return jnp.dot(a, b, preferred_element_type=F32)
    return lax.dot_general(a, b, dims, preferred_element_type=F32)


def _mean(a):
    return jnp.mean(a, axis=-1, keepdims=True)


def _ln_fwd(z, g, b):
    zc = z - _mean(z)
    rstd = lax.rsqrt(_mean(zc * zc) + LN_EPS)
    xhat = zc * rstd
    return xhat * g + b, xhat, rstd


def _ln_bwd(dy, xhat, rstd, g):
    dxhat = dy * g
    return rstd * (dxhat - _mean(dxhat) - xhat * _mean(dxhat * xhat))


_GELU_C = math.sqrt(2.0 / math.pi)


def _gelu(x):
    t = jnp.tanh(_GELU_C * (x + 0.044715 * (x * x * x)))
    return 0.5 * x * (1.0 + t)


def _gelu_and_grad(x):
    x2 = x * x
    t = jnp.tanh(_GELU_C * (x + 0.044715 * (x2 * x)))
    hx, ht = 0.5 * x, 0.5 * (1.0 + t)
    return x * ht, ht + hx * (1.0 - t * t) * (_GELU_C * (1.0 + 3.0 * 0.044715 * x2))


def _mean0(a):
    return jnp.mean(a, axis=0, keepdims=True)


def _ln_fwd_t(z, g, b):
    zc = z - _mean0(z)
    rstd = lax.rsqrt(_mean0(zc * zc) + LN_EPS)
    xhat = zc * rstd
    return xhat * g + b, xhat, rstd


def _ln_bwd_t(dy, xhat, rstd, g):
    dxhat = dy * g
    return rstd * (dxhat - _mean0(dxhat) - xhat * _mean0(dxhat * xhat))


def _rope_t(t, cos, sin_signed, bwd=False):
    half = HEAD_DIM // 2
    outs = []
    for r in range(0, t.shape[0], HEAD_DIM):
        th = t[r:r + HEAD_DIM]
        sw = jnp.concatenate([th[half:], th[:half]], axis=0) * sin_signed
        outs.append(th * cos - sw if bwd else th * cos + sw)
    return jnp.concatenate(outs, axis=0)


ANY = pl.BlockSpec(memory_space=pl.ANY)
GATHER_PIECES = 2
BF16_ROWS = 16


def _place():
    return lax.axis_index("x"), lax.axis_index("y"), lax.axis_index("c")


class _Comm:
    def __init__(self, ins, outs, sems, start, finish):
        self.ins, self.outs, self.sems, self.start, self.finish = ins, outs, sems, start, finish


def _gather_comm(arrs):
    n = len(arrs)
    pieces = []
    for a, arr in enumerate(arrs):
        k = GATHER_PIECES
        while arr.shape[0] % (k * BF16_ROWS):
            k //= 2
        pieces += [(a, p * (arr.shape[0] // k), arr.shape[0] // k) for p in range(k)]

    def parts(ins, outs, sems):
        send_sems, recv_sems, local_sems = sems
        x, y, c = _place()
        me, sibling = (x, y, c), (x, y, 1 - c)
        chips = [(1 - x, y), (x, 1 - y), (1 - x, 1 - y)]

        def copy(u, k, block, to, local=False):
            a, r0, nr = pieces[u]
            px, py, pc = block
            dst = outs[a].at[4 * px + 2 * py + pc, pl.ds(r0, nr)]
            return pltpu.make_async_remote_copy(
                src_ref=ins[a].at[pl.ds(r0, nr)] if local else dst, dst_ref=dst,
                send_sem=send_sems.at[u, k], recv_sem=recv_sems.at[u, k], device_id=to, device_id_type=MESH)

        mine = [pltpu.make_async_copy(ins[a], outs[a].at[4 * x + 2 * y + c], local_sems.at[a]) for a in range(n)]
        first = []
        for u in range(len(pieces)):
            first.append(copy(u, 0, me, sibling, local=True))
            first += [copy(u, 1 + j, me, (*chip, c), local=True) for j, chip in enumerate(chips)]
        return copy, mine, first, me, sibling, chips, c

    def start(ins, outs, sems):
        _, mine, first, *_ = parts(ins, outs, sems)
        for cp in mine + first:
            cp.start()

    def finish(ins, outs, sems):
        copy, mine, first, me, sibling, chips, c = parts(ins, outs, sems)
        passed = []
        for u in range(len(pieces)):
            for j, chip in enumerate(chips):
                copy(u, 1 + j, (*chip, c), me).wait_recv()
                fwd = copy(u, 4 + j, (*chip, c), sibling)
                fwd.start()
                passed.append(fwd)
        for u in range(len(pieces)):
            copy(u, 0, sibling, me).wait_recv()
            for j, chip in enumerate(chips):
                copy(u, 4 + j, (*chip, 1 - c), me).wait_recv()
        for cp in first + passed:
            cp.wait_send()
        for cp in mine:
            cp.wait()

    return _Comm(list(arrs), [jax.ShapeDtypeStruct((N_DEV,) + a.shape, a.dtype) for a in arrs],
                 [pltpu.SemaphoreType.DMA((len(pieces), 7)), pltpu.SemaphoreType.DMA((len(pieces), 7)),
                  pltpu.SemaphoreType.DMA((n,))], start, finish)


def _sibling_comm(parts):
    n = len(parts)

    def copies(ins, outs, sems):
        x, y, c = _place()
        return [pltpu.make_async_remote_copy(
            src_ref=ins[a].at[2 * q + (1 - c)], dst_ref=outs[a].at[q],
            send_sem=sems[0].at[a, q], recv_sem=sems[1].at[a, q],
            device_id=(x, y, 1 - c), device_id_type=MESH) for a in range(n) for q in range(4)]

    return _Comm(list(parts), [jax.ShapeDtypeStruct((4,) + p.shape[1:], p.dtype) for p in parts],
                 [pltpu.SemaphoreType.DMA((n, 4)), pltpu.SemaphoreType.DMA((n, 4))],
                 lambda *r: [cp.start() for cp in copies(*r)], lambda *r: [cp.wait() for cp in copies(*r)])


def _chips_comm(chip_parts):
    n = len(chip_parts)

    def copies(ins, outs, sems):
        x, y, c = _place()
        chips = [(1 - x, y), (x, 1 - y), (1 - x, 1 - y)]
        return [pltpu.make_async_remote_copy(
            src_ref=ins[a].at[2 * px + py], dst_ref=outs[a].at[k],
            send_sem=sems[0].at[a, k], recv_sem=sems[1].at[a, k],
            device_id=(px, py, c), device_id_type=MESH) for a in range(n) for k, (px, py) in enumerate(chips)]

    return _Comm(list(chip_parts), [jax.ShapeDtypeStruct((3,) + p.shape[1:], p.dtype) for p in chip_parts],
                 [pltpu.SemaphoreType.DMA((n, 3)), pltpu.SemaphoreType.DMA((n, 3))],
                 lambda *r: [cp.start() for cp in copies(*r)], lambda *r: [cp.wait() for cp in copies(*r)])


def _flips_comm(sums, first):
    m = sums.shape[0]

    def copies(ins, outs, sems):
        return [pltpu.make_async_remote_copy(
            src_ref=ins[0].at[j], dst_ref=outs[0].at[j], send_sem=sems[0].at[j], recv_sem=sems[1].at[j],
            device_id=_flipped(first + j), device_id_type=MESH) for j in range(m)]

    return _Comm([sums], [jax.ShapeDtypeStruct(sums.shape, sums.dtype)],
                 [pltpu.SemaphoreType.DMA((m,)), pltpu.SemaphoreType.DMA((m,))],
                 lambda *r: [cp.start() for cp in copies(*r)], lambda *r: [cp.wait() for cp in copies(*r)])


def _carry(body, *, name, grid, in_specs, out_specs, out_shape, args, comms=(), scratch_shapes=(), prefetch=()):
    n_pre, n_in, n_out, n_scr = len(prefetch), len(in_specs), len(out_specs), len(scratch_shapes)
    c_ins = [a for cm in comms for a in cm.ins]
    c_outs = [s for cm in comms for s in cm.outs]
    c_sems = [s for cm in comms for s in cm.sems]

    def wrapped(*refs):
        pre, refs = refs[:n_pre], refs[n_pre:]
        ins, refs = refs[:n_in], refs[n_in:]
        cins, refs = refs[:len(c_ins)], refs[len(c_ins):]
        outs, refs = refs[:n_out], refs[n_out:]
        couts, refs = refs[:len(c_outs)], refs[len(c_outs):]
        scr, sems = refs[:n_scr], refs[n_scr:]
        groups, i0, o0, s0 = [], 0, 0, 0
        for cm in comms:
            groups.append((cm, cins[i0:i0 + len(cm.ins)], couts[o0:o0 + len(cm.outs)], sems[s0:s0 + len(cm.sems)]))
            i0, o0, s0 = i0 + len(cm.ins), o0 + len(cm.outs), s0 + len(cm.sems)
        first = pl.program_id(0) == 0
        last = pl.program_id(0) == grid[0] - 1
        for ax in range(1, len(grid)):
            first = first & (pl.program_id(ax) == 0)
            last = last & (pl.program_id(ax) == grid[ax] - 1)
        if comms:
            @pl.when(first)
            def _():
                for cm, ci, co, cs in groups:
                    cm.start(ci, co, cs)
        body(*pre, *ins, *outs, *scr)
        if comms:
            @pl.when(last)
            def _():
                for cm, ci, co, cs in groups:
                    cm.finish(ci, co, cs)

    grid_spec = pltpu.PrefetchScalarGridSpec(
        num_scalar_prefetch=n_pre, grid=grid,
        in_specs=list(in_specs) + [ANY] * len(c_ins), out_specs=list(out_specs) + [ANY] * len(c_outs),
        scratch_shapes=list(scratch_shapes) + c_sems)
    res = pl.pallas_call(
        wrapped, name=name, grid_spec=grid_spec, out_shape=list(out_shape) + c_outs,
        compiler_params=_params(*(["arbitrary"] * len(grid))),
    )(*prefetch, *args, *c_ins)
    outs, rest, per_comm = res[:n_out], res[n_out:], []
    for cm in comms:
        per_comm.append(rest[:len(cm.outs)])
        rest = rest[len(cm.outs):]
    return outs, per_comm


def _rope_tables(pos_row, inv_freq_col, comms=()):
    t_tok = pos_row.shape[1]
    tm = min(512, t_tok)

    def body(pos_ref, invf_ref, cos_ref, sin_ref):
        ang = pos_ref[...].astype(F32) * invf_ref[...]
        row = lax.broadcasted_iota(jnp.int32, ang.shape, 0)
        cos_ref[...] = jnp.cos(ang)
        sin_ref[...] = jnp.sin(ang) * jnp.where(row < HEAD_DIM // 2, -1.0, 1.0)

    return _carry(
        body, name="rope_tables", grid=(t_tok // tm,), comms=comms,
        in_specs=[pl.BlockSpec((1, tm), lambda i: (0, i)), pl.BlockSpec((HEAD_DIM, 1), lambda i: (0, 0))],
        out_specs=[pl.BlockSpec((HEAD_DIM, tm), lambda i: (0, i))] * 2,
        out_shape=[jax.ShapeDtypeStruct((HEAD_DIM, t_tok), F32)] * 2,
        args=(pos_row, inv_freq_col))


def _proj_in(x2, w_in_t, comms=()):
    t_tok, d = x2.shape
    d_in = w_in_t.shape[0]
    tm = min(512, t_tok)

    def body(x_ref, w_ref, h_ref, xb_ref):
        xb = x_ref[...].astype(BF16)
        xb_ref[...] = xb
        h_ref[...] = _dot(w_ref[...], xb, NT)

    return _carry(
        body, name="proj_in", grid=(t_tok // tm,), comms=comms,
        in_specs=[pl.BlockSpec((tm, d), lambda i: (i, 0)), pl.BlockSpec((d_in, d), lambda i: (0, 0))],
        out_specs=[pl.BlockSpec((d_in, tm), lambda i: (0, i)), pl.BlockSpec((tm, d), lambda i: (i, 0))],
        out_shape=[jax.ShapeDtypeStruct((d_in, t_tok), F32), jax.ShapeDtypeStruct((t_tok, d), BF16)],
        args=(x2, w_in_t))


MIX_BLOCKS = 2
MIX_W = MIX_BLOCKS * BLK


def _prev_block(i):
    return jnp.maximum(MIX_BLOCKS * i - 1, 0)


def _h_specs():
    kv_row = COL_K // (2 * D_KV)
    return [
        pl.BlockSpec((D_GMLP, MIX_W), lambda i: (0, i)),
        pl.BlockSpec((D_GMLP, MIX_W), lambda i: (1, i)),
        pl.BlockSpec((D_ATTN, MIX_W), lambda i: (2, i)),
        pl.BlockSpec((2 * D_KV, MIX_W), lambda i: (kv_row, i)),
        pl.BlockSpec((2 * D_KV, BLK), lambda i: (kv_row, _prev_block(i))),
    ]


def _table_specs():
    return [
        pl.BlockSpec((HEAD_DIM, MIX_W), lambda i: (0, i)),
        pl.BlockSpec((HEAD_DIM, MIX_W), lambda i: (0, i)),
        pl.BlockSpec((HEAD_DIM, BLK), lambda i: (0, _prev_block(i))),
        pl.BlockSpec((HEAD_DIM, BLK), lambda i: (0, _prev_block(i))),
    ]


def _cols(b):
    return slice(b * BLK, (b + 1) * BLK)


LSE_ROWS = 8
LSE_SPEC = pl.BlockSpec((LSE_ROWS, D_ATTN), lambda i: (i, 0))


def _block_inputs(b, i, kvc, kvp_ref, cos, sin, cosp_ref, sinp_ref, bias_ref):
    if b == 0:
        kv_prev, cos_prev, sin_prev, bias = kvp_ref[...], cosp_ref[...], sinp_ref[...], bias_ref[jnp.minimum(i, 1)]
    else:
        kv_prev, cos_prev, sin_prev, bias = kvc[:, _cols(b - 1)], cos[:, _cols(b - 1)], sin[:, _cols(b - 1)], bias_ref[1]
    return kvc[:, _cols(b)], kv_prev, cos[:, _cols(b)], sin[:, _cols(b)], cos_prev, sin_prev, bias


def _band_bias():
    ki = lax.broadcasted_iota(jnp.int32, (2, 2 * BLK, BLK), 1)
    qi = lax.broadcasted_iota(jnp.int32, (2, 2 * BLK, BLK), 2)
    later = lax.broadcasted_iota(jnp.int32, (2, 2 * BLK, BLK), 0) > 0
    dist = qi + BLK - ki
    return jnp.where((dist >= 0) & (dist < BLK) & ((ki >= BLK) | later), 0.0, NEG_INF).astype(F32)


BIAS_SPEC = pl.BlockSpec((2, 2 * BLK, BLK), lambda i: (0, 0, 0))


def _keys_values(kvc, kvp, cosc, sinc, cosp, sinp):
    kp, kc = _rope_t(kvp[:D_KV], cosp, sinp), _rope_t(kvc[:D_KV], cosc, sinc)
    k_t = jnp.concatenate([kp, kc], axis=1).astype(BF16)
    k_n = jnp.concatenate([kp.T, kc.T], axis=0).astype(BF16)
    v_t = jnp.concatenate([kvp[D_KV:], kvc[D_KV:]], axis=1).astype(BF16)
    return k_t, k_n, v_t


def _pad_head(th, kv):
    z = jnp.zeros_like(th)
    return jnp.concatenate([th, z] if kv == 0 else [z, th], axis=0)


def _group_lanes(parts):
    return jnp.concatenate(parts, axis=1)


def _softmax_sink_t(s, sink):
    m = jnp.maximum(jnp.max(s, axis=0, keepdims=True), sink)
    e = jnp.exp(s - m)
    denom = jnp.sum(e, axis=0, keepdims=True) + jnp.exp(sink - m)
    return e * (1.0 / denom), m + jnp.log(denom)


def _causal():
    row = lax.broadcasted_iota(jnp.int32, (BLK, BLK), 0)
    col = lax.broadcasted_iota(jnp.int32, (BLK, BLK), 1)
    return row >= col


def _mask_w_once(wsp_ref, wm_scr):
    @pl.when(pl.program_id(0) == 0)
    def _():
        causal = _causal()
        for hh in range(N_HEADS):
            wm_scr[hh] = jnp.where(causal, wsp_ref[hh], 0.0).astype(BF16)


def _mixer_fwd(h_t, cos_t, sin_t, w_spatial, b_spatial, vln_g, vln_b, sinks, band_bias, comms=()):
    t_tok = h_t.shape[1]
    group = N_HEADS // N_KV_HEADS

    def body(sinks_ref, u_ref, vg_ref, q_ref, kvc_ref, kvp_ref, cos_ref, sin_ref, cosp_ref, sinp_ref,
             wsp_ref, bsp_ref, g_ref, b_ref, bias_ref, cat_ref, lse_ref, wm_scr):
        i = pl.program_id(0)
        _mask_w_once(wsp_ref, wm_scr)
        lse_ref[...] = jnp.zeros_like(lse_ref)
        ua = _gelu(u_ref[...])
        vp, _, _ = _ln_fwd_t(_gelu(vg_ref[...]), g_ref[...], b_ref[...])
        vpb = vp.astype(BF16)
        for b in range(MIX_BLOCKS):
            for hh in range(N_HEADS):
                rows = slice(hh * HEAD_DIM, (hh + 1) * HEAD_DIM)
                mixed = _dot(vpb[rows, _cols(b)], wm_scr[hh], NT) + bsp_ref[hh:hh + 1, :]
                cat_ref[rows, _cols(b)] = (ua[rows, _cols(b)] * mixed).astype(BF16)

        kvc, cos, sin = kvc_ref[...], cos_ref[...], sin_ref[...]
        qr = (_rope_t(q_ref[...], cos, sin) * SCORE_SCALE).astype(BF16)
        sinks4 = [_group_lanes([jnp.full((1, BLK), sinks_ref[hh], F32) for hh in range(kv * group, (kv + 1) * group)])
                  for kv in range(N_KV_HEADS)]
        for b in range(MIX_BLOCKS):
            kv_cur, kv_prev, cosc, sinc, cosp, sinp, bias1 = _block_inputs(b, i, kvc, kvp_ref, cos, sin, cosp_ref, sinp_ref, bias_ref)
            _, k_n, v_t = _keys_values(kv_cur, kv_prev, cosc, sinc, cosp, sinp)
            bias = _group_lanes([bias1] * group)
            for kv in range(N_KV_HEADS):
                heads = range(kv * group, (kv + 1) * group)
                qs = _group_lanes([qr[hh * HEAD_DIM:(hh + 1) * HEAD_DIM, _cols(b)] for hh in heads])
                p, lse = _softmax_sink_t(_dot(k_n, _pad_head(qs, kv)) + bias, sinks4[kv])
                lse_ref[b * N_KV_HEADS + kv:b * N_KV_HEADS + kv + 1, :] = lse
                o = _dot(v_t[kv * HEAD_DIM:(kv + 1) * HEAD_DIM], p.astype(BF16)).astype(BF16)
                for j, hh in enumerate(heads):
                    cat_ref[D_GMLP + hh * HEAD_DIM:D_GMLP + (hh + 1) * HEAD_DIM, _cols(b)] = o[:, j * BLK:(j + 1) * BLK]

    full = lambda shape: pl.BlockSpec(shape, lambda i: (0,) * len(shape))
    return _carry(
        body, name="mixer_fwd", grid=(t_tok // MIX_W,), comms=comms,
        in_specs=[pl.BlockSpec(memory_space=pltpu.SMEM)] + _h_specs() + _table_specs() + [
            full((N_HEADS, BLK, BLK)), full((N_HEADS, BLK)), full((D_GMLP, 1)), full((D_GMLP, 1)), BIAS_SPEC],
        out_specs=[pl.BlockSpec((D_GMLP + D_ATTN, MIX_W), lambda i: (0, i)), LSE_SPEC],
        out_shape=[jax.ShapeDtypeStruct((D_GMLP + D_ATTN, t_tok), BF16),
                   jax.ShapeDtypeStruct((t_tok // MIX_W * LSE_ROWS, D_ATTN), F32)],
        scratch_shapes=[pltpu.VMEM((N_HEADS, BLK, BLK), BF16)],
        args=(sinks, h_t, h_t, h_t, h_t, h_t, cos_t, sin_t, cos_t, sin_t, w_spatial, b_spatial, vln_g, vln_b, band_bias))


def _proj_out(cat_t, x2, w_out_b, ln1_g, ln1_b, comms=()):
    t_tok, d = x2.shape
    tm = min(512, t_tok)

    def body(cat_ref, x_ref, w_ref, g_ref, b_ref, xhat_ref, rstd_ref, x1b_ref):
        x1, xhat, rstd = _ln_fwd(ALPHA * x_ref[...] + _dot(cat_ref[...], w_ref[...], TN), g_ref[...], b_ref[...])
        xhat_ref[...] = xhat
        rstd_ref[...] = rstd
        x1b_ref[...] = x1.astype(BF16)

    tok = lambda w: pl.BlockSpec((tm, w), lambda i: (i, 0))
    vec = pl.BlockSpec((1, d), lambda i: (0, 0))
    return _carry(
        body, name="proj_out", grid=(t_tok // tm,), comms=comms,
        in_specs=[pl.BlockSpec((cat_t.shape[0], tm), lambda i: (0, i)), tok(d), pl.BlockSpec(w_out_b.shape, lambda i: (0, 0)), vec, vec],
        out_specs=[tok(d), tok(1), tok(d)],
        out_shape=[jax.ShapeDtypeStruct((t_tok, d), F32), jax.ShapeDtypeStruct((t_tok, 1), F32), jax.ShapeDtypeStruct((t_tok, d), BF16)],
        args=(cat_t, x2, w_out_b, ln1_g, ln1_b))


def _ffn_fwd_bwd(xhat1, rstd1, x1b, target, w1_parts, w2_parts, ln1_g, ln1_b, ln2_g, ln2_b):
    t_tok, d = xhat1.shape
    n_part = len(w1_parts)
    n_chunk, _, fp = w1_parts[0].shape
    f = n_chunk * n_part * fp
    tm = min(FFN_ROWS, t_tok)

    def body(xhat1_ref, rstd1_ref, x1b_ref, tgt_ref, *refs):
        w1_hbm, w2_hbm = refs[:n_part], refs[n_part:2 * n_part]
        (g1_ref, b1_ref, g2_ref, b2_ref, act_ref, dpre_ref, dz2b_ref, dz1_ref, stats_ref,
         r_scr, w1_ref, w2_ref, w_sems) = refs[2 * n_part:]

        @pl.when(pl.program_id(0) == 0)
        def _():
            stats_ref[...] = jnp.zeros_like(stats_ref)
            loads = []
            for j in range(n_chunk):
                for p in range(n_part):
                    units = pl.ds((j * n_part + p) * fp, fp)
                    loads.append(pltpu.make_async_copy(w1_hbm[p].at[j], w1_ref.at[:, units], w_sems.at[0, p, j]))
                    loads.append(pltpu.make_async_copy(w2_hbm[p].at[j], w2_ref.at[units, :], w_sems.at[1, p, j]))
            for cp in loads:
                cp.start()
            for cp in loads:
                cp.wait()

        g1, g2 = g1_ref[...], g2_ref[...]
        xhat1 = xhat1_ref[...]
        r_scr[...] = jnp.maximum(_dot(x1b_ref[...], w1_ref[...]), 0.0)
        r = r_scr[...]
        act = (r * r).astype(BF16)
        act_ref[...] = act
        ff = _dot(act, w2_ref[...])
        y, xhat2, rstd2 = _ln_fwd(ALPHA * (xhat1 * g1 + b1_ref[...]) + ff, g2, b2_ref[...])
        diff = y - tgt_ref[...]
        loss = 0.5 * jnp.sum(jnp.sum(diff * diff, axis=-1, keepdims=True) / d, axis=0, keepdims=True)
        dy = diff / d
        dz2 = _ln_bwd(dy, xhat2, rstd2, g2)
        dz2b = dz2.astype(BF16)
        dz2b_ref[...] = dz2b
        dpre = (_dot(dz2b, w2_ref[...], NT) * (2.0 * r_scr[...])).astype(BF16)
        dpre_ref[...] = dpre
        dx1 = ALPHA * dz2 + _dot(dpre, w1_ref[...], NT)
        dz1_ref[...] = _ln_bwd(dx1, xhat1, rstd1_ref[...], g1)
        stats_ref[0:1, :] += jnp.sum(dx1 * xhat1, axis=0, keepdims=True)
        stats_ref[1:2, :] += jnp.sum(dx1, axis=0, keepdims=True)
        stats_ref[2:3, :] += jnp.sum(dy * xhat2, axis=0, keepdims=True)
        stats_ref[3:4, :] += jnp.sum(dy, axis=0, keepdims=True)
        stats_ref[4:5, :] += jnp.broadcast_to(loss, (1, d))

    tok = lambda w: pl.BlockSpec((tm, w), lambda i: (i, 0))
    vec = pl.BlockSpec((1, d), lambda i: (0, 0))
    return _carry(
        body, name="ffn_fwd_bwd", grid=(t_tok // tm,),
        in_specs=[tok(d), tok(1), tok(d), tok(d)] + [ANY] * (2 * n_part) + [vec, vec, vec, vec],
        out_specs=[tok(f), tok(f), tok(d), tok(d), pl.BlockSpec((8, d), lambda i: (0, 0))],
        out_shape=[jax.ShapeDtypeStruct((t_tok, f), BF16), jax.ShapeDtypeStruct((t_tok, f), BF16),
                   jax.ShapeDtypeStruct((t_tok, d), BF16), jax.ShapeDtypeStruct((t_tok, d), F32), jax.ShapeDtypeStruct((8, d), F32)],
        scratch_shapes=[pltpu.VMEM((tm, f), F32), pltpu.VMEM((d, f), BF16), pltpu.VMEM((f, d), BF16),
                        pltpu.SemaphoreType.DMA((2, n_part, n_chunk))],
        args=(xhat1, rstd1, x1b, target, *w1_parts, *w2_parts, ln1_g, ln1_b, ln2_g, ln2_b))[0]


WGRAD_STEPS = [(True, 0), (True, 1), (False, 0), (True, 2), (False, 1), (True, 3), (False, 2), (False, 3)]
CHIP_FLIPS = [3, 1, 2, 0]


def _pick(table, s):
    out = table[-1]
    for i in range(len(table) - 2, -1, -1):
        out = jnp.where(s == i, table[i], out)
    return out


def _wgrad_shard(s, cc):
    q = jnp.bitwise_xor(cc[1], _pick([CHIP_FLIPS[k] for _, k in WGRAD_STEPS], s))
    return 2 * q + jnp.where(_pick([int(sibling) for sibling, _ in WGRAD_STEPS], s) == 1, 1 - cc[0], cc[0])


def _flipped(k):
    x, y, c = _place()
    return (1 - x if CHIP_FLIPS[k] // 2 else x, 1 - y if CHIP_FLIPS[k] % 2 else y, c)


def _wgrad_pair_sum(name, product, chunk, in_specs, args, core_chip, n_sent, comms=(), scratch_shapes=(), rider=None):
    half = N_DEV // 2
    n_in, n_out = len(in_specs), 2 + (0 < n_sent) + (n_sent < half - 1)
    more = rider or dict(in_specs=[], out_specs=[], out_shape=[], args=[])
    n_rin, n_rout = len(more["in_specs"]), len(more["out_specs"])

    def body(cc_ref, *refs):
        ins, rins, refs = refs[:n_in], refs[n_in:n_in + n_rin], refs[n_in + n_rin:]
        outs, routs, scr = refs[:n_out], refs[n_out:n_out + n_rout], refs[n_out + n_rout:]
        if rider:
            pl.when(pl.program_id(0) == 0)(lambda: rider["body"](cc_ref, rins, routs))
        (own_ref, recv_ref), from_chips_ref, wire_ref = outs[-2:], outs[0], outs[n_out - 3]
        send_buf, got, send_sems, recv_sems, got_sem, wire_buf, leave_sems, arrive_sems = scr[:8]
        s = pl.program_id(0)
        x, y, c = _place()
        def send(q):
            return pltpu.make_async_remote_copy(
                src_ref=send_buf.at[q % 2], dst_ref=recv_ref.at[q], send_sem=send_sems.at[q], recv_sem=recv_sems.at[q],
                device_id=(x, y, 1 - c), device_id_type=MESH)

        def load(q):
            return pltpu.make_async_copy(recv_ref.at[q], got, got_sem.at[0])

        def leave(k):
            if k < n_sent:
                return pltpu.make_async_remote_copy(
                    src_ref=wire_buf.at[k], dst_ref=from_chips_ref.at[k], send_sem=leave_sems.at[k],
                    recv_sem=arrive_sems.at[k], device_id=_flipped(k), device_id_type=MESH)
            return pltpu.make_async_copy(wire_buf.at[k], wire_ref.at[k - n_sent], leave_sems.at[k])

        for step, (sibling, q) in enumerate(WGRAD_STEPS):
            if not sibling:
                @pl.when(s == step)
                def _(q=q):
                    send(q).wait_recv()
                    load(q).start()

        g = product(_wgrad_shard(s, cc_ref), *ins, *scr[8:])

        for step, (sibling, q) in enumerate(WGRAD_STEPS):
            @pl.when(s == step)
            def _(sibling=sibling, q=q):
                if sibling:
                    if q >= 2:
                        send(q - 2).wait_send()
                    send_buf[q % 2] = g
                    send(q).start()
                    return
                load(q).wait()
                total = g + got[...]
                if q < half - 1:
                    wire_buf[q] = total.astype(BF16)
                    leave(q).start()
                else:
                    own_ref[...] = total

        @pl.when(s == N_DEV - 1)
        def _():
            for q in range(half - 2, half):
                send(q).wait_send()
            for k in range(half - 1):
                leave(k).wait()

    sums = lambda n: [jax.ShapeDtypeStruct((n,) + chunk, BF16)] if n else []
    sem = lambda n: pltpu.SemaphoreType.DMA((n,))
    res, per_comm = _carry(
        body, name=name, grid=(N_DEV,), comms=comms, prefetch=(core_chip,), in_specs=list(in_specs) + more["in_specs"],
        out_specs=[ANY] * (n_out - 2) + [pl.BlockSpec(chunk, lambda s, cc: (0, 0)), ANY] + more["out_specs"],
        out_shape=sums(n_sent) + sums(half - 1 - n_sent) + [jax.ShapeDtypeStruct(chunk, F32),
                                                            jax.ShapeDtypeStruct((half,) + chunk, F32)] + more["out_shape"],
        scratch_shapes=[pltpu.VMEM((2,) + chunk, F32), pltpu.VMEM(chunk, F32), sem(half), sem(half), sem(1),
                        pltpu.VMEM((half - 1,) + chunk, BF16), sem(half - 1), sem(half - 1), *scratch_shapes],
        args=list(args) + more["args"])
    first = res[0] if n_sent else None, res[n_out - 3] if n_sent < half - 1 else None
    return (*first, res[n_out - 2], per_comm, res[n_out:]) if rider else (*first, res[n_out - 2], per_comm)


def _resident(a):
    return pl.BlockSpec(a.shape, lambda s, cc: (0,) * a.ndim, pipeline_mode=pl.Buffered(1))


def _pair_sum_rider(parts, recv):
    _, r, c = parts.shape
    by_chip = parts.reshape(N_DEV // 2, 2, r, c)

    def body(cc_ref, ins, outs):
        (parts_ref, recv_ref), (wire_ref, own_ref) = ins, outs
        for q in range(N_DEV // 2):
            total = parts_ref[q] + recv_ref[q]
            wire_ref[q] = total.astype(BF16)

            @pl.when(cc_ref[1] == q)
            def _():
                own_ref[...] = total

    whole = lambda shape: pl.BlockSpec(shape, lambda s, cc: (0,) * len(shape))
    mine = pl.BlockSpec((N_DEV // 2, None, r, c), lambda s, cc: (0, cc[0], 0, 0), pipeline_mode=pl.Buffered(1))
    return dict(body=body, args=[by_chip, recv], in_specs=[mine, _resident(recv)],
                out_specs=[whole((N_DEV // 2, r, c)), whole((r, c))],
                out_shape=[jax.ShapeDtypeStruct((N_DEV // 2, r, c), BF16), jax.ShapeDtypeStruct((r, c), F32)])


def _ffn_wgrad(name, lhs, rhs, chunk_lhs, core_chip, n_sent, comms=(), rider=None):
    t_tok = lhs.shape[0]
    fc = (lhs if chunk_lhs else rhs).shape[1] // N_DEV
    chunked = pl.BlockSpec((t_tok, fc), lambda s, cc: (0, _wgrad_shard(s, cc)))

    def product(shard, lhs_ref, rhs_ref):
        return _dot(lhs_ref[...], rhs_ref[...], TN)

    return _wgrad_pair_sum(
        name, product, (fc, rhs.shape[1]) if chunk_lhs else (lhs.shape[1], fc),
        [chunked, _resident(rhs)] if chunk_lhs else [_resident(lhs), chunked], (lhs, rhs), core_chip, n_sent, comms,
        rider=rider)


def _proj_out_bwd(dz1, cat_t, w_out_b, comms=()):
    t_tok, d = dz1.shape
    d_mix = cat_t.shape[0]
    tm = min(1024, t_tok)

    def body(dz1_ref, cat_ref, w_ref, dcat_ref, gw_ref):
        @pl.when(pl.program_id(0) == 0)
        def _():
            gw_ref[...] = jnp.zeros_like(gw_ref)

        dzb = dz1_ref[...].astype(BF16)
        dcat_ref[...] = _dot(w_ref[...], dzb, NT)
        gw_ref[...] += _dot(cat_ref[...], dzb)

    return _carry(
        body, name="proj_out_bwd", grid=(t_tok // tm,), comms=comms,
        in_specs=[pl.BlockSpec((tm, d), lambda i: (i, 0)), pl.BlockSpec((d_mix, tm), lambda i: (0, i)),
                  pl.BlockSpec((d_mix, d), lambda i: (0, 0))],
        out_specs=[pl.BlockSpec((d_mix, tm), lambda i: (0, i)), pl.BlockSpec((d_mix, d), lambda i: (0, 0))],
        out_shape=[jax.ShapeDtypeStruct((d_mix, t_tok), F32), jax.ShapeDtypeStruct((d_mix, d), F32)],
        args=(dz1, cat_t, w_out_b))


def _mixer_bwd(dcat_t, h_t, cos_t, sin_t, w_spatial, b_spatial, vln_g, vln_b, sinks, band_bias, lse, comms=()):
    t_tok = h_t.shape[1]
    nb, n_step = t_tok // BLK, t_tok // MIX_W
    group = N_HEADS // N_KV_HEADS

    def body(sinks_ref, dcat_ref, u_ref, vg_ref, q_ref, kvc_ref, kvp_ref, cos_ref, sin_ref, cosp_ref, sinp_ref,
             wsp_ref, bsp_ref, g_ref, b_ref, bias_ref, lse_ref, dh_ref, dkvc_ref, dkvp_ref, gwsb_ref, gbsp_ref, gvln_ref, gsink_ref,
             dg_acc, db_acc, wm_scr, gws_ref):
        i = pl.program_id(0)

        @pl.when(i == 0)
        def _():
            gws_ref[...] = jnp.zeros_like(gws_ref)
            gbsp_ref[...] = jnp.zeros_like(gbsp_ref)
            gsink_ref[...] = jnp.zeros_like(gsink_ref)
            dg_acc[...] = jnp.zeros_like(dg_acc)
            db_acc[...] = jnp.zeros_like(db_acc)

        _mask_w_once(wsp_ref, wm_scr)

        g = g_ref[...]
        ua, ua_grad = _gelu_and_grad(u_ref[...])
        vv, vv_grad = _gelu_and_grad(vg_ref[...])
        vp, vhat, rstd = _ln_fwd_t(vv, g, b_ref[...])
        vpb = vp.astype(BF16)
        da = dcat_ref[0:D_GMLP, :]
        dmixed = da * ua
        dvp_blocks = []
        for b in range(MIX_BLOCKS):
            dvp_parts = []
            for hh in range(N_HEADS):
                rows = slice(hh * HEAD_DIM, (hh + 1) * HEAD_DIM)
                vpb_h = vpb[rows, _cols(b)]
                mixed = _dot(vpb_h, wm_scr[hh], NT) + bsp_ref[hh:hh + 1, :]
                dh_ref[COL_U + hh * HEAD_DIM:COL_U + (hh + 1) * HEAD_DIM, _cols(b)] = (
                    da[rows, _cols(b)] * mixed * ua_grad[rows, _cols(b)]).astype(BF16)
                dm = dmixed[rows, _cols(b)]
                dmb = dm.astype(BF16)
                gbsp_ref[hh:hh + 1, :] += jnp.sum(dm, axis=0, keepdims=True)
                gws_ref[hh] += _dot(dmb, vpb_h, TN)
                dvp_parts.append(_dot(dmb, wm_scr[hh]))
            dvp_blocks.append(jnp.concatenate(dvp_parts, axis=0))
        dvp = jnp.concatenate(dvp_blocks, axis=1)
        dgv, dbv = dvp * vhat, dvp
        for b in range(MIX_BLOCKS):
            dg_acc[...] += dgv[:, _cols(b)]
            db_acc[...] += dbv[:, _cols(b)]
        dh_ref[COL_V:COL_V + D_GMLP, :] = (_ln_bwd_t(dvp, vhat, rstd, g) * vv_grad).astype(BF16)

        kvc, cos, sin = kvc_ref[...], cos_ref[...], sin_ref[...]
        qr = (_rope_t(q_ref[...], cos, sin) * SCORE_SCALE).astype(BF16)
        sinks4 = [_group_lanes([jnp.full((1, BLK), sinks_ref[hh], F32) for hh in range(kv * group, (kv + 1) * group)])
                  for kv in range(N_KV_HEADS)]
        dq_blocks, dkv_cur, dkv_prev = [], [], []
        for b in range(MIX_BLOCKS):
            kv_cur, kv_prev, cosc, sinc, cosp, sinp, bias1 = _block_inputs(b, i, kvc, kvp_ref, cos, sin, cosp_ref, sinp_ref, bias_ref)
            k_t, k_n, v_t = _keys_values(kv_cur, kv_prev, cosc, sinc, cosp, sinp)
            v_n = jnp.concatenate([kv_prev[D_KV:].T, kv_cur[D_KV:].T], axis=0).astype(BF16)
            bias = _group_lanes([bias1] * group)
            dk, dv, dq_parts = [], [], []
            for kv in range(N_KV_HEADS):
                heads = range(kv * group, (kv + 1) * group)
                kv_rows = slice(kv * HEAD_DIM, (kv + 1) * HEAD_DIM)
                qs = _group_lanes([qr[hh * HEAD_DIM:(hh + 1) * HEAD_DIM, _cols(b)] for hh in heads])
                dos = _group_lanes([dcat_ref[D_GMLP + hh * HEAD_DIM:D_GMLP + (hh + 1) * HEAD_DIM, _cols(b)]
                                    for hh in heads]).astype(BF16)
                lse_g = lse_ref[b * N_KV_HEADS + kv:b * N_KV_HEADS + kv + 1, :]
                p = jnp.exp(_dot(k_n, _pad_head(qs, kv)) + bias - lse_g)
                p_sink = jnp.exp(sinks4[kv] - lse_g)
                dp = _dot(v_n, _pad_head(dos, kv))
                delta = jnp.sum(p * dp, axis=0, keepdims=True)
                ds = (p * (dp - delta)).astype(BF16)
                dsink = p_sink * delta
                dq = _dot(k_t[kv_rows], ds) * SCORE_SCALE
                for j, hh in enumerate(heads):
                    gsink_ref[hh:hh + 1, :] -= dsink[:, j * BLK:(j + 1) * BLK]
                    dq_parts.append(dq[:, j * BLK:(j + 1) * BLK])
                dk.append(_dot(qs, ds, NT))
                dv.append(_dot(dos, p.astype(BF16), NT))
            dq_blocks.append(jnp.concatenate(dq_parts, axis=0))
            dk_all, dv_all = jnp.concatenate(dk, axis=0), jnp.concatenate(dv, axis=0)
            dkv_cur.append(jnp.concatenate([_rope_t(dk_all[:, BLK:], cosc, sinc, bwd=True), dv_all[:, BLK:]], axis=0))
            dkv_prev.append(jnp.concatenate([_rope_t(dk_all[:, :BLK], cosp, sinp, bwd=True), dv_all[:, :BLK]], axis=0))
        dh_ref[COL_Q:COL_Q + D_ATTN, :] = _rope_t(jnp.concatenate(dq_blocks, axis=1), cos, sin, bwd=True).astype(BF16)
        for b in range(MIX_BLOCKS):
            dkvc_ref[:, _cols(b)] = dkv_cur[b] + dkv_prev[b + 1] if b + 1 < MIX_BLOCKS else dkv_cur[b]
        dkvp_ref[...] = dkv_prev[0]

        @pl.when(i == n_step - 1)
        def _():
            causal = _causal()
            for hh in range(N_HEADS):
                gwsb_ref[hh] = jnp.where(causal, gws_ref[hh], 0.0).astype(BF16)
            gvln_ref[...] = jnp.zeros_like(gvln_ref)
            gvln_ref[0:1, :] = jnp.sum(dg_acc[...].T, axis=0, keepdims=True)
            gvln_ref[1:2, :] = jnp.sum(db_acc[...].T, axis=0, keepdims=True)

    full = lambda shape: pl.BlockSpec(shape, lambda i: (0,) * len(shape))
    return _carry(
        body, name="mixer_bwd", grid=(n_step,), comms=comms,
        in_specs=[pl.BlockSpec(memory_space=pltpu.SMEM), pl.BlockSpec((D_GMLP + D_ATTN, MIX_W), lambda i: (0, i))]
        + _h_specs() + _table_specs()
        + [full((N_HEADS, BLK, BLK)), full((N_HEADS, BLK)), full((D_GMLP, 1)), full((D_GMLP, 1)), BIAS_SPEC, LSE_SPEC],
        out_specs=[pl.BlockSpec((COL_K, MIX_W), lambda i: (0, i)), pl.BlockSpec((2 * D_KV, MIX_W), lambda i: (0, i)),
                   pl.BlockSpec((2 * D_KV, BLK), lambda i: (0, (i + n_step - 1) % n_step)),
                   full((N_HEADS, BLK, BLK)), full((N_HEADS, BLK)), full((8, D_GMLP)), full((N_HEADS, LANES))],
        out_shape=[jax.ShapeDtypeStruct((COL_K, t_tok), BF16), jax.ShapeDtypeStruct((2 * D_KV, t_tok), F32),
                   jax.ShapeDtypeStruct((2 * D_KV, n_step * BLK), F32),
                   jax.ShapeDtypeStruct((N_HEADS, BLK, BLK), BF16), jax.ShapeDtypeStruct((N_HEADS, BLK), F32),
                   jax.ShapeDtypeStruct((8, D_GMLP), F32), jax.ShapeDtypeStruct((N_HEADS, LANES), F32)],
        scratch_shapes=[pltpu.VMEM((D_GMLP, BLK), F32), pltpu.VMEM((D_GMLP, BLK), F32), pltpu.VMEM((N_HEADS, BLK, BLK), BF16),
                        pltpu.VMEM((N_HEADS, BLK, BLK), F32)],
        args=(sinks, dcat_t, h_t, h_t, h_t, h_t, h_t, cos_t, sin_t, cos_t, sin_t, w_spatial, b_spatial, vln_g, vln_b, band_bias, lse))


def _dkv_rows(dkvc_ref, dkvp_ref, width, store):
    for s in range(width // MIX_W):
        rest, last = slice(s * MIX_W, (s + 1) * MIX_W - BLK), slice((s + 1) * MIX_W - BLK, (s + 1) * MIX_W)
        store(rest, dkvc_ref[:, rest].astype(BF16))
        store(last, (dkvc_ref[:, last] + dkvp_ref[:, _cols(s)]).astype(BF16))


def _proj_in_wgrad(dh_b, dkvc_t, dkvp_t, xb, core_chip, comms=()):
    t_tok, d = xb.shape
    d_main, d_kv = dh_b.shape[0], dkvc_t.shape[0]
    rows = (d_main + d_kv) // N_DEV
    whole, cut = d_main // rows, d_main % rows

    def product(shard, dh_ref, dkvc_ref, dkvp_ref, xb_ref, dht_scr, sems):
        copies = [pltpu.make_async_copy(dh_ref.at[j * rows:(j + 1) * rows], dht_scr.at[j], sems.at[j]) for j in range(whole)]
        copies.append(pltpu.make_async_copy(dh_ref.at[whole * rows:d_main], dht_scr.at[whole, 0:cut], sems.at[whole]))

        @pl.when(pl.program_id(0) == 0)
        def _():
            for cp in copies:
                cp.start()

            def store(cols, val):
                dht_scr[whole, cut:rows, cols] = val[0:rows - cut]
                dht_scr[whole + 1, :, cols] = val[rows - cut:]

            _dkv_rows(dkvc_ref, dkvp_ref, t_tok, store)
            for cp in copies:
                cp.wait()

        return _dot(dht_scr[shard], xb_ref[...])

    return _wgrad_pair_sum(
        "proj_in_wgrad", product, (rows, d), [ANY, _resident(dkvc_t), _resident(dkvp_t), _resident(xb)],
        (dh_b, dkvc_t, dkvp_t, xb), core_chip, N_DEV // 2 - 1, comms,
        scratch_shapes=[pltpu.VMEM((N_DEV, rows, t_tok), BF16), pltpu.SemaphoreType.DMA((whole + 1,))])


def _proj_in_dgrad(dh_b, dkvc_t, dkvp_t, dz1, w_in_t, comms=()):
    t_tok, d = dz1.shape
    d_main, d_kv = dh_b.shape[0], dkvc_t.shape[0]
    tm = min(512, t_tok)

    def body(dh_ref, dkvc_ref, dkvp_ref, dz1_ref, w_ref, dx_ref, dkv_scr):
        def store(cols, val):
            dkv_scr[:, cols] = val

        _dkv_rows(dkvc_ref, dkvp_ref, tm, store)
        dx_ref[...] = (ALPHA * dz1_ref[...] + _dot(dh_ref[...], w_ref[0:d_main, :], TN)
                       + _dot(dkv_scr[...], w_ref[d_main:, :], TN))

    return _carry(
        body, name="proj_in_dgrad", grid=(t_tok // tm,), comms=comms,
        in_specs=[pl.BlockSpec((d_main, tm), lambda i: (0, i)), pl.BlockSpec((d_kv, tm), lambda i: (0, i)),
                  pl.BlockSpec((d_kv, tm // MIX_BLOCKS), lambda i: (0, i)),
                  pl.BlockSpec((tm, d), lambda i: (i, 0)), pl.BlockSpec((d_main + d_kv, d), lambda i: (0, 0))],
        out_specs=[pl.BlockSpec((tm, d), lambda i: (i, 0))],
        out_shape=[jax.ShapeDtypeStruct((t_tok, d), F32)],
        scratch_shapes=[pltpu.VMEM((d_kv, tm), BF16)],
        args=(dh_b, dkvc_t, dkvp_t, dz1, w_in_t))


def _adamw(w, g, m, v):
    m = ADAM_B1 * m + (1.0 - ADAM_B1) * g
    v = ADAM_B2 * v + (1.0 - ADAM_B2) * (g * g)
    m_hat = m / (1.0 - ADAM_B1 ** ADAM_STEP)
    v_hat = v / (1.0 - ADAM_B2 ** ADAM_STEP)
    delta = -ADAM_LR * (m_hat / (jnp.sqrt(v_hat) + ADAM_EPS) + ADAM_WD * w)
    return delta, m, v


ADAMW_STEPS = 4


def _adamw_shards(name, items, comms=(), rider=None):
    n_in, n_out = sum(4 + len(it[1]) for it in items), 4 * len(items)
    n_rin = len(rider["args"]) if rider else 0

    def body(*refs):
        ins, rins, outs, routs = refs[:n_in], refs[n_in:n_in + n_rin], refs[n_in + n_rin:n_in + n_rin + n_out], refs[n_in + n_rin + n_out:]
        for i, item in enumerate(items):
            (own_ref, w_ref, m_ref, v_ref), recv_refs, ins = ins[:4], ins[4:4 + len(item[1])], ins[4 + len(item[1]):]
            g = own_ref[...]
            for recv_ref in recv_refs:
                for k in range(recv_ref.shape[0]):
                    g = g + recv_ref[k].astype(F32)
            for o_ref, val in zip(outs[4 * i:4 * i + 4], (g,) + _adamw(w_ref[...], g, m_ref[...], v_ref[...])):
                o_ref[...] = val
        if rider:
            pl.when(pl.program_id(0) == 0)(lambda: rider["body"](rins, routs))

    in_specs, out_specs, out_shape, args = [], [], [], []
    for own, recvs, w, m, v in items:
        r, c = own.shape
        tiles = ADAMW_STEPS
        while (r // tiles) % BF16_ROWS:
            tiles //= 2
        blk = pl.BlockSpec((r // tiles, c), lambda s, k=ADAMW_STEPS // tiles: (s // k, 0))
        in_specs += [blk] * 4 + [pl.BlockSpec((a.shape[0], r // tiles, c), lambda s, k=ADAMW_STEPS // tiles: (0, s // k, 0))
                                 for a in recvs]
        out_specs += [blk] * 4
        out_shape += [jax.ShapeDtypeStruct((r, c), F32)] * 4
        args += [own, w, m, v, *recvs]
    if rider:
        in_specs, out_specs = in_specs + rider["in_specs"], out_specs + rider["out_specs"]
        out_shape, args = out_shape + rider["out_shape"], args + rider["args"]
    res, per_comm = _carry(body, name=name, grid=(ADAMW_STEPS,), comms=comms, in_specs=in_specs, out_specs=out_specs,
                           out_shape=out_shape, args=args)
    return [res[4 * i:4 * i + 4] for i in range(len(items))], res[n_out:], per_comm


VEC_VLN, VEC_LN1G, VEC_LN1B, VEC_LN2G, VEC_LN2B, VEC_SINK, VEC_LOSS, VEC_BSP, VEC_ROWS = 0, 1, 2, 3, 4, 5, 6, 8, 16


def _adamw_small(parts_w, parts_vec, params):
    n = parts_w.shape[0]
    flat = [a for p in params for a in p]
    shapes = [p[0].shape for p in params]

    def grads(gw, gv):
        return [gw, gv[VEC_VLN:VEC_VLN + 1, 0:D_GMLP], gv[VEC_VLN:VEC_VLN + 1, D_GMLP:2 * D_GMLP],
                gv[VEC_BSP:VEC_BSP + N_HEADS, 0:BLK], gv[VEC_LN1G:VEC_LN1G + 1], gv[VEC_LN1B:VEC_LN1B + 1],
                gv[VEC_LN2G:VEC_LN2G + 1], gv[VEC_LN2B:VEC_LN2B + 1], gv[VEC_SINK:VEC_SINK + 1, 0:N_HEADS]]

    def body(ins, outs):
        (pw_ref, pv_ref), ins = ins[:2], ins[2:]
        gw, gv = pw_ref[0].astype(F32), pv_ref[0]
        for k in range(1, n):
            gw, gv = gw + pw_ref[k].astype(F32), gv + pv_ref[k]
        for i, g in enumerate(grads(gw, gv)):
            w_ref, m_ref, v_ref = ins[3 * i:3 * i + 3]
            delta, m_new, v_new = _adamw(w_ref[...], g, m_ref[...], v_ref[...])
            for o_ref, val in zip(outs[4 * i:4 * i + 4], (g, delta, m_new, v_new)):
                o_ref[...] = val
        outs[-1][...] = gv[VEC_LOSS:VEC_LOSS + 1, 0:LANES]

    whole = lambda shape, **kw: pl.BlockSpec(shape, lambda i: (0,) * len(shape), **kw)
    once = dict(pipeline_mode=pl.Buffered(1))
    return dict(
        body=body, args=[parts_w, parts_vec, *flat],
        in_specs=[whole(parts_w.shape, **once), whole(parts_vec.shape, **once)] + [whole(a.shape, **once) for a in flat],
        out_specs=[whole(s) for s in shapes for _ in range(4)] + [whole((1, LANES))],
        out_shape=[jax.ShapeDtypeStruct(s, F32) for s in shapes for _ in range(4)] + [jax.ShapeDtypeStruct((1, LANES), F32)])


def kernel(x, positions, w_in, v_ln_g, v_ln_b, w_spatial, b_spatial, sinks, w_out, ln1_g, ln1_b, w_ff1, w_ff2, ln2_g, ln2_b, loss_target, m_w_in, m_v_ln_g, m_v_ln_b, m_w_spatial, m_b_spatial, m_sinks, m_w_out, m_ln1_g, m_ln1_b, m_w_ff1, m_w_ff2, m_ln2_g, m_ln2_b, v_w_in, v_v_ln_g, v_v_ln_b, v_w_spatial, v_b_spatial, v_sinks, v_w_out, v_ln1_g, v_ln1_b, v_w_ff1, v_w_ff2, v_ln2_g, v_ln2_b):
    _, t_tok, d = x.shape
    xi, yi, ci = _place()
    core_chip = jnp.stack([ci, 2 * xi + yi]).astype(jnp.int32)
    x2 = x.reshape(t_tok, d)
    target = loss_target.reshape(t_tok, d)
    inv_freq = ROPE_THETA ** (-jnp.arange(0, HEAD_DIM, 2, dtype=F32) / HEAD_DIM)
    wsp, bsp, sink_vec = w_spatial[0], b_spatial[0], sinks[0]
    vg_col, vb_col = v_ln_g.reshape(D_GMLP, 1), v_ln_b.reshape(D_GMLP, 1)
    big = {"in": w_in[0], "out": w_out[0], "ff1": w_ff1[0], "ff2": w_ff2[0]}
    half1, half2 = big["ff1"].shape[1] // 2, big["ff2"].shape[0] // 2
    w1_mine = [big["ff1"][:, :half1].astype(BF16), big["ff1"][:, half1:].astype(BF16)]
    w2_mine = [big["ff2"][:half2].astype(BF16), big["ff2"][half2:].astype(BF16)]

    (cos_t, sin_t), ((g_in,),) = _rope_tables(
        positions, jnp.tile(inv_freq, 2).reshape(HEAD_DIM, 1), comms=[_gather_comm([big["in"].T.astype(BF16)])])
    w_in_t = g_in.reshape(D_IN, d)
    (h_t, xb), ((g_out, w1_a),) = _proj_in(x2, w_in_t, comms=[_gather_comm([big["out"].astype(BF16), w1_mine[0]])])
    w_out_b = g_out.reshape(-1, d)
    band_bias = _band_bias()
    (cat_t, lse), ((w1_b, w2_a),) = _mixer_fwd(h_t, cos_t, sin_t, wsp, bsp, vg_col, vb_col, sink_vec, band_bias,
                                                comms=[_gather_comm([w1_mine[1], w2_mine[0]])])
    (xhat1, rstd1, x1b), ((w2_b,),) = _proj_out(cat_t, x2, w_out_b, ln1_g, ln1_b, comms=[_gather_comm([w2_mine[1]])])
    act_b, dpre_b, dz2b, dz1, stats = _ffn_fwd_bwd(xhat1, rstd1, x1b, target, [w1_a, w1_b], [w2_a, w2_b], ln1_g, ln1_b, ln2_g, ln2_b)

    (dcat_t, gw_out), _ = _proj_out_bwd(dz1, cat_t, w_out_b)
    p_out = gw_out.reshape(N_DEV, -1, d)
    r_ff1_a, wire_ff1, own_ff1, ((s_out,),) = _ffn_wgrad(
        "ffn_wgrad1", x1b, dpre_b, False, core_chip, 1, comms=[_sibling_comm([p_out])])
    _, wire_ff2, own_ff2, ((r_ff1_b,),), (wire_out, own_out) = _ffn_wgrad(
        "ffn_wgrad2", act_b, dz2b, True, core_chip, 0, comms=[_flips_comm(wire_ff1, 1)],
        rider=_pair_sum_rider(p_out, s_out))
    (dh_b, dkvc_t, dkvp_t, g_wsp, g_bsp, g_vln, g_sink), ((r_ff2,), (r_out,)) = _mixer_bwd(
        dcat_t, h_t, cos_t, sin_t, wsp, bsp, vg_col, vb_col, sink_vec, band_bias, lse,
        comms=[_flips_comm(wire_ff2, 0), _chips_comm([wire_out])])
    sink_row = jnp.pad(g_sink.sum(axis=1).reshape(1, N_HEADS), ((0, 0), (0, d - N_HEADS)))
    small_vec = jnp.concatenate([g_vln[0:2].reshape(1, d), stats[0:4], sink_row, stats[4:5], jnp.zeros((1, d), F32),
                                 jnp.pad(g_bsp, ((0, 0), (0, d - BLK)))], axis=0)
    r_in, _, own_in, ((parts_w, parts_vec),) = _proj_in_wgrad(
        dh_b, dkvc_t, dkvp_t, xb, core_chip, comms=[_gather_comm([g_wsp.reshape(-1, BLK), small_vec])])
    (grad_x,), _ = _proj_in_dgrad(dh_b, dkvc_t, dkvp_t, dz1, w_in_t)
    small = [(w_spatial, m_w_spatial, v_w_spatial), (v_ln_g, m_v_ln_g, v_v_ln_g), (v_ln_b, m_v_ln_b, v_v_ln_b),
             (b_spatial, m_b_spatial, v_b_spatial), (ln1_g, m_ln1_g, v_ln1_g), (ln1_b, m_ln1_b, v_ln1_b),
             (ln2_g, m_ln2_g, v_ln2_g), (ln2_b, m_ln2_b, v_ln2_b), (sinks, m_sinks, v_sinks)]
    views = [(-1, BLK), None, None, (N_HEADS, BLK)] + [None] * 5
    small_update = _adamw_small(parts_w, parts_vec, [
        tuple(a if vw is None else a.reshape(vw) for a in p) for p, vw in zip(small, views)])
    (out_out, ff1_out, ff2_out, in_out_t), small_res, _ = _adamw_shards("adamw_all", [
        (own_out, [r_out], big["out"], m_w_out[0], v_w_out[0]),
        (own_ff1, [r_ff1_a, r_ff1_b], big["ff1"], m_w_ff1[0], v_w_ff1[0]),
        (own_ff2, [r_ff2], big["ff2"], m_w_ff2[0], v_w_ff2[0]),
        (own_in, [r_in], big["in"].T, m_w_in[0].T, v_w_in[0].T)], rider=small_update)
    in_out = [o.T for o in in_out_t]
    small_out = [[o.reshape(p[0].shape) for o in small_res[4 * i:4 * i + 4]] for i, p in enumerate(small)]
    loss = small_res[-1][0, 0]

    big_out = {0: in_out, 6: out_out, 9: ff1_out, 10: ff2_out}
    small_slot = {3: 0, 1: 1, 2: 2, 4: 3, 7: 4, 8: 5, 11: 6, 12: 7, 5: 8}
    outs = [loss, grad_x.reshape(x.shape)]
    for kind in range(4):
        for wi in range(13):
            outs.append(big_out[wi][kind][None] if wi in big_out else small_out[small_slot[wi]][kind])
    return tuple(outs)
```
